```python
import jax, jax.numpy as jnp
from jax import lax
import numpy as np

D_MODEL = 1024
BATCH = 8
SEQ = 2048
DEPTH = 1

HEAD_DIM = 64
HEADS_PER_GROUP = 8
ATTN_PATTERNS = ((128, 1), (512, 4), (2048, 16))
N_ATTN_GROUPS = len(ATTN_PATTERNS)
ATTN_WIDTH = N_ATTN_GROUPS * HEADS_PER_GROUP * HEAD_DIM
ATTN_OUT = HEADS_PER_GROUP * HEAD_DIM
ATTN_BLOCK = 128
ROPE_THETA = 10000.0
GMLP_CHUNK = 128
GMLP_GROUPS = 8
GMLP_WIDTH = D_MODEL
GMLP_GROUP_DIM = GMLP_WIDTH // GMLP_GROUPS
N_BRANCHES = 2
IN_WIDTH = 3 * ATTN_WIDTH + 2 * GMLP_WIDTH + N_BRANCHES * D_MODEL
D_FF = 2816
ALPHA = (2 * DEPTH) ** 0.25
BETA = (8 * DEPTH) ** -0.25
LN_EPS = 1e-5

kernel_name = 'hybrid_dilated_attn_gmlp_macaron_deepnorm'


def layer_norm(x, g, b):
    xf = x.astype(jnp.float32)
    mu = jnp.mean(xf, -1, keepdims=True)
    var = jnp.mean(jnp.square(xf - mu), -1, keepdims=True)
    y = (xf - mu) * lax.rsqrt(var + LN_EPS) * g.astype(jnp.float32) + b.astype(jnp.float32)
    return y.astype(x.dtype)


def swiglu_ffn(x, w_gate, w_up, w_down):
    return (jax.nn.silu(x @ w_gate) * (x @ w_up)) @ w_down


def rotary(t, cos, sin):
    tf = t.astype(jnp.float32)
    t1, t2 = jnp.split(tf, 2, axis=-1)
    c = cos[:, :, None, None, :]
    s = sin[:, :, None, None, :]
    return jnp.concatenate([t1 * c - t2 * s, t2 * c + t1 * s], axis=-1).astype(t.dtype)


def dilated_window_attention(q, k, v, window, dilation):
    b, s, h, dh = q.shape
    w = window // dilation
    sub_len = s // dilation
    n_blk = -(-sub_len // ATTN_BLOCK)
    pad = n_blk * ATTN_BLOCK - sub_len

    def to_blocks(t):
        t = t.reshape(b, sub_len, dilation, h, dh).transpose(0, 2, 1, 3, 4)
        t = jnp.pad(t, ((0, 0), (0, 0), (0, pad), (0, 0), (0, 0)))
        return t.reshape(b, dilation, n_blk, ATTN_BLOCK, h, dh)

    def with_prev(t):
        prev = jnp.pad(t, ((0, 0), (0, 0), (1, 0), (0, 0), (0, 0), (0, 0)))[:, :, :-1]
        return jnp.concatenate([prev, t], axis=3)

    qb = to_blocks(q)
    kc = with_prev(to_blocks(k))
    vc = with_prev(to_blocks(v))
    scores = jnp.einsum('brnqhd,brnkhd->brnhqk', qb, kc,
                        preferred_element_type=jnp.float32) * (dh ** -0.5)
    blk = np.arange(n_blk)[:, None, None]
    qi = np.arange(ATTN_BLOCK)[None, :, None]
    kj = np.arange(2 * ATTN_BLOCK)[None, None, :]
    dist = qi + ATTN_BLOCK - kj
    kpos = (blk - 1) * ATTN_BLOCK + kj
    mask = (dist >= 0) & (dist <= w) & (kpos >= 0)
    scores = jnp.where(jnp.asarray(mask)[None, None, :, None], scores, -jnp.inf)
    m = jnp.max(scores, axis=-1, keepdims=True)
    p = jnp.exp(scores - m)
    l = jnp.sum(p, axis=-1, keepdims=True)
    o = jnp.einsum('brnhqk,brnkhd->brnqhd', p / l, vc.astype(jnp.float32))
    lse = (m + jnp.log(l))[..., 0].transpose(0, 1, 2, 4, 3)

    def from_blocks(t):
        t = t.reshape((b, dilation, n_blk * ATTN_BLOCK) + t.shape[4:])[:, :, :sub_len]
        t = jnp.swapaxes(t, 1, 2)
        return t.reshape((b, s) + t.shape[3:])

    return from_blocks(o), from_blocks(lse)


def hybrid_mixer(h, cos, sin, w_in, b_gates, gmlp_ln_g, gmlp_ln_b, gmlp_w_s, gmlp_b_s,
                 w_attn_branch, w_gmlp_branch, w_out):
    b, s, _ = h.shape
    proj = h @ w_in
    qkv, z, g = jnp.split(proj, [3 * ATTN_WIDTH, 3 * ATTN_WIDTH + 2 * GMLP_WIDTH], axis=-1)

    qkv = qkv.reshape(b, s, 3, N_ATTN_GROUPS, HEADS_PER_GROUP, HEAD_DIM)
    q = rotary(qkv[:, :, 0], cos, sin)
    k = rotary(qkv[:, :, 1], cos, sin)
    v = qkv[:, :, 2]
    outs, lses = [], []
    for gi, (window, dilation) in enumerate(ATTN_PATTERNS):
        o, lse = dilated_window_attention(q[:, :, gi], k[:, :, gi], v[:, :, gi], window, dilation)
        outs.append(o)
        lses.append(lse)
    wts = jax.nn.softmax(jnp.stack(lses), axis=0)
    y_attn = jnp.sum(wts[..., None] * jnp.stack(outs), axis=0).reshape(b, s, ATTN_OUT).astype(h.dtype)

    u, vg = jnp.split(jax.nn.gelu(z, approximate=False), 2, axis=-1)
    vg = layer_norm(vg, gmlp_ln_g, gmlp_ln_b)
    n_chunk = s // GMLP_CHUNK
    vg = vg.reshape(b, n_chunk, GMLP_CHUNK, GMLP_GROUPS, GMLP_GROUP_DIM)
    w_s = gmlp_w_s * jnp.tril(jnp.ones((GMLP_CHUNK, GMLP_CHUNK), gmlp_w_s.dtype))
    mixed = jnp.einsum('gts,bnsgc->bntgc', w_s, vg) + gmlp_b_s.T[:, :, None]
    y_gmlp = u * mixed.reshape(b, s, GMLP_WIDTH)

    branches = jnp.stack([y_attn @ w_attn_branch, y_gmlp @ w_gmlp_branch], axis=2)
    gates = jax.nn.sigmoid(g.reshape(b, s, N_BRANCHES, D_MODEL) + b_gates.reshape(N_BRANCHES, D_MODEL))
    return jnp.sum(gates * branches, axis=2) @ w_out


def _fwd_setup_inputs(seed: int = 0) -> dict:
    key = jax.random.key(seed)
    ks = jax.random.split(key, 32)

    def nrm(k, shape, scale):
        return jax.random.normal(k, shape, jnp.float32) * scale

    d_s = D_MODEL ** -0.5
    w_in = jnp.concatenate([
        nrm(ks[2], (DEPTH, D_MODEL, ATTN_WIDTH), d_s),
        nrm(ks[3], (DEPTH, D_MODEL, ATTN_WIDTH), d_s),
        nrm(ks[4], (DEPTH, D_MODEL, ATTN_WIDTH), BETA * d_s),
        nrm(ks[5], (DEPTH, D_MODEL, 2 * GMLP_WIDTH), d_s),
        nrm(ks[6], (DEPTH, D_MODEL, N_BRANCHES * D_MODEL), d_s),
    ], axis=-1)
    return {
        'x': jax.random.normal(ks[0], (BATCH, SEQ, D_MODEL), jnp.float32),
        'positions': jnp.broadcast_to(jnp.arange(SEQ, dtype=jnp.int32), (BATCH, SEQ)),
        'ffn1_w_gate': nrm(ks[7], (DEPTH, D_MODEL, D_FF), d_s),
        'ffn1_w_up': nrm(ks[8], (DEPTH, D_MODEL, D_FF), d_s),
        'ffn1_w_down': nrm(ks[9], (DEPTH, D_FF, D_MODEL), BETA * D_FF ** -0.5),
        'ln1_g': 1.0 + nrm(ks[10], (DEPTH, D_MODEL), 0.02),
        'ln1_b': nrm(ks[11], (DEPTH, D_MODEL), 0.02),
        'w_in': w_in,
        'b_gates': nrm(ks[12], (DEPTH, N_BRANCHES * D_MODEL), 0.02),
        'gmlp_ln_g': 1.0 + nrm(ks[13], (DEPTH, GMLP_WIDTH), 0.02),
        'gmlp_ln_b': nrm(ks[14], (DEPTH, GMLP_WIDTH), 0.02),
        'gmlp_w_s': nrm(ks[15], (DEPTH, GMLP_GROUPS, GMLP_CHUNK, GMLP_CHUNK), 0.5 * GMLP_CHUNK ** -0.5),
        'gmlp_b_s': 1.0 + nrm(ks[16], (DEPTH, GMLP_GROUPS, GMLP_CHUNK), 0.02),
        'w_attn_branch': nrm(ks[17], (DEPTH, ATTN_OUT, D_MODEL), BETA * ATTN_OUT ** -0.5),
        'w_gmlp_branch': nrm(ks[18], (DEPTH, GMLP_WIDTH, D_MODEL), BETA * GMLP_WIDTH ** -0.5),
        'w_out': nrm(ks[19], (DEPTH, D_MODEL, D_MODEL), BETA * d_s),
        'ln2_g': 1.0 + nrm(ks[20], (DEPTH, D_MODEL), 0.02),
        'ln2_b': nrm(ks[21], (DEPTH, D_MODEL), 0.02),
        'ffn2_w_gate': nrm(ks[22], (DEPTH, D_MODEL, D_FF), d_s),
        'ffn2_w_up': nrm(ks[23], (DEPTH, D_MODEL, D_FF), d_s),
        'ffn2_w_down': nrm(ks[24], (DEPTH, D_FF, D_MODEL), BETA * D_FF ** -0.5),
        'ln3_g': 1.0 + nrm(ks[25], (DEPTH, D_MODEL), 0.02),
        'ln3_b': nrm(ks[26], (DEPTH, D_MODEL), 0.02),
    }


def _fwd_reference(x, positions, ffn1_w_gate, ffn1_w_up, ffn1_w_down, ln1_g, ln1_b, w_in, b_gates,
              gmlp_ln_g, gmlp_ln_b, gmlp_w_s, gmlp_b_s, w_attn_branch, w_gmlp_branch, w_out,
              ln2_g, ln2_b, ffn2_w_gate, ffn2_w_up, ffn2_w_down, ln3_g, ln3_b):
    inv_freq = ROPE_THETA ** (-jnp.arange(0, HEAD_DIM, 2, dtype=jnp.float32) / HEAD_DIM)
    ang = positions.astype(jnp.float32)[..., None] * inv_freq
    cos, sin = jnp.cos(ang), jnp.sin(ang)
    h = x
    for l in range(DEPTH):
        h = layer_norm(ALPHA * h + 0.5 * swiglu_ffn(h, ffn1_w_gate[l], ffn1_w_up[l], ffn1_w_down[l]),
                       ln1_g[l], ln1_b[l])
        mix = hybrid_mixer(h, cos, sin, w_in[l], b_gates[l], gmlp_ln_g[l], gmlp_ln_b[l], gmlp_w_s[l],
                           gmlp_b_s[l], w_attn_branch[l], w_gmlp_branch[l], w_out[l])
        h = layer_norm(ALPHA * h + mix, ln2_g[l], ln2_b[l])
        h = layer_norm(ALPHA * h + 0.5 * swiglu_ffn(h, ffn2_w_gate[l], ffn2_w_up[l], ffn2_w_down[l]),
                       ln3_g[l], ln3_b[l])
    return h


import jax as _jax
import jax.numpy as _jnp

TWIN_FORMAT = 'train_step'
FWD_PARAMS = ['x', 'positions', 'ffn1_w_gate', 'ffn1_w_up', 'ffn1_w_down', 'ln1_g', 'ln1_b', 'w_in', 'b_gates', 'gmlp_ln_g', 'gmlp_ln_b', 'gmlp_w_s', 'gmlp_b_s', 'w_attn_branch', 'w_gmlp_branch', 'w_out', 'ln2_g', 'ln2_b', 'ffn2_w_gate', 'ffn2_w_up', 'ffn2_w_down', 'ln3_g', 'ln3_b']
TWIN_WEIGHTS = ['ffn1_w_gate', 'ffn1_w_up', 'ffn1_w_down', 'ln1_g', 'ln1_b', 'w_in', 'b_gates', 'gmlp_ln_g', 'gmlp_ln_b', 'gmlp_w_s', 'gmlp_b_s', 'w_attn_branch', 'w_gmlp_branch', 'w_out', 'ln2_g', 'ln2_b', 'ffn2_w_gate', 'ffn2_w_up', 'ffn2_w_down', 'ln3_g', 'ln3_b']
TWIN_DIFF_INPUT = 'x'
TWIN_INPUTS = ['x', 'positions', 'ffn1_w_gate', 'ffn1_w_up', 'ffn1_w_down', 'ln1_g', 'ln1_b', 'w_in', 'b_gates', 'gmlp_ln_g', 'gmlp_ln_b', 'gmlp_w_s', 'gmlp_b_s', 'w_attn_branch', 'w_gmlp_branch', 'w_out', 'ln2_g', 'ln2_b', 'ffn2_w_gate', 'ffn2_w_up', 'ffn2_w_down', 'ln3_g', 'ln3_b', 'loss_target', 'm_ffn1_w_gate', 'm_ffn1_w_up', 'm_ffn1_w_down', 'm_ln1_g', 'm_ln1_b', 'm_w_in', 'm_b_gates', 'm_gmlp_ln_g', 'm_gmlp_ln_b', 'm_gmlp_w_s', 'm_gmlp_b_s', 'm_w_attn_branch', 'm_w_gmlp_branch', 'm_w_out', 'm_ln2_g', 'm_ln2_b', 'm_ffn2_w_gate', 'm_ffn2_w_up', 'm_ffn2_w_down', 'm_ln3_g', 'm_ln3_b', 'v_ffn1_w_gate', 'v_ffn1_w_up', 'v_ffn1_w_down', 'v_ln1_g', 'v_ln1_b', 'v_w_in', 'v_b_gates', 'v_gmlp_ln_g', 'v_gmlp_ln_b', 'v_gmlp_w_s', 'v_gmlp_b_s', 'v_w_attn_branch', 'v_w_gmlp_branch', 'v_w_out', 'v_ln2_g', 'v_ln2_b', 'v_ffn2_w_gate', 'v_ffn2_w_up', 'v_ffn2_w_down', 'v_ln3_g', 'v_ln3_b']
TWIN_OUTPUTS = ['loss', 'grad_x', 'grad_ffn1_w_gate', 'grad_ffn1_w_up', 'grad_ffn1_w_down', 'grad_ln1_g', 'grad_ln1_b', 'grad_w_in', 'grad_b_gates', 'grad_gmlp_ln_g', 'grad_gmlp_ln_b', 'grad_gmlp_w_s', 'grad_gmlp_b_s', 'grad_w_attn_branch', 'grad_w_gmlp_branch', 'grad_w_out', 'grad_ln2_g', 'grad_ln2_b', 'grad_ffn2_w_gate', 'grad_ffn2_w_up', 'grad_ffn2_w_down', 'grad_ln3_g', 'grad_ln3_b', 'delta_ffn1_w_gate', 'delta_ffn1_w_up', 'delta_ffn1_w_down', 'delta_ln1_g', 'delta_ln1_b', 'delta_w_in', 'delta_b_gates', 'delta_gmlp_ln_g', 'delta_gmlp_ln_b', 'delta_gmlp_w_s', 'delta_gmlp_b_s', 'delta_w_attn_branch', 'delta_w_gmlp_branch', 'delta_w_out', 'delta_ln2_g', 'delta_ln2_b', 'delta_ffn2_w_gate', 'delta_ffn2_w_up', 'delta_ffn2_w_down', 'delta_ln3_g', 'delta_ln3_b', 'new_m_ffn1_w_gate', 'new_m_ffn1_w_up', 'new_m_ffn1_w_down', 'new_m_ln1_g', 'new_m_ln1_b', 'new_m_w_in', 'new_m_b_gates', 'new_m_gmlp_ln_g', 'new_m_gmlp_ln_b', 'new_m_gmlp_w_s', 'new_m_gmlp_b_s', 'new_m_w_attn_branch', 'new_m_w_gmlp_branch', 'new_m_w_out', 'new_m_ln2_g', 'new_m_ln2_b', 'new_m_ffn2_w_gate', 'new_m_ffn2_w_up', 'new_m_ffn2_w_down', 'new_m_ln3_g', 'new_m_ln3_b', 'new_v_ffn1_w_gate', 'new_v_ffn1_w_up', 'new_v_ffn1_w_down', 'new_v_ln1_g', 'new_v_ln1_b', 'new_v_w_in', 'new_v_b_gates', 'new_v_gmlp_ln_g', 'new_v_gmlp_ln_b', 'new_v_gmlp_w_s', 'new_v_gmlp_b_s', 'new_v_w_attn_branch', 'new_v_w_gmlp_branch', 'new_v_w_out', 'new_v_ln2_g', 'new_v_ln2_b', 'new_v_ffn2_w_gate', 'new_v_ffn2_w_up', 'new_v_ffn2_w_down', 'new_v_ln3_g', 'new_v_ln3_b']
TWIN_LEAF_KINDS = {'loss': 'loss', 'grad_x': 'grad_x', 'grad_ffn1_w_gate': 'grad_w', 'grad_ffn1_w_up': 'grad_w', 'grad_ffn1_w_down': 'grad_w', 'grad_ln1_g': 'grad_w', 'grad_ln1_b': 'grad_w', 'grad_w_in': 'grad_w', 'grad_b_gates': 'grad_w', 'grad_gmlp_ln_g': 'grad_w', 'grad_gmlp_ln_b': 'grad_w', 'grad_gmlp_w_s': 'grad_w', 'grad_gmlp_b_s': 'grad_w', 'grad_w_attn_branch': 'grad_w', 'grad_w_gmlp_branch': 'grad_w', 'grad_w_out': 'grad_w', 'grad_ln2_g': 'grad_w', 'grad_ln2_b': 'grad_w', 'grad_ffn2_w_gate': 'grad_w', 'grad_ffn2_w_up': 'grad_w', 'grad_ffn2_w_down': 'grad_w', 'grad_ln3_g': 'grad_w', 'grad_ln3_b': 'grad_w', 'delta_ffn1_w_gate': 'delta_w', 'delta_ffn1_w_up': 'delta_w', 'delta_ffn1_w_down': 'delta_w', 'delta_ln1_g': 'delta_w', 'delta_ln1_b': 'delta_w', 'delta_w_in': 'delta_w', 'delta_b_gates': 'delta_w', 'delta_gmlp_ln_g': 'delta_w', 'delta_gmlp_ln_b': 'delta_w', 'delta_gmlp_w_s': 'delta_w', 'delta_gmlp_b_s': 'delta_w', 'delta_w_attn_branch': 'delta_w', 'delta_w_gmlp_branch': 'delta_w', 'delta_w_out': 'delta_w', 'delta_ln2_g': 'delta_w', 'delta_ln2_b': 'delta_w', 'delta_ffn2_w_gate': 'delta_w', 'delta_ffn2_w_up': 'delta_w', 'delta_ffn2_w_down': 'delta_w', 'delta_ln3_g': 'delta_w', 'delta_ln3_b': 'delta_w', 'new_m_ffn1_w_gate': 'new_m', 'new_m_ffn1_w_up': 'new_m', 'new_m_ffn1_w_down': 'new_m', 'new_m_ln1_g': 'new_m', 'new_m_ln1_b': 'new_m', 'new_m_w_in': 'new_m', 'new_m_b_gates': 'new_m', 'new_m_gmlp_ln_g': 'new_m', 'new_m_gmlp_ln_b': 'new_m', 'new_m_gmlp_w_s': 'new_m', 'new_m_gmlp_b_s': 'new_m', 'new_m_w_attn_branch': 'new_m', 'new_m_w_gmlp_branch': 'new_m', 'new_m_w_out': 'new_m', 'new_m_ln2_g': 'new_m', 'new_m_ln2_b': 'new_m', 'new_m_ffn2_w_gate': 'new_m', 'new_m_ffn2_w_up': 'new_m', 'new_m_ffn2_w_down': 'new_m', 'new_m_ln3_g': 'new_m', 'new_m_ln3_b': 'new_m', 'new_v_ffn1_w_gate': 'new_v', 'new_v_ffn1_w_up': 'new_v', 'new_v_ffn1_w_down': 'new_v', 'new_v_ln1_g': 'new_v', 'new_v_ln1_b': 'new_v', 'new_v_w_in': 'new_v', 'new_v_b_gates': 'new_v', 'new_v_gmlp_ln_g': 'new_v', 'new_v_gmlp_ln_b': 'new_v', 'new_v_gmlp_w_s': 'new_v', 'new_v_gmlp_b_s': 'new_v', 'new_v_w_attn_branch': 'new_v', 'new_v_w_gmlp_branch': 'new_v', 'new_v_w_out': 'new_v', 'new_v_ln2_g': 'new_v', 'new_v_ln2_b': 'new_v', 'new_v_ffn2_w_gate': 'new_v', 'new_v_ffn2_w_up': 'new_v', 'new_v_ffn2_w_down': 'new_v', 'new_v_ln3_g': 'new_v', 'new_v_ln3_b': 'new_v'}


def _forward(args):
    return _fwd_reference(*[args[k] for k in FWD_PARAMS])


def _output_shape():
    out = _jax.eval_shape(lambda: _forward(_fwd_setup_inputs(0)))
    return out.shape, out.dtype

N_MICROBATCH = 1
ADAM_LR = 0.001
ADAM_B1 = 0.9
ADAM_B2 = 0.999
ADAM_EPS = 1e-08
ADAM_WD = 0.01
ADAM_STEP = 10
PER_EXAMPLE_BATCH_AXIS = {'x': 0, 'positions': 0, 'loss_target': 0}
SHARED_INPUTS = []
_WEIGHT_DTYPES = {'ffn1_w_gate': _jnp.float32, 'ffn1_w_up': _jnp.float32, 'ffn1_w_down': _jnp.float32, 'ln1_g': _jnp.float32, 'ln1_b': _jnp.float32, 'w_in': _jnp.float32, 'b_gates': _jnp.float32, 'gmlp_ln_g': _jnp.float32, 'gmlp_ln_b': _jnp.float32, 'gmlp_w_s': _jnp.float32, 'gmlp_b_s': _jnp.float32, 'w_attn_branch': _jnp.float32, 'w_gmlp_branch': _jnp.float32, 'w_out': _jnp.float32, 'ln2_g': _jnp.float32, 'ln2_b': _jnp.float32, 'ffn2_w_gate': _jnp.float32, 'ffn2_w_up': _jnp.float32, 'ffn2_w_down': _jnp.float32, 'ln3_g': _jnp.float32, 'ln3_b': _jnp.float32}
MOMENT_SCALE = {'ffn1_w_gate': 1.165712e-02, 'ffn1_w_up': 1.130968e-02, 'ffn1_w_down': 3.151204e-02, 'ln1_g': 5.901029e-01, 'ln1_b': 2.737507e-01, 'w_in': 5.887807e-03, 'b_gates': 4.515999e-03, 'gmlp_ln_g': 4.753456e-03, 'gmlp_ln_b': 4.744638e-03, 'gmlp_w_s': 9.458294e-03, 'gmlp_b_s': 1.400461e-02, 'w_attn_branch': 3.907336e-03, 'w_gmlp_branch': 2.807626e-02, 'w_out': 2.790652e-02, 'ln2_g': 5.973805e-01, 'ln2_b': 2.753592e-01, 'ffn2_w_gate': 1.151275e-02, 'ffn2_w_up': 1.113444e-02, 'ffn2_w_down': 3.109757e-02, 'ln3_g': 1.601883e+01, 'ln3_b': 7.665500e-01}


def _to_microbatches(a, axis):
    t = _jnp.moveaxis(a, axis, 0)
    t = t.reshape((N_MICROBATCH, t.shape[0] // N_MICROBATCH) + t.shape[1:])
    return _jnp.moveaxis(t, 1, axis + 1)


def setup_inputs(seed: int = 0) -> dict:
    inp = _fwd_setup_inputs(seed)
    key = _jax.random.fold_in(_jax.random.key(seed), 7919)
    shape, _ = _output_shape()
    out = dict(inp)
    out["loss_target"] = _jax.random.normal(_jax.random.fold_in(key, 0), shape, _jnp.float32)
    for i, name in enumerate(TWIN_WEIGHTS):
        w = inp[name].astype(_jnp.float32)
        if MOMENT_SCALE is None:
            s = _jnp.sqrt(_jnp.mean(_jnp.square(w)) + 1e-30)
        else:
            s = MOMENT_SCALE[name]
        km, kv = _jax.random.split(_jax.random.fold_in(key, i + 1))
        out[name] = w
        out["m_" + name] = s * _jax.random.normal(km, w.shape, _jnp.float32)
        out["v_" + name] = (s * s) * _jax.random.uniform(kv, w.shape, _jnp.float32, 0.5, 1.5)
    if N_MICROBATCH > 1:
        for name, axis in PER_EXAMPLE_BATCH_AXIS.items():
            out[name] = _to_microbatches(out[name], axis)
    return {'x': out['x'], 'positions': out['positions'], 'ffn1_w_gate': out['ffn1_w_gate'], 'ffn1_w_up': out['ffn1_w_up'], 'ffn1_w_down': out['ffn1_w_down'], 'ln1_g': out['ln1_g'], 'ln1_b': out['ln1_b'], 'w_in': out['w_in'], 'b_gates': out['b_gates'], 'gmlp_ln_g': out['gmlp_ln_g'], 'gmlp_ln_b': out['gmlp_ln_b'], 'gmlp_w_s': out['gmlp_w_s'], 'gmlp_b_s': out['gmlp_b_s'], 'w_attn_branch': out['w_attn_branch'], 'w_gmlp_branch': out['w_gmlp_branch'], 'w_out': out['w_out'], 'ln2_g': out['ln2_g'], 'ln2_b': out['ln2_b'], 'ffn2_w_gate': out['ffn2_w_gate'], 'ffn2_w_up': out['ffn2_w_up'], 'ffn2_w_down': out['ffn2_w_down'], 'ln3_g': out['ln3_g'], 'ln3_b': out['ln3_b'], 'loss_target': out['loss_target'], 'm_ffn1_w_gate': out['m_ffn1_w_gate'], 'm_ffn1_w_up': out['m_ffn1_w_up'], 'm_ffn1_w_down': out['m_ffn1_w_down'], 'm_ln1_g': out['m_ln1_g'], 'm_ln1_b': out['m_ln1_b'], 'm_w_in': out['m_w_in'], 'm_b_gates': out['m_b_gates'], 'm_gmlp_ln_g': out['m_gmlp_ln_g'], 'm_gmlp_ln_b': out['m_gmlp_ln_b'], 'm_gmlp_w_s': out['m_gmlp_w_s'], 'm_gmlp_b_s': out['m_gmlp_b_s'], 'm_w_attn_branch': out['m_w_attn_branch'], 'm_w_gmlp_branch': out['m_w_gmlp_branch'], 'm_w_out': out['m_w_out'], 'm_ln2_g': out['m_ln2_g'], 'm_ln2_b': out['m_ln2_b'], 'm_ffn2_w_gate': out['m_ffn2_w_gate'], 'm_ffn2_w_up': out['m_ffn2_w_up'], 'm_ffn2_w_down': out['m_ffn2_w_down'], 'm_ln3_g': out['m_ln3_g'], 'm_ln3_b': out['m_ln3_b'], 'v_ffn1_w_gate': out['v_ffn1_w_gate'], 'v_ffn1_w_up': out['v_ffn1_w_up'], 'v_ffn1_w_down': out['v_ffn1_w_down'], 'v_ln1_g': out['v_ln1_g'], 'v_ln1_b': out['v_ln1_b'], 'v_w_in': out['v_w_in'], 'v_b_gates': out['v_b_gates'], 'v_gmlp_ln_g': out['v_gmlp_ln_g'], 'v_gmlp_ln_b': out['v_gmlp_ln_b'], 'v_gmlp_w_s': out['v_gmlp_w_s'], 'v_gmlp_b_s': out['v_gmlp_b_s'], 'v_w_attn_branch': out['v_w_attn_branch'], 'v_w_gmlp_branch': out['v_w_gmlp_branch'], 'v_w_out': out['v_w_out'], 'v_ln2_g': out['v_ln2_g'], 'v_ln2_b': out['v_ln2_b'], 'v_ffn2_w_gate': out['v_ffn2_w_gate'], 'v_ffn2_w_up': out['v_ffn2_w_up'], 'v_ffn2_w_down': out['v_ffn2_w_down'], 'v_ln3_g': out['v_ln3_g'], 'v_ln3_b': out['v_ln3_b']}


def _loss(weights, diff, rest, loss_target):
    with _jax.named_scope("forward"):
        args = {**rest, TWIN_DIFF_INPUT: diff, **{k: w.astype(_WEIGHT_DTYPES[k]) for k, w in weights.items()}}
        y = _forward(args)
    with _jax.named_scope("loss_head"):
        err = _jnp.square(y.astype(_jnp.float32) - loss_target)
        return 0.5 * _jnp.sum(_jnp.mean(err, axis=-1)) if err.ndim else 0.5 * err


def _adamw(w, g, m, v):
    m = ADAM_B1 * m + (1.0 - ADAM_B1) * g
    v = ADAM_B2 * v + (1.0 - ADAM_B2) * _jnp.square(g)
    m_hat = m / (1.0 - ADAM_B1 ** ADAM_STEP)
    v_hat = v / (1.0 - ADAM_B2 ** ADAM_STEP)
    delta = -ADAM_LR * (m_hat / (_jnp.sqrt(v_hat) + ADAM_EPS) + ADAM_WD * w)
    return delta, m, v


def reference(x, positions, ffn1_w_gate, ffn1_w_up, ffn1_w_down, ln1_g, ln1_b, w_in, b_gates, gmlp_ln_g, gmlp_ln_b, gmlp_w_s, gmlp_b_s, w_attn_branch, w_gmlp_branch, w_out, ln2_g, ln2_b, ffn2_w_gate, ffn2_w_up, ffn2_w_down, ln3_g, ln3_b, loss_target, m_ffn1_w_gate, m_ffn1_w_up, m_ffn1_w_down, m_ln1_g, m_ln1_b, m_w_in, m_b_gates, m_gmlp_ln_g, m_gmlp_ln_b, m_gmlp_w_s, m_gmlp_b_s, m_w_attn_branch, m_w_gmlp_branch, m_w_out, m_ln2_g, m_ln2_b, m_ffn2_w_gate, m_ffn2_w_up, m_ffn2_w_down, m_ln3_g, m_ln3_b, v_ffn1_w_gate, v_ffn1_w_up, v_ffn1_w_down, v_ln1_g, v_ln1_b, v_w_in, v_b_gates, v_gmlp_ln_g, v_gmlp_ln_b, v_gmlp_w_s, v_gmlp_b_s, v_w_attn_branch, v_w_gmlp_branch, v_w_out, v_ln2_g, v_ln2_b, v_ffn2_w_gate, v_ffn2_w_up, v_ffn2_w_down, v_ln3_g, v_ln3_b):
    given = dict(x=x, positions=positions, ffn1_w_gate=ffn1_w_gate, ffn1_w_up=ffn1_w_up, ffn1_w_down=ffn1_w_down, ln1_g=ln1_g, ln1_b=ln1_b, w_in=w_in, b_gates=b_gates, gmlp_ln_g=gmlp_ln_g, gmlp_ln_b=gmlp_ln_b, gmlp_w_s=gmlp_w_s, gmlp_b_s=gmlp_b_s, w_attn_branch=w_attn_branch, w_gmlp_branch=w_gmlp_branch, w_out=w_out, ln2_g=ln2_g, ln2_b=ln2_b, ffn2_w_gate=ffn2_w_gate, ffn2_w_up=ffn2_w_up, ffn2_w_down=ffn2_w_down, ln3_g=ln3_g, ln3_b=ln3_b, loss_target=loss_target, m_ffn1_w_gate=m_ffn1_w_gate, m_ffn1_w_up=m_ffn1_w_up, m_ffn1_w_down=m_ffn1_w_down, m_ln1_g=m_ln1_g, m_ln1_b=m_ln1_b, m_w_in=m_w_in, m_b_gates=m_b_gates, m_gmlp_ln_g=m_gmlp_ln_g, m_gmlp_ln_b=m_gmlp_ln_b, m_gmlp_w_s=m_gmlp_w_s, m_gmlp_b_s=m_gmlp_b_s, m_w_attn_branch=m_w_attn_branch, m_w_gmlp_branch=m_w_gmlp_branch, m_w_out=m_w_out, m_ln2_g=m_ln2_g, m_ln2_b=m_ln2_b, m_ffn2_w_gate=m_ffn2_w_gate, m_ffn2_w_up=m_ffn2_w_up, m_ffn2_w_down=m_ffn2_w_down, m_ln3_g=m_ln3_g, m_ln3_b=m_ln3_b, v_ffn1_w_gate=v_ffn1_w_gate, v_ffn1_w_up=v_ffn1_w_up, v_ffn1_w_down=v_ffn1_w_down, v_ln1_g=v_ln1_g, v_ln1_b=v_ln1_b, v_w_in=v_w_in, v_b_gates=v_b_gates, v_gmlp_ln_g=v_gmlp_ln_g, v_gmlp_ln_b=v_gmlp_ln_b, v_gmlp_w_s=v_gmlp_w_s, v_gmlp_b_s=v_gmlp_b_s, v_w_attn_branch=v_w_attn_branch, v_w_gmlp_branch=v_w_gmlp_branch, v_w_out=v_w_out, v_ln2_g=v_ln2_g, v_ln2_b=v_ln2_b, v_ffn2_w_gate=v_ffn2_w_gate, v_ffn2_w_up=v_ffn2_w_up, v_ffn2_w_down=v_ffn2_w_down, v_ln3_g=v_ln3_g, v_ln3_b=v_ln3_b)
    weights = {n: given[n] for n in TWIN_WEIGHTS}
    shared = {n: given[n] for n in SHARED_INPUTS}
    per_example = {n: given[n] for n in ['x', 'positions']}
    grad_fn = _jax.value_and_grad(_loss, argnums=(0, 1))

    def one_microbatch(ex, loss_target):
        ex = dict(ex)
        diff = ex.pop(TWIN_DIFF_INPUT)
        return grad_fn(weights, diff, {**shared, **ex}, loss_target)

    if N_MICROBATCH == 1:
        loss, (grad_w, grad_x) = one_microbatch(per_example, given["loss_target"])
    else:
        def body(carry, xs):
            loss_sum, grad_sum = carry
            l_k, (gw_k, gx_k) = one_microbatch(xs[0], xs[1])
            with _jax.named_scope("update"):
                return (loss_sum + l_k, _jax.tree.map(_jnp.add, grad_sum, gw_k)), gx_k

        init = (_jnp.zeros((), _jnp.float32), _jax.tree.map(_jnp.zeros_like, weights))
        (loss, grad_w), grad_x = _jax.lax.scan(body, init, (per_example, given["loss_target"]))
    with _jax.named_scope("update"):
        delta_w, new_m, new_v = {}, {}, {}
        for n in TWIN_WEIGHTS:
            delta_w[n], new_m[n], new_v[n] = _adamw(weights[n], grad_w[n], given["m_" + n], given["v_" + n])
    return (loss, grad_x, *[grad_w[n] for n in TWIN_WEIGHTS], *[delta_w[n] for n in TWIN_WEIGHTS],
            *[new_m[n] for n in TWIN_WEIGHTS], *[new_v[n] for n in TWIN_WEIGHTS])
```

```python
import functools
import math

import jax
import jax.numpy as jnp
from jax import lax
from jax.experimental import pallas as pl
from jax.experimental.pallas import tpu as pltpu

F32 = jnp.float32
BF16 = jnp.bfloat16

N_DEV = 8
D = 1024
S = 2048
F = 2816
HEAD_DIM = 64
HEADS = 8
GW = HEADS * HEAD_DIM
PATTERN_DILATIONS = (1, 4, 16)
BLK = 128
QKV_W = 3 * 3 * GW
IN_W = QKV_W + 2 * D + 2 * D
ROPE_THETA = 10000.0
ALPHA = 2.0 ** 0.25
LN_EPS = 1e-5
ADAM_LR, ADAM_B1, ADAM_B2, ADAM_EPS, ADAM_WD, ADAM_STEP = 0.001, 0.9, 0.999, 1e-08, 0.01, 10
NEG = -1e30
MESH = pl.DeviceIdType.MESH


def _cp(sem=None, vmem_mb=None):
    kw = {}
    if sem is not None:
        kw["dimension_semantics"] = sem
    if vmem_mb is not None:
        kw["vmem_limit_bytes"] = vmem_mb << 20
    return pltpu.CompilerParams(**kw)


def _dot_nn(a, b):
    return lax.dot_general(a, b, (((1,), (0,)), ((), ())), preferred_element_type=F32)


def _dot_nt(a, b):
    return lax.dot_general(a, b, (((1,), (1,)), ((), ())), preferred_element_type=F32)


def _dot_tn(a, b):
    return lax.dot_general(a, b, (((0,), (0,)), ((), ())), preferred_element_type=F32)


def _ln_fwd(r, g, b):
    mu = jnp.mean(r, axis=-1, keepdims=True)
    xc = r - mu
    var = jnp.mean(xc * xc, axis=-1, keepdims=True)
    rstd = lax.rsqrt(var + LN_EPS)
    xhat = xc * rstd
    return xhat * g + b, xhat, rstd


def _ln_bwd(dh, xhat, rstd, g):
    dxh = dh * g
    m1 = jnp.mean(dxh, axis=-1, keepdims=True)
    m2 = jnp.mean(dxh * xhat, axis=-1, keepdims=True)
    return rstd * (dxh - m1 - xhat * m2)


def _sigmoid(x):
    return 1.0 / (1.0 + jnp.exp(-x))


def _colsum(x):
    return jnp.sum(x, axis=0, keepdims=True)


def _matmul(a, b, mode, out_dtype, tm, tn, tk, name, b_off=0, n_out=None):
    if mode == "nn":
        m, k = a.shape
        n = b.shape[1]
    elif mode == "nt":
        m, k = a.shape
        n = n_out if n_out is not None else b.shape[0]
    else:
        k, m = a.shape
        n = b.shape[1]
    nk = k // tk
    assert m % tm == 0 and n % tn == 0 and k % tk == 0
    dot = {"nn": _dot_nn, "nt": _dot_nt, "tn": _dot_tn}[mode]

    def body(a_ref, b_ref, o_ref, *scr):
        r = dot(a_ref[...].astype(BF16), b_ref[...].astype(BF16))
        if nk == 1:
            o_ref[...] = r.astype(out_dtype)
        else:
            acc = scr[0]
            kk = pl.program_id(2)

            @pl.when(kk == 0)
            def _():
                acc[...] = r

            @pl.when(kk > 0)
            def _():
                acc[...] += r

            @pl.when(kk == nk - 1)
            def _():
                o_ref[...] = acc[...].astype(out_dtype)

    if mode == "nn":
        a_spec = pl.BlockSpec((tm, tk), lambda i, j, kk: (i, kk))
        b_spec = pl.BlockSpec((tk, tn), lambda i, j, kk: (kk, j))
    elif mode == "nt":
        a_spec = pl.BlockSpec((tm, tk), lambda i, j, kk: (i, kk))
        b_spec = pl.BlockSpec((tn, tk), lambda i, j, kk: (j + b_off, kk))
    else:
        a_spec = pl.BlockSpec((tk, tm), lambda i, j, kk: (kk, i))
        b_spec = pl.BlockSpec((tk, tn), lambda i, j, kk: (kk, j))
    return pl.pallas_call(
        body,
        name=name,
        grid=(m // tm, n // tn, nk),
        in_specs=[a_spec, b_spec],
        out_specs=pl.BlockSpec((tm, tn), lambda i, j, kk: (i, j)),
        out_shape=jax.ShapeDtypeStruct((m, n), out_dtype),
        scratch_shapes=[] if nk == 1 else [pltpu.VMEM((tm, tn), F32)],
        compiler_params=_cp(("parallel", "parallel", "arbitrary"), 56),
    )(a, b)


def _ffn_fwd(x, wgt, wut, wd, ln_g, ln_b, name, tm=512, tk=256):
    nk = F // tk

    def body(x_ref, wg_ref, wu_ref, wd_ref, g_ref, b_ref, h_ref, hb_ref, xh_ref, rs_ref, a_ref, bb_ref, acc):
        kk = pl.program_id(1)
        xb = x_ref[...].astype(BF16)
        a = _dot_nt(xb, wg_ref[...])
        b = _dot_nt(xb, wu_ref[...])
        a_ref[...] = a
        bb_ref[...] = b
        hm = (a * _sigmoid(a)) * b
        contrib = _dot_nn(hm.astype(BF16), wd_ref[...])

        @pl.when(kk == 0)
        def _():
            acc[...] = contrib

        @pl.when(kk > 0)
        def _():
            acc[...] += contrib

        @pl.when(kk == nk - 1)
        def _():
            r = ALPHA * x_ref[...] + 0.5 * acc[...]
            h, xhat, rstd = _ln_fwd(r, g_ref[...], b_ref[...])
            h_ref[...] = h
            hb_ref[...] = h.astype(BF16)
            xh_ref[...] = xhat
            rs_ref[...] = rstd

    row = pl.BlockSpec((tm, D), lambda i, kk: (i, 0))
    wsp = pl.BlockSpec((tk, D), lambda i, kk: (kk, 0))
    vec = pl.BlockSpec((1, D), lambda i, kk: (0, 0))
    mid = pl.BlockSpec((tm, tk), lambda i, kk: (i, kk))
    return pl.pallas_call(
        body,
        name=name,
        grid=(S // tm, nk),
        in_specs=[row, wsp, wsp, wsp, vec, vec],
        out_specs=[row, row, row, pl.BlockSpec((tm, 1), lambda i, kk: (i, 0)), mid, mid],
        out_shape=[
            jax.ShapeDtypeStruct((S, D), F32),
            jax.ShapeDtypeStruct((S, D), BF16),
            jax.ShapeDtypeStruct((S, D), F32),
            jax.ShapeDtypeStruct((S, 1), F32),
            jax.ShapeDtypeStruct((S, F), F32),
            jax.ShapeDtypeStruct((S, F), F32),
        ],
        scratch_shapes=[pltpu.VMEM((tm, D), F32)],
        compiler_params=_cp(("parallel", "arbitrary"), 56),
    )(x, wgt, wut, wd, ln_g, ln_b)


def _ffn_bwd(dh_a, dh_b, xhat, rstd, ln_g, a, b, wgt, wut, wd, name, tm=512, tk=256):
    nk = F // tk
    two = dh_b is not None

    def body(*refs):
        if two:
            dha_ref, dhb_ref = refs[0], refs[1]
            refs = refs[2:]
        else:
            dha_ref = refs[0]
            refs = refs[1:]
        (xh_ref, rs_ref, g_ref, a_ref, b_ref, wg_ref, wu_ref, wd_ref,
         dx_ref, da_ref, db_ref, hm_ref, df_ref, dg_ref, dbias_ref, acc, df_scr) = refs
        i = pl.program_id(0)
        kk = pl.program_id(1)

        @pl.when(kk == 0)
        def _():
            dh = dha_ref[...]
            if two:
                dh = ALPHA * dh + dhb_ref[...]
            xhat = xh_ref[...]
            dr = _ln_bwd(dh, xhat, rs_ref[...], g_ref[...])
            dfb = (0.5 * dr).astype(BF16)
            df_scr[...] = dfb
            df_ref[...] = dfb
            acc[...] = ALPHA * dr
            sg = _colsum(dh * xhat)
            sb = _colsum(dh)

            @pl.when(i == 0)
            def _():
                dg_ref[...] = sg
                dbias_ref[...] = sb

            @pl.when(i > 0)
            def _():
                dg_ref[...] += sg
                dbias_ref[...] += sb

        dhm = _dot_nt(df_scr[...], wd_ref[...])
        av = a_ref[...]
        bv = b_ref[...]
        sig = _sigmoid(av)
        silu = av * sig
        da = (dhm * bv * (sig * (1.0 + av * (1.0 - sig)))).astype(BF16)
        db = (dhm * silu).astype(BF16)
        da_ref[...] = da
        db_ref[...] = db
        hm_ref[...] = (silu * bv).astype(BF16)
        acc[...] += _dot_nn(da, wg_ref[...]) + _dot_nn(db, wu_ref[...])

        @pl.when(kk == nk - 1)
        def _():
            dx_ref[...] = acc[...]

    row = pl.BlockSpec((tm, D), lambda i, kk: (i, 0))
    wsp = pl.BlockSpec((tk, D), lambda i, kk: (kk, 0))
    vec = pl.BlockSpec((1, D), lambda i, kk: (0, 0))
    mid = pl.BlockSpec((tm, tk), lambda i, kk: (i, kk))
    ins = [dh_a] + ([dh_b] if two else []) + [xhat, rstd, ln_g, a, b, wgt, wut, wd]
    in_specs = [row] * (2 if two else 1) + [row, pl.BlockSpec((tm, 1), lambda i, kk: (i, 0)), vec, mid, mid, wsp, wsp, wsp]
    return pl.pallas_call(
        body,
        name=name,
        grid=(S // tm, nk),
        in_specs=in_specs,
        out_specs=[row, mid, mid, mid, row, vec, vec],
        out_shape=[
            jax.ShapeDtypeStruct((S, D), F32),
            jax.ShapeDtypeStruct((S, F), BF16),
            jax.ShapeDtypeStruct((S, F), BF16),
            jax.ShapeDtypeStruct((S, F), BF16),
            jax.ShapeDtypeStruct((S, D), BF16),
            jax.ShapeDtypeStruct((1, D), F32),
            jax.ShapeDtypeStruct((1, D), F32),
        ],
        scratch_shapes=[pltpu.VMEM((tm, D), F32), pltpu.VMEM((tm, D), BF16)],
        compiler_params=_cp(("arbitrary", "arbitrary"), 56),
    )(*ins)


def _rope_tables(pos_col, invf, sign, tm=512):
    def body(p_ref, f_ref, s_ref, c_out, s_out):
        ang = p_ref[...].astype(F32) * f_ref[...]
        c_out[...] = jnp.cos(ang)
        s_out[...] = jnp.sin(ang) * s_ref[...]

    vec = pl.BlockSpec((1, BLK), lambda i: (0, 0))
    out = pl.BlockSpec((tm, BLK), lambda i: (i, 0))
    return pl.pallas_call(
        body,
        name="rope_tables",
        grid=(S // tm,),
        in_specs=[pl.BlockSpec((tm, 1), lambda i: (i, 0)), vec, vec],
        out_specs=[out, out],
        out_shape=[jax.ShapeDtypeStruct((S, BLK), F32)] * 2,
        compiler_params=_cp(("parallel",)),
    )(pos_col, invf, sign)


def _lane_lo():
    return (lax.broadcasted_iota(jnp.int32, (BLK, GW), 1) % HEAD_DIM) < (HEAD_DIM // 2)


def _swap_halves(t, lo):
    return jnp.where(lo, pltpu.roll(t, GW - HEAD_DIM // 2, 1), pltpu.roll(t, HEAD_DIM // 2, 1))


def _rope(t, cosf, sinf, lo):
    return t * cosf + _swap_halves(t, lo) * sinf


def _unrope(g, cosf, sinf, lo):
    return g * cosf + _swap_halves(g * sinf, lo)


def _tile4(v):
    return jnp.concatenate([v, v, v, v], axis=1)


def _band_mask(n):
    qi = lax.broadcasted_iota(jnp.int32, (BLK, 2 * BLK), 0)
    kj = lax.broadcasted_iota(jnp.int32, (BLK, 2 * BLK), 1)
    dist = qi + BLK - kj
    return (dist >= 0) & (dist <= BLK) & ((kj >= BLK) | (n >= 1))


def _attn_specs(gi, d):
    nq = QKV_W // GW

    def cur(off):
        return pl.BlockSpec((BLK, GW), lambda r, n: (n, r * nq + off + gi))

    def prev(off):
        return pl.BlockSpec((BLK, GW), lambda r, n: (jnp.maximum(n - 1, 0), r * nq + off + gi))

    tab_c = pl.BlockSpec((BLK, BLK), lambda r, n: (n, r))
    tab_p = pl.BlockSpec((BLK, BLK), lambda r, n: (jnp.maximum(n - 1, 0), r))
    blk = pl.BlockSpec((BLK, GW), lambda r, n: (n, r))
    return cur, prev, tab_c, tab_p, blk


def _attn_fwd(qkv, cos_t, sin_t, gi):
    d = PATTERN_DILATIONS[gi]
    sub = S // d
    nb = sub // BLK
    cur, prev, tab_c, tab_p, blk = _attn_specs(gi, d)

    def body(q_ref, kc_ref, kp_ref, vc_ref, vp_ref, cc_ref, cp_ref, sc_ref, sp_ref, o_ref, l_ref):
        n = pl.program_id(1)
        lo = _lane_lo()
        cc, sc = _tile4(cc_ref[...]), _tile4(sc_ref[...])
        cp, sp = _tile4(cp_ref[...]), _tile4(sp_ref[...])
        q = (_rope(q_ref[...], cc, sc, lo) * (HEAD_DIM ** -0.5)).astype(BF16)
        k = jnp.concatenate([_rope(kp_ref[...], cp, sp, lo), _rope(kc_ref[...], cc, sc, lo)], axis=0).astype(BF16)
        v = jnp.concatenate([vp_ref[...], vc_ref[...]], axis=0).astype(BF16)
        mask = _band_mask(n)
        outs, lses = [], []
        for h in range(HEADS):
            sl = slice(h * HEAD_DIM, (h + 1) * HEAD_DIM)
            s = jnp.where(mask, _dot_nt(q[:, sl], k[:, sl]), NEG)
            m = jnp.max(s, axis=1, keepdims=True)
            p = jnp.exp(s - m)
            l = jnp.sum(p, axis=1, keepdims=True)
            outs.append(_dot_nn((p / l).astype(BF16), v[:, sl]))
            lses.append(jnp.broadcast_to(m + jnp.log(l), (BLK, HEAD_DIM)))
        o_ref[...] = jnp.concatenate(outs, axis=1)
        l_ref[...] = jnp.concatenate(lses, axis=1)

    qv = qkv.reshape(sub, d * QKV_W)
    cv = cos_t.reshape(sub, d * BLK)
    sv = sin_t.reshape(sub, d * BLK)
    o, lse = pl.pallas_call(
        body,
        name=f"attn_fwd_g{gi}",
        grid=(d, nb),
        in_specs=[cur(0), cur(3), prev(3), cur(6), prev(6), tab_c, tab_p, tab_c, tab_p],
        out_specs=[blk, blk],
        out_shape=[jax.ShapeDtypeStruct((sub, d * GW), F32)] * 2,
        compiler_params=_cp(("parallel", "parallel")),
    )(qv, qv, qv, qv, qv, cv, cv, sv, sv)
    return o.reshape(S, GW), lse.reshape(S, GW)


def _attn_bwd(qkv, cos_t, sin_t, do, lse, cterm, gi):
    d = PATTERN_DILATIONS[gi]
    sub = S // d
    nb = sub // BLK
    cur, prev, tab_c, tab_p, blk = _attn_specs(gi, d)

    def body(q_ref, kc_ref, kp_ref, vc_ref, vp_ref, cc_ref, cp_ref, sc_ref, sp_ref, do_ref, l_ref, c_ref,
             dq_ref, dk_ref, dv_ref):
        n = pl.program_id(1)
        lo = _lane_lo()
        cc, sc = _tile4(cc_ref[...]), _tile4(sc_ref[...])
        cp, sp = _tile4(cp_ref[...]), _tile4(sp_ref[...])
        q = (_rope(q_ref[...], cc, sc, lo) * (HEAD_DIM ** -0.5)).astype(BF16)
        k = jnp.concatenate([_rope(kp_ref[...], cp, sp, lo), _rope(kc_ref[...], cc, sc, lo)], axis=0).astype(BF16)
        v = jnp.concatenate([vp_ref[...], vc_ref[...]], axis=0).astype(BF16)
        dov = do_ref[...]
        lse_v = l_ref[...]
        cv_ = c_ref[...]
        mask = _band_mask(n)
        dqs, dks, dvs = [], [], []
        for h in range(HEADS):
            sl = slice(h * HEAD_DIM, (h + 1) * HEAD_DIM)
            s = jnp.where(mask, _dot_nt(q[:, sl], k[:, sl]), NEG)
            p = jnp.exp(s - lse_v[:, h * HEAD_DIM:h * HEAD_DIM + 1])
            dp = _dot_nt(dov[:, sl], v[:, sl])
            ds = (p * (dp - cv_[:, h * HEAD_DIM:h * HEAD_DIM + 1])).astype(BF16)
            dqs.append(_dot_nn(ds, k[:, sl]) * (HEAD_DIM ** -0.5))
            dks.append(_dot_tn(ds, q[:, sl]))
            dvs.append(_dot_tn(p.astype(BF16), dov[:, sl]))
        dq_ref[...] = _unrope(jnp.concatenate(dqs, axis=1), cc, sc, lo).astype(BF16)
        dk = jnp.concatenate(dks, axis=1)
        dv = jnp.concatenate(dvs, axis=1)
        here = pl.ds(pl.multiple_of(n * BLK, BLK), BLK)
        dk_ref[here, :] = _unrope(dk[BLK:], cc, sc, lo)
        dv_ref[here, :] = dv[BLK:]

        @pl.when(n > 0)
        def _():
            before = pl.ds(pl.multiple_of((n - 1) * BLK, BLK), BLK)
            dk_ref[before, :] += _unrope(dk[:BLK], cp, sp, lo)
            dv_ref[before, :] += dv[:BLK]

    qv = qkv.reshape(sub, d * QKV_W)
    cv = cos_t.reshape(sub, d * BLK)
    sv = sin_t.reshape(sub, d * BLK)
    whole = pl.BlockSpec((sub, GW), lambda r, n: (0, r))
    dq, dk, dv = pl.pallas_call(
        body,
        name=f"attn_bwd_g{gi}",
        grid=(d, nb),
        in_specs=[cur(0), cur(3), prev(3), cur(6), prev(6), tab_c, tab_p, tab_c, tab_p, blk, blk, blk],
        out_specs=[blk, whole, whole],
        out_shape=[
            jax.ShapeDtypeStruct((sub, d * GW), BF16),
            jax.ShapeDtypeStruct((sub, d * GW), F32),
            jax.ShapeDtypeStruct((sub, d * GW), F32),
        ],
        compiler_params=_cp(("arbitrary", "arbitrary"), 48),
    )(qv, qv, qv, qv, qv, cv, cv, sv, sv, do.reshape(sub, d * GW), lse.reshape(sub, d * GW), cterm.reshape(sub, d * GW))
    return dq.reshape(S, GW), dk.reshape(S, GW), dv.reshape(S, GW)


def _group_weights(l0, l1, l2):
    mx = jnp.maximum(jnp.maximum(l0, l1), l2)
    e0, e1, e2 = jnp.exp(l0 - mx), jnp.exp(l1 - mx), jnp.exp(l2 - mx)
    inv = 1.0 / (e0 + e1 + e2)
    return e0 * inv, e1 * inv, e2 * inv


def _combine_fwd(os_, lses, tm=256):
    def body(o0, o1, o2, l0, l1, l2, y_ref):
        w0, w1, w2 = _group_weights(l0[...], l1[...], l2[...])
        y_ref[...] = (w0 * o0[...] + w1 * o1[...] + w2 * o2[...]).astype(BF16)

    sp = pl.BlockSpec((tm, GW), lambda i: (i, 0))
    return pl.pallas_call(
        body,
        name="attn_combine_fwd",
        grid=(S // tm,),
        in_specs=[sp] * 6,
        out_specs=sp,
        out_shape=jax.ShapeDtypeStruct((S, GW), BF16),
        compiler_params=_cp(("parallel",)),
    )(*os_, *lses)


def _combine_bwd(dy, os_, lses, seg, tm=256):
    def body(dy_ref, o0, o1, o2, l0, l1, l2, seg_ref, d0, d1, d2, c0, c1, c2):
        w0, w1, w2 = _group_weights(l0[...], l1[...], l2[...])
        dyv = dy_ref[...]
        y = w0 * o0[...] + w1 * o1[...] + w2 * o2[...]
        t = dyv * y
        t_hi = t.astype(BF16)
        r1 = t - t_hi.astype(F32)
        t_mid = r1.astype(BF16)
        t_lo = (r1 - t_mid.astype(F32)).astype(BF16)
        sg = seg_ref[...]
        e = _dot_nn(t_hi, sg) + _dot_nn(t_mid, sg) + _dot_nn(t_lo, sg)
        d0[...] = (w0 * dyv).astype(BF16)
        d1[...] = (w1 * dyv).astype(BF16)
        d2[...] = (w2 * dyv).astype(BF16)
        c0[...] = w0 * e
        c1[...] = w1 * e
        c2[...] = w2 * e

    sp = pl.BlockSpec((tm, GW), lambda i: (i, 0))
    return pl.pallas_call(
        body,
        name="attn_combine_bwd",
        grid=(S // tm,),
        in_specs=[sp] * 7 + [pl.BlockSpec((GW, GW), lambda i: (0, 0))],
        out_specs=[sp] * 6,
        out_shape=[jax.ShapeDtypeStruct((S, GW), BF16)] * 3 + [jax.ShapeDtypeStruct((S, GW), F32)] * 3,
        compiler_params=_cp(("parallel",)),
    )(dy, *os_, *lses, seg)


_SQRT_HALF = 0.7071067811865476
_INV_SQRT_2PI = 0.3989422804014327


def _gelu(z):
    return 0.5 * z * (1.0 + lax.erf(z * _SQRT_HALF))


def _gelu_grad(z):
    return 0.5 * (1.0 + lax.erf(z * _SQRT_HALF)) + z * (_INV_SQRT_2PI * jnp.exp(-0.5 * z * z))


def _tril_ws(ws_ref, g):
    t = lax.broadcasted_iota(jnp.int32, (BLK, BLK), 0)
    s = lax.broadcasted_iota(jnp.int32, (BLK, BLK), 1)
    return jnp.where(t >= s, ws_ref[g], 0.0)


def _gmlp_fwd(z, ws, bst, ln_g, ln_b, tm=256):
    nch = tm // BLK

    def body(z_ref, ws_ref, b_ref, g_ref, be_ref, y_ref):
        zg = _gelu(z_ref[...])
        u = zg[:, :D]
        vn, _, _ = _ln_fwd(zg[:, D:], g_ref[...], be_ref[...])
        vnb = vn.astype(BF16)
        bt = b_ref[...]
        for g in range(8):
            w = _tril_ws(ws_ref, g).astype(BF16)
            cols = slice(g * BLK, (g + 1) * BLK)
            for c in range(nch):
                rows = slice(c * BLK, (c + 1) * BLK)
                mixed = _dot_nn(w, vnb[rows, cols]) + bt[:, g:g + 1]
                y_ref[rows, cols] = (u[rows, cols] * mixed).astype(BF16)

    return pl.pallas_call(
        body,
        name="gmlp_fwd",
        grid=(S // tm,),
        in_specs=[
            pl.BlockSpec((tm, 2 * D), lambda i: (i, 0)),
            pl.BlockSpec((8, BLK, BLK), lambda i: (0, 0, 0)),
            pl.BlockSpec((BLK, 8), lambda i: (0, 0)),
            pl.BlockSpec((1, D), lambda i: (0, 0)),
            pl.BlockSpec((1, D), lambda i: (0, 0)),
        ],
        out_specs=pl.BlockSpec((tm, D), lambda i: (i, 0)),
        out_shape=jax.ShapeDtypeStruct((S, D), BF16),
        compiler_params=_cp(("parallel",), 48),
    )(z, ws, bst, ln_g, ln_b)


def _gmlp_bwd(z, dy, ws, bst, ln_g, ln_b, tm=256):
    nch = tm // BLK

    def body(z_ref, dy_ref, ws_ref, b_ref, g_ref, be_ref, dz_ref, dws_ref, dbs_ref, dg_ref, dbe_ref, dvn_scr, dm_acc):
        i = pl.program_id(0)
        zv = z_ref[...]
        zg = _gelu(zv)
        u = zg[:, :D]
        gam = g_ref[...]
        vn, xhat, rstd = _ln_fwd(zg[:, D:], gam, be_ref[...])
        vnb = vn.astype(BF16)
        dyv = dy_ref[...]
        dmix = dyv * u
        dmb = dmix.astype(BF16)
        bt = b_ref[...]
        tmask = lax.broadcasted_iota(jnp.int32, (BLK, BLK), 0) >= lax.broadcasted_iota(jnp.int32, (BLK, BLK), 1)
        dm_sum = dmix[0:BLK]
        for c in range(1, nch):
            dm_sum = dm_sum + dmix[c * BLK:(c + 1) * BLK]

        @pl.when(i == 0)
        def _():
            dm_acc[...] = dm_sum

        @pl.when(i > 0)
        def _():
            dm_acc[...] += dm_sum

        dus = []
        for g in range(8):
            w = _tril_ws(ws_ref, g).astype(BF16)
            cols = slice(g * BLK, (g + 1) * BLK)
            dw = None
            du_rows = []
            for c in range(nch):
                rows = slice(c * BLK, (c + 1) * BLK)
                mixed = _dot_nn(w, vnb[rows, cols]) + bt[:, g:g + 1]
                du_rows.append(dyv[rows, cols] * mixed)
                part = _dot_nt(dmb[rows, cols], vnb[rows, cols])
                dw = part if dw is None else dw + part
                dvn_scr[rows, cols] = _dot_tn(w, dmb[rows, cols])
            dus.append(jnp.concatenate(du_rows, axis=0))
            dw = jnp.where(tmask, dw, 0.0)

            @pl.when(i == 0)
            def _():
                dws_ref[g] = dw

            @pl.when(i > 0)
            def _():
                dws_ref[g] += dw

        dvn = dvn_scr[...]
        sg = _colsum(dvn * xhat)
        sb = _colsum(dvn)

        @pl.when(i == 0)
        def _():
            dg_ref[...] = sg
            dbe_ref[...] = sb

        @pl.when(i > 0)
        def _():
            dg_ref[...] += sg
            dbe_ref[...] += sb

        dvg = _ln_bwd(dvn, xhat, rstd, gam)
        gp = _gelu_grad(zv)
        dz_ref[:, :D] = (jnp.concatenate(dus, axis=1) * gp[:, :D]).astype(BF16)
        dz_ref[:, D:] = (dvg * gp[:, D:]).astype(BF16)

        @pl.when(i == S // tm - 1)
        def _():
            acc = dm_acc[...]
            for g in range(8):
                dbs_ref[:, g:g + 1] = jnp.sum(acc[:, g * BLK:(g + 1) * BLK], axis=1, keepdims=True)

    vec = pl.BlockSpec((1, D), lambda i: (0, 0))
    return pl.pallas_call(
        body,
        name="gmlp_bwd",
        grid=(S // tm,),
        in_specs=[
            pl.BlockSpec((tm, 2 * D), lambda i: (i, 0)),
            pl.BlockSpec((tm, D), lambda i: (i, 0)),
            pl.BlockSpec((8, BLK, BLK), lambda i: (0, 0, 0)),
            pl.BlockSpec((BLK, 8), lambda i: (0, 0)),
            vec,
            vec,
        ],
        out_specs=[
            pl.BlockSpec((tm, 2 * D), lambda i: (i, 0)),
            pl.BlockSpec((8, BLK, BLK), lambda i: (0, 0, 0)),
            pl.BlockSpec((BLK, 8), lambda i: (0, 0)),
            vec,
            vec,
        ],
        out_shape=[
            jax.ShapeDtypeStruct((S, 2 * D), BF16),
            jax.ShapeDtypeStruct((8, BLK, BLK), F32),
            jax.ShapeDtypeStruct((BLK, 8), F32),
            jax.ShapeDtypeStruct((1, D), F32),
            jax.ShapeDtypeStruct((1, D), F32),
        ],
        scratch_shapes=[pltpu.VMEM((tm, D), F32), pltpu.VMEM((BLK, D), F32)],
        compiler_params=_cp(("arbitrary",), 48),
    )(z, dy, ws, bst, ln_g, ln_b)


def _merge_fwd(ya, yg, glog, bgate, h1, wabt, wgb, wo, ln_g, ln_b, tm=256):
    def body(ya_ref, yg_ref, gl_ref, bg_ref, h1_ref, wab_ref, wgb_ref, wo_ref, g_ref, b_ref,
             h_ref, hb_ref, xh_ref, rs_ref, mg_ref, bra_ref, brg_ref):
        bra = _dot_nt(ya_ref[...], wab_ref[...])
        brg = _dot_nn(yg_ref[...], wgb_ref[...])
        gates = _sigmoid(gl_ref[...] + bg_ref[...])
        merged = (gates[:, :D] * bra + gates[:, D:] * brg).astype(BF16)
        mix = _dot_nn(merged, wo_ref[...])
        h, xhat, rstd = _ln_fwd(ALPHA * h1_ref[...] + mix, g_ref[...], b_ref[...])
        h_ref[...] = h
        hb_ref[...] = h.astype(BF16)
        xh_ref[...] = xhat
        rs_ref[...] = rstd
        mg_ref[...] = merged
        bra_ref[...] = bra
        brg_ref[...] = brg

    row = pl.BlockSpec((tm, D), lambda i: (i, 0))
    vec = pl.BlockSpec((1, D), lambda i: (0, 0))
    full = lambda shape: pl.BlockSpec(shape, lambda i: (0, 0))
    return pl.pallas_call(
        body,
        name="merge_fwd",
        grid=(S // tm,),
        in_specs=[
            pl.BlockSpec((tm, GW), lambda i: (i, 0)), row, pl.BlockSpec((tm, 2 * D), lambda i: (i, 0)), full((1, 2 * D)), row,
            full((D, GW)), full((D, D)), full((D, D)), vec, vec,
        ],
        out_specs=[row, row, row, pl.BlockSpec((tm, 1), lambda i: (i, 0)), row, row, row],
        out_shape=[
            jax.ShapeDtypeStruct((S, D), F32),
            jax.ShapeDtypeStruct((S, D), BF16),
            jax.ShapeDtypeStruct((S, D), F32),
            jax.ShapeDtypeStruct((S, 1), F32),
            jax.ShapeDtypeStruct((S, D), BF16),
            jax.ShapeDtypeStruct((S, D), F32),
            jax.ShapeDtypeStruct((S, D), F32),
        ],
        compiler_params=_cp(("parallel",), 48),
    )(ya, yg, glog, bgate, h1, wabt, wgb, wo, ln_g, ln_b)


def _merge_bwd(dh2, xhat, rstd, ln_g, bra, brg, glog, bgate, wabt, wgb, wo, tm=256):
    def body(dh_ref, xh_ref, rs_ref, g_ref, bra_ref, brg_ref, gl_ref, bg_ref, wab_ref, wgb_ref, wo_ref,
             dr_ref, drb_ref, dlog_ref, dba_ref, dbg_ref, dya_ref, dyg_ref, dbgate_ref, dg_ref, dbias_ref):
        i = pl.program_id(0)
        dh = dh_ref[...]
        xh = xh_ref[...]
        dr = _ln_bwd(dh, xh, rs_ref[...], g_ref[...])
        drb = dr.astype(BF16)
        dr_ref[...] = dr
        drb_ref[...] = drb
        dmerged = _dot_nt(drb, wo_ref[...])
        gates = _sigmoid(gl_ref[...] + bg_ref[...])
        g0, g1 = gates[:, :D], gates[:, D:]
        dl0 = dmerged * bra_ref[...] * g0 * (1.0 - g0)
        dl1 = dmerged * brg_ref[...] * g1 * (1.0 - g1)
        dlog_ref[:, :D] = dl0.astype(BF16)
        dlog_ref[:, D:] = dl1.astype(BF16)
        dba = (dmerged * g0).astype(BF16)
        dbg = (dmerged * g1).astype(BF16)
        dba_ref[...] = dba
        dbg_ref[...] = dbg
        dya_ref[...] = _dot_nn(dba, wab_ref[...])
        dyg_ref[...] = _dot_nt(dbg, wgb_ref[...])
        s0, s1 = _colsum(dl0), _colsum(dl1)
        sg, sb = _colsum(dh * xh), _colsum(dh)

        @pl.when(i == 0)
        def _():
            dbgate_ref[:, :D] = s0
            dbgate_ref[:, D:] = s1
            dg_ref[...] = sg
            dbias_ref[...] = sb

        @pl.when(i > 0)
        def _():
            dbgate_ref[:, :D] += s0
            dbgate_ref[:, D:] += s1
            dg_ref[...] += sg
            dbias_ref[...] += sb

    row = pl.BlockSpec((tm, D), lambda i: (i, 0))
    vec = pl.BlockSpec((1, D), lambda i: (0, 0))
    wide = pl.BlockSpec((tm, 2 * D), lambda i: (i, 0))
    full = lambda shape: pl.BlockSpec(shape, lambda i: (0, 0))
    return pl.pallas_call(
        body,
        name="merge_bwd",
        grid=(S // tm,),
        in_specs=[row, row, pl.BlockSpec((tm, 1), lambda i: (i, 0)), vec, row, row, wide, full((1, 2 * D)),
                  full((D, GW)), full((D, D)), full((D, D))],
        out_specs=[row, row, wide, row, row, pl.BlockSpec((tm, GW), lambda i: (i, 0)), row, full((1, 2 * D)), vec, vec],
        out_shape=[
            jax.ShapeDtypeStruct((S, D), F32),
            jax.ShapeDtypeStruct((S, D), BF16),
            jax.ShapeDtypeStruct((S, 2 * D), BF16),
            jax.ShapeDtypeStruct((S, D), BF16),
            jax.ShapeDtypeStruct((S, D), BF16),
            jax.ShapeDtypeStruct((S, GW), F32),
            jax.ShapeDtypeStruct((S, D), F32),
            jax.ShapeDtypeStruct((1, 2 * D), F32),
            jax.ShapeDtypeStruct((1, D), F32),
            jax.ShapeDtypeStruct((1, D), F32),
        ],
        compiler_params=_cp(("arbitrary",), 48),
    )(dh2, xhat, rstd, ln_g, bra, brg, glog, bgate, wabt, wgb, wo)


def _loss_head(h3, target, tm=512):
    def body(h_ref, t_ref, d_ref, l_ref):
        i = pl.program_id(0)
        e = h_ref[...] - t_ref[...]
        d_ref[...] = e * (1.0 / D)
        part = jnp.sum(_colsum(e * e), axis=1, keepdims=True) * (0.5 / D)

        @pl.when(i == 0)
        def _():
            l_ref[...] = part

        @pl.when(i > 0)
        def _():
            l_ref[...] += part

    row = pl.BlockSpec((tm, D), lambda i: (i, 0))
    return pl.pallas_call(
        body,
        name="loss_head",
        grid=(S // tm,),
        in_specs=[row, row],
        out_specs=[row, pl.BlockSpec((1, 1), lambda i: (0, 0))],
        out_shape=[jax.ShapeDtypeStruct((S, D), F32), jax.ShapeDtypeStruct((1, 1), F32)],
        compiler_params=_cp(("arbitrary",)),
    )(h3, target)


def _local_step(x, pos_col, target, w, p):
    h1, h1b, xh1, rs1, a1, b1 = _ffn_fwd(x, w["g1"], w["u1"], w["d1"], p["ln1_g"], p["ln1_b"], "ffn1_fwd")

    qkv = _matmul(h1b, w["win"], "nt", F32, 512, 512, D, "proj_qkv", b_off=0, n_out=QKV_W)
    z = _matmul(h1b, w["win"], "nt", F32, 512, 512, D, "proj_z", b_off=QKV_W // 512, n_out=2 * D)
    glog = _matmul(h1b, w["win"], "nt", F32, 512, 512, D, "proj_g", b_off=(QKV_W + 2 * D) // 512, n_out=2 * D)

    half = jnp.arange(0, HEAD_DIM, 2, dtype=F32) / HEAD_DIM
    inv_freq = ROPE_THETA ** (-half)
    invf = jnp.tile(inv_freq, 4).reshape(1, BLK)
    sign = jnp.tile(jnp.concatenate([-jnp.ones((32,), F32), jnp.ones((32,), F32)]), 2).reshape(1, BLK)
    cos_t, sin_t = _rope_tables(pos_col, invf, sign)

    os_, lses = [], []
    for gi in range(3):
        o, lse = _attn_fwd(qkv, cos_t, sin_t, gi)
        os_.append(o)
        lses.append(lse)
    ya = _combine_fwd(os_, lses)
    bst = p["gmlp_b_s"].T
    yg = _gmlp_fwd(z, p["gmlp_w_s"], bst, p["gmlp_ln_g"], p["gmlp_ln_b"])
    h2, h2b, xh2, rs2, merged, bra, brg = _merge_fwd(ya, yg, glog, p["b_gates"], h1, w["ab"], w["gb"], w["o"],
                                                      p["ln2_g"], p["ln2_b"])
    h3, _, xh3, rs3, a2, b2 = _ffn_fwd(h2, w["g2"], w["u2"], w["d2"], p["ln3_g"], p["ln3_b"], "ffn2_fwd")
    dh3, loss = _loss_head(h3, target)

    gw, gp = {}, {}
    dh2, da2, db2, hm2, df2, gp["ln3_g"], gp["ln3_b"] = _ffn_bwd(dh3, None, xh3, rs3, p["ln3_g"], a2, b2,
                                                                 w["g2"], w["u2"], w["d2"], "ffn2_bwd")
    gw["g2"] = _matmul(da2, h2b, "tn", BF16, 256, D, S, "wgrad_g2")
    gw["u2"] = _matmul(db2, h2b, "tn", BF16, 256, D, S, "wgrad_u2")
    gw["d2"] = _matmul(hm2, df2, "tn", BF16, 256, D, S, "wgrad_d2")

    (dr2, dr2b, dlog, dba, dbg, dya, dyg, gp["b_gates"], gp["ln2_g"], gp["ln2_b"]) = _merge_bwd(
        dh2, xh2, rs2, p["ln2_g"], bra, brg, glog, p["b_gates"], w["ab"], w["gb"], w["o"])
    gw["o"] = _matmul(merged, dr2b, "tn", BF16, 256, D, S, "wgrad_o")
    gw["ab"] = _matmul(dba, ya, "tn", BF16, 256, GW, S, "wgrad_ab")
    gw["gb"] = _matmul(yg, dbg, "tn", BF16, 256, D, S, "wgrad_gb")

    seg = (jnp.arange(GW)[:, None] // HEAD_DIM == jnp.arange(GW)[None, :] // HEAD_DIM).astype(BF16)
    do0, do1, do2, c0, c1, c2 = _combine_bwd(dya, os_, lses, seg)
    dqkv = []
    for gi, (do, ct) in enumerate(((do0, c0), (do1, c1), (do2, c2))):
        dqkv.append(_attn_bwd(qkv, cos_t, sin_t, do, lses[gi], ct, gi))
    dz, gp["gmlp_w_s"], dbst, gp["gmlp_ln_g"], gp["gmlp_ln_b"] = _gmlp_bwd(
        z, dyg, p["gmlp_w_s"], bst, p["gmlp_ln_g"], p["gmlp_ln_b"])
    gp["gmlp_b_s"] = dbst.T
    dproj = jnp.concatenate(
        [dqkv[gi][0] for gi in range(3)] + [dqkv[gi][1].astype(BF16) for gi in range(3)]
        + [dqkv[gi][2].astype(BF16) for gi in range(3)] + [dz, dlog], axis=1)
    gw["win"] = _matmul(dproj, h1b, "tn", BF16, 256, D, S, "wgrad_win")
    dh1m = _matmul(dproj, w["win"], "nn", F32, 512, D, 512, "dproj_to_dh1")

    dx, da1, db1, hm1, df1, gp["ln1_g"], gp["ln1_b"] = _ffn_bwd(dr2, dh1m, xh1, rs1, p["ln1_g"], a1, b1,
                                                                w["g1"], w["u1"], w["d1"], "ffn1_bwd")
    gw["g1"] = _matmul(da1, x, "tn", BF16, 256, D, S, "wgrad_g1")
    gw["u1"] = _matmul(db1, x, "tn", BF16, 256, D, S, "wgrad_u1")
    gw["d1"] = _matmul(hm1, df1, "tn", BF16, 256, D, S, "wgrad_d1")
    return loss, dx, gw, gp


_BIG = ("g1", "u1", "d1", "win", "ab", "gb", "o", "g2", "u2", "d2")
_FLIPS = [(mx, my, mc) for mx in (0, 1) for my in (0, 1) for mc in (0, 1)][1:]


def _exchange(arrays, scatter, name):
    nw = len(arrays)

    def body(*refs):
        ins, outs = refs[:nw], refs[nw:2 * nw]
        send_sems, recv_sems, local_sems = refs[2 * nw:]
        x, y, c = lax.axis_index("x"), lax.axis_index("y"), lax.axis_index("c")
        me = 4 * x + 2 * y + c
        local = []
        for k in range(nw):
            src = ins[k].at[me] if scatter else ins[k]
            cp = pltpu.make_async_copy(src, outs[k].at[me], local_sems.at[k])
            cp.start()
            local.append(cp)
        for mx, my, mc in _FLIPS:
            px, py, pc = x ^ mx, y ^ my, c ^ mc
            peer = 4 * px + 2 * py + pc
            for k in range(nw):
                src = ins[k].at[peer] if scatter else ins[k]
                pltpu.make_async_remote_copy(
                    src_ref=src, dst_ref=outs[k].at[me], send_sem=send_sems.at[k], recv_sem=recv_sems.at[k],
                    device_id=(px, py, pc), device_id_type=MESH).start()
        for k in range(nw):
            seven = outs[k].at[pl.ds(0, N_DEV - 1)]
            pltpu.make_async_remote_copy(
                src_ref=seven, dst_ref=seven, send_sem=send_sems.at[k], recv_sem=recv_sems.at[k],
                device_id=(x, y, c), device_id_type=MESH).wait()
            local[k].wait()

    hbm = pl.BlockSpec(memory_space=pl.ANY)
    out_shape = [jax.ShapeDtypeStruct(a.shape if scatter else (N_DEV,) + a.shape, a.dtype) for a in arrays]
    return pl.pallas_call(
        body,
        name=name,
        in_specs=[hbm] * nw,
        out_specs=[hbm] * nw,
        out_shape=out_shape,
        scratch_shapes=[pltpu.SemaphoreType.DMA((nw,)), pltpu.SemaphoreType.DMA((nw,)), pltpu.SemaphoreType.DMA((nw,))],
    )(*arrays)


def _sum8(parts, name, tm=None):
    _, r, c = parts.shape
    tm = tm or (r if r <= 512 else (r // 2 if (r // 2) % 8 == 0 and r // 2 <= 640 else 64))
    assert r % tm == 0

    def body(p_ref, o_ref):
        acc = p_ref[0].astype(F32)
        for j in range(1, N_DEV):
            acc = acc + p_ref[j].astype(F32)
        o_ref[...] = acc

    return pl.pallas_call(
        body,
        name=name,
        grid=(r // tm,),
        in_specs=[pl.BlockSpec((N_DEV, tm, c), lambda i: (0, i, 0))],
        out_specs=pl.BlockSpec((tm, c), lambda i: (i, 0)),
        out_shape=jax.ShapeDtypeStruct((r, c), F32),
        compiler_params=_cp(("parallel",), 48),
    )(parts)


def _adamw(wv, g, m, v, name):
    r, c = wv.shape
    tm = r if r <= 512 else 256
    assert r % tm == 0

    def body(w_ref, g_ref, m_ref, v_ref, d_ref, mo_ref, vo_ref):
        gv = g_ref[...]
        mn = ADAM_B1 * m_ref[...] + (1.0 - ADAM_B1) * gv
        vn = ADAM_B2 * v_ref[...] + (1.0 - ADAM_B2) * (gv * gv)
        m_hat = mn / (1.0 - ADAM_B1 ** ADAM_STEP)
        v_hat = vn / (1.0 - ADAM_B2 ** ADAM_STEP)
        d_ref[...] = -ADAM_LR * (m_hat / (jnp.sqrt(v_hat) + ADAM_EPS) + ADAM_WD * w_ref[...])
        mo_ref[...] = mn
        vo_ref[...] = vn

    sp = pl.BlockSpec((tm, c), lambda i: (i, 0))
    return pl.pallas_call(
        body,
        name=name,
        grid=(r // tm,),
        in_specs=[sp] * 4,
        out_specs=[sp] * 3,
        out_shape=[jax.ShapeDtypeStruct((r, c), F32)] * 3,
        compiler_params=_cp(("parallel",)),
    )(wv, g, m, v)


_WEIGHTS = ["ffn1_w_gate", "ffn1_w_up", "ffn1_w_down", "ln1_g", "ln1_b", "w_in", "b_gates", "gmlp_ln_g", "gmlp_ln_b",
            "gmlp_w_s", "gmlp_b_s", "w_attn_branch", "w_gmlp_branch", "w_out", "ln2_g", "ln2_b", "ffn2_w_gate",
            "ffn2_w_up", "ffn2_w_down", "ln3_g", "ln3_b"]
_BIG_OF = {"ffn1_w_gate": ("g1", True), "ffn1_w_up": ("u1", True), "ffn1_w_down": ("d1", False), "w_in": ("win", True),
           "w_attn_branch": ("ab", True), "w_gmlp_branch": ("gb", False), "w_out": ("o", False),
           "ffn2_w_gate": ("g2", True), "ffn2_w_up": ("u2", True), "ffn2_w_down": ("d2", False)}
_SMALL = [n for n in _WEIGHTS if n not in _BIG_OF]
_SMALL_ROWS = {"gmlp_w_s": 128, "b_gates": 2}
_SMALL_PAD = 144


def _pack_small(d):
    rows = [d[n].reshape(-1, D) for n in _SMALL]
    used = sum(r.shape[0] for r in rows)
    return jnp.concatenate(rows + [jnp.zeros((_SMALL_PAD - used, D), F32)], axis=0)


def _unpack_small(packed, shapes):
    out, at = {}, 0
    for n in _SMALL:
        k = _SMALL_ROWS.get(n, 1)
        out[n] = packed[at:at + k].reshape(shapes[n])
        at += k
    return out


def kernel(x, positions, ffn1_w_gate, ffn1_w_up, ffn1_w_down, ln1_g, ln1_b, w_in, b_gates, gmlp_ln_g, gmlp_ln_b, gmlp_w_s, gmlp_b_s, w_attn_branch, w_gmlp_branch, w_out, ln2_g, ln2_b, ffn2_w_gate, ffn2_w_up, ffn2_w_down, ln3_g, ln3_b, loss_target, m_ffn1_w_gate, m_ffn1_w_up, m_ffn1_w_down, m_ln1_g, m_ln1_b, m_w_in, m_b_gates, m_gmlp_ln_g, m_gmlp_ln_b, m_gmlp_w_s, m_gmlp_b_s, m_w_attn_branch, m_w_gmlp_branch, m_w_out, m_ln2_g, m_ln2_b, m_ffn2_w_gate, m_ffn2_w_up, m_ffn2_w_down, m_ln3_g, m_ln3_b, v_ffn1_w_gate, v_ffn1_w_up, v_ffn1_w_down, v_ln1_g, v_ln1_b, v_w_in, v_b_gates, v_gmlp_ln_g, v_gmlp_ln_b, v_gmlp_w_s, v_gmlp_b_s, v_w_attn_branch, v_w_gmlp_branch, v_w_out, v_ln2_g, v_ln2_b, v_ffn2_w_gate, v_ffn2_w_up, v_ffn2_w_down, v_ln3_g, v_ln3_b):
    args = dict(locals())
    wts = {n: args[n] for n in _WEIGHTS}
    ms = {n: args["m_" + n] for n in _WEIGHTS}
    vs = {n: args["v_" + n] for n in _WEIGHTS}

    shards = []
    for n, (key, tr) in _BIG_OF.items():
        s2 = wts[n][0]
        shards.append((s2.T if tr else s2).astype(BF16))
    gathered = _exchange(shards, False, "gather_weights")
    w = {key: g.reshape(-1, g.shape[-1]) for (key, _), g in zip(_BIG_OF.values(), gathered)}
    p = {n: (wts[n][0] if n in ("gmlp_w_s", "gmlp_b_s") else wts[n]) for n in _SMALL}

    loss, dx, gw, gp = _local_step(x[0], positions.reshape(S, 1), loss_target[0], w, p)
    loss = lax.psum(loss[0, 0], ("x", "y", "c"))

    parts = _exchange([gw[key].reshape(N_DEV, -1, gw[key].shape[-1]) for key, _ in _BIG_OF.values()], True, "scatter_grads")
    grads, deltas, new_m, new_v = {}, {}, {}, {}
    for (n, (key, tr)), part in zip(_BIG_OF.items(), parts):
        g = _sum8(part, "sum_" + key)
        g = g.T if tr else g
        dl, mn, vn = _adamw(wts[n][0], g, ms[n][0], vs[n][0], "adamw_" + key)
        grads[n], deltas[n], new_m[n], new_v[n] = g[None], dl[None], mn[None], vn[None]

    small_parts = _exchange([_pack_small({n: gp[n] for n in _SMALL})], False, "gather_small_grads")[0]
    g_small = _sum8(small_parts, "sum_small")
    dl, mn, vn = _adamw(_pack_small({n: wts[n] for n in _SMALL}), g_small,
                        _pack_small({n: ms[n] for n in _SMALL}), _pack_small({n: vs[n] for n in _SMALL}), "adamw_small")
    shapes = {n: wts[n].shape for n in _SMALL}
    for dst, packed in ((grads, g_small), (deltas, dl), (new_m, mn), (new_v, vn)):
        dst.update(_unpack_small(packed, shapes))

    return (loss, dx[None], *[grads[n] for n in _WEIGHTS], *[deltas[n] for n in _WEIGHTS],
            *[new_m[n] for n in _WEIGHTS], *[new_v[n] for n in _WEIGHTS])
```

```python
import functools
import math

import jax
import jax.numpy as jnp
from jax import lax
from jax.experimental import pallas as pl
from jax.experimental.pallas import tpu as pltpu

F32 = jnp.float32
BF16 = jnp.bfloat16

N_DEV = 8
D = 1024
S = 2048
F = 2816
HEAD_DIM = 64
HEADS = 8
GW = HEADS * HEAD_DIM
PATTERN_DILATIONS = (1, 4, 16)
BLK = 128
QKV_W = 3 * 3 * GW
IN_W = QKV_W + 2 * D + 2 * D
ROPE_THETA = 10000.0
ALPHA = 2.0 ** 0.25
LN_EPS = 1e-5
ADAM_LR, ADAM_B1, ADAM_B2, ADAM_EPS, ADAM_WD, ADAM_STEP = 0.001, 0.9, 0.999, 1e-08, 0.01, 10
NEG = -1e30
MESH = pl.DeviceIdType.MESH


def _cp(sem=None, vmem_mb=None):
    kw = {}
    if sem is not None:
        kw["dimension_semantics"] = sem
    if vmem_mb is not None:
        kw["vmem_limit_bytes"] = vmem_mb << 20
    return pltpu.CompilerParams(**kw)


def _dot_nn(a, b):
    return lax.dot_general(a, b, (((1,), (0,)), ((), ())), preferred_element_type=F32)


def _dot_nt(a, b):
    return lax.dot_general(a, b, (((1,), (1,)), ((), ())), preferred_element_type=F32)


def _dot_tn(a, b):
    return lax.dot_general(a, b, (((0,), (0,)), ((), ())), preferred_element_type=F32)


def _ln_fwd(r, g, b):
    mu = jnp.mean(r, axis=-1, keepdims=True)
    xc = r - mu
    var = jnp.mean(xc * xc, axis=-1, keepdims=True)
    rstd = lax.rsqrt(var + LN_EPS)
    xhat = xc * rstd
    return xhat * g + b, xhat, rstd


def _ln_bwd(dh, xhat, rstd, g):
    dxh = dh * g
    m1 = jnp.mean(dxh, axis=-1, keepdims=True)
    m2 = jnp.mean(dxh * xhat, axis=-1, keepdims=True)
    return rstd * (dxh - m1 - xhat * m2)


def _sigmoid(x):
    return 1.0 / (1.0 + jnp.exp(-x))


def _colsum(x):
    return jnp.sum(x, axis=0, keepdims=True)


def _matmul(a, b, mode, out_dtype, tm, tn, tk, name, b_off=0, n_out=None):
    if mode == "nn":
        m, k = a.shape
        n = b.shape[1]
    elif mode == "nt":
        m, k = a.shape
        n = n_out if n_out is not None else b.shape[0]
    else:
        k, m = a.shape
        n = b.shape[1]
    nk = k // tk
    assert m % tm == 0 and n % tn == 0 and k % tk == 0
    dot = {"nn": _dot_nn, "nt": _dot_nt, "tn": _dot_tn}[mode]

    def body(a_ref, b_ref, o_ref, *scr):
        r = dot(a_ref[...].astype(BF16), b_ref[...].astype(BF16))
        if nk == 1:
            o_ref[...] = r.astype(out_dtype)
        else:
            acc = scr[0]
            kk = pl.program_id(2)

            @pl.when(kk == 0)
            def _():
                acc[...] = r

            @pl.when(kk > 0)
            def _():
                acc[...] += r

            @pl.when(kk == nk - 1)
            def _():
                o_ref[...] = acc[...].astype(out_dtype)

    if mode == "nn":
        a_spec = pl.BlockSpec((tm, tk), lambda i, j, kk: (i, kk))
        b_spec = pl.BlockSpec((tk, tn), lambda i, j, kk: (kk, j))
    elif mode == "nt":
        a_spec = pl.BlockSpec((tm, tk), lambda i, j, kk: (i, kk))
        b_spec = pl.BlockSpec((tn, tk), lambda i, j, kk: (j + b_off, kk))
    else:
        a_spec = pl.BlockSpec((tk, tm), lambda i, j, kk: (kk, i))
        b_spec = pl.BlockSpec((tk, tn), lambda i, j, kk: (kk, j))
    return pl.pallas_call(
        body,
        name=name,
        grid=(m // tm, n // tn, nk),
        in_specs=[a_spec, b_spec],
        out_specs=pl.BlockSpec((tm, tn), lambda i, j, kk: (i, j)),
        out_shape=jax.ShapeDtypeStruct((m, n), out_dtype),
        scratch_shapes=[] if nk == 1 else [pltpu.VMEM((tm, tn), F32)],
        compiler_params=_cp(("parallel", "parallel", "arbitrary"), 56),
    )(a, b)


def _ffn_fwd(x, wgt, wut, wd, ln_g, ln_b, name, tm=512, tk=256):
    nk = F // tk

    def body(x_ref, wg_ref, wu_ref, wd_ref, g_ref, b_ref, h_ref, hb_ref, xh_ref, rs_ref, a_ref, bb_ref, acc):
        kk = pl.program_id(1)
        xb = x_ref[...].astype(BF16)
        a = _dot_nt(xb, wg_ref[...])
        b = _dot_nt(xb, wu_ref[...])
        a_ref[...] = a
        bb_ref[...] = b
        hm = (a * _sigmoid(a)) * b
        contrib = _dot_nn(hm.astype(BF16), wd_ref[...])

        @pl.when(kk == 0)
        def _():
            acc[...] = contrib

        @pl.when(kk > 0)
        def _():
            acc[...] += contrib

        @pl.when(kk == nk - 1)
        def _():
            r = ALPHA * x_ref[...] + 0.5 * acc[...]
            h, xhat, rstd = _ln_fwd(r, g_ref[...], b_ref[...])
            h_ref[...] = h
            hb_ref[...] = h.astype(BF16)
            xh_ref[...] = xhat
            rs_ref[...] = rstd

    row = pl.BlockSpec((tm, D), lambda i, kk: (i, 0))
    wsp = pl.BlockSpec((tk, D), lambda i, kk: (kk, 0))
    vec = pl.BlockSpec((1, D), lambda i, kk: (0, 0))
    mid = pl.BlockSpec((tm, tk), lambda i, kk: (i, kk))
    return pl.pallas_call(
        body,
        name=name,
        grid=(S // tm, nk),
        in_specs=[row, wsp, wsp, wsp, vec, vec],
        out_specs=[row, row, row, pl.BlockSpec((tm, 1), lambda i, kk: (i, 0)), mid, mid],
        out_shape=[
            jax.ShapeDtypeStruct((S, D), F32),
            jax.ShapeDtypeStruct((S, D), BF16),
            jax.ShapeDtypeStruct((S, D), F32),
            jax.ShapeDtypeStruct((S, 1), F32),
            jax.ShapeDtypeStruct((S, F), F32),
            jax.ShapeDtypeStruct((S, F), F32),
        ],
        scratch_shapes=[pltpu.VMEM((tm, D), F32)],
        compiler_params=_cp(("parallel", "arbitrary"), 56),
    )(x, wgt, wut, wd, ln_g, ln_b)


def _ffn_bwd(dh_a, dh_b, xhat, rstd, ln_g, a, b, wgt, wut, wd, name, tm=512, tk=256):
    nk = F // tk
    two = dh_b is not None

    def body(*refs):
        if two:
            dha_ref, dhb_ref = refs[0], refs[1]
            refs = refs[2:]
        else:
            dha_ref = refs[0]
            refs = refs[1:]
        (xh_ref, rs_ref, g_ref, a_ref, b_ref, wg_ref, wu_ref, wd_ref,
         dx_ref, da_ref, db_ref, hm_ref, df_ref, dg_ref, dbias_ref, acc, df_scr) = refs
        i = pl.program_id(0)
        kk = pl.program_id(1)

        @pl.when(kk == 0)
        def _():
            dh = dha_ref[...]
            if two:
                dh = ALPHA * dh + dhb_ref[...]
            xhat = xh_ref[...]
            dr = _ln_bwd(dh, xhat, rs_ref[...], g_ref[...])
            dfb = (0.5 * dr).astype(BF16)
            df_scr[...] = dfb
            df_ref[...] = dfb
            acc[...] = ALPHA * dr
            sg = _colsum(dh * xhat)
            sb = _colsum(dh)

            @pl.when(i == 0)
            def _():
                dg_ref[...] = sg
                dbias_ref[...] = sb

            @pl.when(i > 0)
            def _():
                dg_ref[...] += sg
                dbias_ref[...] += sb

        dhm = _dot_nt(df_scr[...], wd_ref[...])
        av = a_ref[...]
        bv = b_ref[...]
        sig = _sigmoid(av)
        silu = av * sig
        da = (dhm * bv * (sig * (1.0 + av * (1.0 - sig)))).astype(BF16)
        db = (dhm * silu).astype(BF16)
        da_ref[...] = da
        db_ref[...] = db
        hm_ref[...] = (silu * bv).astype(BF16)
        acc[...] += _dot_nn(da, wg_ref[...]) + _dot_nn(db, wu_ref[...])

        @pl.when(kk == nk - 1)
        def _():
            dx_ref[...] = acc[...]

    row = pl.BlockSpec((tm, D), lambda i, kk: (i, 0))
    wsp = pl.BlockSpec((tk, D), lambda i, kk: (kk, 0))
    vec = pl.BlockSpec((1, D), lambda i, kk: (0, 0))
    mid = pl.BlockSpec((tm, tk), lambda i, kk: (i, kk))
    ins = [dh_a] + ([dh_b] if two else []) + [xhat, rstd, ln_g, a, b, wgt, wut, wd]
    in_specs = [row] * (2 if two else 1) + [row, pl.BlockSpec((tm, 1), lambda i, kk: (i, 0)), vec, mid, mid, wsp, wsp, wsp]
    return pl.pallas_call(
        body,
        name=name,
        grid=(S // tm, nk),
        in_specs=in_specs,
        out_specs=[row, mid, mid, mid, row, vec, vec],
        out_shape=[
            jax.ShapeDtypeStruct((S, D), F32),
            jax.ShapeDtypeStruct((S, F), BF16),
            jax.ShapeDtypeStruct((S, F), BF16),
            jax.ShapeDtypeStruct((S, F), BF16),
            jax.ShapeDtypeStruct((S, D), BF16),
            jax.ShapeDtypeStruct((1, D), F32),
            jax.ShapeDtypeStruct((1, D), F32),
        ],
        scratch_shapes=[pltpu.VMEM((tm, D), F32), pltpu.VMEM((tm, D), BF16)],
        compiler_params=_cp(("arbitrary", "arbitrary"), 56),
    )(*ins)


def _rope_tables(pos_col, invf, sign, tm=512):
    def body(p_ref, f_ref, s_ref, c_out, s_out):
        ang = p_ref[...].astype(F32) * f_ref[...]
        c_out[...] = jnp.cos(ang)
        s_out[...] = jnp.sin(ang) * s_ref[...]

    vec = pl.BlockSpec((1, BLK), lambda i: (0, 0))
    out = pl.BlockSpec((tm, BLK), lambda i: (i, 0))
    return pl.pallas_call(
        body,
        name="rope_tables",
        grid=(S // tm,),
        in_specs=[pl.BlockSpec((tm, 1), lambda i: (i, 0)), vec, vec],
        out_specs=[out, out],
        out_shape=[jax.ShapeDtypeStruct((S, BLK), F32)] * 2,
        compiler_params=_cp(("parallel",)),
    )(pos_col, invf, sign)


def _lane_lo():
    return (lax.broadcasted_iota(jnp.int32, (BLK, GW), 1) % HEAD_DIM) < (HEAD_DIM // 2)


def _swap_halves(t, lo):
    return jnp.where(lo, pltpu.roll(t, GW - HEAD_DIM // 2, 1), pltpu.roll(t, HEAD_DIM // 2, 1))


def _rope(t, cosf, sinf, lo):
    return t * cosf + _swap_halves(t, lo) * sinf


def _unrope(g, cosf, sinf, lo):
    return g * cosf + _swap_halves(g * sinf, lo)


def _tile4(v):
    return jnp.concatenate([v, v, v, v], axis=1)


def _band_mask(n):
    qi = lax.broadcasted_iota(jnp.int32, (BLK, 2 * BLK), 0)
    kj = lax.broadcasted_iota(jnp.int32, (BLK, 2 * BLK), 1)
    dist = qi + BLK - kj
    return (dist >= 0) & (dist <= BLK) & ((kj >= BLK) | (n >= 1))


def _attn_specs(gi, d):
    nq = QKV_W // GW

    def cur(off):
        return pl.BlockSpec((BLK, GW), lambda r, n: (n, r * nq + off + gi))

    def prev(off):
        return pl.BlockSpec((BLK, GW), lambda r, n: (jnp.maximum(n - 1, 0), r * nq + off + gi))

    tab_c = pl.BlockSpec((BLK, BLK), lambda r, n: (n, r))
    tab_p = pl.BlockSpec((BLK, BLK), lambda r, n: (jnp.maximum(n - 1, 0), r))
    blk = pl.BlockSpec((BLK, GW), lambda r, n: (n, r))
    return cur, prev, tab_c, tab_p, blk


def _attn_fwd(qkv, cos_t, sin_t, gi):
    d = PATTERN_DILATIONS[gi]
    sub = S // d
    nb = sub // BLK
    cur, prev, tab_c, tab_p, blk = _attn_specs(gi, d)

    def body(q_ref, kc_ref, kp_ref, vc_ref, vp_ref, cc_ref, cp_ref, sc_ref, sp_ref, o_ref, l_ref):
        n = pl.program_id(1)
        lo = _lane_lo()
        cc, sc = _tile4(cc_ref[...]), _tile4(sc_ref[...])
        cp, sp = _tile4(cp_ref[...]), _tile4(sp_ref[...])
        q = (_rope(q_ref[...], cc, sc, lo) * (HEAD_DIM ** -0.5)).astype(BF16)
        k = jnp.concatenate([_rope(kp_ref[...], cp, sp, lo), _rope(kc_ref[...], cc, sc, lo)], axis=0).astype(BF16)
        v = jnp.concatenate([vp_ref[...], vc_ref[...]], axis=0).astype(BF16)
        mask = _band_mask(n)
        outs, lses = [], []
        for h in range(HEADS):
            sl = slice(h * HEAD_DIM, (h + 1) * HEAD_DIM)
            s = jnp.where(mask, _dot_nt(q[:, sl], k[:, sl]), NEG)
            m = jnp.max(s, axis=1, keepdims=True)
            p = jnp.exp(s - m)
            l = jnp.sum(p, axis=1, keepdims=True)
            outs.append(_dot_nn((p / l).astype(BF16), v[:, sl]))
            lses.append(jnp.broadcast_to(m + jnp.log(l), (BLK, HEAD_DIM)))
        o_ref[...] = jnp.concatenate(outs, axis=1)
        l_ref[...] = jnp.concatenate(lses, axis=1)

    qv = qkv.reshape(sub, d * QKV_W)
    cv = cos_t.reshape(sub, d * BLK)
    sv = sin_t.reshape(sub, d * BLK)
    o, lse = pl.pallas_call(
        body,
        name=f"attn_fwd_g{gi}",
        grid=(d, nb),
        in_specs=[cur(0), cur(3), prev(3), cur(6), prev(6), tab_c, tab_p, tab_c, tab_p],
        out_specs=[blk, blk],
        out_shape=[jax.ShapeDtypeStruct((sub, d * GW), F32)] * 2,
        compiler_params=_cp(("parallel", "parallel")),
    )(qv, qv, qv, qv, qv, cv, cv, sv, sv)
    return o.reshape(S, GW), lse.reshape(S, GW)


def _attn_bwd(qkv, cos_t, sin_t, do, lse, cterm, gi):
    d = PATTERN_DILATIONS[gi]
    sub = S // d
    nb = sub // BLK
    cur, prev, tab_c, tab_p, blk = _attn_specs(gi, d)

    def body(q_ref, kc_ref, kp_ref, vc_ref, vp_ref, cc_ref, cp_ref, sc_ref, sp_ref, do_ref, l_ref, c_ref,
             dq_ref, dk_ref, dv_ref):
        n = pl.program_id(1)
        lo = _lane_lo()
        cc, sc = _tile4(cc_ref[...]), _tile4(sc_ref[...])
        cp, sp = _tile4(cp_ref[...]), _tile4(sp_ref[...])
        q = (_rope(q_ref[...], cc, sc, lo) * (HEAD_DIM ** -0.5)).astype(BF16)
        k = jnp.concatenate([_rope(kp_ref[...], cp, sp, lo), _rope(kc_ref[...], cc, sc, lo)], axis=0).astype(BF16)
        v = jnp.concatenate([vp_ref[...], vc_ref[...]], axis=0).astype(BF16)
        dov = do_ref[...]
        lse_v = l_ref[...]
        cv_ = c_ref[...]
        mask = _band_mask(n)
        dqs, dks, dvs = [], [], []
        for h in range(HEADS):
            sl = slice(h * HEAD_DIM, (h + 1) * HEAD_DIM)
            s = jnp.where(mask, _dot_nt(q[:, sl], k[:, sl]), NEG)
            p = jnp.exp(s - lse_v[:, h * HEAD_DIM:h * HEAD_DIM + 1])
            dp = _dot_nt(dov[:, sl], v[:, sl])
            ds = (p * (dp - cv_[:, h * HEAD_DIM:h * HEAD_DIM + 1])).astype(BF16)
            dqs.append(_dot_nn(ds, k[:, sl]) * (HEAD_DIM ** -0.5))
            dks.append(_dot_tn(ds, q[:, sl]))
            dvs.append(_dot_tn(p.astype(BF16), dov[:, sl]))
        dq_ref[...] = _unrope(jnp.concatenate(dqs, axis=1), cc, sc, lo).astype(BF16)
        dk = jnp.concatenate(dks, axis=1)
        dv = jnp.concatenate(dvs, axis=1)
        here = pl.ds(pl.multiple_of(n * BLK, BLK), BLK)
        dk_ref[here, :] = _unrope(dk[BLK:], cc, sc, lo)
        dv_ref[here, :] = dv[BLK:]

        @pl.when(n > 0)
        def _():
            before = pl.ds(pl.multiple_of((n - 1) * BLK, BLK), BLK)
            dk_ref[before, :] += _unrope(dk[:BLK], cp, sp, lo)
            dv_ref[before, :] += dv[:BLK]

    qv = qkv.reshape(sub, d * QKV_W)
    cv = cos_t.reshape(sub, d * BLK)
    sv = sin_t.reshape(sub, d * BLK)
    whole = pl.BlockSpec((sub, GW), lambda r, n: (0, r))
    dq, dk, dv = pl.pallas_call(
        body,
        name=f"attn_bwd_g{gi}",
        grid=(d, nb),
        in_specs=[cur(0), cur(3), prev(3), cur(6), prev(6), tab_c, tab_p, tab_c, tab_p, blk, blk, blk],
        out_specs=[blk, whole, whole],
        out_shape=[
            jax.ShapeDtypeStruct((sub, d * GW), BF16),
            jax.ShapeDtypeStruct((sub, d * GW), F32),
            jax.ShapeDtypeStruct((sub, d * GW), F32),
        ],
        compiler_params=_cp(("arbitrary", "arbitrary"), 48),
    )(qv, qv, qv, qv, qv, cv, cv, sv, sv, do.reshape(sub, d * GW), lse.reshape(sub, d * GW), cterm.reshape(sub, d * GW))
    return dq.reshape(S, GW), dk.reshape(S, GW), dv.reshape(S, GW)


def _group_weights(l0, l1, l2):
    mx = jnp.maximum(jnp.maximum(l0, l1), l2)
    e0, e1, e2 = jnp.exp(l0 - mx), jnp.exp(l1 - mx), jnp.exp(l2 - mx)
    inv = 1.0 / (e0 + e1 + e2)
    return e0 * inv, e1 * inv, e2 * inv


def _combine_fwd(os_, lses, tm=256):
    def body(o0, o1, o2, l0, l1, l2, y_ref):
        w0, w1, w2 = _group_weights(l0[...], l1[...], l2[...])
        y_ref[...] = (w0 * o0[...] + w1 * o1[...] + w2 * o2[...]).astype(BF16)

    sp = pl.BlockSpec((tm, GW), lambda i: (i, 0))
    return pl.pallas_call(
        body,
        name="attn_combine_fwd",
        grid=(S // tm,),
        in_specs=[sp] * 6,
        out_specs=sp,
        out_shape=jax.ShapeDtypeStruct((S, GW), BF16),
        compiler_params=_cp(("parallel",)),
    )(*os_, *lses)


def _combine_bwd(dy, os_, lses, seg, tm=256):
    def body(dy_ref, o0, o1, o2, l0, l1, l2, seg_ref, d0, d1, d2, c0, c1, c2):
        w0, w1, w2 = _group_weights(l0[...], l1[...], l2[...])
        dyv = dy_ref[...]
        y = w0 * o0[...] + w1 * o1[...] + w2 * o2[...]
        t = dyv * y
        t_hi = t.astype(BF16)
        r1 = t - t_hi.astype(F32)
        t_mid = r1.astype(BF16)
        t_lo = (r1 - t_mid.astype(F32)).astype(BF16)
        sg = seg_ref[...]
        e = _dot_nn(t_hi, sg) + _dot_nn(t_mid, sg) + _dot_nn(t_lo, sg)
        d0[...] = (w0 * dyv).astype(BF16)
        d1[...] = (w1 * dyv).astype(BF16)
        d2[...] = (w2 * dyv).astype(BF16)
        c0[...] = w0 * e
        c1[...] = w1 * e
        c2[...] = w2 * e

    sp = pl.BlockSpec((tm, GW), lambda i: (i, 0))
    return pl.pallas_call(
        body,
        name="attn_combine_bwd",
        grid=(S // tm,),
        in_specs=[sp] * 7 + [pl.BlockSpec((GW, GW), lambda i: (0, 0))],
        out_specs=[sp] * 6,
        out_shape=[jax.ShapeDtypeStruct((S, GW), BF16)] * 3 + [jax.ShapeDtypeStruct((S, GW), F32)] * 3,
        compiler_params=_cp(("parallel",)),
    )(dy, *os_, *lses, seg)


_SQRT_HALF = 0.7071067811865476
_INV_SQRT_2PI = 0.3989422804014327


def _gelu(z):
    return 0.5 * z * (1.0 + lax.erf(z * _SQRT_HALF))


def _gelu_grad(z):
    return 0.5 * (1.0 + lax.erf(z * _SQRT_HALF)) + z * (_INV_SQRT_2PI * jnp.exp(-0.5 * z * z))


def _tril_ws(ws_ref, g):
    t = lax.broadcasted_iota(jnp.int32, (BLK, BLK), 0)
    s = lax.broadcasted_iota(jnp.int32, (BLK, BLK), 1)
    return jnp.where(t >= s, ws_ref[g], 0.0)


def _gmlp_fwd(z, ws, bst, ln_g, ln_b, tm=256):
    nch = tm // BLK

    def body(z_ref, ws_ref, b_ref, g_ref, be_ref, y_ref):
        zg = _gelu(z_ref[...])
        u = zg[:, :D]
        vn, _, _ = _ln_fwd(zg[:, D:], g_ref[...], be_ref[...])
        vnb = vn.astype(BF16)
        bt = b_ref[...]
        for g in range(8):
            w = _tril_ws(ws_ref, g).astype(BF16)
            cols = slice(g * BLK, (g + 1) * BLK)
            for c in range(nch):
                rows = slice(c * BLK, (c + 1) * BLK)
                mixed = _dot_nn(w, vnb[rows, cols]) + bt[:, g:g + 1]
                y_ref[rows, cols] = (u[rows, cols] * mixed).astype(BF16)

    return pl.pallas_call(
        body,
        name="gmlp_fwd",
        grid=(S // tm,),
        in_specs=[
            pl.BlockSpec((tm, 2 * D), lambda i: (i, 0)),
            pl.BlockSpec((8, BLK, BLK), lambda i: (0, 0, 0)),
            pl.BlockSpec((BLK, 8), lambda i: (0, 0)),
            pl.BlockSpec((1, D), lambda i: (0, 0)),
            pl.BlockSpec((1, D), lambda i: (0, 0)),
        ],
        out_specs=pl.BlockSpec((tm, D), lambda i: (i, 0)),
        out_shape=jax.ShapeDtypeStruct((S, D), BF16),
        compiler_params=_cp(("parallel",), 48),
    )(z, ws, bst, ln_g, ln_b)


def _gmlp_bwd(z, dy, ws, bst, ln_g, ln_b, tm=256):
    nch = tm // BLK

    def body(z_ref, dy_ref, ws_ref, b_ref, g_ref, be_ref, dz_ref, dws_ref, dbs_ref, dg_ref, dbe_ref, dvn_scr, dm_acc):
        i = pl.program_id(0)
        zv = z_ref[...]
        zg = _gelu(zv)
        u = zg[:, :D]
        gam = g_ref[...]
        vn, xhat, rstd = _ln_fwd(zg[:, D:], gam, be_ref[...])
        vnb = vn.astype(BF16)
        dyv = dy_ref[...]
        dmix = dyv * u
        dmb = dmix.astype(BF16)
        bt = b_ref[...]
        tmask = lax.broadcasted_iota(jnp.int32, (BLK, BLK), 0) >= lax.broadcasted_iota(jnp.int32, (BLK, BLK), 1)
        dm_sum = dmix[0:BLK]
        for c in range(1, nch):
            dm_sum = dm_sum + dmix[c * BLK:(c + 1) * BLK]

        @pl.when(i == 0)
        def _():
            dm_acc[...] = dm_sum

        @pl.when(i > 0)
        def _():
            dm_acc[...] += dm_sum

        dus = []
        for g in range(8):
            w = _tril_ws(ws_ref, g).astype(BF16)
            cols = slice(g * BLK, (g + 1) * BLK)
            dw = None
            du_rows = []
            for c in range(nch):
                rows = slice(c * BLK, (c + 1) * BLK)
                mixed = _dot_nn(w, vnb[rows, cols]) + bt[:, g:g + 1]
                du_rows.append(dyv[rows, cols] * mixed)
                part = _dot_nt(dmb[rows, cols], vnb[rows, cols])
                dw = part if dw is None else dw + part
                dvn_scr[rows, cols] = _dot_tn(w, dmb[rows, cols])
            dus.append(jnp.concatenate(du_rows, axis=0))
            dw = jnp.where(tmask, dw, 0.0)

            @pl.when(i == 0)
            def _():
                dws_ref[g] = dw

            @pl.when(i > 0)
            def _():
                dws_ref[g] += dw

        dvn = dvn_scr[...]
        sg = _colsum(dvn * xhat)
        sb = _colsum(dvn)

        @pl.when(i == 0)
        def _():
            dg_ref[...] = sg
            dbe_ref[...] = sb

        @pl.when(i > 0)
        def _():
            dg_ref[...] += sg
            dbe_ref[...] += sb

        dvg = _ln_bwd(dvn, xhat, rstd, gam)
        gp = _gelu_grad(zv)
        dz_ref[:, :D] = (jnp.concatenate(dus, axis=1) * gp[:, :D]).astype(BF16)
        dz_ref[:, D:] = (dvg * gp[:, D:]).astype(BF16)

        @pl.when(i == S // tm - 1)
        def _():
            acc = dm_acc[...]
            for g in range(8):
                dbs_ref[:, g:g + 1] = jnp.sum(acc[:, g * BLK:(g + 1) * BLK], axis=1, keepdims=True)

    vec = pl.BlockSpec((1, D), lambda i: (0, 0))
    return pl.pallas_call(
        body,
        name="gmlp_bwd",
        grid=(S // tm,),
        in_specs=[
            pl.BlockSpec((tm, 2 * D), lambda i: (i, 0)),
            pl.BlockSpec((tm, D), lambda i: (i, 0)),
            pl.BlockSpec((8, BLK, BLK), lambda i: (0, 0, 0)),
            pl.BlockSpec((BLK, 8), lambda i: (0, 0)),
            vec,
            vec,
        ],
        out_specs=[
            pl.BlockSpec((tm, 2 * D), lambda i: (i, 0)),
            pl.BlockSpec((8, BLK, BLK), lambda i: (0, 0, 0)),
            pl.BlockSpec((BLK, 8), lambda i: (0, 0)),
            vec,
            vec,
        ],
        out_shape=[
            jax.ShapeDtypeStruct((S, 2 * D), BF16),
            jax.ShapeDtypeStruct((8, BLK, BLK), F32),
            jax.ShapeDtypeStruct((BLK, 8), F32),
            jax.ShapeDtypeStruct((1, D), F32),
            jax.ShapeDtypeStruct((1, D), F32),
        ],
        scratch_shapes=[pltpu.VMEM((tm, D), F32), pltpu.VMEM((BLK, D), F32)],
        compiler_params=_cp(("arbitrary",), 48),
    )(z, dy, ws, bst, ln_g, ln_b)


def _merge_fwd(ya, yg, glog, bgate, h1, wabt, wgb, wo, ln_g, ln_b, tm=256):
    def body(ya_ref, yg_ref, gl_ref, bg_ref, h1_ref, wab_ref, wgb_ref, wo_ref, g_ref, b_ref,
             h_ref, hb_ref, xh_ref, rs_ref, mg_ref, bra_ref, brg_ref):
        bra = _dot_nt(ya_ref[...], wab_ref[...])
        brg = _dot_nn(yg_ref[...], wgb_ref[...])
        gates = _sigmoid(gl_ref[...] + bg_ref[...])
        merged = (gates[:, :D] * bra + gates[:, D:] * brg).astype(BF16)
        mix = _dot_nn(merged, wo_ref[...])
        h, xhat, rstd = _ln_fwd(ALPHA * h1_ref[...] + mix, g_ref[...], b_ref[...])
        h_ref[...] = h
        hb_ref[...] = h.astype(BF16)
        xh_ref[...] = xhat
        rs_ref[...] = rstd
        mg_ref[...] = merged
        bra_ref[...] = bra
        brg_ref[...] = brg

    row = pl.BlockSpec((tm, D), lambda i: (i, 0))
    vec = pl.BlockSpec((1, D), lambda i: (0, 0))
    full = lambda shape: pl.BlockSpec(shape, lambda i: (0, 0))
    return pl.pallas_call(
        body,
        name="merge_fwd",
        grid=(S // tm,),
        in_specs=[
            pl.BlockSpec((tm, GW), lambda i: (i, 0)), row, pl.BlockSpec((tm, 2 * D), lambda i: (i, 0)), full((1, 2 * D)), row,
            full((D, GW)), full((D, D)), full((D, D)), vec, vec,
        ],
        out_specs=[row, row, row, pl.BlockSpec((tm, 1), lambda i: (i, 0)), row, row, row],
        out_shape=[
            jax.ShapeDtypeStruct((S, D), F32),
            jax.ShapeDtypeStruct((S, D), BF16),
            jax.ShapeDtypeStruct((S, D), F32),
            jax.ShapeDtypeStruct((S, 1), F32),
            jax.ShapeDtypeStruct((S, D), BF16),
            jax.ShapeDtypeStruct((S, D), F32),
            jax.ShapeDtypeStruct((S, D), F32),
        ],
        compiler_params=_cp(("parallel",), 48),
    )(ya, yg, glog, bgate, h1, wabt, wgb, wo, ln_g, ln_b)


def _merge_bwd(dh2, xhat, rstd, ln_g, bra, brg, glog, bgate, wabt, wgb, wo, tm=256):
    def body(dh_ref, xh_ref, rs_ref, g_ref, bra_ref, brg_ref, gl_ref, bg_ref, wab_ref, wgb_ref, wo_ref,
             dr_ref, drb_ref, dlog_ref, dba_ref, dbg_ref, dya_ref, dyg_ref, dbgate_ref, dg_ref, dbias_ref):
        i = pl.program_id(0)
        dh = dh_ref[...]
        xh = xh_ref[...]
        dr = _ln_bwd(dh, xh, rs_ref[...], g_ref[...])
        drb = dr.astype(BF16)
        dr_ref[...] = dr
        drb_ref[...] = drb
        dmerged = _dot_nt(drb, wo_ref[...])
        gates = _sigmoid(gl_ref[...] + bg_ref[...])
        g0, g1 = gates[:, :D], gates[:, D:]
        dl0 = dmerged * bra_ref[...] * g0 * (1.0 - g0)
        dl1 = dmerged * brg_ref[...] * g1 * (1.0 - g1)
        dlog_ref[:, :D] = dl0.astype(BF16)
        dlog_ref[:, D:] = dl1.astype(BF16)
        dba = (dmerged * g0).astype(BF16)
        dbg = (dmerged * g1).astype(BF16)
        dba_ref[...] = dba
        dbg_ref[...] = dbg
        dya_ref[...] = _dot_nn(dba, wab_ref[...])
        dyg_ref[...] = _dot_nt(dbg, wgb_ref[...])
        s0, s1 = _colsum(dl0), _colsum(dl1)
        sg, sb = _colsum(dh * xh), _colsum(dh)

        @pl.when(i == 0)
        def _():
            dbgate_ref[:, :D] = s0
            dbgate_ref[:, D:] = s1
            dg_ref[...] = sg
            dbias_ref[...] = sb

        @pl.when(i > 0)
        def _():
            dbgate_ref[:, :D] += s0
            dbgate_ref[:, D:] += s1
            dg_ref[...] += sg
            dbias_ref[...] += sb

    row = pl.BlockSpec((tm, D), lambda i: (i, 0))
    vec = pl.BlockSpec((1, D), lambda i: (0, 0))
    wide = pl.BlockSpec((tm, 2 * D), lambda i: (i, 0))
    full = lambda shape: pl.BlockSpec(shape, lambda i: (0, 0))
    return pl.pallas_call(
        body,
        name="merge_bwd",
        grid=(S // tm,),
        in_specs=[row, row, pl.BlockSpec((tm, 1), lambda i: (i, 0)), vec, row, row, wide, full((1, 2 * D)),
                  full((D, GW)), full((D, D)), full((D, D))],
        out_specs=[row, row, wide, row, row, pl.BlockSpec((tm, GW), lambda i: (i, 0)), row, full((1, 2 * D)), vec, vec],
        out_shape=[
            jax.ShapeDtypeStruct((S, D), F32),
            jax.ShapeDtypeStruct((S, D), BF16),
            jax.ShapeDtypeStruct((S, 2 * D), BF16),
            jax.ShapeDtypeStruct((S, D), BF16),
            jax.ShapeDtypeStruct((S, D), BF16),
            jax.ShapeDtypeStruct((S, GW), F32),
            jax.ShapeDtypeStruct((S, D), F32),
            jax.ShapeDtypeStruct((1, 2 * D), F32),
            jax.ShapeDtypeStruct((1, D), F32),
            jax.ShapeDtypeStruct((1, D), F32),
        ],
        compiler_params=_cp(("arbitrary",), 48),
    )(dh2, xhat, rstd, ln_g, bra, brg, glog, bgate, wabt, wgb, wo)


def _loss_head(h3, target, tm=512):
    def body(h_ref, t_ref, d_ref, l_ref):
        i = pl.program_id(0)
        e = h_ref[...] - t_ref[...]
        d_ref[...] = e * (1.0 / D)
        part = jnp.sum(_colsum(e * e), axis=1, keepdims=True) * (0.5 / D)

        @pl.when(i == 0)
        def _():
            l_ref[...] = part

        @pl.when(i > 0)
        def _():
            l_ref[...] += part

    row = pl.BlockSpec((tm, D), lambda i: (i, 0))
    return pl.pallas_call(
        body,
        name="loss_head",
        grid=(S // tm,),
        in_specs=[row, row],
        out_specs=[row, pl.BlockSpec((1, 1), lambda i: (0, 0))],
        out_shape=[jax.ShapeDtypeStruct((S, D), F32), jax.ShapeDtypeStruct((1, 1), F32)],
        compiler_params=_cp(("arbitrary",)),
    )(h3, target)


def _tie(x, dep):
    if dep is None:
        return x
    return lax.optimization_barrier((x, dep))[0]


def _local_step(x, pos_col, target, get_w, p, emit):
    w = get_w("ffn1", None)
    h1, h1b, xh1, rs1, a1, b1 = _ffn_fwd(x, w["g1"], w["u1"], w["d1"], p["ln1_g"], p["ln1_b"], "ffn1_fwd")

    w.update(get_w("win", h1b))
    qkv = _matmul(h1b, w["win"], "nt", F32, 512, 512, D, "proj_qkv", b_off=0, n_out=QKV_W)
    z = _matmul(h1b, w["win"], "nt", F32, 512, 512, D, "proj_z", b_off=QKV_W // 512, n_out=2 * D)
    glog = _matmul(h1b, w["win"], "nt", F32, 512, 512, D, "proj_g", b_off=(QKV_W + 2 * D) // 512, n_out=2 * D)

    half = jnp.arange(0, HEAD_DIM, 2, dtype=F32) / HEAD_DIM
    inv_freq = ROPE_THETA ** (-half)
    invf = jnp.tile(inv_freq, 4).reshape(1, BLK)
    sign = jnp.tile(jnp.concatenate([-jnp.ones((32,), F32), jnp.ones((32,), F32)]), 2).reshape(1, BLK)
    cos_t, sin_t = _rope_tables(pos_col, invf, sign)

    os_, lses = [], []
    for gi in range(3):
        o, lse = _attn_fwd(qkv, cos_t, sin_t, gi)
        os_.append(o)
        lses.append(lse)
    ya = _combine_fwd(os_, lses)
    bst = p["gmlp_b_s"].T
    yg = _gmlp_fwd(z, p["gmlp_w_s"], bst, p["gmlp_ln_g"], p["gmlp_ln_b"])
    w.update(get_w("mix", yg))
    h2, h2b, xh2, rs2, merged, bra, brg = _merge_fwd(ya, yg, glog, p["b_gates"], h1, w["ab"], w["gb"], w["o"],
                                                      p["ln2_g"], p["ln2_b"])
    w.update(get_w("ffn2", h2b))
    h3, _, xh3, rs3, a2, b2 = _ffn_fwd(h2, w["g2"], w["u2"], w["d2"], p["ln3_g"], p["ln3_b"], "ffn2_fwd")
    dh3, loss = _loss_head(h3, target)

    gp = {}
    dh2, da2, db2, hm2, df2, gp["ln3_g"], gp["ln3_b"] = _ffn_bwd(dh3, None, xh3, rs3, p["ln3_g"], a2, b2,
                                                                 w["g2"], w["u2"], w["d2"], "ffn2_bwd")
    tok = emit("ffn2", {
        "g2": _matmul(da2, h2b, "tn", BF16, 256, D, S, "wgrad_g2"),
        "u2": _matmul(db2, h2b, "tn", BF16, 256, D, S, "wgrad_u2"),
        "d2": _matmul(hm2, df2, "tn", BF16, 256, D, S, "wgrad_d2")})

    (dr2, dr2b, dlog, dba, dbg, dya, dyg, gp["b_gates"], gp["ln2_g"], gp["ln2_b"]) = _merge_bwd(
        _tie(dh2, tok), xh2, rs2, p["ln2_g"], bra, brg, glog, p["b_gates"], w["ab"], w["gb"], w["o"])
    tok = emit("mix", {
        "o": _matmul(merged, dr2b, "tn", BF16, 256, D, S, "wgrad_o"),
        "ab": _matmul(dba, ya, "tn", BF16, 256, GW, S, "wgrad_ab"),
        "gb": _matmul(yg, dbg, "tn", BF16, 256, D, S, "wgrad_gb")})

    seg = (jnp.arange(GW)[:, None] // HEAD_DIM == jnp.arange(GW)[None, :] // HEAD_DIM).astype(BF16)
    do0, do1, do2, c0, c1, c2 = _combine_bwd(_tie(dya, tok), os_, lses, seg)
    dqkv = []
    for gi, (do, ct) in enumerate(((do0, c0), (do1, c1), (do2, c2))):
        dqkv.append(_attn_bwd(qkv, cos_t, sin_t, do, lses[gi], ct, gi))
    dz, gp["gmlp_w_s"], dbst, gp["gmlp_ln_g"], gp["gmlp_ln_b"] = _gmlp_bwd(
        z, dyg, p["gmlp_w_s"], bst, p["gmlp_ln_g"], p["gmlp_ln_b"])
    gp["gmlp_b_s"] = dbst.T
    dproj = jnp.concatenate(
        [dqkv[gi][0] for gi in range(3)] + [dqkv[gi][1].astype(BF16) for gi in range(3)]
        + [dqkv[gi][2].astype(BF16) for gi in range(3)] + [dz, dlog], axis=1)
    tok = emit("win", {"win": _matmul(dproj, h1b, "tn", BF16, 256, D, S, "wgrad_win")})
    dh1m = _matmul(_tie(dproj, tok), w["win"], "nn", F32, 512, D, 512, "dproj_to_dh1")

    dx, da1, db1, hm1, df1, gp["ln1_g"], gp["ln1_b"] = _ffn_bwd(dr2, dh1m, xh1, rs1, p["ln1_g"], a1, b1,
                                                                w["g1"], w["u1"], w["d1"], "ffn1_bwd")
    emit("ffn1", {
        "g1": _matmul(da1, x, "tn", BF16, 256, D, S, "wgrad_g1"),
        "u1": _matmul(db1, x, "tn", BF16, 256, D, S, "wgrad_u1"),
        "d1": _matmul(hm1, df1, "tn", BF16, 256, D, S, "wgrad_d1")})
    return loss, dx, gp


_FLIPS = [(mx, my, mc) for mx in (0, 1) for my in (0, 1) for mc in (0, 1)][1:]
_GROUPS = {"ffn1": ("g1", "u1", "d1"), "win": ("win",), "mix": ("ab", "gb", "o"), "ffn2": ("g2", "u2", "d2")}
_HBM = pl.BlockSpec(memory_space=pltpu.HBM)
_SEM = pl.BlockSpec(memory_space=pltpu.SEMAPHORE)
_EFFECT = pltpu.SideEffectType.DATAFLOW_SIDE_EFFECTING


def _me():
    return 4 * lax.axis_index("x") + 2 * lax.axis_index("y") + lax.axis_index("c")


def _exchange_start(srcs, scatter, name, after=None):
    nw = len(srcs)
    me = _me()
    lands = []
    for a in srcs:
        own = lax.dynamic_index_in_dim(a, me, 0, keepdims=True) if scatter else a[None]
        shape = a.shape if scatter else (N_DEV,) + a.shape
        lands.append(lax.dynamic_update_slice(lax.empty(shape, a.dtype), own, (me, 0, 0)))
    n_after = 0 if after is None else 1

    def body(*refs):
        src, land = refs[:nw], refs[nw:2 * nw]
        send_sems, recv_sems = refs[2 * nw + n_after], refs[2 * nw + n_after + 1]
        token = refs[-1]
        x, y, c = lax.axis_index("x"), lax.axis_index("y"), lax.axis_index("c")
        mine = 4 * x + 2 * y + c
        for mx, my, mc in _FLIPS:
            px, py, pc = x ^ mx, y ^ my, c ^ mc
            peer = 4 * px + 2 * py + pc
            for k in range(nw):
                pltpu.make_async_remote_copy(
                    src_ref=src[k].at[peer] if scatter else src[k], dst_ref=land[k].at[mine],
                    send_sem=send_sems.at[k], recv_sem=recv_sems.at[k],
                    device_id=(px, py, pc), device_id_type=MESH).start()
        token[...] = jnp.zeros_like(token)

    ins = [pltpu.with_memory_space_constraint(a, pltpu.HBM) for a in list(srcs) + lands]
    outs = pl.pallas_call(
        body,
        name=name,
        in_specs=[_HBM] * (2 * nw) + [pl.BlockSpec(memory_space=pl.ANY)] * n_after,
        out_specs=[_SEM, _SEM] + [_HBM] * (2 * nw) + [pl.BlockSpec(memory_space=pltpu.VMEM)],
        out_shape=[pltpu.SemaphoreType.DMA((nw,)), pltpu.SemaphoreType.DMA((nw,))]
        + [pltpu.HBM(a.shape, a.dtype) for a in ins] + [jax.ShapeDtypeStruct((8, 128), F32)],
        input_output_aliases={k: 2 + k for k in range(2 * nw)},
        compiler_params=pltpu.CompilerParams(has_side_effects=_EFFECT),
    )(*ins, *([] if after is None else [after]))
    return outs[0], outs[1], outs[2:2 + nw], outs[2 + nw:2 + 2 * nw], outs[-1]


def _exchange_wait(started, after, name):
    send_sems, recv_sems, srcs, lands, _ = started
    nw = len(srcs)

    def body(*refs):
        land = refs[nw:2 * nw]
        ss, rs = refs[2 * nw], refs[2 * nw + 1]
        me3 = (lax.axis_index("x"), lax.axis_index("y"), lax.axis_index("c"))
        for k in range(nw):
            seven = land[k].at[pl.ds(0, N_DEV - 1)]
            cp = pltpu.make_async_remote_copy(src_ref=seven, dst_ref=seven, send_sem=ss.at[k], recv_sem=rs.at[k],
                                              device_id=me3, device_id_type=MESH)
            cp.wait_send()
            cp.wait_recv()

    outs = pl.pallas_call(
        body,
        name=name,
        in_specs=[_HBM] * (2 * nw) + [_SEM, _SEM, pl.BlockSpec(memory_space=pl.ANY)],
        out_specs=[_HBM] * (2 * nw),
        out_shape=[pltpu.HBM(a.shape, a.dtype) for a in list(srcs) + list(lands)],
        input_output_aliases={k: k for k in range(2 * nw)},
        compiler_params=pltpu.CompilerParams(has_side_effects=_EFFECT),
    )(*srcs, *lands, send_sems, recv_sems, after)
    return outs[nw:]


def _exchange(arrays, scatter, name):
    nw = len(arrays)

    def body(*refs):
        ins, outs = refs[:nw], refs[nw:2 * nw]
        send_sems, recv_sems, local_sems = refs[2 * nw:]
        x, y, c = lax.axis_index("x"), lax.axis_index("y"), lax.axis_index("c")
        me = 4 * x + 2 * y + c
        local = []
        for k in range(nw):
            src = ins[k].at[me] if scatter else ins[k]
            cp = pltpu.make_async_copy(src, outs[k].at[me], local_sems.at[k])
            cp.start()
            local.append(cp)
        for mx, my, mc in _FLIPS:
            px, py, pc = x ^ mx, y ^ my, c ^ mc
            peer = 4 * px + 2 * py + pc
            for k in range(nw):
                src = ins[k].at[peer] if scatter else ins[k]
                pltpu.make_async_remote_copy(
                    src_ref=src, dst_ref=outs[k].at[me], send_sem=send_sems.at[k], recv_sem=recv_sems.at[k],
                    device_id=(px, py, pc), device_id_type=MESH).start()
        for k in range(nw):
            seven = outs[k].at[pl.ds(0, N_DEV - 1)]
            pltpu.make_async_remote_copy(
                src_ref=seven, dst_ref=seven, send_sem=send_sems.at[k], recv_sem=recv_sems.at[k],
                device_id=(x, y, c), device_id_type=MESH).wait()
            local[k].wait()

    hbm = pl.BlockSpec(memory_space=pl.ANY)
    out_shape = [jax.ShapeDtypeStruct(a.shape if scatter else (N_DEV,) + a.shape, a.dtype) for a in arrays]
    return pl.pallas_call(
        body,
        name=name,
        in_specs=[hbm] * nw,
        out_specs=[hbm] * nw,
        out_shape=out_shape,
        scratch_shapes=[pltpu.SemaphoreType.DMA((nw,)), pltpu.SemaphoreType.DMA((nw,)), pltpu.SemaphoreType.DMA((nw,))],
    )(*arrays)


def _sum8(parts, name, tm=None):
    _, r, c = parts.shape
    tm = tm or (r if r <= 512 else (r // 2 if (r // 2) % 8 == 0 and r // 2 <= 640 else 64))
    assert r % tm == 0

    def body(p_ref, o_ref):
        acc = p_ref[0].astype(F32)
        for j in range(1, N_DEV):
            acc = acc + p_ref[j].astype(F32)
        o_ref[...] = acc

    return pl.pallas_call(
        body,
        name=name,
        grid=(r // tm,),
        in_specs=[pl.BlockSpec((N_DEV, tm, c), lambda i: (0, i, 0))],
        out_specs=pl.BlockSpec((tm, c), lambda i: (i, 0)),
        out_shape=jax.ShapeDtypeStruct((r, c), F32),
        compiler_params=_cp(("parallel",), 48),
    )(parts)


def _adamw(wv, g, m, v, name):
    r, c = wv.shape
    tm = r if r <= 512 else 256
    assert r % tm == 0

    def body(w_ref, g_ref, m_ref, v_ref, d_ref, mo_ref, vo_ref):
        gv = g_ref[...]
        mn = ADAM_B1 * m_ref[...] + (1.0 - ADAM_B1) * gv
        vn = ADAM_B2 * v_ref[...] + (1.0 - ADAM_B2) * (gv * gv)
        m_hat = mn / (1.0 - ADAM_B1 ** ADAM_STEP)
        v_hat = vn / (1.0 - ADAM_B2 ** ADAM_STEP)
        d_ref[...] = -ADAM_LR * (m_hat / (jnp.sqrt(v_hat) + ADAM_EPS) + ADAM_WD * w_ref[...])
        mo_ref[...] = mn
        vo_ref[...] = vn

    sp = pl.BlockSpec((tm, c), lambda i: (i, 0))
    return pl.pallas_call(
        body,
        name=name,
        grid=(r // tm,),
        in_specs=[sp] * 4,
        out_specs=[sp] * 3,
        out_shape=[jax.ShapeDtypeStruct((r, c), F32)] * 3,
        compiler_params=_cp(("parallel",)),
    )(wv, g, m, v)


_WEIGHTS = ["ffn1_w_gate", "ffn1_w_up", "ffn1_w_down", "ln1_g", "ln1_b", "w_in", "b_gates", "gmlp_ln_g", "gmlp_ln_b",
            "gmlp_w_s", "gmlp_b_s", "w_attn_branch", "w_gmlp_branch", "w_out", "ln2_g", "ln2_b", "ffn2_w_gate",
            "ffn2_w_up", "ffn2_w_down", "ln3_g", "ln3_b"]
_BIG_OF = {"ffn1_w_gate": ("g1", True), "ffn1_w_up": ("u1", True), "ffn1_w_down": ("d1", False), "w_in": ("win", True),
           "w_attn_branch": ("ab", True), "w_gmlp_branch": ("gb", False), "w_out": ("o", False),
           "ffn2_w_gate": ("g2", True), "ffn2_w_up": ("u2", True), "ffn2_w_down": ("d2", False)}
_SMALL = [n for n in _WEIGHTS if n not in _BIG_OF]
_SMALL_ROWS = {"gmlp_w_s": 128, "b_gates": 2}
_SMALL_SLOT = 8


def _pack_small(d):
    rows = []
    for n in _SMALL:
        r = d[n].reshape(-1, D)
        slot = max(r.shape[0], _SMALL_SLOT)
        rows.append(jnp.pad(r, ((0, slot - r.shape[0]), (0, 0))))
    return jnp.concatenate(rows, axis=0)


def _unpack_small(packed, shapes):
    out, at = {}, 0
    for n in _SMALL:
        k = _SMALL_ROWS.get(n, 1)
        out[n] = packed[at:at + k].reshape(shapes[n])
        at += max(k, _SMALL_SLOT)
    return out


def kernel(x, positions, ffn1_w_gate, ffn1_w_up, ffn1_w_down, ln1_g, ln1_b, w_in, b_gates, gmlp_ln_g, gmlp_ln_b, gmlp_w_s, gmlp_b_s, w_attn_branch, w_gmlp_branch, w_out, ln2_g, ln2_b, ffn2_w_gate, ffn2_w_up, ffn2_w_down, ln3_g, ln3_b, loss_target, m_ffn1_w_gate, m_ffn1_w_up, m_ffn1_w_down, m_ln1_g, m_ln1_b, m_w_in, m_b_gates, m_gmlp_ln_g, m_gmlp_ln_b, m_gmlp_w_s, m_gmlp_b_s, m_w_attn_branch, m_w_gmlp_branch, m_w_out, m_ln2_g, m_ln2_b, m_ffn2_w_gate, m_ffn2_w_up, m_ffn2_w_down, m_ln3_g, m_ln3_b, v_ffn1_w_gate, v_ffn1_w_up, v_ffn1_w_down, v_ln1_g, v_ln1_b, v_w_in, v_b_gates, v_gmlp_ln_g, v_gmlp_ln_b, v_gmlp_w_s, v_gmlp_b_s, v_w_attn_branch, v_w_gmlp_branch, v_w_out, v_ln2_g, v_ln2_b, v_ffn2_w_gate, v_ffn2_w_up, v_ffn2_w_down, v_ln3_g, v_ln3_b):
    args = dict(locals())
    wts = {n: args[n] for n in _WEIGHTS}
    ms = {n: args["m_" + n] for n in _WEIGHTS}
    vs = {n: args["v_" + n] for n in _WEIGHTS}

    name_of = {key: (n, tr) for n, (key, tr) in _BIG_OF.items()}

    started, tok = {}, None
    for grp, keys in _GROUPS.items():
        shards = []
        for key in keys:
            n, tr = name_of[key]
            s2 = wts[n][0]
            shards.append((s2.T if tr else s2).astype(BF16))
        started[grp] = _exchange_start(shards, False, "gather_start_" + grp, after=tok)
        tok = started[grp][4]
    all_started = tok

    def get_w(grp, after):
        if after is None:
            after = all_started
        lands = _exchange_wait(started[grp], after, "gather_wait_" + grp)
        return {key: g.reshape(-1, g.shape[-1]) for key, g in zip(_GROUPS[grp], lands)}

    sent = {}

    def emit(grp, grads):
        parts = [grads[key].reshape(N_DEV, -1, grads[key].shape[-1]) for key in _GROUPS[grp]]
        sent[grp] = _exchange_start(parts, True, "scatter_start_" + grp)
        return sent[grp][4]

    p = {n: (wts[n][0] if n in ("gmlp_w_s", "gmlp_b_s") else wts[n]) for n in _SMALL}
    loss, dx, gp = _local_step(x[0], positions.reshape(S, 1), loss_target[0], get_w, p, emit)
    loss = lax.psum(loss[0, 0], ("x", "y", "c"))

    small_parts = _exchange([_pack_small({n: gp[n] for n in _SMALL})], False, "gather_small_grads")[0]
    g_small = _sum8(small_parts, "sum_small")
    grads, deltas, new_m, new_v = {}, {}, {}, {}
    dl, mn, vn = _adamw(_pack_small({n: wts[n] for n in _SMALL}), g_small,
                        _pack_small({n: ms[n] for n in _SMALL}), _pack_small({n: vs[n] for n in _SMALL}), "adamw_small")
    shapes = {n: wts[n].shape for n in _SMALL}
    for dst, packed in ((grads, g_small), (deltas, dl), (new_m, mn), (new_v, vn)):
        dst.update(_unpack_small(packed, shapes))

    after = dl
    for grp in ("ffn2", "mix", "win", "ffn1"):
        parts = _exchange_wait(sent[grp], after, "scatter_wait_" + grp)
        for key, part in zip(_GROUPS[grp], parts):
            n, tr = name_of[key]
            g = _sum8(part, "sum_" + key)
            g = g.T if tr else g
            dl, mn, vn = _adamw(wts[n][0], g, ms[n][0], vs[n][0], "adamw_" + key)
            grads[n], deltas[n], new_m[n], new_v[n] = g[None], dl[None], mn[None], vn[None]
            after = dl

    return (loss, dx[None], *[grads[n] for n in _WEIGHTS], *[deltas[n] for n in _WEIGHTS],
            *[new_m[n] for n in _WEIGHTS], *[new_v[n] for n in _WEIGHTS])
```

```python
import functools
import math

import jax
import jax.numpy as jnp
from jax import lax
from jax.experimental import pallas as pl
from jax.experimental.pallas import tpu as pltpu

F32 = jnp.float32
BF16 = jnp.bfloat16

N_DEV = 8
D = 1024
S = 2048
F = 2816
HEAD_DIM = 64
HEADS = 8
GW = HEADS * HEAD_DIM
PATTERN_DILATIONS = (1, 4, 16)
BLK = 128
QKV_W = 3 * 3 * GW
IN_W = QKV_W + 2 * D + 2 * D
ROPE_THETA = 10000.0
ALPHA = 2.0 ** 0.25
LN_EPS = 1e-5
ADAM_LR, ADAM_B1, ADAM_B2, ADAM_EPS, ADAM_WD, ADAM_STEP = 0.001, 0.9, 0.999, 1e-08, 0.01, 10
NEG = -1e30
MESH = pl.DeviceIdType.MESH


def _cp(sem=None, vmem_mb=None):
    kw = {}
    if sem is not None:
        kw["dimension_semantics"] = sem
    if vmem_mb is not None:
        kw["vmem_limit_bytes"] = vmem_mb << 20
    return pltpu.CompilerParams(**kw)


def _dot_nn(a, b):
    return lax.dot_general(a, b, (((1,), (0,)), ((), ())), preferred_element_type=F32)


def _dot_nt(a, b):
    return lax.dot_general(a, b, (((1,), (1,)), ((), ())), preferred_element_type=F32)


def _dot_tn(a, b):
    return lax.dot_general(a, b, (((0,), (0,)), ((), ())), preferred_element_type=F32)


def _ln_fwd(r, g, b):
    mu = jnp.mean(r, axis=-1, keepdims=True)
    xc = r - mu
    var = jnp.mean(xc * xc, axis=-1, keepdims=True)
    rstd = lax.rsqrt(var + LN_EPS)
    xhat = xc * rstd
    return xhat * g + b, xhat, rstd


def _ln_bwd(dh, xhat, rstd, g):
    dxh = dh * g
    m1 = jnp.mean(dxh, axis=-1, keepdims=True)
    m2 = jnp.mean(dxh * xhat, axis=-1, keepdims=True)
    return rstd * (dxh - m1 - xhat * m2)


def _sigmoid(x):
    return 1.0 / (1.0 + jnp.exp(-x))


def _colsum(x):
    return jnp.sum(x, axis=0, keepdims=True)


def _matmul(a, b, mode, out_dtype, tm, tn, tk, name, b_off=0, n_out=None, dep=None):
    n_dep = 0 if dep is None else 1
    if mode == "nn":
        m, k = a.shape
        n = b.shape[1]
    elif mode == "nt":
        m, k = a.shape
        n = n_out if n_out is not None else b.shape[0]
    else:
        k, m = a.shape
        n = b.shape[1]
    nk = k // tk
    assert m % tm == 0 and n % tn == 0 and k % tk == 0
    dot = {"nn": _dot_nn, "nt": _dot_nt, "tn": _dot_tn}[mode]

    def body(a_ref, b_ref, *rest):
        o_ref, scr = rest[n_dep], rest[n_dep + 1:]
        r = dot(a_ref[...].astype(BF16), b_ref[...].astype(BF16))
        if nk == 1:
            o_ref[...] = r.astype(out_dtype)
        else:
            acc = scr[0]
            kk = pl.program_id(2)

            @pl.when(kk == 0)
            def _():
                acc[...] = r

            @pl.when(kk > 0)
            def _():
                acc[...] += r

            @pl.when(kk == nk - 1)
            def _():
                o_ref[...] = acc[...].astype(out_dtype)

    if mode == "nn":
        a_spec = pl.BlockSpec((tm, tk), lambda i, j, kk: (i, kk))
        b_spec = pl.BlockSpec((tk, tn), lambda i, j, kk: (kk, j))
    elif mode == "nt":
        a_spec = pl.BlockSpec((tm, tk), lambda i, j, kk: (i, kk))
        b_spec = pl.BlockSpec((tn, tk), lambda i, j, kk: (j + b_off, kk))
    else:
        a_spec = pl.BlockSpec((tk, tm), lambda i, j, kk: (kk, i))
        b_spec = pl.BlockSpec((tk, tn), lambda i, j, kk: (kk, j))
    return pl.pallas_call(
        body,
        name=name,
        grid=(m // tm, n // tn, nk),
        in_specs=[a_spec, b_spec] + [pl.BlockSpec(memory_space=pl.ANY)] * n_dep,
        out_specs=pl.BlockSpec((tm, tn), lambda i, j, kk: (i, j)),
        out_shape=jax.ShapeDtypeStruct((m, n), out_dtype),
        scratch_shapes=[] if nk == 1 else [pltpu.VMEM((tm, tn), F32)],
        compiler_params=_cp(("parallel", "parallel", "arbitrary"), 56),
    )(a, b, *([] if dep is None else [dep]))


def _ffn_fwd(x, wgt, wut, wd, ln_g, ln_b, name, tm=512, tk=256):
    nk = F // tk

    def body(x_ref, wg_ref, wu_ref, wd_ref, g_ref, b_ref, h_ref, hb_ref, xh_ref, rs_ref, a_ref, bb_ref, acc):
        kk = pl.program_id(1)
        xb = x_ref[...].astype(BF16)
        a = _dot_nt(xb, wg_ref[...])
        b = _dot_nt(xb, wu_ref[...])
        a_ref[...] = a
        bb_ref[...] = b
        hm = (a * _sigmoid(a)) * b
        contrib = _dot_nn(hm.astype(BF16), wd_ref[...])

        @pl.when(kk == 0)
        def _():
            acc[...] = contrib

        @pl.when(kk > 0)
        def _():
            acc[...] += contrib

        @pl.when(kk == nk - 1)
        def _():
            r = ALPHA * x_ref[...] + 0.5 * acc[...]
            h, xhat, rstd = _ln_fwd(r, g_ref[...], b_ref[...])
            h_ref[...] = h
            hb_ref[...] = h.astype(BF16)
            xh_ref[...] = xhat
            rs_ref[...] = rstd

    row = pl.BlockSpec((tm, D), lambda i, kk: (i, 0))
    wsp = pl.BlockSpec((tk, D), lambda i, kk: (kk, 0))
    vec = pl.BlockSpec((1, D), lambda i, kk: (0, 0))
    mid = pl.BlockSpec((tm, tk), lambda i, kk: (i, kk))
    return pl.pallas_call(
        body,
        name=name,
        grid=(S // tm, nk),
        in_specs=[row, wsp, wsp, wsp, vec, vec],
        out_specs=[row, row, row, pl.BlockSpec((tm, 1), lambda i, kk: (i, 0)), mid, mid],
        out_shape=[
            jax.ShapeDtypeStruct((S, D), F32),
            jax.ShapeDtypeStruct((S, D), BF16),
            jax.ShapeDtypeStruct((S, D), F32),
            jax.ShapeDtypeStruct((S, 1), F32),
            jax.ShapeDtypeStruct((S, F), F32),
            jax.ShapeDtypeStruct((S, F), F32),
        ],
        scratch_shapes=[pltpu.VMEM((tm, D), F32)],
        compiler_params=_cp(("parallel", "arbitrary"), 56),
    )(x, wgt, wut, wd, ln_g, ln_b)


def _ffn_bwd(dh_a, dh_b, xhat, rstd, ln_g, a, b, wgt, wut, wd, name, tm=512, tk=256):
    nk = F // tk
    two = dh_b is not None

    def body(*refs):
        if two:
            dha_ref, dhb_ref = refs[0], refs[1]
            refs = refs[2:]
        else:
            dha_ref = refs[0]
            refs = refs[1:]
        (xh_ref, rs_ref, g_ref, a_ref, b_ref, wg_ref, wu_ref, wd_ref,
         dx_ref, da_ref, db_ref, hm_ref, df_ref, dg_ref, dbias_ref, acc, df_scr) = refs
        i = pl.program_id(0)
        kk = pl.program_id(1)

        @pl.when(kk == 0)
        def _():
            dh = dha_ref[...]
            if two:
                dh = ALPHA * dh + dhb_ref[...]
            xhat = xh_ref[...]
            dr = _ln_bwd(dh, xhat, rs_ref[...], g_ref[...])
            dfb = (0.5 * dr).astype(BF16)
            df_scr[...] = dfb
            df_ref[...] = dfb
            acc[...] = ALPHA * dr
            sg = _colsum(dh * xhat)
            sb = _colsum(dh)

            @pl.when(i == 0)
            def _():
                dg_ref[...] = sg
                dbias_ref[...] = sb

            @pl.when(i > 0)
            def _():
                dg_ref[...] += sg
                dbias_ref[...] += sb

        dhm = _dot_nt(df_scr[...], wd_ref[...])
        av = a_ref[...]
        bv = b_ref[...]
        sig = _sigmoid(av)
        silu = av * sig
        da = (dhm * bv * (sig * (1.0 + av * (1.0 - sig)))).astype(BF16)
        db = (dhm * silu).astype(BF16)
        da_ref[...] = da
        db_ref[...] = db
        hm_ref[...] = (silu * bv).astype(BF16)
        acc[...] += _dot_nn(da, wg_ref[...]) + _dot_nn(db, wu_ref[...])

        @pl.when(kk == nk - 1)
        def _():
            dx_ref[...] = acc[...]

    row = pl.BlockSpec((tm, D), lambda i, kk: (i, 0))
    wsp = pl.BlockSpec((tk, D), lambda i, kk: (kk, 0))
    vec = pl.BlockSpec((1, D), lambda i, kk: (0, 0))
    mid = pl.BlockSpec((tm, tk), lambda i, kk: (i, kk))
    ins = [dh_a] + ([dh_b] if two else []) + [xhat, rstd, ln_g, a, b, wgt, wut, wd]
    in_specs = [row] * (2 if two else 1) + [row, pl.BlockSpec((tm, 1), lambda i, kk: (i, 0)), vec, mid, mid, wsp, wsp, wsp]
    return pl.pallas_call(
        body,
        name=name,
        grid=(S // tm, nk),
        in_specs=in_specs,
        out_specs=[row, mid, mid, mid, row, vec, vec],
        out_shape=[
            jax.ShapeDtypeStruct((S, D), F32),
            jax.ShapeDtypeStruct((S, F), BF16),
            jax.ShapeDtypeStruct((S, F), BF16),
            jax.ShapeDtypeStruct((S, F), BF16),
            jax.ShapeDtypeStruct((S, D), BF16),
            jax.ShapeDtypeStruct((1, D), F32),
            jax.ShapeDtypeStruct((1, D), F32),
        ],
        scratch_shapes=[pltpu.VMEM((tm, D), F32), pltpu.VMEM((tm, D), BF16)],
        compiler_params=_cp(("arbitrary", "arbitrary"), 56),
    )(*ins)


def _rope_tables(pos_col, invf, sign, tm=512):
    def body(p_ref, f_ref, s_ref, c_out, s_out):
        ang = p_ref[...].astype(F32) * f_ref[...]
        c_out[...] = jnp.cos(ang)
        s_out[...] = jnp.sin(ang) * s_ref[...]

    vec = pl.BlockSpec((1, BLK), lambda i: (0, 0))
    out = pl.BlockSpec((tm, BLK), lambda i: (i, 0))
    return pl.pallas_call(
        body,
        name="rope_tables",
        grid=(S // tm,),
        in_specs=[pl.BlockSpec((tm, 1), lambda i: (i, 0)), vec, vec],
        out_specs=[out, out],
        out_shape=[jax.ShapeDtypeStruct((S, BLK), F32)] * 2,
        compiler_params=_cp(("parallel",)),
    )(pos_col, invf, sign)


def _lane_lo():
    return (lax.broadcasted_iota(jnp.int32, (BLK, GW), 1) % HEAD_DIM) < (HEAD_DIM // 2)


def _swap_halves(t, lo):
    return jnp.where(lo, pltpu.roll(t, GW - HEAD_DIM // 2, 1), pltpu.roll(t, HEAD_DIM // 2, 1))


def _rope(t, cosf, sinf, lo):
    return t * cosf + _swap_halves(t, lo) * sinf


def _unrope(g, cosf, sinf, lo):
    return g * cosf + _swap_halves(g * sinf, lo)


def _tile4(v):
    return jnp.concatenate([v, v, v, v], axis=1)


def _band_mask(n):
    qi = lax.broadcasted_iota(jnp.int32, (BLK, 2 * BLK), 0)
    kj = lax.broadcasted_iota(jnp.int32, (BLK, 2 * BLK), 1)
    dist = qi + BLK - kj
    return (dist >= 0) & (dist <= BLK) & ((kj >= BLK) | (n >= 1))


def _attn_specs(gi, d):
    nq = QKV_W // GW

    def cur(off):
        return pl.BlockSpec((BLK, GW), lambda r, n: (n, r * nq + off + gi))

    def prev(off):
        return pl.BlockSpec((BLK, GW), lambda r, n: (jnp.maximum(n - 1, 0), r * nq + off + gi))

    tab_c = pl.BlockSpec((BLK, BLK), lambda r, n: (n, r))
    tab_p = pl.BlockSpec((BLK, BLK), lambda r, n: (jnp.maximum(n - 1, 0), r))
    blk = pl.BlockSpec((BLK, GW), lambda r, n: (n, r))
    return cur, prev, tab_c, tab_p, blk


def _attn_fwd(qkv, cos_t, sin_t, gi):
    d = PATTERN_DILATIONS[gi]
    sub = S // d
    nb = sub // BLK
    cur, prev, tab_c, tab_p, blk = _attn_specs(gi, d)

    def body(q_ref, kc_ref, kp_ref, vc_ref, vp_ref, cc_ref, cp_ref, sc_ref, sp_ref, o_ref, l_ref):
        n = pl.program_id(1)
        lo = _lane_lo()
        cc, sc = _tile4(cc_ref[...]), _tile4(sc_ref[...])
        cp, sp = _tile4(cp_ref[...]), _tile4(sp_ref[...])
        q = (_rope(q_ref[...], cc, sc, lo) * (HEAD_DIM ** -0.5)).astype(BF16)
        k = jnp.concatenate([_rope(kp_ref[...], cp, sp, lo), _rope(kc_ref[...], cc, sc, lo)], axis=0).astype(BF16)
        v = jnp.concatenate([vp_ref[...], vc_ref[...]], axis=0).astype(BF16)
        mask = _band_mask(n)
        outs, lses = [], []
        for h in range(HEADS):
            sl = slice(h * HEAD_DIM, (h + 1) * HEAD_DIM)
            s = jnp.where(mask, _dot_nt(q[:, sl], k[:, sl]), NEG)
            m = jnp.max(s, axis=1, keepdims=True)
            p = jnp.exp(s - m)
            l = jnp.sum(p, axis=1, keepdims=True)
            outs.append(_dot_nn((p / l).astype(BF16), v[:, sl]))
            lses.append(jnp.broadcast_to(m + jnp.log(l), (BLK, HEAD_DIM)))
        o_ref[...] = jnp.concatenate(outs, axis=1)
        l_ref[...] = jnp.concatenate(lses, axis=1)

    qv = qkv.reshape(sub, d * QKV_W)
    cv = cos_t.reshape(sub, d * BLK)
    sv = sin_t.reshape(sub, d * BLK)
    o, lse = pl.pallas_call(
        body,
        name=f"attn_fwd_g{gi}",
        grid=(d, nb),
        in_specs=[cur(0), cur(3), prev(3), cur(6), prev(6), tab_c, tab_p, tab_c, tab_p],
        out_specs=[blk, blk],
        out_shape=[jax.ShapeDtypeStruct((sub, d * GW), F32)] * 2,
        compiler_params=_cp(("parallel", "parallel")),
    )(qv, qv, qv, qv, qv, cv, cv, sv, sv)
    return o.reshape(S, GW), lse.reshape(S, GW)


def _attn_bwd(qkv, cos_t, sin_t, do, lse, cterm, gi):
    d = PATTERN_DILATIONS[gi]
    sub = S // d
    nb = sub // BLK
    cur, prev, tab_c, tab_p, blk = _attn_specs(gi, d)

    def body(q_ref, kc_ref, kp_ref, vc_ref, vp_ref, cc_ref, cp_ref, sc_ref, sp_ref, do_ref, l_ref, c_ref,
             dq_ref, dk_ref, dv_ref):
        n = pl.program_id(1)
        lo = _lane_lo()
        cc, sc = _tile4(cc_ref[...]), _tile4(sc_ref[...])
        cp, sp = _tile4(cp_ref[...]), _tile4(sp_ref[...])
        q = (_rope(q_ref[...], cc, sc, lo) * (HEAD_DIM ** -0.5)).astype(BF16)
        k = jnp.concatenate([_rope(kp_ref[...], cp, sp, lo), _rope(kc_ref[...], cc, sc, lo)], axis=0).astype(BF16)
        v = jnp.concatenate([vp_ref[...], vc_ref[...]], axis=0).astype(BF16)
        dov = do_ref[...]
        lse_v = l_ref[...]
        cv_ = c_ref[...]
        mask = _band_mask(n)
        dqs, dks, dvs = [], [], []
        for h in range(HEADS):
            sl = slice(h * HEAD_DIM, (h + 1) * HEAD_DIM)
            s = jnp.where(mask, _dot_nt(q[:, sl], k[:, sl]), NEG)
            p = jnp.exp(s - lse_v[:, h * HEAD_DIM:h * HEAD_DIM + 1])
            dp = _dot_nt(dov[:, sl], v[:, sl])
            ds = (p * (dp - cv_[:, h * HEAD_DIM:h * HEAD_DIM + 1])).astype(BF16)
            dqs.append(_dot_nn(ds, k[:, sl]) * (HEAD_DIM ** -0.5))
            dks.append(_dot_tn(ds, q[:, sl]))
            dvs.append(_dot_tn(p.astype(BF16), dov[:, sl]))
        dq_ref[...] = _unrope(jnp.concatenate(dqs, axis=1), cc, sc, lo).astype(BF16)
        dk = jnp.concatenate(dks, axis=1)
        dv = jnp.concatenate(dvs, axis=1)
        here = pl.ds(pl.multiple_of(n * BLK, BLK), BLK)
        dk_ref[here, :] = _unrope(dk[BLK:], cc, sc, lo)
        dv_ref[here, :] = dv[BLK:]

        @pl.when(n > 0)
        def _():
            before = pl.ds(pl.multiple_of((n - 1) * BLK, BLK), BLK)
            dk_ref[before, :] += _unrope(dk[:BLK], cp, sp, lo)
            dv_ref[before, :] += dv[:BLK]

    qv = qkv.reshape(sub, d * QKV_W)
    cv = cos_t.reshape(sub, d * BLK)
    sv = sin_t.reshape(sub, d * BLK)
    whole = pl.BlockSpec((sub, GW), lambda r, n: (0, r))
    dq, dk, dv = pl.pallas_call(
        body,
        name=f"attn_bwd_g{gi}",
        grid=(d, nb),
        in_specs=[cur(0), cur(3), prev(3), cur(6), prev(6), tab_c, tab_p, tab_c, tab_p, blk, blk, blk],
        out_specs=[blk, whole, whole],
        out_shape=[
            jax.ShapeDtypeStruct((sub, d * GW), BF16),
            jax.ShapeDtypeStruct((sub, d * GW), F32),
            jax.ShapeDtypeStruct((sub, d * GW), F32),
        ],
        compiler_params=_cp(("arbitrary", "arbitrary"), 48),
    )(qv, qv, qv, qv, qv, cv, cv, sv, sv, do.reshape(sub, d * GW), lse.reshape(sub, d * GW), cterm.reshape(sub, d * GW))
    return dq.reshape(S, GW), dk.reshape(S, GW), dv.reshape(S, GW)


def _group_weights(l0, l1, l2):
    mx = jnp.maximum(jnp.maximum(l0, l1), l2)
    e0, e1, e2 = jnp.exp(l0 - mx), jnp.exp(l1 - mx), jnp.exp(l2 - mx)
    inv = 1.0 / (e0 + e1 + e2)
    return e0 * inv, e1 * inv, e2 * inv


def _combine_fwd(os_, lses, tm=256):
    def body(o0, o1, o2, l0, l1, l2, y_ref):
        w0, w1, w2 = _group_weights(l0[...], l1[...], l2[...])
        y_ref[...] = (w0 * o0[...] + w1 * o1[...] + w2 * o2[...]).astype(BF16)

    sp = pl.BlockSpec((tm, GW), lambda i: (i, 0))
    return pl.pallas_call(
        body,
        name="attn_combine_fwd",
        grid=(S // tm,),
        in_specs=[sp] * 6,
        out_specs=sp,
        out_shape=jax.ShapeDtypeStruct((S, GW), BF16),
        compiler_params=_cp(("parallel",)),
    )(*os_, *lses)


def _combine_bwd(dy, os_, lses, seg, tm=256):
    def body(dy_ref, o0, o1, o2, l0, l1, l2, seg_ref, d0, d1, d2, c0, c1, c2):
        w0, w1, w2 = _group_weights(l0[...], l1[...], l2[...])
        dyv = dy_ref[...]
        y = w0 * o0[...] + w1 * o1[...] + w2 * o2[...]
        t = dyv * y
        t_hi = t.astype(BF16)
        r1 = t - t_hi.astype(F32)
        t_mid = r1.astype(BF16)
        t_lo = (r1 - t_mid.astype(F32)).astype(BF16)
        sg = seg_ref[...]
        e = _dot_nn(t_hi, sg) + _dot_nn(t_mid, sg) + _dot_nn(t_lo, sg)
        d0[...] = (w0 * dyv).astype(BF16)
        d1[...] = (w1 * dyv).astype(BF16)
        d2[...] = (w2 * dyv).astype(BF16)
        c0[...] = w0 * e
        c1[...] = w1 * e
        c2[...] = w2 * e

    sp = pl.BlockSpec((tm, GW), lambda i: (i, 0))
    return pl.pallas_call(
        body,
        name="attn_combine_bwd",
        grid=(S // tm,),
        in_specs=[sp] * 7 + [pl.BlockSpec((GW, GW), lambda i: (0, 0))],
        out_specs=[sp] * 6,
        out_shape=[jax.ShapeDtypeStruct((S, GW), BF16)] * 3 + [jax.ShapeDtypeStruct((S, GW), F32)] * 3,
        compiler_params=_cp(("parallel",)),
    )(dy, *os_, *lses, seg)


_SQRT_HALF = 0.7071067811865476
_INV_SQRT_2PI = 0.3989422804014327


def _gelu(z):
    return 0.5 * z * (1.0 + lax.erf(z * _SQRT_HALF))


def _gelu_grad(z):
    return 0.5 * (1.0 + lax.erf(z * _SQRT_HALF)) + z * (_INV_SQRT_2PI * jnp.exp(-0.5 * z * z))


def _tril_ws(ws_ref, g):
    t = lax.broadcasted_iota(jnp.int32, (BLK, BLK), 0)
    s = lax.broadcasted_iota(jnp.int32, (BLK, BLK), 1)
    return jnp.where(t >= s, ws_ref[g], 0.0)


def _gmlp_fwd(z, ws, bst, ln_g, ln_b, tm=256):
    nch = tm // BLK

    def body(z_ref, ws_ref, b_ref, g_ref, be_ref, y_ref):
        zg = _gelu(z_ref[...])
        u = zg[:, :D]
        vn, _, _ = _ln_fwd(zg[:, D:], g_ref[...], be_ref[...])
        vnb = vn.astype(BF16)
        bt = b_ref[...]
        for g in range(8):
            w = _tril_ws(ws_ref, g).astype(BF16)
            cols = slice(g * BLK, (g + 1) * BLK)
            for c in range(nch):
                rows = slice(c * BLK, (c + 1) * BLK)
                mixed = _dot_nn(w, vnb[rows, cols]) + bt[:, g:g + 1]
                y_ref[rows, cols] = (u[rows, cols] * mixed).astype(BF16)

    return pl.pallas_call(
        body,
        name="gmlp_fwd",
        grid=(S // tm,),
        in_specs=[
            pl.BlockSpec((tm, 2 * D), lambda i: (i, 0)),
            pl.BlockSpec((8, BLK, BLK), lambda i: (0, 0, 0)),
            pl.BlockSpec((BLK, 8), lambda i: (0, 0)),
            pl.BlockSpec((1, D), lambda i: (0, 0)),
            pl.BlockSpec((1, D), lambda i: (0, 0)),
        ],
        out_specs=pl.BlockSpec((tm, D), lambda i: (i, 0)),
        out_shape=jax.ShapeDtypeStruct((S, D), BF16),
        compiler_params=_cp(("parallel",), 48),
    )(z, ws, bst, ln_g, ln_b)


def _gmlp_bwd(z, dy, ws, bst, ln_g, ln_b, tm=256):
    nch = tm // BLK

    def body(z_ref, dy_ref, ws_ref, b_ref, g_ref, be_ref, dz_ref, dws_ref, dbs_ref, dg_ref, dbe_ref, dvn_scr, dm_acc):
        i = pl.program_id(0)
        zv = z_ref[...]
        zg = _gelu(zv)
        u = zg[:, :D]
        gam = g_ref[...]
        vn, xhat, rstd = _ln_fwd(zg[:, D:], gam, be_ref[...])
        vnb = vn.astype(BF16)
        dyv = dy_ref[...]
        dmix = dyv * u
        dmb = dmix.astype(BF16)
        bt = b_ref[...]
        tmask = lax.broadcasted_iota(jnp.int32, (BLK, BLK), 0) >= lax.broadcasted_iota(jnp.int32, (BLK, BLK), 1)
        dm_sum = dmix[0:BLK]
        for c in range(1, nch):
            dm_sum = dm_sum + dmix[c * BLK:(c + 1) * BLK]

        @pl.when(i == 0)
        def _():
            dm_acc[...] = dm_sum

        @pl.when(i > 0)
        def _():
            dm_acc[...] += dm_sum

        dus = []
        for g in range(8):
            w = _tril_ws(ws_ref, g).astype(BF16)
            cols = slice(g * BLK, (g + 1) * BLK)
            dw = None
            du_rows = []
            for c in range(nch):
                rows = slice(c * BLK, (c + 1) * BLK)
                mixed = _dot_nn(w, vnb[rows, cols]) + bt[:, g:g + 1]
                du_rows.append(dyv[rows, cols] * mixed)
                part = _dot_nt(dmb[rows, cols], vnb[rows, cols])
                dw = part if dw is None else dw + part
                dvn_scr[rows, cols] = _dot_tn(w, dmb[rows, cols])
            dus.append(jnp.concatenate(du_rows, axis=0))
            dw = jnp.where(tmask, dw, 0.0)

            @pl.when(i == 0)
            def _():
                dws_ref[g] = dw

            @pl.when(i > 0)
            def _():
                dws_ref[g] += dw

        dvn = dvn_scr[...]
        sg = _colsum(dvn * xhat)
        sb = _colsum(dvn)

        @pl.when(i == 0)
        def _():
            dg_ref[...] = sg
            dbe_ref[...] = sb

        @pl.when(i > 0)
        def _():
            dg_ref[...] += sg
            dbe_ref[...] += sb

        dvg = _ln_bwd(dvn, xhat, rstd, gam)
        gp = _gelu_grad(zv)
        dz_ref[:, :D] = (jnp.concatenate(dus, axis=1) * gp[:, :D]).astype(BF16)
        dz_ref[:, D:] = (dvg * gp[:, D:]).astype(BF16)

        @pl.when(i == S // tm - 1)
        def _():
            acc = dm_acc[...]
            for g in range(8):
                dbs_ref[:, g:g + 1] = jnp.sum(acc[:, g * BLK:(g + 1) * BLK], axis=1, keepdims=True)

    vec = pl.BlockSpec((1, D), lambda i: (0, 0))
    return pl.pallas_call(
        body,
        name="gmlp_bwd",
        grid=(S // tm,),
        in_specs=[
            pl.BlockSpec((tm, 2 * D), lambda i: (i, 0)),
            pl.BlockSpec((tm, D), lambda i: (i, 0)),
            pl.BlockSpec((8, BLK, BLK), lambda i: (0, 0, 0)),
            pl.BlockSpec((BLK, 8), lambda i: (0, 0)),
            vec,
            vec,
        ],
        out_specs=[
            pl.BlockSpec((tm, 2 * D), lambda i: (i, 0)),
            pl.BlockSpec((8, BLK, BLK), lambda i: (0, 0, 0)),
            pl.BlockSpec((BLK, 8), lambda i: (0, 0)),
            vec,
            vec,
        ],
        out_shape=[
            jax.ShapeDtypeStruct((S, 2 * D), BF16),
            jax.ShapeDtypeStruct((8, BLK, BLK), F32),
            jax.ShapeDtypeStruct((BLK, 8), F32),
            jax.ShapeDtypeStruct((1, D), F32),
            jax.ShapeDtypeStruct((1, D), F32),
        ],
        scratch_shapes=[pltpu.VMEM((tm, D), F32), pltpu.VMEM((BLK, D), F32)],
        compiler_params=_cp(("arbitrary",), 48),
    )(z, dy, ws, bst, ln_g, ln_b)


def _merge_fwd(ya, yg, glog, bgate, h1, wabt, wgb, wo, ln_g, ln_b, tm=256):
    def body(ya_ref, yg_ref, gl_ref, bg_ref, h1_ref, wab_ref, wgb_ref, wo_ref, g_ref, b_ref,
             h_ref, hb_ref, xh_ref, rs_ref, mg_ref, bra_ref, brg_ref):
        bra = _dot_nt(ya_ref[...], wab_ref[...])
        brg = _dot_nn(yg_ref[...], wgb_ref[...])
        gates = _sigmoid(gl_ref[...] + bg_ref[...])
        merged = (gates[:, :D] * bra + gates[:, D:] * brg).astype(BF16)
        mix = _dot_nn(merged, wo_ref[...])
        h, xhat, rstd = _ln_fwd(ALPHA * h1_ref[...] + mix, g_ref[...], b_ref[...])
        h_ref[...] = h
        hb_ref[...] = h.astype(BF16)
        xh_ref[...] = xhat
        rs_ref[...] = rstd
        mg_ref[...] = merged
        bra_ref[...] = bra
        brg_ref[...] = brg

    row = pl.BlockSpec((tm, D), lambda i: (i, 0))
    vec = pl.BlockSpec((1, D), lambda i: (0, 0))
    full = lambda shape: pl.BlockSpec(shape, lambda i: (0, 0))
    return pl.pallas_call(
        body,
        name="merge_fwd",
        grid=(S // tm,),
        in_specs=[
            pl.BlockSpec((tm, GW), lambda i: (i, 0)), row, pl.BlockSpec((tm, 2 * D), lambda i: (i, 0)), full((1, 2 * D)), row,
            full((D, GW)), full((D, D)), full((D, D)), vec, vec,
        ],
        out_specs=[row, row, row, pl.BlockSpec((tm, 1), lambda i: (i, 0)), row, row, row],
        out_shape=[
            jax.ShapeDtypeStruct((S, D), F32),
            jax.ShapeDtypeStruct((S, D), BF16),
            jax.ShapeDtypeStruct((S, D), F32),
            jax.ShapeDtypeStruct((S, 1), F32),
            jax.ShapeDtypeStruct((S, D), BF16),
            jax.ShapeDtypeStruct((S, D), F32),
            jax.ShapeDtypeStruct((S, D), F32),
        ],
        compiler_params=_cp(("parallel",), 48),
    )(ya, yg, glog, bgate, h1, wabt, wgb, wo, ln_g, ln_b)


def _merge_bwd(dh2, xhat, rstd, ln_g, bra, brg, glog, bgate, wabt, wgb, wo, tm=256):
    def body(dh_ref, xh_ref, rs_ref, g_ref, bra_ref, brg_ref, gl_ref, bg_ref, wab_ref, wgb_ref, wo_ref,
             dr_ref, drb_ref, dlog_ref, dba_ref, dbg_ref, dya_ref, dyg_ref, dbgate_ref, dg_ref, dbias_ref):
        i = pl.program_id(0)
        dh = dh_ref[...]
        xh = xh_ref[...]
        dr = _ln_bwd(dh, xh, rs_ref[...], g_ref[...])
        drb = dr.astype(BF16)
        dr_ref[...] = dr
        drb_ref[...] = drb
        dmerged = _dot_nt(drb, wo_ref[...])
        gates = _sigmoid(gl_ref[...] + bg_ref[...])
        g0, g1 = gates[:, :D], gates[:, D:]
        dl0 = dmerged * bra_ref[...] * g0 * (1.0 - g0)
        dl1 = dmerged * brg_ref[...] * g1 * (1.0 - g1)
        dlog_ref[:, :D] = dl0.astype(BF16)
        dlog_ref[:, D:] = dl1.astype(BF16)
        dba = (dmerged * g0).astype(BF16)
        dbg = (dmerged * g1).astype(BF16)
        dba_ref[...] = dba
        dbg_ref[...] = dbg
        dya_ref[...] = _dot_nn(dba, wab_ref[...])
        dyg_ref[...] = _dot_nt(dbg, wgb_ref[...])
        s0, s1 = _colsum(dl0), _colsum(dl1)
        sg, sb = _colsum(dh * xh), _colsum(dh)

        @pl.when(i == 0)
        def _():
            dbgate_ref[:, :D] = s0
            dbgate_ref[:, D:] = s1
            dg_ref[...] = sg
            dbias_ref[...] = sb

        @pl.when(i > 0)
        def _():
            dbgate_ref[:, :D] += s0
            dbgate_ref[:, D:] += s1
            dg_ref[...] += sg
            dbias_ref[...] += sb

    row = pl.BlockSpec((tm, D), lambda i: (i, 0))
    vec = pl.BlockSpec((1, D), lambda i: (0, 0))
    wide = pl.BlockSpec((tm, 2 * D), lambda i: (i, 0))
    full = lambda shape: pl.BlockSpec(shape, lambda i: (0, 0))
    return pl.pallas_call(
        body,
        name="merge_bwd",
        grid=(S // tm,),
        in_specs=[row, row, pl.BlockSpec((tm, 1), lambda i: (i, 0)), vec, row, row, wide, full((1, 2 * D)),
                  full((D, GW)), full((D, D)), full((D, D))],
        out_specs=[row, row, wide, row, row, pl.BlockSpec((tm, GW), lambda i: (i, 0)), row, full((1, 2 * D)), vec, vec],
        out_shape=[
            jax.ShapeDtypeStruct((S, D), F32),
            jax.ShapeDtypeStruct((S, D), BF16),
            jax.ShapeDtypeStruct((S, 2 * D), BF16),
            jax.ShapeDtypeStruct((S, D), BF16),
            jax.ShapeDtypeStruct((S, D), BF16),
            jax.ShapeDtypeStruct((S, GW), F32),
            jax.ShapeDtypeStruct((S, D), F32),
            jax.ShapeDtypeStruct((1, 2 * D), F32),
            jax.ShapeDtypeStruct((1, D), F32),
            jax.ShapeDtypeStruct((1, D), F32),
        ],
        compiler_params=_cp(("arbitrary",), 48),
    )(dh2, xhat, rstd, ln_g, bra, brg, glog, bgate, wabt, wgb, wo)


def _loss_head(h3, target, tm=512):
    def body(h_ref, t_ref, d_ref, l_ref):
        i = pl.program_id(0)
        e = h_ref[...] - t_ref[...]
        d_ref[...] = e * (1.0 / D)
        part = jnp.sum(_colsum(e * e), axis=1, keepdims=True) * (0.5 / D)

        @pl.when(i == 0)
        def _():
            l_ref[...] = part

        @pl.when(i > 0)
        def _():
            l_ref[...] += part

    row = pl.BlockSpec((tm, D), lambda i: (i, 0))
    return pl.pallas_call(
        body,
        name="loss_head",
        grid=(S // tm,),
        in_specs=[row, row],
        out_specs=[row, pl.BlockSpec((1, 1), lambda i: (0, 0))],
        out_shape=[jax.ShapeDtypeStruct((S, D), F32), jax.ShapeDtypeStruct((1, 1), F32)],
        compiler_params=_cp(("arbitrary",)),
    )(h3, target)


def _tie(x, dep):
    if dep is None:
        return x
    return x + dep[0, 0].astype(x.dtype)


def _local_step(x, pos_col, target, get_w, p, emit):
    w = get_w("ffn1", None)
    h1, h1b, xh1, rs1, a1, b1 = _ffn_fwd(x, w["g1"], w["u1"], w["d1"], p["ln1_g"], p["ln1_b"], "ffn1_fwd")

    w.update(get_w("win", h1b))
    qkv = _matmul(h1b, w["win"], "nt", F32, 512, 512, D, "proj_qkv", b_off=0, n_out=QKV_W)
    z = _matmul(h1b, w["win"], "nt", F32, 512, 512, D, "proj_z", b_off=QKV_W // 512, n_out=2 * D)
    glog = _matmul(h1b, w["win"], "nt", F32, 512, 512, D, "proj_g", b_off=(QKV_W + 2 * D) // 512, n_out=2 * D)

    half = jnp.arange(0, HEAD_DIM, 2, dtype=F32) / HEAD_DIM
    inv_freq = ROPE_THETA ** (-half)
    invf = jnp.tile(inv_freq, 4).reshape(1, BLK)
    sign = jnp.tile(jnp.concatenate([-jnp.ones((32,), F32), jnp.ones((32,), F32)]), 2).reshape(1, BLK)
    cos_t, sin_t = _rope_tables(pos_col, invf, sign)

    os_, lses = [], []
    for gi in range(3):
        o, lse = _attn_fwd(qkv, cos_t, sin_t, gi)
        os_.append(o)
        lses.append(lse)
    ya = _combine_fwd(os_, lses)
    bst = p["gmlp_b_s"].T
    yg = _gmlp_fwd(z, p["gmlp_w_s"], bst, p["gmlp_ln_g"], p["gmlp_ln_b"])
    w.update(get_w("mix", yg))
    h2, h2b, xh2, rs2, merged, bra, brg = _merge_fwd(ya, yg, glog, p["b_gates"], h1, w["ab"], w["gb"], w["o"],
                                                      p["ln2_g"], p["ln2_b"])
    w.update(get_w("ffn2", h2b))
    h3, _, xh3, rs3, a2, b2 = _ffn_fwd(h2, w["g2"], w["u2"], w["d2"], p["ln3_g"], p["ln3_b"], "ffn2_fwd")
    dh3, loss = _loss_head(h3, target)

    gp = {}
    dh2, da2, db2, hm2, df2, gp["ln3_g"], gp["ln3_b"] = _ffn_bwd(dh3, None, xh3, rs3, p["ln3_g"], a2, b2,
                                                                 w["g2"], w["u2"], w["d2"], "ffn2_bwd")
    tok = emit("ffn2", {
        "g2": _matmul(da2, h2b, "tn", BF16, 256, D, S, "wgrad_g2"),
        "u2": _matmul(db2, h2b, "tn", BF16, 256, D, S, "wgrad_u2"),
        "d2": _matmul(hm2, df2, "tn", BF16, 256, D, S, "wgrad_d2")})

    (dr2, dr2b, dlog, dba, dbg, dya, dyg, gp["b_gates"], gp["ln2_g"], gp["ln2_b"]) = _merge_bwd(
        dh2, xh2, rs2, _tie(p["ln2_g"], tok), bra, brg, glog, p["b_gates"], w["ab"], w["gb"], w["o"])
    tok = emit("mix", {
        "o": _matmul(merged, dr2b, "tn", BF16, 256, D, S, "wgrad_o"),
        "ab": _matmul(dba, ya, "tn", BF16, 256, GW, S, "wgrad_ab"),
        "gb": _matmul(yg, dbg, "tn", BF16, 256, D, S, "wgrad_gb")})

    seg = (jnp.arange(GW)[:, None] // HEAD_DIM == jnp.arange(GW)[None, :] // HEAD_DIM).astype(BF16)
    do0, do1, do2, c0, c1, c2 = _combine_bwd(dya, os_, lses, _tie(seg, tok))
    dqkv = []
    for gi, (do, ct) in enumerate(((do0, c0), (do1, c1), (do2, c2))):
        dqkv.append(_attn_bwd(qkv, cos_t, sin_t, do, lses[gi], ct, gi))
    dz, gp["gmlp_w_s"], dbst, gp["gmlp_ln_g"], gp["gmlp_ln_b"] = _gmlp_bwd(
        z, dyg, p["gmlp_w_s"], bst, p["gmlp_ln_g"], p["gmlp_ln_b"])
    gp["gmlp_b_s"] = dbst.T
    dproj = jnp.concatenate(
        [dqkv[gi][0] for gi in range(3)] + [dqkv[gi][1].astype(BF16) for gi in range(3)]
        + [dqkv[gi][2].astype(BF16) for gi in range(3)] + [dz, dlog], axis=1)
    tok = emit("win", {"win": _matmul(dproj, h1b, "tn", BF16, 256, D, S, "wgrad_win")})
    dh1m = _matmul(dproj, w["win"], "nn", F32, 512, D, 512, "dproj_to_dh1", dep=tok)

    dx, da1, db1, hm1, df1, gp["ln1_g"], gp["ln1_b"] = _ffn_bwd(dr2, dh1m, xh1, rs1, p["ln1_g"], a1, b1,
                                                                w["g1"], w["u1"], w["d1"], "ffn1_bwd")
    emit("ffn1", {
        "g1": _matmul(da1, x, "tn", BF16, 256, D, S, "wgrad_g1"),
        "u1": _matmul(db1, x, "tn", BF16, 256, D, S, "wgrad_u1"),
        "d1": _matmul(hm1, df1, "tn", BF16, 256, D, S, "wgrad_d1")})
    return loss, dx, gp


_FLIPS = [(mx, my, mc) for mx in (0, 1) for my in (0, 1) for mc in (0, 1)][1:]
_GROUPS = {"ffn1": ("g1", "u1", "d1"), "win": ("win",), "mix": ("ab", "gb", "o"), "ffn2": ("g2", "u2", "d2")}
_HBM = pl.BlockSpec(memory_space=pltpu.HBM)
_SEM = pl.BlockSpec(memory_space=pltpu.SEMAPHORE)
_EFFECT = pltpu.SideEffectType.DATAFLOW_SIDE_EFFECTING


def _me():
    return 4 * lax.axis_index("x") + 2 * lax.axis_index("y") + lax.axis_index("c")


def _exchange_start(srcs, scatter, name, after=None):
    nw = len(srcs)
    me = _me()
    lands = []
    for a in srcs:
        own = lax.dynamic_index_in_dim(a, me, 0, keepdims=True) if scatter else a[None]
        shape = a.shape if scatter else (N_DEV,) + a.shape
        lands.append(lax.dynamic_update_slice(lax.empty(shape, a.dtype), own, (me, 0, 0)))
    n_after = 0 if after is None else 1

    def body(*refs):
        src, land = refs[:nw], refs[nw:2 * nw]
        send_sems, recv_sems = refs[2 * nw + n_after], refs[2 * nw + n_after + 1]
        token = refs[-1]
        x, y, c = lax.axis_index("x"), lax.axis_index("y"), lax.axis_index("c")
        mine = 4 * x + 2 * y + c
        for mx, my, mc in _FLIPS:
            px, py, pc = x ^ mx, y ^ my, c ^ mc
            peer = 4 * px + 2 * py + pc
            for k in range(nw):
                pltpu.make_async_remote_copy(
                    src_ref=src[k].at[peer] if scatter else src[k], dst_ref=land[k].at[mine],
                    send_sem=send_sems.at[k], recv_sem=recv_sems.at[k],
                    device_id=(px, py, pc), device_id_type=MESH).start()
        token[...] = jnp.zeros_like(token)

    ins = [pltpu.with_memory_space_constraint(a, pltpu.HBM) for a in list(srcs) + lands]
    outs = pl.pallas_call(
        body,
        name=name,
        in_specs=[_HBM] * (2 * nw) + [pl.BlockSpec(memory_space=pl.ANY)] * n_after,
        out_specs=[_SEM, _SEM] + [_HBM] * (2 * nw) + [pl.BlockSpec(memory_space=pltpu.VMEM)],
        out_shape=[pltpu.SemaphoreType.DMA((nw,)), pltpu.SemaphoreType.DMA((nw,))]
        + [pltpu.HBM(a.shape, a.dtype) for a in ins] + [jax.ShapeDtypeStruct((8, 128), F32)],
        input_output_aliases={k: 2 + k for k in range(2 * nw)},
        compiler_params=pltpu.CompilerParams(has_side_effects=_EFFECT),
    )(*ins, *([] if after is None else [after]))
    return outs[0], outs[1], outs[2:2 + nw], outs[2 + nw:2 + 2 * nw], outs[-1]


def _exchange_wait(started, after, name):
    send_sems, recv_sems, srcs, lands, _ = started
    nw = len(srcs)

    def body(*refs):
        land = refs[nw:2 * nw]
        ss, rs = refs[2 * nw], refs[2 * nw + 1]
        me3 = (lax.axis_index("x"), lax.axis_index("y"), lax.axis_index("c"))
        for k in range(nw):
            seven = land[k].at[pl.ds(0, N_DEV - 1)]
            cp = pltpu.make_async_remote_copy(src_ref=seven, dst_ref=seven, send_sem=ss.at[k], recv_sem=rs.at[k],
                                              device_id=me3, device_id_type=MESH)
            cp.wait_send()
            cp.wait_recv()

    outs = pl.pallas_call(
        body,
        name=name,
        in_specs=[_HBM] * (2 * nw) + [_SEM, _SEM, pl.BlockSpec(memory_space=pl.ANY)],
        out_specs=[_HBM] * (2 * nw),
        out_shape=[pltpu.HBM(a.shape, a.dtype) for a in list(srcs) + list(lands)],
        input_output_aliases={k: k for k in range(2 * nw)},
        compiler_params=pltpu.CompilerParams(has_side_effects=_EFFECT),
    )(*srcs, *lands, send_sems, recv_sems, after)
    return outs[nw:]


def _exchange(arrays, scatter, name):
    nw = len(arrays)

    def body(*refs):
        ins, outs = refs[:nw], refs[nw:2 * nw]
        send_sems, recv_sems, local_sems = refs[2 * nw:]
        x, y, c = lax.axis_index("x"), lax.axis_index("y"), lax.axis_index("c")
        me = 4 * x + 2 * y + c
        local = []
        for k in range(nw):
            src = ins[k].at[me] if scatter else ins[k]
            cp = pltpu.make_async_copy(src, outs[k].at[me], local_sems.at[k])
            cp.start()
            local.append(cp)
        for mx, my, mc in _FLIPS:
            px, py, pc = x ^ mx, y ^ my, c ^ mc
            peer = 4 * px + 2 * py + pc
            for k in range(nw):
                src = ins[k].at[peer] if scatter else ins[k]
                pltpu.make_async_remote_copy(
                    src_ref=src, dst_ref=outs[k].at[me], send_sem=send_sems.at[k], recv_sem=recv_sems.at[k],
                    device_id=(px, py, pc), device_id_type=MESH).start()
        for k in range(nw):
            seven = outs[k].at[pl.ds(0, N_DEV - 1)]
            pltpu.make_async_remote_copy(
                src_ref=seven, dst_ref=seven, send_sem=send_sems.at[k], recv_sem=recv_sems.at[k],
                device_id=(x, y, c), device_id_type=MESH).wait()
            local[k].wait()

    hbm = pl.BlockSpec(memory_space=pl.ANY)
    out_shape = [jax.ShapeDtypeStruct(a.shape if scatter else (N_DEV,) + a.shape, a.dtype) for a in arrays]
    return pl.pallas_call(
        body,
        name=name,
        in_specs=[hbm] * nw,
        out_specs=[hbm] * nw,
        out_shape=out_shape,
        scratch_shapes=[pltpu.SemaphoreType.DMA((nw,)), pltpu.SemaphoreType.DMA((nw,)), pltpu.SemaphoreType.DMA((nw,))],
    )(*arrays)


def _sum8(parts, name, tm=None):
    _, r, c = parts.shape
    tm = tm or (r if r <= 512 else (r // 2 if (r // 2) % 8 == 0 and r // 2 <= 640 else 64))
    assert r % tm == 0

    def body(p_ref, o_ref):
        acc = p_ref[0].astype(F32)
        for j in range(1, N_DEV):
            acc = acc + p_ref[j].astype(F32)
        o_ref[...] = acc

    return pl.pallas_call(
        body,
        name=name,
        grid=(r // tm,),
        in_specs=[pl.BlockSpec((N_DEV, tm, c), lambda i: (0, i, 0))],
        out_specs=pl.BlockSpec((tm, c), lambda i: (i, 0)),
        out_shape=jax.ShapeDtypeStruct((r, c), F32),
        compiler_params=_cp(("parallel",), 48),
    )(parts)


def _adamw(wv, g, m, v, name):
    r, c = wv.shape
    tm = r if r <= 512 else 256
    assert r % tm == 0

    def body(w_ref, g_ref, m_ref, v_ref, d_ref, mo_ref, vo_ref):
        gv = g_ref[...]
        mn = ADAM_B1 * m_ref[...] + (1.0 - ADAM_B1) * gv
        vn = ADAM_B2 * v_ref[...] + (1.0 - ADAM_B2) * (gv * gv)
        m_hat = mn / (1.0 - ADAM_B1 ** ADAM_STEP)
        v_hat = vn / (1.0 - ADAM_B2 ** ADAM_STEP)
        d_ref[...] = -ADAM_LR * (m_hat / (jnp.sqrt(v_hat) + ADAM_EPS) + ADAM_WD * w_ref[...])
        mo_ref[...] = mn
        vo_ref[...] = vn

    sp = pl.BlockSpec((tm, c), lambda i: (i, 0))
    return pl.pallas_call(
        body,
        name=name,
        grid=(r // tm,),
        in_specs=[sp] * 4,
        out_specs=[sp] * 3,
        out_shape=[jax.ShapeDtypeStruct((r, c), F32)] * 3,
        compiler_params=_cp(("parallel",)),
    )(wv, g, m, v)


_WEIGHTS = ["ffn1_w_gate", "ffn1_w_up", "ffn1_w_down", "ln1_g", "ln1_b", "w_in", "b_gates", "gmlp_ln_g", "gmlp_ln_b",
            "gmlp_w_s", "gmlp_b_s", "w_attn_branch", "w_gmlp_branch", "w_out", "ln2_g", "ln2_b", "ffn2_w_gate",
            "ffn2_w_up", "ffn2_w_down", "ln3_g", "ln3_b"]
_BIG_OF = {"ffn1_w_gate": ("g1", True), "ffn1_w_up": ("u1", True), "ffn1_w_down": ("d1", False), "w_in": ("win", True),
           "w_attn_branch": ("ab", True), "w_gmlp_branch": ("gb", False), "w_out": ("o", False),
           "ffn2_w_gate": ("g2", True), "ffn2_w_up": ("u2", True), "ffn2_w_down": ("d2", False)}
_SMALL = [n for n in _WEIGHTS if n not in _BIG_OF]
_SMALL_ROWS = {"gmlp_w_s": 128, "b_gates": 2}
_SMALL_SLOT = 8


def _pack_small(d):
    rows = []
    for n in _SMALL:
        r = d[n].reshape(-1, D)
        slot = max(r.shape[0], _SMALL_SLOT)
        rows.append(jnp.pad(r, ((0, slot - r.shape[0]), (0, 0))))
    return jnp.concatenate(rows, axis=0)


def _unpack_small(packed, shapes):
    out, at = {}, 0
    for n in _SMALL:
        k = _SMALL_ROWS.get(n, 1)
        out[n] = packed[at:at + k].reshape(shapes[n])
        at += max(k, _SMALL_SLOT)
    return out


def kernel(x, positions, ffn1_w_gate, ffn1_w_up, ffn1_w_down, ln1_g, ln1_b, w_in, b_gates, gmlp_ln_g, gmlp_ln_b, gmlp_w_s, gmlp_b_s, w_attn_branch, w_gmlp_branch, w_out, ln2_g, ln2_b, ffn2_w_gate, ffn2_w_up, ffn2_w_down, ln3_g, ln3_b, loss_target, m_ffn1_w_gate, m_ffn1_w_up, m_ffn1_w_down, m_ln1_g, m_ln1_b, m_w_in, m_b_gates, m_gmlp_ln_g, m_gmlp_ln_b, m_gmlp_w_s, m_gmlp_b_s, m_w_attn_branch, m_w_gmlp_branch, m_w_out, m_ln2_g, m_ln2_b, m_ffn2_w_gate, m_ffn2_w_up, m_ffn2_w_down, m_ln3_g, m_ln3_b, v_ffn1_w_gate, v_ffn1_w_up, v_ffn1_w_down, v_ln1_g, v_ln1_b, v_w_in, v_b_gates, v_gmlp_ln_g, v_gmlp_ln_b, v_gmlp_w_s, v_gmlp_b_s, v_w_attn_branch, v_w_gmlp_branch, v_w_out, v_ln2_g, v_ln2_b, v_ffn2_w_gate, v_ffn2_w_up, v_ffn2_w_down, v_ln3_g, v_ln3_b):
    args = dict(locals())
    wts = {n: args[n] for n in _WEIGHTS}
    ms = {n: args["m_" + n] for n in _WEIGHTS}
    vs = {n: args["v_" + n] for n in _WEIGHTS}

    name_of = {key: (n, tr) for n, (key, tr) in _BIG_OF.items()}

    started, tok = {}, None
    for grp, keys in _GROUPS.items():
        shards = []
        for key in keys:
            n, tr = name_of[key]
            s2 = wts[n][0]
            shards.append((s2.T if tr else s2).astype(BF16))
        started[grp] = _exchange_start(shards, False, "gather_start_" + grp, after=tok)
        tok = started[grp][4]
    all_started = tok

    def get_w(grp, after):
        if after is None:
            after = all_started
        lands = _exchange_wait(started[grp], after, "gather_wait_" + grp)
        return {key: g.reshape(-1, g.shape[-1]) for key, g in zip(_GROUPS[grp], lands)}

    sent = {}

    def emit(grp, grads):
        parts = [grads[key].reshape(N_DEV, -1, grads[key].shape[-1]) for key in _GROUPS[grp]]
        sent[grp] = _exchange_start(parts, True, "scatter_start_" + grp)
        return sent[grp][4]

    p = {n: (wts[n][0] if n in ("gmlp_w_s", "gmlp_b_s") else wts[n]) for n in _SMALL}
    loss, dx, gp = _local_step(x[0], positions.reshape(S, 1), loss_target[0], get_w, p, emit)
    loss = lax.psum(loss[0, 0], ("x", "y", "c"))

    small_parts = _exchange([_tie(_pack_small({n: gp[n] for n in _SMALL}), sent["ffn1"][4])], False, "gather_small_grads")[0]
    g_small = _sum8(small_parts, "sum_small")
    grads, deltas, new_m, new_v = {}, {}, {}, {}
    dl, mn, vn = _adamw(_pack_small({n: wts[n] for n in _SMALL}), g_small,
                        _pack_small({n: ms[n] for n in _SMALL}), _pack_small({n: vs[n] for n in _SMALL}), "adamw_small")
    shapes = {n: wts[n].shape for n in _SMALL}
    for dst, packed in ((grads, g_small), (deltas, dl), (new_m, mn), (new_v, vn)):
        dst.update(_unpack_small(packed, shapes))

    after = dl
    for grp in ("ffn2", "mix", "win", "ffn1"):
        parts = _exchange_wait(sent[grp], after, "scatter_wait_" + grp)
        for key, part in zip(_GROUPS[grp], parts):
            n, tr = name_of[key]
            g = _sum8(part, "sum_" + key)
            g = g.T if tr else g
            dl, mn, vn = _adamw(wts[n][0], g, ms[n][0], vs[n][0], "adamw_" + key)
            grads[n], deltas[n], new_m[n], new_v[n] = g[None], dl[None], mn[None], vn[None]
            after = dl

    return (loss, dx[None], *[grads[n] for n in _WEIGHTS], *[deltas[n] for n in _WEIGHTS],
            *[new_m[n] for n in _WEIGHTS], *[new_v[n] for n in _WEIGHTS])
```

```python
import functools
import math

import jax
import jax.numpy as jnp
from jax import lax
from jax.experimental import pallas as pl
from jax.experimental.pallas import tpu as pltpu

F32 = jnp.float32
BF16 = jnp.bfloat16

N_DEV = 8
D = 1024
S = 2048
F = 2816
HEAD_DIM = 64
HEADS = 8
GW = HEADS * HEAD_DIM
PATTERN_DILATIONS = (1, 4, 16)
BLK = 128
QKV_W = 3 * 3 * GW
IN_W = QKV_W + 2 * D + 2 * D
ROPE_THETA = 10000.0
ALPHA = 2.0 ** 0.25
LN_EPS = 1e-5
ADAM_LR, ADAM_B1, ADAM_B2, ADAM_EPS, ADAM_WD, ADAM_STEP = 0.001, 0.9, 0.999, 1e-08, 0.01, 10
NEG = -1e30
MESH = pl.DeviceIdType.MESH


def _cp(sem=None, vmem_mb=None):
    kw = {}
    if sem is not None:
        kw["dimension_semantics"] = sem
    if vmem_mb is not None:
        kw["vmem_limit_bytes"] = vmem_mb << 20
    return pltpu.CompilerParams(**kw)


def _dot_nn(a, b):
    return lax.dot_general(a, b, (((1,), (0,)), ((), ())), preferred_element_type=F32)


def _dot_nt(a, b):
    return lax.dot_general(a, b, (((1,), (1,)), ((), ())), preferred_element_type=F32)


def _dot_tn(a, b):
    return lax.dot_general(a, b, (((0,), (0,)), ((), ())), preferred_element_type=F32)


def _ln_fwd(r, g, b):
    mu = jnp.mean(r, axis=-1, keepdims=True)
    xc = r - mu
    var = jnp.mean(xc * xc, axis=-1, keepdims=True)
    rstd = lax.rsqrt(var + LN_EPS)
    xhat = xc * rstd
    return xhat * g + b, xhat, rstd


def _ln_bwd(dh, xhat, rstd, g):
    dxh = dh * g
    m1 = jnp.mean(dxh, axis=-1, keepdims=True)
    m2 = jnp.mean(dxh * xhat, axis=-1, keepdims=True)
    return rstd * (dxh - m1 - xhat * m2)


def _sigmoid(x):
    return 1.0 / (1.0 + jnp.exp(-x))


def _colsum(x):
    return jnp.sum(x, axis=0, keepdims=True)


def _matmul(a, b, mode, out_dtype, tm, tn, tk, name, b_off=0, n_out=None, dep=None):
    n_dep = 0 if dep is None else 1
    if mode == "nn":
        m, k = a.shape
        n = b.shape[1]
    elif mode == "nt":
        m, k = a.shape
        n = n_out if n_out is not None else b.shape[0]
    else:
        k, m = a.shape
        n = b.shape[1]
    nk = k // tk
    assert m % tm == 0 and n % tn == 0 and k % tk == 0
    dot = {"nn": _dot_nn, "nt": _dot_nt, "tn": _dot_tn}[mode]

    def body(a_ref, b_ref, *rest):
        o_ref, scr = rest[n_dep], rest[n_dep + 1:]
        r = dot(a_ref[...].astype(BF16), b_ref[...].astype(BF16))
        if nk == 1:
            o_ref[...] = r.astype(out_dtype)
        else:
            acc = scr[0]
            kk = pl.program_id(2)

            @pl.when(kk == 0)
            def _():
                acc[...] = r

            @pl.when(kk > 0)
            def _():
                acc[...] += r

            @pl.when(kk == nk - 1)
            def _():
                o_ref[...] = acc[...].astype(out_dtype)

    if mode == "nn":
        a_spec = pl.BlockSpec((tm, tk), lambda i, j, kk: (i, kk))
        b_spec = pl.BlockSpec((tk, tn), lambda i, j, kk: (kk, j))
    elif mode == "nt":
        a_spec = pl.BlockSpec((tm, tk), lambda i, j, kk: (i, kk))
        b_spec = pl.BlockSpec((tn, tk), lambda i, j, kk: (j + b_off, kk))
    else:
        a_spec = pl.BlockSpec((tk, tm), lambda i, j, kk: (kk, i))
        b_spec = pl.BlockSpec((tk, tn), lambda i, j, kk: (kk, j))
    return pl.pallas_call(
        body,
        name=name,
        grid=(m // tm, n // tn, nk),
        in_specs=[a_spec, b_spec] + [pl.BlockSpec(memory_space=pl.ANY)] * n_dep,
        out_specs=pl.BlockSpec((tm, tn), lambda i, j, kk: (i, j)),
        out_shape=jax.ShapeDtypeStruct((m, n), out_dtype),
        scratch_shapes=[] if nk == 1 else [pltpu.VMEM((tm, tn), F32)],
        compiler_params=_cp(("parallel", "parallel", "arbitrary"), 56),
    )(a, b, *([] if dep is None else [dep]))


def _ffn_fwd(x, wgt, wut, wd, ln_g, ln_b, name, tm=512, tk=256):
    nk = F // tk

    def body(x_ref, wg_ref, wu_ref, wd_ref, g_ref, b_ref, h_ref, hb_ref, xh_ref, rs_ref, a_ref, bb_ref, acc):
        kk = pl.program_id(1)
        xb = x_ref[...].astype(BF16)
        a = _dot_nt(xb, wg_ref[...])
        b = _dot_nt(xb, wu_ref[...])
        a_ref[...] = a
        bb_ref[...] = b
        hm = (a * _sigmoid(a)) * b
        contrib = _dot_nn(hm.astype(BF16), wd_ref[...])

        @pl.when(kk == 0)
        def _():
            acc[...] = contrib

        @pl.when(kk > 0)
        def _():
            acc[...] += contrib

        @pl.when(kk == nk - 1)
        def _():
            r = ALPHA * x_ref[...] + 0.5 * acc[...]
            h, xhat, rstd = _ln_fwd(r, g_ref[...], b_ref[...])
            h_ref[...] = h
            hb_ref[...] = h.astype(BF16)
            xh_ref[...] = xhat
            rs_ref[...] = rstd

    row = pl.BlockSpec((tm, D), lambda i, kk: (i, 0))
    wsp = pl.BlockSpec((tk, D), lambda i, kk: (kk, 0))
    vec = pl.BlockSpec((1, D), lambda i, kk: (0, 0))
    mid = pl.BlockSpec((tm, tk), lambda i, kk: (i, kk))
    return pl.pallas_call(
        body,
        name=name,
        grid=(S // tm, nk),
        in_specs=[row, wsp, wsp, wsp, vec, vec],
        out_specs=[row, row, row, pl.BlockSpec((tm, 1), lambda i, kk: (i, 0)), mid, mid],
        out_shape=[
            jax.ShapeDtypeStruct((S, D), F32),
            jax.ShapeDtypeStruct((S, D), BF16),
            jax.ShapeDtypeStruct((S, D), F32),
            jax.ShapeDtypeStruct((S, 1), F32),
            jax.ShapeDtypeStruct((S, F), F32),
            jax.ShapeDtypeStruct((S, F), F32),
        ],
        scratch_shapes=[pltpu.VMEM((tm, D), F32)],
        compiler_params=_cp(("parallel", "arbitrary"), 56),
    )(x, wgt, wut, wd, ln_g, ln_b)


def _ffn_bwd(dh_a, dh_b, xhat, rstd, ln_g, a, b, wgt, wut, wd, name, tm=512, tk=256):
    nk = F // tk
    two = dh_b is not None

    def body(*refs):
        if two:
            dha_ref, dhb_ref = refs[0], refs[1]
            refs = refs[2:]
        else:
            dha_ref = refs[0]
            refs = refs[1:]
        (xh_ref, rs_ref, g_ref, a_ref, b_ref, wg_ref, wu_ref, wd_ref,
         dx_ref, da_ref, db_ref, hm_ref, df_ref, dg_ref, dbias_ref, acc, df_scr) = refs
        i = pl.program_id(0)
        kk = pl.program_id(1)

        @pl.when(kk == 0)
        def _():
            dh = dha_ref[...]
            if two:
                dh = ALPHA * dh + dhb_ref[...]
            xhat = xh_ref[...]
            dr = _ln_bwd(dh, xhat, rs_ref[...], g_ref[...])
            dfb = (0.5 * dr).astype(BF16)
            df_scr[...] = dfb
            df_ref[...] = dfb
            acc[...] = ALPHA * dr
            sg = _colsum(dh * xhat)
            sb = _colsum(dh)

            @pl.when(i == 0)
            def _():
                dg_ref[...] = sg
                dbias_ref[...] = sb

            @pl.when(i > 0)
            def _():
                dg_ref[...] += sg
                dbias_ref[...] += sb

        dhm = _dot_nt(df_scr[...], wd_ref[...])
        av = a_ref[...]
        bv = b_ref[...]
        sig = _sigmoid(av)
        silu = av * sig
        da = (dhm * bv * (sig * (1.0 + av * (1.0 - sig)))).astype(BF16)
        db = (dhm * silu).astype(BF16)
        da_ref[...] = da
        db_ref[...] = db
        hm_ref[...] = (silu * bv).astype(BF16)
        acc[...] += _dot_nn(da, wg_ref[...]) + _dot_nn(db, wu_ref[...])

        @pl.when(kk == nk - 1)
        def _():
            dx_ref[...] = acc[...]

    row = pl.BlockSpec((tm, D), lambda i, kk: (i, 0))
    wsp = pl.BlockSpec((tk, D), lambda i, kk: (kk, 0))
    vec = pl.BlockSpec((1, D), lambda i, kk: (0, 0))
    mid = pl.BlockSpec((tm, tk), lambda i, kk: (i, kk))
    ins = [dh_a] + ([dh_b] if two else []) + [xhat, rstd, ln_g, a, b, wgt, wut, wd]
    in_specs = [row] * (2 if two else 1) + [row, pl.BlockSpec((tm, 1), lambda i, kk: (i, 0)), vec, mid, mid, wsp, wsp, wsp]
    return pl.pallas_call(
        body,
        name=name,
        grid=(S // tm, nk),
        in_specs=in_specs,
        out_specs=[row, mid, mid, mid, row, vec, vec],
        out_shape=[
            jax.ShapeDtypeStruct((S, D), F32),
            jax.ShapeDtypeStruct((S, F), BF16),
            jax.ShapeDtypeStruct((S, F), BF16),
            jax.ShapeDtypeStruct((S, F), BF16),
            jax.ShapeDtypeStruct((S, D), BF16),
            jax.ShapeDtypeStruct((1, D), F32),
            jax.ShapeDtypeStruct((1, D), F32),
        ],
        scratch_shapes=[pltpu.VMEM((tm, D), F32), pltpu.VMEM((tm, D), BF16)],
        compiler_params=_cp(("arbitrary", "arbitrary"), 56),
    )(*ins)


def _rope_tables(pos_col, invf, sign, tm=512):
    def body(p_ref, f_ref, s_ref, c_out, s_out):
        ang = p_ref[...].astype(F32) * f_ref[...]
        c_out[...] = jnp.cos(ang)
        s_out[...] = jnp.sin(ang) * s_ref[...]

    vec = pl.BlockSpec((1, BLK), lambda i: (0, 0))
    out = pl.BlockSpec((tm, BLK), lambda i: (i, 0))
    return pl.pallas_call(
        body,
        name="rope_tables",
        grid=(S // tm,),
        in_specs=[pl.BlockSpec((tm, 1), lambda i: (i, 0)), vec, vec],
        out_specs=[out, out],
        out_shape=[jax.ShapeDtypeStruct((S, BLK), F32)] * 2,
        compiler_params=_cp(("parallel",)),
    )(pos_col, invf, sign)


def _lane_lo():
    return (lax.broadcasted_iota(jnp.int32, (BLK, GW), 1) % HEAD_DIM) < (HEAD_DIM // 2)


def _swap_halves(t, lo):
    return jnp.where(lo, pltpu.roll(t, GW - HEAD_DIM // 2, 1), pltpu.roll(t, HEAD_DIM // 2, 1))


def _rope(t, cosf, sinf, lo):
    return t * cosf + _swap_halves(t, lo) * sinf


def _unrope(g, cosf, sinf, lo):
    return g * cosf + _swap_halves(g * sinf, lo)


def _tile4(v):
    return jnp.concatenate([v, v, v, v], axis=1)


def _band_mask(n):
    qi = lax.broadcasted_iota(jnp.int32, (BLK, 2 * BLK), 0)
    kj = lax.broadcasted_iota(jnp.int32, (BLK, 2 * BLK), 1)
    dist = qi + BLK - kj
    return (dist >= 0) & (dist <= BLK) & ((kj >= BLK) | (n >= 1))


def _attn_specs(gi, d):
    nq = QKV_W // GW

    def cur(off):
        return pl.BlockSpec((BLK, GW), lambda r, n: (n, r * nq + off + gi))

    def prev(off):
        return pl.BlockSpec((BLK, GW), lambda r, n: (jnp.maximum(n - 1, 0), r * nq + off + gi))

    tab_c = pl.BlockSpec((BLK, BLK), lambda r, n: (n, r))
    tab_p = pl.BlockSpec((BLK, BLK), lambda r, n: (jnp.maximum(n - 1, 0), r))
    blk = pl.BlockSpec((BLK, GW), lambda r, n: (n, r))
    return cur, prev, tab_c, tab_p, blk


def _attn_fwd(qkv, cos_t, sin_t, gi):
    d = PATTERN_DILATIONS[gi]
    sub = S // d
    nb = sub // BLK
    cur, prev, tab_c, tab_p, blk = _attn_specs(gi, d)

    def body(q_ref, kc_ref, kp_ref, vc_ref, vp_ref, cc_ref, cp_ref, sc_ref, sp_ref, o_ref, l_ref):
        n = pl.program_id(1)
        lo = _lane_lo()
        cc, sc = _tile4(cc_ref[...]), _tile4(sc_ref[...])
        cp, sp = _tile4(cp_ref[...]), _tile4(sp_ref[...])
        q = (_rope(q_ref[...], cc, sc, lo) * (HEAD_DIM ** -0.5)).astype(BF16)
        k = jnp.concatenate([_rope(kp_ref[...], cp, sp, lo), _rope(kc_ref[...], cc, sc, lo)], axis=0).astype(BF16)
        v = jnp.concatenate([vp_ref[...], vc_ref[...]], axis=0).astype(BF16)
        mask = _band_mask(n)
        outs, lses = [], []
        for h in range(HEADS):
            sl = slice(h * HEAD_DIM, (h + 1) * HEAD_DIM)
            s = jnp.where(mask, _dot_nt(q[:, sl], k[:, sl]), NEG)
            m = jnp.max(s, axis=1, keepdims=True)
            p = jnp.exp(s - m)
            l = jnp.sum(p, axis=1, keepdims=True)
            outs.append(_dot_nn((p / l).astype(BF16), v[:, sl]))
            lses.append(jnp.broadcast_to(m + jnp.log(l), (BLK, HEAD_DIM)))
        o_ref[...] = jnp.concatenate(outs, axis=1)
        l_ref[...] = jnp.concatenate(lses, axis=1)

    qv = qkv.reshape(sub, d * QKV_W)
    cv = cos_t.reshape(sub, d * BLK)
    sv = sin_t.reshape(sub, d * BLK)
    o, lse = pl.pallas_call(
        body,
        name=f"attn_fwd_g{gi}",
        grid=(d, nb),
        in_specs=[cur(0), cur(3), prev(3), cur(6), prev(6), tab_c, tab_p, tab_c, tab_p],
        out_specs=[blk, blk],
        out_shape=[jax.ShapeDtypeStruct((sub, d * GW), F32)] * 2,
        compiler_params=_cp(("parallel", "parallel")),
    )(qv, qv, qv, qv, qv, cv, cv, sv, sv)
    return o.reshape(S, GW), lse.reshape(S, GW)


def _attn_bwd(qkv, cos_t, sin_t, do, lse, cterm, gi):
    d = PATTERN_DILATIONS[gi]
    sub = S // d
    nb = sub // BLK
    cur, prev, tab_c, tab_p, blk = _attn_specs(gi, d)

    def body(q_ref, kc_ref, kp_ref, vc_ref, vp_ref, cc_ref, cp_ref, sc_ref, sp_ref, do_ref, l_ref, c_ref,
             dq_ref, dk_ref, dv_ref):
        n = pl.program_id(1)
        lo = _lane_lo()
        cc, sc = _tile4(cc_ref[...]), _tile4(sc_ref[...])
        cp, sp = _tile4(cp_ref[...]), _tile4(sp_ref[...])
        q = (_rope(q_ref[...], cc, sc, lo) * (HEAD_DIM ** -0.5)).astype(BF16)
        k = jnp.concatenate([_rope(kp_ref[...], cp, sp, lo), _rope(kc_ref[...], cc, sc, lo)], axis=0).astype(BF16)
        v = jnp.concatenate([vp_ref[...], vc_ref[...]], axis=0).astype(BF16)
        dov = do_ref[...]
        lse_v = l_ref[...]
        cv_ = c_ref[...]
        mask = _band_mask(n)
        dqs, dks, dvs = [], [], []
        for h in range(HEADS):
            sl = slice(h * HEAD_DIM, (h + 1) * HEAD_DIM)
            s = jnp.where(mask, _dot_nt(q[:, sl], k[:, sl]), NEG)
            p = jnp.exp(s - lse_v[:, h * HEAD_DIM:h * HEAD_DIM + 1])
            dp = _dot_nt(dov[:, sl], v[:, sl])
            ds = (p * (dp - cv_[:, h * HEAD_DIM:h * HEAD_DIM + 1])).astype(BF16)
            dqs.append(_dot_nn(ds, k[:, sl]) * (HEAD_DIM ** -0.5))
            dks.append(_dot_tn(ds, q[:, sl]))
            dvs.append(_dot_tn(p.astype(BF16), dov[:, sl]))
        dq_ref[...] = _unrope(jnp.concatenate(dqs, axis=1), cc, sc, lo).astype(BF16)
        dk = jnp.concatenate(dks, axis=1)
        dv = jnp.concatenate(dvs, axis=1)
        here = pl.ds(pl.multiple_of(n * BLK, BLK), BLK)
        dk_ref[here, :] = _unrope(dk[BLK:], cc, sc, lo)
        dv_ref[here, :] = dv[BLK:]

        @pl.when(n > 0)
        def _():
            before = pl.ds(pl.multiple_of((n - 1) * BLK, BLK), BLK)
            dk_ref[before, :] += _unrope(dk[:BLK], cp, sp, lo)
            dv_ref[before, :] += dv[:BLK]

    qv = qkv.reshape(sub, d * QKV_W)
    cv = cos_t.reshape(sub, d * BLK)
    sv = sin_t.reshape(sub, d * BLK)
    whole = pl.BlockSpec((sub, GW), lambda r, n: (0, r))
    dq, dk, dv = pl.pallas_call(
        body,
        name=f"attn_bwd_g{gi}",
        grid=(d, nb),
        in_specs=[cur(0), cur(3), prev(3), cur(6), prev(6), tab_c, tab_p, tab_c, tab_p, blk, blk, blk],
        out_specs=[blk, whole, whole],
        out_shape=[
            jax.ShapeDtypeStruct((sub, d * GW), BF16),
            jax.ShapeDtypeStruct((sub, d * GW), F32),
            jax.ShapeDtypeStruct((sub, d * GW), F32),
        ],
        compiler_params=_cp(("arbitrary", "arbitrary"), 48),
    )(qv, qv, qv, qv, qv, cv, cv, sv, sv, do.reshape(sub, d * GW), lse.reshape(sub, d * GW), cterm.reshape(sub, d * GW))
    return dq.reshape(S, GW), dk.reshape(S, GW), dv.reshape(S, GW)


def _group_weights(l0, l1, l2):
    mx = jnp.maximum(jnp.maximum(l0, l1), l2)
    e0, e1, e2 = jnp.exp(l0 - mx), jnp.exp(l1 - mx), jnp.exp(l2 - mx)
    inv = 1.0 / (e0 + e1 + e2)
    return e0 * inv, e1 * inv, e2 * inv


def _combine_fwd(os_, lses, tm=256):
    def body(o0, o1, o2, l0, l1, l2, y_ref):
        w0, w1, w2 = _group_weights(l0[...], l1[...], l2[...])
        y_ref[...] = (w0 * o0[...] + w1 * o1[...] + w2 * o2[...]).astype(BF16)

    sp = pl.BlockSpec((tm, GW), lambda i: (i, 0))
    return pl.pallas_call(
        body,
        name="attn_combine_fwd",
        grid=(S // tm,),
        in_specs=[sp] * 6,
        out_specs=sp,
        out_shape=jax.ShapeDtypeStruct((S, GW), BF16),
        compiler_params=_cp(("parallel",)),
    )(*os_, *lses)


def _combine_bwd(dy, os_, lses, seg, tm=256):
    def body(dy_ref, o0, o1, o2, l0, l1, l2, seg_ref, d0, d1, d2, c0, c1, c2):
        w0, w1, w2 = _group_weights(l0[...], l1[...], l2[...])
        dyv = dy_ref[...]
        y = w0 * o0[...] + w1 * o1[...] + w2 * o2[...]
        t = dyv * y
        t_hi = t.astype(BF16)
        r1 = t - t_hi.astype(F32)
        t_mid = r1.astype(BF16)
        t_lo = (r1 - t_mid.astype(F32)).astype(BF16)
        sg = seg_ref[...]
        e = _dot_nn(t_hi, sg) + _dot_nn(t_mid, sg) + _dot_nn(t_lo, sg)
        d0[...] = (w0 * dyv).astype(BF16)
        d1[...] = (w1 * dyv).astype(BF16)
        d2[...] = (w2 * dyv).astype(BF16)
        c0[...] = w0 * e
        c1[...] = w1 * e
        c2[...] = w2 * e

    sp = pl.BlockSpec((tm, GW), lambda i: (i, 0))
    return pl.pallas_call(
        body,
        name="attn_combine_bwd",
        grid=(S // tm,),
        in_specs=[sp] * 7 + [pl.BlockSpec((GW, GW), lambda i: (0, 0))],
        out_specs=[sp] * 6,
        out_shape=[jax.ShapeDtypeStruct((S, GW), BF16)] * 3 + [jax.ShapeDtypeStruct((S, GW), F32)] * 3,
        compiler_params=_cp(("parallel",)),
    )(dy, *os_, *lses, seg)


_SQRT_HALF = 0.7071067811865476
_INV_SQRT_2PI = 0.3989422804014327


def _gelu(z):
    return 0.5 * z * (1.0 + lax.erf(z * _SQRT_HALF))


def _gelu_grad(z):
    return 0.5 * (1.0 + lax.erf(z * _SQRT_HALF)) + z * (_INV_SQRT_2PI * jnp.exp(-0.5 * z * z))


def _tril_ws(ws_ref, g):
    t = lax.broadcasted_iota(jnp.int32, (BLK, BLK), 0)
    s = lax.broadcasted_iota(jnp.int32, (BLK, BLK), 1)
    return jnp.where(t >= s, ws_ref[g], 0.0)


def _gmlp_fwd(z, ws, bst, ln_g, ln_b, tm=256):
    nch = tm // BLK

    def body(z_ref, ws_ref, b_ref, g_ref, be_ref, y_ref):
        zg = _gelu(z_ref[...])
        u = zg[:, :D]
        vn, _, _ = _ln_fwd(zg[:, D:], g_ref[...], be_ref[...])
        vnb = vn.astype(BF16)
        bt = b_ref[...]
        for g in range(8):
            w = _tril_ws(ws_ref, g).astype(BF16)
            cols = slice(g * BLK, (g + 1) * BLK)
            for c in range(nch):
                rows = slice(c * BLK, (c + 1) * BLK)
                mixed = _dot_nn(w, vnb[rows, cols]) + bt[:, g:g + 1]
                y_ref[rows, cols] = (u[rows, cols] * mixed).astype(BF16)

    return pl.pallas_call(
        body,
        name="gmlp_fwd",
        grid=(S // tm,),
        in_specs=[
            pl.BlockSpec((tm, 2 * D), lambda i: (i, 0)),
            pl.BlockSpec((8, BLK, BLK), lambda i: (0, 0, 0)),
            pl.BlockSpec((BLK, 8), lambda i: (0, 0)),
            pl.BlockSpec((1, D), lambda i: (0, 0)),
            pl.BlockSpec((1, D), lambda i: (0, 0)),
        ],
        out_specs=pl.BlockSpec((tm, D), lambda i: (i, 0)),
        out_shape=jax.ShapeDtypeStruct((S, D), BF16),
        compiler_params=_cp(("parallel",), 48),
    )(z, ws, bst, ln_g, ln_b)


def _gmlp_bwd(z, dy, ws, bst, ln_g, ln_b, tm=256):
    nch = tm // BLK

    def body(z_ref, dy_ref, ws_ref, b_ref, g_ref, be_ref, dz_ref, dws_ref, dbs_ref, dg_ref, dbe_ref, dvn_scr, dm_acc):
        i = pl.program_id(0)
        zv = z_ref[...]
        zg = _gelu(zv)
        u = zg[:, :D]
        gam = g_ref[...]
        vn, xhat, rstd = _ln_fwd(zg[:, D:], gam, be_ref[...])
        vnb = vn.astype(BF16)
        dyv = dy_ref[...]
        dmix = dyv * u
        dmb = dmix.astype(BF16)
        bt = b_ref[...]
        tmask = lax.broadcasted_iota(jnp.int32, (BLK, BLK), 0) >= lax.broadcasted_iota(jnp.int32, (BLK, BLK), 1)
        dm_sum = dmix[0:BLK]
        for c in range(1, nch):
            dm_sum = dm_sum + dmix[c * BLK:(c + 1) * BLK]

        @pl.when(i == 0)
        def _():
            dm_acc[...] = dm_sum

        @pl.when(i > 0)
        def _():
            dm_acc[...] += dm_sum

        dus = []
        for g in range(8):
            w = _tril_ws(ws_ref, g).astype(BF16)
            cols = slice(g * BLK, (g + 1) * BLK)
            dw = None
            du_rows = []
            for c in range(nch):
                rows = slice(c * BLK, (c + 1) * BLK)
                mixed = _dot_nn(w, vnb[rows, cols]) + bt[:, g:g + 1]
                du_rows.append(dyv[rows, cols] * mixed)
                part = _dot_nt(dmb[rows, cols], vnb[rows, cols])
                dw = part if dw is None else dw + part
                dvn_scr[rows, cols] = _dot_tn(w, dmb[rows, cols])
            dus.append(jnp.concatenate(du_rows, axis=0))
            dw = jnp.where(tmask, dw, 0.0)

            @pl.when(i == 0)
            def _():
                dws_ref[g] = dw

            @pl.when(i > 0)
            def _():
                dws_ref[g] += dw

        dvn = dvn_scr[...]
        sg = _colsum(dvn * xhat)
        sb = _colsum(dvn)

        @pl.when(i == 0)
        def _():
            dg_ref[...] = sg
            dbe_ref[...] = sb

        @pl.when(i > 0)
        def _():
            dg_ref[...] += sg
            dbe_ref[...] += sb

        dvg = _ln_bwd(dvn, xhat, rstd, gam)
        gp = _gelu_grad(zv)
        dz_ref[:, :D] = (jnp.concatenate(dus, axis=1) * gp[:, :D]).astype(BF16)
        dz_ref[:, D:] = (dvg * gp[:, D:]).astype(BF16)

        @pl.when(i == S // tm - 1)
        def _():
            acc = dm_acc[...]
            for g in range(8):
                dbs_ref[:, g:g + 1] = jnp.sum(acc[:, g * BLK:(g + 1) * BLK], axis=1, keepdims=True)

    vec = pl.BlockSpec((1, D), lambda i: (0, 0))
    return pl.pallas_call(
        body,
        name="gmlp_bwd",
        grid=(S // tm,),
        in_specs=[
            pl.BlockSpec((tm, 2 * D), lambda i: (i, 0)),
            pl.BlockSpec((tm, D), lambda i: (i, 0)),
            pl.BlockSpec((8, BLK, BLK), lambda i: (0, 0, 0)),
            pl.BlockSpec((BLK, 8), lambda i: (0, 0)),
            vec,
            vec,
        ],
        out_specs=[
            pl.BlockSpec((tm, 2 * D), lambda i: (i, 0)),
            pl.BlockSpec((8, BLK, BLK), lambda i: (0, 0, 0)),
            pl.BlockSpec((BLK, 8), lambda i: (0, 0)),
            vec,
            vec,
        ],
        out_shape=[
            jax.ShapeDtypeStruct((S, 2 * D), BF16),
            jax.ShapeDtypeStruct((8, BLK, BLK), F32),
            jax.ShapeDtypeStruct((BLK, 8), F32),
            jax.ShapeDtypeStruct((1, D), F32),
            jax.ShapeDtypeStruct((1, D), F32),
        ],
        scratch_shapes=[pltpu.VMEM((tm, D), F32), pltpu.VMEM((BLK, D), F32)],
        compiler_params=_cp(("arbitrary",), 48),
    )(z, dy, ws, bst, ln_g, ln_b)


def _merge_fwd(ya, yg, glog, bgate, h1, wabt, wgb, wo, ln_g, ln_b, tm=256):
    def body(ya_ref, yg_ref, gl_ref, bg_ref, h1_ref, wab_ref, wgb_ref, wo_ref, g_ref, b_ref,
             h_ref, hb_ref, xh_ref, rs_ref, mg_ref, bra_ref, brg_ref):
        bra = _dot_nt(ya_ref[...], wab_ref[...])
        brg = _dot_nn(yg_ref[...], wgb_ref[...])
        gates = _sigmoid(gl_ref[...] + bg_ref[...])
        merged = (gates[:, :D] * bra + gates[:, D:] * brg).astype(BF16)
        mix = _dot_nn(merged, wo_ref[...])
        h, xhat, rstd = _ln_fwd(ALPHA * h1_ref[...] + mix, g_ref[...], b_ref[...])
        h_ref[...] = h
        hb_ref[...] = h.astype(BF16)
        xh_ref[...] = xhat
        rs_ref[...] = rstd
        mg_ref[...] = merged
        bra_ref[...] = bra
        brg_ref[...] = brg

    row = pl.BlockSpec((tm, D), lambda i: (i, 0))
    vec = pl.BlockSpec((1, D), lambda i: (0, 0))
    full = lambda shape: pl.BlockSpec(shape, lambda i: (0, 0))
    return pl.pallas_call(
        body,
        name="merge_fwd",
        grid=(S // tm,),
        in_specs=[
            pl.BlockSpec((tm, GW), lambda i: (i, 0)), row, pl.BlockSpec((tm, 2 * D), lambda i: (i, 0)), full((1, 2 * D)), row,
            full((D, GW)), full((D, D)), full((D, D)), vec, vec,
        ],
        out_specs=[row, row, row, pl.BlockSpec((tm, 1), lambda i: (i, 0)), row, row, row],
        out_shape=[
            jax.ShapeDtypeStruct((S, D), F32),
            jax.ShapeDtypeStruct((S, D), BF16),
            jax.ShapeDtypeStruct((S, D), F32),
            jax.ShapeDtypeStruct((S, 1), F32),
            jax.ShapeDtypeStruct((S, D), BF16),
            jax.ShapeDtypeStruct((S, D), F32),
            jax.ShapeDtypeStruct((S, D), F32),
        ],
        compiler_params=_cp(("parallel",), 48),
    )(ya, yg, glog, bgate, h1, wabt, wgb, wo, ln_g, ln_b)


def _merge_bwd(dh2, xhat, rstd, ln_g, bra, brg, glog, bgate, wabt, wgb, wo, tm=256):
    def body(dh_ref, xh_ref, rs_ref, g_ref, bra_ref, brg_ref, gl_ref, bg_ref, wab_ref, wgb_ref, wo_ref,
             dr_ref, drb_ref, dlog_ref, dba_ref, dbg_ref, dya_ref, dyg_ref, dbgate_ref, dg_ref, dbias_ref):
        i = pl.program_id(0)
        dh = dh_ref[...]
        xh = xh_ref[...]
        dr = _ln_bwd(dh, xh, rs_ref[...], g_ref[...])
        drb = dr.astype(BF16)
        dr_ref[...] = dr
        drb_ref[...] = drb
        dmerged = _dot_nt(drb, wo_ref[...])
        gates = _sigmoid(gl_ref[...] + bg_ref[...])
        g0, g1 = gates[:, :D], gates[:, D:]
        dl0 = dmerged * bra_ref[...] * g0 * (1.0 - g0)
        dl1 = dmerged * brg_ref[...] * g1 * (1.0 - g1)
        dlog_ref[:, :D] = dl0.astype(BF16)
        dlog_ref[:, D:] = dl1.astype(BF16)
        dba = (dmerged * g0).astype(BF16)
        dbg = (dmerged * g1).astype(BF16)
        dba_ref[...] = dba
        dbg_ref[...] = dbg
        dya_ref[...] = _dot_nn(dba, wab_ref[...])
        dyg_ref[...] = _dot_nt(dbg, wgb_ref[...])
        s0, s1 = _colsum(dl0), _colsum(dl1)
        sg, sb = _colsum(dh * xh), _colsum(dh)

        @pl.when(i == 0)
        def _():
            dbgate_ref[:, :D] = s0
            dbgate_ref[:, D:] = s1
            dg_ref[...] = sg
            dbias_ref[...] = sb

        @pl.when(i > 0)
        def _():
            dbgate_ref[:, :D] += s0
            dbgate_ref[:, D:] += s1
            dg_ref[...] += sg
            dbias_ref[...] += sb

    row = pl.BlockSpec((tm, D), lambda i: (i, 0))
    vec = pl.BlockSpec((1, D), lambda i: (0, 0))
    wide = pl.BlockSpec((tm, 2 * D), lambda i: (i, 0))
    full = lambda shape: pl.BlockSpec(shape, lambda i: (0, 0))
    return pl.pallas_call(
        body,
        name="merge_bwd",
        grid=(S // tm,),
        in_specs=[row, row, pl.BlockSpec((tm, 1), lambda i: (i, 0)), vec, row, row, wide, full((1, 2 * D)),
                  full((D, GW)), full((D, D)), full((D, D))],
        out_specs=[row, row, wide, row, row, pl.BlockSpec((tm, GW), lambda i: (i, 0)), row, full((1, 2 * D)), vec, vec],
        out_shape=[
            jax.ShapeDtypeStruct((S, D), F32),
            jax.ShapeDtypeStruct((S, D), BF16),
            jax.ShapeDtypeStruct((S, 2 * D), BF16),
            jax.ShapeDtypeStruct((S, D), BF16),
            jax.ShapeDtypeStruct((S, D), BF16),
            jax.ShapeDtypeStruct((S, GW), F32),
            jax.ShapeDtypeStruct((S, D), F32),
            jax.ShapeDtypeStruct((1, 2 * D), F32),
            jax.ShapeDtypeStruct((1, D), F32),
            jax.ShapeDtypeStruct((1, D), F32),
        ],
        compiler_params=_cp(("arbitrary",), 48),
    )(dh2, xhat, rstd, ln_g, bra, brg, glog, bgate, wabt, wgb, wo)


def _loss_head(h3, target, tm=512):
    def body(h_ref, t_ref, d_ref, l_ref):
        i = pl.program_id(0)
        e = h_ref[...] - t_ref[...]
        d_ref[...] = e * (1.0 / D)
        part = jnp.sum(_colsum(e * e), axis=1, keepdims=True) * (0.5 / D)

        @pl.when(i == 0)
        def _():
            l_ref[...] = part

        @pl.when(i > 0)
        def _():
            l_ref[...] += part

    row = pl.BlockSpec((tm, D), lambda i: (i, 0))
    return pl.pallas_call(
        body,
        name="loss_head",
        grid=(S // tm,),
        in_specs=[row, row],
        out_specs=[row, pl.BlockSpec((1, 1), lambda i: (0, 0))],
        out_shape=[jax.ShapeDtypeStruct((S, D), F32), jax.ShapeDtypeStruct((1, 1), F32)],
        compiler_params=_cp(("arbitrary",)),
    )(h3, target)


def _tie(x, dep):
    if dep is None:
        return x
    return x + dep[0, 0].astype(x.dtype)


def _local_step(x, pos_col, target, get_w, p, emit):
    w = get_w("ffn1", None)
    h1, h1b, xh1, rs1, a1, b1 = _ffn_fwd(x, w["g1"], w["u1"], w["d1"], p["ln1_g"], p["ln1_b"], "ffn1_fwd")

    w.update(get_w("win", h1b))
    qkv = _matmul(h1b, w["win"], "nt", F32, 512, 512, D, "proj_qkv", b_off=0, n_out=QKV_W)
    z = _matmul(h1b, w["win"], "nt", F32, 512, 512, D, "proj_z", b_off=QKV_W // 512, n_out=2 * D)
    glog = _matmul(h1b, w["win"], "nt", F32, 512, 512, D, "proj_g", b_off=(QKV_W + 2 * D) // 512, n_out=2 * D)

    half = jnp.arange(0, HEAD_DIM, 2, dtype=F32) / HEAD_DIM
    inv_freq = ROPE_THETA ** (-half)
    invf = jnp.tile(inv_freq, 4).reshape(1, BLK)
    sign = jnp.tile(jnp.concatenate([-jnp.ones((32,), F32), jnp.ones((32,), F32)]), 2).reshape(1, BLK)
    cos_t, sin_t = _rope_tables(pos_col, invf, sign)

    os_, lses = [], []
    for gi in range(3):
        o, lse = _attn_fwd(qkv, cos_t, sin_t, gi)
        os_.append(o)
        lses.append(lse)
    ya = _combine_fwd(os_, lses)
    bst = p["gmlp_b_s"].T
    yg = _gmlp_fwd(z, p["gmlp_w_s"], bst, p["gmlp_ln_g"], p["gmlp_ln_b"])
    w.update(get_w("mix", yg))
    h2, h2b, xh2, rs2, merged, bra, brg = _merge_fwd(ya, yg, glog, p["b_gates"], h1, w["ab"], w["gb"], w["o"],
                                                      p["ln2_g"], p["ln2_b"])
    w.update(get_w("ffn2", h2b))
    h3, _, xh3, rs3, a2, b2 = _ffn_fwd(h2, w["g2"], w["u2"], w["d2"], p["ln3_g"], p["ln3_b"], "ffn2_fwd")
    dh3, loss = _loss_head(h3, target)

    gp = {}
    dh2, da2, db2, hm2, df2, gp["ln3_g"], gp["ln3_b"] = _ffn_bwd(dh3, None, xh3, rs3, p["ln3_g"], a2, b2,
                                                                 w["g2"], w["u2"], w["d2"], "ffn2_bwd")
    tok = emit("ffn2", {
        "g2": _matmul(da2, h2b, "tn", BF16, 256, D, S, "wgrad_g2"),
        "u2": _matmul(db2, h2b, "tn", BF16, 256, D, S, "wgrad_u2"),
        "d2": _matmul(hm2, df2, "tn", BF16, 256, D, S, "wgrad_d2")})

    (dr2, dr2b, dlog, dba, dbg, dya, dyg, gp["b_gates"], gp["ln2_g"], gp["ln2_b"]) = _merge_bwd(
        dh2, xh2, rs2, _tie(p["ln2_g"], tok), bra, brg, glog, p["b_gates"], w["ab"], w["gb"], w["o"])
    tok = emit("mix", {
        "o": _matmul(merged, dr2b, "tn", BF16, 256, D, S, "wgrad_o"),
        "ab": _matmul(dba, ya, "tn", BF16, 256, GW, S, "wgrad_ab"),
        "gb": _matmul(yg, dbg, "tn", BF16, 256, D, S, "wgrad_gb")})

    seg = (jnp.arange(GW)[:, None] // HEAD_DIM == jnp.arange(GW)[None, :] // HEAD_DIM).astype(BF16)
    do0, do1, do2, c0, c1, c2 = _combine_bwd(dya, os_, lses, _tie(seg, tok))
    dqkv = []
    for gi, (do, ct) in enumerate(((do0, c0), (do1, c1), (do2, c2))):
        dqkv.append(_attn_bwd(qkv, cos_t, sin_t, do, lses[gi], ct, gi))
    dz, gp["gmlp_w_s"], dbst, gp["gmlp_ln_g"], gp["gmlp_ln_b"] = _gmlp_bwd(
        z, dyg, p["gmlp_w_s"], bst, p["gmlp_ln_g"], p["gmlp_ln_b"])
    gp["gmlp_b_s"] = dbst.T
    dproj = jnp.concatenate(
        [dqkv[gi][0] for gi in range(3)] + [dqkv[gi][1].astype(BF16) for gi in range(3)]
        + [dqkv[gi][2].astype(BF16) for gi in range(3)] + [dz, dlog], axis=1)
    tok = emit("win", {"win": _matmul(dproj, h1b, "tn", BF16, 256, D, S, "wgrad_win")})
    dh1m = _matmul(dproj, w["win"], "nn", F32, 512, D, 512, "dproj_to_dh1", dep=tok)

    dx, da1, db1, hm1, df1, gp["ln1_g"], gp["ln1_b"] = _ffn_bwd(dr2, dh1m, xh1, rs1, p["ln1_g"], a1, b1,
                                                                w["g1"], w["u1"], w["d1"], "ffn1_bwd")
    tok = emit("small", gp)
    tok = emit("g1", {"g1": _matmul(da1, x, "tn", BF16, 256, D, S, "wgrad_g1", dep=tok)})
    tok = emit("u1", {"u1": _matmul(db1, x, "tn", BF16, 256, D, S, "wgrad_u1", dep=tok)})
    emit("d1", {"d1": _matmul(hm1, df1, "tn", BF16, 256, D, S, "wgrad_d1", dep=tok)})
    return loss, dx, gp


_FLIPS = [(mx, my, mc) for mx in (0, 1) for my in (0, 1) for mc in (0, 1)][1:]
_GROUPS = {"ffn1": ("g1", "u1", "d1"), "win": ("win",), "mix": ("ab", "gb", "o"), "ffn2": ("g2", "u2", "d2")}
_SCATTERS = {"ffn2": ("g2", "u2", "d2"), "mix": ("ab", "gb", "o"), "win": ("win",), "g1": ("g1",), "u1": ("u1",), "d1": ("d1",)}
_HBM = pl.BlockSpec(memory_space=pltpu.HBM)
_SEM = pl.BlockSpec(memory_space=pltpu.SEMAPHORE)
_EFFECT = pltpu.SideEffectType.DATAFLOW_SIDE_EFFECTING


def _me():
    return 4 * lax.axis_index("x") + 2 * lax.axis_index("y") + lax.axis_index("c")


def _copies_start(bufs, copies, n_sem, name, after=None):
    nb = len(bufs)
    n_after = 0 if after is None else 1

    def body(*refs):
        b = refs[:nb]
        send_sems, recv_sems = refs[nb + n_after], refs[nb + n_after + 1]
        token = refs[-1]
        x, y, c = lax.axis_index("x"), lax.axis_index("y"), lax.axis_index("c")
        me = 4 * x + 2 * y + c
        for si, s_slot, di, d_slot, (mx, my, mc), sem in copies:
            s_idx, d_idx = s_slot(me), d_slot(me)
            pltpu.make_async_remote_copy(
                src_ref=b[si] if s_idx is None else b[si].at[s_idx],
                dst_ref=b[di] if d_idx is None else b[di].at[d_idx],
                send_sem=send_sems.at[sem], recv_sem=recv_sems.at[sem],
                device_id=(x ^ mx, y ^ my, c ^ mc), device_id_type=MESH).start()
        token[...] = jnp.zeros_like(token)

    ins = [pltpu.with_memory_space_constraint(a, pltpu.HBM) for a in bufs]
    outs = pl.pallas_call(
        body,
        name=name,
        in_specs=[_HBM] * nb + [pl.BlockSpec(memory_space=pl.ANY)] * n_after,
        out_specs=[_SEM, _SEM] + [_HBM] * nb + [pl.BlockSpec(memory_space=pltpu.VMEM)],
        out_shape=[pltpu.SemaphoreType.DMA((n_sem,)), pltpu.SemaphoreType.DMA((n_sem,))]
        + [pltpu.HBM(a.shape, a.dtype) for a in ins] + [jax.ShapeDtypeStruct((8, 128), F32)],
        input_output_aliases={k: 2 + k for k in range(nb)},
        compiler_params=pltpu.CompilerParams(has_side_effects=_EFFECT),
    )(*ins, *([] if after is None else [after]))
    return outs[0], outs[1], list(outs[2:2 + nb]), outs[-1]


def _copies_wait(started, waits, after, name):
    send_sems, recv_sems, bufs, _ = started
    nb = len(bufs)

    def body(*refs):
        b = refs[:nb]
        ss, rs = refs[nb], refs[nb + 1]
        me3 = (lax.axis_index("x"), lax.axis_index("y"), lax.axis_index("c"))
        for bi, n_blocks, sem, is_send in waits:
            blocks = b[bi].at[pl.ds(0, n_blocks)]
            cp = pltpu.make_async_remote_copy(src_ref=blocks, dst_ref=blocks, send_sem=ss.at[sem], recv_sem=rs.at[sem],
                                              device_id=me3, device_id_type=MESH)
            if is_send:
                cp.wait_send()
            else:
                cp.wait_recv()

    return pl.pallas_call(
        body,
        name=name,
        in_specs=[_HBM] * nb + [_SEM, _SEM, pl.BlockSpec(memory_space=pl.ANY)],
        out_specs=[_HBM] * nb,
        out_shape=[pltpu.HBM(a.shape, a.dtype) for a in bufs],
        input_output_aliases={k: k for k in range(nb)},
        compiler_params=pltpu.CompilerParams(has_side_effects=_EFFECT),
    )(*bufs, send_sems, recv_sems, after)


def _landing(own, me):
    return lax.dynamic_update_slice(lax.empty((N_DEV,) + own.shape[1:], own.dtype), own, (me, 0, 0))


_SIBLING = (0, 0, 1)
_OTHER_CHIPS = ((1, 0, 0), (0, 1, 0), (1, 1, 0))


def _bits(flip):
    return 4 * flip[0] + 2 * flip[1] + flip[2]


def _gather_send(shards, grp, after):
    nw = len(shards)
    me = _me()
    bufs = list(shards) + [_landing(a[None], me) for a in shards]
    copies = []
    for k in range(nw):
        copies.append((k, lambda me: None, nw + k, lambda me: me, _SIBLING, nw + k))
        copies += [(k, lambda me: None, nw + k, lambda me: me, f, k) for f in _OTHER_CHIPS]
    return _copies_start(bufs, copies, 2 * nw, "gather_send_" + grp, after)


def _gather_finish(started, grp, after):
    nw = len(started[2]) // 2
    waits = []
    for k in range(nw):
        waits += [(nw + k, 3, k, True), (nw + k, 1, nw + k, True), (nw + k, 3, k, False), (nw + k, 1, nw + k, False)]
    lands = list(_copies_wait(started, waits, after, "gather_arrived_" + grp)[nw:])
    copies = []
    for k in range(nw):
        for f in _OTHER_CHIPS:
            slot = functools.partial(lambda me, bits: me ^ bits, bits=_bits(f))
            copies.append((k, slot, k, slot, _SIBLING, k))
    passed = _copies_start(lands, copies, nw, "gather_pass_" + grp)
    waits = [(k, 3, k, is_send) for k in range(nw) for is_send in (True, False)]
    return _copies_wait(passed, waits, passed[3], "gather_done_" + grp)


def _scatter_send(parts, grp, after=None):
    nw = len(parts)
    me = _me()
    bufs = list(parts) + [_landing(lax.dynamic_index_in_dim(a, me, 0, keepdims=True), me) for a in parts]
    copies = []
    for k in range(nw):
        for f in _FLIPS:
            to = functools.partial(lambda me, bits: me ^ bits, bits=_bits(f))
            copies.append((k, to, nw + k, lambda me: me, f, k))
    return _copies_start(bufs, copies, nw, "scatter_send_" + grp, after)


def _scatter_finish(started, grp, after):
    nw = len(started[2]) // 2
    waits = [(nw + k, N_DEV - 1, k, is_send) for k in range(nw) for is_send in (True, False)]
    return _copies_wait(started, waits, after, "scatter_done_" + grp)[nw:]


def _allgather_send(block, name, after=None):
    me = _me()
    copies = [(0, lambda me: None, 1, lambda me: me, f, 0) for f in _FLIPS]
    return _copies_start([block, _landing(block[None], me)], copies, 1, name, after)


def _allgather_finish(started, name, after):
    waits = [(1, N_DEV - 1, 0, True), (1, N_DEV - 1, 0, False)]
    return _copies_wait(started, waits, after, name)[1]


_ADAM_ROWS = {352: 176, 1088: 272}


def _sum_adamw(parts, wv, m, v, name):
    _, r, c = parts.shape
    tm = _ADAM_ROWS.get(r, r)
    assert r % tm == 0 and wv.shape == (r, c)

    def body(p_ref, w_ref, m_ref, v_ref, g_ref, d_ref, mo_ref, vo_ref):
        gv = p_ref[0].astype(F32)
        for j in range(1, N_DEV):
            gv = gv + p_ref[j].astype(F32)
        g_ref[...] = gv
        mn = ADAM_B1 * m_ref[...] + (1.0 - ADAM_B1) * gv
        vn = ADAM_B2 * v_ref[...] + (1.0 - ADAM_B2) * (gv * gv)
        m_hat = mn / (1.0 - ADAM_B1 ** ADAM_STEP)
        v_hat = vn / (1.0 - ADAM_B2 ** ADAM_STEP)
        d_ref[...] = -ADAM_LR * (m_hat / (jnp.sqrt(v_hat) + ADAM_EPS) + ADAM_WD * w_ref[...])
        mo_ref[...] = mn
        vo_ref[...] = vn

    sp = pl.BlockSpec((tm, c), lambda i: (i, 0))
    return pl.pallas_call(
        body,
        name=name,
        grid=(r // tm,),
        in_specs=[pl.BlockSpec((N_DEV, tm, c), lambda i: (0, i, 0))] + [sp] * 3,
        out_specs=[sp] * 4,
        out_shape=[jax.ShapeDtypeStruct((r, c), F32)] * 4,
        compiler_params=_cp(("parallel",), 48),
    )(parts, wv, m, v)


_WEIGHTS = ["ffn1_w_gate", "ffn1_w_up", "ffn1_w_down", "ln1_g", "ln1_b", "w_in", "b_gates", "gmlp_ln_g", "gmlp_ln_b",
            "gmlp_w_s", "gmlp_b_s", "w_attn_branch", "w_gmlp_branch", "w_out", "ln2_g", "ln2_b", "ffn2_w_gate",
            "ffn2_w_up", "ffn2_w_down", "ln3_g", "ln3_b"]
_BIG_OF = {"ffn1_w_gate": ("g1", True), "ffn1_w_up": ("u1", True), "ffn1_w_down": ("d1", False), "w_in": ("win", True),
           "w_attn_branch": ("ab", True), "w_gmlp_branch": ("gb", False), "w_out": ("o", False),
           "ffn2_w_gate": ("g2", True), "ffn2_w_up": ("u2", True), "ffn2_w_down": ("d2", False)}
_SMALL = [n for n in _WEIGHTS if n not in _BIG_OF]
_SMALL_ROWS = {"gmlp_w_s": 128, "b_gates": 2}
_SMALL_SLOT = 8


def _pack_small(d):
    rows = []
    for n in _SMALL:
        r = d[n].reshape(-1, D)
        slot = max(r.shape[0], _SMALL_SLOT)
        rows.append(jnp.pad(r, ((0, slot - r.shape[0]), (0, 0))))
    return jnp.concatenate(rows, axis=0)


def _unpack_small(packed, shapes):
    out, at = {}, 0
    for n in _SMALL:
        k = _SMALL_ROWS.get(n, 1)
        out[n] = packed[at:at + k].reshape(shapes[n])
        at += max(k, _SMALL_SLOT)
    return out


def kernel(x, positions, ffn1_w_gate, ffn1_w_up, ffn1_w_down, ln1_g, ln1_b, w_in, b_gates, gmlp_ln_g, gmlp_ln_b, gmlp_w_s, gmlp_b_s, w_attn_branch, w_gmlp_branch, w_out, ln2_g, ln2_b, ffn2_w_gate, ffn2_w_up, ffn2_w_down, ln3_g, ln3_b, loss_target, m_ffn1_w_gate, m_ffn1_w_up, m_ffn1_w_down, m_ln1_g, m_ln1_b, m_w_in, m_b_gates, m_gmlp_ln_g, m_gmlp_ln_b, m_gmlp_w_s, m_gmlp_b_s, m_w_attn_branch, m_w_gmlp_branch, m_w_out, m_ln2_g, m_ln2_b, m_ffn2_w_gate, m_ffn2_w_up, m_ffn2_w_down, m_ln3_g, m_ln3_b, v_ffn1_w_gate, v_ffn1_w_up, v_ffn1_w_down, v_ln1_g, v_ln1_b, v_w_in, v_b_gates, v_gmlp_ln_g, v_gmlp_ln_b, v_gmlp_w_s, v_gmlp_b_s, v_w_attn_branch, v_w_gmlp_branch, v_w_out, v_ln2_g, v_ln2_b, v_ffn2_w_gate, v_ffn2_w_up, v_ffn2_w_down, v_ln3_g, v_ln3_b):
    args = dict(locals())
    wts = {n: args[n] for n in _WEIGHTS}
    ms = {n: args["m_" + n] for n in _WEIGHTS}
    vs = {n: args["v_" + n] for n in _WEIGHTS}

    name_of = {key: (n, tr) for n, (key, tr) in _BIG_OF.items()}

    started, tok = {}, None
    for grp, keys in _GROUPS.items():
        shards = []
        for key in keys:
            n, tr = name_of[key]
            s2 = wts[n][0]
            shards.append((s2.T if tr else s2).astype(BF16))
        started[grp] = _gather_send(shards, grp, tok)
        tok = started[grp][3]
    all_started = tok

    def get_w(grp, after):
        lands = _gather_finish(started[grp], grp, all_started if after is None else after)
        return {key: g.reshape(-1, g.shape[-1]) for key, g in zip(_GROUPS[grp], lands)}

    sent = {}

    def emit(grp, grads):
        if grp == "small":
            sent[grp] = _allgather_send(_pack_small(grads), "small_grads_send")
        else:
            parts = [grads[key].reshape(N_DEV, -1, grads[key].shape[-1]) for key in _SCATTERS[grp]]
            sent[grp] = _scatter_send(parts, grp)
        return sent[grp][3]

    p = {n: (wts[n][0] if n in ("gmlp_w_s", "gmlp_b_s") else wts[n]) for n in _SMALL}
    loss, dx, gp = _local_step(x[0], positions.reshape(S, 1), loss_target[0], get_w, p, emit)
    loss = lax.psum(loss[0, 0], ("x", "y", "c"))

    grads, deltas, new_m, new_v = {}, {}, {}, {}
    after = sent["d1"][3]
    for grp in ("ffn2", "mix", "win", "small", "g1", "u1", "d1"):
        if grp == "small":
            parts = _allgather_finish(sent[grp], "small_grads_done", after)
            outs = _sum_adamw(parts, *[_pack_small({n: d[n] for n in _SMALL}) for d in (wts, ms, vs)], "update_small")
            shapes = {n: wts[n].shape for n in _SMALL}
            for dst, packed in zip((grads, deltas, new_m, new_v), outs):
                dst.update(_unpack_small(packed, shapes))
            after = outs[1]
            continue
        for key, part in zip(_SCATTERS[grp], _scatter_finish(sent[grp], grp, after)):
            n, tr = name_of[key]
            outs = _sum_adamw(part, *[(d[n][0].T if tr else d[n][0]) for d in (wts, ms, vs)], "update_" + key)
            for dst, o in zip((grads, deltas, new_m, new_v), outs):
                dst[n] = (o.T if tr else o)[None]
            after = outs[1]

    return (loss, dx[None], *[grads[n] for n in _WEIGHTS], *[deltas[n] for n in _WEIGHTS],
            *[new_m[n] for n in _WEIGHTS], *[new_v[n] for n in _WEIGHTS])
```

```python
import functools
import math

import jax
import jax.numpy as jnp
from jax import lax
from jax.experimental import pallas as pl
from jax.experimental.pallas import tpu as pltpu

F32 = jnp.float32
BF16 = jnp.bfloat16

N_DEV = 8
D = 1024
S = 2048
F = 2816
HEAD_DIM = 64
HEADS = 8
GW = HEADS * HEAD_DIM
PATTERN_DILATIONS = (1, 4, 16)
BLK = 128
QKV_W = 3 * 3 * GW
IN_W = QKV_W + 2 * D + 2 * D
ROPE_THETA = 10000.0
ALPHA = 2.0 ** 0.25
LN_EPS = 1e-5
ADAM_LR, ADAM_B1, ADAM_B2, ADAM_EPS, ADAM_WD, ADAM_STEP = 0.001, 0.9, 0.999, 1e-08, 0.01, 10
NEG = -1e30
MESH = pl.DeviceIdType.MESH


def _cp(sem=None, vmem_mb=None):
    kw = {}
    if sem is not None:
        kw["dimension_semantics"] = sem
    if vmem_mb is not None:
        kw["vmem_limit_bytes"] = vmem_mb << 20
    return pltpu.CompilerParams(**kw)


def _dot_nn(a, b):
    return lax.dot_general(a, b, (((1,), (0,)), ((), ())), preferred_element_type=F32)


def _dot_nt(a, b):
    return lax.dot_general(a, b, (((1,), (1,)), ((), ())), preferred_element_type=F32)


def _dot_tn(a, b):
    return lax.dot_general(a, b, (((0,), (0,)), ((), ())), preferred_element_type=F32)


def _ln_fwd(r, g, b):
    mu = jnp.mean(r, axis=-1, keepdims=True)
    xc = r - mu
    var = jnp.mean(xc * xc, axis=-1, keepdims=True)
    rstd = lax.rsqrt(var + LN_EPS)
    xhat = xc * rstd
    return xhat * g + b, xhat, rstd


def _ln_bwd(dh, xhat, rstd, g):
    dxh = dh * g
    m1 = jnp.mean(dxh, axis=-1, keepdims=True)
    m2 = jnp.mean(dxh * xhat, axis=-1, keepdims=True)
    return rstd * (dxh - m1 - xhat * m2)


def _sigmoid(x):
    return 1.0 / (1.0 + jnp.exp(-x))


def _colsum(x):
    return jnp.sum(x, axis=0, keepdims=True)


def _matmul(a, b, mode, out_dtype, tm, tn, tk, name, b_off=0, n_out=None, dep=None):
    n_dep = 0 if dep is None else 1
    if mode == "nn":
        m, k = a.shape
        n = b.shape[1]
    elif mode == "nt":
        m, k = a.shape
        n = n_out if n_out is not None else b.shape[0]
    else:
        k, m = a.shape
        n = b.shape[1]
    nk = k // tk
    assert m % tm == 0 and n % tn == 0 and k % tk == 0
    dot = {"nn": _dot_nn, "nt": _dot_nt, "tn": _dot_tn}[mode]

    def body(a_ref, b_ref, *rest):
        o_ref, scr = rest[n_dep], rest[n_dep + 1:]
        r = dot(a_ref[...].astype(BF16), b_ref[...].astype(BF16))
        if nk == 1:
            o_ref[...] = r.astype(out_dtype)
        else:
            acc = scr[0]
            kk = pl.program_id(2)

            @pl.when(kk == 0)
            def _():
                acc[...] = r

            @pl.when(kk > 0)
            def _():
                acc[...] += r

            @pl.when(kk == nk - 1)
            def _():
                o_ref[...] = acc[...].astype(out_dtype)

    if mode == "nn":
        a_spec = pl.BlockSpec((tm, tk), lambda i, j, kk: (i, kk))
        b_spec = pl.BlockSpec((tk, tn), lambda i, j, kk: (kk, j))
    elif mode == "nt":
        a_spec = pl.BlockSpec((tm, tk), lambda i, j, kk: (i, kk))
        b_spec = pl.BlockSpec((tn, tk), lambda i, j, kk: (j + b_off, kk))
    else:
        a_spec = pl.BlockSpec((tk, tm), lambda i, j, kk: (kk, i))
        b_spec = pl.BlockSpec((tk, tn), lambda i, j, kk: (kk, j))
    return pl.pallas_call(
        body,
        name=name,
        grid=(m // tm, n // tn, nk),
        in_specs=[a_spec, b_spec] + [pl.BlockSpec(memory_space=pl.ANY)] * n_dep,
        out_specs=pl.BlockSpec((tm, tn), lambda i, j, kk: (i, j)),
        out_shape=jax.ShapeDtypeStruct((m, n), out_dtype),
        scratch_shapes=[] if nk == 1 else [pltpu.VMEM((tm, tn), F32)],
        compiler_params=_cp(("parallel", "parallel", "arbitrary"), 56),
    )(a, b, *([] if dep is None else [dep]))


def _ffn_up(x, wgt, wut, name, tm=512, tn=1408):
    def body(x_ref, wg_ref, wu_ref, a_ref, b_ref, hm_ref):
        xb = x_ref[...].astype(BF16)
        a = _dot_nt(xb, wg_ref[...])
        b = _dot_nt(xb, wu_ref[...])
        a_ref[...] = a.astype(BF16)
        b_ref[...] = b.astype(BF16)
        hm_ref[...] = ((a * _sigmoid(a)) * b).astype(BF16)

    wsp = pl.BlockSpec((tn, D), lambda i, j: (j, 0))
    mid = pl.BlockSpec((tm, tn), lambda i, j: (i, j))
    return pl.pallas_call(
        body,
        name=name,
        grid=(S // tm, F // tn),
        in_specs=[pl.BlockSpec((tm, D), lambda i, j: (i, 0)), wsp, wsp],
        out_specs=[mid, mid, mid],
        out_shape=[jax.ShapeDtypeStruct((S, F), BF16)] * 3,
        compiler_params=_cp(("parallel", "arbitrary"), 56),
    )(x, wgt, wut)


def _ffn_down(hm, wd, x, ln_g, ln_b, name, tm=256):
    def body(hm_ref, wd_ref, x_ref, g_ref, b_ref, h_ref, hb_ref, xh_ref, rs_ref):
        r = ALPHA * x_ref[...] + 0.5 * _dot_nn(hm_ref[...], wd_ref[...])
        h, xhat, rstd = _ln_fwd(r, g_ref[...], b_ref[...])
        h_ref[...] = h
        hb_ref[...] = h.astype(BF16)
        xh_ref[...] = xhat
        rs_ref[...] = rstd

    row = pl.BlockSpec((tm, D), lambda i: (i, 0))
    vec = pl.BlockSpec((1, D), lambda i: (0, 0))
    return pl.pallas_call(
        body,
        name=name,
        grid=(S // tm,),
        in_specs=[pl.BlockSpec((tm, F), lambda i: (i, 0)), pl.BlockSpec((F, D), lambda i: (0, 0)), row, vec, vec],
        out_specs=[row, row, row, pl.BlockSpec((tm, 1), lambda i: (i, 0))],
        out_shape=[
            jax.ShapeDtypeStruct((S, D), F32),
            jax.ShapeDtypeStruct((S, D), BF16),
            jax.ShapeDtypeStruct((S, D), F32),
            jax.ShapeDtypeStruct((S, 1), F32),
        ],
        compiler_params=_cp(("parallel",), 56),
    )(hm, wd, x, ln_g, ln_b)


def _ffn_bwd_mid(dh_a, dh_b, xhat, rstd, ln_g, a, b, wd, name, tm=512, tn=1408):
    two = dh_b is not None

    def body(*refs):
        dha_ref = refs[0]
        dhb_ref = refs[1] if two else None
        (xh_ref, rs_ref, g_ref, a_ref, b_ref, wd_ref,
         dr_ref, df_ref, da_ref, db_ref, dg_ref, dbias_ref, df_scr) = refs[2 if two else 1:]
        i = pl.program_id(0)
        j = pl.program_id(1)

        @pl.when(j == 0)
        def _():
            dh = dha_ref[...]
            if two:
                dh = ALPHA * dh + dhb_ref[...]
            xhat = xh_ref[...]
            dr = _ln_bwd(dh, xhat, rs_ref[...], g_ref[...])
            dfb = (0.5 * dr).astype(BF16)
            dr_ref[...] = dr
            df_scr[...] = dfb
            df_ref[...] = dfb
            sg = _colsum(dh * xhat)
            sb = _colsum(dh)

            @pl.when(i == 0)
            def _():
                dg_ref[...] = sg
                dbias_ref[...] = sb

            @pl.when(i > 0)
            def _():
                dg_ref[...] += sg
                dbias_ref[...] += sb

        dhm = _dot_nt(df_scr[...], wd_ref[...])
        av = a_ref[...].astype(F32)
        bv = b_ref[...].astype(F32)
        sig = _sigmoid(av)
        da_ref[...] = (dhm * bv * (sig * (1.0 + av * (1.0 - sig)))).astype(BF16)
        db_ref[...] = (dhm * (av * sig)).astype(BF16)

    row = pl.BlockSpec((tm, D), lambda i, j: (i, 0))
    vec = pl.BlockSpec((1, D), lambda i, j: (0, 0))
    mid = pl.BlockSpec((tm, tn), lambda i, j: (i, j))
    ins = [dh_a] + ([dh_b] if two else []) + [xhat, rstd, ln_g, a, b, wd]
    in_specs = [row] * (2 if two else 1) + [row, pl.BlockSpec((tm, 1), lambda i, j: (i, 0)), vec, mid, mid,
                                            pl.BlockSpec((tn, D), lambda i, j: (j, 0))]
    return pl.pallas_call(
        body,
        name=name,
        grid=(S // tm, F // tn),
        in_specs=in_specs,
        out_specs=[row, row, mid, mid, vec, vec],
        out_shape=[
            jax.ShapeDtypeStruct((S, D), F32),
            jax.ShapeDtypeStruct((S, D), BF16),
            jax.ShapeDtypeStruct((S, F), BF16),
            jax.ShapeDtypeStruct((S, F), BF16),
            jax.ShapeDtypeStruct((1, D), F32),
            jax.ShapeDtypeStruct((1, D), F32),
        ],
        scratch_shapes=[pltpu.VMEM((tm, D), BF16)],
        compiler_params=_cp(("arbitrary", "arbitrary"), 56),
    )(*ins)


def _ffn_bwd_dx(dr, da, db, wgt, wut, name, tm=512, tk=1408):
    nk = F // tk

    def body(dr_ref, da_ref, db_ref, wg_ref, wu_ref, dx_ref, acc):
        kk = pl.program_id(1)
        part = _dot_nn(da_ref[...], wg_ref[...]) + _dot_nn(db_ref[...], wu_ref[...])

        @pl.when(kk == 0)
        def _():
            acc[...] = ALPHA * dr_ref[...] + part

        @pl.when(kk > 0)
        def _():
            acc[...] += part

        @pl.when(kk == nk - 1)
        def _():
            dx_ref[...] = acc[...]

    row = pl.BlockSpec((tm, D), lambda i, kk: (i, 0))
    mid = pl.BlockSpec((tm, tk), lambda i, kk: (i, kk))
    wsp = pl.BlockSpec((tk, D), lambda i, kk: (kk, 0))
    return pl.pallas_call(
        body,
        name=name,
        grid=(S // tm, nk),
        in_specs=[row, mid, mid, wsp, wsp],
        out_specs=row,
        out_shape=jax.ShapeDtypeStruct((S, D), F32),
        scratch_shapes=[pltpu.VMEM((tm, D), F32)],
        compiler_params=_cp(("parallel", "arbitrary"), 56),
    )(dr, da, db, wgt, wut)


def _rope_tables(pos_col, invf, sign, tm=512):
    def body(p_ref, f_ref, s_ref, c_out, s_out):
        ang = p_ref[...].astype(F32) * f_ref[...]
        c_out[...] = jnp.cos(ang)
        s_out[...] = jnp.sin(ang) * s_ref[...]

    vec = pl.BlockSpec((1, BLK), lambda i: (0, 0))
    out = pl.BlockSpec((tm, BLK), lambda i: (i, 0))
    return pl.pallas_call(
        body,
        name="rope_tables",
        grid=(S // tm,),
        in_specs=[pl.BlockSpec((tm, 1), lambda i: (i, 0)), vec, vec],
        out_specs=[out, out],
        out_shape=[jax.ShapeDtypeStruct((S, BLK), F32)] * 2,
        compiler_params=_cp(("parallel",)),
    )(pos_col, invf, sign)


def _lane_lo():
    return (lax.broadcasted_iota(jnp.int32, (BLK, GW), 1) % HEAD_DIM) < (HEAD_DIM // 2)


def _swap_halves(t, lo):
    return jnp.where(lo, pltpu.roll(t, GW - HEAD_DIM // 2, 1), pltpu.roll(t, HEAD_DIM // 2, 1))


def _rope(t, cosf, sinf, lo):
    return t * cosf + _swap_halves(t, lo) * sinf


def _unrope(g, cosf, sinf, lo):
    return g * cosf + _swap_halves(g * sinf, lo)


def _tile4(v):
    return jnp.concatenate([v, v, v, v], axis=1)


def _band_mask(n):
    qi = lax.broadcasted_iota(jnp.int32, (BLK, 2 * BLK), 0)
    kj = lax.broadcasted_iota(jnp.int32, (BLK, 2 * BLK), 1)
    dist = qi + BLK - kj
    return (dist >= 0) & (dist <= BLK) & ((kj >= BLK) | (n >= 1))


def _attn_specs(gi, d):
    nq = QKV_W // GW

    def cur(off):
        return pl.BlockSpec((BLK, GW), lambda r, n: (n, r * nq + off + gi))

    def prev(off):
        return pl.BlockSpec((BLK, GW), lambda r, n: (jnp.maximum(n - 1, 0), r * nq + off + gi))

    tab_c = pl.BlockSpec((BLK, BLK), lambda r, n: (n, r))
    tab_p = pl.BlockSpec((BLK, BLK), lambda r, n: (jnp.maximum(n - 1, 0), r))
    blk = pl.BlockSpec((BLK, GW), lambda r, n: (n, r))
    return cur, prev, tab_c, tab_p, blk


def _attn_fwd(qkv, cos_t, sin_t, gi):
    d = PATTERN_DILATIONS[gi]
    sub = S // d
    nb = sub // BLK
    cur, prev, tab_c, tab_p, blk = _attn_specs(gi, d)

    def body(q_ref, kc_ref, kp_ref, vc_ref, vp_ref, cc_ref, cp_ref, sc_ref, sp_ref, o_ref, l_ref):
        n = pl.program_id(1)
        lo = _lane_lo()
        cc, sc = _tile4(cc_ref[...]), _tile4(sc_ref[...])
        cp, sp = _tile4(cp_ref[...]), _tile4(sp_ref[...])
        q = (_rope(q_ref[...], cc, sc, lo) * (HEAD_DIM ** -0.5)).astype(BF16)
        k = jnp.concatenate([_rope(kp_ref[...], cp, sp, lo), _rope(kc_ref[...], cc, sc, lo)], axis=0).astype(BF16)
        v = jnp.concatenate([vp_ref[...], vc_ref[...]], axis=0).astype(BF16)
        mask = _band_mask(n)
        outs, lses = [], []
        for h in range(HEADS):
            sl = slice(h * HEAD_DIM, (h + 1) * HEAD_DIM)
            s = jnp.where(mask, _dot_nt(q[:, sl], k[:, sl]), NEG)
            m = jnp.max(s, axis=1, keepdims=True)
            p = jnp.exp(s - m)
            l = jnp.sum(p, axis=1, keepdims=True)
            outs.append(_dot_nn((p / l).astype(BF16), v[:, sl]))
            lses.append(jnp.broadcast_to(m + jnp.log(l), (BLK, HEAD_DIM)))
        o_ref[...] = jnp.concatenate(outs, axis=1)
        l_ref[...] = jnp.concatenate(lses, axis=1)

    qv = qkv.reshape(sub, d * QKV_W)
    cv = cos_t.reshape(sub, d * BLK)
    sv = sin_t.reshape(sub, d * BLK)
    o, lse = pl.pallas_call(
        body,
        name=f"attn_fwd_g{gi}",
        grid=(d, nb),
        in_specs=[cur(0), cur(3), prev(3), cur(6), prev(6), tab_c, tab_p, tab_c, tab_p],
        out_specs=[blk, blk],
        out_shape=[jax.ShapeDtypeStruct((sub, d * GW), F32)] * 2,
        compiler_params=_cp(("parallel", "parallel")),
    )(qv, qv, qv, qv, qv, cv, cv, sv, sv)
    return o.reshape(S, GW), lse.reshape(S, GW)


def _attn_bwd(qkv, cos_t, sin_t, do, lse, cterm, gi):
    d = PATTERN_DILATIONS[gi]
    sub = S // d
    nb = sub // BLK
    cur, prev, tab_c, tab_p, blk = _attn_specs(gi, d)

    def body(q_ref, kc_ref, kp_ref, vc_ref, vp_ref, cc_ref, cp_ref, sc_ref, sp_ref, do_ref, l_ref, c_ref,
             dq_ref, dk_ref, dv_ref):
        n = pl.program_id(1)
        lo = _lane_lo()
        cc, sc = _tile4(cc_ref[...]), _tile4(sc_ref[...])
        cp, sp = _tile4(cp_ref[...]), _tile4(sp_ref[...])
        q = (_rope(q_ref[...], cc, sc, lo) * (HEAD_DIM ** -0.5)).astype(BF16)
        k = jnp.concatenate([_rope(kp_ref[...], cp, sp, lo), _rope(kc_ref[...], cc, sc, lo)], axis=0).astype(BF16)
        v = jnp.concatenate([vp_ref[...], vc_ref[...]], axis=0).astype(BF16)
        dov = do_ref[...]
        lse_v = l_ref[...]
        cv_ = c_ref[...]
        mask = _band_mask(n)
        dqs, dks, dvs = [], [], []
        for h in range(HEADS):
            sl = slice(h * HEAD_DIM, (h + 1) * HEAD_DIM)
            s = jnp.where(mask, _dot_nt(q[:, sl], k[:, sl]), NEG)
            p = jnp.exp(s - lse_v[:, h * HEAD_DIM:h * HEAD_DIM + 1])
            dp = _dot_nt(dov[:, sl], v[:, sl])
            ds = (p * (dp - cv_[:, h * HEAD_DIM:h * HEAD_DIM + 1])).astype(BF16)
            dqs.append(_dot_nn(ds, k[:, sl]) * (HEAD_DIM ** -0.5))
            dks.append(_dot_tn(ds, q[:, sl]))
            dvs.append(_dot_tn(p.astype(BF16), dov[:, sl]))
        dq_ref[...] = _unrope(jnp.concatenate(dqs, axis=1), cc, sc, lo).astype(BF16)
        dk = jnp.concatenate(dks, axis=1)
        dv = jnp.concatenate(dvs, axis=1)
        here = pl.ds(pl.multiple_of(n * BLK, BLK), BLK)
        dk_ref[here, :] = _unrope(dk[BLK:], cc, sc, lo)
        dv_ref[here, :] = dv[BLK:]

        @pl.when(n > 0)
        def _():
            before = pl.ds(pl.multiple_of((n - 1) * BLK, BLK), BLK)
            dk_ref[before, :] += _unrope(dk[:BLK], cp, sp, lo)
            dv_ref[before, :] += dv[:BLK]

    qv = qkv.reshape(sub, d * QKV_W)
    cv = cos_t.reshape(sub, d * BLK)
    sv = sin_t.reshape(sub, d * BLK)
    whole = pl.BlockSpec((sub, GW), lambda r, n: (0, r))
    dq, dk, dv = pl.pallas_call(
        body,
        name=f"attn_bwd_g{gi}",
        grid=(d, nb),
        in_specs=[cur(0), cur(3), prev(3), cur(6), prev(6), tab_c, tab_p, tab_c, tab_p, blk, blk, blk],
        out_specs=[blk, whole, whole],
        out_shape=[
            jax.ShapeDtypeStruct((sub, d * GW), BF16),
            jax.ShapeDtypeStruct((sub, d * GW), F32),
            jax.ShapeDtypeStruct((sub, d * GW), F32),
        ],
        compiler_params=_cp(("arbitrary", "arbitrary"), 48),
    )(qv, qv, qv, qv, qv, cv, cv, sv, sv, do.reshape(sub, d * GW), lse.reshape(sub, d * GW), cterm.reshape(sub, d * GW))
    return dq.reshape(S, GW), dk.reshape(S, GW), dv.reshape(S, GW)


def _group_weights(l0, l1, l2):
    mx = jnp.maximum(jnp.maximum(l0, l1), l2)
    e0, e1, e2 = jnp.exp(l0 - mx), jnp.exp(l1 - mx), jnp.exp(l2 - mx)
    inv = 1.0 / (e0 + e1 + e2)
    return e0 * inv, e1 * inv, e2 * inv


def _combine_fwd(os_, lses, tm=256):
    def body(o0, o1, o2, l0, l1, l2, y_ref):
        w0, w1, w2 = _group_weights(l0[...], l1[...], l2[...])
        y_ref[...] = (w0 * o0[...] + w1 * o1[...] + w2 * o2[...]).astype(BF16)

    sp = pl.BlockSpec((tm, GW), lambda i: (i, 0))
    return pl.pallas_call(
        body,
        name="attn_combine_fwd",
        grid=(S // tm,),
        in_specs=[sp] * 6,
        out_specs=sp,
        out_shape=jax.ShapeDtypeStruct((S, GW), BF16),
        compiler_params=_cp(("parallel",)),
    )(*os_, *lses)


def _combine_bwd(dy, os_, lses, seg, tm=256):
    def body(dy_ref, o0, o1, o2, l0, l1, l2, seg_ref, d0, d1, d2, c0, c1, c2):
        w0, w1, w2 = _group_weights(l0[...], l1[...], l2[...])
        dyv = dy_ref[...]
        y = w0 * o0[...] + w1 * o1[...] + w2 * o2[...]
        t = dyv * y
        t_hi = t.astype(BF16)
        r1 = t - t_hi.astype(F32)
        t_mid = r1.astype(BF16)
        t_lo = (r1 - t_mid.astype(F32)).astype(BF16)
        sg = seg_ref[...]
        e = _dot_nn(t_hi, sg) + _dot_nn(t_mid, sg) + _dot_nn(t_lo, sg)
        d0[...] = (w0 * dyv).astype(BF16)
        d1[...] = (w1 * dyv).astype(BF16)
        d2[...] = (w2 * dyv).astype(BF16)
        c0[...] = w0 * e
        c1[...] = w1 * e
        c2[...] = w2 * e

    sp = pl.BlockSpec((tm, GW), lambda i: (i, 0))
    return pl.pallas_call(
        body,
        name="attn_combine_bwd",
        grid=(S // tm,),
        in_specs=[sp] * 7 + [pl.BlockSpec((GW, GW), lambda i: (0, 0))],
        out_specs=[sp] * 6,
        out_shape=[jax.ShapeDtypeStruct((S, GW), BF16)] * 3 + [jax.ShapeDtypeStruct((S, GW), F32)] * 3,
        compiler_params=_cp(("parallel",)),
    )(dy, *os_, *lses, seg)


_SQRT_HALF = 0.7071067811865476
_INV_SQRT_2PI = 0.3989422804014327


def _gelu(z):
    return 0.5 * z * (1.0 + lax.erf(z * _SQRT_HALF))


def _gelu_grad(z):
    return 0.5 * (1.0 + lax.erf(z * _SQRT_HALF)) + z * (_INV_SQRT_2PI * jnp.exp(-0.5 * z * z))


def _tril_ws(ws_ref, g):
    t = lax.broadcasted_iota(jnp.int32, (BLK, BLK), 0)
    s = lax.broadcasted_iota(jnp.int32, (BLK, BLK), 1)
    return jnp.where(t >= s, ws_ref[g], 0.0)


def _gmlp_fwd(z, ws, bst, ln_g, ln_b, tm=256):
    nch = tm // BLK

    def body(z_ref, ws_ref, b_ref, g_ref, be_ref, y_ref):
        zg = _gelu(z_ref[...])
        u = zg[:, :D]
        vn, _, _ = _ln_fwd(zg[:, D:], g_ref[...], be_ref[...])
        vnb = vn.astype(BF16)
        bt = b_ref[...]
        for g in range(8):
            w = _tril_ws(ws_ref, g).astype(BF16)
            cols = slice(g * BLK, (g + 1) * BLK)
            for c in range(nch):
                rows = slice(c * BLK, (c + 1) * BLK)
                mixed = _dot_nn(w, vnb[rows, cols]) + bt[:, g:g + 1]
                y_ref[rows, cols] = (u[rows, cols] * mixed).astype(BF16)

    return pl.pallas_call(
        body,
        name="gmlp_fwd",
        grid=(S // tm,),
        in_specs=[
            pl.BlockSpec((tm, 2 * D), lambda i: (i, 0)),
            pl.BlockSpec((8, BLK, BLK), lambda i: (0, 0, 0)),
            pl.BlockSpec((BLK, 8), lambda i: (0, 0)),
            pl.BlockSpec((1, D), lambda i: (0, 0)),
            pl.BlockSpec((1, D), lambda i: (0, 0)),
        ],
        out_specs=pl.BlockSpec((tm, D), lambda i: (i, 0)),
        out_shape=jax.ShapeDtypeStruct((S, D), BF16),
        compiler_params=_cp(("parallel",), 48),
    )(z, ws, bst, ln_g, ln_b)


def _gmlp_bwd(z, dy, ws, bst, ln_g, ln_b, tm=256):
    nch = tm // BLK

    def body(z_ref, dy_ref, ws_ref, b_ref, g_ref, be_ref, dz_ref, dws_ref, dbs_ref, dg_ref, dbe_ref, dvn_scr, dm_acc):
        i = pl.program_id(0)
        zv = z_ref[...]
        zg = _gelu(zv)
        u = zg[:, :D]
        gam = g_ref[...]
        vn, xhat, rstd = _ln_fwd(zg[:, D:], gam, be_ref[...])
        vnb = vn.astype(BF16)
        dyv = dy_ref[...]
        dmix = dyv * u
        dmb = dmix.astype(BF16)
        bt = b_ref[...]
        tmask = lax.broadcasted_iota(jnp.int32, (BLK, BLK), 0) >= lax.broadcasted_iota(jnp.int32, (BLK, BLK), 1)
        dm_sum = dmix[0:BLK]
        for c in range(1, nch):
            dm_sum = dm_sum + dmix[c * BLK:(c + 1) * BLK]

        @pl.when(i == 0)
        def _():
            dm_acc[...] = dm_sum

        @pl.when(i > 0)
        def _():
            dm_acc[...] += dm_sum

        dus = []
        for g in range(8):
            w = _tril_ws(ws_ref, g).astype(BF16)
            cols = slice(g * BLK, (g + 1) * BLK)
            dw = None
            du_rows = []
            for c in range(nch):
                rows = slice(c * BLK, (c + 1) * BLK)
                mixed = _dot_nn(w, vnb[rows, cols]) + bt[:, g:g + 1]
                du_rows.append(dyv[rows, cols] * mixed)
                part = _dot_nt(dmb[rows, cols], vnb[rows, cols])
                dw = part if dw is None else dw + part
                dvn_scr[rows, cols] = _dot_tn(w, dmb[rows, cols])
            dus.append(jnp.concatenate(du_rows, axis=0))
            dw = jnp.where(tmask, dw, 0.0)

            @pl.when(i == 0)
            def _():
                dws_ref[g] = dw

            @pl.when(i > 0)
            def _():
                dws_ref[g] += dw

        dvn = dvn_scr[...]
        sg = _colsum(dvn * xhat)
        sb = _colsum(dvn)

        @pl.when(i == 0)
        def _():
            dg_ref[...] = sg
            dbe_ref[...] = sb

        @pl.when(i > 0)
        def _():
            dg_ref[...] += sg
            dbe_ref[...] += sb

        dvg = _ln_bwd(dvn, xhat, rstd, gam)
        gp = _gelu_grad(zv)
        dz_ref[:, :D] = (jnp.concatenate(dus, axis=1) * gp[:, :D]).astype(BF16)
        dz_ref[:, D:] = (dvg * gp[:, D:]).astype(BF16)

        @pl.when(i == S // tm - 1)
        def _():
            acc = dm_acc[...]
            for g in range(8):
                dbs_ref[:, g:g + 1] = jnp.sum(acc[:, g * BLK:(g + 1) * BLK], axis=1, keepdims=True)

    vec = pl.BlockSpec((1, D), lambda i: (0, 0))
    return pl.pallas_call(
        body,
        name="gmlp_bwd",
        grid=(S // tm,),
        in_specs=[
            pl.BlockSpec((tm, 2 * D), lambda i: (i, 0)),
            pl.BlockSpec((tm, D), lambda i: (i, 0)),
            pl.BlockSpec((8, BLK, BLK), lambda i: (0, 0, 0)),
            pl.BlockSpec((BLK, 8), lambda i: (0, 0)),
            vec,
            vec,
        ],
        out_specs=[
            pl.BlockSpec((tm, 2 * D), lambda i: (i, 0)),
            pl.BlockSpec((8, BLK, BLK), lambda i: (0, 0, 0)),
            pl.BlockSpec((BLK, 8), lambda i: (0, 0)),
            vec,
            vec,
        ],
        out_shape=[
            jax.ShapeDtypeStruct((S, 2 * D), BF16),
            jax.ShapeDtypeStruct((8, BLK, BLK), F32),
            jax.ShapeDtypeStruct((BLK, 8), F32),
            jax.ShapeDtypeStruct((1, D), F32),
            jax.ShapeDtypeStruct((1, D), F32),
        ],
        scratch_shapes=[pltpu.VMEM((tm, D), F32), pltpu.VMEM((BLK, D), F32)],
        compiler_params=_cp(("arbitrary",), 48),
    )(z, dy, ws, bst, ln_g, ln_b)


def _merge_fwd(ya, yg, glog, bgate, h1, wabt, wgb, wo, ln_g, ln_b, tm=256):
    def body(ya_ref, yg_ref, gl_ref, bg_ref, h1_ref, wab_ref, wgb_ref, wo_ref, g_ref, b_ref,
             h_ref, hb_ref, xh_ref, rs_ref, mg_ref, bra_ref, brg_ref):
        bra = _dot_nt(ya_ref[...], wab_ref[...])
        brg = _dot_nn(yg_ref[...], wgb_ref[...])
        gates = _sigmoid(gl_ref[...] + bg_ref[...])
        merged = (gates[:, :D] * bra + gates[:, D:] * brg).astype(BF16)
        mix = _dot_nn(merged, wo_ref[...])
        h, xhat, rstd = _ln_fwd(ALPHA * h1_ref[...] + mix, g_ref[...], b_ref[...])
        h_ref[...] = h
        hb_ref[...] = h.astype(BF16)
        xh_ref[...] = xhat
        rs_ref[...] = rstd
        mg_ref[...] = merged
        bra_ref[...] = bra
        brg_ref[...] = brg

    row = pl.BlockSpec((tm, D), lambda i: (i, 0))
    vec = pl.BlockSpec((1, D), lambda i: (0, 0))
    full = lambda shape: pl.BlockSpec(shape, lambda i: (0, 0))
    return pl.pallas_call(
        body,
        name="merge_fwd",
        grid=(S // tm,),
        in_specs=[
            pl.BlockSpec((tm, GW), lambda i: (i, 0)), row,
            pl.BlockSpec((tm, 2 * D), lambda i: (i, glog.shape[1] // (2 * D) - 1)),
            full((1, 2 * D)), row,
            full((D, GW)), full((D, D)), full((D, D)), vec, vec,
        ],
        out_specs=[row, row, row, pl.BlockSpec((tm, 1), lambda i: (i, 0)), row, row, row],
        out_shape=[
            jax.ShapeDtypeStruct((S, D), F32),
            jax.ShapeDtypeStruct((S, D), BF16),
            jax.ShapeDtypeStruct((S, D), F32),
            jax.ShapeDtypeStruct((S, 1), F32),
            jax.ShapeDtypeStruct((S, D), BF16),
            jax.ShapeDtypeStruct((S, D), F32),
            jax.ShapeDtypeStruct((S, D), F32),
        ],
        compiler_params=_cp(("parallel",), 48),
    )(ya, yg, glog, bgate, h1, wabt, wgb, wo, ln_g, ln_b)


def _merge_bwd(dh2, xhat, rstd, ln_g, bra, brg, glog, bgate, wabt, wgb, wo, tm=256):
    def body(dh_ref, xh_ref, rs_ref, g_ref, bra_ref, brg_ref, gl_ref, bg_ref, wab_ref, wgb_ref, wo_ref,
             dr_ref, drb_ref, dlog_ref, dba_ref, dbg_ref, dya_ref, dyg_ref, dbgate_ref, dg_ref, dbias_ref):
        i = pl.program_id(0)
        dh = dh_ref[...]
        xh = xh_ref[...]
        dr = _ln_bwd(dh, xh, rs_ref[...], g_ref[...])
        drb = dr.astype(BF16)
        dr_ref[...] = dr
        drb_ref[...] = drb
        dmerged = _dot_nt(drb, wo_ref[...])
        gates = _sigmoid(gl_ref[...] + bg_ref[...])
        g0, g1 = gates[:, :D], gates[:, D:]
        dl0 = dmerged * bra_ref[...] * g0 * (1.0 - g0)
        dl1 = dmerged * brg_ref[...] * g1 * (1.0 - g1)
        dlog_ref[:, :D] = dl0.astype(BF16)
        dlog_ref[:, D:] = dl1.astype(BF16)
        dba = (dmerged * g0).astype(BF16)
        dbg = (dmerged * g1).astype(BF16)
        dba_ref[...] = dba
        dbg_ref[...] = dbg
        dya_ref[...] = _dot_nn(dba, wab_ref[...])
        dyg_ref[...] = _dot_nt(dbg, wgb_ref[...])
        s0, s1 = _colsum(dl0), _colsum(dl1)
        sg, sb = _colsum(dh * xh), _colsum(dh)

        @pl.when(i == 0)
        def _():
            dbgate_ref[:, :D] = s0
            dbgate_ref[:, D:] = s1
            dg_ref[...] = sg
            dbias_ref[...] = sb

        @pl.when(i > 0)
        def _():
            dbgate_ref[:, :D] += s0
            dbgate_ref[:, D:] += s1
            dg_ref[...] += sg
            dbias_ref[...] += sb

    row = pl.BlockSpec((tm, D), lambda i: (i, 0))
    vec = pl.BlockSpec((1, D), lambda i: (0, 0))
    wide = pl.BlockSpec((tm, 2 * D), lambda i: (i, 0))
    full = lambda shape: pl.BlockSpec(shape, lambda i: (0, 0))
    return pl.pallas_call(
        body,
        name="merge_bwd",
        grid=(S // tm,),
        in_specs=[row, row, pl.BlockSpec((tm, 1), lambda i: (i, 0)), vec, row, row,
                  pl.BlockSpec((tm, 2 * D), lambda i: (i, glog.shape[1] // (2 * D) - 1)),
                  full((1, 2 * D)), full((D, GW)), full((D, D)), full((D, D))],
        out_specs=[row, row, wide, row, row, pl.BlockSpec((tm, GW), lambda i: (i, 0)), row, full((1, 2 * D)), vec, vec],
        out_shape=[
            jax.ShapeDtypeStruct((S, D), F32),
            jax.ShapeDtypeStruct((S, D), BF16),
            jax.ShapeDtypeStruct((S, 2 * D), BF16),
            jax.ShapeDtypeStruct((S, D), BF16),
            jax.ShapeDtypeStruct((S, D), BF16),
            jax.ShapeDtypeStruct((S, GW), F32),
            jax.ShapeDtypeStruct((S, D), F32),
            jax.ShapeDtypeStruct((1, 2 * D), F32),
            jax.ShapeDtypeStruct((1, D), F32),
            jax.ShapeDtypeStruct((1, D), F32),
        ],
        compiler_params=_cp(("arbitrary",), 48),
    )(dh2, xhat, rstd, ln_g, bra, brg, glog, bgate, wabt, wgb, wo)


def _loss_head(h3, target, tm=512):
    def body(h_ref, t_ref, d_ref, l_ref):
        i = pl.program_id(0)
        e = h_ref[...] - t_ref[...]
        d_ref[...] = e * (1.0 / D)
        part = jnp.sum(_colsum(e * e), axis=1, keepdims=True) * (0.5 / D)

        @pl.when(i == 0)
        def _():
            l_ref[...] = part

        @pl.when(i > 0)
        def _():
            l_ref[...] += part

    row = pl.BlockSpec((tm, D), lambda i: (i, 0))
    return pl.pallas_call(
        body,
        name="loss_head",
        grid=(S // tm,),
        in_specs=[row, row],
        out_specs=[row, pl.BlockSpec((1, 1), lambda i: (0, 0))],
        out_shape=[jax.ShapeDtypeStruct((S, D), F32), jax.ShapeDtypeStruct((1, 1), F32)],
        compiler_params=_cp(("arbitrary",)),
    )(h3, target)


def _tie(x, dep):
    if dep is None:
        return x
    return x + dep[0, 0].astype(x.dtype)


def _local_step(x, pos_col, target, get_w, p, emit):
    w = get_w("ffn1", None)
    a1, b1, hm1 = _ffn_up(x, w["g1"], w["u1"], "ffn1_up")
    h1, h1b, xh1, rs1 = _ffn_down(hm1, w["d1"], x, p["ln1_g"], p["ln1_b"], "ffn1_down")

    w.update(get_w("win", h1b))
    qkv = _matmul(h1b, w["win"], "nt", F32, 1024, 1536, D, "proj_qkv", b_off=0, n_out=QKV_W)
    z = glog = _matmul(h1b, w["win"], "nt", F32, 1024, 512, D, "proj_zg", b_off=QKV_W // 512, n_out=4 * D)

    half = jnp.arange(0, HEAD_DIM, 2, dtype=F32) / HEAD_DIM
    inv_freq = ROPE_THETA ** (-half)
    invf = jnp.tile(inv_freq, 4).reshape(1, BLK)
    sign = jnp.tile(jnp.concatenate([-jnp.ones((32,), F32), jnp.ones((32,), F32)]), 2).reshape(1, BLK)
    cos_t, sin_t = _rope_tables(pos_col, invf, sign)

    os_, lses = [], []
    for gi in range(3):
        o, lse = _attn_fwd(qkv, cos_t, sin_t, gi)
        os_.append(o)
        lses.append(lse)
    ya = _combine_fwd(os_, lses)
    bst = p["gmlp_b_s"].T
    yg = _gmlp_fwd(z, p["gmlp_w_s"], bst, p["gmlp_ln_g"], p["gmlp_ln_b"])
    w.update(get_w("mix", yg))
    h2, h2b, xh2, rs2, merged, bra, brg = _merge_fwd(ya, yg, glog, p["b_gates"], h1, w["ab"], w["gb"], w["o"],
                                                      p["ln2_g"], p["ln2_b"])
    w.update(get_w("ffn2", h2b))
    a2, b2, hm2 = _ffn_up(h2b, w["g2"], w["u2"], "ffn2_up")
    h3, _, xh3, rs3 = _ffn_down(hm2, w["d2"], h2, p["ln3_g"], p["ln3_b"], "ffn2_down")
    dh3, loss = _loss_head(h3, target)

    gp = {}
    dr3, df2, da2, db2, gp["ln3_g"], gp["ln3_b"] = _ffn_bwd_mid(dh3, None, xh3, rs3, p["ln3_g"], a2, b2, w["d2"],
                                                                "ffn2_bwd_mid")
    tok = emit("ffn2", {
        "g2": _matmul(da2, h2b, "tn", BF16, 1408, D, S, "wgrad_g2"),
        "u2": _matmul(db2, h2b, "tn", BF16, 1408, D, S, "wgrad_u2"),
        "d2": _matmul(hm2, df2, "tn", BF16, 1408, D, S, "wgrad_d2")})
    dh2 = _ffn_bwd_dx(dr3, da2, db2, w["g2"], w["u2"], "ffn2_bwd_dx")

    (dr2, dr2b, dlog, dba, dbg, dya, dyg, gp["b_gates"], gp["ln2_g"], gp["ln2_b"]) = _merge_bwd(
        dh2, xh2, rs2, _tie(p["ln2_g"], tok), bra, brg, glog, p["b_gates"], w["ab"], w["gb"], w["o"])
    tok = emit("mix", {
        "o": _matmul(merged, dr2b, "tn", BF16, 512, D, S, "wgrad_o"),
        "ab": _matmul(dba, ya, "tn", BF16, 512, GW, S, "wgrad_ab"),
        "gb": _matmul(yg, dbg, "tn", BF16, 512, D, S, "wgrad_gb")})

    seg = (jnp.arange(GW)[:, None] // HEAD_DIM == jnp.arange(GW)[None, :] // HEAD_DIM).astype(BF16)
    do0, do1, do2, c0, c1, c2 = _combine_bwd(dya, os_, lses, _tie(seg, tok))
    dqkv = []
    for gi, (do, ct) in enumerate(((do0, c0), (do1, c1), (do2, c2))):
        dqkv.append(_attn_bwd(qkv, cos_t, sin_t, do, lses[gi], ct, gi))
    dz, gp["gmlp_w_s"], dbst, gp["gmlp_ln_g"], gp["gmlp_ln_b"] = _gmlp_bwd(
        z, dyg, p["gmlp_w_s"], bst, p["gmlp_ln_g"], p["gmlp_ln_b"])
    gp["gmlp_b_s"] = dbst.T
    dproj = jnp.concatenate(
        [dqkv[gi][0] for gi in range(3)] + [dqkv[gi][1].astype(BF16) for gi in range(3)]
        + [dqkv[gi][2].astype(BF16) for gi in range(3)] + [dz, dlog], axis=1)
    tok = emit("win", {"win": _matmul(dproj, h1b, "tn", BF16, 512, D, S, "wgrad_win")})
    dh1m = _matmul(dproj, w["win"], "nn", F32, 512, D, IN_W // 4, "dproj_to_dh1", dep=tok)

    dr1, df1, da1, db1, gp["ln1_g"], gp["ln1_b"] = _ffn_bwd_mid(dr2, dh1m, xh1, rs1, p["ln1_g"], a1, b1, w["d1"],
                                                                "ffn1_bwd_mid")
    tok = emit("small", gp)
    tok = emit("g1", {"g1": _matmul(da1, x, "tn", BF16, 1408, D, S, "wgrad_g1", dep=tok)})
    tok = emit("u1", {"u1": _matmul(db1, x, "tn", BF16, 1408, D, S, "wgrad_u1", dep=tok)})
    emit("d1", {"d1": _matmul(hm1, df1, "tn", BF16, 1408, D, S, "wgrad_d1", dep=tok)})
    dx = _ffn_bwd_dx(dr1, da1, db1, w["g1"], w["u1"], "ffn1_bwd_dx")
    return loss, dx, gp


_FLIPS = [(mx, my, mc) for mx in (0, 1) for my in (0, 1) for mc in (0, 1)][1:]
_GROUPS = {"ffn1": ("g1", "u1", "d1"), "win": ("win",), "mix": ("ab", "gb", "o"), "ffn2": ("g2", "u2", "d2")}
_SCATTERS = {"ffn2": ("g2", "u2", "d2"), "mix": ("ab", "gb", "o"), "win": ("win",), "g1": ("g1",), "u1": ("u1",), "d1": ("d1",)}
_HBM = pl.BlockSpec(memory_space=pltpu.HBM)
_SEM = pl.BlockSpec(memory_space=pltpu.SEMAPHORE)
_EFFECT = pltpu.SideEffectType.DATAFLOW_SIDE_EFFECTING


def _me():
    return 4 * lax.axis_index("x") + 2 * lax.axis_index("y") + lax.axis_index("c")


def _copies_start(bufs, copies, n_sem, name, after=None):
    nb = len(bufs)
    n_after = 0 if after is None else 1

    def body(*refs):
        b = refs[:nb]
        send_sems, recv_sems = refs[nb + n_after], refs[nb + n_after + 1]
        token = refs[-1]
        x, y, c = lax.axis_index("x"), lax.axis_index("y"), lax.axis_index("c")
        me = 4 * x + 2 * y + c
        for si, s_slot, di, d_slot, (mx, my, mc), sem in copies:
            s_idx, d_idx = s_slot(me), d_slot(me)
            pltpu.make_async_remote_copy(
                src_ref=b[si] if s_idx is None else b[si].at[s_idx],
                dst_ref=b[di] if d_idx is None else b[di].at[d_idx],
                send_sem=send_sems.at[sem], recv_sem=recv_sems.at[sem],
                device_id=(x ^ mx, y ^ my, c ^ mc), device_id_type=MESH).start()
        token[...] = jnp.zeros_like(token)

    ins = [pltpu.with_memory_space_constraint(a, pltpu.HBM) for a in bufs]
    outs = pl.pallas_call(
        body,
        name=name,
        in_specs=[_HBM] * nb + [pl.BlockSpec(memory_space=pl.ANY)] * n_after,
        out_specs=[_SEM, _SEM] + [_HBM] * nb + [pl.BlockSpec(memory_space=pltpu.VMEM)],
        out_shape=[pltpu.SemaphoreType.DMA((n_sem,)), pltpu.SemaphoreType.DMA((n_sem,))]
        + [pltpu.HBM(a.shape, a.dtype) for a in ins] + [jax.ShapeDtypeStruct((8, 128), F32)],
        input_output_aliases={k: 2 + k for k in range(nb)},
        compiler_params=pltpu.CompilerParams(has_side_effects=_EFFECT),
    )(*ins, *([] if after is None else [after]))
    return outs[0], outs[1], list(outs[2:2 + nb]), outs[-1]


def _copies_wait(started, waits, after, name):
    send_sems, recv_sems, bufs, _ = started
    nb = len(bufs)

    def body(*refs):
        b = refs[:nb]
        ss, rs = refs[nb], refs[nb + 1]
        me3 = (lax.axis_index("x"), lax.axis_index("y"), lax.axis_index("c"))
        for bi, n_blocks, sem, is_send in waits:
            blocks = b[bi].at[pl.ds(0, n_blocks)]
            cp = pltpu.make_async_remote_copy(src_ref=blocks, dst_ref=blocks, send_sem=ss.at[sem], recv_sem=rs.at[sem],
                                              device_id=me3, device_id_type=MESH)
            if is_send:
                cp.wait_send()
            else:
                cp.wait_recv()

    return pl.pallas_call(
        body,
        name=name,
        in_specs=[_HBM] * nb + [_SEM, _SEM, pl.BlockSpec(memory_space=pl.ANY)],
        out_specs=[_HBM] * nb,
        out_shape=[pltpu.HBM(a.shape, a.dtype) for a in bufs],
        input_output_aliases={k: k for k in range(nb)},
        compiler_params=pltpu.CompilerParams(has_side_effects=_EFFECT),
    )(*bufs, send_sems, recv_sems, after)


def _landing(own, me):
    return lax.dynamic_update_slice(lax.empty((N_DEV,) + own.shape[1:], own.dtype), own, (me, 0, 0))


_SIBLING = (0, 0, 1)
_OTHER_CHIPS = ((1, 0, 0), (0, 1, 0), (1, 1, 0))


def _bits(flip):
    return 4 * flip[0] + 2 * flip[1] + flip[2]


def _gather_send(shards, grp, after):
    nw = len(shards)
    me = _me()
    bufs = list(shards) + [_landing(a[None], me) for a in shards]
    copies = []
    for k in range(nw):
        copies.append((k, lambda me: None, nw + k, lambda me: me, _SIBLING, nw + k))
        copies += [(k, lambda me: None, nw + k, lambda me: me, f, k) for f in _OTHER_CHIPS]
    return _copies_start(bufs, copies, 2 * nw, "gather_send_" + grp, after)


def _gather_finish(started, grp, after):
    nw = len(started[2]) // 2
    waits = []
    for k in range(nw):
        waits += [(nw + k, 3, k, True), (nw + k, 1, nw + k, True), (nw + k, 3, k, False), (nw + k, 1, nw + k, False)]
    lands = list(_copies_wait(started, waits, after, "gather_arrived_" + grp)[nw:])
    copies = []
    for k in range(nw):
        for f in _OTHER_CHIPS:
            slot = functools.partial(lambda me, bits: me ^ bits, bits=_bits(f))
            copies.append((k, slot, k, slot, _SIBLING, k))
    passed = _copies_start(lands, copies, nw, "gather_pass_" + grp)
    waits = [(k, 3, k, is_send) for k in range(nw) for is_send in (True, False)]
    return _copies_wait(passed, waits, passed[3], "gather_done_" + grp)


def _scatter_send(parts, grp, after=None):
    nw = len(parts)
    me = _me()
    bufs = list(parts) + [_landing(lax.dynamic_index_in_dim(a, me, 0, keepdims=True), me) for a in parts]
    copies = []
    for k in range(nw):
        for f in _FLIPS:
            to = functools.partial(lambda me, bits: me ^ bits, bits=_bits(f))
            copies.append((k, to, nw + k, lambda me: me, f, k))
    return _copies_start(bufs, copies, nw, "scatter_send_" + grp, after)


def _scatter_finish(started, grp, after):
    nw = len(started[2]) // 2
    waits = [(nw + k, N_DEV - 1, k, is_send) for k in range(nw) for is_send in (True, False)]
    return _copies_wait(started, waits, after, "scatter_done_" + grp)[nw:]


def _allgather_send(block, name, after=None):
    me = _me()
    copies = [(0, lambda me: None, 1, lambda me: me, f, 0) for f in _FLIPS]
    return _copies_start([block, _landing(block[None], me)], copies, 1, name, after)


def _allgather_finish(started, name, after):
    waits = [(1, N_DEV - 1, 0, True), (1, N_DEV - 1, 0, False)]
    return _copies_wait(started, waits, after, name)[1]


_ADAM_ROWS = {352: 176, 1088: 272}


def _sum_adamw(parts, wv, m, v, name):
    _, r, c = parts.shape
    tm = _ADAM_ROWS.get(r, r)
    assert r % tm == 0 and wv.shape == (r, c)

    def body(p_ref, w_ref, m_ref, v_ref, g_ref, d_ref, mo_ref, vo_ref):
        gv = p_ref[0].astype(F32)
        for j in range(1, N_DEV):
            gv = gv + p_ref[j].astype(F32)
        g_ref[...] = gv
        mn = ADAM_B1 * m_ref[...] + (1.0 - ADAM_B1) * gv
        vn = ADAM_B2 * v_ref[...] + (1.0 - ADAM_B2) * (gv * gv)
        m_hat = mn / (1.0 - ADAM_B1 ** ADAM_STEP)
        v_hat = vn / (1.0 - ADAM_B2 ** ADAM_STEP)
        d_ref[...] = -ADAM_LR * (m_hat / (jnp.sqrt(v_hat) + ADAM_EPS) + ADAM_WD * w_ref[...])
        mo_ref[...] = mn
        vo_ref[...] = vn

    sp = pl.BlockSpec((tm, c), lambda i: (i, 0))
    return pl.pallas_call(
        body,
        name=name,
        grid=(r // tm,),
        in_specs=[pl.BlockSpec((N_DEV, tm, c), lambda i: (0, i, 0))] + [sp] * 3,
        out_specs=[sp] * 4,
        out_shape=[jax.ShapeDtypeStruct((r, c), F32)] * 4,
        compiler_params=_cp(("parallel",), 48),
    )(parts, wv, m, v)


_WEIGHTS = ["ffn1_w_gate", "ffn1_w_up", "ffn1_w_down", "ln1_g", "ln1_b", "w_in", "b_gates", "gmlp_ln_g", "gmlp_ln_b",
            "gmlp_w_s", "gmlp_b_s", "w_attn_branch", "w_gmlp_branch", "w_out", "ln2_g", "ln2_b", "ffn2_w_gate",
            "ffn2_w_up", "ffn2_w_down", "ln3_g", "ln3_b"]
_BIG_OF = {"ffn1_w_gate": ("g1", True), "ffn1_w_up": ("u1", True), "ffn1_w_down": ("d1", False), "w_in": ("win", True),
           "w_attn_branch": ("ab", True), "w_gmlp_branch": ("gb", False), "w_out": ("o", False),
           "ffn2_w_gate": ("g2", True), "ffn2_w_up": ("u2", True), "ffn2_w_down": ("d2", False)}
_SMALL = [n for n in _WEIGHTS if n not in _BIG_OF]
_SMALL_ROWS = {"gmlp_w_s": 128, "b_gates": 2}
_SMALL_SLOT = 8


def _pack_small(d):
    rows = []
    for n in _SMALL:
        r = d[n].reshape(-1, D)
        slot = max(r.shape[0], _SMALL_SLOT)
        rows.append(jnp.pad(r, ((0, slot - r.shape[0]), (0, 0))))
    return jnp.concatenate(rows, axis=0)


def _unpack_small(packed, shapes):
    out, at = {}, 0
    for n in _SMALL:
        k = _SMALL_ROWS.get(n, 1)
        out[n] = packed[at:at + k].reshape(shapes[n])
        at += max(k, _SMALL_SLOT)
    return out


def kernel(x, positions, ffn1_w_gate, ffn1_w_up, ffn1_w_down, ln1_g, ln1_b, w_in, b_gates, gmlp_ln_g, gmlp_ln_b, gmlp_w_s, gmlp_b_s, w_attn_branch, w_gmlp_branch, w_out, ln2_g, ln2_b, ffn2_w_gate, ffn2_w_up, ffn2_w_down, ln3_g, ln3_b, loss_target, m_ffn1_w_gate, m_ffn1_w_up, m_ffn1_w_down, m_ln1_g, m_ln1_b, m_w_in, m_b_gates, m_gmlp_ln_g, m_gmlp_ln_b, m_gmlp_w_s, m_gmlp_b_s, m_w_attn_branch, m_w_gmlp_branch, m_w_out, m_ln2_g, m_ln2_b, m_ffn2_w_gate, m_ffn2_w_up, m_ffn2_w_down, m_ln3_g, m_ln3_b, v_ffn1_w_gate, v_ffn1_w_up, v_ffn1_w_down, v_ln1_g, v_ln1_b, v_w_in, v_b_gates, v_gmlp_ln_g, v_gmlp_ln_b, v_gmlp_w_s, v_gmlp_b_s, v_w_attn_branch, v_w_gmlp_branch, v_w_out, v_ln2_g, v_ln2_b, v_ffn2_w_gate, v_ffn2_w_up, v_ffn2_w_down, v_ln3_g, v_ln3_b):
    args = dict(locals())
    wts = {n: args[n] for n in _WEIGHTS}
    ms = {n: args["m_" + n] for n in _WEIGHTS}
    vs = {n: args["v_" + n] for n in _WEIGHTS}

    name_of = {key: (n, tr) for n, (key, tr) in _BIG_OF.items()}

    started, tok = {}, None
    for grp, keys in _GROUPS.items():
        shards = []
        for key in keys:
            n, tr = name_of[key]
            s2 = wts[n][0]
            shards.append((s2.T if tr else s2).astype(BF16))
        started[grp] = _gather_send(shards, grp, tok)
        tok = started[grp][3]
    all_started = tok

    def get_w(grp, after):
        lands = _gather_finish(started[grp], grp, all_started if after is None else after)
        return {key: g.reshape(-1, g.shape[-1]) for key, g in zip(_GROUPS[grp], lands)}

    sent = {}

    def emit(grp, grads):
        if grp == "small":
            sent[grp] = _allgather_send(_pack_small(grads), "small_grads_send")
        else:
            parts = [grads[key].reshape(N_DEV, -1, grads[key].shape[-1]) for key in _SCATTERS[grp]]
            sent[grp] = _scatter_send(parts, grp)
        return sent[grp][3]

    p = {n: (wts[n][0] if n in ("gmlp_w_s", "gmlp_b_s") else wts[n]) for n in _SMALL}
    loss, dx, gp = _local_step(x[0], positions.reshape(S, 1), loss_target[0], get_w, p, emit)
    loss = lax.psum(loss[0, 0], ("x", "y", "c"))

    grads, deltas, new_m, new_v = {}, {}, {}, {}
    after = sent["d1"][3]
    for grp in ("ffn2", "mix", "win", "small", "g1", "u1", "d1"):
        if grp == "small":
            parts = _allgather_finish(sent[grp], "small_grads_done", after)
            outs = _sum_adamw(parts, *[_pack_small({n: d[n] for n in _SMALL}) for d in (wts, ms, vs)], "update_small")
            shapes = {n: wts[n].shape for n in _SMALL}
            for dst, packed in zip((grads, deltas, new_m, new_v), outs):
                dst.update(_unpack_small(packed, shapes))
            after = outs[1]
            continue
        for key, part in zip(_SCATTERS[grp], _scatter_finish(sent[grp], grp, after)):
            n, tr = name_of[key]
            outs = _sum_adamw(part, *[(d[n][0].T if tr else d[n][0]) for d in (wts, ms, vs)], "update_" + key)
            for dst, o in zip((grads, deltas, new_m, new_v), outs):
                dst[n] = (o.T if tr else o)[None]
            after = outs[1]

    return (loss, dx[None], *[grads[n] for n in _WEIGHTS], *[deltas[n] for n in _WEIGHTS],
            *[new_m[n] for n in _WEIGHTS], *[new_v[n] for n in _WEIGHTS])
```

```python
import functools
import math

import jax
import jax.numpy as jnp
from jax import lax
from jax.experimental import pallas as pl
from jax.experimental.pallas import tpu as pltpu

F32 = jnp.float32
BF16 = jnp.bfloat16

N_DEV = 8
D = 1024
S = 2048
F = 2816
HEAD_DIM = 64
HEADS = 8
GW = HEADS * HEAD_DIM
PATTERN_DILATIONS = (1, 4, 16)
BLK = 128
QKV_W = 3 * 3 * GW
IN_W = QKV_W + 2 * D + 2 * D
ROPE_THETA = 10000.0
ALPHA = 2.0 ** 0.25
LN_EPS = 1e-5
ADAM_LR, ADAM_B1, ADAM_B2, ADAM_EPS, ADAM_WD, ADAM_STEP = 0.001, 0.9, 0.999, 1e-08, 0.01, 10
NEG = -1e30
MESH = pl.DeviceIdType.MESH


def _cp(sem=None, vmem_mb=None):
    kw = {}
    if sem is not None:
        kw["dimension_semantics"] = sem
    if vmem_mb is not None:
        kw["vmem_limit_bytes"] = vmem_mb << 20
    return pltpu.CompilerParams(**kw)


def _dot_nn(a, b):
    return lax.dot_general(a, b, (((1,), (0,)), ((), ())), preferred_element_type=F32)


def _dot_nt(a, b):
    return lax.dot_general(a, b, (((1,), (1,)), ((), ())), preferred_element_type=F32)


def _dot_tn(a, b):
    return lax.dot_general(a, b, (((0,), (0,)), ((), ())), preferred_element_type=F32)


def _ln_fwd(r, g, b):
    mu = jnp.mean(r, axis=-1, keepdims=True)
    xc = r - mu
    var = jnp.mean(xc * xc, axis=-1, keepdims=True)
    rstd = lax.rsqrt(var + LN_EPS)
    xhat = xc * rstd
    return xhat * g + b, xhat, rstd


def _ln_bwd(dh, xhat, rstd, g):
    dxh = dh * g
    m1 = jnp.mean(dxh, axis=-1, keepdims=True)
    m2 = jnp.mean(dxh * xhat, axis=-1, keepdims=True)
    return rstd * (dxh - m1 - xhat * m2)


def _sigmoid(x):
    return 1.0 / (1.0 + jnp.exp(-x))


def _colsum(x):
    return jnp.sum(x, axis=0, keepdims=True)


def _matmul(a, b, mode, out_dtype, tm, tn, tk, name, b_off=0, n_out=None, dep=None):
    n_dep = 0 if dep is None else 1
    if mode == "nn":
        m, k = a.shape
        n = b.shape[1]
    elif mode == "nt":
        m, k = a.shape
        n = n_out if n_out is not None else b.shape[0]
    else:
        k, m = a.shape
        n = b.shape[1]
    nk = k // tk
    assert m % tm == 0 and n % tn == 0 and k % tk == 0
    dot = {"nn": _dot_nn, "nt": _dot_nt, "tn": _dot_tn}[mode]

    def body(a_ref, b_ref, *rest):
        o_ref, scr = rest[n_dep], rest[n_dep + 1:]
        r = dot(a_ref[...].astype(BF16), b_ref[...].astype(BF16))
        if nk == 1:
            o_ref[...] = r.astype(out_dtype)
        else:
            acc = scr[0]
            kk = pl.program_id(2)

            @pl.when(kk == 0)
            def _():
                acc[...] = r

            @pl.when(kk > 0)
            def _():
                acc[...] += r

            @pl.when(kk == nk - 1)
            def _():
                o_ref[...] = acc[...].astype(out_dtype)

    if mode == "nn":
        a_spec = pl.BlockSpec((tm, tk), lambda i, j, kk: (i, kk))
        b_spec = pl.BlockSpec((tk, tn), lambda i, j, kk: (kk, j))
    elif mode == "nt":
        a_spec = pl.BlockSpec((tm, tk), lambda i, j, kk: (i, kk))
        b_spec = pl.BlockSpec((tn, tk), lambda i, j, kk: (j + b_off, kk))
    else:
        a_spec = pl.BlockSpec((tk, tm), lambda i, j, kk: (kk, i))
        b_spec = pl.BlockSpec((tk, tn), lambda i, j, kk: (kk, j))
    return pl.pallas_call(
        body,
        name=name,
        grid=(m // tm, n // tn, nk),
        in_specs=[a_spec, b_spec] + [pl.BlockSpec(memory_space=pl.ANY)] * n_dep,
        out_specs=pl.BlockSpec((tm, tn), lambda i, j, kk: (i, j)),
        out_shape=jax.ShapeDtypeStruct((m, n), out_dtype),
        scratch_shapes=[] if nk == 1 else [pltpu.VMEM((tm, tn), F32)],
        compiler_params=_cp(("parallel", "parallel", "arbitrary"), 56),
    )(a, b, *([] if dep is None else [dep]))


def _ffn_up(x, wgt, wut, name, tm=512, tn=1408):
    def body(x_ref, wg_ref, wu_ref, a_ref, b_ref, hm_ref):
        xb = x_ref[...].astype(BF16)
        a = _dot_nt(xb, wg_ref[...])
        b = _dot_nt(xb, wu_ref[...])
        a_ref[...] = a.astype(BF16)
        b_ref[...] = b.astype(BF16)
        hm_ref[...] = ((a * _sigmoid(a)) * b).astype(BF16)

    wsp = pl.BlockSpec((tn, D), lambda i, j: (j, 0))
    mid = pl.BlockSpec((tm, tn), lambda i, j: (i, j))
    return pl.pallas_call(
        body,
        name=name,
        grid=(S // tm, F // tn),
        in_specs=[pl.BlockSpec((tm, D), lambda i, j: (i, 0)), wsp, wsp],
        out_specs=[mid, mid, mid],
        out_shape=[jax.ShapeDtypeStruct((S, F), BF16)] * 3,
        compiler_params=_cp(("parallel", "arbitrary"), 56),
    )(x, wgt, wut)


def _ffn_down(hm, wd, x, ln_g, ln_b, name, tm=256):
    def body(hm_ref, wd_ref, x_ref, g_ref, b_ref, h_ref, hb_ref, xh_ref, rs_ref):
        r = ALPHA * x_ref[...] + 0.5 * _dot_nn(hm_ref[...], wd_ref[...])
        h, xhat, rstd = _ln_fwd(r, g_ref[...], b_ref[...])
        h_ref[...] = h
        hb_ref[...] = h.astype(BF16)
        xh_ref[...] = xhat
        rs_ref[...] = rstd

    row = pl.BlockSpec((tm, D), lambda i: (i, 0))
    vec = pl.BlockSpec((1, D), lambda i: (0, 0))
    return pl.pallas_call(
        body,
        name=name,
        grid=(S // tm,),
        in_specs=[pl.BlockSpec((tm, F), lambda i: (i, 0)), pl.BlockSpec((F, D), lambda i: (0, 0)), row, vec, vec],
        out_specs=[row, row, row, pl.BlockSpec((tm, 1), lambda i: (i, 0))],
        out_shape=[
            jax.ShapeDtypeStruct((S, D), F32),
            jax.ShapeDtypeStruct((S, D), BF16),
            jax.ShapeDtypeStruct((S, D), F32),
            jax.ShapeDtypeStruct((S, 1), F32),
        ],
        compiler_params=_cp(("parallel",), 56),
    )(hm, wd, x, ln_g, ln_b)


def _ffn_bwd_mid(dh_a, dh_b, xhat, rstd, ln_g, a, b, wd, name, tm=512, tn=1408):
    two = dh_b is not None

    def body(*refs):
        dha_ref = refs[0]
        dhb_ref = refs[1] if two else None
        (xh_ref, rs_ref, g_ref, a_ref, b_ref, wd_ref,
         dr_ref, df_ref, da_ref, db_ref, dg_ref, dbias_ref, df_scr) = refs[2 if two else 1:]
        i = pl.program_id(0)
        j = pl.program_id(1)

        @pl.when(j == 0)
        def _():
            dh = dha_ref[...]
            if two:
                dh = ALPHA * dh + dhb_ref[...]
            xhat = xh_ref[...]
            dr = _ln_bwd(dh, xhat, rs_ref[...], g_ref[...])
            dfb = (0.5 * dr).astype(BF16)
            dr_ref[...] = dr
            df_scr[...] = dfb
            df_ref[...] = dfb
            sg = _colsum(dh * xhat)
            sb = _colsum(dh)

            @pl.when(i == 0)
            def _():
                dg_ref[...] = sg
                dbias_ref[...] = sb

            @pl.when(i > 0)
            def _():
                dg_ref[...] += sg
                dbias_ref[...] += sb

        dhm = _dot_nt(df_scr[...], wd_ref[...])
        av = a_ref[...].astype(F32)
        bv = b_ref[...].astype(F32)
        sig = _sigmoid(av)
        da_ref[...] = (dhm * bv * (sig * (1.0 + av * (1.0 - sig)))).astype(BF16)
        db_ref[...] = (dhm * (av * sig)).astype(BF16)

    row = pl.BlockSpec((tm, D), lambda i, j: (i, 0))
    vec = pl.BlockSpec((1, D), lambda i, j: (0, 0))
    mid = pl.BlockSpec((tm, tn), lambda i, j: (i, j))
    ins = [dh_a] + ([dh_b] if two else []) + [xhat, rstd, ln_g, a, b, wd]
    in_specs = [row] * (2 if two else 1) + [row, pl.BlockSpec((tm, 1), lambda i, j: (i, 0)), vec, mid, mid,
                                            pl.BlockSpec((tn, D), lambda i, j: (j, 0))]
    return pl.pallas_call(
        body,
        name=name,
        grid=(S // tm, F // tn),
        in_specs=in_specs,
        out_specs=[row, row, mid, mid, vec, vec],
        out_shape=[
            jax.ShapeDtypeStruct((S, D), F32),
            jax.ShapeDtypeStruct((S, D), BF16),
            jax.ShapeDtypeStruct((S, F), BF16),
            jax.ShapeDtypeStruct((S, F), BF16),
            jax.ShapeDtypeStruct((1, D), F32),
            jax.ShapeDtypeStruct((1, D), F32),
        ],
        scratch_shapes=[pltpu.VMEM((tm, D), BF16)],
        compiler_params=_cp(("arbitrary", "arbitrary"), 56),
    )(*ins)


def _ffn_bwd_dx(dr, da, db, wgt, wut, name, tm=512, tk=1408, dep=None):
    nk = F // tk
    n_dep = 0 if dep is None else 1

    def body(dr_ref, da_ref, db_ref, wg_ref, wu_ref, *rest):
        dx_ref, acc = rest[n_dep], rest[n_dep + 1]
        kk = pl.program_id(1)
        part = _dot_nn(da_ref[...], wg_ref[...]) + _dot_nn(db_ref[...], wu_ref[...])

        @pl.when(kk == 0)
        def _():
            acc[...] = ALPHA * dr_ref[...] + part

        @pl.when(kk > 0)
        def _():
            acc[...] += part

        @pl.when(kk == nk - 1)
        def _():
            dx_ref[...] = acc[...]

    row = pl.BlockSpec((tm, D), lambda i, kk: (i, 0))
    mid = pl.BlockSpec((tm, tk), lambda i, kk: (i, kk))
    wsp = pl.BlockSpec((tk, D), lambda i, kk: (kk, 0))
    return pl.pallas_call(
        body,
        name=name,
        grid=(S // tm, nk),
        in_specs=[row, mid, mid, wsp, wsp] + [pl.BlockSpec(memory_space=pl.ANY)] * n_dep,
        out_specs=row,
        out_shape=jax.ShapeDtypeStruct((S, D), F32),
        scratch_shapes=[pltpu.VMEM((tm, D), F32)],
        compiler_params=_cp(("parallel", "arbitrary"), 56),
    )(dr, da, db, wgt, wut, *([] if dep is None else [dep]))


def _rope_tables(pos_col, invf, sign, tm=512):
    def body(p_ref, f_ref, s_ref, c_out, s_out):
        ang = p_ref[...].astype(F32) * f_ref[...]
        c_out[...] = jnp.cos(ang)
        s_out[...] = jnp.sin(ang) * s_ref[...]

    vec = pl.BlockSpec((1, BLK), lambda i: (0, 0))
    out = pl.BlockSpec((tm, BLK), lambda i: (i, 0))
    return pl.pallas_call(
        body,
        name="rope_tables",
        grid=(S // tm,),
        in_specs=[pl.BlockSpec((tm, 1), lambda i: (i, 0)), vec, vec],
        out_specs=[out, out],
        out_shape=[jax.ShapeDtypeStruct((S, BLK), F32)] * 2,
        compiler_params=_cp(("parallel",)),
    )(pos_col, invf, sign)


def _lane_lo():
    return (lax.broadcasted_iota(jnp.int32, (BLK, GW), 1) % HEAD_DIM) < (HEAD_DIM // 2)


def _swap_halves(t, lo):
    return jnp.where(lo, pltpu.roll(t, GW - HEAD_DIM // 2, 1), pltpu.roll(t, HEAD_DIM // 2, 1))


def _rope(t, cosf, sinf, lo):
    return t * cosf + _swap_halves(t, lo) * sinf


def _unrope(g, cosf, sinf, lo):
    return g * cosf + _swap_halves(g * sinf, lo)


def _tile4(v):
    return jnp.concatenate([v, v, v, v], axis=1)


def _band_mask(n):
    qi = lax.broadcasted_iota(jnp.int32, (BLK, 2 * BLK), 0)
    kj = lax.broadcasted_iota(jnp.int32, (BLK, 2 * BLK), 1)
    dist = qi + BLK - kj
    return (dist >= 0) & (dist <= BLK) & ((kj >= BLK) | (n >= 1))


def _attn_specs(gi, d):
    nq = QKV_W // GW

    def cur(off):
        return pl.BlockSpec((BLK, GW), lambda r, n: (n, r * nq + off + gi))

    def prev(off):
        return pl.BlockSpec((BLK, GW), lambda r, n: (jnp.maximum(n - 1, 0), r * nq + off + gi))

    tab_c = pl.BlockSpec((BLK, BLK), lambda r, n: (n, r))
    tab_p = pl.BlockSpec((BLK, BLK), lambda r, n: (jnp.maximum(n - 1, 0), r))
    blk = pl.BlockSpec((BLK, GW), lambda r, n: (n, r))
    return cur, prev, tab_c, tab_p, blk


def _attn_fwd(qkv, cos_t, sin_t, gi):
    d = PATTERN_DILATIONS[gi]
    sub = S // d
    nb = sub // BLK
    cur, prev, tab_c, tab_p, blk = _attn_specs(gi, d)

    def body(q_ref, kc_ref, kp_ref, vc_ref, vp_ref, cc_ref, cp_ref, sc_ref, sp_ref, o_ref, l_ref):
        n = pl.program_id(1)
        lo = _lane_lo()
        cc, sc = _tile4(cc_ref[...]), _tile4(sc_ref[...])
        cp, sp = _tile4(cp_ref[...]), _tile4(sp_ref[...])
        q = (_rope(q_ref[...], cc, sc, lo) * (HEAD_DIM ** -0.5)).astype(BF16)
        k = jnp.concatenate([_rope(kp_ref[...], cp, sp, lo), _rope(kc_ref[...], cc, sc, lo)], axis=0).astype(BF16)
        v = jnp.concatenate([vp_ref[...], vc_ref[...]], axis=0).astype(BF16)
        mask = _band_mask(n)
        outs, lses = [], []
        for h in range(HEADS):
            sl = slice(h * HEAD_DIM, (h + 1) * HEAD_DIM)
            s = jnp.where(mask, _dot_nt(q[:, sl], k[:, sl]), NEG)
            m = jnp.max(s, axis=1, keepdims=True)
            p = jnp.exp(s - m)
            l = jnp.sum(p, axis=1, keepdims=True)
            outs.append(_dot_nn((p / l).astype(BF16), v[:, sl]))
            lses.append(jnp.broadcast_to(m + jnp.log(l), (BLK, HEAD_DIM)))
        o_ref[...] = jnp.concatenate(outs, axis=1)
        l_ref[...] = jnp.concatenate(lses, axis=1)

    qv = qkv.reshape(sub, d * QKV_W)
    cv = cos_t.reshape(sub, d * BLK)
    sv = sin_t.reshape(sub, d * BLK)
    o, lse = pl.pallas_call(
        body,
        name=f"attn_fwd_g{gi}",
        grid=(d, nb),
        in_specs=[cur(0), cur(3), prev(3), cur(6), prev(6), tab_c, tab_p, tab_c, tab_p],
        out_specs=[blk, blk],
        out_shape=[jax.ShapeDtypeStruct((sub, d * GW), F32)] * 2,
        compiler_params=_cp(("parallel", "parallel")),
    )(qv, qv, qv, qv, qv, cv, cv, sv, sv)
    return o.reshape(S, GW), lse.reshape(S, GW)


def _attn_bwd(qkv, cos_t, sin_t, do, lse, cterm, gi):
    d = PATTERN_DILATIONS[gi]
    sub = S // d
    nb = sub // BLK
    cur, prev, tab_c, tab_p, blk = _attn_specs(gi, d)

    def body(q_ref, kc_ref, kp_ref, vc_ref, vp_ref, cc_ref, cp_ref, sc_ref, sp_ref, do_ref, l_ref, c_ref,
             dq_ref, dk_ref, dv_ref):
        n = pl.program_id(1)
        lo = _lane_lo()
        cc, sc = _tile4(cc_ref[...]), _tile4(sc_ref[...])
        cp, sp = _tile4(cp_ref[...]), _tile4(sp_ref[...])
        q = (_rope(q_ref[...], cc, sc, lo) * (HEAD_DIM ** -0.5)).astype(BF16)
        k = jnp.concatenate([_rope(kp_ref[...], cp, sp, lo), _rope(kc_ref[...], cc, sc, lo)], axis=0).astype(BF16)
        v = jnp.concatenate([vp_ref[...], vc_ref[...]], axis=0).astype(BF16)
        dov = do_ref[...]
        lse_v = l_ref[...]
        cv_ = c_ref[...]
        mask = _band_mask(n)
        dqs, dks, dvs = [], [], []
        for h in range(HEADS):
            sl = slice(h * HEAD_DIM, (h + 1) * HEAD_DIM)
            s = jnp.where(mask, _dot_nt(q[:, sl], k[:, sl]), NEG)
            p = jnp.exp(s - lse_v[:, h * HEAD_DIM:h * HEAD_DIM + 1])
            dp = _dot_nt(dov[:, sl], v[:, sl])
            ds = (p * (dp - cv_[:, h * HEAD_DIM:h * HEAD_DIM + 1])).astype(BF16)
            dqs.append(_dot_nn(ds, k[:, sl]) * (HEAD_DIM ** -0.5))
            dks.append(_dot_tn(ds, q[:, sl]))
            dvs.append(_dot_tn(p.astype(BF16), dov[:, sl]))
        dq_ref[...] = _unrope(jnp.concatenate(dqs, axis=1), cc, sc, lo).astype(BF16)
        dk = jnp.concatenate(dks, axis=1)
        dv = jnp.concatenate(dvs, axis=1)
        here = pl.ds(pl.multiple_of(n * BLK, BLK), BLK)
        dk_ref[here, :] = _unrope(dk[BLK:], cc, sc, lo)
        dv_ref[here, :] = dv[BLK:]

        @pl.when(n > 0)
        def _():
            before = pl.ds(pl.multiple_of((n - 1) * BLK, BLK), BLK)
            dk_ref[before, :] += _unrope(dk[:BLK], cp, sp, lo)
            dv_ref[before, :] += dv[:BLK]

    qv = qkv.reshape(sub, d * QKV_W)
    cv = cos_t.reshape(sub, d * BLK)
    sv = sin_t.reshape(sub, d * BLK)
    whole = pl.BlockSpec((sub, GW), lambda r, n: (0, r))
    dq, dk, dv = pl.pallas_call(
        body,
        name=f"attn_bwd_g{gi}",
        grid=(d, nb),
        in_specs=[cur(0), cur(3), prev(3), cur(6), prev(6), tab_c, tab_p, tab_c, tab_p, blk, blk, blk],
        out_specs=[blk, whole, whole],
        out_shape=[
            jax.ShapeDtypeStruct((sub, d * GW), BF16),
            jax.ShapeDtypeStruct((sub, d * GW), F32),
            jax.ShapeDtypeStruct((sub, d * GW), F32),
        ],
        compiler_params=_cp(("arbitrary", "arbitrary"), 48),
    )(qv, qv, qv, qv, qv, cv, cv, sv, sv, do.reshape(sub, d * GW), lse.reshape(sub, d * GW), cterm.reshape(sub, d * GW))
    return dq.reshape(S, GW), dk.reshape(S, GW), dv.reshape(S, GW)


def _group_weights(l0, l1, l2):
    mx = jnp.maximum(jnp.maximum(l0, l1), l2)
    e0, e1, e2 = jnp.exp(l0 - mx), jnp.exp(l1 - mx), jnp.exp(l2 - mx)
    inv = 1.0 / (e0 + e1 + e2)
    return e0 * inv, e1 * inv, e2 * inv


def _combine_fwd(os_, lses, tm=256):
    def body(o0, o1, o2, l0, l1, l2, y_ref):
        w0, w1, w2 = _group_weights(l0[...], l1[...], l2[...])
        y_ref[...] = (w0 * o0[...] + w1 * o1[...] + w2 * o2[...]).astype(BF16)

    sp = pl.BlockSpec((tm, GW), lambda i: (i, 0))
    return pl.pallas_call(
        body,
        name="attn_combine_fwd",
        grid=(S // tm,),
        in_specs=[sp] * 6,
        out_specs=sp,
        out_shape=jax.ShapeDtypeStruct((S, GW), BF16),
        compiler_params=_cp(("parallel",)),
    )(*os_, *lses)


def _combine_bwd(dy, os_, lses, seg, tm=256):
    def body(dy_ref, o0, o1, o2, l0, l1, l2, seg_ref, d0, d1, d2, c0, c1, c2):
        w0, w1, w2 = _group_weights(l0[...], l1[...], l2[...])
        dyv = dy_ref[...]
        y = w0 * o0[...] + w1 * o1[...] + w2 * o2[...]
        t = dyv * y
        t_hi = t.astype(BF16)
        r1 = t - t_hi.astype(F32)
        t_mid = r1.astype(BF16)
        t_lo = (r1 - t_mid.astype(F32)).astype(BF16)
        sg = seg_ref[...]
        e = _dot_nn(t_hi, sg) + _dot_nn(t_mid, sg) + _dot_nn(t_lo, sg)
        d0[...] = (w0 * dyv).astype(BF16)
        d1[...] = (w1 * dyv).astype(BF16)
        d2[...] = (w2 * dyv).astype(BF16)
        c0[...] = w0 * e
        c1[...] = w1 * e
        c2[...] = w2 * e

    sp = pl.BlockSpec((tm, GW), lambda i: (i, 0))
    return pl.pallas_call(
        body,
        name="attn_combine_bwd",
        grid=(S // tm,),
        in_specs=[sp] * 7 + [pl.BlockSpec((GW, GW), lambda i: (0, 0))],
        out_specs=[sp] * 6,
        out_shape=[jax.ShapeDtypeStruct((S, GW), BF16)] * 3 + [jax.ShapeDtypeStruct((S, GW), F32)] * 3,
        compiler_params=_cp(("parallel",)),
    )(dy, *os_, *lses, seg)


_SQRT_HALF = 0.7071067811865476
_INV_SQRT_2PI = 0.3989422804014327


def _gelu(z):
    return 0.5 * z * (1.0 + lax.erf(z * _SQRT_HALF))


def _gelu_grad(z):
    return 0.5 * (1.0 + lax.erf(z * _SQRT_HALF)) + z * (_INV_SQRT_2PI * jnp.exp(-0.5 * z * z))


def _tril_ws(ws_ref, g):
    t = lax.broadcasted_iota(jnp.int32, (BLK, BLK), 0)
    s = lax.broadcasted_iota(jnp.int32, (BLK, BLK), 1)
    return jnp.where(t >= s, ws_ref[g], 0.0)


def _gmlp_fwd(z, ws, bst, ln_g, ln_b, tm=256):
    nch = tm // BLK

    def body(z_ref, ws_ref, b_ref, g_ref, be_ref, y_ref):
        zg = _gelu(z_ref[...])
        u = zg[:, :D]
        vn, _, _ = _ln_fwd(zg[:, D:], g_ref[...], be_ref[...])
        vnb = vn.astype(BF16)
        bt = b_ref[...]
        for g in range(8):
            w = _tril_ws(ws_ref, g).astype(BF16)
            cols = slice(g * BLK, (g + 1) * BLK)
            for c in range(nch):
                rows = slice(c * BLK, (c + 1) * BLK)
                mixed = _dot_nn(w, vnb[rows, cols]) + bt[:, g:g + 1]
                y_ref[rows, cols] = (u[rows, cols] * mixed).astype(BF16)

    return pl.pallas_call(
        body,
        name="gmlp_fwd",
        grid=(S // tm,),
        in_specs=[
            pl.BlockSpec((tm, 2 * D), lambda i: (i, 0)),
            pl.BlockSpec((8, BLK, BLK), lambda i: (0, 0, 0)),
            pl.BlockSpec((BLK, 8), lambda i: (0, 0)),
            pl.BlockSpec((1, D), lambda i: (0, 0)),
            pl.BlockSpec((1, D), lambda i: (0, 0)),
        ],
        out_specs=pl.BlockSpec((tm, D), lambda i: (i, 0)),
        out_shape=jax.ShapeDtypeStruct((S, D), BF16),
        compiler_params=_cp(("parallel",), 48),
    )(z, ws, bst, ln_g, ln_b)


def _gmlp_bwd(z, dy, ws, bst, ln_g, ln_b, tm=256):
    nch = tm // BLK

    def body(z_ref, dy_ref, ws_ref, b_ref, g_ref, be_ref, dz_ref, dws_ref, dbs_ref, dg_ref, dbe_ref, dvn_scr, dm_acc):
        i = pl.program_id(0)
        zv = z_ref[...]
        zg = _gelu(zv)
        u = zg[:, :D]
        gam = g_ref[...]
        vn, xhat, rstd = _ln_fwd(zg[:, D:], gam, be_ref[...])
        vnb = vn.astype(BF16)
        dyv = dy_ref[...]
        dmix = dyv * u
        dmb = dmix.astype(BF16)
        bt = b_ref[...]
        tmask = lax.broadcasted_iota(jnp.int32, (BLK, BLK), 0) >= lax.broadcasted_iota(jnp.int32, (BLK, BLK), 1)
        dm_sum = dmix[0:BLK]
        for c in range(1, nch):
            dm_sum = dm_sum + dmix[c * BLK:(c + 1) * BLK]

        @pl.when(i == 0)
        def _():
            dm_acc[...] = dm_sum

        @pl.when(i > 0)
        def _():
            dm_acc[...] += dm_sum

        dus = []
        for g in range(8):
            w = _tril_ws(ws_ref, g).astype(BF16)
            cols = slice(g * BLK, (g + 1) * BLK)
            dw = None
            du_rows = []
            for c in range(nch):
                rows = slice(c * BLK, (c + 1) * BLK)
                mixed = _dot_nn(w, vnb[rows, cols]) + bt[:, g:g + 1]
                du_rows.append(dyv[rows, cols] * mixed)
                part = _dot_nt(dmb[rows, cols], vnb[rows, cols])
                dw = part if dw is None else dw + part
                dvn_scr[rows, cols] = _dot_tn(w, dmb[rows, cols])
            dus.append(jnp.concatenate(du_rows, axis=0))
            dw = jnp.where(tmask, dw, 0.0)

            @pl.when(i == 0)
            def _():
                dws_ref[g] = dw

            @pl.when(i > 0)
            def _():
                dws_ref[g] += dw

        dvn = dvn_scr[...]
        sg = _colsum(dvn * xhat)
        sb = _colsum(dvn)

        @pl.when(i == 0)
        def _():
            dg_ref[...] = sg
            dbe_ref[...] = sb

        @pl.when(i > 0)
        def _():
            dg_ref[...] += sg
            dbe_ref[...] += sb

        dvg = _ln_bwd(dvn, xhat, rstd, gam)
        gp = _gelu_grad(zv)
        dz_ref[:, :D] = (jnp.concatenate(dus, axis=1) * gp[:, :D]).astype(BF16)
        dz_ref[:, D:] = (dvg * gp[:, D:]).astype(BF16)

        @pl.when(i == S // tm - 1)
        def _():
            acc = dm_acc[...]
            for g in range(8):
                dbs_ref[:, g:g + 1] = jnp.sum(acc[:, g * BLK:(g + 1) * BLK], axis=1, keepdims=True)

    vec = pl.BlockSpec((1, D), lambda i: (0, 0))
    return pl.pallas_call(
        body,
        name="gmlp_bwd",
        grid=(S // tm,),
        in_specs=[
            pl.BlockSpec((tm, 2 * D), lambda i: (i, 0)),
            pl.BlockSpec((tm, D), lambda i: (i, 0)),
            pl.BlockSpec((8, BLK, BLK), lambda i: (0, 0, 0)),
            pl.BlockSpec((BLK, 8), lambda i: (0, 0)),
            vec,
            vec,
        ],
        out_specs=[
            pl.BlockSpec((tm, 2 * D), lambda i: (i, 0)),
            pl.BlockSpec((8, BLK, BLK), lambda i: (0, 0, 0)),
            pl.BlockSpec((BLK, 8), lambda i: (0, 0)),
            vec,
            vec,
        ],
        out_shape=[
            jax.ShapeDtypeStruct((S, 2 * D), BF16),
            jax.ShapeDtypeStruct((8, BLK, BLK), F32),
            jax.ShapeDtypeStruct((BLK, 8), F32),
            jax.ShapeDtypeStruct((1, D), F32),
            jax.ShapeDtypeStruct((1, D), F32),
        ],
        scratch_shapes=[pltpu.VMEM((tm, D), F32), pltpu.VMEM((BLK, D), F32)],
        compiler_params=_cp(("arbitrary",), 48),
    )(z, dy, ws, bst, ln_g, ln_b)


def _merge_fwd(ya, yg, glog, bgate, h1, wabt, wgb, wo, ln_g, ln_b, tm=256):
    def body(ya_ref, yg_ref, gl_ref, bg_ref, h1_ref, wab_ref, wgb_ref, wo_ref, g_ref, b_ref,
             h_ref, hb_ref, xh_ref, rs_ref, mg_ref, bra_ref, brg_ref):
        bra = _dot_nt(ya_ref[...], wab_ref[...])
        brg = _dot_nn(yg_ref[...], wgb_ref[...])
        gates = _sigmoid(gl_ref[...] + bg_ref[...])
        merged = (gates[:, :D] * bra + gates[:, D:] * brg).astype(BF16)
        mix = _dot_nn(merged, wo_ref[...])
        h, xhat, rstd = _ln_fwd(ALPHA * h1_ref[...] + mix, g_ref[...], b_ref[...])
        h_ref[...] = h
        hb_ref[...] = h.astype(BF16)
        xh_ref[...] = xhat
        rs_ref[...] = rstd
        mg_ref[...] = merged
        bra_ref[...] = bra
        brg_ref[...] = brg

    row = pl.BlockSpec((tm, D), lambda i: (i, 0))
    vec = pl.BlockSpec((1, D), lambda i: (0, 0))
    full = lambda shape: pl.BlockSpec(shape, lambda i: (0, 0))
    return pl.pallas_call(
        body,
        name="merge_fwd",
        grid=(S // tm,),
        in_specs=[
            pl.BlockSpec((tm, GW), lambda i: (i, 0)), row,
            pl.BlockSpec((tm, 2 * D), lambda i: (i, glog.shape[1] // (2 * D) - 1)),
            full((1, 2 * D)), row,
            full((D, GW)), full((D, D)), full((D, D)), vec, vec,
        ],
        out_specs=[row, row, row, pl.BlockSpec((tm, 1), lambda i: (i, 0)), row, row, row],
        out_shape=[
            jax.ShapeDtypeStruct((S, D), F32),
            jax.ShapeDtypeStruct((S, D), BF16),
            jax.ShapeDtypeStruct((S, D), F32),
            jax.ShapeDtypeStruct((S, 1), F32),
            jax.ShapeDtypeStruct((S, D), BF16),
            jax.ShapeDtypeStruct((S, D), F32),
            jax.ShapeDtypeStruct((S, D), F32),
        ],
        compiler_params=_cp(("parallel",), 48),
    )(ya, yg, glog, bgate, h1, wabt, wgb, wo, ln_g, ln_b)


def _merge_bwd(dh2, xhat, rstd, ln_g, bra, brg, glog, bgate, wabt, wgb, wo, tm=256):
    def body(dh_ref, xh_ref, rs_ref, g_ref, bra_ref, brg_ref, gl_ref, bg_ref, wab_ref, wgb_ref, wo_ref,
             dr_ref, drb_ref, dlog_ref, dba_ref, dbg_ref, dya_ref, dyg_ref, dbgate_ref, dg_ref, dbias_ref):
        i = pl.program_id(0)
        dh = dh_ref[...]
        xh = xh_ref[...]
        dr = _ln_bwd(dh, xh, rs_ref[...], g_ref[...])
        drb = dr.astype(BF16)
        dr_ref[...] = dr
        drb_ref[...] = drb
        dmerged = _dot_nt(drb, wo_ref[...])
        gates = _sigmoid(gl_ref[...] + bg_ref[...])
        g0, g1 = gates[:, :D], gates[:, D:]
        dl0 = dmerged * bra_ref[...] * g0 * (1.0 - g0)
        dl1 = dmerged * brg_ref[...] * g1 * (1.0 - g1)
        dlog_ref[:, :D] = dl0.astype(BF16)
        dlog_ref[:, D:] = dl1.astype(BF16)
        dba = (dmerged * g0).astype(BF16)
        dbg = (dmerged * g1).astype(BF16)
        dba_ref[...] = dba
        dbg_ref[...] = dbg
        dya_ref[...] = _dot_nn(dba, wab_ref[...])
        dyg_ref[...] = _dot_nt(dbg, wgb_ref[...])
        s0, s1 = _colsum(dl0), _colsum(dl1)
        sg, sb = _colsum(dh * xh), _colsum(dh)

        @pl.when(i == 0)
        def _():
            dbgate_ref[:, :D] = s0
            dbgate_ref[:, D:] = s1
            dg_ref[...] = sg
            dbias_ref[...] = sb

        @pl.when(i > 0)
        def _():
            dbgate_ref[:, :D] += s0
            dbgate_ref[:, D:] += s1
            dg_ref[...] += sg
            dbias_ref[...] += sb

    row = pl.BlockSpec((tm, D), lambda i: (i, 0))
    vec = pl.BlockSpec((1, D), lambda i: (0, 0))
    wide = pl.BlockSpec((tm, 2 * D), lambda i: (i, 0))
    full = lambda shape: pl.BlockSpec(shape, lambda i: (0, 0))
    return pl.pallas_call(
        body,
        name="merge_bwd",
        grid=(S // tm,),
        in_specs=[row, row, pl.BlockSpec((tm, 1), lambda i: (i, 0)), vec, row, row,
                  pl.BlockSpec((tm, 2 * D), lambda i: (i, glog.shape[1] // (2 * D) - 1)),
                  full((1, 2 * D)), full((D, GW)), full((D, D)), full((D, D))],
        out_specs=[row, row, wide, row, row, pl.BlockSpec((tm, GW), lambda i: (i, 0)), row, full((1, 2 * D)), vec, vec],
        out_shape=[
            jax.ShapeDtypeStruct((S, D), F32),
            jax.ShapeDtypeStruct((S, D), BF16),
            jax.ShapeDtypeStruct((S, 2 * D), BF16),
            jax.ShapeDtypeStruct((S, D), BF16),
            jax.ShapeDtypeStruct((S, D), BF16),
            jax.ShapeDtypeStruct((S, GW), F32),
            jax.ShapeDtypeStruct((S, D), F32),
            jax.ShapeDtypeStruct((1, 2 * D), F32),
            jax.ShapeDtypeStruct((1, D), F32),
            jax.ShapeDtypeStruct((1, D), F32),
        ],
        compiler_params=_cp(("arbitrary",), 48),
    )(dh2, xhat, rstd, ln_g, bra, brg, glog, bgate, wabt, wgb, wo)


def _loss_head(h3, target, tm=512):
    def body(h_ref, t_ref, d_ref, l_ref):
        i = pl.program_id(0)
        e = h_ref[...] - t_ref[...]
        d_ref[...] = e * (1.0 / D)
        part = jnp.sum(_colsum(e * e), axis=1, keepdims=True) * (0.5 / D)

        @pl.when(i == 0)
        def _():
            l_ref[...] = part

        @pl.when(i > 0)
        def _():
            l_ref[...] += part

    row = pl.BlockSpec((tm, D), lambda i: (i, 0))
    return pl.pallas_call(
        body,
        name="loss_head",
        grid=(S // tm,),
        in_specs=[row, row],
        out_specs=[row, pl.BlockSpec((1, 1), lambda i: (0, 0))],
        out_shape=[jax.ShapeDtypeStruct((S, D), F32), jax.ShapeDtypeStruct((1, 1), F32)],
        compiler_params=_cp(("arbitrary",)),
    )(h3, target)


def _tie(x, dep):
    if dep is None:
        return x
    return x + dep[0, 0].astype(x.dtype)


def _local_step(x, pos_col, target, get_w, p, emit):
    w = get_w("ffn1", None)
    a1, b1, hm1 = _ffn_up(x, w["g1"], w["u1"], "ffn1_up")
    h1, h1b, xh1, rs1 = _ffn_down(hm1, w["d1"], x, p["ln1_g"], p["ln1_b"], "ffn1_down")

    w.update(get_w("win", h1b))
    qkv = _matmul(h1b, w["win"], "nt", F32, 1024, 1536, D, "proj_qkv", b_off=0, n_out=QKV_W)
    z = glog = _matmul(h1b, w["win"], "nt", F32, 1024, 512, D, "proj_zg", b_off=QKV_W // 512, n_out=4 * D)

    half = jnp.arange(0, HEAD_DIM, 2, dtype=F32) / HEAD_DIM
    inv_freq = ROPE_THETA ** (-half)
    invf = jnp.tile(inv_freq, 4).reshape(1, BLK)
    sign = jnp.tile(jnp.concatenate([-jnp.ones((32,), F32), jnp.ones((32,), F32)]), 2).reshape(1, BLK)
    cos_t, sin_t = _rope_tables(pos_col, invf, sign)

    os_, lses = [], []
    for gi in range(3):
        o, lse = _attn_fwd(qkv, cos_t, sin_t, gi)
        os_.append(o)
        lses.append(lse)
    ya = _combine_fwd(os_, lses)
    bst = p["gmlp_b_s"].T
    yg = _gmlp_fwd(z, p["gmlp_w_s"], bst, p["gmlp_ln_g"], p["gmlp_ln_b"])
    w.update(get_w("mix", yg))
    h2, h2b, xh2, rs2, merged, bra, brg = _merge_fwd(ya, yg, glog, p["b_gates"], h1, w["ab"], w["gb"], w["o"],
                                                      p["ln2_g"], p["ln2_b"])
    w.update(get_w("ffn2", h2b))
    a2, b2, hm2 = _ffn_up(h2b, w["g2"], w["u2"], "ffn2_up")
    h3, _, xh3, rs3 = _ffn_down(hm2, w["d2"], h2, p["ln3_g"], p["ln3_b"], "ffn2_down")
    dh3, loss = _loss_head(h3, target)

    gp = {}
    dr3, df2, da2, db2, gp["ln3_g"], gp["ln3_b"] = _ffn_bwd_mid(dh3, None, xh3, rs3, p["ln3_g"], a2, b2, w["d2"],
                                                                "ffn2_bwd_mid")
    tok = emit("ffn2", {
        "g2": _matmul(da2, h2b, "tn", BF16, 1408, D, S, "wgrad_g2"),
        "u2": _matmul(db2, h2b, "tn", BF16, 1408, D, S, "wgrad_u2"),
        "d2": _matmul(hm2, df2, "tn", BF16, 1408, D, S, "wgrad_d2")})
    dh2 = _ffn_bwd_dx(dr3, da2, db2, w["g2"], w["u2"], "ffn2_bwd_dx")

    (dr2, dr2b, dlog, dba, dbg, dya, dyg, gp["b_gates"], gp["ln2_g"], gp["ln2_b"]) = _merge_bwd(
        dh2, xh2, rs2, _tie(p["ln2_g"], tok), bra, brg, glog, p["b_gates"], w["ab"], w["gb"], w["o"])
    tok = emit("mix", {
        "o": _matmul(merged, dr2b, "tn", BF16, 512, D, S, "wgrad_o"),
        "ab": _matmul(dba, ya, "tn", BF16, 512, GW, S, "wgrad_ab"),
        "gb": _matmul(yg, dbg, "tn", BF16, 512, D, S, "wgrad_gb")})

    seg = (jnp.arange(GW)[:, None] // HEAD_DIM == jnp.arange(GW)[None, :] // HEAD_DIM).astype(BF16)
    do0, do1, do2, c0, c1, c2 = _combine_bwd(dya, os_, lses, _tie(seg, tok))
    dqkv = []
    for gi, (do, ct) in enumerate(((do0, c0), (do1, c1), (do2, c2))):
        dqkv.append(_attn_bwd(qkv, cos_t, sin_t, do, lses[gi], ct, gi))
    dz, gp["gmlp_w_s"], dbst, gp["gmlp_ln_g"], gp["gmlp_ln_b"] = _gmlp_bwd(
        z, dyg, p["gmlp_w_s"], bst, p["gmlp_ln_g"], p["gmlp_ln_b"])
    gp["gmlp_b_s"] = dbst.T
    dproj = jnp.concatenate(
        [dqkv[gi][0] for gi in range(3)] + [dqkv[gi][1].astype(BF16) for gi in range(3)]
        + [dqkv[gi][2].astype(BF16) for gi in range(3)] + [dz, dlog], axis=1)
    tok = emit("win", {"win": _matmul(dproj, h1b, "tn", BF16, 512, D, S, "wgrad_win")})
    dh1m = _matmul(dproj, w["win"], "nn", F32, 512, D, IN_W // 4, "dproj_to_dh1", dep=tok)

    dr1, df1, da1, db1, gp["ln1_g"], gp["ln1_b"] = _ffn_bwd_mid(dr2, dh1m, xh1, rs1, p["ln1_g"], a1, b1, w["d1"],
                                                                "ffn1_bwd_mid")
    tok = emit("small", gp)
    tok = emit("g1", {"g1": _matmul(da1, x, "tn", BF16, 1408, D, S, "wgrad_g1", dep=tok)})
    tok = emit("u1", {"u1": _matmul(db1, x, "tn", BF16, 1408, D, S, "wgrad_u1", dep=tok)})
    tok = emit("d1", {"d1": _matmul(hm1, df1, "tn", BF16, 1408, D, S, "wgrad_d1", dep=tok)})
    dx = _ffn_bwd_dx(dr1, da1, db1, w["g1"], w["u1"], "ffn1_bwd_dx", dep=tok)
    return loss, dx, gp


_FLIPS = [(mx, my, mc) for mx in (0, 1) for my in (0, 1) for mc in (0, 1)][1:]
_GROUPS = {"ffn1": ("g1", "u1", "d1"), "win": ("win",), "mix": ("ab", "gb", "o"), "ffn2": ("g2", "u2", "d2")}
_SCATTERS = {"ffn2": ("g2", "u2", "d2"), "mix": ("ab", "gb", "o"), "win": ("win",), "g1": ("g1",), "u1": ("u1",), "d1": ("d1",)}
_TWO_STAGE = ("win", "g1", "u1", "d1")
_HBM = pl.BlockSpec(memory_space=pltpu.HBM)
_SEM = pl.BlockSpec(memory_space=pltpu.SEMAPHORE)
_EFFECT = pltpu.SideEffectType.DATAFLOW_SIDE_EFFECTING


def _me():
    return 4 * lax.axis_index("x") + 2 * lax.axis_index("y") + lax.axis_index("c")


def _copies_start(bufs, copies, n_sem, name, after=None):
    nb = len(bufs)
    n_after = 0 if after is None else 1

    def body(*refs):
        b = refs[:nb]
        send_sems, recv_sems = refs[nb + n_after], refs[nb + n_after + 1]
        token = refs[-1]
        x, y, c = lax.axis_index("x"), lax.axis_index("y"), lax.axis_index("c")
        me = 4 * x + 2 * y + c
        for si, s_slot, di, d_slot, (mx, my, mc), sem in copies:
            s_idx, d_idx = s_slot(me), d_slot(me)
            pltpu.make_async_remote_copy(
                src_ref=b[si] if s_idx is None else b[si].at[s_idx],
                dst_ref=b[di] if d_idx is None else b[di].at[d_idx],
                send_sem=send_sems.at[sem], recv_sem=recv_sems.at[sem],
                device_id=(x ^ mx, y ^ my, c ^ mc), device_id_type=MESH).start()
        token[...] = jnp.zeros_like(token)

    ins = [pltpu.with_memory_space_constraint(a, pltpu.HBM) for a in bufs]
    outs = pl.pallas_call(
        body,
        name=name,
        in_specs=[_HBM] * nb + [pl.BlockSpec(memory_space=pl.ANY)] * n_after,
        out_specs=[_SEM, _SEM] + [_HBM] * nb + [pl.BlockSpec(memory_space=pltpu.VMEM)],
        out_shape=[pltpu.SemaphoreType.DMA((n_sem,)), pltpu.SemaphoreType.DMA((n_sem,))]
        + [pltpu.HBM(a.shape, a.dtype) for a in ins] + [jax.ShapeDtypeStruct((8, 128), F32)],
        input_output_aliases={k: 2 + k for k in range(nb)},
        compiler_params=pltpu.CompilerParams(has_side_effects=_EFFECT),
    )(*ins, *([] if after is None else [after]))
    return outs[0], outs[1], list(outs[2:2 + nb]), outs[-1]


def _copies_wait(started, waits, after, name):
    send_sems, recv_sems, bufs, _ = started
    nb = len(bufs)

    def body(*refs):
        b = refs[:nb]
        ss, rs = refs[nb], refs[nb + 1]
        me3 = (lax.axis_index("x"), lax.axis_index("y"), lax.axis_index("c"))
        for bi, n_blocks, sem, is_send in waits:
            blocks = b[bi].at[pl.ds(0, n_blocks)]
            cp = pltpu.make_async_remote_copy(src_ref=blocks, dst_ref=blocks, send_sem=ss.at[sem], recv_sem=rs.at[sem],
                                              device_id=me3, device_id_type=MESH)
            if is_send:
                cp.wait_send()
            else:
                cp.wait_recv()

    return pl.pallas_call(
        body,
        name=name,
        in_specs=[_HBM] * nb + [_SEM, _SEM, pl.BlockSpec(memory_space=pl.ANY)],
        out_specs=[_HBM] * nb,
        out_shape=[pltpu.HBM(a.shape, a.dtype) for a in bufs],
        input_output_aliases={k: k for k in range(nb)},
        compiler_params=pltpu.CompilerParams(has_side_effects=_EFFECT),
    )(*bufs, send_sems, recv_sems, after)


def _landing(own, me):
    return lax.dynamic_update_slice(lax.empty((N_DEV,) + own.shape[1:], own.dtype), own, (me, 0, 0))


_SIBLING = (0, 0, 1)
_OTHER_CHIPS = ((1, 0, 0), (0, 1, 0), (1, 1, 0))


def _bits(flip):
    return 4 * flip[0] + 2 * flip[1] + flip[2]


def _gather_send(shards, grp, after):
    nw = len(shards)
    me = _me()
    bufs = list(shards) + [_landing(a[None], me) for a in shards]
    copies = []
    for k in range(nw):
        copies.append((k, lambda me: None, nw + k, lambda me: me, _SIBLING, nw + k))
        copies += [(k, lambda me: None, nw + k, lambda me: me, f, k) for f in _OTHER_CHIPS]
    return _copies_start(bufs, copies, 2 * nw, "gather_send_" + grp, after)


def _gather_finish(started, grp, after):
    nw = len(started[2]) // 2
    waits = []
    for k in range(nw):
        waits += [(nw + k, 3, k, True), (nw + k, 1, nw + k, True), (nw + k, 3, k, False), (nw + k, 1, nw + k, False)]
    lands = list(_copies_wait(started, waits, after, "gather_arrived_" + grp)[nw:])
    copies = []
    for k in range(nw):
        for f in _OTHER_CHIPS:
            slot = functools.partial(lambda me, bits: me ^ bits, bits=_bits(f))
            copies.append((k, slot, k, slot, _SIBLING, k))
    passed = _copies_start(lands, copies, nw, "gather_pass_" + grp)
    waits = [(k, 3, k, is_send) for k in range(nw) for is_send in (True, False)]
    return _copies_wait(passed, waits, passed[3], "gather_done_" + grp)


def _scatter_send(parts, grp, after=None):
    nw = len(parts)
    me = _me()
    bufs = list(parts) + [_landing(lax.dynamic_index_in_dim(a, me, 0, keepdims=True), me) for a in parts]
    copies = []
    for k in range(nw):
        for f in _FLIPS:
            to = functools.partial(lambda me, bits: me ^ bits, bits=_bits(f))
            copies.append((k, to, nw + k, lambda me: me, f, k))
    return _copies_start(bufs, copies, nw, "scatter_send_" + grp, after)


def _scatter_finish(started, grp, after):
    nw = len(started[2]) // 2
    waits = [(nw + k, N_DEV - 1, k, is_send) for k in range(nw) for is_send in (True, False)]
    return _copies_wait(started, waits, after, "scatter_done_" + grp)[nw:]


N_CHIP = N_DEV // 2


def _pair_sum(part, other, key):
    _, r, c = part.shape
    tm = _ADAM_ROWS.get(r, r)
    core = lax.axis_index("c").astype(jnp.int32).reshape(1)

    def body(core_ref, a_ref, b_ref, o_ref):
        o_ref[...] = (a_ref[...].astype(F32) + b_ref[...].astype(F32)).astype(BF16)

    return pl.pallas_call(
        body,
        name="pair_sum_" + key,
        grid_spec=pltpu.PrefetchScalarGridSpec(
            num_scalar_prefetch=1,
            grid=(N_CHIP, r // tm),
            in_specs=[pl.BlockSpec((None, tm, c), lambda q, i, core_ref: (2 * q + core_ref[0], i, 0)),
                      pl.BlockSpec((None, tm, c), lambda q, i, core_ref: (q, i, 0))],
            out_specs=pl.BlockSpec((None, tm, c), lambda q, i, core_ref: (q, i, 0)),
        ),
        out_shape=jax.ShapeDtypeStruct((N_CHIP, r, c), BF16),
        compiler_params=_cp(("parallel", "parallel")),
    )(core, part, other)


def _scatter2_send(part, key, dep=None):
    me = _me()
    _, r, c = part.shape
    swap = []
    for q in range(N_CHIP):
        src = functools.partial(lambda me, q: 2 * q + 1 - me % 2, q=q)
        swap.append((0, src, 1, functools.partial(lambda me, q: q, q=q), _SIBLING, 0))
    started = _copies_start([part, lax.empty((N_CHIP, r, c), part.dtype)], swap, 1, "scatter_swap_" + key, dep)
    waits = [(1, N_CHIP, 0, True), (1, N_CHIP, 0, False)]
    part, other = _copies_wait(started, waits, started[3], "scatter_swapped_" + key)
    pair = _pair_sum(part, other, key)
    chip = me // 2
    land = lax.dynamic_update_slice(lax.empty((N_CHIP, r, c), pair.dtype),
                                    lax.dynamic_index_in_dim(pair, chip, 0, keepdims=True), (chip, 0, 0))
    copies = []
    for f in _OTHER_CHIPS:
        to = functools.partial(lambda me, bits: (me ^ bits) // 2, bits=_bits(f))
        copies.append((0, to, 1, lambda me: me // 2, f, 0))
    return _copies_start([pair, land], copies, 1, "scatter_send_" + key)


def _scatter2_finish(started, key, after):
    waits = [(1, N_CHIP - 1, 0, True), (1, N_CHIP - 1, 0, False)]
    return _copies_wait(started, waits, after, "scatter_done_" + key)[1]


def _allgather_send(block, name, after=None):
    me = _me()
    copies = [(0, lambda me: None, 1, lambda me: me, f, 0) for f in _FLIPS]
    return _copies_start([block, _landing(block[None], me)], copies, 1, name, after)


def _allgather_finish(started, name, after):
    waits = [(1, N_DEV - 1, 0, True), (1, N_DEV - 1, 0, False)]
    return _copies_wait(started, waits, after, name)[1]


_ADAM_ROWS = {352: 176, 1088: 272}


def _sum_adamw(parts, wv, m, v, name):
    n_parts, r, c = parts.shape
    tm = _ADAM_ROWS.get(r, r)
    assert r % tm == 0 and wv.shape == (r, c)

    def body(p_ref, w_ref, m_ref, v_ref, g_ref, d_ref, mo_ref, vo_ref):
        gv = p_ref[0].astype(F32)
        for j in range(1, n_parts):
            gv = gv + p_ref[j].astype(F32)
        g_ref[...] = gv
        mn = ADAM_B1 * m_ref[...] + (1.0 - ADAM_B1) * gv
        vn = ADAM_B2 * v_ref[...] + (1.0 - ADAM_B2) * (gv * gv)
        m_hat = mn / (1.0 - ADAM_B1 ** ADAM_STEP)
        v_hat = vn / (1.0 - ADAM_B2 ** ADAM_STEP)
        d_ref[...] = -ADAM_LR * (m_hat / (jnp.sqrt(v_hat) + ADAM_EPS) + ADAM_WD * w_ref[...])
        mo_ref[...] = mn
        vo_ref[...] = vn

    sp = pl.BlockSpec((tm, c), lambda i: (i, 0))
    return pl.pallas_call(
        body,
        name=name,
        grid=(r // tm,),
        in_specs=[pl.BlockSpec((n_parts, tm, c), lambda i: (0, i, 0))] + [sp] * 3,
        out_specs=[sp] * 4,
        out_shape=[jax.ShapeDtypeStruct((r, c), F32)] * 4,
        compiler_params=_cp(("parallel",), 48),
    )(parts, wv, m, v)


_WEIGHTS = ["ffn1_w_gate", "ffn1_w_up", "ffn1_w_down", "ln1_g", "ln1_b", "w_in", "b_gates", "gmlp_ln_g", "gmlp_ln_b",
            "gmlp_w_s", "gmlp_b_s", "w_attn_branch", "w_gmlp_branch", "w_out", "ln2_g", "ln2_b", "ffn2_w_gate",
            "ffn2_w_up", "ffn2_w_down", "ln3_g", "ln3_b"]
_BIG_OF = {"ffn1_w_gate": ("g1", True), "ffn1_w_up": ("u1", True), "ffn1_w_down": ("d1", False), "w_in": ("win", True),
           "w_attn_branch": ("ab", True), "w_gmlp_branch": ("gb", False), "w_out": ("o", False),
           "ffn2_w_gate": ("g2", True), "ffn2_w_up": ("u2", True), "ffn2_w_down": ("d2", False)}
_SMALL = [n for n in _WEIGHTS if n not in _BIG_OF]
_SMALL_ROWS = {"gmlp_w_s": 128, "b_gates": 2}
_SMALL_SLOT = 8


def _pack_small(d):
    rows = []
    for n in _SMALL:
        r = d[n].reshape(-1, D)
        slot = max(r.shape[0], _SMALL_SLOT)
        rows.append(jnp.pad(r, ((0, slot - r.shape[0]), (0, 0))))
    return jnp.concatenate(rows, axis=0)


def _unpack_small(packed, shapes):
    out, at = {}, 0
    for n in _SMALL:
        k = _SMALL_ROWS.get(n, 1)
        out[n] = packed[at:at + k].reshape(shapes[n])
        at += max(k, _SMALL_SLOT)
    return out


def kernel(x, positions, ffn1_w_gate, ffn1_w_up, ffn1_w_down, ln1_g, ln1_b, w_in, b_gates, gmlp_ln_g, gmlp_ln_b, gmlp_w_s, gmlp_b_s, w_attn_branch, w_gmlp_branch, w_out, ln2_g, ln2_b, ffn2_w_gate, ffn2_w_up, ffn2_w_down, ln3_g, ln3_b, loss_target, m_ffn1_w_gate, m_ffn1_w_up, m_ffn1_w_down, m_ln1_g, m_ln1_b, m_w_in, m_b_gates, m_gmlp_ln_g, m_gmlp_ln_b, m_gmlp_w_s, m_gmlp_b_s, m_w_attn_branch, m_w_gmlp_branch, m_w_out, m_ln2_g, m_ln2_b, m_ffn2_w_gate, m_ffn2_w_up, m_ffn2_w_down, m_ln3_g, m_ln3_b, v_ffn1_w_gate, v_ffn1_w_up, v_ffn1_w_down, v_ln1_g, v_ln1_b, v_w_in, v_b_gates, v_gmlp_ln_g, v_gmlp_ln_b, v_gmlp_w_s, v_gmlp_b_s, v_w_attn_branch, v_w_gmlp_branch, v_w_out, v_ln2_g, v_ln2_b, v_ffn2_w_gate, v_ffn2_w_up, v_ffn2_w_down, v_ln3_g, v_ln3_b):
    args = dict(locals())
    wts = {n: args[n] for n in _WEIGHTS}
    ms = {n: args["m_" + n] for n in _WEIGHTS}
    vs = {n: args["v_" + n] for n in _WEIGHTS}

    name_of = {key: (n, tr) for n, (key, tr) in _BIG_OF.items()}

    started, tok = {}, None
    for grp, keys in _GROUPS.items():
        shards = []
        for key in keys:
            n, tr = name_of[key]
            s2 = wts[n][0]
            shards.append((s2.T if tr else s2).astype(BF16))
        started[grp] = _gather_send(shards, grp, tok)
        tok = started[grp][3]
    all_started = tok

    def get_w(grp, after):
        lands = _gather_finish(started[grp], grp, all_started if after is None else after)
        return {key: g.reshape(-1, g.shape[-1]) for key, g in zip(_GROUPS[grp], lands)}

    sent = {}

    def emit(grp, grads):
        if grp == "small":
            sent[grp] = _allgather_send(_pack_small(grads), "small_grads_send")
        else:
            parts = [grads[key].reshape(N_DEV, -1, grads[key].shape[-1]) for key in _SCATTERS[grp]]
            sent[grp] = _scatter2_send(parts[0], grp) if grp in _TWO_STAGE else _scatter_send(parts, grp)
        return sent[grp][3]

    p = {n: (wts[n][0] if n in ("gmlp_w_s", "gmlp_b_s") else wts[n]) for n in _SMALL}
    loss, dx, gp = _local_step(x[0], positions.reshape(S, 1), loss_target[0], get_w, p, emit)
    loss = lax.psum(loss[0, 0], ("x", "y", "c"))

    grads, deltas, new_m, new_v = {}, {}, {}, {}
    after = dx
    for grp in ("ffn2", "mix", "win", "small", "g1", "u1", "d1"):
        if grp == "small":
            parts = _allgather_finish(sent[grp], "small_grads_done", after)
            outs = _sum_adamw(parts, *[_pack_small({n: d[n] for n in _SMALL}) for d in (wts, ms, vs)], "update_small")
            shapes = {n: wts[n].shape for n in _SMALL}
            for dst, packed in zip((grads, deltas, new_m, new_v), outs):
                dst.update(_unpack_small(packed, shapes))
            after = outs[1]
            continue
        arrived = ([_scatter2_finish(sent[grp], grp, after)] if grp in _TWO_STAGE
                   else _scatter_finish(sent[grp], grp, after))
        for key, part in zip(_SCATTERS[grp], arrived):
            n, tr = name_of[key]
            outs = _sum_adamw(part, *[(d[n][0].T if tr else d[n][0]) for d in (wts, ms, vs)], "update_" + key)
            for dst, o in zip((grads, deltas, new_m, new_v), outs):
                dst[n] = (o.T if tr else o)[None]
            after = outs[1]

    return (loss, dx[None], *[grads[n] for n in _WEIGHTS], *[deltas[n] for n in _WEIGHTS],
            *[new_m[n] for n in _WEIGHTS], *[new_v[n] for n in _WEIGHTS])
```

```python
import functools
import math

import jax
import jax.numpy as jnp
from jax import lax
from jax.experimental import pallas as pl
from jax.experimental.pallas import tpu as pltpu

F32 = jnp.float32
BF16 = jnp.bfloat16

N_DEV = 8
D = 1024
S = 2048
F = 2816
HEAD_DIM = 64
HEADS = 8
GW = HEADS * HEAD_DIM
PATTERN_DILATIONS = (1, 4, 16)
BLK = 128
QKV_W = 3 * 3 * GW
IN_W = QKV_W + 2 * D + 2 * D
DPROJ_W = 5 * 2 * D
DPROJ_Z_COL, DPROJ_G_COL = 3, 4


def _dproj_tile(t):
    three = jnp.int32(3)
    return jnp.where(t < 9, lax.rem(t, three) * 3 + lax.div(t, three), t + 3)
ROPE_THETA = 10000.0
ALPHA = 2.0 ** 0.25
LN_EPS = 1e-5
ADAM_LR, ADAM_B1, ADAM_B2, ADAM_EPS, ADAM_WD, ADAM_STEP = 0.001, 0.9, 0.999, 1e-08, 0.01, 10
NEG = -1e30
MESH = pl.DeviceIdType.MESH


def _cp(sem=None, vmem_mb=None):
    kw = {}
    if sem is not None:
        kw["dimension_semantics"] = sem
    if vmem_mb is not None:
        kw["vmem_limit_bytes"] = vmem_mb << 20
    return pltpu.CompilerParams(**kw)


def _dot_nn(a, b):
    return lax.dot_general(a, b, (((1,), (0,)), ((), ())), preferred_element_type=F32)


def _dot_nt(a, b):
    return lax.dot_general(a, b, (((1,), (1,)), ((), ())), preferred_element_type=F32)


def _dot_tn(a, b):
    return lax.dot_general(a, b, (((0,), (0,)), ((), ())), preferred_element_type=F32)


def _ln_fwd(r, g, b):
    mu = jnp.mean(r, axis=-1, keepdims=True)
    xc = r - mu
    var = jnp.mean(xc * xc, axis=-1, keepdims=True)
    rstd = lax.rsqrt(var + LN_EPS)
    xhat = xc * rstd
    return xhat * g + b, xhat, rstd


def _ln_bwd(dh, xhat, rstd, g):
    dxh = dh * g
    m1 = jnp.mean(dxh, axis=-1, keepdims=True)
    m2 = jnp.mean(dxh * xhat, axis=-1, keepdims=True)
    return rstd * (dxh - m1 - xhat * m2)


def _sigmoid(x):
    return 1.0 / (1.0 + jnp.exp(-x))


def _colsum(x):
    return jnp.sum(x, axis=0, keepdims=True)


def _matmul(a, b, mode, out_dtype, tm, tn, tk, name, b_off=0, n_out=None, dep=None, a_map=None, m_out=None):
    n_dep = 0 if dep is None else 1
    a_map = a_map or (lambda t: t)
    if mode == "nn":
        m, k = a.shape[0], b.shape[0]
        n = b.shape[1]
    elif mode == "nt":
        m, k = a.shape
        n = n_out if n_out is not None else b.shape[0]
    else:
        k, m = a.shape[0], m_out or a.shape[1]
        n = b.shape[1]
    nk = k // tk
    assert m % tm == 0 and n % tn == 0 and k % tk == 0
    dot = {"nn": _dot_nn, "nt": _dot_nt, "tn": _dot_tn}[mode]

    def body(a_ref, b_ref, *rest):
        o_ref, scr = rest[n_dep], rest[n_dep + 1:]
        r = dot(a_ref[...].astype(BF16), b_ref[...].astype(BF16))
        if nk == 1:
            o_ref[...] = r.astype(out_dtype)
        else:
            acc = scr[0]
            kk = pl.program_id(2)

            @pl.when(kk == 0)
            def _():
                acc[...] = r

            @pl.when(kk > 0)
            def _():
                acc[...] += r

            @pl.when(kk == nk - 1)
            def _():
                o_ref[...] = acc[...].astype(out_dtype)

    if mode == "nn":
        a_spec = pl.BlockSpec((tm, tk), lambda i, j, kk: (i, a_map(kk)))
        b_spec = pl.BlockSpec((tk, tn), lambda i, j, kk: (kk, j))
    elif mode == "nt":
        a_spec = pl.BlockSpec((tm, tk), lambda i, j, kk: (i, kk))
        b_spec = pl.BlockSpec((tn, tk), lambda i, j, kk: (j + b_off, kk))
    else:
        a_spec = pl.BlockSpec((tk, tm), lambda i, j, kk: (kk, a_map(i)))
        b_spec = pl.BlockSpec((tk, tn), lambda i, j, kk: (kk, j))
    return pl.pallas_call(
        body,
        name=name,
        grid=(m // tm, n // tn, nk),
        in_specs=[a_spec, b_spec] + [pl.BlockSpec(memory_space=pl.ANY)] * n_dep,
        out_specs=pl.BlockSpec((tm, tn), lambda i, j, kk: (i, j)),
        out_shape=jax.ShapeDtypeStruct((m, n), out_dtype),
        scratch_shapes=[] if nk == 1 else [pltpu.VMEM((tm, tn), F32)],
        compiler_params=_cp(("parallel", "parallel", "arbitrary"), 56),
    )(a, b, *([] if dep is None else [dep]))


def _ffn_up(x, wgt, wut, name, tm=512, tn=1408):
    def body(x_ref, wg_ref, wu_ref, a_ref, b_ref, hm_ref):
        xb = x_ref[...].astype(BF16)
        a = _dot_nt(xb, wg_ref[...])
        b = _dot_nt(xb, wu_ref[...])
        a_ref[...] = a.astype(BF16)
        b_ref[...] = b.astype(BF16)
        hm_ref[...] = ((a * _sigmoid(a)) * b).astype(BF16)

    wsp = pl.BlockSpec((tn, D), lambda i, j: (j, 0))
    mid = pl.BlockSpec((tm, tn), lambda i, j: (i, j))
    return pl.pallas_call(
        body,
        name=name,
        grid=(S // tm, F // tn),
        in_specs=[pl.BlockSpec((tm, D), lambda i, j: (i, 0)), wsp, wsp],
        out_specs=[mid, mid, mid],
        out_shape=[jax.ShapeDtypeStruct((S, F), BF16)] * 3,
        compiler_params=_cp(("parallel", "arbitrary"), 56),
    )(x, wgt, wut)


def _ffn_down(hm, wd, x, ln_g, ln_b, name, tm=256):
    def body(hm_ref, wd_ref, x_ref, g_ref, b_ref, h_ref, hb_ref, xh_ref, rs_ref):
        r = ALPHA * x_ref[...] + 0.5 * _dot_nn(hm_ref[...], wd_ref[...])
        h, xhat, rstd = _ln_fwd(r, g_ref[...], b_ref[...])
        h_ref[...] = h
        hb_ref[...] = h.astype(BF16)
        xh_ref[...] = xhat
        rs_ref[...] = rstd

    row = pl.BlockSpec((tm, D), lambda i: (i, 0))
    vec = pl.BlockSpec((1, D), lambda i: (0, 0))
    return pl.pallas_call(
        body,
        name=name,
        grid=(S // tm,),
        in_specs=[pl.BlockSpec((tm, F), lambda i: (i, 0)), pl.BlockSpec((F, D), lambda i: (0, 0)), row, vec, vec],
        out_specs=[row, row, row, pl.BlockSpec((tm, 1), lambda i: (i, 0))],
        out_shape=[
            jax.ShapeDtypeStruct((S, D), F32),
            jax.ShapeDtypeStruct((S, D), BF16),
            jax.ShapeDtypeStruct((S, D), F32),
            jax.ShapeDtypeStruct((S, 1), F32),
        ],
        compiler_params=_cp(("parallel",), 56),
    )(hm, wd, x, ln_g, ln_b)


def _ffn_bwd_mid(dh_a, dh_b, xhat, rstd, ln_g, a, b, wd, name, tm=512, tn=1408):
    two = dh_b is not None

    def body(*refs):
        dha_ref = refs[0]
        dhb_ref = refs[1] if two else None
        (xh_ref, rs_ref, g_ref, a_ref, b_ref, wd_ref,
         dr_ref, df_ref, da_ref, db_ref, dg_ref, dbias_ref, df_scr) = refs[2 if two else 1:]
        i = pl.program_id(0)
        j = pl.program_id(1)

        @pl.when(j == 0)
        def _():
            dh = dha_ref[...]
            if two:
                dh = ALPHA * dh + dhb_ref[...]
            xhat = xh_ref[...]
            dr = _ln_bwd(dh, xhat, rs_ref[...], g_ref[...])
            dfb = (0.5 * dr).astype(BF16)
            dr_ref[...] = dr
            df_scr[...] = dfb
            df_ref[...] = dfb
            sg = _colsum(dh * xhat)
            sb = _colsum(dh)

            @pl.when(i == 0)
            def _():
                dg_ref[...] = sg
                dbias_ref[...] = sb

            @pl.when(i > 0)
            def _():
                dg_ref[...] += sg
                dbias_ref[...] += sb

        dhm = _dot_nt(df_scr[...], wd_ref[...])
        av = a_ref[...].astype(F32)
        bv = b_ref[...].astype(F32)
        sig = _sigmoid(av)
        da_ref[...] = (dhm * bv * (sig * (1.0 + av * (1.0 - sig)))).astype(BF16)
        db_ref[...] = (dhm * (av * sig)).astype(BF16)

    row = pl.BlockSpec((tm, D), lambda i, j: (i, 0))
    vec = pl.BlockSpec((1, D), lambda i, j: (0, 0))
    mid = pl.BlockSpec((tm, tn), lambda i, j: (i, j))
    ins = [dh_a] + ([dh_b] if two else []) + [xhat, rstd, ln_g, a, b, wd]
    in_specs = [row] * (2 if two else 1) + [row, pl.BlockSpec((tm, 1), lambda i, j: (i, 0)), vec, mid, mid,
                                            pl.BlockSpec((tn, D), lambda i, j: (j, 0))]
    return pl.pallas_call(
        body,
        name=name,
        grid=(S // tm, F // tn),
        in_specs=in_specs,
        out_specs=[row, row, mid, mid, vec, vec],
        out_shape=[
            jax.ShapeDtypeStruct((S, D), F32),
            jax.ShapeDtypeStruct((S, D), BF16),
            jax.ShapeDtypeStruct((S, F), BF16),
            jax.ShapeDtypeStruct((S, F), BF16),
            jax.ShapeDtypeStruct((1, D), F32),
            jax.ShapeDtypeStruct((1, D), F32),
        ],
        scratch_shapes=[pltpu.VMEM((tm, D), BF16)],
        compiler_params=_cp(("arbitrary", "arbitrary"), 56),
    )(*ins)


def _ffn_bwd_dx(dr, da, db, wgt, wut, name, tm=512, tk=1408, dep=None):
    nk = F // tk
    n_dep = 0 if dep is None else 1

    def body(dr_ref, da_ref, db_ref, wg_ref, wu_ref, *rest):
        dx_ref, acc = rest[n_dep], rest[n_dep + 1]
        kk = pl.program_id(1)
        part = _dot_nn(da_ref[...], wg_ref[...]) + _dot_nn(db_ref[...], wu_ref[...])

        @pl.when(kk == 0)
        def _():
            acc[...] = ALPHA * dr_ref[...] + part

        @pl.when(kk > 0)
        def _():
            acc[...] += part

        @pl.when(kk == nk - 1)
        def _():
            dx_ref[...] = acc[...]

    row = pl.BlockSpec((tm, D), lambda i, kk: (i, 0))
    mid = pl.BlockSpec((tm, tk), lambda i, kk: (i, kk))
    wsp = pl.BlockSpec((tk, D), lambda i, kk: (kk, 0))
    return pl.pallas_call(
        body,
        name=name,
        grid=(S // tm, nk),
        in_specs=[row, mid, mid, wsp, wsp] + [pl.BlockSpec(memory_space=pl.ANY)] * n_dep,
        out_specs=row,
        out_shape=jax.ShapeDtypeStruct((S, D), F32),
        scratch_shapes=[pltpu.VMEM((tm, D), F32)],
        compiler_params=_cp(("parallel", "arbitrary"), 56),
    )(dr, da, db, wgt, wut, *([] if dep is None else [dep]))


def _rope_tables(pos_col, invf, sign, tm=512):
    def body(p_ref, f_ref, s_ref, c_out, s_out):
        ang = p_ref[...].astype(F32) * f_ref[...]
        c_out[...] = jnp.cos(ang)
        s_out[...] = jnp.sin(ang) * s_ref[...]

    vec = pl.BlockSpec((1, BLK), lambda i: (0, 0))
    out = pl.BlockSpec((tm, BLK), lambda i: (i, 0))
    return pl.pallas_call(
        body,
        name="rope_tables",
        grid=(S // tm,),
        in_specs=[pl.BlockSpec((tm, 1), lambda i: (i, 0)), vec, vec],
        out_specs=[out, out],
        out_shape=[jax.ShapeDtypeStruct((S, BLK), F32)] * 2,
        compiler_params=_cp(("parallel",)),
    )(pos_col, invf, sign)


def _lane_lo(rows=BLK):
    return (lax.broadcasted_iota(jnp.int32, (rows, GW), 1) % HEAD_DIM) < (HEAD_DIM // 2)


def _swap_halves(t, lo):
    return jnp.where(lo, pltpu.roll(t, GW - HEAD_DIM // 2, 1), pltpu.roll(t, HEAD_DIM // 2, 1))


def _rope(t, cosf, sinf, lo):
    return t * cosf + _swap_halves(t, lo) * sinf


def _unrope(g, cosf, sinf, lo):
    return g * cosf + _swap_halves(g * sinf, lo)


def _tile4(v):
    return jnp.concatenate([v, v, v, v], axis=1)


def _band_mask(n):
    qi = lax.broadcasted_iota(jnp.int32, (BLK, 2 * BLK), 0)
    kj = lax.broadcasted_iota(jnp.int32, (BLK, 2 * BLK), 1)
    dist = qi + BLK - kj
    return (dist >= 0) & (dist <= BLK) & ((kj >= BLK) | (n >= 1))


ROWS = 256
LANES = 128


def _to_view(scr, y, dst_ref, d, dtype, col0=0):
    if d == 1:
        dst_ref[:, col0:col0 + GW] = y.astype(dtype)
        return
    for cb in range(GW // LANES):
        scr[cb][...] = y[:, cb * LANES:(cb + 1) * LANES]
    for r in range(d):
        for cb in range(GW // LANES):
            at = col0 + r * GW + cb * LANES
            dst_ref[:, at:at + LANES] = scr[cb][pl.ds(r, ROWS // d, stride=d), :].astype(dtype)


def _from_view(scr, src_ref, d):
    if d == 1:
        return src_ref[...].astype(F32)
    for r in range(d):
        for cb in range(GW // LANES):
            at = r * GW + cb * LANES
            scr[cb][pl.ds(r, ROWS // d, stride=d), :] = src_ref[:, at:at + LANES].astype(F32)
    return jnp.concatenate([scr[cb][...] for cb in range(GW // LANES)], axis=1)


def _view_spec(d):
    return pl.BlockSpec((ROWS // d, d * GW), lambda i: (i, 0))


_VIEW_SCRATCH = [pltpu.VMEM((ROWS, LANES), F32)] * (GW // LANES)


def _qkv_prep(qkv, cos_t, sin_t, gi):
    d = PATTERN_DILATIONS[gi]

    def body(q_ref, k_ref, v_ref, c_ref, s_ref, qo, ko, vo, *scr):
        lo = _lane_lo(ROWS)
        cf, sf = _tile4(c_ref[...]), _tile4(s_ref[...])
        _to_view(scr, _rope(q_ref[...], cf, sf, lo) * (HEAD_DIM ** -0.5), qo, d, BF16)
        _to_view(scr, _rope(k_ref[...], cf, sf, lo), ko, d, BF16)
        _to_view(scr, v_ref[...], vo, d, BF16)

    col = lambda c: pl.BlockSpec((ROWS, GW), lambda i: (i, c))
    tab = pl.BlockSpec((ROWS, BLK), lambda i: (i, 0))
    return pl.pallas_call(
        body,
        name=f"qkv_prep_g{gi}",
        grid=(S // ROWS,),
        in_specs=[col(gi), col(3 + gi), col(6 + gi), tab, tab],
        out_specs=[_view_spec(d)] * 3,
        out_shape=[jax.ShapeDtypeStruct((S // d, d * GW), BF16)] * 3,
        scratch_shapes=_VIEW_SCRATCH,
        compiler_params=_cp(("parallel",)),
    )(qkv, qkv, qkv, cos_t, sin_t)


def _attn_specs(d):
    cur = pl.BlockSpec((BLK, GW), lambda r, n: (n, r))
    prev = pl.BlockSpec((BLK, GW), lambda r, n: (jnp.maximum(n - 1, 0), r))
    return cur, prev


def _left_lanes():
    return lax.broadcasted_iota(jnp.int32, (BLK, LANES), 1) < HEAD_DIM


def _attn_fwd(qr, kr, vv, gi):
    d = PATTERN_DILATIONS[gi]
    sub = S // d
    cur, prev = _attn_specs(d)

    def body(q_ref, kc_ref, kp_ref, vc_ref, vp_ref, o_ref, l_ref):
        n = pl.program_id(1)
        mask = _band_mask(n)
        left = _left_lanes()
        zero = jnp.zeros((BLK, LANES), BF16)
        for pr in range(GW // LANES):
            cs = slice(pr * LANES, (pr + 1) * LANES)
            q2 = q_ref[:, cs]
            k2 = jnp.concatenate([kp_ref[:, cs], kc_ref[:, cs]], axis=0)
            v2 = jnp.concatenate([vp_ref[:, cs], vc_ref[:, cs]], axis=0)
            o_h, lse_h = [], []
            for side in (left, ~left):
                s = jnp.where(mask, _dot_nt(jnp.where(side, q2, zero), k2), NEG)
                m = jnp.max(s, axis=1, keepdims=True)
                p = jnp.exp(s - m)
                l = jnp.sum(p, axis=1, keepdims=True)
                o_h.append(_dot_nn((p / l).astype(BF16), v2))
                lse_h.append(m + jnp.log(l))
            o_ref[:, cs] = jnp.where(left, o_h[0], o_h[1])
            l_ref[:, cs] = jnp.where(left, lse_h[0], lse_h[1])

    return pl.pallas_call(
        body,
        name=f"attn_fwd_g{gi}",
        grid=(d, sub // BLK),
        in_specs=[cur, cur, prev, cur, prev],
        out_specs=[cur, cur],
        out_shape=[jax.ShapeDtypeStruct((sub, d * GW), F32)] * 2,
        compiler_params=_cp(("parallel", "parallel")),
    )(qr, kr, kr, vv, vv)


def _attn_bwd(qr, kr, vv, do, lse, cterm, gi):
    d = PATTERN_DILATIONS[gi]
    sub = S // d
    cur, prev = _attn_specs(d)

    def body(q_ref, kc_ref, kp_ref, vc_ref, vp_ref, do_ref, l_ref, c_ref, dq_ref, dk_ref, dv_ref):
        n = pl.program_id(1)
        mask = _band_mask(n)
        left = _left_lanes()
        zero = jnp.zeros((BLK, LANES), BF16)
        here = pl.ds(pl.multiple_of(n * BLK, BLK), BLK)
        before = pl.ds(pl.multiple_of(jnp.maximum(n - 1, 0) * BLK, BLK), BLK)
        for pr in range(GW // LANES):
            cs = slice(pr * LANES, (pr + 1) * LANES)
            q2 = q_ref[:, cs]
            k2 = jnp.concatenate([kp_ref[:, cs], kc_ref[:, cs]], axis=0)
            v2 = jnp.concatenate([vp_ref[:, cs], vc_ref[:, cs]], axis=0)
            do2 = do_ref[:, cs]
            dq_h, dk2, dv2 = [], None, None
            for h, side in enumerate((left, ~left)):
                at = pr * LANES + h * HEAD_DIM
                qm = jnp.where(side, q2, zero)
                dom = jnp.where(side, do2, zero)
                s = jnp.where(mask, _dot_nt(qm, k2), NEG)
                p = jnp.exp(s - l_ref[:, at:at + 1])
                ds = (p * (_dot_nt(dom, v2) - c_ref[:, at:at + 1])).astype(BF16)
                dq_h.append(_dot_nn(ds, k2))
                dk_part = _dot_tn(ds, qm)
                dv_part = _dot_tn(p.astype(BF16), dom)
                dk2 = dk_part if dk2 is None else dk2 + dk_part
                dv2 = dv_part if dv2 is None else dv2 + dv_part
            dq_ref[:, cs] = jnp.where(left, dq_h[0], dq_h[1]) * (HEAD_DIM ** -0.5)
            dk_ref[here, cs] = dk2[BLK:]
            dv_ref[here, cs] = dv2[BLK:]

            @pl.when(n > 0)
            def _():
                dk_ref[before, cs] += dk2[:BLK]
                dv_ref[before, cs] += dv2[:BLK]

    whole = pl.BlockSpec((sub, GW), lambda r, n: (0, r))
    return pl.pallas_call(
        body,
        name=f"attn_bwd_g{gi}",
        grid=(d, sub // BLK),
        in_specs=[cur, cur, prev, cur, prev, cur, cur, cur],
        out_specs=[cur, whole, whole],
        out_shape=[jax.ShapeDtypeStruct((sub, d * GW), F32)] * 3,
        compiler_params=_cp(("arbitrary", "arbitrary"), 48),
    )(qr, kr, kr, vv, vv, do, lse, cterm)


def _qkv_unprep(dq, dk, dv, cos_t, sin_t, gi, dproj):
    d = PATTERN_DILATIONS[gi]

    def body(dq_ref, dk_ref, dv_ref, c_ref, s_ref, _, out_ref, *scr):
        lo = _lane_lo(ROWS)
        cf, sf = _tile4(c_ref[...]), _tile4(s_ref[...])
        out_ref[:, 0:GW] = _unrope(_from_view(scr, dq_ref, d), cf, sf, lo).astype(BF16)
        out_ref[:, GW:2 * GW] = _unrope(_from_view(scr, dk_ref, d), cf, sf, lo).astype(BF16)
        out_ref[:, 2 * GW:3 * GW] = _from_view(scr, dv_ref, d).astype(BF16)

    tab = pl.BlockSpec((ROWS, BLK), lambda i: (i, 0))
    return pl.pallas_call(
        body,
        name=f"qkv_unprep_g{gi}",
        grid=(S // ROWS,),
        in_specs=[_view_spec(d)] * 3 + [tab, tab, pl.BlockSpec(memory_space=pl.ANY)],
        out_specs=pl.BlockSpec((ROWS, 3 * GW), lambda i: (i, gi)),
        out_shape=jax.ShapeDtypeStruct(dproj.shape, dproj.dtype),
        input_output_aliases={5: 0},
        scratch_shapes=_VIEW_SCRATCH,
        compiler_params=_cp(("parallel",)),
    )(dq, dk, dv, cos_t, sin_t, dproj)


def _group_weights(l0, l1, l2):
    mx = jnp.maximum(jnp.maximum(l0, l1), l2)
    e0, e1, e2 = jnp.exp(l0 - mx), jnp.exp(l1 - mx), jnp.exp(l2 - mx)
    inv = 1.0 / (e0 + e1 + e2)
    return e0 * inv, e1 * inv, e2 * inv


def _combine_fwd(os_, lses):
    def body(o0, o1, o2, l0, l1, l2, y_ref, *scr):
        ov = [_from_view(scr, o, d) for o, d in zip((o0, o1, o2), PATTERN_DILATIONS)]
        lv = [_from_view(scr, l, d) for l, d in zip((l0, l1, l2), PATTERN_DILATIONS)]
        w0, w1, w2 = _group_weights(*lv)
        y_ref[...] = (w0 * ov[0] + w1 * ov[1] + w2 * ov[2]).astype(BF16)

    views = [_view_spec(d) for d in PATTERN_DILATIONS]
    return pl.pallas_call(
        body,
        name="attn_combine_fwd",
        grid=(S // ROWS,),
        in_specs=views + views,
        out_specs=pl.BlockSpec((ROWS, GW), lambda i: (i, 0)),
        out_shape=jax.ShapeDtypeStruct((S, GW), BF16),
        scratch_shapes=_VIEW_SCRATCH,
        compiler_params=_cp(("parallel",)),
    )(*os_, *lses)


def _combine_bwd(dy, os_, lses, seg):
    def body(dy_ref, o0, o1, o2, l0, l1, l2, seg_ref, d0, d1, d2, c0, c1, c2, *scr):
        ov = [_from_view(scr, o, d) for o, d in zip((o0, o1, o2), PATTERN_DILATIONS)]
        lv = [_from_view(scr, l, d) for l, d in zip((l0, l1, l2), PATTERN_DILATIONS)]
        ws = _group_weights(*lv)
        dyv = dy_ref[...]
        t = dyv * (ws[0] * ov[0] + ws[1] * ov[1] + ws[2] * ov[2])
        t_hi = t.astype(BF16)
        r1 = t - t_hi.astype(F32)
        t_mid = r1.astype(BF16)
        t_lo = (r1 - t_mid.astype(F32)).astype(BF16)
        sg = seg_ref[...]
        e = _dot_nn(t_hi, sg) + _dot_nn(t_mid, sg) + _dot_nn(t_lo, sg)
        for w, do_ref, c_ref, d in zip(ws, (d0, d1, d2), (c0, c1, c2), PATTERN_DILATIONS):
            _to_view(scr, w * dyv, do_ref, d, BF16)
            _to_view(scr, w * e, c_ref, d, F32)

    views = [_view_spec(d) for d in PATTERN_DILATIONS]
    return pl.pallas_call(
        body,
        name="attn_combine_bwd",
        grid=(S // ROWS,),
        in_specs=[pl.BlockSpec((ROWS, GW), lambda i: (i, 0))] + views + views + [pl.BlockSpec((GW, GW), lambda i: (0, 0))],
        out_specs=views + views,
        out_shape=[jax.ShapeDtypeStruct((S // d, d * GW), BF16) for d in PATTERN_DILATIONS]
        + [jax.ShapeDtypeStruct((S // d, d * GW), F32) for d in PATTERN_DILATIONS],
        scratch_shapes=_VIEW_SCRATCH,
        compiler_params=_cp(("parallel",)),
    )(dy, *os_, *lses, seg)


_SQRT_HALF = 0.7071067811865476
_INV_SQRT_2PI = 0.3989422804014327


def _gelu(z):
    return 0.5 * z * (1.0 + lax.erf(z * _SQRT_HALF))


def _gelu_grad(z):
    return 0.5 * (1.0 + lax.erf(z * _SQRT_HALF)) + z * (_INV_SQRT_2PI * jnp.exp(-0.5 * z * z))


def _tril_ws(ws_ref, g):
    t = lax.broadcasted_iota(jnp.int32, (BLK, BLK), 0)
    s = lax.broadcasted_iota(jnp.int32, (BLK, BLK), 1)
    return jnp.where(t >= s, ws_ref[g], 0.0)


def _gmlp_fwd(z, ws, bst, ln_g, ln_b, tm=256):
    nch = tm // BLK

    def body(z_ref, ws_ref, b_ref, g_ref, be_ref, y_ref):
        zg = _gelu(z_ref[...])
        u = zg[:, :D]
        vn, _, _ = _ln_fwd(zg[:, D:], g_ref[...], be_ref[...])
        vnb = vn.astype(BF16)
        bt = b_ref[...]
        for g in range(8):
            w = _tril_ws(ws_ref, g).astype(BF16)
            cols = slice(g * BLK, (g + 1) * BLK)
            for c in range(nch):
                rows = slice(c * BLK, (c + 1) * BLK)
                mixed = _dot_nn(w, vnb[rows, cols]) + bt[:, g:g + 1]
                y_ref[rows, cols] = (u[rows, cols] * mixed).astype(BF16)

    return pl.pallas_call(
        body,
        name="gmlp_fwd",
        grid=(S // tm,),
        in_specs=[
            pl.BlockSpec((tm, 2 * D), lambda i: (i, 0)),
            pl.BlockSpec((8, BLK, BLK), lambda i: (0, 0, 0)),
            pl.BlockSpec((BLK, 8), lambda i: (0, 0)),
            pl.BlockSpec((1, D), lambda i: (0, 0)),
            pl.BlockSpec((1, D), lambda i: (0, 0)),
        ],
        out_specs=pl.BlockSpec((tm, D), lambda i: (i, 0)),
        out_shape=jax.ShapeDtypeStruct((S, D), BF16),
        compiler_params=_cp(("parallel",), 48),
    )(z, ws, bst, ln_g, ln_b)


def _gmlp_bwd(z, dy, ws, bst, ln_g, ln_b, dproj, tm=256):
    nch = tm // BLK

    def body(z_ref, dy_ref, ws_ref, b_ref, g_ref, be_ref, _, dz_ref, dws_ref, dbs_ref, dg_ref, dbe_ref, dvn_scr, dm_acc):
        i = pl.program_id(0)
        zv = z_ref[...]
        zg = _gelu(zv)
        u = zg[:, :D]
        gam = g_ref[...]
        vn, xhat, rstd = _ln_fwd(zg[:, D:], gam, be_ref[...])
        vnb = vn.astype(BF16)
        dyv = dy_ref[...]
        dmix = dyv * u
        dmb = dmix.astype(BF16)
        bt = b_ref[...]
        tmask = lax.broadcasted_iota(jnp.int32, (BLK, BLK), 0) >= lax.broadcasted_iota(jnp.int32, (BLK, BLK), 1)
        dm_sum = dmix[0:BLK]
        for c in range(1, nch):
            dm_sum = dm_sum + dmix[c * BLK:(c + 1) * BLK]

        @pl.when(i == 0)
        def _():
            dm_acc[...] = dm_sum

        @pl.when(i > 0)
        def _():
            dm_acc[...] += dm_sum

        dus = []
        for g in range(8):
            w = _tril_ws(ws_ref, g).astype(BF16)
            cols = slice(g * BLK, (g + 1) * BLK)
            dw = None
            du_rows = []
            for c in range(nch):
                rows = slice(c * BLK, (c + 1) * BLK)
                mixed = _dot_nn(w, vnb[rows, cols]) + bt[:, g:g + 1]
                du_rows.append(dyv[rows, cols] * mixed)
                part = _dot_nt(dmb[rows, cols], vnb[rows, cols])
                dw = part if dw is None else dw + part
                dvn_scr[rows, cols] = _dot_tn(w, dmb[rows, cols])
            dus.append(jnp.concatenate(du_rows, axis=0))
            dw = jnp.where(tmask, dw, 0.0)

            @pl.when(i == 0)
            def _():
                dws_ref[g] = dw

            @pl.when(i > 0)
            def _():
                dws_ref[g] += dw

        dvn = dvn_scr[...]
        sg = _colsum(dvn * xhat)
        sb = _colsum(dvn)

        @pl.when(i == 0)
        def _():
            dg_ref[...] = sg
            dbe_ref[...] = sb

        @pl.when(i > 0)
        def _():
            dg_ref[...] += sg
            dbe_ref[...] += sb

        dvg = _ln_bwd(dvn, xhat, rstd, gam)
        gp = _gelu_grad(zv)
        dz_ref[:, :D] = (jnp.concatenate(dus, axis=1) * gp[:, :D]).astype(BF16)
        dz_ref[:, D:] = (dvg * gp[:, D:]).astype(BF16)

        @pl.when(i == S // tm - 1)
        def _():
            acc = dm_acc[...]
            for g in range(8):
                dbs_ref[:, g:g + 1] = jnp.sum(acc[:, g * BLK:(g + 1) * BLK], axis=1, keepdims=True)

    vec = pl.BlockSpec((1, D), lambda i: (0, 0))
    return pl.pallas_call(
        body,
        name="gmlp_bwd",
        grid=(S // tm,),
        in_specs=[
            pl.BlockSpec((tm, 2 * D), lambda i: (i, 0)),
            pl.BlockSpec((tm, D), lambda i: (i, 0)),
            pl.BlockSpec((8, BLK, BLK), lambda i: (0, 0, 0)),
            pl.BlockSpec((BLK, 8), lambda i: (0, 0)),
            vec,
            vec,
            pl.BlockSpec(memory_space=pl.ANY),
        ],
        out_specs=[
            pl.BlockSpec((tm, 2 * D), lambda i: (i, DPROJ_Z_COL)),
            pl.BlockSpec((8, BLK, BLK), lambda i: (0, 0, 0)),
            pl.BlockSpec((BLK, 8), lambda i: (0, 0)),
            vec,
            vec,
        ],
        out_shape=[
            jax.ShapeDtypeStruct(dproj.shape, dproj.dtype),
            jax.ShapeDtypeStruct((8, BLK, BLK), F32),
            jax.ShapeDtypeStruct((BLK, 8), F32),
            jax.ShapeDtypeStruct((1, D), F32),
            jax.ShapeDtypeStruct((1, D), F32),
        ],
        input_output_aliases={6: 0},
        scratch_shapes=[pltpu.VMEM((tm, D), F32), pltpu.VMEM((BLK, D), F32)],
        compiler_params=_cp(("arbitrary",), 48),
    )(z, dy, ws, bst, ln_g, ln_b, dproj)


def _merge_fwd(ya, yg, glog, bgate, h1, wabt, wgb, wo, ln_g, ln_b, tm=256):
    def body(ya_ref, yg_ref, gl_ref, bg_ref, h1_ref, wab_ref, wgb_ref, wo_ref, g_ref, b_ref,
             h_ref, hb_ref, xh_ref, rs_ref, mg_ref, bra_ref, brg_ref):
        bra = _dot_nt(ya_ref[...], wab_ref[...])
        brg = _dot_nn(yg_ref[...], wgb_ref[...])
        gates = _sigmoid(gl_ref[...] + bg_ref[...])
        merged = (gates[:, :D] * bra + gates[:, D:] * brg).astype(BF16)
        mix = _dot_nn(merged, wo_ref[...])
        h, xhat, rstd = _ln_fwd(ALPHA * h1_ref[...] + mix, g_ref[...], b_ref[...])
        h_ref[...] = h
        hb_ref[...] = h.astype(BF16)
        xh_ref[...] = xhat
        rs_ref[...] = rstd
        mg_ref[...] = merged
        bra_ref[...] = bra
        brg_ref[...] = brg

    row = pl.BlockSpec((tm, D), lambda i: (i, 0))
    vec = pl.BlockSpec((1, D), lambda i: (0, 0))
    full = lambda shape: pl.BlockSpec(shape, lambda i: (0, 0))
    return pl.pallas_call(
        body,
        name="merge_fwd",
        grid=(S // tm,),
        in_specs=[
            pl.BlockSpec((tm, GW), lambda i: (i, 0)), row,
            pl.BlockSpec((tm, 2 * D), lambda i: (i, glog.shape[1] // (2 * D) - 1)),
            full((1, 2 * D)), row,
            full((D, GW)), full((D, D)), full((D, D)), vec, vec,
        ],
        out_specs=[row, row, row, pl.BlockSpec((tm, 1), lambda i: (i, 0)), row, row, row],
        out_shape=[
            jax.ShapeDtypeStruct((S, D), F32),
            jax.ShapeDtypeStruct((S, D), BF16),
            jax.ShapeDtypeStruct((S, D), F32),
            jax.ShapeDtypeStruct((S, 1), F32),
            jax.ShapeDtypeStruct((S, D), BF16),
            jax.ShapeDtypeStruct((S, D), F32),
            jax.ShapeDtypeStruct((S, D), F32),
        ],
        compiler_params=_cp(("parallel",), 48),
    )(ya, yg, glog, bgate, h1, wabt, wgb, wo, ln_g, ln_b)


def _merge_bwd(dh2, xhat, rstd, ln_g, bra, brg, glog, bgate, wabt, wgb, wo, tm=256):
    def body(dh_ref, xh_ref, rs_ref, g_ref, bra_ref, brg_ref, gl_ref, bg_ref, wab_ref, wgb_ref, wo_ref,
             dr_ref, drb_ref, dlog_ref, dba_ref, dbg_ref, dya_ref, dyg_ref, dbgate_ref, dg_ref, dbias_ref):
        i = pl.program_id(0)
        dh = dh_ref[...]
        xh = xh_ref[...]
        dr = _ln_bwd(dh, xh, rs_ref[...], g_ref[...])
        drb = dr.astype(BF16)
        dr_ref[...] = dr
        drb_ref[...] = drb
        dmerged = _dot_nt(drb, wo_ref[...])
        gates = _sigmoid(gl_ref[...] + bg_ref[...])
        g0, g1 = gates[:, :D], gates[:, D:]
        dl0 = dmerged * bra_ref[...] * g0 * (1.0 - g0)
        dl1 = dmerged * brg_ref[...] * g1 * (1.0 - g1)
        dlog_ref[:, :D] = dl0.astype(BF16)
        dlog_ref[:, D:] = dl1.astype(BF16)
        dba = (dmerged * g0).astype(BF16)
        dbg = (dmerged * g1).astype(BF16)
        dba_ref[...] = dba
        dbg_ref[...] = dbg
        dya_ref[...] = _dot_nn(dba, wab_ref[...])
        dyg_ref[...] = _dot_nt(dbg, wgb_ref[...])
        s0, s1 = _colsum(dl0), _colsum(dl1)
        sg, sb = _colsum(dh * xh), _colsum(dh)

        @pl.when(i == 0)
        def _():
            dbgate_ref[:, :D] = s0
            dbgate_ref[:, D:] = s1
            dg_ref[...] = sg
            dbias_ref[...] = sb

        @pl.when(i > 0)
        def _():
            dbgate_ref[:, :D] += s0
            dbgate_ref[:, D:] += s1
            dg_ref[...] += sg
            dbias_ref[...] += sb

    row = pl.BlockSpec((tm, D), lambda i: (i, 0))
    vec = pl.BlockSpec((1, D), lambda i: (0, 0))
    wide = pl.BlockSpec((tm, 2 * D), lambda i: (i, 0))
    full = lambda shape: pl.BlockSpec(shape, lambda i: (0, 0))
    return pl.pallas_call(
        body,
        name="merge_bwd",
        grid=(S // tm,),
        in_specs=[row, row, pl.BlockSpec((tm, 1), lambda i: (i, 0)), vec, row, row,
                  pl.BlockSpec((tm, 2 * D), lambda i: (i, glog.shape[1] // (2 * D) - 1)),
                  full((1, 2 * D)), full((D, GW)), full((D, D)), full((D, D))],
        out_specs=[row, row, pl.BlockSpec((tm, 2 * D), lambda i: (i, DPROJ_G_COL)), row, row,
                   pl.BlockSpec((tm, GW), lambda i: (i, 0)), row, full((1, 2 * D)), vec, vec],
        out_shape=[
            jax.ShapeDtypeStruct((S, D), F32),
            jax.ShapeDtypeStruct((S, D), BF16),
            jax.ShapeDtypeStruct((S, DPROJ_W), BF16),
            jax.ShapeDtypeStruct((S, D), BF16),
            jax.ShapeDtypeStruct((S, D), BF16),
            jax.ShapeDtypeStruct((S, GW), F32),
            jax.ShapeDtypeStruct((S, D), F32),
            jax.ShapeDtypeStruct((1, 2 * D), F32),
            jax.ShapeDtypeStruct((1, D), F32),
            jax.ShapeDtypeStruct((1, D), F32),
        ],
        compiler_params=_cp(("arbitrary",), 48),
    )(dh2, xhat, rstd, ln_g, bra, brg, glog, bgate, wabt, wgb, wo)


def _loss_head(h3, target, tm=512):
    def body(h_ref, t_ref, d_ref, l_ref):
        i = pl.program_id(0)
        e = h_ref[...] - t_ref[...]
        d_ref[...] = e * (1.0 / D)
        part = jnp.sum(_colsum(e * e), axis=1, keepdims=True) * (0.5 / D)

        @pl.when(i == 0)
        def _():
            l_ref[...] = part

        @pl.when(i > 0)
        def _():
            l_ref[...] += part

    row = pl.BlockSpec((tm, D), lambda i: (i, 0))
    return pl.pallas_call(
        body,
        name="loss_head",
        grid=(S // tm,),
        in_specs=[row, row],
        out_specs=[row, pl.BlockSpec((1, 1), lambda i: (0, 0))],
        out_shape=[jax.ShapeDtypeStruct((S, D), F32), jax.ShapeDtypeStruct((1, 1), F32)],
        compiler_params=_cp(("arbitrary",)),
    )(h3, target)


def _tie(x, dep):
    if dep is None:
        return x
    return x + dep[0, 0].astype(x.dtype)


def _local_step(x, pos_col, target, get_w, p, emit):
    w = get_w("ffn1", None)
    a1, b1, hm1 = _ffn_up(x, w["g1"], w["u1"], "ffn1_up")
    h1, h1b, xh1, rs1 = _ffn_down(hm1, w["d1"], x, p["ln1_g"], p["ln1_b"], "ffn1_down")

    w.update(get_w("win", h1b))
    qkv = _matmul(h1b, w["win"], "nt", F32, 1024, 1536, D, "proj_qkv", b_off=0, n_out=QKV_W)
    z = glog = _matmul(h1b, w["win"], "nt", F32, 1024, 512, D, "proj_zg", b_off=QKV_W // 512, n_out=4 * D)

    half = jnp.arange(0, HEAD_DIM, 2, dtype=F32) / HEAD_DIM
    inv_freq = ROPE_THETA ** (-half)
    invf = jnp.tile(inv_freq, 4).reshape(1, BLK)
    sign = jnp.tile(jnp.concatenate([-jnp.ones((32,), F32), jnp.ones((32,), F32)]), 2).reshape(1, BLK)
    cos_t, sin_t = _rope_tables(pos_col, invf, sign)

    preps = [_qkv_prep(qkv, cos_t, sin_t, gi) for gi in range(3)]
    os_, lses = [], []
    for gi in range(3):
        o, lse = _attn_fwd(*preps[gi], gi)
        os_.append(o)
        lses.append(lse)
    ya = _combine_fwd(os_, lses)
    bst = p["gmlp_b_s"].T
    yg = _gmlp_fwd(z, p["gmlp_w_s"], bst, p["gmlp_ln_g"], p["gmlp_ln_b"])
    w.update(get_w("mix", yg))
    h2, h2b, xh2, rs2, merged, bra, brg = _merge_fwd(ya, yg, glog, p["b_gates"], h1, w["ab"], w["gb"], w["o"],
                                                      p["ln2_g"], p["ln2_b"])
    w.update(get_w("ffn2", h2b))
    a2, b2, hm2 = _ffn_up(h2b, w["g2"], w["u2"], "ffn2_up")
    h3, _, xh3, rs3 = _ffn_down(hm2, w["d2"], h2, p["ln3_g"], p["ln3_b"], "ffn2_down")
    dh3, loss = _loss_head(h3, target)

    gp = {}
    dr3, df2, da2, db2, gp["ln3_g"], gp["ln3_b"] = _ffn_bwd_mid(dh3, None, xh3, rs3, p["ln3_g"], a2, b2, w["d2"],
                                                                "ffn2_bwd_mid")
    tok = emit("ffn2", {
        "g2": _matmul(da2, h2b, "tn", BF16, 1408, D, S, "wgrad_g2"),
        "u2": _matmul(db2, h2b, "tn", BF16, 1408, D, S, "wgrad_u2"),
        "d2": _matmul(hm2, df2, "tn", BF16, 1408, D, S, "wgrad_d2")})
    dh2 = _ffn_bwd_dx(dr3, da2, db2, w["g2"], w["u2"], "ffn2_bwd_dx")

    (dr2, dr2b, dproj, dba, dbg, dya, dyg, gp["b_gates"], gp["ln2_g"], gp["ln2_b"]) = _merge_bwd(
        dh2, xh2, rs2, _tie(p["ln2_g"], tok), bra, brg, glog, p["b_gates"], w["ab"], w["gb"], w["o"])
    tok = emit("mix", {
        "o": _matmul(merged, dr2b, "tn", BF16, 512, D, S, "wgrad_o"),
        "ab": _matmul(dba, ya, "tn", BF16, 512, GW, S, "wgrad_ab"),
        "gb": _matmul(yg, dbg, "tn", BF16, 512, D, S, "wgrad_gb")})

    seg = (jnp.arange(GW)[:, None] // HEAD_DIM == jnp.arange(GW)[None, :] // HEAD_DIM).astype(BF16)
    do0, do1, do2, c0, c1, c2 = _combine_bwd(dya, os_, lses, _tie(seg, tok))
    for gi, (do, ct) in enumerate(((do0, c0), (do1, c1), (do2, c2))):
        dq, dk, dv = _attn_bwd(*preps[gi], do, lses[gi], ct, gi)
        dproj = _qkv_unprep(dq, dk, dv, cos_t, sin_t, gi, dproj)
    dproj, gp["gmlp_w_s"], dbst, gp["gmlp_ln_g"], gp["gmlp_ln_b"] = _gmlp_bwd(
        z, dyg, p["gmlp_w_s"], bst, p["gmlp_ln_g"], p["gmlp_ln_b"], dproj)
    gp["gmlp_b_s"] = dbst.T
    tok = emit("win", {"win": _matmul(dproj, h1b, "tn", BF16, 512, D, S, "wgrad_win", a_map=_dproj_tile, m_out=IN_W)})
    dh1m = _matmul(dproj, w["win"], "nn", F32, 1024, D, 512, "dproj_to_dh1", dep=tok, a_map=_dproj_tile)

    dr1, df1, da1, db1, gp["ln1_g"], gp["ln1_b"] = _ffn_bwd_mid(dr2, dh1m, xh1, rs1, p["ln1_g"], a1, b1, w["d1"],
                                                                "ffn1_bwd_mid")
    tok = emit("small", gp)
    tok = emit("g1", {"g1": _matmul(da1, x, "tn", BF16, 1408, D, S, "wgrad_g1", dep=tok)})
    tok = emit("u1", {"u1": _matmul(db1, x, "tn", BF16, 1408, D, S, "wgrad_u1", dep=tok)})
    tok = emit("d1", {"d1": _matmul(hm1, df1, "tn", BF16, 1408, D, S, "wgrad_d1", dep=tok)})
    dx = _ffn_bwd_dx(dr1, da1, db1, w["g1"], w["u1"], "ffn1_bwd_dx", dep=tok)
    return loss, dx, gp


_FLIPS = [(mx, my, mc) for mx in (0, 1) for my in (0, 1) for mc in (0, 1)][1:]
_GROUPS = {"ffn1": ("g1", "u1", "d1"), "win": ("win",), "mix": ("ab", "gb", "o"), "ffn2": ("g2", "u2", "d2")}
_SCATTERS = {"ffn2": ("g2", "u2", "d2"), "mix": ("ab", "gb", "o"), "win": ("win",), "g1": ("g1",), "u1": ("u1",), "d1": ("d1",)}
_TWO_STAGE = ("win", "g1", "u1", "d1")
_HBM = pl.BlockSpec(memory_space=pltpu.HBM)
_SEM = pl.BlockSpec(memory_space=pltpu.SEMAPHORE)
_EFFECT = pltpu.SideEffectType.DATAFLOW_SIDE_EFFECTING


def _me():
    return 4 * lax.axis_index("x") + 2 * lax.axis_index("y") + lax.axis_index("c")


def _copies_start(bufs, copies, n_sem, name, after=None):
    nb = len(bufs)
    n_after = 0 if after is None else 1

    def body(*refs):
        b = refs[:nb]
        send_sems, recv_sems = refs[nb + n_after], refs[nb + n_after + 1]
        token = refs[-1]
        x, y, c = lax.axis_index("x"), lax.axis_index("y"), lax.axis_index("c")
        me = 4 * x + 2 * y + c
        for si, s_slot, di, d_slot, (mx, my, mc), sem in copies:
            s_idx, d_idx = s_slot(me), d_slot(me)
            pltpu.make_async_remote_copy(
                src_ref=b[si] if s_idx is None else b[si].at[s_idx],
                dst_ref=b[di] if d_idx is None else b[di].at[d_idx],
                send_sem=send_sems.at[sem], recv_sem=recv_sems.at[sem],
                device_id=(x ^ mx, y ^ my, c ^ mc), device_id_type=MESH).start()
        token[...] = jnp.zeros_like(token)

    ins = [pltpu.with_memory_space_constraint(a, pltpu.HBM) for a in bufs]
    outs = pl.pallas_call(
        body,
        name=name,
        in_specs=[_HBM] * nb + [pl.BlockSpec(memory_space=pl.ANY)] * n_after,
        out_specs=[_SEM, _SEM] + [_HBM] * nb + [pl.BlockSpec(memory_space=pltpu.VMEM)],
        out_shape=[pltpu.SemaphoreType.DMA((n_sem,)), pltpu.SemaphoreType.DMA((n_sem,))]
        + [pltpu.HBM(a.shape, a.dtype) for a in ins] + [jax.ShapeDtypeStruct((8, 128), F32)],
        input_output_aliases={k: 2 + k for k in range(nb)},
        compiler_params=pltpu.CompilerParams(has_side_effects=_EFFECT),
    )(*ins, *([] if after is None else [after]))
    return outs[0], outs[1], list(outs[2:2 + nb]), outs[-1]


def _copies_wait(started, waits, after, name):
    send_sems, recv_sems, bufs, _ = started
    nb = len(bufs)

    def body(*refs):
        b = refs[:nb]
        ss, rs = refs[nb], refs[nb + 1]
        me3 = (lax.axis_index("x"), lax.axis_index("y"), lax.axis_index("c"))
        for bi, n_blocks, sem, is_send in waits:
            blocks = b[bi].at[pl.ds(0, n_blocks)]
            cp = pltpu.make_async_remote_copy(src_ref=blocks, dst_ref=blocks, send_sem=ss.at[sem], recv_sem=rs.at[sem],
                                              device_id=me3, device_id_type=MESH)
            if is_send:
                cp.wait_send()
            else:
                cp.wait_recv()

    return pl.pallas_call(
        body,
        name=name,
        in_specs=[_HBM] * nb + [_SEM, _SEM, pl.BlockSpec(memory_space=pl.ANY)],
        out_specs=[_HBM] * nb,
        out_shape=[pltpu.HBM(a.shape, a.dtype) for a in bufs],
        input_output_aliases={k: k for k in range(nb)},
        compiler_params=pltpu.CompilerParams(has_side_effects=_EFFECT),
    )(*bufs, send_sems, recv_sems, after)


def _landing(own, me):
    return lax.dynamic_update_slice(lax.empty((N_DEV,) + own.shape[1:], own.dtype), own, (me, 0, 0))


_SIBLING = (0, 0, 1)
_OTHER_CHIPS = ((1, 0, 0), (0, 1, 0), (1, 1, 0))


def _bits(flip):
    return 4 * flip[0] + 2 * flip[1] + flip[2]


def _gather_send(shards, grp, after):
    nw = len(shards)
    me = _me()
    bufs = list(shards) + [_landing(a[None], me) for a in shards]
    copies = []
    for k in range(nw):
        copies.append((k, lambda me: None, nw + k, lambda me: me, _SIBLING, nw + k))
        copies += [(k, lambda me: None, nw + k, lambda me: me, f, k) for f in _OTHER_CHIPS]
    return _copies_start(bufs, copies, 2 * nw, "gather_send_" + grp, after)


def _gather_finish(started, grp, after):
    nw = len(started[2]) // 2
    waits = []
    for k in range(nw):
        waits += [(nw + k, 3, k, True), (nw + k, 1, nw + k, True), (nw + k, 3, k, False), (nw + k, 1, nw + k, False)]
    lands = list(_copies_wait(started, waits, after, "gather_arrived_" + grp)[nw:])
    copies = []
    for k in range(nw):
        for f in _OTHER_CHIPS:
            slot = functools.partial(lambda me, bits: me ^ bits, bits=_bits(f))
            copies.append((k, slot, k, slot, _SIBLING, k))
    passed = _copies_start(lands, copies, nw, "gather_pass_" + grp)
    waits = [(k, 3, k, is_send) for k in range(nw) for is_send in (True, False)]
    return _copies_wait(passed, waits, passed[3], "gather_done_" + grp)


def _scatter_send(parts, grp, after=None):
    nw = len(parts)
    me = _me()
    bufs = list(parts) + [_landing(lax.dynamic_index_in_dim(a, me, 0, keepdims=True), me) for a in parts]
    copies = []
    for k in range(nw):
        for f in _FLIPS:
            to = functools.partial(lambda me, bits: me ^ bits, bits=_bits(f))
            copies.append((k, to, nw + k, lambda me: me, f, k))
    return _copies_start(bufs, copies, nw, "scatter_send_" + grp, after)


def _scatter_finish(started, grp, after):
    nw = len(started[2]) // 2
    waits = [(nw + k, N_DEV - 1, k, is_send) for k in range(nw) for is_send in (True, False)]
    return _copies_wait(started, waits, after, "scatter_done_" + grp)[nw:]


N_CHIP = N_DEV // 2


def _pair_sum(part, other, key):
    _, r, c = part.shape
    tm = _ADAM_ROWS.get(r, r)
    core = lax.axis_index("c").astype(jnp.int32).reshape(1)

    def body(core_ref, a_ref, b_ref, o_ref):
        o_ref[...] = (a_ref[...].astype(F32) + b_ref[...].astype(F32)).astype(BF16)

    return pl.pallas_call(
        body,
        name="pair_sum_" + key,
        grid_spec=pltpu.PrefetchScalarGridSpec(
            num_scalar_prefetch=1,
            grid=(N_CHIP, r // tm),
            in_specs=[pl.BlockSpec((None, tm, c), lambda q, i, core_ref: (2 * q + core_ref[0], i, 0)),
                      pl.BlockSpec((None, tm, c), lambda q, i, core_ref: (q, i, 0))],
            out_specs=pl.BlockSpec((None, tm, c), lambda q, i, core_ref: (q, i, 0)),
        ),
        out_shape=jax.ShapeDtypeStruct((N_CHIP, r, c), BF16),
        compiler_params=_cp(("parallel", "parallel")),
    )(core, part, other)


def _scatter2_send(part, key, dep=None):
    me = _me()
    _, r, c = part.shape
    swap = []
    for q in range(N_CHIP):
        src = functools.partial(lambda me, q: 2 * q + 1 - me % 2, q=q)
        swap.append((0, src, 1, functools.partial(lambda me, q: q, q=q), _SIBLING, 0))
    started = _copies_start([part, lax.empty((N_CHIP, r, c), part.dtype)], swap, 1, "scatter_swap_" + key, dep)
    waits = [(1, N_CHIP, 0, True), (1, N_CHIP, 0, False)]
    part, other = _copies_wait(started, waits, started[3], "scatter_swapped_" + key)
    pair = _pair_sum(part, other, key)
    chip = me // 2
    land = lax.dynamic_update_slice(lax.empty((N_CHIP, r, c), pair.dtype),
                                    lax.dynamic_index_in_dim(pair, chip, 0, keepdims=True), (chip, 0, 0))
    copies = []
    for f in _OTHER_CHIPS:
        to = functools.partial(lambda me, bits: (me ^ bits) // 2, bits=_bits(f))
        copies.append((0, to, 1, lambda me: me // 2, f, 0))
    return _copies_start([pair, land], copies, 1, "scatter_send_" + key)


def _scatter2_finish(started, key, after):
    waits = [(1, N_CHIP - 1, 0, True), (1, N_CHIP - 1, 0, False)]
    return _copies_wait(started, waits, after, "scatter_done_" + key)[1]


def _allgather_send(block, name, after=None):
    me = _me()
    copies = [(0, lambda me: None, 1, lambda me: me, f, 0) for f in _FLIPS]
    return _copies_start([block, _landing(block[None], me)], copies, 1, name, after)


def _allgather_finish(started, name, after):
    waits = [(1, N_DEV - 1, 0, True), (1, N_DEV - 1, 0, False)]
    return _copies_wait(started, waits, after, name)[1]


_ADAM_ROWS = {352: 176, 1088: 272}


def _sum_adamw(parts, wv, m, v, name):
    n_parts, r, c = parts.shape
    tm = _ADAM_ROWS.get(r, r)
    assert r % tm == 0 and wv.shape == (r, c)

    def body(p_ref, w_ref, m_ref, v_ref, g_ref, d_ref, mo_ref, vo_ref):
        gv = p_ref[0].astype(F32)
        for j in range(1, n_parts):
            gv = gv + p_ref[j].astype(F32)
        g_ref[...] = gv
        mn = ADAM_B1 * m_ref[...] + (1.0 - ADAM_B1) * gv
        vn = ADAM_B2 * v_ref[...] + (1.0 - ADAM_B2) * (gv * gv)
        m_hat = mn / (1.0 - ADAM_B1 ** ADAM_STEP)
        v_hat = vn / (1.0 - ADAM_B2 ** ADAM_STEP)
        d_ref[...] = -ADAM_LR * (m_hat / (jnp.sqrt(v_hat) + ADAM_EPS) + ADAM_WD * w_ref[...])
        mo_ref[...] = mn
        vo_ref[...] = vn

    sp = pl.BlockSpec((tm, c), lambda i: (i, 0))
    return pl.pallas_call(
        body,
        name=name,
        grid=(r // tm,),
        in_specs=[pl.BlockSpec((n_parts, tm, c), lambda i: (0, i, 0))] + [sp] * 3,
        out_specs=[sp] * 4,
        out_shape=[jax.ShapeDtypeStruct((r, c), F32)] * 4,
        compiler_params=_cp(("parallel",), 48),
    )(parts, wv, m, v)


_WEIGHTS = ["ffn1_w_gate", "ffn1_w_up", "ffn1_w_down", "ln1_g", "ln1_b", "w_in", "b_gates", "gmlp_ln_g", "gmlp_ln_b",
            "gmlp_w_s", "gmlp_b_s", "w_attn_branch", "w_gmlp_branch", "w_out", "ln2_g", "ln2_b", "ffn2_w_gate",
            "ffn2_w_up", "ffn2_w_down", "ln3_g", "ln3_b"]
_BIG_OF = {"ffn1_w_gate": ("g1", True), "ffn1_w_up": ("u1", True), "ffn1_w_down": ("d1", False), "w_in": ("win", True),
           "w_attn_branch": ("ab", True), "w_gmlp_branch": ("gb", False), "w_out": ("o", False),
           "ffn2_w_gate": ("g2", True), "ffn2_w_up": ("u2", True), "ffn2_w_down": ("d2", False)}
_SMALL = [n for n in _WEIGHTS if n not in _BIG_OF]
_SMALL_ROWS = {"gmlp_w_s": 128, "b_gates": 2}
_SMALL_SLOT = 8


def _pack_small(d):
    rows = []
    for n in _SMALL:
        r = d[n].reshape(-1, D)
        slot = max(r.shape[0], _SMALL_SLOT)
        rows.append(jnp.pad(r, ((0, slot - r.shape[0]), (0, 0))))
    return jnp.concatenate(rows, axis=0)


def _unpack_small(packed, shapes):
    out, at = {}, 0
    for n in _SMALL:
        k = _SMALL_ROWS.get(n, 1)
        out[n] = packed[at:at + k].reshape(shapes[n])
        at += max(k, _SMALL_SLOT)
    return out


def kernel(x, positions, ffn1_w_gate, ffn1_w_up, ffn1_w_down, ln1_g, ln1_b, w_in, b_gates, gmlp_ln_g, gmlp_ln_b, gmlp_w_s, gmlp_b_s, w_attn_branch, w_gmlp_branch, w_out, ln2_g, ln2_b, ffn2_w_gate, ffn2_w_up, ffn2_w_down, ln3_g, ln3_b, loss_target, m_ffn1_w_gate, m_ffn1_w_up, m_ffn1_w_down, m_ln1_g, m_ln1_b, m_w_in, m_b_gates, m_gmlp_ln_g, m_gmlp_ln_b, m_gmlp_w_s, m_gmlp_b_s, m_w_attn_branch, m_w_gmlp_branch, m_w_out, m_ln2_g, m_ln2_b, m_ffn2_w_gate, m_ffn2_w_up, m_ffn2_w_down, m_ln3_g, m_ln3_b, v_ffn1_w_gate, v_ffn1_w_up, v_ffn1_w_down, v_ln1_g, v_ln1_b, v_w_in, v_b_gates, v_gmlp_ln_g, v_gmlp_ln_b, v_gmlp_w_s, v_gmlp_b_s, v_w_attn_branch, v_w_gmlp_branch, v_w_out, v_ln2_g, v_ln2_b, v_ffn2_w_gate, v_ffn2_w_up, v_ffn2_w_down, v_ln3_g, v_ln3_b):
    args = dict(locals())
    wts = {n: args[n] for n in _WEIGHTS}
    ms = {n: args["m_" + n] for n in _WEIGHTS}
    vs = {n: args["v_" + n] for n in _WEIGHTS}

    name_of = {key: (n, tr) for n, (key, tr) in _BIG_OF.items()}

    started, tok = {}, None
    for grp, keys in _GROUPS.items():
        shards = []
        for key in keys:
            n, tr = name_of[key]
            s2 = wts[n][0]
            shards.append((s2.T if tr else s2).astype(BF16))
        started[grp] = _gather_send(shards, grp, tok)
        tok = started[grp][3]
    all_started = tok

    def get_w(grp, after):
        lands = _gather_finish(started[grp], grp, all_started if after is None else after)
        return {key: g.reshape(-1, g.shape[-1]) for key, g in zip(_GROUPS[grp], lands)}

    sent = {}

    def emit(grp, grads):
        if grp == "small":
            sent[grp] = _allgather_send(_pack_small(grads), "small_grads_send")
        else:
            parts = [grads[key].reshape(N_DEV, -1, grads[key].shape[-1]) for key in _SCATTERS[grp]]
            sent[grp] = _scatter2_send(parts[0], grp) if grp in _TWO_STAGE else _scatter_send(parts, grp)
        return sent[grp][3]

    p = {n: (wts[n][0] if n in ("gmlp_w_s", "gmlp_b_s") else wts[n]) for n in _SMALL}
    loss, dx, gp = _local_step(x[0], positions.reshape(S, 1), loss_target[0], get_w, p, emit)
    loss = lax.psum(loss[0, 0], ("x", "y", "c"))

    grads, deltas, new_m, new_v = {}, {}, {}, {}
    after = dx
    for grp in ("ffn2", "mix", "win", "small", "g1", "u1", "d1"):
        if grp == "small":
            parts = _allgather_finish(sent[grp], "small_grads_done", after)
            outs = _sum_adamw(parts, *[_pack_small({n: d[n] for n in _SMALL}) for d in (wts, ms, vs)], "update_small")
            shapes = {n: wts[n].shape for n in _SMALL}
            for dst, packed in zip((grads, deltas, new_m, new_v), outs):
                dst.update(_unpack_small(packed, shapes))
            after = outs[1]
            continue
        arrived = ([_scatter2_finish(sent[grp], grp, after)] if grp in _TWO_STAGE
                   else _scatter_finish(sent[grp], grp, after))
        for key, part in zip(_SCATTERS[grp], arrived):
            n, tr = name_of[key]
            outs = _sum_adamw(part, *[(d[n][0].T if tr else d[n][0]) for d in (wts, ms, vs)], "update_" + key)
            for dst, o in zip((grads, deltas, new_m, new_v), outs):
                dst[n] = (o.T if tr else o)[None]
            after = outs[1]

    return (loss, dx[None], *[grads[n] for n in _WEIGHTS], *[deltas[n] for n in _WEIGHTS],
            *[new_m[n] for n in _WEIGHTS], *[new_v[n] for n in _WEIGHTS])
```

```python
import functools
import math

import jax
import jax.numpy as jnp
from jax import lax
from jax.experimental import pallas as pl
from jax.experimental.pallas import tpu as pltpu

F32 = jnp.float32
BF16 = jnp.bfloat16

N_DEV = 8
D = 1024
S = 2048
F = 2816
HEAD_DIM = 64
HEADS = 8
GW = HEADS * HEAD_DIM
PATTERN_DILATIONS = (1, 4, 16)
BLK = 128
QKV_W = 3 * 3 * GW
IN_W = QKV_W + 2 * D + 2 * D
DPROJ_W = 5 * 2 * D
DPROJ_Z_COL, DPROJ_G_COL = 3, 4


def _dproj_tile(t):
    three = jnp.int32(3)
    return jnp.where(t < 9, lax.rem(t, three) * 3 + lax.div(t, three), t + 3)
ROPE_THETA = 10000.0
ALPHA = 2.0 ** 0.25
LN_EPS = 1e-5
ADAM_LR, ADAM_B1, ADAM_B2, ADAM_EPS, ADAM_WD, ADAM_STEP = 0.001, 0.9, 0.999, 1e-08, 0.01, 10
NEG = -1e30
MESH = pl.DeviceIdType.MESH


def _cp(sem=None, vmem_mb=None):
    kw = {}
    if sem is not None:
        kw["dimension_semantics"] = sem
    if vmem_mb is not None:
        kw["vmem_limit_bytes"] = vmem_mb << 20
    return pltpu.CompilerParams(**kw)


def _dot_nn(a, b):
    return lax.dot_general(a, b, (((1,), (0,)), ((), ())), preferred_element_type=F32)


def _dot_nt(a, b):
    return lax.dot_general(a, b, (((1,), (1,)), ((), ())), preferred_element_type=F32)


def _dot_tn(a, b):
    return lax.dot_general(a, b, (((0,), (0,)), ((), ())), preferred_element_type=F32)


def _ln_fwd(r, g, b):
    mu = jnp.mean(r, axis=-1, keepdims=True)
    xc = r - mu
    var = jnp.mean(xc * xc, axis=-1, keepdims=True)
    rstd = lax.rsqrt(var + LN_EPS)
    xhat = xc * rstd
    return xhat * g + b, xhat, rstd


def _ln_bwd(dh, xhat, rstd, g):
    dxh = dh * g
    m1 = jnp.mean(dxh, axis=-1, keepdims=True)
    m2 = jnp.mean(dxh * xhat, axis=-1, keepdims=True)
    return rstd * (dxh - m1 - xhat * m2)


def _sigmoid(x):
    return 0.5 * jnp.tanh(0.5 * x) + 0.5


def _colsum(x):
    return jnp.sum(x, axis=0, keepdims=True)


def _matmul(a, b, mode, out_dtype, tm, tn, tk, name, b_off=0, n_out=None, dep=None, a_map=None, m_out=None):
    n_dep = 0 if dep is None else 1
    a_map = a_map or (lambda t: t)
    if mode == "nn":
        m, k = a.shape[0], b.shape[0]
        n = b.shape[1]
    elif mode == "nt":
        m, k = a.shape
        n = n_out if n_out is not None else b.shape[0]
    else:
        k, m = a.shape[0], m_out or a.shape[1]
        n = b.shape[1]
    nk = k // tk
    assert m % tm == 0 and n % tn == 0 and k % tk == 0
    dot = {"nn": _dot_nn, "nt": _dot_nt, "tn": _dot_tn}[mode]

    def body(a_ref, b_ref, *rest):
        o_ref, scr = rest[n_dep], rest[n_dep + 1:]
        r = dot(a_ref[...].astype(BF16), b_ref[...].astype(BF16))
        if nk == 1:
            o_ref[...] = r.astype(out_dtype)
        else:
            acc = scr[0]
            kk = pl.program_id(2)

            @pl.when(kk == 0)
            def _():
                acc[...] = r

            @pl.when(kk > 0)
            def _():
                acc[...] += r

            @pl.when(kk == nk - 1)
            def _():
                o_ref[...] = acc[...].astype(out_dtype)

    if mode == "nn":
        a_spec = pl.BlockSpec((tm, tk), lambda i, j, kk: (i, a_map(kk)))
        b_spec = pl.BlockSpec((tk, tn), lambda i, j, kk: (kk, j))
    elif mode == "nt":
        a_spec = pl.BlockSpec((tm, tk), lambda i, j, kk: (i, kk))
        b_spec = pl.BlockSpec((tn, tk), lambda i, j, kk: (j + b_off, kk))
    else:
        a_spec = pl.BlockSpec((tk, tm), lambda i, j, kk: (kk, a_map(i)))
        b_spec = pl.BlockSpec((tk, tn), lambda i, j, kk: (kk, j))
    return pl.pallas_call(
        body,
        name=name,
        grid=(m // tm, n // tn, nk),
        in_specs=[a_spec, b_spec] + [pl.BlockSpec(memory_space=pl.ANY)] * n_dep,
        out_specs=pl.BlockSpec((tm, tn), lambda i, j, kk: (i, j)),
        out_shape=jax.ShapeDtypeStruct((m, n), out_dtype),
        scratch_shapes=[] if nk == 1 else [pltpu.VMEM((tm, tn), F32)],
        compiler_params=_cp(("parallel", "parallel", "arbitrary"), 56),
    )(a, b, *([] if dep is None else [dep]))


def _ffn_up(x, wgt, wut, name, tm=512, tn=1408):
    def body(x_ref, wg_ref, wu_ref, a_ref, b_ref, hm_ref):
        xb = x_ref[...].astype(BF16)
        a = _dot_nt(xb, wg_ref[...])
        b = _dot_nt(xb, wu_ref[...])
        a_ref[...] = a.astype(BF16)
        b_ref[...] = b.astype(BF16)
        hm_ref[...] = ((a * _sigmoid(a)) * b).astype(BF16)

    wsp = pl.BlockSpec((tn, D), lambda i, j: (j, 0))
    mid = pl.BlockSpec((tm, tn), lambda i, j: (i, j))
    return pl.pallas_call(
        body,
        name=name,
        grid=(S // tm, F // tn),
        in_specs=[pl.BlockSpec((tm, D), lambda i, j: (i, 0)), wsp, wsp],
        out_specs=[mid, mid, mid],
        out_shape=[jax.ShapeDtypeStruct((S, F), BF16)] * 3,
        compiler_params=_cp(("parallel", "arbitrary"), 56),
    )(x, wgt, wut)


def _ffn_down(hm, wd, x, ln_g, ln_b, name, tm=256):
    def body(hm_ref, wd_ref, x_ref, g_ref, b_ref, h_ref, hb_ref, xh_ref, rs_ref):
        r = ALPHA * x_ref[...] + 0.5 * _dot_nn(hm_ref[...], wd_ref[...])
        h, xhat, rstd = _ln_fwd(r, g_ref[...], b_ref[...])
        h_ref[...] = h
        hb_ref[...] = h.astype(BF16)
        xh_ref[...] = xhat
        rs_ref[...] = rstd

    row = pl.BlockSpec((tm, D), lambda i: (i, 0))
    vec = pl.BlockSpec((1, D), lambda i: (0, 0))
    return pl.pallas_call(
        body,
        name=name,
        grid=(S // tm,),
        in_specs=[pl.BlockSpec((tm, F), lambda i: (i, 0)), pl.BlockSpec((F, D), lambda i: (0, 0)), row, vec, vec],
        out_specs=[row, row, row, pl.BlockSpec((tm, 1), lambda i: (i, 0))],
        out_shape=[
            jax.ShapeDtypeStruct((S, D), F32),
            jax.ShapeDtypeStruct((S, D), BF16),
            jax.ShapeDtypeStruct((S, D), F32),
            jax.ShapeDtypeStruct((S, 1), F32),
        ],
        compiler_params=_cp(("parallel",), 56),
    )(hm, wd, x, ln_g, ln_b)


def _ffn_bwd_mid(dh_a, dh_b, xhat, rstd, ln_g, a, b, wd, name, tm=512, tn=1408):
    two = dh_b is not None

    def body(*refs):
        dha_ref = refs[0]
        dhb_ref = refs[1] if two else None
        (xh_ref, rs_ref, g_ref, a_ref, b_ref, wd_ref,
         dr_ref, df_ref, da_ref, db_ref, dg_ref, dbias_ref, df_scr) = refs[2 if two else 1:]
        i = pl.program_id(0)
        j = pl.program_id(1)

        @pl.when(j == 0)
        def _():
            dh = dha_ref[...]
            if two:
                dh = ALPHA * dh + dhb_ref[...]
            xhat = xh_ref[...]
            dr = _ln_bwd(dh, xhat, rs_ref[...], g_ref[...])
            dfb = (0.5 * dr).astype(BF16)
            dr_ref[...] = dr
            df_scr[...] = dfb
            df_ref[...] = dfb
            sg = _colsum(dh * xhat)
            sb = _colsum(dh)

            @pl.when(i == 0)
            def _():
                dg_ref[...] = sg
                dbias_ref[...] = sb

            @pl.when(i > 0)
            def _():
                dg_ref[...] += sg
                dbias_ref[...] += sb

        dhm = _dot_nt(df_scr[...], wd_ref[...])
        av = a_ref[...].astype(F32)
        bv = b_ref[...].astype(F32)
        sig = _sigmoid(av)
        da_ref[...] = (dhm * bv * (sig * (1.0 + av * (1.0 - sig)))).astype(BF16)
        db_ref[...] = (dhm * (av * sig)).astype(BF16)

    row = pl.BlockSpec((tm, D), lambda i, j: (i, 0))
    vec = pl.BlockSpec((1, D), lambda i, j: (0, 0))
    mid = pl.BlockSpec((tm, tn), lambda i, j: (i, j))
    ins = [dh_a] + ([dh_b] if two else []) + [xhat, rstd, ln_g, a, b, wd]
    in_specs = [row] * (2 if two else 1) + [row, pl.BlockSpec((tm, 1), lambda i, j: (i, 0)), vec, mid, mid,
                                            pl.BlockSpec((tn, D), lambda i, j: (j, 0))]
    return pl.pallas_call(
        body,
        name=name,
        grid=(S // tm, F // tn),
        in_specs=in_specs,
        out_specs=[row, row, mid, mid, vec, vec],
        out_shape=[
            jax.ShapeDtypeStruct((S, D), F32),
            jax.ShapeDtypeStruct((S, D), BF16),
            jax.ShapeDtypeStruct((S, F), BF16),
            jax.ShapeDtypeStruct((S, F), BF16),
            jax.ShapeDtypeStruct((1, D), F32),
            jax.ShapeDtypeStruct((1, D), F32),
        ],
        scratch_shapes=[pltpu.VMEM((tm, D), BF16)],
        compiler_params=_cp(("arbitrary", "arbitrary"), 56),
    )(*ins)


def _ffn_bwd_dx(dr, da, db, wgt, wut, name, tm=512, tk=1408, dep=None):
    nk = F // tk
    n_dep = 0 if dep is None else 1

    def body(dr_ref, da_ref, db_ref, wg_ref, wu_ref, *rest):
        dx_ref, acc = rest[n_dep], rest[n_dep + 1]
        kk = pl.program_id(1)
        part = _dot_nn(da_ref[...], wg_ref[...]) + _dot_nn(db_ref[...], wu_ref[...])

        @pl.when(kk == 0)
        def _():
            acc[...] = ALPHA * dr_ref[...] + part

        @pl.when(kk > 0)
        def _():
            acc[...] += part

        @pl.when(kk == nk - 1)
        def _():
            dx_ref[...] = acc[...]

    row = pl.BlockSpec((tm, D), lambda i, kk: (i, 0))
    mid = pl.BlockSpec((tm, tk), lambda i, kk: (i, kk))
    wsp = pl.BlockSpec((tk, D), lambda i, kk: (kk, 0))
    return pl.pallas_call(
        body,
        name=name,
        grid=(S // tm, nk),
        in_specs=[row, mid, mid, wsp, wsp] + [pl.BlockSpec(memory_space=pl.ANY)] * n_dep,
        out_specs=row,
        out_shape=jax.ShapeDtypeStruct((S, D), F32),
        scratch_shapes=[pltpu.VMEM((tm, D), F32)],
        compiler_params=_cp(("parallel", "arbitrary"), 56),
    )(dr, da, db, wgt, wut, *([] if dep is None else [dep]))


def _rope_tables(pos_col, invf, sign, tm=512):
    def body(p_ref, f_ref, s_ref, c_out, s_out):
        ang = p_ref[...].astype(F32) * f_ref[...]
        c_out[...] = jnp.cos(ang)
        s_out[...] = jnp.sin(ang) * s_ref[...]

    vec = pl.BlockSpec((1, BLK), lambda i: (0, 0))
    out = pl.BlockSpec((tm, BLK), lambda i: (i, 0))
    return pl.pallas_call(
        body,
        name="rope_tables",
        grid=(S // tm,),
        in_specs=[pl.BlockSpec((tm, 1), lambda i: (i, 0)), vec, vec],
        out_specs=[out, out],
        out_shape=[jax.ShapeDtypeStruct((S, BLK), F32)] * 2,
        compiler_params=_cp(("parallel",)),
    )(pos_col, invf, sign)


def _lane_lo(rows=BLK):
    return (lax.broadcasted_iota(jnp.int32, (rows, GW), 1) % HEAD_DIM) < (HEAD_DIM // 2)


def _swap_halves(t, lo):
    return jnp.where(lo, pltpu.roll(t, GW - HEAD_DIM // 2, 1), pltpu.roll(t, HEAD_DIM // 2, 1))


def _rope(t, cosf, sinf, lo):
    return t * cosf + _swap_halves(t, lo) * sinf


def _unrope(g, cosf, sinf, lo):
    return g * cosf + _swap_halves(g * sinf, lo)


def _tile4(v):
    return jnp.concatenate([v, v, v, v], axis=1)


def _band_mask(n):
    qi = lax.broadcasted_iota(jnp.int32, (BLK, 2 * BLK), 0)
    kj = lax.broadcasted_iota(jnp.int32, (BLK, 2 * BLK), 1)
    dist = qi + BLK - kj
    return (dist >= 0) & (dist <= BLK) & ((kj >= BLK) | (n >= 1))


ROWS = 256
LANES = 128


def _to_view(scr, y, dst_ref, d, dtype, col0=0):
    if d == 1:
        dst_ref[:, col0:col0 + GW] = y.astype(dtype)
        return
    for cb in range(GW // LANES):
        scr[cb][...] = y[:, cb * LANES:(cb + 1) * LANES]
    for r in range(d):
        for cb in range(GW // LANES):
            at = col0 + r * GW + cb * LANES
            dst_ref[:, at:at + LANES] = scr[cb][pl.ds(r, ROWS // d, stride=d), :].astype(dtype)


def _from_view(scr, src_ref, d):
    if d == 1:
        return src_ref[...].astype(F32)
    for r in range(d):
        for cb in range(GW // LANES):
            at = r * GW + cb * LANES
            scr[cb][pl.ds(r, ROWS // d, stride=d), :] = src_ref[:, at:at + LANES].astype(F32)
    return jnp.concatenate([scr[cb][...] for cb in range(GW // LANES)], axis=1)


def _view_spec(d):
    return pl.BlockSpec((ROWS // d, d * GW), lambda i: (i, 0))


_VIEW_SCRATCH = [pltpu.VMEM((ROWS, LANES), F32)] * (GW // LANES)


def _qkv_prep(qkv, cos_t, sin_t, gi):
    d = PATTERN_DILATIONS[gi]

    def body(q_ref, k_ref, v_ref, c_ref, s_ref, qo, ko, vo, *scr):
        lo = _lane_lo(ROWS)
        cf, sf = _tile4(c_ref[...]), _tile4(s_ref[...])
        _to_view(scr, _rope(q_ref[...], cf, sf, lo) * (HEAD_DIM ** -0.5), qo, d, BF16)
        _to_view(scr, _rope(k_ref[...], cf, sf, lo), ko, d, BF16)
        _to_view(scr, v_ref[...], vo, d, BF16)

    col = lambda c: pl.BlockSpec((ROWS, GW), lambda i: (i, c))
    tab = pl.BlockSpec((ROWS, BLK), lambda i: (i, 0))
    return pl.pallas_call(
        body,
        name=f"qkv_prep_g{gi}",
        grid=(S // ROWS,),
        in_specs=[col(gi), col(3 + gi), col(6 + gi), tab, tab],
        out_specs=[_view_spec(d)] * 3,
        out_shape=[jax.ShapeDtypeStruct((S // d, d * GW), BF16)] * 3,
        scratch_shapes=_VIEW_SCRATCH,
        compiler_params=_cp(("parallel",)),
    )(qkv, qkv, qkv, cos_t, sin_t)


def _attn_specs(d):
    cur = pl.BlockSpec((BLK, GW), lambda r, n: (n, r))
    prev = pl.BlockSpec((BLK, GW), lambda r, n: (jnp.maximum(n - 1, 0), r))
    return cur, prev


def _left_lanes():
    return lax.broadcasted_iota(jnp.int32, (BLK, LANES), 1) < HEAD_DIM


def _attn_fwd(qr, kr, vv, gi):
    d = PATTERN_DILATIONS[gi]
    sub = S // d
    cur, prev = _attn_specs(d)

    def body(q_ref, kc_ref, kp_ref, vc_ref, vp_ref, o_ref, l_ref):
        n = pl.program_id(1)
        mask = _band_mask(n)
        left = _left_lanes()
        zero = jnp.zeros((BLK, LANES), BF16)
        for pr in range(GW // LANES):
            cs = slice(pr * LANES, (pr + 1) * LANES)
            q2 = q_ref[:, cs]
            k2 = jnp.concatenate([kp_ref[:, cs], kc_ref[:, cs]], axis=0)
            v2 = jnp.concatenate([vp_ref[:, cs], vc_ref[:, cs]], axis=0)
            o_h, lse_h = [], []
            for side in (left, ~left):
                s = jnp.where(mask, _dot_nt(jnp.where(side, q2, zero), k2), NEG)
                m = jnp.max(s, axis=1, keepdims=True)
                p = jnp.exp(s - m)
                l = jnp.sum(p, axis=1, keepdims=True)
                o_h.append(_dot_nn((p / l).astype(BF16), v2))
                lse_h.append(m + jnp.log(l))
            o_ref[:, cs] = jnp.where(left, o_h[0], o_h[1])
            l_ref[:, cs] = jnp.where(left, lse_h[0], lse_h[1])

    return pl.pallas_call(
        body,
        name=f"attn_fwd_g{gi}",
        grid=(d, sub // BLK),
        in_specs=[cur, cur, prev, cur, prev],
        out_specs=[cur, cur],
        out_shape=[jax.ShapeDtypeStruct((sub, d * GW), F32)] * 2,
        compiler_params=_cp(("parallel", "parallel")),
    )(qr, kr, kr, vv, vv)


def _attn_bwd(qr, kr, vv, do, lse, cterm, gi):
    d = PATTERN_DILATIONS[gi]
    sub = S // d
    cur, prev = _attn_specs(d)

    def body(q_ref, kc_ref, kp_ref, vc_ref, vp_ref, do_ref, l_ref, c_ref, dq_ref, dk_ref, dv_ref):
        n = pl.program_id(1)
        mask = _band_mask(n)
        left = _left_lanes()
        zero = jnp.zeros((BLK, LANES), BF16)
        here = pl.ds(pl.multiple_of(n * BLK, BLK), BLK)
        before = pl.ds(pl.multiple_of(jnp.maximum(n - 1, 0) * BLK, BLK), BLK)
        for pr in range(GW // LANES):
            cs = slice(pr * LANES, (pr + 1) * LANES)
            q2 = q_ref[:, cs]
            k2 = jnp.concatenate([kp_ref[:, cs], kc_ref[:, cs]], axis=0)
            v2 = jnp.concatenate([vp_ref[:, cs], vc_ref[:, cs]], axis=0)
            do2 = do_ref[:, cs]
            dq_h, dk2, dv2 = [], None, None
            for h, side in enumerate((left, ~left)):
                at = pr * LANES + h * HEAD_DIM
                qm = jnp.where(side, q2, zero)
                dom = jnp.where(side, do2, zero)
                s = jnp.where(mask, _dot_nt(qm, k2), NEG)
                p = jnp.exp(s - l_ref[:, at:at + 1])
                ds = (p * (_dot_nt(dom, v2) - c_ref[:, at:at + 1])).astype(BF16)
                dq_h.append(_dot_nn(ds, k2))
                dk_part = _dot_tn(ds, qm)
                dv_part = _dot_tn(p.astype(BF16), dom)
                dk2 = dk_part if dk2 is None else dk2 + dk_part
                dv2 = dv_part if dv2 is None else dv2 + dv_part
            dq_ref[:, cs] = jnp.where(left, dq_h[0], dq_h[1]) * (HEAD_DIM ** -0.5)
            dk_ref[here, cs] = dk2[BLK:]
            dv_ref[here, cs] = dv2[BLK:]
            dk_ref[before, cs] += dk2[:BLK]
            dv_ref[before, cs] += dv2[:BLK]

    whole = pl.BlockSpec((sub, GW), lambda r, n: (0, r))
    return pl.pallas_call(
        body,
        name=f"attn_bwd_g{gi}",
        grid=(d, sub // BLK),
        in_specs=[cur, cur, prev, cur, prev, cur, cur, cur],
        out_specs=[cur, whole, whole],
        out_shape=[jax.ShapeDtypeStruct((sub, d * GW), F32)] * 3,
        compiler_params=_cp(("arbitrary", "arbitrary"), 48),
    )(qr, kr, kr, vv, vv, do, lse, cterm)


def _qkv_unprep(dq, dk, dv, cos_t, sin_t, gi, dproj):
    d = PATTERN_DILATIONS[gi]

    def body(dq_ref, dk_ref, dv_ref, c_ref, s_ref, _, out_ref, *scr):
        lo = _lane_lo(ROWS)
        cf, sf = _tile4(c_ref[...]), _tile4(s_ref[...])
        out_ref[:, 0:GW] = _unrope(_from_view(scr, dq_ref, d), cf, sf, lo).astype(BF16)
        out_ref[:, GW:2 * GW] = _unrope(_from_view(scr, dk_ref, d), cf, sf, lo).astype(BF16)
        out_ref[:, 2 * GW:3 * GW] = _from_view(scr, dv_ref, d).astype(BF16)

    tab = pl.BlockSpec((ROWS, BLK), lambda i: (i, 0))
    return pl.pallas_call(
        body,
        name=f"qkv_unprep_g{gi}",
        grid=(S // ROWS,),
        in_specs=[_view_spec(d)] * 3 + [tab, tab, pl.BlockSpec(memory_space=pl.ANY)],
        out_specs=pl.BlockSpec((ROWS, 3 * GW), lambda i: (i, gi)),
        out_shape=jax.ShapeDtypeStruct(dproj.shape, dproj.dtype),
        input_output_aliases={5: 0},
        scratch_shapes=_VIEW_SCRATCH,
        compiler_params=_cp(("parallel",)),
    )(dq, dk, dv, cos_t, sin_t, dproj)


def _group_weights(l0, l1, l2):
    mx = jnp.maximum(jnp.maximum(l0, l1), l2)
    e0, e1, e2 = jnp.exp(l0 - mx), jnp.exp(l1 - mx), jnp.exp(l2 - mx)
    inv = 1.0 / (e0 + e1 + e2)
    return e0 * inv, e1 * inv, e2 * inv


def _combine_fwd(os_, lses):
    def body(o0, o1, o2, l0, l1, l2, y_ref, *scr):
        ov = [_from_view(scr, o, d) for o, d in zip((o0, o1, o2), PATTERN_DILATIONS)]
        lv = [_from_view(scr, l, d) for l, d in zip((l0, l1, l2), PATTERN_DILATIONS)]
        w0, w1, w2 = _group_weights(*lv)
        y_ref[...] = (w0 * ov[0] + w1 * ov[1] + w2 * ov[2]).astype(BF16)

    views = [_view_spec(d) for d in PATTERN_DILATIONS]
    return pl.pallas_call(
        body,
        name="attn_combine_fwd",
        grid=(S // ROWS,),
        in_specs=views + views,
        out_specs=pl.BlockSpec((ROWS, GW), lambda i: (i, 0)),
        out_shape=jax.ShapeDtypeStruct((S, GW), BF16),
        scratch_shapes=_VIEW_SCRATCH,
        compiler_params=_cp(("parallel",)),
    )(*os_, *lses)


def _combine_bwd(dy, os_, lses, seg):
    def body(dy_ref, o0, o1, o2, l0, l1, l2, seg_ref, d0, d1, d2, c0, c1, c2, *scr):
        ov = [_from_view(scr, o, d) for o, d in zip((o0, o1, o2), PATTERN_DILATIONS)]
        lv = [_from_view(scr, l, d) for l, d in zip((l0, l1, l2), PATTERN_DILATIONS)]
        ws = _group_weights(*lv)
        dyv = dy_ref[...]
        t = dyv * (ws[0] * ov[0] + ws[1] * ov[1] + ws[2] * ov[2])
        t_hi = t.astype(BF16)
        r1 = t - t_hi.astype(F32)
        t_mid = r1.astype(BF16)
        t_lo = (r1 - t_mid.astype(F32)).astype(BF16)
        sg = seg_ref[...]
        e = _dot_nn(t_hi, sg) + _dot_nn(t_mid, sg) + _dot_nn(t_lo, sg)
        for w, do_ref, c_ref, d in zip(ws, (d0, d1, d2), (c0, c1, c2), PATTERN_DILATIONS):
            _to_view(scr, w * dyv, do_ref, d, BF16)
            _to_view(scr, w * e, c_ref, d, F32)

    views = [_view_spec(d) for d in PATTERN_DILATIONS]
    return pl.pallas_call(
        body,
        name="attn_combine_bwd",
        grid=(S // ROWS,),
        in_specs=[pl.BlockSpec((ROWS, GW), lambda i: (i, 0))] + views + views + [pl.BlockSpec((GW, GW), lambda i: (0, 0))],
        out_specs=views + views,
        out_shape=[jax.ShapeDtypeStruct((S // d, d * GW), BF16) for d in PATTERN_DILATIONS]
        + [jax.ShapeDtypeStruct((S // d, d * GW), F32) for d in PATTERN_DILATIONS],
        scratch_shapes=_VIEW_SCRATCH,
        compiler_params=_cp(("parallel",)),
    )(dy, *os_, *lses, seg)


_SQRT_HALF = 0.7071067811865476
_INV_SQRT_2PI = 0.3989422804014327


def _gelu(z):
    return 0.5 * z * (1.0 + lax.erf(z * _SQRT_HALF))


def _gelu_grad(z):
    return 0.5 * (1.0 + lax.erf(z * _SQRT_HALF)) + z * (_INV_SQRT_2PI * jnp.exp(-0.5 * z * z))


def _tril_ws(ws_ref, g):
    t = lax.broadcasted_iota(jnp.int32, (BLK, BLK), 0)
    s = lax.broadcasted_iota(jnp.int32, (BLK, BLK), 1)
    return jnp.where(t >= s, ws_ref[g], 0.0)


def _gmlp_fwd(z, ws, bst, ln_g, ln_b, tm=256):
    nch = tm // BLK

    def body(z_ref, ws_ref, b_ref, g_ref, be_ref, y_ref):
        zg = _gelu(z_ref[...])
        u = zg[:, :D]
        vn, _, _ = _ln_fwd(zg[:, D:], g_ref[...], be_ref[...])
        vnb = vn.astype(BF16)
        bt = b_ref[...]
        for g in range(8):
            w = _tril_ws(ws_ref, g).astype(BF16)
            cols = slice(g * BLK, (g + 1) * BLK)
            for c in range(nch):
                rows = slice(c * BLK, (c + 1) * BLK)
                mixed = _dot_nn(w, vnb[rows, cols]) + bt[:, g:g + 1]
                y_ref[rows, cols] = (u[rows, cols] * mixed).astype(BF16)

    return pl.pallas_call(
        body,
        name="gmlp_fwd",
        grid=(S // tm,),
        in_specs=[
            pl.BlockSpec((tm, 2 * D), lambda i: (i, 0)),
            pl.BlockSpec((8, BLK, BLK), lambda i: (0, 0, 0)),
            pl.BlockSpec((BLK, 8), lambda i: (0, 0)),
            pl.BlockSpec((1, D), lambda i: (0, 0)),
            pl.BlockSpec((1, D), lambda i: (0, 0)),
        ],
        out_specs=pl.BlockSpec((tm, D), lambda i: (i, 0)),
        out_shape=jax.ShapeDtypeStruct((S, D), BF16),
        compiler_params=_cp(("parallel",), 48),
    )(z, ws, bst, ln_g, ln_b)


def _gmlp_bwd(z, dy, ws, bst, ln_g, ln_b, dproj, tm=256):
    nch = tm // BLK

    def body(z_ref, dy_ref, ws_ref, b_ref, g_ref, be_ref, _, dz_ref, dws_ref, dbs_ref, dg_ref, dbe_ref, dvn_scr, dm_acc):
        i = pl.program_id(0)
        zv = z_ref[...]
        zg = _gelu(zv)
        u = zg[:, :D]
        gam = g_ref[...]
        vn, xhat, rstd = _ln_fwd(zg[:, D:], gam, be_ref[...])
        vnb = vn.astype(BF16)
        dyv = dy_ref[...]
        dmix = dyv * u
        dmb = dmix.astype(BF16)
        bt = b_ref[...]
        tmask = lax.broadcasted_iota(jnp.int32, (BLK, BLK), 0) >= lax.broadcasted_iota(jnp.int32, (BLK, BLK), 1)
        dm_sum = dmix[0:BLK]
        for c in range(1, nch):
            dm_sum = dm_sum + dmix[c * BLK:(c + 1) * BLK]

        @pl.when(i == 0)
        def _():
            dm_acc[...] = dm_sum

        @pl.when(i > 0)
        def _():
            dm_acc[...] += dm_sum

        dus = []
        for g in range(8):
            w = _tril_ws(ws_ref, g).astype(BF16)
            cols = slice(g * BLK, (g + 1) * BLK)
            dw = None
            du_rows = []
            for c in range(nch):
                rows = slice(c * BLK, (c + 1) * BLK)
                mixed = _dot_nn(w, vnb[rows, cols]) + bt[:, g:g + 1]
                du_rows.append(dyv[rows, cols] * mixed)
                part = _dot_nt(dmb[rows, cols], vnb[rows, cols])
                dw = part if dw is None else dw + part
                dvn_scr[rows, cols] = _dot_tn(w, dmb[rows, cols])
            dus.append(jnp.concatenate(du_rows, axis=0))
            dw = jnp.where(tmask, dw, 0.0)

            @pl.when(i == 0)
            def _():
                dws_ref[g] = dw

            @pl.when(i > 0)
            def _():
                dws_ref[g] += dw

        dvn = dvn_scr[...]
        sg = _colsum(dvn * xhat)
        sb = _colsum(dvn)

        @pl.when(i == 0)
        def _():
            dg_ref[...] = sg
            dbe_ref[...] = sb

        @pl.when(i > 0)
        def _():
            dg_ref[...] += sg
            dbe_ref[...] += sb

        dvg = _ln_bwd(dvn, xhat, rstd, gam)
        gp = _gelu_grad(zv)
        dz_ref[:, :D] = (jnp.concatenate(dus, axis=1) * gp[:, :D]).astype(BF16)
        dz_ref[:, D:] = (dvg * gp[:, D:]).astype(BF16)

        @pl.when(i == S // tm - 1)
        def _():
            acc = dm_acc[...]
            for g in range(8):
                dbs_ref[:, g:g + 1] = jnp.sum(acc[:, g * BLK:(g + 1) * BLK], axis=1, keepdims=True)

    vec = pl.BlockSpec((1, D), lambda i: (0, 0))
    return pl.pallas_call(
        body,
        name="gmlp_bwd",
        grid=(S // tm,),
        in_specs=[
            pl.BlockSpec((tm, 2 * D), lambda i: (i, 0)),
            pl.BlockSpec((tm, D), lambda i: (i, 0)),
            pl.BlockSpec((8, BLK, BLK), lambda i: (0, 0, 0)),
            pl.BlockSpec((BLK, 8), lambda i: (0, 0)),
            vec,
            vec,
            pl.BlockSpec(memory_space=pl.ANY),
        ],
        out_specs=[
            pl.BlockSpec((tm, 2 * D), lambda i: (i, DPROJ_Z_COL)),
            pl.BlockSpec((8, BLK, BLK), lambda i: (0, 0, 0)),
            pl.BlockSpec((BLK, 8), lambda i: (0, 0)),
            vec,
            vec,
        ],
        out_shape=[
            jax.ShapeDtypeStruct(dproj.shape, dproj.dtype),
            jax.ShapeDtypeStruct((8, BLK, BLK), F32),
            jax.ShapeDtypeStruct((BLK, 8), F32),
            jax.ShapeDtypeStruct((1, D), F32),
            jax.ShapeDtypeStruct((1, D), F32),
        ],
        input_output_aliases={6: 0},
        scratch_shapes=[pltpu.VMEM((tm, D), F32), pltpu.VMEM((BLK, D), F32)],
        compiler_params=_cp(("arbitrary",), 48),
    )(z, dy, ws, bst, ln_g, ln_b, dproj)


def _merge_fwd(ya, yg, glog, bgate, h1, wabt, wgb, wo, ln_g, ln_b, tm=256):
    def body(ya_ref, yg_ref, gl_ref, bg_ref, h1_ref, wab_ref, wgb_ref, wo_ref, g_ref, b_ref,
             h_ref, hb_ref, xh_ref, rs_ref, mg_ref, bra_ref, brg_ref):
        bra = _dot_nt(ya_ref[...], wab_ref[...])
        brg = _dot_nn(yg_ref[...], wgb_ref[...])
        gates = _sigmoid(gl_ref[...] + bg_ref[...])
        merged = (gates[:, :D] * bra + gates[:, D:] * brg).astype(BF16)
        mix = _dot_nn(merged, wo_ref[...])
        h, xhat, rstd = _ln_fwd(ALPHA * h1_ref[...] + mix, g_ref[...], b_ref[...])
        h_ref[...] = h
        hb_ref[...] = h.astype(BF16)
        xh_ref[...] = xhat
        rs_ref[...] = rstd
        mg_ref[...] = merged
        bra_ref[...] = bra
        brg_ref[...] = brg

    row = pl.BlockSpec((tm, D), lambda i: (i, 0))
    vec = pl.BlockSpec((1, D), lambda i: (0, 0))
    full = lambda shape: pl.BlockSpec(shape, lambda i: (0, 0))
    return pl.pallas_call(
        body,
        name="merge_fwd",
        grid=(S // tm,),
        in_specs=[
            pl.BlockSpec((tm, GW), lambda i: (i, 0)), row,
            pl.BlockSpec((tm, 2 * D), lambda i: (i, glog.shape[1] // (2 * D) - 1)),
            full((1, 2 * D)), row,
            full((D, GW)), full((D, D)), full((D, D)), vec, vec,
        ],
        out_specs=[row, row, row, pl.BlockSpec((tm, 1), lambda i: (i, 0)), row, row, row],
        out_shape=[
            jax.ShapeDtypeStruct((S, D), F32),
            jax.ShapeDtypeStruct((S, D), BF16),
            jax.ShapeDtypeStruct((S, D), F32),
            jax.ShapeDtypeStruct((S, 1), F32),
            jax.ShapeDtypeStruct((S, D), BF16),
            jax.ShapeDtypeStruct((S, D), F32),
            jax.ShapeDtypeStruct((S, D), F32),
        ],
        compiler_params=_cp(("parallel",), 48),
    )(ya, yg, glog, bgate, h1, wabt, wgb, wo, ln_g, ln_b)


def _merge_bwd(dh2, xhat, rstd, ln_g, bra, brg, glog, bgate, wabt, wgb, wo, tm=256):
    def body(dh_ref, xh_ref, rs_ref, g_ref, bra_ref, brg_ref, gl_ref, bg_ref, wab_ref, wgb_ref, wo_ref,
             dr_ref, drb_ref, dlog_ref, dba_ref, dbg_ref, dya_ref, dyg_ref, dbgate_ref, dg_ref, dbias_ref):
        i = pl.program_id(0)
        dh = dh_ref[...]
        xh = xh_ref[...]
        dr = _ln_bwd(dh, xh, rs_ref[...], g_ref[...])
        drb = dr.astype(BF16)
        dr_ref[...] = dr
        drb_ref[...] = drb
        dmerged = _dot_nt(drb, wo_ref[...])
        gates = _sigmoid(gl_ref[...] + bg_ref[...])
        g0, g1 = gates[:, :D], gates[:, D:]
        dl0 = dmerged * bra_ref[...] * g0 * (1.0 - g0)
        dl1 = dmerged * brg_ref[...] * g1 * (1.0 - g1)
        dlog_ref[:, :D] = dl0.astype(BF16)
        dlog_ref[:, D:] = dl1.astype(BF16)
        dba = (dmerged * g0).astype(BF16)
        dbg = (dmerged * g1).astype(BF16)
        dba_ref[...] = dba
        dbg_ref[...] = dbg
        dya_ref[...] = _dot_nn(dba, wab_ref[...])
        dyg_ref[...] = _dot_nt(dbg, wgb_ref[...])
        s0, s1 = _colsum(dl0), _colsum(dl1)
        sg, sb = _colsum(dh * xh), _colsum(dh)

        @pl.when(i == 0)
        def _():
            dbgate_ref[:, :D] = s0
            dbgate_ref[:, D:] = s1
            dg_ref[...] = sg
            dbias_ref[...] = sb

        @pl.when(i > 0)
        def _():
            dbgate_ref[:, :D] += s0
            dbgate_ref[:, D:] += s1
            dg_ref[...] += sg
            dbias_ref[...] += sb

    row = pl.BlockSpec((tm, D), lambda i: (i, 0))
    vec = pl.BlockSpec((1, D), lambda i: (0, 0))
    wide = pl.BlockSpec((tm, 2 * D), lambda i: (i, 0))
    full = lambda shape: pl.BlockSpec(shape, lambda i: (0, 0))
    return pl.pallas_call(
        body,
        name="merge_bwd",
        grid=(S // tm,),
        in_specs=[row, row, pl.BlockSpec((tm, 1), lambda i: (i, 0)), vec, row, row,
                  pl.BlockSpec((tm, 2 * D), lambda i: (i, glog.shape[1] // (2 * D) - 1)),
                  full((1, 2 * D)), full((D, GW)), full((D, D)), full((D, D))],
        out_specs=[row, row, pl.BlockSpec((tm, 2 * D), lambda i: (i, DPROJ_G_COL)), row, row,
                   pl.BlockSpec((tm, GW), lambda i: (i, 0)), row, full((1, 2 * D)), vec, vec],
        out_shape=[
            jax.ShapeDtypeStruct((S, D), F32),
            jax.ShapeDtypeStruct((S, D), BF16),
            jax.ShapeDtypeStruct((S, DPROJ_W), BF16),
            jax.ShapeDtypeStruct((S, D), BF16),
            jax.ShapeDtypeStruct((S, D), BF16),
            jax.ShapeDtypeStruct((S, GW), F32),
            jax.ShapeDtypeStruct((S, D), F32),
            jax.ShapeDtypeStruct((1, 2 * D), F32),
            jax.ShapeDtypeStruct((1, D), F32),
            jax.ShapeDtypeStruct((1, D), F32),
        ],
        compiler_params=_cp(("arbitrary",), 48),
    )(dh2, xhat, rstd, ln_g, bra, brg, glog, bgate, wabt, wgb, wo)


def _loss_head(h3, target, tm=512):
    def body(h_ref, t_ref, d_ref, l_ref):
        i = pl.program_id(0)
        e = h_ref[...] - t_ref[...]
        d_ref[...] = e * (1.0 / D)
        part = jnp.sum(_colsum(e * e), axis=1, keepdims=True) * (0.5 / D)

        @pl.when(i == 0)
        def _():
            l_ref[...] = part

        @pl.when(i > 0)
        def _():
            l_ref[...] += part

    row = pl.BlockSpec((tm, D), lambda i: (i, 0))
    return pl.pallas_call(
        body,
        name="loss_head",
        grid=(S // tm,),
        in_specs=[row, row],
        out_specs=[row, pl.BlockSpec((1, 1), lambda i: (0, 0))],
        out_shape=[jax.ShapeDtypeStruct((S, D), F32), jax.ShapeDtypeStruct((1, 1), F32)],
        compiler_params=_cp(("arbitrary",)),
    )(h3, target)


def _tie(x, dep):
    if dep is None:
        return x
    return x + dep[0, 0].astype(x.dtype)


def _local_step(x, pos_col, target, get_w, p, emit):
    w = get_w("ffn1", None)
    a1, b1, hm1 = _ffn_up(x, w["g1"], w["u1"], "ffn1_up")
    h1, h1b, xh1, rs1 = _ffn_down(hm1, w["d1"], x, p["ln1_g"], p["ln1_b"], "ffn1_down")

    w.update(get_w("win", h1b))
    qkv = _matmul(h1b, w["win"], "nt", F32, 1024, 1536, D, "proj_qkv", b_off=0, n_out=QKV_W)
    z = glog = _matmul(h1b, w["win"], "nt", F32, 1024, 512, D, "proj_zg", b_off=QKV_W // 512, n_out=4 * D)

    half = jnp.arange(0, HEAD_DIM, 2, dtype=F32) / HEAD_DIM
    inv_freq = ROPE_THETA ** (-half)
    invf = jnp.tile(inv_freq, 4).reshape(1, BLK)
    sign = jnp.tile(jnp.concatenate([-jnp.ones((32,), F32), jnp.ones((32,), F32)]), 2).reshape(1, BLK)
    cos_t, sin_t = _rope_tables(pos_col, invf, sign)

    get_w("mix", qkv, early=True)
    preps = [_qkv_prep(qkv, cos_t, sin_t, gi) for gi in range(3)]
    os_, lses = [], []
    for gi in range(3):
        o, lse = _attn_fwd(*preps[gi], gi)
        os_.append(o)
        lses.append(lse)
    get_w("ffn2", os_[1], early=True)
    ya = _combine_fwd(os_, lses)
    bst = p["gmlp_b_s"].T
    yg = _gmlp_fwd(z, p["gmlp_w_s"], bst, p["gmlp_ln_g"], p["gmlp_ln_b"])
    w.update(get_w("mix", yg))
    h2, h2b, xh2, rs2, merged, bra, brg = _merge_fwd(ya, yg, glog, p["b_gates"], h1, w["ab"], w["gb"], w["o"],
                                                      p["ln2_g"], p["ln2_b"])
    w.update(get_w("ffn2", h2b))
    a2, b2, hm2 = _ffn_up(h2b, w["g2"], w["u2"], "ffn2_up")
    h3, _, xh3, rs3 = _ffn_down(hm2, w["d2"], h2, p["ln3_g"], p["ln3_b"], "ffn2_down")
    dh3, loss = _loss_head(h3, target)

    gp = {}
    dr3, df2, da2, db2, gp["ln3_g"], gp["ln3_b"] = _ffn_bwd_mid(dh3, None, xh3, rs3, p["ln3_g"], a2, b2, w["d2"],
                                                                "ffn2_bwd_mid")
    tok = emit("ffn2", {
        "g2": _matmul(da2, h2b, "tn", BF16, 1408, D, S, "wgrad_g2"),
        "u2": _matmul(db2, h2b, "tn", BF16, 1408, D, S, "wgrad_u2"),
        "d2": _matmul(hm2, df2, "tn", BF16, 1408, D, S, "wgrad_d2")})
    dh2 = _ffn_bwd_dx(dr3, da2, db2, w["g2"], w["u2"], "ffn2_bwd_dx")

    (dr2, dr2b, dproj, dba, dbg, dya, dyg, gp["b_gates"], gp["ln2_g"], gp["ln2_b"]) = _merge_bwd(
        dh2, xh2, rs2, _tie(p["ln2_g"], tok), bra, brg, glog, p["b_gates"], w["ab"], w["gb"], w["o"])
    tok = emit("mix", {
        "o": _matmul(merged, dr2b, "tn", BF16, 512, D, S, "wgrad_o"),
        "ab": _matmul(dba, ya, "tn", BF16, 512, GW, S, "wgrad_ab"),
        "gb": _matmul(yg, dbg, "tn", BF16, 512, D, S, "wgrad_gb")})

    seg = (jnp.arange(GW)[:, None] // HEAD_DIM == jnp.arange(GW)[None, :] // HEAD_DIM).astype(BF16)
    do0, do1, do2, c0, c1, c2 = _combine_bwd(dya, os_, lses, _tie(seg, tok))
    for gi, (do, ct) in enumerate(((do0, c0), (do1, c1), (do2, c2))):
        dq, dk, dv = _attn_bwd(*preps[gi], do, lses[gi], ct, gi)
        dproj = _qkv_unprep(dq, dk, dv, cos_t, sin_t, gi, dproj)
    dproj, gp["gmlp_w_s"], dbst, gp["gmlp_ln_g"], gp["gmlp_ln_b"] = _gmlp_bwd(
        z, dyg, p["gmlp_w_s"], bst, p["gmlp_ln_g"], p["gmlp_ln_b"], dproj)
    gp["gmlp_b_s"] = dbst.T
    tok = emit("win", {"win": _matmul(dproj, h1b, "tn", BF16, 512, D, S, "wgrad_win", a_map=_dproj_tile, m_out=IN_W)})
    dh1m = _matmul(dproj, w["win"], "nn", F32, 1024, D, 512, "dproj_to_dh1", dep=tok, a_map=_dproj_tile)

    dr1, df1, da1, db1, gp["ln1_g"], gp["ln1_b"] = _ffn_bwd_mid(dr2, dh1m, xh1, rs1, p["ln1_g"], a1, b1, w["d1"],
                                                                "ffn1_bwd_mid")
    tok = emit("small", {**gp, "loss": loss})
    tok = emit("g1", {"g1": _matmul(da1, x, "tn", BF16, 1408, D, S, "wgrad_g1", dep=tok)})
    tok = emit("u1", {"u1": _matmul(db1, x, "tn", BF16, 1408, D, S, "wgrad_u1", dep=tok)})
    tok = emit("d1", {"d1": _matmul(hm1, df1, "tn", BF16, 1408, D, S, "wgrad_d1", dep=tok)})
    dx = _ffn_bwd_dx(dr1, da1, db1, w["g1"], w["u1"], "ffn1_bwd_dx", dep=tok)
    return loss, dx, gp


_FLIPS = [(mx, my, mc) for mx in (0, 1) for my in (0, 1) for mc in (0, 1)][1:]
_GROUPS = {"ffn1": ("g1", "u1", "d1"), "win": ("win",), "mix": ("ab", "gb", "o"), "ffn2": ("g2", "u2", "d2")}
_SCATTERS = {"ffn2": ("g2", "u2", "d2"), "mix": ("ab", "gb", "o"), "win": ("win",), "g1": ("g1",), "u1": ("u1",), "d1": ("d1",)}
_TWO_STAGE = ("g1", "u1", "d1")
_HBM = pl.BlockSpec(memory_space=pltpu.HBM)
_SEM = pl.BlockSpec(memory_space=pltpu.SEMAPHORE)
_EFFECT = pltpu.SideEffectType.DATAFLOW_SIDE_EFFECTING


def _me():
    return 4 * lax.axis_index("x") + 2 * lax.axis_index("y") + lax.axis_index("c")


def _copies_start(bufs, copies, n_sem, name, after=None):
    nb = len(bufs)
    n_after = 0 if after is None else 1

    def body(*refs):
        b = refs[:nb]
        send_sems, recv_sems = refs[nb + n_after], refs[nb + n_after + 1]
        token = refs[-1]
        x, y, c = lax.axis_index("x"), lax.axis_index("y"), lax.axis_index("c")
        me = 4 * x + 2 * y + c
        for si, s_slot, di, d_slot, (mx, my, mc), sem in copies:
            s_idx, d_idx = s_slot(me), d_slot(me)
            pltpu.make_async_remote_copy(
                src_ref=b[si] if s_idx is None else b[si].at[s_idx],
                dst_ref=b[di] if d_idx is None else b[di].at[d_idx],
                send_sem=send_sems.at[sem], recv_sem=recv_sems.at[sem],
                device_id=(x ^ mx, y ^ my, c ^ mc), device_id_type=MESH).start()
        token[...] = jnp.zeros_like(token)

    ins = [pltpu.with_memory_space_constraint(a, pltpu.HBM) for a in bufs]
    outs = pl.pallas_call(
        body,
        name=name,
        in_specs=[_HBM] * nb + [pl.BlockSpec(memory_space=pl.ANY)] * n_after,
        out_specs=[_SEM, _SEM] + [_HBM] * nb + [pl.BlockSpec(memory_space=pltpu.VMEM)],
        out_shape=[pltpu.SemaphoreType.DMA((n_sem,)), pltpu.SemaphoreType.DMA((n_sem,))]
        + [pltpu.HBM(a.shape, a.dtype) for a in ins] + [jax.ShapeDtypeStruct((8, 128), F32)],
        input_output_aliases={k: 2 + k for k in range(nb)},
        compiler_params=pltpu.CompilerParams(has_side_effects=_EFFECT),
    )(*ins, *([] if after is None else [after]))
    return outs[0], outs[1], list(outs[2:2 + nb]), outs[-1]


def _copies_wait(started, waits, after, name):
    send_sems, recv_sems, bufs, _ = started
    nb = len(bufs)

    def body(*refs):
        b = refs[:nb]
        ss, rs = refs[nb], refs[nb + 1]
        me3 = (lax.axis_index("x"), lax.axis_index("y"), lax.axis_index("c"))
        for bi, n_blocks, sem, is_send in waits:
            blocks = b[bi].at[pl.ds(0, n_blocks)]
            cp = pltpu.make_async_remote_copy(src_ref=blocks, dst_ref=blocks, send_sem=ss.at[sem], recv_sem=rs.at[sem],
                                              device_id=me3, device_id_type=MESH)
            if is_send:
                cp.wait_send()
            else:
                cp.wait_recv()

    return pl.pallas_call(
        body,
        name=name,
        in_specs=[_HBM] * nb + [_SEM, _SEM, pl.BlockSpec(memory_space=pl.ANY)],
        out_specs=[_HBM] * nb,
        out_shape=[pltpu.HBM(a.shape, a.dtype) for a in bufs],
        input_output_aliases={k: k for k in range(nb)},
        compiler_params=pltpu.CompilerParams(has_side_effects=_EFFECT),
    )(*bufs, send_sems, recv_sems, after)


def _landing(own, me):
    return lax.dynamic_update_slice(lax.empty((N_DEV,) + own.shape[1:], own.dtype), own, (me, 0, 0))


_SIBLING = (0, 0, 1)
_OTHER_CHIPS = ((1, 0, 0), (0, 1, 0), (1, 1, 0))


def _bits(flip):
    return 4 * flip[0] + 2 * flip[1] + flip[2]


def _gather_send(shards, grp, after):
    nw = len(shards)
    me = _me()
    bufs = list(shards) + [_landing(a[None], me) for a in shards]
    copies = []
    for k in range(nw):
        copies.append((k, lambda me: None, nw + k, lambda me: me, _SIBLING, nw + k))
        copies += [(k, lambda me: None, nw + k, lambda me: me, f, k) for f in _OTHER_CHIPS]
    return _copies_start(bufs, copies, 2 * nw, "gather_send_" + grp, after)


def _gather_pass(started, grp, after):
    nw = len(started[2]) // 2
    waits = []
    for k in range(nw):
        waits += [(nw + k, 3, k, True), (nw + k, 1, nw + k, True), (nw + k, 3, k, False), (nw + k, 1, nw + k, False)]
    lands = list(_copies_wait(started, waits, after, "gather_arrived_" + grp)[nw:])
    copies = []
    for k in range(nw):
        for f in _OTHER_CHIPS:
            slot = functools.partial(lambda me, bits: me ^ bits, bits=_bits(f))
            copies.append((k, slot, k, slot, _SIBLING, k))
    return _copies_start(lands, copies, nw, "gather_pass_" + grp)


def _gather_finish(passed, grp, after):
    nw = len(passed[2])
    waits = [(k, 3, k, is_send) for k in range(nw) for is_send in (True, False)]
    return _copies_wait(passed, waits, passed[3] if after is None else after, "gather_done_" + grp)


def _scatter_send(parts, grp, after=None):
    nw = len(parts)
    me = _me()
    bufs = list(parts) + [_landing(lax.dynamic_index_in_dim(a, me, 0, keepdims=True), me) for a in parts]
    copies = []
    for k in range(nw):
        for f in _FLIPS:
            to = functools.partial(lambda me, bits: me ^ bits, bits=_bits(f))
            copies.append((k, to, nw + k, lambda me: me, f, k))
    return _copies_start(bufs, copies, nw, "scatter_send_" + grp, after)


def _scatter_finish(started, grp, after):
    nw = len(started[2]) // 2
    waits = [(nw + k, N_DEV - 1, k, is_send) for k in range(nw) for is_send in (True, False)]
    return _copies_wait(started, waits, after, "scatter_done_" + grp)[nw:]


N_CHIP = N_DEV // 2


def _pair_sum(part, other, key):
    _, r, c = part.shape
    tm = _ADAM_ROWS.get(r, r)
    core = lax.axis_index("c").astype(jnp.int32).reshape(1)

    def body(core_ref, a_ref, b_ref, o_ref):
        o_ref[...] = (a_ref[...].astype(F32) + b_ref[...].astype(F32)).astype(BF16)

    return pl.pallas_call(
        body,
        name="pair_sum_" + key,
        grid_spec=pltpu.PrefetchScalarGridSpec(
            num_scalar_prefetch=1,
            grid=(N_CHIP, r // tm),
            in_specs=[pl.BlockSpec((None, tm, c), lambda q, i, core_ref: (2 * q + core_ref[0], i, 0)),
                      pl.BlockSpec((None, tm, c), lambda q, i, core_ref: (q, i, 0))],
            out_specs=pl.BlockSpec((None, tm, c), lambda q, i, core_ref: (q, i, 0)),
        ),
        out_shape=jax.ShapeDtypeStruct((N_CHIP, r, c), BF16),
        compiler_params=_cp(("parallel", "parallel")),
    )(core, part, other)


def _scatter2_send(part, key, dep=None):
    me = _me()
    _, r, c = part.shape
    swap = []
    for q in range(N_CHIP):
        src = functools.partial(lambda me, q: 2 * q + 1 - me % 2, q=q)
        swap.append((0, src, 1, functools.partial(lambda me, q: q, q=q), _SIBLING, 0))
    started = _copies_start([part, lax.empty((N_CHIP, r, c), part.dtype)], swap, 1, "scatter_swap_" + key, dep)
    waits = [(1, N_CHIP, 0, True), (1, N_CHIP, 0, False)]
    part, other = _copies_wait(started, waits, started[3], "scatter_swapped_" + key)
    pair = _pair_sum(part, other, key)
    chip = me // 2
    land = lax.dynamic_update_slice(lax.empty((N_CHIP, r, c), pair.dtype),
                                    lax.dynamic_index_in_dim(pair, chip, 0, keepdims=True), (chip, 0, 0))
    copies = []
    for f in _OTHER_CHIPS:
        to = functools.partial(lambda me, bits: (me ^ bits) // 2, bits=_bits(f))
        copies.append((0, to, 1, lambda me: me // 2, f, 0))
    return _copies_start([pair, land], copies, 1, "scatter_send_" + key)


def _scatter2_finish(started, key, after):
    waits = [(1, N_CHIP - 1, 0, True), (1, N_CHIP - 1, 0, False)]
    return _copies_wait(started, waits, after, "scatter_done_" + key)[1]


def _allgather_send(block, name, after=None):
    me = _me()
    copies = [(0, lambda me: None, 1, lambda me: me, f, 0) for f in _FLIPS]
    return _copies_start([block, _landing(block[None], me)], copies, 1, name, after)


def _allgather_finish(started, name, after):
    waits = [(1, N_DEV - 1, 0, True), (1, N_DEV - 1, 0, False)]
    return _copies_wait(started, waits, after, name)[1]


_ADAM_ROWS = {352: 176, 1088: 272}


def _sum_adamw(parts, wv, m, v, name):
    n_parts, r, c = parts.shape
    tm = _ADAM_ROWS.get(r, r)
    assert r % tm == 0 and wv.shape == (r, c)

    def body(p_ref, w_ref, m_ref, v_ref, g_ref, d_ref, mo_ref, vo_ref):
        gv = p_ref[0].astype(F32)
        for j in range(1, n_parts):
            gv = gv + p_ref[j].astype(F32)
        g_ref[...] = gv
        mn = ADAM_B1 * m_ref[...] + (1.0 - ADAM_B1) * gv
        vn = ADAM_B2 * v_ref[...] + (1.0 - ADAM_B2) * (gv * gv)
        m_hat = mn / (1.0 - ADAM_B1 ** ADAM_STEP)
        v_hat = vn / (1.0 - ADAM_B2 ** ADAM_STEP)
        d_ref[...] = -ADAM_LR * (m_hat / (jnp.sqrt(v_hat) + ADAM_EPS) + ADAM_WD * w_ref[...])
        mo_ref[...] = mn
        vo_ref[...] = vn

    sp = pl.BlockSpec((tm, c), lambda i: (i, 0))
    return pl.pallas_call(
        body,
        name=name,
        grid=(r // tm,),
        in_specs=[pl.BlockSpec((n_parts, tm, c), lambda i: (0, i, 0))] + [sp] * 3,
        out_specs=[sp] * 4,
        out_shape=[jax.ShapeDtypeStruct((r, c), F32)] * 4,
        compiler_params=_cp(("parallel",), 48),
    )(parts, wv, m, v)


_WEIGHTS = ["ffn1_w_gate", "ffn1_w_up", "ffn1_w_down", "ln1_g", "ln1_b", "w_in", "b_gates", "gmlp_ln_g", "gmlp_ln_b",
            "gmlp_w_s", "gmlp_b_s", "w_attn_branch", "w_gmlp_branch", "w_out", "ln2_g", "ln2_b", "ffn2_w_gate",
            "ffn2_w_up", "ffn2_w_down", "ln3_g", "ln3_b"]
_BIG_OF = {"ffn1_w_gate": ("g1", True), "ffn1_w_up": ("u1", True), "ffn1_w_down": ("d1", False), "w_in": ("win", True),
           "w_attn_branch": ("ab", True), "w_gmlp_branch": ("gb", False), "w_out": ("o", False),
           "ffn2_w_gate": ("g2", True), "ffn2_w_up": ("u2", True), "ffn2_w_down": ("d2", False)}
_SMALL = [n for n in _WEIGHTS if n not in _BIG_OF]
_SMALL_ROWS = {"gmlp_w_s": 128, "b_gates": 2}
_SMALL_SLOT = 8


def _pack_small(d, last=None):
    rows = []
    for n in _SMALL:
        r = d[n].reshape(-1, D)
        slot = max(r.shape[0], _SMALL_SLOT)
        rows.append(jnp.pad(r, ((0, slot - r.shape[0]), (0, 0))))
    rows.append(jnp.zeros((_SMALL_SLOT, D), F32) if last is None else jnp.broadcast_to(last.reshape(1, 1), (_SMALL_SLOT, D)))
    return jnp.concatenate(rows, axis=0)


def _unpack_small(packed, shapes):
    out, at = {}, 0
    for n in _SMALL:
        k = _SMALL_ROWS.get(n, 1)
        out[n] = packed[at:at + k].reshape(shapes[n])
        at += max(k, _SMALL_SLOT)
    return out


def kernel(x, positions, ffn1_w_gate, ffn1_w_up, ffn1_w_down, ln1_g, ln1_b, w_in, b_gates, gmlp_ln_g, gmlp_ln_b, gmlp_w_s, gmlp_b_s, w_attn_branch, w_gmlp_branch, w_out, ln2_g, ln2_b, ffn2_w_gate, ffn2_w_up, ffn2_w_down, ln3_g, ln3_b, loss_target, m_ffn1_w_gate, m_ffn1_w_up, m_ffn1_w_down, m_ln1_g, m_ln1_b, m_w_in, m_b_gates, m_gmlp_ln_g, m_gmlp_ln_b, m_gmlp_w_s, m_gmlp_b_s, m_w_attn_branch, m_w_gmlp_branch, m_w_out, m_ln2_g, m_ln2_b, m_ffn2_w_gate, m_ffn2_w_up, m_ffn2_w_down, m_ln3_g, m_ln3_b, v_ffn1_w_gate, v_ffn1_w_up, v_ffn1_w_down, v_ln1_g, v_ln1_b, v_w_in, v_b_gates, v_gmlp_ln_g, v_gmlp_ln_b, v_gmlp_w_s, v_gmlp_b_s, v_w_attn_branch, v_w_gmlp_branch, v_w_out, v_ln2_g, v_ln2_b, v_ffn2_w_gate, v_ffn2_w_up, v_ffn2_w_down, v_ln3_g, v_ln3_b):
    args = dict(locals())
    wts = {n: args[n] for n in _WEIGHTS}
    ms = {n: args["m_" + n] for n in _WEIGHTS}
    vs = {n: args["v_" + n] for n in _WEIGHTS}

    name_of = {key: (n, tr) for n, (key, tr) in _BIG_OF.items()}

    started, tok = {}, None
    for grp, keys in _GROUPS.items():
        shards = []
        for key in keys:
            n, tr = name_of[key]
            s2 = wts[n][0]
            shards.append((s2.T if tr else s2).astype(BF16))
        started[grp] = _gather_send(shards, grp, tok)
        tok = started[grp][3]
    all_started = tok

    passed = {}

    def get_w(grp, after, early=False):
        if grp not in passed:
            passed[grp] = _gather_pass(started[grp], grp, all_started if after is None else after)
            after = None
        if early:
            return None
        lands = _gather_finish(passed[grp], grp, after)
        return {key: g.reshape(-1, g.shape[-1]) for key, g in zip(_GROUPS[grp], lands)}

    sent = {}

    def emit(grp, grads):
        if grp == "small":
            sent[grp] = _allgather_send(_pack_small(grads, last=grads["loss"]), "small_grads_send")
        else:
            parts = [grads[key].reshape(N_DEV, -1, grads[key].shape[-1]) for key in _SCATTERS[grp]]
            sent[grp] = _scatter2_send(parts[0], grp) if grp in _TWO_STAGE else _scatter_send(parts, grp)
        return sent[grp][3]

    p = {n: (wts[n][0] if n in ("gmlp_w_s", "gmlp_b_s") else wts[n]) for n in _SMALL}
    loss, dx, gp = _local_step(x[0], positions.reshape(S, 1), loss_target[0], get_w, p, emit)
    grads, deltas, new_m, new_v = {}, {}, {}, {}
    after = dx
    for grp in ("ffn2", "mix", "win", "small", "g1", "u1", "d1"):
        if grp == "small":
            parts = _allgather_finish(sent[grp], "small_grads_done", after)
            outs = _sum_adamw(parts, *[_pack_small({n: d[n] for n in _SMALL}) for d in (wts, ms, vs)], "update_small")
            shapes = {n: wts[n].shape for n in _SMALL}
            for dst, packed in zip((grads, deltas, new_m, new_v), outs):
                dst.update(_unpack_small(packed, shapes))
            loss = outs[0][-_SMALL_SLOT, 0]
            after = outs[1]
            continue
        arrived = ([_scatter2_finish(sent[grp], grp, after)] if grp in _TWO_STAGE
                   else _scatter_finish(sent[grp], grp, after))
        for key, part in zip(_SCATTERS[grp], arrived):
            n, tr = name_of[key]
            outs = _sum_adamw(part, *[(d[n][0].T if tr else d[n][0]) for d in (wts, ms, vs)], "update_" + key)
            for dst, o in zip((grads, deltas, new_m, new_v), outs):
                dst[n] = (o.T if tr else o)[None]
            after = outs[1]

    return (loss, dx[None], *[grads[n] for n in _WEIGHTS], *[deltas[n] for n in _WEIGHTS],
            *[new_m[n] for n in _WEIGHTS], *[new_v[n] for n in _WEIGHTS])
```

```python
import functools
import math

import jax
import jax.numpy as jnp
from jax import lax
from jax.experimental import pallas as pl
from jax.experimental.pallas import tpu as pltpu

F32 = jnp.float32
BF16 = jnp.bfloat16

N_DEV = 8
D = 1024
S = 2048
F = 2816
HEAD_DIM = 64
HEADS = 8
GW = HEADS * HEAD_DIM
PATTERN_DILATIONS = (1, 4, 16)
BLK = 128
QKV_W = 3 * 3 * GW
IN_W = QKV_W + 2 * D + 2 * D
DPROJ_W = 5 * 2 * D
DPROJ_Z_COL, DPROJ_G_COL = 3, 4


def _dproj_tile(t):
    three = jnp.int32(3)
    return jnp.where(t < 9, lax.rem(t, three) * 3 + lax.div(t, three), t + 3)
ROPE_THETA = 10000.0
ALPHA = 2.0 ** 0.25
LN_EPS = 1e-5
ADAM_LR, ADAM_B1, ADAM_B2, ADAM_EPS, ADAM_WD, ADAM_STEP = 0.001, 0.9, 0.999, 1e-08, 0.01, 10
NEG = -1e30
MESH = pl.DeviceIdType.MESH


def _cp(sem=None, vmem_mb=None):
    kw = {}
    if sem is not None:
        kw["dimension_semantics"] = sem
    if vmem_mb is not None:
        kw["vmem_limit_bytes"] = vmem_mb << 20
    return pltpu.CompilerParams(**kw)


def _dot_nn(a, b):
    return lax.dot_general(a, b, (((1,), (0,)), ((), ())), preferred_element_type=F32)


def _dot_nt(a, b):
    return lax.dot_general(a, b, (((1,), (1,)), ((), ())), preferred_element_type=F32)


def _dot_tn(a, b):
    return lax.dot_general(a, b, (((0,), (0,)), ((), ())), preferred_element_type=F32)


def _ln_fwd(r, g, b):
    mu = jnp.mean(r, axis=-1, keepdims=True)
    xc = r - mu
    var = jnp.mean(xc * xc, axis=-1, keepdims=True)
    rstd = lax.rsqrt(var + LN_EPS)
    xhat = xc * rstd
    return xhat * g + b, xhat, rstd


def _ln_bwd(dh, xhat, rstd, g):
    dxh = dh * g
    m1 = jnp.mean(dxh, axis=-1, keepdims=True)
    m2 = jnp.mean(dxh * xhat, axis=-1, keepdims=True)
    return rstd * (dxh - m1 - xhat * m2)


def _sigmoid(x):
    return 0.5 * jnp.tanh(0.5 * x) + 0.5


def _colsum(x):
    return jnp.sum(x, axis=0, keepdims=True)


def _matmul(a, b, mode, out_dtype, tm, tn, tk, name, b_off=0, n_out=None, dep=None, a_map=None, m_out=None):
    n_dep = 0 if dep is None else 1
    a_map = a_map or (lambda t: t)
    if mode == "nn":
        m, k = a.shape[0], b.shape[0]
        n = b.shape[1]
    elif mode == "nt":
        m, k = a.shape
        n = n_out if n_out is not None else b.shape[0]
    else:
        k, m = a.shape[0], m_out or a.shape[1]
        n = b.shape[1]
    nk = k // tk
    assert m % tm == 0 and n % tn == 0 and k % tk == 0
    dot = {"nn": _dot_nn, "nt": _dot_nt, "tn": _dot_tn}[mode]

    def body(a_ref, b_ref, *rest):
        o_ref, scr = rest[n_dep], rest[n_dep + 1:]
        r = dot(a_ref[...].astype(BF16), b_ref[...].astype(BF16))
        if nk == 1:
            o_ref[...] = r.astype(out_dtype)
        else:
            acc = scr[0]
            kk = pl.program_id(2)

            @pl.when(kk == 0)
            def _():
                acc[...] = r

            @pl.when(kk > 0)
            def _():
                acc[...] += r

            @pl.when(kk == nk - 1)
            def _():
                o_ref[...] = acc[...].astype(out_dtype)

    if mode == "nn":
        a_spec = pl.BlockSpec((tm, tk), lambda i, j, kk: (i, a_map(kk)))
        b_spec = pl.BlockSpec((tk, tn), lambda i, j, kk: (kk, j))
    elif mode == "nt":
        a_spec = pl.BlockSpec((tm, tk), lambda i, j, kk: (i, kk))
        b_spec = pl.BlockSpec((tn, tk), lambda i, j, kk: (j + b_off, kk))
    else:
        a_spec = pl.BlockSpec((tk, tm), lambda i, j, kk: (kk, a_map(i)))
        b_spec = pl.BlockSpec((tk, tn), lambda i, j, kk: (kk, j))
    return pl.pallas_call(
        body,
        name=name,
        grid=(m // tm, n // tn, nk),
        in_specs=[a_spec, b_spec] + [pl.BlockSpec(memory_space=pl.ANY)] * n_dep,
        out_specs=pl.BlockSpec((tm, tn), lambda i, j, kk: (i, j)),
        out_shape=jax.ShapeDtypeStruct((m, n), out_dtype),
        scratch_shapes=[] if nk == 1 else [pltpu.VMEM((tm, tn), F32)],
        compiler_params=_cp(("parallel", "parallel", "arbitrary"), 56),
    )(a, b, *([] if dep is None else [dep]))


def _ffn_up(x, wgt, wut, name, tm=512, tn=1408, dep=None):
    n_dep = 0 if dep is None else 1

    def body(x_ref, wg_ref, wu_ref, *rest):
        a_ref, b_ref, hm_ref = rest[n_dep:]
        xb = x_ref[...].astype(BF16)
        a = _dot_nt(xb, wg_ref[...])
        b = _dot_nt(xb, wu_ref[...])
        a_ref[...] = a.astype(BF16)
        b_ref[...] = b.astype(BF16)
        hm_ref[...] = ((a * _sigmoid(a)) * b).astype(BF16)

    wsp = pl.BlockSpec((tn, D), lambda i, j: (j, 0))
    mid = pl.BlockSpec((tm, tn), lambda i, j: (i, j))
    return pl.pallas_call(
        body,
        name=name,
        grid=(S // tm, F // tn),
        in_specs=[pl.BlockSpec((tm, D), lambda i, j: (i, 0)), wsp, wsp] + [pl.BlockSpec(memory_space=pl.ANY)] * n_dep,
        out_specs=[mid, mid, mid],
        out_shape=[jax.ShapeDtypeStruct((S, F), BF16)] * 3,
        compiler_params=_cp(("parallel", "arbitrary"), 56),
    )(x, wgt, wut, *([] if dep is None else [dep]))


def _ffn_down(hm, wd, x, ln_g, ln_b, name, tm=256):
    def body(hm_ref, wd_ref, x_ref, g_ref, b_ref, h_ref, hb_ref, xh_ref, rs_ref):
        r = ALPHA * x_ref[...] + 0.5 * _dot_nn(hm_ref[...], wd_ref[...])
        h, xhat, rstd = _ln_fwd(r, g_ref[...], b_ref[...])
        h_ref[...] = h
        hb_ref[...] = h.astype(BF16)
        xh_ref[...] = xhat
        rs_ref[...] = rstd

    row = pl.BlockSpec((tm, D), lambda i: (i, 0))
    vec = pl.BlockSpec((1, D), lambda i: (0, 0))
    return pl.pallas_call(
        body,
        name=name,
        grid=(S // tm,),
        in_specs=[pl.BlockSpec((tm, F), lambda i: (i, 0)), pl.BlockSpec((F, D), lambda i: (0, 0)), row, vec, vec],
        out_specs=[row, row, row, pl.BlockSpec((tm, 1), lambda i: (i, 0))],
        out_shape=[
            jax.ShapeDtypeStruct((S, D), F32),
            jax.ShapeDtypeStruct((S, D), BF16),
            jax.ShapeDtypeStruct((S, D), F32),
            jax.ShapeDtypeStruct((S, 1), F32),
        ],
        compiler_params=_cp(("parallel",), 56),
    )(hm, wd, x, ln_g, ln_b)


def _ffn_bwd_mid(dh_a, dh_b, xhat, rstd, ln_g, a, b, wd, name, tm=512, tn=1408):
    two = dh_b is not None

    def body(*refs):
        dha_ref = refs[0]
        dhb_ref = refs[1] if two else None
        (xh_ref, rs_ref, g_ref, a_ref, b_ref, wd_ref,
         dr_ref, df_ref, da_ref, db_ref, dg_ref, dbias_ref, df_scr) = refs[2 if two else 1:]
        i = pl.program_id(0)
        j = pl.program_id(1)

        @pl.when(j == 0)
        def _():
            dh = dha_ref[...]
            if two:
                dh = ALPHA * dh + dhb_ref[...]
            xhat = xh_ref[...]
            dr = _ln_bwd(dh, xhat, rs_ref[...], g_ref[...])
            dfb = (0.5 * dr).astype(BF16)
            dr_ref[...] = dr
            df_scr[...] = dfb
            df_ref[...] = dfb
            sg = _colsum(dh * xhat)
            sb = _colsum(dh)

            @pl.when(i == 0)
            def _():
                dg_ref[...] = sg
                dbias_ref[...] = sb

            @pl.when(i > 0)
            def _():
                dg_ref[...] += sg
                dbias_ref[...] += sb

        dhm = _dot_nt(df_scr[...], wd_ref[...])
        av = a_ref[...].astype(F32)
        bv = b_ref[...].astype(F32)
        sig = _sigmoid(av)
        da_ref[...] = (dhm * bv * (sig * (1.0 + av * (1.0 - sig)))).astype(BF16)
        db_ref[...] = (dhm * (av * sig)).astype(BF16)

    row = pl.BlockSpec((tm, D), lambda i, j: (i, 0))
    vec = pl.BlockSpec((1, D), lambda i, j: (0, 0))
    mid = pl.BlockSpec((tm, tn), lambda i, j: (i, j))
    ins = [dh_a] + ([dh_b] if two else []) + [xhat, rstd, ln_g, a, b, wd]
    in_specs = [row] * (2 if two else 1) + [row, pl.BlockSpec((tm, 1), lambda i, j: (i, 0)), vec, mid, mid,
                                            pl.BlockSpec((tn, D), lambda i, j: (j, 0))]
    return pl.pallas_call(
        body,
        name=name,
        grid=(S // tm, F // tn),
        in_specs=in_specs,
        out_specs=[row, row, mid, mid, vec, vec],
        out_shape=[
            jax.ShapeDtypeStruct((S, D), F32),
            jax.ShapeDtypeStruct((S, D), BF16),
            jax.ShapeDtypeStruct((S, F), BF16),
            jax.ShapeDtypeStruct((S, F), BF16),
            jax.ShapeDtypeStruct((1, D), F32),
            jax.ShapeDtypeStruct((1, D), F32),
        ],
        scratch_shapes=[pltpu.VMEM((tm, D), BF16)],
        compiler_params=_cp(("arbitrary", "arbitrary"), 56),
    )(*ins)


def _ffn_bwd_dx(dr, da, db, wgt, wut, name, tm=512, tk=1408, dep=None):
    nk = F // tk
    n_dep = 0 if dep is None else 1

    def body(dr_ref, da_ref, db_ref, wg_ref, wu_ref, *rest):
        dx_ref, acc = rest[n_dep], rest[n_dep + 1]
        kk = pl.program_id(1)
        part = _dot_nn(da_ref[...], wg_ref[...]) + _dot_nn(db_ref[...], wu_ref[...])

        @pl.when(kk == 0)
        def _():
            acc[...] = ALPHA * dr_ref[...] + part

        @pl.when(kk > 0)
        def _():
            acc[...] += part

        @pl.when(kk == nk - 1)
        def _():
            dx_ref[...] = acc[...]

    row = pl.BlockSpec((tm, D), lambda i, kk: (i, 0))
    mid = pl.BlockSpec((tm, tk), lambda i, kk: (i, kk))
    wsp = pl.BlockSpec((tk, D), lambda i, kk: (kk, 0))
    return pl.pallas_call(
        body,
        name=name,
        grid=(S // tm, nk),
        in_specs=[row, mid, mid, wsp, wsp] + [pl.BlockSpec(memory_space=pl.ANY)] * n_dep,
        out_specs=row,
        out_shape=jax.ShapeDtypeStruct((S, D), F32),
        scratch_shapes=[pltpu.VMEM((tm, D), F32)],
        compiler_params=_cp(("parallel", "arbitrary"), 56),
    )(dr, da, db, wgt, wut, *([] if dep is None else [dep]))


def _rope_tables(pos_col, invf, sign, tm=512):
    def body(p_ref, f_ref, s_ref, c_out, s_out):
        ang = p_ref[...].astype(F32) * f_ref[...]
        c_out[...] = jnp.cos(ang)
        s_out[...] = jnp.sin(ang) * s_ref[...]

    vec = pl.BlockSpec((1, BLK), lambda i: (0, 0))
    out = pl.BlockSpec((tm, BLK), lambda i: (i, 0))
    return pl.pallas_call(
        body,
        name="rope_tables",
        grid=(S // tm,),
        in_specs=[pl.BlockSpec((tm, 1), lambda i: (i, 0)), vec, vec],
        out_specs=[out, out],
        out_shape=[jax.ShapeDtypeStruct((S, BLK), F32)] * 2,
        compiler_params=_cp(("parallel",)),
    )(pos_col, invf, sign)


def _lane_lo(rows=BLK):
    return (lax.broadcasted_iota(jnp.int32, (rows, GW), 1) % HEAD_DIM) < (HEAD_DIM // 2)


def _swap_halves(t, lo):
    return jnp.where(lo, pltpu.roll(t, GW - HEAD_DIM // 2, 1), pltpu.roll(t, HEAD_DIM // 2, 1))


def _rope(t, cosf, sinf, lo):
    return t * cosf + _swap_halves(t, lo) * sinf


def _unrope(g, cosf, sinf, lo):
    return g * cosf + _swap_halves(g * sinf, lo)


def _tile4(v):
    return jnp.concatenate([v, v, v, v], axis=1)


def _band_mask(n):
    qi = lax.broadcasted_iota(jnp.int32, (BLK, 2 * BLK), 0)
    kj = lax.broadcasted_iota(jnp.int32, (BLK, 2 * BLK), 1)
    dist = qi + BLK - kj
    return (dist >= 0) & (dist <= BLK) & ((kj >= BLK) | (n >= 1))


ROWS = 256
LANES = 128


def _to_view(scr, y, dst_ref, d, dtype, col0=0):
    if d == 1:
        dst_ref[:, col0:col0 + GW] = y.astype(dtype)
        return
    for cb in range(GW // LANES):
        scr[cb][...] = y[:, cb * LANES:(cb + 1) * LANES]
    for r in range(d):
        for cb in range(GW // LANES):
            at = col0 + r * GW + cb * LANES
            dst_ref[:, at:at + LANES] = scr[cb][pl.ds(r, ROWS // d, stride=d), :].astype(dtype)


def _from_view(scr, src_ref, d):
    if d == 1:
        return src_ref[...].astype(F32)
    for r in range(d):
        for cb in range(GW // LANES):
            at = r * GW + cb * LANES
            scr[cb][pl.ds(r, ROWS // d, stride=d), :] = src_ref[:, at:at + LANES].astype(F32)
    return jnp.concatenate([scr[cb][...] for cb in range(GW // LANES)], axis=1)


def _view_spec(d):
    return pl.BlockSpec((ROWS // d, d * GW), lambda i: (i, 0))


_VIEW_SCRATCH = [pltpu.VMEM((ROWS, LANES), F32)] * (GW // LANES)


def _qkv_prep(qkv, cos_t, sin_t, gi):
    d = PATTERN_DILATIONS[gi]

    def body(q_ref, k_ref, v_ref, c_ref, s_ref, qo, ko, vo, *scr):
        lo = _lane_lo(ROWS)
        cf, sf = _tile4(c_ref[...]), _tile4(s_ref[...])
        _to_view(scr, _rope(q_ref[...], cf, sf, lo) * (HEAD_DIM ** -0.5), qo, d, BF16)
        _to_view(scr, _rope(k_ref[...], cf, sf, lo), ko, d, BF16)
        _to_view(scr, v_ref[...], vo, d, BF16)

    col = lambda c: pl.BlockSpec((ROWS, GW), lambda i: (i, c))
    tab = pl.BlockSpec((ROWS, BLK), lambda i: (i, 0))
    return pl.pallas_call(
        body,
        name=f"qkv_prep_g{gi}",
        grid=(S // ROWS,),
        in_specs=[col(gi), col(3 + gi), col(6 + gi), tab, tab],
        out_specs=[_view_spec(d)] * 3,
        out_shape=[jax.ShapeDtypeStruct((S // d, d * GW), BF16)] * 3,
        scratch_shapes=_VIEW_SCRATCH,
        compiler_params=_cp(("parallel",)),
    )(qkv, qkv, qkv, cos_t, sin_t)


def _attn_specs(d):
    cur = pl.BlockSpec((BLK, GW), lambda r, n: (n, r))
    prev = pl.BlockSpec((BLK, GW), lambda r, n: (jnp.maximum(n - 1, 0), r))
    return cur, prev


def _left_lanes():
    return lax.broadcasted_iota(jnp.int32, (BLK, LANES), 1) < HEAD_DIM


def _attn_fwd(qr, kr, vv, gi):
    d = PATTERN_DILATIONS[gi]
    sub = S // d
    cur, prev = _attn_specs(d)

    def body(q_ref, kc_ref, kp_ref, vc_ref, vp_ref, o_ref, l_ref):
        n = pl.program_id(1)
        mask = _band_mask(n)
        left = _left_lanes()
        zero = jnp.zeros((BLK, LANES), BF16)
        for pr in range(GW // LANES):
            cs = slice(pr * LANES, (pr + 1) * LANES)
            q2 = q_ref[:, cs]
            k2 = jnp.concatenate([kp_ref[:, cs], kc_ref[:, cs]], axis=0)
            v2 = jnp.concatenate([vp_ref[:, cs], vc_ref[:, cs]], axis=0)
            o_h, lse_h = [], []
            for side in (left, ~left):
                s = jnp.where(mask, _dot_nt(jnp.where(side, q2, zero), k2), NEG)
                m = jnp.max(s, axis=1, keepdims=True)
                p = jnp.exp(s - m)
                l = jnp.sum(p, axis=1, keepdims=True)
                o_h.append(_dot_nn((p / l).astype(BF16), v2))
                lse_h.append(m + jnp.log(l))
            o_ref[:, cs] = jnp.where(left, o_h[0], o_h[1])
            l_ref[:, cs] = jnp.where(left, lse_h[0], lse_h[1])

    return pl.pallas_call(
        body,
        name=f"attn_fwd_g{gi}",
        grid=(d, sub // BLK),
        in_specs=[cur, cur, prev, cur, prev],
        out_specs=[cur, cur],
        out_shape=[jax.ShapeDtypeStruct((sub, d * GW), F32)] * 2,
        compiler_params=_cp(("parallel", "parallel")),
    )(qr, kr, kr, vv, vv)


def _attn_bwd(qr, kr, vv, do, lse, cterm, gi):
    d = PATTERN_DILATIONS[gi]
    sub = S // d
    cur, prev = _attn_specs(d)

    def body(q_ref, kc_ref, kp_ref, vc_ref, vp_ref, do_ref, l_ref, c_ref, dq_ref, dk_ref, dv_ref):
        n = pl.program_id(1)
        mask = _band_mask(n)
        left = _left_lanes()
        zero = jnp.zeros((BLK, LANES), BF16)
        here = pl.ds(pl.multiple_of(n * BLK, BLK), BLK)
        before = pl.ds(pl.multiple_of(jnp.maximum(n - 1, 0) * BLK, BLK), BLK)
        for pr in range(GW // LANES):
            cs = slice(pr * LANES, (pr + 1) * LANES)
            q2 = q_ref[:, cs]
            k2 = jnp.concatenate([kp_ref[:, cs], kc_ref[:, cs]], axis=0)
            v2 = jnp.concatenate([vp_ref[:, cs], vc_ref[:, cs]], axis=0)
            do2 = do_ref[:, cs]
            dq_h, dk2, dv2 = [], None, None
            for h, side in enumerate((left, ~left)):
                at = pr * LANES + h * HEAD_DIM
                qm = jnp.where(side, q2, zero)
                dom = jnp.where(side, do2, zero)
                s = jnp.where(mask, _dot_nt(qm, k2), NEG)
                p = jnp.exp(s - l_ref[:, at:at + 1])
                ds = (p * (_dot_nt(dom, v2) - c_ref[:, at:at + 1])).astype(BF16)
                dq_h.append(_dot_nn(ds, k2))
                dk_part = _dot_tn(ds, qm)
                dv_part = _dot_tn(p.astype(BF16), dom)
                dk2 = dk_part if dk2 is None else dk2 + dk_part
                dv2 = dv_part if dv2 is None else dv2 + dv_part
            dq_ref[:, cs] = jnp.where(left, dq_h[0], dq_h[1]) * (HEAD_DIM ** -0.5)
            dk_ref[here, cs] = dk2[BLK:]
            dv_ref[here, cs] = dv2[BLK:]
            dk_ref[before, cs] += dk2[:BLK]
            dv_ref[before, cs] += dv2[:BLK]

    whole = pl.BlockSpec((sub, GW), lambda r, n: (0, r))
    return pl.pallas_call(
        body,
        name=f"attn_bwd_g{gi}",
        grid=(d, sub // BLK),
        in_specs=[cur, cur, prev, cur, prev, cur, cur, cur],
        out_specs=[cur, whole, whole],
        out_shape=[jax.ShapeDtypeStruct((sub, d * GW), F32)] * 3,
        compiler_params=_cp(("arbitrary", "arbitrary"), 48),
    )(qr, kr, kr, vv, vv, do, lse, cterm)


def _qkv_unprep(dq, dk, dv, cos_t, sin_t, gi, dproj):
    d = PATTERN_DILATIONS[gi]

    def body(dq_ref, dk_ref, dv_ref, c_ref, s_ref, _, out_ref, *scr):
        lo = _lane_lo(ROWS)
        cf, sf = _tile4(c_ref[...]), _tile4(s_ref[...])
        out_ref[:, 0:GW] = _unrope(_from_view(scr, dq_ref, d), cf, sf, lo).astype(BF16)
        out_ref[:, GW:2 * GW] = _unrope(_from_view(scr, dk_ref, d), cf, sf, lo).astype(BF16)
        out_ref[:, 2 * GW:3 * GW] = _from_view(scr, dv_ref, d).astype(BF16)

    tab = pl.BlockSpec((ROWS, BLK), lambda i: (i, 0))
    return pl.pallas_call(
        body,
        name=f"qkv_unprep_g{gi}",
        grid=(S // ROWS,),
        in_specs=[_view_spec(d)] * 3 + [tab, tab, pl.BlockSpec(memory_space=pl.ANY)],
        out_specs=pl.BlockSpec((ROWS, 3 * GW), lambda i: (i, gi)),
        out_shape=jax.ShapeDtypeStruct(dproj.shape, dproj.dtype),
        input_output_aliases={5: 0},
        scratch_shapes=_VIEW_SCRATCH,
        compiler_params=_cp(("parallel",)),
    )(dq, dk, dv, cos_t, sin_t, dproj)


def _group_weights(l0, l1, l2):
    mx = jnp.maximum(jnp.maximum(l0, l1), l2)
    e0, e1, e2 = jnp.exp(l0 - mx), jnp.exp(l1 - mx), jnp.exp(l2 - mx)
    inv = 1.0 / (e0 + e1 + e2)
    return e0 * inv, e1 * inv, e2 * inv


def _combine_fwd(os_, lses):
    def body(o0, o1, o2, l0, l1, l2, y_ref, *scr):
        ov = [_from_view(scr, o, d) for o, d in zip((o0, o1, o2), PATTERN_DILATIONS)]
        lv = [_from_view(scr, l, d) for l, d in zip((l0, l1, l2), PATTERN_DILATIONS)]
        w0, w1, w2 = _group_weights(*lv)
        y_ref[...] = (w0 * ov[0] + w1 * ov[1] + w2 * ov[2]).astype(BF16)

    views = [_view_spec(d) for d in PATTERN_DILATIONS]
    return pl.pallas_call(
        body,
        name="attn_combine_fwd",
        grid=(S // ROWS,),
        in_specs=views + views,
        out_specs=pl.BlockSpec((ROWS, GW), lambda i: (i, 0)),
        out_shape=jax.ShapeDtypeStruct((S, GW), BF16),
        scratch_shapes=_VIEW_SCRATCH,
        compiler_params=_cp(("parallel",)),
    )(*os_, *lses)


def _combine_bwd(dy, os_, lses, seg):
    def body(dy_ref, o0, o1, o2, l0, l1, l2, seg_ref, d0, d1, d2, c0, c1, c2, *scr):
        ov = [_from_view(scr, o, d) for o, d in zip((o0, o1, o2), PATTERN_DILATIONS)]
        lv = [_from_view(scr, l, d) for l, d in zip((l0, l1, l2), PATTERN_DILATIONS)]
        ws = _group_weights(*lv)
        dyv = dy_ref[...]
        t = dyv * (ws[0] * ov[0] + ws[1] * ov[1] + ws[2] * ov[2])
        t_hi = t.astype(BF16)
        r1 = t - t_hi.astype(F32)
        t_mid = r1.astype(BF16)
        t_lo = (r1 - t_mid.astype(F32)).astype(BF16)
        sg = seg_ref[...]
        e = _dot_nn(t_hi, sg) + _dot_nn(t_mid, sg) + _dot_nn(t_lo, sg)
        for w, do_ref, c_ref, d in zip(ws, (d0, d1, d2), (c0, c1, c2), PATTERN_DILATIONS):
            _to_view(scr, w * dyv, do_ref, d, BF16)
            _to_view(scr, w * e, c_ref, d, F32)

    views = [_view_spec(d) for d in PATTERN_DILATIONS]
    return pl.pallas_call(
        body,
        name="attn_combine_bwd",
        grid=(S // ROWS,),
        in_specs=[pl.BlockSpec((ROWS, GW), lambda i: (i, 0))] + views + views + [pl.BlockSpec((GW, GW), lambda i: (0, 0))],
        out_specs=views + views,
        out_shape=[jax.ShapeDtypeStruct((S // d, d * GW), BF16) for d in PATTERN_DILATIONS]
        + [jax.ShapeDtypeStruct((S // d, d * GW), F32) for d in PATTERN_DILATIONS],
        scratch_shapes=_VIEW_SCRATCH,
        compiler_params=_cp(("parallel",)),
    )(dy, *os_, *lses, seg)


_SQRT_HALF = 0.7071067811865476
_INV_SQRT_2PI = 0.3989422804014327


def _gelu(z):
    return 0.5 * z * (1.0 + lax.erf(z * _SQRT_HALF))


def _gelu_grad(z):
    return 0.5 * (1.0 + lax.erf(z * _SQRT_HALF)) + z * (_INV_SQRT_2PI * jnp.exp(-0.5 * z * z))


def _tril_ws(ws_ref, g):
    t = lax.broadcasted_iota(jnp.int32, (BLK, BLK), 0)
    s = lax.broadcasted_iota(jnp.int32, (BLK, BLK), 1)
    return jnp.where(t >= s, ws_ref[g], 0.0)


def _gmlp_fwd(z, ws, bst, ln_g, ln_b, tm=256):
    nch = tm // BLK

    def body(z_ref, ws_ref, b_ref, g_ref, be_ref, y_ref):
        zg = _gelu(z_ref[...])
        u = zg[:, :D]
        vn, _, _ = _ln_fwd(zg[:, D:], g_ref[...], be_ref[...])
        vnb = vn.astype(BF16)
        bt = b_ref[...]
        for g in range(8):
            w = _tril_ws(ws_ref, g).astype(BF16)
            cols = slice(g * BLK, (g + 1) * BLK)
            for c in range(nch):
                rows = slice(c * BLK, (c + 1) * BLK)
                mixed = _dot_nn(w, vnb[rows, cols]) + bt[:, g:g + 1]
                y_ref[rows, cols] = (u[rows, cols] * mixed).astype(BF16)

    return pl.pallas_call(
        body,
        name="gmlp_fwd",
        grid=(S // tm,),
        in_specs=[
            pl.BlockSpec((tm, 2 * D), lambda i: (i, 0)),
            pl.BlockSpec((8, BLK, BLK), lambda i: (0, 0, 0)),
            pl.BlockSpec((BLK, 8), lambda i: (0, 0)),
            pl.BlockSpec((1, D), lambda i: (0, 0)),
            pl.BlockSpec((1, D), lambda i: (0, 0)),
        ],
        out_specs=pl.BlockSpec((tm, D), lambda i: (i, 0)),
        out_shape=jax.ShapeDtypeStruct((S, D), BF16),
        compiler_params=_cp(("parallel",), 48),
    )(z, ws, bst, ln_g, ln_b)


def _gmlp_bwd(z, dy, ws, bst, ln_g, ln_b, dproj, tm=256):
    nch = tm // BLK

    def body(z_ref, dy_ref, ws_ref, b_ref, g_ref, be_ref, _, dz_ref, dws_ref, dbs_ref, dg_ref, dbe_ref, dvn_scr, dm_acc):
        i = pl.program_id(0)
        zv = z_ref[...]
        zg = _gelu(zv)
        u = zg[:, :D]
        gam = g_ref[...]
        vn, xhat, rstd = _ln_fwd(zg[:, D:], gam, be_ref[...])
        vnb = vn.astype(BF16)
        dyv = dy_ref[...]
        dmix = dyv * u
        dmb = dmix.astype(BF16)
        bt = b_ref[...]
        tmask = lax.broadcasted_iota(jnp.int32, (BLK, BLK), 0) >= lax.broadcasted_iota(jnp.int32, (BLK, BLK), 1)
        dm_sum = dmix[0:BLK]
        for c in range(1, nch):
            dm_sum = dm_sum + dmix[c * BLK:(c + 1) * BLK]

        @pl.when(i == 0)
        def _():
            dm_acc[...] = dm_sum

        @pl.when(i > 0)
        def _():
            dm_acc[...] += dm_sum

        dus = []
        for g in range(8):
            w = _tril_ws(ws_ref, g).astype(BF16)
            cols = slice(g * BLK, (g + 1) * BLK)
            dw = None
            du_rows = []
            for c in range(nch):
                rows = slice(c * BLK, (c + 1) * BLK)
                mixed = _dot_nn(w, vnb[rows, cols]) + bt[:, g:g + 1]
                du_rows.append(dyv[rows, cols] * mixed)
                part = _dot_nt(dmb[rows, cols], vnb[rows, cols])
                dw = part if dw is None else dw + part
                dvn_scr[rows, cols] = _dot_tn(w, dmb[rows, cols])
            dus.append(jnp.concatenate(du_rows, axis=0))
            dw = jnp.where(tmask, dw, 0.0)

            @pl.when(i == 0)
            def _():
                dws_ref[g] = dw

            @pl.when(i > 0)
            def _():
                dws_ref[g] += dw

        dvn = dvn_scr[...]
        sg = _colsum(dvn * xhat)
        sb = _colsum(dvn)

        @pl.when(i == 0)
        def _():
            dg_ref[...] = sg
            dbe_ref[...] = sb

        @pl.when(i > 0)
        def _():
            dg_ref[...] += sg
            dbe_ref[...] += sb

        dvg = _ln_bwd(dvn, xhat, rstd, gam)
        gp = _gelu_grad(zv)
        dz_ref[:, :D] = (jnp.concatenate(dus, axis=1) * gp[:, :D]).astype(BF16)
        dz_ref[:, D:] = (dvg * gp[:, D:]).astype(BF16)

        @pl.when(i == S // tm - 1)
        def _():
            acc = dm_acc[...]
            for g in range(8):
                dbs_ref[:, g:g + 1] = jnp.sum(acc[:, g * BLK:(g + 1) * BLK], axis=1, keepdims=True)

    vec = pl.BlockSpec((1, D), lambda i: (0, 0))
    return pl.pallas_call(
        body,
        name="gmlp_bwd",
        grid=(S // tm,),
        in_specs=[
            pl.BlockSpec((tm, 2 * D), lambda i: (i, 0)),
            pl.BlockSpec((tm, D), lambda i: (i, 0)),
            pl.BlockSpec((8, BLK, BLK), lambda i: (0, 0, 0)),
            pl.BlockSpec((BLK, 8), lambda i: (0, 0)),
            vec,
            vec,
            pl.BlockSpec(memory_space=pl.ANY),
        ],
        out_specs=[
            pl.BlockSpec((tm, 2 * D), lambda i: (i, DPROJ_Z_COL)),
            pl.BlockSpec((8, BLK, BLK), lambda i: (0, 0, 0)),
            pl.BlockSpec((BLK, 8), lambda i: (0, 0)),
            vec,
            vec,
        ],
        out_shape=[
            jax.ShapeDtypeStruct(dproj.shape, dproj.dtype),
            jax.ShapeDtypeStruct((8, BLK, BLK), F32),
            jax.ShapeDtypeStruct((BLK, 8), F32),
            jax.ShapeDtypeStruct((1, D), F32),
            jax.ShapeDtypeStruct((1, D), F32),
        ],
        input_output_aliases={6: 0},
        scratch_shapes=[pltpu.VMEM((tm, D), F32), pltpu.VMEM((BLK, D), F32)],
        compiler_params=_cp(("arbitrary",), 48),
    )(z, dy, ws, bst, ln_g, ln_b, dproj)


def _merge_fwd(ya, yg, glog, bgate, h1, wabt, wgb, wo, ln_g, ln_b, tm=256):
    def body(ya_ref, yg_ref, gl_ref, bg_ref, h1_ref, wab_ref, wgb_ref, wo_ref, g_ref, b_ref,
             h_ref, hb_ref, xh_ref, rs_ref, mg_ref, bra_ref, brg_ref):
        bra = _dot_nt(ya_ref[...], wab_ref[...])
        brg = _dot_nn(yg_ref[...], wgb_ref[...])
        gates = _sigmoid(gl_ref[...] + bg_ref[...])
        merged = (gates[:, :D] * bra + gates[:, D:] * brg).astype(BF16)
        mix = _dot_nn(merged, wo_ref[...])
        h, xhat, rstd = _ln_fwd(ALPHA * h1_ref[...] + mix, g_ref[...], b_ref[...])
        h_ref[...] = h
        hb_ref[...] = h.astype(BF16)
        xh_ref[...] = xhat
        rs_ref[...] = rstd
        mg_ref[...] = merged
        bra_ref[...] = bra
        brg_ref[...] = brg

    row = pl.BlockSpec((tm, D), lambda i: (i, 0))
    vec = pl.BlockSpec((1, D), lambda i: (0, 0))
    full = lambda shape: pl.BlockSpec(shape, lambda i: (0, 0))
    return pl.pallas_call(
        body,
        name="merge_fwd",
        grid=(S // tm,),
        in_specs=[
            pl.BlockSpec((tm, GW), lambda i: (i, 0)), row,
            pl.BlockSpec((tm, 2 * D), lambda i: (i, glog.shape[1] // (2 * D) - 1)),
            full((1, 2 * D)), row,
            full((D, GW)), full((D, D)), full((D, D)), vec, vec,
        ],
        out_specs=[row, row, row, pl.BlockSpec((tm, 1), lambda i: (i, 0)), row, row, row],
        out_shape=[
            jax.ShapeDtypeStruct((S, D), F32),
            jax.ShapeDtypeStruct((S, D), BF16),
            jax.ShapeDtypeStruct((S, D), F32),
            jax.ShapeDtypeStruct((S, 1), F32),
            jax.ShapeDtypeStruct((S, D), BF16),
            jax.ShapeDtypeStruct((S, D), F32),
            jax.ShapeDtypeStruct((S, D), F32),
        ],
        compiler_params=_cp(("parallel",), 48),
    )(ya, yg, glog, bgate, h1, wabt, wgb, wo, ln_g, ln_b)


def _merge_bwd(dh2, xhat, rstd, ln_g, bra, brg, glog, bgate, wabt, wgb, wo, tm=256):
    def body(dh_ref, xh_ref, rs_ref, g_ref, bra_ref, brg_ref, gl_ref, bg_ref, wab_ref, wgb_ref, wo_ref,
             dr_ref, drb_ref, dlog_ref, dba_ref, dbg_ref, dya_ref, dyg_ref, dbgate_ref, dg_ref, dbias_ref):
        i = pl.program_id(0)
        dh = dh_ref[...]
        xh = xh_ref[...]
        dr = _ln_bwd(dh, xh, rs_ref[...], g_ref[...])
        drb = dr.astype(BF16)
        dr_ref[...] = dr
        drb_ref[...] = drb
        dmerged = _dot_nt(drb, wo_ref[...])
        gates = _sigmoid(gl_ref[...] + bg_ref[...])
        g0, g1 = gates[:, :D], gates[:, D:]
        dl0 = dmerged * bra_ref[...] * g0 * (1.0 - g0)
        dl1 = dmerged * brg_ref[...] * g1 * (1.0 - g1)
        dlog_ref[:, :D] = dl0.astype(BF16)
        dlog_ref[:, D:] = dl1.astype(BF16)
        dba = (dmerged * g0).astype(BF16)
        dbg = (dmerged * g1).astype(BF16)
        dba_ref[...] = dba
        dbg_ref[...] = dbg
        dya_ref[...] = _dot_nn(dba, wab_ref[...])
        dyg_ref[...] = _dot_nt(dbg, wgb_ref[...])
        s0, s1 = _colsum(dl0), _colsum(dl1)
        sg, sb = _colsum(dh * xh), _colsum(dh)

        @pl.when(i == 0)
        def _():
            dbgate_ref[:, :D] = s0
            dbgate_ref[:, D:] = s1
            dg_ref[...] = sg
            dbias_ref[...] = sb

        @pl.when(i > 0)
        def _():
            dbgate_ref[:, :D] += s0
            dbgate_ref[:, D:] += s1
            dg_ref[...] += sg
            dbias_ref[...] += sb

    row = pl.BlockSpec((tm, D), lambda i: (i, 0))
    vec = pl.BlockSpec((1, D), lambda i: (0, 0))
    wide = pl.BlockSpec((tm, 2 * D), lambda i: (i, 0))
    full = lambda shape: pl.BlockSpec(shape, lambda i: (0, 0))
    return pl.pallas_call(
        body,
        name="merge_bwd",
        grid=(S // tm,),
        in_specs=[row, row, pl.BlockSpec((tm, 1), lambda i: (i, 0)), vec, row, row,
                  pl.BlockSpec((tm, 2 * D), lambda i: (i, glog.shape[1] // (2 * D) - 1)),
                  full((1, 2 * D)), full((D, GW)), full((D, D)), full((D, D))],
        out_specs=[row, row, pl.BlockSpec((tm, 2 * D), lambda i: (i, DPROJ_G_COL)), row, row,
                   pl.BlockSpec((tm, GW), lambda i: (i, 0)), row, full((1, 2 * D)), vec, vec],
        out_shape=[
            jax.ShapeDtypeStruct((S, D), F32),
            jax.ShapeDtypeStruct((S, D), BF16),
            jax.ShapeDtypeStruct((S, DPROJ_W), BF16),
            jax.ShapeDtypeStruct((S, D), BF16),
            jax.ShapeDtypeStruct((S, D), BF16),
            jax.ShapeDtypeStruct((S, GW), F32),
            jax.ShapeDtypeStruct((S, D), F32),
            jax.ShapeDtypeStruct((1, 2 * D), F32),
            jax.ShapeDtypeStruct((1, D), F32),
            jax.ShapeDtypeStruct((1, D), F32),
        ],
        compiler_params=_cp(("arbitrary",), 48),
    )(dh2, xhat, rstd, ln_g, bra, brg, glog, bgate, wabt, wgb, wo)


def _loss_head(h3, target, tm=512):
    def body(h_ref, t_ref, d_ref, l_ref):
        i = pl.program_id(0)
        e = h_ref[...] - t_ref[...]
        d_ref[...] = e * (1.0 / D)
        part = jnp.sum(_colsum(e * e), axis=1, keepdims=True) * (0.5 / D)

        @pl.when(i == 0)
        def _():
            l_ref[...] = part

        @pl.when(i > 0)
        def _():
            l_ref[...] += part

    row = pl.BlockSpec((tm, D), lambda i: (i, 0))
    return pl.pallas_call(
        body,
        name="loss_head",
        grid=(S // tm,),
        in_specs=[row, row],
        out_specs=[row, pl.BlockSpec((1, 1), lambda i: (0, 0))],
        out_shape=[jax.ShapeDtypeStruct((S, D), F32), jax.ShapeDtypeStruct((1, 1), F32)],
        compiler_params=_cp(("arbitrary",)),
    )(h3, target)


def _tie(x, dep):
    if dep is None:
        return x
    return x + dep[0, 0].astype(x.dtype)


def _local_step(x, pos_col, target, get_w, p, emit):
    w = get_w("ffn1", None)
    a1, b1, hm1 = _ffn_up(x, w["g1"], w["u1"], "ffn1_up", dep=w.get("_dep"))
    h1, h1b, xh1, rs1 = _ffn_down(hm1, w["d1"], x, p["ln1_g"], p["ln1_b"], "ffn1_down")

    w.update(get_w("win", h1b))
    qkv = _matmul(h1b, w["win"], "nt", F32, 1024, 1536, D, "proj_qkv", b_off=0, n_out=QKV_W, dep=w.get("_dep"))
    z = glog = _matmul(h1b, w["win"], "nt", F32, 1024, 512, D, "proj_zg", b_off=QKV_W // 512, n_out=4 * D)

    half = jnp.arange(0, HEAD_DIM, 2, dtype=F32) / HEAD_DIM
    inv_freq = ROPE_THETA ** (-half)
    invf = jnp.tile(inv_freq, 4).reshape(1, BLK)
    sign = jnp.tile(jnp.concatenate([-jnp.ones((32,), F32), jnp.ones((32,), F32)]), 2).reshape(1, BLK)
    cos_t, sin_t = _rope_tables(pos_col, invf, sign)

    get_w("mix", qkv, early=True)
    preps = [_qkv_prep(qkv, cos_t, sin_t, gi) for gi in range(3)]
    os_, lses = [], []
    for gi in range(3):
        o, lse = _attn_fwd(*preps[gi], gi)
        os_.append(o)
        lses.append(lse)
    ya = _combine_fwd(os_, lses)
    bst = p["gmlp_b_s"].T
    yg = _gmlp_fwd(z, p["gmlp_w_s"], bst, p["gmlp_ln_g"], p["gmlp_ln_b"])
    w.update(get_w("mix", yg))
    get_w("ffn2", yg, early=True)
    h2, h2b, xh2, rs2, merged, bra, brg = _merge_fwd(ya, yg, glog, p["b_gates"], h1, w["ab"], w["gb"], w["o"],
                                                      p["ln2_g"], p["ln2_b"])
    w.update(get_w("ffn2", h2b))
    a2, b2, hm2 = _ffn_up(h2b, w["g2"], w["u2"], "ffn2_up")
    h3, _, xh3, rs3 = _ffn_down(hm2, w["d2"], h2, p["ln3_g"], p["ln3_b"], "ffn2_down")
    dh3, loss = _loss_head(h3, target)

    gp = {}
    dr3, df2, da2, db2, gp["ln3_g"], gp["ln3_b"] = _ffn_bwd_mid(dh3, None, xh3, rs3, p["ln3_g"], a2, b2, w["d2"],
                                                                "ffn2_bwd_mid")
    tok = emit("ffn2", {
        "g2": _matmul(da2, h2b, "tn", BF16, 1408, D, S, "wgrad_g2"),
        "u2": _matmul(db2, h2b, "tn", BF16, 1408, D, S, "wgrad_u2"),
        "d2": _matmul(hm2, df2, "tn", BF16, 1408, D, S, "wgrad_d2")})
    dh2 = _ffn_bwd_dx(dr3, da2, db2, w["g2"], w["u2"], "ffn2_bwd_dx")

    (dr2, dr2b, dproj, dba, dbg, dya, dyg, gp["b_gates"], gp["ln2_g"], gp["ln2_b"]) = _merge_bwd(
        dh2, xh2, rs2, _tie(p["ln2_g"], tok), bra, brg, glog, p["b_gates"], w["ab"], w["gb"], w["o"])
    tok = emit("mix", {
        "o": _matmul(merged, dr2b, "tn", BF16, 512, D, S, "wgrad_o"),
        "ab": _matmul(dba, ya, "tn", BF16, 512, GW, S, "wgrad_ab"),
        "gb": _matmul(yg, dbg, "tn", BF16, 512, D, S, "wgrad_gb")})

    seg = (jnp.arange(GW)[:, None] // HEAD_DIM == jnp.arange(GW)[None, :] // HEAD_DIM).astype(BF16)
    do0, do1, do2, c0, c1, c2 = _combine_bwd(dya, os_, lses, _tie(seg, tok))
    for gi, (do, ct) in enumerate(((do0, c0), (do1, c1), (do2, c2))):
        dq, dk, dv = _attn_bwd(*preps[gi], do, lses[gi], ct, gi)
        dproj = _qkv_unprep(dq, dk, dv, cos_t, sin_t, gi, dproj)
    dproj, gp["gmlp_w_s"], dbst, gp["gmlp_ln_g"], gp["gmlp_ln_b"] = _gmlp_bwd(
        z, dyg, p["gmlp_w_s"], bst, p["gmlp_ln_g"], p["gmlp_ln_b"], dproj)
    gp["gmlp_b_s"] = dbst.T
    tok = emit("win", {"win": _matmul(dproj, h1b, "tn", BF16, 512, D, S, "wgrad_win", a_map=_dproj_tile, m_out=IN_W)})
    dh1m = _matmul(dproj, w["win"], "nn", F32, 1024, D, 512, "dproj_to_dh1", dep=tok, a_map=_dproj_tile)

    dr1, df1, da1, db1, gp["ln1_g"], gp["ln1_b"] = _ffn_bwd_mid(dr2, dh1m, xh1, rs1, p["ln1_g"], a1, b1, w["d1"],
                                                                "ffn1_bwd_mid")
    tok = emit("small", {**gp, "loss": loss})
    tok = emit("g1", {"g1": _matmul(da1, x, "tn", BF16, 1408, D, S, "wgrad_g1", dep=tok)})
    tok = emit("u1", {"u1": _matmul(db1, x, "tn", BF16, 1408, D, S, "wgrad_u1", dep=tok)})
    tok = emit("d1", {"d1": _matmul(hm1, df1, "tn", BF16, 1408, D, S, "wgrad_d1", dep=tok)})
    dx = _ffn_bwd_dx(dr1, da1, db1, w["g1"], w["u1"], "ffn1_bwd_dx", dep=tok)
    return loss, dx, gp


_FLIPS = [(mx, my, mc) for mx in (0, 1) for my in (0, 1) for mc in (0, 1)][1:]
_GROUPS = {"ffn1": ("g1", "u1", "d1"), "win": ("win",), "mix": ("ab", "gb", "o"), "ffn2": ("g2", "u2", "d2")}
_SCATTERS = {"ffn2": ("g2", "u2", "d2"), "mix": ("ab", "gb", "o"), "win": ("win",), "g1": ("g1",), "u1": ("u1",), "d1": ("d1",)}
_TWO_STAGE = ("g1", "u1", "d1")
_HBM = pl.BlockSpec(memory_space=pltpu.HBM)
_SEM = pl.BlockSpec(memory_space=pltpu.SEMAPHORE)
_EFFECT = pltpu.SideEffectType.DATAFLOW_SIDE_EFFECTING


def _me():
    return 4 * lax.axis_index("x") + 2 * lax.axis_index("y") + lax.axis_index("c")


def _copies_start(bufs, copies, n_sem, name, after=None):
    nb = len(bufs)
    n_after = 0 if after is None else 1

    def body(*refs):
        b = refs[:nb]
        send_sems, recv_sems = refs[nb + n_after], refs[nb + n_after + 1]
        token = refs[-1]
        x, y, c = lax.axis_index("x"), lax.axis_index("y"), lax.axis_index("c")
        me = 4 * x + 2 * y + c
        for si, s_slot, di, d_slot, (mx, my, mc), sem in copies:
            s_idx, d_idx = s_slot(me), d_slot(me)
            pltpu.make_async_remote_copy(
                src_ref=b[si] if s_idx is None else b[si].at[s_idx],
                dst_ref=b[di] if d_idx is None else b[di].at[d_idx],
                send_sem=send_sems.at[sem], recv_sem=recv_sems.at[sem],
                device_id=(x ^ mx, y ^ my, c ^ mc), device_id_type=MESH).start()
        token[...] = jnp.zeros_like(token)

    ins = [pltpu.with_memory_space_constraint(a, pltpu.HBM) for a in bufs]
    outs = pl.pallas_call(
        body,
        name=name,
        in_specs=[_HBM] * nb + [pl.BlockSpec(memory_space=pl.ANY)] * n_after,
        out_specs=[_SEM, _SEM] + [_HBM] * nb + [pl.BlockSpec(memory_space=pltpu.VMEM)],
        out_shape=[pltpu.SemaphoreType.DMA((n_sem,)), pltpu.SemaphoreType.DMA((n_sem,))]
        + [pltpu.HBM(a.shape, a.dtype) for a in ins] + [jax.ShapeDtypeStruct((8, 128), F32)],
        input_output_aliases={k: 2 + k for k in range(nb)},
        compiler_params=pltpu.CompilerParams(has_side_effects=_EFFECT),
    )(*ins, *([] if after is None else [after]))
    return outs[0], outs[1], list(outs[2:2 + nb]), outs[-1]


def _copies_wait(started, waits, after, name):
    send_sems, recv_sems, bufs, _ = started
    nb = len(bufs)

    def body(*refs):
        b = refs[:nb]
        ss, rs = refs[nb], refs[nb + 1]
        me3 = (lax.axis_index("x"), lax.axis_index("y"), lax.axis_index("c"))
        for bi, n_blocks, sem, is_send in waits:
            blocks = b[bi].at[pl.ds(0, n_blocks)]
            cp = pltpu.make_async_remote_copy(src_ref=blocks, dst_ref=blocks, send_sem=ss.at[sem], recv_sem=rs.at[sem],
                                              device_id=me3, device_id_type=MESH)
            if is_send:
                cp.wait_send()
            else:
                cp.wait_recv()

    return pl.pallas_call(
        body,
        name=name,
        in_specs=[_HBM] * nb + [_SEM, _SEM, pl.BlockSpec(memory_space=pl.ANY)],
        out_specs=[_HBM] * nb,
        out_shape=[pltpu.HBM(a.shape, a.dtype) for a in bufs],
        input_output_aliases={k: k for k in range(nb)},
        compiler_params=pltpu.CompilerParams(has_side_effects=_EFFECT),
    )(*bufs, send_sems, recv_sems, after)


def _landing(own, me):
    return lax.dynamic_update_slice(lax.empty((N_DEV,) + own.shape[1:], own.dtype), own, (me, 0, 0))


_SIBLING = (0, 0, 1)
_OTHER_CHIPS = ((1, 0, 0), (0, 1, 0), (1, 1, 0))


def _bits(flip):
    return 4 * flip[0] + 2 * flip[1] + flip[2]


def _gather_send(shards, grp, after):
    nw = len(shards)
    me = _me()
    bufs = list(shards) + [_landing(a[None], me) for a in shards]
    copies = []
    for k in range(nw):
        copies.append((k, lambda me: None, nw + k, lambda me: me, _SIBLING, nw + k))
        copies += [(k, lambda me: None, nw + k, lambda me: me, f, k) for f in _OTHER_CHIPS]
    return _copies_start(bufs, copies, 2 * nw, "gather_send_" + grp, after)


def _gather_pass(started, grp, after):
    nw = len(started[2]) // 2
    waits = []
    for k in range(nw):
        waits += [(nw + k, 3, k, True), (nw + k, 1, nw + k, True), (nw + k, 3, k, False), (nw + k, 1, nw + k, False)]
    lands = list(_copies_wait(started, waits, after, "gather_arrived_" + grp)[nw:])
    copies = []
    for k in range(nw):
        for f in _OTHER_CHIPS:
            slot = functools.partial(lambda me, bits: me ^ bits, bits=_bits(f))
            copies.append((k, slot, k, slot, _SIBLING, k))
    return _copies_start(lands, copies, nw, "gather_pass_" + grp)


def _gather_finish(passed, grp, after):
    nw = len(passed[2])
    waits = [(k, 3, k, is_send) for k in range(nw) for is_send in (True, False)]
    return _copies_wait(passed, waits, passed[3] if after is None else after, "gather_done_" + grp)


def _scatter_send(parts, grp, after=None):
    nw = len(parts)
    me = _me()
    bufs = list(parts) + [_landing(lax.dynamic_index_in_dim(a, me, 0, keepdims=True), me) for a in parts]
    copies = []
    for k in range(nw):
        for f in _FLIPS:
            to = functools.partial(lambda me, bits: me ^ bits, bits=_bits(f))
            copies.append((k, to, nw + k, lambda me: me, f, k))
    return _copies_start(bufs, copies, nw, "scatter_send_" + grp, after)


def _scatter_finish(started, grp, after):
    nw = len(started[2]) // 2
    waits = [(nw + k, N_DEV - 1, k, is_send) for k in range(nw) for is_send in (True, False)]
    return _copies_wait(started, waits, after, "scatter_done_" + grp)[nw:]


N_CHIP = N_DEV // 2


def _pair_sum(part, other, key):
    _, r, c = part.shape
    tm = _ADAM_ROWS.get(r, r)
    core = lax.axis_index("c").astype(jnp.int32).reshape(1)

    def body(core_ref, a_ref, b_ref, o_ref):
        o_ref[...] = (a_ref[...].astype(F32) + b_ref[...].astype(F32)).astype(BF16)

    return pl.pallas_call(
        body,
        name="pair_sum_" + key,
        grid_spec=pltpu.PrefetchScalarGridSpec(
            num_scalar_prefetch=1,
            grid=(N_CHIP, r // tm),
            in_specs=[pl.BlockSpec((None, tm, c), lambda q, i, core_ref: (2 * q + core_ref[0], i, 0)),
                      pl.BlockSpec((None, tm, c), lambda q, i, core_ref: (q, i, 0))],
            out_specs=pl.BlockSpec((None, tm, c), lambda q, i, core_ref: (q, i, 0)),
        ),
        out_shape=jax.ShapeDtypeStruct((N_CHIP, r, c), BF16),
        compiler_params=_cp(("parallel", "parallel")),
    )(core, part, other)


def _scatter2_send(part, key, dep=None):
    me = _me()
    _, r, c = part.shape
    swap = []
    for q in range(N_CHIP):
        src = functools.partial(lambda me, q: 2 * q + 1 - me % 2, q=q)
        swap.append((0, src, 1, functools.partial(lambda me, q: q, q=q), _SIBLING, 0))
    started = _copies_start([part, lax.empty((N_CHIP, r, c), part.dtype)], swap, 1, "scatter_swap_" + key, dep)
    waits = [(1, N_CHIP, 0, True), (1, N_CHIP, 0, False)]
    part, other = _copies_wait(started, waits, started[3], "scatter_swapped_" + key)
    pair = _pair_sum(part, other, key)
    chip = me // 2
    land = lax.dynamic_update_slice(lax.empty((N_CHIP, r, c), pair.dtype),
                                    lax.dynamic_index_in_dim(pair, chip, 0, keepdims=True), (chip, 0, 0))
    copies = []
    for f in _OTHER_CHIPS:
        to = functools.partial(lambda me, bits: (me ^ bits) // 2, bits=_bits(f))
        copies.append((0, to, 1, lambda me: me // 2, f, 0))
    return _copies_start([pair, land], copies, 1, "scatter_send_" + key)


def _scatter2_finish(started, key, after):
    waits = [(1, N_CHIP - 1, 0, True), (1, N_CHIP - 1, 0, False)]
    return _copies_wait(started, waits, after, "scatter_done_" + key)[1]


def _allgather_send(block, name, after=None):
    me = _me()
    copies = [(0, lambda me: None, 1, lambda me: me, f, 0) for f in _FLIPS]
    return _copies_start([block, _landing(block[None], me)], copies, 1, name, after)


def _allgather_finish(started, name, after):
    waits = [(1, N_DEV - 1, 0, True), (1, N_DEV - 1, 0, False)]
    return _copies_wait(started, waits, after, name)[1]


_ADAM_ROWS = {352: 176, 1088: 272}


def _sum_adamw(parts, wv, m, v, name):
    n_parts, r, c = parts.shape
    tm = _ADAM_ROWS.get(r, r)
    assert r % tm == 0 and wv.shape == (r, c)

    def body(p_ref, w_ref, m_ref, v_ref, g_ref, d_ref, mo_ref, vo_ref):
        gv = p_ref[0].astype(F32)
        for j in range(1, n_parts):
            gv = gv + p_ref[j].astype(F32)
        g_ref[...] = gv
        mn = ADAM_B1 * m_ref[...] + (1.0 - ADAM_B1) * gv
        vn = ADAM_B2 * v_ref[...] + (1.0 - ADAM_B2) * (gv * gv)
        m_hat = mn / (1.0 - ADAM_B1 ** ADAM_STEP)
        v_hat = vn / (1.0 - ADAM_B2 ** ADAM_STEP)
        d_ref[...] = -ADAM_LR * (m_hat / (jnp.sqrt(v_hat) + ADAM_EPS) + ADAM_WD * w_ref[...])
        mo_ref[...] = mn
        vo_ref[...] = vn

    sp = pl.BlockSpec((tm, c), lambda i: (i, 0))
    return pl.pallas_call(
        body,
        name=name,
        grid=(r // tm,),
        in_specs=[pl.BlockSpec((n_parts, tm, c), lambda i: (0, i, 0))] + [sp] * 3,
        out_specs=[sp] * 4,
        out_shape=[jax.ShapeDtypeStruct((r, c), F32)] * 4,
        compiler_params=_cp(("parallel",), 48),
    )(parts, wv, m, v)


_WEIGHTS = ["ffn1_w_gate", "ffn1_w_up", "ffn1_w_down", "ln1_g", "ln1_b", "w_in", "b_gates", "gmlp_ln_g", "gmlp_ln_b",
            "gmlp_w_s", "gmlp_b_s", "w_attn_branch", "w_gmlp_branch", "w_out", "ln2_g", "ln2_b", "ffn2_w_gate",
            "ffn2_w_up", "ffn2_w_down", "ln3_g", "ln3_b"]
_BIG_OF = {"ffn1_w_gate": ("g1", True), "ffn1_w_up": ("u1", True), "ffn1_w_down": ("d1", False), "w_in": ("win", True),
           "w_attn_branch": ("ab", True), "w_gmlp_branch": ("gb", False), "w_out": ("o", False),
           "ffn2_w_gate": ("g2", True), "ffn2_w_up": ("u2", True), "ffn2_w_down": ("d2", False)}
_SMALL = [n for n in _WEIGHTS if n not in _BIG_OF]
_SMALL_ROWS = {"gmlp_w_s": 128, "b_gates": 2}
_SMALL_SLOT = 8


def _pack_small(d, last=None):
    rows = []
    for n in _SMALL:
        r = d[n].reshape(-1, D)
        slot = max(r.shape[0], _SMALL_SLOT)
        rows.append(jnp.pad(r, ((0, slot - r.shape[0]), (0, 0))))
    rows.append(jnp.zeros((_SMALL_SLOT, D), F32) if last is None else jnp.broadcast_to(last.reshape(1, 1), (_SMALL_SLOT, D)))
    return jnp.concatenate(rows, axis=0)


def _unpack_small(packed, shapes):
    out, at = {}, 0
    for n in _SMALL:
        k = _SMALL_ROWS.get(n, 1)
        out[n] = packed[at:at + k].reshape(shapes[n])
        at += max(k, _SMALL_SLOT)
    return out


def kernel(x, positions, ffn1_w_gate, ffn1_w_up, ffn1_w_down, ln1_g, ln1_b, w_in, b_gates, gmlp_ln_g, gmlp_ln_b, gmlp_w_s, gmlp_b_s, w_attn_branch, w_gmlp_branch, w_out, ln2_g, ln2_b, ffn2_w_gate, ffn2_w_up, ffn2_w_down, ln3_g, ln3_b, loss_target, m_ffn1_w_gate, m_ffn1_w_up, m_ffn1_w_down, m_ln1_g, m_ln1_b, m_w_in, m_b_gates, m_gmlp_ln_g, m_gmlp_ln_b, m_gmlp_w_s, m_gmlp_b_s, m_w_attn_branch, m_w_gmlp_branch, m_w_out, m_ln2_g, m_ln2_b, m_ffn2_w_gate, m_ffn2_w_up, m_ffn2_w_down, m_ln3_g, m_ln3_b, v_ffn1_w_gate, v_ffn1_w_up, v_ffn1_w_down, v_ln1_g, v_ln1_b, v_w_in, v_b_gates, v_gmlp_ln_g, v_gmlp_ln_b, v_gmlp_w_s, v_gmlp_b_s, v_w_attn_branch, v_w_gmlp_branch, v_w_out, v_ln2_g, v_ln2_b, v_ffn2_w_gate, v_ffn2_w_up, v_ffn2_w_down, v_ln3_g, v_ln3_b):
    args = dict(locals())
    wts = {n: args[n] for n in _WEIGHTS}
    ms = {n: args["m_" + n] for n in _WEIGHTS}
    vs = {n: args["v_" + n] for n in _WEIGHTS}

    name_of = {key: (n, tr) for n, (key, tr) in _BIG_OF.items()}

    shards = {}
    for grp, keys in _GROUPS.items():
        shards[grp] = []
        for key in keys:
            n, tr = name_of[key]
            s2 = wts[n][0]
            shards[grp].append((s2.T if tr else s2).astype(BF16))
    started = {"ffn1": _gather_send(shards["ffn1"], "ffn1", None)}
    passed = {}

    def get_w(grp, after, early=False):
        out = {}
        if grp not in passed:
            passed[grp] = _gather_pass(started[grp], grp, started[grp][3] if after is None else after)
            after = None
            later = {"ffn1": ("win",), "win": ("mix", "ffn2")}.get(grp, ())
            tok = passed[grp][3]
            for nxt in later:
                started[nxt] = _gather_send(shards[nxt], nxt, tok)
                tok = started[nxt][3]
            if later:
                out["_dep"] = tok
        if early:
            return None
        lands = _gather_finish(passed[grp], grp, after)
        out.update({key: g.reshape(-1, g.shape[-1]) for key, g in zip(_GROUPS[grp], lands)})
        return out

    sent = {}

    def emit(grp, grads):
        if grp == "small":
            sent[grp] = _allgather_send(_pack_small(grads, last=grads["loss"]), "small_grads_send")
        else:
            parts = [grads[key].reshape(N_DEV, -1, grads[key].shape[-1]) for key in _SCATTERS[grp]]
            sent[grp] = _scatter2_send(parts[0], grp) if grp in _TWO_STAGE else _scatter_send(parts, grp)
        return sent[grp][3]

    p = {n: (wts[n][0] if n in ("gmlp_w_s", "gmlp_b_s") else wts[n]) for n in _SMALL}
    loss, dx, gp = _local_step(x[0], positions.reshape(S, 1), loss_target[0], get_w, p, emit)
    grads, deltas, new_m, new_v = {}, {}, {}, {}
    after = dx
    for grp in ("ffn2", "mix", "win", "small", "g1", "u1", "d1"):
        if grp == "small":
            parts = _allgather_finish(sent[grp], "small_grads_done", after)
            outs = _sum_adamw(parts, *[_pack_small({n: d[n] for n in _SMALL}) for d in (wts, ms, vs)], "update_small")
            shapes = {n: wts[n].shape for n in _SMALL}
            for dst, packed in zip((grads, deltas, new_m, new_v), outs):
                dst.update(_unpack_small(packed, shapes))
            loss = outs[0][-_SMALL_SLOT, 0]
            after = outs[1]
            continue
        arrived = ([_scatter2_finish(sent[grp], grp, after)] if grp in _TWO_STAGE
                   else _scatter_finish(sent[grp], grp, after))
        for key, part in zip(_SCATTERS[grp], arrived):
            n, tr = name_of[key]
            outs = _sum_adamw(part, *[(d[n][0].T if tr else d[n][0]) for d in (wts, ms, vs)], "update_" + key)
            for dst, o in zip((grads, deltas, new_m, new_v), outs):
                dst[n] = (o.T if tr else o)[None]
            after = outs[1]

    return (loss, dx[None], *[grads[n] for n in _WEIGHTS], *[deltas[n] for n in _WEIGHTS],
            *[new_m[n] for n in _WEIGHTS], *[new_v[n] for n in _WEIGHTS])
```

```python
import functools
import math

import jax
import jax.numpy as jnp
from jax import lax
from jax.experimental import pallas as pl
from jax.experimental.pallas import tpu as pltpu

F32 = jnp.float32
BF16 = jnp.bfloat16

N_DEV = 8
D = 1024
S = 2048
F = 2816
HEAD_DIM = 64
HEADS = 8
GW = HEADS * HEAD_DIM
PATTERN_DILATIONS = (1, 4, 16)
BLK = 128
QKV_W = 3 * 3 * GW
IN_W = QKV_W + 2 * D + 2 * D
DPROJ_W = 5 * 2 * D
DPROJ_Z_COL, DPROJ_G_COL = 3, 4


def _dproj_tile(t):
    three = jnp.int32(3)
    return jnp.where(t < 9, lax.rem(t, three) * 3 + lax.div(t, three), t + 3)
ROPE_THETA = 10000.0
ALPHA = 2.0 ** 0.25
LN_EPS = 1e-5
ADAM_LR, ADAM_B1, ADAM_B2, ADAM_EPS, ADAM_WD, ADAM_STEP = 0.001, 0.9, 0.999, 1e-08, 0.01, 10
NEG = -1e30
MESH = pl.DeviceIdType.MESH


def _cp(sem=None, vmem_mb=None):
    kw = {}
    if sem is not None:
        kw["dimension_semantics"] = sem
    if vmem_mb is not None:
        kw["vmem_limit_bytes"] = vmem_mb << 20
    return pltpu.CompilerParams(**kw)


def _dot_nn(a, b):
    return lax.dot_general(a, b, (((1,), (0,)), ((), ())), preferred_element_type=F32)


def _dot_nt(a, b):
    return lax.dot_general(a, b, (((1,), (1,)), ((), ())), preferred_element_type=F32)


def _dot_tn(a, b):
    return lax.dot_general(a, b, (((0,), (0,)), ((), ())), preferred_element_type=F32)


def _ln_fwd(r, g, b):
    mu = jnp.mean(r, axis=-1, keepdims=True)
    xc = r - mu
    var = jnp.mean(xc * xc, axis=-1, keepdims=True)
    rstd = lax.rsqrt(var + LN_EPS)
    xhat = xc * rstd
    return xhat * g + b, xhat, rstd


def _ln_bwd(dh, xhat, rstd, g):
    dxh = dh * g
    m1 = jnp.mean(dxh, axis=-1, keepdims=True)
    m2 = jnp.mean(dxh * xhat, axis=-1, keepdims=True)
    return rstd * (dxh - m1 - xhat * m2)


def _sigmoid(x):
    return 0.5 * jnp.tanh(0.5 * x) + 0.5


def _colsum(x):
    return jnp.sum(x, axis=0, keepdims=True)


def _matmul(a, b, mode, out_dtype, tm, tn, tk, name, b_off=0, n_out=None, dep=None, a_map=None, m_out=None):
    n_dep = 0 if dep is None else 1
    a_map = a_map or (lambda t: t)
    if mode == "nn":
        m, k = a.shape[0], b.shape[0]
        n = b.shape[1]
    elif mode == "nt":
        m, k = a.shape
        n = n_out if n_out is not None else b.shape[0]
    else:
        k, m = a.shape[0], m_out or a.shape[1]
        n = b.shape[1]
    nk = k // tk
    assert m % tm == 0 and n % tn == 0 and k % tk == 0
    dot = {"nn": _dot_nn, "nt": _dot_nt, "tn": _dot_tn}[mode]

    def body(a_ref, b_ref, *rest):
        o_ref, scr = rest[n_dep], rest[n_dep + 1:]
        r = dot(a_ref[...].astype(BF16), b_ref[...].astype(BF16))
        if nk == 1:
            o_ref[...] = r.astype(out_dtype)
        else:
            acc = scr[0]
            kk = pl.program_id(2)

            @pl.when(kk == 0)
            def _():
                acc[...] = r

            @pl.when(kk > 0)
            def _():
                acc[...] += r

            @pl.when(kk == nk - 1)
            def _():
                o_ref[...] = acc[...].astype(out_dtype)

    if mode == "nn":
        a_spec = pl.BlockSpec((tm, tk), lambda i, j, kk: (i, a_map(kk)))
        b_spec = pl.BlockSpec((tk, tn), lambda i, j, kk: (kk, j))
    elif mode == "nt":
        a_spec = pl.BlockSpec((tm, tk), lambda i, j, kk: (i, kk))
        b_spec = pl.BlockSpec((tn, tk), lambda i, j, kk: (j + b_off, kk))
    else:
        a_spec = pl.BlockSpec((tk, tm), lambda i, j, kk: (kk, a_map(i)))
        b_spec = pl.BlockSpec((tk, tn), lambda i, j, kk: (kk, j))
    return pl.pallas_call(
        body,
        name=name,
        grid=(m // tm, n // tn, nk),
        in_specs=[a_spec, b_spec] + [pl.BlockSpec(memory_space=pl.ANY)] * n_dep,
        out_specs=pl.BlockSpec((tm, tn), lambda i, j, kk: (i, j)),
        out_shape=jax.ShapeDtypeStruct((m, n), out_dtype),
        scratch_shapes=[] if nk == 1 else [pltpu.VMEM((tm, tn), F32)],
        compiler_params=_cp(("parallel", "parallel", "arbitrary"), 56),
    )(a, b, *([] if dep is None else [dep]))


def _ffn_up(x, wgt, wut, name, tm=512, tn=1408, dep=None):
    n_dep = 0 if dep is None else 1

    def body(x_ref, wg_ref, wu_ref, *rest):
        a_ref, b_ref, hm_ref = rest[n_dep:]
        xb = x_ref[...].astype(BF16)
        a = _dot_nt(xb, wg_ref[...])
        b = _dot_nt(xb, wu_ref[...])
        a_ref[...] = a.astype(BF16)
        b_ref[...] = b.astype(BF16)
        hm_ref[...] = ((a * _sigmoid(a)) * b).astype(BF16)

    wsp = pl.BlockSpec((tn, D), lambda i, j: (j, 0))
    mid = pl.BlockSpec((tm, tn), lambda i, j: (i, j))
    return pl.pallas_call(
        body,
        name=name,
        grid=(S // tm, F // tn),
        in_specs=[pl.BlockSpec((tm, D), lambda i, j: (i, 0)), wsp, wsp] + [pl.BlockSpec(memory_space=pl.ANY)] * n_dep,
        out_specs=[mid, mid, mid],
        out_shape=[jax.ShapeDtypeStruct((S, F), BF16)] * 3,
        compiler_params=_cp(("parallel", "arbitrary"), 56),
    )(x, wgt, wut, *([] if dep is None else [dep]))


def _ffn_down(hm, wd, x, ln_g, ln_b, name, tm=256):
    def body(hm_ref, wd_ref, x_ref, g_ref, b_ref, h_ref, hb_ref, xh_ref, rs_ref):
        r = ALPHA * x_ref[...] + 0.5 * _dot_nn(hm_ref[...], wd_ref[...])
        h, xhat, rstd = _ln_fwd(r, g_ref[...], b_ref[...])
        h_ref[...] = h
        hb_ref[...] = h.astype(BF16)
        xh_ref[...] = xhat
        rs_ref[...] = rstd

    row = pl.BlockSpec((tm, D), lambda i: (i, 0))
    vec = pl.BlockSpec((1, D), lambda i: (0, 0))
    return pl.pallas_call(
        body,
        name=name,
        grid=(S // tm,),
        in_specs=[pl.BlockSpec((tm, F), lambda i: (i, 0)), pl.BlockSpec((F, D), lambda i: (0, 0)), row, vec, vec],
        out_specs=[row, row, row, pl.BlockSpec((tm, 1), lambda i: (i, 0))],
        out_shape=[
            jax.ShapeDtypeStruct((S, D), F32),
            jax.ShapeDtypeStruct((S, D), BF16),
            jax.ShapeDtypeStruct((S, D), F32),
            jax.ShapeDtypeStruct((S, 1), F32),
        ],
        compiler_params=_cp(("parallel",), 56),
    )(hm, wd, x, ln_g, ln_b)


def _ffn_bwd_mid(dh_a, dh_b, xhat, rstd, ln_g, a, b, wd, name, tm=512, tn=1408):
    two = dh_b is not None

    def body(*refs):
        dha_ref = refs[0]
        dhb_ref = refs[1] if two else None
        (xh_ref, rs_ref, g_ref, a_ref, b_ref, wd_ref,
         dr_ref, df_ref, da_ref, db_ref, dg_ref, dbias_ref, df_scr) = refs[2 if two else 1:]
        i = pl.program_id(0)
        j = pl.program_id(1)

        @pl.when(j == 0)
        def _():
            dh = dha_ref[...]
            if two:
                dh = ALPHA * dh + dhb_ref[...]
            xhat = xh_ref[...]
            dr = _ln_bwd(dh, xhat, rs_ref[...], g_ref[...])
            dfb = (0.5 * dr).astype(BF16)
            dr_ref[...] = dr
            df_scr[...] = dfb
            df_ref[...] = dfb
            sg = _colsum(dh * xhat)
            sb = _colsum(dh)

            @pl.when(i == 0)
            def _():
                dg_ref[...] = sg
                dbias_ref[...] = sb

            @pl.when(i > 0)
            def _():
                dg_ref[...] += sg
                dbias_ref[...] += sb

        dhm = _dot_nt(df_scr[...], wd_ref[...])
        av = a_ref[...].astype(F32)
        bv = b_ref[...].astype(F32)
        sig = _sigmoid(av)
        da_ref[...] = (dhm * bv * (sig * (1.0 + av * (1.0 - sig)))).astype(BF16)
        db_ref[...] = (dhm * (av * sig)).astype(BF16)

    row = pl.BlockSpec((tm, D), lambda i, j: (i, 0))
    vec = pl.BlockSpec((1, D), lambda i, j: (0, 0))
    mid = pl.BlockSpec((tm, tn), lambda i, j: (i, j))
    ins = [dh_a] + ([dh_b] if two else []) + [xhat, rstd, ln_g, a, b, wd]
    in_specs = [row] * (2 if two else 1) + [row, pl.BlockSpec((tm, 1), lambda i, j: (i, 0)), vec, mid, mid,
                                            pl.BlockSpec((tn, D), lambda i, j: (j, 0))]
    return pl.pallas_call(
        body,
        name=name,
        grid=(S // tm, F // tn),
        in_specs=in_specs,
        out_specs=[row, row, mid, mid, vec, vec],
        out_shape=[
            jax.ShapeDtypeStruct((S, D), F32),
            jax.ShapeDtypeStruct((S, D), BF16),
            jax.ShapeDtypeStruct((S, F), BF16),
            jax.ShapeDtypeStruct((S, F), BF16),
            jax.ShapeDtypeStruct((1, D), F32),
            jax.ShapeDtypeStruct((1, D), F32),
        ],
        scratch_shapes=[pltpu.VMEM((tm, D), BF16)],
        compiler_params=_cp(("arbitrary", "arbitrary"), 56),
    )(*ins)


def _ffn_bwd_dx(dr, da, db, wgt, wut, name, tm=512, tk=1408, dep=None):
    nk = F // tk
    n_dep = 0 if dep is None else 1

    def body(dr_ref, da_ref, db_ref, wg_ref, wu_ref, *rest):
        dx_ref, acc = rest[n_dep], rest[n_dep + 1]
        kk = pl.program_id(1)
        part = _dot_nn(da_ref[...], wg_ref[...]) + _dot_nn(db_ref[...], wu_ref[...])

        @pl.when(kk == 0)
        def _():
            acc[...] = ALPHA * dr_ref[...] + part

        @pl.when(kk > 0)
        def _():
            acc[...] += part

        @pl.when(kk == nk - 1)
        def _():
            dx_ref[...] = acc[...]

    row = pl.BlockSpec((tm, D), lambda i, kk: (i, 0))
    mid = pl.BlockSpec((tm, tk), lambda i, kk: (i, kk))
    wsp = pl.BlockSpec((tk, D), lambda i, kk: (kk, 0))
    return pl.pallas_call(
        body,
        name=name,
        grid=(S // tm, nk),
        in_specs=[row, mid, mid, wsp, wsp] + [pl.BlockSpec(memory_space=pl.ANY)] * n_dep,
        out_specs=row,
        out_shape=jax.ShapeDtypeStruct((S, D), F32),
        scratch_shapes=[pltpu.VMEM((tm, D), F32)],
        compiler_params=_cp(("parallel", "arbitrary"), 56),
    )(dr, da, db, wgt, wut, *([] if dep is None else [dep]))


def _rope_tables(pos_col, invf, sign, tm=512):
    def body(p_ref, f_ref, s_ref, c_out, s_out):
        ang = p_ref[...].astype(F32) * f_ref[...]
        c_out[...] = jnp.cos(ang)
        s_out[...] = jnp.sin(ang) * s_ref[...]

    vec = pl.BlockSpec((1, BLK), lambda i: (0, 0))
    out = pl.BlockSpec((tm, BLK), lambda i: (i, 0))
    return pl.pallas_call(
        body,
        name="rope_tables",
        grid=(S // tm,),
        in_specs=[pl.BlockSpec((tm, 1), lambda i: (i, 0)), vec, vec],
        out_specs=[out, out],
        out_shape=[jax.ShapeDtypeStruct((S, BLK), F32)] * 2,
        compiler_params=_cp(("parallel",)),
    )(pos_col, invf, sign)


def _lane_lo(rows=BLK):
    return (lax.broadcasted_iota(jnp.int32, (rows, GW), 1) % HEAD_DIM) < (HEAD_DIM // 2)


def _swap_halves(t, lo):
    return jnp.where(lo, pltpu.roll(t, GW - HEAD_DIM // 2, 1), pltpu.roll(t, HEAD_DIM // 2, 1))


def _rope(t, cosf, sinf, lo):
    return t * cosf + _swap_halves(t, lo) * sinf


def _unrope(g, cosf, sinf, lo):
    return g * cosf + _swap_halves(g * sinf, lo)


def _tile4(v):
    return jnp.concatenate([v, v, v, v], axis=1)


def _band_mask(n):
    qi = lax.broadcasted_iota(jnp.int32, (BLK, 2 * BLK), 0)
    kj = lax.broadcasted_iota(jnp.int32, (BLK, 2 * BLK), 1)
    dist = qi + BLK - kj
    return (dist >= 0) & (dist <= BLK) & ((kj >= BLK) | (n >= 1))


ROWS = 256
LANES = 128


def _to_view(scr, y, dst_ref, d, dtype, col0=0):
    if d == 1:
        dst_ref[:, col0:col0 + GW] = y.astype(dtype)
        return
    for cb in range(GW // LANES):
        scr[cb][...] = y[:, cb * LANES:(cb + 1) * LANES]
    for r in range(d):
        for cb in range(GW // LANES):
            at = col0 + r * GW + cb * LANES
            dst_ref[:, at:at + LANES] = scr[cb][pl.ds(r, ROWS // d, stride=d), :].astype(dtype)


def _from_view(scr, src_ref, d):
    if d == 1:
        return src_ref[...].astype(F32)
    for r in range(d):
        for cb in range(GW // LANES):
            at = r * GW + cb * LANES
            scr[cb][pl.ds(r, ROWS // d, stride=d), :] = src_ref[:, at:at + LANES].astype(F32)
    return jnp.concatenate([scr[cb][...] for cb in range(GW // LANES)], axis=1)


def _view_spec(d):
    return pl.BlockSpec((ROWS // d, d * GW), lambda i: (i, 0))


_VIEW_SCRATCH = [pltpu.VMEM((ROWS, LANES), F32)] * (GW // LANES)


def _qkv_prep(qkv, cos_t, sin_t, gi):
    d = PATTERN_DILATIONS[gi]

    def body(q_ref, k_ref, v_ref, c_ref, s_ref, qo, ko, vo, *scr):
        lo = _lane_lo(ROWS)
        cf, sf = _tile4(c_ref[...]), _tile4(s_ref[...])
        _to_view(scr, _rope(q_ref[...], cf, sf, lo) * (HEAD_DIM ** -0.5), qo, d, BF16)
        _to_view(scr, _rope(k_ref[...], cf, sf, lo), ko, d, BF16)
        _to_view(scr, v_ref[...], vo, d, BF16)

    col = lambda c: pl.BlockSpec((ROWS, GW), lambda i: (i, c))
    tab = pl.BlockSpec((ROWS, BLK), lambda i: (i, 0))
    return pl.pallas_call(
        body,
        name=f"qkv_prep_g{gi}",
        grid=(S // ROWS,),
        in_specs=[col(gi), col(3 + gi), col(6 + gi), tab, tab],
        out_specs=[_view_spec(d)] * 3,
        out_shape=[jax.ShapeDtypeStruct((S // d, d * GW), BF16)] * 3,
        scratch_shapes=_VIEW_SCRATCH,
        compiler_params=_cp(("parallel",)),
    )(qkv, qkv, qkv, cos_t, sin_t)


def _attn_specs(d):
    cur = pl.BlockSpec((BLK, GW), lambda r, n: (n, r))
    prev = pl.BlockSpec((BLK, GW), lambda r, n: (jnp.maximum(n - 1, 0), r))
    return cur, prev


def _left_lanes():
    return lax.broadcasted_iota(jnp.int32, (BLK, LANES), 1) < HEAD_DIM


def _attn_fwd(qr, kr, vv, gi):
    d = PATTERN_DILATIONS[gi]
    sub = S // d
    cur, prev = _attn_specs(d)

    def body(q_ref, kc_ref, kp_ref, vc_ref, vp_ref, o_ref, l_ref):
        n = pl.program_id(1)
        mask = _band_mask(n)
        left = _left_lanes()
        zero = jnp.zeros((BLK, LANES), BF16)
        for pr in range(GW // LANES):
            cs = slice(pr * LANES, (pr + 1) * LANES)
            q2 = q_ref[:, cs]
            k2 = jnp.concatenate([kp_ref[:, cs], kc_ref[:, cs]], axis=0)
            v2 = jnp.concatenate([vp_ref[:, cs], vc_ref[:, cs]], axis=0)
            o_h, lse_h = [], []
            for side in (left, ~left):
                s = jnp.where(mask, _dot_nt(jnp.where(side, q2, zero), k2), NEG)
                m = jnp.max(s, axis=1, keepdims=True)
                p = jnp.exp(s - m)
                l = jnp.sum(p, axis=1, keepdims=True)
                o_h.append(_dot_nn((p / l).astype(BF16), v2))
                lse_h.append(m + jnp.log(l))
            o_ref[:, cs] = jnp.where(left, o_h[0], o_h[1])
            l_ref[:, cs] = jnp.where(left, lse_h[0], lse_h[1])

    return pl.pallas_call(
        body,
        name=f"attn_fwd_g{gi}",
        grid=(d, sub // BLK),
        in_specs=[cur, cur, prev, cur, prev],
        out_specs=[cur, cur],
        out_shape=[jax.ShapeDtypeStruct((sub, d * GW), F32)] * 2,
        compiler_params=_cp(("parallel", "parallel")),
    )(qr, kr, kr, vv, vv)


def _attn_bwd(qr, kr, vv, do, lse, cterm, gi):
    d = PATTERN_DILATIONS[gi]
    sub = S // d
    cur, prev = _attn_specs(d)

    def body(q_ref, kc_ref, kp_ref, vc_ref, vp_ref, do_ref, l_ref, c_ref, dq_ref, dk_ref, dv_ref):
        n = pl.program_id(1)
        mask = _band_mask(n)
        left = _left_lanes()
        zero = jnp.zeros((BLK, LANES), BF16)
        here = pl.ds(pl.multiple_of(n * BLK, BLK), BLK)
        before = pl.ds(pl.multiple_of(jnp.maximum(n - 1, 0) * BLK, BLK), BLK)
        for pr in range(GW // LANES):
            cs = slice(pr * LANES, (pr + 1) * LANES)
            q2 = q_ref[:, cs]
            k2 = jnp.concatenate([kp_ref[:, cs], kc_ref[:, cs]], axis=0)
            v2 = jnp.concatenate([vp_ref[:, cs], vc_ref[:, cs]], axis=0)
            do2 = do_ref[:, cs]
            dq_h, dk2, dv2 = [], None, None
            for h, side in enumerate((left, ~left)):
                at = pr * LANES + h * HEAD_DIM
                qm = jnp.where(side, q2, zero)
                dom = jnp.where(side, do2, zero)
                s = jnp.where(mask, _dot_nt(qm, k2), NEG)
                p = jnp.exp(s - l_ref[:, at:at + 1])
                ds = (p * (_dot_nt(dom, v2) - c_ref[:, at:at + 1])).astype(BF16)
                dq_h.append(_dot_nn(ds, k2))
                dk_part = _dot_tn(ds, qm)
                dv_part = _dot_tn(p.astype(BF16), dom)
                dk2 = dk_part if dk2 is None else dk2 + dk_part
                dv2 = dv_part if dv2 is None else dv2 + dv_part
            dq_ref[:, cs] = jnp.where(left, dq_h[0], dq_h[1]) * (HEAD_DIM ** -0.5)
            dk_ref[here, cs] = dk2[BLK:]
            dv_ref[here, cs] = dv2[BLK:]
            dk_ref[before, cs] += dk2[:BLK]
            dv_ref[before, cs] += dv2[:BLK]

    whole = pl.BlockSpec((sub, GW), lambda r, n: (0, r))
    return pl.pallas_call(
        body,
        name=f"attn_bwd_g{gi}",
        grid=(d, sub // BLK),
        in_specs=[cur, cur, prev, cur, prev, cur, cur, cur],
        out_specs=[cur, whole, whole],
        out_shape=[jax.ShapeDtypeStruct((sub, d * GW), F32)] * 3,
        compiler_params=_cp(("arbitrary", "arbitrary"), 48),
    )(qr, kr, kr, vv, vv, do, lse, cterm)


def _qkv_unprep(dq, dk, dv, cos_t, sin_t, gi, dproj):
    d = PATTERN_DILATIONS[gi]

    def body(dq_ref, dk_ref, dv_ref, c_ref, s_ref, _, out_ref, *scr):
        lo = _lane_lo(ROWS)
        cf, sf = _tile4(c_ref[...]), _tile4(s_ref[...])
        out_ref[:, 0:GW] = _unrope(_from_view(scr, dq_ref, d), cf, sf, lo).astype(BF16)
        out_ref[:, GW:2 * GW] = _unrope(_from_view(scr, dk_ref, d), cf, sf, lo).astype(BF16)
        out_ref[:, 2 * GW:3 * GW] = _from_view(scr, dv_ref, d).astype(BF16)

    tab = pl.BlockSpec((ROWS, BLK), lambda i: (i, 0))
    return pl.pallas_call(
        body,
        name=f"qkv_unprep_g{gi}",
        grid=(S // ROWS,),
        in_specs=[_view_spec(d)] * 3 + [tab, tab, pl.BlockSpec(memory_space=pl.ANY)],
        out_specs=pl.BlockSpec((ROWS, 3 * GW), lambda i: (i, gi)),
        out_shape=jax.ShapeDtypeStruct(dproj.shape, dproj.dtype),
        input_output_aliases={5: 0},
        scratch_shapes=_VIEW_SCRATCH,
        compiler_params=_cp(("parallel",)),
    )(dq, dk, dv, cos_t, sin_t, dproj)


def _group_weights(l0, l1, l2):
    mx = jnp.maximum(jnp.maximum(l0, l1), l2)
    e0, e1, e2 = jnp.exp(l0 - mx), jnp.exp(l1 - mx), jnp.exp(l2 - mx)
    inv = 1.0 / (e0 + e1 + e2)
    return e0 * inv, e1 * inv, e2 * inv


def _combine_fwd(os_, lses):
    def body(o0, o1, o2, l0, l1, l2, y_ref, *scr):
        ov = [_from_view(scr, o, d) for o, d in zip((o0, o1, o2), PATTERN_DILATIONS)]
        lv = [_from_view(scr, l, d) for l, d in zip((l0, l1, l2), PATTERN_DILATIONS)]
        w0, w1, w2 = _group_weights(*lv)
        y_ref[...] = (w0 * ov[0] + w1 * ov[1] + w2 * ov[2]).astype(BF16)

    views = [_view_spec(d) for d in PATTERN_DILATIONS]
    return pl.pallas_call(
        body,
        name="attn_combine_fwd",
        grid=(S // ROWS,),
        in_specs=views + views,
        out_specs=pl.BlockSpec((ROWS, GW), lambda i: (i, 0)),
        out_shape=jax.ShapeDtypeStruct((S, GW), BF16),
        scratch_shapes=_VIEW_SCRATCH,
        compiler_params=_cp(("parallel",)),
    )(*os_, *lses)


def _combine_bwd(dy, os_, lses, seg):
    def body(dy_ref, o0, o1, o2, l0, l1, l2, seg_ref, d0, d1, d2, c0, c1, c2, *scr):
        ov = [_from_view(scr, o, d) for o, d in zip((o0, o1, o2), PATTERN_DILATIONS)]
        lv = [_from_view(scr, l, d) for l, d in zip((l0, l1, l2), PATTERN_DILATIONS)]
        ws = _group_weights(*lv)
        dyv = dy_ref[...]
        t = dyv * (ws[0] * ov[0] + ws[1] * ov[1] + ws[2] * ov[2])
        t_hi = t.astype(BF16)
        r1 = t - t_hi.astype(F32)
        t_mid = r1.astype(BF16)
        t_lo = (r1 - t_mid.astype(F32)).astype(BF16)
        sg = seg_ref[...]
        e = _dot_nn(t_hi, sg) + _dot_nn(t_mid, sg) + _dot_nn(t_lo, sg)
        for w, do_ref, c_ref, d in zip(ws, (d0, d1, d2), (c0, c1, c2), PATTERN_DILATIONS):
            _to_view(scr, w * dyv, do_ref, d, BF16)
            _to_view(scr, w * e, c_ref, d, F32)

    views = [_view_spec(d) for d in PATTERN_DILATIONS]
    return pl.pallas_call(
        body,
        name="attn_combine_bwd",
        grid=(S // ROWS,),
        in_specs=[pl.BlockSpec((ROWS, GW), lambda i: (i, 0))] + views + views + [pl.BlockSpec((GW, GW), lambda i: (0, 0))],
        out_specs=views + views,
        out_shape=[jax.ShapeDtypeStruct((S // d, d * GW), BF16) for d in PATTERN_DILATIONS]
        + [jax.ShapeDtypeStruct((S // d, d * GW), F32) for d in PATTERN_DILATIONS],
        scratch_shapes=_VIEW_SCRATCH,
        compiler_params=_cp(("parallel",)),
    )(dy, *os_, *lses, seg)


_SQRT_HALF = 0.7071067811865476
_INV_SQRT_2PI = 0.3989422804014327


def _gelu(z):
    return 0.5 * z * (1.0 + lax.erf(z * _SQRT_HALF))


def _gelu_grad(z):
    return 0.5 * (1.0 + lax.erf(z * _SQRT_HALF)) + z * (_INV_SQRT_2PI * jnp.exp(-0.5 * z * z))


def _tril_ws(ws_ref, g):
    t = lax.broadcasted_iota(jnp.int32, (BLK, BLK), 0)
    s = lax.broadcasted_iota(jnp.int32, (BLK, BLK), 1)
    return jnp.where(t >= s, ws_ref[g], 0.0)


def _gmlp_fwd(z, ws, bst, ln_g, ln_b, tm=256):
    nch = tm // BLK

    def body(z_ref, ws_ref, b_ref, g_ref, be_ref, y_ref):
        zg = _gelu(z_ref[...])
        u = zg[:, :D]
        vn, _, _ = _ln_fwd(zg[:, D:], g_ref[...], be_ref[...])
        vnb = vn.astype(BF16)
        bt = b_ref[...]
        for g in range(8):
            w = _tril_ws(ws_ref, g).astype(BF16)
            cols = slice(g * BLK, (g + 1) * BLK)
            for c in range(nch):
                rows = slice(c * BLK, (c + 1) * BLK)
                mixed = _dot_nn(w, vnb[rows, cols]) + bt[:, g:g + 1]
                y_ref[rows, cols] = (u[rows, cols] * mixed).astype(BF16)

    return pl.pallas_call(
        body,
        name="gmlp_fwd",
        grid=(S // tm,),
        in_specs=[
            pl.BlockSpec((tm, 2 * D), lambda i: (i, 0)),
            pl.BlockSpec((8, BLK, BLK), lambda i: (0, 0, 0)),
            pl.BlockSpec((BLK, 8), lambda i: (0, 0)),
            pl.BlockSpec((1, D), lambda i: (0, 0)),
            pl.BlockSpec((1, D), lambda i: (0, 0)),
        ],
        out_specs=pl.BlockSpec((tm, D), lambda i: (i, 0)),
        out_shape=jax.ShapeDtypeStruct((S, D), BF16),
        compiler_params=_cp(("parallel",), 48),
    )(z, ws, bst, ln_g, ln_b)


def _gmlp_bwd(z, dy, ws, bst, ln_g, ln_b, dproj, tm=256):
    nch = tm // BLK

    def body(z_ref, dy_ref, ws_ref, b_ref, g_ref, be_ref, _, dz_ref, dws_ref, dbs_ref, dg_ref, dbe_ref, dvn_scr, dm_acc):
        i = pl.program_id(0)
        zv = z_ref[...]
        zg = _gelu(zv)
        u = zg[:, :D]
        gam = g_ref[...]
        vn, xhat, rstd = _ln_fwd(zg[:, D:], gam, be_ref[...])
        vnb = vn.astype(BF16)
        dyv = dy_ref[...]
        dmix = dyv * u
        dmb = dmix.astype(BF16)
        bt = b_ref[...]
        tmask = lax.broadcasted_iota(jnp.int32, (BLK, BLK), 0) >= lax.broadcasted_iota(jnp.int32, (BLK, BLK), 1)
        dm_sum = dmix[0:BLK]
        for c in range(1, nch):
            dm_sum = dm_sum + dmix[c * BLK:(c + 1) * BLK]

        @pl.when(i == 0)
        def _():
            dm_acc[...] = dm_sum

        @pl.when(i > 0)
        def _():
            dm_acc[...] += dm_sum

        dus = []
        for g in range(8):
            w = _tril_ws(ws_ref, g).astype(BF16)
            cols = slice(g * BLK, (g + 1) * BLK)
            dw = None
            du_rows = []
            for c in range(nch):
                rows = slice(c * BLK, (c + 1) * BLK)
                mixed = _dot_nn(w, vnb[rows, cols]) + bt[:, g:g + 1]
                du_rows.append(dyv[rows, cols] * mixed)
                part = _dot_nt(dmb[rows, cols], vnb[rows, cols])
                dw = part if dw is None else dw + part
                dvn_scr[rows, cols] = _dot_tn(w, dmb[rows, cols])
            dus.append(jnp.concatenate(du_rows, axis=0))
            dw = jnp.where(tmask, dw, 0.0)

            @pl.when(i == 0)
            def _():
                dws_ref[g] = dw

            @pl.when(i > 0)
            def _():
                dws_ref[g] += dw

        dvn = dvn_scr[...]
        sg = _colsum(dvn * xhat)
        sb = _colsum(dvn)

        @pl.when(i == 0)
        def _():
            dg_ref[...] = sg
            dbe_ref[...] = sb

        @pl.when(i > 0)
        def _():
            dg_ref[...] += sg
            dbe_ref[...] += sb

        dvg = _ln_bwd(dvn, xhat, rstd, gam)
        gp = _gelu_grad(zv)
        dz_ref[:, :D] = (jnp.concatenate(dus, axis=1) * gp[:, :D]).astype(BF16)
        dz_ref[:, D:] = (dvg * gp[:, D:]).astype(BF16)

        @pl.when(i == S // tm - 1)
        def _():
            acc = dm_acc[...]
            for g in range(8):
                dbs_ref[:, g:g + 1] = jnp.sum(acc[:, g * BLK:(g + 1) * BLK], axis=1, keepdims=True)

    vec = pl.BlockSpec((1, D), lambda i: (0, 0))
    return pl.pallas_call(
        body,
        name="gmlp_bwd",
        grid=(S // tm,),
        in_specs=[
            pl.BlockSpec((tm, 2 * D), lambda i: (i, 0)),
            pl.BlockSpec((tm, D), lambda i: (i, 0)),
            pl.BlockSpec((8, BLK, BLK), lambda i: (0, 0, 0)),
            pl.BlockSpec((BLK, 8), lambda i: (0, 0)),
            vec,
            vec,
            pl.BlockSpec(memory_space=pl.ANY),
        ],
        out_specs=[
            pl.BlockSpec((tm, 2 * D), lambda i: (i, DPROJ_Z_COL)),
            pl.BlockSpec((8, BLK, BLK), lambda i: (0, 0, 0)),
            pl.BlockSpec((BLK, 8), lambda i: (0, 0)),
            vec,
            vec,
        ],
        out_shape=[
            jax.ShapeDtypeStruct(dproj.shape, dproj.dtype),
            jax.ShapeDtypeStruct((8, BLK, BLK), F32),
            jax.ShapeDtypeStruct((BLK, 8), F32),
            jax.ShapeDtypeStruct((1, D), F32),
            jax.ShapeDtypeStruct((1, D), F32),
        ],
        input_output_aliases={6: 0},
        scratch_shapes=[pltpu.VMEM((tm, D), F32), pltpu.VMEM((BLK, D), F32)],
        compiler_params=_cp(("arbitrary",), 48),
    )(z, dy, ws, bst, ln_g, ln_b, dproj)


def _merge_fwd(ya, yg, glog, bgate, h1, wabt, wgb, wo, ln_g, ln_b, tm=256):
    def body(ya_ref, yg_ref, gl_ref, bg_ref, h1_ref, wab_ref, wgb_ref, wo_ref, g_ref, b_ref,
             h_ref, hb_ref, xh_ref, rs_ref, mg_ref, bra_ref, brg_ref):
        bra = _dot_nt(ya_ref[...], wab_ref[...])
        brg = _dot_nn(yg_ref[...], wgb_ref[...])
        gates = _sigmoid(gl_ref[...] + bg_ref[...])
        merged = (gates[:, :D] * bra + gates[:, D:] * brg).astype(BF16)
        mix = _dot_nn(merged, wo_ref[...])
        h, xhat, rstd = _ln_fwd(ALPHA * h1_ref[...] + mix, g_ref[...], b_ref[...])
        h_ref[...] = h
        hb_ref[...] = h.astype(BF16)
        xh_ref[...] = xhat
        rs_ref[...] = rstd
        mg_ref[...] = merged
        bra_ref[...] = bra
        brg_ref[...] = brg

    row = pl.BlockSpec((tm, D), lambda i: (i, 0))
    vec = pl.BlockSpec((1, D), lambda i: (0, 0))
    full = lambda shape: pl.BlockSpec(shape, lambda i: (0, 0))
    return pl.pallas_call(
        body,
        name="merge_fwd",
        grid=(S // tm,),
        in_specs=[
            pl.BlockSpec((tm, GW), lambda i: (i, 0)), row,
            pl.BlockSpec((tm, 2 * D), lambda i: (i, glog.shape[1] // (2 * D) - 1)),
            full((1, 2 * D)), row,
            full((D, GW)), full((D, D)), full((D, D)), vec, vec,
        ],
        out_specs=[row, row, row, pl.BlockSpec((tm, 1), lambda i: (i, 0)), row, row, row],
        out_shape=[
            jax.ShapeDtypeStruct((S, D), F32),
            jax.ShapeDtypeStruct((S, D), BF16),
            jax.ShapeDtypeStruct((S, D), F32),
            jax.ShapeDtypeStruct((S, 1), F32),
            jax.ShapeDtypeStruct((S, D), BF16),
            jax.ShapeDtypeStruct((S, D), F32),
            jax.ShapeDtypeStruct((S, D), F32),
        ],
        compiler_params=_cp(("parallel",), 48),
    )(ya, yg, glog, bgate, h1, wabt, wgb, wo, ln_g, ln_b)


def _merge_bwd(dh2, xhat, rstd, ln_g, bra, brg, glog, bgate, wabt, wgb, wo, tm=256):
    def body(dh_ref, xh_ref, rs_ref, g_ref, bra_ref, brg_ref, gl_ref, bg_ref, wab_ref, wgb_ref, wo_ref,
             dr_ref, drb_ref, dlog_ref, dba_ref, dbg_ref, dya_ref, dyg_ref, dbgate_ref, dg_ref, dbias_ref):
        i = pl.program_id(0)
        dh = dh_ref[...]
        xh = xh_ref[...]
        dr = _ln_bwd(dh, xh, rs_ref[...], g_ref[...])
        drb = dr.astype(BF16)
        dr_ref[...] = dr
        drb_ref[...] = drb
        dmerged = _dot_nt(drb, wo_ref[...])
        gates = _sigmoid(gl_ref[...] + bg_ref[...])
        g0, g1 = gates[:, :D], gates[:, D:]
        dl0 = dmerged * bra_ref[...] * g0 * (1.0 - g0)
        dl1 = dmerged * brg_ref[...] * g1 * (1.0 - g1)
        dlog_ref[:, :D] = dl0.astype(BF16)
        dlog_ref[:, D:] = dl1.astype(BF16)
        dba = (dmerged * g0).astype(BF16)
        dbg = (dmerged * g1).astype(BF16)
        dba_ref[...] = dba
        dbg_ref[...] = dbg
        dya_ref[...] = _dot_nn(dba, wab_ref[...])
        dyg_ref[...] = _dot_nt(dbg, wgb_ref[...])
        s0, s1 = _colsum(dl0), _colsum(dl1)
        sg, sb = _colsum(dh * xh), _colsum(dh)

        @pl.when(i == 0)
        def _():
            dbgate_ref[:, :D] = s0
            dbgate_ref[:, D:] = s1
            dg_ref[...] = sg
            dbias_ref[...] = sb

        @pl.when(i > 0)
        def _():
            dbgate_ref[:, :D] += s0
            dbgate_ref[:, D:] += s1
            dg_ref[...] += sg
            dbias_ref[...] += sb

    row = pl.BlockSpec((tm, D), lambda i: (i, 0))
    vec = pl.BlockSpec((1, D), lambda i: (0, 0))
    wide = pl.BlockSpec((tm, 2 * D), lambda i: (i, 0))
    full = lambda shape: pl.BlockSpec(shape, lambda i: (0, 0))
    return pl.pallas_call(
        body,
        name="merge_bwd",
        grid=(S // tm,),
        in_specs=[row, row, pl.BlockSpec((tm, 1), lambda i: (i, 0)), vec, row, row,
                  pl.BlockSpec((tm, 2 * D), lambda i: (i, glog.shape[1] // (2 * D) - 1)),
                  full((1, 2 * D)), full((D, GW)), full((D, D)), full((D, D))],
        out_specs=[row, row, pl.BlockSpec((tm, 2 * D), lambda i: (i, DPROJ_G_COL)), row, row,
                   pl.BlockSpec((tm, GW), lambda i: (i, 0)), row, full((1, 2 * D)), vec, vec],
        out_shape=[
            jax.ShapeDtypeStruct((S, D), F32),
            jax.ShapeDtypeStruct((S, D), BF16),
            jax.ShapeDtypeStruct((S, DPROJ_W), BF16),
            jax.ShapeDtypeStruct((S, D), BF16),
            jax.ShapeDtypeStruct((S, D), BF16),
            jax.ShapeDtypeStruct((S, GW), F32),
            jax.ShapeDtypeStruct((S, D), F32),
            jax.ShapeDtypeStruct((1, 2 * D), F32),
            jax.ShapeDtypeStruct((1, D), F32),
            jax.ShapeDtypeStruct((1, D), F32),
        ],
        compiler_params=_cp(("arbitrary",), 48),
    )(dh2, xhat, rstd, ln_g, bra, brg, glog, bgate, wabt, wgb, wo)


def _loss_head(h3, target, tm=512):
    def body(h_ref, t_ref, d_ref, l_ref):
        i = pl.program_id(0)
        e = h_ref[...] - t_ref[...]
        d_ref[...] = e * (1.0 / D)
        part = jnp.sum(_colsum(e * e), axis=1, keepdims=True) * (0.5 / D)

        @pl.when(i == 0)
        def _():
            l_ref[...] = part

        @pl.when(i > 0)
        def _():
            l_ref[...] += part

    row = pl.BlockSpec((tm, D), lambda i: (i, 0))
    return pl.pallas_call(
        body,
        name="loss_head",
        grid=(S // tm,),
        in_specs=[row, row],
        out_specs=[row, pl.BlockSpec((1, 1), lambda i: (0, 0))],
        out_shape=[jax.ShapeDtypeStruct((S, D), F32), jax.ShapeDtypeStruct((1, 1), F32)],
        compiler_params=_cp(("arbitrary",)),
    )(h3, target)


def _tie(x, dep):
    if dep is None:
        return x
    return x + dep[0, 0].astype(x.dtype)


def _local_step(x, pos_col, target, get_w, p, emit):
    w = get_w("ffn1", None)
    a1, b1, hm1 = _ffn_up(x, w["g1"], w["u1"], "ffn1_up")
    w.update(get_w("ffn1d", hm1))
    h1, h1b, xh1, rs1 = _ffn_down(hm1, w["d1"], x, p["ln1_g"], p["ln1_b"], "ffn1_down")

    w.update(get_w("win", h1b))
    qkv = _matmul(h1b, w["win"], "nt", F32, 1024, 1536, D, "proj_qkv", b_off=0, n_out=QKV_W, dep=w.get("_dep"))
    z = glog = _matmul(h1b, w["win"], "nt", F32, 1024, 512, D, "proj_zg", b_off=QKV_W // 512, n_out=4 * D)

    half = jnp.arange(0, HEAD_DIM, 2, dtype=F32) / HEAD_DIM
    inv_freq = ROPE_THETA ** (-half)
    invf = jnp.tile(inv_freq, 4).reshape(1, BLK)
    sign = jnp.tile(jnp.concatenate([-jnp.ones((32,), F32), jnp.ones((32,), F32)]), 2).reshape(1, BLK)
    cos_t, sin_t = _rope_tables(pos_col, invf, sign)

    get_w("mix", qkv, early=True)
    preps = [_qkv_prep(qkv, cos_t, sin_t, gi) for gi in range(3)]
    os_, lses = [], []
    for gi in range(3):
        o, lse = _attn_fwd(*preps[gi], gi)
        os_.append(o)
        lses.append(lse)
    ya = _combine_fwd(os_, lses)
    bst = p["gmlp_b_s"].T
    yg = _gmlp_fwd(z, p["gmlp_w_s"], bst, p["gmlp_ln_g"], p["gmlp_ln_b"])
    w.update(get_w("mix", yg))
    get_w("ffn2", yg, early=True)
    h2, h2b, xh2, rs2, merged, bra, brg = _merge_fwd(ya, yg, glog, p["b_gates"], h1, w["ab"], w["gb"], w["o"],
                                                      p["ln2_g"], p["ln2_b"])
    w.update(get_w("ffn2", h2b))
    a2, b2, hm2 = _ffn_up(h2b, w["g2"], w["u2"], "ffn2_up")
    h3, _, xh3, rs3 = _ffn_down(hm2, w["d2"], h2, p["ln3_g"], p["ln3_b"], "ffn2_down")
    dh3, loss = _loss_head(h3, target)

    gp = {}
    dr3, df2, da2, db2, gp["ln3_g"], gp["ln3_b"] = _ffn_bwd_mid(dh3, None, xh3, rs3, p["ln3_g"], a2, b2, w["d2"],
                                                                "ffn2_bwd_mid")
    tok = emit("ffn2", {
        "g2": _matmul(da2, h2b, "tn", BF16, 1408, D, S, "wgrad_g2"),
        "u2": _matmul(db2, h2b, "tn", BF16, 1408, D, S, "wgrad_u2"),
        "d2": _matmul(hm2, df2, "tn", BF16, 1408, D, S, "wgrad_d2")})
    dh2 = _ffn_bwd_dx(dr3, da2, db2, w["g2"], w["u2"], "ffn2_bwd_dx")

    (dr2, dr2b, dproj, dba, dbg, dya, dyg, gp["b_gates"], gp["ln2_g"], gp["ln2_b"]) = _merge_bwd(
        dh2, xh2, rs2, _tie(p["ln2_g"], tok), bra, brg, glog, p["b_gates"], w["ab"], w["gb"], w["o"])
    tok = emit("mix", {
        "o": _matmul(merged, dr2b, "tn", BF16, 512, D, S, "wgrad_o"),
        "ab": _matmul(dba, ya, "tn", BF16, 512, GW, S, "wgrad_ab"),
        "gb": _matmul(yg, dbg, "tn", BF16, 512, D, S, "wgrad_gb")})

    seg = (jnp.arange(GW)[:, None] // HEAD_DIM == jnp.arange(GW)[None, :] // HEAD_DIM).astype(BF16)
    do0, do1, do2, c0, c1, c2 = _combine_bwd(dya, os_, lses, _tie(seg, tok))
    for gi, (do, ct) in enumerate(((do0, c0), (do1, c1), (do2, c2))):
        dq, dk, dv = _attn_bwd(*preps[gi], do, lses[gi], ct, gi)
        dproj = _qkv_unprep(dq, dk, dv, cos_t, sin_t, gi, dproj)
    dproj, gp["gmlp_w_s"], dbst, gp["gmlp_ln_g"], gp["gmlp_ln_b"] = _gmlp_bwd(
        z, dyg, p["gmlp_w_s"], bst, p["gmlp_ln_g"], p["gmlp_ln_b"], dproj)
    gp["gmlp_b_s"] = dbst.T
    tok = emit("win", {"win": _matmul(dproj, h1b, "tn", BF16, 512, D, S, "wgrad_win", a_map=_dproj_tile, m_out=IN_W)})
    dh1m = _matmul(dproj, w["win"], "nn", F32, 1024, D, 512, "dproj_to_dh1", dep=tok, a_map=_dproj_tile)

    dr1, df1, da1, db1, gp["ln1_g"], gp["ln1_b"] = _ffn_bwd_mid(dr2, dh1m, xh1, rs1, p["ln1_g"], a1, b1, w["d1"],
                                                                "ffn1_bwd_mid")
    tok = emit("small", {**gp, "loss": loss})
    tok = emit("g1", {"g1": _matmul(da1, x, "tn", BF16, 1408, D, S, "wgrad_g1", dep=tok)})
    tok = emit("u1", {"u1": _matmul(db1, x, "tn", BF16, 1408, D, S, "wgrad_u1", dep=tok)})
    tok = emit("d1", {"d1": _matmul(hm1, df1, "tn", BF16, 1408, D, S, "wgrad_d1", dep=tok)})
    dx = _ffn_bwd_dx(dr1, da1, db1, w["g1"], w["u1"], "ffn1_bwd_dx", dep=tok)
    return loss, dx, gp


_FLIPS = [(mx, my, mc) for mx in (0, 1) for my in (0, 1) for mc in (0, 1)][1:]
_GROUPS = {"ffn1": ("g1", "u1"), "ffn1d": ("d1",), "win": ("win",), "mix": ("ab", "gb", "o"), "ffn2": ("g2", "u2", "d2")}
_SCATTERS = {"ffn2": ("g2", "u2", "d2"), "mix": ("ab", "gb", "o"), "win": ("win",), "g1": ("g1",), "u1": ("u1",), "d1": ("d1",)}
_TWO_STAGE = ("g1", "u1", "d1")
_HBM = pl.BlockSpec(memory_space=pltpu.HBM)
_SEM = pl.BlockSpec(memory_space=pltpu.SEMAPHORE)
_EFFECT = pltpu.SideEffectType.DATAFLOW_SIDE_EFFECTING


def _me():
    return 4 * lax.axis_index("x") + 2 * lax.axis_index("y") + lax.axis_index("c")


def _copies_start(bufs, copies, n_sem, name, after=None):
    nb = len(bufs)
    n_after = 0 if after is None else 1

    def body(*refs):
        b = refs[:nb]
        send_sems, recv_sems = refs[nb + n_after], refs[nb + n_after + 1]
        token = refs[-1]
        x, y, c = lax.axis_index("x"), lax.axis_index("y"), lax.axis_index("c")
        me = 4 * x + 2 * y + c
        for si, s_slot, di, d_slot, (mx, my, mc), sem in copies:
            s_idx, d_idx = s_slot(me), d_slot(me)
            pltpu.make_async_remote_copy(
                src_ref=b[si] if s_idx is None else b[si].at[s_idx],
                dst_ref=b[di] if d_idx is None else b[di].at[d_idx],
                send_sem=send_sems.at[sem], recv_sem=recv_sems.at[sem],
                device_id=(x ^ mx, y ^ my, c ^ mc), device_id_type=MESH).start()
        token[...] = jnp.zeros_like(token)

    ins = [pltpu.with_memory_space_constraint(a, pltpu.HBM) for a in bufs]
    outs = pl.pallas_call(
        body,
        name=name,
        in_specs=[_HBM] * nb + [pl.BlockSpec(memory_space=pl.ANY)] * n_after,
        out_specs=[_SEM, _SEM] + [_HBM] * nb + [pl.BlockSpec(memory_space=pltpu.VMEM)],
        out_shape=[pltpu.SemaphoreType.DMA((n_sem,)), pltpu.SemaphoreType.DMA((n_sem,))]
        + [pltpu.HBM(a.shape, a.dtype) for a in ins] + [jax.ShapeDtypeStruct((8, 128), F32)],
        input_output_aliases={k: 2 + k for k in range(nb)},
        compiler_params=pltpu.CompilerParams(has_side_effects=_EFFECT),
    )(*ins, *([] if after is None else [after]))
    return outs[0], outs[1], list(outs[2:2 + nb]), outs[-1]


def _copies_wait(started, waits, after, name):
    send_sems, recv_sems, bufs, _ = started
    nb = len(bufs)

    def body(*refs):
        b = refs[:nb]
        ss, rs = refs[nb], refs[nb + 1]
        me3 = (lax.axis_index("x"), lax.axis_index("y"), lax.axis_index("c"))
        for bi, n_blocks, sem, is_send in waits:
            blocks = b[bi].at[pl.ds(0, n_blocks)]
            cp = pltpu.make_async_remote_copy(src_ref=blocks, dst_ref=blocks, send_sem=ss.at[sem], recv_sem=rs.at[sem],
                                              device_id=me3, device_id_type=MESH)
            if is_send:
                cp.wait_send()
            else:
                cp.wait_recv()

    return pl.pallas_call(
        body,
        name=name,
        in_specs=[_HBM] * nb + [_SEM, _SEM, pl.BlockSpec(memory_space=pl.ANY)],
        out_specs=[_HBM] * nb,
        out_shape=[pltpu.HBM(a.shape, a.dtype) for a in bufs],
        input_output_aliases={k: k for k in range(nb)},
        compiler_params=pltpu.CompilerParams(has_side_effects=_EFFECT),
    )(*bufs, send_sems, recv_sems, after)


def _landing(own, me):
    return lax.dynamic_update_slice(lax.empty((N_DEV,) + own.shape[1:], own.dtype), own, (me, 0, 0))


_SIBLING = (0, 0, 1)
_OTHER_CHIPS = ((1, 0, 0), (0, 1, 0), (1, 1, 0))


def _bits(flip):
    return 4 * flip[0] + 2 * flip[1] + flip[2]


def _gather_send(shards, grp, after):
    nw = len(shards)
    me = _me()
    bufs = list(shards) + [_landing(a[None], me) for a in shards]
    copies = []
    for k in range(nw):
        copies.append((k, lambda me: None, nw + k, lambda me: me, _SIBLING, nw + k))
        copies += [(k, lambda me: None, nw + k, lambda me: me, f, k) for f in _OTHER_CHIPS]
    return _copies_start(bufs, copies, 2 * nw, "gather_send_" + grp, after)


def _gather_pass(started, grp, after):
    nw = len(started[2]) // 2
    waits = []
    for k in range(nw):
        waits += [(nw + k, 3, k, True), (nw + k, 1, nw + k, True), (nw + k, 3, k, False), (nw + k, 1, nw + k, False)]
    lands = list(_copies_wait(started, waits, after, "gather_arrived_" + grp)[nw:])
    copies = []
    for k in range(nw):
        for f in _OTHER_CHIPS:
            slot = functools.partial(lambda me, bits: me ^ bits, bits=_bits(f))
            copies.append((k, slot, k, slot, _SIBLING, k))
    return _copies_start(lands, copies, nw, "gather_pass_" + grp)


def _gather_finish(passed, grp, after):
    nw = len(passed[2])
    waits = [(k, 3, k, is_send) for k in range(nw) for is_send in (True, False)]
    return _copies_wait(passed, waits, passed[3] if after is None else after, "gather_done_" + grp)


def _scatter_send(parts, grp, after=None):
    nw = len(parts)
    me = _me()
    bufs = list(parts) + [_landing(lax.dynamic_index_in_dim(a, me, 0, keepdims=True), me) for a in parts]
    copies = []
    for k in range(nw):
        for f in _FLIPS:
            to = functools.partial(lambda me, bits: me ^ bits, bits=_bits(f))
            copies.append((k, to, nw + k, lambda me: me, f, k))
    return _copies_start(bufs, copies, nw, "scatter_send_" + grp, after)


def _scatter_finish(started, grp, after):
    nw = len(started[2]) // 2
    waits = [(nw + k, N_DEV - 1, k, is_send) for k in range(nw) for is_send in (True, False)]
    return _copies_wait(started, waits, after, "scatter_done_" + grp)[nw:]


N_CHIP = N_DEV // 2


def _pair_sum(part, other, key):
    _, r, c = part.shape
    tm = _ADAM_ROWS.get(r, r)
    core = lax.axis_index("c").astype(jnp.int32).reshape(1)

    def body(core_ref, a_ref, b_ref, o_ref):
        o_ref[...] = (a_ref[...].astype(F32) + b_ref[...].astype(F32)).astype(BF16)

    return pl.pallas_call(
        body,
        name="pair_sum_" + key,
        grid_spec=pltpu.PrefetchScalarGridSpec(
            num_scalar_prefetch=1,
            grid=(N_CHIP, r // tm),
            in_specs=[pl.BlockSpec((None, tm, c), lambda q, i, core_ref: (2 * q + core_ref[0], i, 0)),
                      pl.BlockSpec((None, tm, c), lambda q, i, core_ref: (q, i, 0))],
            out_specs=pl.BlockSpec((None, tm, c), lambda q, i, core_ref: (q, i, 0)),
        ),
        out_shape=jax.ShapeDtypeStruct((N_CHIP, r, c), BF16),
        compiler_params=_cp(("parallel", "parallel")),
    )(core, part, other)


def _scatter2_send(part, key, dep=None):
    me = _me()
    _, r, c = part.shape
    swap = []
    for q in range(N_CHIP):
        src = functools.partial(lambda me, q: 2 * q + 1 - me % 2, q=q)
        swap.append((0, src, 1, functools.partial(lambda me, q: q, q=q), _SIBLING, 0))
    started = _copies_start([part, lax.empty((N_CHIP, r, c), part.dtype)], swap, 1, "scatter_swap_" + key, dep)
    waits = [(1, N_CHIP, 0, True), (1, N_CHIP, 0, False)]
    part, other = _copies_wait(started, waits, started[3], "scatter_swapped_" + key)
    pair = _pair_sum(part, other, key)
    chip = me // 2
    land = lax.dynamic_update_slice(lax.empty((N_CHIP, r, c), pair.dtype),
                                    lax.dynamic_index_in_dim(pair, chip, 0, keepdims=True), (chip, 0, 0))
    copies = []
    for f in _OTHER_CHIPS:
        to = functools.partial(lambda me, bits: (me ^ bits) // 2, bits=_bits(f))
        copies.append((0, to, 1, lambda me: me // 2, f, 0))
    return _copies_start([pair, land], copies, 1, "scatter_send_" + key)


def _scatter2_finish(started, key, after):
    waits = [(1, N_CHIP - 1, 0, True), (1, N_CHIP - 1, 0, False)]
    return _copies_wait(started, waits, after, "scatter_done_" + key)[1]


def _allgather_send(block, name, after=None):
    me = _me()
    copies = [(0, lambda me: None, 1, lambda me: me, f, 0) for f in _FLIPS]
    return _copies_start([block, _landing(block[None], me)], copies, 1, name, after)


def _allgather_finish(started, name, after):
    waits = [(1, N_DEV - 1, 0, True), (1, N_DEV - 1, 0, False)]
    return _copies_wait(started, waits, after, name)[1]


_ADAM_ROWS = {352: 176, 1088: 272}


def _sum_adamw(parts, wv, m, v, name):
    n_parts, r, c = parts.shape
    tm = _ADAM_ROWS.get(r, r)
    assert r % tm == 0 and wv.shape == (r, c)

    def body(p_ref, w_ref, m_ref, v_ref, g_ref, d_ref, mo_ref, vo_ref):
        gv = p_ref[0].astype(F32)
        for j in range(1, n_parts):
            gv = gv + p_ref[j].astype(F32)
        g_ref[...] = gv
        mn = ADAM_B1 * m_ref[...] + (1.0 - ADAM_B1) * gv
        vn = ADAM_B2 * v_ref[...] + (1.0 - ADAM_B2) * (gv * gv)
        m_hat = mn / (1.0 - ADAM_B1 ** ADAM_STEP)
        v_hat = vn / (1.0 - ADAM_B2 ** ADAM_STEP)
        d_ref[...] = -ADAM_LR * (m_hat / (jnp.sqrt(v_hat) + ADAM_EPS) + ADAM_WD * w_ref[...])
        mo_ref[...] = mn
        vo_ref[...] = vn

    sp = pl.BlockSpec((tm, c), lambda i: (i, 0))
    return pl.pallas_call(
        body,
        name=name,
        grid=(r // tm,),
        in_specs=[pl.BlockSpec((n_parts, tm, c), lambda i: (0, i, 0))] + [sp] * 3,
        out_specs=[sp] * 4,
        out_shape=[jax.ShapeDtypeStruct((r, c), F32)] * 4,
        compiler_params=_cp(("parallel",), 48),
    )(parts, wv, m, v)


_WEIGHTS = ["ffn1_w_gate", "ffn1_w_up", "ffn1_w_down", "ln1_g", "ln1_b", "w_in", "b_gates", "gmlp_ln_g", "gmlp_ln_b",
            "gmlp_w_s", "gmlp_b_s", "w_attn_branch", "w_gmlp_branch", "w_out", "ln2_g", "ln2_b", "ffn2_w_gate",
            "ffn2_w_up", "ffn2_w_down", "ln3_g", "ln3_b"]
_BIG_OF = {"ffn1_w_gate": ("g1", True), "ffn1_w_up": ("u1", True), "ffn1_w_down": ("d1", False), "w_in": ("win", True),
           "w_attn_branch": ("ab", True), "w_gmlp_branch": ("gb", False), "w_out": ("o", False),
           "ffn2_w_gate": ("g2", True), "ffn2_w_up": ("u2", True), "ffn2_w_down": ("d2", False)}
_SMALL = [n for n in _WEIGHTS if n not in _BIG_OF]
_SMALL_ROWS = {"gmlp_w_s": 128, "b_gates": 2}
_SMALL_SLOT = 8


def _pack_small(d, last=None):
    rows = []
    for n in _SMALL:
        r = d[n].reshape(-1, D)
        slot = max(r.shape[0], _SMALL_SLOT)
        rows.append(jnp.pad(r, ((0, slot - r.shape[0]), (0, 0))))
    rows.append(jnp.zeros((_SMALL_SLOT, D), F32) if last is None else jnp.broadcast_to(last.reshape(1, 1), (_SMALL_SLOT, D)))
    return jnp.concatenate(rows, axis=0)


def _unpack_small(packed, shapes):
    out, at = {}, 0
    for n in _SMALL:
        k = _SMALL_ROWS.get(n, 1)
        out[n] = packed[at:at + k].reshape(shapes[n])
        at += max(k, _SMALL_SLOT)
    return out


def kernel(x, positions, ffn1_w_gate, ffn1_w_up, ffn1_w_down, ln1_g, ln1_b, w_in, b_gates, gmlp_ln_g, gmlp_ln_b, gmlp_w_s, gmlp_b_s, w_attn_branch, w_gmlp_branch, w_out, ln2_g, ln2_b, ffn2_w_gate, ffn2_w_up, ffn2_w_down, ln3_g, ln3_b, loss_target, m_ffn1_w_gate, m_ffn1_w_up, m_ffn1_w_down, m_ln1_g, m_ln1_b, m_w_in, m_b_gates, m_gmlp_ln_g, m_gmlp_ln_b, m_gmlp_w_s, m_gmlp_b_s, m_w_attn_branch, m_w_gmlp_branch, m_w_out, m_ln2_g, m_ln2_b, m_ffn2_w_gate, m_ffn2_w_up, m_ffn2_w_down, m_ln3_g, m_ln3_b, v_ffn1_w_gate, v_ffn1_w_up, v_ffn1_w_down, v_ln1_g, v_ln1_b, v_w_in, v_b_gates, v_gmlp_ln_g, v_gmlp_ln_b, v_gmlp_w_s, v_gmlp_b_s, v_w_attn_branch, v_w_gmlp_branch, v_w_out, v_ln2_g, v_ln2_b, v_ffn2_w_gate, v_ffn2_w_up, v_ffn2_w_down, v_ln3_g, v_ln3_b):
    args = dict(locals())
    wts = {n: args[n] for n in _WEIGHTS}
    ms = {n: args["m_" + n] for n in _WEIGHTS}
    vs = {n: args["v_" + n] for n in _WEIGHTS}

    name_of = {key: (n, tr) for n, (key, tr) in _BIG_OF.items()}

    shards = {}
    for grp, keys in _GROUPS.items():
        shards[grp] = []
        for key in keys:
            n, tr = name_of[key]
            s2 = wts[n][0]
            shards[grp].append((s2.T if tr else s2).astype(BF16))
    started, tok = {}, None
    for grp in _GROUPS:
        started[grp] = _gather_send(shards[grp], grp, tok)
        tok = started[grp][3]
    all_started = tok
    passed = {}

    def get_w(grp, after, early=False):
        if grp not in passed:
            passed[grp] = _gather_pass(started[grp], grp, all_started if after is None else after)
            after = None
        if early:
            return None
        lands = _gather_finish(passed[grp], grp, after)
        return {key: g.reshape(-1, g.shape[-1]) for key, g in zip(_GROUPS[grp], lands)}

    sent = {}

    def emit(grp, grads):
        if grp == "small":
            sent[grp] = _allgather_send(_pack_small(grads, last=grads["loss"]), "small_grads_send")
        else:
            parts = [grads[key].reshape(N_DEV, -1, grads[key].shape[-1]) for key in _SCATTERS[grp]]
            sent[grp] = _scatter2_send(parts[0], grp) if grp in _TWO_STAGE else _scatter_send(parts, grp)
        return sent[grp][3]

    p = {n: (wts[n][0] if n in ("gmlp_w_s", "gmlp_b_s") else wts[n]) for n in _SMALL}
    loss, dx, gp = _local_step(x[0], positions.reshape(S, 1), loss_target[0], get_w, p, emit)
    grads, deltas, new_m, new_v = {}, {}, {}, {}
    after = dx
    for grp in ("ffn2", "mix", "win", "small", "g1", "u1", "d1"):
        if grp == "small":
            parts = _allgather_finish(sent[grp], "small_grads_done", after)
            outs = _sum_adamw(parts, *[_pack_small({n: d[n] for n in _SMALL}) for d in (wts, ms, vs)], "update_small")
            shapes = {n: wts[n].shape for n in _SMALL}
            for dst, packed in zip((grads, deltas, new_m, new_v), outs):
                dst.update(_unpack_small(packed, shapes))
            loss = outs[0][-_SMALL_SLOT, 0]
            after = outs[1]
            continue
        arrived = ([_scatter2_finish(sent[grp], grp, after)] if grp in _TWO_STAGE
                   else _scatter_finish(sent[grp], grp, after))
        for key, part in zip(_SCATTERS[grp], arrived):
            n, tr = name_of[key]
            outs = _sum_adamw(part, *[(d[n][0].T if tr else d[n][0]) for d in (wts, ms, vs)], "update_" + key)
            for dst, o in zip((grads, deltas, new_m, new_v), outs):
                dst[n] = (o.T if tr else o)[None]
            after = outs[1]

    return (loss, dx[None], *[grads[n] for n in _WEIGHTS], *[deltas[n] for n in _WEIGHTS],
            *[new_m[n] for n in _WEIGHTS], *[new_v[n] for n in _WEIGHTS])
```

```python
import functools
import math

import jax
import jax.numpy as jnp
from jax import lax
from jax.experimental import pallas as pl
from jax.experimental.pallas import tpu as pltpu

F32 = jnp.float32
BF16 = jnp.bfloat16

N_DEV = 8
D = 1024
S = 2048
F = 2816
HEAD_DIM = 64
HEADS = 8
GW = HEADS * HEAD_DIM
PATTERN_DILATIONS = (1, 4, 16)
BLK = 128
QKV_W = 3 * 3 * GW
IN_W = QKV_W + 2 * D + 2 * D
DPROJ_W = 5 * 2 * D
DPROJ_Z_COL, DPROJ_G_COL = 3, 4


def _dproj_tile(t):
    three = jnp.int32(3)
    return jnp.where(t < 9, lax.rem(t, three) * 3 + lax.div(t, three), t + 3)
ROPE_THETA = 10000.0
ALPHA = 2.0 ** 0.25
LN_EPS = 1e-5
ADAM_LR, ADAM_B1, ADAM_B2, ADAM_EPS, ADAM_WD, ADAM_STEP = 0.001, 0.9, 0.999, 1e-08, 0.01, 10
NEG = -1e30
MESH = pl.DeviceIdType.MESH


def _cp(sem=None, vmem_mb=None):
    kw = {}
    if sem is not None:
        kw["dimension_semantics"] = sem
    if vmem_mb is not None:
        kw["vmem_limit_bytes"] = vmem_mb << 20
    return pltpu.CompilerParams(**kw)


def _dot_nn(a, b):
    return lax.dot_general(a, b, (((1,), (0,)), ((), ())), preferred_element_type=F32)


def _dot_nt(a, b):
    return lax.dot_general(a, b, (((1,), (1,)), ((), ())), preferred_element_type=F32)


def _dot_tn(a, b):
    return lax.dot_general(a, b, (((0,), (0,)), ((), ())), preferred_element_type=F32)


def _ln_fwd(r, g, b):
    mu = jnp.mean(r, axis=-1, keepdims=True)
    xc = r - mu
    var = jnp.mean(xc * xc, axis=-1, keepdims=True)
    rstd = lax.rsqrt(var + LN_EPS)
    xhat = xc * rstd
    return xhat * g + b, xhat, rstd


def _ln_bwd(dh, xhat, rstd, g):
    dxh = dh * g
    m1 = jnp.mean(dxh, axis=-1, keepdims=True)
    m2 = jnp.mean(dxh * xhat, axis=-1, keepdims=True)
    return rstd * (dxh - m1 - xhat * m2)


def _sigmoid(x):
    return 0.5 * jnp.tanh(0.5 * x) + 0.5


def _colsum(x):
    return jnp.sum(x, axis=0, keepdims=True)


def _matmul(a, b, mode, out_dtype, tm, tn, tk, name, b_off=0, n_out=None, dep=None, a_map=None, m_out=None):
    n_dep = 0 if dep is None else 1
    a_map = a_map or (lambda t: t)
    if mode == "nn":
        m, k = a.shape[0], b.shape[0]
        n = b.shape[1]
    elif mode == "nt":
        m, k = a.shape
        n = n_out if n_out is not None else b.shape[0]
    else:
        k, m = a.shape[0], m_out or a.shape[1]
        n = b.shape[1]
    nk = k // tk
    assert m % tm == 0 and n % tn == 0 and k % tk == 0
    dot = {"nn": _dot_nn, "nt": _dot_nt, "tn": _dot_tn}[mode]

    def body(a_ref, b_ref, *rest):
        o_ref, scr = rest[n_dep], rest[n_dep + 1:]
        r = dot(a_ref[...].astype(BF16), b_ref[...].astype(BF16))
        if nk == 1:
            o_ref[...] = r.astype(out_dtype)
        else:
            acc = scr[0]
            kk = pl.program_id(2)

            @pl.when(kk == 0)
            def _():
                acc[...] = r

            @pl.when(kk > 0)
            def _():
                acc[...] += r

            @pl.when(kk == nk - 1)
            def _():
                o_ref[...] = acc[...].astype(out_dtype)

    if mode == "nn":
        a_spec = pl.BlockSpec((tm, tk), lambda i, j, kk: (i, a_map(kk)))
        b_spec = pl.BlockSpec((tk, tn), lambda i, j, kk: (kk, j))
    elif mode == "nt":
        a_spec = pl.BlockSpec((tm, tk), lambda i, j, kk: (i, kk))
        b_spec = pl.BlockSpec((tn, tk), lambda i, j, kk: (j + b_off, kk))
    else:
        a_spec = pl.BlockSpec((tk, tm), lambda i, j, kk: (kk, a_map(i)))
        b_spec = pl.BlockSpec((tk, tn), lambda i, j, kk: (kk, j))
    return pl.pallas_call(
        body,
        name=name,
        grid=(m // tm, n // tn, nk),
        in_specs=[a_spec, b_spec] + [pl.BlockSpec(memory_space=pl.ANY)] * n_dep,
        out_specs=pl.BlockSpec((tm, tn), lambda i, j, kk: (i, j)),
        out_shape=jax.ShapeDtypeStruct((m, n), out_dtype),
        scratch_shapes=[] if nk == 1 else [pltpu.VMEM((tm, tn), F32)],
        compiler_params=_cp(("parallel", "parallel", "arbitrary"), 56),
    )(a, b, *([] if dep is None else [dep]))


def _ffn_up(x, wgt, wut, name, tm=512, tn=1408, dep=None):
    n_dep = 0 if dep is None else 1

    def body(x_ref, wg_ref, wu_ref, *rest):
        a_ref, b_ref, hm_ref = rest[n_dep:]
        xb = x_ref[...].astype(BF16)
        a = _dot_nt(xb, wg_ref[...])
        b = _dot_nt(xb, wu_ref[...])
        a_ref[...] = a.astype(BF16)
        b_ref[...] = b.astype(BF16)
        hm_ref[...] = ((a * _sigmoid(a)) * b).astype(BF16)

    wsp = pl.BlockSpec((tn, D), lambda i, j: (j, 0))
    mid = pl.BlockSpec((tm, tn), lambda i, j: (i, j))
    return pl.pallas_call(
        body,
        name=name,
        grid=(S // tm, F // tn),
        in_specs=[pl.BlockSpec((tm, D), lambda i, j: (i, 0)), wsp, wsp] + [pl.BlockSpec(memory_space=pl.ANY)] * n_dep,
        out_specs=[mid, mid, mid],
        out_shape=[jax.ShapeDtypeStruct((S, F), BF16)] * 3,
        compiler_params=_cp(("parallel", "arbitrary"), 56),
    )(x, wgt, wut, *([] if dep is None else [dep]))


def _ffn_down(hm, wd, x, ln_g, ln_b, name, tm=256):
    def body(hm_ref, wd_ref, x_ref, g_ref, b_ref, h_ref, hb_ref, xh_ref, rs_ref):
        r = ALPHA * x_ref[...] + 0.5 * _dot_nn(hm_ref[...], wd_ref[...])
        h, xhat, rstd = _ln_fwd(r, g_ref[...], b_ref[...])
        h_ref[...] = h
        hb_ref[...] = h.astype(BF16)
        xh_ref[...] = xhat
        rs_ref[...] = rstd

    row = pl.BlockSpec((tm, D), lambda i: (i, 0))
    vec = pl.BlockSpec((1, D), lambda i: (0, 0))
    return pl.pallas_call(
        body,
        name=name,
        grid=(S // tm,),
        in_specs=[pl.BlockSpec((tm, F), lambda i: (i, 0)), pl.BlockSpec((F, D), lambda i: (0, 0)), row, vec, vec],
        out_specs=[row, row, row, pl.BlockSpec((tm, 1), lambda i: (i, 0))],
        out_shape=[
            jax.ShapeDtypeStruct((S, D), F32),
            jax.ShapeDtypeStruct((S, D), BF16),
            jax.ShapeDtypeStruct((S, D), F32),
            jax.ShapeDtypeStruct((S, 1), F32),
        ],
        compiler_params=_cp(("parallel",), 56),
    )(hm, wd, x, ln_g, ln_b)


def _ffn_bwd_mid(dh_a, dh_b, xhat, rstd, ln_g, a, b, wd, name, tm=512, tn=1408):
    two = dh_b is not None

    def body(*refs):
        dha_ref = refs[0]
        dhb_ref = refs[1] if two else None
        (xh_ref, rs_ref, g_ref, a_ref, b_ref, wd_ref,
         dr_ref, df_ref, da_ref, db_ref, dg_ref, dbias_ref, df_scr) = refs[2 if two else 1:]
        i = pl.program_id(0)
        j = pl.program_id(1)

        @pl.when(j == 0)
        def _():
            dh = dha_ref[...]
            if two:
                dh = ALPHA * dh + dhb_ref[...]
            xhat = xh_ref[...]
            dr = _ln_bwd(dh, xhat, rs_ref[...], g_ref[...])
            dfb = (0.5 * dr).astype(BF16)
            dr_ref[...] = dr
            df_scr[...] = dfb
            df_ref[...] = dfb
            sg = _colsum(dh * xhat)
            sb = _colsum(dh)

            @pl.when(i == 0)
            def _():
                dg_ref[...] = sg
                dbias_ref[...] = sb

            @pl.when(i > 0)
            def _():
                dg_ref[...] += sg
                dbias_ref[...] += sb

        dhm = _dot_nt(df_scr[...], wd_ref[...])
        av = a_ref[...].astype(F32)
        bv = b_ref[...].astype(F32)
        sig = _sigmoid(av)
        da_ref[...] = (dhm * bv * (sig * (1.0 + av * (1.0 - sig)))).astype(BF16)
        db_ref[...] = (dhm * (av * sig)).astype(BF16)

    row = pl.BlockSpec((tm, D), lambda i, j: (i, 0))
    vec = pl.BlockSpec((1, D), lambda i, j: (0, 0))
    mid = pl.BlockSpec((tm, tn), lambda i, j: (i, j))
    ins = [dh_a] + ([dh_b] if two else []) + [xhat, rstd, ln_g, a, b, wd]
    in_specs = [row] * (2 if two else 1) + [row, pl.BlockSpec((tm, 1), lambda i, j: (i, 0)), vec, mid, mid,
                                            pl.BlockSpec((tn, D), lambda i, j: (j, 0))]
    return pl.pallas_call(
        body,
        name=name,
        grid=(S // tm, F // tn),
        in_specs=in_specs,
        out_specs=[row, row, mid, mid, vec, vec],
        out_shape=[
            jax.ShapeDtypeStruct((S, D), F32),
            jax.ShapeDtypeStruct((S, D), BF16),
            jax.ShapeDtypeStruct((S, F), BF16),
            jax.ShapeDtypeStruct((S, F), BF16),
            jax.ShapeDtypeStruct((1, D), F32),
            jax.ShapeDtypeStruct((1, D), F32),
        ],
        scratch_shapes=[pltpu.VMEM((tm, D), BF16)],
        compiler_params=_cp(("arbitrary", "arbitrary"), 56),
    )(*ins)


def _ffn_bwd_dx(dr, da, db, wgt, wut, name, tm=512, tk=1408, dep=None):
    nk = F // tk
    n_dep = 0 if dep is None else 1

    def body(dr_ref, da_ref, db_ref, wg_ref, wu_ref, *rest):
        dx_ref, acc = rest[n_dep], rest[n_dep + 1]
        kk = pl.program_id(1)
        part = _dot_nn(da_ref[...], wg_ref[...]) + _dot_nn(db_ref[...], wu_ref[...])

        @pl.when(kk == 0)
        def _():
            acc[...] = ALPHA * dr_ref[...] + part

        @pl.when(kk > 0)
        def _():
            acc[...] += part

        @pl.when(kk == nk - 1)
        def _():
            dx_ref[...] = acc[...]

    row = pl.BlockSpec((tm, D), lambda i, kk: (i, 0))
    mid = pl.BlockSpec((tm, tk), lambda i, kk: (i, kk))
    wsp = pl.BlockSpec((tk, D), lambda i, kk: (kk, 0))
    return pl.pallas_call(
        body,
        name=name,
        grid=(S // tm, nk),
        in_specs=[row, mid, mid, wsp, wsp] + [pl.BlockSpec(memory_space=pl.ANY)] * n_dep,
        out_specs=row,
        out_shape=jax.ShapeDtypeStruct((S, D), F32),
        scratch_shapes=[pltpu.VMEM((tm, D), F32)],
        compiler_params=_cp(("parallel", "arbitrary"), 56),
    )(dr, da, db, wgt, wut, *([] if dep is None else [dep]))


def _rope_tables(pos_col, invf, sign, tm=512):
    def body(p_ref, f_ref, s_ref, c_out, s_out):
        ang = p_ref[...].astype(F32) * f_ref[...]
        c_out[...] = jnp.cos(ang)
        s_out[...] = jnp.sin(ang) * s_ref[...]

    vec = pl.BlockSpec((1, BLK), lambda i: (0, 0))
    out = pl.BlockSpec((tm, BLK), lambda i: (i, 0))
    return pl.pallas_call(
        body,
        name="rope_tables",
        grid=(S // tm,),
        in_specs=[pl.BlockSpec((tm, 1), lambda i: (i, 0)), vec, vec],
        out_specs=[out, out],
        out_shape=[jax.ShapeDtypeStruct((S, BLK), F32)] * 2,
        compiler_params=_cp(("parallel",)),
    )(pos_col, invf, sign)


def _lane_lo(rows=BLK):
    return (lax.broadcasted_iota(jnp.int32, (rows, GW), 1) % HEAD_DIM) < (HEAD_DIM // 2)


def _swap_halves(t, lo):
    return jnp.where(lo, pltpu.roll(t, GW - HEAD_DIM // 2, 1), pltpu.roll(t, HEAD_DIM // 2, 1))


def _rope(t, cosf, sinf, lo):
    return t * cosf + _swap_halves(t, lo) * sinf


def _unrope(g, cosf, sinf, lo):
    return g * cosf + _swap_halves(g * sinf, lo)


def _tile4(v):
    return jnp.concatenate([v, v, v, v], axis=1)


def _band_mask(n):
    qi = lax.broadcasted_iota(jnp.int32, (BLK, 2 * BLK), 0)
    kj = lax.broadcasted_iota(jnp.int32, (BLK, 2 * BLK), 1)
    dist = qi + BLK - kj
    return (dist >= 0) & (dist <= BLK) & ((kj >= BLK) | (n >= 1))


ROWS = 256
LANES = 128


def _to_view(scr, y, dst_ref, d, dtype, col0=0):
    if d == 1:
        dst_ref[:, col0:col0 + GW] = y.astype(dtype)
        return
    for cb in range(GW // LANES):
        scr[cb][...] = y[:, cb * LANES:(cb + 1) * LANES]
    for r in range(d):
        for cb in range(GW // LANES):
            at = col0 + r * GW + cb * LANES
            dst_ref[:, at:at + LANES] = scr[cb][pl.ds(r, ROWS // d, stride=d), :].astype(dtype)


def _from_view(scr, src_ref, d):
    if d == 1:
        return src_ref[...].astype(F32)
    for r in range(d):
        for cb in range(GW // LANES):
            at = r * GW + cb * LANES
            scr[cb][pl.ds(r, ROWS // d, stride=d), :] = src_ref[:, at:at + LANES].astype(F32)
    return jnp.concatenate([scr[cb][...] for cb in range(GW // LANES)], axis=1)


def _view_spec(d):
    return pl.BlockSpec((ROWS // d, d * GW), lambda i: (i, 0))


_VIEW_SCRATCH = [pltpu.VMEM((ROWS, LANES), F32)] * (GW // LANES)


def _qkv_prep(qkv, cos_t, sin_t):
    def body(x_ref, c_ref, s_ref, *rest):
        outs, scr = rest[:9], rest[9:]
        lo = _lane_lo(ROWS)
        cf, sf = _tile4(c_ref[...]), _tile4(s_ref[...])
        for gi, d in enumerate(PATTERN_DILATIONS):
            q, k, v = (x_ref[:, (3 * part + gi) * GW:(3 * part + gi + 1) * GW] for part in range(3))
            _to_view(scr, _rope(q, cf, sf, lo) * (HEAD_DIM ** -0.5), outs[3 * gi], d, BF16)
            _to_view(scr, _rope(k, cf, sf, lo), outs[3 * gi + 1], d, BF16)
            _to_view(scr, v, outs[3 * gi + 2], d, BF16)

    tab = pl.BlockSpec((ROWS, BLK), lambda i: (i, 0))
    outs = pl.pallas_call(
        body,
        name="qkv_prep",
        grid=(S // ROWS,),
        in_specs=[pl.BlockSpec((ROWS, QKV_W), lambda i: (i, 0)), tab, tab],
        out_specs=[_view_spec(d) for d in PATTERN_DILATIONS for _ in range(3)],
        out_shape=[jax.ShapeDtypeStruct((S // d, d * GW), BF16) for d in PATTERN_DILATIONS for _ in range(3)],
        scratch_shapes=_VIEW_SCRATCH,
        compiler_params=_cp(("parallel",), 48),
    )(qkv, cos_t, sin_t)
    return [tuple(outs[3 * gi:3 * gi + 3]) for gi in range(3)]


N_ATTN_STEPS = S // BLK


def _class_and_block(gi, t):
    per_class = N_ATTN_STEPS // PATTERN_DILATIONS[gi]
    return t >> (per_class.bit_length() - 1), t & (per_class - 1)


def _attn_specs(gi):
    def cur(t):
        r, n = _class_and_block(gi, t)
        return n, r

    def prev(t):
        r, n = _class_and_block(gi, t)
        return jnp.maximum(n - 1, 0), r

    def whole(t):
        return 0, _class_and_block(gi, t)[0]

    sub = S // PATTERN_DILATIONS[gi]
    return (pl.BlockSpec((BLK, GW), cur), pl.BlockSpec((BLK, GW), prev), pl.BlockSpec((sub, GW), whole))


def _left_lanes():
    return lax.broadcasted_iota(jnp.int32, (BLK, LANES), 1) < HEAD_DIM


def _attn_fwd_block(n, q_ref, kc_ref, kp_ref, vc_ref, vp_ref, o_ref, l_ref):
    mask = _band_mask(n)
    left = _left_lanes()
    zero = jnp.zeros((BLK, LANES), BF16)
    for pr in range(GW // LANES):
        cs = slice(pr * LANES, (pr + 1) * LANES)
        q2 = q_ref[:, cs]
        k2 = jnp.concatenate([kp_ref[:, cs], kc_ref[:, cs]], axis=0)
        v2 = jnp.concatenate([vp_ref[:, cs], vc_ref[:, cs]], axis=0)
        o_h, lse_h = [], []
        for side in (left, ~left):
            s = jnp.where(mask, _dot_nt(jnp.where(side, q2, zero), k2), NEG)
            m = jnp.max(s, axis=1, keepdims=True)
            p = jnp.exp(s - m)
            l = jnp.sum(p, axis=1, keepdims=True)
            o_h.append(_dot_nn((p / l).astype(BF16), v2))
            lse_h.append(m + jnp.log(l))
        o_ref[:, cs] = jnp.where(left, o_h[0], o_h[1])
        l_ref[:, cs] = jnp.where(left, lse_h[0], lse_h[1])


def _attn_fwd(preps):
    def body(*refs):
        t = pl.program_id(0)
        for gi in range(3):
            _attn_fwd_block(_class_and_block(gi, t)[1], *refs[5 * gi:5 * gi + 5], *refs[15 + 2 * gi:17 + 2 * gi])

    in_specs, out_specs, out_shape, args = [], [], [], []
    for gi, d in enumerate(PATTERN_DILATIONS):
        cur, prev, _ = _attn_specs(gi)
        qr, kr, vv = preps[gi]
        in_specs += [cur, cur, prev, cur, prev]
        args += [qr, kr, kr, vv, vv]
        out_specs += [cur, cur]
        out_shape += [jax.ShapeDtypeStruct((S // d, d * GW), F32)] * 2
    outs = pl.pallas_call(
        body,
        name="attn_fwd",
        grid=(N_ATTN_STEPS,),
        in_specs=in_specs,
        out_specs=out_specs,
        out_shape=out_shape,
        compiler_params=_cp(("arbitrary",), 48),
    )(*args)
    return [outs[0], outs[2], outs[4]], [outs[1], outs[3], outs[5]]


def _attn_bwd_block(n, q_ref, kc_ref, kp_ref, vc_ref, vp_ref, do_ref, l_ref, c_ref, dq_ref, dk_ref, dv_ref):
    mask = _band_mask(n)
    left = _left_lanes()
    zero = jnp.zeros((BLK, LANES), BF16)
    here = pl.ds(pl.multiple_of(n * BLK, BLK), BLK)
    before = pl.ds(pl.multiple_of(jnp.maximum(n - 1, 0) * BLK, BLK), BLK)
    for pr in range(GW // LANES):
        cs = slice(pr * LANES, (pr + 1) * LANES)
        q2 = q_ref[:, cs]
        k2 = jnp.concatenate([kp_ref[:, cs], kc_ref[:, cs]], axis=0)
        v2 = jnp.concatenate([vp_ref[:, cs], vc_ref[:, cs]], axis=0)
        do2 = do_ref[:, cs]
        dq_h, dk2, dv2 = [], None, None
        for h, side in enumerate((left, ~left)):
            at = pr * LANES + h * HEAD_DIM
            qm = jnp.where(side, q2, zero)
            dom = jnp.where(side, do2, zero)
            s = jnp.where(mask, _dot_nt(qm, k2), NEG)
            p = jnp.exp(s - l_ref[:, at:at + 1])
            ds = (p * (_dot_nt(dom, v2) - c_ref[:, at:at + 1])).astype(BF16)
            dq_h.append(_dot_nn(ds, k2))
            dk_part = _dot_tn(ds, qm)
            dv_part = _dot_tn(p.astype(BF16), dom)
            dk2 = dk_part if dk2 is None else dk2 + dk_part
            dv2 = dv_part if dv2 is None else dv2 + dv_part
        dq_ref[:, cs] = jnp.where(left, dq_h[0], dq_h[1]) * (HEAD_DIM ** -0.5)
        dk_ref[here, cs] = dk2[BLK:]
        dv_ref[here, cs] = dv2[BLK:]
        dk_ref[before, cs] += dk2[:BLK]
        dv_ref[before, cs] += dv2[:BLK]


def _attn_bwd(preps, dos, lses, cterms):
    def body(*refs):
        t = pl.program_id(0)
        for gi in range(3):
            _attn_bwd_block(_class_and_block(gi, t)[1], *refs[8 * gi:8 * gi + 8], *refs[24 + 3 * gi:27 + 3 * gi])

    in_specs, out_specs, out_shape, args = [], [], [], []
    for gi, d in enumerate(PATTERN_DILATIONS):
        cur, prev, whole = _attn_specs(gi)
        qr, kr, vv = preps[gi]
        in_specs += [cur, cur, prev, cur, prev, cur, cur, cur]
        args += [qr, kr, kr, vv, vv, dos[gi], lses[gi], cterms[gi]]
        out_specs += [cur, whole, whole]
        out_shape += [jax.ShapeDtypeStruct((S // d, d * GW), F32)] * 3
    outs = pl.pallas_call(
        body,
        name="attn_bwd",
        grid=(N_ATTN_STEPS,),
        in_specs=in_specs,
        out_specs=out_specs,
        out_shape=out_shape,
        compiler_params=_cp(("arbitrary",), 56),
    )(*args)
    return [tuple(outs[3 * gi:3 * gi + 3]) for gi in range(3)]


def _qkv_unprep(grads, cos_t, sin_t, dproj):
    def body(*refs):
        views, (c_ref, s_ref, _, out_ref), scr = refs[:9], refs[9:13], refs[13:]
        lo = _lane_lo(ROWS)
        cf, sf = _tile4(c_ref[...]), _tile4(s_ref[...])
        for gi, d in enumerate(PATTERN_DILATIONS):
            dq_ref, dk_ref, dv_ref = views[3 * gi:3 * gi + 3]
            at = 3 * gi * GW
            out_ref[:, at:at + GW] = _unrope(_from_view(scr, dq_ref, d), cf, sf, lo).astype(BF16)
            out_ref[:, at + GW:at + 2 * GW] = _unrope(_from_view(scr, dk_ref, d), cf, sf, lo).astype(BF16)
            out_ref[:, at + 2 * GW:at + 3 * GW] = _from_view(scr, dv_ref, d).astype(BF16)

    tab = pl.BlockSpec((ROWS, BLK), lambda i: (i, 0))
    return pl.pallas_call(
        body,
        name="qkv_unprep",
        grid=(S // ROWS,),
        in_specs=[_view_spec(d) for d in PATTERN_DILATIONS for _ in range(3)] + [tab, tab, pl.BlockSpec(memory_space=pl.ANY)],
        out_specs=pl.BlockSpec((ROWS, QKV_W), lambda i: (i, 0)),
        out_shape=jax.ShapeDtypeStruct(dproj.shape, dproj.dtype),
        input_output_aliases={11: 0},
        scratch_shapes=_VIEW_SCRATCH,
        compiler_params=_cp(("parallel",), 48),
    )(*[g for grp in grads for g in grp], cos_t, sin_t, dproj)


def _group_weights(l0, l1, l2):
    mx = jnp.maximum(jnp.maximum(l0, l1), l2)
    e0, e1, e2 = jnp.exp(l0 - mx), jnp.exp(l1 - mx), jnp.exp(l2 - mx)
    inv = 1.0 / (e0 + e1 + e2)
    return e0 * inv, e1 * inv, e2 * inv


def _combine_fwd(os_, lses):
    def body(o0, o1, o2, l0, l1, l2, y_ref, *scr):
        ov = [_from_view(scr, o, d) for o, d in zip((o0, o1, o2), PATTERN_DILATIONS)]
        lv = [_from_view(scr, l, d) for l, d in zip((l0, l1, l2), PATTERN_DILATIONS)]
        w0, w1, w2 = _group_weights(*lv)
        y_ref[...] = (w0 * ov[0] + w1 * ov[1] + w2 * ov[2]).astype(BF16)

    views = [_view_spec(d) for d in PATTERN_DILATIONS]
    return pl.pallas_call(
        body,
        name="attn_combine_fwd",
        grid=(S // ROWS,),
        in_specs=views + views,
        out_specs=pl.BlockSpec((ROWS, GW), lambda i: (i, 0)),
        out_shape=jax.ShapeDtypeStruct((S, GW), BF16),
        scratch_shapes=_VIEW_SCRATCH,
        compiler_params=_cp(("parallel",)),
    )(*os_, *lses)


def _combine_bwd(dy, os_, lses, seg):
    def body(dy_ref, o0, o1, o2, l0, l1, l2, seg_ref, d0, d1, d2, c0, c1, c2, *scr):
        ov = [_from_view(scr, o, d) for o, d in zip((o0, o1, o2), PATTERN_DILATIONS)]
        lv = [_from_view(scr, l, d) for l, d in zip((l0, l1, l2), PATTERN_DILATIONS)]
        ws = _group_weights(*lv)
        dyv = dy_ref[...]
        t = dyv * (ws[0] * ov[0] + ws[1] * ov[1] + ws[2] * ov[2])
        t_hi = t.astype(BF16)
        r1 = t - t_hi.astype(F32)
        t_mid = r1.astype(BF16)
        t_lo = (r1 - t_mid.astype(F32)).astype(BF16)
        sg = seg_ref[...]
        e = _dot_nn(t_hi, sg) + _dot_nn(t_mid, sg) + _dot_nn(t_lo, sg)
        for w, do_ref, c_ref, d in zip(ws, (d0, d1, d2), (c0, c1, c2), PATTERN_DILATIONS):
            _to_view(scr, w * dyv, do_ref, d, BF16)
            _to_view(scr, w * e, c_ref, d, F32)

    views = [_view_spec(d) for d in PATTERN_DILATIONS]
    return pl.pallas_call(
        body,
        name="attn_combine_bwd",
        grid=(S // ROWS,),
        in_specs=[pl.BlockSpec((ROWS, GW), lambda i: (i, 0))] + views + views + [pl.BlockSpec((GW, GW), lambda i: (0, 0))],
        out_specs=views + views,
        out_shape=[jax.ShapeDtypeStruct((S // d, d * GW), BF16) for d in PATTERN_DILATIONS]
        + [jax.ShapeDtypeStruct((S // d, d * GW), F32) for d in PATTERN_DILATIONS],
        scratch_shapes=_VIEW_SCRATCH,
        compiler_params=_cp(("parallel",)),
    )(dy, *os_, *lses, seg)


_SQRT_HALF = 0.7071067811865476
_INV_SQRT_2PI = 0.3989422804014327


def _gelu(z):
    return 0.5 * z * (1.0 + lax.erf(z * _SQRT_HALF))


def _gelu_grad(z):
    return 0.5 * (1.0 + lax.erf(z * _SQRT_HALF)) + z * (_INV_SQRT_2PI * jnp.exp(-0.5 * z * z))


def _tril_ws(ws_ref, g):
    t = lax.broadcasted_iota(jnp.int32, (BLK, BLK), 0)
    s = lax.broadcasted_iota(jnp.int32, (BLK, BLK), 1)
    return jnp.where(t >= s, ws_ref[g], 0.0)


def _gmlp_fwd(z, ws, bst, ln_g, ln_b, tm=256):
    nch = tm // BLK

    def body(z_ref, ws_ref, b_ref, g_ref, be_ref, y_ref):
        zg = _gelu(z_ref[...])
        u = zg[:, :D]
        vn, _, _ = _ln_fwd(zg[:, D:], g_ref[...], be_ref[...])
        vnb = vn.astype(BF16)
        bt = b_ref[...]
        for g in range(8):
            w = _tril_ws(ws_ref, g).astype(BF16)
            cols = slice(g * BLK, (g + 1) * BLK)
            for c in range(nch):
                rows = slice(c * BLK, (c + 1) * BLK)
                mixed = _dot_nn(w, vnb[rows, cols]) + bt[:, g:g + 1]
                y_ref[rows, cols] = (u[rows, cols] * mixed).astype(BF16)

    return pl.pallas_call(
        body,
        name="gmlp_fwd",
        grid=(S // tm,),
        in_specs=[
            pl.BlockSpec((tm, 2 * D), lambda i: (i, 0)),
            pl.BlockSpec((8, BLK, BLK), lambda i: (0, 0, 0)),
            pl.BlockSpec((BLK, 8), lambda i: (0, 0)),
            pl.BlockSpec((1, D), lambda i: (0, 0)),
            pl.BlockSpec((1, D), lambda i: (0, 0)),
        ],
        out_specs=pl.BlockSpec((tm, D), lambda i: (i, 0)),
        out_shape=jax.ShapeDtypeStruct((S, D), BF16),
        compiler_params=_cp(("parallel",), 48),
    )(z, ws, bst, ln_g, ln_b)


def _gmlp_bwd(z, dy, ws, bst, ln_g, ln_b, dproj, tm=256):
    nch = tm // BLK

    def body(z_ref, dy_ref, ws_ref, b_ref, g_ref, be_ref, _, dz_ref, dws_ref, dbs_ref, dg_ref, dbe_ref, dvn_scr, dm_acc):
        i = pl.program_id(0)
        zv = z_ref[...]
        zg = _gelu(zv)
        u = zg[:, :D]
        gam = g_ref[...]
        vn, xhat, rstd = _ln_fwd(zg[:, D:], gam, be_ref[...])
        vnb = vn.astype(BF16)
        dyv = dy_ref[...]
        dmix = dyv * u
        dmb = dmix.astype(BF16)
        bt = b_ref[...]
        tmask = lax.broadcasted_iota(jnp.int32, (BLK, BLK), 0) >= lax.broadcasted_iota(jnp.int32, (BLK, BLK), 1)
        dm_sum = dmix[0:BLK]
        for c in range(1, nch):
            dm_sum = dm_sum + dmix[c * BLK:(c + 1) * BLK]

        @pl.when(i == 0)
        def _():
            dm_acc[...] = dm_sum

        @pl.when(i > 0)
        def _():
            dm_acc[...] += dm_sum

        dus = []
        for g in range(8):
            w = _tril_ws(ws_ref, g).astype(BF16)
            cols = slice(g * BLK, (g + 1) * BLK)
            dw = None
            du_rows = []
            for c in range(nch):
                rows = slice(c * BLK, (c + 1) * BLK)
                mixed = _dot_nn(w, vnb[rows, cols]) + bt[:, g:g + 1]
                du_rows.append(dyv[rows, cols] * mixed)
                part = _dot_nt(dmb[rows, cols], vnb[rows, cols])
                dw = part if dw is None else dw + part
                dvn_scr[rows, cols] = _dot_tn(w, dmb[rows, cols])
            dus.append(jnp.concatenate(du_rows, axis=0))
            dw = jnp.where(tmask, dw, 0.0)

            @pl.when(i == 0)
            def _():
                dws_ref[g] = dw

            @pl.when(i > 0)
            def _():
                dws_ref[g] += dw

        dvn = dvn_scr[...]
        sg = _colsum(dvn * xhat)
        sb = _colsum(dvn)

        @pl.when(i == 0)
        def _():
            dg_ref[...] = sg
            dbe_ref[...] = sb

        @pl.when(i > 0)
        def _():
            dg_ref[...] += sg
            dbe_ref[...] += sb

        dvg = _ln_bwd(dvn, xhat, rstd, gam)
        gp = _gelu_grad(zv)
        dz_ref[:, :D] = (jnp.concatenate(dus, axis=1) * gp[:, :D]).astype(BF16)
        dz_ref[:, D:] = (dvg * gp[:, D:]).astype(BF16)

        @pl.when(i == S // tm - 1)
        def _():
            acc = dm_acc[...]
            for g in range(8):
                dbs_ref[:, g:g + 1] = jnp.sum(acc[:, g * BLK:(g + 1) * BLK], axis=1, keepdims=True)

    vec = pl.BlockSpec((1, D), lambda i: (0, 0))
    return pl.pallas_call(
        body,
        name="gmlp_bwd",
        grid=(S // tm,),
        in_specs=[
            pl.BlockSpec((tm, 2 * D), lambda i: (i, 0)),
            pl.BlockSpec((tm, D), lambda i: (i, 0)),
            pl.BlockSpec((8, BLK, BLK), lambda i: (0, 0, 0)),
            pl.BlockSpec((BLK, 8), lambda i: (0, 0)),
            vec,
            vec,
            pl.BlockSpec(memory_space=pl.ANY),
        ],
        out_specs=[
            pl.BlockSpec((tm, 2 * D), lambda i: (i, DPROJ_Z_COL)),
            pl.BlockSpec((8, BLK, BLK), lambda i: (0, 0, 0)),
            pl.BlockSpec((BLK, 8), lambda i: (0, 0)),
            vec,
            vec,
        ],
        out_shape=[
            jax.ShapeDtypeStruct(dproj.shape, dproj.dtype),
            jax.ShapeDtypeStruct((8, BLK, BLK), F32),
            jax.ShapeDtypeStruct((BLK, 8), F32),
            jax.ShapeDtypeStruct((1, D), F32),
            jax.ShapeDtypeStruct((1, D), F32),
        ],
        input_output_aliases={6: 0},
        scratch_shapes=[pltpu.VMEM((tm, D), F32), pltpu.VMEM((BLK, D), F32)],
        compiler_params=_cp(("arbitrary",), 48),
    )(z, dy, ws, bst, ln_g, ln_b, dproj)


def _merge_fwd(ya, yg, glog, bgate, h1, wabt, wgb, wo, ln_g, ln_b, tm=256):
    def body(ya_ref, yg_ref, gl_ref, bg_ref, h1_ref, wab_ref, wgb_ref, wo_ref, g_ref, b_ref,
             h_ref, hb_ref, xh_ref, rs_ref, mg_ref, bra_ref, brg_ref):
        bra = _dot_nt(ya_ref[...], wab_ref[...])
        brg = _dot_nn(yg_ref[...], wgb_ref[...])
        gates = _sigmoid(gl_ref[...] + bg_ref[...])
        merged = (gates[:, :D] * bra + gates[:, D:] * brg).astype(BF16)
        mix = _dot_nn(merged, wo_ref[...])
        h, xhat, rstd = _ln_fwd(ALPHA * h1_ref[...] + mix, g_ref[...], b_ref[...])
        h_ref[...] = h
        hb_ref[...] = h.astype(BF16)
        xh_ref[...] = xhat
        rs_ref[...] = rstd
        mg_ref[...] = merged
        bra_ref[...] = bra
        brg_ref[...] = brg

    row = pl.BlockSpec((tm, D), lambda i: (i, 0))
    vec = pl.BlockSpec((1, D), lambda i: (0, 0))
    full = lambda shape: pl.BlockSpec(shape, lambda i: (0, 0))
    return pl.pallas_call(
        body,
        name="merge_fwd",
        grid=(S // tm,),
        in_specs=[
            pl.BlockSpec((tm, GW), lambda i: (i, 0)), row,
            pl.BlockSpec((tm, 2 * D), lambda i: (i, glog.shape[1] // (2 * D) - 1)),
            full((1, 2 * D)), row,
            full((D, GW)), full((D, D)), full((D, D)), vec, vec,
        ],
        out_specs=[row, row, row, pl.BlockSpec((tm, 1), lambda i: (i, 0)), row, row, row],
        out_shape=[
            jax.ShapeDtypeStruct((S, D), F32),
            jax.ShapeDtypeStruct((S, D), BF16),
            jax.ShapeDtypeStruct((S, D), F32),
            jax.ShapeDtypeStruct((S, 1), F32),
            jax.ShapeDtypeStruct((S, D), BF16),
            jax.ShapeDtypeStruct((S, D), F32),
            jax.ShapeDtypeStruct((S, D), F32),
        ],
        compiler_params=_cp(("parallel",), 48),
    )(ya, yg, glog, bgate, h1, wabt, wgb, wo, ln_g, ln_b)


def _merge_bwd(dh2, xhat, rstd, ln_g, bra, brg, glog, bgate, wabt, wgb, wo, tm=256):
    def body(dh_ref, xh_ref, rs_ref, g_ref, bra_ref, brg_ref, gl_ref, bg_ref, wab_ref, wgb_ref, wo_ref,
             dr_ref, drb_ref, dlog_ref, dba_ref, dbg_ref, dya_ref, dyg_ref, dbgate_ref, dg_ref, dbias_ref):
        i = pl.program_id(0)
        dh = dh_ref[...]
        xh = xh_ref[...]
        dr = _ln_bwd(dh, xh, rs_ref[...], g_ref[...])
        drb = dr.astype(BF16)
        dr_ref[...] = dr
        drb_ref[...] = drb
        dmerged = _dot_nt(drb, wo_ref[...])
        gates = _sigmoid(gl_ref[...] + bg_ref[...])
        g0, g1 = gates[:, :D], gates[:, D:]
        dl0 = dmerged * bra_ref[...] * g0 * (1.0 - g0)
        dl1 = dmerged * brg_ref[...] * g1 * (1.0 - g1)
        dlog_ref[:, :D] = dl0.astype(BF16)
        dlog_ref[:, D:] = dl1.astype(BF16)
        dba = (dmerged * g0).astype(BF16)
        dbg = (dmerged * g1).astype(BF16)
        dba_ref[...] = dba
        dbg_ref[...] = dbg
        dya_ref[...] = _dot_nn(dba, wab_ref[...])
        dyg_ref[...] = _dot_nt(dbg, wgb_ref[...])
        s0, s1 = _colsum(dl0), _colsum(dl1)
        sg, sb = _colsum(dh * xh), _colsum(dh)

        @pl.when(i == 0)
        def _():
            dbgate_ref[:, :D] = s0
            dbgate_ref[:, D:] = s1
            dg_ref[...] = sg
            dbias_ref[...] = sb

        @pl.when(i > 0)
        def _():
            dbgate_ref[:, :D] += s0
            dbgate_ref[:, D:] += s1
            dg_ref[...] += sg
            dbias_ref[...] += sb

    row = pl.BlockSpec((tm, D), lambda i: (i, 0))
    vec = pl.BlockSpec((1, D), lambda i: (0, 0))
    wide = pl.BlockSpec((tm, 2 * D), lambda i: (i, 0))
    full = lambda shape: pl.BlockSpec(shape, lambda i: (0, 0))
    return pl.pallas_call(
        body,
        name="merge_bwd",
        grid=(S // tm,),
        in_specs=[row, row, pl.BlockSpec((tm, 1), lambda i: (i, 0)), vec, row, row,
                  pl.BlockSpec((tm, 2 * D), lambda i: (i, glog.shape[1] // (2 * D) - 1)),
                  full((1, 2 * D)), full((D, GW)), full((D, D)), full((D, D))],
        out_specs=[row, row, pl.BlockSpec((tm, 2 * D), lambda i: (i, DPROJ_G_COL)), row, row,
                   pl.BlockSpec((tm, GW), lambda i: (i, 0)), row, full((1, 2 * D)), vec, vec],
        out_shape=[
            jax.ShapeDtypeStruct((S, D), F32),
            jax.ShapeDtypeStruct((S, D), BF16),
            jax.ShapeDtypeStruct((S, DPROJ_W), BF16),
            jax.ShapeDtypeStruct((S, D), BF16),
            jax.ShapeDtypeStruct((S, D), BF16),
            jax.ShapeDtypeStruct((S, GW), F32),
            jax.ShapeDtypeStruct((S, D), F32),
            jax.ShapeDtypeStruct((1, 2 * D), F32),
            jax.ShapeDtypeStruct((1, D), F32),
            jax.ShapeDtypeStruct((1, D), F32),
        ],
        compiler_params=_cp(("arbitrary",), 48),
    )(dh2, xhat, rstd, ln_g, bra, brg, glog, bgate, wabt, wgb, wo)


def _loss_head(h3, target, tm=512):
    def body(h_ref, t_ref, d_ref, l_ref):
        i = pl.program_id(0)
        e = h_ref[...] - t_ref[...]
        d_ref[...] = e * (1.0 / D)
        part = jnp.sum(_colsum(e * e), axis=1, keepdims=True) * (0.5 / D)

        @pl.when(i == 0)
        def _():
            l_ref[...] = part

        @pl.when(i > 0)
        def _():
            l_ref[...] += part

    row = pl.BlockSpec((tm, D), lambda i: (i, 0))
    return pl.pallas_call(
        body,
        name="loss_head",
        grid=(S // tm,),
        in_specs=[row, row],
        out_specs=[row, pl.BlockSpec((1, 1), lambda i: (0, 0))],
        out_shape=[jax.ShapeDtypeStruct((S, D), F32), jax.ShapeDtypeStruct((1, 1), F32)],
        compiler_params=_cp(("arbitrary",)),
    )(h3, target)


def _tie(x, dep):
    if dep is None:
        return x
    return x + dep[0, 0].astype(x.dtype)


def _local_step(x, pos_col, target, get_w, p, emit):
    w = get_w("ffn1", None)
    a1, b1, hm1 = _ffn_up(x, w["g1"], w["u1"], "ffn1_up")
    w.update(get_w("ffn1d", hm1))
    h1, h1b, xh1, rs1 = _ffn_down(hm1, w["d1"], x, p["ln1_g"], p["ln1_b"], "ffn1_down")

    w.update(get_w("win", h1b))
    qkv = _matmul(h1b, w["win"], "nt", F32, 1024, 1536, D, "proj_qkv", b_off=0, n_out=QKV_W, dep=w.get("_dep"))
    z = glog = _matmul(h1b, w["win"], "nt", F32, 1024, 512, D, "proj_zg", b_off=QKV_W // 512, n_out=4 * D)

    half = jnp.arange(0, HEAD_DIM, 2, dtype=F32) / HEAD_DIM
    inv_freq = ROPE_THETA ** (-half)
    invf = jnp.tile(inv_freq, 4).reshape(1, BLK)
    sign = jnp.tile(jnp.concatenate([-jnp.ones((32,), F32), jnp.ones((32,), F32)]), 2).reshape(1, BLK)
    cos_t, sin_t = _rope_tables(pos_col, invf, sign)

    get_w("mix", qkv, early=True)
    preps = _qkv_prep(qkv, cos_t, sin_t)
    os_, lses = _attn_fwd(preps)
    ya = _combine_fwd(os_, lses)
    bst = p["gmlp_b_s"].T
    yg = _gmlp_fwd(z, p["gmlp_w_s"], bst, p["gmlp_ln_g"], p["gmlp_ln_b"])
    w.update(get_w("mix", yg))
    get_w("ffn2", ya, early=True)
    h2, h2b, xh2, rs2, merged, bra, brg = _merge_fwd(ya, yg, glog, p["b_gates"], h1, w["ab"], w["gb"], w["o"],
                                                      p["ln2_g"], p["ln2_b"])
    w.update(get_w("ffn2", h2b))
    a2, b2, hm2 = _ffn_up(h2b, w["g2"], w["u2"], "ffn2_up")
    h3, _, xh3, rs3 = _ffn_down(hm2, w["d2"], h2, p["ln3_g"], p["ln3_b"], "ffn2_down")
    dh3, loss = _loss_head(h3, target)

    gp = {}
    dr3, df2, da2, db2, gp["ln3_g"], gp["ln3_b"] = _ffn_bwd_mid(dh3, None, xh3, rs3, p["ln3_g"], a2, b2, w["d2"],
                                                                "ffn2_bwd_mid")
    tok = emit("ffn2", {
        "g2": _matmul(da2, h2b, "tn", BF16, 1408, D, S, "wgrad_g2"),
        "u2": _matmul(db2, h2b, "tn", BF16, 1408, D, S, "wgrad_u2"),
        "d2": _matmul(hm2, df2, "tn", BF16, 1408, D, S, "wgrad_d2")})
    dh2 = _ffn_bwd_dx(dr3, da2, db2, w["g2"], w["u2"], "ffn2_bwd_dx")

    (dr2, dr2b, dproj, dba, dbg, dya, dyg, gp["b_gates"], gp["ln2_g"], gp["ln2_b"]) = _merge_bwd(
        dh2, xh2, rs2, _tie(p["ln2_g"], tok), bra, brg, glog, p["b_gates"], w["ab"], w["gb"], w["o"])
    tok = emit("mix", {
        "o": _matmul(merged, dr2b, "tn", BF16, 512, D, S, "wgrad_o"),
        "ab": _matmul(dba, ya, "tn", BF16, 512, GW, S, "wgrad_ab"),
        "gb": _matmul(yg, dbg, "tn", BF16, 512, D, S, "wgrad_gb")})

    seg = (jnp.arange(GW)[:, None] // HEAD_DIM == jnp.arange(GW)[None, :] // HEAD_DIM).astype(BF16)
    do0, do1, do2, c0, c1, c2 = _combine_bwd(dya, os_, lses, _tie(seg, tok))
    dproj = _qkv_unprep(_attn_bwd(preps, (do0, do1, do2), lses, (c0, c1, c2)), cos_t, sin_t, dproj)
    dproj, gp["gmlp_w_s"], dbst, gp["gmlp_ln_g"], gp["gmlp_ln_b"] = _gmlp_bwd(
        z, dyg, p["gmlp_w_s"], bst, p["gmlp_ln_g"], p["gmlp_ln_b"], dproj)
    gp["gmlp_b_s"] = dbst.T
    tok = emit("win", {"win": _matmul(dproj, h1b, "tn", BF16, 512, D, S, "wgrad_win", a_map=_dproj_tile, m_out=IN_W)})
    dh1m = _matmul(dproj, w["win"], "nn", F32, 1024, D, 512, "dproj_to_dh1", dep=tok, a_map=_dproj_tile)

    dr1, df1, da1, db1, gp["ln1_g"], gp["ln1_b"] = _ffn_bwd_mid(dr2, dh1m, xh1, rs1, p["ln1_g"], a1, b1, w["d1"],
                                                                "ffn1_bwd_mid")
    tok = emit("small", {**gp, "loss": loss})
    tok = emit("g1", {"g1": _matmul(da1, x, "tn", BF16, 1408, D, S, "wgrad_g1", dep=tok)})
    tok = emit("u1", {"u1": _matmul(db1, x, "tn", BF16, 1408, D, S, "wgrad_u1", dep=tok)})
    tok = emit("d1", {"d1": _matmul(hm1, df1, "tn", BF16, 1408, D, S, "wgrad_d1", dep=tok)})
    dx = _ffn_bwd_dx(dr1, da1, db1, w["g1"], w["u1"], "ffn1_bwd_dx", dep=tok)
    return loss, dx, gp


_FLIPS = [(mx, my, mc) for mx in (0, 1) for my in (0, 1) for mc in (0, 1)][1:]
_GROUPS = {"ffn1": ("g1", "u1"), "ffn1d": ("d1",), "win": ("win",), "mix": ("ab", "gb", "o"), "ffn2": ("g2", "u2", "d2")}
_SCATTERS = {"ffn2": ("g2", "u2", "d2"), "mix": ("ab", "gb", "o"), "win": ("win",), "g1": ("g1",), "u1": ("u1",), "d1": ("d1",)}
_TWO_STAGE = ("g1", "u1", "d1")
_HBM = pl.BlockSpec(memory_space=pltpu.HBM)
_SEM = pl.BlockSpec(memory_space=pltpu.SEMAPHORE)
_EFFECT = pltpu.SideEffectType.DATAFLOW_SIDE_EFFECTING


def _me():
    return 4 * lax.axis_index("x") + 2 * lax.axis_index("y") + lax.axis_index("c")


def _copies_start(bufs, copies, n_sem, name, after=None):
    nb = len(bufs)
    n_after = 0 if after is None else 1

    def body(*refs):
        b = refs[:nb]
        send_sems, recv_sems = refs[nb + n_after], refs[nb + n_after + 1]
        token = refs[-1]
        x, y, c = lax.axis_index("x"), lax.axis_index("y"), lax.axis_index("c")
        me = 4 * x + 2 * y + c
        for si, s_slot, di, d_slot, (mx, my, mc), sem in copies:
            s_idx, d_idx = s_slot(me), d_slot(me)
            pltpu.make_async_remote_copy(
                src_ref=b[si] if s_idx is None else b[si].at[s_idx],
                dst_ref=b[di] if d_idx is None else b[di].at[d_idx],
                send_sem=send_sems.at[sem], recv_sem=recv_sems.at[sem],
                device_id=(x ^ mx, y ^ my, c ^ mc), device_id_type=MESH).start()
        token[...] = jnp.zeros_like(token)

    ins = [pltpu.with_memory_space_constraint(a, pltpu.HBM) for a in bufs]
    outs = pl.pallas_call(
        body,
        name=name,
        in_specs=[_HBM] * nb + [pl.BlockSpec(memory_space=pl.ANY)] * n_after,
        out_specs=[_SEM, _SEM] + [_HBM] * nb + [pl.BlockSpec(memory_space=pltpu.VMEM)],
        out_shape=[pltpu.SemaphoreType.DMA((n_sem,)), pltpu.SemaphoreType.DMA((n_sem,))]
        + [pltpu.HBM(a.shape, a.dtype) for a in ins] + [jax.ShapeDtypeStruct((8, 128), F32)],
        input_output_aliases={k: 2 + k for k in range(nb)},
        compiler_params=pltpu.CompilerParams(has_side_effects=_EFFECT),
    )(*ins, *([] if after is None else [after]))
    return outs[0], outs[1], list(outs[2:2 + nb]), outs[-1]


def _copies_wait(started, waits, after, name):
    send_sems, recv_sems, bufs, _ = started
    nb = len(bufs)

    def body(*refs):
        b = refs[:nb]
        ss, rs = refs[nb], refs[nb + 1]
        me3 = (lax.axis_index("x"), lax.axis_index("y"), lax.axis_index("c"))
        for bi, n_blocks, sem, is_send in waits:
            blocks = b[bi].at[pl.ds(0, n_blocks)]
            cp = pltpu.make_async_remote_copy(src_ref=blocks, dst_ref=blocks, send_sem=ss.at[sem], recv_sem=rs.at[sem],
                                              device_id=me3, device_id_type=MESH)
            if is_send:
                cp.wait_send()
            else:
                cp.wait_recv()

    return pl.pallas_call(
        body,
        name=name,
        in_specs=[_HBM] * nb + [_SEM, _SEM, pl.BlockSpec(memory_space=pl.ANY)],
        out_specs=[_HBM] * nb,
        out_shape=[pltpu.HBM(a.shape, a.dtype) for a in bufs],
        input_output_aliases={k: k for k in range(nb)},
        compiler_params=pltpu.CompilerParams(has_side_effects=_EFFECT),
    )(*bufs, send_sems, recv_sems, after)


def _landing(own, me):
    return lax.dynamic_update_slice(lax.empty((N_DEV,) + own.shape[1:], own.dtype), own, (me, 0, 0))


_SIBLING = (0, 0, 1)
_OTHER_CHIPS = ((1, 0, 0), (0, 1, 0), (1, 1, 0))


def _bits(flip):
    return 4 * flip[0] + 2 * flip[1] + flip[2]


def _gather_send(shards, grp, after):
    nw = len(shards)
    me = _me()
    bufs = list(shards) + [_landing(a[None], me) for a in shards]
    copies = []
    for k in range(nw):
        copies.append((k, lambda me: None, nw + k, lambda me: me, _SIBLING, nw + k))
        copies += [(k, lambda me: None, nw + k, lambda me: me, f, k) for f in _OTHER_CHIPS]
    return _copies_start(bufs, copies, 2 * nw, "gather_send_" + grp, after)


def _gather_pass(started, grp, after):
    nw = len(started[2]) // 2
    waits = []
    for k in range(nw):
        waits += [(nw + k, 3, k, True), (nw + k, 1, nw + k, True), (nw + k, 3, k, False), (nw + k, 1, nw + k, False)]
    lands = list(_copies_wait(started, waits, after, "gather_arrived_" + grp)[nw:])
    copies = []
    for k in range(nw):
        for f in _OTHER_CHIPS:
            slot = functools.partial(lambda me, bits: me ^ bits, bits=_bits(f))
            copies.append((k, slot, k, slot, _SIBLING, k))
    return _copies_start(lands, copies, nw, "gather_pass_" + grp)


def _gather_finish(passed, grp, after):
    nw = len(passed[2])
    waits = [(k, 3, k, is_send) for k in range(nw) for is_send in (True, False)]
    return _copies_wait(passed, waits, passed[3] if after is None else after, "gather_done_" + grp)


def _scatter_send(parts, grp, after=None):
    nw = len(parts)
    me = _me()
    bufs = list(parts) + [_landing(lax.dynamic_index_in_dim(a, me, 0, keepdims=True), me) for a in parts]
    copies = []
    for k in range(nw):
        for f in _FLIPS:
            to = functools.partial(lambda me, bits: me ^ bits, bits=_bits(f))
            copies.append((k, to, nw + k, lambda me: me, f, k))
    return _copies_start(bufs, copies, nw, "scatter_send_" + grp, after)


def _scatter_finish(started, grp, after):
    nw = len(started[2]) // 2
    waits = [(nw + k, N_DEV - 1, k, is_send) for k in range(nw) for is_send in (True, False)]
    return _copies_wait(started, waits, after, "scatter_done_" + grp)[nw:]


N_CHIP = N_DEV // 2


def _pair_sum(part, other, key):
    _, r, c = part.shape
    tm = _ADAM_ROWS.get(r, r)
    core = lax.axis_index("c").astype(jnp.int32).reshape(1)

    def body(core_ref, a_ref, b_ref, o_ref):
        o_ref[...] = (a_ref[...].astype(F32) + b_ref[...].astype(F32)).astype(BF16)

    return pl.pallas_call(
        body,
        name="pair_sum_" + key,
        grid_spec=pltpu.PrefetchScalarGridSpec(
            num_scalar_prefetch=1,
            grid=(N_CHIP, r // tm),
            in_specs=[pl.BlockSpec((None, tm, c), lambda q, i, core_ref: (2 * q + core_ref[0], i, 0)),
                      pl.BlockSpec((None, tm, c), lambda q, i, core_ref: (q, i, 0))],
            out_specs=pl.BlockSpec((None, tm, c), lambda q, i, core_ref: (q, i, 0)),
        ),
        out_shape=jax.ShapeDtypeStruct((N_CHIP, r, c), BF16),
        compiler_params=_cp(("parallel", "parallel")),
    )(core, part, other)


def _scatter2_send(part, key, dep=None):
    me = _me()
    _, r, c = part.shape
    swap = []
    for q in range(N_CHIP):
        src = functools.partial(lambda me, q: 2 * q + 1 - me % 2, q=q)
        swap.append((0, src, 1, functools.partial(lambda me, q: q, q=q), _SIBLING, 0))
    started = _copies_start([part, lax.empty((N_CHIP, r, c), part.dtype)], swap, 1, "scatter_swap_" + key, dep)
    waits = [(1, N_CHIP, 0, True), (1, N_CHIP, 0, False)]
    part, other = _copies_wait(started, waits, started[3], "scatter_swapped_" + key)
    pair = _pair_sum(part, other, key)
    chip = me // 2
    land = lax.dynamic_update_slice(lax.empty((N_CHIP, r, c), pair.dtype),
                                    lax.dynamic_index_in_dim(pair, chip, 0, keepdims=True), (chip, 0, 0))
    copies = []
    for f in _OTHER_CHIPS:
        to = functools.partial(lambda me, bits: (me ^ bits) // 2, bits=_bits(f))
        copies.append((0, to, 1, lambda me: me // 2, f, 0))
    return _copies_start([pair, land], copies, 1, "scatter_send_" + key)


def _scatter2_finish(started, key, after):
    waits = [(1, N_CHIP - 1, 0, True), (1, N_CHIP - 1, 0, False)]
    return _copies_wait(started, waits, after, "scatter_done_" + key)[1]


def _allgather_send(block, name, after=None):
    me = _me()
    copies = [(0, lambda me: None, 1, lambda me: me, f, 0) for f in _FLIPS]
    return _copies_start([block, _landing(block[None], me)], copies, 1, name, after)


def _allgather_finish(started, name, after):
    waits = [(1, N_DEV - 1, 0, True), (1, N_DEV - 1, 0, False)]
    return _copies_wait(started, waits, after, name)[1]


_ADAM_ROWS = {352: 176, 1088: 272}


def _sum_adamw(parts, wv, m, v, name):
    n_parts, r, c = parts.shape
    tm = _ADAM_ROWS.get(r, r)
    assert r % tm == 0 and wv.shape == (r, c)

    def body(p_ref, w_ref, m_ref, v_ref, g_ref, d_ref, mo_ref, vo_ref):
        gv = p_ref[0].astype(F32)
        for j in range(1, n_parts):
            gv = gv + p_ref[j].astype(F32)
        g_ref[...] = gv
        mn = ADAM_B1 * m_ref[...] + (1.0 - ADAM_B1) * gv
        vn = ADAM_B2 * v_ref[...] + (1.0 - ADAM_B2) * (gv * gv)
        m_hat = mn / (1.0 - ADAM_B1 ** ADAM_STEP)
        v_hat = vn / (1.0 - ADAM_B2 ** ADAM_STEP)
        d_ref[...] = -ADAM_LR * (m_hat / (jnp.sqrt(v_hat) + ADAM_EPS) + ADAM_WD * w_ref[...])
        mo_ref[...] = mn
        vo_ref[...] = vn

    sp = pl.BlockSpec((tm, c), lambda i: (i, 0))
    return pl.pallas_call(
        body,
        name=name,
        grid=(r // tm,),
        in_specs=[pl.BlockSpec((n_parts, tm, c), lambda i: (0, i, 0))] + [sp] * 3,
        out_specs=[sp] * 4,
        out_shape=[jax.ShapeDtypeStruct((r, c), F32)] * 4,
        compiler_params=_cp(("parallel",), 48),
    )(parts, wv, m, v)


_WEIGHTS = ["ffn1_w_gate", "ffn1_w_up", "ffn1_w_down", "ln1_g", "ln1_b", "w_in", "b_gates", "gmlp_ln_g", "gmlp_ln_b",
            "gmlp_w_s", "gmlp_b_s", "w_attn_branch", "w_gmlp_branch", "w_out", "ln2_g", "ln2_b", "ffn2_w_gate",
            "ffn2_w_up", "ffn2_w_down", "ln3_g", "ln3_b"]
_BIG_OF = {"ffn1_w_gate": ("g1", True), "ffn1_w_up": ("u1", True), "ffn1_w_down": ("d1", False), "w_in": ("win", True),
           "w_attn_branch": ("ab", True), "w_gmlp_branch": ("gb", False), "w_out": ("o", False),
           "ffn2_w_gate": ("g2", True), "ffn2_w_up": ("u2", True), "ffn2_w_down": ("d2", False)}
_SMALL = [n for n in _WEIGHTS if n not in _BIG_OF]
_SMALL_ROWS = {"gmlp_w_s": 128, "b_gates": 2}
_SMALL_SLOT = 8


def _pack_small(d, last=None):
    rows = []
    for n in _SMALL:
        r = d[n].reshape(-1, D)
        slot = max(r.shape[0], _SMALL_SLOT)
        rows.append(jnp.pad(r, ((0, slot - r.shape[0]), (0, 0))))
    rows.append(jnp.zeros((_SMALL_SLOT, D), F32) if last is None else jnp.broadcast_to(last.reshape(1, 1), (_SMALL_SLOT, D)))
    return jnp.concatenate(rows, axis=0)


def _unpack_small(packed, shapes):
    out, at = {}, 0
    for n in _SMALL:
        k = _SMALL_ROWS.get(n, 1)
        out[n] = packed[at:at + k].reshape(shapes[n])
        at += max(k, _SMALL_SLOT)
    return out


def kernel(x, positions, ffn1_w_gate, ffn1_w_up, ffn1_w_down, ln1_g, ln1_b, w_in, b_gates, gmlp_ln_g, gmlp_ln_b, gmlp_w_s, gmlp_b_s, w_attn_branch, w_gmlp_branch, w_out, ln2_g, ln2_b, ffn2_w_gate, ffn2_w_up, ffn2_w_down, ln3_g, ln3_b, loss_target, m_ffn1_w_gate, m_ffn1_w_up, m_ffn1_w_down, m_ln1_g, m_ln1_b, m_w_in, m_b_gates, m_gmlp_ln_g, m_gmlp_ln_b, m_gmlp_w_s, m_gmlp_b_s, m_w_attn_branch, m_w_gmlp_branch, m_w_out, m_ln2_g, m_ln2_b, m_ffn2_w_gate, m_ffn2_w_up, m_ffn2_w_down, m_ln3_g, m_ln3_b, v_ffn1_w_gate, v_ffn1_w_up, v_ffn1_w_down, v_ln1_g, v_ln1_b, v_w_in, v_b_gates, v_gmlp_ln_g, v_gmlp_ln_b, v_gmlp_w_s, v_gmlp_b_s, v_w_attn_branch, v_w_gmlp_branch, v_w_out, v_ln2_g, v_ln2_b, v_ffn2_w_gate, v_ffn2_w_up, v_ffn2_w_down, v_ln3_g, v_ln3_b):
    args = dict(locals())
    wts = {n: args[n] for n in _WEIGHTS}
    ms = {n: args["m_" + n] for n in _WEIGHTS}
    vs = {n: args["v_" + n] for n in _WEIGHTS}

    name_of = {key: (n, tr) for n, (key, tr) in _BIG_OF.items()}

    shards = {}
    for grp, keys in _GROUPS.items():
        shards[grp] = []
        for key in keys:
            n, tr = name_of[key]
            s2 = wts[n][0]
            shards[grp].append((s2.T if tr else s2).astype(BF16))
    started, tok = {}, None
    for grp in _GROUPS:
        started[grp] = _gather_send(shards[grp], grp, tok)
        tok = started[grp][3]
    all_started = tok
    passed = {}

    def get_w(grp, after, early=False):
        if grp not in passed:
            passed[grp] = _gather_pass(started[grp], grp, all_started if after is None else after)
            after = None
        if early:
            return None
        lands = _gather_finish(passed[grp], grp, after)
        return {key: g.reshape(-1, g.shape[-1]) for key, g in zip(_GROUPS[grp], lands)}

    sent = {}

    def emit(grp, grads):
        if grp == "small":
            sent[grp] = _allgather_send(_pack_small(grads, last=grads["loss"]), "small_grads_send")
        else:
            parts = [grads[key].reshape(N_DEV, -1, grads[key].shape[-1]) for key in _SCATTERS[grp]]
            sent[grp] = _scatter2_send(parts[0], grp) if grp in _TWO_STAGE else _scatter_send(parts, grp)
        return sent[grp][3]

    p = {n: (wts[n][0] if n in ("gmlp_w_s", "gmlp_b_s") else wts[n]) for n in _SMALL}
    loss, dx, gp = _local_step(x[0], positions.reshape(S, 1), loss_target[0], get_w, p, emit)
    grads, deltas, new_m, new_v = {}, {}, {}, {}
    after = dx
    for grp in ("ffn2", "mix", "win", "small", "g1", "u1", "d1"):
        if grp == "small":
            parts = _allgather_finish(sent[grp], "small_grads_done", after)
            outs = _sum_adamw(parts, *[_pack_small({n: d[n] for n in _SMALL}) for d in (wts, ms, vs)], "update_small")
            shapes = {n: wts[n].shape for n in _SMALL}
            for dst, packed in zip((grads, deltas, new_m, new_v), outs):
                dst.update(_unpack_small(packed, shapes))
            loss = outs[0][-_SMALL_SLOT, 0]
            after = outs[1]
            continue
        arrived = ([_scatter2_finish(sent[grp], grp, after)] if grp in _TWO_STAGE
                   else _scatter_finish(sent[grp], grp, after))
        for key, part in zip(_SCATTERS[grp], arrived):
            n, tr = name_of[key]
            outs = _sum_adamw(part, *[(d[n][0].T if tr else d[n][0]) for d in (wts, ms, vs)], "update_" + key)
            for dst, o in zip((grads, deltas, new_m, new_v), outs):
                dst[n] = (o.T if tr else o)[None]
            after = outs[1]

    return (loss, dx[None], *[grads[n] for n in _WEIGHTS], *[deltas[n] for n in _WEIGHTS],
            *[new_m[n] for n in _WEIGHTS], *[new_v[n] for n in _WEIGHTS])
```

```python
import functools
import math

import jax
import jax.numpy as jnp
from jax import lax
from jax.experimental import pallas as pl
from jax.experimental.pallas import tpu as pltpu

F32 = jnp.float32
BF16 = jnp.bfloat16

N_DEV = 8
D = 1024
S = 2048
F = 2816
HEAD_DIM = 64
HEADS = 8
GW = HEADS * HEAD_DIM
PATTERN_DILATIONS = (1, 4, 16)
BLK = 128
QKV_W = 3 * 3 * GW
IN_W = QKV_W + 2 * D + 2 * D
DPROJ_W = 5 * 2 * D
DPROJ_Z_COL, DPROJ_G_COL = 3, 4


def _dproj_tile(t):
    three = jnp.int32(3)
    return jnp.where(t < 9, lax.rem(t, three) * 3 + lax.div(t, three), t + 3)
ROPE_THETA = 10000.0
ALPHA = 2.0 ** 0.25
LN_EPS = 1e-5
ADAM_LR, ADAM_B1, ADAM_B2, ADAM_EPS, ADAM_WD, ADAM_STEP = 0.001, 0.9, 0.999, 1e-08, 0.01, 10
NEG = -1e30
MESH = pl.DeviceIdType.MESH


def _cp(sem=None, vmem_mb=None):
    kw = {}
    if sem is not None:
        kw["dimension_semantics"] = sem
    if vmem_mb is not None:
        kw["vmem_limit_bytes"] = vmem_mb << 20
    return pltpu.CompilerParams(**kw)


def _dot_nn(a, b):
    return lax.dot_general(a, b, (((1,), (0,)), ((), ())), preferred_element_type=F32)


def _dot_nt(a, b):
    return lax.dot_general(a, b, (((1,), (1,)), ((), ())), preferred_element_type=F32)


def _dot_tn(a, b):
    return lax.dot_general(a, b, (((0,), (0,)), ((), ())), preferred_element_type=F32)


def _ln_fwd(r, g, b):
    mu = jnp.mean(r, axis=-1, keepdims=True)
    xc = r - mu
    var = jnp.mean(xc * xc, axis=-1, keepdims=True)
    rstd = lax.rsqrt(var + LN_EPS)
    xhat = xc * rstd
    return xhat * g + b, xhat, rstd


def _ln_bwd(dh, xhat, rstd, g):
    dxh = dh * g
    m1 = jnp.mean(dxh, axis=-1, keepdims=True)
    m2 = jnp.mean(dxh * xhat, axis=-1, keepdims=True)
    return rstd * (dxh - m1 - xhat * m2)


def _sigmoid(x):
    return 0.5 * jnp.tanh(0.5 * x) + 0.5


def _colsum(x):
    return jnp.sum(x, axis=0, keepdims=True)


def _matmul(a, b, mode, out_dtype, tm, tn, tk, name, b_off=0, n_out=None, dep=None, a_map=None, m_out=None):
    n_dep = 0 if dep is None else 1
    a_map = a_map or (lambda t: t)
    if mode == "nn":
        m, k = a.shape[0], b.shape[0]
        n = b.shape[1]
    elif mode == "nt":
        m, k = a.shape
        n = n_out if n_out is not None else b.shape[0]
    else:
        k, m = a.shape[0], m_out or a.shape[1]
        n = b.shape[1]
    nk = k // tk
    assert m % tm == 0 and n % tn == 0 and k % tk == 0
    dot = {"nn": _dot_nn, "nt": _dot_nt, "tn": _dot_tn}[mode]

    def body(a_ref, b_ref, *rest):
        o_ref, scr = rest[n_dep], rest[n_dep + 1:]
        r = dot(a_ref[...].astype(BF16), b_ref[...].astype(BF16))
        if nk == 1:
            o_ref[...] = r.astype(out_dtype)
        else:
            acc = scr[0]
            kk = pl.program_id(2)

            @pl.when(kk == 0)
            def _():
                acc[...] = r

            @pl.when(kk > 0)
            def _():
                acc[...] += r

            @pl.when(kk == nk - 1)
            def _():
                o_ref[...] = acc[...].astype(out_dtype)

    if mode == "nn":
        a_spec = pl.BlockSpec((tm, tk), lambda i, j, kk: (i, a_map(kk)))
        b_spec = pl.BlockSpec((tk, tn), lambda i, j, kk: (kk, j))
    elif mode == "nt":
        a_spec = pl.BlockSpec((tm, tk), lambda i, j, kk: (i, kk))
        b_spec = pl.BlockSpec((tn, tk), lambda i, j, kk: (j + b_off, kk))
    else:
        a_spec = pl.BlockSpec((tk, tm), lambda i, j, kk: (kk, a_map(i)))
        b_spec = pl.BlockSpec((tk, tn), lambda i, j, kk: (kk, j))
    return pl.pallas_call(
        body,
        name=name,
        grid=(m // tm, n // tn, nk),
        in_specs=[a_spec, b_spec] + [pl.BlockSpec(memory_space=pl.ANY)] * n_dep,
        out_specs=pl.BlockSpec((tm, tn), lambda i, j, kk: (i, j)),
        out_shape=jax.ShapeDtypeStruct((m, n), out_dtype),
        scratch_shapes=[] if nk == 1 else [pltpu.VMEM((tm, tn), F32)],
        compiler_params=_cp(("parallel", "parallel", "arbitrary"), 56),
    )(a, b, *([] if dep is None else [dep]))


def _ffn_up(x, wgt, wut, name, tm=512, tn=1408, dep=None):
    n_dep = 0 if dep is None else 1

    def body(x_ref, wg_ref, wu_ref, *rest):
        ga_ref, gb_ref, hm_ref = rest[n_dep:]
        xb = x_ref[...].astype(BF16)
        a = _dot_nt(xb, wg_ref[...])
        b = _dot_nt(xb, wu_ref[...])
        sig = _sigmoid(a)
        silu = a * sig
        ga_ref[...] = (b * (sig + silu * (1.0 - sig))).astype(BF16)
        gb_ref[...] = silu.astype(BF16)
        hm_ref[...] = (silu * b).astype(BF16)

    wsp = pl.BlockSpec((tn, D), lambda i, j: (j, 0))
    mid = pl.BlockSpec((tm, tn), lambda i, j: (i, j))
    return pl.pallas_call(
        body,
        name=name,
        grid=(S // tm, F // tn),
        in_specs=[pl.BlockSpec((tm, D), lambda i, j: (i, 0)), wsp, wsp] + [pl.BlockSpec(memory_space=pl.ANY)] * n_dep,
        out_specs=[mid, mid, mid],
        out_shape=[jax.ShapeDtypeStruct((S, F), BF16)] * 3,
        compiler_params=_cp(("parallel", "arbitrary"), 56),
    )(x, wgt, wut, *([] if dep is None else [dep]))


def _ffn_down(hm, wd, x, ln_g, ln_b, name, tm=256):
    def body(hm_ref, wd_ref, x_ref, g_ref, b_ref, h_ref, hb_ref, xh_ref, rs_ref):
        r = ALPHA * x_ref[...] + 0.5 * _dot_nn(hm_ref[...], wd_ref[...])
        h, xhat, rstd = _ln_fwd(r, g_ref[...], b_ref[...])
        h_ref[...] = h
        hb_ref[...] = h.astype(BF16)
        xh_ref[...] = xhat
        rs_ref[...] = rstd

    row = pl.BlockSpec((tm, D), lambda i: (i, 0))
    vec = pl.BlockSpec((1, D), lambda i: (0, 0))
    return pl.pallas_call(
        body,
        name=name,
        grid=(S // tm,),
        in_specs=[pl.BlockSpec((tm, F), lambda i: (i, 0)), pl.BlockSpec((F, D), lambda i: (0, 0)), row, vec, vec],
        out_specs=[row, row, row, pl.BlockSpec((tm, 1), lambda i: (i, 0))],
        out_shape=[
            jax.ShapeDtypeStruct((S, D), F32),
            jax.ShapeDtypeStruct((S, D), BF16),
            jax.ShapeDtypeStruct((S, D), F32),
            jax.ShapeDtypeStruct((S, 1), F32),
        ],
        compiler_params=_cp(("parallel",), 56),
    )(hm, wd, x, ln_g, ln_b)


def _ffn_bwd_mid(dh_a, dh_b, xhat, rstd, ln_g, a, b, wd, name, tm=512, tn=1408):
    two = dh_b is not None

    def body(*refs):
        dha_ref = refs[0]
        dhb_ref = refs[1] if two else None
        (xh_ref, rs_ref, g_ref, a_ref, b_ref, wd_ref,
         dr_ref, df_ref, da_ref, db_ref, dg_ref, dbias_ref, df_scr) = refs[2 if two else 1:]
        i = pl.program_id(0)
        j = pl.program_id(1)

        @pl.when(j == 0)
        def _():
            dh = dha_ref[...]
            if two:
                dh = ALPHA * dh + dhb_ref[...]
            xhat = xh_ref[...]
            dr = _ln_bwd(dh, xhat, rs_ref[...], g_ref[...])
            dfb = (0.5 * dr).astype(BF16)
            dr_ref[...] = dr
            df_scr[...] = dfb
            df_ref[...] = dfb
            sg = _colsum(dh * xhat)
            sb = _colsum(dh)

            @pl.when(i == 0)
            def _():
                dg_ref[...] = sg
                dbias_ref[...] = sb

            @pl.when(i > 0)
            def _():
                dg_ref[...] += sg
                dbias_ref[...] += sb

        dhm = _dot_nt(df_scr[...], wd_ref[...])
        da_ref[...] = (dhm * a_ref[...].astype(F32)).astype(BF16)
        db_ref[...] = (dhm * b_ref[...].astype(F32)).astype(BF16)

    row = pl.BlockSpec((tm, D), lambda i, j: (i, 0))
    vec = pl.BlockSpec((1, D), lambda i, j: (0, 0))
    mid = pl.BlockSpec((tm, tn), lambda i, j: (i, j))
    ins = [dh_a] + ([dh_b] if two else []) + [xhat, rstd, ln_g, a, b, wd]
    in_specs = [row] * (2 if two else 1) + [row, pl.BlockSpec((tm, 1), lambda i, j: (i, 0)), vec, mid, mid,
                                            pl.BlockSpec((tn, D), lambda i, j: (j, 0))]
    return pl.pallas_call(
        body,
        name=name,
        grid=(S // tm, F // tn),
        in_specs=in_specs,
        out_specs=[row, row, mid, mid, vec, vec],
        out_shape=[
            jax.ShapeDtypeStruct((S, D), F32),
            jax.ShapeDtypeStruct((S, D), BF16),
            jax.ShapeDtypeStruct((S, F), BF16),
            jax.ShapeDtypeStruct((S, F), BF16),
            jax.ShapeDtypeStruct((1, D), F32),
            jax.ShapeDtypeStruct((1, D), F32),
        ],
        scratch_shapes=[pltpu.VMEM((tm, D), BF16)],
        compiler_params=_cp(("arbitrary", "arbitrary"), 56),
    )(*ins)


def _ffn_bwd_dx(dr, da, db, wgt, wut, name, tm=512, tk=1408, dep=None):
    nk = F // tk
    n_dep = 0 if dep is None else 1

    def body(dr_ref, da_ref, db_ref, wg_ref, wu_ref, *rest):
        dx_ref, acc = rest[n_dep], rest[n_dep + 1]
        kk = pl.program_id(1)
        part = _dot_nn(da_ref[...], wg_ref[...]) + _dot_nn(db_ref[...], wu_ref[...])

        @pl.when(kk == 0)
        def _():
            acc[...] = ALPHA * dr_ref[...] + part

        @pl.when(kk > 0)
        def _():
            acc[...] += part

        @pl.when(kk == nk - 1)
        def _():
            dx_ref[...] = acc[...]

    row = pl.BlockSpec((tm, D), lambda i, kk: (i, 0))
    mid = pl.BlockSpec((tm, tk), lambda i, kk: (i, kk))
    wsp = pl.BlockSpec((tk, D), lambda i, kk: (kk, 0))
    return pl.pallas_call(
        body,
        name=name,
        grid=(S // tm, nk),
        in_specs=[row, mid, mid, wsp, wsp] + [pl.BlockSpec(memory_space=pl.ANY)] * n_dep,
        out_specs=row,
        out_shape=jax.ShapeDtypeStruct((S, D), F32),
        scratch_shapes=[pltpu.VMEM((tm, D), F32)],
        compiler_params=_cp(("parallel", "arbitrary"), 56),
    )(dr, da, db, wgt, wut, *([] if dep is None else [dep]))


def _rope_tables(pos_col, invf, sign, tm=512):
    def body(p_ref, f_ref, s_ref, c_out, s_out):
        ang = p_ref[...].astype(F32) * f_ref[...]
        c_out[...] = jnp.cos(ang)
        s_out[...] = jnp.sin(ang) * s_ref[...]

    vec = pl.BlockSpec((1, BLK), lambda i: (0, 0))
    out = pl.BlockSpec((tm, BLK), lambda i: (i, 0))
    return pl.pallas_call(
        body,
        name="rope_tables",
        grid=(S // tm,),
        in_specs=[pl.BlockSpec((tm, 1), lambda i: (i, 0)), vec, vec],
        out_specs=[out, out],
        out_shape=[jax.ShapeDtypeStruct((S, BLK), F32)] * 2,
        compiler_params=_cp(("parallel",)),
    )(pos_col, invf, sign)


def _lane_lo(rows=BLK):
    return (lax.broadcasted_iota(jnp.int32, (rows, GW), 1) % HEAD_DIM) < (HEAD_DIM // 2)


def _swap_halves(t, lo):
    return jnp.where(lo, pltpu.roll(t, GW - HEAD_DIM // 2, 1), pltpu.roll(t, HEAD_DIM // 2, 1))


def _rope(t, cosf, sinf, lo):
    return t * cosf + _swap_halves(t, lo) * sinf


def _unrope(g, cosf, sinf, lo):
    return g * cosf + _swap_halves(g * sinf, lo)


def _tile4(v):
    return jnp.concatenate([v, v, v, v], axis=1)


def _band_mask(n):
    qi = lax.broadcasted_iota(jnp.int32, (BLK, 2 * BLK), 0)
    kj = lax.broadcasted_iota(jnp.int32, (BLK, 2 * BLK), 1)
    dist = qi + BLK - kj
    return (dist >= 0) & (dist <= BLK) & ((kj >= BLK) | (n >= 1))


ROWS = 256
LANES = 128


def _to_view(scr, y, dst_ref, d, dtype, col0=0):
    if d == 1:
        dst_ref[:, col0:col0 + GW] = y.astype(dtype)
        return
    for cb in range(GW // LANES):
        scr[cb][...] = y[:, cb * LANES:(cb + 1) * LANES]
    for r in range(d):
        for cb in range(GW // LANES):
            at = col0 + r * GW + cb * LANES
            dst_ref[:, at:at + LANES] = scr[cb][pl.ds(r, ROWS // d, stride=d), :].astype(dtype)


def _from_view(scr, src_ref, d):
    if d == 1:
        return src_ref[...].astype(F32)
    for r in range(d):
        for cb in range(GW // LANES):
            at = r * GW + cb * LANES
            scr[cb][pl.ds(r, ROWS // d, stride=d), :] = src_ref[:, at:at + LANES].astype(F32)
    return jnp.concatenate([scr[cb][...] for cb in range(GW // LANES)], axis=1)


def _view_spec(d):
    return pl.BlockSpec((ROWS // d, d * GW), lambda i: (i, 0))


_VIEW_SCRATCH = [pltpu.VMEM((ROWS, LANES), F32)] * (GW // LANES)


def _qkv_prep(qkv, cos_t, sin_t):
    def body(x_ref, c_ref, s_ref, *rest):
        outs, scr = rest[:9], rest[9:]
        lo = _lane_lo(ROWS)
        cf, sf = _tile4(c_ref[...]), _tile4(s_ref[...])
        for gi, d in enumerate(PATTERN_DILATIONS):
            q, k, v = (x_ref[:, (3 * part + gi) * GW:(3 * part + gi + 1) * GW] for part in range(3))
            _to_view(scr, _rope(q, cf, sf, lo) * (HEAD_DIM ** -0.5), outs[3 * gi], d, BF16)
            _to_view(scr, _rope(k, cf, sf, lo), outs[3 * gi + 1], d, BF16)
            _to_view(scr, v, outs[3 * gi + 2], d, BF16)

    tab = pl.BlockSpec((ROWS, BLK), lambda i: (i, 0))
    outs = pl.pallas_call(
        body,
        name="qkv_prep",
        grid=(S // ROWS,),
        in_specs=[pl.BlockSpec((ROWS, QKV_W), lambda i: (i, 0)), tab, tab],
        out_specs=[_view_spec(d) for d in PATTERN_DILATIONS for _ in range(3)],
        out_shape=[jax.ShapeDtypeStruct((S // d, d * GW), BF16) for d in PATTERN_DILATIONS for _ in range(3)],
        scratch_shapes=_VIEW_SCRATCH,
        compiler_params=_cp(("parallel",), 48),
    )(qkv, cos_t, sin_t)
    return [tuple(outs[3 * gi:3 * gi + 3]) for gi in range(3)]


N_ATTN_STEPS = S // BLK


def _class_and_block(gi, t):
    per_class = N_ATTN_STEPS // PATTERN_DILATIONS[gi]
    return t >> (per_class.bit_length() - 1), t & (per_class - 1)


def _attn_specs(gi):
    def cur(t):
        r, n = _class_and_block(gi, t)
        return n, r

    def prev(t):
        r, n = _class_and_block(gi, t)
        return jnp.maximum(n - 1, 0), r

    def whole(t):
        return 0, _class_and_block(gi, t)[0]

    sub = S // PATTERN_DILATIONS[gi]
    return (pl.BlockSpec((BLK, GW), cur), pl.BlockSpec((BLK, GW), prev), pl.BlockSpec((sub, GW), whole))


def _left_lanes():
    return lax.broadcasted_iota(jnp.int32, (BLK, LANES), 1) < HEAD_DIM


def _stack_heads(t2, left):
    zero = jnp.zeros_like(t2)
    return jnp.concatenate([jnp.where(left, t2, zero), jnp.where(left, zero, t2)], axis=0)


def _attn_fwd_block(n, q_ref, kc_ref, kp_ref, vc_ref, vp_ref, o_ref, l_ref):
    mask = _band_mask(n)
    left = _left_lanes()
    left2 = jnp.concatenate([left, left], axis=0)
    for pr in range(GW // LANES):
        cs = slice(pr * LANES, (pr + 1) * LANES)
        k_bd = _stack_heads(jnp.concatenate([kp_ref[:, cs], kc_ref[:, cs]], axis=0), left2)
        v_bd = _stack_heads(jnp.concatenate([vp_ref[:, cs], vc_ref[:, cs]], axis=0), left2)
        s2 = _dot_nt(q_ref[:, cs], k_bd)
        ps, lses = [], []
        for h in range(2):
            s = jnp.where(mask, s2[:, h * 2 * BLK:(h + 1) * 2 * BLK], NEG)
            m = jnp.max(s, axis=1, keepdims=True)
            p = jnp.exp(s - m)
            l = jnp.sum(p, axis=1, keepdims=True)
            ps.append((p / l).astype(BF16))
            lses.append(m + jnp.log(l))
        o_ref[:, cs] = _dot_nn(jnp.concatenate(ps, axis=1), v_bd)
        l_ref[:, cs] = jnp.where(left, lses[0], lses[1])


def _attn_fwd(preps):
    def body(*refs):
        t = pl.program_id(0)
        for gi in range(3):
            _attn_fwd_block(_class_and_block(gi, t)[1], *refs[5 * gi:5 * gi + 5], *refs[15 + 2 * gi:17 + 2 * gi])

    in_specs, out_specs, out_shape, args = [], [], [], []
    for gi, d in enumerate(PATTERN_DILATIONS):
        cur, prev, _ = _attn_specs(gi)
        qr, kr, vv = preps[gi]
        in_specs += [cur, cur, prev, cur, prev]
        args += [qr, kr, kr, vv, vv]
        out_specs += [cur, cur]
        out_shape += [jax.ShapeDtypeStruct((S // d, d * GW), F32)] * 2
    outs = pl.pallas_call(
        body,
        name="attn_fwd",
        grid=(N_ATTN_STEPS,),
        in_specs=in_specs,
        out_specs=out_specs,
        out_shape=out_shape,
        compiler_params=_cp(("arbitrary",), 48),
    )(*args)
    return [outs[0], outs[2], outs[4]], [outs[1], outs[3], outs[5]]


def _attn_bwd_block(n, q_ref, kc_ref, kp_ref, vc_ref, vp_ref, do_ref, l_ref, c_ref, dq_ref, dk_ref, dv_ref):
    mask = _band_mask(n)
    left = _left_lanes()
    left2 = jnp.concatenate([left, left], axis=0)
    here = pl.ds(pl.multiple_of(n * BLK, BLK), BLK)
    before = pl.ds(pl.multiple_of(jnp.maximum(n - 1, 0) * BLK, BLK), BLK)
    for pr in range(GW // LANES):
        cs = slice(pr * LANES, (pr + 1) * LANES)
        q2 = q_ref[:, cs]
        do2 = do_ref[:, cs]
        k_bd = _stack_heads(jnp.concatenate([kp_ref[:, cs], kc_ref[:, cs]], axis=0), left2)
        v_bd = _stack_heads(jnp.concatenate([vp_ref[:, cs], vc_ref[:, cs]], axis=0), left2)
        s2 = _dot_nt(q2, k_bd)
        dp2 = _dot_nt(do2, v_bd)
        ps, dss = [], []
        for h in range(2):
            at = pr * LANES + h * HEAD_DIM
            half = slice(h * 2 * BLK, (h + 1) * 2 * BLK)
            p = jnp.exp(jnp.where(mask, s2[:, half], NEG) - l_ref[:, at:at + 1])
            ps.append(p.astype(BF16))
            dss.append((p * (dp2[:, half] - c_ref[:, at:at + 1])).astype(BF16))
        ds_both = jnp.concatenate(dss, axis=1)
        dq_ref[:, cs] = _dot_nn(ds_both, k_bd) * (HEAD_DIM ** -0.5)
        dk_bd = _dot_tn(ds_both, q2)
        dv_bd = _dot_tn(jnp.concatenate(ps, axis=1), do2)
        dk2 = jnp.where(left2, dk_bd[:2 * BLK], dk_bd[2 * BLK:])
        dv2 = jnp.where(left2, dv_bd[:2 * BLK], dv_bd[2 * BLK:])
        dk_ref[here, cs] = dk2[BLK:]
        dv_ref[here, cs] = dv2[BLK:]
        dk_ref[before, cs] += dk2[:BLK]
        dv_ref[before, cs] += dv2[:BLK]


def _attn_bwd(preps, dos, lses, cterms):
    def body(*refs):
        t = pl.program_id(0)
        for gi in range(3):
            _attn_bwd_block(_class_and_block(gi, t)[1], *refs[8 * gi:8 * gi + 8], *refs[24 + 3 * gi:27 + 3 * gi])

    in_specs, out_specs, out_shape, args = [], [], [], []
    for gi, d in enumerate(PATTERN_DILATIONS):
        cur, prev, whole = _attn_specs(gi)
        qr, kr, vv = preps[gi]
        in_specs += [cur, cur, prev, cur, prev, cur, cur, cur]
        args += [qr, kr, kr, vv, vv, dos[gi], lses[gi], cterms[gi]]
        out_specs += [cur, whole, whole]
        out_shape += [jax.ShapeDtypeStruct((S // d, d * GW), F32)] * 3
    outs = pl.pallas_call(
        body,
        name="attn_bwd",
        grid=(N_ATTN_STEPS,),
        in_specs=in_specs,
        out_specs=out_specs,
        out_shape=out_shape,
        compiler_params=_cp(("arbitrary",), 56),
    )(*args)
    return [tuple(outs[3 * gi:3 * gi + 3]) for gi in range(3)]


def _qkv_unprep(grads, cos_t, sin_t, dproj):
    def body(*refs):
        views, (c_ref, s_ref, _, out_ref), scr = refs[:9], refs[9:13], refs[13:]
        lo = _lane_lo(ROWS)
        cf, sf = _tile4(c_ref[...]), _tile4(s_ref[...])
        for gi, d in enumerate(PATTERN_DILATIONS):
            dq_ref, dk_ref, dv_ref = views[3 * gi:3 * gi + 3]
            at = 3 * gi * GW
            out_ref[:, at:at + GW] = _unrope(_from_view(scr, dq_ref, d), cf, sf, lo).astype(BF16)
            out_ref[:, at + GW:at + 2 * GW] = _unrope(_from_view(scr, dk_ref, d), cf, sf, lo).astype(BF16)
            out_ref[:, at + 2 * GW:at + 3 * GW] = _from_view(scr, dv_ref, d).astype(BF16)

    tab = pl.BlockSpec((ROWS, BLK), lambda i: (i, 0))
    return pl.pallas_call(
        body,
        name="qkv_unprep",
        grid=(S // ROWS,),
        in_specs=[_view_spec(d) for d in PATTERN_DILATIONS for _ in range(3)] + [tab, tab, pl.BlockSpec(memory_space=pl.ANY)],
        out_specs=pl.BlockSpec((ROWS, QKV_W), lambda i: (i, 0)),
        out_shape=jax.ShapeDtypeStruct(dproj.shape, dproj.dtype),
        input_output_aliases={11: 0},
        scratch_shapes=_VIEW_SCRATCH,
        compiler_params=_cp(("parallel",), 48),
    )(*[g for grp in grads for g in grp], cos_t, sin_t, dproj)


def _group_weights(l0, l1, l2):
    mx = jnp.maximum(jnp.maximum(l0, l1), l2)
    e0, e1, e2 = jnp.exp(l0 - mx), jnp.exp(l1 - mx), jnp.exp(l2 - mx)
    inv = 1.0 / (e0 + e1 + e2)
    return e0 * inv, e1 * inv, e2 * inv


def _combine_fwd(os_, lses):
    def body(o0, o1, o2, l0, l1, l2, y_ref, *scr):
        ov = [_from_view(scr, o, d) for o, d in zip((o0, o1, o2), PATTERN_DILATIONS)]
        lv = [_from_view(scr, l, d) for l, d in zip((l0, l1, l2), PATTERN_DILATIONS)]
        w0, w1, w2 = _group_weights(*lv)
        y_ref[...] = (w0 * ov[0] + w1 * ov[1] + w2 * ov[2]).astype(BF16)

    views = [_view_spec(d) for d in PATTERN_DILATIONS]
    return pl.pallas_call(
        body,
        name="attn_combine_fwd",
        grid=(S // ROWS,),
        in_specs=views + views,
        out_specs=pl.BlockSpec((ROWS, GW), lambda i: (i, 0)),
        out_shape=jax.ShapeDtypeStruct((S, GW), BF16),
        scratch_shapes=_VIEW_SCRATCH,
        compiler_params=_cp(("parallel",)),
    )(*os_, *lses)


def _combine_bwd(dy, os_, lses, seg):
    def body(dy_ref, o0, o1, o2, l0, l1, l2, seg_ref, d0, d1, d2, c0, c1, c2, *scr):
        ov = [_from_view(scr, o, d) for o, d in zip((o0, o1, o2), PATTERN_DILATIONS)]
        lv = [_from_view(scr, l, d) for l, d in zip((l0, l1, l2), PATTERN_DILATIONS)]
        ws = _group_weights(*lv)
        dyv = dy_ref[...]
        t = dyv * (ws[0] * ov[0] + ws[1] * ov[1] + ws[2] * ov[2])
        t_hi = t.astype(BF16)
        r1 = t - t_hi.astype(F32)
        t_mid = r1.astype(BF16)
        t_lo = (r1 - t_mid.astype(F32)).astype(BF16)
        sg = seg_ref[...]
        e = _dot_nn(t_hi, sg) + _dot_nn(t_mid, sg) + _dot_nn(t_lo, sg)
        for w, do_ref, c_ref, d in zip(ws, (d0, d1, d2), (c0, c1, c2), PATTERN_DILATIONS):
            _to_view(scr, w * dyv, do_ref, d, BF16)
            _to_view(scr, w * e, c_ref, d, F32)

    views = [_view_spec(d) for d in PATTERN_DILATIONS]
    return pl.pallas_call(
        body,
        name="attn_combine_bwd",
        grid=(S // ROWS,),
        in_specs=[pl.BlockSpec((ROWS, GW), lambda i: (i, 0))] + views + views + [pl.BlockSpec((GW, GW), lambda i: (0, 0))],
        out_specs=views + views,
        out_shape=[jax.ShapeDtypeStruct((S // d, d * GW), BF16) for d in PATTERN_DILATIONS]
        + [jax.ShapeDtypeStruct((S // d, d * GW), F32) for d in PATTERN_DILATIONS],
        scratch_shapes=_VIEW_SCRATCH,
        compiler_params=_cp(("parallel",)),
    )(dy, *os_, *lses, seg)


_SQRT_HALF = 0.7071067811865476
_INV_SQRT_2PI = 0.3989422804014327


def _gelu(z):
    return 0.5 * z * (1.0 + lax.erf(z * _SQRT_HALF))


def _gelu_grad(z):
    return 0.5 * (1.0 + lax.erf(z * _SQRT_HALF)) + z * (_INV_SQRT_2PI * jnp.exp(-0.5 * z * z))


def _tril_ws(ws_ref, g):
    t = lax.broadcasted_iota(jnp.int32, (BLK, BLK), 0)
    s = lax.broadcasted_iota(jnp.int32, (BLK, BLK), 1)
    return jnp.where(t >= s, ws_ref[g], 0.0)


def _gmlp_fwd(z, ws, bst, ln_g, ln_b, tm=256):
    nch = tm // BLK

    def body(z_ref, ws_ref, b_ref, g_ref, be_ref, y_ref):
        zg = _gelu(z_ref[...])
        u = zg[:, :D]
        vn, _, _ = _ln_fwd(zg[:, D:], g_ref[...], be_ref[...])
        vnb = vn.astype(BF16)
        bt = b_ref[...]
        for g in range(8):
            w = _tril_ws(ws_ref, g).astype(BF16)
            cols = slice(g * BLK, (g + 1) * BLK)
            for c in range(nch):
                rows = slice(c * BLK, (c + 1) * BLK)
                mixed = _dot_nn(w, vnb[rows, cols]) + bt[:, g:g + 1]
                y_ref[rows, cols] = (u[rows, cols] * mixed).astype(BF16)

    return pl.pallas_call(
        body,
        name="gmlp_fwd",
        grid=(S // tm,),
        in_specs=[
            pl.BlockSpec((tm, 2 * D), lambda i: (i, 0)),
            pl.BlockSpec((8, BLK, BLK), lambda i: (0, 0, 0)),
            pl.BlockSpec((BLK, 8), lambda i: (0, 0)),
            pl.BlockSpec((1, D), lambda i: (0, 0)),
            pl.BlockSpec((1, D), lambda i: (0, 0)),
        ],
        out_specs=pl.BlockSpec((tm, D), lambda i: (i, 0)),
        out_shape=jax.ShapeDtypeStruct((S, D), BF16),
        compiler_params=_cp(("parallel",), 48),
    )(z, ws, bst, ln_g, ln_b)


def _gmlp_bwd(z, dy, ws, bst, ln_g, ln_b, dproj, tm=256):
    nch = tm // BLK

    def body(z_ref, dy_ref, ws_ref, b_ref, g_ref, be_ref, _, dz_ref, dws_ref, dbs_ref, dg_ref, dbe_ref, dvn_scr, dm_acc):
        i = pl.program_id(0)
        zv = z_ref[...]
        zg = _gelu(zv)
        u = zg[:, :D]
        gam = g_ref[...]
        vn, xhat, rstd = _ln_fwd(zg[:, D:], gam, be_ref[...])
        vnb = vn.astype(BF16)
        dyv = dy_ref[...]
        dmix = dyv * u
        dmb = dmix.astype(BF16)
        bt = b_ref[...]
        tmask = lax.broadcasted_iota(jnp.int32, (BLK, BLK), 0) >= lax.broadcasted_iota(jnp.int32, (BLK, BLK), 1)
        dm_sum = dmix[0:BLK]
        for c in range(1, nch):
            dm_sum = dm_sum + dmix[c * BLK:(c + 1) * BLK]

        @pl.when(i == 0)
        def _():
            dm_acc[...] = dm_sum

        @pl.when(i > 0)
        def _():
            dm_acc[...] += dm_sum

        dus = []
        for g in range(8):
            w = _tril_ws(ws_ref, g).astype(BF16)
            cols = slice(g * BLK, (g + 1) * BLK)
            dw = None
            du_rows = []
            for c in range(nch):
                rows = slice(c * BLK, (c + 1) * BLK)
                mixed = _dot_nn(w, vnb[rows, cols]) + bt[:, g:g + 1]
                du_rows.append(dyv[rows, cols] * mixed)
                part = _dot_nt(dmb[rows, cols], vnb[rows, cols])
                dw = part if dw is None else dw + part
                dvn_scr[rows, cols] = _dot_tn(w, dmb[rows, cols])
            dus.append(jnp.concatenate(du_rows, axis=0))
            dw = jnp.where(tmask, dw, 0.0)

            @pl.when(i == 0)
            def _():
                dws_ref[g] = dw

            @pl.when(i > 0)
            def _():
                dws_ref[g] += dw

        dvn = dvn_scr[...]
        sg = _colsum(dvn * xhat)
        sb = _colsum(dvn)

        @pl.when(i == 0)
        def _():
            dg_ref[...] = sg
            dbe_ref[...] = sb

        @pl.when(i > 0)
        def _():
            dg_ref[...] += sg
            dbe_ref[...] += sb

        dvg = _ln_bwd(dvn, xhat, rstd, gam)
        gp = _gelu_grad(zv)
        dz_ref[:, :D] = (jnp.concatenate(dus, axis=1) * gp[:, :D]).astype(BF16)
        dz_ref[:, D:] = (dvg * gp[:, D:]).astype(BF16)

        @pl.when(i == S // tm - 1)
        def _():
            acc = dm_acc[...]
            for g in range(8):
                dbs_ref[:, g:g + 1] = jnp.sum(acc[:, g * BLK:(g + 1) * BLK], axis=1, keepdims=True)

    vec = pl.BlockSpec((1, D), lambda i: (0, 0))
    return pl.pallas_call(
        body,
        name="gmlp_bwd",
        grid=(S // tm,),
        in_specs=[
            pl.BlockSpec((tm, 2 * D), lambda i: (i, 0)),
            pl.BlockSpec((tm, D), lambda i: (i, 0)),
            pl.BlockSpec((8, BLK, BLK), lambda i: (0, 0, 0)),
            pl.BlockSpec((BLK, 8), lambda i: (0, 0)),
            vec,
            vec,
            pl.BlockSpec(memory_space=pl.ANY),
        ],
        out_specs=[
            pl.BlockSpec((tm, 2 * D), lambda i: (i, DPROJ_Z_COL)),
            pl.BlockSpec((8, BLK, BLK), lambda i: (0, 0, 0)),
            pl.BlockSpec((BLK, 8), lambda i: (0, 0)),
            vec,
            vec,
        ],
        out_shape=[
            jax.ShapeDtypeStruct(dproj.shape, dproj.dtype),
            jax.ShapeDtypeStruct((8, BLK, BLK), F32),
            jax.ShapeDtypeStruct((BLK, 8), F32),
            jax.ShapeDtypeStruct((1, D), F32),
            jax.ShapeDtypeStruct((1, D), F32),
        ],
        input_output_aliases={6: 0},
        scratch_shapes=[pltpu.VMEM((tm, D), F32), pltpu.VMEM((BLK, D), F32)],
        compiler_params=_cp(("arbitrary",), 48),
    )(z, dy, ws, bst, ln_g, ln_b, dproj)


def _merge_fwd(ya, yg, glog, bgate, h1, wabt, wgb, wo, ln_g, ln_b, tm=256):
    def body(ya_ref, yg_ref, gl_ref, bg_ref, h1_ref, wab_ref, wgb_ref, wo_ref, g_ref, b_ref,
             h_ref, hb_ref, xh_ref, rs_ref, mg_ref, bra_ref, brg_ref):
        bra = _dot_nt(ya_ref[...], wab_ref[...])
        brg = _dot_nn(yg_ref[...], wgb_ref[...])
        gates = _sigmoid(gl_ref[...] + bg_ref[...])
        merged = (gates[:, :D] * bra + gates[:, D:] * brg).astype(BF16)
        mix = _dot_nn(merged, wo_ref[...])
        h, xhat, rstd = _ln_fwd(ALPHA * h1_ref[...] + mix, g_ref[...], b_ref[...])
        h_ref[...] = h
        hb_ref[...] = h.astype(BF16)
        xh_ref[...] = xhat
        rs_ref[...] = rstd
        mg_ref[...] = merged
        bra_ref[...] = bra
        brg_ref[...] = brg

    row = pl.BlockSpec((tm, D), lambda i: (i, 0))
    vec = pl.BlockSpec((1, D), lambda i: (0, 0))
    full = lambda shape: pl.BlockSpec(shape, lambda i: (0, 0))
    return pl.pallas_call(
        body,
        name="merge_fwd",
        grid=(S // tm,),
        in_specs=[
            pl.BlockSpec((tm, GW), lambda i: (i, 0)), row,
            pl.BlockSpec((tm, 2 * D), lambda i: (i, glog.shape[1] // (2 * D) - 1)),
            full((1, 2 * D)), row,
            full((D, GW)), full((D, D)), full((D, D)), vec, vec,
        ],
        out_specs=[row, row, row, pl.BlockSpec((tm, 1), lambda i: (i, 0)), row, row, row],
        out_shape=[
            jax.ShapeDtypeStruct((S, D), F32),
            jax.ShapeDtypeStruct((S, D), BF16),
            jax.ShapeDtypeStruct((S, D), F32),
            jax.ShapeDtypeStruct((S, 1), F32),
            jax.ShapeDtypeStruct((S, D), BF16),
            jax.ShapeDtypeStruct((S, D), F32),
            jax.ShapeDtypeStruct((S, D), F32),
        ],
        compiler_params=_cp(("parallel",), 48),
    )(ya, yg, glog, bgate, h1, wabt, wgb, wo, ln_g, ln_b)


def _merge_bwd(dh2, xhat, rstd, ln_g, bra, brg, glog, bgate, wabt, wgb, wo, tm=256):
    def body(dh_ref, xh_ref, rs_ref, g_ref, bra_ref, brg_ref, gl_ref, bg_ref, wab_ref, wgb_ref, wo_ref,
             dr_ref, drb_ref, dlog_ref, dba_ref, dbg_ref, dya_ref, dyg_ref, dbgate_ref, dg_ref, dbias_ref):
        i = pl.program_id(0)
        dh = dh_ref[...]
        xh = xh_ref[...]
        dr = _ln_bwd(dh, xh, rs_ref[...], g_ref[...])
        drb = dr.astype(BF16)
        dr_ref[...] = dr
        drb_ref[...] = drb
        dmerged = _dot_nt(drb, wo_ref[...])
        gates = _sigmoid(gl_ref[...] + bg_ref[...])
        g0, g1 = gates[:, :D], gates[:, D:]
        dl0 = dmerged * bra_ref[...] * g0 * (1.0 - g0)
        dl1 = dmerged * brg_ref[...] * g1 * (1.0 - g1)
        dlog_ref[:, :D] = dl0.astype(BF16)
        dlog_ref[:, D:] = dl1.astype(BF16)
        dba = (dmerged * g0).astype(BF16)
        dbg = (dmerged * g1).astype(BF16)
        dba_ref[...] = dba
        dbg_ref[...] = dbg
        dya_ref[...] = _dot_nn(dba, wab_ref[...])
        dyg_ref[...] = _dot_nt(dbg, wgb_ref[...])
        s0, s1 = _colsum(dl0), _colsum(dl1)
        sg, sb = _colsum(dh * xh), _colsum(dh)

        @pl.when(i == 0)
        def _():
            dbgate_ref[:, :D] = s0
            dbgate_ref[:, D:] = s1
            dg_ref[...] = sg
            dbias_ref[...] = sb

        @pl.when(i > 0)
        def _():
            dbgate_ref[:, :D] += s0
            dbgate_ref[:, D:] += s1
            dg_ref[...] += sg
            dbias_ref[...] += sb

    row = pl.BlockSpec((tm, D), lambda i: (i, 0))
    vec = pl.BlockSpec((1, D), lambda i: (0, 0))
    wide = pl.BlockSpec((tm, 2 * D), lambda i: (i, 0))
    full = lambda shape: pl.BlockSpec(shape, lambda i: (0, 0))
    return pl.pallas_call(
        body,
        name="merge_bwd",
        grid=(S // tm,),
        in_specs=[row, row, pl.BlockSpec((tm, 1), lambda i: (i, 0)), vec, row, row,
                  pl.BlockSpec((tm, 2 * D), lambda i: (i, glog.shape[1] // (2 * D) - 1)),
                  full((1, 2 * D)), full((D, GW)), full((D, D)), full((D, D))],
        out_specs=[row, row, pl.BlockSpec((tm, 2 * D), lambda i: (i, DPROJ_G_COL)), row, row,
                   pl.BlockSpec((tm, GW), lambda i: (i, 0)), row, full((1, 2 * D)), vec, vec],
        out_shape=[
            jax.ShapeDtypeStruct((S, D), F32),
            jax.ShapeDtypeStruct((S, D), BF16),
            jax.ShapeDtypeStruct((S, DPROJ_W), BF16),
            jax.ShapeDtypeStruct((S, D), BF16),
            jax.ShapeDtypeStruct((S, D), BF16),
            jax.ShapeDtypeStruct((S, GW), F32),
            jax.ShapeDtypeStruct((S, D), F32),
            jax.ShapeDtypeStruct((1, 2 * D), F32),
            jax.ShapeDtypeStruct((1, D), F32),
            jax.ShapeDtypeStruct((1, D), F32),
        ],
        compiler_params=_cp(("arbitrary",), 48),
    )(dh2, xhat, rstd, ln_g, bra, brg, glog, bgate, wabt, wgb, wo)


def _loss_head(h3, target, tm=512):
    def body(h_ref, t_ref, d_ref, l_ref):
        i = pl.program_id(0)
        e = h_ref[...] - t_ref[...]
        d_ref[...] = e * (1.0 / D)
        part = jnp.sum(_colsum(e * e), axis=1, keepdims=True) * (0.5 / D)

        @pl.when(i == 0)
        def _():
            l_ref[...] = part

        @pl.when(i > 0)
        def _():
            l_ref[...] += part

    row = pl.BlockSpec((tm, D), lambda i: (i, 0))
    return pl.pallas_call(
        body,
        name="loss_head",
        grid=(S // tm,),
        in_specs=[row, row],
        out_specs=[row, pl.BlockSpec((1, 1), lambda i: (0, 0))],
        out_shape=[jax.ShapeDtypeStruct((S, D), F32), jax.ShapeDtypeStruct((1, 1), F32)],
        compiler_params=_cp(("arbitrary",)),
    )(h3, target)


def _tie(x, dep):
    if dep is None:
        return x
    return x + dep[0, 0].astype(x.dtype)


def _local_step(x, pos_col, target, get_w, p, emit):
    w = get_w("ffn1", None)
    a1, b1, hm1 = _ffn_up(x, w["g1"], w["u1"], "ffn1_up")
    w.update(get_w("ffn1d", hm1))
    h1, h1b, xh1, rs1 = _ffn_down(hm1, w["d1"], x, p["ln1_g"], p["ln1_b"], "ffn1_down")

    w.update(get_w("win", h1b))
    qkv = _matmul(h1b, w["win"], "nt", F32, 1024, 1536, D, "proj_qkv", b_off=0, n_out=QKV_W, dep=w.get("_dep"))
    z = glog = _matmul(h1b, w["win"], "nt", F32, 1024, 512, D, "proj_zg", b_off=QKV_W // 512, n_out=4 * D)

    half = jnp.arange(0, HEAD_DIM, 2, dtype=F32) / HEAD_DIM
    inv_freq = ROPE_THETA ** (-half)
    invf = jnp.tile(inv_freq, 4).reshape(1, BLK)
    sign = jnp.tile(jnp.concatenate([-jnp.ones((32,), F32), jnp.ones((32,), F32)]), 2).reshape(1, BLK)
    cos_t, sin_t = _rope_tables(pos_col, invf, sign)

    get_w("mix", qkv, early=True)
    preps = _qkv_prep(qkv, cos_t, sin_t)
    os_, lses = _attn_fwd(preps)
    ya = _combine_fwd(os_, lses)
    bst = p["gmlp_b_s"].T
    yg = _gmlp_fwd(z, p["gmlp_w_s"], bst, p["gmlp_ln_g"], p["gmlp_ln_b"])
    w.update(get_w("mix", yg))
    get_w("ffn2", ya, early=True)
    h2, h2b, xh2, rs2, merged, bra, brg = _merge_fwd(ya, yg, glog, p["b_gates"], h1, w["ab"], w["gb"], w["o"],
                                                      p["ln2_g"], p["ln2_b"])
    w.update(get_w("ffn2", h2b))
    a2, b2, hm2 = _ffn_up(h2b, w["g2"], w["u2"], "ffn2_up")
    h3, _, xh3, rs3 = _ffn_down(hm2, w["d2"], h2, p["ln3_g"], p["ln3_b"], "ffn2_down")
    dh3, loss = _loss_head(h3, target)

    gp = {}
    dr3, df2, da2, db2, gp["ln3_g"], gp["ln3_b"] = _ffn_bwd_mid(dh3, None, xh3, rs3, p["ln3_g"], a2, b2, w["d2"],
                                                                "ffn2_bwd_mid")
    tok = emit("ffn2", {
        "g2": _matmul(da2, h2b, "tn", BF16, 1408, D, S, "wgrad_g2"),
        "u2": _matmul(db2, h2b, "tn", BF16, 1408, D, S, "wgrad_u2"),
        "d2": _matmul(hm2, df2, "tn", BF16, 1408, D, S, "wgrad_d2")})
    dh2 = _ffn_bwd_dx(dr3, da2, db2, w["g2"], w["u2"], "ffn2_bwd_dx")

    (dr2, dr2b, dproj, dba, dbg, dya, dyg, gp["b_gates"], gp["ln2_g"], gp["ln2_b"]) = _merge_bwd(
        dh2, xh2, rs2, _tie(p["ln2_g"], tok), bra, brg, glog, p["b_gates"], w["ab"], w["gb"], w["o"])
    tok = emit("mix", {
        "o": _matmul(merged, dr2b, "tn", BF16, 512, D, S, "wgrad_o"),
        "ab": _matmul(dba, ya, "tn", BF16, 512, GW, S, "wgrad_ab"),
        "gb": _matmul(yg, dbg, "tn", BF16, 512, D, S, "wgrad_gb")})

    seg = (jnp.arange(GW)[:, None] // HEAD_DIM == jnp.arange(GW)[None, :] // HEAD_DIM).astype(BF16)
    do0, do1, do2, c0, c1, c2 = _combine_bwd(dya, os_, lses, _tie(seg, tok))
    dproj = _qkv_unprep(_attn_bwd(preps, (do0, do1, do2), lses, (c0, c1, c2)), cos_t, sin_t, dproj)
    dproj, gp["gmlp_w_s"], dbst, gp["gmlp_ln_g"], gp["gmlp_ln_b"] = _gmlp_bwd(
        z, dyg, p["gmlp_w_s"], bst, p["gmlp_ln_g"], p["gmlp_ln_b"], dproj)
    gp["gmlp_b_s"] = dbst.T
    tok = emit("win", {"win": _matmul(dproj, h1b, "tn", BF16, 512, D, S, "wgrad_win", a_map=_dproj_tile, m_out=IN_W)})
    dh1m = _matmul(dproj, w["win"], "nn", F32, S, D, 512, "dproj_to_dh1", dep=tok, a_map=_dproj_tile)

    dr1, df1, da1, db1, gp["ln1_g"], gp["ln1_b"] = _ffn_bwd_mid(dr2, dh1m, xh1, rs1, p["ln1_g"], a1, b1, w["d1"],
                                                                "ffn1_bwd_mid")
    tok = emit("small", {**gp, "loss": loss})
    tok = emit("g1", {"g1": _matmul(da1, x, "tn", BF16, 1408, D, S, "wgrad_g1", dep=tok)})
    tok = emit("u1", {"u1": _matmul(db1, x, "tn", BF16, 1408, D, S, "wgrad_u1", dep=tok)})
    tok = emit("d1", {"d1": _matmul(hm1, df1, "tn", BF16, 1408, D, S, "wgrad_d1", dep=tok)})
    dx = _ffn_bwd_dx(dr1, da1, db1, w["g1"], w["u1"], "ffn1_bwd_dx", dep=tok)
    return loss, dx, gp


_FLIPS = [(mx, my, mc) for mx in (0, 1) for my in (0, 1) for mc in (0, 1)][1:]
_GROUPS = {"ffn1": ("g1", "u1"), "ffn1d": ("d1",), "win": ("win",), "mix": ("ab", "gb", "o"), "ffn2": ("g2", "u2", "d2")}
_SCATTERS = {"ffn2": ("g2", "u2", "d2"), "mix": ("ab", "gb", "o"), "win": ("win",), "g1": ("g1",), "u1": ("u1",), "d1": ("d1",)}
_TWO_STAGE = ("g1", "u1", "d1")
_HBM = pl.BlockSpec(memory_space=pltpu.HBM)
_SEM = pl.BlockSpec(memory_space=pltpu.SEMAPHORE)
_EFFECT = pltpu.SideEffectType.DATAFLOW_SIDE_EFFECTING


def _me():
    return 4 * lax.axis_index("x") + 2 * lax.axis_index("y") + lax.axis_index("c")


def _copies_start(bufs, copies, n_sem, name, after=None):
    nb = len(bufs)
    n_after = 0 if after is None else 1

    def body(*refs):
        b = refs[:nb]
        send_sems, recv_sems = refs[nb + n_after], refs[nb + n_after + 1]
        token = refs[-1]
        x, y, c = lax.axis_index("x"), lax.axis_index("y"), lax.axis_index("c")
        me = 4 * x + 2 * y + c
        for si, s_slot, di, d_slot, (mx, my, mc), sem in copies:
            s_idx, d_idx = s_slot(me), d_slot(me)
            pltpu.make_async_remote_copy(
                src_ref=b[si] if s_idx is None else b[si].at[s_idx],
                dst_ref=b[di] if d_idx is None else b[di].at[d_idx],
                send_sem=send_sems.at[sem], recv_sem=recv_sems.at[sem],
                device_id=(x ^ mx, y ^ my, c ^ mc), device_id_type=MESH).start()
        token[...] = jnp.zeros_like(token)

    ins = [pltpu.with_memory_space_constraint(a, pltpu.HBM) for a in bufs]
    outs = pl.pallas_call(
        body,
        name=name,
        in_specs=[_HBM] * nb + [pl.BlockSpec(memory_space=pl.ANY)] * n_after,
        out_specs=[_SEM, _SEM] + [_HBM] * nb + [pl.BlockSpec(memory_space=pltpu.VMEM)],
        out_shape=[pltpu.SemaphoreType.DMA((n_sem,)), pltpu.SemaphoreType.DMA((n_sem,))]
        + [pltpu.HBM(a.shape, a.dtype) for a in ins] + [jax.ShapeDtypeStruct((8, 128), F32)],
        input_output_aliases={k: 2 + k for k in range(nb)},
        compiler_params=pltpu.CompilerParams(has_side_effects=_EFFECT),
    )(*ins, *([] if after is None else [after]))
    return outs[0], outs[1], list(outs[2:2 + nb]), outs[-1]


def _copies_wait(started, waits, after, name):
    send_sems, recv_sems, bufs, _ = started
    nb = len(bufs)

    def body(*refs):
        b = refs[:nb]
        ss, rs = refs[nb], refs[nb + 1]
        me3 = (lax.axis_index("x"), lax.axis_index("y"), lax.axis_index("c"))
        for bi, n_blocks, sem, is_send in waits:
            blocks = b[bi].at[pl.ds(0, n_blocks)]
            cp = pltpu.make_async_remote_copy(src_ref=blocks, dst_ref=blocks, send_sem=ss.at[sem], recv_sem=rs.at[sem],
                                              device_id=me3, device_id_type=MESH)
            if is_send:
                cp.wait_send()
            else:
                cp.wait_recv()

    return pl.pallas_call(
        body,
        name=name,
        in_specs=[_HBM] * nb + [_SEM, _SEM, pl.BlockSpec(memory_space=pl.ANY)],
        out_specs=[_HBM] * nb,
        out_shape=[pltpu.HBM(a.shape, a.dtype) for a in bufs],
        input_output_aliases={k: k for k in range(nb)},
        compiler_params=pltpu.CompilerParams(has_side_effects=_EFFECT),
    )(*bufs, send_sems, recv_sems, after)


def _landing(own, me):
    return lax.dynamic_update_slice(lax.empty((N_DEV,) + own.shape[1:], own.dtype), own, (me, 0, 0))


_SIBLING = (0, 0, 1)
_OTHER_CHIPS = ((1, 0, 0), (0, 1, 0), (1, 1, 0))


def _bits(flip):
    return 4 * flip[0] + 2 * flip[1] + flip[2]


def _gather_send(shards, grp, after):
    nw = len(shards)
    me = _me()
    bufs = list(shards) + [_landing(a[None], me) for a in shards]
    copies = []
    for k in range(nw):
        copies.append((k, lambda me: None, nw + k, lambda me: me, _SIBLING, nw + k))
        copies += [(k, lambda me: None, nw + k, lambda me: me, f, k) for f in _OTHER_CHIPS]
    return _copies_start(bufs, copies, 2 * nw, "gather_send_" + grp, after)


def _gather_pass(started, grp, after):
    nw = len(started[2]) // 2
    waits = []
    for k in range(nw):
        waits += [(nw + k, 3, k, True), (nw + k, 1, nw + k, True), (nw + k, 3, k, False), (nw + k, 1, nw + k, False)]
    lands = list(_copies_wait(started, waits, after, "gather_arrived_" + grp)[nw:])
    copies = []
    for k in range(nw):
        for f in _OTHER_CHIPS:
            slot = functools.partial(lambda me, bits: me ^ bits, bits=_bits(f))
            copies.append((k, slot, k, slot, _SIBLING, k))
    return _copies_start(lands, copies, nw, "gather_pass_" + grp)


def _gather_finish(passed, grp, after):
    nw = len(passed[2])
    waits = [(k, 3, k, is_send) for k in range(nw) for is_send in (True, False)]
    return _copies_wait(passed, waits, passed[3] if after is None else after, "gather_done_" + grp)


def _scatter_send(parts, grp, after=None):
    nw = len(parts)
    me = _me()
    bufs = list(parts) + [_landing(lax.dynamic_index_in_dim(a, me, 0, keepdims=True), me) for a in parts]
    copies = []
    for k in range(nw):
        for f in _FLIPS:
            to = functools.partial(lambda me, bits: me ^ bits, bits=_bits(f))
            copies.append((k, to, nw + k, lambda me: me, f, k))
    return _copies_start(bufs, copies, nw, "scatter_send_" + grp, after)


def _scatter_finish(started, grp, after):
    nw = len(started[2]) // 2
    waits = [(nw + k, N_DEV - 1, k, is_send) for k in range(nw) for is_send in (True, False)]
    return _copies_wait(started, waits, after, "scatter_done_" + grp)[nw:]


N_CHIP = N_DEV // 2


def _pair_sum(part, other, key):
    _, r, c = part.shape
    tm = _ADAM_ROWS.get(r, r)
    core = lax.axis_index("c").astype(jnp.int32).reshape(1)

    def body(core_ref, a_ref, b_ref, o_ref):
        o_ref[...] = (a_ref[...].astype(F32) + b_ref[...].astype(F32)).astype(BF16)

    return pl.pallas_call(
        body,
        name="pair_sum_" + key,
        grid_spec=pltpu.PrefetchScalarGridSpec(
            num_scalar_prefetch=1,
            grid=(N_CHIP, r // tm),
            in_specs=[pl.BlockSpec((None, tm, c), lambda q, i, core_ref: (2 * q + core_ref[0], i, 0)),
                      pl.BlockSpec((None, tm, c), lambda q, i, core_ref: (q, i, 0))],
            out_specs=pl.BlockSpec((None, tm, c), lambda q, i, core_ref: (q, i, 0)),
        ),
        out_shape=jax.ShapeDtypeStruct((N_CHIP, r, c), BF16),
        compiler_params=_cp(("parallel", "parallel")),
    )(core, part, other)


def _scatter2_send(part, key, dep=None):
    me = _me()
    _, r, c = part.shape
    swap = []
    for q in range(N_CHIP):
        src = functools.partial(lambda me, q: 2 * q + 1 - me % 2, q=q)
        swap.append((0, src, 1, functools.partial(lambda me, q: q, q=q), _SIBLING, 0))
    started = _copies_start([part, lax.empty((N_CHIP, r, c), part.dtype)], swap, 1, "scatter_swap_" + key, dep)
    waits = [(1, N_CHIP, 0, True), (1, N_CHIP, 0, False)]
    part, other = _copies_wait(started, waits, started[3], "scatter_swapped_" + key)
    pair = _pair_sum(part, other, key)
    chip = me // 2
    land = lax.dynamic_update_slice(lax.empty((N_CHIP, r, c), pair.dtype),
                                    lax.dynamic_index_in_dim(pair, chip, 0, keepdims=True), (chip, 0, 0))
    copies = []
    for f in _OTHER_CHIPS:
        to = functools.partial(lambda me, bits: (me ^ bits) // 2, bits=_bits(f))
        copies.append((0, to, 1, lambda me: me // 2, f, 0))
    return _copies_start([pair, land], copies, 1, "scatter_send_" + key)


def _scatter2_finish(started, key, after):
    waits = [(1, N_CHIP - 1, 0, True), (1, N_CHIP - 1, 0, False)]
    return _copies_wait(started, waits, after, "scatter_done_" + key)[1]


def _allgather_send(block, name, after=None):
    me = _me()
    copies = [(0, lambda me: None, 1, lambda me: me, f, 0) for f in _FLIPS]
    return _copies_start([block, _landing(block[None], me)], copies, 1, name, after)


def _allgather_finish(started, name, after):
    waits = [(1, N_DEV - 1, 0, True), (1, N_DEV - 1, 0, False)]
    return _copies_wait(started, waits, after, name)[1]


_ADAM_ROWS = {352: 176, 1088: 272}


def _sum_adamw(parts, wv, m, v, name):
    n_parts, r, c = parts.shape
    tm = _ADAM_ROWS.get(r, r)
    assert r % tm == 0 and wv.shape == (r, c)

    def body(p_ref, w_ref, m_ref, v_ref, g_ref, d_ref, mo_ref, vo_ref):
        gv = p_ref[0].astype(F32)
        for j in range(1, n_parts):
            gv = gv + p_ref[j].astype(F32)
        g_ref[...] = gv
        mn = ADAM_B1 * m_ref[...] + (1.0 - ADAM_B1) * gv
        vn = ADAM_B2 * v_ref[...] + (1.0 - ADAM_B2) * (gv * gv)
        m_hat = mn / (1.0 - ADAM_B1 ** ADAM_STEP)
        v_hat = vn / (1.0 - ADAM_B2 ** ADAM_STEP)
        d_ref[...] = -ADAM_LR * (m_hat / (jnp.sqrt(v_hat) + ADAM_EPS) + ADAM_WD * w_ref[...])
        mo_ref[...] = mn
        vo_ref[...] = vn

    sp = pl.BlockSpec((tm, c), lambda i: (i, 0))
    return pl.pallas_call(
        body,
        name=name,
        grid=(r // tm,),
        in_specs=[pl.BlockSpec((n_parts, tm, c), lambda i: (0, i, 0))] + [sp] * 3,
        out_specs=[sp] * 4,
        out_shape=[jax.ShapeDtypeStruct((r, c), F32)] * 4,
        compiler_params=_cp(("parallel",), 48),
    )(parts, wv, m, v)


_WEIGHTS = ["ffn1_w_gate", "ffn1_w_up", "ffn1_w_down", "ln1_g", "ln1_b", "w_in", "b_gates", "gmlp_ln_g", "gmlp_ln_b",
            "gmlp_w_s", "gmlp_b_s", "w_attn_branch", "w_gmlp_branch", "w_out", "ln2_g", "ln2_b", "ffn2_w_gate",
            "ffn2_w_up", "ffn2_w_down", "ln3_g", "ln3_b"]
_BIG_OF = {"ffn1_w_gate": ("g1", True), "ffn1_w_up": ("u1", True), "ffn1_w_down": ("d1", False), "w_in": ("win", True),
           "w_attn_branch": ("ab", True), "w_gmlp_branch": ("gb", False), "w_out": ("o", False),
           "ffn2_w_gate": ("g2", True), "ffn2_w_up": ("u2", True), "ffn2_w_down": ("d2", False)}
_SMALL = [n for n in _WEIGHTS if n not in _BIG_OF]
_SMALL_ROWS = {"gmlp_w_s": 128, "b_gates": 2}
_SMALL_SLOT = 8


def _pack_small(d, last=None):
    rows = []
    for n in _SMALL:
        r = d[n].reshape(-1, D)
        slot = max(r.shape[0], _SMALL_SLOT)
        rows.append(jnp.pad(r, ((0, slot - r.shape[0]), (0, 0))))
    rows.append(jnp.zeros((_SMALL_SLOT, D), F32) if last is None else jnp.broadcast_to(last.reshape(1, 1), (_SMALL_SLOT, D)))
    return jnp.concatenate(rows, axis=0)


def _unpack_small(packed, shapes):
    out, at = {}, 0
    for n in _SMALL:
        k = _SMALL_ROWS.get(n, 1)
        out[n] = packed[at:at + k].reshape(shapes[n])
        at += max(k, _SMALL_SLOT)
    return out


def kernel(x, positions, ffn1_w_gate, ffn1_w_up, ffn1_w_down, ln1_g, ln1_b, w_in, b_gates, gmlp_ln_g, gmlp_ln_b, gmlp_w_s, gmlp_b_s, w_attn_branch, w_gmlp_branch, w_out, ln2_g, ln2_b, ffn2_w_gate, ffn2_w_up, ffn2_w_down, ln3_g, ln3_b, loss_target, m_ffn1_w_gate, m_ffn1_w_up, m_ffn1_w_down, m_ln1_g, m_ln1_b, m_w_in, m_b_gates, m_gmlp_ln_g, m_gmlp_ln_b, m_gmlp_w_s, m_gmlp_b_s, m_w_attn_branch, m_w_gmlp_branch, m_w_out, m_ln2_g, m_ln2_b, m_ffn2_w_gate, m_ffn2_w_up, m_ffn2_w_down, m_ln3_g, m_ln3_b, v_ffn1_w_gate, v_ffn1_w_up, v_ffn1_w_down, v_ln1_g, v_ln1_b, v_w_in, v_b_gates, v_gmlp_ln_g, v_gmlp_ln_b, v_gmlp_w_s, v_gmlp_b_s, v_w_attn_branch, v_w_gmlp_branch, v_w_out, v_ln2_g, v_ln2_b, v_ffn2_w_gate, v_ffn2_w_up, v_ffn2_w_down, v_ln3_g, v_ln3_b):
    args = dict(locals())
    wts = {n: args[n] for n in _WEIGHTS}
    ms = {n: args["m_" + n] for n in _WEIGHTS}
    vs = {n: args["v_" + n] for n in _WEIGHTS}

    name_of = {key: (n, tr) for n, (key, tr) in _BIG_OF.items()}

    shards = {}
    for grp, keys in _GROUPS.items():
        shards[grp] = []
        for key in keys:
            n, tr = name_of[key]
            s2 = wts[n][0]
            shards[grp].append((s2.T if tr else s2).astype(BF16))
    started, tok = {}, None
    for grp in _GROUPS:
        started[grp] = _gather_send(shards[grp], grp, tok)
        tok = started[grp][3]
    all_started = tok
    passed = {}

    def get_w(grp, after, early=False):
        if grp not in passed:
            passed[grp] = _gather_pass(started[grp], grp, all_started if after is None else after)
            after = None
        if early:
            return None
        lands = _gather_finish(passed[grp], grp, after)
        return {key: g.reshape(-1, g.shape[-1]) for key, g in zip(_GROUPS[grp], lands)}

    sent = {}

    def emit(grp, grads):
        if grp == "small":
            sent[grp] = _allgather_send(_pack_small(grads, last=grads["loss"]), "small_grads_send")
        else:
            parts = [grads[key].reshape(N_DEV, -1, grads[key].shape[-1]) for key in _SCATTERS[grp]]
            sent[grp] = _scatter2_send(parts[0], grp) if grp in _TWO_STAGE else _scatter_send(parts, grp)
        return sent[grp][3]

    p = {n: (wts[n][0] if n in ("gmlp_w_s", "gmlp_b_s") else wts[n]) for n in _SMALL}
    loss, dx, gp = _local_step(x[0], positions.reshape(S, 1), loss_target[0], get_w, p, emit)
    grads, deltas, new_m, new_v = {}, {}, {}, {}
    after = dx
    for grp in ("ffn2", "mix", "win", "small", "g1", "u1", "d1"):
        if grp == "small":
            parts = _allgather_finish(sent[grp], "small_grads_done", after)
            outs = _sum_adamw(parts, *[_pack_small({n: d[n] for n in _SMALL}) for d in (wts, ms, vs)], "update_small")
            shapes = {n: wts[n].shape for n in _SMALL}
            for dst, packed in zip((grads, deltas, new_m, new_v), outs):
                dst.update(_unpack_small(packed, shapes))
            loss = outs[0][-_SMALL_SLOT, 0]
            after = outs[1]
            continue
        arrived = ([_scatter2_finish(sent[grp], grp, after)] if grp in _TWO_STAGE
                   else _scatter_finish(sent[grp], grp, after))
        for key, part in zip(_SCATTERS[grp], arrived):
            n, tr = name_of[key]
            outs = _sum_adamw(part, *[(d[n][0].T if tr else d[n][0]) for d in (wts, ms, vs)], "update_" + key)
            for dst, o in zip((grads, deltas, new_m, new_v), outs):
                dst[n] = (o.T if tr else o)[None]
            after = outs[1]

    return (loss, dx[None], *[grads[n] for n in _WEIGHTS], *[deltas[n] for n in _WEIGHTS],
            *[new_m[n] for n in _WEIGHTS], *[new_v[n] for n in _WEIGHTS])
```

```python
import functools
import math

import jax
import jax.numpy as jnp
from jax import lax
from jax.experimental import pallas as pl
from jax.experimental.pallas import tpu as pltpu

F32 = jnp.float32
BF16 = jnp.bfloat16

N_DEV = 8
D = 1024
S = 2048
F = 2816
HEAD_DIM = 64
HEADS = 8
GW = HEADS * HEAD_DIM
PATTERN_DILATIONS = (1, 4, 16)
BLK = 128
QKV_W = 3 * 3 * GW
IN_W = QKV_W + 2 * D + 2 * D
DPROJ_W = 5 * 2 * D
DPROJ_Z_COL, DPROJ_G_COL = 3, 4


def _dproj_tile(t):
    three = jnp.int32(3)
    return jnp.where(t < 9, lax.rem(t, three) * 3 + lax.div(t, three), t + 3)
ROPE_THETA = 10000.0
ALPHA = 2.0 ** 0.25
LN_EPS = 1e-5
ADAM_LR, ADAM_B1, ADAM_B2, ADAM_EPS, ADAM_WD, ADAM_STEP = 0.001, 0.9, 0.999, 1e-08, 0.01, 10
NEG = -1e30
MESH = pl.DeviceIdType.MESH


def _cp(sem=None, vmem_mb=None):
    kw = {}
    if sem is not None:
        kw["dimension_semantics"] = sem
    if vmem_mb is not None:
        kw["vmem_limit_bytes"] = vmem_mb << 20
    return pltpu.CompilerParams(**kw)


def _dot_nn(a, b):
    return lax.dot_general(a, b, (((1,), (0,)), ((), ())), preferred_element_type=F32)


def _dot_nt(a, b):
    return lax.dot_general(a, b, (((1,), (1,)), ((), ())), preferred_element_type=F32)


def _dot_tn(a, b):
    return lax.dot_general(a, b, (((0,), (0,)), ((), ())), preferred_element_type=F32)


def _ln_fwd(r, g, b):
    mu = jnp.mean(r, axis=-1, keepdims=True)
    xc = r - mu
    var = jnp.mean(xc * xc, axis=-1, keepdims=True)
    rstd = lax.rsqrt(var + LN_EPS)
    xhat = xc * rstd
    return xhat * g + b, xhat, rstd


def _ln_bwd(dh, xhat, rstd, g):
    dxh = dh * g
    m1 = jnp.mean(dxh, axis=-1, keepdims=True)
    m2 = jnp.mean(dxh * xhat, axis=-1, keepdims=True)
    return rstd * (dxh - m1 - xhat * m2)


def _sigmoid(x):
    return 0.5 * jnp.tanh(0.5 * x) + 0.5


def _colsum(x):
    return jnp.sum(x, axis=0, keepdims=True)


def _matmul(a, b, mode, out_dtype, tm, tn, tk, name, b_off=0, n_out=None, dep=None, a_map=None, m_out=None):
    n_dep = 0 if dep is None else 1
    a_map = a_map or (lambda t: t)
    if mode == "nn":
        m, k = a.shape[0], b.shape[0]
        n = b.shape[1]
    elif mode == "nt":
        m, k = a.shape
        n = n_out if n_out is not None else b.shape[0]
    else:
        k, m = a.shape[0], m_out or a.shape[1]
        n = b.shape[1]
    nk = k // tk
    assert m % tm == 0 and n % tn == 0 and k % tk == 0
    dot = {"nn": _dot_nn, "nt": _dot_nt, "tn": _dot_tn}[mode]

    def body(a_ref, b_ref, *rest):
        o_ref, scr = rest[n_dep], rest[n_dep + 1:]
        r = dot(a_ref[...].astype(BF16), b_ref[...].astype(BF16))
        if nk == 1:
            o_ref[...] = r.astype(out_dtype)
        else:
            acc = scr[0]
            kk = pl.program_id(2)

            @pl.when(kk == 0)
            def _():
                acc[...] = r

            @pl.when(kk > 0)
            def _():
                acc[...] += r

            @pl.when(kk == nk - 1)
            def _():
                o_ref[...] = acc[...].astype(out_dtype)

    if mode == "nn":
        a_spec = pl.BlockSpec((tm, tk), lambda i, j, kk: (i, a_map(kk)))
        b_spec = pl.BlockSpec((tk, tn), lambda i, j, kk: (kk, j))
    elif mode == "nt":
        a_spec = pl.BlockSpec((tm, tk), lambda i, j, kk: (i, kk))
        b_spec = pl.BlockSpec((tn, tk), lambda i, j, kk: (j + b_off, kk))
    else:
        a_spec = pl.BlockSpec((tk, tm), lambda i, j, kk: (kk, a_map(i)))
        b_spec = pl.BlockSpec((tk, tn), lambda i, j, kk: (kk, j))
    return pl.pallas_call(
        body,
        name=name,
        grid=(m // tm, n // tn, nk),
        in_specs=[a_spec, b_spec] + [pl.BlockSpec(memory_space=pl.ANY)] * n_dep,
        out_specs=pl.BlockSpec((tm, tn), lambda i, j, kk: (i, j)),
        out_shape=jax.ShapeDtypeStruct((m, n), out_dtype),
        scratch_shapes=[] if nk == 1 else [pltpu.VMEM((tm, tn), F32)],
        compiler_params=_cp(("parallel", "parallel", "arbitrary"), 56),
    )(a, b, *([] if dep is None else [dep]))


def _ffn_up(x, wgt, wut, name, tm=512, tn=1408, dep=None):
    n_dep = 0 if dep is None else 1

    def body(x_ref, wg_ref, wu_ref, *rest):
        ga_ref, gb_ref, hm_ref = rest[n_dep:]
        xb = x_ref[...].astype(BF16)
        a = _dot_nt(xb, wg_ref[...])
        b = _dot_nt(xb, wu_ref[...])
        sig = _sigmoid(a)
        silu = a * sig
        ga_ref[...] = (b * (sig + silu * (1.0 - sig))).astype(BF16)
        gb_ref[...] = silu.astype(BF16)
        hm_ref[...] = (silu * b).astype(BF16)

    wsp = pl.BlockSpec((tn, D), lambda i, j: (j, 0))
    mid = pl.BlockSpec((tm, tn), lambda i, j: (i, j))
    return pl.pallas_call(
        body,
        name=name,
        grid=(S // tm, F // tn),
        in_specs=[pl.BlockSpec((tm, D), lambda i, j: (i, 0)), wsp, wsp] + [pl.BlockSpec(memory_space=pl.ANY)] * n_dep,
        out_specs=[mid, mid, mid],
        out_shape=[jax.ShapeDtypeStruct((S, F), BF16)] * 3,
        compiler_params=_cp(("parallel", "arbitrary"), 56),
    )(x, wgt, wut, *([] if dep is None else [dep]))


def _ffn_down(hm, wd, x, ln_g, ln_b, name, tm=512):
    def body(hm_ref, wd_ref, x_ref, g_ref, b_ref, h_ref, hb_ref, xh_ref, rs_ref):
        r = ALPHA * x_ref[...] + 0.5 * _dot_nn(hm_ref[...], wd_ref[...])
        h, xhat, rstd = _ln_fwd(r, g_ref[...], b_ref[...])
        h_ref[...] = h
        hb_ref[...] = h.astype(BF16)
        xh_ref[...] = xhat
        rs_ref[...] = rstd

    row = pl.BlockSpec((tm, D), lambda i: (i, 0))
    vec = pl.BlockSpec((1, D), lambda i: (0, 0))
    return pl.pallas_call(
        body,
        name=name,
        grid=(S // tm,),
        in_specs=[pl.BlockSpec((tm, F), lambda i: (i, 0)), pl.BlockSpec((F, D), lambda i: (0, 0)), row, vec, vec],
        out_specs=[row, row, row, pl.BlockSpec((tm, 1), lambda i: (i, 0))],
        out_shape=[
            jax.ShapeDtypeStruct((S, D), F32),
            jax.ShapeDtypeStruct((S, D), BF16),
            jax.ShapeDtypeStruct((S, D), F32),
            jax.ShapeDtypeStruct((S, 1), F32),
        ],
        compiler_params=_cp(("parallel",), 56),
    )(hm, wd, x, ln_g, ln_b)


def _ffn_bwd_mid(dh_a, dh_b, xhat, rstd, ln_g, a, b, wd, name, tm=512, tn=1408):
    two = dh_b is not None

    def body(*refs):
        dha_ref = refs[0]
        dhb_ref = refs[1] if two else None
        (xh_ref, rs_ref, g_ref, a_ref, b_ref, wd_ref,
         dr_ref, df_ref, da_ref, db_ref, dg_ref, dbias_ref, df_scr) = refs[2 if two else 1:]
        i = pl.program_id(0)
        j = pl.program_id(1)

        @pl.when(j == 0)
        def _():
            dh = dha_ref[...]
            if two:
                dh = ALPHA * dh + dhb_ref[...]
            xhat = xh_ref[...]
            dr = _ln_bwd(dh, xhat, rs_ref[...], g_ref[...])
            dfb = (0.5 * dr).astype(BF16)
            dr_ref[...] = dr
            df_scr[...] = dfb
            df_ref[...] = dfb
            sg = _colsum(dh * xhat)
            sb = _colsum(dh)

            @pl.when(i == 0)
            def _():
                dg_ref[...] = sg
                dbias_ref[...] = sb

            @pl.when(i > 0)
            def _():
                dg_ref[...] += sg
                dbias_ref[...] += sb

        dhm = _dot_nt(df_scr[...], wd_ref[...])
        da_ref[...] = (dhm * a_ref[...].astype(F32)).astype(BF16)
        db_ref[...] = (dhm * b_ref[...].astype(F32)).astype(BF16)

    row = pl.BlockSpec((tm, D), lambda i, j: (i, 0))
    vec = pl.BlockSpec((1, D), lambda i, j: (0, 0))
    mid = pl.BlockSpec((tm, tn), lambda i, j: (i, j))
    ins = [dh_a] + ([dh_b] if two else []) + [xhat, rstd, ln_g, a, b, wd]
    in_specs = [row] * (2 if two else 1) + [row, pl.BlockSpec((tm, 1), lambda i, j: (i, 0)), vec, mid, mid,
                                            pl.BlockSpec((tn, D), lambda i, j: (j, 0))]
    return pl.pallas_call(
        body,
        name=name,
        grid=(S // tm, F // tn),
        in_specs=in_specs,
        out_specs=[row, row, mid, mid, vec, vec],
        out_shape=[
            jax.ShapeDtypeStruct((S, D), F32),
            jax.ShapeDtypeStruct((S, D), BF16),
            jax.ShapeDtypeStruct((S, F), BF16),
            jax.ShapeDtypeStruct((S, F), BF16),
            jax.ShapeDtypeStruct((1, D), F32),
            jax.ShapeDtypeStruct((1, D), F32),
        ],
        scratch_shapes=[pltpu.VMEM((tm, D), BF16)],
        compiler_params=_cp(("arbitrary", "arbitrary"), 56),
    )(*ins)


def _ffn_bwd_dx(dr, da, db, wgt, wut, name, tm=512, tk=1408, dep=None):
    nk = F // tk
    n_dep = 0 if dep is None else 1

    def body(dr_ref, da_ref, db_ref, wg_ref, wu_ref, *rest):
        dx_ref, acc = rest[n_dep], rest[n_dep + 1]
        kk = pl.program_id(1)
        part = _dot_nn(da_ref[...], wg_ref[...]) + _dot_nn(db_ref[...], wu_ref[...])

        @pl.when(kk == 0)
        def _():
            acc[...] = ALPHA * dr_ref[...] + part

        @pl.when(kk > 0)
        def _():
            acc[...] += part

        @pl.when(kk == nk - 1)
        def _():
            dx_ref[...] = acc[...]

    row = pl.BlockSpec((tm, D), lambda i, kk: (i, 0))
    mid = pl.BlockSpec((tm, tk), lambda i, kk: (i, kk))
    wsp = pl.BlockSpec((tk, D), lambda i, kk: (kk, 0))
    return pl.pallas_call(
        body,
        name=name,
        grid=(S // tm, nk),
        in_specs=[row, mid, mid, wsp, wsp] + [pl.BlockSpec(memory_space=pl.ANY)] * n_dep,
        out_specs=row,
        out_shape=jax.ShapeDtypeStruct((S, D), F32),
        scratch_shapes=[pltpu.VMEM((tm, D), F32)],
        compiler_params=_cp(("parallel", "arbitrary"), 56),
    )(dr, da, db, wgt, wut, *([] if dep is None else [dep]))


def _rope_tables(pos_col, invf, sign, tm=512):
    def body(p_ref, f_ref, s_ref, c_out, s_out):
        ang = p_ref[...].astype(F32) * f_ref[...]
        c_out[...] = jnp.cos(ang)
        s_out[...] = jnp.sin(ang) * s_ref[...]

    vec = pl.BlockSpec((1, BLK), lambda i: (0, 0))
    out = pl.BlockSpec((tm, BLK), lambda i: (i, 0))
    return pl.pallas_call(
        body,
        name="rope_tables",
        grid=(S // tm,),
        in_specs=[pl.BlockSpec((tm, 1), lambda i: (i, 0)), vec, vec],
        out_specs=[out, out],
        out_shape=[jax.ShapeDtypeStruct((S, BLK), F32)] * 2,
        compiler_params=_cp(("parallel",)),
    )(pos_col, invf, sign)


def _lane_lo(rows=BLK):
    return (lax.broadcasted_iota(jnp.int32, (rows, GW), 1) % HEAD_DIM) < (HEAD_DIM // 2)


def _swap_halves(t, lo):
    return jnp.where(lo, pltpu.roll(t, GW - HEAD_DIM // 2, 1), pltpu.roll(t, HEAD_DIM // 2, 1))


def _rope(t, cosf, sinf, lo):
    return t * cosf + _swap_halves(t, lo) * sinf


def _unrope(g, cosf, sinf, lo):
    return g * cosf + _swap_halves(g * sinf, lo)


def _tile4(v):
    return jnp.concatenate([v, v, v, v], axis=1)


def _band_mask(n):
    qi = lax.broadcasted_iota(jnp.int32, (BLK, 2 * BLK), 0)
    kj = lax.broadcasted_iota(jnp.int32, (BLK, 2 * BLK), 1)
    dist = qi + BLK - kj
    return (dist >= 0) & (dist <= BLK) & ((kj >= BLK) | (n >= 1))


ROWS = 256
LANES = 128


def _to_view(scr, y, dst_ref, d, dtype, col0=0):
    if d == 1:
        dst_ref[:, col0:col0 + GW] = y.astype(dtype)
        return
    for cb in range(GW // LANES):
        scr[cb][...] = y[:, cb * LANES:(cb + 1) * LANES]
    for r in range(d):
        for cb in range(GW // LANES):
            at = col0 + r * GW + cb * LANES
            dst_ref[:, at:at + LANES] = scr[cb][pl.ds(r, ROWS // d, stride=d), :].astype(dtype)


def _from_view(scr, src_ref, d):
    if d == 1:
        return src_ref[...].astype(F32)
    for r in range(d):
        for cb in range(GW // LANES):
            at = r * GW + cb * LANES
            scr[cb][pl.ds(r, ROWS // d, stride=d), :] = src_ref[:, at:at + LANES].astype(F32)
    return jnp.concatenate([scr[cb][...] for cb in range(GW // LANES)], axis=1)


def _view_spec(d):
    return pl.BlockSpec((ROWS // d, d * GW), lambda i: (i, 0))


_VIEW_SCRATCH = [pltpu.VMEM((ROWS, LANES), F32)] * (GW // LANES)


def _qkv_prep(qkv, cos_t, sin_t):
    def body(x_ref, c_ref, s_ref, *rest):
        outs, scr = rest[:9], rest[9:]
        lo = _lane_lo(ROWS)
        cf, sf = _tile4(c_ref[...]), _tile4(s_ref[...])
        for gi, d in enumerate(PATTERN_DILATIONS):
            q, k, v = (x_ref[:, (3 * part + gi) * GW:(3 * part + gi + 1) * GW] for part in range(3))
            _to_view(scr, _rope(q, cf, sf, lo) * (HEAD_DIM ** -0.5), outs[3 * gi], d, BF16)
            _to_view(scr, _rope(k, cf, sf, lo), outs[3 * gi + 1], d, BF16)
            _to_view(scr, v, outs[3 * gi + 2], d, BF16)

    tab = pl.BlockSpec((ROWS, BLK), lambda i: (i, 0))
    outs = pl.pallas_call(
        body,
        name="qkv_prep",
        grid=(S // ROWS,),
        in_specs=[pl.BlockSpec((ROWS, QKV_W), lambda i: (i, 0)), tab, tab],
        out_specs=[_view_spec(d) for d in PATTERN_DILATIONS for _ in range(3)],
        out_shape=[jax.ShapeDtypeStruct((S // d, d * GW), BF16) for d in PATTERN_DILATIONS for _ in range(3)],
        scratch_shapes=_VIEW_SCRATCH,
        compiler_params=_cp(("parallel",), 48),
    )(qkv, cos_t, sin_t)
    return [tuple(outs[3 * gi:3 * gi + 3]) for gi in range(3)]


N_ATTN_STEPS = S // BLK


def _class_and_block(gi, t):
    per_class = N_ATTN_STEPS // PATTERN_DILATIONS[gi]
    return t >> (per_class.bit_length() - 1), t & (per_class - 1)


def _attn_specs(gi):
    def cur(t):
        r, n = _class_and_block(gi, t)
        return n, r

    def prev(t):
        r, n = _class_and_block(gi, t)
        return jnp.maximum(n - 1, 0), r

    def whole(t):
        return 0, _class_and_block(gi, t)[0]

    sub = S // PATTERN_DILATIONS[gi]
    return (pl.BlockSpec((BLK, GW), cur), pl.BlockSpec((BLK, GW), prev), pl.BlockSpec((sub, GW), whole))


def _left_lanes():
    return lax.broadcasted_iota(jnp.int32, (BLK, LANES), 1) < HEAD_DIM


def _stack_heads(t2, left):
    zero = jnp.zeros_like(t2)
    return jnp.concatenate([jnp.where(left, t2, zero), jnp.where(left, zero, t2)], axis=0)


def _attn_fwd_block(n, q_ref, kc_ref, kp_ref, vc_ref, vp_ref, o_ref, l_ref):
    mask = _band_mask(n)
    left = _left_lanes()
    zero = jnp.zeros((BLK, LANES), BF16)
    for pr in range(GW // LANES):
        cs = slice(pr * LANES, (pr + 1) * LANES)
        q2 = q_ref[:, cs]
        k2 = jnp.concatenate([kp_ref[:, cs], kc_ref[:, cs]], axis=0)
        v2 = jnp.concatenate([vp_ref[:, cs], vc_ref[:, cs]], axis=0)
        o_h, lse_h = [], []
        for side in (left, ~left):
            s = jnp.where(mask, _dot_nt(jnp.where(side, q2, zero), k2), NEG)
            m = jnp.max(s, axis=1, keepdims=True)
            p = jnp.exp(s - m)
            l = jnp.sum(p, axis=1, keepdims=True)
            o_h.append(_dot_nn((p / l).astype(BF16), v2))
            lse_h.append(m + jnp.log(l))
        o_ref[:, cs] = jnp.where(left, o_h[0], o_h[1])
        l_ref[:, cs] = jnp.where(left, lse_h[0], lse_h[1])


def _attn_fwd(preps):
    def body(*refs):
        t = pl.program_id(0)
        for gi in range(3):
            _attn_fwd_block(_class_and_block(gi, t)[1], *refs[5 * gi:5 * gi + 5], *refs[15 + 2 * gi:17 + 2 * gi])

    in_specs, out_specs, out_shape, args = [], [], [], []
    for gi, d in enumerate(PATTERN_DILATIONS):
        cur, prev, _ = _attn_specs(gi)
        qr, kr, vv = preps[gi]
        in_specs += [cur, cur, prev, cur, prev]
        args += [qr, kr, kr, vv, vv]
        out_specs += [cur, cur]
        out_shape += [jax.ShapeDtypeStruct((S // d, d * GW), F32)] * 2
    outs = pl.pallas_call(
        body,
        name="attn_fwd",
        grid=(N_ATTN_STEPS,),
        in_specs=in_specs,
        out_specs=out_specs,
        out_shape=out_shape,
        compiler_params=_cp(("arbitrary",), 48),
    )(*args)
    return [outs[0], outs[2], outs[4]], [outs[1], outs[3], outs[5]]


def _attn_bwd_block(n, q_ref, kc_ref, kp_ref, vc_ref, vp_ref, do_ref, l_ref, c_ref, dq_ref, dk_ref, dv_ref):
    mask = _band_mask(n)
    left = _left_lanes()
    left2 = jnp.concatenate([left, left], axis=0)
    here = pl.ds(pl.multiple_of(n * BLK, BLK), BLK)
    before = pl.ds(pl.multiple_of(jnp.maximum(n - 1, 0) * BLK, BLK), BLK)
    for pr in range(GW // LANES):
        cs = slice(pr * LANES, (pr + 1) * LANES)
        q2 = q_ref[:, cs]
        do2 = do_ref[:, cs]
        k_bd = _stack_heads(jnp.concatenate([kp_ref[:, cs], kc_ref[:, cs]], axis=0), left2)
        v_bd = _stack_heads(jnp.concatenate([vp_ref[:, cs], vc_ref[:, cs]], axis=0), left2)
        s2 = _dot_nt(q2, k_bd)
        dp2 = _dot_nt(do2, v_bd)
        ps, dss = [], []
        for h in range(2):
            at = pr * LANES + h * HEAD_DIM
            half = slice(h * 2 * BLK, (h + 1) * 2 * BLK)
            p = jnp.exp(jnp.where(mask, s2[:, half], NEG) - l_ref[:, at:at + 1])
            ps.append(p.astype(BF16))
            dss.append((p * (dp2[:, half] - c_ref[:, at:at + 1])).astype(BF16))
        ds_both = jnp.concatenate(dss, axis=1)
        dq_ref[:, cs] = _dot_nn(ds_both, k_bd) * (HEAD_DIM ** -0.5)
        dk_bd = _dot_tn(ds_both, q2)
        dv_bd = _dot_tn(jnp.concatenate(ps, axis=1), do2)
        dk2 = jnp.where(left2, dk_bd[:2 * BLK], dk_bd[2 * BLK:])
        dv2 = jnp.where(left2, dv_bd[:2 * BLK], dv_bd[2 * BLK:])
        dk_ref[here, cs] = dk2[BLK:]
        dv_ref[here, cs] = dv2[BLK:]
        dk_ref[before, cs] += dk2[:BLK]
        dv_ref[before, cs] += dv2[:BLK]


def _attn_bwd(preps, dos, lses, cterms):
    def body(*refs):
        t = pl.program_id(0)
        for gi in range(3):
            _attn_bwd_block(_class_and_block(gi, t)[1], *refs[8 * gi:8 * gi + 8], *refs[24 + 3 * gi:27 + 3 * gi])

    in_specs, out_specs, out_shape, args = [], [], [], []
    for gi, d in enumerate(PATTERN_DILATIONS):
        cur, prev, whole = _attn_specs(gi)
        qr, kr, vv = preps[gi]
        in_specs += [cur, cur, prev, cur, prev, cur, cur, cur]
        args += [qr, kr, kr, vv, vv, dos[gi], lses[gi], cterms[gi]]
        out_specs += [cur, whole, whole]
        out_shape += [jax.ShapeDtypeStruct((S // d, d * GW), F32)] * 3
    outs = pl.pallas_call(
        body,
        name="attn_bwd",
        grid=(N_ATTN_STEPS,),
        in_specs=in_specs,
        out_specs=out_specs,
        out_shape=out_shape,
        compiler_params=_cp(("arbitrary",), 56),
    )(*args)
    return [tuple(outs[3 * gi:3 * gi + 3]) for gi in range(3)]


def _qkv_unprep(grads, cos_t, sin_t, dproj):
    def body(*refs):
        views, (c_ref, s_ref, _, out_ref), scr = refs[:9], refs[9:13], refs[13:]
        lo = _lane_lo(ROWS)
        cf, sf = _tile4(c_ref[...]), _tile4(s_ref[...])
        for gi, d in enumerate(PATTERN_DILATIONS):
            dq_ref, dk_ref, dv_ref = views[3 * gi:3 * gi + 3]
            at = 3 * gi * GW
            out_ref[:, at:at + GW] = _unrope(_from_view(scr, dq_ref, d), cf, sf, lo).astype(BF16)
            out_ref[:, at + GW:at + 2 * GW] = _unrope(_from_view(scr, dk_ref, d), cf, sf, lo).astype(BF16)
            out_ref[:, at + 2 * GW:at + 3 * GW] = _from_view(scr, dv_ref, d).astype(BF16)

    tab = pl.BlockSpec((ROWS, BLK), lambda i: (i, 0))
    return pl.pallas_call(
        body,
        name="qkv_unprep",
        grid=(S // ROWS,),
        in_specs=[_view_spec(d) for d in PATTERN_DILATIONS for _ in range(3)] + [tab, tab, pl.BlockSpec(memory_space=pl.ANY)],
        out_specs=pl.BlockSpec((ROWS, QKV_W), lambda i: (i, 0)),
        out_shape=jax.ShapeDtypeStruct(dproj.shape, dproj.dtype),
        input_output_aliases={11: 0},
        scratch_shapes=_VIEW_SCRATCH,
        compiler_params=_cp(("parallel",), 48),
    )(*[g for grp in grads for g in grp], cos_t, sin_t, dproj)


def _group_weights(l0, l1, l2):
    mx = jnp.maximum(jnp.maximum(l0, l1), l2)
    e0, e1, e2 = jnp.exp(l0 - mx), jnp.exp(l1 - mx), jnp.exp(l2 - mx)
    inv = 1.0 / (e0 + e1 + e2)
    return e0 * inv, e1 * inv, e2 * inv


def _combine_fwd(os_, lses):
    def body(o0, o1, o2, l0, l1, l2, y_ref, *scr):
        ov = [_from_view(scr, o, d) for o, d in zip((o0, o1, o2), PATTERN_DILATIONS)]
        lv = [_from_view(scr, l, d) for l, d in zip((l0, l1, l2), PATTERN_DILATIONS)]
        w0, w1, w2 = _group_weights(*lv)
        y_ref[...] = (w0 * ov[0] + w1 * ov[1] + w2 * ov[2]).astype(BF16)

    views = [_view_spec(d) for d in PATTERN_DILATIONS]
    return pl.pallas_call(
        body,
        name="attn_combine_fwd",
        grid=(S // ROWS,),
        in_specs=views + views,
        out_specs=pl.BlockSpec((ROWS, GW), lambda i: (i, 0)),
        out_shape=jax.ShapeDtypeStruct((S, GW), BF16),
        scratch_shapes=_VIEW_SCRATCH,
        compiler_params=_cp(("parallel",)),
    )(*os_, *lses)


def _combine_bwd(dy, os_, lses, seg):
    def body(dy_ref, o0, o1, o2, l0, l1, l2, seg_ref, d0, d1, d2, c0, c1, c2, *scr):
        ov = [_from_view(scr, o, d) for o, d in zip((o0, o1, o2), PATTERN_DILATIONS)]
        lv = [_from_view(scr, l, d) for l, d in zip((l0, l1, l2), PATTERN_DILATIONS)]
        ws = _group_weights(*lv)
        dyv = dy_ref[...]
        t = dyv * (ws[0] * ov[0] + ws[1] * ov[1] + ws[2] * ov[2])
        t_hi = t.astype(BF16)
        r1 = t - t_hi.astype(F32)
        t_mid = r1.astype(BF16)
        t_lo = (r1 - t_mid.astype(F32)).astype(BF16)
        sg = seg_ref[...]
        e = _dot_nn(t_hi, sg) + _dot_nn(t_mid, sg) + _dot_nn(t_lo, sg)
        for w, do_ref, c_ref, d in zip(ws, (d0, d1, d2), (c0, c1, c2), PATTERN_DILATIONS):
            _to_view(scr, w * dyv, do_ref, d, BF16)
            _to_view(scr, w * e, c_ref, d, F32)

    views = [_view_spec(d) for d in PATTERN_DILATIONS]
    return pl.pallas_call(
        body,
        name="attn_combine_bwd",
        grid=(S // ROWS,),
        in_specs=[pl.BlockSpec((ROWS, GW), lambda i: (i, 0))] + views + views + [pl.BlockSpec((GW, GW), lambda i: (0, 0))],
        out_specs=views + views,
        out_shape=[jax.ShapeDtypeStruct((S // d, d * GW), BF16) for d in PATTERN_DILATIONS]
        + [jax.ShapeDtypeStruct((S // d, d * GW), F32) for d in PATTERN_DILATIONS],
        scratch_shapes=_VIEW_SCRATCH,
        compiler_params=_cp(("parallel",)),
    )(dy, *os_, *lses, seg)


_SQRT_HALF = 0.7071067811865476
_INV_SQRT_2PI = 0.3989422804014327


def _gelu(z):
    return 0.5 * z * (1.0 + lax.erf(z * _SQRT_HALF))


def _gelu_grad(z):
    return 0.5 * (1.0 + lax.erf(z * _SQRT_HALF)) + z * (_INV_SQRT_2PI * jnp.exp(-0.5 * z * z))


def _tril_ws(ws_ref, g):
    t = lax.broadcasted_iota(jnp.int32, (BLK, BLK), 0)
    s = lax.broadcasted_iota(jnp.int32, (BLK, BLK), 1)
    return jnp.where(t >= s, ws_ref[g], 0.0)


def _chunks_side_by_side(x, cols, nch):
    return jnp.concatenate([x[c * BLK:(c + 1) * BLK, cols] for c in range(nch)], axis=1)


def _gmlp_fwd(z, ws, bst, ln_g, ln_b, tm=512):
    nch = tm // BLK

    def body(z_ref, ws_ref, b_ref, g_ref, be_ref, y_ref):
        zg = _gelu(z_ref[...])
        u = zg[:, :D]
        vn, _, _ = _ln_fwd(zg[:, D:], g_ref[...], be_ref[...])
        vnb = vn.astype(BF16)
        bt = b_ref[...]
        for g in range(8):
            w = _tril_ws(ws_ref, g).astype(BF16)
            cols = slice(g * BLK, (g + 1) * BLK)
            mixed = _dot_nn(w, _chunks_side_by_side(vnb, cols, nch)) + bt[:, g:g + 1]
            for c in range(nch):
                rows = slice(c * BLK, (c + 1) * BLK)
                y_ref[rows, cols] = (u[rows, cols] * mixed[:, c * BLK:(c + 1) * BLK]).astype(BF16)

    return pl.pallas_call(
        body,
        name="gmlp_fwd",
        grid=(S // tm,),
        in_specs=[
            pl.BlockSpec((tm, 2 * D), lambda i: (i, 0)),
            pl.BlockSpec((8, BLK, BLK), lambda i: (0, 0, 0)),
            pl.BlockSpec((BLK, 8), lambda i: (0, 0)),
            pl.BlockSpec((1, D), lambda i: (0, 0)),
            pl.BlockSpec((1, D), lambda i: (0, 0)),
        ],
        out_specs=pl.BlockSpec((tm, D), lambda i: (i, 0)),
        out_shape=jax.ShapeDtypeStruct((S, D), BF16),
        compiler_params=_cp(("parallel",), 48),
    )(z, ws, bst, ln_g, ln_b)


def _gmlp_bwd(z, dy, ws, bst, ln_g, ln_b, dproj, tm=512):
    nch = tm // BLK

    def body(z_ref, dy_ref, ws_ref, b_ref, g_ref, be_ref, _, dz_ref, dws_ref, dbs_ref, dg_ref, dbe_ref, dvn_scr, dm_acc):
        i = pl.program_id(0)
        zv = z_ref[...]
        zg = _gelu(zv)
        u = zg[:, :D]
        gam = g_ref[...]
        vn, xhat, rstd = _ln_fwd(zg[:, D:], gam, be_ref[...])
        vnb = vn.astype(BF16)
        dyv = dy_ref[...]
        dmix = dyv * u
        dmb = dmix.astype(BF16)
        bt = b_ref[...]
        tmask = lax.broadcasted_iota(jnp.int32, (BLK, BLK), 0) >= lax.broadcasted_iota(jnp.int32, (BLK, BLK), 1)
        dm_sum = dmix[0:BLK]
        for c in range(1, nch):
            dm_sum = dm_sum + dmix[c * BLK:(c + 1) * BLK]

        @pl.when(i == 0)
        def _():
            dm_acc[...] = jnp.zeros_like(dm_acc)
            dws_ref[...] = jnp.zeros_like(dws_ref)
            dg_ref[...] = jnp.zeros_like(dg_ref)
            dbe_ref[...] = jnp.zeros_like(dbe_ref)

        dm_acc[...] += dm_sum
        dus = []
        for g in range(8):
            w = _tril_ws(ws_ref, g).astype(BF16)
            cols = slice(g * BLK, (g + 1) * BLK)
            v_cat = _chunks_side_by_side(vnb, cols, nch)
            dm_cat = _chunks_side_by_side(dmb, cols, nch)
            mixed = _dot_nn(w, v_cat) + bt[:, g:g + 1]
            dus.append(jnp.concatenate(
                [dyv[c * BLK:(c + 1) * BLK, cols] * mixed[:, c * BLK:(c + 1) * BLK] for c in range(nch)], axis=0))
            dws_ref[g] += jnp.where(tmask, _dot_nt(dm_cat, v_cat), 0.0)
            dvn_cat = _dot_tn(w, dm_cat)
            for c in range(nch):
                dvn_scr[c * BLK:(c + 1) * BLK, cols] = dvn_cat[:, c * BLK:(c + 1) * BLK]

        dvn = dvn_scr[...]
        dg_ref[...] += _colsum(dvn * xhat)
        dbe_ref[...] += _colsum(dvn)
        dvg = _ln_bwd(dvn, xhat, rstd, gam)
        gp = _gelu_grad(zv)
        dz_ref[:, :D] = (jnp.concatenate(dus, axis=1) * gp[:, :D]).astype(BF16)
        dz_ref[:, D:] = (dvg * gp[:, D:]).astype(BF16)

        @pl.when(i == S // tm - 1)
        def _():
            acc = dm_acc[...]
            for g in range(8):
                dbs_ref[:, g:g + 1] = jnp.sum(acc[:, g * BLK:(g + 1) * BLK], axis=1, keepdims=True)

    vec = pl.BlockSpec((1, D), lambda i: (0, 0))
    return pl.pallas_call(
        body,
        name="gmlp_bwd",
        grid=(S // tm,),
        in_specs=[
            pl.BlockSpec((tm, 2 * D), lambda i: (i, 0)),
            pl.BlockSpec((tm, D), lambda i: (i, 0)),
            pl.BlockSpec((8, BLK, BLK), lambda i: (0, 0, 0)),
            pl.BlockSpec((BLK, 8), lambda i: (0, 0)),
            vec,
            vec,
            pl.BlockSpec(memory_space=pl.ANY),
        ],
        out_specs=[
            pl.BlockSpec((tm, 2 * D), lambda i: (i, DPROJ_Z_COL)),
            pl.BlockSpec((8, BLK, BLK), lambda i: (0, 0, 0)),
            pl.BlockSpec((BLK, 8), lambda i: (0, 0)),
            vec,
            vec,
        ],
        out_shape=[
            jax.ShapeDtypeStruct(dproj.shape, dproj.dtype),
            jax.ShapeDtypeStruct((8, BLK, BLK), F32),
            jax.ShapeDtypeStruct((BLK, 8), F32),
            jax.ShapeDtypeStruct((1, D), F32),
            jax.ShapeDtypeStruct((1, D), F32),
        ],
        input_output_aliases={6: 0},
        scratch_shapes=[pltpu.VMEM((tm, D), F32), pltpu.VMEM((BLK, D), F32)],
        compiler_params=_cp(("arbitrary",), 48),
    )(z, dy, ws, bst, ln_g, ln_b, dproj)


def _merge_fwd(ya, yg, glog, bgate, h1, wabt, wgb, wo, ln_g, ln_b, tm=256):
    def body(ya_ref, yg_ref, gl_ref, bg_ref, h1_ref, wab_ref, wgb_ref, wo_ref, g_ref, b_ref,
             h_ref, hb_ref, xh_ref, rs_ref, mg_ref, bra_ref, brg_ref):
        bra = _dot_nt(ya_ref[...], wab_ref[...])
        brg = _dot_nn(yg_ref[...], wgb_ref[...])
        gates = _sigmoid(gl_ref[...] + bg_ref[...])
        merged = (gates[:, :D] * bra + gates[:, D:] * brg).astype(BF16)
        mix = _dot_nn(merged, wo_ref[...])
        h, xhat, rstd = _ln_fwd(ALPHA * h1_ref[...] + mix, g_ref[...], b_ref[...])
        h_ref[...] = h
        hb_ref[...] = h.astype(BF16)
        xh_ref[...] = xhat
        rs_ref[...] = rstd
        mg_ref[...] = merged
        bra_ref[...] = bra
        brg_ref[...] = brg

    row = pl.BlockSpec((tm, D), lambda i: (i, 0))
    vec = pl.BlockSpec((1, D), lambda i: (0, 0))
    full = lambda shape: pl.BlockSpec(shape, lambda i: (0, 0))
    return pl.pallas_call(
        body,
        name="merge_fwd",
        grid=(S // tm,),
        in_specs=[
            pl.BlockSpec((tm, GW), lambda i: (i, 0)), row,
            pl.BlockSpec((tm, 2 * D), lambda i: (i, glog.shape[1] // (2 * D) - 1)),
            full((1, 2 * D)), row,
            full((D, GW)), full((D, D)), full((D, D)), vec, vec,
        ],
        out_specs=[row, row, row, pl.BlockSpec((tm, 1), lambda i: (i, 0)), row, row, row],
        out_shape=[
            jax.ShapeDtypeStruct((S, D), F32),
            jax.ShapeDtypeStruct((S, D), BF16),
            jax.ShapeDtypeStruct((S, D), F32),
            jax.ShapeDtypeStruct((S, 1), F32),
            jax.ShapeDtypeStruct((S, D), BF16),
            jax.ShapeDtypeStruct((S, D), F32),
            jax.ShapeDtypeStruct((S, D), F32),
        ],
        compiler_params=_cp(("parallel",), 48),
    )(ya, yg, glog, bgate, h1, wabt, wgb, wo, ln_g, ln_b)


def _merge_bwd(dh2, xhat, rstd, ln_g, bra, brg, glog, bgate, wabt, wgb, wo, tm=256):
    def body(dh_ref, xh_ref, rs_ref, g_ref, bra_ref, brg_ref, gl_ref, bg_ref, wab_ref, wgb_ref, wo_ref,
             dr_ref, drb_ref, dlog_ref, dba_ref, dbg_ref, dya_ref, dyg_ref, dbgate_ref, dg_ref, dbias_ref):
        i = pl.program_id(0)
        dh = dh_ref[...]
        xh = xh_ref[...]
        dr = _ln_bwd(dh, xh, rs_ref[...], g_ref[...])
        drb = dr.astype(BF16)
        dr_ref[...] = dr
        drb_ref[...] = drb
        dmerged = _dot_nt(drb, wo_ref[...])
        gates = _sigmoid(gl_ref[...] + bg_ref[...])
        g0, g1 = gates[:, :D], gates[:, D:]
        dl0 = dmerged * bra_ref[...] * g0 * (1.0 - g0)
        dl1 = dmerged * brg_ref[...] * g1 * (1.0 - g1)
        dlog_ref[:, :D] = dl0.astype(BF16)
        dlog_ref[:, D:] = dl1.astype(BF16)
        dba = (dmerged * g0).astype(BF16)
        dbg = (dmerged * g1).astype(BF16)
        dba_ref[...] = dba
        dbg_ref[...] = dbg
        dya_ref[...] = _dot_nn(dba, wab_ref[...])
        dyg_ref[...] = _dot_nt(dbg, wgb_ref[...])
        s0, s1 = _colsum(dl0), _colsum(dl1)
        sg, sb = _colsum(dh * xh), _colsum(dh)

        @pl.when(i == 0)
        def _():
            dbgate_ref[:, :D] = s0
            dbgate_ref[:, D:] = s1
            dg_ref[...] = sg
            dbias_ref[...] = sb

        @pl.when(i > 0)
        def _():
            dbgate_ref[:, :D] += s0
            dbgate_ref[:, D:] += s1
            dg_ref[...] += sg
            dbias_ref[...] += sb

    row = pl.BlockSpec((tm, D), lambda i: (i, 0))
    vec = pl.BlockSpec((1, D), lambda i: (0, 0))
    wide = pl.BlockSpec((tm, 2 * D), lambda i: (i, 0))
    full = lambda shape: pl.BlockSpec(shape, lambda i: (0, 0))
    return pl.pallas_call(
        body,
        name="merge_bwd",
        grid=(S // tm,),
        in_specs=[row, row, pl.BlockSpec((tm, 1), lambda i: (i, 0)), vec, row, row,
                  pl.BlockSpec((tm, 2 * D), lambda i: (i, glog.shape[1] // (2 * D) - 1)),
                  full((1, 2 * D)), full((D, GW)), full((D, D)), full((D, D))],
        out_specs=[row, row, pl.BlockSpec((tm, 2 * D), lambda i: (i, DPROJ_G_COL)), row, row,
                   pl.BlockSpec((tm, GW), lambda i: (i, 0)), row, full((1, 2 * D)), vec, vec],
        out_shape=[
            jax.ShapeDtypeStruct((S, D), F32),
            jax.ShapeDtypeStruct((S, D), BF16),
            jax.ShapeDtypeStruct((S, DPROJ_W), BF16),
            jax.ShapeDtypeStruct((S, D), BF16),
            jax.ShapeDtypeStruct((S, D), BF16),
            jax.ShapeDtypeStruct((S, GW), F32),
            jax.ShapeDtypeStruct((S, D), F32),
            jax.ShapeDtypeStruct((1, 2 * D), F32),
            jax.ShapeDtypeStruct((1, D), F32),
            jax.ShapeDtypeStruct((1, D), F32),
        ],
        compiler_params=_cp(("arbitrary",), 48),
    )(dh2, xhat, rstd, ln_g, bra, brg, glog, bgate, wabt, wgb, wo)


def _loss_head(h3, target, tm=512):
    def body(h_ref, t_ref, d_ref, l_ref):
        i = pl.program_id(0)
        e = h_ref[...] - t_ref[...]
        d_ref[...] = e * (1.0 / D)
        part = jnp.sum(_colsum(e * e), axis=1, keepdims=True) * (0.5 / D)

        @pl.when(i == 0)
        def _():
            l_ref[...] = part

        @pl.when(i > 0)
        def _():
            l_ref[...] += part

    row = pl.BlockSpec((tm, D), lambda i: (i, 0))
    return pl.pallas_call(
        body,
        name="loss_head",
        grid=(S // tm,),
        in_specs=[row, row],
        out_specs=[row, pl.BlockSpec((1, 1), lambda i: (0, 0))],
        out_shape=[jax.ShapeDtypeStruct((S, D), F32), jax.ShapeDtypeStruct((1, 1), F32)],
        compiler_params=_cp(("arbitrary",)),
    )(h3, target)


def _tie(x, dep):
    if dep is None:
        return x
    return x + dep[0, 0].astype(x.dtype)


def _local_step(x, pos_col, target, get_w, p, emit):
    w = get_w("ffn1", None)
    a1, b1, hm1 = _ffn_up(x, w["g1"], w["u1"], "ffn1_up")
    w.update(get_w("ffn1d", hm1))
    h1, h1b, xh1, rs1 = _ffn_down(hm1, w["d1"], x, p["ln1_g"], p["ln1_b"], "ffn1_down")

    w.update(get_w("win", h1b))
    qkv = _matmul(h1b, w["win"], "nt", F32, 1024, 1536, D, "proj_qkv", b_off=0, n_out=QKV_W, dep=w.get("_dep"))
    z = glog = _matmul(h1b, w["win"], "nt", F32, 1024, 512, D, "proj_zg", b_off=QKV_W // 512, n_out=4 * D)

    half = jnp.arange(0, HEAD_DIM, 2, dtype=F32) / HEAD_DIM
    inv_freq = ROPE_THETA ** (-half)
    invf = jnp.tile(inv_freq, 4).reshape(1, BLK)
    sign = jnp.tile(jnp.concatenate([-jnp.ones((32,), F32), jnp.ones((32,), F32)]), 2).reshape(1, BLK)
    cos_t, sin_t = _rope_tables(pos_col, invf, sign)

    get_w("mix", qkv, early=True)
    preps = _qkv_prep(qkv, cos_t, sin_t)
    os_, lses = _attn_fwd(preps)
    ya = _combine_fwd(os_, lses)
    bst = p["gmlp_b_s"].T
    yg = _gmlp_fwd(z, p["gmlp_w_s"], bst, p["gmlp_ln_g"], p["gmlp_ln_b"])
    w.update(get_w("mix", yg))
    get_w("ffn2", ya, early=True)
    h2, h2b, xh2, rs2, merged, bra, brg = _merge_fwd(ya, yg, glog, p["b_gates"], h1, w["ab"], w["gb"], w["o"],
                                                      p["ln2_g"], p["ln2_b"])
    w.update(get_w("ffn2", h2b))
    a2, b2, hm2 = _ffn_up(h2b, w["g2"], w["u2"], "ffn2_up")
    h3, _, xh3, rs3 = _ffn_down(hm2, w["d2"], h2, p["ln3_g"], p["ln3_b"], "ffn2_down")
    dh3, loss = _loss_head(h3, target)

    gp = {}
    dr3, df2, da2, db2, gp["ln3_g"], gp["ln3_b"] = _ffn_bwd_mid(dh3, None, xh3, rs3, p["ln3_g"], a2, b2, w["d2"],
                                                                "ffn2_bwd_mid")
    tok = emit("ffn2", {
        "g2": _matmul(da2, h2b, "tn", BF16, 1408, D, S, "wgrad_g2"),
        "u2": _matmul(db2, h2b, "tn", BF16, 1408, D, S, "wgrad_u2"),
        "d2": _matmul(hm2, df2, "tn", BF16, 1408, D, S, "wgrad_d2")})
    dh2 = _ffn_bwd_dx(dr3, da2, db2, w["g2"], w["u2"], "ffn2_bwd_dx")

    (dr2, dr2b, dproj, dba, dbg, dya, dyg, gp["b_gates"], gp["ln2_g"], gp["ln2_b"]) = _merge_bwd(
        dh2, xh2, rs2, _tie(p["ln2_g"], tok), bra, brg, glog, p["b_gates"], w["ab"], w["gb"], w["o"])
    tok = emit("mix", {
        "o": _matmul(merged, dr2b, "tn", BF16, 512, D, S, "wgrad_o"),
        "ab": _matmul(dba, ya, "tn", BF16, 512, GW, S, "wgrad_ab"),
        "gb": _matmul(yg, dbg, "tn", BF16, 512, D, S, "wgrad_gb")})

    seg = (jnp.arange(GW)[:, None] // HEAD_DIM == jnp.arange(GW)[None, :] // HEAD_DIM).astype(BF16)
    do0, do1, do2, c0, c1, c2 = _combine_bwd(dya, os_, lses, _tie(seg, tok))
    dproj = _qkv_unprep(_attn_bwd(preps, (do0, do1, do2), lses, (c0, c1, c2)), cos_t, sin_t, dproj)
    dproj, gp["gmlp_w_s"], dbst, gp["gmlp_ln_g"], gp["gmlp_ln_b"] = _gmlp_bwd(
        z, dyg, p["gmlp_w_s"], bst, p["gmlp_ln_g"], p["gmlp_ln_b"], dproj)
    gp["gmlp_b_s"] = dbst.T
    tok = emit("win", {"win": _matmul(dproj, h1b, "tn", BF16, 512, D, S, "wgrad_win", a_map=_dproj_tile, m_out=IN_W)})
    dh1m = _matmul(dproj, w["win"], "nn", F32, S, D, 512, "dproj_to_dh1", dep=tok, a_map=_dproj_tile)

    dr1, df1, da1, db1, gp["ln1_g"], gp["ln1_b"] = _ffn_bwd_mid(dr2, dh1m, xh1, rs1, p["ln1_g"], a1, b1, w["d1"],
                                                                "ffn1_bwd_mid")
    tok = emit("small", {**gp, "loss": loss})
    tok = emit("g1", {"g1": _matmul(da1, x, "tn", BF16, 1408, D, S, "wgrad_g1", dep=tok)})
    tok = emit("u1", {"u1": _matmul(db1, x, "tn", BF16, 1408, D, S, "wgrad_u1", dep=tok)})
    tok = emit("d1", {"d1": _matmul(hm1, df1, "tn", BF16, 1408, D, S, "wgrad_d1", dep=tok)})
    dx = _ffn_bwd_dx(dr1, da1, db1, w["g1"], w["u1"], "ffn1_bwd_dx", dep=tok)
    return loss, dx, gp


_FLIPS = [(mx, my, mc) for mx in (0, 1) for my in (0, 1) for mc in (0, 1)][1:]
_GROUPS = {"ffn1": ("g1", "u1"), "ffn1d": ("d1",), "win": ("win",), "mix": ("ab", "gb", "o"), "ffn2": ("g2", "u2", "d2")}
_SCATTERS = {"ffn2": ("g2", "u2", "d2"), "mix": ("ab", "gb", "o"), "win": ("win",), "g1": ("g1",), "u1": ("u1",), "d1": ("d1",)}
_TWO_STAGE = ("g1", "u1", "d1")
_HBM = pl.BlockSpec(memory_space=pltpu.HBM)
_SEM = pl.BlockSpec(memory_space=pltpu.SEMAPHORE)
_EFFECT = pltpu.SideEffectType.DATAFLOW_SIDE_EFFECTING


def _me():
    return 4 * lax.axis_index("x") + 2 * lax.axis_index("y") + lax.axis_index("c")


def _copies_start(bufs, copies, n_sem, name, after=None):
    nb = len(bufs)
    n_after = 0 if after is None else 1

    def body(*refs):
        b = refs[:nb]
        send_sems, recv_sems = refs[nb + n_after], refs[nb + n_after + 1]
        token = refs[-1]
        x, y, c = lax.axis_index("x"), lax.axis_index("y"), lax.axis_index("c")
        me = 4 * x + 2 * y + c
        for si, s_slot, di, d_slot, (mx, my, mc), sem in copies:
            s_idx, d_idx = s_slot(me), d_slot(me)
            pltpu.make_async_remote_copy(
                src_ref=b[si] if s_idx is None else b[si].at[s_idx],
                dst_ref=b[di] if d_idx is None else b[di].at[d_idx],
                send_sem=send_sems.at[sem], recv_sem=recv_sems.at[sem],
                device_id=(x ^ mx, y ^ my, c ^ mc), device_id_type=MESH).start()
        token[...] = jnp.zeros_like(token)

    ins = [pltpu.with_memory_space_constraint(a, pltpu.HBM) for a in bufs]
    outs = pl.pallas_call(
        body,
        name=name,
        in_specs=[_HBM] * nb + [pl.BlockSpec(memory_space=pl.ANY)] * n_after,
        out_specs=[_SEM, _SEM] + [_HBM] * nb + [pl.BlockSpec(memory_space=pltpu.VMEM)],
        out_shape=[pltpu.SemaphoreType.DMA((n_sem,)), pltpu.SemaphoreType.DMA((n_sem,))]
        + [pltpu.HBM(a.shape, a.dtype) for a in ins] + [jax.ShapeDtypeStruct((8, 128), F32)],
        input_output_aliases={k: 2 + k for k in range(nb)},
        compiler_params=pltpu.CompilerParams(has_side_effects=_EFFECT),
    )(*ins, *([] if after is None else [after]))
    return outs[0], outs[1], list(outs[2:2 + nb]), outs[-1]


def _copies_wait(started, waits, after, name):
    send_sems, recv_sems, bufs, _ = started
    nb = len(bufs)

    def body(*refs):
        b = refs[:nb]
        ss, rs = refs[nb], refs[nb + 1]
        me3 = (lax.axis_index("x"), lax.axis_index("y"), lax.axis_index("c"))
        for bi, n_blocks, sem, is_send in waits:
            blocks = b[bi].at[pl.ds(0, n_blocks)]
            cp = pltpu.make_async_remote_copy(src_ref=blocks, dst_ref=blocks, send_sem=ss.at[sem], recv_sem=rs.at[sem],
                                              device_id=me3, device_id_type=MESH)
            if is_send:
                cp.wait_send()
            else:
                cp.wait_recv()

    return pl.pallas_call(
        body,
        name=name,
        in_specs=[_HBM] * nb + [_SEM, _SEM, pl.BlockSpec(memory_space=pl.ANY)],
        out_specs=[_HBM] * nb,
        out_shape=[pltpu.HBM(a.shape, a.dtype) for a in bufs],
        input_output_aliases={k: k for k in range(nb)},
        compiler_params=pltpu.CompilerParams(has_side_effects=_EFFECT),
    )(*bufs, send_sems, recv_sems, after)


def _landing(own, me):
    return lax.dynamic_update_slice(lax.empty((N_DEV,) + own.shape[1:], own.dtype), own, (me, 0, 0))


_SIBLING = (0, 0, 1)
_OTHER_CHIPS = ((1, 0, 0), (0, 1, 0), (1, 1, 0))


def _bits(flip):
    return 4 * flip[0] + 2 * flip[1] + flip[2]


def _gather_send(shards, grp, after):
    nw = len(shards)
    me = _me()
    bufs = list(shards) + [_landing(a[None], me) for a in shards]
    copies = []
    for k in range(nw):
        copies.append((k, lambda me: None, nw + k, lambda me: me, _SIBLING, nw + k))
        copies += [(k, lambda me: None, nw + k, lambda me: me, f, k) for f in _OTHER_CHIPS]
    return _copies_start(bufs, copies, 2 * nw, "gather_send_" + grp, after)


def _gather_pass(started, grp, after):
    nw = len(started[2]) // 2
    waits = []
    for k in range(nw):
        waits += [(nw + k, 3, k, True), (nw + k, 1, nw + k, True), (nw + k, 3, k, False), (nw + k, 1, nw + k, False)]
    lands = list(_copies_wait(started, waits, after, "gather_arrived_" + grp)[nw:])
    copies = []
    for k in range(nw):
        for f in _OTHER_CHIPS:
            slot = functools.partial(lambda me, bits: me ^ bits, bits=_bits(f))
            copies.append((k, slot, k, slot, _SIBLING, k))
    return _copies_start(lands, copies, nw, "gather_pass_" + grp)


def _gather_finish(passed, grp, after):
    nw = len(passed[2])
    waits = [(k, 3, k, is_send) for k in range(nw) for is_send in (True, False)]
    return _copies_wait(passed, waits, passed[3] if after is None else after, "gather_done_" + grp)


def _scatter_send(parts, grp, after=None):
    nw = len(parts)
    me = _me()
    bufs = list(parts) + [_landing(lax.dynamic_index_in_dim(a, me, 0, keepdims=True), me) for a in parts]
    copies = []
    for k in range(nw):
        for f in _FLIPS:
            to = functools.partial(lambda me, bits: me ^ bits, bits=_bits(f))
            copies.append((k, to, nw + k, lambda me: me, f, k))
    return _copies_start(bufs, copies, nw, "scatter_send_" + grp, after)


def _scatter_finish(started, grp, after):
    nw = len(started[2]) // 2
    waits = [(nw + k, N_DEV - 1, k, is_send) for k in range(nw) for is_send in (True, False)]
    return _copies_wait(started, waits, after, "scatter_done_" + grp)[nw:]


N_CHIP = N_DEV // 2


def _pair_sum(part, other, key):
    _, r, c = part.shape
    tm = _ADAM_ROWS.get(r, r)
    core = lax.axis_index("c").astype(jnp.int32).reshape(1)

    def body(core_ref, a_ref, b_ref, o_ref):
        o_ref[...] = (a_ref[...].astype(F32) + b_ref[...].astype(F32)).astype(BF16)

    return pl.pallas_call(
        body,
        name="pair_sum_" + key,
        grid_spec=pltpu.PrefetchScalarGridSpec(
            num_scalar_prefetch=1,
            grid=(N_CHIP, r // tm),
            in_specs=[pl.BlockSpec((None, tm, c), lambda q, i, core_ref: (2 * q + core_ref[0], i, 0)),
                      pl.BlockSpec((None, tm, c), lambda q, i, core_ref: (q, i, 0))],
            out_specs=pl.BlockSpec((None, tm, c), lambda q, i, core_ref: (q, i, 0)),
        ),
        out_shape=jax.ShapeDtypeStruct((N_CHIP, r, c), BF16),
        compiler_params=_cp(("parallel", "parallel")),
    )(core, part, other)


def _scatter2_send(part, key, dep=None):
    me = _me()
    _, r, c = part.shape
    swap = []
    for q in range(N_CHIP):
        src = functools.partial(lambda me, q: 2 * q + 1 - me % 2, q=q)
        swap.append((0, src, 1, functools.partial(lambda me, q: q, q=q), _SIBLING, 0))
    started = _copies_start([part, lax.empty((N_CHIP, r, c), part.dtype)], swap, 1, "scatter_swap_" + key, dep)
    waits = [(1, N_CHIP, 0, True), (1, N_CHIP, 0, False)]
    part, other = _copies_wait(started, waits, started[3], "scatter_swapped_" + key)
    pair = _pair_sum(part, other, key)
    chip = me // 2
    land = lax.dynamic_update_slice(lax.empty((N_CHIP, r, c), pair.dtype),
                                    lax.dynamic_index_in_dim(pair, chip, 0, keepdims=True), (chip, 0, 0))
    copies = []
    for f in _OTHER_CHIPS:
        to = functools.partial(lambda me, bits: (me ^ bits) // 2, bits=_bits(f))
        copies.append((0, to, 1, lambda me: me // 2, f, 0))
    return _copies_start([pair, land], copies, 1, "scatter_send_" + key)


def _scatter2_finish(started, key, after):
    waits = [(1, N_CHIP - 1, 0, True), (1, N_CHIP - 1, 0, False)]
    return _copies_wait(started, waits, after, "scatter_done_" + key)[1]


def _allgather_send(block, name, after=None):
    me = _me()
    copies = [(0, lambda me: None, 1, lambda me: me, f, 0) for f in _FLIPS]
    return _copies_start([block, _landing(block[None], me)], copies, 1, name, after)


def _allgather_finish(started, name, after):
    waits = [(1, N_DEV - 1, 0, True), (1, N_DEV - 1, 0, False)]
    return _copies_wait(started, waits, after, name)[1]


_ADAM_ROWS = {352: 176, 1088: 272}


def _sum_adamw(parts, wv, m, v, name):
    n_parts, r, c = parts.shape
    tm = _ADAM_ROWS.get(r, r)
    assert r % tm == 0 and wv.shape == (r, c)

    def body(p_ref, w_ref, m_ref, v_ref, g_ref, d_ref, mo_ref, vo_ref):
        gv = p_ref[0].astype(F32)
        for j in range(1, n_parts):
            gv = gv + p_ref[j].astype(F32)
        g_ref[...] = gv
        mn = ADAM_B1 * m_ref[...] + (1.0 - ADAM_B1) * gv
        vn = ADAM_B2 * v_ref[...] + (1.0 - ADAM_B2) * (gv * gv)
        m_hat = mn / (1.0 - ADAM_B1 ** ADAM_STEP)
        v_hat = vn / (1.0 - ADAM_B2 ** ADAM_STEP)
        d_ref[...] = -ADAM_LR * (m_hat / (jnp.sqrt(v_hat) + ADAM_EPS) + ADAM_WD * w_ref[...])
        mo_ref[...] = mn
        vo_ref[...] = vn

    sp = pl.BlockSpec((tm, c), lambda i: (i, 0))
    return pl.pallas_call(
        body,
        name=name,
        grid=(r // tm,),
        in_specs=[pl.BlockSpec((n_parts, tm, c), lambda i: (0, i, 0))] + [sp] * 3,
        out_specs=[sp] * 4,
        out_shape=[jax.ShapeDtypeStruct((r, c), F32)] * 4,
        compiler_params=_cp(("parallel",), 48),
    )(parts, wv, m, v)


_WEIGHTS = ["ffn1_w_gate", "ffn1_w_up", "ffn1_w_down", "ln1_g", "ln1_b", "w_in", "b_gates", "gmlp_ln_g", "gmlp_ln_b",
            "gmlp_w_s", "gmlp_b_s", "w_attn_branch", "w_gmlp_branch", "w_out", "ln2_g", "ln2_b", "ffn2_w_gate",
            "ffn2_w_up", "ffn2_w_down", "ln3_g", "ln3_b"]
_BIG_OF = {"ffn1_w_gate": ("g1", True), "ffn1_w_up": ("u1", True), "ffn1_w_down": ("d1", False), "w_in": ("win", True),
           "w_attn_branch": ("ab", True), "w_gmlp_branch": ("gb", False), "w_out": ("o", False),
           "ffn2_w_gate": ("g2", True), "ffn2_w_up": ("u2", True), "ffn2_w_down": ("d2", False)}
_SMALL = [n for n in _WEIGHTS if n not in _BIG_OF]
_SMALL_ROWS = {"gmlp_w_s": 128, "b_gates": 2}
_SMALL_SLOT = 8


def _pack_small(d, last=None):
    rows = []
    for n in _SMALL:
        r = d[n].reshape(-1, D)
        slot = max(r.shape[0], _SMALL_SLOT)
        rows.append(jnp.pad(r, ((0, slot - r.shape[0]), (0, 0))))
    rows.append(jnp.zeros((_SMALL_SLOT, D), F32) if last is None else jnp.broadcast_to(last.reshape(1, 1), (_SMALL_SLOT, D)))
    return jnp.concatenate(rows, axis=0)


def _unpack_small(packed, shapes):
    out, at = {}, 0
    for n in _SMALL:
        k = _SMALL_ROWS.get(n, 1)
        out[n] = packed[at:at + k].reshape(shapes[n])
        at += max(k, _SMALL_SLOT)
    return out


def kernel(x, positions, ffn1_w_gate, ffn1_w_up, ffn1_w_down, ln1_g, ln1_b, w_in, b_gates, gmlp_ln_g, gmlp_ln_b, gmlp_w_s, gmlp_b_s, w_attn_branch, w_gmlp_branch, w_out, ln2_g, ln2_b, ffn2_w_gate, ffn2_w_up, ffn2_w_down, ln3_g, ln3_b, loss_target, m_ffn1_w_gate, m_ffn1_w_up, m_ffn1_w_down, m_ln1_g, m_ln1_b, m_w_in, m_b_gates, m_gmlp_ln_g, m_gmlp_ln_b, m_gmlp_w_s, m_gmlp_b_s, m_w_attn_branch, m_w_gmlp_branch, m_w_out, m_ln2_g, m_ln2_b, m_ffn2_w_gate, m_ffn2_w_up, m_ffn2_w_down, m_ln3_g, m_ln3_b, v_ffn1_w_gate, v_ffn1_w_up, v_ffn1_w_down, v_ln1_g, v_ln1_b, v_w_in, v_b_gates, v_gmlp_ln_g, v_gmlp_ln_b, v_gmlp_w_s, v_gmlp_b_s, v_w_attn_branch, v_w_gmlp_branch, v_w_out, v_ln2_g, v_ln2_b, v_ffn2_w_gate, v_ffn2_w_up, v_ffn2_w_down, v_ln3_g, v_ln3_b):
    args = dict(locals())
    wts = {n: args[n] for n in _WEIGHTS}
    ms = {n: args["m_" + n] for n in _WEIGHTS}
    vs = {n: args["v_" + n] for n in _WEIGHTS}

    name_of = {key: (n, tr) for n, (key, tr) in _BIG_OF.items()}

    shards = {}
    for grp, keys in _GROUPS.items():
        shards[grp] = []
        for key in keys:
            n, tr = name_of[key]
            s2 = wts[n][0]
            shards[grp].append((s2.T if tr else s2).astype(BF16))
    started, tok = {}, None
    for grp in _GROUPS:
        started[grp] = _gather_send(shards[grp], grp, tok)
        tok = started[grp][3]
    all_started = tok
    passed = {}

    def get_w(grp, after, early=False):
        if grp not in passed:
            passed[grp] = _gather_pass(started[grp], grp, all_started if after is None else after)
            after = None
        if early:
            return None
        lands = _gather_finish(passed[grp], grp, after)
        return {key: g.reshape(-1, g.shape[-1]) for key, g in zip(_GROUPS[grp], lands)}

    sent = {}

    def emit(grp, grads):
        if grp == "small":
            sent[grp] = _allgather_send(_pack_small(grads, last=grads["loss"]), "small_grads_send")
        else:
            parts = [grads[key].reshape(N_DEV, -1, grads[key].shape[-1]) for key in _SCATTERS[grp]]
            sent[grp] = _scatter2_send(parts[0], grp) if grp in _TWO_STAGE else _scatter_send(parts, grp)
        return sent[grp][3]

    p = {n: (wts[n][0] if n in ("gmlp_w_s", "gmlp_b_s") else wts[n]) for n in _SMALL}
    loss, dx, gp = _local_step(x[0], positions.reshape(S, 1), loss_target[0], get_w, p, emit)
    grads, deltas, new_m, new_v = {}, {}, {}, {}
    after = dx
    for grp in ("ffn2", "mix", "win", "small", "g1", "u1", "d1"):
        if grp == "small":
            parts = _allgather_finish(sent[grp], "small_grads_done", after)
            outs = _sum_adamw(parts, *[_pack_small({n: d[n] for n in _SMALL}) for d in (wts, ms, vs)], "update_small")
            shapes = {n: wts[n].shape for n in _SMALL}
            for dst, packed in zip((grads, deltas, new_m, new_v), outs):
                dst.update(_unpack_small(packed, shapes))
            loss = outs[0][-_SMALL_SLOT, 0]
            after = outs[1]
            continue
        arrived = ([_scatter2_finish(sent[grp], grp, after)] if grp in _TWO_STAGE
                   else _scatter_finish(sent[grp], grp, after))
        for key, part in zip(_SCATTERS[grp], arrived):
            n, tr = name_of[key]
            outs = _sum_adamw(part, *[(d[n][0].T if tr else d[n][0]) for d in (wts, ms, vs)], "update_" + key)
            for dst, o in zip((grads, deltas, new_m, new_v), outs):
                dst[n] = (o.T if tr else o)[None]
            after = outs[1]

    return (loss, dx[None], *[grads[n] for n in _WEIGHTS], *[deltas[n] for n in _WEIGHTS],
            *[new_m[n] for n in _WEIGHTS], *[new_v[n] for n in _WEIGHTS])
```

```python
import functools
import math

import jax
import jax.numpy as jnp
from jax import lax
from jax.experimental import pallas as pl
from jax.experimental.pallas import tpu as pltpu

F32 = jnp.float32
BF16 = jnp.bfloat16

N_DEV = 8
D = 1024
S = 2048
F = 2816
HEAD_DIM = 64
HEADS = 8
GW = HEADS * HEAD_DIM
PATTERN_DILATIONS = (1, 4, 16)
BLK = 128
QKV_W = 3 * 3 * GW
IN_W = QKV_W + 2 * D + 2 * D
DPROJ_W = 5 * 2 * D
DPROJ_Z_COL, DPROJ_G_COL = 3, 4


def _dproj_tile(t):
    three = jnp.int32(3)
    return jnp.where(t < 9, lax.rem(t, three) * 3 + lax.div(t, three), t + 3)
ROPE_THETA = 10000.0
ALPHA = 2.0 ** 0.25
LN_EPS = 1e-5
ADAM_LR, ADAM_B1, ADAM_B2, ADAM_EPS, ADAM_WD, ADAM_STEP = 0.001, 0.9, 0.999, 1e-08, 0.01, 10
NEG = -1e30
MESH = pl.DeviceIdType.MESH


def _cp(sem=None, vmem_mb=None):
    kw = {}
    if sem is not None:
        kw["dimension_semantics"] = sem
    if vmem_mb is not None:
        kw["vmem_limit_bytes"] = vmem_mb << 20
    return pltpu.CompilerParams(**kw)


def _dot_nn(a, b):
    return lax.dot_general(a, b, (((1,), (0,)), ((), ())), preferred_element_type=F32)


def _dot_nt(a, b):
    return lax.dot_general(a, b, (((1,), (1,)), ((), ())), preferred_element_type=F32)


def _dot_tn(a, b):
    return lax.dot_general(a, b, (((0,), (0,)), ((), ())), preferred_element_type=F32)


def _ln_fwd(r, g, b):
    mu = jnp.mean(r, axis=-1, keepdims=True)
    xc = r - mu
    var = jnp.mean(xc * xc, axis=-1, keepdims=True)
    rstd = lax.rsqrt(var + LN_EPS)
    xhat = xc * rstd
    return xhat * g + b, xhat, rstd


def _ln_bwd(dh, xhat, rstd, g):
    dxh = dh * g
    m1 = jnp.mean(dxh, axis=-1, keepdims=True)
    m2 = jnp.mean(dxh * xhat, axis=-1, keepdims=True)
    return rstd * (dxh - m1 - xhat * m2)


def _sigmoid(x):
    return 0.5 * jnp.tanh(0.5 * x) + 0.5


def _colsum(x):
    return jnp.sum(x, axis=0, keepdims=True)


def _matmul(a, b, mode, out_dtype, tm, tn, tk, name, b_off=0, n_out=None, dep=None, a_map=None, m_out=None):
    n_dep = 0 if dep is None else 1
    a_map = a_map or (lambda t: t)
    if mode == "nn":
        m, k = a.shape[0], b.shape[0]
        n = b.shape[1]
    elif mode == "nt":
        m, k = a.shape
        n = n_out if n_out is not None else b.shape[0]
    else:
        k, m = a.shape[0], m_out or a.shape[1]
        n = b.shape[1]
    nk = k // tk
    assert m % tm == 0 and n % tn == 0 and k % tk == 0
    dot = {"nn": _dot_nn, "nt": _dot_nt, "tn": _dot_tn}[mode]

    def body(a_ref, b_ref, *rest):
        o_ref, scr = rest[n_dep], rest[n_dep + 1:]
        r = dot(a_ref[...].astype(BF16), b_ref[...].astype(BF16))
        if nk == 1:
            o_ref[...] = r.astype(out_dtype)
        else:
            acc = scr[0]
            kk = pl.program_id(2)

            @pl.when(kk == 0)
            def _():
                acc[...] = r

            @pl.when(kk > 0)
            def _():
                acc[...] += r

            @pl.when(kk == nk - 1)
            def _():
                o_ref[...] = acc[...].astype(out_dtype)

    if mode == "nn":
        a_spec = pl.BlockSpec((tm, tk), lambda i, j, kk: (i, a_map(kk)))
        b_spec = pl.BlockSpec((tk, tn), lambda i, j, kk: (kk, j))
    elif mode == "nt":
        a_spec = pl.BlockSpec((tm, tk), lambda i, j, kk: (i, kk))
        b_spec = pl.BlockSpec((tn, tk), lambda i, j, kk: (j + b_off, kk))
    else:
        a_spec = pl.BlockSpec((tk, tm), lambda i, j, kk: (kk, a_map(i)))
        b_spec = pl.BlockSpec((tk, tn), lambda i, j, kk: (kk, j))
    return pl.pallas_call(
        body,
        name=name,
        grid=(m // tm, n // tn, nk),
        in_specs=[a_spec, b_spec] + [pl.BlockSpec(memory_space=pl.ANY)] * n_dep,
        out_specs=pl.BlockSpec((tm, tn), lambda i, j, kk: (i, j)),
        out_shape=jax.ShapeDtypeStruct((m, n), out_dtype),
        scratch_shapes=[] if nk == 1 else [pltpu.VMEM((tm, tn), F32)],
        compiler_params=_cp(("parallel", "parallel", "arbitrary"), 56),
    )(a, b, *([] if dep is None else [dep]))


def _ffn_up(x, wgt, wut, name, tm=512, tn=1408, dep=None):
    n_dep = 0 if dep is None else 1

    def body(x_ref, wg_ref, wu_ref, *rest):
        ga_ref, gb_ref, hm_ref = rest[n_dep:]
        xb = x_ref[...].astype(BF16)
        a = _dot_nt(xb, wg_ref[...])
        b = _dot_nt(xb, wu_ref[...])
        sig = _sigmoid(a)
        silu = a * sig
        ga_ref[...] = (b * (sig + silu * (1.0 - sig))).astype(BF16)
        gb_ref[...] = silu.astype(BF16)
        hm_ref[...] = (silu * b).astype(BF16)

    wsp = pl.BlockSpec((tn, D), lambda i, j: (j, 0))
    mid = pl.BlockSpec((tm, tn), lambda i, j: (i, j))
    return pl.pallas_call(
        body,
        name=name,
        grid=(S // tm, F // tn),
        in_specs=[pl.BlockSpec((tm, D), lambda i, j: (i, 0)), wsp, wsp] + [pl.BlockSpec(memory_space=pl.ANY)] * n_dep,
        out_specs=[mid, mid, mid],
        out_shape=[jax.ShapeDtypeStruct((S, F), BF16)] * 3,
        compiler_params=_cp(("parallel", "arbitrary"), 56),
    )(x, wgt, wut, *([] if dep is None else [dep]))


def _ffn_down(hm, wd, x, ln_g, ln_b, name, tm=512):
    def body(hm_ref, wd_ref, x_ref, g_ref, b_ref, h_ref, hb_ref, xh_ref, rs_ref):
        r = ALPHA * x_ref[...] + 0.5 * _dot_nn(hm_ref[...], wd_ref[...])
        h, xhat, rstd = _ln_fwd(r, g_ref[...], b_ref[...])
        h_ref[...] = h
        hb_ref[...] = h.astype(BF16)
        xh_ref[...] = xhat
        rs_ref[...] = rstd

    row = pl.BlockSpec((tm, D), lambda i: (i, 0))
    vec = pl.BlockSpec((1, D), lambda i: (0, 0))
    return pl.pallas_call(
        body,
        name=name,
        grid=(S // tm,),
        in_specs=[pl.BlockSpec((tm, F), lambda i: (i, 0)), pl.BlockSpec((F, D), lambda i: (0, 0)), row, vec, vec],
        out_specs=[row, row, row, pl.BlockSpec((tm, 1), lambda i: (i, 0))],
        out_shape=[
            jax.ShapeDtypeStruct((S, D), F32),
            jax.ShapeDtypeStruct((S, D), BF16),
            jax.ShapeDtypeStruct((S, D), F32),
            jax.ShapeDtypeStruct((S, 1), F32),
        ],
        compiler_params=_cp(("parallel",), 56),
    )(hm, wd, x, ln_g, ln_b)


def _ffn_bwd_mid(dh_a, dh_b, xhat, rstd, ln_g, a, b, wd, name, tm=512, tn=1408):
    two = dh_b is not None

    def body(*refs):
        dha_ref = refs[0]
        dhb_ref = refs[1] if two else None
        (xh_ref, rs_ref, g_ref, a_ref, b_ref, wd_ref,
         dr_ref, df_ref, da_ref, db_ref, dg_ref, dbias_ref, df_scr) = refs[2 if two else 1:]
        i = pl.program_id(0)
        j = pl.program_id(1)

        @pl.when(j == 0)
        def _():
            dh = dha_ref[...]
            if two:
                dh = ALPHA * dh + dhb_ref[...]
            xhat = xh_ref[...]
            dr = _ln_bwd(dh, xhat, rs_ref[...], g_ref[...])
            dfb = (0.5 * dr).astype(BF16)
            dr_ref[...] = dr
            df_scr[...] = dfb
            df_ref[...] = dfb
            sg = _colsum(dh * xhat)
            sb = _colsum(dh)

            @pl.when(i == 0)
            def _():
                dg_ref[...] = sg
                dbias_ref[...] = sb

            @pl.when(i > 0)
            def _():
                dg_ref[...] += sg
                dbias_ref[...] += sb

        dhm = _dot_nt(df_scr[...], wd_ref[...])
        da_ref[...] = (dhm * a_ref[...].astype(F32)).astype(BF16)
        db_ref[...] = (dhm * b_ref[...].astype(F32)).astype(BF16)

    row = pl.BlockSpec((tm, D), lambda i, j: (i, 0))
    vec = pl.BlockSpec((1, D), lambda i, j: (0, 0))
    mid = pl.BlockSpec((tm, tn), lambda i, j: (i, j))
    ins = [dh_a] + ([dh_b] if two else []) + [xhat, rstd, ln_g, a, b, wd]
    in_specs = [row] * (2 if two else 1) + [row, pl.BlockSpec((tm, 1), lambda i, j: (i, 0)), vec, mid, mid,
                                            pl.BlockSpec((tn, D), lambda i, j: (j, 0))]
    return pl.pallas_call(
        body,
        name=name,
        grid=(S // tm, F // tn),
        in_specs=in_specs,
        out_specs=[row, row, mid, mid, vec, vec],
        out_shape=[
            jax.ShapeDtypeStruct((S, D), F32),
            jax.ShapeDtypeStruct((S, D), BF16),
            jax.ShapeDtypeStruct((S, F), BF16),
            jax.ShapeDtypeStruct((S, F), BF16),
            jax.ShapeDtypeStruct((1, D), F32),
            jax.ShapeDtypeStruct((1, D), F32),
        ],
        scratch_shapes=[pltpu.VMEM((tm, D), BF16)],
        compiler_params=_cp(("arbitrary", "arbitrary"), 56),
    )(*ins)


def _ffn_bwd_dx(dr, da, db, wgt, wut, name, tm=512, tk=1408, dep=None):
    nk = F // tk
    n_dep = 0 if dep is None else 1

    def body(dr_ref, da_ref, db_ref, wg_ref, wu_ref, *rest):
        dx_ref, acc = rest[n_dep], rest[n_dep + 1]
        kk = pl.program_id(1)
        part = _dot_nn(da_ref[...], wg_ref[...]) + _dot_nn(db_ref[...], wu_ref[...])

        @pl.when(kk == 0)
        def _():
            acc[...] = ALPHA * dr_ref[...] + part

        @pl.when(kk > 0)
        def _():
            acc[...] += part

        @pl.when(kk == nk - 1)
        def _():
            dx_ref[...] = acc[...]

    row = pl.BlockSpec((tm, D), lambda i, kk: (i, 0))
    mid = pl.BlockSpec((tm, tk), lambda i, kk: (i, kk))
    wsp = pl.BlockSpec((tk, D), lambda i, kk: (kk, 0))
    return pl.pallas_call(
        body,
        name=name,
        grid=(S // tm, nk),
        in_specs=[row, mid, mid, wsp, wsp] + [pl.BlockSpec(memory_space=pl.ANY)] * n_dep,
        out_specs=row,
        out_shape=jax.ShapeDtypeStruct((S, D), F32),
        scratch_shapes=[pltpu.VMEM((tm, D), F32)],
        compiler_params=_cp(("parallel", "arbitrary"), 56),
    )(dr, da, db, wgt, wut, *([] if dep is None else [dep]))


def _rope_tables(pos_col, invf, sign, tm=512):
    def body(p_ref, f_ref, s_ref, c_out, s_out):
        ang = p_ref[...].astype(F32) * f_ref[...]
        c_out[...] = jnp.cos(ang)
        s_out[...] = jnp.sin(ang) * s_ref[...]

    vec = pl.BlockSpec((1, BLK), lambda i: (0, 0))
    out = pl.BlockSpec((tm, BLK), lambda i: (i, 0))
    return pl.pallas_call(
        body,
        name="rope_tables",
        grid=(S // tm,),
        in_specs=[pl.BlockSpec((tm, 1), lambda i: (i, 0)), vec, vec],
        out_specs=[out, out],
        out_shape=[jax.ShapeDtypeStruct((S, BLK), F32)] * 2,
        compiler_params=_cp(("parallel",)),
    )(pos_col, invf, sign)


def _lane_lo(rows=BLK):
    return (lax.broadcasted_iota(jnp.int32, (rows, GW), 1) % HEAD_DIM) < (HEAD_DIM // 2)


def _swap_halves(t, lo):
    return jnp.where(lo, pltpu.roll(t, GW - HEAD_DIM // 2, 1), pltpu.roll(t, HEAD_DIM // 2, 1))


def _rope(t, cosf, sinf, lo):
    return t * cosf + _swap_halves(t, lo) * sinf


def _unrope(g, cosf, sinf, lo):
    return g * cosf + _swap_halves(g * sinf, lo)


def _tile4(v):
    return jnp.concatenate([v, v, v, v], axis=1)


def _band_mask(n):
    qi = lax.broadcasted_iota(jnp.int32, (BLK, 2 * BLK), 0)
    kj = lax.broadcasted_iota(jnp.int32, (BLK, 2 * BLK), 1)
    dist = qi + BLK - kj
    return (dist >= 0) & (dist <= BLK) & ((kj >= BLK) | (n >= 1))


ROWS = 256
LANES = 128


def _to_view(scr, y, dst_ref, d, dtype, col0=0):
    if d == 1:
        dst_ref[:, col0:col0 + GW] = y.astype(dtype)
        return
    for cb in range(GW // LANES):
        scr[cb][...] = y[:, cb * LANES:(cb + 1) * LANES]
    for r in range(d):
        for cb in range(GW // LANES):
            at = col0 + r * GW + cb * LANES
            dst_ref[:, at:at + LANES] = scr[cb][pl.ds(r, ROWS // d, stride=d), :].astype(dtype)


def _from_view(scr, src_ref, d):
    if d == 1:
        return src_ref[...].astype(F32)
    for r in range(d):
        for cb in range(GW // LANES):
            at = r * GW + cb * LANES
            scr[cb][pl.ds(r, ROWS // d, stride=d), :] = src_ref[:, at:at + LANES].astype(F32)
    return jnp.concatenate([scr[cb][...] for cb in range(GW // LANES)], axis=1)


def _view_spec(d):
    return pl.BlockSpec((ROWS // d, d * GW), lambda i: (i, 0))


_VIEW_SCRATCH = [pltpu.VMEM((ROWS, LANES), F32)] * (GW // LANES)


def _qkv_prep(qkv, cos_t, sin_t):
    def body(x_ref, c_ref, s_ref, *rest):
        outs, scr = rest[:9], rest[9:]
        lo = _lane_lo(ROWS)
        cf, sf = _tile4(c_ref[...]), _tile4(s_ref[...])
        for gi, d in enumerate(PATTERN_DILATIONS):
            q, k, v = (x_ref[:, (3 * part + gi) * GW:(3 * part + gi + 1) * GW] for part in range(3))
            _to_view(scr, _rope(q, cf, sf, lo) * (HEAD_DIM ** -0.5), outs[3 * gi], d, BF16)
            _to_view(scr, _rope(k, cf, sf, lo), outs[3 * gi + 1], d, BF16)
            _to_view(scr, v, outs[3 * gi + 2], d, BF16)

    tab = pl.BlockSpec((ROWS, BLK), lambda i: (i, 0))
    outs = pl.pallas_call(
        body,
        name="qkv_prep",
        grid=(S // ROWS,),
        in_specs=[pl.BlockSpec((ROWS, QKV_W), lambda i: (i, 0)), tab, tab],
        out_specs=[_view_spec(d) for d in PATTERN_DILATIONS for _ in range(3)],
        out_shape=[jax.ShapeDtypeStruct((S // d, d * GW), BF16) for d in PATTERN_DILATIONS for _ in range(3)],
        scratch_shapes=_VIEW_SCRATCH,
        compiler_params=_cp(("parallel",), 48),
    )(qkv, cos_t, sin_t)
    return [tuple(outs[3 * gi:3 * gi + 3]) for gi in range(3)]


N_ATTN_STEPS = S // BLK


def _class_and_block(gi, t):
    per_class = N_ATTN_STEPS // PATTERN_DILATIONS[gi]
    return t >> (per_class.bit_length() - 1), t & (per_class - 1)


def _attn_specs(gi):
    def cur(t):
        r, n = _class_and_block(gi, t)
        return n, r

    def prev(t):
        r, n = _class_and_block(gi, t)
        return jnp.maximum(n - 1, 0), r

    def whole(t):
        return 0, _class_and_block(gi, t)[0]

    sub = S // PATTERN_DILATIONS[gi]
    return (pl.BlockSpec((BLK, GW), cur), pl.BlockSpec((BLK, GW), prev), pl.BlockSpec((sub, GW), whole))


def _left_lanes():
    return lax.broadcasted_iota(jnp.int32, (BLK, LANES), 1) < HEAD_DIM


def _stack_heads(t2, left):
    zero = jnp.zeros_like(t2)
    return jnp.concatenate([jnp.where(left, t2, zero), jnp.where(left, zero, t2)], axis=0)


def _attn_fwd_block(n, q_ref, kc_ref, kp_ref, vc_ref, vp_ref, o_ref, l_ref):
    mask = _band_mask(n)
    left = _left_lanes()
    zero = jnp.zeros((BLK, LANES), BF16)
    for pr in range(GW // LANES):
        cs = slice(pr * LANES, (pr + 1) * LANES)
        q2 = q_ref[:, cs]
        k2 = jnp.concatenate([kp_ref[:, cs], kc_ref[:, cs]], axis=0)
        v2 = jnp.concatenate([vp_ref[:, cs], vc_ref[:, cs]], axis=0)
        o_h, lse_h = [], []
        for side in (left, ~left):
            s = jnp.where(mask, _dot_nt(jnp.where(side, q2, zero), k2), NEG)
            m = jnp.max(s, axis=1, keepdims=True)
            p = jnp.exp(s - m)
            l = jnp.sum(p, axis=1, keepdims=True)
            o_h.append(_dot_nn((p / l).astype(BF16), v2))
            lse_h.append(m + jnp.log(l))
        o_ref[:, cs] = jnp.where(left, o_h[0], o_h[1])
        l_ref[:, cs] = jnp.where(left, lse_h[0], lse_h[1])


def _attn_fwd(preps):
    def body(*refs):
        t = pl.program_id(0)
        for gi in range(3):
            _attn_fwd_block(_class_and_block(gi, t)[1], *refs[5 * gi:5 * gi + 5], *refs[15 + 2 * gi:17 + 2 * gi])

    in_specs, out_specs, out_shape, args = [], [], [], []
    for gi, d in enumerate(PATTERN_DILATIONS):
        cur, prev, _ = _attn_specs(gi)
        qr, kr, vv = preps[gi]
        in_specs += [cur, cur, prev, cur, prev]
        args += [qr, kr, kr, vv, vv]
        out_specs += [cur, cur]
        out_shape += [jax.ShapeDtypeStruct((S // d, d * GW), F32)] * 2
    outs = pl.pallas_call(
        body,
        name="attn_fwd",
        grid=(N_ATTN_STEPS,),
        in_specs=in_specs,
        out_specs=out_specs,
        out_shape=out_shape,
        compiler_params=_cp(("arbitrary",), 48),
    )(*args)
    return [outs[0], outs[2], outs[4]], [outs[1], outs[3], outs[5]]


def _attn_bwd_block(n, q_ref, kc_ref, kp_ref, vc_ref, vp_ref, do_ref, l_ref, c_ref, dq_ref, dk_ref, dv_ref):
    mask = _band_mask(n)
    left = _left_lanes()
    left2 = jnp.concatenate([left, left], axis=0)
    here = pl.ds(pl.multiple_of(n * BLK, BLK), BLK)
    before = pl.ds(pl.multiple_of(jnp.maximum(n - 1, 0) * BLK, BLK), BLK)
    for pr in range(GW // LANES):
        cs = slice(pr * LANES, (pr + 1) * LANES)
        q2 = q_ref[:, cs]
        do2 = do_ref[:, cs]
        k_bd = _stack_heads(jnp.concatenate([kp_ref[:, cs], kc_ref[:, cs]], axis=0), left2)
        v_bd = _stack_heads(jnp.concatenate([vp_ref[:, cs], vc_ref[:, cs]], axis=0), left2)
        s2 = _dot_nt(q2, k_bd)
        dp2 = _dot_nt(do2, v_bd)
        ps, dss = [], []
        for h in range(2):
            at = pr * LANES + h * HEAD_DIM
            half = slice(h * 2 * BLK, (h + 1) * 2 * BLK)
            p = jnp.exp(jnp.where(mask, s2[:, half], NEG) - l_ref[:, at:at + 1])
            ps.append(p.astype(BF16))
            dss.append((p * (dp2[:, half] - c_ref[:, at:at + 1])).astype(BF16))
        ds_both = jnp.concatenate(dss, axis=1)
        dq_ref[:, cs] = _dot_nn(ds_both, k_bd) * (HEAD_DIM ** -0.5)
        dk_bd = _dot_tn(ds_both, q2)
        dv_bd = _dot_tn(jnp.concatenate(ps, axis=1), do2)
        dk2 = jnp.where(left2, dk_bd[:2 * BLK], dk_bd[2 * BLK:])
        dv2 = jnp.where(left2, dv_bd[:2 * BLK], dv_bd[2 * BLK:])
        dk_ref[here, cs] = dk2[BLK:]
        dv_ref[here, cs] = dv2[BLK:]
        dk_ref[before, cs] += dk2[:BLK]
        dv_ref[before, cs] += dv2[:BLK]


def _attn_bwd(preps, dos, lses, cterms):
    def body(*refs):
        t = pl.program_id(0)
        for gi in range(3):
            _attn_bwd_block(_class_and_block(gi, t)[1], *refs[8 * gi:8 * gi + 8], *refs[24 + 3 * gi:27 + 3 * gi])

    in_specs, out_specs, out_shape, args = [], [], [], []
    for gi, d in enumerate(PATTERN_DILATIONS):
        cur, prev, whole = _attn_specs(gi)
        qr, kr, vv = preps[gi]
        in_specs += [cur, cur, prev, cur, prev, cur, cur, cur]
        args += [qr, kr, kr, vv, vv, dos[gi], lses[gi], cterms[gi]]
        out_specs += [cur, whole, whole]
        out_shape += [jax.ShapeDtypeStruct((S // d, d * GW), F32)] * 3
    outs = pl.pallas_call(
        body,
        name="attn_bwd",
        grid=(N_ATTN_STEPS,),
        in_specs=in_specs,
        out_specs=out_specs,
        out_shape=out_shape,
        compiler_params=_cp(("arbitrary",), 56),
    )(*args)
    return [tuple(outs[3 * gi:3 * gi + 3]) for gi in range(3)]


def _qkv_unprep(grads, cos_t, sin_t, dproj):
    def body(*refs):
        views, (c_ref, s_ref, _, out_ref), scr = refs[:9], refs[9:13], refs[13:]
        lo = _lane_lo(ROWS)
        cf, sf = _tile4(c_ref[...]), _tile4(s_ref[...])
        for gi, d in enumerate(PATTERN_DILATIONS):
            dq_ref, dk_ref, dv_ref = views[3 * gi:3 * gi + 3]
            at = 3 * gi * GW
            out_ref[:, at:at + GW] = _unrope(_from_view(scr, dq_ref, d), cf, sf, lo).astype(BF16)
            out_ref[:, at + GW:at + 2 * GW] = _unrope(_from_view(scr, dk_ref, d), cf, sf, lo).astype(BF16)
            out_ref[:, at + 2 * GW:at + 3 * GW] = _from_view(scr, dv_ref, d).astype(BF16)

    tab = pl.BlockSpec((ROWS, BLK), lambda i: (i, 0))
    return pl.pallas_call(
        body,
        name="qkv_unprep",
        grid=(S // ROWS,),
        in_specs=[_view_spec(d) for d in PATTERN_DILATIONS for _ in range(3)] + [tab, tab, pl.BlockSpec(memory_space=pl.ANY)],
        out_specs=pl.BlockSpec((ROWS, QKV_W), lambda i: (i, 0)),
        out_shape=jax.ShapeDtypeStruct(dproj.shape, dproj.dtype),
        input_output_aliases={11: 0},
        scratch_shapes=_VIEW_SCRATCH,
        compiler_params=_cp(("parallel",), 48),
    )(*[g for grp in grads for g in grp], cos_t, sin_t, dproj)


def _group_weights(l0, l1, l2):
    mx = jnp.maximum(jnp.maximum(l0, l1), l2)
    e0, e1, e2 = jnp.exp(l0 - mx), jnp.exp(l1 - mx), jnp.exp(l2 - mx)
    inv = 1.0 / (e0 + e1 + e2)
    return e0 * inv, e1 * inv, e2 * inv


def _combine_fwd(os_, lses):
    def body(o0, o1, o2, l0, l1, l2, y_ref, *scr):
        ov = [_from_view(scr, o, d) for o, d in zip((o0, o1, o2), PATTERN_DILATIONS)]
        lv = [_from_view(scr, l, d) for l, d in zip((l0, l1, l2), PATTERN_DILATIONS)]
        w0, w1, w2 = _group_weights(*lv)
        y_ref[...] = (w0 * ov[0] + w1 * ov[1] + w2 * ov[2]).astype(BF16)

    views = [_view_spec(d) for d in PATTERN_DILATIONS]
    return pl.pallas_call(
        body,
        name="attn_combine_fwd",
        grid=(S // ROWS,),
        in_specs=views + views,
        out_specs=pl.BlockSpec((ROWS, GW), lambda i: (i, 0)),
        out_shape=jax.ShapeDtypeStruct((S, GW), BF16),
        scratch_shapes=_VIEW_SCRATCH,
        compiler_params=_cp(("parallel",)),
    )(*os_, *lses)


def _combine_bwd(dy, os_, lses, seg):
    def body(dy_ref, o0, o1, o2, l0, l1, l2, seg_ref, d0, d1, d2, c0, c1, c2, *scr):
        ov = [_from_view(scr, o, d) for o, d in zip((o0, o1, o2), PATTERN_DILATIONS)]
        lv = [_from_view(scr, l, d) for l, d in zip((l0, l1, l2), PATTERN_DILATIONS)]
        ws = _group_weights(*lv)
        dyv = dy_ref[...]
        t = dyv * (ws[0] * ov[0] + ws[1] * ov[1] + ws[2] * ov[2])
        t_hi = t.astype(BF16)
        r1 = t - t_hi.astype(F32)
        t_mid = r1.astype(BF16)
        t_lo = (r1 - t_mid.astype(F32)).astype(BF16)
        sg = seg_ref[...]
        e = _dot_nn(t_hi, sg) + _dot_nn(t_mid, sg) + _dot_nn(t_lo, sg)
        for w, do_ref, c_ref, d in zip(ws, (d0, d1, d2), (c0, c1, c2), PATTERN_DILATIONS):
            _to_view(scr, w * dyv, do_ref, d, BF16)
            _to_view(scr, w * e, c_ref, d, F32)

    views = [_view_spec(d) for d in PATTERN_DILATIONS]
    return pl.pallas_call(
        body,
        name="attn_combine_bwd",
        grid=(S // ROWS,),
        in_specs=[pl.BlockSpec((ROWS, GW), lambda i: (i, 0))] + views + views + [pl.BlockSpec((GW, GW), lambda i: (0, 0))],
        out_specs=views + views,
        out_shape=[jax.ShapeDtypeStruct((S // d, d * GW), BF16) for d in PATTERN_DILATIONS]
        + [jax.ShapeDtypeStruct((S // d, d * GW), F32) for d in PATTERN_DILATIONS],
        scratch_shapes=_VIEW_SCRATCH,
        compiler_params=_cp(("parallel",)),
    )(dy, *os_, *lses, seg)


_SQRT_HALF = 0.7071067811865476
_INV_SQRT_2PI = 0.3989422804014327


def _gelu(z):
    return 0.5 * z * (1.0 + lax.erf(z * _SQRT_HALF))


def _gelu_grad(z):
    return 0.5 * (1.0 + lax.erf(z * _SQRT_HALF)) + z * (_INV_SQRT_2PI * jnp.exp(-0.5 * z * z))


def _tril_ws(ws_ref, g):
    t = lax.broadcasted_iota(jnp.int32, (BLK, BLK), 0)
    s = lax.broadcasted_iota(jnp.int32, (BLK, BLK), 1)
    return jnp.where(t >= s, ws_ref[g], 0.0)


def _chunks_side_by_side(x, cols, nch):
    return jnp.concatenate([x[c * BLK:(c + 1) * BLK, cols] for c in range(nch)], axis=1)


def _gmlp_fwd(z, ws, bst, ln_g, ln_b, tm=512):
    nch = tm // BLK

    def body(z_ref, ws_ref, b_ref, g_ref, be_ref, y_ref):
        zg = _gelu(z_ref[...])
        u = zg[:, :D]
        vn, _, _ = _ln_fwd(zg[:, D:], g_ref[...], be_ref[...])
        vnb = vn.astype(BF16)
        bt = b_ref[...]
        for g in range(8):
            w = _tril_ws(ws_ref, g).astype(BF16)
            cols = slice(g * BLK, (g + 1) * BLK)
            mixed = _dot_nn(w, _chunks_side_by_side(vnb, cols, nch)) + bt[:, g:g + 1]
            for c in range(nch):
                rows = slice(c * BLK, (c + 1) * BLK)
                y_ref[rows, cols] = (u[rows, cols] * mixed[:, c * BLK:(c + 1) * BLK]).astype(BF16)

    return pl.pallas_call(
        body,
        name="gmlp_fwd",
        grid=(S // tm,),
        in_specs=[
            pl.BlockSpec((tm, 2 * D), lambda i: (i, 0)),
            pl.BlockSpec((8, BLK, BLK), lambda i: (0, 0, 0)),
            pl.BlockSpec((BLK, 8), lambda i: (0, 0)),
            pl.BlockSpec((1, D), lambda i: (0, 0)),
            pl.BlockSpec((1, D), lambda i: (0, 0)),
        ],
        out_specs=pl.BlockSpec((tm, D), lambda i: (i, 0)),
        out_shape=jax.ShapeDtypeStruct((S, D), BF16),
        compiler_params=_cp(("parallel",), 48),
    )(z, ws, bst, ln_g, ln_b)


def _gmlp_bwd(z, dy, ws, bst, ln_g, ln_b, dproj, tm=512):
    nch = tm // BLK

    def body(z_ref, dy_ref, ws_ref, b_ref, g_ref, be_ref, _, dz_ref, dws_ref, dbs_ref, dg_ref, dbe_ref, dvn_scr, dm_acc):
        i = pl.program_id(0)
        zv = z_ref[...]
        zg = _gelu(zv)
        u = zg[:, :D]
        gam = g_ref[...]
        vn, xhat, rstd = _ln_fwd(zg[:, D:], gam, be_ref[...])
        vnb = vn.astype(BF16)
        dyv = dy_ref[...]
        dmix = dyv * u
        dmb = dmix.astype(BF16)
        bt = b_ref[...]
        tmask = lax.broadcasted_iota(jnp.int32, (BLK, BLK), 0) >= lax.broadcasted_iota(jnp.int32, (BLK, BLK), 1)
        dm_sum = dmix[0:BLK]
        for c in range(1, nch):
            dm_sum = dm_sum + dmix[c * BLK:(c + 1) * BLK]

        @pl.when(i == 0)
        def _():
            dm_acc[...] = jnp.zeros_like(dm_acc)
            dws_ref[...] = jnp.zeros_like(dws_ref)
            dg_ref[...] = jnp.zeros_like(dg_ref)
            dbe_ref[...] = jnp.zeros_like(dbe_ref)

        dm_acc[...] += dm_sum
        dus = []
        for g in range(8):
            w = _tril_ws(ws_ref, g).astype(BF16)
            cols = slice(g * BLK, (g + 1) * BLK)
            v_cat = _chunks_side_by_side(vnb, cols, nch)
            dm_cat = _chunks_side_by_side(dmb, cols, nch)
            mixed = _dot_nn(w, v_cat) + bt[:, g:g + 1]
            dus.append(jnp.concatenate(
                [dyv[c * BLK:(c + 1) * BLK, cols] * mixed[:, c * BLK:(c + 1) * BLK] for c in range(nch)], axis=0))
            dws_ref[g] += jnp.where(tmask, _dot_nt(dm_cat, v_cat), 0.0)
            dvn_cat = _dot_tn(w, dm_cat)
            for c in range(nch):
                dvn_scr[c * BLK:(c + 1) * BLK, cols] = dvn_cat[:, c * BLK:(c + 1) * BLK]

        dvn = dvn_scr[...]
        dg_ref[...] += _colsum(dvn * xhat)
        dbe_ref[...] += _colsum(dvn)
        dvg = _ln_bwd(dvn, xhat, rstd, gam)
        gp = _gelu_grad(zv)
        dz_ref[:, :D] = (jnp.concatenate(dus, axis=1) * gp[:, :D]).astype(BF16)
        dz_ref[:, D:] = (dvg * gp[:, D:]).astype(BF16)

        @pl.when(i == S // tm - 1)
        def _():
            acc = dm_acc[...]
            for g in range(8):
                dbs_ref[:, g:g + 1] = jnp.sum(acc[:, g * BLK:(g + 1) * BLK], axis=1, keepdims=True)

    vec = pl.BlockSpec((1, D), lambda i: (0, 0))
    return pl.pallas_call(
        body,
        name="gmlp_bwd",
        grid=(S // tm,),
        in_specs=[
            pl.BlockSpec((tm, 2 * D), lambda i: (i, 0)),
            pl.BlockSpec((tm, D), lambda i: (i, 0)),
            pl.BlockSpec((8, BLK, BLK), lambda i: (0, 0, 0)),
            pl.BlockSpec((BLK, 8), lambda i: (0, 0)),
            vec,
            vec,
            pl.BlockSpec(memory_space=pl.ANY),
        ],
        out_specs=[
            pl.BlockSpec((tm, 2 * D), lambda i: (i, DPROJ_Z_COL)),
            pl.BlockSpec((8, BLK, BLK), lambda i: (0, 0, 0)),
            pl.BlockSpec((BLK, 8), lambda i: (0, 0)),
            vec,
            vec,
        ],
        out_shape=[
            jax.ShapeDtypeStruct(dproj.shape, dproj.dtype),
            jax.ShapeDtypeStruct((8, BLK, BLK), F32),
            jax.ShapeDtypeStruct((BLK, 8), F32),
            jax.ShapeDtypeStruct((1, D), F32),
            jax.ShapeDtypeStruct((1, D), F32),
        ],
        input_output_aliases={6: 0},
        scratch_shapes=[pltpu.VMEM((tm, D), F32), pltpu.VMEM((BLK, D), F32)],
        compiler_params=_cp(("arbitrary",), 48),
    )(z, dy, ws, bst, ln_g, ln_b, dproj)


def _merge_fwd(ya, yg, glog, bgate, h1, wabt, wgb, wo, ln_g, ln_b, tm=256):
    def body(ya_ref, yg_ref, gl_ref, bg_ref, h1_ref, wab_ref, wgb_ref, wo_ref, g_ref, b_ref,
             h_ref, hb_ref, xh_ref, rs_ref, mg_ref, bra_ref, brg_ref):
        bra = _dot_nt(ya_ref[...], wab_ref[...])
        brg = _dot_nn(yg_ref[...], wgb_ref[...])
        gates = _sigmoid(gl_ref[...] + bg_ref[...])
        merged = (gates[:, :D] * bra + gates[:, D:] * brg).astype(BF16)
        mix = _dot_nn(merged, wo_ref[...])
        h, xhat, rstd = _ln_fwd(ALPHA * h1_ref[...] + mix, g_ref[...], b_ref[...])
        h_ref[...] = h
        hb_ref[...] = h.astype(BF16)
        xh_ref[...] = xhat
        rs_ref[...] = rstd
        mg_ref[...] = merged
        bra_ref[...] = bra
        brg_ref[...] = brg

    row = pl.BlockSpec((tm, D), lambda i: (i, 0))
    vec = pl.BlockSpec((1, D), lambda i: (0, 0))
    full = lambda shape: pl.BlockSpec(shape, lambda i: (0, 0))
    return pl.pallas_call(
        body,
        name="merge_fwd",
        grid=(S // tm,),
        in_specs=[
            pl.BlockSpec((tm, GW), lambda i: (i, 0)), row,
            pl.BlockSpec((tm, 2 * D), lambda i: (i, glog.shape[1] // (2 * D) - 1)),
            full((1, 2 * D)), row,
            full((D, GW)), full((D, D)), full((D, D)), vec, vec,
        ],
        out_specs=[row, row, row, pl.BlockSpec((tm, 1), lambda i: (i, 0)), row, row, row],
        out_shape=[
            jax.ShapeDtypeStruct((S, D), F32),
            jax.ShapeDtypeStruct((S, D), BF16),
            jax.ShapeDtypeStruct((S, D), F32),
            jax.ShapeDtypeStruct((S, 1), F32),
            jax.ShapeDtypeStruct((S, D), BF16),
            jax.ShapeDtypeStruct((S, D), F32),
            jax.ShapeDtypeStruct((S, D), F32),
        ],
        compiler_params=_cp(("parallel",), 48),
    )(ya, yg, glog, bgate, h1, wabt, wgb, wo, ln_g, ln_b)


def _merge_bwd(dh2, xhat, rstd, ln_g, bra, brg, glog, bgate, wabt, wgb, wo, tm=256):
    def body(dh_ref, xh_ref, rs_ref, g_ref, bra_ref, brg_ref, gl_ref, bg_ref, wab_ref, wgb_ref, wo_ref,
             dr_ref, drb_ref, dlog_ref, dba_ref, dbg_ref, dya_ref, dyg_ref, dbgate_ref, dg_ref, dbias_ref):
        i = pl.program_id(0)
        dh = dh_ref[...]
        xh = xh_ref[...]
        dr = _ln_bwd(dh, xh, rs_ref[...], g_ref[...])
        drb = dr.astype(BF16)
        dr_ref[...] = dr
        drb_ref[...] = drb
        dmerged = _dot_nt(drb, wo_ref[...])
        gates = _sigmoid(gl_ref[...] + bg_ref[...])
        g0, g1 = gates[:, :D], gates[:, D:]
        dl0 = dmerged * bra_ref[...] * g0 * (1.0 - g0)
        dl1 = dmerged * brg_ref[...] * g1 * (1.0 - g1)
        dlog_ref[:, :D] = dl0.astype(BF16)
        dlog_ref[:, D:] = dl1.astype(BF16)
        dba = (dmerged * g0).astype(BF16)
        dbg = (dmerged * g1).astype(BF16)
        dba_ref[...] = dba
        dbg_ref[...] = dbg
        dya_ref[...] = _dot_nn(dba, wab_ref[...])
        dyg_ref[...] = _dot_nt(dbg, wgb_ref[...])
        s0, s1 = _colsum(dl0), _colsum(dl1)
        sg, sb = _colsum(dh * xh), _colsum(dh)

        @pl.when(i == 0)
        def _():
            dbgate_ref[:, :D] = s0
            dbgate_ref[:, D:] = s1
            dg_ref[...] = sg
            dbias_ref[...] = sb

        @pl.when(i > 0)
        def _():
            dbgate_ref[:, :D] += s0
            dbgate_ref[:, D:] += s1
            dg_ref[...] += sg
            dbias_ref[...] += sb

    row = pl.BlockSpec((tm, D), lambda i: (i, 0))
    vec = pl.BlockSpec((1, D), lambda i: (0, 0))
    wide = pl.BlockSpec((tm, 2 * D), lambda i: (i, 0))
    full = lambda shape: pl.BlockSpec(shape, lambda i: (0, 0))
    return pl.pallas_call(
        body,
        name="merge_bwd",
        grid=(S // tm,),
        in_specs=[row, row, pl.BlockSpec((tm, 1), lambda i: (i, 0)), vec, row, row,
                  pl.BlockSpec((tm, 2 * D), lambda i: (i, glog.shape[1] // (2 * D) - 1)),
                  full((1, 2 * D)), full((D, GW)), full((D, D)), full((D, D))],
        out_specs=[row, row, pl.BlockSpec((tm, 2 * D), lambda i: (i, DPROJ_G_COL)), row, row,
                   pl.BlockSpec((tm, GW), lambda i: (i, 0)), row, full((1, 2 * D)), vec, vec],
        out_shape=[
            jax.ShapeDtypeStruct((S, D), F32),
            jax.ShapeDtypeStruct((S, D), BF16),
            jax.ShapeDtypeStruct((S, DPROJ_W), BF16),
            jax.ShapeDtypeStruct((S, D), BF16),
            jax.ShapeDtypeStruct((S, D), BF16),
            jax.ShapeDtypeStruct((S, GW), F32),
            jax.ShapeDtypeStruct((S, D), F32),
            jax.ShapeDtypeStruct((1, 2 * D), F32),
            jax.ShapeDtypeStruct((1, D), F32),
            jax.ShapeDtypeStruct((1, D), F32),
        ],
        compiler_params=_cp(("arbitrary",), 48),
    )(dh2, xhat, rstd, ln_g, bra, brg, glog, bgate, wabt, wgb, wo)


def _loss_head(h3, target, tm=512):
    def body(h_ref, t_ref, d_ref, l_ref):
        i = pl.program_id(0)
        e = h_ref[...] - t_ref[...]
        d_ref[...] = e * (1.0 / D)
        part = jnp.sum(_colsum(e * e), axis=1, keepdims=True) * (0.5 / D)

        @pl.when(i == 0)
        def _():
            l_ref[...] = part

        @pl.when(i > 0)
        def _():
            l_ref[...] += part

    row = pl.BlockSpec((tm, D), lambda i: (i, 0))
    return pl.pallas_call(
        body,
        name="loss_head",
        grid=(S // tm,),
        in_specs=[row, row],
        out_specs=[row, pl.BlockSpec((1, 1), lambda i: (0, 0))],
        out_shape=[jax.ShapeDtypeStruct((S, D), F32), jax.ShapeDtypeStruct((1, 1), F32)],
        compiler_params=_cp(("arbitrary",)),
    )(h3, target)


def _tie(x, dep):
    if dep is None:
        return x
    return x + dep[0, 0].astype(x.dtype)


def _local_step(x, pos_col, target, get_w, p, emit):
    w = get_w("ffn1", None)
    a1, b1, hm1 = _ffn_up(x, w["g1"], w["u1"], "ffn1_up")
    w.update(get_w("ffn1d", hm1))
    h1, h1b, xh1, rs1 = _ffn_down(hm1, w["d1"], x, p["ln1_g"], p["ln1_b"], "ffn1_down")

    w.update(get_w("win", h1b))
    qkv = _matmul(h1b, w["win"], "nt", F32, 1024, 1536, D, "proj_qkv", b_off=0, n_out=QKV_W, dep=w.get("_dep"))
    z = glog = _matmul(h1b, w["win"], "nt", F32, 1024, 512, D, "proj_zg", b_off=QKV_W // 512, n_out=4 * D)

    half = jnp.arange(0, HEAD_DIM, 2, dtype=F32) / HEAD_DIM
    inv_freq = ROPE_THETA ** (-half)
    invf = jnp.tile(inv_freq, 4).reshape(1, BLK)
    sign = jnp.tile(jnp.concatenate([-jnp.ones((32,), F32), jnp.ones((32,), F32)]), 2).reshape(1, BLK)
    cos_t, sin_t = _rope_tables(pos_col, invf, sign)

    preps = _qkv_prep(qkv, cos_t, sin_t)
    os_, lses = _attn_fwd(preps)
    ya = _combine_fwd(os_, lses)
    get_w("late", ya, early=True)
    bst = p["gmlp_b_s"].T
    yg = _gmlp_fwd(z, p["gmlp_w_s"], bst, p["gmlp_ln_g"], p["gmlp_ln_b"])
    w.update(get_w("late", yg))
    h2, h2b, xh2, rs2, merged, bra, brg = _merge_fwd(ya, yg, glog, p["b_gates"], h1, w["ab"], w["gb"], w["o"],
                                                      p["ln2_g"], p["ln2_b"])
    a2, b2, hm2 = _ffn_up(h2b, w["g2"], w["u2"], "ffn2_up")
    h3, _, xh3, rs3 = _ffn_down(hm2, w["d2"], h2, p["ln3_g"], p["ln3_b"], "ffn2_down")
    dh3, loss = _loss_head(h3, target)

    gp = {}
    dr3, df2, da2, db2, gp["ln3_g"], gp["ln3_b"] = _ffn_bwd_mid(dh3, None, xh3, rs3, p["ln3_g"], a2, b2, w["d2"],
                                                                "ffn2_bwd_mid")
    tok = emit("ffn2", {
        "g2": _matmul(da2, h2b, "tn", BF16, 1408, D, S, "wgrad_g2"),
        "u2": _matmul(db2, h2b, "tn", BF16, 1408, D, S, "wgrad_u2"),
        "d2": _matmul(hm2, df2, "tn", BF16, 1408, D, S, "wgrad_d2")})
    dh2 = _ffn_bwd_dx(dr3, da2, db2, w["g2"], w["u2"], "ffn2_bwd_dx")

    (dr2, dr2b, dproj, dba, dbg, dya, dyg, gp["b_gates"], gp["ln2_g"], gp["ln2_b"]) = _merge_bwd(
        dh2, xh2, rs2, _tie(p["ln2_g"], tok), bra, brg, glog, p["b_gates"], w["ab"], w["gb"], w["o"])
    g_mix = {
        "o": _matmul(merged, dr2b, "tn", BF16, 512, D, S, "wgrad_o"),
        "ab": _matmul(dba, ya, "tn", BF16, 512, GW, S, "wgrad_ab"),
        "gb": _matmul(yg, dbg, "tn", BF16, 512, D, S, "wgrad_gb")}

    seg = (jnp.arange(GW)[:, None] // HEAD_DIM == jnp.arange(GW)[None, :] // HEAD_DIM).astype(BF16)
    do0, do1, do2, c0, c1, c2 = _combine_bwd(dya, os_, lses, seg)
    dproj = _qkv_unprep(_attn_bwd(preps, (do0, do1, do2), lses, (c0, c1, c2)), cos_t, sin_t, dproj)
    dproj, gp["gmlp_w_s"], dbst, gp["gmlp_ln_g"], gp["gmlp_ln_b"] = _gmlp_bwd(
        z, dyg, p["gmlp_w_s"], bst, p["gmlp_ln_g"], p["gmlp_ln_b"], dproj)
    gp["gmlp_b_s"] = dbst.T
    g_mix["win"] = _matmul(dproj, h1b, "tn", BF16, 512, D, S, "wgrad_win", a_map=_dproj_tile, m_out=IN_W)
    tok = emit("mixwin", g_mix)
    dh1m = _matmul(dproj, w["win"], "nn", F32, S, D, 512, "dproj_to_dh1", dep=tok, a_map=_dproj_tile)

    dr1, df1, da1, db1, gp["ln1_g"], gp["ln1_b"] = _ffn_bwd_mid(dr2, dh1m, xh1, rs1, p["ln1_g"], a1, b1, w["d1"],
                                                                "ffn1_bwd_mid")
    tok = emit("small", {**gp, "loss": loss})
    tok = emit("g1", {"g1": _matmul(da1, x, "tn", BF16, 1408, D, S, "wgrad_g1", dep=tok)})
    tok = emit("u1", {"u1": _matmul(db1, x, "tn", BF16, 1408, D, S, "wgrad_u1", dep=tok)})
    tok = emit("d1", {"d1": _matmul(hm1, df1, "tn", BF16, 1408, D, S, "wgrad_d1", dep=tok)})
    dx = _ffn_bwd_dx(dr1, da1, db1, w["g1"], w["u1"], "ffn1_bwd_dx", dep=tok)
    return loss, dx, gp


_FLIPS = [(mx, my, mc) for mx in (0, 1) for my in (0, 1) for mc in (0, 1)][1:]
_GROUPS = {"ffn1": ("g1", "u1"), "ffn1d": ("d1",), "win": ("win",), "late": ("ab", "gb", "o", "g2", "u2", "d2")}
_SCATTERS = {"ffn2": ("g2", "u2", "d2"), "mixwin": ("ab", "gb", "o", "win"), "g1": ("g1",), "u1": ("u1",), "d1": ("d1",)}
_TWO_STAGE = ("g1", "u1", "d1")
_HBM = pl.BlockSpec(memory_space=pltpu.HBM)
_SEM = pl.BlockSpec(memory_space=pltpu.SEMAPHORE)
_EFFECT = pltpu.SideEffectType.DATAFLOW_SIDE_EFFECTING


def _me():
    return 4 * lax.axis_index("x") + 2 * lax.axis_index("y") + lax.axis_index("c")


def _copies_start(bufs, copies, n_sem, name, after=None):
    nb = len(bufs)
    n_after = 0 if after is None else 1

    def body(*refs):
        b = refs[:nb]
        send_sems, recv_sems = refs[nb + n_after], refs[nb + n_after + 1]
        token = refs[-1]
        x, y, c = lax.axis_index("x"), lax.axis_index("y"), lax.axis_index("c")
        me = 4 * x + 2 * y + c
        for si, s_slot, di, d_slot, (mx, my, mc), sem in copies:
            s_idx, d_idx = s_slot(me), d_slot(me)
            pltpu.make_async_remote_copy(
                src_ref=b[si] if s_idx is None else b[si].at[s_idx],
                dst_ref=b[di] if d_idx is None else b[di].at[d_idx],
                send_sem=send_sems.at[sem], recv_sem=recv_sems.at[sem],
                device_id=(x ^ mx, y ^ my, c ^ mc), device_id_type=MESH).start()
        token[...] = jnp.zeros_like(token)

    ins = [pltpu.with_memory_space_constraint(a, pltpu.HBM) for a in bufs]
    outs = pl.pallas_call(
        body,
        name=name,
        in_specs=[_HBM] * nb + [pl.BlockSpec(memory_space=pl.ANY)] * n_after,
        out_specs=[_SEM, _SEM] + [_HBM] * nb + [pl.BlockSpec(memory_space=pltpu.VMEM)],
        out_shape=[pltpu.SemaphoreType.DMA((n_sem,)), pltpu.SemaphoreType.DMA((n_sem,))]
        + [pltpu.HBM(a.shape, a.dtype) for a in ins] + [jax.ShapeDtypeStruct((8, 128), F32)],
        input_output_aliases={k: 2 + k for k in range(nb)},
        compiler_params=pltpu.CompilerParams(has_side_effects=_EFFECT),
    )(*ins, *([] if after is None else [after]))
    return outs[0], outs[1], list(outs[2:2 + nb]), outs[-1]


def _copies_wait(started, waits, after, name):
    send_sems, recv_sems, bufs, _ = started
    nb = len(bufs)

    def body(*refs):
        b = refs[:nb]
        ss, rs = refs[nb], refs[nb + 1]
        me3 = (lax.axis_index("x"), lax.axis_index("y"), lax.axis_index("c"))
        for bi, n_blocks, sem, is_send in waits:
            blocks = b[bi].at[pl.ds(0, n_blocks)]
            cp = pltpu.make_async_remote_copy(src_ref=blocks, dst_ref=blocks, send_sem=ss.at[sem], recv_sem=rs.at[sem],
                                              device_id=me3, device_id_type=MESH)
            if is_send:
                cp.wait_send()
            else:
                cp.wait_recv()

    return pl.pallas_call(
        body,
        name=name,
        in_specs=[_HBM] * nb + [_SEM, _SEM, pl.BlockSpec(memory_space=pl.ANY)],
        out_specs=[_HBM] * nb,
        out_shape=[pltpu.HBM(a.shape, a.dtype) for a in bufs],
        input_output_aliases={k: k for k in range(nb)},
        compiler_params=pltpu.CompilerParams(has_side_effects=_EFFECT),
    )(*bufs, send_sems, recv_sems, after)


def _landing(own, me):
    return lax.dynamic_update_slice(lax.empty((N_DEV,) + own.shape[1:], own.dtype), own, (me, 0, 0))


_SIBLING = (0, 0, 1)
_OTHER_CHIPS = ((1, 0, 0), (0, 1, 0), (1, 1, 0))


def _bits(flip):
    return 4 * flip[0] + 2 * flip[1] + flip[2]


def _gather_send(shards, grp, after):
    nw = len(shards)
    me = _me()
    bufs = list(shards) + [_landing(a[None], me) for a in shards]
    copies = []
    for k in range(nw):
        copies.append((k, lambda me: None, nw + k, lambda me: me, _SIBLING, nw + k))
        copies += [(k, lambda me: None, nw + k, lambda me: me, f, k) for f in _OTHER_CHIPS]
    return _copies_start(bufs, copies, 2 * nw, "gather_send_" + grp, after)


def _gather_pass(started, grp, after):
    nw = len(started[2]) // 2
    waits = []
    for k in range(nw):
        waits += [(nw + k, 3, k, True), (nw + k, 1, nw + k, True), (nw + k, 3, k, False), (nw + k, 1, nw + k, False)]
    lands = list(_copies_wait(started, waits, after, "gather_arrived_" + grp)[nw:])
    copies = []
    for k in range(nw):
        for f in _OTHER_CHIPS:
            slot = functools.partial(lambda me, bits: me ^ bits, bits=_bits(f))
            copies.append((k, slot, k, slot, _SIBLING, k))
    return _copies_start(lands, copies, nw, "gather_pass_" + grp)


def _gather_finish(passed, grp, after):
    nw = len(passed[2])
    waits = [(k, 3, k, is_send) for k in range(nw) for is_send in (True, False)]
    return _copies_wait(passed, waits, passed[3] if after is None else after, "gather_done_" + grp)


def _scatter_send(parts, grp, after=None):
    nw = len(parts)
    me = _me()
    bufs = list(parts) + [_landing(lax.dynamic_index_in_dim(a, me, 0, keepdims=True), me) for a in parts]
    copies = []
    for k in range(nw):
        for f in _FLIPS:
            to = functools.partial(lambda me, bits: me ^ bits, bits=_bits(f))
            copies.append((k, to, nw + k, lambda me: me, f, k))
    return _copies_start(bufs, copies, nw, "scatter_send_" + grp, after)


def _scatter_finish(started, grp, after):
    nw = len(started[2]) // 2
    waits = [(nw + k, N_DEV - 1, k, is_send) for k in range(nw) for is_send in (True, False)]
    return _copies_wait(started, waits, after, "scatter_done_" + grp)[nw:]


N_CHIP = N_DEV // 2


def _pair_sum(part, other, key):
    _, r, c = part.shape
    tm = _ADAM_ROWS.get(r, r)
    core = lax.axis_index("c").astype(jnp.int32).reshape(1)

    def body(core_ref, a_ref, b_ref, o_ref):
        o_ref[...] = (a_ref[...].astype(F32) + b_ref[...].astype(F32)).astype(BF16)

    return pl.pallas_call(
        body,
        name="pair_sum_" + key,
        grid_spec=pltpu.PrefetchScalarGridSpec(
            num_scalar_prefetch=1,
            grid=(N_CHIP, r // tm),
            in_specs=[pl.BlockSpec((None, tm, c), lambda q, i, core_ref: (2 * q + core_ref[0], i, 0)),
                      pl.BlockSpec((None, tm, c), lambda q, i, core_ref: (q, i, 0))],
            out_specs=pl.BlockSpec((None, tm, c), lambda q, i, core_ref: (q, i, 0)),
        ),
        out_shape=jax.ShapeDtypeStruct((N_CHIP, r, c), BF16),
        compiler_params=_cp(("parallel", "parallel")),
    )(core, part, other)


def _scatter2_send(part, key, dep=None):
    me = _me()
    _, r, c = part.shape
    swap = []
    for q in range(N_CHIP):
        src = functools.partial(lambda me, q: 2 * q + 1 - me % 2, q=q)
        swap.append((0, src, 1, functools.partial(lambda me, q: q, q=q), _SIBLING, 0))
    started = _copies_start([part, lax.empty((N_CHIP, r, c), part.dtype)], swap, 1, "scatter_swap_" + key, dep)
    waits = [(1, N_CHIP, 0, True), (1, N_CHIP, 0, False)]
    part, other = _copies_wait(started, waits, started[3], "scatter_swapped_" + key)
    pair = _pair_sum(part, other, key)
    chip = me // 2
    land = lax.dynamic_update_slice(lax.empty((N_CHIP, r, c), pair.dtype),
                                    lax.dynamic_index_in_dim(pair, chip, 0, keepdims=True), (chip, 0, 0))
    copies = []
    for f in _OTHER_CHIPS:
        to = functools.partial(lambda me, bits: (me ^ bits) // 2, bits=_bits(f))
        copies.append((0, to, 1, lambda me: me // 2, f, 0))
    return _copies_start([pair, land], copies, 1, "scatter_send_" + key)


def _scatter2_finish(started, key, after):
    waits = [(1, N_CHIP - 1, 0, True), (1, N_CHIP - 1, 0, False)]
    return _copies_wait(started, waits, after, "scatter_done_" + key)[1]


def _allgather_send(block, name, after=None):
    me = _me()
    copies = [(0, lambda me: None, 1, lambda me: me, f, 0) for f in _FLIPS]
    return _copies_start([block, _landing(block[None], me)], copies, 1, name, after)


def _allgather_finish(started, name, after):
    waits = [(1, N_DEV - 1, 0, True), (1, N_DEV - 1, 0, False)]
    return _copies_wait(started, waits, after, name)[1]


_ADAM_ROWS = {352: 176, 1088: 272}


def _sum_adamw(parts, wv, m, v, name):
    n_parts, r, c = parts.shape
    tm = _ADAM_ROWS.get(r, r)
    assert r % tm == 0 and wv.shape == (r, c)

    def body(p_ref, w_ref, m_ref, v_ref, g_ref, d_ref, mo_ref, vo_ref):
        gv = p_ref[0].astype(F32)
        for j in range(1, n_parts):
            gv = gv + p_ref[j].astype(F32)
        g_ref[...] = gv
        mn = ADAM_B1 * m_ref[...] + (1.0 - ADAM_B1) * gv
        vn = ADAM_B2 * v_ref[...] + (1.0 - ADAM_B2) * (gv * gv)
        m_hat = mn / (1.0 - ADAM_B1 ** ADAM_STEP)
        v_hat = vn / (1.0 - ADAM_B2 ** ADAM_STEP)
        d_ref[...] = -ADAM_LR * (m_hat / (jnp.sqrt(v_hat) + ADAM_EPS) + ADAM_WD * w_ref[...])
        mo_ref[...] = mn
        vo_ref[...] = vn

    sp = pl.BlockSpec((tm, c), lambda i: (i, 0))
    return pl.pallas_call(
        body,
        name=name,
        grid=(r // tm,),
        in_specs=[pl.BlockSpec((n_parts, tm, c), lambda i: (0, i, 0))] + [sp] * 3,
        out_specs=[sp] * 4,
        out_shape=[jax.ShapeDtypeStruct((r, c), F32)] * 4,
        compiler_params=_cp(("parallel",), 48),
    )(parts, wv, m, v)


_WEIGHTS = ["ffn1_w_gate", "ffn1_w_up", "ffn1_w_down", "ln1_g", "ln1_b", "w_in", "b_gates", "gmlp_ln_g", "gmlp_ln_b",
            "gmlp_w_s", "gmlp_b_s", "w_attn_branch", "w_gmlp_branch", "w_out", "ln2_g", "ln2_b", "ffn2_w_gate",
            "ffn2_w_up", "ffn2_w_down", "ln3_g", "ln3_b"]
_BIG_OF = {"ffn1_w_gate": ("g1", True), "ffn1_w_up": ("u1", True), "ffn1_w_down": ("d1", False), "w_in": ("win", True),
           "w_attn_branch": ("ab", True), "w_gmlp_branch": ("gb", False), "w_out": ("o", False),
           "ffn2_w_gate": ("g2", True), "ffn2_w_up": ("u2", True), "ffn2_w_down": ("d2", False)}
_SMALL = [n for n in _WEIGHTS if n not in _BIG_OF]
_SMALL_ROWS = {"gmlp_w_s": 128, "b_gates": 2}
_SMALL_SLOT = 8


def _pack_small(d, last=None):
    rows = []
    for n in _SMALL:
        r = d[n].reshape(-1, D)
        slot = max(r.shape[0], _SMALL_SLOT)
        rows.append(jnp.pad(r, ((0, slot - r.shape[0]), (0, 0))))
    rows.append(jnp.zeros((_SMALL_SLOT, D), F32) if last is None else jnp.broadcast_to(last.reshape(1, 1), (_SMALL_SLOT, D)))
    return jnp.concatenate(rows, axis=0)


def _unpack_small(packed, shapes):
    out, at = {}, 0
    for n in _SMALL:
        k = _SMALL_ROWS.get(n, 1)
        out[n] = packed[at:at + k].reshape(shapes[n])
        at += max(k, _SMALL_SLOT)
    return out


def kernel(x, positions, ffn1_w_gate, ffn1_w_up, ffn1_w_down, ln1_g, ln1_b, w_in, b_gates, gmlp_ln_g, gmlp_ln_b, gmlp_w_s, gmlp_b_s, w_attn_branch, w_gmlp_branch, w_out, ln2_g, ln2_b, ffn2_w_gate, ffn2_w_up, ffn2_w_down, ln3_g, ln3_b, loss_target, m_ffn1_w_gate, m_ffn1_w_up, m_ffn1_w_down, m_ln1_g, m_ln1_b, m_w_in, m_b_gates, m_gmlp_ln_g, m_gmlp_ln_b, m_gmlp_w_s, m_gmlp_b_s, m_w_attn_branch, m_w_gmlp_branch, m_w_out, m_ln2_g, m_ln2_b, m_ffn2_w_gate, m_ffn2_w_up, m_ffn2_w_down, m_ln3_g, m_ln3_b, v_ffn1_w_gate, v_ffn1_w_up, v_ffn1_w_down, v_ln1_g, v_ln1_b, v_w_in, v_b_gates, v_gmlp_ln_g, v_gmlp_ln_b, v_gmlp_w_s, v_gmlp_b_s, v_w_attn_branch, v_w_gmlp_branch, v_w_out, v_ln2_g, v_ln2_b, v_ffn2_w_gate, v_ffn2_w_up, v_ffn2_w_down, v_ln3_g, v_ln3_b):
    args = dict(locals())
    wts = {n: args[n] for n in _WEIGHTS}
    ms = {n: args["m_" + n] for n in _WEIGHTS}
    vs = {n: args["v_" + n] for n in _WEIGHTS}

    name_of = {key: (n, tr) for n, (key, tr) in _BIG_OF.items()}

    started, tok = {}, None
    for grp, keys in _GROUPS.items():
        shards = []
        for key in keys:
            n, tr = name_of[key]
            s2 = wts[n][0]
            shards.append(_tie(s2.T if tr else s2, tok).astype(BF16))
        started[grp] = _gather_send(shards, grp, tok)
        tok = started[grp][3]
    all_started = tok
    passed = {}

    def get_w(grp, after, early=False):
        if grp not in passed:
            passed[grp] = _gather_pass(started[grp], grp, all_started if after is None else after)
            after = None
        if early:
            return None
        lands = _gather_finish(passed[grp], grp, after)
        return {key: g.reshape(-1, g.shape[-1]) for key, g in zip(_GROUPS[grp], lands)}

    sent = {}

    def emit(grp, grads):
        if grp == "small":
            sent[grp] = _allgather_send(_pack_small(grads, last=grads["loss"]), "small_grads_send")
        else:
            parts = [grads[key].reshape(N_DEV, -1, grads[key].shape[-1]) for key in _SCATTERS[grp]]
            sent[grp] = _scatter2_send(parts[0], grp) if grp in _TWO_STAGE else _scatter_send(parts, grp)
        return sent[grp][3]

    p = {n: (wts[n][0] if n in ("gmlp_w_s", "gmlp_b_s") else wts[n]) for n in _SMALL}
    loss, dx, gp = _local_step(x[0], positions.reshape(S, 1), loss_target[0], get_w, p, emit)
    grads, deltas, new_m, new_v = {}, {}, {}, {}
    after = dx
    for grp in ("ffn2", "mixwin", "small", "g1", "u1", "d1"):
        if grp == "small":
            parts = _allgather_finish(sent[grp], "small_grads_done", after)
            outs = _sum_adamw(parts, *[_pack_small({n: d[n] for n in _SMALL}) for d in (wts, ms, vs)], "update_small")
            shapes = {n: wts[n].shape for n in _SMALL}
            for dst, packed in zip((grads, deltas, new_m, new_v), outs):
                dst.update(_unpack_small(packed, shapes))
            loss = outs[0][-_SMALL_SLOT, 0]
            after = outs[1]
            continue
        arrived = ([_scatter2_finish(sent[grp], grp, after)] if grp in _TWO_STAGE
                   else _scatter_finish(sent[grp], grp, after))
        for key, part in zip(_SCATTERS[grp], arrived):
            n, tr = name_of[key]
            outs = _sum_adamw(part, *[(d[n][0].T if tr else d[n][0]) for d in (wts, ms, vs)], "update_" + key)
            for dst, o in zip((grads, deltas, new_m, new_v), outs):
                dst[n] = (o.T if tr else o)[None]
            after = outs[1]

    return (loss, dx[None], *[grads[n] for n in _WEIGHTS], *[deltas[n] for n in _WEIGHTS],
            *[new_m[n] for n in _WEIGHTS], *[new_v[n] for n in _WEIGHTS])
```

```python
import functools
import math

import jax
import jax.numpy as jnp
from jax import lax
from jax.experimental import pallas as pl
from jax.experimental.pallas import tpu as pltpu

F32 = jnp.float32
BF16 = jnp.bfloat16

N_DEV = 8
D = 1024
S = 2048
F = 2816
HEAD_DIM = 64
HEADS = 8
GW = HEADS * HEAD_DIM
PATTERN_DILATIONS = (1, 4, 16)
BLK = 128
QKV_W = 3 * 3 * GW
IN_W = QKV_W + 2 * D + 2 * D
DPROJ_W = 5 * 2 * D
DPROJ_Z_COL, DPROJ_G_COL = 3, 4


def _dproj_tile(t):
    three = jnp.int32(3)
    return jnp.where(t < 9, lax.rem(t, three) * 3 + lax.div(t, three), t + 3)
ROPE_THETA = 10000.0
ALPHA = 2.0 ** 0.25
LN_EPS = 1e-5
ADAM_LR, ADAM_B1, ADAM_B2, ADAM_EPS, ADAM_WD, ADAM_STEP = 0.001, 0.9, 0.999, 1e-08, 0.01, 10
NEG = -1e30
MESH = pl.DeviceIdType.MESH


def _cp(sem=None, vmem_mb=None):
    kw = {}
    if sem is not None:
        kw["dimension_semantics"] = sem
    if vmem_mb is not None:
        kw["vmem_limit_bytes"] = vmem_mb << 20
    return pltpu.CompilerParams(**kw)


def _dot_nn(a, b):
    return lax.dot_general(a, b, (((1,), (0,)), ((), ())), preferred_element_type=F32)


def _dot_nt(a, b):
    return lax.dot_general(a, b, (((1,), (1,)), ((), ())), preferred_element_type=F32)


def _dot_tn(a, b):
    return lax.dot_general(a, b, (((0,), (0,)), ((), ())), preferred_element_type=F32)


def _ln_fwd(r, g, b):
    mu = jnp.mean(r, axis=-1, keepdims=True)
    xc = r - mu
    var = jnp.mean(xc * xc, axis=-1, keepdims=True)
    rstd = lax.rsqrt(var + LN_EPS)
    xhat = xc * rstd
    return xhat * g + b, xhat, rstd


def _ln_bwd(dh, xhat, rstd, g):
    dxh = dh * g
    m1 = jnp.mean(dxh, axis=-1, keepdims=True)
    m2 = jnp.mean(dxh * xhat, axis=-1, keepdims=True)
    return rstd * (dxh - m1 - xhat * m2)


def _sigmoid(x):
    return 0.5 * jnp.tanh(0.5 * x) + 0.5


def _colsum(x):
    return jnp.sum(x, axis=0, keepdims=True)


def _matmul(a, b, mode, out_dtype, tm, tn, tk, name, b_off=0, n_out=None, dep=None, a_map=None, m_out=None):
    n_dep = 0 if dep is None else 1
    a_map = a_map or (lambda t: t)
    if mode == "nn":
        m, k = a.shape[0], b.shape[0]
        n = b.shape[1]
    elif mode == "nt":
        m, k = a.shape
        n = n_out if n_out is not None else b.shape[0]
    else:
        k, m = a.shape[0], m_out or a.shape[1]
        n = b.shape[1]
    nk = k // tk
    assert m % tm == 0 and n % tn == 0 and k % tk == 0
    dot = {"nn": _dot_nn, "nt": _dot_nt, "tn": _dot_tn}[mode]

    def body(a_ref, b_ref, *rest):
        o_ref, scr = rest[n_dep], rest[n_dep + 1:]
        r = dot(a_ref[...].astype(BF16), b_ref[...].astype(BF16))
        if nk == 1:
            o_ref[...] = r.astype(out_dtype)
        else:
            acc = scr[0]
            kk = pl.program_id(2)

            @pl.when(kk == 0)
            def _():
                acc[...] = r

            @pl.when(kk > 0)
            def _():
                acc[...] += r

            @pl.when(kk == nk - 1)
            def _():
                o_ref[...] = acc[...].astype(out_dtype)

    if mode == "nn":
        a_spec = pl.BlockSpec((tm, tk), lambda i, j, kk: (i, a_map(kk)))
        b_spec = pl.BlockSpec((tk, tn), lambda i, j, kk: (kk, j))
    elif mode == "nt":
        a_spec = pl.BlockSpec((tm, tk), lambda i, j, kk: (i, kk))
        b_spec = pl.BlockSpec((tn, tk), lambda i, j, kk: (j + b_off, kk))
    else:
        a_spec = pl.BlockSpec((tk, tm), lambda i, j, kk: (kk, a_map(i)))
        b_spec = pl.BlockSpec((tk, tn), lambda i, j, kk: (kk, j))
    return pl.pallas_call(
        body,
        name=name,
        grid=(m // tm, n // tn, nk),
        in_specs=[a_spec, b_spec] + [pl.BlockSpec(memory_space=pl.ANY)] * n_dep,
        out_specs=pl.BlockSpec((tm, tn), lambda i, j, kk: (i, j)),
        out_shape=jax.ShapeDtypeStruct((m, n), out_dtype),
        scratch_shapes=[] if nk == 1 else [pltpu.VMEM((tm, tn), F32)],
        compiler_params=_cp(("parallel", "parallel", "arbitrary"), 56),
    )(a, b, *([] if dep is None else [dep]))


def _ffn_up(x, wgt, wut, name, tm=512, tn=1408, dep=None):
    n_dep = 0 if dep is None else 1

    def body(x_ref, wg_ref, wu_ref, *rest):
        ga_ref, gb_ref, hm_ref = rest[n_dep:]
        xb = x_ref[...].astype(BF16)
        a = _dot_nt(xb, wg_ref[...])
        b = _dot_nt(xb, wu_ref[...])
        sig = _sigmoid(a)
        silu = a * sig
        ga_ref[...] = (b * (sig + silu * (1.0 - sig))).astype(BF16)
        gb_ref[...] = silu.astype(BF16)
        hm_ref[...] = (silu * b).astype(BF16)

    wsp = pl.BlockSpec((tn, D), lambda i, j: (j, 0))
    mid = pl.BlockSpec((tm, tn), lambda i, j: (i, j))
    return pl.pallas_call(
        body,
        name=name,
        grid=(S // tm, F // tn),
        in_specs=[pl.BlockSpec((tm, D), lambda i, j: (i, 0)), wsp, wsp] + [pl.BlockSpec(memory_space=pl.ANY)] * n_dep,
        out_specs=[mid, mid, mid],
        out_shape=[jax.ShapeDtypeStruct((S, F), BF16)] * 3,
        compiler_params=_cp(("parallel", "arbitrary"), 56),
    )(x, wgt, wut, *([] if dep is None else [dep]))


def _ffn_down(hm, wd, x, ln_g, ln_b, name, tm=512):
    def body(hm_ref, wd_ref, x_ref, g_ref, b_ref, h_ref, hb_ref, xh_ref, rs_ref):
        r = ALPHA * x_ref[...] + 0.5 * _dot_nn(hm_ref[...], wd_ref[...])
        h, xhat, rstd = _ln_fwd(r, g_ref[...], b_ref[...])
        h_ref[...] = h
        hb_ref[...] = h.astype(BF16)
        xh_ref[...] = xhat
        rs_ref[...] = rstd

    row = pl.BlockSpec((tm, D), lambda i: (i, 0))
    vec = pl.BlockSpec((1, D), lambda i: (0, 0))
    return pl.pallas_call(
        body,
        name=name,
        grid=(S // tm,),
        in_specs=[pl.BlockSpec((tm, F), lambda i: (i, 0)), pl.BlockSpec((F, D), lambda i: (0, 0)), row, vec, vec],
        out_specs=[row, row, row, pl.BlockSpec((tm, 1), lambda i: (i, 0))],
        out_shape=[
            jax.ShapeDtypeStruct((S, D), F32),
            jax.ShapeDtypeStruct((S, D), BF16),
            jax.ShapeDtypeStruct((S, D), F32),
            jax.ShapeDtypeStruct((S, 1), F32),
        ],
        compiler_params=_cp(("parallel",), 56),
    )(hm, wd, x, ln_g, ln_b)


def _ffn_bwd_mid(dh_a, dh_b, xhat, rstd, ln_g, a, b, wd, name, tm=512, tn=1408):
    two = dh_b is not None

    def body(*refs):
        dha_ref = refs[0]
        dhb_ref = refs[1] if two else None
        (xh_ref, rs_ref, g_ref, a_ref, b_ref, wd_ref,
         dr_ref, df_ref, da_ref, db_ref, dg_ref, dbias_ref, df_scr) = refs[2 if two else 1:]
        i = pl.program_id(0)
        j = pl.program_id(1)

        @pl.when(j == 0)
        def _():
            dh = dha_ref[...]
            if two:
                dh = ALPHA * dh + dhb_ref[...]
            xhat = xh_ref[...]
            dr = _ln_bwd(dh, xhat, rs_ref[...], g_ref[...])
            dfb = (0.5 * dr).astype(BF16)
            dr_ref[...] = dr
            df_scr[...] = dfb
            df_ref[...] = dfb
            sg = _colsum(dh * xhat)
            sb = _colsum(dh)

            @pl.when(i == 0)
            def _():
                dg_ref[...] = sg
                dbias_ref[...] = sb

            @pl.when(i > 0)
            def _():
                dg_ref[...] += sg
                dbias_ref[...] += sb

        dhm = _dot_nt(df_scr[...], wd_ref[...])
        da_ref[...] = (dhm * a_ref[...].astype(F32)).astype(BF16)
        db_ref[...] = (dhm * b_ref[...].astype(F32)).astype(BF16)

    row = pl.BlockSpec((tm, D), lambda i, j: (i, 0))
    vec = pl.BlockSpec((1, D), lambda i, j: (0, 0))
    mid = pl.BlockSpec((tm, tn), lambda i, j: (i, j))
    ins = [dh_a] + ([dh_b] if two else []) + [xhat, rstd, ln_g, a, b, wd]
    in_specs = [row] * (2 if two else 1) + [row, pl.BlockSpec((tm, 1), lambda i, j: (i, 0)), vec, mid, mid,
                                            pl.BlockSpec((tn, D), lambda i, j: (j, 0))]
    return pl.pallas_call(
        body,
        name=name,
        grid=(S // tm, F // tn),
        in_specs=in_specs,
        out_specs=[row, row, mid, mid, vec, vec],
        out_shape=[
            jax.ShapeDtypeStruct((S, D), F32),
            jax.ShapeDtypeStruct((S, D), BF16),
            jax.ShapeDtypeStruct((S, F), BF16),
            jax.ShapeDtypeStruct((S, F), BF16),
            jax.ShapeDtypeStruct((1, D), F32),
            jax.ShapeDtypeStruct((1, D), F32),
        ],
        scratch_shapes=[pltpu.VMEM((tm, D), BF16)],
        compiler_params=_cp(("arbitrary", "arbitrary"), 56),
    )(*ins)


def _ffn_bwd_dx(dr, da, db, wgt, wut, name, tm=512, tk=1408, dep=None):
    nk = F // tk
    n_dep = 0 if dep is None else 1

    def body(dr_ref, da_ref, db_ref, wg_ref, wu_ref, *rest):
        dx_ref, acc = rest[n_dep], rest[n_dep + 1]
        kk = pl.program_id(1)
        part = _dot_nn(da_ref[...], wg_ref[...]) + _dot_nn(db_ref[...], wu_ref[...])

        @pl.when(kk == 0)
        def _():
            acc[...] = ALPHA * dr_ref[...] + part

        @pl.when(kk > 0)
        def _():
            acc[...] += part

        @pl.when(kk == nk - 1)
        def _():
            dx_ref[...] = acc[...]

    row = pl.BlockSpec((tm, D), lambda i, kk: (i, 0))
    mid = pl.BlockSpec((tm, tk), lambda i, kk: (i, kk))
    wsp = pl.BlockSpec((tk, D), lambda i, kk: (kk, 0))
    return pl.pallas_call(
        body,
        name=name,
        grid=(S // tm, nk),
        in_specs=[row, mid, mid, wsp, wsp] + [pl.BlockSpec(memory_space=pl.ANY)] * n_dep,
        out_specs=row,
        out_shape=jax.ShapeDtypeStruct((S, D), F32),
        scratch_shapes=[pltpu.VMEM((tm, D), F32)],
        compiler_params=_cp(("parallel", "arbitrary"), 56),
    )(dr, da, db, wgt, wut, *([] if dep is None else [dep]))


def _rope_tables(pos_col, invf, sign, tm=512):
    def body(p_ref, f_ref, s_ref, c_out, s_out):
        ang = p_ref[...].astype(F32) * f_ref[...]
        c_out[...] = jnp.cos(ang)
        s_out[...] = jnp.sin(ang) * s_ref[...]

    vec = pl.BlockSpec((1, BLK), lambda i: (0, 0))
    out = pl.BlockSpec((tm, BLK), lambda i: (i, 0))
    return pl.pallas_call(
        body,
        name="rope_tables",
        grid=(S // tm,),
        in_specs=[pl.BlockSpec((tm, 1), lambda i: (i, 0)), vec, vec],
        out_specs=[out, out],
        out_shape=[jax.ShapeDtypeStruct((S, BLK), F32)] * 2,
        compiler_params=_cp(("parallel",)),
    )(pos_col, invf, sign)


def _lane_lo(rows=BLK):
    return (lax.broadcasted_iota(jnp.int32, (rows, GW), 1) % HEAD_DIM) < (HEAD_DIM // 2)


def _swap_halves(t, lo):
    return jnp.where(lo, pltpu.roll(t, GW - HEAD_DIM // 2, 1), pltpu.roll(t, HEAD_DIM // 2, 1))


def _rope(t, cosf, sinf, lo):
    return t * cosf + _swap_halves(t, lo) * sinf


def _unrope(g, cosf, sinf, lo):
    return g * cosf + _swap_halves(g * sinf, lo)


def _tile4(v):
    return jnp.concatenate([v, v, v, v], axis=1)


def _band_mask(n):
    qi = lax.broadcasted_iota(jnp.int32, (BLK, 2 * BLK), 0)
    kj = lax.broadcasted_iota(jnp.int32, (BLK, 2 * BLK), 1)
    dist = qi + BLK - kj
    return (dist >= 0) & (dist <= BLK) & ((kj >= BLK) | (n >= 1))


ROWS = 256
LANES = 128


def _to_view(scr, y, dst_ref, d, dtype, col0=0):
    if d == 1:
        dst_ref[:, col0:col0 + GW] = y.astype(dtype)
        return
    for cb in range(GW // LANES):
        scr[cb][...] = y[:, cb * LANES:(cb + 1) * LANES]
    for r in range(d):
        for cb in range(GW // LANES):
            at = col0 + r * GW + cb * LANES
            dst_ref[:, at:at + LANES] = scr[cb][pl.ds(r, ROWS // d, stride=d), :].astype(dtype)


def _from_view(scr, src_ref, d):
    if d == 1:
        return src_ref[...].astype(F32)
    for r in range(d):
        for cb in range(GW // LANES):
            at = r * GW + cb * LANES
            scr[cb][pl.ds(r, ROWS // d, stride=d), :] = src_ref[:, at:at + LANES].astype(F32)
    return jnp.concatenate([scr[cb][...] for cb in range(GW // LANES)], axis=1)


def _view_spec(d):
    return pl.BlockSpec((ROWS // d, d * GW), lambda i: (i, 0))


_VIEW_SCRATCH = [pltpu.VMEM((ROWS, LANES), F32)] * (GW // LANES)


def _qkv_prep(qkv, cos_t, sin_t):
    def body(x_ref, c_ref, s_ref, *rest):
        outs, scr = rest[:9], rest[9:]
        lo = _lane_lo(ROWS)
        cf, sf = _tile4(c_ref[...]), _tile4(s_ref[...])
        for gi, d in enumerate(PATTERN_DILATIONS):
            q, k, v = (x_ref[:, (3 * part + gi) * GW:(3 * part + gi + 1) * GW].astype(F32) for part in range(3))
            _to_view(scr, _rope(q, cf, sf, lo) * (HEAD_DIM ** -0.5), outs[3 * gi], d, BF16)
            _to_view(scr, _rope(k, cf, sf, lo), outs[3 * gi + 1], d, BF16)
            _to_view(scr, v, outs[3 * gi + 2], d, BF16)

    tab = pl.BlockSpec((ROWS, BLK), lambda i: (i, 0))
    outs = pl.pallas_call(
        body,
        name="qkv_prep",
        grid=(S // ROWS,),
        in_specs=[pl.BlockSpec((ROWS, QKV_W), lambda i: (i, 0)), tab, tab],
        out_specs=[_view_spec(d) for d in PATTERN_DILATIONS for _ in range(3)],
        out_shape=[jax.ShapeDtypeStruct((S // d, d * GW), BF16) for d in PATTERN_DILATIONS for _ in range(3)],
        scratch_shapes=_VIEW_SCRATCH,
        compiler_params=_cp(("parallel",), 48),
    )(qkv, cos_t, sin_t)
    return [tuple(outs[3 * gi:3 * gi + 3]) for gi in range(3)]


N_ATTN_STEPS = S // BLK


def _class_and_block(gi, t):
    per_class = N_ATTN_STEPS // PATTERN_DILATIONS[gi]
    return t >> (per_class.bit_length() - 1), t & (per_class - 1)


def _attn_specs(gi):
    def cur(t):
        r, n = _class_and_block(gi, t)
        return n, r

    def prev(t):
        r, n = _class_and_block(gi, t)
        return jnp.maximum(n - 1, 0), r

    def whole(t):
        return 0, _class_and_block(gi, t)[0]

    sub = S // PATTERN_DILATIONS[gi]
    return (pl.BlockSpec((BLK, GW), cur), pl.BlockSpec((BLK, GW), prev), pl.BlockSpec((sub, GW), whole))


def _left_lanes():
    return lax.broadcasted_iota(jnp.int32, (BLK, LANES), 1) < HEAD_DIM


def _stack_heads(t2, left):
    zero = jnp.zeros_like(t2)
    return jnp.concatenate([jnp.where(left, t2, zero), jnp.where(left, zero, t2)], axis=0)


def _attn_fwd_block(n, q_ref, kc_ref, kp_ref, vc_ref, vp_ref, o_ref, l_ref):
    mask = _band_mask(n)
    left = _left_lanes()
    zero = jnp.zeros((BLK, LANES), BF16)
    for pr in range(GW // LANES):
        cs = slice(pr * LANES, (pr + 1) * LANES)
        q2 = q_ref[:, cs]
        k2 = jnp.concatenate([kp_ref[:, cs], kc_ref[:, cs]], axis=0)
        v2 = jnp.concatenate([vp_ref[:, cs], vc_ref[:, cs]], axis=0)
        o_h, lse_h = [], []
        for side in (left, ~left):
            s = jnp.where(mask, _dot_nt(jnp.where(side, q2, zero), k2), NEG)
            m = jnp.max(s, axis=1, keepdims=True)
            p = jnp.exp(s - m)
            l = jnp.sum(p, axis=1, keepdims=True)
            o_h.append(_dot_nn((p / l).astype(BF16), v2))
            lse_h.append(m + jnp.log(l))
        o_ref[:, cs] = jnp.where(left, o_h[0], o_h[1])
        l_ref[:, cs] = jnp.where(left, lse_h[0], lse_h[1])


def _attn_fwd(preps):
    def body(*refs):
        t = pl.program_id(0)
        for gi in range(3):
            _attn_fwd_block(_class_and_block(gi, t)[1], *refs[5 * gi:5 * gi + 5], *refs[15 + 2 * gi:17 + 2 * gi])

    in_specs, out_specs, out_shape, args = [], [], [], []
    for gi, d in enumerate(PATTERN_DILATIONS):
        cur, prev, _ = _attn_specs(gi)
        qr, kr, vv = preps[gi]
        in_specs += [cur, cur, prev, cur, prev]
        args += [qr, kr, kr, vv, vv]
        out_specs += [cur, cur]
        out_shape += [jax.ShapeDtypeStruct((S // d, d * GW), F32)] * 2
    outs = pl.pallas_call(
        body,
        name="attn_fwd",
        grid=(N_ATTN_STEPS,),
        in_specs=in_specs,
        out_specs=out_specs,
        out_shape=out_shape,
        compiler_params=_cp(("arbitrary",), 48),
    )(*args)
    return [outs[0], outs[2], outs[4]], [outs[1], outs[3], outs[5]]


def _attn_bwd_block(n, q_ref, kc_ref, kp_ref, vc_ref, vp_ref, do_ref, l_ref, c_ref, dq_ref, dk_ref, dv_ref):
    mask = _band_mask(n)
    left = _left_lanes()
    left2 = jnp.concatenate([left, left], axis=0)
    here = pl.ds(pl.multiple_of(n * BLK, BLK), BLK)
    before = pl.ds(pl.multiple_of(jnp.maximum(n - 1, 0) * BLK, BLK), BLK)
    for pr in range(GW // LANES):
        cs = slice(pr * LANES, (pr + 1) * LANES)
        q2 = q_ref[:, cs]
        do2 = do_ref[:, cs]
        k_bd = _stack_heads(jnp.concatenate([kp_ref[:, cs], kc_ref[:, cs]], axis=0), left2)
        v_bd = _stack_heads(jnp.concatenate([vp_ref[:, cs], vc_ref[:, cs]], axis=0), left2)
        s2 = _dot_nt(q2, k_bd)
        dp2 = _dot_nt(do2, v_bd)
        ps, dss = [], []
        for h in range(2):
            at = pr * LANES + h * HEAD_DIM
            half = slice(h * 2 * BLK, (h + 1) * 2 * BLK)
            p = jnp.exp(jnp.where(mask, s2[:, half], NEG) - l_ref[:, at:at + 1])
            ps.append(p.astype(BF16))
            dss.append((p * (dp2[:, half] - c_ref[:, at:at + 1])).astype(BF16))
        ds_both = jnp.concatenate(dss, axis=1)
        dq_ref[:, cs] = _dot_nn(ds_both, k_bd) * (HEAD_DIM ** -0.5)
        dk_bd = _dot_tn(ds_both, q2)
        dv_bd = _dot_tn(jnp.concatenate(ps, axis=1), do2)
        dk2 = jnp.where(left2, dk_bd[:2 * BLK], dk_bd[2 * BLK:])
        dv2 = jnp.where(left2, dv_bd[:2 * BLK], dv_bd[2 * BLK:])
        dk_ref[here, cs] = dk2[BLK:]
        dv_ref[here, cs] = dv2[BLK:]
        dk_ref[before, cs] += dk2[:BLK]
        dv_ref[before, cs] += dv2[:BLK]


def _attn_bwd(preps, dos, lses, cterms):
    def body(*refs):
        t = pl.program_id(0)
        for gi in range(3):
            _attn_bwd_block(_class_and_block(gi, t)[1], *refs[8 * gi:8 * gi + 8], *refs[24 + 3 * gi:27 + 3 * gi])

    in_specs, out_specs, out_shape, args = [], [], [], []
    for gi, d in enumerate(PATTERN_DILATIONS):
        cur, prev, whole = _attn_specs(gi)
        qr, kr, vv = preps[gi]
        in_specs += [cur, cur, prev, cur, prev, cur, cur, cur]
        args += [qr, kr, kr, vv, vv, dos[gi], lses[gi], cterms[gi]]
        out_specs += [cur, whole, whole]
        out_shape += [jax.ShapeDtypeStruct((S // d, d * GW), F32)] * 3
    outs = pl.pallas_call(
        body,
        name="attn_bwd",
        grid=(N_ATTN_STEPS,),
        in_specs=in_specs,
        out_specs=out_specs,
        out_shape=out_shape,
        compiler_params=_cp(("arbitrary",), 56),
    )(*args)
    return [tuple(outs[3 * gi:3 * gi + 3]) for gi in range(3)]


def _qkv_unprep(grads, cos_t, sin_t, dproj):
    def body(*refs):
        views, (c_ref, s_ref, _, out_ref), scr = refs[:9], refs[9:13], refs[13:]
        lo = _lane_lo(ROWS)
        cf, sf = _tile4(c_ref[...]), _tile4(s_ref[...])
        for gi, d in enumerate(PATTERN_DILATIONS):
            dq_ref, dk_ref, dv_ref = views[3 * gi:3 * gi + 3]
            at = 3 * gi * GW
            out_ref[:, at:at + GW] = _unrope(_from_view(scr, dq_ref, d), cf, sf, lo).astype(BF16)
            out_ref[:, at + GW:at + 2 * GW] = _unrope(_from_view(scr, dk_ref, d), cf, sf, lo).astype(BF16)
            out_ref[:, at + 2 * GW:at + 3 * GW] = _from_view(scr, dv_ref, d).astype(BF16)

    tab = pl.BlockSpec((ROWS, BLK), lambda i: (i, 0))
    return pl.pallas_call(
        body,
        name="qkv_unprep",
        grid=(S // ROWS,),
        in_specs=[_view_spec(d) for d in PATTERN_DILATIONS for _ in range(3)] + [tab, tab, pl.BlockSpec(memory_space=pl.ANY)],
        out_specs=pl.BlockSpec((ROWS, QKV_W), lambda i: (i, 0)),
        out_shape=jax.ShapeDtypeStruct(dproj.shape, dproj.dtype),
        input_output_aliases={11: 0},
        scratch_shapes=_VIEW_SCRATCH,
        compiler_params=_cp(("parallel",), 48),
    )(*[g for grp in grads for g in grp], cos_t, sin_t, dproj)


def _group_weights(l0, l1, l2):
    mx = jnp.maximum(jnp.maximum(l0, l1), l2)
    e0, e1, e2 = jnp.exp(l0 - mx), jnp.exp(l1 - mx), jnp.exp(l2 - mx)
    inv = 1.0 / (e0 + e1 + e2)
    return e0 * inv, e1 * inv, e2 * inv


def _combine_fwd(os_, lses):
    def body(o0, o1, o2, l0, l1, l2, y_ref, *scr):
        ov = [_from_view(scr, o, d) for o, d in zip((o0, o1, o2), PATTERN_DILATIONS)]
        lv = [_from_view(scr, l, d) for l, d in zip((l0, l1, l2), PATTERN_DILATIONS)]
        w0, w1, w2 = _group_weights(*lv)
        y_ref[...] = (w0 * ov[0] + w1 * ov[1] + w2 * ov[2]).astype(BF16)

    views = [_view_spec(d) for d in PATTERN_DILATIONS]
    return pl.pallas_call(
        body,
        name="attn_combine_fwd",
        grid=(S // ROWS,),
        in_specs=views + views,
        out_specs=pl.BlockSpec((ROWS, GW), lambda i: (i, 0)),
        out_shape=jax.ShapeDtypeStruct((S, GW), BF16),
        scratch_shapes=_VIEW_SCRATCH,
        compiler_params=_cp(("parallel",)),
    )(*os_, *lses)


def _combine_bwd(dy, os_, lses, seg):
    def body(dy_ref, o0, o1, o2, l0, l1, l2, seg_ref, d0, d1, d2, c0, c1, c2, *scr):
        ov = [_from_view(scr, o, d) for o, d in zip((o0, o1, o2), PATTERN_DILATIONS)]
        lv = [_from_view(scr, l, d) for l, d in zip((l0, l1, l2), PATTERN_DILATIONS)]
        ws = _group_weights(*lv)
        dyv = dy_ref[...]
        t = dyv * (ws[0] * ov[0] + ws[1] * ov[1] + ws[2] * ov[2])
        t_hi = t.astype(BF16)
        r1 = t - t_hi.astype(F32)
        t_mid = r1.astype(BF16)
        t_lo = (r1 - t_mid.astype(F32)).astype(BF16)
        sg = seg_ref[...]
        e = _dot_nn(t_hi, sg) + _dot_nn(t_mid, sg) + _dot_nn(t_lo, sg)
        for w, do_ref, c_ref, d in zip(ws, (d0, d1, d2), (c0, c1, c2), PATTERN_DILATIONS):
            _to_view(scr, w * dyv, do_ref, d, BF16)
            _to_view(scr, w * e, c_ref, d, F32)

    views = [_view_spec(d) for d in PATTERN_DILATIONS]
    return pl.pallas_call(
        body,
        name="attn_combine_bwd",
        grid=(S // ROWS,),
        in_specs=[pl.BlockSpec((ROWS, GW), lambda i: (i, 0))] + views + views + [pl.BlockSpec((GW, GW), lambda i: (0, 0))],
        out_specs=views + views,
        out_shape=[jax.ShapeDtypeStruct((S // d, d * GW), BF16) for d in PATTERN_DILATIONS]
        + [jax.ShapeDtypeStruct((S // d, d * GW), F32) for d in PATTERN_DILATIONS],
        scratch_shapes=_VIEW_SCRATCH,
        compiler_params=_cp(("parallel",)),
    )(dy, *os_, *lses, seg)


_SQRT_HALF = 0.7071067811865476
_INV_SQRT_2PI = 0.3989422804014327


def _gelu(z):
    return 0.5 * z * (1.0 + lax.erf(z * _SQRT_HALF))


def _gelu_grad(z):
    return 0.5 * (1.0 + lax.erf(z * _SQRT_HALF)) + z * (_INV_SQRT_2PI * jnp.exp(-0.5 * z * z))


def _tril_ws(ws_ref, g):
    t = lax.broadcasted_iota(jnp.int32, (BLK, BLK), 0)
    s = lax.broadcasted_iota(jnp.int32, (BLK, BLK), 1)
    return jnp.where(t >= s, ws_ref[g], 0.0)


def _chunks_side_by_side(x, cols, nch):
    return jnp.concatenate([x[c * BLK:(c + 1) * BLK, cols] for c in range(nch)], axis=1)


def _gmlp_fwd(z, ws, bst, ln_g, ln_b, tm=512):
    nch = tm // BLK

    def body(z_ref, ws_ref, b_ref, g_ref, be_ref, y_ref):
        zg = _gelu(z_ref[...].astype(F32))
        u = zg[:, :D]
        vn, _, _ = _ln_fwd(zg[:, D:], g_ref[...], be_ref[...])
        vnb = vn.astype(BF16)
        bt = b_ref[...]
        for g in range(8):
            w = _tril_ws(ws_ref, g).astype(BF16)
            cols = slice(g * BLK, (g + 1) * BLK)
            mixed = _dot_nn(w, _chunks_side_by_side(vnb, cols, nch)) + bt[:, g:g + 1]
            for c in range(nch):
                rows = slice(c * BLK, (c + 1) * BLK)
                y_ref[rows, cols] = (u[rows, cols] * mixed[:, c * BLK:(c + 1) * BLK]).astype(BF16)

    return pl.pallas_call(
        body,
        name="gmlp_fwd",
        grid=(S // tm,),
        in_specs=[
            pl.BlockSpec((tm, 2 * D), lambda i: (i, 0)),
            pl.BlockSpec((8, BLK, BLK), lambda i: (0, 0, 0)),
            pl.BlockSpec((BLK, 8), lambda i: (0, 0)),
            pl.BlockSpec((1, D), lambda i: (0, 0)),
            pl.BlockSpec((1, D), lambda i: (0, 0)),
        ],
        out_specs=pl.BlockSpec((tm, D), lambda i: (i, 0)),
        out_shape=jax.ShapeDtypeStruct((S, D), BF16),
        compiler_params=_cp(("parallel",), 48),
    )(z, ws, bst, ln_g, ln_b)


def _gmlp_bwd(z, dy, ws, bst, ln_g, ln_b, dproj, tm=512):
    nch = tm // BLK

    def body(z_ref, dy_ref, ws_ref, b_ref, g_ref, be_ref, _, dz_ref, dws_ref, dbs_ref, dg_ref, dbe_ref, dvn_scr, dm_acc):
        i = pl.program_id(0)
        zv = z_ref[...].astype(F32)
        zg = _gelu(zv)
        u = zg[:, :D]
        gam = g_ref[...]
        vn, xhat, rstd = _ln_fwd(zg[:, D:], gam, be_ref[...])
        vnb = vn.astype(BF16)
        dyv = dy_ref[...]
        dmix = dyv * u
        dmb = dmix.astype(BF16)
        bt = b_ref[...]
        tmask = lax.broadcasted_iota(jnp.int32, (BLK, BLK), 0) >= lax.broadcasted_iota(jnp.int32, (BLK, BLK), 1)
        dm_sum = dmix[0:BLK]
        for c in range(1, nch):
            dm_sum = dm_sum + dmix[c * BLK:(c + 1) * BLK]

        @pl.when(i == 0)
        def _():
            dm_acc[...] = jnp.zeros_like(dm_acc)
            dws_ref[...] = jnp.zeros_like(dws_ref)
            dg_ref[...] = jnp.zeros_like(dg_ref)
            dbe_ref[...] = jnp.zeros_like(dbe_ref)

        dm_acc[...] += dm_sum
        dus = []
        for g in range(8):
            w = _tril_ws(ws_ref, g).astype(BF16)
            cols = slice(g * BLK, (g + 1) * BLK)
            v_cat = _chunks_side_by_side(vnb, cols, nch)
            dm_cat = _chunks_side_by_side(dmb, cols, nch)
            mixed = _dot_nn(w, v_cat) + bt[:, g:g + 1]
            dus.append(jnp.concatenate(
                [dyv[c * BLK:(c + 1) * BLK, cols] * mixed[:, c * BLK:(c + 1) * BLK] for c in range(nch)], axis=0))
            dws_ref[g] += jnp.where(tmask, _dot_nt(dm_cat, v_cat), 0.0)
            dvn_cat = _dot_tn(w, dm_cat)
            for c in range(nch):
                dvn_scr[c * BLK:(c + 1) * BLK, cols] = dvn_cat[:, c * BLK:(c + 1) * BLK]

        dvn = dvn_scr[...]
        dg_ref[...] += _colsum(dvn * xhat)
        dbe_ref[...] += _colsum(dvn)
        dvg = _ln_bwd(dvn, xhat, rstd, gam)
        gp = _gelu_grad(zv)
        dz_ref[:, :D] = (jnp.concatenate(dus, axis=1) * gp[:, :D]).astype(BF16)
        dz_ref[:, D:] = (dvg * gp[:, D:]).astype(BF16)

        @pl.when(i == S // tm - 1)
        def _():
            acc = dm_acc[...]
            for g in range(8):
                dbs_ref[:, g:g + 1] = jnp.sum(acc[:, g * BLK:(g + 1) * BLK], axis=1, keepdims=True)

    vec = pl.BlockSpec((1, D), lambda i: (0, 0))
    return pl.pallas_call(
        body,
        name="gmlp_bwd",
        grid=(S // tm,),
        in_specs=[
            pl.BlockSpec((tm, 2 * D), lambda i: (i, 0)),
            pl.BlockSpec((tm, D), lambda i: (i, 0)),
            pl.BlockSpec((8, BLK, BLK), lambda i: (0, 0, 0)),
            pl.BlockSpec((BLK, 8), lambda i: (0, 0)),
            vec,
            vec,
            pl.BlockSpec(memory_space=pl.ANY),
        ],
        out_specs=[
            pl.BlockSpec((tm, 2 * D), lambda i: (i, DPROJ_Z_COL)),
            pl.BlockSpec((8, BLK, BLK), lambda i: (0, 0, 0)),
            pl.BlockSpec((BLK, 8), lambda i: (0, 0)),
            vec,
            vec,
        ],
        out_shape=[
            jax.ShapeDtypeStruct(dproj.shape, dproj.dtype),
            jax.ShapeDtypeStruct((8, BLK, BLK), F32),
            jax.ShapeDtypeStruct((BLK, 8), F32),
            jax.ShapeDtypeStruct((1, D), F32),
            jax.ShapeDtypeStruct((1, D), F32),
        ],
        input_output_aliases={6: 0},
        scratch_shapes=[pltpu.VMEM((tm, D), F32), pltpu.VMEM((BLK, D), F32)],
        compiler_params=_cp(("arbitrary",), 48),
    )(z, dy, ws, bst, ln_g, ln_b, dproj)


def _merge_fwd(ya, yg, glog, bgate, h1, wabt, wgb, wo, ln_g, ln_b, tm=256):
    def body(ya_ref, yg_ref, gl_ref, bg_ref, h1_ref, wab_ref, wgb_ref, wo_ref, g_ref, b_ref,
             h_ref, hb_ref, xh_ref, rs_ref, mg_ref, bra_ref, brg_ref):
        bra = _dot_nt(ya_ref[...], wab_ref[...])
        brg = _dot_nn(yg_ref[...], wgb_ref[...])
        gates = _sigmoid(gl_ref[...].astype(F32) + bg_ref[...])
        merged = (gates[:, :D] * bra + gates[:, D:] * brg).astype(BF16)
        mix = _dot_nn(merged, wo_ref[...])
        h, xhat, rstd = _ln_fwd(ALPHA * h1_ref[...] + mix, g_ref[...], b_ref[...])
        h_ref[...] = h
        hb_ref[...] = h.astype(BF16)
        xh_ref[...] = xhat
        rs_ref[...] = rstd
        mg_ref[...] = merged
        bra_ref[...] = bra
        brg_ref[...] = brg

    row = pl.BlockSpec((tm, D), lambda i: (i, 0))
    vec = pl.BlockSpec((1, D), lambda i: (0, 0))
    full = lambda shape: pl.BlockSpec(shape, lambda i: (0, 0))
    return pl.pallas_call(
        body,
        name="merge_fwd",
        grid=(S // tm,),
        in_specs=[
            pl.BlockSpec((tm, GW), lambda i: (i, 0)), row,
            pl.BlockSpec((tm, 2 * D), lambda i: (i, glog.shape[1] // (2 * D) - 1)),
            full((1, 2 * D)), row,
            full((D, GW)), full((D, D)), full((D, D)), vec, vec,
        ],
        out_specs=[row, row, row, pl.BlockSpec((tm, 1), lambda i: (i, 0)), row, row, row],
        out_shape=[
            jax.ShapeDtypeStruct((S, D), F32),
            jax.ShapeDtypeStruct((S, D), BF16),
            jax.ShapeDtypeStruct((S, D), F32),
            jax.ShapeDtypeStruct((S, 1), F32),
            jax.ShapeDtypeStruct((S, D), BF16),
            jax.ShapeDtypeStruct((S, D), F32),
            jax.ShapeDtypeStruct((S, D), F32),
        ],
        compiler_params=_cp(("parallel",), 48),
    )(ya, yg, glog, bgate, h1, wabt, wgb, wo, ln_g, ln_b)


def _merge_bwd(dh2, xhat, rstd, ln_g, bra, brg, glog, bgate, wabt, wgb, wo, tm=256):
    def body(dh_ref, xh_ref, rs_ref, g_ref, bra_ref, brg_ref, gl_ref, bg_ref, wab_ref, wgb_ref, wo_ref,
             dr_ref, drb_ref, dlog_ref, dba_ref, dbg_ref, dya_ref, dyg_ref, dbgate_ref, dg_ref, dbias_ref):
        i = pl.program_id(0)
        dh = dh_ref[...]
        xh = xh_ref[...]
        dr = _ln_bwd(dh, xh, rs_ref[...], g_ref[...])
        drb = dr.astype(BF16)
        dr_ref[...] = dr
        drb_ref[...] = drb
        dmerged = _dot_nt(drb, wo_ref[...])
        gates = _sigmoid(gl_ref[...].astype(F32) + bg_ref[...])
        g0, g1 = gates[:, :D], gates[:, D:]
        dl0 = dmerged * bra_ref[...] * g0 * (1.0 - g0)
        dl1 = dmerged * brg_ref[...] * g1 * (1.0 - g1)
        dlog_ref[:, :D] = dl0.astype(BF16)
        dlog_ref[:, D:] = dl1.astype(BF16)
        dba = (dmerged * g0).astype(BF16)
        dbg = (dmerged * g1).astype(BF16)
        dba_ref[...] = dba
        dbg_ref[...] = dbg
        dya_ref[...] = _dot_nn(dba, wab_ref[...])
        dyg_ref[...] = _dot_nt(dbg, wgb_ref[...])
        s0, s1 = _colsum(dl0), _colsum(dl1)
        sg, sb = _colsum(dh * xh), _colsum(dh)

        @pl.when(i == 0)
        def _():
            dbgate_ref[:, :D] = s0
            dbgate_ref[:, D:] = s1
            dg_ref[...] = sg
            dbias_ref[...] = sb

        @pl.when(i > 0)
        def _():
            dbgate_ref[:, :D] += s0
            dbgate_ref[:, D:] += s1
            dg_ref[...] += sg
            dbias_ref[...] += sb

    row = pl.BlockSpec((tm, D), lambda i: (i, 0))
    vec = pl.BlockSpec((1, D), lambda i: (0, 0))
    wide = pl.BlockSpec((tm, 2 * D), lambda i: (i, 0))
    full = lambda shape: pl.BlockSpec(shape, lambda i: (0, 0))
    return pl.pallas_call(
        body,
        name="merge_bwd",
        grid=(S // tm,),
        in_specs=[row, row, pl.BlockSpec((tm, 1), lambda i: (i, 0)), vec, row, row,
                  pl.BlockSpec((tm, 2 * D), lambda i: (i, glog.shape[1] // (2 * D) - 1)),
                  full((1, 2 * D)), full((D, GW)), full((D, D)), full((D, D))],
        out_specs=[row, row, pl.BlockSpec((tm, 2 * D), lambda i: (i, DPROJ_G_COL)), row, row,
                   pl.BlockSpec((tm, GW), lambda i: (i, 0)), row, full((1, 2 * D)), vec, vec],
        out_shape=[
            jax.ShapeDtypeStruct((S, D), F32),
            jax.ShapeDtypeStruct((S, D), BF16),
            jax.ShapeDtypeStruct((S, DPROJ_W), BF16),
            jax.ShapeDtypeStruct((S, D), BF16),
            jax.ShapeDtypeStruct((S, D), BF16),
            jax.ShapeDtypeStruct((S, GW), F32),
            jax.ShapeDtypeStruct((S, D), F32),
            jax.ShapeDtypeStruct((1, 2 * D), F32),
            jax.ShapeDtypeStruct((1, D), F32),
            jax.ShapeDtypeStruct((1, D), F32),
        ],
        compiler_params=_cp(("arbitrary",), 48),
    )(dh2, xhat, rstd, ln_g, bra, brg, glog, bgate, wabt, wgb, wo)


def _loss_head(h3, target, tm=512):
    def body(h_ref, t_ref, d_ref, l_ref):
        i = pl.program_id(0)
        e = h_ref[...] - t_ref[...]
        d_ref[...] = e * (1.0 / D)
        part = jnp.sum(_colsum(e * e), axis=1, keepdims=True) * (0.5 / D)

        @pl.when(i == 0)
        def _():
            l_ref[...] = part

        @pl.when(i > 0)
        def _():
            l_ref[...] += part

    row = pl.BlockSpec((tm, D), lambda i: (i, 0))
    return pl.pallas_call(
        body,
        name="loss_head",
        grid=(S // tm,),
        in_specs=[row, row],
        out_specs=[row, pl.BlockSpec((1, 1), lambda i: (0, 0))],
        out_shape=[jax.ShapeDtypeStruct((S, D), F32), jax.ShapeDtypeStruct((1, 1), F32)],
        compiler_params=_cp(("arbitrary",)),
    )(h3, target)


def _tie(x, dep):
    if dep is None:
        return x
    return x + dep[0, 0].astype(x.dtype)


def _local_step(x, pos_col, target, get_w, p, emit):
    w = get_w("ffn1", None)
    a1, b1, hm1 = _ffn_up(x, w["g1"], w["u1"], "ffn1_up")
    w.update(get_w("ffn1d", hm1))
    h1, h1b, xh1, rs1 = _ffn_down(hm1, w["d1"], x, p["ln1_g"], p["ln1_b"], "ffn1_down")

    w.update(get_w("win", h1b))
    qkv = _matmul(h1b, w["win"], "nt", BF16, 1024, 1536, D, "proj_qkv", b_off=0, n_out=QKV_W)
    z = glog = _matmul(h1b, w["win"], "nt", BF16, 1024, 512, D, "proj_zg", b_off=QKV_W // 512, n_out=4 * D)

    half = jnp.arange(0, HEAD_DIM, 2, dtype=F32) / HEAD_DIM
    inv_freq = ROPE_THETA ** (-half)
    invf = jnp.tile(inv_freq, 4).reshape(1, BLK)
    sign = jnp.tile(jnp.concatenate([-jnp.ones((32,), F32), jnp.ones((32,), F32)]), 2).reshape(1, BLK)
    cos_t, sin_t = _rope_tables(pos_col, invf, sign)

    preps = _qkv_prep(qkv, cos_t, sin_t)
    os_, lses = _attn_fwd(preps)
    ya = _combine_fwd(os_, lses)
    get_w("late", ya, early=True)
    bst = p["gmlp_b_s"].T
    yg = _gmlp_fwd(z, p["gmlp_w_s"], bst, p["gmlp_ln_g"], p["gmlp_ln_b"])
    w.update(get_w("late", yg))
    h2, h2b, xh2, rs2, merged, bra, brg = _merge_fwd(ya, yg, glog, p["b_gates"], h1, w["ab"], w["gb"], w["o"],
                                                      p["ln2_g"], p["ln2_b"])
    a2, b2, hm2 = _ffn_up(h2b, w["g2"], w["u2"], "ffn2_up")
    h3, _, xh3, rs3 = _ffn_down(hm2, w["d2"], h2, p["ln3_g"], p["ln3_b"], "ffn2_down")
    dh3, loss = _loss_head(h3, target)

    gp = {}
    dr3, df2, da2, db2, gp["ln3_g"], gp["ln3_b"] = _ffn_bwd_mid(dh3, None, xh3, rs3, p["ln3_g"], a2, b2, w["d2"],
                                                                "ffn2_bwd_mid")
    tok = emit("ffn2", {
        "g2": _matmul(da2, h2b, "tn", BF16, 1408, D, S, "wgrad_g2"),
        "u2": _matmul(db2, h2b, "tn", BF16, 1408, D, S, "wgrad_u2"),
        "d2": _matmul(hm2, df2, "tn", BF16, 1408, D, S, "wgrad_d2")})
    dh2 = _ffn_bwd_dx(dr3, da2, db2, w["g2"], w["u2"], "ffn2_bwd_dx")

    (dr2, dr2b, dproj, dba, dbg, dya, dyg, gp["b_gates"], gp["ln2_g"], gp["ln2_b"]) = _merge_bwd(
        dh2, xh2, rs2, _tie(p["ln2_g"], tok), bra, brg, glog, p["b_gates"], w["ab"], w["gb"], w["o"])
    tok = emit("mix", {
        "o": _matmul(merged, dr2b, "tn", BF16, 512, D, S, "wgrad_o"),
        "ab": _matmul(dba, ya, "tn", BF16, 512, GW, S, "wgrad_ab"),
        "gb": _matmul(yg, dbg, "tn", BF16, 512, D, S, "wgrad_gb")})

    seg = (jnp.arange(GW)[:, None] // HEAD_DIM == jnp.arange(GW)[None, :] // HEAD_DIM).astype(BF16)
    do0, do1, do2, c0, c1, c2 = _combine_bwd(dya, os_, lses, _tie(seg, tok))
    dproj = _qkv_unprep(_attn_bwd(preps, (do0, do1, do2), lses, (c0, c1, c2)), cos_t, sin_t, dproj)
    dproj, gp["gmlp_w_s"], dbst, gp["gmlp_ln_g"], gp["gmlp_ln_b"] = _gmlp_bwd(
        z, dyg, p["gmlp_w_s"], bst, p["gmlp_ln_g"], p["gmlp_ln_b"], dproj)
    gp["gmlp_b_s"] = dbst.T
    tok = emit("win", {"win": _matmul(dproj, h1b, "tn", BF16, 512, D, S, "wgrad_win", a_map=_dproj_tile, m_out=IN_W)})
    dh1m = _matmul(dproj, w["win"], "nn", F32, S, D, 512, "dproj_to_dh1", dep=tok, a_map=_dproj_tile)

    dr1, df1, da1, db1, gp["ln1_g"], gp["ln1_b"] = _ffn_bwd_mid(dr2, dh1m, xh1, rs1, p["ln1_g"], a1, b1, w["d1"],
                                                                "ffn1_bwd_mid")
    tok = emit("small", {**gp, "loss": loss})
    tok = emit("g1", {"g1": _matmul(da1, x, "tn", BF16, 1408, D, S, "wgrad_g1", dep=tok)})
    tok = emit("u1", {"u1": _matmul(db1, x, "tn", BF16, 1408, D, S, "wgrad_u1", dep=tok)})
    tok = emit("d1", {"d1": _matmul(hm1, df1, "tn", BF16, 1408, D, S, "wgrad_d1", dep=tok)})
    dx = _ffn_bwd_dx(dr1, da1, db1, w["g1"], w["u1"], "ffn1_bwd_dx", dep=tok)
    return loss, dx, gp


_FLIPS = [(mx, my, mc) for mx in (0, 1) for my in (0, 1) for mc in (0, 1)][1:]
_GROUPS = {"ffn1": ("g1", "u1"), "ffn1d": ("d1",), "win": ("win",), "late": ("ab", "gb", "o", "g2", "u2", "d2")}
_SCATTERS = {"ffn2": ("g2", "u2", "d2"), "mix": ("ab", "gb", "o"), "win": ("win",), "g1": ("g1",), "u1": ("u1",), "d1": ("d1",)}
_TWO_STAGE = ("g1", "u1", "d1")
_HBM = pl.BlockSpec(memory_space=pltpu.HBM)
_SEM = pl.BlockSpec(memory_space=pltpu.SEMAPHORE)
_EFFECT = pltpu.SideEffectType.DATAFLOW_SIDE_EFFECTING


def _me():
    return 4 * lax.axis_index("x") + 2 * lax.axis_index("y") + lax.axis_index("c")


def _copies_start(bufs, copies, n_sem, name, after=None):
    nb = len(bufs)
    n_after = 0 if after is None else 1

    def body(*refs):
        b = refs[:nb]
        send_sems, recv_sems = refs[nb + n_after], refs[nb + n_after + 1]
        token = refs[-1]
        x, y, c = lax.axis_index("x"), lax.axis_index("y"), lax.axis_index("c")
        me = 4 * x + 2 * y + c
        for si, s_slot, di, d_slot, (mx, my, mc), sem in copies:
            s_idx, d_idx = s_slot(me), d_slot(me)
            pltpu.make_async_remote_copy(
                src_ref=b[si] if s_idx is None else b[si].at[s_idx],
                dst_ref=b[di] if d_idx is None else b[di].at[d_idx],
                send_sem=send_sems.at[sem], recv_sem=recv_sems.at[sem],
                device_id=(x ^ mx, y ^ my, c ^ mc), device_id_type=MESH).start()
        token[...] = jnp.zeros_like(token)

    ins = [pltpu.with_memory_space_constraint(a, pltpu.HBM) for a in bufs]
    outs = pl.pallas_call(
        body,
        name=name,
        in_specs=[_HBM] * nb + [pl.BlockSpec(memory_space=pl.ANY)] * n_after,
        out_specs=[_SEM, _SEM] + [_HBM] * nb + [pl.BlockSpec(memory_space=pltpu.VMEM)],
        out_shape=[pltpu.SemaphoreType.DMA((n_sem,)), pltpu.SemaphoreType.DMA((n_sem,))]
        + [pltpu.HBM(a.shape, a.dtype) for a in ins] + [jax.ShapeDtypeStruct((8, 128), F32)],
        input_output_aliases={k: 2 + k for k in range(nb)},
        compiler_params=pltpu.CompilerParams(has_side_effects=_EFFECT),
    )(*ins, *([] if after is None else [after]))
    return outs[0], outs[1], list(outs[2:2 + nb]), outs[-1]


def _copies_wait(started, waits, after, name):
    send_sems, recv_sems, bufs, _ = started
    nb = len(bufs)

    def body(*refs):
        b = refs[:nb]
        ss, rs = refs[nb], refs[nb + 1]
        me3 = (lax.axis_index("x"), lax.axis_index("y"), lax.axis_index("c"))
        for bi, n_blocks, sem, is_send in waits:
            blocks = b[bi].at[pl.ds(0, n_blocks)]
            cp = pltpu.make_async_remote_copy(src_ref=blocks, dst_ref=blocks, send_sem=ss.at[sem], recv_sem=rs.at[sem],
                                              device_id=me3, device_id_type=MESH)
            if is_send:
                cp.wait_send()
            else:
                cp.wait_recv()

    return pl.pallas_call(
        body,
        name=name,
        in_specs=[_HBM] * nb + [_SEM, _SEM, pl.BlockSpec(memory_space=pl.ANY)],
        out_specs=[_HBM] * nb,
        out_shape=[pltpu.HBM(a.shape, a.dtype) for a in bufs],
        input_output_aliases={k: k for k in range(nb)},
        compiler_params=pltpu.CompilerParams(has_side_effects=_EFFECT),
    )(*bufs, send_sems, recv_sems, after)


def _landing(own, me):
    return lax.dynamic_update_slice(lax.empty((N_DEV,) + own.shape[1:], own.dtype), own, (me, 0, 0))


_SIBLING = (0, 0, 1)
_OTHER_CHIPS = ((1, 0, 0), (0, 1, 0), (1, 1, 0))


def _bits(flip):
    return 4 * flip[0] + 2 * flip[1] + flip[2]


def _gather_send(shards, grp, after):
    nw = len(shards)
    me = _me()
    bufs = list(shards) + [_landing(a[None], me) for a in shards]
    copies = []
    for k in range(nw):
        copies.append((k, lambda me: None, nw + k, lambda me: me, _SIBLING, nw + k))
        copies += [(k, lambda me: None, nw + k, lambda me: me, f, k) for f in _OTHER_CHIPS]
    return _copies_start(bufs, copies, 2 * nw, "gather_send_" + grp, after)


def _gather_pass(started, grp, after):
    nw = len(started[2]) // 2
    waits = []
    for k in range(nw):
        waits += [(nw + k, 3, k, True), (nw + k, 1, nw + k, True), (nw + k, 3, k, False), (nw + k, 1, nw + k, False)]
    lands = list(_copies_wait(started, waits, after, "gather_arrived_" + grp)[nw:])
    copies = []
    for k in range(nw):
        for f in _OTHER_CHIPS:
            slot = functools.partial(lambda me, bits: me ^ bits, bits=_bits(f))
            copies.append((k, slot, k, slot, _SIBLING, k))
    return _copies_start(lands, copies, nw, "gather_pass_" + grp)


def _gather_finish(passed, grp, after):
    nw = len(passed[2])
    waits = [(k, 3, k, is_send) for k in range(nw) for is_send in (True, False)]
    return _copies_wait(passed, waits, passed[3] if after is None else after, "gather_done_" + grp)


def _scatter_send(parts, grp, after=None):
    nw = len(parts)
    me = _me()
    bufs = list(parts) + [_landing(lax.dynamic_index_in_dim(a, me, 0, keepdims=True), me) for a in parts]
    copies = []
    for k in range(nw):
        for f in _FLIPS:
            to = functools.partial(lambda me, bits: me ^ bits, bits=_bits(f))
            copies.append((k, to, nw + k, lambda me: me, f, k))
    return _copies_start(bufs, copies, nw, "scatter_send_" + grp, after)


def _scatter_finish(started, grp, after):
    nw = len(started[2]) // 2
    waits = [(nw + k, N_DEV - 1, k, is_send) for k in range(nw) for is_send in (True, False)]
    return _copies_wait(started, waits, after, "scatter_done_" + grp)[nw:]


N_CHIP = N_DEV // 2


def _pair_sum(part, other, key):
    _, r, c = part.shape
    tm = _ADAM_ROWS.get(r, r)
    core = lax.axis_index("c").astype(jnp.int32).reshape(1)

    def body(core_ref, a_ref, b_ref, o_ref):
        o_ref[...] = (a_ref[...].astype(F32) + b_ref[...].astype(F32)).astype(BF16)

    return pl.pallas_call(
        body,
        name="pair_sum_" + key,
        grid_spec=pltpu.PrefetchScalarGridSpec(
            num_scalar_prefetch=1,
            grid=(N_CHIP, r // tm),
            in_specs=[pl.BlockSpec((None, tm, c), lambda q, i, core_ref: (2 * q + core_ref[0], i, 0)),
                      pl.BlockSpec((None, tm, c), lambda q, i, core_ref: (q, i, 0))],
            out_specs=pl.BlockSpec((None, tm, c), lambda q, i, core_ref: (q, i, 0)),
        ),
        out_shape=jax.ShapeDtypeStruct((N_CHIP, r, c), BF16),
        compiler_params=_cp(("parallel", "parallel")),
    )(core, part, other)


def _scatter2_send(part, key, dep=None):
    me = _me()
    _, r, c = part.shape
    swap = []
    for q in range(N_CHIP):
        src = functools.partial(lambda me, q: 2 * q + 1 - me % 2, q=q)
        swap.append((0, src, 1, functools.partial(lambda me, q: q, q=q), _SIBLING, 0))
    started = _copies_start([part, lax.empty((N_CHIP, r, c), part.dtype)], swap, 1, "scatter_swap_" + key, dep)
    waits = [(1, N_CHIP, 0, True), (1, N_CHIP, 0, False)]
    part, other = _copies_wait(started, waits, started[3], "scatter_swapped_" + key)
    pair = _pair_sum(part, other, key)
    chip = me // 2
    land = lax.dynamic_update_slice(lax.empty((N_CHIP, r, c), pair.dtype),
                                    lax.dynamic_index_in_dim(pair, chip, 0, keepdims=True), (chip, 0, 0))
    copies = []
    for f in _OTHER_CHIPS:
        to = functools.partial(lambda me, bits: (me ^ bits) // 2, bits=_bits(f))
        copies.append((0, to, 1, lambda me: me // 2, f, 0))
    return _copies_start([pair, land], copies, 1, "scatter_send_" + key)


def _scatter2_finish(started, key, after):
    waits = [(1, N_CHIP - 1, 0, True), (1, N_CHIP - 1, 0, False)]
    return _copies_wait(started, waits, after, "scatter_done_" + key)[1]


def _allgather_send(block, name, after=None):
    me = _me()
    copies = [(0, lambda me: None, 1, lambda me: me, f, 0) for f in _FLIPS]
    return _copies_start([block, _landing(block[None], me)], copies, 1, name, after)


def _allgather_finish(started, name, after):
    waits = [(1, N_DEV - 1, 0, True), (1, N_DEV - 1, 0, False)]
    return _copies_wait(started, waits, after, name)[1]


_ADAM_ROWS = {352: 176, 1088: 272}


def _sum_adamw(parts, wv, m, v, name):
    n_parts, r, c = parts.shape
    tm = _ADAM_ROWS.get(r, r)
    assert r % tm == 0 and wv.shape == (r, c)

    def body(p_ref, w_ref, m_ref, v_ref, g_ref, d_ref, mo_ref, vo_ref):
        gv = p_ref[0].astype(F32)
        for j in range(1, n_parts):
            gv = gv + p_ref[j].astype(F32)
        g_ref[...] = gv
        mn = ADAM_B1 * m_ref[...] + (1.0 - ADAM_B1) * gv
        vn = ADAM_B2 * v_ref[...] + (1.0 - ADAM_B2) * (gv * gv)
        m_hat = mn / (1.0 - ADAM_B1 ** ADAM_STEP)
        v_hat = vn / (1.0 - ADAM_B2 ** ADAM_STEP)
        d_ref[...] = -ADAM_LR * (m_hat / (jnp.sqrt(v_hat) + ADAM_EPS) + ADAM_WD * w_ref[...])
        mo_ref[...] = mn
        vo_ref[...] = vn

    sp = pl.BlockSpec((tm, c), lambda i: (i, 0))
    return pl.pallas_call(
        body,
        name=name,
        grid=(r // tm,),
        in_specs=[pl.BlockSpec((n_parts, tm, c), lambda i: (0, i, 0))] + [sp] * 3,
        out_specs=[sp] * 4,
        out_shape=[jax.ShapeDtypeStruct((r, c), F32)] * 4,
        compiler_params=_cp(("parallel",), 48),
    )(parts, wv, m, v)


_WEIGHTS = ["ffn1_w_gate", "ffn1_w_up", "ffn1_w_down", "ln1_g", "ln1_b", "w_in", "b_gates", "gmlp_ln_g", "gmlp_ln_b",
            "gmlp_w_s", "gmlp_b_s", "w_attn_branch", "w_gmlp_branch", "w_out", "ln2_g", "ln2_b", "ffn2_w_gate",
            "ffn2_w_up", "ffn2_w_down", "ln3_g", "ln3_b"]
_BIG_OF = {"ffn1_w_gate": ("g1", True), "ffn1_w_up": ("u1", True), "ffn1_w_down": ("d1", False), "w_in": ("win", True),
           "w_attn_branch": ("ab", True), "w_gmlp_branch": ("gb", False), "w_out": ("o", False),
           "ffn2_w_gate": ("g2", True), "ffn2_w_up": ("u2", True), "ffn2_w_down": ("d2", False)}
_SMALL = [n for n in _WEIGHTS if n not in _BIG_OF]
_SMALL_ROWS = {"gmlp_w_s": 128, "b_gates": 2}
_SMALL_SLOT = 8


def _pack_small(d, last=None):
    rows = []
    for n in _SMALL:
        r = d[n].reshape(-1, D)
        slot = max(r.shape[0], _SMALL_SLOT)
        rows.append(jnp.pad(r, ((0, slot - r.shape[0]), (0, 0))))
    rows.append(jnp.zeros((_SMALL_SLOT, D), F32) if last is None else jnp.broadcast_to(last.reshape(1, 1), (_SMALL_SLOT, D)))
    return jnp.concatenate(rows, axis=0)


def _unpack_small(packed, shapes):
    out, at = {}, 0
    for n in _SMALL:
        k = _SMALL_ROWS.get(n, 1)
        out[n] = packed[at:at + k].reshape(shapes[n])
        at += max(k, _SMALL_SLOT)
    return out


def kernel(x, positions, ffn1_w_gate, ffn1_w_up, ffn1_w_down, ln1_g, ln1_b, w_in, b_gates, gmlp_ln_g, gmlp_ln_b, gmlp_w_s, gmlp_b_s, w_attn_branch, w_gmlp_branch, w_out, ln2_g, ln2_b, ffn2_w_gate, ffn2_w_up, ffn2_w_down, ln3_g, ln3_b, loss_target, m_ffn1_w_gate, m_ffn1_w_up, m_ffn1_w_down, m_ln1_g, m_ln1_b, m_w_in, m_b_gates, m_gmlp_ln_g, m_gmlp_ln_b, m_gmlp_w_s, m_gmlp_b_s, m_w_attn_branch, m_w_gmlp_branch, m_w_out, m_ln2_g, m_ln2_b, m_ffn2_w_gate, m_ffn2_w_up, m_ffn2_w_down, m_ln3_g, m_ln3_b, v_ffn1_w_gate, v_ffn1_w_up, v_ffn1_w_down, v_ln1_g, v_ln1_b, v_w_in, v_b_gates, v_gmlp_ln_g, v_gmlp_ln_b, v_gmlp_w_s, v_gmlp_b_s, v_w_attn_branch, v_w_gmlp_branch, v_w_out, v_ln2_g, v_ln2_b, v_ffn2_w_gate, v_ffn2_w_up, v_ffn2_w_down, v_ln3_g, v_ln3_b):
    args = dict(locals())
    wts = {n: args[n] for n in _WEIGHTS}
    ms = {n: args["m_" + n] for n in _WEIGHTS}
    vs = {n: args["v_" + n] for n in _WEIGHTS}

    name_of = {key: (n, tr) for n, (key, tr) in _BIG_OF.items()}

    started, tok = {}, None
    for grp, keys in _GROUPS.items():
        shards = []
        for key in keys:
            n, tr = name_of[key]
            s2 = wts[n][0]
            shards.append(_tie(s2.T if tr else s2, tok).astype(BF16))
        started[grp] = _gather_send(shards, grp, tok)
        tok = started[grp][3]
    all_started = tok
    passed = {}

    def get_w(grp, after, early=False):
        if grp not in passed:
            passed[grp] = _gather_pass(started[grp], grp, all_started if after is None else after)
            after = None
        if early:
            return None
        lands = _gather_finish(passed[grp], grp, after)
        return {key: g.reshape(-1, g.shape[-1]) for key, g in zip(_GROUPS[grp], lands)}

    sent = {}

    def emit(grp, grads):
        if grp == "small":
            sent[grp] = _allgather_send(_pack_small(grads, last=grads["loss"]), "small_grads_send")
        else:
            parts = [grads[key].reshape(N_DEV, -1, grads[key].shape[-1]) for key in _SCATTERS[grp]]
            sent[grp] = _scatter2_send(parts[0], grp) if grp in _TWO_STAGE else _scatter_send(parts, grp)
        return sent[grp][3]

    p = {n: (wts[n][0] if n in ("gmlp_w_s", "gmlp_b_s") else wts[n]) for n in _SMALL}
    loss, dx, gp = _local_step(x[0], positions.reshape(S, 1), loss_target[0], get_w, p, emit)
    grads, deltas, new_m, new_v = {}, {}, {}, {}
    after = dx
    for grp in ("ffn2", "mix", "win", "small", "g1", "u1", "d1"):
        if grp == "small":
            parts = _allgather_finish(sent[grp], "small_grads_done", after)
            outs = _sum_adamw(parts, *[_pack_small({n: d[n] for n in _SMALL}) for d in (wts, ms, vs)], "update_small")
            shapes = {n: wts[n].shape for n in _SMALL}
            for dst, packed in zip((grads, deltas, new_m, new_v), outs):
                dst.update(_unpack_small(packed, shapes))
            loss = outs[0][-_SMALL_SLOT, 0]
            after = outs[1]
            continue
        arrived = ([_scatter2_finish(sent[grp], grp, after)] if grp in _TWO_STAGE
                   else _scatter_finish(sent[grp], grp, after))
        for key, part in zip(_SCATTERS[grp], arrived):
            n, tr = name_of[key]
            outs = _sum_adamw(part, *[(d[n][0].T if tr else d[n][0]) for d in (wts, ms, vs)], "update_" + key)
            for dst, o in zip((grads, deltas, new_m, new_v), outs):
                dst[n] = (o.T if tr else o)[None]
            after = outs[1]

    return (loss, dx[None], *[grads[n] for n in _WEIGHTS], *[deltas[n] for n in _WEIGHTS],
            *[new_m[n] for n in _WEIGHTS], *[new_v[n] for n in _WEIGHTS])
```

```python
import functools
import math

import jax
import jax.numpy as jnp
from jax import lax
from jax.experimental import pallas as pl
from jax.experimental.pallas import tpu as pltpu

F32 = jnp.float32
BF16 = jnp.bfloat16

N_DEV = 8
D = 1024
S = 2048
F = 2816
HEAD_DIM = 64
HEADS = 8
GW = HEADS * HEAD_DIM
PATTERN_DILATIONS = (1, 4, 16)
BLK = 128
QKV_W = 3 * 3 * GW
IN_W = QKV_W + 2 * D + 2 * D
DPROJ_W = 5 * 2 * D
DPROJ_Z_COL, DPROJ_G_COL = 3, 4


def _dproj_tile(t):
    three = jnp.int32(3)
    return jnp.where(t < 9, lax.rem(t, three) * 3 + lax.div(t, three), t + 3)
ROPE_THETA = 10000.0
ALPHA = 2.0 ** 0.25
LN_EPS = 1e-5
ADAM_LR, ADAM_B1, ADAM_B2, ADAM_EPS, ADAM_WD, ADAM_STEP = 0.001, 0.9, 0.999, 1e-08, 0.01, 10
NEG = -1e30
MESH = pl.DeviceIdType.MESH


def _cp(sem=None, vmem_mb=None):
    kw = {}
    if sem is not None:
        kw["dimension_semantics"] = sem
    if vmem_mb is not None:
        kw["vmem_limit_bytes"] = vmem_mb << 20
    return pltpu.CompilerParams(**kw)


def _dot_nn(a, b):
    return lax.dot_general(a, b, (((1,), (0,)), ((), ())), preferred_element_type=F32)


def _dot_nt(a, b):
    return lax.dot_general(a, b, (((1,), (1,)), ((), ())), preferred_element_type=F32)


def _dot_tn(a, b):
    return lax.dot_general(a, b, (((0,), (0,)), ((), ())), preferred_element_type=F32)


def _ln_fwd(r, g, b):
    mu = jnp.mean(r, axis=-1, keepdims=True)
    xc = r - mu
    var = jnp.mean(xc * xc, axis=-1, keepdims=True)
    rstd = lax.rsqrt(var + LN_EPS)
    xhat = xc * rstd
    return xhat * g + b, xhat, rstd


def _ln_bwd(dh, xhat, rstd, g):
    dxh = dh * g
    m1 = jnp.mean(dxh, axis=-1, keepdims=True)
    m2 = jnp.mean(dxh * xhat, axis=-1, keepdims=True)
    return rstd * (dxh - m1 - xhat * m2)


def _sigmoid(x):
    return 0.5 * jnp.tanh(0.5 * x) + 0.5


def _colsum(x):
    return jnp.sum(x, axis=0, keepdims=True)


def _matmul(a, b, mode, out_dtype, tm, tn, tk, name, b_off=0, n_out=None, dep=None, a_map=None, m_out=None):
    n_dep = 0 if dep is None else 1
    a_map = a_map or (lambda t: t)
    if mode == "nn":
        m, k = a.shape[0], b.shape[0]
        n = b.shape[1]
    elif mode == "nt":
        m, k = a.shape
        n = n_out if n_out is not None else b.shape[0]
    else:
        k, m = a.shape[0], m_out or a.shape[1]
        n = b.shape[1]
    nk = k // tk
    assert m % tm == 0 and n % tn == 0 and k % tk == 0
    dot = {"nn": _dot_nn, "nt": _dot_nt, "tn": _dot_tn}[mode]

    def body(a_ref, b_ref, *rest):
        o_ref, scr = rest[n_dep], rest[n_dep + 1:]
        r = dot(a_ref[...].astype(BF16), b_ref[...].astype(BF16))
        if nk == 1:
            o_ref[...] = r.astype(out_dtype)
        else:
            acc = scr[0]
            kk = pl.program_id(2)

            @pl.when(kk == 0)
            def _():
                acc[...] = r

            @pl.when(kk > 0)
            def _():
                acc[...] += r

            @pl.when(kk == nk - 1)
            def _():
                o_ref[...] = acc[...].astype(out_dtype)

    if mode == "nn":
        a_spec = pl.BlockSpec((tm, tk), lambda i, j, kk: (i, a_map(kk)))
        b_spec = pl.BlockSpec((tk, tn), lambda i, j, kk: (kk, j))
    elif mode == "nt":
        a_spec = pl.BlockSpec((tm, tk), lambda i, j, kk: (i, kk))
        b_spec = pl.BlockSpec((tn, tk), lambda i, j, kk: (j + b_off, kk))
    else:
        a_spec = pl.BlockSpec((tk, tm), lambda i, j, kk: (kk, a_map(i)))
        b_spec = pl.BlockSpec((tk, tn), lambda i, j, kk: (kk, j))
    return pl.pallas_call(
        body,
        name=name,
        grid=(m // tm, n // tn, nk),
        in_specs=[a_spec, b_spec] + [pl.BlockSpec(memory_space=pl.ANY)] * n_dep,
        out_specs=pl.BlockSpec((tm, tn), lambda i, j, kk: (i, j)),
        out_shape=jax.ShapeDtypeStruct((m, n), out_dtype),
        scratch_shapes=[] if nk == 1 else [pltpu.VMEM((tm, tn), F32)],
        compiler_params=_cp(("parallel", "parallel", "arbitrary"), 56),
    )(a, b, *([] if dep is None else [dep]))


def _ffn_up(x, wgt, wut, name, tm=512, tn=1408, dep=None):
    n_dep = 0 if dep is None else 1

    def body(x_ref, wg_ref, wu_ref, *rest):
        ga_ref, gb_ref, hm_ref = rest[n_dep:]
        xb = x_ref[...].astype(BF16)
        a = _dot_nt(xb, wg_ref[...])
        b = _dot_nt(xb, wu_ref[...])
        sig = _sigmoid(a)
        silu = a * sig
        ga_ref[...] = (b * (sig + silu * (1.0 - sig))).astype(BF16)
        gb_ref[...] = silu.astype(BF16)
        hm_ref[...] = (silu * b).astype(BF16)

    wsp = pl.BlockSpec((tn, D), lambda i, j: (j, 0))
    mid = pl.BlockSpec((tm, tn), lambda i, j: (i, j))
    return pl.pallas_call(
        body,
        name=name,
        grid=(S // tm, F // tn),
        in_specs=[pl.BlockSpec((tm, D), lambda i, j: (i, 0)), wsp, wsp] + [pl.BlockSpec(memory_space=pl.ANY)] * n_dep,
        out_specs=[mid, mid, mid],
        out_shape=[jax.ShapeDtypeStruct((S, F), BF16)] * 3,
        compiler_params=_cp(("parallel", "arbitrary"), 56),
    )(x, wgt, wut, *([] if dep is None else [dep]))


def _ffn_down(hm, wd, x, ln_g, ln_b, name, target=None, tm=512):
    head = target is not None

    def body(hm_ref, wd_ref, x_ref, g_ref, b_ref, *rest):
        t_ref = rest[0] if head else None
        o1_ref, o2_ref, xh_ref, rs_ref = rest[1 if head else 0:]
        r = ALPHA * x_ref[...] + 0.5 * _dot_nn(hm_ref[...], wd_ref[...])
        h, xhat, rstd = _ln_fwd(r, g_ref[...], b_ref[...])
        xh_ref[...] = xhat
        rs_ref[...] = rstd
        if head:
            e = h - t_ref[...]
            o1_ref[...] = e * (1.0 / D)
            part = jnp.sum(_colsum(e * e), axis=1, keepdims=True) * (0.5 / D)

            @pl.when(pl.program_id(0) == 0)
            def _():
                o2_ref[...] = jnp.zeros_like(o2_ref)

            o2_ref[...] += part
        else:
            o1_ref[...] = h
            o2_ref[...] = h.astype(BF16)

    row = pl.BlockSpec((tm, D), lambda i: (i, 0))
    vec = pl.BlockSpec((1, D), lambda i: (0, 0))
    second = (pl.BlockSpec((1, 1), lambda i: (0, 0)), jax.ShapeDtypeStruct((1, 1), F32)) if head else (
        row, jax.ShapeDtypeStruct((S, D), BF16))
    return pl.pallas_call(
        body,
        name=name,
        grid=(S // tm,),
        in_specs=[pl.BlockSpec((tm, F), lambda i: (i, 0)), pl.BlockSpec((F, D), lambda i: (0, 0)), row, vec, vec]
        + ([row] if head else []),
        out_specs=[row, second[0], row, pl.BlockSpec((tm, 1), lambda i: (i, 0))],
        out_shape=[
            jax.ShapeDtypeStruct((S, D), F32),
            second[1],
            jax.ShapeDtypeStruct((S, D), F32),
            jax.ShapeDtypeStruct((S, 1), F32),
        ],
        compiler_params=_cp(("arbitrary",), 56),
    )(hm, wd, x, ln_g, ln_b, *([target] if head else []))


def _ffn_bwd_mid(dh_a, dh_b, xhat, rstd, ln_g, a, b, wd, name, tm=512, tn=1408):
    two = dh_b is not None

    def body(*refs):
        dha_ref = refs[0]
        dhb_ref = refs[1] if two else None
        (xh_ref, rs_ref, g_ref, a_ref, b_ref, wd_ref,
         dr_ref, df_ref, da_ref, db_ref, dg_ref, dbias_ref, df_scr) = refs[2 if two else 1:]
        i = pl.program_id(0)
        j = pl.program_id(1)

        @pl.when(j == 0)
        def _():
            dh = dha_ref[...]
            if two:
                dh = ALPHA * dh + dhb_ref[...]
            xhat = xh_ref[...]
            dr = _ln_bwd(dh, xhat, rs_ref[...], g_ref[...])
            dfb = (0.5 * dr).astype(BF16)
            dr_ref[...] = dr
            df_scr[...] = dfb
            df_ref[...] = dfb
            sg = _colsum(dh * xhat)
            sb = _colsum(dh)

            @pl.when(i == 0)
            def _():
                dg_ref[...] = sg
                dbias_ref[...] = sb

            @pl.when(i > 0)
            def _():
                dg_ref[...] += sg
                dbias_ref[...] += sb

        dhm = _dot_nt(df_scr[...], wd_ref[...])
        da_ref[...] = (dhm * a_ref[...].astype(F32)).astype(BF16)
        db_ref[...] = (dhm * b_ref[...].astype(F32)).astype(BF16)

    row = pl.BlockSpec((tm, D), lambda i, j: (i, 0))
    vec = pl.BlockSpec((1, D), lambda i, j: (0, 0))
    mid = pl.BlockSpec((tm, tn), lambda i, j: (i, j))
    ins = [dh_a] + ([dh_b] if two else []) + [xhat, rstd, ln_g, a, b, wd]
    in_specs = [row] * (2 if two else 1) + [row, pl.BlockSpec((tm, 1), lambda i, j: (i, 0)), vec, mid, mid,
                                            pl.BlockSpec((tn, D), lambda i, j: (j, 0))]
    return pl.pallas_call(
        body,
        name=name,
        grid=(S // tm, F // tn),
        in_specs=in_specs,
        out_specs=[row, row, mid, mid, vec, vec],
        out_shape=[
            jax.ShapeDtypeStruct((S, D), F32),
            jax.ShapeDtypeStruct((S, D), BF16),
            jax.ShapeDtypeStruct((S, F), BF16),
            jax.ShapeDtypeStruct((S, F), BF16),
            jax.ShapeDtypeStruct((1, D), F32),
            jax.ShapeDtypeStruct((1, D), F32),
        ],
        scratch_shapes=[pltpu.VMEM((tm, D), BF16)],
        compiler_params=_cp(("arbitrary", "arbitrary"), 56),
    )(*ins)


def _ffn_bwd_dx(dr, da, db, wgt, wut, name, tm=512, tk=1408, dep=None):
    nk = F // tk
    n_dep = 0 if dep is None else 1

    def body(dr_ref, da_ref, db_ref, wg_ref, wu_ref, *rest):
        dx_ref, acc = rest[n_dep], rest[n_dep + 1]
        kk = pl.program_id(1)
        part = _dot_nn(da_ref[...], wg_ref[...]) + _dot_nn(db_ref[...], wu_ref[...])

        @pl.when(kk == 0)
        def _():
            acc[...] = ALPHA * dr_ref[...] + part

        @pl.when(kk > 0)
        def _():
            acc[...] += part

        @pl.when(kk == nk - 1)
        def _():
            dx_ref[...] = acc[...]

    row = pl.BlockSpec((tm, D), lambda i, kk: (i, 0))
    mid = pl.BlockSpec((tm, tk), lambda i, kk: (i, kk))
    wsp = pl.BlockSpec((tk, D), lambda i, kk: (kk, 0))
    return pl.pallas_call(
        body,
        name=name,
        grid=(S // tm, nk),
        in_specs=[row, mid, mid, wsp, wsp] + [pl.BlockSpec(memory_space=pl.ANY)] * n_dep,
        out_specs=row,
        out_shape=jax.ShapeDtypeStruct((S, D), F32),
        scratch_shapes=[pltpu.VMEM((tm, D), F32)],
        compiler_params=_cp(("parallel", "arbitrary"), 56),
    )(dr, da, db, wgt, wut, *([] if dep is None else [dep]))


def _rope_tables(pos_col, invf, sign, tm=512):
    def body(p_ref, f_ref, s_ref, c_out, s_out):
        ang = p_ref[...].astype(F32) * f_ref[...]
        c_out[...] = jnp.cos(ang)
        s_out[...] = jnp.sin(ang) * s_ref[...]

    vec = pl.BlockSpec((1, BLK), lambda i: (0, 0))
    out = pl.BlockSpec((tm, BLK), lambda i: (i, 0))
    return pl.pallas_call(
        body,
        name="rope_tables",
        grid=(S // tm,),
        in_specs=[pl.BlockSpec((tm, 1), lambda i: (i, 0)), vec, vec],
        out_specs=[out, out],
        out_shape=[jax.ShapeDtypeStruct((S, BLK), F32)] * 2,
        compiler_params=_cp(("parallel",)),
    )(pos_col, invf, sign)


def _lane_lo(rows=BLK):
    return (lax.broadcasted_iota(jnp.int32, (rows, GW), 1) % HEAD_DIM) < (HEAD_DIM // 2)


def _swap_halves(t, lo):
    return jnp.where(lo, pltpu.roll(t, GW - HEAD_DIM // 2, 1), pltpu.roll(t, HEAD_DIM // 2, 1))


def _rope(t, cosf, sinf, lo):
    return t * cosf + _swap_halves(t, lo) * sinf


def _unrope(g, cosf, sinf, lo):
    return g * cosf + _swap_halves(g * sinf, lo)


def _tile4(v):
    return jnp.concatenate([v, v, v, v], axis=1)


def _band_mask(n):
    qi = lax.broadcasted_iota(jnp.int32, (BLK, 2 * BLK), 0)
    kj = lax.broadcasted_iota(jnp.int32, (BLK, 2 * BLK), 1)
    dist = qi + BLK - kj
    return (dist >= 0) & (dist <= BLK) & ((kj >= BLK) | (n >= 1))


ROWS = 256
LANES = 128


def _to_view(scr, y, dst_ref, d, dtype, col0=0):
    if d == 1:
        dst_ref[:, col0:col0 + GW] = y.astype(dtype)
        return
    for cb in range(GW // LANES):
        scr[cb][...] = y[:, cb * LANES:(cb + 1) * LANES]
    for r in range(d):
        for cb in range(GW // LANES):
            at = col0 + r * GW + cb * LANES
            dst_ref[:, at:at + LANES] = scr[cb][pl.ds(r, ROWS // d, stride=d), :].astype(dtype)


def _from_view(scr, src_ref, d):
    if d == 1:
        return src_ref[...].astype(F32)
    for r in range(d):
        for cb in range(GW // LANES):
            at = r * GW + cb * LANES
            scr[cb][pl.ds(r, ROWS // d, stride=d), :] = src_ref[:, at:at + LANES].astype(F32)
    return jnp.concatenate([scr[cb][...] for cb in range(GW // LANES)], axis=1)


def _view_spec(d):
    return pl.BlockSpec((ROWS // d, d * GW), lambda i: (i, 0))


_VIEW_SCRATCH = [pltpu.VMEM((ROWS, LANES), F32)] * (GW // LANES)


def _qkv_prep(qkv, cos_t, sin_t):
    def body(x_ref, c_ref, s_ref, *rest):
        outs, scr = rest[:9], rest[9:]
        lo = _lane_lo(ROWS)
        cf, sf = _tile4(c_ref[...]), _tile4(s_ref[...])
        for gi, d in enumerate(PATTERN_DILATIONS):
            q, k, v = (x_ref[:, (3 * part + gi) * GW:(3 * part + gi + 1) * GW].astype(F32) for part in range(3))
            _to_view(scr, _rope(q, cf, sf, lo) * (HEAD_DIM ** -0.5), outs[3 * gi], d, BF16)
            _to_view(scr, _rope(k, cf, sf, lo), outs[3 * gi + 1], d, BF16)
            _to_view(scr, v, outs[3 * gi + 2], d, BF16)

    tab = pl.BlockSpec((ROWS, BLK), lambda i: (i, 0))
    outs = pl.pallas_call(
        body,
        name="qkv_prep",
        grid=(S // ROWS,),
        in_specs=[pl.BlockSpec((ROWS, QKV_W), lambda i: (i, 0)), tab, tab],
        out_specs=[_view_spec(d) for d in PATTERN_DILATIONS for _ in range(3)],
        out_shape=[jax.ShapeDtypeStruct((S // d, d * GW), BF16) for d in PATTERN_DILATIONS for _ in range(3)],
        scratch_shapes=_VIEW_SCRATCH,
        compiler_params=_cp(("parallel",), 48),
    )(qkv, cos_t, sin_t)
    return [tuple(outs[3 * gi:3 * gi + 3]) for gi in range(3)]


N_ATTN_STEPS = S // BLK


def _class_and_block(gi, t):
    per_class = N_ATTN_STEPS // PATTERN_DILATIONS[gi]
    return t >> (per_class.bit_length() - 1), t & (per_class - 1)


def _attn_specs(gi):
    def cur(t):
        r, n = _class_and_block(gi, t)
        return n, r

    def prev(t):
        r, n = _class_and_block(gi, t)
        return jnp.maximum(n - 1, 0), r

    def whole(t):
        return 0, _class_and_block(gi, t)[0]

    sub = S // PATTERN_DILATIONS[gi]
    return (pl.BlockSpec((BLK, GW), cur), pl.BlockSpec((BLK, GW), prev), pl.BlockSpec((sub, GW), whole))


def _left_lanes():
    return lax.broadcasted_iota(jnp.int32, (BLK, LANES), 1) < HEAD_DIM


def _stack_heads(t2, left):
    zero = jnp.zeros_like(t2)
    return jnp.concatenate([jnp.where(left, t2, zero), jnp.where(left, zero, t2)], axis=0)


def _attn_fwd_block(n, q_ref, kc_ref, kp_ref, vc_ref, vp_ref, o_ref, l_ref):
    mask = _band_mask(n)
    left = _left_lanes()
    zero = jnp.zeros((BLK, LANES), BF16)
    for pr in range(GW // LANES):
        cs = slice(pr * LANES, (pr + 1) * LANES)
        q2 = q_ref[:, cs]
        k2 = jnp.concatenate([kp_ref[:, cs], kc_ref[:, cs]], axis=0)
        v2 = jnp.concatenate([vp_ref[:, cs], vc_ref[:, cs]], axis=0)
        o_h, lse_h = [], []
        for side in (left, ~left):
            s = jnp.where(mask, _dot_nt(jnp.where(side, q2, zero), k2), NEG)
            m = jnp.max(s, axis=1, keepdims=True)
            p = jnp.exp(s - m)
            l = jnp.sum(p, axis=1, keepdims=True)
            o_h.append(_dot_nn((p / l).astype(BF16), v2))
            lse_h.append(m + jnp.log(l))
        o_ref[:, cs] = jnp.where(left, o_h[0], o_h[1])
        l_ref[:, cs] = jnp.where(left, lse_h[0], lse_h[1])


def _attn_fwd(preps):
    def body(*refs):
        t = pl.program_id(0)
        for gi in range(3):
            _attn_fwd_block(_class_and_block(gi, t)[1], *refs[5 * gi:5 * gi + 5], *refs[15 + 2 * gi:17 + 2 * gi])

    in_specs, out_specs, out_shape, args = [], [], [], []
    for gi, d in enumerate(PATTERN_DILATIONS):
        cur, prev, _ = _attn_specs(gi)
        qr, kr, vv = preps[gi]
        in_specs += [cur, cur, prev, cur, prev]
        args += [qr, kr, kr, vv, vv]
        out_specs += [cur, cur]
        out_shape += [jax.ShapeDtypeStruct((S // d, d * GW), F32)] * 2
    outs = pl.pallas_call(
        body,
        name="attn_fwd",
        grid=(N_ATTN_STEPS,),
        in_specs=in_specs,
        out_specs=out_specs,
        out_shape=out_shape,
        compiler_params=_cp(("arbitrary",), 48),
    )(*args)
    return [outs[0], outs[2], outs[4]], [outs[1], outs[3], outs[5]]


def _attn_bwd_block(n, q_ref, kc_ref, kp_ref, vc_ref, vp_ref, do_ref, l_ref, c_ref, dq_ref, dk_ref, dv_ref):
    mask = _band_mask(n)
    left = _left_lanes()
    left2 = jnp.concatenate([left, left], axis=0)
    here = pl.ds(pl.multiple_of(n * BLK, BLK), BLK)
    before = pl.ds(pl.multiple_of(jnp.maximum(n - 1, 0) * BLK, BLK), BLK)
    for pr in range(GW // LANES):
        cs = slice(pr * LANES, (pr + 1) * LANES)
        q2 = q_ref[:, cs]
        do2 = do_ref[:, cs]
        k_bd = _stack_heads(jnp.concatenate([kp_ref[:, cs], kc_ref[:, cs]], axis=0), left2)
        v_bd = _stack_heads(jnp.concatenate([vp_ref[:, cs], vc_ref[:, cs]], axis=0), left2)
        s2 = _dot_nt(q2, k_bd)
        dp2 = _dot_nt(do2, v_bd)
        ps, dss = [], []
        for h in range(2):
            at = pr * LANES + h * HEAD_DIM
            half = slice(h * 2 * BLK, (h + 1) * 2 * BLK)
            p = jnp.exp(jnp.where(mask, s2[:, half], NEG) - l_ref[:, at:at + 1])
            ps.append(p.astype(BF16))
            dss.append((p * (dp2[:, half] - c_ref[:, at:at + 1])).astype(BF16))
        ds_both = jnp.concatenate(dss, axis=1)
        dq_ref[:, cs] = _dot_nn(ds_both, k_bd) * (HEAD_DIM ** -0.5)
        dk_bd = _dot_tn(ds_both, q2)
        dv_bd = _dot_tn(jnp.concatenate(ps, axis=1), do2)
        dk2 = jnp.where(left2, dk_bd[:2 * BLK], dk_bd[2 * BLK:])
        dv2 = jnp.where(left2, dv_bd[:2 * BLK], dv_bd[2 * BLK:])
        dk_ref[here, cs] = dk2[BLK:]
        dv_ref[here, cs] = dv2[BLK:]
        dk_ref[before, cs] += dk2[:BLK]
        dv_ref[before, cs] += dv2[:BLK]


def _attn_bwd(preps, dos, lses, cterms):
    def body(*refs):
        t = pl.program_id(0)
        for gi in range(3):
            _attn_bwd_block(_class_and_block(gi, t)[1], *refs[8 * gi:8 * gi + 8], *refs[24 + 3 * gi:27 + 3 * gi])

    in_specs, out_specs, out_shape, args = [], [], [], []
    for gi, d in enumerate(PATTERN_DILATIONS):
        cur, prev, whole = _attn_specs(gi)
        qr, kr, vv = preps[gi]
        in_specs += [cur, cur, prev, cur, prev, cur, cur, cur]
        args += [qr, kr, kr, vv, vv, dos[gi], lses[gi], cterms[gi]]
        out_specs += [cur, whole, whole]
        out_shape += [jax.ShapeDtypeStruct((S // d, d * GW), F32)] * 3
    outs = pl.pallas_call(
        body,
        name="attn_bwd",
        grid=(N_ATTN_STEPS,),
        in_specs=in_specs,
        out_specs=out_specs,
        out_shape=out_shape,
        compiler_params=_cp(("arbitrary",), 56),
    )(*args)
    return [tuple(outs[3 * gi:3 * gi + 3]) for gi in range(3)]


def _qkv_unprep(grads, cos_t, sin_t, dproj):
    def body(*refs):
        views, (c_ref, s_ref, _, out_ref), scr = refs[:9], refs[9:13], refs[13:]
        lo = _lane_lo(ROWS)
        cf, sf = _tile4(c_ref[...]), _tile4(s_ref[...])
        for gi, d in enumerate(PATTERN_DILATIONS):
            dq_ref, dk_ref, dv_ref = views[3 * gi:3 * gi + 3]
            at = 3 * gi * GW
            out_ref[:, at:at + GW] = _unrope(_from_view(scr, dq_ref, d), cf, sf, lo).astype(BF16)
            out_ref[:, at + GW:at + 2 * GW] = _unrope(_from_view(scr, dk_ref, d), cf, sf, lo).astype(BF16)
            out_ref[:, at + 2 * GW:at + 3 * GW] = _from_view(scr, dv_ref, d).astype(BF16)

    tab = pl.BlockSpec((ROWS, BLK), lambda i: (i, 0))
    return pl.pallas_call(
        body,
        name="qkv_unprep",
        grid=(S // ROWS,),
        in_specs=[_view_spec(d) for d in PATTERN_DILATIONS for _ in range(3)] + [tab, tab, pl.BlockSpec(memory_space=pl.ANY)],
        out_specs=pl.BlockSpec((ROWS, QKV_W), lambda i: (i, 0)),
        out_shape=jax.ShapeDtypeStruct(dproj.shape, dproj.dtype),
        input_output_aliases={11: 0},
        scratch_shapes=_VIEW_SCRATCH,
        compiler_params=_cp(("parallel",), 48),
    )(*[g for grp in grads for g in grp], cos_t, sin_t, dproj)


def _group_weights(l0, l1, l2):
    mx = jnp.maximum(jnp.maximum(l0, l1), l2)
    e0, e1, e2 = jnp.exp(l0 - mx), jnp.exp(l1 - mx), jnp.exp(l2 - mx)
    inv = 1.0 / (e0 + e1 + e2)
    return e0 * inv, e1 * inv, e2 * inv


def _combine_fwd(os_, lses):
    def body(o0, o1, o2, l0, l1, l2, y_ref, *scr):
        ov = [_from_view(scr, o, d) for o, d in zip((o0, o1, o2), PATTERN_DILATIONS)]
        lv = [_from_view(scr, l, d) for l, d in zip((l0, l1, l2), PATTERN_DILATIONS)]
        w0, w1, w2 = _group_weights(*lv)
        y_ref[...] = (w0 * ov[0] + w1 * ov[1] + w2 * ov[2]).astype(BF16)

    views = [_view_spec(d) for d in PATTERN_DILATIONS]
    return pl.pallas_call(
        body,
        name="attn_combine_fwd",
        grid=(S // ROWS,),
        in_specs=views + views,
        out_specs=pl.BlockSpec((ROWS, GW), lambda i: (i, 0)),
        out_shape=jax.ShapeDtypeStruct((S, GW), BF16),
        scratch_shapes=_VIEW_SCRATCH,
        compiler_params=_cp(("parallel",)),
    )(*os_, *lses)


def _combine_bwd(dy, os_, lses, seg):
    def body(dy_ref, o0, o1, o2, l0, l1, l2, seg_ref, d0, d1, d2, c0, c1, c2, *scr):
        ov = [_from_view(scr, o, d) for o, d in zip((o0, o1, o2), PATTERN_DILATIONS)]
        lv = [_from_view(scr, l, d) for l, d in zip((l0, l1, l2), PATTERN_DILATIONS)]
        ws = _group_weights(*lv)
        dyv = dy_ref[...]
        t = dyv * (ws[0] * ov[0] + ws[1] * ov[1] + ws[2] * ov[2])
        t_hi = t.astype(BF16)
        r1 = t - t_hi.astype(F32)
        t_mid = r1.astype(BF16)
        t_lo = (r1 - t_mid.astype(F32)).astype(BF16)
        sg = seg_ref[...]
        e = _dot_nn(t_hi, sg) + _dot_nn(t_mid, sg) + _dot_nn(t_lo, sg)
        for w, do_ref, c_ref, d in zip(ws, (d0, d1, d2), (c0, c1, c2), PATTERN_DILATIONS):
            _to_view(scr, w * dyv, do_ref, d, BF16)
            _to_view(scr, w * e, c_ref, d, F32)

    views = [_view_spec(d) for d in PATTERN_DILATIONS]
    return pl.pallas_call(
        body,
        name="attn_combine_bwd",
        grid=(S // ROWS,),
        in_specs=[pl.BlockSpec((ROWS, GW), lambda i: (i, 0))] + views + views + [pl.BlockSpec((GW, GW), lambda i: (0, 0))],
        out_specs=views + views,
        out_shape=[jax.ShapeDtypeStruct((S // d, d * GW), BF16) for d in PATTERN_DILATIONS]
        + [jax.ShapeDtypeStruct((S // d, d * GW), F32) for d in PATTERN_DILATIONS],
        scratch_shapes=_VIEW_SCRATCH,
        compiler_params=_cp(("parallel",)),
    )(dy, *os_, *lses, seg)


_SQRT_HALF = 0.7071067811865476
_INV_SQRT_2PI = 0.3989422804014327


def _gelu(z):
    return 0.5 * z * (1.0 + lax.erf(z * _SQRT_HALF))


def _gelu_grad(z):
    return 0.5 * (1.0 + lax.erf(z * _SQRT_HALF)) + z * (_INV_SQRT_2PI * jnp.exp(-0.5 * z * z))


def _tril_ws(ws_ref, g):
    t = lax.broadcasted_iota(jnp.int32, (BLK, BLK), 0)
    s = lax.broadcasted_iota(jnp.int32, (BLK, BLK), 1)
    return jnp.where(t >= s, ws_ref[g], 0.0)


def _chunks_side_by_side(x, cols, nch):
    return jnp.concatenate([x[c * BLK:(c + 1) * BLK, cols] for c in range(nch)], axis=1)


def _gmlp_fwd(z, ws, bst, ln_g, ln_b, tm=512):
    nch = tm // BLK

    def body(z_ref, ws_ref, b_ref, g_ref, be_ref, y_ref):
        zg = _gelu(z_ref[...].astype(F32))
        u = zg[:, :D]
        vn, _, _ = _ln_fwd(zg[:, D:], g_ref[...], be_ref[...])
        vnb = vn.astype(BF16)
        bt = b_ref[...]
        for g in range(8):
            w = _tril_ws(ws_ref, g).astype(BF16)
            cols = slice(g * BLK, (g + 1) * BLK)
            mixed = _dot_nn(w, _chunks_side_by_side(vnb, cols, nch)) + bt[:, g:g + 1]
            for c in range(nch):
                rows = slice(c * BLK, (c + 1) * BLK)
                y_ref[rows, cols] = (u[rows, cols] * mixed[:, c * BLK:(c + 1) * BLK]).astype(BF16)

    return pl.pallas_call(
        body,
        name="gmlp_fwd",
        grid=(S // tm,),
        in_specs=[
            pl.BlockSpec((tm, 2 * D), lambda i: (i, 0)),
            pl.BlockSpec((8, BLK, BLK), lambda i: (0, 0, 0)),
            pl.BlockSpec((BLK, 8), lambda i: (0, 0)),
            pl.BlockSpec((1, D), lambda i: (0, 0)),
            pl.BlockSpec((1, D), lambda i: (0, 0)),
        ],
        out_specs=pl.BlockSpec((tm, D), lambda i: (i, 0)),
        out_shape=jax.ShapeDtypeStruct((S, D), BF16),
        compiler_params=_cp(("parallel",), 48),
    )(z, ws, bst, ln_g, ln_b)


def _gmlp_bwd(z, dy, ws, bst, ln_g, ln_b, dproj, tm=512):
    nch = tm // BLK

    def body(z_ref, dy_ref, ws_ref, b_ref, g_ref, be_ref, _, dz_ref, dws_ref, dbs_ref, dg_ref, dbe_ref, dvn_scr, dm_acc):
        i = pl.program_id(0)
        zv = z_ref[...].astype(F32)
        zg = _gelu(zv)
        u = zg[:, :D]
        gam = g_ref[...]
        vn, xhat, rstd = _ln_fwd(zg[:, D:], gam, be_ref[...])
        vnb = vn.astype(BF16)
        dyv = dy_ref[...]
        dmix = dyv * u
        dmb = dmix.astype(BF16)
        bt = b_ref[...]
        tmask = lax.broadcasted_iota(jnp.int32, (BLK, BLK), 0) >= lax.broadcasted_iota(jnp.int32, (BLK, BLK), 1)
        dm_sum = dmix[0:BLK]
        for c in range(1, nch):
            dm_sum = dm_sum + dmix[c * BLK:(c + 1) * BLK]

        @pl.when(i == 0)
        def _():
            dm_acc[...] = jnp.zeros_like(dm_acc)
            dws_ref[...] = jnp.zeros_like(dws_ref)
            dg_ref[...] = jnp.zeros_like(dg_ref)
            dbe_ref[...] = jnp.zeros_like(dbe_ref)

        dm_acc[...] += dm_sum
        dus = []
        for g in range(8):
            w = _tril_ws(ws_ref, g).astype(BF16)
            cols = slice(g * BLK, (g + 1) * BLK)
            v_cat = _chunks_side_by_side(vnb, cols, nch)
            dm_cat = _chunks_side_by_side(dmb, cols, nch)
            mixed = _dot_nn(w, v_cat) + bt[:, g:g + 1]
            dus.append(jnp.concatenate(
                [dyv[c * BLK:(c + 1) * BLK, cols] * mixed[:, c * BLK:(c + 1) * BLK] for c in range(nch)], axis=0))
            dws_ref[g] += jnp.where(tmask, _dot_nt(dm_cat, v_cat), 0.0)
            dvn_cat = _dot_tn(w, dm_cat)
            for c in range(nch):
                dvn_scr[c * BLK:(c + 1) * BLK, cols] = dvn_cat[:, c * BLK:(c + 1) * BLK]

        dvn = dvn_scr[...]
        dg_ref[...] += _colsum(dvn * xhat)
        dbe_ref[...] += _colsum(dvn)
        dvg = _ln_bwd(dvn, xhat, rstd, gam)
        gp = _gelu_grad(zv)
        dz_ref[:, :D] = (jnp.concatenate(dus, axis=1) * gp[:, :D]).astype(BF16)
        dz_ref[:, D:] = (dvg * gp[:, D:]).astype(BF16)

        @pl.when(i == S // tm - 1)
        def _():
            acc = dm_acc[...]
            for g in range(8):
                dbs_ref[:, g:g + 1] = jnp.sum(acc[:, g * BLK:(g + 1) * BLK], axis=1, keepdims=True)

    vec = pl.BlockSpec((1, D), lambda i: (0, 0))
    return pl.pallas_call(
        body,
        name="gmlp_bwd",
        grid=(S // tm,),
        in_specs=[
            pl.BlockSpec((tm, 2 * D), lambda i: (i, 0)),
            pl.BlockSpec((tm, D), lambda i: (i, 0)),
            pl.BlockSpec((8, BLK, BLK), lambda i: (0, 0, 0)),
            pl.BlockSpec((BLK, 8), lambda i: (0, 0)),
            vec,
            vec,
            pl.BlockSpec(memory_space=pl.ANY),
        ],
        out_specs=[
            pl.BlockSpec((tm, 2 * D), lambda i: (i, DPROJ_Z_COL)),
            pl.BlockSpec((8, BLK, BLK), lambda i: (0, 0, 0)),
            pl.BlockSpec((BLK, 8), lambda i: (0, 0)),
            vec,
            vec,
        ],
        out_shape=[
            jax.ShapeDtypeStruct(dproj.shape, dproj.dtype),
            jax.ShapeDtypeStruct((8, BLK, BLK), F32),
            jax.ShapeDtypeStruct((BLK, 8), F32),
            jax.ShapeDtypeStruct((1, D), F32),
            jax.ShapeDtypeStruct((1, D), F32),
        ],
        input_output_aliases={6: 0},
        scratch_shapes=[pltpu.VMEM((tm, D), F32), pltpu.VMEM((BLK, D), F32)],
        compiler_params=_cp(("arbitrary",), 48),
    )(z, dy, ws, bst, ln_g, ln_b, dproj)


def _merge_fwd(ya, yg, glog, bgate, h1, wabt, wgb, wo, ln_g, ln_b, tm=256):
    def body(ya_ref, yg_ref, gl_ref, bg_ref, h1_ref, wab_ref, wgb_ref, wo_ref, g_ref, b_ref,
             h_ref, hb_ref, xh_ref, rs_ref, mg_ref, bra_ref, brg_ref):
        bra = _dot_nt(ya_ref[...], wab_ref[...])
        brg = _dot_nn(yg_ref[...], wgb_ref[...])
        gates = _sigmoid(gl_ref[...].astype(F32) + bg_ref[...])
        merged = (gates[:, :D] * bra + gates[:, D:] * brg).astype(BF16)
        mix = _dot_nn(merged, wo_ref[...])
        h, xhat, rstd = _ln_fwd(ALPHA * h1_ref[...] + mix, g_ref[...], b_ref[...])
        h_ref[...] = h
        hb_ref[...] = h.astype(BF16)
        xh_ref[...] = xhat
        rs_ref[...] = rstd
        mg_ref[...] = merged
        bra_ref[...] = bra
        brg_ref[...] = brg

    row = pl.BlockSpec((tm, D), lambda i: (i, 0))
    vec = pl.BlockSpec((1, D), lambda i: (0, 0))
    full = lambda shape: pl.BlockSpec(shape, lambda i: (0, 0))
    return pl.pallas_call(
        body,
        name="merge_fwd",
        grid=(S // tm,),
        in_specs=[
            pl.BlockSpec((tm, GW), lambda i: (i, 0)), row,
            pl.BlockSpec((tm, 2 * D), lambda i: (i, glog.shape[1] // (2 * D) - 1)),
            full((1, 2 * D)), row,
            full((D, GW)), full((D, D)), full((D, D)), vec, vec,
        ],
        out_specs=[row, row, row, pl.BlockSpec((tm, 1), lambda i: (i, 0)), row, row, row],
        out_shape=[
            jax.ShapeDtypeStruct((S, D), F32),
            jax.ShapeDtypeStruct((S, D), BF16),
            jax.ShapeDtypeStruct((S, D), F32),
            jax.ShapeDtypeStruct((S, 1), F32),
            jax.ShapeDtypeStruct((S, D), BF16),
            jax.ShapeDtypeStruct((S, D), F32),
            jax.ShapeDtypeStruct((S, D), F32),
        ],
        compiler_params=_cp(("parallel",), 48),
    )(ya, yg, glog, bgate, h1, wabt, wgb, wo, ln_g, ln_b)


def _merge_bwd(dh2, xhat, rstd, ln_g, bra, brg, glog, bgate, wabt, wgb, wo, tm=256):
    def body(dh_ref, xh_ref, rs_ref, g_ref, bra_ref, brg_ref, gl_ref, bg_ref, wab_ref, wgb_ref, wo_ref,
             dr_ref, drb_ref, dlog_ref, dba_ref, dbg_ref, dya_ref, dyg_ref, dbgate_ref, dg_ref, dbias_ref):
        i = pl.program_id(0)
        dh = dh_ref[...]
        xh = xh_ref[...]
        dr = _ln_bwd(dh, xh, rs_ref[...], g_ref[...])
        drb = dr.astype(BF16)
        dr_ref[...] = dr
        drb_ref[...] = drb
        dmerged = _dot_nt(drb, wo_ref[...])
        gates = _sigmoid(gl_ref[...].astype(F32) + bg_ref[...])
        g0, g1 = gates[:, :D], gates[:, D:]
        dl0 = dmerged * bra_ref[...] * g0 * (1.0 - g0)
        dl1 = dmerged * brg_ref[...] * g1 * (1.0 - g1)
        dlog_ref[:, :D] = dl0.astype(BF16)
        dlog_ref[:, D:] = dl1.astype(BF16)
        dba = (dmerged * g0).astype(BF16)
        dbg = (dmerged * g1).astype(BF16)
        dba_ref[...] = dba
        dbg_ref[...] = dbg
        dya_ref[...] = _dot_nn(dba, wab_ref[...])
        dyg_ref[...] = _dot_nt(dbg, wgb_ref[...])
        s0, s1 = _colsum(dl0), _colsum(dl1)
        sg, sb = _colsum(dh * xh), _colsum(dh)

        @pl.when(i == 0)
        def _():
            dbgate_ref[:, :D] = s0
            dbgate_ref[:, D:] = s1
            dg_ref[...] = sg
            dbias_ref[...] = sb

        @pl.when(i > 0)
        def _():
            dbgate_ref[:, :D] += s0
            dbgate_ref[:, D:] += s1
            dg_ref[...] += sg
            dbias_ref[...] += sb

    row = pl.BlockSpec((tm, D), lambda i: (i, 0))
    vec = pl.BlockSpec((1, D), lambda i: (0, 0))
    wide = pl.BlockSpec((tm, 2 * D), lambda i: (i, 0))
    full = lambda shape: pl.BlockSpec(shape, lambda i: (0, 0))
    return pl.pallas_call(
        body,
        name="merge_bwd",
        grid=(S // tm,),
        in_specs=[row, row, pl.BlockSpec((tm, 1), lambda i: (i, 0)), vec, row, row,
                  pl.BlockSpec((tm, 2 * D), lambda i: (i, glog.shape[1] // (2 * D) - 1)),
                  full((1, 2 * D)), full((D, GW)), full((D, D)), full((D, D))],
        out_specs=[row, row, pl.BlockSpec((tm, 2 * D), lambda i: (i, DPROJ_G_COL)), row, row,
                   pl.BlockSpec((tm, GW), lambda i: (i, 0)), row, full((1, 2 * D)), vec, vec],
        out_shape=[
            jax.ShapeDtypeStruct((S, D), F32),
            jax.ShapeDtypeStruct((S, D), BF16),
            jax.ShapeDtypeStruct((S, DPROJ_W), BF16),
            jax.ShapeDtypeStruct((S, D), BF16),
            jax.ShapeDtypeStruct((S, D), BF16),
            jax.ShapeDtypeStruct((S, GW), F32),
            jax.ShapeDtypeStruct((S, D), F32),
            jax.ShapeDtypeStruct((1, 2 * D), F32),
            jax.ShapeDtypeStruct((1, D), F32),
            jax.ShapeDtypeStruct((1, D), F32),
        ],
        compiler_params=_cp(("arbitrary",), 48),
    )(dh2, xhat, rstd, ln_g, bra, brg, glog, bgate, wabt, wgb, wo)


def _tie(x, dep):
    if dep is None:
        return x
    return x + dep[0, 0].astype(x.dtype)


def _local_step(x, pos_col, target, get_w, p, emit):
    w = get_w("ffn1", None)
    a1, b1, hm1 = _ffn_up(x, w["g1"], w["u1"], "ffn1_up")
    w.update(get_w("ffn1d", hm1))
    h1, h1b, xh1, rs1 = _ffn_down(hm1, w["d1"], x, p["ln1_g"], p["ln1_b"], "ffn1_down")

    w.update(get_w("win", h1b))
    qkv = _matmul(h1b, w["win"], "nt", BF16, 1024, 1536, D, "proj_qkv", b_off=0, n_out=QKV_W)
    z = glog = _matmul(h1b, w["win"], "nt", BF16, 1024, 512, D, "proj_zg", b_off=QKV_W // 512, n_out=4 * D)

    half = jnp.arange(0, HEAD_DIM, 2, dtype=F32) / HEAD_DIM
    inv_freq = ROPE_THETA ** (-half)
    invf = jnp.tile(inv_freq, 4).reshape(1, BLK)
    sign = jnp.tile(jnp.concatenate([-jnp.ones((32,), F32), jnp.ones((32,), F32)]), 2).reshape(1, BLK)
    cos_t, sin_t = _rope_tables(pos_col, invf, sign)

    preps = _qkv_prep(qkv, cos_t, sin_t)
    os_, lses = _attn_fwd(preps)
    ya = _combine_fwd(os_, lses)
    get_w("late", ya, early=True)
    bst = p["gmlp_b_s"].T
    yg = _gmlp_fwd(z, p["gmlp_w_s"], bst, p["gmlp_ln_g"], p["gmlp_ln_b"])
    w.update(get_w("late", yg))
    h2, h2b, xh2, rs2, merged, bra, brg = _merge_fwd(ya, yg, glog, p["b_gates"], h1, w["ab"], w["gb"], w["o"],
                                                      p["ln2_g"], p["ln2_b"])
    a2, b2, hm2 = _ffn_up(h2b, w["g2"], w["u2"], "ffn2_up")
    dh3, loss, xh3, rs3 = _ffn_down(hm2, w["d2"], h2, p["ln3_g"], p["ln3_b"], "ffn2_down", target=target)

    gp = {}
    dr3, df2, da2, db2, gp["ln3_g"], gp["ln3_b"] = _ffn_bwd_mid(dh3, None, xh3, rs3, p["ln3_g"], a2, b2, w["d2"],
                                                                "ffn2_bwd_mid")
    tok = emit("ffn2", {
        "g2": _matmul(da2, h2b, "tn", BF16, 1408, D, S, "wgrad_g2"),
        "u2": _matmul(db2, h2b, "tn", BF16, 1408, D, S, "wgrad_u2"),
        "d2": _matmul(hm2, df2, "tn", BF16, 1408, D, S, "wgrad_d2")})
    dh2 = _ffn_bwd_dx(dr3, da2, db2, w["g2"], w["u2"], "ffn2_bwd_dx")

    (dr2, dr2b, dproj, dba, dbg, dya, dyg, gp["b_gates"], gp["ln2_g"], gp["ln2_b"]) = _merge_bwd(
        dh2, xh2, rs2, _tie(p["ln2_g"], tok), bra, brg, glog, p["b_gates"], w["ab"], w["gb"], w["o"])
    tok = emit("mix", {
        "o": _matmul(merged, dr2b, "tn", BF16, 512, D, S, "wgrad_o"),
        "ab": _matmul(dba, ya, "tn", BF16, 512, GW, S, "wgrad_ab"),
        "gb": _matmul(yg, dbg, "tn", BF16, 512, D, S, "wgrad_gb")})

    seg = (jnp.arange(GW)[:, None] // HEAD_DIM == jnp.arange(GW)[None, :] // HEAD_DIM).astype(BF16)
    do0, do1, do2, c0, c1, c2 = _combine_bwd(dya, os_, lses, _tie(seg, tok))
    dproj = _qkv_unprep(_attn_bwd(preps, (do0, do1, do2), lses, (c0, c1, c2)), cos_t, sin_t, dproj)
    dproj, gp["gmlp_w_s"], dbst, gp["gmlp_ln_g"], gp["gmlp_ln_b"] = _gmlp_bwd(
        z, dyg, p["gmlp_w_s"], bst, p["gmlp_ln_g"], p["gmlp_ln_b"], dproj)
    gp["gmlp_b_s"] = dbst.T
    tok = emit("win", {"win": _matmul(dproj, h1b, "tn", BF16, 512, D, S, "wgrad_win", a_map=_dproj_tile, m_out=IN_W)})
    dh1m = _matmul(dproj, w["win"], "nn", F32, S, D, 512, "dproj_to_dh1", dep=tok, a_map=_dproj_tile)

    dr1, df1, da1, db1, gp["ln1_g"], gp["ln1_b"] = _ffn_bwd_mid(dr2, dh1m, xh1, rs1, p["ln1_g"], a1, b1, w["d1"],
                                                                "ffn1_bwd_mid")
    tok = emit("small", {**gp, "loss": loss})
    tok = emit("g1", {"g1": _matmul(da1, x, "tn", BF16, 1408, D, S, "wgrad_g1", dep=tok)})
    tok = emit("u1", {"u1": _matmul(db1, x, "tn", BF16, 1408, D, S, "wgrad_u1", dep=tok)})
    tok = emit("d1", {"d1": _matmul(hm1, df1, "tn", BF16, 1408, D, S, "wgrad_d1", dep=tok)})
    dx = _ffn_bwd_dx(dr1, da1, db1, w["g1"], w["u1"], "ffn1_bwd_dx", dep=tok)
    return loss, dx, gp


_FLIPS = [(mx, my, mc) for mx in (0, 1) for my in (0, 1) for mc in (0, 1)][1:]
_GROUPS = {"ffn1": ("g1", "u1"), "ffn1d": ("d1",), "win": ("win",), "late": ("ab", "gb", "o", "g2", "u2", "d2")}
_SCATTERS = {"ffn2": ("g2", "u2", "d2"), "mix": ("ab", "gb", "o"), "win": ("win",), "g1": ("g1",), "u1": ("u1",), "d1": ("d1",)}
_TWO_STAGE = ("d1",)
_HBM = pl.BlockSpec(memory_space=pltpu.HBM)
_SEM = pl.BlockSpec(memory_space=pltpu.SEMAPHORE)
_EFFECT = pltpu.SideEffectType.DATAFLOW_SIDE_EFFECTING


def _me():
    return 4 * lax.axis_index("x") + 2 * lax.axis_index("y") + lax.axis_index("c")


def _copies_start(bufs, copies, n_sem, name, after=None):
    nb = len(bufs)
    n_after = 0 if after is None else 1

    def body(*refs):
        b = refs[:nb]
        send_sems, recv_sems = refs[nb + n_after], refs[nb + n_after + 1]
        token = refs[-1]
        x, y, c = lax.axis_index("x"), lax.axis_index("y"), lax.axis_index("c")
        me = 4 * x + 2 * y + c
        for si, s_slot, di, d_slot, (mx, my, mc), sem in copies:
            s_idx, d_idx = s_slot(me), d_slot(me)
            pltpu.make_async_remote_copy(
                src_ref=b[si] if s_idx is None else b[si].at[s_idx],
                dst_ref=b[di] if d_idx is None else b[di].at[d_idx],
                send_sem=send_sems.at[sem], recv_sem=recv_sems.at[sem],
                device_id=(x ^ mx, y ^ my, c ^ mc), device_id_type=MESH).start()
        token[...] = jnp.zeros_like(token)

    ins = [pltpu.with_memory_space_constraint(a, pltpu.HBM) for a in bufs]
    outs = pl.pallas_call(
        body,
        name=name,
        in_specs=[_HBM] * nb + [pl.BlockSpec(memory_space=pl.ANY)] * n_after,
        out_specs=[_SEM, _SEM] + [_HBM] * nb + [pl.BlockSpec(memory_space=pltpu.VMEM)],
        out_shape=[pltpu.SemaphoreType.DMA((n_sem,)), pltpu.SemaphoreType.DMA((n_sem,))]
        + [pltpu.HBM(a.shape, a.dtype) for a in ins] + [jax.ShapeDtypeStruct((8, 128), F32)],
        input_output_aliases={k: 2 + k for k in range(nb)},
        compiler_params=pltpu.CompilerParams(has_side_effects=_EFFECT),
    )(*ins, *([] if after is None else [after]))
    return outs[0], outs[1], list(outs[2:2 + nb]), outs[-1]


def _copies_wait(started, waits, after, name):
    send_sems, recv_sems, bufs, _ = started
    nb = len(bufs)

    def body(*refs):
        b = refs[:nb]
        ss, rs = refs[nb], refs[nb + 1]
        me3 = (lax.axis_index("x"), lax.axis_index("y"), lax.axis_index("c"))
        for bi, n_blocks, sem, is_send in waits:
            blocks = b[bi].at[pl.ds(0, n_blocks)]
            cp = pltpu.make_async_remote_copy(src_ref=blocks, dst_ref=blocks, send_sem=ss.at[sem], recv_sem=rs.at[sem],
                                              device_id=me3, device_id_type=MESH)
            if is_send:
                cp.wait_send()
            else:
                cp.wait_recv()

    return pl.pallas_call(
        body,
        name=name,
        in_specs=[_HBM] * nb + [_SEM, _SEM, pl.BlockSpec(memory_space=pl.ANY)],
        out_specs=[_HBM] * nb,
        out_shape=[pltpu.HBM(a.shape, a.dtype) for a in bufs],
        input_output_aliases={k: k for k in range(nb)},
        compiler_params=pltpu.CompilerParams(has_side_effects=_EFFECT),
    )(*bufs, send_sems, recv_sems, after)


def _landing(own, me):
    return lax.dynamic_update_slice(lax.empty((N_DEV,) + own.shape[1:], own.dtype), own, (me, 0, 0))


_SIBLING = (0, 0, 1)
_OTHER_CHIPS = ((1, 0, 0), (0, 1, 0), (1, 1, 0))


def _bits(flip):
    return 4 * flip[0] + 2 * flip[1] + flip[2]


def _gather_send(shards, grp, after):
    nw = len(shards)
    me = _me()
    bufs = list(shards) + [_landing(a[None], me) for a in shards]
    copies = []
    for k in range(nw):
        copies.append((k, lambda me: None, nw + k, lambda me: me, _SIBLING, nw + k))
        copies += [(k, lambda me: None, nw + k, lambda me: me, f, k) for f in _OTHER_CHIPS]
    return _copies_start(bufs, copies, 2 * nw, "gather_send_" + grp, after)


def _gather_pass(started, grp, after):
    nw = len(started[2]) // 2
    waits = []
    for k in range(nw):
        waits += [(nw + k, 3, k, True), (nw + k, 1, nw + k, True), (nw + k, 3, k, False), (nw + k, 1, nw + k, False)]
    lands = list(_copies_wait(started, waits, after, "gather_arrived_" + grp)[nw:])
    copies = []
    for k in range(nw):
        for f in _OTHER_CHIPS:
            slot = functools.partial(lambda me, bits: me ^ bits, bits=_bits(f))
            copies.append((k, slot, k, slot, _SIBLING, k))
    return _copies_start(lands, copies, nw, "gather_pass_" + grp)


def _gather_finish(passed, grp, after):
    nw = len(passed[2])
    waits = [(k, 3, k, is_send) for k in range(nw) for is_send in (True, False)]
    return _copies_wait(passed, waits, passed[3] if after is None else after, "gather_done_" + grp)


def _scatter_send(parts, grp, after=None):
    nw = len(parts)
    me = _me()
    bufs = list(parts) + [_landing(lax.dynamic_index_in_dim(a, me, 0, keepdims=True), me) for a in parts]
    copies = []
    for k in range(nw):
        for f in _FLIPS:
            to = functools.partial(lambda me, bits: me ^ bits, bits=_bits(f))
            copies.append((k, to, nw + k, lambda me: me, f, k))
    return _copies_start(bufs, copies, nw, "scatter_send_" + grp, after)


def _scatter_finish(started, grp, after):
    nw = len(started[2]) // 2
    waits = [(nw + k, N_DEV - 1, k, is_send) for k in range(nw) for is_send in (True, False)]
    return _copies_wait(started, waits, after, "scatter_done_" + grp)[nw:]


N_CHIP = N_DEV // 2


def _pair_sum(part, other, key):
    _, r, c = part.shape
    tm = _ADAM_ROWS.get(r, r)
    core = lax.axis_index("c").astype(jnp.int32).reshape(1)

    def body(core_ref, a_ref, b_ref, o_ref):
        o_ref[...] = (a_ref[...].astype(F32) + b_ref[...].astype(F32)).astype(BF16)

    return pl.pallas_call(
        body,
        name="pair_sum_" + key,
        grid_spec=pltpu.PrefetchScalarGridSpec(
            num_scalar_prefetch=1,
            grid=(N_CHIP, r // tm),
            in_specs=[pl.BlockSpec((None, tm, c), lambda q, i, core_ref: (2 * q + core_ref[0], i, 0)),
                      pl.BlockSpec((None, tm, c), lambda q, i, core_ref: (q, i, 0))],
            out_specs=pl.BlockSpec((None, tm, c), lambda q, i, core_ref: (q, i, 0)),
        ),
        out_shape=jax.ShapeDtypeStruct((N_CHIP, r, c), BF16),
        compiler_params=_cp(("parallel", "parallel")),
    )(core, part, other)


def _scatter2_send(part, key, dep=None):
    me = _me()
    _, r, c = part.shape
    swap = []
    for q in range(N_CHIP):
        src = functools.partial(lambda me, q: 2 * q + 1 - me % 2, q=q)
        swap.append((0, src, 1, functools.partial(lambda me, q: q, q=q), _SIBLING, 0))
    started = _copies_start([part, lax.empty((N_CHIP, r, c), part.dtype)], swap, 1, "scatter_swap_" + key, dep)
    waits = [(1, N_CHIP, 0, True), (1, N_CHIP, 0, False)]
    part, other = _copies_wait(started, waits, started[3], "scatter_swapped_" + key)
    pair = _pair_sum(part, other, key)
    chip = me // 2
    land = lax.dynamic_update_slice(lax.empty((N_CHIP, r, c), pair.dtype),
                                    lax.dynamic_index_in_dim(pair, chip, 0, keepdims=True), (chip, 0, 0))
    copies = []
    for f in _OTHER_CHIPS:
        to = functools.partial(lambda me, bits: (me ^ bits) // 2, bits=_bits(f))
        copies.append((0, to, 1, lambda me: me // 2, f, 0))
    return _copies_start([pair, land], copies, 1, "scatter_send_" + key)


def _scatter2_finish(started, key, after):
    waits = [(1, N_CHIP - 1, 0, True), (1, N_CHIP - 1, 0, False)]
    return _copies_wait(started, waits, after, "scatter_done_" + key)[1]


def _allgather_send(block, name, after=None):
    me = _me()
    copies = [(0, lambda me: None, 1, lambda me: me, f, 0) for f in _FLIPS]
    return _copies_start([block, _landing(block[None], me)], copies, 1, name, after)


def _allgather_finish(started, name, after):
    waits = [(1, N_DEV - 1, 0, True), (1, N_DEV - 1, 0, False)]
    return _copies_wait(started, waits, after, name)[1]


_ADAM_ROWS = {352: 176, 1088: 272}


def _sum_adamw(parts, wv, m, v, name):
    n_parts, r, c = parts.shape
    tm = _ADAM_ROWS.get(r, r)
    assert r % tm == 0 and wv.shape == (r, c)

    def body(p_ref, w_ref, m_ref, v_ref, g_ref, d_ref, mo_ref, vo_ref):
        gv = p_ref[0].astype(F32)
        for j in range(1, n_parts):
            gv = gv + p_ref[j].astype(F32)
        g_ref[...] = gv
        mn = ADAM_B1 * m_ref[...] + (1.0 - ADAM_B1) * gv
        vn = ADAM_B2 * v_ref[...] + (1.0 - ADAM_B2) * (gv * gv)
        m_hat = mn / (1.0 - ADAM_B1 ** ADAM_STEP)
        v_hat = vn / (1.0 - ADAM_B2 ** ADAM_STEP)
        d_ref[...] = -ADAM_LR * (m_hat / (jnp.sqrt(v_hat) + ADAM_EPS) + ADAM_WD * w_ref[...])
        mo_ref[...] = mn
        vo_ref[...] = vn

    sp = pl.BlockSpec((tm, c), lambda i: (i, 0))
    return pl.pallas_call(
        body,
        name=name,
        grid=(r // tm,),
        in_specs=[pl.BlockSpec((n_parts, tm, c), lambda i: (0, i, 0))] + [sp] * 3,
        out_specs=[sp] * 4,
        out_shape=[jax.ShapeDtypeStruct((r, c), F32)] * 4,
        compiler_params=_cp(("parallel",), 48),
    )(parts, wv, m, v)


_WEIGHTS = ["ffn1_w_gate", "ffn1_w_up", "ffn1_w_down", "ln1_g", "ln1_b", "w_in", "b_gates", "gmlp_ln_g", "gmlp_ln_b",
            "gmlp_w_s", "gmlp_b_s", "w_attn_branch", "w_gmlp_branch", "w_out", "ln2_g", "ln2_b", "ffn2_w_gate",
            "ffn2_w_up", "ffn2_w_down", "ln3_g", "ln3_b"]
_BIG_OF = {"ffn1_w_gate": ("g1", True), "ffn1_w_up": ("u1", True), "ffn1_w_down": ("d1", False), "w_in": ("win", True),
           "w_attn_branch": ("ab", True), "w_gmlp_branch": ("gb", False), "w_out": ("o", False),
           "ffn2_w_gate": ("g2", True), "ffn2_w_up": ("u2", True), "ffn2_w_down": ("d2", False)}
_SMALL = [n for n in _WEIGHTS if n not in _BIG_OF]
_SMALL_ROWS = {"gmlp_w_s": 128, "b_gates": 2}
_SMALL_SLOT = 8


def _pack_small(d, last=None):
    rows = []
    for n in _SMALL:
        r = d[n].reshape(-1, D)
        slot = max(r.shape[0], _SMALL_SLOT)
        rows.append(jnp.pad(r, ((0, slot - r.shape[0]), (0, 0))))
    rows.append(jnp.zeros((_SMALL_SLOT, D), F32) if last is None else jnp.broadcast_to(last.reshape(1, 1), (_SMALL_SLOT, D)))
    return jnp.concatenate(rows, axis=0)


def _unpack_small(packed, shapes):
    out, at = {}, 0
    for n in _SMALL:
        k = _SMALL_ROWS.get(n, 1)
        out[n] = packed[at:at + k].reshape(shapes[n])
        at += max(k, _SMALL_SLOT)
    return out


def kernel(x, positions, ffn1_w_gate, ffn1_w_up, ffn1_w_down, ln1_g, ln1_b, w_in, b_gates, gmlp_ln_g, gmlp_ln_b, gmlp_w_s, gmlp_b_s, w_attn_branch, w_gmlp_branch, w_out, ln2_g, ln2_b, ffn2_w_gate, ffn2_w_up, ffn2_w_down, ln3_g, ln3_b, loss_target, m_ffn1_w_gate, m_ffn1_w_up, m_ffn1_w_down, m_ln1_g, m_ln1_b, m_w_in, m_b_gates, m_gmlp_ln_g, m_gmlp_ln_b, m_gmlp_w_s, m_gmlp_b_s, m_w_attn_branch, m_w_gmlp_branch, m_w_out, m_ln2_g, m_ln2_b, m_ffn2_w_gate, m_ffn2_w_up, m_ffn2_w_down, m_ln3_g, m_ln3_b, v_ffn1_w_gate, v_ffn1_w_up, v_ffn1_w_down, v_ln1_g, v_ln1_b, v_w_in, v_b_gates, v_gmlp_ln_g, v_gmlp_ln_b, v_gmlp_w_s, v_gmlp_b_s, v_w_attn_branch, v_w_gmlp_branch, v_w_out, v_ln2_g, v_ln2_b, v_ffn2_w_gate, v_ffn2_w_up, v_ffn2_w_down, v_ln3_g, v_ln3_b):
    args = dict(locals())
    wts = {n: args[n] for n in _WEIGHTS}
    ms = {n: args["m_" + n] for n in _WEIGHTS}
    vs = {n: args["v_" + n] for n in _WEIGHTS}

    name_of = {key: (n, tr) for n, (key, tr) in _BIG_OF.items()}

    started, tok = {}, None
    for grp, keys in _GROUPS.items():
        shards = []
        for key in keys:
            n, tr = name_of[key]
            s2 = wts[n][0]
            shards.append(_tie(s2.T if tr else s2, tok).astype(BF16))
        started[grp] = _gather_send(shards, grp, tok)
        tok = started[grp][3]
    all_started = tok
    passed = {}

    def get_w(grp, after, early=False):
        if grp not in passed:
            passed[grp] = _gather_pass(started[grp], grp, all_started if after is None else after)
            after = None
        if early:
            return None
        lands = _gather_finish(passed[grp], grp, after)
        return {key: g.reshape(-1, g.shape[-1]) for key, g in zip(_GROUPS[grp], lands)}

    sent = {}

    def emit(grp, grads):
        if grp == "small":
            sent[grp] = _allgather_send(_pack_small(grads, last=grads["loss"]), "small_grads_send")
        else:
            parts = [grads[key].reshape(N_DEV, -1, grads[key].shape[-1]) for key in _SCATTERS[grp]]
            sent[grp] = _scatter2_send(parts[0], grp) if grp in _TWO_STAGE else _scatter_send(parts, grp)
        return sent[grp][3]

    p = {n: (wts[n][0] if n in ("gmlp_w_s", "gmlp_b_s") else wts[n]) for n in _SMALL}
    loss, dx, gp = _local_step(x[0], positions.reshape(S, 1), loss_target[0], get_w, p, emit)
    grads, deltas, new_m, new_v = {}, {}, {}, {}
    after = dx
    for grp in ("ffn2", "mix", "win", "small", "g1", "u1", "d1"):
        if grp == "small":
            parts = _allgather_finish(sent[grp], "small_grads_done", after)
            outs = _sum_adamw(parts, *[_pack_small({n: d[n] for n in _SMALL}) for d in (wts, ms, vs)], "update_small")
            shapes = {n: wts[n].shape for n in _SMALL}
            for dst, packed in zip((grads, deltas, new_m, new_v), outs):
                dst.update(_unpack_small(packed, shapes))
            loss = outs[0][-_SMALL_SLOT, 0]
            after = outs[1]
            continue
        arrived = ([_scatter2_finish(sent[grp], grp, after)] if grp in _TWO_STAGE
                   else _scatter_finish(sent[grp], grp, after))
        for key, part in zip(_SCATTERS[grp], arrived):
            n, tr = name_of[key]
            outs = _sum_adamw(part, *[(d[n][0].T if tr else d[n][0]) for d in (wts, ms, vs)], "update_" + key)
            for dst, o in zip((grads, deltas, new_m, new_v), outs):
                dst[n] = (o.T if tr else o)[None]
            after = outs[1]

    return (loss, dx[None], *[grads[n] for n in _WEIGHTS], *[deltas[n] for n in _WEIGHTS],
            *[new_m[n] for n in _WEIGHTS], *[new_v[n] for n in _WEIGHTS])
```

```python
import functools
import math

import jax
import jax.numpy as jnp
from jax import lax
from jax.experimental import pallas as pl
from jax.experimental.pallas import tpu as pltpu

F32 = jnp.float32
BF16 = jnp.bfloat16

N_DEV = 8
D = 1024
S = 2048
F = 2816
HEAD_DIM = 64
HEADS = 8
GW = HEADS * HEAD_DIM
PATTERN_DILATIONS = (1, 4, 16)
BLK = 128
QKV_W = 3 * 3 * GW
IN_W = QKV_W + 2 * D + 2 * D
DPROJ_W = 5 * 2 * D
DPROJ_Z_COL, DPROJ_G_COL = 3, 4


def _dproj_tile(t):
    three = jnp.int32(3)
    return jnp.where(t < 9, lax.rem(t, three) * 3 + lax.div(t, three), t + 3)
ROPE_THETA = 10000.0
ALPHA = 2.0 ** 0.25
LN_EPS = 1e-5
ADAM_LR, ADAM_B1, ADAM_B2, ADAM_EPS, ADAM_WD, ADAM_STEP = 0.001, 0.9, 0.999, 1e-08, 0.01, 10
NEG = -1e30
MESH = pl.DeviceIdType.MESH


def _cp(sem=None, vmem_mb=None):
    kw = {}
    if sem is not None:
        kw["dimension_semantics"] = sem
    if vmem_mb is not None:
        kw["vmem_limit_bytes"] = vmem_mb << 20
    return pltpu.CompilerParams(**kw)


def _dot_nn(a, b):
    return lax.dot_general(a, b, (((1,), (0,)), ((), ())), preferred_element_type=F32)


def _dot_nt(a, b):
    return lax.dot_general(a, b, (((1,), (1,)), ((), ())), preferred_element_type=F32)


def _dot_tn(a, b):
    return lax.dot_general(a, b, (((0,), (0,)), ((), ())), preferred_element_type=F32)


def _ln_fwd(r, g, b):
    mu = jnp.mean(r, axis=-1, keepdims=True)
    xc = r - mu
    var = jnp.mean(xc * xc, axis=-1, keepdims=True)
    rstd = lax.rsqrt(var + LN_EPS)
    xhat = xc * rstd
    return xhat * g + b, xhat, rstd


def _ln_bwd(dh, xhat, rstd, g):
    dxh = dh * g
    m1 = jnp.mean(dxh, axis=-1, keepdims=True)
    m2 = jnp.mean(dxh * xhat, axis=-1, keepdims=True)
    return rstd * (dxh - m1 - xhat * m2)


def _sigmoid(x):
    return 0.5 * jnp.tanh(0.5 * x) + 0.5


def _colsum(x):
    return jnp.sum(x, axis=0, keepdims=True)


def _matmul(a, b, mode, out_dtype, tm, tn, tk, name, b_off=0, n_out=None, dep=None, a_map=None, m_out=None):
    n_dep = 0 if dep is None else 1
    a_map = a_map or (lambda t: t)
    if mode == "nn":
        m, k = a.shape[0], b.shape[0]
        n = b.shape[1]
    elif mode == "nt":
        m, k = a.shape
        n = n_out if n_out is not None else b.shape[0]
    else:
        k, m = a.shape[0], m_out or a.shape[1]
        n = b.shape[1]
    nk = k // tk
    assert m % tm == 0 and n % tn == 0 and k % tk == 0
    dot = {"nn": _dot_nn, "nt": _dot_nt, "tn": _dot_tn}[mode]

    def body(a_ref, b_ref, *rest):
        o_ref, scr = rest[n_dep], rest[n_dep + 1:]
        r = dot(a_ref[...].astype(BF16), b_ref[...].astype(BF16))
        if nk == 1:
            o_ref[...] = r.astype(out_dtype)
        else:
            acc = scr[0]
            kk = pl.program_id(2)

            @pl.when(kk == 0)
            def _():
                acc[...] = r

            @pl.when(kk > 0)
            def _():
                acc[...] += r

            @pl.when(kk == nk - 1)
            def _():
                o_ref[...] = acc[...].astype(out_dtype)

    if mode == "nn":
        a_spec = pl.BlockSpec((tm, tk), lambda i, j, kk: (i, a_map(kk)))
        b_spec = pl.BlockSpec((tk, tn), lambda i, j, kk: (kk, j))
    elif mode == "nt":
        a_spec = pl.BlockSpec((tm, tk), lambda i, j, kk: (i, kk))
        b_spec = pl.BlockSpec((tn, tk), lambda i, j, kk: (j + b_off, kk))
    else:
        a_spec = pl.BlockSpec((tk, tm), lambda i, j, kk: (kk, a_map(i)))
        b_spec = pl.BlockSpec((tk, tn), lambda i, j, kk: (kk, j))
    return pl.pallas_call(
        body,
        name=name,
        grid=(m // tm, n // tn, nk),
        in_specs=[a_spec, b_spec] + [pl.BlockSpec(memory_space=pl.ANY)] * n_dep,
        out_specs=pl.BlockSpec((tm, tn), lambda i, j, kk: (i, j)),
        out_shape=jax.ShapeDtypeStruct((m, n), out_dtype),
        scratch_shapes=[] if nk == 1 else [pltpu.VMEM((tm, tn), F32)],
        compiler_params=_cp(("parallel", "parallel", "arbitrary"), 56),
    )(a, b, *([] if dep is None else [dep]))


def _ffn_up(x, wgt, wut, name, tm=512, tn=1408, dep=None):
    n_dep = 0 if dep is None else 1

    def body(x_ref, wg_ref, wu_ref, *rest):
        ga_ref, gb_ref, hm_ref = rest[n_dep:]
        xb = x_ref[...].astype(BF16)
        a = _dot_nt(xb, wg_ref[...])
        b = _dot_nt(xb, wu_ref[...])
        sig = _sigmoid(a)
        silu = a * sig
        ga_ref[...] = (b * (sig + silu * (1.0 - sig))).astype(BF16)
        gb_ref[...] = silu.astype(BF16)
        hm_ref[...] = (silu * b).astype(BF16)

    wsp = pl.BlockSpec((tn, D), lambda i, j: (j, 0))
    mid = pl.BlockSpec((tm, tn), lambda i, j: (i, j))
    return pl.pallas_call(
        body,
        name=name,
        grid=(S // tm, F // tn),
        in_specs=[pl.BlockSpec((tm, D), lambda i, j: (i, 0)), wsp, wsp] + [pl.BlockSpec(memory_space=pl.ANY)] * n_dep,
        out_specs=[mid, mid, mid],
        out_shape=[jax.ShapeDtypeStruct((S, F), BF16)] * 3,
        compiler_params=_cp(("parallel", "arbitrary"), 56),
    )(x, wgt, wut, *([] if dep is None else [dep]))


def _ffn_down(hm, wd, x, ln_g, ln_b, name, target=None, tm=512):
    head = target is not None

    def body(hm_ref, wd_ref, x_ref, g_ref, b_ref, *rest):
        t_ref = rest[0] if head else None
        o1_ref, o2_ref, xh_ref, rs_ref = rest[1 if head else 0:]
        r = ALPHA * x_ref[...] + 0.5 * _dot_nn(hm_ref[...], wd_ref[...])
        h, xhat, rstd = _ln_fwd(r, g_ref[...], b_ref[...])
        xh_ref[...] = xhat
        rs_ref[...] = rstd
        if head:
            e = h - t_ref[...]
            o1_ref[...] = e * (1.0 / D)
            part = jnp.sum(_colsum(e * e), axis=1, keepdims=True) * (0.5 / D)

            @pl.when(pl.program_id(0) == 0)
            def _():
                o2_ref[...] = jnp.zeros_like(o2_ref)

            o2_ref[...] += part
        else:
            o1_ref[...] = h
            o2_ref[...] = h.astype(BF16)

    row = pl.BlockSpec((tm, D), lambda i: (i, 0))
    vec = pl.BlockSpec((1, D), lambda i: (0, 0))
    second = (pl.BlockSpec((1, 1), lambda i: (0, 0)), jax.ShapeDtypeStruct((1, 1), F32)) if head else (
        row, jax.ShapeDtypeStruct((S, D), BF16))
    return pl.pallas_call(
        body,
        name=name,
        grid=(S // tm,),
        in_specs=[pl.BlockSpec((tm, F), lambda i: (i, 0)), pl.BlockSpec((F, D), lambda i: (0, 0)), row, vec, vec]
        + ([row] if head else []),
        out_specs=[row, second[0], row, pl.BlockSpec((tm, 1), lambda i: (i, 0))],
        out_shape=[
            jax.ShapeDtypeStruct((S, D), F32),
            second[1],
            jax.ShapeDtypeStruct((S, D), F32),
            jax.ShapeDtypeStruct((S, 1), F32),
        ],
        compiler_params=_cp(("arbitrary",), 56),
    )(hm, wd, x, ln_g, ln_b, *([target] if head else []))


def _ffn_bwd_mid(dh_a, dh_b, xhat, rstd, ln_g, a, b, wd, name, tm=512, tn=1408):
    two = dh_b is not None

    def body(*refs):
        dha_ref = refs[0]
        dhb_ref = refs[1] if two else None
        (xh_ref, rs_ref, g_ref, a_ref, b_ref, wd_ref,
         dr_ref, df_ref, da_ref, db_ref, dg_ref, dbias_ref, df_scr) = refs[2 if two else 1:]
        i = pl.program_id(0)
        j = pl.program_id(1)

        @pl.when(j == 0)
        def _():
            dh = dha_ref[...]
            if two:
                dh = ALPHA * dh + dhb_ref[...]
            xhat = xh_ref[...]
            dr = _ln_bwd(dh, xhat, rs_ref[...], g_ref[...])
            dfb = (0.5 * dr).astype(BF16)
            dr_ref[...] = dr
            df_scr[...] = dfb
            df_ref[...] = dfb
            sg = _colsum(dh * xhat)
            sb = _colsum(dh)

            @pl.when(i == 0)
            def _():
                dg_ref[...] = sg
                dbias_ref[...] = sb

            @pl.when(i > 0)
            def _():
                dg_ref[...] += sg
                dbias_ref[...] += sb

        dhm = _dot_nt(df_scr[...], wd_ref[...])
        da_ref[...] = (dhm * a_ref[...].astype(F32)).astype(BF16)
        db_ref[...] = (dhm * b_ref[...].astype(F32)).astype(BF16)

    row = pl.BlockSpec((tm, D), lambda i, j: (i, 0))
    vec = pl.BlockSpec((1, D), lambda i, j: (0, 0))
    mid = pl.BlockSpec((tm, tn), lambda i, j: (i, j))
    ins = [dh_a] + ([dh_b] if two else []) + [xhat, rstd, ln_g, a, b, wd]
    in_specs = [row] * (2 if two else 1) + [row, pl.BlockSpec((tm, 1), lambda i, j: (i, 0)), vec, mid, mid,
                                            pl.BlockSpec((tn, D), lambda i, j: (j, 0))]
    return pl.pallas_call(
        body,
        name=name,
        grid=(S // tm, F // tn),
        in_specs=in_specs,
        out_specs=[row, row, mid, mid, vec, vec],
        out_shape=[
            jax.ShapeDtypeStruct((S, D), F32),
            jax.ShapeDtypeStruct((S, D), BF16),
            jax.ShapeDtypeStruct((S, F), BF16),
            jax.ShapeDtypeStruct((S, F), BF16),
            jax.ShapeDtypeStruct((1, D), F32),
            jax.ShapeDtypeStruct((1, D), F32),
        ],
        scratch_shapes=[pltpu.VMEM((tm, D), BF16)],
        compiler_params=_cp(("arbitrary", "arbitrary"), 56),
    )(*ins)


def _ffn_bwd_dx(dr, da, db, wgt, wut, name, tm=512, tk=1408, dep=None):
    nk = F // tk
    n_dep = 0 if dep is None else 1

    def body(dr_ref, da_ref, db_ref, wg_ref, wu_ref, *rest):
        dx_ref, acc = rest[n_dep], rest[n_dep + 1]
        kk = pl.program_id(1)
        part = _dot_nn(da_ref[...], wg_ref[...]) + _dot_nn(db_ref[...], wu_ref[...])

        @pl.when(kk == 0)
        def _():
            acc[...] = ALPHA * dr_ref[...] + part

        @pl.when(kk > 0)
        def _():
            acc[...] += part

        @pl.when(kk == nk - 1)
        def _():
            dx_ref[...] = acc[...]

    row = pl.BlockSpec((tm, D), lambda i, kk: (i, 0))
    mid = pl.BlockSpec((tm, tk), lambda i, kk: (i, kk))
    wsp = pl.BlockSpec((tk, D), lambda i, kk: (kk, 0))
    return pl.pallas_call(
        body,
        name=name,
        grid=(S // tm, nk),
        in_specs=[row, mid, mid, wsp, wsp] + [pl.BlockSpec(memory_space=pl.ANY)] * n_dep,
        out_specs=row,
        out_shape=jax.ShapeDtypeStruct((S, D), F32),
        scratch_shapes=[pltpu.VMEM((tm, D), F32)],
        compiler_params=_cp(("parallel", "arbitrary"), 56),
    )(dr, da, db, wgt, wut, *([] if dep is None else [dep]))


def _rope_tables(pos_col, invf, sign, tm=512):
    def body(p_ref, f_ref, s_ref, c_out, s_out):
        ang = p_ref[...].astype(F32) * f_ref[...]
        c_out[...] = jnp.cos(ang)
        s_out[...] = jnp.sin(ang) * s_ref[...]

    vec = pl.BlockSpec((1, BLK), lambda i: (0, 0))
    out = pl.BlockSpec((tm, BLK), lambda i: (i, 0))
    return pl.pallas_call(
        body,
        name="rope_tables",
        grid=(S // tm,),
        in_specs=[pl.BlockSpec((tm, 1), lambda i: (i, 0)), vec, vec],
        out_specs=[out, out],
        out_shape=[jax.ShapeDtypeStruct((S, BLK), F32)] * 2,
        compiler_params=_cp(("parallel",)),
    )(pos_col, invf, sign)


def _lane_lo(rows=BLK):
    return (lax.broadcasted_iota(jnp.int32, (rows, GW), 1) % HEAD_DIM) < (HEAD_DIM // 2)


def _swap_halves(t, lo):
    return jnp.where(lo, pltpu.roll(t, GW - HEAD_DIM // 2, 1), pltpu.roll(t, HEAD_DIM // 2, 1))


def _rope(t, cosf, sinf, lo):
    return t * cosf + _swap_halves(t, lo) * sinf


def _unrope(g, cosf, sinf, lo):
    return g * cosf + _swap_halves(g * sinf, lo)


def _tile4(v):
    return jnp.concatenate([v, v, v, v], axis=1)


def _band_mask(n):
    qi = lax.broadcasted_iota(jnp.int32, (BLK, 2 * BLK), 0)
    kj = lax.broadcasted_iota(jnp.int32, (BLK, 2 * BLK), 1)
    dist = qi + BLK - kj
    return (dist >= 0) & (dist <= BLK) & ((kj >= BLK) | (n >= 1))


ROWS = 256
LANES = 128


def _to_view(scr, y, dst_ref, d, dtype, col0=0):
    if d == 1:
        dst_ref[:, col0:col0 + GW] = y.astype(dtype)
        return
    for cb in range(GW // LANES):
        scr[cb][...] = y[:, cb * LANES:(cb + 1) * LANES]
    for r in range(d):
        for cb in range(GW // LANES):
            at = col0 + r * GW + cb * LANES
            dst_ref[:, at:at + LANES] = scr[cb][pl.ds(r, ROWS // d, stride=d), :].astype(dtype)


def _from_view(scr, src_ref, d):
    if d == 1:
        return src_ref[...].astype(F32)
    for r in range(d):
        for cb in range(GW // LANES):
            at = r * GW + cb * LANES
            scr[cb][pl.ds(r, ROWS // d, stride=d), :] = src_ref[:, at:at + LANES].astype(F32)
    return jnp.concatenate([scr[cb][...] for cb in range(GW // LANES)], axis=1)


def _view_spec(d):
    return pl.BlockSpec((ROWS // d, d * GW), lambda i: (i, 0))


_VIEW_SCRATCH = [pltpu.VMEM((ROWS, LANES), F32)] * (GW // LANES)


def _qkv_prep(qkv, cos_t, sin_t):
    def body(x_ref, c_ref, s_ref, *rest):
        outs, scr = rest[:9], rest[9:]
        lo = _lane_lo(ROWS)
        cf, sf = _tile4(c_ref[...]), _tile4(s_ref[...])
        for gi, d in enumerate(PATTERN_DILATIONS):
            q, k, v = (x_ref[:, (3 * part + gi) * GW:(3 * part + gi + 1) * GW].astype(F32) for part in range(3))
            _to_view(scr, _rope(q, cf, sf, lo) * (HEAD_DIM ** -0.5), outs[3 * gi], d, BF16)
            _to_view(scr, _rope(k, cf, sf, lo), outs[3 * gi + 1], d, BF16)
            _to_view(scr, v, outs[3 * gi + 2], d, BF16)

    tab = pl.BlockSpec((ROWS, BLK), lambda i: (i, 0))
    outs = pl.pallas_call(
        body,
        name="qkv_prep",
        grid=(S // ROWS,),
        in_specs=[pl.BlockSpec((ROWS, QKV_W), lambda i: (i, 0)), tab, tab],
        out_specs=[_view_spec(d) for d in PATTERN_DILATIONS for _ in range(3)],
        out_shape=[jax.ShapeDtypeStruct((S // d, d * GW), BF16) for d in PATTERN_DILATIONS for _ in range(3)],
        scratch_shapes=_VIEW_SCRATCH,
        compiler_params=_cp(("parallel",), 48),
    )(qkv, cos_t, sin_t)
    return [tuple(outs[3 * gi:3 * gi + 3]) for gi in range(3)]


N_ATTN_STEPS = S // BLK


def _class_and_block(gi, t):
    per_class = N_ATTN_STEPS // PATTERN_DILATIONS[gi]
    return t >> (per_class.bit_length() - 1), t & (per_class - 1)


def _attn_specs(gi):
    def cur(t):
        r, n = _class_and_block(gi, t)
        return n, r

    def prev(t):
        r, n = _class_and_block(gi, t)
        return jnp.maximum(n - 1, 0), r

    def whole(t):
        return 0, _class_and_block(gi, t)[0]

    sub = S // PATTERN_DILATIONS[gi]
    return (pl.BlockSpec((BLK, GW), cur), pl.BlockSpec((BLK, GW), prev), pl.BlockSpec((sub, GW), whole))


def _left_lanes():
    return lax.broadcasted_iota(jnp.int32, (BLK, LANES), 1) < HEAD_DIM


def _stack_heads(t2, left):
    zero = jnp.zeros_like(t2)
    return jnp.concatenate([jnp.where(left, t2, zero), jnp.where(left, zero, t2)], axis=0)


def _attn_fwd_block(n, q_ref, kc_ref, kp_ref, vc_ref, vp_ref, o_ref, l_ref):
    mask = _band_mask(n)
    left = _left_lanes()
    zero = jnp.zeros((BLK, LANES), BF16)
    for pr in range(GW // LANES):
        cs = slice(pr * LANES, (pr + 1) * LANES)
        q2 = q_ref[:, cs]
        k2 = jnp.concatenate([kp_ref[:, cs], kc_ref[:, cs]], axis=0)
        v2 = jnp.concatenate([vp_ref[:, cs], vc_ref[:, cs]], axis=0)
        o_h, lse_h = [], []
        for side in (left, ~left):
            s = jnp.where(mask, _dot_nt(jnp.where(side, q2, zero), k2), NEG)
            m = jnp.max(s, axis=1, keepdims=True)
            p = jnp.exp(s - m)
            l = jnp.sum(p, axis=1, keepdims=True)
            o_h.append(_dot_nn((p / l).astype(BF16), v2))
            lse_h.append(m + jnp.log(l))
        o_ref[:, cs] = jnp.where(left, o_h[0], o_h[1])
        l_ref[:, cs] = jnp.where(left, lse_h[0], lse_h[1])


def _attn_fwd(preps):
    def body(*refs):
        t = pl.program_id(0)
        for gi in range(3):
            _attn_fwd_block(_class_and_block(gi, t)[1], *refs[5 * gi:5 * gi + 5], *refs[15 + 2 * gi:17 + 2 * gi])

    in_specs, out_specs, out_shape, args = [], [], [], []
    for gi, d in enumerate(PATTERN_DILATIONS):
        cur, prev, _ = _attn_specs(gi)
        qr, kr, vv = preps[gi]
        in_specs += [cur, cur, prev, cur, prev]
        args += [qr, kr, kr, vv, vv]
        out_specs += [cur, cur]
        out_shape += [jax.ShapeDtypeStruct((S // d, d * GW), F32)] * 2
    outs = pl.pallas_call(
        body,
        name="attn_fwd",
        grid=(N_ATTN_STEPS,),
        in_specs=in_specs,
        out_specs=out_specs,
        out_shape=out_shape,
        compiler_params=_cp(("arbitrary",), 48),
    )(*args)
    return [outs[0], outs[2], outs[4]], [outs[1], outs[3], outs[5]]


def _attn_bwd_block(n, q_ref, kc_ref, kp_ref, vc_ref, vp_ref, do_ref, l_ref, c_ref, dq_ref, dk_ref, dv_ref):
    mask = _band_mask(n)
    left = _left_lanes()
    left2 = jnp.concatenate([left, left], axis=0)
    here = pl.ds(pl.multiple_of(n * BLK, BLK), BLK)
    before = pl.ds(pl.multiple_of(jnp.maximum(n - 1, 0) * BLK, BLK), BLK)
    for pr in range(GW // LANES):
        cs = slice(pr * LANES, (pr + 1) * LANES)
        q2 = q_ref[:, cs]
        do2 = do_ref[:, cs]
        k_bd = _stack_heads(jnp.concatenate([kp_ref[:, cs], kc_ref[:, cs]], axis=0), left2)
        v_bd = _stack_heads(jnp.concatenate([vp_ref[:, cs], vc_ref[:, cs]], axis=0), left2)
        s2 = _dot_nt(q2, k_bd)
        dp2 = _dot_nt(do2, v_bd)
        ps, dss = [], []
        for h in range(2):
            at = pr * LANES + h * HEAD_DIM
            half = slice(h * 2 * BLK, (h + 1) * 2 * BLK)
            p = jnp.exp(jnp.where(mask, s2[:, half], NEG) - l_ref[:, at:at + 1])
            ps.append(p.astype(BF16))
            dss.append((p * (dp2[:, half] - c_ref[:, at:at + 1])).astype(BF16))
        ds_both = jnp.concatenate(dss, axis=1)
        dq_ref[:, cs] = _dot_nn(ds_both, k_bd) * (HEAD_DIM ** -0.5)
        dk_bd = _dot_tn(ds_both, q2)
        dv_bd = _dot_tn(jnp.concatenate(ps, axis=1), do2)
        dk2 = jnp.where(left2, dk_bd[:2 * BLK], dk_bd[2 * BLK:])
        dv2 = jnp.where(left2, dv_bd[:2 * BLK], dv_bd[2 * BLK:])
        dk_ref[here, cs] = dk2[BLK:]
        dv_ref[here, cs] = dv2[BLK:]
        dk_ref[before, cs] += dk2[:BLK]
        dv_ref[before, cs] += dv2[:BLK]


def _attn_bwd(preps, dos, lses, cterms):
    def body(*refs):
        t = pl.program_id(0)
        for gi in range(3):
            _attn_bwd_block(_class_and_block(gi, t)[1], *refs[8 * gi:8 * gi + 8], *refs[24 + 3 * gi:27 + 3 * gi])

    in_specs, out_specs, out_shape, args = [], [], [], []
    for gi, d in enumerate(PATTERN_DILATIONS):
        cur, prev, whole = _attn_specs(gi)
        qr, kr, vv = preps[gi]
        in_specs += [cur, cur, prev, cur, prev, cur, cur, cur]
        args += [qr, kr, kr, vv, vv, dos[gi], lses[gi], cterms[gi]]
        out_specs += [cur, whole, whole]
        out_shape += [jax.ShapeDtypeStruct((S // d, d * GW), F32)] * 3
    outs = pl.pallas_call(
        body,
        name="attn_bwd",
        grid=(N_ATTN_STEPS,),
        in_specs=in_specs,
        out_specs=out_specs,
        out_shape=out_shape,
        compiler_params=_cp(("arbitrary",), 56),
    )(*args)
    return [tuple(outs[3 * gi:3 * gi + 3]) for gi in range(3)]


def _qkv_unprep(grads, cos_t, sin_t, dproj):
    def body(*refs):
        views, (c_ref, s_ref, _, out_ref), scr = refs[:9], refs[9:13], refs[13:]
        lo = _lane_lo(ROWS)
        cf, sf = _tile4(c_ref[...]), _tile4(s_ref[...])
        for gi, d in enumerate(PATTERN_DILATIONS):
            dq_ref, dk_ref, dv_ref = views[3 * gi:3 * gi + 3]
            at = 3 * gi * GW
            out_ref[:, at:at + GW] = _unrope(_from_view(scr, dq_ref, d), cf, sf, lo).astype(BF16)
            out_ref[:, at + GW:at + 2 * GW] = _unrope(_from_view(scr, dk_ref, d), cf, sf, lo).astype(BF16)
            out_ref[:, at + 2 * GW:at + 3 * GW] = _from_view(scr, dv_ref, d).astype(BF16)

    tab = pl.BlockSpec((ROWS, BLK), lambda i: (i, 0))
    return pl.pallas_call(
        body,
        name="qkv_unprep",
        grid=(S // ROWS,),
        in_specs=[_view_spec(d) for d in PATTERN_DILATIONS for _ in range(3)] + [tab, tab, pl.BlockSpec(memory_space=pl.ANY)],
        out_specs=pl.BlockSpec((ROWS, QKV_W), lambda i: (i, 0)),
        out_shape=jax.ShapeDtypeStruct(dproj.shape, dproj.dtype),
        input_output_aliases={11: 0},
        scratch_shapes=_VIEW_SCRATCH,
        compiler_params=_cp(("parallel",), 48),
    )(*[g for grp in grads for g in grp], cos_t, sin_t, dproj)


def _group_weights(l0, l1, l2):
    mx = jnp.maximum(jnp.maximum(l0, l1), l2)
    e0, e1, e2 = jnp.exp(l0 - mx), jnp.exp(l1 - mx), jnp.exp(l2 - mx)
    inv = 1.0 / (e0 + e1 + e2)
    return e0 * inv, e1 * inv, e2 * inv


def _combine_fwd(os_, lses):
    def body(o0, o1, o2, l0, l1, l2, y_ref, *scr):
        ov = [_from_view(scr, o, d) for o, d in zip((o0, o1, o2), PATTERN_DILATIONS)]
        lv = [_from_view(scr, l, d) for l, d in zip((l0, l1, l2), PATTERN_DILATIONS)]
        w0, w1, w2 = _group_weights(*lv)
        y_ref[...] = (w0 * ov[0] + w1 * ov[1] + w2 * ov[2]).astype(BF16)

    views = [_view_spec(d) for d in PATTERN_DILATIONS]
    return pl.pallas_call(
        body,
        name="attn_combine_fwd",
        grid=(S // ROWS,),
        in_specs=views + views,
        out_specs=pl.BlockSpec((ROWS, GW), lambda i: (i, 0)),
        out_shape=jax.ShapeDtypeStruct((S, GW), BF16),
        scratch_shapes=_VIEW_SCRATCH,
        compiler_params=_cp(("parallel",)),
    )(*os_, *lses)


def _combine_bwd(dy, os_, lses, seg):
    def body(dy_ref, o0, o1, o2, l0, l1, l2, seg_ref, d0, d1, d2, c0, c1, c2, *scr):
        ov = [_from_view(scr, o, d) for o, d in zip((o0, o1, o2), PATTERN_DILATIONS)]
        lv = [_from_view(scr, l, d) for l, d in zip((l0, l1, l2), PATTERN_DILATIONS)]
        ws = _group_weights(*lv)
        dyv = dy_ref[...]
        t = dyv * (ws[0] * ov[0] + ws[1] * ov[1] + ws[2] * ov[2])
        t_hi = t.astype(BF16)
        r1 = t - t_hi.astype(F32)
        t_mid = r1.astype(BF16)
        t_lo = (r1 - t_mid.astype(F32)).astype(BF16)
        sg = seg_ref[...]
        e = _dot_nn(t_hi, sg) + _dot_nn(t_mid, sg) + _dot_nn(t_lo, sg)
        for w, do_ref, c_ref, d in zip(ws, (d0, d1, d2), (c0, c1, c2), PATTERN_DILATIONS):
            _to_view(scr, w * dyv, do_ref, d, BF16)
            _to_view(scr, w * e, c_ref, d, F32)

    views = [_view_spec(d) for d in PATTERN_DILATIONS]
    return pl.pallas_call(
        body,
        name="attn_combine_bwd",
        grid=(S // ROWS,),
        in_specs=[pl.BlockSpec((ROWS, GW), lambda i: (i, 0))] + views + views + [pl.BlockSpec((GW, GW), lambda i: (0, 0))],
        out_specs=views + views,
        out_shape=[jax.ShapeDtypeStruct((S // d, d * GW), BF16) for d in PATTERN_DILATIONS]
        + [jax.ShapeDtypeStruct((S // d, d * GW), F32) for d in PATTERN_DILATIONS],
        scratch_shapes=_VIEW_SCRATCH,
        compiler_params=_cp(("parallel",)),
    )(dy, *os_, *lses, seg)


_SQRT_HALF = 0.7071067811865476
_INV_SQRT_2PI = 0.3989422804014327


def _gelu(z):
    return 0.5 * z * (1.0 + lax.erf(z * _SQRT_HALF))


def _gelu_grad(z):
    return 0.5 * (1.0 + lax.erf(z * _SQRT_HALF)) + z * (_INV_SQRT_2PI * jnp.exp(-0.5 * z * z))


def _tril_ws(ws_ref, g):
    t = lax.broadcasted_iota(jnp.int32, (BLK, BLK), 0)
    s = lax.broadcasted_iota(jnp.int32, (BLK, BLK), 1)
    return jnp.where(t >= s, ws_ref[g], 0.0)


def _chunks_side_by_side(x, cols, nch):
    return jnp.concatenate([x[c * BLK:(c + 1) * BLK, cols] for c in range(nch)], axis=1)


def _gmlp_fwd(z, ws, bst, ln_g, ln_b, tm=512):
    nch = tm // BLK

    def body(z_ref, ws_ref, b_ref, g_ref, be_ref, y_ref):
        zg = _gelu(z_ref[...].astype(F32))
        u = zg[:, :D]
        vn, _, _ = _ln_fwd(zg[:, D:], g_ref[...], be_ref[...])
        vnb = vn.astype(BF16)
        bt = b_ref[...]
        for g in range(8):
            w = _tril_ws(ws_ref, g).astype(BF16)
            cols = slice(g * BLK, (g + 1) * BLK)
            mixed = _dot_nn(w, _chunks_side_by_side(vnb, cols, nch)) + bt[:, g:g + 1]
            for c in range(nch):
                rows = slice(c * BLK, (c + 1) * BLK)
                y_ref[rows, cols] = (u[rows, cols] * mixed[:, c * BLK:(c + 1) * BLK]).astype(BF16)

    return pl.pallas_call(
        body,
        name="gmlp_fwd",
        grid=(S // tm,),
        in_specs=[
            pl.BlockSpec((tm, 2 * D), lambda i: (i, 0)),
            pl.BlockSpec((8, BLK, BLK), lambda i: (0, 0, 0)),
            pl.BlockSpec((BLK, 8), lambda i: (0, 0)),
            pl.BlockSpec((1, D), lambda i: (0, 0)),
            pl.BlockSpec((1, D), lambda i: (0, 0)),
        ],
        out_specs=pl.BlockSpec((tm, D), lambda i: (i, 0)),
        out_shape=jax.ShapeDtypeStruct((S, D), BF16),
        compiler_params=_cp(("parallel",), 48),
    )(z, ws, bst, ln_g, ln_b)


def _gmlp_bwd(z, dy, ws, bst, ln_g, ln_b, dproj, tm=512):
    nch = tm // BLK

    def body(z_ref, dy_ref, ws_ref, b_ref, g_ref, be_ref, _, dz_ref, dws_ref, dbs_ref, dg_ref, dbe_ref, dvn_scr, dm_acc):
        i = pl.program_id(0)
        zv = z_ref[...].astype(F32)
        zg = _gelu(zv)
        u = zg[:, :D]
        gam = g_ref[...]
        vn, xhat, rstd = _ln_fwd(zg[:, D:], gam, be_ref[...])
        vnb = vn.astype(BF16)
        dyv = dy_ref[...]
        dmix = dyv * u
        dmb = dmix.astype(BF16)
        bt = b_ref[...]
        tmask = lax.broadcasted_iota(jnp.int32, (BLK, BLK), 0) >= lax.broadcasted_iota(jnp.int32, (BLK, BLK), 1)
        dm_sum = dmix[0:BLK]
        for c in range(1, nch):
            dm_sum = dm_sum + dmix[c * BLK:(c + 1) * BLK]

        @pl.when(i == 0)
        def _():
            dm_acc[...] = jnp.zeros_like(dm_acc)
            dws_ref[...] = jnp.zeros_like(dws_ref)
            dg_ref[...] = jnp.zeros_like(dg_ref)
            dbe_ref[...] = jnp.zeros_like(dbe_ref)

        dm_acc[...] += dm_sum
        dus = []
        for g in range(8):
            w = _tril_ws(ws_ref, g).astype(BF16)
            cols = slice(g * BLK, (g + 1) * BLK)
            v_cat = _chunks_side_by_side(vnb, cols, nch)
            dm_cat = _chunks_side_by_side(dmb, cols, nch)
            mixed = _dot_nn(w, v_cat) + bt[:, g:g + 1]
            dus.append(jnp.concatenate(
                [dyv[c * BLK:(c + 1) * BLK, cols] * mixed[:, c * BLK:(c + 1) * BLK] for c in range(nch)], axis=0))
            dws_ref[g] += jnp.where(tmask, _dot_nt(dm_cat, v_cat), 0.0)
            dvn_cat = _dot_tn(w, dm_cat)
            for c in range(nch):
                dvn_scr[c * BLK:(c + 1) * BLK, cols] = dvn_cat[:, c * BLK:(c + 1) * BLK]

        dvn = dvn_scr[...]
        dg_ref[...] += _colsum(dvn * xhat)
        dbe_ref[...] += _colsum(dvn)
        dvg = _ln_bwd(dvn, xhat, rstd, gam)
        gp = _gelu_grad(zv)
        dz_ref[:, :D] = (jnp.concatenate(dus, axis=1) * gp[:, :D]).astype(BF16)
        dz_ref[:, D:] = (dvg * gp[:, D:]).astype(BF16)

        @pl.when(i == S // tm - 1)
        def _():
            acc = dm_acc[...]
            for g in range(8):
                dbs_ref[:, g:g + 1] = jnp.sum(acc[:, g * BLK:(g + 1) * BLK], axis=1, keepdims=True)

    vec = pl.BlockSpec((1, D), lambda i: (0, 0))
    return pl.pallas_call(
        body,
        name="gmlp_bwd",
        grid=(S // tm,),
        in_specs=[
            pl.BlockSpec((tm, 2 * D), lambda i: (i, 0)),
            pl.BlockSpec((tm, D), lambda i: (i, 0)),
            pl.BlockSpec((8, BLK, BLK), lambda i: (0, 0, 0)),
            pl.BlockSpec((BLK, 8), lambda i: (0, 0)),
            vec,
            vec,
            pl.BlockSpec(memory_space=pl.ANY),
        ],
        out_specs=[
            pl.BlockSpec((tm, 2 * D), lambda i: (i, DPROJ_Z_COL)),
            pl.BlockSpec((8, BLK, BLK), lambda i: (0, 0, 0)),
            pl.BlockSpec((BLK, 8), lambda i: (0, 0)),
            vec,
            vec,
        ],
        out_shape=[
            jax.ShapeDtypeStruct(dproj.shape, dproj.dtype),
            jax.ShapeDtypeStruct((8, BLK, BLK), F32),
            jax.ShapeDtypeStruct((BLK, 8), F32),
            jax.ShapeDtypeStruct((1, D), F32),
            jax.ShapeDtypeStruct((1, D), F32),
        ],
        input_output_aliases={6: 0},
        scratch_shapes=[pltpu.VMEM((tm, D), F32), pltpu.VMEM((BLK, D), F32)],
        compiler_params=_cp(("arbitrary",), 48),
    )(z, dy, ws, bst, ln_g, ln_b, dproj)


def _merge_fwd(ya, yg, glog, bgate, h1, wabt, wgb, wo, ln_g, ln_b, tm=256):
    def body(ya_ref, yg_ref, gl_ref, bg_ref, h1_ref, wab_ref, wgb_ref, wo_ref, g_ref, b_ref,
             h_ref, hb_ref, xh_ref, rs_ref, mg_ref, bra_ref, brg_ref):
        bra = _dot_nt(ya_ref[...], wab_ref[...])
        brg = _dot_nn(yg_ref[...], wgb_ref[...])
        gates = _sigmoid(gl_ref[...].astype(F32) + bg_ref[...])
        merged = (gates[:, :D] * bra + gates[:, D:] * brg).astype(BF16)
        mix = _dot_nn(merged, wo_ref[...])
        h, xhat, rstd = _ln_fwd(ALPHA * h1_ref[...] + mix, g_ref[...], b_ref[...])
        h_ref[...] = h
        hb_ref[...] = h.astype(BF16)
        xh_ref[...] = xhat
        rs_ref[...] = rstd
        mg_ref[...] = merged
        bra_ref[...] = bra
        brg_ref[...] = brg

    row = pl.BlockSpec((tm, D), lambda i: (i, 0))
    vec = pl.BlockSpec((1, D), lambda i: (0, 0))
    full = lambda shape: pl.BlockSpec(shape, lambda i: (0, 0))
    return pl.pallas_call(
        body,
        name="merge_fwd",
        grid=(S // tm,),
        in_specs=[
            pl.BlockSpec((tm, GW), lambda i: (i, 0)), row,
            pl.BlockSpec((tm, 2 * D), lambda i: (i, glog.shape[1] // (2 * D) - 1)),
            full((1, 2 * D)), row,
            full((D, GW)), full((D, D)), full((D, D)), vec, vec,
        ],
        out_specs=[row, row, row, pl.BlockSpec((tm, 1), lambda i: (i, 0)), row, row, row],
        out_shape=[
            jax.ShapeDtypeStruct((S, D), F32),
            jax.ShapeDtypeStruct((S, D), BF16),
            jax.ShapeDtypeStruct((S, D), F32),
            jax.ShapeDtypeStruct((S, 1), F32),
            jax.ShapeDtypeStruct((S, D), BF16),
            jax.ShapeDtypeStruct((S, D), F32),
            jax.ShapeDtypeStruct((S, D), F32),
        ],
        compiler_params=_cp(("parallel",), 48),
    )(ya, yg, glog, bgate, h1, wabt, wgb, wo, ln_g, ln_b)


def _merge_bwd(dh2, xhat, rstd, ln_g, bra, brg, glog, bgate, wabt, wgb, wo, tm=256):
    def body(dh_ref, xh_ref, rs_ref, g_ref, bra_ref, brg_ref, gl_ref, bg_ref, wab_ref, wgb_ref, wo_ref,
             dr_ref, drb_ref, dlog_ref, dba_ref, dbg_ref, dya_ref, dyg_ref, dbgate_ref, dg_ref, dbias_ref):
        i = pl.program_id(0)
        dh = dh_ref[...]
        xh = xh_ref[...]
        dr = _ln_bwd(dh, xh, rs_ref[...], g_ref[...])
        drb = dr.astype(BF16)
        dr_ref[...] = dr
        drb_ref[...] = drb
        dmerged = _dot_nt(drb, wo_ref[...])
        gates = _sigmoid(gl_ref[...].astype(F32) + bg_ref[...])
        g0, g1 = gates[:, :D], gates[:, D:]
        dl0 = dmerged * bra_ref[...] * g0 * (1.0 - g0)
        dl1 = dmerged * brg_ref[...] * g1 * (1.0 - g1)
        dlog_ref[:, :D] = dl0.astype(BF16)
        dlog_ref[:, D:] = dl1.astype(BF16)
        dba = (dmerged * g0).astype(BF16)
        dbg = (dmerged * g1).astype(BF16)
        dba_ref[...] = dba
        dbg_ref[...] = dbg
        dya_ref[...] = _dot_nn(dba, wab_ref[...])
        dyg_ref[...] = _dot_nt(dbg, wgb_ref[...])
        s0, s1 = _colsum(dl0), _colsum(dl1)
        sg, sb = _colsum(dh * xh), _colsum(dh)

        @pl.when(i == 0)
        def _():
            dbgate_ref[:, :D] = s0
            dbgate_ref[:, D:] = s1
            dg_ref[...] = sg
            dbias_ref[...] = sb

        @pl.when(i > 0)
        def _():
            dbgate_ref[:, :D] += s0
            dbgate_ref[:, D:] += s1
            dg_ref[...] += sg
            dbias_ref[...] += sb

    row = pl.BlockSpec((tm, D), lambda i: (i, 0))
    vec = pl.BlockSpec((1, D), lambda i: (0, 0))
    wide = pl.BlockSpec((tm, 2 * D), lambda i: (i, 0))
    full = lambda shape: pl.BlockSpec(shape, lambda i: (0, 0))
    return pl.pallas_call(
        body,
        name="merge_bwd",
        grid=(S // tm,),
        in_specs=[row, row, pl.BlockSpec((tm, 1), lambda i: (i, 0)), vec, row, row,
                  pl.BlockSpec((tm, 2 * D), lambda i: (i, glog.shape[1] // (2 * D) - 1)),
                  full((1, 2 * D)), full((D, GW)), full((D, D)), full((D, D))],
        out_specs=[row, row, pl.BlockSpec((tm, 2 * D), lambda i: (i, DPROJ_G_COL)), row, row,
                   pl.BlockSpec((tm, GW), lambda i: (i, 0)), row, full((1, 2 * D)), vec, vec],
        out_shape=[
            jax.ShapeDtypeStruct((S, D), F32),
            jax.ShapeDtypeStruct((S, D), BF16),
            jax.ShapeDtypeStruct((S, DPROJ_W), BF16),
            jax.ShapeDtypeStruct((S, D), BF16),
            jax.ShapeDtypeStruct((S, D), BF16),
            jax.ShapeDtypeStruct((S, GW), F32),
            jax.ShapeDtypeStruct((S, D), F32),
            jax.ShapeDtypeStruct((1, 2 * D), F32),
            jax.ShapeDtypeStruct((1, D), F32),
            jax.ShapeDtypeStruct((1, D), F32),
        ],
        compiler_params=_cp(("arbitrary",), 48),
    )(dh2, xhat, rstd, ln_g, bra, brg, glog, bgate, wabt, wgb, wo)


def _tie(x, dep):
    if dep is None:
        return x
    return x + dep[0, 0].astype(x.dtype)


def _local_step(x, pos_col, target, get_w, p, emit):
    w = get_w("ffn1", None)
    a1, b1, hm1 = _ffn_up(x, w["g1"], w["u1"], "ffn1_up")
    w.update(get_w("ffn1d", hm1))
    h1, h1b, xh1, rs1 = _ffn_down(hm1, w["d1"], x, p["ln1_g"], p["ln1_b"], "ffn1_down")

    w.update(get_w("win", h1b))
    qkv = _matmul(h1b, w["win"], "nt", BF16, S, 1536, D, "proj_qkv", b_off=0, n_out=QKV_W)
    z = glog = _matmul(h1b, w["win"], "nt", BF16, S, 512, D, "proj_zg", b_off=QKV_W // 512, n_out=4 * D)

    half = jnp.arange(0, HEAD_DIM, 2, dtype=F32) / HEAD_DIM
    inv_freq = ROPE_THETA ** (-half)
    invf = jnp.tile(inv_freq, 4).reshape(1, BLK)
    sign = jnp.tile(jnp.concatenate([-jnp.ones((32,), F32), jnp.ones((32,), F32)]), 2).reshape(1, BLK)
    cos_t, sin_t = _rope_tables(pos_col, invf, sign)

    preps = _qkv_prep(qkv, cos_t, sin_t)
    os_, lses = _attn_fwd(preps)
    ya = _combine_fwd(os_, lses)
    get_w("late", ya, early=True)
    bst = p["gmlp_b_s"].T
    yg = _gmlp_fwd(z, p["gmlp_w_s"], bst, p["gmlp_ln_g"], p["gmlp_ln_b"])
    w.update(get_w("late", yg))
    h2, h2b, xh2, rs2, merged, bra, brg = _merge_fwd(ya, yg, glog, p["b_gates"], h1, w["ab"], w["gb"], w["o"],
                                                      p["ln2_g"], p["ln2_b"])
    a2, b2, hm2 = _ffn_up(h2b, w["g2"], w["u2"], "ffn2_up")
    dh3, loss, xh3, rs3 = _ffn_down(hm2, w["d2"], h2, p["ln3_g"], p["ln3_b"], "ffn2_down", target=target)

    gp = {}
    dr3, df2, da2, db2, gp["ln3_g"], gp["ln3_b"] = _ffn_bwd_mid(dh3, None, xh3, rs3, p["ln3_g"], a2, b2, w["d2"],
                                                                "ffn2_bwd_mid")
    tok = emit("ffn2", {
        "g2": _matmul(da2, h2b, "tn", BF16, 1408, D, S, "wgrad_g2"),
        "u2": _matmul(db2, h2b, "tn", BF16, 1408, D, S, "wgrad_u2"),
        "d2": _matmul(hm2, df2, "tn", BF16, 1408, D, S, "wgrad_d2")})
    dh2 = _ffn_bwd_dx(dr3, da2, db2, w["g2"], w["u2"], "ffn2_bwd_dx")

    (dr2, dr2b, dproj, dba, dbg, dya, dyg, gp["b_gates"], gp["ln2_g"], gp["ln2_b"]) = _merge_bwd(
        dh2, xh2, rs2, _tie(p["ln2_g"], tok), bra, brg, glog, p["b_gates"], w["ab"], w["gb"], w["o"])
    tok = emit("mix", {
        "o": _matmul(merged, dr2b, "tn", BF16, 512, D, S, "wgrad_o"),
        "ab": _matmul(dba, ya, "tn", BF16, 512, GW, S, "wgrad_ab"),
        "gb": _matmul(yg, dbg, "tn", BF16, 512, D, S, "wgrad_gb")})

    seg = (jnp.arange(GW)[:, None] // HEAD_DIM == jnp.arange(GW)[None, :] // HEAD_DIM).astype(BF16)
    do0, do1, do2, c0, c1, c2 = _combine_bwd(dya, os_, lses, _tie(seg, tok))
    dproj = _qkv_unprep(_attn_bwd(preps, (do0, do1, do2), lses, (c0, c1, c2)), cos_t, sin_t, dproj)
    dproj, gp["gmlp_w_s"], dbst, gp["gmlp_ln_g"], gp["gmlp_ln_b"] = _gmlp_bwd(
        z, dyg, p["gmlp_w_s"], bst, p["gmlp_ln_g"], p["gmlp_ln_b"], dproj)
    gp["gmlp_b_s"] = dbst.T
    tok = emit("win", {"win": _matmul(dproj, h1b, "tn", BF16, 512, D, S, "wgrad_win", a_map=_dproj_tile, m_out=IN_W)})
    dh1m = _matmul(dproj, w["win"], "nn", F32, S, D, 512, "dproj_to_dh1", dep=tok, a_map=_dproj_tile)

    dr1, df1, da1, db1, gp["ln1_g"], gp["ln1_b"] = _ffn_bwd_mid(dr2, dh1m, xh1, rs1, p["ln1_g"], a1, b1, w["d1"],
                                                                "ffn1_bwd_mid")
    tok = emit("small", {**gp, "loss": loss})
    tok = emit("g1", {"g1": _matmul(da1, x, "tn", BF16, 1408, D, S, "wgrad_g1", dep=tok)})
    tok = emit("ud1", {"u1": _matmul(db1, x, "tn", BF16, 1408, D, S, "wgrad_u1", dep=tok),
                       "d1": _matmul(hm1, df1, "tn", BF16, 1408, D, S, "wgrad_d1", dep=tok)})
    dx = _ffn_bwd_dx(dr1, da1, db1, w["g1"], w["u1"], "ffn1_bwd_dx", dep=tok)
    return loss, dx, gp


_FLIPS = [(mx, my, mc) for mx in (0, 1) for my in (0, 1) for mc in (0, 1)][1:]
_GROUPS = {"ffn1": ("g1", "u1"), "ffn1d": ("d1",), "win": ("win",), "late": ("ab", "gb", "o", "g2", "u2", "d2")}
_SCATTERS = {"ffn2": ("g2", "u2", "d2"), "mix": ("ab", "gb", "o"), "win": ("win",), "g1": ("g1",), "ud1": ("u1", "d1")}
_TWO_STAGE = ("g1", "ud1")
_HBM = pl.BlockSpec(memory_space=pltpu.HBM)
_SEM = pl.BlockSpec(memory_space=pltpu.SEMAPHORE)
_EFFECT = pltpu.SideEffectType.DATAFLOW_SIDE_EFFECTING


def _me():
    return 4 * lax.axis_index("x") + 2 * lax.axis_index("y") + lax.axis_index("c")


def _copies_start(bufs, copies, n_sem, name, after=None):
    nb = len(bufs)
    n_after = 0 if after is None else 1

    def body(*refs):
        b = refs[:nb]
        send_sems, recv_sems = refs[nb + n_after], refs[nb + n_after + 1]
        token = refs[-1]
        x, y, c = lax.axis_index("x"), lax.axis_index("y"), lax.axis_index("c")
        me = 4 * x + 2 * y + c
        for si, s_slot, di, d_slot, (mx, my, mc), sem in copies:
            s_idx, d_idx = s_slot(me), d_slot(me)
            pltpu.make_async_remote_copy(
                src_ref=b[si] if s_idx is None else b[si].at[s_idx],
                dst_ref=b[di] if d_idx is None else b[di].at[d_idx],
                send_sem=send_sems.at[sem], recv_sem=recv_sems.at[sem],
                device_id=(x ^ mx, y ^ my, c ^ mc), device_id_type=MESH).start()
        token[...] = jnp.zeros_like(token)

    ins = [pltpu.with_memory_space_constraint(a, pltpu.HBM) for a in bufs]
    outs = pl.pallas_call(
        body,
        name=name,
        in_specs=[_HBM] * nb + [pl.BlockSpec(memory_space=pl.ANY)] * n_after,
        out_specs=[_SEM, _SEM] + [_HBM] * nb + [pl.BlockSpec(memory_space=pltpu.VMEM)],
        out_shape=[pltpu.SemaphoreType.DMA((n_sem,)), pltpu.SemaphoreType.DMA((n_sem,))]
        + [pltpu.HBM(a.shape, a.dtype) for a in ins] + [jax.ShapeDtypeStruct((8, 128), F32)],
        input_output_aliases={k: 2 + k for k in range(nb)},
        compiler_params=pltpu.CompilerParams(has_side_effects=_EFFECT),
    )(*ins, *([] if after is None else [after]))
    return outs[0], outs[1], list(outs[2:2 + nb]), outs[-1]


def _copies_wait(started, waits, after, name):
    send_sems, recv_sems, bufs, _ = started
    nb = len(bufs)

    def body(*refs):
        b = refs[:nb]
        ss, rs = refs[nb], refs[nb + 1]
        me3 = (lax.axis_index("x"), lax.axis_index("y"), lax.axis_index("c"))
        for bi, n_blocks, sem, is_send in waits:
            blocks = b[bi].at[pl.ds(0, n_blocks)]
            cp = pltpu.make_async_remote_copy(src_ref=blocks, dst_ref=blocks, send_sem=ss.at[sem], recv_sem=rs.at[sem],
                                              device_id=me3, device_id_type=MESH)
            if is_send:
                cp.wait_send()
            else:
                cp.wait_recv()

    return pl.pallas_call(
        body,
        name=name,
        in_specs=[_HBM] * nb + [_SEM, _SEM, pl.BlockSpec(memory_space=pl.ANY)],
        out_specs=[_HBM] * nb,
        out_shape=[pltpu.HBM(a.shape, a.dtype) for a in bufs],
        input_output_aliases={k: k for k in range(nb)},
        compiler_params=pltpu.CompilerParams(has_side_effects=_EFFECT),
    )(*bufs, send_sems, recv_sems, after)


def _landing(own, me):
    return lax.dynamic_update_slice(lax.empty((N_DEV,) + own.shape[1:], own.dtype), own, (me, 0, 0))


_SIBLING = (0, 0, 1)
_OTHER_CHIPS = ((1, 0, 0), (0, 1, 0), (1, 1, 0))


def _bits(flip):
    return 4 * flip[0] + 2 * flip[1] + flip[2]


def _gather_send(shards, grp, after):
    nw = len(shards)
    me = _me()
    bufs = list(shards) + [_landing(a[None], me) for a in shards]
    copies = []
    for k in range(nw):
        copies.append((k, lambda me: None, nw + k, lambda me: me, _SIBLING, nw + k))
        copies += [(k, lambda me: None, nw + k, lambda me: me, f, k) for f in _OTHER_CHIPS]
    return _copies_start(bufs, copies, 2 * nw, "gather_send_" + grp, after)


def _gather_pass(started, grp, after):
    nw = len(started[2]) // 2
    waits = []
    for k in range(nw):
        waits += [(nw + k, 3, k, True), (nw + k, 1, nw + k, True), (nw + k, 3, k, False), (nw + k, 1, nw + k, False)]
    lands = list(_copies_wait(started, waits, after, "gather_arrived_" + grp)[nw:])
    copies = []
    for k in range(nw):
        for f in _OTHER_CHIPS:
            slot = functools.partial(lambda me, bits: me ^ bits, bits=_bits(f))
            copies.append((k, slot, k, slot, _SIBLING, k))
    return _copies_start(lands, copies, nw, "gather_pass_" + grp)


def _gather_finish(passed, grp, after):
    nw = len(passed[2])
    waits = [(k, 3, k, is_send) for k in range(nw) for is_send in (True, False)]
    return _copies_wait(passed, waits, passed[3] if after is None else after, "gather_done_" + grp)


def _scatter_send(parts, grp, after=None):
    nw = len(parts)
    me = _me()
    bufs = list(parts) + [_landing(lax.dynamic_index_in_dim(a, me, 0, keepdims=True), me) for a in parts]
    copies = []
    for k in range(nw):
        for f in _FLIPS:
            to = functools.partial(lambda me, bits: me ^ bits, bits=_bits(f))
            copies.append((k, to, nw + k, lambda me: me, f, k))
    return _copies_start(bufs, copies, nw, "scatter_send_" + grp, after)


def _scatter_finish(started, grp, after):
    nw = len(started[2]) // 2
    waits = [(nw + k, N_DEV - 1, k, is_send) for k in range(nw) for is_send in (True, False)]
    return _copies_wait(started, waits, after, "scatter_done_" + grp)[nw:]


N_CHIP = N_DEV // 2


def _pair_sum(part, other, key):
    _, r, c = part.shape
    tm = _ADAM_ROWS.get(r, r)
    core = lax.axis_index("c").astype(jnp.int32).reshape(1)

    def body(core_ref, a_ref, b_ref, o_ref):
        o_ref[...] = (a_ref[...].astype(F32) + b_ref[...].astype(F32)).astype(BF16)

    return pl.pallas_call(
        body,
        name="pair_sum_" + key,
        grid_spec=pltpu.PrefetchScalarGridSpec(
            num_scalar_prefetch=1,
            grid=(N_CHIP, r // tm),
            in_specs=[pl.BlockSpec((None, tm, c), lambda q, i, core_ref: (2 * q + core_ref[0], i, 0)),
                      pl.BlockSpec((None, tm, c), lambda q, i, core_ref: (q, i, 0))],
            out_specs=pl.BlockSpec((None, tm, c), lambda q, i, core_ref: (q, i, 0)),
        ),
        out_shape=jax.ShapeDtypeStruct((N_CHIP, r, c), BF16),
        compiler_params=_cp(("parallel", "parallel")),
    )(core, part, other)


def _scatter2_send(parts, grp, keys):
    nw = len(parts)
    chip = _me() // 2
    swap = []
    for k in range(nw):
        for q in range(N_CHIP):
            src = functools.partial(lambda me, q: 2 * q + 1 - me % 2, q=q)
            swap.append((k, src, nw + k, functools.partial(lambda me, q: q, q=q), _SIBLING, k))
    others = [lax.empty((N_CHIP,) + a.shape[1:], a.dtype) for a in parts]
    started = _copies_start(list(parts) + others, swap, nw, "scatter_swap_" + grp)
    waits = [(nw + k, N_CHIP, k, is_send) for k in range(nw) for is_send in (True, False)]
    swapped = _copies_wait(started, waits, started[3], "scatter_swapped_" + grp)
    pairs = [_pair_sum(swapped[k], swapped[nw + k], keys[k]) for k in range(nw)]
    lands = [lax.dynamic_update_slice(lax.empty(p.shape, p.dtype), lax.dynamic_index_in_dim(p, chip, 0, keepdims=True),
                                      (chip, 0, 0)) for p in pairs]
    copies = []
    for k in range(nw):
        for f in _OTHER_CHIPS:
            to = functools.partial(lambda me, bits: (me ^ bits) // 2, bits=_bits(f))
            copies.append((k, to, nw + k, lambda me: me // 2, f, k))
    return _copies_start(pairs + lands, copies, nw, "scatter_send_" + grp)


def _scatter2_finish(started, grp, after):
    nw = len(started[2]) // 2
    waits = [(nw + k, N_CHIP - 1, k, is_send) for k in range(nw) for is_send in (True, False)]
    return _copies_wait(started, waits, after, "scatter_done_" + grp)[nw:]


def _allgather_send(block, name, after=None):
    me = _me()
    copies = [(0, lambda me: None, 1, lambda me: me, f, 0) for f in _FLIPS]
    return _copies_start([block, _landing(block[None], me)], copies, 1, name, after)


def _allgather_finish(started, name, after):
    waits = [(1, N_DEV - 1, 0, True), (1, N_DEV - 1, 0, False)]
    return _copies_wait(started, waits, after, name)[1]


_ADAM_ROWS = {352: 176, 1088: 272}


def _sum_adamw(parts, wv, m, v, name):
    n_parts, r, c = parts.shape
    tm = _ADAM_ROWS.get(r, r)
    assert r % tm == 0 and wv.shape == (r, c)

    def body(p_ref, w_ref, m_ref, v_ref, g_ref, d_ref, mo_ref, vo_ref):
        gv = p_ref[0].astype(F32)
        for j in range(1, n_parts):
            gv = gv + p_ref[j].astype(F32)
        g_ref[...] = gv
        mn = ADAM_B1 * m_ref[...] + (1.0 - ADAM_B1) * gv
        vn = ADAM_B2 * v_ref[...] + (1.0 - ADAM_B2) * (gv * gv)
        m_hat = mn / (1.0 - ADAM_B1 ** ADAM_STEP)
        v_hat = vn / (1.0 - ADAM_B2 ** ADAM_STEP)
        d_ref[...] = -ADAM_LR * (m_hat / (jnp.sqrt(v_hat) + ADAM_EPS) + ADAM_WD * w_ref[...])
        mo_ref[...] = mn
        vo_ref[...] = vn

    sp = pl.BlockSpec((tm, c), lambda i: (i, 0))
    return pl.pallas_call(
        body,
        name=name,
        grid=(r // tm,),
        in_specs=[pl.BlockSpec((n_parts, tm, c), lambda i: (0, i, 0))] + [sp] * 3,
        out_specs=[sp] * 4,
        out_shape=[jax.ShapeDtypeStruct((r, c), F32)] * 4,
        compiler_params=_cp(("parallel",), 48),
    )(parts, wv, m, v)


_WEIGHTS = ["ffn1_w_gate", "ffn1_w_up", "ffn1_w_down", "ln1_g", "ln1_b", "w_in", "b_gates", "gmlp_ln_g", "gmlp_ln_b",
            "gmlp_w_s", "gmlp_b_s", "w_attn_branch", "w_gmlp_branch", "w_out", "ln2_g", "ln2_b", "ffn2_w_gate",
            "ffn2_w_up", "ffn2_w_down", "ln3_g", "ln3_b"]
_BIG_OF = {"ffn1_w_gate": ("g1", True), "ffn1_w_up": ("u1", True), "ffn1_w_down": ("d1", False), "w_in": ("win", True),
           "w_attn_branch": ("ab", True), "w_gmlp_branch": ("gb", False), "w_out": ("o", False),
           "ffn2_w_gate": ("g2", True), "ffn2_w_up": ("u2", True), "ffn2_w_down": ("d2", False)}
_SMALL = [n for n in _WEIGHTS if n not in _BIG_OF]
_SMALL_ROWS = {"gmlp_w_s": 128, "b_gates": 2}
_SMALL_SLOT = 8


def _pack_small(d, last=None):
    rows = []
    for n in _SMALL:
        r = d[n].reshape(-1, D)
        slot = max(r.shape[0], _SMALL_SLOT)
        rows.append(jnp.pad(r, ((0, slot - r.shape[0]), (0, 0))))
    rows.append(jnp.zeros((_SMALL_SLOT, D), F32) if last is None else jnp.broadcast_to(last.reshape(1, 1), (_SMALL_SLOT, D)))
    return jnp.concatenate(rows, axis=0)


def _unpack_small(packed, shapes):
    out, at = {}, 0
    for n in _SMALL:
        k = _SMALL_ROWS.get(n, 1)
        out[n] = packed[at:at + k].reshape(shapes[n])
        at += max(k, _SMALL_SLOT)
    return out


def kernel(x, positions, ffn1_w_gate, ffn1_w_up, ffn1_w_down, ln1_g, ln1_b, w_in, b_gates, gmlp_ln_g, gmlp_ln_b, gmlp_w_s, gmlp_b_s, w_attn_branch, w_gmlp_branch, w_out, ln2_g, ln2_b, ffn2_w_gate, ffn2_w_up, ffn2_w_down, ln3_g, ln3_b, loss_target, m_ffn1_w_gate, m_ffn1_w_up, m_ffn1_w_down, m_ln1_g, m_ln1_b, m_w_in, m_b_gates, m_gmlp_ln_g, m_gmlp_ln_b, m_gmlp_w_s, m_gmlp_b_s, m_w_attn_branch, m_w_gmlp_branch, m_w_out, m_ln2_g, m_ln2_b, m_ffn2_w_gate, m_ffn2_w_up, m_ffn2_w_down, m_ln3_g, m_ln3_b, v_ffn1_w_gate, v_ffn1_w_up, v_ffn1_w_down, v_ln1_g, v_ln1_b, v_w_in, v_b_gates, v_gmlp_ln_g, v_gmlp_ln_b, v_gmlp_w_s, v_gmlp_b_s, v_w_attn_branch, v_w_gmlp_branch, v_w_out, v_ln2_g, v_ln2_b, v_ffn2_w_gate, v_ffn2_w_up, v_ffn2_w_down, v_ln3_g, v_ln3_b):
    args = dict(locals())
    wts = {n: args[n] for n in _WEIGHTS}
    ms = {n: args["m_" + n] for n in _WEIGHTS}
    vs = {n: args["v_" + n] for n in _WEIGHTS}

    name_of = {key: (n, tr) for n, (key, tr) in _BIG_OF.items()}

    started, tok = {}, None
    for grp, keys in _GROUPS.items():
        shards = []
        for key in keys:
            n, tr = name_of[key]
            s2 = wts[n][0]
            shards.append(_tie(s2.T if tr else s2, tok).astype(BF16))
        started[grp] = _gather_send(shards, grp, tok)
        tok = started[grp][3]
    all_started = tok
    passed = {}

    def get_w(grp, after, early=False):
        if grp not in passed:
            passed[grp] = _gather_pass(started[grp], grp, all_started if after is None else after)
            after = None
        if early:
            return None
        lands = _gather_finish(passed[grp], grp, after)
        return {key: g.reshape(-1, g.shape[-1]) for key, g in zip(_GROUPS[grp], lands)}

    sent = {}

    def emit(grp, grads):
        if grp == "small":
            sent[grp] = _allgather_send(_pack_small(grads, last=grads["loss"]), "small_grads_send")
        else:
            parts = [grads[key].reshape(N_DEV, -1, grads[key].shape[-1]) for key in _SCATTERS[grp]]
            sent[grp] = (_scatter2_send(parts, grp, _SCATTERS[grp]) if grp in _TWO_STAGE else _scatter_send(parts, grp))
        return sent[grp][3]

    p = {n: (wts[n][0] if n in ("gmlp_w_s", "gmlp_b_s") else wts[n]) for n in _SMALL}
    loss, dx, gp = _local_step(x[0], positions.reshape(S, 1), loss_target[0], get_w, p, emit)
    grads, deltas, new_m, new_v = {}, {}, {}, {}
    after = dx
    for grp in ("ffn2", "mix", "win", "small", "g1", "ud1"):
        if grp == "small":
            parts = _allgather_finish(sent[grp], "small_grads_done", after)
            outs = _sum_adamw(parts, *[_pack_small({n: d[n] for n in _SMALL}) for d in (wts, ms, vs)], "update_small")
            shapes = {n: wts[n].shape for n in _SMALL}
            for dst, packed in zip((grads, deltas, new_m, new_v), outs):
                dst.update(_unpack_small(packed, shapes))
            loss = outs[0][-_SMALL_SLOT, 0]
            after = outs[1]
            continue
        arrived = (_scatter2_finish if grp in _TWO_STAGE else _scatter_finish)(sent[grp], grp, after)
        for key, part in zip(_SCATTERS[grp], arrived):
            n, tr = name_of[key]
            outs = _sum_adamw(part, *[(d[n][0].T if tr else d[n][0]) for d in (wts, ms, vs)], "update_" + key)
            for dst, o in zip((grads, deltas, new_m, new_v), outs):
                dst[n] = (o.T if tr else o)[None]
            after = outs[1]

    return (loss, dx[None], *[grads[n] for n in _WEIGHTS], *[deltas[n] for n in _WEIGHTS],
            *[new_m[n] for n in _WEIGHTS], *[new_v[n] for n in _WEIGHTS])
```

```python
import functools
import math

import jax
import jax.numpy as jnp
from jax import lax
from jax.experimental import pallas as pl
from jax.experimental.pallas import tpu as pltpu

F32 = jnp.float32
BF16 = jnp.bfloat16

N_DEV = 8
D = 1024
S = 2048
F = 2816
HEAD_DIM = 64
HEADS = 8
GW = HEADS * HEAD_DIM
PATTERN_DILATIONS = (1, 4, 16)
BLK = 128
QKV_W = 3 * 3 * GW
IN_W = QKV_W + 2 * D + 2 * D
DPROJ_W = 5 * 2 * D
DPROJ_Z_COL, DPROJ_G_COL = 3, 4


def _dproj_tile(t):
    three = jnp.int32(3)
    return jnp.where(t < 9, lax.rem(t, three) * 3 + lax.div(t, three), t + 3)
ROPE_THETA = 10000.0
ALPHA = 2.0 ** 0.25
LN_EPS = 1e-5
ADAM_LR, ADAM_B1, ADAM_B2, ADAM_EPS, ADAM_WD, ADAM_STEP = 0.001, 0.9, 0.999, 1e-08, 0.01, 10
NEG = -1e30
MESH = pl.DeviceIdType.MESH


def _cp(sem=None, vmem_mb=None):
    kw = {}
    if sem is not None:
        kw["dimension_semantics"] = sem
    if vmem_mb is not None:
        kw["vmem_limit_bytes"] = vmem_mb << 20
    return pltpu.CompilerParams(**kw)


def _dot_nn(a, b):
    return lax.dot_general(a, b, (((1,), (0,)), ((), ())), preferred_element_type=F32)


def _dot_nt(a, b):
    return lax.dot_general(a, b, (((1,), (1,)), ((), ())), preferred_element_type=F32)


def _dot_tn(a, b):
    return lax.dot_general(a, b, (((0,), (0,)), ((), ())), preferred_element_type=F32)


def _ln_fwd(r, g, b):
    mu = jnp.mean(r, axis=-1, keepdims=True)
    xc = r - mu
    var = jnp.mean(xc * xc, axis=-1, keepdims=True)
    rstd = lax.rsqrt(var + LN_EPS)
    xhat = xc * rstd
    return xhat * g + b, xhat, rstd


def _ln_bwd(dh, xhat, rstd, g):
    dxh = dh * g
    m1 = jnp.mean(dxh, axis=-1, keepdims=True)
    m2 = jnp.mean(dxh * xhat, axis=-1, keepdims=True)
    return rstd * (dxh - m1 - xhat * m2)


def _sigmoid(x):
    return 0.5 * jnp.tanh(0.5 * x) + 0.5


def _colsum(x):
    return jnp.sum(x, axis=0, keepdims=True)


def _matmul(a, b, mode, out_dtype, tm, tn, tk, name, b_off=0, n_out=None, dep=None, a_map=None, m_out=None):
    n_dep = 0 if dep is None else 1
    a_map = a_map or (lambda t: t)
    if mode == "nn":
        m, k = a.shape[0], b.shape[0]
        n = b.shape[1]
    elif mode == "nt":
        m, k = a.shape
        n = n_out if n_out is not None else b.shape[0]
    else:
        k, m = a.shape[0], m_out or a.shape[1]
        n = b.shape[1]
    nk = k // tk
    assert m % tm == 0 and n % tn == 0 and k % tk == 0
    dot = {"nn": _dot_nn, "nt": _dot_nt, "tn": _dot_tn}[mode]

    def body(a_ref, b_ref, *rest):
        o_ref, scr = rest[n_dep], rest[n_dep + 1:]
        r = dot(a_ref[...].astype(BF16), b_ref[...].astype(BF16))
        if nk == 1:
            o_ref[...] = r.astype(out_dtype)
        else:
            acc = scr[0]
            kk = pl.program_id(2)

            @pl.when(kk == 0)
            def _():
                acc[...] = r

            @pl.when(kk > 0)
            def _():
                acc[...] += r

            @pl.when(kk == nk - 1)
            def _():
                o_ref[...] = acc[...].astype(out_dtype)

    if mode == "nn":
        a_spec = pl.BlockSpec((tm, tk), lambda i, j, kk: (i, a_map(kk)))
        b_spec = pl.BlockSpec((tk, tn), lambda i, j, kk: (kk, j))
    elif mode == "nt":
        a_spec = pl.BlockSpec((tm, tk), lambda i, j, kk: (i, kk))
        b_spec = pl.BlockSpec((tn, tk), lambda i, j, kk: (j + b_off, kk))
    else:
        a_spec = pl.BlockSpec((tk, tm), lambda i, j, kk: (kk, a_map(i)))
        b_spec = pl.BlockSpec((tk, tn), lambda i, j, kk: (kk, j))
    return pl.pallas_call(
        body,
        name=name,
        grid=(m // tm, n // tn, nk),
        in_specs=[a_spec, b_spec] + [pl.BlockSpec(memory_space=pl.ANY)] * n_dep,
        out_specs=pl.BlockSpec((tm, tn), lambda i, j, kk: (i, j)),
        out_shape=jax.ShapeDtypeStruct((m, n), out_dtype),
        scratch_shapes=[] if nk == 1 else [pltpu.VMEM((tm, tn), F32)],
        compiler_params=_cp(("parallel", "parallel", "arbitrary"), 56),
    )(a, b, *([] if dep is None else [dep]))


def _ffn_up(x, wgt, wut, name, tm=512, tn=1408, dep=None):
    n_dep = 0 if dep is None else 1

    def body(x_ref, wg_ref, wu_ref, *rest):
        ga_ref, gb_ref, hm_ref = rest[n_dep:]
        xb = x_ref[...].astype(BF16)
        a = _dot_nt(xb, wg_ref[...])
        b = _dot_nt(xb, wu_ref[...])
        sig = _sigmoid(a)
        silu = a * sig
        ga_ref[...] = (b * (sig + silu * (1.0 - sig))).astype(BF16)
        gb_ref[...] = silu.astype(BF16)
        hm_ref[...] = (silu * b).astype(BF16)

    wsp = pl.BlockSpec((tn, D), lambda i, j: (j, 0))
    mid = pl.BlockSpec((tm, tn), lambda i, j: (i, j))
    return pl.pallas_call(
        body,
        name=name,
        grid=(S // tm, F // tn),
        in_specs=[pl.BlockSpec((tm, D), lambda i, j: (i, 0)), wsp, wsp] + [pl.BlockSpec(memory_space=pl.ANY)] * n_dep,
        out_specs=[mid, mid, mid],
        out_shape=[jax.ShapeDtypeStruct((S, F), BF16)] * 3,
        compiler_params=_cp(("parallel", "arbitrary"), 56),
    )(x, wgt, wut, *([] if dep is None else [dep]))


def _ffn_down(hm, wd, x, ln_g, ln_b, name, target=None, tm=512):
    head = target is not None

    def body(hm_ref, wd_ref, x_ref, g_ref, b_ref, *rest):
        t_ref = rest[0] if head else None
        o1_ref, o2_ref, xh_ref, rs_ref = rest[1 if head else 0:]
        r = ALPHA * x_ref[...] + 0.5 * _dot_nn(hm_ref[...], wd_ref[...])
        h, xhat, rstd = _ln_fwd(r, g_ref[...], b_ref[...])
        xh_ref[...] = xhat
        rs_ref[...] = rstd
        if head:
            e = h - t_ref[...]
            o1_ref[...] = e * (1.0 / D)
            part = jnp.sum(_colsum(e * e), axis=1, keepdims=True) * (0.5 / D)

            @pl.when(pl.program_id(0) == 0)
            def _():
                o2_ref[...] = jnp.zeros_like(o2_ref)

            o2_ref[...] += part
        else:
            o1_ref[...] = h
            o2_ref[...] = h.astype(BF16)

    row = pl.BlockSpec((tm, D), lambda i: (i, 0))
    vec = pl.BlockSpec((1, D), lambda i: (0, 0))
    second = (pl.BlockSpec((1, 1), lambda i: (0, 0)), jax.ShapeDtypeStruct((1, 1), F32)) if head else (
        row, jax.ShapeDtypeStruct((S, D), BF16))
    return pl.pallas_call(
        body,
        name=name,
        grid=(S // tm,),
        in_specs=[pl.BlockSpec((tm, F), lambda i: (i, 0)), pl.BlockSpec((F, D), lambda i: (0, 0)), row, vec, vec]
        + ([row] if head else []),
        out_specs=[row, second[0], row, pl.BlockSpec((tm, 1), lambda i: (i, 0))],
        out_shape=[
            jax.ShapeDtypeStruct((S, D), F32),
            second[1],
            jax.ShapeDtypeStruct((S, D), F32),
            jax.ShapeDtypeStruct((S, 1), F32),
        ],
        compiler_params=_cp(("arbitrary",), 56),
    )(hm, wd, x, ln_g, ln_b, *([target] if head else []))


def _ffn_bwd_mid(dh_a, dh_b, xhat, rstd, ln_g, a, b, wd, name, tm=512, tn=1408):
    two = dh_b is not None

    def body(*refs):
        dha_ref = refs[0]
        dhb_ref = refs[1] if two else None
        (xh_ref, rs_ref, g_ref, a_ref, b_ref, wd_ref,
         dr_ref, df_ref, da_ref, db_ref, dg_ref, dbias_ref, df_scr) = refs[2 if two else 1:]
        i = pl.program_id(0)
        j = pl.program_id(1)

        @pl.when(j == 0)
        def _():
            dh = dha_ref[...]
            if two:
                dh = ALPHA * dh + dhb_ref[...]
            xhat = xh_ref[...]
            dr = _ln_bwd(dh, xhat, rs_ref[...], g_ref[...])
            dfb = (0.5 * dr).astype(BF16)
            dr_ref[...] = dr
            df_scr[...] = dfb
            df_ref[...] = dfb
            sg = _colsum(dh * xhat)
            sb = _colsum(dh)

            @pl.when(i == 0)
            def _():
                dg_ref[...] = sg
                dbias_ref[...] = sb

            @pl.when(i > 0)
            def _():
                dg_ref[...] += sg
                dbias_ref[...] += sb

        dhm = _dot_nt(df_scr[...], wd_ref[...])
        da_ref[...] = (dhm * a_ref[...].astype(F32)).astype(BF16)
        db_ref[...] = (dhm * b_ref[...].astype(F32)).astype(BF16)

    row = pl.BlockSpec((tm, D), lambda i, j: (i, 0))
    vec = pl.BlockSpec((1, D), lambda i, j: (0, 0))
    mid = pl.BlockSpec((tm, tn), lambda i, j: (i, j))
    ins = [dh_a] + ([dh_b] if two else []) + [xhat, rstd, ln_g, a, b, wd]
    in_specs = [row] * (2 if two else 1) + [row, pl.BlockSpec((tm, 1), lambda i, j: (i, 0)), vec, mid, mid,
                                            pl.BlockSpec((tn, D), lambda i, j: (j, 0))]
    return pl.pallas_call(
        body,
        name=name,
        grid=(S // tm, F // tn),
        in_specs=in_specs,
        out_specs=[row, row, mid, mid, vec, vec],
        out_shape=[
            jax.ShapeDtypeStruct((S, D), F32),
            jax.ShapeDtypeStruct((S, D), BF16),
            jax.ShapeDtypeStruct((S, F), BF16),
            jax.ShapeDtypeStruct((S, F), BF16),
            jax.ShapeDtypeStruct((1, D), F32),
            jax.ShapeDtypeStruct((1, D), F32),
        ],
        scratch_shapes=[pltpu.VMEM((tm, D), BF16)],
        compiler_params=_cp(("arbitrary", "arbitrary"), 56),
    )(*ins)


def _ffn_bwd_dx(dr, da, db, wgt, wut, name, tm=512, tk=1408, dep=None):
    nk = F // tk
    n_dep = 0 if dep is None else 1

    def body(dr_ref, da_ref, db_ref, wg_ref, wu_ref, *rest):
        dx_ref, acc = rest[n_dep], rest[n_dep + 1]
        kk = pl.program_id(1)
        part = _dot_nn(da_ref[...], wg_ref[...]) + _dot_nn(db_ref[...], wu_ref[...])

        @pl.when(kk == 0)
        def _():
            acc[...] = ALPHA * dr_ref[...] + part

        @pl.when(kk > 0)
        def _():
            acc[...] += part

        @pl.when(kk == nk - 1)
        def _():
            dx_ref[...] = acc[...]

    row = pl.BlockSpec((tm, D), lambda i, kk: (i, 0))
    mid = pl.BlockSpec((tm, tk), lambda i, kk: (i, kk))
    wsp = pl.BlockSpec((tk, D), lambda i, kk: (kk, 0))
    return pl.pallas_call(
        body,
        name=name,
        grid=(S // tm, nk),
        in_specs=[row, mid, mid, wsp, wsp] + [pl.BlockSpec(memory_space=pl.ANY)] * n_dep,
        out_specs=row,
        out_shape=jax.ShapeDtypeStruct((S, D), F32),
        scratch_shapes=[pltpu.VMEM((tm, D), F32)],
        compiler_params=_cp(("parallel", "arbitrary"), 56),
    )(dr, da, db, wgt, wut, *([] if dep is None else [dep]))


def _rope_tables(pos_col, invf, sign, tm=512):
    def body(p_ref, f_ref, s_ref, c_out, s_out):
        ang = p_ref[...].astype(F32) * f_ref[...]
        c_out[...] = jnp.cos(ang)
        s_out[...] = jnp.sin(ang) * s_ref[...]

    vec = pl.BlockSpec((1, BLK), lambda i: (0, 0))
    out = pl.BlockSpec((tm, BLK), lambda i: (i, 0))
    return pl.pallas_call(
        body,
        name="rope_tables",
        grid=(S // tm,),
        in_specs=[pl.BlockSpec((tm, 1), lambda i: (i, 0)), vec, vec],
        out_specs=[out, out],
        out_shape=[jax.ShapeDtypeStruct((S, BLK), F32)] * 2,
        compiler_params=_cp(("parallel",)),
    )(pos_col, invf, sign)


def _lane_lo(rows=BLK):
    return (lax.broadcasted_iota(jnp.int32, (rows, GW), 1) % HEAD_DIM) < (HEAD_DIM // 2)


def _swap_halves(t, lo):
    return jnp.where(lo, pltpu.roll(t, GW - HEAD_DIM // 2, 1), pltpu.roll(t, HEAD_DIM // 2, 1))


def _rope(t, cosf, sinf, lo):
    return t * cosf + _swap_halves(t, lo) * sinf


def _unrope(g, cosf, sinf, lo):
    return g * cosf + _swap_halves(g * sinf, lo)


def _tile4(v):
    return jnp.concatenate([v, v, v, v], axis=1)


def _band_mask(n):
    qi = lax.broadcasted_iota(jnp.int32, (BLK, 2 * BLK), 0)
    kj = lax.broadcasted_iota(jnp.int32, (BLK, 2 * BLK), 1)
    dist = qi + BLK - kj
    return (dist >= 0) & (dist <= BLK) & ((kj >= BLK) | (n >= 1))


ROWS = 256
LANES = 128


def _to_view(scr, y, dst_ref, d, dtype, col0=0):
    if d == 1:
        dst_ref[:, col0:col0 + GW] = y.astype(dtype)
        return
    for cb in range(GW // LANES):
        scr[cb][...] = y[:, cb * LANES:(cb + 1) * LANES]
    for r in range(d):
        for cb in range(GW // LANES):
            at = col0 + r * GW + cb * LANES
            dst_ref[:, at:at + LANES] = scr[cb][pl.ds(r, ROWS // d, stride=d), :].astype(dtype)


def _from_view(scr, src_ref, d):
    if d == 1:
        return src_ref[...].astype(F32)
    for r in range(d):
        for cb in range(GW // LANES):
            at = r * GW + cb * LANES
            scr[cb][pl.ds(r, ROWS // d, stride=d), :] = src_ref[:, at:at + LANES].astype(F32)
    return jnp.concatenate([scr[cb][...] for cb in range(GW // LANES)], axis=1)


def _view_spec(d):
    return pl.BlockSpec((ROWS // d, d * GW), lambda i: (i, 0))


_VIEW_SCRATCH = [pltpu.VMEM((ROWS, LANES), F32)] * (GW // LANES)


def _qkv_prep(qkv, cos_t, sin_t):
    def body(x_ref, c_ref, s_ref, *rest):
        outs, scr = rest[:9], rest[9:]
        lo = _lane_lo(ROWS)
        cf, sf = _tile4(c_ref[...]), _tile4(s_ref[...])
        for gi, d in enumerate(PATTERN_DILATIONS):
            q, k, v = (x_ref[:, (3 * part + gi) * GW:(3 * part + gi + 1) * GW].astype(F32) for part in range(3))
            _to_view(scr, _rope(q, cf, sf, lo) * (HEAD_DIM ** -0.5), outs[3 * gi], d, BF16)
            _to_view(scr, _rope(k, cf, sf, lo), outs[3 * gi + 1], d, BF16)
            _to_view(scr, v, outs[3 * gi + 2], d, BF16)

    tab = pl.BlockSpec((ROWS, BLK), lambda i: (i, 0))
    outs = pl.pallas_call(
        body,
        name="qkv_prep",
        grid=(S // ROWS,),
        in_specs=[pl.BlockSpec((ROWS, QKV_W), lambda i: (i, 0)), tab, tab],
        out_specs=[_view_spec(d) for d in PATTERN_DILATIONS for _ in range(3)],
        out_shape=[jax.ShapeDtypeStruct((S // d, d * GW), BF16) for d in PATTERN_DILATIONS for _ in range(3)],
        scratch_shapes=_VIEW_SCRATCH,
        compiler_params=_cp(("parallel",), 48),
    )(qkv, cos_t, sin_t)
    return [tuple(outs[3 * gi:3 * gi + 3]) for gi in range(3)]


N_ATTN_STEPS = S // BLK


def _class_and_block(gi, t):
    per_class = N_ATTN_STEPS // PATTERN_DILATIONS[gi]
    return t >> (per_class.bit_length() - 1), t & (per_class - 1)


def _attn_specs(gi):
    def cur(t):
        r, n = _class_and_block(gi, t)
        return n, r

    def prev(t):
        r, n = _class_and_block(gi, t)
        return jnp.maximum(n - 1, 0), r

    def whole(t):
        return 0, _class_and_block(gi, t)[0]

    sub = S // PATTERN_DILATIONS[gi]
    return (pl.BlockSpec((BLK, GW), cur), pl.BlockSpec((BLK, GW), prev), pl.BlockSpec((sub, GW), whole))


def _left_lanes():
    return lax.broadcasted_iota(jnp.int32, (BLK, LANES), 1) < HEAD_DIM


def _stack_heads(t2, left):
    zero = jnp.zeros_like(t2)
    return jnp.concatenate([jnp.where(left, t2, zero), jnp.where(left, zero, t2)], axis=0)


def _attn_fwd_block(n, q_ref, kc_ref, kp_ref, vc_ref, vp_ref, o_ref, l_ref):
    mask = _band_mask(n)
    left = _left_lanes()
    zero = jnp.zeros((BLK, LANES), BF16)
    for pr in range(GW // LANES):
        cs = slice(pr * LANES, (pr + 1) * LANES)
        q2 = q_ref[:, cs]
        k2 = jnp.concatenate([kp_ref[:, cs], kc_ref[:, cs]], axis=0)
        v2 = jnp.concatenate([vp_ref[:, cs], vc_ref[:, cs]], axis=0)
        o_h, lse_h = [], []
        for side in (left, ~left):
            s = jnp.where(mask, _dot_nt(jnp.where(side, q2, zero), k2), NEG)
            m = jnp.max(s, axis=1, keepdims=True)
            p = jnp.exp(s - m)
            l = jnp.sum(p, axis=1, keepdims=True)
            o_h.append(_dot_nn((p / l).astype(BF16), v2))
            lse_h.append(m + jnp.log(l))
        o_ref[:, cs] = jnp.where(left, o_h[0], o_h[1])
        l_ref[:, cs] = jnp.where(left, lse_h[0], lse_h[1])


def _attn_fwd(preps):
    def body(*refs):
        t = pl.program_id(0)
        for gi in range(3):
            _attn_fwd_block(_class_and_block(gi, t)[1], *refs[5 * gi:5 * gi + 5], *refs[15 + 2 * gi:17 + 2 * gi])

    in_specs, out_specs, out_shape, args = [], [], [], []
    for gi, d in enumerate(PATTERN_DILATIONS):
        cur, prev, _ = _attn_specs(gi)
        qr, kr, vv = preps[gi]
        in_specs += [cur, cur, prev, cur, prev]
        args += [qr, kr, kr, vv, vv]
        out_specs += [cur, cur]
        out_shape += [jax.ShapeDtypeStruct((S // d, d * GW), F32)] * 2
    outs = pl.pallas_call(
        body,
        name="attn_fwd",
        grid=(N_ATTN_STEPS,),
        in_specs=in_specs,
        out_specs=out_specs,
        out_shape=out_shape,
        compiler_params=_cp(("arbitrary",), 48),
    )(*args)
    return [outs[0], outs[2], outs[4]], [outs[1], outs[3], outs[5]]


def _attn_bwd_block(n, q_ref, kc_ref, kp_ref, vc_ref, vp_ref, do_ref, l_ref, c_ref, dq_ref, dk_ref, dv_ref):
    mask = _band_mask(n)
    left = _left_lanes()
    left2 = jnp.concatenate([left, left], axis=0)
    here = pl.ds(pl.multiple_of(n * BLK, BLK), BLK)
    before = pl.ds(pl.multiple_of(jnp.maximum(n - 1, 0) * BLK, BLK), BLK)
    for pr in range(GW // LANES):
        cs = slice(pr * LANES, (pr + 1) * LANES)
        q2 = q_ref[:, cs]
        do2 = do_ref[:, cs]
        k_bd = _stack_heads(jnp.concatenate([kp_ref[:, cs], kc_ref[:, cs]], axis=0), left2)
        v_bd = _stack_heads(jnp.concatenate([vp_ref[:, cs], vc_ref[:, cs]], axis=0), left2)
        s2 = _dot_nt(q2, k_bd)
        dp2 = _dot_nt(do2, v_bd)
        ps, dss = [], []
        for h in range(2):
            at = pr * LANES + h * HEAD_DIM
            half = slice(h * 2 * BLK, (h + 1) * 2 * BLK)
            p = jnp.exp(jnp.where(mask, s2[:, half], NEG) - l_ref[:, at:at + 1])
            ps.append(p.astype(BF16))
            dss.append((p * (dp2[:, half] - c_ref[:, at:at + 1])).astype(BF16))
        ds_both = jnp.concatenate(dss, axis=1)
        dq_ref[:, cs] = _dot_nn(ds_both, k_bd) * (HEAD_DIM ** -0.5)
        dk_bd = _dot_tn(ds_both, q2)
        dv_bd = _dot_tn(jnp.concatenate(ps, axis=1), do2)
        dk2 = jnp.where(left2, dk_bd[:2 * BLK], dk_bd[2 * BLK:])
        dv2 = jnp.where(left2, dv_bd[:2 * BLK], dv_bd[2 * BLK:])
        dk_ref[here, cs] = dk2[BLK:]
        dv_ref[here, cs] = dv2[BLK:]
        dk_ref[before, cs] += dk2[:BLK]
        dv_ref[before, cs] += dv2[:BLK]


def _attn_bwd(preps, dos, lses, cterms):
    def body(*refs):
        t = pl.program_id(0)
        for gi in range(3):
            _attn_bwd_block(_class_and_block(gi, t)[1], *refs[8 * gi:8 * gi + 8], *refs[24 + 3 * gi:27 + 3 * gi])

    in_specs, out_specs, out_shape, args = [], [], [], []
    for gi, d in enumerate(PATTERN_DILATIONS):
        cur, prev, whole = _attn_specs(gi)
        qr, kr, vv = preps[gi]
        in_specs += [cur, cur, prev, cur, prev, cur, cur, cur]
        args += [qr, kr, kr, vv, vv, dos[gi], lses[gi], cterms[gi]]
        out_specs += [cur, whole, whole]
        out_shape += [jax.ShapeDtypeStruct((S // d, d * GW), F32)] * 3
    outs = pl.pallas_call(
        body,
        name="attn_bwd",
        grid=(N_ATTN_STEPS,),
        in_specs=in_specs,
        out_specs=out_specs,
        out_shape=out_shape,
        compiler_params=_cp(("arbitrary",), 56),
    )(*args)
    return [tuple(outs[3 * gi:3 * gi + 3]) for gi in range(3)]


def _qkv_unprep(grads, cos_t, sin_t, dproj):
    def body(*refs):
        views, (c_ref, s_ref, _, out_ref), scr = refs[:9], refs[9:13], refs[13:]
        lo = _lane_lo(ROWS)
        cf, sf = _tile4(c_ref[...]), _tile4(s_ref[...])
        for gi, d in enumerate(PATTERN_DILATIONS):
            dq_ref, dk_ref, dv_ref = views[3 * gi:3 * gi + 3]
            at = 3 * gi * GW
            out_ref[:, at:at + GW] = _unrope(_from_view(scr, dq_ref, d), cf, sf, lo).astype(BF16)
            out_ref[:, at + GW:at + 2 * GW] = _unrope(_from_view(scr, dk_ref, d), cf, sf, lo).astype(BF16)
            out_ref[:, at + 2 * GW:at + 3 * GW] = _from_view(scr, dv_ref, d).astype(BF16)

    tab = pl.BlockSpec((ROWS, BLK), lambda i: (i, 0))
    return pl.pallas_call(
        body,
        name="qkv_unprep",
        grid=(S // ROWS,),
        in_specs=[_view_spec(d) for d in PATTERN_DILATIONS for _ in range(3)] + [tab, tab, pl.BlockSpec(memory_space=pl.ANY)],
        out_specs=pl.BlockSpec((ROWS, QKV_W), lambda i: (i, 0)),
        out_shape=jax.ShapeDtypeStruct(dproj.shape, dproj.dtype),
        input_output_aliases={11: 0},
        scratch_shapes=_VIEW_SCRATCH,
        compiler_params=_cp(("parallel",), 48),
    )(*[g for grp in grads for g in grp], cos_t, sin_t, dproj)


def _group_weights(l0, l1, l2):
    mx = jnp.maximum(jnp.maximum(l0, l1), l2)
    e0, e1, e2 = jnp.exp(l0 - mx), jnp.exp(l1 - mx), jnp.exp(l2 - mx)
    inv = 1.0 / (e0 + e1 + e2)
    return e0 * inv, e1 * inv, e2 * inv


def _combine_fwd(os_, lses):
    def body(o0, o1, o2, l0, l1, l2, y_ref, *scr):
        ov = [_from_view(scr, o, d) for o, d in zip((o0, o1, o2), PATTERN_DILATIONS)]
        lv = [_from_view(scr, l, d) for l, d in zip((l0, l1, l2), PATTERN_DILATIONS)]
        w0, w1, w2 = _group_weights(*lv)
        y_ref[...] = (w0 * ov[0] + w1 * ov[1] + w2 * ov[2]).astype(BF16)

    views = [_view_spec(d) for d in PATTERN_DILATIONS]
    return pl.pallas_call(
        body,
        name="attn_combine_fwd",
        grid=(S // ROWS,),
        in_specs=views + views,
        out_specs=pl.BlockSpec((ROWS, GW), lambda i: (i, 0)),
        out_shape=jax.ShapeDtypeStruct((S, GW), BF16),
        scratch_shapes=_VIEW_SCRATCH,
        compiler_params=_cp(("parallel",)),
    )(*os_, *lses)


def _combine_bwd(dy, os_, lses, seg):
    def body(dy_ref, o0, o1, o2, l0, l1, l2, seg_ref, d0, d1, d2, c0, c1, c2, *scr):
        ov = [_from_view(scr, o, d) for o, d in zip((o0, o1, o2), PATTERN_DILATIONS)]
        lv = [_from_view(scr, l, d) for l, d in zip((l0, l1, l2), PATTERN_DILATIONS)]
        ws = _group_weights(*lv)
        dyv = dy_ref[...]
        t = dyv * (ws[0] * ov[0] + ws[1] * ov[1] + ws[2] * ov[2])
        t_hi = t.astype(BF16)
        r1 = t - t_hi.astype(F32)
        t_mid = r1.astype(BF16)
        t_lo = (r1 - t_mid.astype(F32)).astype(BF16)
        sg = seg_ref[...]
        e = _dot_nn(t_hi, sg) + _dot_nn(t_mid, sg) + _dot_nn(t_lo, sg)
        for w, do_ref, c_ref, d in zip(ws, (d0, d1, d2), (c0, c1, c2), PATTERN_DILATIONS):
            _to_view(scr, w * dyv, do_ref, d, BF16)
            _to_view(scr, w * e, c_ref, d, F32)

    views = [_view_spec(d) for d in PATTERN_DILATIONS]
    return pl.pallas_call(
        body,
        name="attn_combine_bwd",
        grid=(S // ROWS,),
        in_specs=[pl.BlockSpec((ROWS, GW), lambda i: (i, 0))] + views + views + [pl.BlockSpec((GW, GW), lambda i: (0, 0))],
        out_specs=views + views,
        out_shape=[jax.ShapeDtypeStruct((S // d, d * GW), BF16) for d in PATTERN_DILATIONS]
        + [jax.ShapeDtypeStruct((S // d, d * GW), F32) for d in PATTERN_DILATIONS],
        scratch_shapes=_VIEW_SCRATCH,
        compiler_params=_cp(("parallel",)),
    )(dy, *os_, *lses, seg)


_SQRT_HALF = 0.7071067811865476
_INV_SQRT_2PI = 0.3989422804014327


def _gelu(z):
    return 0.5 * z * (1.0 + lax.erf(z * _SQRT_HALF))


def _gelu_grad(z):
    return 0.5 * (1.0 + lax.erf(z * _SQRT_HALF)) + z * (_INV_SQRT_2PI * jnp.exp(-0.5 * z * z))


def _tril_ws(ws_ref, g):
    t = lax.broadcasted_iota(jnp.int32, (BLK, BLK), 0)
    s = lax.broadcasted_iota(jnp.int32, (BLK, BLK), 1)
    return jnp.where(t >= s, ws_ref[g], 0.0)


def _chunks_side_by_side(x, cols, nch):
    return jnp.concatenate([x[c * BLK:(c + 1) * BLK, cols] for c in range(nch)], axis=1)


def _gmlp_fwd(z, ws, bst, ln_g, ln_b, tm=512):
    nch = tm // BLK

    def body(z_ref, ws_ref, b_ref, g_ref, be_ref, y_ref):
        zg = _gelu(z_ref[...].astype(F32))
        u = zg[:, :D]
        vn, _, _ = _ln_fwd(zg[:, D:], g_ref[...], be_ref[...])
        vnb = vn.astype(BF16)
        bt = b_ref[...]
        for g in range(8):
            w = _tril_ws(ws_ref, g).astype(BF16)
            cols = slice(g * BLK, (g + 1) * BLK)
            mixed = _dot_nn(w, _chunks_side_by_side(vnb, cols, nch)) + bt[:, g:g + 1]
            for c in range(nch):
                rows = slice(c * BLK, (c + 1) * BLK)
                y_ref[rows, cols] = (u[rows, cols] * mixed[:, c * BLK:(c + 1) * BLK]).astype(BF16)

    return pl.pallas_call(
        body,
        name="gmlp_fwd",
        grid=(S // tm,),
        in_specs=[
            pl.BlockSpec((tm, 2 * D), lambda i: (i, 0)),
            pl.BlockSpec((8, BLK, BLK), lambda i: (0, 0, 0)),
            pl.BlockSpec((BLK, 8), lambda i: (0, 0)),
            pl.BlockSpec((1, D), lambda i: (0, 0)),
            pl.BlockSpec((1, D), lambda i: (0, 0)),
        ],
        out_specs=pl.BlockSpec((tm, D), lambda i: (i, 0)),
        out_shape=jax.ShapeDtypeStruct((S, D), BF16),
        compiler_params=_cp(("parallel",), 48),
    )(z, ws, bst, ln_g, ln_b)


def _gmlp_bwd(z, dy, ws, bst, ln_g, ln_b, dproj, tm=512):
    nch = tm // BLK

    def body(z_ref, dy_ref, ws_ref, b_ref, g_ref, be_ref, _, dz_ref, dws_ref, dbs_ref, dg_ref, dbe_ref, dvn_scr, dm_acc):
        i = pl.program_id(0)
        zv = z_ref[...].astype(F32)
        zg = _gelu(zv)
        u = zg[:, :D]
        gam = g_ref[...]
        vn, xhat, rstd = _ln_fwd(zg[:, D:], gam, be_ref[...])
        vnb = vn.astype(BF16)
        dyv = dy_ref[...]
        dmix = dyv * u
        dmb = dmix.astype(BF16)
        bt = b_ref[...]
        tmask = lax.broadcasted_iota(jnp.int32, (BLK, BLK), 0) >= lax.broadcasted_iota(jnp.int32, (BLK, BLK), 1)
        dm_sum = dmix[0:BLK]
        for c in range(1, nch):
            dm_sum = dm_sum + dmix[c * BLK:(c + 1) * BLK]

        @pl.when(i == 0)
        def _():
            dm_acc[...] = jnp.zeros_like(dm_acc)
            dws_ref[...] = jnp.zeros_like(dws_ref)
            dg_ref[...] = jnp.zeros_like(dg_ref)
            dbe_ref[...] = jnp.zeros_like(dbe_ref)

        dm_acc[...] += dm_sum
        dus = []
        for g in range(8):
            w = _tril_ws(ws_ref, g).astype(BF16)
            cols = slice(g * BLK, (g + 1) * BLK)
            v_cat = _chunks_side_by_side(vnb, cols, nch)
            dm_cat = _chunks_side_by_side(dmb, cols, nch)
            mixed = _dot_nn(w, v_cat) + bt[:, g:g + 1]
            dus.append(jnp.concatenate(
                [dyv[c * BLK:(c + 1) * BLK, cols] * mixed[:, c * BLK:(c + 1) * BLK] for c in range(nch)], axis=0))
            dws_ref[g] += jnp.where(tmask, _dot_nt(dm_cat, v_cat), 0.0)
            dvn_cat = _dot_tn(w, dm_cat)
            for c in range(nch):
                dvn_scr[c * BLK:(c + 1) * BLK, cols] = dvn_cat[:, c * BLK:(c + 1) * BLK]

        dvn = dvn_scr[...]
        dg_ref[...] += _colsum(dvn * xhat)
        dbe_ref[...] += _colsum(dvn)
        dvg = _ln_bwd(dvn, xhat, rstd, gam)
        gp = _gelu_grad(zv)
        dz_ref[:, :D] = (jnp.concatenate(dus, axis=1) * gp[:, :D]).astype(BF16)
        dz_ref[:, D:] = (dvg * gp[:, D:]).astype(BF16)

        @pl.when(i == S // tm - 1)
        def _():
            acc = dm_acc[...]
            for g in range(8):
                dbs_ref[:, g:g + 1] = jnp.sum(acc[:, g * BLK:(g + 1) * BLK], axis=1, keepdims=True)

    vec = pl.BlockSpec((1, D), lambda i: (0, 0))
    return pl.pallas_call(
        body,
        name="gmlp_bwd",
        grid=(S // tm,),
        in_specs=[
            pl.BlockSpec((tm, 2 * D), lambda i: (i, 0)),
            pl.BlockSpec((tm, D), lambda i: (i, 0)),
            pl.BlockSpec((8, BLK, BLK), lambda i: (0, 0, 0)),
            pl.BlockSpec((BLK, 8), lambda i: (0, 0)),
            vec,
            vec,
            pl.BlockSpec(memory_space=pl.ANY),
        ],
        out_specs=[
            pl.BlockSpec((tm, 2 * D), lambda i: (i, DPROJ_Z_COL)),
            pl.BlockSpec((8, BLK, BLK), lambda i: (0, 0, 0)),
            pl.BlockSpec((BLK, 8), lambda i: (0, 0)),
            vec,
            vec,
        ],
        out_shape=[
            jax.ShapeDtypeStruct(dproj.shape, dproj.dtype),
            jax.ShapeDtypeStruct((8, BLK, BLK), F32),
            jax.ShapeDtypeStruct((BLK, 8), F32),
            jax.ShapeDtypeStruct((1, D), F32),
            jax.ShapeDtypeStruct((1, D), F32),
        ],
        input_output_aliases={6: 0},
        scratch_shapes=[pltpu.VMEM((tm, D), F32), pltpu.VMEM((BLK, D), F32)],
        compiler_params=_cp(("arbitrary",), 48),
    )(z, dy, ws, bst, ln_g, ln_b, dproj)


def _merge_fwd(ya, yg, glog, bgate, h1, wabt, wgb, wo, ln_g, ln_b, tm=256):
    def body(ya_ref, yg_ref, gl_ref, bg_ref, h1_ref, wab_ref, wgb_ref, wo_ref, g_ref, b_ref,
             h_ref, hb_ref, xh_ref, rs_ref, mg_ref, bra_ref, brg_ref):
        bra = _dot_nt(ya_ref[...], wab_ref[...])
        brg = _dot_nn(yg_ref[...], wgb_ref[...])
        gates = _sigmoid(gl_ref[...].astype(F32) + bg_ref[...])
        merged = (gates[:, :D] * bra + gates[:, D:] * brg).astype(BF16)
        mix = _dot_nn(merged, wo_ref[...])
        h, xhat, rstd = _ln_fwd(ALPHA * h1_ref[...] + mix, g_ref[...], b_ref[...])
        h_ref[...] = h
        hb_ref[...] = h.astype(BF16)
        xh_ref[...] = xhat
        rs_ref[...] = rstd
        mg_ref[...] = merged
        bra_ref[...] = bra
        brg_ref[...] = brg

    row = pl.BlockSpec((tm, D), lambda i: (i, 0))
    vec = pl.BlockSpec((1, D), lambda i: (0, 0))
    full = lambda shape: pl.BlockSpec(shape, lambda i: (0, 0))
    return pl.pallas_call(
        body,
        name="merge_fwd",
        grid=(S // tm,),
        in_specs=[
            pl.BlockSpec((tm, GW), lambda i: (i, 0)), row,
            pl.BlockSpec((tm, 2 * D), lambda i: (i, glog.shape[1] // (2 * D) - 1)),
            full((1, 2 * D)), row,
            full((D, GW)), full((D, D)), full((D, D)), vec, vec,
        ],
        out_specs=[row, row, row, pl.BlockSpec((tm, 1), lambda i: (i, 0)), row, row, row],
        out_shape=[
            jax.ShapeDtypeStruct((S, D), F32),
            jax.ShapeDtypeStruct((S, D), BF16),
            jax.ShapeDtypeStruct((S, D), F32),
            jax.ShapeDtypeStruct((S, 1), F32),
            jax.ShapeDtypeStruct((S, D), BF16),
            jax.ShapeDtypeStruct((S, D), F32),
            jax.ShapeDtypeStruct((S, D), F32),
        ],
        compiler_params=_cp(("parallel",), 48),
    )(ya, yg, glog, bgate, h1, wabt, wgb, wo, ln_g, ln_b)


def _merge_bwd(dh2, xhat, rstd, ln_g, bra, brg, glog, bgate, wabt, wgb, wo, tm=256):
    def body(dh_ref, xh_ref, rs_ref, g_ref, bra_ref, brg_ref, gl_ref, bg_ref, wab_ref, wgb_ref, wo_ref,
             dr_ref, drb_ref, dlog_ref, dba_ref, dbg_ref, dya_ref, dyg_ref, dbgate_ref, dg_ref, dbias_ref):
        i = pl.program_id(0)
        dh = dh_ref[...]
        xh = xh_ref[...]
        dr = _ln_bwd(dh, xh, rs_ref[...], g_ref[...])
        drb = dr.astype(BF16)
        dr_ref[...] = dr
        drb_ref[...] = drb
        dmerged = _dot_nt(drb, wo_ref[...])
        gates = _sigmoid(gl_ref[...].astype(F32) + bg_ref[...])
        g0, g1 = gates[:, :D], gates[:, D:]
        dl0 = dmerged * bra_ref[...] * g0 * (1.0 - g0)
        dl1 = dmerged * brg_ref[...] * g1 * (1.0 - g1)
        dlog_ref[:, :D] = dl0.astype(BF16)
        dlog_ref[:, D:] = dl1.astype(BF16)
        dba = (dmerged * g0).astype(BF16)
        dbg = (dmerged * g1).astype(BF16)
        dba_ref[...] = dba
        dbg_ref[...] = dbg
        dya_ref[...] = _dot_nn(dba, wab_ref[...])
        dyg_ref[...] = _dot_nt(dbg, wgb_ref[...])
        s0, s1 = _colsum(dl0), _colsum(dl1)
        sg, sb = _colsum(dh * xh), _colsum(dh)

        @pl.when(i == 0)
        def _():
            dbgate_ref[:, :D] = s0
            dbgate_ref[:, D:] = s1
            dg_ref[...] = sg
            dbias_ref[...] = sb

        @pl.when(i > 0)
        def _():
            dbgate_ref[:, :D] += s0
            dbgate_ref[:, D:] += s1
            dg_ref[...] += sg
            dbias_ref[...] += sb

    row = pl.BlockSpec((tm, D), lambda i: (i, 0))
    vec = pl.BlockSpec((1, D), lambda i: (0, 0))
    wide = pl.BlockSpec((tm, 2 * D), lambda i: (i, 0))
    full = lambda shape: pl.BlockSpec(shape, lambda i: (0, 0))
    return pl.pallas_call(
        body,
        name="merge_bwd",
        grid=(S // tm,),
        in_specs=[row, row, pl.BlockSpec((tm, 1), lambda i: (i, 0)), vec, row, row,
                  pl.BlockSpec((tm, 2 * D), lambda i: (i, glog.shape[1] // (2 * D) - 1)),
                  full((1, 2 * D)), full((D, GW)), full((D, D)), full((D, D))],
        out_specs=[row, row, pl.BlockSpec((tm, 2 * D), lambda i: (i, DPROJ_G_COL)), row, row,
                   pl.BlockSpec((tm, GW), lambda i: (i, 0)), row, full((1, 2 * D)), vec, vec],
        out_shape=[
            jax.ShapeDtypeStruct((S, D), F32),
            jax.ShapeDtypeStruct((S, D), BF16),
            jax.ShapeDtypeStruct((S, DPROJ_W), BF16),
            jax.ShapeDtypeStruct((S, D), BF16),
            jax.ShapeDtypeStruct((S, D), BF16),
            jax.ShapeDtypeStruct((S, GW), F32),
            jax.ShapeDtypeStruct((S, D), F32),
            jax.ShapeDtypeStruct((1, 2 * D), F32),
            jax.ShapeDtypeStruct((1, D), F32),
            jax.ShapeDtypeStruct((1, D), F32),
        ],
        compiler_params=_cp(("arbitrary",), 48),
    )(dh2, xhat, rstd, ln_g, bra, brg, glog, bgate, wabt, wgb, wo)


def _tie(x, dep):
    if dep is None:
        return x
    return x + dep[0, 0].astype(x.dtype)


def _local_step(x, pos_col, target, get_w, p, emit):
    w = get_w("ffn1", None)
    a1, b1, hm1 = _ffn_up(x, w["g1"], w["u1"], "ffn1_up")
    w.update(get_w("ffn1d", hm1))
    h1, h1b, xh1, rs1 = _ffn_down(hm1, w["d1"], x, p["ln1_g"], p["ln1_b"], "ffn1_down")

    w.update(get_w("win", h1b))
    qkv = _matmul(h1b, w["win"], "nt", BF16, S, 1536, D, "proj_qkv", b_off=0, n_out=QKV_W)
    z = glog = _matmul(h1b, w["win"], "nt", BF16, S, 512, D, "proj_zg", b_off=QKV_W // 512, n_out=4 * D)

    half = jnp.arange(0, HEAD_DIM, 2, dtype=F32) / HEAD_DIM
    inv_freq = ROPE_THETA ** (-half)
    invf = jnp.tile(inv_freq, 4).reshape(1, BLK)
    sign = jnp.tile(jnp.concatenate([-jnp.ones((32,), F32), jnp.ones((32,), F32)]), 2).reshape(1, BLK)
    cos_t, sin_t = _rope_tables(pos_col, invf, sign)

    preps = _qkv_prep(qkv, cos_t, sin_t)
    os_, lses = _attn_fwd(preps)
    ya = _combine_fwd(os_, lses)
    get_w("late", ya, early=True)
    bst = p["gmlp_b_s"].T
    yg = _gmlp_fwd(z, p["gmlp_w_s"], bst, p["gmlp_ln_g"], p["gmlp_ln_b"])
    w.update(get_w("late", yg))
    h2, h2b, xh2, rs2, merged, bra, brg = _merge_fwd(ya, yg, glog, p["b_gates"], h1, w["ab"], w["gb"], w["o"],
                                                      p["ln2_g"], p["ln2_b"])
    a2, b2, hm2 = _ffn_up(h2b, w["g2"], w["u2"], "ffn2_up")
    dh3, loss, xh3, rs3 = _ffn_down(hm2, w["d2"], h2, p["ln3_g"], p["ln3_b"], "ffn2_down", target=target)

    gp = {}
    dr3, df2, da2, db2, gp["ln3_g"], gp["ln3_b"] = _ffn_bwd_mid(dh3, None, xh3, rs3, p["ln3_g"], a2, b2, w["d2"],
                                                                "ffn2_bwd_mid")
    tok = emit("ffn2", {
        "g2": _matmul(da2, h2b, "tn", BF16, 1408, D, S, "wgrad_g2"),
        "u2": _matmul(db2, h2b, "tn", BF16, 1408, D, S, "wgrad_u2"),
        "d2": _matmul(hm2, df2, "tn", BF16, 1408, D, S, "wgrad_d2")})
    dh2 = _ffn_bwd_dx(dr3, da2, db2, w["g2"], w["u2"], "ffn2_bwd_dx")

    (dr2, dr2b, dproj, dba, dbg, dya, dyg, gp["b_gates"], gp["ln2_g"], gp["ln2_b"]) = _merge_bwd(
        dh2, xh2, rs2, _tie(p["ln2_g"], tok), bra, brg, glog, p["b_gates"], w["ab"], w["gb"], w["o"])
    tok = emit("mix", {
        "o": _matmul(merged, dr2b, "tn", BF16, 512, D, S, "wgrad_o"),
        "ab": _matmul(dba, ya, "tn", BF16, 512, GW, S, "wgrad_ab"),
        "gb": _matmul(yg, dbg, "tn", BF16, 512, D, S, "wgrad_gb")})

    seg = (jnp.arange(GW)[:, None] // HEAD_DIM == jnp.arange(GW)[None, :] // HEAD_DIM).astype(BF16)
    do0, do1, do2, c0, c1, c2 = _combine_bwd(dya, os_, lses, _tie(seg, tok))
    dproj = _qkv_unprep(_attn_bwd(preps, (do0, do1, do2), lses, (c0, c1, c2)), cos_t, sin_t, dproj)
    dproj, gp["gmlp_w_s"], dbst, gp["gmlp_ln_g"], gp["gmlp_ln_b"] = _gmlp_bwd(
        z, dyg, p["gmlp_w_s"], bst, p["gmlp_ln_g"], p["gmlp_ln_b"], dproj)
    gp["gmlp_b_s"] = dbst.T
    tok = emit("win", {"win": _matmul(dproj, h1b, "tn", BF16, 512, D, S, "wgrad_win", a_map=_dproj_tile, m_out=IN_W)})
    dh1m = _matmul(dproj, w["win"], "nn", F32, S, D, 512, "dproj_to_dh1", dep=tok, a_map=_dproj_tile)

    dr1, df1, da1, db1, gp["ln1_g"], gp["ln1_b"] = _ffn_bwd_mid(dr2, dh1m, xh1, rs1, p["ln1_g"], a1, b1, w["d1"],
                                                                "ffn1_bwd_mid")
    tok = emit("small", {**gp, "loss": loss})
    tok = emit("g1", {"g1": _matmul(da1, x, "tn", BF16, 1408, D, S, "wgrad_g1", dep=tok)})
    tok = emit("ud1", {"u1": _matmul(db1, x, "tn", BF16, 1408, D, S, "wgrad_u1", dep=tok),
                       "d1": _matmul(hm1, df1, "tn", BF16, 1408, D, S, "wgrad_d1", dep=tok)})
    dx = _ffn_bwd_dx(dr1, da1, db1, w["g1"], w["u1"], "ffn1_bwd_dx", dep=tok)
    return loss, dx, gp


_FLIPS = [(mx, my, mc) for mx in (0, 1) for my in (0, 1) for mc in (0, 1)][1:]
_GROUPS = {"ffn1": ("g1", "u1"), "ffn1d": ("d1",), "win": ("win",), "late": ("ab", "gb", "o", "g2", "u2", "d2")}
_SCATTERS = {"ffn2": ("g2", "u2", "d2"), "mix": ("ab", "gb", "o"), "win": ("win",), "g1": ("g1",), "ud1": ("u1", "d1")}
_TWO_STAGE = ("g1", "ud1")
_HBM = pl.BlockSpec(memory_space=pltpu.HBM)
_SEM = pl.BlockSpec(memory_space=pltpu.SEMAPHORE)
_EFFECT = pltpu.SideEffectType.DATAFLOW_SIDE_EFFECTING


def _me():
    return 4 * lax.axis_index("x") + 2 * lax.axis_index("y") + lax.axis_index("c")


def _copies_start(bufs, copies, n_sem, name, after=None):
    nb = len(bufs)
    n_after = 0 if after is None else 1

    def body(*refs):
        b = refs[:nb]
        send_sems, recv_sems = refs[nb + n_after], refs[nb + n_after + 1]
        token = refs[-1]
        x, y, c = lax.axis_index("x"), lax.axis_index("y"), lax.axis_index("c")
        me = 4 * x + 2 * y + c
        for si, s_slot, di, d_slot, (mx, my, mc), sem in copies:
            s_idx, d_idx = s_slot(me), d_slot(me)
            pltpu.make_async_remote_copy(
                src_ref=b[si] if s_idx is None else b[si].at[s_idx],
                dst_ref=b[di] if d_idx is None else b[di].at[d_idx],
                send_sem=send_sems.at[sem], recv_sem=recv_sems.at[sem],
                device_id=(x ^ mx, y ^ my, c ^ mc), device_id_type=MESH).start()
        token[...] = jnp.zeros_like(token)

    ins = [pltpu.with_memory_space_constraint(a, pltpu.HBM) for a in bufs]
    outs = pl.pallas_call(
        body,
        name=name,
        in_specs=[_HBM] * nb + [pl.BlockSpec(memory_space=pl.ANY)] * n_after,
        out_specs=[_SEM, _SEM] + [_HBM] * nb + [pl.BlockSpec(memory_space=pltpu.VMEM)],
        out_shape=[pltpu.SemaphoreType.DMA((n_sem,)), pltpu.SemaphoreType.DMA((n_sem,))]
        + [pltpu.HBM(a.shape, a.dtype) for a in ins] + [jax.ShapeDtypeStruct((8, 128), F32)],
        input_output_aliases={k: 2 + k for k in range(nb)},
        compiler_params=pltpu.CompilerParams(has_side_effects=_EFFECT),
    )(*ins, *([] if after is None else [after]))
    return outs[0], outs[1], list(outs[2:2 + nb]), outs[-1]


def _copies_wait(started, waits, after, name):
    send_sems, recv_sems, bufs, _ = started
    nb = len(bufs)

    def body(*refs):
        b = refs[:nb]
        ss, rs = refs[nb], refs[nb + 1]
        me3 = (lax.axis_index("x"), lax.axis_index("y"), lax.axis_index("c"))
        for bi, n_blocks, sem, is_send in waits:
            blocks = b[bi].at[pl.ds(0, n_blocks)]
            cp = pltpu.make_async_remote_copy(src_ref=blocks, dst_ref=blocks, send_sem=ss.at[sem], recv_sem=rs.at[sem],
                                              device_id=me3, device_id_type=MESH)
            if is_send:
                cp.wait_send()
            else:
                cp.wait_recv()

    return pl.pallas_call(
        body,
        name=name,
        in_specs=[_HBM] * nb + [_SEM, _SEM, pl.BlockSpec(memory_space=pl.ANY)],
        out_specs=[_HBM] * nb,
        out_shape=[pltpu.HBM(a.shape, a.dtype) for a in bufs],
        input_output_aliases={k: k for k in range(nb)},
        compiler_params=pltpu.CompilerParams(has_side_effects=_EFFECT),
    )(*bufs, send_sems, recv_sems, after)


def _landing(own, me):
    return lax.dynamic_update_slice(lax.empty((N_DEV,) + own.shape[1:], own.dtype), own, (me, 0, 0))


_SIBLING = (0, 0, 1)
_OTHER_CHIPS = ((1, 0, 0), (0, 1, 0), (1, 1, 0))


def _bits(flip):
    return 4 * flip[0] + 2 * flip[1] + flip[2]


def _gather_send(shards, grp, after):
    nw = len(shards)
    me = _me()
    bufs = list(shards) + [_landing(a[None], me) for a in shards]
    copies = []
    for k in range(nw):
        copies.append((k, lambda me: None, nw + k, lambda me: me, _SIBLING, nw + k))
        copies += [(k, lambda me: None, nw + k, lambda me: me, f, k) for f in _OTHER_CHIPS]
    return _copies_start(bufs, copies, 2 * nw, "gather_send_" + grp, after)


def _gather_pass(started, grp, after):
    nw = len(started[2]) // 2
    waits = []
    for k in range(nw):
        waits += [(nw + k, 3, k, True), (nw + k, 1, nw + k, True), (nw + k, 3, k, False), (nw + k, 1, nw + k, False)]
    lands = list(_copies_wait(started, waits, after, "gather_arrived_" + grp)[nw:])
    copies = []
    for k in range(nw):
        for f in _OTHER_CHIPS:
            slot = functools.partial(lambda me, bits: me ^ bits, bits=_bits(f))
            copies.append((k, slot, k, slot, _SIBLING, k))
    return _copies_start(lands, copies, nw, "gather_pass_" + grp)


def _gather_finish(passed, grp, after):
    nw = len(passed[2])
    waits = [(k, 3, k, is_send) for k in range(nw) for is_send in (True, False)]
    return _copies_wait(passed, waits, passed[3] if after is None else after, "gather_done_" + grp)


def _scatter_send(parts, grp, after=None):
    nw = len(parts)
    me = _me()
    bufs = list(parts) + [_landing(lax.dynamic_index_in_dim(a, me, 0, keepdims=True), me) for a in parts]
    copies = []
    for k in range(nw):
        for f in _FLIPS:
            to = functools.partial(lambda me, bits: me ^ bits, bits=_bits(f))
            copies.append((k, to, nw + k, lambda me: me, f, k))
    return _copies_start(bufs, copies, nw, "scatter_send_" + grp, after)


def _scatter_finish(started, grp, after):
    nw = len(started[2]) // 2
    waits = [(nw + k, N_DEV - 1, k, is_send) for k in range(nw) for is_send in (True, False)]
    return _copies_wait(started, waits, after, "scatter_done_" + grp)[nw:]


N_CHIP = N_DEV // 2


def _pair_sum(part, other, key):
    _, r, c = part.shape
    tm = _ADAM_ROWS.get(r, r)
    core = lax.axis_index("c").astype(jnp.int32).reshape(1)

    def body(core_ref, a_ref, b_ref, o_ref):
        o_ref[...] = (a_ref[...].astype(F32) + b_ref[...].astype(F32)).astype(BF16)

    return pl.pallas_call(
        body,
        name="pair_sum_" + key,
        grid_spec=pltpu.PrefetchScalarGridSpec(
            num_scalar_prefetch=1,
            grid=(N_CHIP, r // tm),
            in_specs=[pl.BlockSpec((None, tm, c), lambda q, i, core_ref: (2 * q + core_ref[0], i, 0)),
                      pl.BlockSpec((None, tm, c), lambda q, i, core_ref: (q, i, 0))],
            out_specs=pl.BlockSpec((None, tm, c), lambda q, i, core_ref: (q, i, 0)),
        ),
        out_shape=jax.ShapeDtypeStruct((N_CHIP, r, c), BF16),
        compiler_params=_cp(("parallel", "parallel")),
    )(core, part, other)


def _scatter2_send(parts, grp, keys):
    nw = len(parts)
    chip = _me() // 2
    swap = []
    for k in range(nw):
        for q in range(N_CHIP):
            src = functools.partial(lambda me, q: 2 * q + 1 - me % 2, q=q)
            swap.append((k, src, nw + k, functools.partial(lambda me, q: q, q=q), _SIBLING, k))
    others = [lax.empty((N_CHIP,) + a.shape[1:], a.dtype) for a in parts]
    started = _copies_start(list(parts) + others, swap, nw, "scatter_swap_" + grp)
    waits = [(nw + k, N_CHIP, k, is_send) for k in range(nw) for is_send in (True, False)]
    swapped = _copies_wait(started, waits, started[3], "scatter_swapped_" + grp)
    pairs = [_pair_sum(swapped[k], swapped[nw + k], keys[k]) for k in range(nw)]
    lands = [lax.dynamic_update_slice(lax.empty(p.shape, p.dtype), lax.dynamic_index_in_dim(p, chip, 0, keepdims=True),
                                      (chip, 0, 0)) for p in pairs]
    copies = []
    for k in range(nw):
        for f in _OTHER_CHIPS:
            to = functools.partial(lambda me, bits: (me ^ bits) // 2, bits=_bits(f))
            copies.append((k, to, nw + k, lambda me: me // 2, f, k))
    return _copies_start(pairs + lands, copies, nw, "scatter_send_" + grp)


def _scatter2_finish(started, grp, after):
    nw = len(started[2]) // 2
    waits = [(nw + k, N_CHIP - 1, k, is_send) for k in range(nw) for is_send in (True, False)]
    return _copies_wait(started, waits, after, "scatter_done_" + grp)[nw:]


def _allgather_send(block, name, after=None):
    me = _me()
    copies = [(0, lambda me: None, 1, lambda me: me, f, 0) for f in _FLIPS]
    return _copies_start([block, _landing(block[None], me)], copies, 1, name, after)


def _allgather_finish(started, name, after):
    waits = [(1, N_DEV - 1, 0, True), (1, N_DEV - 1, 0, False)]
    return _copies_wait(started, waits, after, name)[1]


_ADAM_ROWS = {352: 176, 1088: 272}


def _sum_adamw(parts, wv, m, v, name, dep=None):
    n_parts, r, c = parts.shape
    tm = _ADAM_ROWS.get(r, r)
    assert r % tm == 0 and wv.shape == (r, c)
    n_dep = 0 if dep is None else 1

    def body(p_ref, w_ref, m_ref, v_ref, *rest):
        g_ref, d_ref, mo_ref, vo_ref = rest[n_dep:]
        gv = p_ref[0].astype(F32)
        for j in range(1, n_parts):
            gv = gv + p_ref[j].astype(F32)
        g_ref[...] = gv
        mn = ADAM_B1 * m_ref[...] + (1.0 - ADAM_B1) * gv
        vn = ADAM_B2 * v_ref[...] + (1.0 - ADAM_B2) * (gv * gv)
        m_hat = mn / (1.0 - ADAM_B1 ** ADAM_STEP)
        v_hat = vn / (1.0 - ADAM_B2 ** ADAM_STEP)
        d_ref[...] = -ADAM_LR * (m_hat / (jnp.sqrt(v_hat) + ADAM_EPS) + ADAM_WD * w_ref[...])
        mo_ref[...] = mn
        vo_ref[...] = vn

    sp = pl.BlockSpec((tm, c), lambda i: (i, 0))
    return pl.pallas_call(
        body,
        name=name,
        grid=(r // tm,),
        in_specs=[pl.BlockSpec((n_parts, tm, c), lambda i: (0, i, 0))] + [sp] * 3 + [pl.BlockSpec(memory_space=pl.ANY)] * n_dep,
        out_specs=[sp] * 4,
        out_shape=[jax.ShapeDtypeStruct((r, c), F32)] * 4,
        compiler_params=_cp(("parallel",), 48),
    )(parts, wv, m, v, *([] if dep is None else [dep]))


_WEIGHTS = ["ffn1_w_gate", "ffn1_w_up", "ffn1_w_down", "ln1_g", "ln1_b", "w_in", "b_gates", "gmlp_ln_g", "gmlp_ln_b",
            "gmlp_w_s", "gmlp_b_s", "w_attn_branch", "w_gmlp_branch", "w_out", "ln2_g", "ln2_b", "ffn2_w_gate",
            "ffn2_w_up", "ffn2_w_down", "ln3_g", "ln3_b"]
_BIG_OF = {"ffn1_w_gate": ("g1", True), "ffn1_w_up": ("u1", True), "ffn1_w_down": ("d1", False), "w_in": ("win", True),
           "w_attn_branch": ("ab", True), "w_gmlp_branch": ("gb", False), "w_out": ("o", False),
           "ffn2_w_gate": ("g2", True), "ffn2_w_up": ("u2", True), "ffn2_w_down": ("d2", False)}
_SMALL = [n for n in _WEIGHTS if n not in _BIG_OF]
_SMALL_ROWS = {"gmlp_w_s": 128, "b_gates": 2}
_SMALL_SLOT = 8


def _pack_small(d, last=None):
    rows = []
    for n in _SMALL:
        r = d[n].reshape(-1, D)
        slot = max(r.shape[0], _SMALL_SLOT)
        rows.append(jnp.pad(r, ((0, slot - r.shape[0]), (0, 0))))
    rows.append(jnp.zeros((_SMALL_SLOT, D), F32) if last is None else jnp.broadcast_to(last.reshape(1, 1), (_SMALL_SLOT, D)))
    return jnp.concatenate(rows, axis=0)


def _unpack_small(packed, shapes):
    out, at = {}, 0
    for n in _SMALL:
        k = _SMALL_ROWS.get(n, 1)
        out[n] = packed[at:at + k].reshape(shapes[n])
        at += max(k, _SMALL_SLOT)
    return out


def kernel(x, positions, ffn1_w_gate, ffn1_w_up, ffn1_w_down, ln1_g, ln1_b, w_in, b_gates, gmlp_ln_g, gmlp_ln_b, gmlp_w_s, gmlp_b_s, w_attn_branch, w_gmlp_branch, w_out, ln2_g, ln2_b, ffn2_w_gate, ffn2_w_up, ffn2_w_down, ln3_g, ln3_b, loss_target, m_ffn1_w_gate, m_ffn1_w_up, m_ffn1_w_down, m_ln1_g, m_ln1_b, m_w_in, m_b_gates, m_gmlp_ln_g, m_gmlp_ln_b, m_gmlp_w_s, m_gmlp_b_s, m_w_attn_branch, m_w_gmlp_branch, m_w_out, m_ln2_g, m_ln2_b, m_ffn2_w_gate, m_ffn2_w_up, m_ffn2_w_down, m_ln3_g, m_ln3_b, v_ffn1_w_gate, v_ffn1_w_up, v_ffn1_w_down, v_ln1_g, v_ln1_b, v_w_in, v_b_gates, v_gmlp_ln_g, v_gmlp_ln_b, v_gmlp_w_s, v_gmlp_b_s, v_w_attn_branch, v_w_gmlp_branch, v_w_out, v_ln2_g, v_ln2_b, v_ffn2_w_gate, v_ffn2_w_up, v_ffn2_w_down, v_ln3_g, v_ln3_b):
    args = dict(locals())
    wts = {n: args[n] for n in _WEIGHTS}
    ms = {n: args["m_" + n] for n in _WEIGHTS}
    vs = {n: args["v_" + n] for n in _WEIGHTS}

    name_of = {key: (n, tr) for n, (key, tr) in _BIG_OF.items()}

    started, tok = {}, None
    for grp, keys in _GROUPS.items():
        shards = []
        for key in keys:
            n, tr = name_of[key]
            s2 = wts[n][0]
            shards.append(_tie(s2.T if tr else s2, tok).astype(BF16))
        started[grp] = _gather_send(shards, grp, tok)
        tok = started[grp][3]
    all_started = tok
    passed = {}

    def get_w(grp, after, early=False):
        if grp not in passed:
            passed[grp] = _gather_pass(started[grp], grp, all_started if after is None else after)
            after = None
        if early:
            return None
        lands = _gather_finish(passed[grp], grp, after)
        return {key: g.reshape(-1, g.shape[-1]) for key, g in zip(_GROUPS[grp], lands)}

    sent = {}

    def emit(grp, grads):
        if grp == "small":
            sent[grp] = _allgather_send(_pack_small(grads, last=grads["loss"]), "small_grads_send")
        else:
            parts = [grads[key].reshape(N_DEV, -1, grads[key].shape[-1]) for key in _SCATTERS[grp]]
            sent[grp] = (_scatter2_send(parts, grp, _SCATTERS[grp]) if grp in _TWO_STAGE else _scatter_send(parts, grp))
        return sent[grp][3]

    p = {n: (wts[n][0] if n in ("gmlp_w_s", "gmlp_b_s") else wts[n]) for n in _SMALL}
    loss, dx, gp = _local_step(x[0], positions.reshape(S, 1), loss_target[0], get_w, p, emit)
    grads, deltas, new_m, new_v = {}, {}, {}, {}
    after = dx
    for grp in ("ffn2", "mix", "win", "small", "g1", "ud1"):
        if grp == "small":
            parts = _allgather_finish(sent[grp], "small_grads_done", after)
            outs = _sum_adamw(parts, *[_pack_small({n: d[n] for n in _SMALL}) for d in (wts, ms, vs)], "update_small")
            shapes = {n: wts[n].shape for n in _SMALL}
            for dst, packed in zip((grads, deltas, new_m, new_v), outs):
                dst.update(_unpack_small(packed, shapes))
            loss = outs[0][-_SMALL_SLOT, 0]
            after = outs[1]
            continue
        arrived = (_scatter2_finish if grp in _TWO_STAGE else _scatter_finish)(sent[grp], grp, after)
        for key, part in zip(_SCATTERS[grp], arrived):
            n, tr = name_of[key]
            outs = _sum_adamw(part, *[(d[n][0].T if tr else d[n][0]) for d in (wts, ms, vs)], "update_" + key, dep=after)
            for dst, o in zip((grads, deltas, new_m, new_v), outs):
                dst[n] = (o.T if tr else o)[None]
            after = outs[1]

    return (loss, dx[None], *[grads[n] for n in _WEIGHTS], *[deltas[n] for n in _WEIGHTS],
            *[new_m[n] for n in _WEIGHTS], *[new_v[n] for n in _WEIGHTS])
```

```python
import functools
import math

import jax
import jax.numpy as jnp
from jax import lax
from jax.experimental import pallas as pl
from jax.experimental.pallas import tpu as pltpu

F32 = jnp.float32
BF16 = jnp.bfloat16

N_DEV = 8
D = 1024
S = 2048
F = 2816
HEAD_DIM = 64
HEADS = 8
GW = HEADS * HEAD_DIM
PATTERN_DILATIONS = (1, 4, 16)
BLK = 128
QKV_W = 3 * 3 * GW
IN_W = QKV_W + 2 * D + 2 * D
DPROJ_W = 5 * 2 * D
DPROJ_Z_COL, DPROJ_G_COL = 3, 4
DPROJ_ZG_AT = DPROJ_Z_COL * 2 * D


def _dproj_tile(t):
    return jnp.where(t < QKV_W // 512, t, t + (DPROJ_ZG_AT - QKV_W) // 512)
ROPE_THETA = 10000.0
ALPHA = 2.0 ** 0.25
LN_EPS = 1e-5
ADAM_LR, ADAM_B1, ADAM_B2, ADAM_EPS, ADAM_WD, ADAM_STEP = 0.001, 0.9, 0.999, 1e-08, 0.01, 10
NEG = -1e30
MESH = pl.DeviceIdType.MESH


def _cp(sem=None, vmem_mb=None):
    kw = {}
    if sem is not None:
        kw["dimension_semantics"] = sem
    if vmem_mb is not None:
        kw["vmem_limit_bytes"] = vmem_mb << 20
    return pltpu.CompilerParams(**kw)


def _dot_nn(a, b):
    return lax.dot_general(a, b, (((1,), (0,)), ((), ())), preferred_element_type=F32)


def _dot_nt(a, b):
    return lax.dot_general(a, b, (((1,), (1,)), ((), ())), preferred_element_type=F32)


def _dot_tn(a, b):
    return lax.dot_general(a, b, (((0,), (0,)), ((), ())), preferred_element_type=F32)


def _ln_fwd(r, g, b):
    mu = jnp.mean(r, axis=-1, keepdims=True)
    xc = r - mu
    var = jnp.mean(xc * xc, axis=-1, keepdims=True)
    rstd = lax.rsqrt(var + LN_EPS)
    xhat = xc * rstd
    return xhat * g + b, xhat, rstd


def _ln_bwd(dh, xhat, rstd, g):
    dxh = dh * g
    m1 = jnp.mean(dxh, axis=-1, keepdims=True)
    m2 = jnp.mean(dxh * xhat, axis=-1, keepdims=True)
    return rstd * (dxh - m1 - xhat * m2)


def _sigmoid(x):
    return 0.5 * jnp.tanh(0.5 * x) + 0.5


def _colsum(x):
    return jnp.sum(x, axis=0, keepdims=True)


def _matmul(a, b, mode, out_dtype, tm, tn, tk, name, b_off=0, n_out=None, dep=None, a_map=None, m_out=None):
    n_dep = 0 if dep is None else 1
    a_map = a_map or (lambda t: t)
    if mode == "nn":
        m, k = a.shape[0], b.shape[0]
        n = b.shape[1]
    elif mode == "nt":
        m, k = a.shape
        n = n_out if n_out is not None else b.shape[0]
    else:
        k, m = a.shape[0], m_out or a.shape[1]
        n = b.shape[1]
    nk = k // tk
    assert m % tm == 0 and n % tn == 0 and k % tk == 0
    dot = {"nn": _dot_nn, "nt": _dot_nt, "tn": _dot_tn}[mode]

    def body(a_ref, b_ref, *rest):
        o_ref, scr = rest[n_dep], rest[n_dep + 1:]
        r = dot(a_ref[...].astype(BF16), b_ref[...].astype(BF16))
        if nk == 1:
            o_ref[...] = r.astype(out_dtype)
        else:
            acc = scr[0]
            kk = pl.program_id(2)

            @pl.when(kk == 0)
            def _():
                acc[...] = r

            @pl.when(kk > 0)
            def _():
                acc[...] += r

            @pl.when(kk == nk - 1)
            def _():
                o_ref[...] = acc[...].astype(out_dtype)

    if mode == "nn":
        a_spec = pl.BlockSpec((tm, tk), lambda i, j, kk: (i, a_map(kk)))
        b_spec = pl.BlockSpec((tk, tn), lambda i, j, kk: (kk, j))
    elif mode == "nt":
        a_spec = pl.BlockSpec((tm, tk), lambda i, j, kk: (i, kk))
        b_spec = pl.BlockSpec((tn, tk), lambda i, j, kk: (j + b_off, kk))
    else:
        a_spec = pl.BlockSpec((tk, tm), lambda i, j, kk: (kk, a_map(i)))
        b_spec = pl.BlockSpec((tk, tn), lambda i, j, kk: (kk, j))
    return pl.pallas_call(
        body,
        name=name,
        grid=(m // tm, n // tn, nk),
        in_specs=[a_spec, b_spec] + [pl.BlockSpec(memory_space=pl.ANY)] * n_dep,
        out_specs=pl.BlockSpec((tm, tn), lambda i, j, kk: (i, j)),
        out_shape=jax.ShapeDtypeStruct((m, n), out_dtype),
        scratch_shapes=[] if nk == 1 else [pltpu.VMEM((tm, tn), F32)],
        compiler_params=_cp(("parallel", "parallel", "arbitrary"), 56),
    )(a, b, *([] if dep is None else [dep]))


def _ffn_up(x, wgt, wut, name, tm=512, tn=1408, dep=None):
    n_dep = 0 if dep is None else 1

    def body(x_ref, wg_ref, wu_ref, *rest):
        ga_ref, gb_ref, hm_ref = rest[n_dep:]
        xb = x_ref[...].astype(BF16)
        a = _dot_nt(xb, wg_ref[...])
        b = _dot_nt(xb, wu_ref[...])
        sig = _sigmoid(a)
        silu = a * sig
        ga_ref[...] = (b * (sig + silu * (1.0 - sig))).astype(BF16)
        gb_ref[...] = silu.astype(BF16)
        hm_ref[...] = (silu * b).astype(BF16)

    wsp = pl.BlockSpec((tn, D), lambda i, j: (j, 0))
    mid = pl.BlockSpec((tm, tn), lambda i, j: (i, j))
    return pl.pallas_call(
        body,
        name=name,
        grid=(S // tm, F // tn),
        in_specs=[pl.BlockSpec((tm, D), lambda i, j: (i, 0)), wsp, wsp] + [pl.BlockSpec(memory_space=pl.ANY)] * n_dep,
        out_specs=[mid, mid, mid],
        out_shape=[jax.ShapeDtypeStruct((S, F), BF16)] * 3,
        compiler_params=_cp(("parallel", "arbitrary"), 56),
    )(x, wgt, wut, *([] if dep is None else [dep]))


def _ffn_down(hm, wd, x, ln_g, ln_b, name, target=None, tm=512):
    head = target is not None

    def body(hm_ref, wd_ref, x_ref, g_ref, b_ref, *rest):
        t_ref = rest[0] if head else None
        o1_ref, o2_ref, xh_ref, rs_ref = rest[1 if head else 0:]
        r = ALPHA * x_ref[...] + 0.5 * _dot_nn(hm_ref[...], wd_ref[...])
        h, xhat, rstd = _ln_fwd(r, g_ref[...], b_ref[...])
        xh_ref[...] = xhat
        rs_ref[...] = rstd
        if head:
            e = h - t_ref[...]
            o1_ref[...] = e * (1.0 / D)
            part = jnp.sum(_colsum(e * e), axis=1, keepdims=True) * (0.5 / D)

            @pl.when(pl.program_id(0) == 0)
            def _():
                o2_ref[...] = jnp.zeros_like(o2_ref)

            o2_ref[...] += part
        else:
            o1_ref[...] = h
            o2_ref[...] = h.astype(BF16)

    row = pl.BlockSpec((tm, D), lambda i: (i, 0))
    vec = pl.BlockSpec((1, D), lambda i: (0, 0))
    second = (pl.BlockSpec((1, 1), lambda i: (0, 0)), jax.ShapeDtypeStruct((1, 1), F32)) if head else (
        row, jax.ShapeDtypeStruct((S, D), BF16))
    return pl.pallas_call(
        body,
        name=name,
        grid=(S // tm,),
        in_specs=[pl.BlockSpec((tm, F), lambda i: (i, 0)), pl.BlockSpec((F, D), lambda i: (0, 0)), row, vec, vec]
        + ([row] if head else []),
        out_specs=[row, second[0], row, pl.BlockSpec((tm, 1), lambda i: (i, 0))],
        out_shape=[
            jax.ShapeDtypeStruct((S, D), F32),
            second[1],
            jax.ShapeDtypeStruct((S, D), F32),
            jax.ShapeDtypeStruct((S, 1), F32),
        ],
        compiler_params=_cp(("arbitrary",), 56),
    )(hm, wd, x, ln_g, ln_b, *([target] if head else []))


def _ffn_bwd_mid(dh_a, dh_b, xhat, rstd, ln_g, a, b, wd, name, tm=512, tn=1408):
    two = dh_b is not None

    def body(*refs):
        dha_ref = refs[0]
        dhb_ref = refs[1] if two else None
        (xh_ref, rs_ref, g_ref, a_ref, b_ref, wd_ref,
         dr_ref, df_ref, da_ref, db_ref, dg_ref, dbias_ref, df_scr) = refs[2 if two else 1:]
        i = pl.program_id(0)
        j = pl.program_id(1)

        @pl.when(j == 0)
        def _():
            dh = dha_ref[...]
            if two:
                dh = ALPHA * dh + dhb_ref[...]
            xhat = xh_ref[...]
            dr = _ln_bwd(dh, xhat, rs_ref[...], g_ref[...])
            dfb = (0.5 * dr).astype(BF16)
            dr_ref[...] = dr
            df_scr[...] = dfb
            df_ref[...] = dfb
            sg = _colsum(dh * xhat)
            sb = _colsum(dh)

            @pl.when(i == 0)
            def _():
                dg_ref[...] = sg
                dbias_ref[...] = sb

            @pl.when(i > 0)
            def _():
                dg_ref[...] += sg
                dbias_ref[...] += sb

        dhm = _dot_nt(df_scr[...], wd_ref[...])
        da_ref[...] = (dhm * a_ref[...].astype(F32)).astype(BF16)
        db_ref[...] = (dhm * b_ref[...].astype(F32)).astype(BF16)

    row = pl.BlockSpec((tm, D), lambda i, j: (i, 0))
    vec = pl.BlockSpec((1, D), lambda i, j: (0, 0))
    mid = pl.BlockSpec((tm, tn), lambda i, j: (i, j))
    ins = [dh_a] + ([dh_b] if two else []) + [xhat, rstd, ln_g, a, b, wd]
    in_specs = [row] * (2 if two else 1) + [row, pl.BlockSpec((tm, 1), lambda i, j: (i, 0)), vec, mid, mid,
                                            pl.BlockSpec((tn, D), lambda i, j: (j, 0))]
    return pl.pallas_call(
        body,
        name=name,
        grid=(S // tm, F // tn),
        in_specs=in_specs,
        out_specs=[row, row, mid, mid, vec, vec],
        out_shape=[
            jax.ShapeDtypeStruct((S, D), F32),
            jax.ShapeDtypeStruct((S, D), BF16),
            jax.ShapeDtypeStruct((S, F), BF16),
            jax.ShapeDtypeStruct((S, F), BF16),
            jax.ShapeDtypeStruct((1, D), F32),
            jax.ShapeDtypeStruct((1, D), F32),
        ],
        scratch_shapes=[pltpu.VMEM((tm, D), BF16)],
        compiler_params=_cp(("arbitrary", "arbitrary"), 56),
    )(*ins)


def _ffn_bwd_dx(dr, da, db, wgt, wut, name, tm=512, tk=1408, dep=None):
    nk = F // tk
    n_dep = 0 if dep is None else 1

    def body(dr_ref, da_ref, db_ref, wg_ref, wu_ref, *rest):
        dx_ref, acc = rest[n_dep], rest[n_dep + 1]
        kk = pl.program_id(1)
        part = _dot_nn(da_ref[...], wg_ref[...]) + _dot_nn(db_ref[...], wu_ref[...])

        @pl.when(kk == 0)
        def _():
            acc[...] = ALPHA * dr_ref[...] + part

        @pl.when(kk > 0)
        def _():
            acc[...] += part

        @pl.when(kk == nk - 1)
        def _():
            dx_ref[...] = acc[...]

    row = pl.BlockSpec((tm, D), lambda i, kk: (i, 0))
    mid = pl.BlockSpec((tm, tk), lambda i, kk: (i, kk))
    wsp = pl.BlockSpec((tk, D), lambda i, kk: (kk, 0))
    return pl.pallas_call(
        body,
        name=name,
        grid=(S // tm, nk),
        in_specs=[row, mid, mid, wsp, wsp] + [pl.BlockSpec(memory_space=pl.ANY)] * n_dep,
        out_specs=row,
        out_shape=jax.ShapeDtypeStruct((S, D), F32),
        scratch_shapes=[pltpu.VMEM((tm, D), F32)],
        compiler_params=_cp(("parallel", "arbitrary"), 56),
    )(dr, da, db, wgt, wut, *([] if dep is None else [dep]))


def _rope_tables(pos_col, invf, sign, tm=512):
    def body(p_ref, f_ref, s_ref, c_out, s_out):
        ang = p_ref[...].astype(F32) * f_ref[...]
        c_out[...] = jnp.cos(ang)
        s_out[...] = jnp.sin(ang) * s_ref[...]

    vec = pl.BlockSpec((1, BLK), lambda i: (0, 0))
    out = pl.BlockSpec((tm, BLK), lambda i: (i, 0))
    return pl.pallas_call(
        body,
        name="rope_tables",
        grid=(S // tm,),
        in_specs=[pl.BlockSpec((tm, 1), lambda i: (i, 0)), vec, vec],
        out_specs=[out, out],
        out_shape=[jax.ShapeDtypeStruct((S, BLK), F32)] * 2,
        compiler_params=_cp(("parallel",)),
    )(pos_col, invf, sign)


def _lane_lo(rows=BLK):
    return (lax.broadcasted_iota(jnp.int32, (rows, GW), 1) % HEAD_DIM) < (HEAD_DIM // 2)


def _swap_halves(t, lo):
    return jnp.where(lo, pltpu.roll(t, GW - HEAD_DIM // 2, 1), pltpu.roll(t, HEAD_DIM // 2, 1))


def _rope(t, cosf, sinf, lo):
    return t * cosf + _swap_halves(t, lo) * sinf


def _unrope(g, cosf, sinf, lo):
    return g * cosf + _swap_halves(g * sinf, lo)


def _tile4(v):
    return jnp.concatenate([v, v, v, v], axis=1)


def _band_mask(n):
    qi = lax.broadcasted_iota(jnp.int32, (BLK, 2 * BLK), 0)
    kj = lax.broadcasted_iota(jnp.int32, (BLK, 2 * BLK), 1)
    dist = qi + BLK - kj
    return (dist >= 0) & (dist <= BLK) & ((kj >= BLK) | (n >= 1))


ROWS = 256
LANES = 128


def _to_view(scr, y, dst_ref, d, dtype, col0=0):
    if d == 1:
        dst_ref[:, col0:col0 + GW] = y.astype(dtype)
        return
    for cb in range(GW // LANES):
        scr[cb][...] = y[:, cb * LANES:(cb + 1) * LANES]
    for r in range(d):
        for cb in range(GW // LANES):
            at = col0 + r * GW + cb * LANES
            dst_ref[:, at:at + LANES] = scr[cb][pl.ds(r, ROWS // d, stride=d), :].astype(dtype)


def _from_view(scr, src_ref, d):
    if d == 1:
        return src_ref[...].astype(F32)
    for r in range(d):
        for cb in range(GW // LANES):
            at = r * GW + cb * LANES
            scr[cb][pl.ds(r, ROWS // d, stride=d), :] = src_ref[:, at:at + LANES].astype(F32)
    return jnp.concatenate([scr[cb][...] for cb in range(GW // LANES)], axis=1)


def _view_spec(d):
    return pl.BlockSpec((ROWS // d, d * GW), lambda i: (i, 0))


_VIEW_SCRATCH = [pltpu.VMEM((ROWS, LANES), F32)] * (GW // LANES)


def _qkv_prep(qkv, cos_t, sin_t):
    def body(x_ref, c_ref, s_ref, *rest):
        outs, scr = rest[:9], rest[9:]
        lo = _lane_lo(ROWS)
        cf, sf = _tile4(c_ref[...]), _tile4(s_ref[...])
        for gi, d in enumerate(PATTERN_DILATIONS):
            q, k, v = (x_ref[:, (3 * part + gi) * GW:(3 * part + gi + 1) * GW].astype(F32) for part in range(3))
            _to_view(scr, _rope(q, cf, sf, lo) * (HEAD_DIM ** -0.5), outs[3 * gi], d, BF16)
            _to_view(scr, _rope(k, cf, sf, lo), outs[3 * gi + 1], d, BF16)
            _to_view(scr, v, outs[3 * gi + 2], d, BF16)

    tab = pl.BlockSpec((ROWS, BLK), lambda i: (i, 0))
    outs = pl.pallas_call(
        body,
        name="qkv_prep",
        grid=(S // ROWS,),
        in_specs=[pl.BlockSpec((ROWS, QKV_W), lambda i: (i, 0)), tab, tab],
        out_specs=[_view_spec(d) for d in PATTERN_DILATIONS for _ in range(3)],
        out_shape=[jax.ShapeDtypeStruct((S // d, d * GW), BF16) for d in PATTERN_DILATIONS for _ in range(3)],
        scratch_shapes=_VIEW_SCRATCH,
        compiler_params=_cp(("parallel",), 48),
    )(qkv, cos_t, sin_t)
    return [tuple(outs[3 * gi:3 * gi + 3]) for gi in range(3)]


N_ATTN_STEPS = S // BLK


def _class_and_block(gi, t):
    per_class = N_ATTN_STEPS // PATTERN_DILATIONS[gi]
    return t >> (per_class.bit_length() - 1), t & (per_class - 1)


def _attn_specs(gi):
    def cur(t):
        r, n = _class_and_block(gi, t)
        return n, r

    def prev(t):
        r, n = _class_and_block(gi, t)
        return jnp.maximum(n - 1, 0), r

    def whole(t):
        return 0, _class_and_block(gi, t)[0]

    sub = S // PATTERN_DILATIONS[gi]
    return (pl.BlockSpec((BLK, GW), cur), pl.BlockSpec((BLK, GW), prev), pl.BlockSpec((sub, GW), whole))


def _left_lanes():
    return lax.broadcasted_iota(jnp.int32, (BLK, LANES), 1) < HEAD_DIM


def _stack_heads(t2, left):
    zero = jnp.zeros_like(t2)
    return jnp.concatenate([jnp.where(left, t2, zero), jnp.where(left, zero, t2)], axis=0)


def _attn_fwd_block(n, q_ref, kc_ref, kp_ref, vc_ref, vp_ref, o_ref, l_ref):
    mask = _band_mask(n)
    left = _left_lanes()
    zero = jnp.zeros((BLK, LANES), BF16)
    for pr in range(GW // LANES):
        cs = slice(pr * LANES, (pr + 1) * LANES)
        q2 = q_ref[:, cs]
        k2 = jnp.concatenate([kp_ref[:, cs], kc_ref[:, cs]], axis=0)
        v2 = jnp.concatenate([vp_ref[:, cs], vc_ref[:, cs]], axis=0)
        o_h, lse_h = [], []
        for side in (left, ~left):
            s = jnp.where(mask, _dot_nt(jnp.where(side, q2, zero), k2), NEG)
            m = jnp.max(s, axis=1, keepdims=True)
            p = jnp.exp(s - m)
            l = jnp.sum(p, axis=1, keepdims=True)
            o_h.append(_dot_nn((p / l).astype(BF16), v2))
            lse_h.append(m + jnp.log(l))
        o_ref[:, cs] = jnp.where(left, o_h[0], o_h[1])
        l_ref[:, cs] = jnp.where(left, lse_h[0], lse_h[1])


def _attn_fwd(preps):
    def body(*refs):
        t = pl.program_id(0)
        for gi in range(3):
            _attn_fwd_block(_class_and_block(gi, t)[1], *refs[5 * gi:5 * gi + 5], *refs[15 + 2 * gi:17 + 2 * gi])

    in_specs, out_specs, out_shape, args = [], [], [], []
    for gi, d in enumerate(PATTERN_DILATIONS):
        cur, prev, _ = _attn_specs(gi)
        qr, kr, vv = preps[gi]
        in_specs += [cur, cur, prev, cur, prev]
        args += [qr, kr, kr, vv, vv]
        out_specs += [cur, cur]
        out_shape += [jax.ShapeDtypeStruct((S // d, d * GW), F32)] * 2
    outs = pl.pallas_call(
        body,
        name="attn_fwd",
        grid=(N_ATTN_STEPS,),
        in_specs=in_specs,
        out_specs=out_specs,
        out_shape=out_shape,
        compiler_params=_cp(("arbitrary",), 48),
    )(*args)
    return [outs[0], outs[2], outs[4]], [outs[1], outs[3], outs[5]]


def _attn_bwd_block(n, q_ref, kc_ref, kp_ref, vc_ref, vp_ref, do_ref, l_ref, c_ref, dq_ref, dk_ref, dv_ref):
    mask = _band_mask(n)
    left = _left_lanes()
    left2 = jnp.concatenate([left, left], axis=0)
    here = pl.ds(pl.multiple_of(n * BLK, BLK), BLK)
    before = pl.ds(pl.multiple_of(jnp.maximum(n - 1, 0) * BLK, BLK), BLK)
    for pr in range(GW // LANES):
        cs = slice(pr * LANES, (pr + 1) * LANES)
        q2 = q_ref[:, cs]
        do2 = do_ref[:, cs]
        k_bd = _stack_heads(jnp.concatenate([kp_ref[:, cs], kc_ref[:, cs]], axis=0), left2)
        v_bd = _stack_heads(jnp.concatenate([vp_ref[:, cs], vc_ref[:, cs]], axis=0), left2)
        s2 = _dot_nt(q2, k_bd)
        dp2 = _dot_nt(do2, v_bd)
        ps, dss = [], []
        for h in range(2):
            at = pr * LANES + h * HEAD_DIM
            half = slice(h * 2 * BLK, (h + 1) * 2 * BLK)
            p = jnp.exp(jnp.where(mask, s2[:, half], NEG) - l_ref[:, at:at + 1])
            ps.append(p.astype(BF16))
            dss.append((p * (dp2[:, half] - c_ref[:, at:at + 1])).astype(BF16))
        ds_both = jnp.concatenate(dss, axis=1)
        dq_ref[:, cs] = _dot_nn(ds_both, k_bd) * (HEAD_DIM ** -0.5)
        dk_bd = _dot_tn(ds_both, q2)
        dv_bd = _dot_tn(jnp.concatenate(ps, axis=1), do2)
        dk2 = jnp.where(left2, dk_bd[:2 * BLK], dk_bd[2 * BLK:])
        dv2 = jnp.where(left2, dv_bd[:2 * BLK], dv_bd[2 * BLK:])
        dk_ref[here, cs] = dk2[BLK:]
        dv_ref[here, cs] = dv2[BLK:]
        dk_ref[before, cs] += dk2[:BLK]
        dv_ref[before, cs] += dv2[:BLK]


def _attn_bwd(preps, dos, lses, cterms):
    def body(*refs):
        t = pl.program_id(0)
        for gi in range(3):
            _attn_bwd_block(_class_and_block(gi, t)[1], *refs[8 * gi:8 * gi + 8], *refs[24 + 3 * gi:27 + 3 * gi])

    in_specs, out_specs, out_shape, args = [], [], [], []
    for gi, d in enumerate(PATTERN_DILATIONS):
        cur, prev, whole = _attn_specs(gi)
        qr, kr, vv = preps[gi]
        in_specs += [cur, cur, prev, cur, prev, cur, cur, cur]
        args += [qr, kr, kr, vv, vv, dos[gi], lses[gi], cterms[gi]]
        out_specs += [cur, whole, whole]
        out_shape += [jax.ShapeDtypeStruct((S // d, d * GW), F32)] * 3
    outs = pl.pallas_call(
        body,
        name="attn_bwd",
        grid=(N_ATTN_STEPS,),
        in_specs=in_specs,
        out_specs=out_specs,
        out_shape=out_shape,
        compiler_params=_cp(("arbitrary",), 56),
    )(*args)
    return [tuple(outs[3 * gi:3 * gi + 3]) for gi in range(3)]


def _qkv_unprep(grads, cos_t, sin_t, dproj):
    def body(*refs):
        views, (c_ref, s_ref, _, out_ref), scr = refs[:9], refs[9:13], refs[13:]
        lo = _lane_lo(ROWS)
        cf, sf = _tile4(c_ref[...]), _tile4(s_ref[...])
        for gi, d in enumerate(PATTERN_DILATIONS):
            dq_ref, dk_ref, dv_ref = views[3 * gi:3 * gi + 3]
            qa, ka, va = gi * GW, (3 + gi) * GW, (6 + gi) * GW
            out_ref[:, qa:qa + GW] = _unrope(_from_view(scr, dq_ref, d), cf, sf, lo).astype(BF16)
            out_ref[:, ka:ka + GW] = _unrope(_from_view(scr, dk_ref, d), cf, sf, lo).astype(BF16)
            out_ref[:, va:va + GW] = _from_view(scr, dv_ref, d).astype(BF16)

    tab = pl.BlockSpec((ROWS, BLK), lambda i: (i, 0))
    return pl.pallas_call(
        body,
        name="qkv_unprep",
        grid=(S // ROWS,),
        in_specs=[_view_spec(d) for d in PATTERN_DILATIONS for _ in range(3)] + [tab, tab, pl.BlockSpec(memory_space=pl.ANY)],
        out_specs=pl.BlockSpec((ROWS, QKV_W), lambda i: (i, 0)),
        out_shape=jax.ShapeDtypeStruct(dproj.shape, dproj.dtype),
        input_output_aliases={11: 0},
        scratch_shapes=_VIEW_SCRATCH,
        compiler_params=_cp(("parallel",), 48),
    )(*[g for grp in grads for g in grp], cos_t, sin_t, dproj)


def _group_weights(l0, l1, l2):
    mx = jnp.maximum(jnp.maximum(l0, l1), l2)
    e0, e1, e2 = jnp.exp(l0 - mx), jnp.exp(l1 - mx), jnp.exp(l2 - mx)
    inv = 1.0 / (e0 + e1 + e2)
    return e0 * inv, e1 * inv, e2 * inv


def _combine_fwd(os_, lses):
    def body(o0, o1, o2, l0, l1, l2, y_ref, *scr):
        ov = [_from_view(scr, o, d) for o, d in zip((o0, o1, o2), PATTERN_DILATIONS)]
        lv = [_from_view(scr, l, d) for l, d in zip((l0, l1, l2), PATTERN_DILATIONS)]
        w0, w1, w2 = _group_weights(*lv)
        y_ref[...] = (w0 * ov[0] + w1 * ov[1] + w2 * ov[2]).astype(BF16)

    views = [_view_spec(d) for d in PATTERN_DILATIONS]
    return pl.pallas_call(
        body,
        name="attn_combine_fwd",
        grid=(S // ROWS,),
        in_specs=views + views,
        out_specs=pl.BlockSpec((ROWS, GW), lambda i: (i, 0)),
        out_shape=jax.ShapeDtypeStruct((S, GW), BF16),
        scratch_shapes=_VIEW_SCRATCH,
        compiler_params=_cp(("parallel",)),
    )(*os_, *lses)


def _combine_bwd(dy, os_, lses, seg):
    def body(dy_ref, o0, o1, o2, l0, l1, l2, seg_ref, d0, d1, d2, c0, c1, c2, *scr):
        ov = [_from_view(scr, o, d) for o, d in zip((o0, o1, o2), PATTERN_DILATIONS)]
        lv = [_from_view(scr, l, d) for l, d in zip((l0, l1, l2), PATTERN_DILATIONS)]
        ws = _group_weights(*lv)
        dyv = dy_ref[...]
        t = dyv * (ws[0] * ov[0] + ws[1] * ov[1] + ws[2] * ov[2])
        t_hi = t.astype(BF16)
        r1 = t - t_hi.astype(F32)
        t_mid = r1.astype(BF16)
        t_lo = (r1 - t_mid.astype(F32)).astype(BF16)
        sg = seg_ref[...]
        e = _dot_nn(t_hi, sg) + _dot_nn(t_mid, sg) + _dot_nn(t_lo, sg)
        for w, do_ref, c_ref, d in zip(ws, (d0, d1, d2), (c0, c1, c2), PATTERN_DILATIONS):
            _to_view(scr, w * dyv, do_ref, d, BF16)
            _to_view(scr, w * e, c_ref, d, F32)

    views = [_view_spec(d) for d in PATTERN_DILATIONS]
    return pl.pallas_call(
        body,
        name="attn_combine_bwd",
        grid=(S // ROWS,),
        in_specs=[pl.BlockSpec((ROWS, GW), lambda i: (i, 0))] + views + views + [pl.BlockSpec((GW, GW), lambda i: (0, 0))],
        out_specs=views + views,
        out_shape=[jax.ShapeDtypeStruct((S // d, d * GW), BF16) for d in PATTERN_DILATIONS]
        + [jax.ShapeDtypeStruct((S // d, d * GW), F32) for d in PATTERN_DILATIONS],
        scratch_shapes=_VIEW_SCRATCH,
        compiler_params=_cp(("parallel",)),
    )(dy, *os_, *lses, seg)


_SQRT_HALF = 0.7071067811865476
_INV_SQRT_2PI = 0.3989422804014327


def _gelu(z):
    return 0.5 * z * (1.0 + lax.erf(z * _SQRT_HALF))


def _gelu_grad(z):
    return 0.5 * (1.0 + lax.erf(z * _SQRT_HALF)) + z * (_INV_SQRT_2PI * jnp.exp(-0.5 * z * z))


def _tril_ws(ws_ref, g):
    t = lax.broadcasted_iota(jnp.int32, (BLK, BLK), 0)
    s = lax.broadcasted_iota(jnp.int32, (BLK, BLK), 1)
    return jnp.where(t >= s, ws_ref[g], 0.0)


def _chunks_side_by_side(x, cols, nch):
    return jnp.concatenate([x[c * BLK:(c + 1) * BLK, cols] for c in range(nch)], axis=1)


def _gmlp_fwd(z, ws, bst, ln_g, ln_b, tm=512):
    nch = tm // BLK

    def body(z_ref, ws_ref, b_ref, g_ref, be_ref, y_ref):
        zg = _gelu(z_ref[...].astype(F32))
        u = zg[:, :D]
        vn, _, _ = _ln_fwd(zg[:, D:], g_ref[...], be_ref[...])
        vnb = vn.astype(BF16)
        bt = b_ref[...]
        for g in range(8):
            w = _tril_ws(ws_ref, g).astype(BF16)
            cols = slice(g * BLK, (g + 1) * BLK)
            mixed = _dot_nn(w, _chunks_side_by_side(vnb, cols, nch)) + bt[:, g:g + 1]
            for c in range(nch):
                rows = slice(c * BLK, (c + 1) * BLK)
                y_ref[rows, cols] = (u[rows, cols] * mixed[:, c * BLK:(c + 1) * BLK]).astype(BF16)

    return pl.pallas_call(
        body,
        name="gmlp_fwd",
        grid=(S // tm,),
        in_specs=[
            pl.BlockSpec((tm, 2 * D), lambda i: (i, 0)),
            pl.BlockSpec((8, BLK, BLK), lambda i: (0, 0, 0)),
            pl.BlockSpec((BLK, 8), lambda i: (0, 0)),
            pl.BlockSpec((1, D), lambda i: (0, 0)),
            pl.BlockSpec((1, D), lambda i: (0, 0)),
        ],
        out_specs=pl.BlockSpec((tm, D), lambda i: (i, 0)),
        out_shape=jax.ShapeDtypeStruct((S, D), BF16),
        compiler_params=_cp(("parallel",), 48),
    )(z, ws, bst, ln_g, ln_b)


def _gmlp_bwd(z, dy, ws, bst, ln_g, ln_b, dproj, tm=512):
    nch = tm // BLK

    def body(z_ref, dy_ref, ws_ref, b_ref, g_ref, be_ref, _, dz_ref, dws_ref, dbs_ref, dg_ref, dbe_ref, dvn_scr, dm_acc):
        i = pl.program_id(0)
        zv = z_ref[...].astype(F32)
        zg = _gelu(zv)
        u = zg[:, :D]
        gam = g_ref[...]
        vn, xhat, rstd = _ln_fwd(zg[:, D:], gam, be_ref[...])
        vnb = vn.astype(BF16)
        dyv = dy_ref[...]
        dmix = dyv * u
        dmb = dmix.astype(BF16)
        bt = b_ref[...]
        tmask = lax.broadcasted_iota(jnp.int32, (BLK, BLK), 0) >= lax.broadcasted_iota(jnp.int32, (BLK, BLK), 1)
        dm_sum = dmix[0:BLK]
        for c in range(1, nch):
            dm_sum = dm_sum + dmix[c * BLK:(c + 1) * BLK]

        @pl.when(i == 0)
        def _():
            dm_acc[...] = jnp.zeros_like(dm_acc)
            dws_ref[...] = jnp.zeros_like(dws_ref)
            dg_ref[...] = jnp.zeros_like(dg_ref)
            dbe_ref[...] = jnp.zeros_like(dbe_ref)

        dm_acc[...] += dm_sum
        dus = []
        for g in range(8):
            w = _tril_ws(ws_ref, g).astype(BF16)
            cols = slice(g * BLK, (g + 1) * BLK)
            v_cat = _chunks_side_by_side(vnb, cols, nch)
            dm_cat = _chunks_side_by_side(dmb, cols, nch)
            mixed = _dot_nn(w, v_cat) + bt[:, g:g + 1]
            dus.append(jnp.concatenate(
                [dyv[c * BLK:(c + 1) * BLK, cols] * mixed[:, c * BLK:(c + 1) * BLK] for c in range(nch)], axis=0))
            dws_ref[g] += jnp.where(tmask, _dot_nt(dm_cat, v_cat), 0.0)
            dvn_cat = _dot_tn(w, dm_cat)
            for c in range(nch):
                dvn_scr[c * BLK:(c + 1) * BLK, cols] = dvn_cat[:, c * BLK:(c + 1) * BLK]

        dvn = dvn_scr[...]
        dg_ref[...] += _colsum(dvn * xhat)
        dbe_ref[...] += _colsum(dvn)
        dvg = _ln_bwd(dvn, xhat, rstd, gam)
        gp = _gelu_grad(zv)
        dz_ref[:, :D] = (jnp.concatenate(dus, axis=1) * gp[:, :D]).astype(BF16)
        dz_ref[:, D:] = (dvg * gp[:, D:]).astype(BF16)

        @pl.when(i == S // tm - 1)
        def _():
            acc = dm_acc[...]
            for g in range(8):
                dbs_ref[:, g:g + 1] = jnp.sum(acc[:, g * BLK:(g + 1) * BLK], axis=1, keepdims=True)

    vec = pl.BlockSpec((1, D), lambda i: (0, 0))
    return pl.pallas_call(
        body,
        name="gmlp_bwd",
        grid=(S // tm,),
        in_specs=[
            pl.BlockSpec((tm, 2 * D), lambda i: (i, 0)),
            pl.BlockSpec((tm, D), lambda i: (i, 0)),
            pl.BlockSpec((8, BLK, BLK), lambda i: (0, 0, 0)),
            pl.BlockSpec((BLK, 8), lambda i: (0, 0)),
            vec,
            vec,
            pl.BlockSpec(memory_space=pl.ANY),
        ],
        out_specs=[
            pl.BlockSpec((tm, 2 * D), lambda i: (i, DPROJ_Z_COL)),
            pl.BlockSpec((8, BLK, BLK), lambda i: (0, 0, 0)),
            pl.BlockSpec((BLK, 8), lambda i: (0, 0)),
            vec,
            vec,
        ],
        out_shape=[
            jax.ShapeDtypeStruct(dproj.shape, dproj.dtype),
            jax.ShapeDtypeStruct((8, BLK, BLK), F32),
            jax.ShapeDtypeStruct((BLK, 8), F32),
            jax.ShapeDtypeStruct((1, D), F32),
            jax.ShapeDtypeStruct((1, D), F32),
        ],
        input_output_aliases={6: 0},
        scratch_shapes=[pltpu.VMEM((tm, D), F32), pltpu.VMEM((BLK, D), F32)],
        compiler_params=_cp(("arbitrary",), 48),
    )(z, dy, ws, bst, ln_g, ln_b, dproj)


def _merge_fwd(ya, yg, glog, bgate, h1, wabt, wgb, wo, ln_g, ln_b, tm=256):
    def body(ya_ref, yg_ref, gl_ref, bg_ref, h1_ref, wab_ref, wgb_ref, wo_ref, g_ref, b_ref,
             h_ref, hb_ref, xh_ref, rs_ref, mg_ref, bra_ref, brg_ref):
        bra = _dot_nt(ya_ref[...], wab_ref[...])
        brg = _dot_nn(yg_ref[...], wgb_ref[...])
        gates = _sigmoid(gl_ref[...].astype(F32) + bg_ref[...])
        merged = (gates[:, :D] * bra + gates[:, D:] * brg).astype(BF16)
        mix = _dot_nn(merged, wo_ref[...])
        h, xhat, rstd = _ln_fwd(ALPHA * h1_ref[...] + mix, g_ref[...], b_ref[...])
        h_ref[...] = h
        hb_ref[...] = h.astype(BF16)
        xh_ref[...] = xhat
        rs_ref[...] = rstd
        mg_ref[...] = merged
        bra_ref[...] = bra
        brg_ref[...] = brg

    row = pl.BlockSpec((tm, D), lambda i: (i, 0))
    vec = pl.BlockSpec((1, D), lambda i: (0, 0))
    full = lambda shape: pl.BlockSpec(shape, lambda i: (0, 0))
    return pl.pallas_call(
        body,
        name="merge_fwd",
        grid=(S // tm,),
        in_specs=[
            pl.BlockSpec((tm, GW), lambda i: (i, 0)), row,
            pl.BlockSpec((tm, 2 * D), lambda i: (i, glog.shape[1] // (2 * D) - 1)),
            full((1, 2 * D)), row,
            full((D, GW)), full((D, D)), full((D, D)), vec, vec,
        ],
        out_specs=[row, row, row, pl.BlockSpec((tm, 1), lambda i: (i, 0)), row, row, row],
        out_shape=[
            jax.ShapeDtypeStruct((S, D), F32),
            jax.ShapeDtypeStruct((S, D), BF16),
            jax.ShapeDtypeStruct((S, D), F32),
            jax.ShapeDtypeStruct((S, 1), F32),
            jax.ShapeDtypeStruct((S, D), BF16),
            jax.ShapeDtypeStruct((S, D), F32),
            jax.ShapeDtypeStruct((S, D), F32),
        ],
        compiler_params=_cp(("parallel",), 48),
    )(ya, yg, glog, bgate, h1, wabt, wgb, wo, ln_g, ln_b)


def _merge_bwd(dh2, xhat, rstd, ln_g, bra, brg, glog, bgate, wabt, wgb, wo, tm=256):
    def body(dh_ref, xh_ref, rs_ref, g_ref, bra_ref, brg_ref, gl_ref, bg_ref, wab_ref, wgb_ref, wo_ref,
             dr_ref, drb_ref, dlog_ref, dba_ref, dbg_ref, dya_ref, dyg_ref, dbgate_ref, dg_ref, dbias_ref):
        i = pl.program_id(0)
        dh = dh_ref[...]
        xh = xh_ref[...]
        dr = _ln_bwd(dh, xh, rs_ref[...], g_ref[...])
        drb = dr.astype(BF16)
        dr_ref[...] = dr
        drb_ref[...] = drb
        dmerged = _dot_nt(drb, wo_ref[...])
        gates = _sigmoid(gl_ref[...].astype(F32) + bg_ref[...])
        g0, g1 = gates[:, :D], gates[:, D:]
        dl0 = dmerged * bra_ref[...] * g0 * (1.0 - g0)
        dl1 = dmerged * brg_ref[...] * g1 * (1.0 - g1)
        dlog_ref[:, :D] = dl0.astype(BF16)
        dlog_ref[:, D:] = dl1.astype(BF16)
        dba = (dmerged * g0).astype(BF16)
        dbg = (dmerged * g1).astype(BF16)
        dba_ref[...] = dba
        dbg_ref[...] = dbg
        dya_ref[...] = _dot_nn(dba, wab_ref[...])
        dyg_ref[...] = _dot_nt(dbg, wgb_ref[...])
        s0, s1 = _colsum(dl0), _colsum(dl1)
        sg, sb = _colsum(dh * xh), _colsum(dh)

        @pl.when(i == 0)
        def _():
            dbgate_ref[:, :D] = s0
            dbgate_ref[:, D:] = s1
            dg_ref[...] = sg
            dbias_ref[...] = sb

        @pl.when(i > 0)
        def _():
            dbgate_ref[:, :D] += s0
            dbgate_ref[:, D:] += s1
            dg_ref[...] += sg
            dbias_ref[...] += sb

    row = pl.BlockSpec((tm, D), lambda i: (i, 0))
    vec = pl.BlockSpec((1, D), lambda i: (0, 0))
    wide = pl.BlockSpec((tm, 2 * D), lambda i: (i, 0))
    full = lambda shape: pl.BlockSpec(shape, lambda i: (0, 0))
    return pl.pallas_call(
        body,
        name="merge_bwd",
        grid=(S // tm,),
        in_specs=[row, row, pl.BlockSpec((tm, 1), lambda i: (i, 0)), vec, row, row,
                  pl.BlockSpec((tm, 2 * D), lambda i: (i, glog.shape[1] // (2 * D) - 1)),
                  full((1, 2 * D)), full((D, GW)), full((D, D)), full((D, D))],
        out_specs=[row, row, pl.BlockSpec((tm, 2 * D), lambda i: (i, DPROJ_G_COL)), row, row,
                   pl.BlockSpec((tm, GW), lambda i: (i, 0)), row, full((1, 2 * D)), vec, vec],
        out_shape=[
            jax.ShapeDtypeStruct((S, D), F32),
            jax.ShapeDtypeStruct((S, D), BF16),
            jax.ShapeDtypeStruct((S, DPROJ_W), BF16),
            jax.ShapeDtypeStruct((S, D), BF16),
            jax.ShapeDtypeStruct((S, D), BF16),
            jax.ShapeDtypeStruct((S, GW), F32),
            jax.ShapeDtypeStruct((S, D), F32),
            jax.ShapeDtypeStruct((1, 2 * D), F32),
            jax.ShapeDtypeStruct((1, D), F32),
            jax.ShapeDtypeStruct((1, D), F32),
        ],
        compiler_params=_cp(("arbitrary",), 48),
    )(dh2, xhat, rstd, ln_g, bra, brg, glog, bgate, wabt, wgb, wo)


def _dproj_to_dh1(dproj, win, dep, tm=512):
    n_dep = 0 if dep is None else 1

    def body(a_ref, b_ref, *rest):
        rest[n_dep][...] = (_dot_nn(a_ref[:, :QKV_W], b_ref[:QKV_W, :])
                            + _dot_nn(a_ref[:, DPROJ_ZG_AT:], b_ref[QKV_W:, :]))

    return pl.pallas_call(
        body,
        name="dproj_to_dh1",
        grid=(S // tm,),
        in_specs=[pl.BlockSpec((tm, DPROJ_W), lambda i: (i, 0)),
                  pl.BlockSpec((IN_W, D), lambda i: (0, 0), pipeline_mode=pl.Buffered(1))]
        + [pl.BlockSpec(memory_space=pl.ANY)] * n_dep,
        out_specs=pl.BlockSpec((tm, D), lambda i: (i, 0)),
        out_shape=jax.ShapeDtypeStruct((S, D), F32),
        compiler_params=_cp(("parallel",), 56),
    )(dproj, win, *([] if dep is None else [dep]))


def _tie(x, dep):
    if dep is None:
        return x
    return x + dep[0, 0].astype(x.dtype)


def _local_step(x, pos_col, target, get_w, p, emit):
    w = get_w("ffn1", None)
    a1, b1, hm1 = _ffn_up(x, w["g1"], w["u1"], "ffn1_up")
    w.update(get_w("ffn1d", hm1))
    h1, h1b, xh1, rs1 = _ffn_down(hm1, w["d1"], x, p["ln1_g"], p["ln1_b"], "ffn1_down")

    w.update(get_w("win", h1b))
    qkv = _matmul(h1b, w["win"], "nt", BF16, S, 1536, D, "proj_qkv", b_off=0, n_out=QKV_W)
    z = glog = _matmul(h1b, w["win"], "nt", BF16, S, 512, D, "proj_zg", b_off=QKV_W // 512, n_out=4 * D)

    half = jnp.arange(0, HEAD_DIM, 2, dtype=F32) / HEAD_DIM
    inv_freq = ROPE_THETA ** (-half)
    invf = jnp.tile(inv_freq, 4).reshape(1, BLK)
    sign = jnp.tile(jnp.concatenate([-jnp.ones((32,), F32), jnp.ones((32,), F32)]), 2).reshape(1, BLK)
    cos_t, sin_t = _rope_tables(pos_col, invf, sign)

    preps = _qkv_prep(qkv, cos_t, sin_t)
    os_, lses = _attn_fwd(preps)
    ya = _combine_fwd(os_, lses)
    get_w("late", ya, early=True)
    bst = p["gmlp_b_s"].T
    yg = _gmlp_fwd(z, p["gmlp_w_s"], bst, p["gmlp_ln_g"], p["gmlp_ln_b"])
    w.update(get_w("late", yg))
    h2, h2b, xh2, rs2, merged, bra, brg = _merge_fwd(ya, yg, glog, p["b_gates"], h1, w["ab"], w["gb"], w["o"],
                                                      p["ln2_g"], p["ln2_b"])
    a2, b2, hm2 = _ffn_up(h2b, w["g2"], w["u2"], "ffn2_up")
    dh3, loss, xh3, rs3 = _ffn_down(hm2, w["d2"], h2, p["ln3_g"], p["ln3_b"], "ffn2_down", target=target)

    gp = {}
    dr3, df2, da2, db2, gp["ln3_g"], gp["ln3_b"] = _ffn_bwd_mid(dh3, None, xh3, rs3, p["ln3_g"], a2, b2, w["d2"],
                                                                "ffn2_bwd_mid")
    tok = emit("ffn2", {
        "g2": _matmul(da2, h2b, "tn", BF16, 1408, D, S, "wgrad_g2"),
        "u2": _matmul(db2, h2b, "tn", BF16, 1408, D, S, "wgrad_u2"),
        "d2": _matmul(hm2, df2, "tn", BF16, 1408, D, S, "wgrad_d2")})
    dh2 = _ffn_bwd_dx(dr3, da2, db2, w["g2"], w["u2"], "ffn2_bwd_dx")

    (dr2, dr2b, dproj, dba, dbg, dya, dyg, gp["b_gates"], gp["ln2_g"], gp["ln2_b"]) = _merge_bwd(
        dh2, xh2, rs2, _tie(p["ln2_g"], tok), bra, brg, glog, p["b_gates"], w["ab"], w["gb"], w["o"])
    tok = emit("mix", {
        "o": _matmul(merged, dr2b, "tn", BF16, 512, D, S, "wgrad_o"),
        "ab": _matmul(dba, ya, "tn", BF16, 512, GW, S, "wgrad_ab"),
        "gb": _matmul(yg, dbg, "tn", BF16, 512, D, S, "wgrad_gb")})

    seg = (jnp.arange(GW)[:, None] // HEAD_DIM == jnp.arange(GW)[None, :] // HEAD_DIM).astype(BF16)
    do0, do1, do2, c0, c1, c2 = _combine_bwd(dya, os_, lses, _tie(seg, tok))
    dproj = _qkv_unprep(_attn_bwd(preps, (do0, do1, do2), lses, (c0, c1, c2)), cos_t, sin_t, dproj)
    dproj, gp["gmlp_w_s"], dbst, gp["gmlp_ln_g"], gp["gmlp_ln_b"] = _gmlp_bwd(
        z, dyg, p["gmlp_w_s"], bst, p["gmlp_ln_g"], p["gmlp_ln_b"], dproj)
    gp["gmlp_b_s"] = dbst.T
    tok = emit("win", {"win": _matmul(dproj, h1b, "tn", BF16, 512, D, S, "wgrad_win", a_map=_dproj_tile, m_out=IN_W)})
    dh1m = _dproj_to_dh1(dproj, w["win"], tok)

    dr1, df1, da1, db1, gp["ln1_g"], gp["ln1_b"] = _ffn_bwd_mid(dr2, dh1m, xh1, rs1, p["ln1_g"], a1, b1, w["d1"],
                                                                "ffn1_bwd_mid")
    tok = emit("small", {**gp, "loss": loss})
    tok = emit("g1", {"g1": _matmul(da1, x, "tn", BF16, 1408, D, S, "wgrad_g1", dep=tok)})
    tok = emit("ud1", {"u1": _matmul(db1, x, "tn", BF16, 1408, D, S, "wgrad_u1", dep=tok),
                       "d1": _matmul(hm1, df1, "tn", BF16, 1408, D, S, "wgrad_d1", dep=tok)})
    dx = _ffn_bwd_dx(dr1, da1, db1, w["g1"], w["u1"], "ffn1_bwd_dx", dep=tok)
    return loss, dx, gp


_FLIPS = [(mx, my, mc) for mx in (0, 1) for my in (0, 1) for mc in (0, 1)][1:]
_GROUPS = {"ffn1": ("g1", "u1"), "ffn1d": ("d1",), "win": ("win",), "late": ("ab", "gb", "o", "g2", "u2", "d2")}
_SCATTERS = {"ffn2": ("g2", "u2", "d2"), "mix": ("ab", "gb", "o"), "win": ("win",), "g1": ("g1",), "ud1": ("u1", "d1")}
_TWO_STAGE = ("g1", "ud1")
_HBM = pl.BlockSpec(memory_space=pltpu.HBM)
_SEM = pl.BlockSpec(memory_space=pltpu.SEMAPHORE)
_EFFECT = pltpu.SideEffectType.DATAFLOW_SIDE_EFFECTING


def _me():
    return 4 * lax.axis_index("x") + 2 * lax.axis_index("y") + lax.axis_index("c")


def _copies_start(bufs, copies, n_sem, name, after=None):
    nb = len(bufs)
    n_after = 0 if after is None else 1

    def body(*refs):
        b = refs[:nb]
        send_sems, recv_sems = refs[nb + n_after], refs[nb + n_after + 1]
        token = refs[-1]
        x, y, c = lax.axis_index("x"), lax.axis_index("y"), lax.axis_index("c")
        me = 4 * x + 2 * y + c
        for si, s_slot, di, d_slot, (mx, my, mc), sem in copies:
            s_idx, d_idx = s_slot(me), d_slot(me)
            pltpu.make_async_remote_copy(
                src_ref=b[si] if s_idx is None else b[si].at[s_idx],
                dst_ref=b[di] if d_idx is None else b[di].at[d_idx],
                send_sem=send_sems.at[sem], recv_sem=recv_sems.at[sem],
                device_id=(x ^ mx, y ^ my, c ^ mc), device_id_type=MESH).start()
        token[...] = jnp.zeros_like(token)

    ins = [pltpu.with_memory_space_constraint(a, pltpu.HBM) for a in bufs]
    outs = pl.pallas_call(
        body,
        name=name,
        in_specs=[_HBM] * nb + [pl.BlockSpec(memory_space=pl.ANY)] * n_after,
        out_specs=[_SEM, _SEM] + [_HBM] * nb + [pl.BlockSpec(memory_space=pltpu.VMEM)],
        out_shape=[pltpu.SemaphoreType.DMA((n_sem,)), pltpu.SemaphoreType.DMA((n_sem,))]
        + [pltpu.HBM(a.shape, a.dtype) for a in ins] + [jax.ShapeDtypeStruct((8, 128), F32)],
        input_output_aliases={k: 2 + k for k in range(nb)},
        compiler_params=pltpu.CompilerParams(has_side_effects=_EFFECT),
    )(*ins, *([] if after is None else [after]))
    return outs[0], outs[1], list(outs[2:2 + nb]), outs[-1]


def _copies_wait(started, waits, after, name):
    send_sems, recv_sems, bufs, _ = started
    nb = len(bufs)

    def body(*refs):
        b = refs[:nb]
        ss, rs = refs[nb], refs[nb + 1]
        me3 = (lax.axis_index("x"), lax.axis_index("y"), lax.axis_index("c"))
        for bi, n_blocks, sem, is_send in waits:
            blocks = b[bi].at[pl.ds(0, n_blocks)]
            cp = pltpu.make_async_remote_copy(src_ref=blocks, dst_ref=blocks, send_sem=ss.at[sem], recv_sem=rs.at[sem],
                                              device_id=me3, device_id_type=MESH)
            if is_send:
                cp.wait_send()
            else:
                cp.wait_recv()

    return pl.pallas_call(
        body,
        name=name,
        in_specs=[_HBM] * nb + [_SEM, _SEM, pl.BlockSpec(memory_space=pl.ANY)],
        out_specs=[_HBM] * nb,
        out_shape=[pltpu.HBM(a.shape, a.dtype) for a in bufs],
        input_output_aliases={k: k for k in range(nb)},
        compiler_params=pltpu.CompilerParams(has_side_effects=_EFFECT),
    )(*bufs, send_sems, recv_sems, after)


def _landing(own, me):
    return lax.dynamic_update_slice(lax.empty((N_DEV,) + own.shape[1:], own.dtype), own, (me, 0, 0))


_SIBLING = (0, 0, 1)
_OTHER_CHIPS = ((1, 0, 0), (0, 1, 0), (1, 1, 0))


def _bits(flip):
    return 4 * flip[0] + 2 * flip[1] + flip[2]


def _gather_send(shards, grp, after):
    nw = len(shards)
    me = _me()
    bufs = list(shards) + [_landing(a[None], me) for a in shards]
    copies = []
    for k in range(nw):
        copies.append((k, lambda me: None, nw + k, lambda me: me, _SIBLING, nw + k))
        copies += [(k, lambda me: None, nw + k, lambda me: me, f, k) for f in _OTHER_CHIPS]
    return _copies_start(bufs, copies, 2 * nw, "gather_send_" + grp, after)


def _gather_pass(started, grp, after):
    nw = len(started[2]) // 2
    waits = []
    for k in range(nw):
        waits += [(nw + k, 3, k, True), (nw + k, 1, nw + k, True), (nw + k, 3, k, False), (nw + k, 1, nw + k, False)]
    lands = list(_copies_wait(started, waits, after, "gather_arrived_" + grp)[nw:])
    copies = []
    for k in range(nw):
        for f in _OTHER_CHIPS:
            slot = functools.partial(lambda me, bits: me ^ bits, bits=_bits(f))
            copies.append((k, slot, k, slot, _SIBLING, k))
    return _copies_start(lands, copies, nw, "gather_pass_" + grp)


def _gather_finish(passed, grp, after):
    nw = len(passed[2])
    waits = [(k, 3, k, is_send) for k in range(nw) for is_send in (True, False)]
    return _copies_wait(passed, waits, passed[3] if after is None else after, "gather_done_" + grp)


def _scatter_send(parts, grp, after=None):
    nw = len(parts)
    me = _me()
    bufs = list(parts) + [_landing(lax.dynamic_index_in_dim(a, me, 0, keepdims=True), me) for a in parts]
    copies = []
    for k in range(nw):
        for f in _FLIPS:
            to = functools.partial(lambda me, bits: me ^ bits, bits=_bits(f))
            copies.append((k, to, nw + k, lambda me: me, f, k))
    return _copies_start(bufs, copies, nw, "scatter_send_" + grp, after)


def _scatter_finish(started, grp, after):
    nw = len(started[2]) // 2
    waits = [(nw + k, N_DEV - 1, k, is_send) for k in range(nw) for is_send in (True, False)]
    return _copies_wait(started, waits, after, "scatter_done_" + grp)[nw:]


N_CHIP = N_DEV // 2


def _pair_sum(part, other, key):
    _, r, c = part.shape
    tm = _ADAM_ROWS.get(r, r)
    core = lax.axis_index("c").astype(jnp.int32).reshape(1)

    def body(core_ref, a_ref, b_ref, o_ref):
        o_ref[...] = (a_ref[...].astype(F32) + b_ref[...].astype(F32)).astype(BF16)

    return pl.pallas_call(
        body,
        name="pair_sum_" + key,
        grid_spec=pltpu.PrefetchScalarGridSpec(
            num_scalar_prefetch=1,
            grid=(N_CHIP, r // tm),
            in_specs=[pl.BlockSpec((None, tm, c), lambda q, i, core_ref: (2 * q + core_ref[0], i, 0)),
                      pl.BlockSpec((None, tm, c), lambda q, i, core_ref: (q, i, 0))],
            out_specs=pl.BlockSpec((None, tm, c), lambda q, i, core_ref: (q, i, 0)),
        ),
        out_shape=jax.ShapeDtypeStruct((N_CHIP, r, c), BF16),
        compiler_params=_cp(("parallel", "parallel")),
    )(core, part, other)


def _scatter2_send(parts, grp, keys):
    nw = len(parts)
    chip = _me() // 2
    swap = []
    for k in range(nw):
        for q in range(N_CHIP):
            src = functools.partial(lambda me, q: 2 * q + 1 - me % 2, q=q)
            swap.append((k, src, nw + k, functools.partial(lambda me, q: q, q=q), _SIBLING, k))
    others = [lax.empty((N_CHIP,) + a.shape[1:], a.dtype) for a in parts]
    started = _copies_start(list(parts) + others, swap, nw, "scatter_swap_" + grp)
    waits = [(nw + k, N_CHIP, k, is_send) for k in range(nw) for is_send in (True, False)]
    swapped = _copies_wait(started, waits, started[3], "scatter_swapped_" + grp)
    pairs = [_pair_sum(swapped[k], swapped[nw + k], keys[k]) for k in range(nw)]
    lands = [lax.dynamic_update_slice(lax.empty(p.shape, p.dtype), lax.dynamic_index_in_dim(p, chip, 0, keepdims=True),
                                      (chip, 0, 0)) for p in pairs]
    copies = []
    for k in range(nw):
        for f in _OTHER_CHIPS:
            to = functools.partial(lambda me, bits: (me ^ bits) // 2, bits=_bits(f))
            copies.append((k, to, nw + k, lambda me: me // 2, f, k))
    return _copies_start(pairs + lands, copies, nw, "scatter_send_" + grp)


def _scatter2_finish(started, grp, after):
    nw = len(started[2]) // 2
    waits = [(nw + k, N_CHIP - 1, k, is_send) for k in range(nw) for is_send in (True, False)]
    return _copies_wait(started, waits, after, "scatter_done_" + grp)[nw:]


_ADAM_ROWS = {352: 176, 1088: 272}


def _sum_adamw(parts, wv, m, v, name, dep=None):
    n_parts, r, c = parts.shape
    tm = _ADAM_ROWS.get(r, r)
    assert r % tm == 0 and wv.shape == (r, c)
    n_dep = 0 if dep is None else 1

    def body(p_ref, w_ref, m_ref, v_ref, *rest):
        g_ref, d_ref, mo_ref, vo_ref = rest[n_dep:]
        gv = p_ref[0].astype(F32)
        for j in range(1, n_parts):
            gv = gv + p_ref[j].astype(F32)
        g_ref[...] = gv
        mn = ADAM_B1 * m_ref[...] + (1.0 - ADAM_B1) * gv
        vn = ADAM_B2 * v_ref[...] + (1.0 - ADAM_B2) * (gv * gv)
        m_hat = mn / (1.0 - ADAM_B1 ** ADAM_STEP)
        v_hat = vn / (1.0 - ADAM_B2 ** ADAM_STEP)
        d_ref[...] = -ADAM_LR * (m_hat / (jnp.sqrt(v_hat) + ADAM_EPS) + ADAM_WD * w_ref[...])
        mo_ref[...] = mn
        vo_ref[...] = vn

    sp = pl.BlockSpec((tm, c), lambda i: (i, 0))
    return pl.pallas_call(
        body,
        name=name,
        grid=(r // tm,),
        in_specs=[pl.BlockSpec((n_parts, tm, c), lambda i: (0, i, 0))] + [sp] * 3 + [pl.BlockSpec(memory_space=pl.ANY)] * n_dep,
        out_specs=[sp] * 4,
        out_shape=[jax.ShapeDtypeStruct((r, c), F32)] * 4,
        compiler_params=_cp(("parallel",), 48),
    )(parts, wv, m, v, *([] if dep is None else [dep]))


_WEIGHTS = ["ffn1_w_gate", "ffn1_w_up", "ffn1_w_down", "ln1_g", "ln1_b", "w_in", "b_gates", "gmlp_ln_g", "gmlp_ln_b",
            "gmlp_w_s", "gmlp_b_s", "w_attn_branch", "w_gmlp_branch", "w_out", "ln2_g", "ln2_b", "ffn2_w_gate",
            "ffn2_w_up", "ffn2_w_down", "ln3_g", "ln3_b"]
_BIG_OF = {"ffn1_w_gate": ("g1", True), "ffn1_w_up": ("u1", True), "ffn1_w_down": ("d1", False), "w_in": ("win", True),
           "w_attn_branch": ("ab", True), "w_gmlp_branch": ("gb", False), "w_out": ("o", False),
           "ffn2_w_gate": ("g2", True), "ffn2_w_up": ("u2", True), "ffn2_w_down": ("d2", False)}
_SMALL = [n for n in _WEIGHTS if n not in _BIG_OF]
_SMALL_ROWS = {"gmlp_w_s": 128, "b_gates": 2}
_SMALL_SLOT = 8


def _pack_small(d, last=None):
    rows = []
    for n in _SMALL:
        r = d[n].reshape(-1, D)
        slot = max(r.shape[0], _SMALL_SLOT)
        rows.append(jnp.pad(r, ((0, slot - r.shape[0]), (0, 0))))
    rows.append(jnp.zeros((_SMALL_SLOT, D), F32) if last is None else jnp.broadcast_to(last.reshape(1, 1), (_SMALL_SLOT, D)))
    return jnp.concatenate(rows, axis=0)


def _unpack_small(packed, shapes):
    out, at = {}, 0
    for n in _SMALL:
        k = _SMALL_ROWS.get(n, 1)
        out[n] = packed[at:at + k].reshape(shapes[n])
        at += max(k, _SMALL_SLOT)
    return out


def kernel(x, positions, ffn1_w_gate, ffn1_w_up, ffn1_w_down, ln1_g, ln1_b, w_in, b_gates, gmlp_ln_g, gmlp_ln_b, gmlp_w_s, gmlp_b_s, w_attn_branch, w_gmlp_branch, w_out, ln2_g, ln2_b, ffn2_w_gate, ffn2_w_up, ffn2_w_down, ln3_g, ln3_b, loss_target, m_ffn1_w_gate, m_ffn1_w_up, m_ffn1_w_down, m_ln1_g, m_ln1_b, m_w_in, m_b_gates, m_gmlp_ln_g, m_gmlp_ln_b, m_gmlp_w_s, m_gmlp_b_s, m_w_attn_branch, m_w_gmlp_branch, m_w_out, m_ln2_g, m_ln2_b, m_ffn2_w_gate, m_ffn2_w_up, m_ffn2_w_down, m_ln3_g, m_ln3_b, v_ffn1_w_gate, v_ffn1_w_up, v_ffn1_w_down, v_ln1_g, v_ln1_b, v_w_in, v_b_gates, v_gmlp_ln_g, v_gmlp_ln_b, v_gmlp_w_s, v_gmlp_b_s, v_w_attn_branch, v_w_gmlp_branch, v_w_out, v_ln2_g, v_ln2_b, v_ffn2_w_gate, v_ffn2_w_up, v_ffn2_w_down, v_ln3_g, v_ln3_b):
    args = dict(locals())
    wts = {n: args[n] for n in _WEIGHTS}
    ms = {n: args["m_" + n] for n in _WEIGHTS}
    vs = {n: args["v_" + n] for n in _WEIGHTS}

    name_of = {key: (n, tr) for n, (key, tr) in _BIG_OF.items()}

    started, tok = {}, None
    for grp, keys in _GROUPS.items():
        shards = []
        for key in keys:
            n, tr = name_of[key]
            s2 = wts[n][0]
            shards.append(_tie(s2.T if tr else s2, tok).astype(BF16))
        started[grp] = _gather_send(shards, grp, tok)
        tok = started[grp][3]
    all_started = tok
    passed = {}

    def get_w(grp, after, early=False):
        if grp not in passed:
            passed[grp] = _gather_pass(started[grp], grp, all_started if after is None else after)
            after = None
        if early:
            return None
        lands = _gather_finish(passed[grp], grp, after)
        return {key: g.reshape(-1, g.shape[-1]) for key, g in zip(_GROUPS[grp], lands)}

    sent = {}

    def emit(grp, grads):
        if grp == "small":
            sent[grp] = _gather_send([_pack_small(grads, last=grads["loss"])], "small", None)
        else:
            parts = [grads[key].reshape(N_DEV, -1, grads[key].shape[-1]) for key in _SCATTERS[grp]]
            sent[grp] = (_scatter2_send(parts, grp, _SCATTERS[grp]) if grp in _TWO_STAGE else _scatter_send(parts, grp))
        return sent[grp][3]

    p = {n: (wts[n][0] if n in ("gmlp_w_s", "gmlp_b_s") else wts[n]) for n in _SMALL}
    loss, dx, gp = _local_step(x[0], positions.reshape(S, 1), loss_target[0], get_w, p, emit)
    grads, deltas, new_m, new_v = {}, {}, {}, {}
    small_passed = _gather_pass(sent["small"], "small", dx)
    after = small_passed[3]
    for grp in ("ffn2", "mix", "win", "small", "g1", "ud1"):
        if grp == "small":
            parts = _gather_finish(small_passed, "small", after)[0]
            outs = _sum_adamw(parts, *[_pack_small({n: d[n] for n in _SMALL}) for d in (wts, ms, vs)], "update_small")
            shapes = {n: wts[n].shape for n in _SMALL}
            for dst, packed in zip((grads, deltas, new_m, new_v), outs):
                dst.update(_unpack_small(packed, shapes))
            loss = outs[0][-_SMALL_SLOT, 0]
            after = outs[1]
            continue
        arrived = (_scatter2_finish if grp in _TWO_STAGE else _scatter_finish)(sent[grp], grp, after)
        for key, part in zip(_SCATTERS[grp], arrived):
            n, tr = name_of[key]
            outs = _sum_adamw(part, *[(d[n][0].T if tr else d[n][0]) for d in (wts, ms, vs)], "update_" + key, dep=after)
            for dst, o in zip((grads, deltas, new_m, new_v), outs):
                dst[n] = (o.T if tr else o)[None]
            after = outs[1]

    return (loss, dx[None], *[grads[n] for n in _WEIGHTS], *[deltas[n] for n in _WEIGHTS],
            *[new_m[n] for n in _WEIGHTS], *[new_v[n] for n in _WEIGHTS])
```

```python
import functools
import math

import jax
import jax.numpy as jnp
from jax import lax
from jax.experimental import pallas as pl
from jax.experimental.pallas import tpu as pltpu

F32 = jnp.float32
BF16 = jnp.bfloat16

N_DEV = 8
D = 1024
S = 2048
F = 2816
HEAD_DIM = 64
HEADS = 8
GW = HEADS * HEAD_DIM
PATTERN_DILATIONS = (1, 4, 16)
BLK = 128
QKV_W = 3 * 3 * GW
IN_W = QKV_W + 2 * D + 2 * D
DPROJ_W = 5 * 2 * D
DPROJ_Z_COL, DPROJ_G_COL = 3, 4
DPROJ_ZG_AT = DPROJ_Z_COL * 2 * D


def _dproj_tile(t):
    return jnp.where(t < QKV_W // 512, t, t + (DPROJ_ZG_AT - QKV_W) // 512)
ROPE_THETA = 10000.0
ALPHA = 2.0 ** 0.25
LN_EPS = 1e-5
ADAM_LR, ADAM_B1, ADAM_B2, ADAM_EPS, ADAM_WD, ADAM_STEP = 0.001, 0.9, 0.999, 1e-08, 0.01, 10
NEG = -1e30
MESH = pl.DeviceIdType.MESH


def _cp(sem=None, vmem_mb=None):
    kw = {}
    if sem is not None:
        kw["dimension_semantics"] = sem
    if vmem_mb is not None:
        kw["vmem_limit_bytes"] = vmem_mb << 20
    return pltpu.CompilerParams(**kw)


def _dot_nn(a, b):
    return lax.dot_general(a, b, (((1,), (0,)), ((), ())), preferred_element_type=F32)


def _dot_nt(a, b):
    return lax.dot_general(a, b, (((1,), (1,)), ((), ())), preferred_element_type=F32)


def _dot_tn(a, b):
    return lax.dot_general(a, b, (((0,), (0,)), ((), ())), preferred_element_type=F32)


def _ln_fwd(r, g, b):
    mu = jnp.mean(r, axis=-1, keepdims=True)
    xc = r - mu
    var = jnp.mean(xc * xc, axis=-1, keepdims=True)
    rstd = lax.rsqrt(var + LN_EPS)
    xhat = xc * rstd
    return xhat * g + b, xhat, rstd


def _ln_bwd(dh, xhat, rstd, g):
    dxh = dh * g
    m1 = jnp.mean(dxh, axis=-1, keepdims=True)
    m2 = jnp.mean(dxh * xhat, axis=-1, keepdims=True)
    return rstd * (dxh - m1 - xhat * m2)


def _sigmoid(x):
    return 0.5 * jnp.tanh(0.5 * x) + 0.5


def _colsum(x):
    return jnp.sum(x, axis=0, keepdims=True)


def _matmul(a, b, mode, out_dtype, tm, tn, tk, name, b_off=0, n_out=None, dep=None, a_map=None, m_out=None):
    n_dep = 0 if dep is None else 1
    a_map = a_map or (lambda t: t)
    if mode == "nn":
        m, k = a.shape[0], b.shape[0]
        n = b.shape[1]
    elif mode == "nt":
        m, k = a.shape
        n = n_out if n_out is not None else b.shape[0]
    else:
        k, m = a.shape[0], m_out or a.shape[1]
        n = b.shape[1]
    nk = k // tk
    assert m % tm == 0 and n % tn == 0 and k % tk == 0
    dot = {"nn": _dot_nn, "nt": _dot_nt, "tn": _dot_tn}[mode]

    def body(a_ref, b_ref, *rest):
        o_ref, scr = rest[n_dep], rest[n_dep + 1:]
        r = dot(a_ref[...].astype(BF16), b_ref[...].astype(BF16))
        if nk == 1:
            o_ref[...] = r.astype(out_dtype)
        else:
            acc = scr[0]
            kk = pl.program_id(2)

            @pl.when(kk == 0)
            def _():
                acc[...] = r

            @pl.when(kk > 0)
            def _():
                acc[...] += r

            @pl.when(kk == nk - 1)
            def _():
                o_ref[...] = acc[...].astype(out_dtype)

    if mode == "nn":
        a_spec = pl.BlockSpec((tm, tk), lambda i, j, kk: (i, a_map(kk)))
        b_spec = pl.BlockSpec((tk, tn), lambda i, j, kk: (kk, j))
    elif mode == "nt":
        a_spec = pl.BlockSpec((tm, tk), lambda i, j, kk: (i, kk))
        b_spec = pl.BlockSpec((tn, tk), lambda i, j, kk: (j + b_off, kk))
    else:
        a_spec = pl.BlockSpec((tk, tm), lambda i, j, kk: (kk, a_map(i)))
        b_spec = pl.BlockSpec((tk, tn), lambda i, j, kk: (kk, j))
    return pl.pallas_call(
        body,
        name=name,
        grid=(m // tm, n // tn, nk),
        in_specs=[a_spec, b_spec] + [pl.BlockSpec(memory_space=pl.ANY)] * n_dep,
        out_specs=pl.BlockSpec((tm, tn), lambda i, j, kk: (i, j)),
        out_shape=jax.ShapeDtypeStruct((m, n), out_dtype),
        scratch_shapes=[] if nk == 1 else [pltpu.VMEM((tm, tn), F32)],
        compiler_params=_cp(("parallel", "parallel", "arbitrary"), 56),
    )(a, b, *([] if dep is None else [dep]))


def _ffn_up(x, wgt, wut, name, tm=256, tn=F, dep=None):
    n_dep = 0 if dep is None else 1

    def body(x_ref, wg_ref, wu_ref, *rest):
        ga_ref, gb_ref, hm_ref = rest[n_dep:]
        xb = x_ref[...].astype(BF16)
        a = _dot_nt(xb, wg_ref[...])
        b = _dot_nt(xb, wu_ref[...])
        sig = _sigmoid(a)
        silu = a * sig
        ga_ref[...] = (b * (sig + silu * (1.0 - sig))).astype(BF16)
        gb_ref[...] = silu.astype(BF16)
        hm_ref[...] = (silu * b).astype(BF16)

    wsp = pl.BlockSpec((tn, D), lambda i, j: (j, 0))
    mid = pl.BlockSpec((tm, tn), lambda i, j: (i, j))
    return pl.pallas_call(
        body,
        name=name,
        grid=(S // tm, F // tn),
        in_specs=[pl.BlockSpec((tm, D), lambda i, j: (i, 0)), wsp, wsp] + [pl.BlockSpec(memory_space=pl.ANY)] * n_dep,
        out_specs=[mid, mid, mid],
        out_shape=[jax.ShapeDtypeStruct((S, F), BF16)] * 3,
        compiler_params=_cp(("parallel", "arbitrary"), 56),
    )(x, wgt, wut, *([] if dep is None else [dep]))


def _ffn_down(hm, wd, x, ln_g, ln_b, name, target=None, tm=512):
    head = target is not None

    def body(hm_ref, wd_ref, x_ref, g_ref, b_ref, *rest):
        t_ref = rest[0] if head else None
        o1_ref, o2_ref, xh_ref, rs_ref = rest[1 if head else 0:]
        r = ALPHA * x_ref[...] + 0.5 * _dot_nn(hm_ref[...], wd_ref[...])
        h, xhat, rstd = _ln_fwd(r, g_ref[...], b_ref[...])
        xh_ref[...] = xhat
        rs_ref[...] = rstd
        if head:
            e = h - t_ref[...]
            o1_ref[...] = e * (1.0 / D)
            part = jnp.sum(_colsum(e * e), axis=1, keepdims=True) * (0.5 / D)

            @pl.when(pl.program_id(0) == 0)
            def _():
                o2_ref[...] = jnp.zeros_like(o2_ref)

            o2_ref[...] += part
        else:
            o1_ref[...] = h
            o2_ref[...] = h.astype(BF16)

    row = pl.BlockSpec((tm, D), lambda i: (i, 0))
    vec = pl.BlockSpec((1, D), lambda i: (0, 0))
    second = (pl.BlockSpec((1, 1), lambda i: (0, 0)), jax.ShapeDtypeStruct((1, 1), F32)) if head else (
        row, jax.ShapeDtypeStruct((S, D), BF16))
    return pl.pallas_call(
        body,
        name=name,
        grid=(S // tm,),
        in_specs=[pl.BlockSpec((tm, F), lambda i: (i, 0)), pl.BlockSpec((F, D), lambda i: (0, 0)), row, vec, vec]
        + ([row] if head else []),
        out_specs=[row, second[0], row, pl.BlockSpec((tm, 1), lambda i: (i, 0))],
        out_shape=[
            jax.ShapeDtypeStruct((S, D), F32),
            second[1],
            jax.ShapeDtypeStruct((S, D), F32),
            jax.ShapeDtypeStruct((S, 1), F32),
        ],
        compiler_params=_cp(("arbitrary",), 56),
    )(hm, wd, x, ln_g, ln_b, *([target] if head else []))


def _ffn_bwd_mid(dh_a, dh_b, xhat, rstd, ln_g, a, b, wd, name, tm=512, tn=1408):
    two = dh_b is not None

    def body(*refs):
        dha_ref = refs[0]
        dhb_ref = refs[1] if two else None
        (xh_ref, rs_ref, g_ref, a_ref, b_ref, wd_ref,
         dr_ref, df_ref, da_ref, db_ref, dg_ref, dbias_ref, df_scr) = refs[2 if two else 1:]
        i = pl.program_id(0)
        j = pl.program_id(1)

        @pl.when(j == 0)
        def _():
            dh = dha_ref[...]
            if two:
                dh = ALPHA * dh + dhb_ref[...]
            xhat = xh_ref[...]
            dr = _ln_bwd(dh, xhat, rs_ref[...], g_ref[...])
            dfb = (0.5 * dr).astype(BF16)
            dr_ref[...] = dr
            df_scr[...] = dfb
            df_ref[...] = dfb
            sg = _colsum(dh * xhat)
            sb = _colsum(dh)

            @pl.when(i == 0)
            def _():
                dg_ref[...] = sg
                dbias_ref[...] = sb

            @pl.when(i > 0)
            def _():
                dg_ref[...] += sg
                dbias_ref[...] += sb

        dhm = _dot_nt(df_scr[...], wd_ref[...])
        da_ref[...] = (dhm * a_ref[...].astype(F32)).astype(BF16)
        db_ref[...] = (dhm * b_ref[...].astype(F32)).astype(BF16)

    row = pl.BlockSpec((tm, D), lambda i, j: (i, 0))
    vec = pl.BlockSpec((1, D), lambda i, j: (0, 0))
    mid = pl.BlockSpec((tm, tn), lambda i, j: (i, j))
    ins = [dh_a] + ([dh_b] if two else []) + [xhat, rstd, ln_g, a, b, wd]
    in_specs = [row] * (2 if two else 1) + [row, pl.BlockSpec((tm, 1), lambda i, j: (i, 0)), vec, mid, mid,
                                            pl.BlockSpec((tn, D), lambda i, j: (j, 0))]
    return pl.pallas_call(
        body,
        name=name,
        grid=(S // tm, F // tn),
        in_specs=in_specs,
        out_specs=[row, row, mid, mid, vec, vec],
        out_shape=[
            jax.ShapeDtypeStruct((S, D), F32),
            jax.ShapeDtypeStruct((S, D), BF16),
            jax.ShapeDtypeStruct((S, F), BF16),
            jax.ShapeDtypeStruct((S, F), BF16),
            jax.ShapeDtypeStruct((1, D), F32),
            jax.ShapeDtypeStruct((1, D), F32),
        ],
        scratch_shapes=[pltpu.VMEM((tm, D), BF16)],
        compiler_params=_cp(("arbitrary", "arbitrary"), 56),
    )(*ins)


def _ffn_bwd_dx(dr, da, db, wgt, wut, name, tm=512, tk=1408, dep=None):
    nk = F // tk
    n_dep = 0 if dep is None else 1

    def body(dr_ref, da_ref, db_ref, wg_ref, wu_ref, *rest):
        dx_ref, acc = rest[n_dep], rest[n_dep + 1]
        kk = pl.program_id(1)
        part = _dot_nn(da_ref[...], wg_ref[...]) + _dot_nn(db_ref[...], wu_ref[...])

        @pl.when(kk == 0)
        def _():
            acc[...] = ALPHA * dr_ref[...] + part

        @pl.when(kk > 0)
        def _():
            acc[...] += part

        @pl.when(kk == nk - 1)
        def _():
            dx_ref[...] = acc[...]

    row = pl.BlockSpec((tm, D), lambda i, kk: (i, 0))
    mid = pl.BlockSpec((tm, tk), lambda i, kk: (i, kk))
    wsp = pl.BlockSpec((tk, D), lambda i, kk: (kk, 0))
    return pl.pallas_call(
        body,
        name=name,
        grid=(S // tm, nk),
        in_specs=[row, mid, mid, wsp, wsp] + [pl.BlockSpec(memory_space=pl.ANY)] * n_dep,
        out_specs=row,
        out_shape=jax.ShapeDtypeStruct((S, D), F32),
        scratch_shapes=[pltpu.VMEM((tm, D), F32)],
        compiler_params=_cp(("parallel", "arbitrary"), 56),
    )(dr, da, db, wgt, wut, *([] if dep is None else [dep]))


def _rope_tables(pos_col, invf, sign, tm=512):
    def body(p_ref, f_ref, s_ref, c_out, s_out):
        ang = p_ref[...].astype(F32) * f_ref[...]
        c_out[...] = jnp.cos(ang)
        s_out[...] = jnp.sin(ang) * s_ref[...]

    vec = pl.BlockSpec((1, BLK), lambda i: (0, 0))
    out = pl.BlockSpec((tm, BLK), lambda i: (i, 0))
    return pl.pallas_call(
        body,
        name="rope_tables",
        grid=(S // tm,),
        in_specs=[pl.BlockSpec((tm, 1), lambda i: (i, 0)), vec, vec],
        out_specs=[out, out],
        out_shape=[jax.ShapeDtypeStruct((S, BLK), F32)] * 2,
        compiler_params=_cp(("parallel",)),
    )(pos_col, invf, sign)


def _lane_lo(rows=BLK):
    return (lax.broadcasted_iota(jnp.int32, (rows, GW), 1) % HEAD_DIM) < (HEAD_DIM // 2)


def _swap_halves(t, lo):
    return jnp.where(lo, pltpu.roll(t, GW - HEAD_DIM // 2, 1), pltpu.roll(t, HEAD_DIM // 2, 1))


def _rope(t, cosf, sinf, lo):
    return t * cosf + _swap_halves(t, lo) * sinf


def _unrope(g, cosf, sinf, lo):
    return g * cosf + _swap_halves(g * sinf, lo)


def _tile4(v):
    return jnp.concatenate([v, v, v, v], axis=1)


def _band_mask(n):
    qi = lax.broadcasted_iota(jnp.int32, (BLK, 2 * BLK), 0)
    kj = lax.broadcasted_iota(jnp.int32, (BLK, 2 * BLK), 1)
    dist = qi + BLK - kj
    return (dist >= 0) & (dist <= BLK) & ((kj >= BLK) | (n >= 1))


ROWS = 256
LANES = 128


def _to_view(scr, y, dst_ref, d, dtype, col0=0):
    if d == 1:
        dst_ref[:, col0:col0 + GW] = y.astype(dtype)
        return
    for cb in range(GW // LANES):
        scr[cb][...] = y[:, cb * LANES:(cb + 1) * LANES]
    for r in range(d):
        for cb in range(GW // LANES):
            at = col0 + r * GW + cb * LANES
            dst_ref[:, at:at + LANES] = scr[cb][pl.ds(r, ROWS // d, stride=d), :].astype(dtype)


def _from_view(scr, src_ref, d):
    if d == 1:
        return src_ref[...].astype(F32)
    for r in range(d):
        for cb in range(GW // LANES):
            at = r * GW + cb * LANES
            scr[cb][pl.ds(r, ROWS // d, stride=d), :] = src_ref[:, at:at + LANES].astype(F32)
    return jnp.concatenate([scr[cb][...] for cb in range(GW // LANES)], axis=1)


def _view_spec(d):
    return pl.BlockSpec((ROWS // d, d * GW), lambda i: (i, 0))


_VIEW_SCRATCH = [pltpu.VMEM((ROWS, LANES), F32)] * (GW // LANES)


def _qkv_prep(qkv, cos_t, sin_t):
    def body(x_ref, c_ref, s_ref, *rest):
        outs, scr = rest[:9], rest[9:]
        lo = _lane_lo(ROWS)
        cf, sf = _tile4(c_ref[...]), _tile4(s_ref[...])
        for gi, d in enumerate(PATTERN_DILATIONS):
            q, k, v = (x_ref[:, (3 * part + gi) * GW:(3 * part + gi + 1) * GW].astype(F32) for part in range(3))
            _to_view(scr, _rope(q, cf, sf, lo) * (HEAD_DIM ** -0.5), outs[3 * gi], d, BF16)
            _to_view(scr, _rope(k, cf, sf, lo), outs[3 * gi + 1], d, BF16)
            _to_view(scr, v, outs[3 * gi + 2], d, BF16)

    tab = pl.BlockSpec((ROWS, BLK), lambda i: (i, 0))
    outs = pl.pallas_call(
        body,
        name="qkv_prep",
        grid=(S // ROWS,),
        in_specs=[pl.BlockSpec((ROWS, QKV_W), lambda i: (i, 0)), tab, tab],
        out_specs=[_view_spec(d) for d in PATTERN_DILATIONS for _ in range(3)],
        out_shape=[jax.ShapeDtypeStruct((S // d, d * GW), BF16) for d in PATTERN_DILATIONS for _ in range(3)],
        scratch_shapes=_VIEW_SCRATCH,
        compiler_params=_cp(("parallel",), 48),
    )(qkv, cos_t, sin_t)
    return [tuple(outs[3 * gi:3 * gi + 3]) for gi in range(3)]


N_ATTN_STEPS = S // BLK


def _class_and_block(gi, t):
    per_class = N_ATTN_STEPS // PATTERN_DILATIONS[gi]
    return t >> (per_class.bit_length() - 1), t & (per_class - 1)


def _attn_specs(gi):
    def cur(t):
        r, n = _class_and_block(gi, t)
        return n, r

    def prev(t):
        r, n = _class_and_block(gi, t)
        return jnp.maximum(n - 1, 0), r

    def whole(t):
        return 0, _class_and_block(gi, t)[0]

    sub = S // PATTERN_DILATIONS[gi]
    return (pl.BlockSpec((BLK, GW), cur), pl.BlockSpec((BLK, GW), prev), pl.BlockSpec((sub, GW), whole))


def _left_lanes():
    return lax.broadcasted_iota(jnp.int32, (BLK, LANES), 1) < HEAD_DIM


def _stack_heads(t2, left):
    zero = jnp.zeros_like(t2)
    return jnp.concatenate([jnp.where(left, t2, zero), jnp.where(left, zero, t2)], axis=0)


def _attn_fwd_block(n, q_ref, kc_ref, kp_ref, vc_ref, vp_ref, o_ref, l_ref):
    mask = _band_mask(n)
    left = _left_lanes()
    zero = jnp.zeros((BLK, LANES), BF16)
    for pr in range(GW // LANES):
        cs = slice(pr * LANES, (pr + 1) * LANES)
        q2 = q_ref[:, cs]
        k2 = jnp.concatenate([kp_ref[:, cs], kc_ref[:, cs]], axis=0)
        v2 = jnp.concatenate([vp_ref[:, cs], vc_ref[:, cs]], axis=0)
        o_h, lse_h = [], []
        for side in (left, ~left):
            s = jnp.where(mask, _dot_nt(jnp.where(side, q2, zero), k2), NEG)
            m = jnp.max(s, axis=1, keepdims=True)
            p = jnp.exp(s - m)
            l = jnp.sum(p, axis=1, keepdims=True)
            o_h.append(_dot_nn((p / l).astype(BF16), v2))
            lse_h.append(m + jnp.log(l))
        o_ref[:, cs] = jnp.where(left, o_h[0], o_h[1])
        l_ref[:, cs] = jnp.where(left, lse_h[0], lse_h[1])


def _attn_fwd(preps):
    def body(*refs):
        t = pl.program_id(0)
        for gi in range(3):
            _attn_fwd_block(_class_and_block(gi, t)[1], *refs[5 * gi:5 * gi + 5], *refs[15 + 2 * gi:17 + 2 * gi])

    in_specs, out_specs, out_shape, args = [], [], [], []
    for gi, d in enumerate(PATTERN_DILATIONS):
        cur, prev, _ = _attn_specs(gi)
        qr, kr, vv = preps[gi]
        in_specs += [cur, cur, prev, cur, prev]
        args += [qr, kr, kr, vv, vv]
        out_specs += [cur, cur]
        out_shape += [jax.ShapeDtypeStruct((S // d, d * GW), F32)] * 2
    outs = pl.pallas_call(
        body,
        name="attn_fwd",
        grid=(N_ATTN_STEPS,),
        in_specs=in_specs,
        out_specs=out_specs,
        out_shape=out_shape,
        compiler_params=_cp(("arbitrary",), 48),
    )(*args)
    return [outs[0], outs[2], outs[4]], [outs[1], outs[3], outs[5]]


def _attn_bwd_block(n, q_ref, kc_ref, kp_ref, vc_ref, vp_ref, do_ref, l_ref, c_ref, dq_ref, dk_ref, dv_ref):
    mask = _band_mask(n)
    left = _left_lanes()
    left2 = jnp.concatenate([left, left], axis=0)
    here = pl.ds(pl.multiple_of(n * BLK, BLK), BLK)
    before = pl.ds(pl.multiple_of(jnp.maximum(n - 1, 0) * BLK, BLK), BLK)
    for pr in range(GW // LANES):
        cs = slice(pr * LANES, (pr + 1) * LANES)
        q2 = q_ref[:, cs]
        do2 = do_ref[:, cs]
        k_bd = _stack_heads(jnp.concatenate([kp_ref[:, cs], kc_ref[:, cs]], axis=0), left2)
        v_bd = _stack_heads(jnp.concatenate([vp_ref[:, cs], vc_ref[:, cs]], axis=0), left2)
        s2 = _dot_nt(q2, k_bd)
        dp2 = _dot_nt(do2, v_bd)
        ps, dss = [], []
        for h in range(2):
            at = pr * LANES + h * HEAD_DIM
            half = slice(h * 2 * BLK, (h + 1) * 2 * BLK)
            p = jnp.exp(jnp.where(mask, s2[:, half], NEG) - l_ref[:, at:at + 1])
            ps.append(p.astype(BF16))
            dss.append((p * (dp2[:, half] - c_ref[:, at:at + 1])).astype(BF16))
        ds_both = jnp.concatenate(dss, axis=1)
        dq_ref[:, cs] = _dot_nn(ds_both, k_bd) * (HEAD_DIM ** -0.5)
        dk_bd = _dot_tn(ds_both, q2)
        dv_bd = _dot_tn(jnp.concatenate(ps, axis=1), do2)
        dk2 = jnp.where(left2, dk_bd[:2 * BLK], dk_bd[2 * BLK:])
        dv2 = jnp.where(left2, dv_bd[:2 * BLK], dv_bd[2 * BLK:])
        dk_ref[here, cs] = dk2[BLK:]
        dv_ref[here, cs] = dv2[BLK:]
        dk_ref[before, cs] += dk2[:BLK]
        dv_ref[before, cs] += dv2[:BLK]


def _attn_bwd(preps, dos, lses, cterms):
    def body(*refs):
        t = pl.program_id(0)
        for gi in range(3):
            _attn_bwd_block(_class_and_block(gi, t)[1], *refs[8 * gi:8 * gi + 8], *refs[24 + 3 * gi:27 + 3 * gi])

    in_specs, out_specs, out_shape, args = [], [], [], []
    for gi, d in enumerate(PATTERN_DILATIONS):
        cur, prev, whole = _attn_specs(gi)
        qr, kr, vv = preps[gi]
        in_specs += [cur, cur, prev, cur, prev, cur, cur, cur]
        args += [qr, kr, kr, vv, vv, dos[gi], lses[gi], cterms[gi]]
        out_specs += [cur, whole, whole]
        out_shape += [jax.ShapeDtypeStruct((S // d, d * GW), F32)] * 3
    outs = pl.pallas_call(
        body,
        name="attn_bwd",
        grid=(N_ATTN_STEPS,),
        in_specs=in_specs,
        out_specs=out_specs,
        out_shape=out_shape,
        compiler_params=_cp(("arbitrary",), 56),
    )(*args)
    return [tuple(outs[3 * gi:3 * gi + 3]) for gi in range(3)]


def _qkv_unprep(grads, cos_t, sin_t, dproj):
    def body(*refs):
        views, (c_ref, s_ref, _, out_ref), scr = refs[:9], refs[9:13], refs[13:]
        lo = _lane_lo(ROWS)
        cf, sf = _tile4(c_ref[...]), _tile4(s_ref[...])
        for gi, d in enumerate(PATTERN_DILATIONS):
            dq_ref, dk_ref, dv_ref = views[3 * gi:3 * gi + 3]
            qa, ka, va = gi * GW, (3 + gi) * GW, (6 + gi) * GW
            out_ref[:, qa:qa + GW] = _unrope(_from_view(scr, dq_ref, d), cf, sf, lo).astype(BF16)
            out_ref[:, ka:ka + GW] = _unrope(_from_view(scr, dk_ref, d), cf, sf, lo).astype(BF16)
            out_ref[:, va:va + GW] = _from_view(scr, dv_ref, d).astype(BF16)

    tab = pl.BlockSpec((ROWS, BLK), lambda i: (i, 0))
    return pl.pallas_call(
        body,
        name="qkv_unprep",
        grid=(S // ROWS,),
        in_specs=[_view_spec(d) for d in PATTERN_DILATIONS for _ in range(3)] + [tab, tab, pl.BlockSpec(memory_space=pl.ANY)],
        out_specs=pl.BlockSpec((ROWS, QKV_W), lambda i: (i, 0)),
        out_shape=jax.ShapeDtypeStruct(dproj.shape, dproj.dtype),
        input_output_aliases={11: 0},
        scratch_shapes=_VIEW_SCRATCH,
        compiler_params=_cp(("parallel",), 48),
    )(*[g for grp in grads for g in grp], cos_t, sin_t, dproj)


def _group_weights(l0, l1, l2):
    mx = jnp.maximum(jnp.maximum(l0, l1), l2)
    e0, e1, e2 = jnp.exp(l0 - mx), jnp.exp(l1 - mx), jnp.exp(l2 - mx)
    inv = 1.0 / (e0 + e1 + e2)
    return e0 * inv, e1 * inv, e2 * inv


def _combine_fwd(os_, lses):
    def body(o0, o1, o2, l0, l1, l2, y_ref, *scr):
        ov = [_from_view(scr, o, d) for o, d in zip((o0, o1, o2), PATTERN_DILATIONS)]
        lv = [_from_view(scr, l, d) for l, d in zip((l0, l1, l2), PATTERN_DILATIONS)]
        w0, w1, w2 = _group_weights(*lv)
        y_ref[...] = (w0 * ov[0] + w1 * ov[1] + w2 * ov[2]).astype(BF16)

    views = [_view_spec(d) for d in PATTERN_DILATIONS]
    return pl.pallas_call(
        body,
        name="attn_combine_fwd",
        grid=(S // ROWS,),
        in_specs=views + views,
        out_specs=pl.BlockSpec((ROWS, GW), lambda i: (i, 0)),
        out_shape=jax.ShapeDtypeStruct((S, GW), BF16),
        scratch_shapes=_VIEW_SCRATCH,
        compiler_params=_cp(("parallel",)),
    )(*os_, *lses)


def _combine_bwd(dy, os_, lses, seg):
    def body(dy_ref, o0, o1, o2, l0, l1, l2, seg_ref, d0, d1, d2, c0, c1, c2, *scr):
        ov = [_from_view(scr, o, d) for o, d in zip((o0, o1, o2), PATTERN_DILATIONS)]
        lv = [_from_view(scr, l, d) for l, d in zip((l0, l1, l2), PATTERN_DILATIONS)]
        ws = _group_weights(*lv)
        dyv = dy_ref[...]
        t = dyv * (ws[0] * ov[0] + ws[1] * ov[1] + ws[2] * ov[2])
        t_hi = t.astype(BF16)
        r1 = t - t_hi.astype(F32)
        t_mid = r1.astype(BF16)
        t_lo = (r1 - t_mid.astype(F32)).astype(BF16)
        sg = seg_ref[...]
        e = _dot_nn(t_hi, sg) + _dot_nn(t_mid, sg) + _dot_nn(t_lo, sg)
        for w, do_ref, c_ref, d in zip(ws, (d0, d1, d2), (c0, c1, c2), PATTERN_DILATIONS):
            _to_view(scr, w * dyv, do_ref, d, BF16)
            _to_view(scr, w * e, c_ref, d, F32)

    views = [_view_spec(d) for d in PATTERN_DILATIONS]
    return pl.pallas_call(
        body,
        name="attn_combine_bwd",
        grid=(S // ROWS,),
        in_specs=[pl.BlockSpec((ROWS, GW), lambda i: (i, 0))] + views + views + [pl.BlockSpec((GW, GW), lambda i: (0, 0))],
        out_specs=views + views,
        out_shape=[jax.ShapeDtypeStruct((S // d, d * GW), BF16) for d in PATTERN_DILATIONS]
        + [jax.ShapeDtypeStruct((S // d, d * GW), F32) for d in PATTERN_DILATIONS],
        scratch_shapes=_VIEW_SCRATCH,
        compiler_params=_cp(("parallel",)),
    )(dy, *os_, *lses, seg)


_SQRT_HALF = 0.7071067811865476
_INV_SQRT_2PI = 0.3989422804014327


def _gelu(z):
    return 0.5 * z * (1.0 + lax.erf(z * _SQRT_HALF))


def _gelu_grad(z):
    return 0.5 * (1.0 + lax.erf(z * _SQRT_HALF)) + z * (_INV_SQRT_2PI * jnp.exp(-0.5 * z * z))


def _tril_ws(ws_ref, g):
    t = lax.broadcasted_iota(jnp.int32, (BLK, BLK), 0)
    s = lax.broadcasted_iota(jnp.int32, (BLK, BLK), 1)
    return jnp.where(t >= s, ws_ref[g], 0.0)


def _chunks_side_by_side(x, cols, nch):
    return jnp.concatenate([x[c * BLK:(c + 1) * BLK, cols] for c in range(nch)], axis=1)


def _gmlp_fwd(z, ws, bst, ln_g, ln_b, tm=512):
    nch = tm // BLK

    def body(z_ref, ws_ref, b_ref, g_ref, be_ref, y_ref):
        zg = _gelu(z_ref[...].astype(F32))
        u = zg[:, :D]
        vn, _, _ = _ln_fwd(zg[:, D:], g_ref[...], be_ref[...])
        vnb = vn.astype(BF16)
        bt = b_ref[...]
        for g in range(8):
            w = _tril_ws(ws_ref, g).astype(BF16)
            cols = slice(g * BLK, (g + 1) * BLK)
            mixed = _dot_nn(w, _chunks_side_by_side(vnb, cols, nch)) + bt[:, g:g + 1]
            for c in range(nch):
                rows = slice(c * BLK, (c + 1) * BLK)
                y_ref[rows, cols] = (u[rows, cols] * mixed[:, c * BLK:(c + 1) * BLK]).astype(BF16)

    return pl.pallas_call(
        body,
        name="gmlp_fwd",
        grid=(S // tm,),
        in_specs=[
            pl.BlockSpec((tm, 2 * D), lambda i: (i, 0)),
            pl.BlockSpec((8, BLK, BLK), lambda i: (0, 0, 0)),
            pl.BlockSpec((BLK, 8), lambda i: (0, 0)),
            pl.BlockSpec((1, D), lambda i: (0, 0)),
            pl.BlockSpec((1, D), lambda i: (0, 0)),
        ],
        out_specs=pl.BlockSpec((tm, D), lambda i: (i, 0)),
        out_shape=jax.ShapeDtypeStruct((S, D), BF16),
        compiler_params=_cp(("parallel",), 48),
    )(z, ws, bst, ln_g, ln_b)


def _gmlp_bwd(z, dy, ws, bst, ln_g, ln_b, dproj, tm=512):
    nch = tm // BLK

    def body(z_ref, dy_ref, ws_ref, b_ref, g_ref, be_ref, _, dz_ref, dws_ref, dbs_ref, dg_ref, dbe_ref, dvn_scr, dm_acc):
        i = pl.program_id(0)
        zv = z_ref[...].astype(F32)
        zg = _gelu(zv)
        u = zg[:, :D]
        gam = g_ref[...]
        vn, xhat, rstd = _ln_fwd(zg[:, D:], gam, be_ref[...])
        vnb = vn.astype(BF16)
        dyv = dy_ref[...]
        dmix = dyv * u
        dmb = dmix.astype(BF16)
        bt = b_ref[...]
        tmask = lax.broadcasted_iota(jnp.int32, (BLK, BLK), 0) >= lax.broadcasted_iota(jnp.int32, (BLK, BLK), 1)
        dm_sum = dmix[0:BLK]
        for c in range(1, nch):
            dm_sum = dm_sum + dmix[c * BLK:(c + 1) * BLK]

        @pl.when(i == 0)
        def _():
            dm_acc[...] = jnp.zeros_like(dm_acc)
            dws_ref[...] = jnp.zeros_like(dws_ref)
            dg_ref[...] = jnp.zeros_like(dg_ref)
            dbe_ref[...] = jnp.zeros_like(dbe_ref)

        dm_acc[...] += dm_sum
        dus = []
        for g in range(8):
            w = _tril_ws(ws_ref, g).astype(BF16)
            cols = slice(g * BLK, (g + 1) * BLK)
            v_cat = _chunks_side_by_side(vnb, cols, nch)
            dm_cat = _chunks_side_by_side(dmb, cols, nch)
            mixed = _dot_nn(w, v_cat) + bt[:, g:g + 1]
            dus.append(jnp.concatenate(
                [dyv[c * BLK:(c + 1) * BLK, cols] * mixed[:, c * BLK:(c + 1) * BLK] for c in range(nch)], axis=0))
            dws_ref[g] += jnp.where(tmask, _dot_nt(dm_cat, v_cat), 0.0)
            dvn_cat = _dot_tn(w, dm_cat)
            for c in range(nch):
                dvn_scr[c * BLK:(c + 1) * BLK, cols] = dvn_cat[:, c * BLK:(c + 1) * BLK]

        dvn = dvn_scr[...]
        dg_ref[...] += _colsum(dvn * xhat)
        dbe_ref[...] += _colsum(dvn)
        dvg = _ln_bwd(dvn, xhat, rstd, gam)
        gp = _gelu_grad(zv)
        dz_ref[:, :D] = (jnp.concatenate(dus, axis=1) * gp[:, :D]).astype(BF16)
        dz_ref[:, D:] = (dvg * gp[:, D:]).astype(BF16)

        @pl.when(i == S // tm - 1)
        def _():
            acc = dm_acc[...]
            for g in range(8):
                dbs_ref[:, g:g + 1] = jnp.sum(acc[:, g * BLK:(g + 1) * BLK], axis=1, keepdims=True)

    vec = pl.BlockSpec((1, D), lambda i: (0, 0))
    return pl.pallas_call(
        body,
        name="gmlp_bwd",
        grid=(S // tm,),
        in_specs=[
            pl.BlockSpec((tm, 2 * D), lambda i: (i, 0)),
            pl.BlockSpec((tm, D), lambda i: (i, 0)),
            pl.BlockSpec((8, BLK, BLK), lambda i: (0, 0, 0)),
            pl.BlockSpec((BLK, 8), lambda i: (0, 0)),
            vec,
            vec,
            pl.BlockSpec(memory_space=pl.ANY),
        ],
        out_specs=[
            pl.BlockSpec((tm, 2 * D), lambda i: (i, DPROJ_Z_COL)),
            pl.BlockSpec((8, BLK, BLK), lambda i: (0, 0, 0)),
            pl.BlockSpec((BLK, 8), lambda i: (0, 0)),
            vec,
            vec,
        ],
        out_shape=[
            jax.ShapeDtypeStruct(dproj.shape, dproj.dtype),
            jax.ShapeDtypeStruct((8, BLK, BLK), F32),
            jax.ShapeDtypeStruct((BLK, 8), F32),
            jax.ShapeDtypeStruct((1, D), F32),
            jax.ShapeDtypeStruct((1, D), F32),
        ],
        input_output_aliases={6: 0},
        scratch_shapes=[pltpu.VMEM((tm, D), F32), pltpu.VMEM((BLK, D), F32)],
        compiler_params=_cp(("arbitrary",), 48),
    )(z, dy, ws, bst, ln_g, ln_b, dproj)


def _merge_fwd(ya, yg, glog, bgate, h1, wabt, wgb, wo, ln_g, ln_b, tm=256):
    def body(ya_ref, yg_ref, gl_ref, bg_ref, h1_ref, wab_ref, wgb_ref, wo_ref, g_ref, b_ref,
             h_ref, hb_ref, xh_ref, rs_ref, mg_ref, bra_ref, brg_ref):
        bra = _dot_nt(ya_ref[...], wab_ref[...])
        brg = _dot_nn(yg_ref[...], wgb_ref[...])
        gates = _sigmoid(gl_ref[...].astype(F32) + bg_ref[...])
        merged = (gates[:, :D] * bra + gates[:, D:] * brg).astype(BF16)
        mix = _dot_nn(merged, wo_ref[...])
        h, xhat, rstd = _ln_fwd(ALPHA * h1_ref[...] + mix, g_ref[...], b_ref[...])
        h_ref[...] = h
        hb_ref[...] = h.astype(BF16)
        xh_ref[...] = xhat
        rs_ref[...] = rstd
        mg_ref[...] = merged
        bra_ref[...] = bra
        brg_ref[...] = brg

    row = pl.BlockSpec((tm, D), lambda i: (i, 0))
    vec = pl.BlockSpec((1, D), lambda i: (0, 0))
    full = lambda shape: pl.BlockSpec(shape, lambda i: (0, 0))
    return pl.pallas_call(
        body,
        name="merge_fwd",
        grid=(S // tm,),
        in_specs=[
            pl.BlockSpec((tm, GW), lambda i: (i, 0)), row,
            pl.BlockSpec((tm, 2 * D), lambda i: (i, glog.shape[1] // (2 * D) - 1)),
            full((1, 2 * D)), row,
            full((D, GW)), full((D, D)), full((D, D)), vec, vec,
        ],
        out_specs=[row, row, row, pl.BlockSpec((tm, 1), lambda i: (i, 0)), row, row, row],
        out_shape=[
            jax.ShapeDtypeStruct((S, D), F32),
            jax.ShapeDtypeStruct((S, D), BF16),
            jax.ShapeDtypeStruct((S, D), F32),
            jax.ShapeDtypeStruct((S, 1), F32),
            jax.ShapeDtypeStruct((S, D), BF16),
            jax.ShapeDtypeStruct((S, D), F32),
            jax.ShapeDtypeStruct((S, D), F32),
        ],
        compiler_params=_cp(("parallel",), 48),
    )(ya, yg, glog, bgate, h1, wabt, wgb, wo, ln_g, ln_b)


def _merge_bwd(dh2, xhat, rstd, ln_g, bra, brg, glog, bgate, wabt, wgb, wo, tm=256):
    def body(dh_ref, xh_ref, rs_ref, g_ref, bra_ref, brg_ref, gl_ref, bg_ref, wab_ref, wgb_ref, wo_ref,
             dr_ref, drb_ref, dlog_ref, dba_ref, dbg_ref, dya_ref, dyg_ref, dbgate_ref, dg_ref, dbias_ref):
        i = pl.program_id(0)
        dh = dh_ref[...]
        xh = xh_ref[...]
        dr = _ln_bwd(dh, xh, rs_ref[...], g_ref[...])
        drb = dr.astype(BF16)
        dr_ref[...] = dr
        drb_ref[...] = drb
        dmerged = _dot_nt(drb, wo_ref[...])
        gates = _sigmoid(gl_ref[...].astype(F32) + bg_ref[...])
        g0, g1 = gates[:, :D], gates[:, D:]
        dl0 = dmerged * bra_ref[...] * g0 * (1.0 - g0)
        dl1 = dmerged * brg_ref[...] * g1 * (1.0 - g1)
        dlog_ref[:, :D] = dl0.astype(BF16)
        dlog_ref[:, D:] = dl1.astype(BF16)
        dba = (dmerged * g0).astype(BF16)
        dbg = (dmerged * g1).astype(BF16)
        dba_ref[...] = dba
        dbg_ref[...] = dbg
        dya_ref[...] = _dot_nn(dba, wab_ref[...])
        dyg_ref[...] = _dot_nt(dbg, wgb_ref[...])
        s0, s1 = _colsum(dl0), _colsum(dl1)
        sg, sb = _colsum(dh * xh), _colsum(dh)

        @pl.when(i == 0)
        def _():
            dbgate_ref[:, :D] = s0
            dbgate_ref[:, D:] = s1
            dg_ref[...] = sg
            dbias_ref[...] = sb

        @pl.when(i > 0)
        def _():
            dbgate_ref[:, :D] += s0
            dbgate_ref[:, D:] += s1
            dg_ref[...] += sg
            dbias_ref[...] += sb

    row = pl.BlockSpec((tm, D), lambda i: (i, 0))
    vec = pl.BlockSpec((1, D), lambda i: (0, 0))
    wide = pl.BlockSpec((tm, 2 * D), lambda i: (i, 0))
    full = lambda shape: pl.BlockSpec(shape, lambda i: (0, 0))
    return pl.pallas_call(
        body,
        name="merge_bwd",
        grid=(S // tm,),
        in_specs=[row, row, pl.BlockSpec((tm, 1), lambda i: (i, 0)), vec, row, row,
                  pl.BlockSpec((tm, 2 * D), lambda i: (i, glog.shape[1] // (2 * D) - 1)),
                  full((1, 2 * D)), full((D, GW)), full((D, D)), full((D, D))],
        out_specs=[row, row, pl.BlockSpec((tm, 2 * D), lambda i: (i, DPROJ_G_COL)), row, row,
                   pl.BlockSpec((tm, GW), lambda i: (i, 0)), row, full((1, 2 * D)), vec, vec],
        out_shape=[
            jax.ShapeDtypeStruct((S, D), F32),
            jax.ShapeDtypeStruct((S, D), BF16),
            jax.ShapeDtypeStruct((S, DPROJ_W), BF16),
            jax.ShapeDtypeStruct((S, D), BF16),
            jax.ShapeDtypeStruct((S, D), BF16),
            jax.ShapeDtypeStruct((S, GW), F32),
            jax.ShapeDtypeStruct((S, D), F32),
            jax.ShapeDtypeStruct((1, 2 * D), F32),
            jax.ShapeDtypeStruct((1, D), F32),
            jax.ShapeDtypeStruct((1, D), F32),
        ],
        compiler_params=_cp(("arbitrary",), 48),
    )(dh2, xhat, rstd, ln_g, bra, brg, glog, bgate, wabt, wgb, wo)


def _dproj_to_dh1(dproj, win, dep, tm=512):
    n_dep = 0 if dep is None else 1

    def body(a_ref, b_ref, *rest):
        rest[n_dep][...] = (_dot_nn(a_ref[:, :QKV_W], b_ref[:QKV_W, :])
                            + _dot_nn(a_ref[:, DPROJ_ZG_AT:], b_ref[QKV_W:, :]))

    return pl.pallas_call(
        body,
        name="dproj_to_dh1",
        grid=(S // tm,),
        in_specs=[pl.BlockSpec((tm, DPROJ_W), lambda i: (i, 0)),
                  pl.BlockSpec((IN_W, D), lambda i: (0, 0), pipeline_mode=pl.Buffered(1))]
        + [pl.BlockSpec(memory_space=pl.ANY)] * n_dep,
        out_specs=pl.BlockSpec((tm, D), lambda i: (i, 0)),
        out_shape=jax.ShapeDtypeStruct((S, D), F32),
        compiler_params=_cp(("parallel",), 56),
    )(dproj, win, *([] if dep is None else [dep]))


def _tie(x, dep):
    if dep is None:
        return x
    return x + dep[0, 0].astype(x.dtype)


def _local_step(x, pos_col, target, get_w, p, emit):
    w = get_w("ffn1", None)
    a1, b1, hm1 = _ffn_up(x, w["g1"], w["u1"], "ffn1_up")
    w.update(get_w("ffn1d", hm1))
    tok = get_w("win", hm1, early=True)
    h1, h1b, xh1, rs1 = _ffn_down(hm1, w["d1"], x, _tie(p["ln1_g"], tok), p["ln1_b"], "ffn1_down")

    w.update(get_w("win", h1b))
    qkv = _matmul(h1b, w["win"], "nt", BF16, S, 1536, D, "proj_qkv", b_off=0, n_out=QKV_W)
    z = glog = _matmul(h1b, w["win"], "nt", BF16, S, 512, D, "proj_zg", b_off=QKV_W // 512, n_out=4 * D)

    half = jnp.arange(0, HEAD_DIM, 2, dtype=F32) / HEAD_DIM
    inv_freq = ROPE_THETA ** (-half)
    invf = jnp.tile(inv_freq, 4).reshape(1, BLK)
    sign = jnp.tile(jnp.concatenate([-jnp.ones((32,), F32), jnp.ones((32,), F32)]), 2).reshape(1, BLK)
    cos_t, sin_t = _rope_tables(pos_col, invf, sign)

    preps = _qkv_prep(qkv, cos_t, sin_t)
    os_, lses = _attn_fwd(preps)
    ya = _combine_fwd(os_, lses)
    get_w("late", ya, early=True)
    bst = p["gmlp_b_s"].T
    yg = _gmlp_fwd(z, p["gmlp_w_s"], bst, p["gmlp_ln_g"], p["gmlp_ln_b"])
    w.update(get_w("late", yg))
    h2, h2b, xh2, rs2, merged, bra, brg = _merge_fwd(ya, yg, glog, p["b_gates"], h1, w["ab"], w["gb"], w["o"],
                                                      p["ln2_g"], p["ln2_b"])
    a2, b2, hm2 = _ffn_up(h2b, w["g2"], w["u2"], "ffn2_up")
    dh3, loss, xh3, rs3 = _ffn_down(hm2, w["d2"], h2, p["ln3_g"], p["ln3_b"], "ffn2_down", target=target)

    gp = {}
    dr3, df2, da2, db2, gp["ln3_g"], gp["ln3_b"] = _ffn_bwd_mid(dh3, None, xh3, rs3, p["ln3_g"], a2, b2, w["d2"],
                                                                "ffn2_bwd_mid")
    tok = emit("ffn2", {
        "g2": _matmul(da2, h2b, "tn", BF16, 1408, D, S, "wgrad_g2"),
        "u2": _matmul(db2, h2b, "tn", BF16, 1408, D, S, "wgrad_u2"),
        "d2": _matmul(hm2, df2, "tn", BF16, 1408, D, S, "wgrad_d2")})
    dh2 = _ffn_bwd_dx(dr3, da2, db2, w["g2"], w["u2"], "ffn2_bwd_dx")

    (dr2, dr2b, dproj, dba, dbg, dya, dyg, gp["b_gates"], gp["ln2_g"], gp["ln2_b"]) = _merge_bwd(
        dh2, xh2, rs2, _tie(p["ln2_g"], tok), bra, brg, glog, p["b_gates"], w["ab"], w["gb"], w["o"])
    tok = emit("mix", {
        "o": _matmul(merged, dr2b, "tn", BF16, 512, D, S, "wgrad_o"),
        "ab": _matmul(dba, ya, "tn", BF16, 512, GW, S, "wgrad_ab"),
        "gb": _matmul(yg, dbg, "tn", BF16, 512, D, S, "wgrad_gb")})

    seg = (jnp.arange(GW)[:, None] // HEAD_DIM == jnp.arange(GW)[None, :] // HEAD_DIM).astype(BF16)
    do0, do1, do2, c0, c1, c2 = _combine_bwd(dya, os_, lses, _tie(seg, tok))
    dproj = _qkv_unprep(_attn_bwd(preps, (do0, do1, do2), lses, (c0, c1, c2)), cos_t, sin_t, dproj)
    dproj, gp["gmlp_w_s"], dbst, gp["gmlp_ln_g"], gp["gmlp_ln_b"] = _gmlp_bwd(
        z, dyg, p["gmlp_w_s"], bst, p["gmlp_ln_g"], p["gmlp_ln_b"], dproj)
    gp["gmlp_b_s"] = dbst.T
    tok = emit("win", {"win": _matmul(dproj, h1b, "tn", BF16, 512, D, S, "wgrad_win", a_map=_dproj_tile, m_out=IN_W)})
    dh1m = _dproj_to_dh1(dproj, w["win"], tok)

    dr1, df1, da1, db1, gp["ln1_g"], gp["ln1_b"] = _ffn_bwd_mid(dr2, dh1m, xh1, rs1, p["ln1_g"], a1, b1, w["d1"],
                                                                "ffn1_bwd_mid")
    tok = emit("small", {**gp, "loss": loss})
    tok = emit("g1", {"g1": _matmul(da1, x, "tn", BF16, 1408, D, S, "wgrad_g1", dep=tok)})
    tok = emit("ud1", {"u1": _matmul(db1, x, "tn", BF16, 1408, D, S, "wgrad_u1", dep=tok),
                       "d1": _matmul(hm1, df1, "tn", BF16, 1408, D, S, "wgrad_d1", dep=tok)})
    dx = _ffn_bwd_dx(dr1, da1, db1, w["g1"], w["u1"], "ffn1_bwd_dx", dep=tok)
    return loss, dx, gp


_FLIPS = [(mx, my, mc) for mx in (0, 1) for my in (0, 1) for mc in (0, 1)][1:]
_GROUPS = {"ffn1": ("g1", "u1"), "ffn1d": ("d1",), "win": ("win",), "late": ("ab", "gb", "o", "g2", "u2", "d2")}
_SCATTERS = {"ffn2": ("g2", "u2", "d2"), "mix": ("ab", "gb", "o"), "win": ("win",), "g1": ("g1",), "ud1": ("u1", "d1")}
_TWO_STAGE = ("g1", "ud1")
_HBM = pl.BlockSpec(memory_space=pltpu.HBM)
_SEM = pl.BlockSpec(memory_space=pltpu.SEMAPHORE)
_EFFECT = pltpu.SideEffectType.DATAFLOW_SIDE_EFFECTING


def _me():
    return 4 * lax.axis_index("x") + 2 * lax.axis_index("y") + lax.axis_index("c")


def _copies_start(bufs, copies, n_sem, name, after=None):
    nb = len(bufs)
    n_after = 0 if after is None else 1

    def body(*refs):
        b = refs[:nb]
        send_sems, recv_sems = refs[nb + n_after], refs[nb + n_after + 1]
        token = refs[-1]
        x, y, c = lax.axis_index("x"), lax.axis_index("y"), lax.axis_index("c")
        me = 4 * x + 2 * y + c
        for si, s_slot, di, d_slot, flip, sem in copies:
            s_idx, d_idx = s_slot(me), d_slot(me)
            mx, my, mc = flip(c) if callable(flip) else flip
            pltpu.make_async_remote_copy(
                src_ref=b[si] if s_idx is None else b[si].at[s_idx],
                dst_ref=b[di] if d_idx is None else b[di].at[d_idx],
                send_sem=send_sems.at[sem], recv_sem=recv_sems.at[sem],
                device_id=(x ^ mx, y ^ my, c ^ mc), device_id_type=MESH).start()
        token[...] = jnp.zeros_like(token)

    ins = [pltpu.with_memory_space_constraint(a, pltpu.HBM) for a in bufs]
    outs = pl.pallas_call(
        body,
        name=name,
        in_specs=[_HBM] * nb + [pl.BlockSpec(memory_space=pl.ANY)] * n_after,
        out_specs=[_SEM, _SEM] + [_HBM] * nb + [pl.BlockSpec(memory_space=pltpu.VMEM)],
        out_shape=[pltpu.SemaphoreType.DMA((n_sem,)), pltpu.SemaphoreType.DMA((n_sem,))]
        + [pltpu.HBM(a.shape, a.dtype) for a in ins] + [jax.ShapeDtypeStruct((8, 128), F32)],
        input_output_aliases={k: 2 + k for k in range(nb)},
        compiler_params=pltpu.CompilerParams(has_side_effects=_EFFECT),
    )(*ins, *([] if after is None else [after]))
    return outs[0], outs[1], list(outs[2:2 + nb]), outs[-1]


def _copies_wait(started, waits, after, name):
    send_sems, recv_sems, bufs, _ = started
    nb = len(bufs)

    def body(*refs):
        b = refs[:nb]
        ss, rs = refs[nb], refs[nb + 1]
        me3 = (lax.axis_index("x"), lax.axis_index("y"), lax.axis_index("c"))
        for bi, n_blocks, sem, is_send in waits:
            blocks = b[bi].at[pl.ds(0, n_blocks)]
            cp = pltpu.make_async_remote_copy(src_ref=blocks, dst_ref=blocks, send_sem=ss.at[sem], recv_sem=rs.at[sem],
                                              device_id=me3, device_id_type=MESH)
            if is_send:
                cp.wait_send()
            else:
                cp.wait_recv()

    return pl.pallas_call(
        body,
        name=name,
        in_specs=[_HBM] * nb + [_SEM, _SEM, pl.BlockSpec(memory_space=pl.ANY)],
        out_specs=[_HBM] * nb,
        out_shape=[pltpu.HBM(a.shape, a.dtype) for a in bufs],
        input_output_aliases={k: k for k in range(nb)},
        compiler_params=pltpu.CompilerParams(has_side_effects=_EFFECT),
    )(*bufs, send_sems, recv_sems, after)


def _landing(own, me):
    return lax.dynamic_update_slice(lax.empty((N_DEV,) + own.shape[1:], own.dtype), own, (me, 0, 0))


_SIBLING = (0, 0, 1)
_OTHER_CHIPS = ((1, 0, 0), (0, 1, 0), (1, 1, 0))


def _bits(flip):
    return 4 * flip[0] + 2 * flip[1] + flip[2]


_X_NBR, _Y_NBR = (1, 0, 0), (0, 1, 0)


def _xor(bits):
    return functools.partial(lambda me, bits: me ^ bits, bits=bits)


def _gather_send(shards, grp, after):
    nw = len(shards)
    me = _me()
    bufs = list(shards) + [_landing(a[None], me) for a in shards]
    copies = []
    for k in range(nw):
        for s, flip in enumerate((_SIBLING, _X_NBR, _Y_NBR)):
            copies.append((k, lambda me: None, nw + k, lambda me: me, flip, 3 * k + s))
    return _copies_start(bufs, copies, 3 * nw, "gather_send_" + grp, after)


def _gather_pass(started, grp, after):
    nw = len(started[2]) // 2
    waits = [(nw + k, 1, 3 * k + s, is_send) for k in range(nw) for s in range(3) for is_send in (True, False)]
    lands = list(_copies_wait(started, waits, after, "gather_arrived_" + grp)[nw:])
    copies = []
    for k in range(nw):
        held = lambda me: me ^ (4 - 2 * (me % 2))
        copies.append((k, held, k, held, lambda c: (c, 1 - c, 0), 3 * k))
        copies.append((k, _xor(4), k, _xor(4), _SIBLING, 3 * k + 1))
        copies.append((k, _xor(2), k, _xor(2), _SIBLING, 3 * k + 2))
    return _copies_start(lands, copies, 3 * nw, "gather_pass_" + grp)


def _gather_finish(passed, grp, after):
    nw = len(passed[2])
    waits = [(k, 1, 3 * k + s, is_send) for k in range(nw) for s in range(3) for is_send in (True, False)]
    lands = list(_copies_wait(passed, waits, passed[3] if after is None else after, "gather_forwarded_" + grp))
    copies = [(k, _xor(6), k, _xor(6), _SIBLING, k) for k in range(nw)]
    last = _copies_start(lands, copies, nw, "gather_pass2_" + grp)
    waits = [(k, 1, k, is_send) for k in range(nw) for is_send in (True, False)]
    return _copies_wait(last, waits, last[3], "gather_done_" + grp)


def _scatter_send(parts, grp, after=None):
    nw = len(parts)
    me = _me()
    bufs = list(parts) + [_landing(lax.dynamic_index_in_dim(a, me, 0, keepdims=True), me) for a in parts]
    copies = []
    for k in range(nw):
        for f in _FLIPS:
            to = functools.partial(lambda me, bits: me ^ bits, bits=_bits(f))
            copies.append((k, to, nw + k, lambda me: me, f, k))
    return _copies_start(bufs, copies, nw, "scatter_send_" + grp, after)


def _scatter_finish(started, grp, after):
    nw = len(started[2]) // 2
    waits = [(nw + k, N_DEV - 1, k, is_send) for k in range(nw) for is_send in (True, False)]
    return _copies_wait(started, waits, after, "scatter_done_" + grp)[nw:]


N_CHIP = N_DEV // 2


def _pair_sum(part, other, key):
    _, r, c = part.shape
    tm = _ADAM_ROWS.get(r, r)
    core = lax.axis_index("c").astype(jnp.int32).reshape(1)

    def body(core_ref, a_ref, b_ref, o_ref):
        o_ref[...] = (a_ref[...].astype(F32) + b_ref[...].astype(F32)).astype(BF16)

    return pl.pallas_call(
        body,
        name="pair_sum_" + key,
        grid_spec=pltpu.PrefetchScalarGridSpec(
            num_scalar_prefetch=1,
            grid=(N_CHIP, r // tm),
            in_specs=[pl.BlockSpec((None, tm, c), lambda q, i, core_ref: (2 * q + core_ref[0], i, 0)),
                      pl.BlockSpec((None, tm, c), lambda q, i, core_ref: (q, i, 0))],
            out_specs=pl.BlockSpec((None, tm, c), lambda q, i, core_ref: (q, i, 0)),
        ),
        out_shape=jax.ShapeDtypeStruct((N_CHIP, r, c), BF16),
        compiler_params=_cp(("parallel", "parallel")),
    )(core, part, other)


def _scatter2_send(parts, grp, keys):
    nw = len(parts)
    chip = _me() // 2
    swap = []
    for k in range(nw):
        for q in range(N_CHIP):
            src = functools.partial(lambda me, q: 2 * q + 1 - me % 2, q=q)
            swap.append((k, src, nw + k, functools.partial(lambda me, q: q, q=q), _SIBLING, k))
    others = [lax.empty((N_CHIP,) + a.shape[1:], a.dtype) for a in parts]
    started = _copies_start(list(parts) + others, swap, nw, "scatter_swap_" + grp)
    waits = [(nw + k, N_CHIP, k, is_send) for k in range(nw) for is_send in (True, False)]
    swapped = _copies_wait(started, waits, started[3], "scatter_swapped_" + grp)
    pairs = [_pair_sum(swapped[k], swapped[nw + k], keys[k]) for k in range(nw)]
    lands = [lax.dynamic_update_slice(lax.empty(p.shape, p.dtype), lax.dynamic_index_in_dim(p, chip, 0, keepdims=True),
                                      (chip, 0, 0)) for p in pairs]
    copies = []
    for k in range(nw):
        for f in _OTHER_CHIPS:
            to = functools.partial(lambda me, bits: (me ^ bits) // 2, bits=_bits(f))
            copies.append((k, to, nw + k, lambda me: me // 2, f, k))
    return _copies_start(pairs + lands, copies, nw, "scatter_send_" + grp)


def _scatter2_finish(started, grp, after):
    nw = len(started[2]) // 2
    waits = [(nw + k, N_CHIP - 1, k, is_send) for k in range(nw) for is_send in (True, False)]
    return _copies_wait(started, waits, after, "scatter_done_" + grp)[nw:]


_ADAM_ROWS = {352: 176, 1088: 272}


def _sum_adamw(parts, wv, m, v, name, dep=None):
    n_parts, r, c = parts.shape
    tm = _ADAM_ROWS.get(r, r)
    assert r % tm == 0 and wv.shape == (r, c)
    n_dep = 0 if dep is None else 1

    def body(p_ref, w_ref, m_ref, v_ref, *rest):
        g_ref, d_ref, mo_ref, vo_ref = rest[n_dep:]
        gv = p_ref[0].astype(F32)
        for j in range(1, n_parts):
            gv = gv + p_ref[j].astype(F32)
        g_ref[...] = gv
        mn = ADAM_B1 * m_ref[...] + (1.0 - ADAM_B1) * gv
        vn = ADAM_B2 * v_ref[...] + (1.0 - ADAM_B2) * (gv * gv)
        m_hat = mn / (1.0 - ADAM_B1 ** ADAM_STEP)
        v_hat = vn / (1.0 - ADAM_B2 ** ADAM_STEP)
        d_ref[...] = -ADAM_LR * (m_hat / (jnp.sqrt(v_hat) + ADAM_EPS) + ADAM_WD * w_ref[...])
        mo_ref[...] = mn
        vo_ref[...] = vn

    sp = pl.BlockSpec((tm, c), lambda i: (i, 0))
    return pl.pallas_call(
        body,
        name=name,
        grid=(r // tm,),
        in_specs=[pl.BlockSpec((n_parts, tm, c), lambda i: (0, i, 0))] + [sp] * 3 + [pl.BlockSpec(memory_space=pl.ANY)] * n_dep,
        out_specs=[sp] * 4,
        out_shape=[jax.ShapeDtypeStruct((r, c), F32)] * 4,
        compiler_params=_cp(("parallel",), 48),
    )(parts, wv, m, v, *([] if dep is None else [dep]))


_WEIGHTS = ["ffn1_w_gate", "ffn1_w_up", "ffn1_w_down", "ln1_g", "ln1_b", "w_in", "b_gates", "gmlp_ln_g", "gmlp_ln_b",
            "gmlp_w_s", "gmlp_b_s", "w_attn_branch", "w_gmlp_branch", "w_out", "ln2_g", "ln2_b", "ffn2_w_gate",
            "ffn2_w_up", "ffn2_w_down", "ln3_g", "ln3_b"]
_BIG_OF = {"ffn1_w_gate": ("g1", True), "ffn1_w_up": ("u1", True), "ffn1_w_down": ("d1", False), "w_in": ("win", True),
           "w_attn_branch": ("ab", True), "w_gmlp_branch": ("gb", False), "w_out": ("o", False),
           "ffn2_w_gate": ("g2", True), "ffn2_w_up": ("u2", True), "ffn2_w_down": ("d2", False)}
_SMALL = [n for n in _WEIGHTS if n not in _BIG_OF]
_SMALL_ROWS = {"gmlp_w_s": 128, "b_gates": 2}
_SMALL_SLOT = 8


def _pack_small(d, last=None):
    rows = []
    for n in _SMALL:
        r = d[n].reshape(-1, D)
        slot = max(r.shape[0], _SMALL_SLOT)
        rows.append(jnp.pad(r, ((0, slot - r.shape[0]), (0, 0))))
    rows.append(jnp.zeros((_SMALL_SLOT, D), F32) if last is None else jnp.broadcast_to(last.reshape(1, 1), (_SMALL_SLOT, D)))
    return jnp.concatenate(rows, axis=0)


def _unpack_small(packed, shapes):
    out, at = {}, 0
    for n in _SMALL:
        k = _SMALL_ROWS.get(n, 1)
        out[n] = packed[at:at + k].reshape(shapes[n])
        at += max(k, _SMALL_SLOT)
    return out


def kernel(x, positions, ffn1_w_gate, ffn1_w_up, ffn1_w_down, ln1_g, ln1_b, w_in, b_gates, gmlp_ln_g, gmlp_ln_b, gmlp_w_s, gmlp_b_s, w_attn_branch, w_gmlp_branch, w_out, ln2_g, ln2_b, ffn2_w_gate, ffn2_w_up, ffn2_w_down, ln3_g, ln3_b, loss_target, m_ffn1_w_gate, m_ffn1_w_up, m_ffn1_w_down, m_ln1_g, m_ln1_b, m_w_in, m_b_gates, m_gmlp_ln_g, m_gmlp_ln_b, m_gmlp_w_s, m_gmlp_b_s, m_w_attn_branch, m_w_gmlp_branch, m_w_out, m_ln2_g, m_ln2_b, m_ffn2_w_gate, m_ffn2_w_up, m_ffn2_w_down, m_ln3_g, m_ln3_b, v_ffn1_w_gate, v_ffn1_w_up, v_ffn1_w_down, v_ln1_g, v_ln1_b, v_w_in, v_b_gates, v_gmlp_ln_g, v_gmlp_ln_b, v_gmlp_w_s, v_gmlp_b_s, v_w_attn_branch, v_w_gmlp_branch, v_w_out, v_ln2_g, v_ln2_b, v_ffn2_w_gate, v_ffn2_w_up, v_ffn2_w_down, v_ln3_g, v_ln3_b):
    args = dict(locals())
    wts = {n: args[n] for n in _WEIGHTS}
    ms = {n: args["m_" + n] for n in _WEIGHTS}
    vs = {n: args["v_" + n] for n in _WEIGHTS}

    name_of = {key: (n, tr) for n, (key, tr) in _BIG_OF.items()}

    started, tok = {}, None
    for grp, keys in _GROUPS.items():
        shards = []
        for key in keys:
            n, tr = name_of[key]
            s2 = wts[n][0]
            shards.append(_tie(s2.T if tr else s2, tok).astype(BF16))
        started[grp] = _gather_send(shards, grp, tok)
        tok = started[grp][3]
    all_started = tok
    passed = {}

    def get_w(grp, after, early=False):
        if grp not in passed:
            passed[grp] = _gather_pass(started[grp], grp, all_started if after is None else after)
            after = None
        if early:
            return passed[grp][3]
        lands = _gather_finish(passed[grp], grp, after)
        return {key: g.reshape(-1, g.shape[-1]) for key, g in zip(_GROUPS[grp], lands)}

    sent = {}

    def emit(grp, grads):
        if grp == "small":
            sent[grp] = _gather_send([_pack_small(grads, last=grads["loss"])], "small", None)
        else:
            parts = [grads[key].reshape(N_DEV, -1, grads[key].shape[-1]) for key in _SCATTERS[grp]]
            sent[grp] = (_scatter2_send(parts, grp, _SCATTERS[grp]) if grp in _TWO_STAGE else _scatter_send(parts, grp))
        return sent[grp][3]

    p = {n: (wts[n][0] if n in ("gmlp_w_s", "gmlp_b_s") else wts[n]) for n in _SMALL}
    loss, dx, gp = _local_step(x[0], positions.reshape(S, 1), loss_target[0], get_w, p, emit)
    grads, deltas, new_m, new_v = {}, {}, {}, {}
    small_passed = _gather_pass(sent["small"], "small", dx)
    after = small_passed[3]
    for grp in ("ffn2", "mix", "win", "small", "g1", "ud1"):
        if grp == "small":
            parts = _gather_finish(small_passed, "small", after)[0]
            outs = _sum_adamw(parts, *[_pack_small({n: d[n] for n in _SMALL}) for d in (wts, ms, vs)], "update_small")
            shapes = {n: wts[n].shape for n in _SMALL}
            for dst, packed in zip((grads, deltas, new_m, new_v), outs):
                dst.update(_unpack_small(packed, shapes))
            loss = outs[0][-_SMALL_SLOT, 0]
            after = outs[1]
            continue
        arrived = (_scatter2_finish if grp in _TWO_STAGE else _scatter_finish)(sent[grp], grp, after)
        for key, part in zip(_SCATTERS[grp], arrived):
            n, tr = name_of[key]
            outs = _sum_adamw(part, *[(d[n][0].T if tr else d[n][0]) for d in (wts, ms, vs)], "update_" + key, dep=after)
            for dst, o in zip((grads, deltas, new_m, new_v), outs):
                dst[n] = (o.T if tr else o)[None]
            after = outs[1]

    return (loss, dx[None], *[grads[n] for n in _WEIGHTS], *[deltas[n] for n in _WEIGHTS],
            *[new_m[n] for n in _WEIGHTS], *[new_v[n] for n in _WEIGHTS])
```

```python
import functools
import math

import jax
import jax.numpy as jnp
from jax import lax
from jax.experimental import pallas as pl
from jax.experimental.pallas import tpu as pltpu

F32 = jnp.float32
BF16 = jnp.bfloat16

N_DEV = 8
D = 1024
S = 2048
F = 2816
HEAD_DIM = 64
HEADS = 8
GW = HEADS * HEAD_DIM
PATTERN_DILATIONS = (1, 4, 16)
BLK = 128
QKV_W = 3 * 3 * GW
IN_W = QKV_W + 2 * D + 2 * D
DPROJ_W = 5 * 2 * D
DPROJ_Z_COL, DPROJ_G_COL = 3, 4
DPROJ_ZG_AT = DPROJ_Z_COL * 2 * D


def _dproj_tile(t):
    return jnp.where(t < QKV_W // 512, t, t + (DPROJ_ZG_AT - QKV_W) // 512)
ROPE_THETA = 10000.0
ALPHA = 2.0 ** 0.25
LN_EPS = 1e-5
ADAM_LR, ADAM_B1, ADAM_B2, ADAM_EPS, ADAM_WD, ADAM_STEP = 0.001, 0.9, 0.999, 1e-08, 0.01, 10
NEG = -1e30
MESH = pl.DeviceIdType.MESH


def _cp(sem=None, vmem_mb=None):
    kw = {}
    if sem is not None:
        kw["dimension_semantics"] = sem
    if vmem_mb is not None:
        kw["vmem_limit_bytes"] = vmem_mb << 20
    return pltpu.CompilerParams(**kw)


def _dot_nn(a, b):
    return lax.dot_general(a, b, (((1,), (0,)), ((), ())), preferred_element_type=F32)


def _dot_nt(a, b):
    return lax.dot_general(a, b, (((1,), (1,)), ((), ())), preferred_element_type=F32)


def _dot_tn(a, b):
    return lax.dot_general(a, b, (((0,), (0,)), ((), ())), preferred_element_type=F32)


def _ln_fwd(r, g, b):
    mu = jnp.mean(r, axis=-1, keepdims=True)
    xc = r - mu
    var = jnp.mean(xc * xc, axis=-1, keepdims=True)
    rstd = lax.rsqrt(var + LN_EPS)
    xhat = xc * rstd
    return xhat * g + b, xhat, rstd


def _ln_bwd(dh, xhat, rstd, g):
    dxh = dh * g
    m1 = jnp.mean(dxh, axis=-1, keepdims=True)
    m2 = jnp.mean(dxh * xhat, axis=-1, keepdims=True)
    return rstd * (dxh - m1 - xhat * m2)


def _sigmoid(x):
    return 0.5 * jnp.tanh(0.5 * x) + 0.5


def _colsum(x):
    return jnp.sum(x, axis=0, keepdims=True)


def _matmul(a, b, mode, out_dtype, tm, tn, tk, name, b_off=0, n_out=None, dep=None, a_map=None, m_out=None):
    n_dep = 0 if dep is None else 1
    a_map = a_map or (lambda t: t)
    if mode == "nn":
        m, k = a.shape[0], b.shape[0]
        n = b.shape[1]
    elif mode == "nt":
        m, k = a.shape
        n = n_out if n_out is not None else b.shape[0]
    else:
        k, m = a.shape[0], m_out or a.shape[1]
        n = b.shape[1]
    nk = k // tk
    assert m % tm == 0 and n % tn == 0 and k % tk == 0
    dot = {"nn": _dot_nn, "nt": _dot_nt, "tn": _dot_tn}[mode]

    def body(a_ref, b_ref, *rest):
        o_ref, scr = rest[n_dep], rest[n_dep + 1:]
        r = dot(a_ref[...].astype(BF16), b_ref[...].astype(BF16))
        if nk == 1:
            o_ref[...] = r.astype(out_dtype)
        else:
            acc = scr[0]
            kk = pl.program_id(2)

            @pl.when(kk == 0)
            def _():
                acc[...] = r

            @pl.when(kk > 0)
            def _():
                acc[...] += r

            @pl.when(kk == nk - 1)
            def _():
                o_ref[...] = acc[...].astype(out_dtype)

    if mode == "nn":
        a_spec = pl.BlockSpec((tm, tk), lambda i, j, kk: (i, a_map(kk)))
        b_spec = pl.BlockSpec((tk, tn), lambda i, j, kk: (kk, j))
    elif mode == "nt":
        a_spec = pl.BlockSpec((tm, tk), lambda i, j, kk: (i, kk))
        b_spec = pl.BlockSpec((tn, tk), lambda i, j, kk: (j + b_off, kk))
    else:
        a_spec = pl.BlockSpec((tk, tm), lambda i, j, kk: (kk, a_map(i)))
        b_spec = pl.BlockSpec((tk, tn), lambda i, j, kk: (kk, j))
    return pl.pallas_call(
        body,
        name=name,
        grid=(m // tm, n // tn, nk),
        in_specs=[a_spec, b_spec] + [pl.BlockSpec(memory_space=pl.ANY)] * n_dep,
        out_specs=pl.BlockSpec((tm, tn), lambda i, j, kk: (i, j)),
        out_shape=jax.ShapeDtypeStruct((m, n), out_dtype),
        scratch_shapes=[] if nk == 1 else [pltpu.VMEM((tm, tn), F32)],
        compiler_params=_cp(("parallel", "parallel", "arbitrary"), 56),
    )(a, b, *([] if dep is None else [dep]))


def _ffn_up(x, wgt, wut, name, tm=512, tn=1408, dep=None):
    n_dep = 0 if dep is None else 1

    def body(x_ref, wg_ref, wu_ref, *rest):
        ga_ref, gb_ref, hm_ref = rest[n_dep:]
        xb = x_ref[...].astype(BF16)
        a = _dot_nt(xb, wg_ref[...])
        b = _dot_nt(xb, wu_ref[...])
        sig = _sigmoid(a)
        silu = a * sig
        ga_ref[...] = (b * (sig + silu * (1.0 - sig))).astype(BF16)
        gb_ref[...] = silu.astype(BF16)
        hm_ref[...] = (silu * b).astype(BF16)

    wsp = pl.BlockSpec((tn, D), lambda i, j: (j, 0))
    mid = pl.BlockSpec((tm, tn), lambda i, j: (i, j))
    return pl.pallas_call(
        body,
        name=name,
        grid=(S // tm, F // tn),
        in_specs=[pl.BlockSpec((tm, D), lambda i, j: (i, 0)), wsp, wsp] + [pl.BlockSpec(memory_space=pl.ANY)] * n_dep,
        out_specs=[mid, mid, mid],
        out_shape=[jax.ShapeDtypeStruct((S, F), BF16)] * 3,
        compiler_params=_cp(("parallel", "arbitrary"), 56),
    )(x, wgt, wut, *([] if dep is None else [dep]))


def _ffn_down(hm, wd, x, ln_g, ln_b, name, target=None, tm=512):
    head = target is not None

    def body(hm_ref, wd_ref, x_ref, g_ref, b_ref, *rest):
        t_ref = rest[0] if head else None
        o1_ref, o2_ref, xh_ref, rs_ref = rest[1 if head else 0:]
        r = ALPHA * x_ref[...] + 0.5 * _dot_nn(hm_ref[...], wd_ref[...])
        h, xhat, rstd = _ln_fwd(r, g_ref[...], b_ref[...])
        xh_ref[...] = xhat
        rs_ref[...] = rstd
        if head:
            e = h - t_ref[...]
            o1_ref[...] = e * (1.0 / D)
            part = jnp.sum(_colsum(e * e), axis=1, keepdims=True) * (0.5 / D)

            @pl.when(pl.program_id(0) == 0)
            def _():
                o2_ref[...] = jnp.zeros_like(o2_ref)

            o2_ref[...] += part
        else:
            o1_ref[...] = h
            o2_ref[...] = h.astype(BF16)

    row = pl.BlockSpec((tm, D), lambda i: (i, 0))
    vec = pl.BlockSpec((1, D), lambda i: (0, 0))
    second = (pl.BlockSpec((1, 1), lambda i: (0, 0)), jax.ShapeDtypeStruct((1, 1), F32)) if head else (
        row, jax.ShapeDtypeStruct((S, D), BF16))
    return pl.pallas_call(
        body,
        name=name,
        grid=(S // tm,),
        in_specs=[pl.BlockSpec((tm, F), lambda i: (i, 0)), pl.BlockSpec((F, D), lambda i: (0, 0)), row, vec, vec]
        + ([row] if head else []),
        out_specs=[row, second[0], row, pl.BlockSpec((tm, 1), lambda i: (i, 0))],
        out_shape=[
            jax.ShapeDtypeStruct((S, D), F32),
            second[1],
            jax.ShapeDtypeStruct((S, D), F32),
            jax.ShapeDtypeStruct((S, 1), F32),
        ],
        compiler_params=_cp(("arbitrary",), 56),
    )(hm, wd, x, ln_g, ln_b, *([target] if head else []))


def _ffn_bwd_mid(dh_a, dh_b, xhat, rstd, ln_g, a, b, wd, name, tm=512, tn=1408):
    two = dh_b is not None

    def body(*refs):
        dha_ref = refs[0]
        dhb_ref = refs[1] if two else None
        (xh_ref, rs_ref, g_ref, a_ref, b_ref, wd_ref,
         dr_ref, df_ref, da_ref, db_ref, dg_ref, dbias_ref, df_scr) = refs[2 if two else 1:]
        i = pl.program_id(0)
        j = pl.program_id(1)

        @pl.when(j == 0)
        def _():
            dh = dha_ref[...]
            if two:
                dh = ALPHA * dh + dhb_ref[...]
            xhat = xh_ref[...]
            dr = _ln_bwd(dh, xhat, rs_ref[...], g_ref[...])
            dfb = (0.5 * dr).astype(BF16)
            dr_ref[...] = dr
            df_scr[...] = dfb
            df_ref[...] = dfb
            sg = _colsum(dh * xhat)
            sb = _colsum(dh)

            @pl.when(i == 0)
            def _():
                dg_ref[...] = sg
                dbias_ref[...] = sb

            @pl.when(i > 0)
            def _():
                dg_ref[...] += sg
                dbias_ref[...] += sb

        dhm = _dot_nt(df_scr[...], wd_ref[...])
        da_ref[...] = (dhm * a_ref[...].astype(F32)).astype(BF16)
        db_ref[...] = (dhm * b_ref[...].astype(F32)).astype(BF16)

    row = pl.BlockSpec((tm, D), lambda i, j: (i, 0))
    vec = pl.BlockSpec((1, D), lambda i, j: (0, 0))
    mid = pl.BlockSpec((tm, tn), lambda i, j: (i, j))
    ins = [dh_a] + ([dh_b] if two else []) + [xhat, rstd, ln_g, a, b, wd]
    in_specs = [row] * (2 if two else 1) + [row, pl.BlockSpec((tm, 1), lambda i, j: (i, 0)), vec, mid, mid,
                                            pl.BlockSpec((tn, D), lambda i, j: (j, 0))]
    return pl.pallas_call(
        body,
        name=name,
        grid=(S // tm, F // tn),
        in_specs=in_specs,
        out_specs=[row, row, mid, mid, vec, vec],
        out_shape=[
            jax.ShapeDtypeStruct((S, D), F32),
            jax.ShapeDtypeStruct((S, D), BF16),
            jax.ShapeDtypeStruct((S, F), BF16),
            jax.ShapeDtypeStruct((S, F), BF16),
            jax.ShapeDtypeStruct((1, D), F32),
            jax.ShapeDtypeStruct((1, D), F32),
        ],
        scratch_shapes=[pltpu.VMEM((tm, D), BF16)],
        compiler_params=_cp(("arbitrary", "arbitrary"), 56),
    )(*ins)


def _ffn_bwd_dx(dr, da, db, wgt, wut, name, tm=512, tk=1408, dep=None):
    nk = F // tk
    n_dep = 0 if dep is None else 1

    def body(dr_ref, da_ref, db_ref, wg_ref, wu_ref, *rest):
        dx_ref, acc = rest[n_dep], rest[n_dep + 1]
        kk = pl.program_id(1)
        part = _dot_nn(da_ref[...], wg_ref[...]) + _dot_nn(db_ref[...], wu_ref[...])

        @pl.when(kk == 0)
        def _():
            acc[...] = ALPHA * dr_ref[...] + part

        @pl.when(kk > 0)
        def _():
            acc[...] += part

        @pl.when(kk == nk - 1)
        def _():
            dx_ref[...] = acc[...]

    row = pl.BlockSpec((tm, D), lambda i, kk: (i, 0))
    mid = pl.BlockSpec((tm, tk), lambda i, kk: (i, kk))
    wsp = pl.BlockSpec((tk, D), lambda i, kk: (kk, 0))
    return pl.pallas_call(
        body,
        name=name,
        grid=(S // tm, nk),
        in_specs=[row, mid, mid, wsp, wsp] + [pl.BlockSpec(memory_space=pl.ANY)] * n_dep,
        out_specs=row,
        out_shape=jax.ShapeDtypeStruct((S, D), F32),
        scratch_shapes=[pltpu.VMEM((tm, D), F32)],
        compiler_params=_cp(("parallel", "arbitrary"), 56),
    )(dr, da, db, wgt, wut, *([] if dep is None else [dep]))


def _rope_tables(pos_col, invf, sign, tm=512):
    def body(p_ref, f_ref, s_ref, c_out, s_out):
        ang = p_ref[...].astype(F32) * f_ref[...]
        c_out[...] = jnp.cos(ang)
        s_out[...] = jnp.sin(ang) * s_ref[...]

    vec = pl.BlockSpec((1, BLK), lambda i: (0, 0))
    out = pl.BlockSpec((tm, BLK), lambda i: (i, 0))
    return pl.pallas_call(
        body,
        name="rope_tables",
        grid=(S // tm,),
        in_specs=[pl.BlockSpec((tm, 1), lambda i: (i, 0)), vec, vec],
        out_specs=[out, out],
        out_shape=[jax.ShapeDtypeStruct((S, BLK), F32)] * 2,
        compiler_params=_cp(("parallel",)),
    )(pos_col, invf, sign)


def _lane_lo(rows=BLK):
    return (lax.broadcasted_iota(jnp.int32, (rows, GW), 1) % HEAD_DIM) < (HEAD_DIM // 2)


def _swap_halves(t, lo):
    return jnp.where(lo, pltpu.roll(t, GW - HEAD_DIM // 2, 1), pltpu.roll(t, HEAD_DIM // 2, 1))


def _rope(t, cosf, sinf, lo):
    return t * cosf + _swap_halves(t, lo) * sinf


def _unrope(g, cosf, sinf, lo):
    return g * cosf + _swap_halves(g * sinf, lo)


def _tile4(v):
    return jnp.concatenate([v, v, v, v], axis=1)


def _band_mask(n):
    qi = lax.broadcasted_iota(jnp.int32, (BLK, 2 * BLK), 0)
    kj = lax.broadcasted_iota(jnp.int32, (BLK, 2 * BLK), 1)
    dist = qi + BLK - kj
    return (dist >= 0) & (dist <= BLK) & ((kj >= BLK) | (n >= 1))


ROWS = 256
LANES = 128


def _to_view(scr, y, dst_ref, d, dtype, col0=0):
    if d == 1:
        dst_ref[:, col0:col0 + GW] = y.astype(dtype)
        return
    for cb in range(GW // LANES):
        scr[cb][...] = y[:, cb * LANES:(cb + 1) * LANES]
    for r in range(d):
        for cb in range(GW // LANES):
            at = col0 + r * GW + cb * LANES
            dst_ref[:, at:at + LANES] = scr[cb][pl.ds(r, ROWS // d, stride=d), :].astype(dtype)


def _from_view(scr, src_ref, d):
    if d == 1:
        return src_ref[...].astype(F32)
    for r in range(d):
        for cb in range(GW // LANES):
            at = r * GW + cb * LANES
            scr[cb][pl.ds(r, ROWS // d, stride=d), :] = src_ref[:, at:at + LANES].astype(F32)
    return jnp.concatenate([scr[cb][...] for cb in range(GW // LANES)], axis=1)


def _view_spec(d):
    return pl.BlockSpec((ROWS // d, d * GW), lambda i: (i, 0))


_VIEW_SCRATCH = [pltpu.VMEM((ROWS, LANES), F32)] * (GW // LANES)


def _qkv_prep(qkv, cos_t, sin_t):
    def body(x_ref, c_ref, s_ref, *rest):
        outs, scr = rest[:9], rest[9:]
        lo = _lane_lo(ROWS)
        cf, sf = _tile4(c_ref[...]), _tile4(s_ref[...])
        for gi, d in enumerate(PATTERN_DILATIONS):
            q, k, v = (x_ref[:, (3 * part + gi) * GW:(3 * part + gi + 1) * GW].astype(F32) for part in range(3))
            _to_view(scr, _rope(q, cf, sf, lo) * (HEAD_DIM ** -0.5), outs[3 * gi], d, BF16)
            _to_view(scr, _rope(k, cf, sf, lo), outs[3 * gi + 1], d, BF16)
            _to_view(scr, v, outs[3 * gi + 2], d, BF16)

    tab = pl.BlockSpec((ROWS, BLK), lambda i: (i, 0))
    outs = pl.pallas_call(
        body,
        name="qkv_prep",
        grid=(S // ROWS,),
        in_specs=[pl.BlockSpec((ROWS, QKV_W), lambda i: (i, 0)), tab, tab],
        out_specs=[_view_spec(d) for d in PATTERN_DILATIONS for _ in range(3)],
        out_shape=[jax.ShapeDtypeStruct((S // d, d * GW), BF16) for d in PATTERN_DILATIONS for _ in range(3)],
        scratch_shapes=_VIEW_SCRATCH,
        compiler_params=_cp(("parallel",), 48),
    )(qkv, cos_t, sin_t)
    return [tuple(outs[3 * gi:3 * gi + 3]) for gi in range(3)]


N_ATTN_STEPS = S // BLK


def _class_and_block(gi, t):
    per_class = N_ATTN_STEPS // PATTERN_DILATIONS[gi]
    return t >> (per_class.bit_length() - 1), t & (per_class - 1)


def _attn_specs(gi):
    def cur(t):
        r, n = _class_and_block(gi, t)
        return n, r

    def prev(t):
        r, n = _class_and_block(gi, t)
        return jnp.maximum(n - 1, 0), r

    def whole(t):
        return 0, _class_and_block(gi, t)[0]

    sub = S // PATTERN_DILATIONS[gi]
    return (pl.BlockSpec((BLK, GW), cur), pl.BlockSpec((BLK, GW), prev), pl.BlockSpec((sub, GW), whole))


def _left_lanes():
    return lax.broadcasted_iota(jnp.int32, (BLK, LANES), 1) < HEAD_DIM


def _stack_heads(t2, left):
    zero = jnp.zeros_like(t2)
    return jnp.concatenate([jnp.where(left, t2, zero), jnp.where(left, zero, t2)], axis=0)


def _attn_fwd_block(n, q_ref, kc_ref, kp_ref, vc_ref, vp_ref, o_ref, l_ref):
    mask = _band_mask(n)
    left = _left_lanes()
    zero = jnp.zeros((BLK, LANES), BF16)
    for pr in range(GW // LANES):
        cs = slice(pr * LANES, (pr + 1) * LANES)
        q2 = q_ref[:, cs]
        k2 = jnp.concatenate([kp_ref[:, cs], kc_ref[:, cs]], axis=0)
        v2 = jnp.concatenate([vp_ref[:, cs], vc_ref[:, cs]], axis=0)
        o_h, lse_h = [], []
        for side in (left, ~left):
            s = jnp.where(mask, _dot_nt(jnp.where(side, q2, zero), k2), NEG)
            m = jnp.max(s, axis=1, keepdims=True)
            p = jnp.exp(s - m)
            l = jnp.sum(p, axis=1, keepdims=True)
            o_h.append(_dot_nn((p / l).astype(BF16), v2))
            lse_h.append(m + jnp.log(l))
        o_ref[:, cs] = jnp.where(left, o_h[0], o_h[1])
        l_ref[:, cs] = jnp.where(left, lse_h[0], lse_h[1])


def _attn_fwd(preps):
    def body(*refs):
        t = pl.program_id(0)
        for gi in range(3):
            _attn_fwd_block(_class_and_block(gi, t)[1], *refs[5 * gi:5 * gi + 5], *refs[15 + 2 * gi:17 + 2 * gi])

    in_specs, out_specs, out_shape, args = [], [], [], []
    for gi, d in enumerate(PATTERN_DILATIONS):
        cur, prev, _ = _attn_specs(gi)
        qr, kr, vv = preps[gi]
        in_specs += [cur, cur, prev, cur, prev]
        args += [qr, kr, kr, vv, vv]
        out_specs += [cur, cur]
        out_shape += [jax.ShapeDtypeStruct((S // d, d * GW), F32)] * 2
    outs = pl.pallas_call(
        body,
        name="attn_fwd",
        grid=(N_ATTN_STEPS,),
        in_specs=in_specs,
        out_specs=out_specs,
        out_shape=out_shape,
        compiler_params=_cp(("arbitrary",), 48),
    )(*args)
    return [outs[0], outs[2], outs[4]], [outs[1], outs[3], outs[5]]


def _attn_bwd_block(n, q_ref, kc_ref, kp_ref, vc_ref, vp_ref, do_ref, l_ref, c_ref, dq_ref, dk_ref, dv_ref):
    mask = _band_mask(n)
    left = _left_lanes()
    left2 = jnp.concatenate([left, left], axis=0)
    here = pl.ds(pl.multiple_of(n * BLK, BLK), BLK)
    before = pl.ds(pl.multiple_of(jnp.maximum(n - 1, 0) * BLK, BLK), BLK)
    for pr in range(GW // LANES):
        cs = slice(pr * LANES, (pr + 1) * LANES)
        q2 = q_ref[:, cs]
        do2 = do_ref[:, cs]
        k_bd = _stack_heads(jnp.concatenate([kp_ref[:, cs], kc_ref[:, cs]], axis=0), left2)
        v_bd = _stack_heads(jnp.concatenate([vp_ref[:, cs], vc_ref[:, cs]], axis=0), left2)
        s2 = _dot_nt(q2, k_bd)
        dp2 = _dot_nt(do2, v_bd)
        ps, dss = [], []
        for h in range(2):
            at = pr * LANES + h * HEAD_DIM
            half = slice(h * 2 * BLK, (h + 1) * 2 * BLK)
            p = jnp.exp(jnp.where(mask, s2[:, half], NEG) - l_ref[:, at:at + 1])
            ps.append(p.astype(BF16))
            dss.append((p * (dp2[:, half] - c_ref[:, at:at + 1])).astype(BF16))
        ds_both = jnp.concatenate(dss, axis=1)
        dq_ref[:, cs] = _dot_nn(ds_both, k_bd) * (HEAD_DIM ** -0.5)
        dk_bd = _dot_tn(ds_both, q2)
        dv_bd = _dot_tn(jnp.concatenate(ps, axis=1), do2)
        dk2 = jnp.where(left2, dk_bd[:2 * BLK], dk_bd[2 * BLK:])
        dv2 = jnp.where(left2, dv_bd[:2 * BLK], dv_bd[2 * BLK:])
        dk_ref[here, cs] = dk2[BLK:]
        dv_ref[here, cs] = dv2[BLK:]
        dk_ref[before, cs] += dk2[:BLK]
        dv_ref[before, cs] += dv2[:BLK]


def _attn_bwd(preps, dos, lses, cterms):
    def body(*refs):
        t = pl.program_id(0)
        for gi in range(3):
            _attn_bwd_block(_class_and_block(gi, t)[1], *refs[8 * gi:8 * gi + 8], *refs[24 + 3 * gi:27 + 3 * gi])

    in_specs, out_specs, out_shape, args = [], [], [], []
    for gi, d in enumerate(PATTERN_DILATIONS):
        cur, prev, whole = _attn_specs(gi)
        qr, kr, vv = preps[gi]
        in_specs += [cur, cur, prev, cur, prev, cur, cur, cur]
        args += [qr, kr, kr, vv, vv, dos[gi], lses[gi], cterms[gi]]
        out_specs += [cur, whole, whole]
        out_shape += [jax.ShapeDtypeStruct((S // d, d * GW), F32)] * 3
    outs = pl.pallas_call(
        body,
        name="attn_bwd",
        grid=(N_ATTN_STEPS,),
        in_specs=in_specs,
        out_specs=out_specs,
        out_shape=out_shape,
        compiler_params=_cp(("arbitrary",), 56),
    )(*args)
    return [tuple(outs[3 * gi:3 * gi + 3]) for gi in range(3)]


def _qkv_unprep(grads, cos_t, sin_t, dproj):
    def body(*refs):
        views, (c_ref, s_ref, _, out_ref), scr = refs[:9], refs[9:13], refs[13:]
        lo = _lane_lo(ROWS)
        cf, sf = _tile4(c_ref[...]), _tile4(s_ref[...])
        for gi, d in enumerate(PATTERN_DILATIONS):
            dq_ref, dk_ref, dv_ref = views[3 * gi:3 * gi + 3]
            qa, ka, va = gi * GW, (3 + gi) * GW, (6 + gi) * GW
            out_ref[:, qa:qa + GW] = _unrope(_from_view(scr, dq_ref, d), cf, sf, lo).astype(BF16)
            out_ref[:, ka:ka + GW] = _unrope(_from_view(scr, dk_ref, d), cf, sf, lo).astype(BF16)
            out_ref[:, va:va + GW] = _from_view(scr, dv_ref, d).astype(BF16)

    tab = pl.BlockSpec((ROWS, BLK), lambda i: (i, 0))
    return pl.pallas_call(
        body,
        name="qkv_unprep",
        grid=(S // ROWS,),
        in_specs=[_view_spec(d) for d in PATTERN_DILATIONS for _ in range(3)] + [tab, tab, pl.BlockSpec(memory_space=pl.ANY)],
        out_specs=pl.BlockSpec((ROWS, QKV_W), lambda i: (i, 0)),
        out_shape=jax.ShapeDtypeStruct(dproj.shape, dproj.dtype),
        input_output_aliases={11: 0},
        scratch_shapes=_VIEW_SCRATCH,
        compiler_params=_cp(("parallel",), 48),
    )(*[g for grp in grads for g in grp], cos_t, sin_t, dproj)


def _group_weights(l0, l1, l2):
    mx = jnp.maximum(jnp.maximum(l0, l1), l2)
    e0, e1, e2 = jnp.exp(l0 - mx), jnp.exp(l1 - mx), jnp.exp(l2 - mx)
    inv = 1.0 / (e0 + e1 + e2)
    return e0 * inv, e1 * inv, e2 * inv


def _combine_fwd(os_, lses):
    def body(o0, o1, o2, l0, l1, l2, y_ref, *scr):
        ov = [_from_view(scr, o, d) for o, d in zip((o0, o1, o2), PATTERN_DILATIONS)]
        lv = [_from_view(scr, l, d) for l, d in zip((l0, l1, l2), PATTERN_DILATIONS)]
        w0, w1, w2 = _group_weights(*lv)
        y_ref[...] = (w0 * ov[0] + w1 * ov[1] + w2 * ov[2]).astype(BF16)

    views = [_view_spec(d) for d in PATTERN_DILATIONS]
    return pl.pallas_call(
        body,
        name="attn_combine_fwd",
        grid=(S // ROWS,),
        in_specs=views + views,
        out_specs=pl.BlockSpec((ROWS, GW), lambda i: (i, 0)),
        out_shape=jax.ShapeDtypeStruct((S, GW), BF16),
        scratch_shapes=_VIEW_SCRATCH,
        compiler_params=_cp(("parallel",)),
    )(*os_, *lses)


def _combine_bwd(dy, os_, lses, seg):
    def body(dy_ref, o0, o1, o2, l0, l1, l2, seg_ref, d0, d1, d2, c0, c1, c2, *scr):
        ov = [_from_view(scr, o, d) for o, d in zip((o0, o1, o2), PATTERN_DILATIONS)]
        lv = [_from_view(scr, l, d) for l, d in zip((l0, l1, l2), PATTERN_DILATIONS)]
        ws = _group_weights(*lv)
        dyv = dy_ref[...]
        t = dyv * (ws[0] * ov[0] + ws[1] * ov[1] + ws[2] * ov[2])
        t_hi = t.astype(BF16)
        r1 = t - t_hi.astype(F32)
        t_mid = r1.astype(BF16)
        t_lo = (r1 - t_mid.astype(F32)).astype(BF16)
        sg = seg_ref[...]
        e = _dot_nn(t_hi, sg) + _dot_nn(t_mid, sg) + _dot_nn(t_lo, sg)
        for w, do_ref, c_ref, d in zip(ws, (d0, d1, d2), (c0, c1, c2), PATTERN_DILATIONS):
            _to_view(scr, w * dyv, do_ref, d, BF16)
            _to_view(scr, w * e, c_ref, d, F32)

    views = [_view_spec(d) for d in PATTERN_DILATIONS]
    return pl.pallas_call(
        body,
        name="attn_combine_bwd",
        grid=(S // ROWS,),
        in_specs=[pl.BlockSpec((ROWS, GW), lambda i: (i, 0))] + views + views + [pl.BlockSpec((GW, GW), lambda i: (0, 0))],
        out_specs=views + views,
        out_shape=[jax.ShapeDtypeStruct((S // d, d * GW), BF16) for d in PATTERN_DILATIONS]
        + [jax.ShapeDtypeStruct((S // d, d * GW), F32) for d in PATTERN_DILATIONS],
        scratch_shapes=_VIEW_SCRATCH,
        compiler_params=_cp(("parallel",)),
    )(dy, *os_, *lses, seg)


_SQRT_HALF = 0.7071067811865476
_INV_SQRT_2PI = 0.3989422804014327


def _gelu(z):
    return 0.5 * z * (1.0 + lax.erf(z * _SQRT_HALF))


def _gelu_grad(z):
    return 0.5 * (1.0 + lax.erf(z * _SQRT_HALF)) + z * (_INV_SQRT_2PI * jnp.exp(-0.5 * z * z))


def _tril_ws(ws_ref, g):
    t = lax.broadcasted_iota(jnp.int32, (BLK, BLK), 0)
    s = lax.broadcasted_iota(jnp.int32, (BLK, BLK), 1)
    return jnp.where(t >= s, ws_ref[g], 0.0)


def _chunks_side_by_side(x, cols, nch):
    return jnp.concatenate([x[c * BLK:(c + 1) * BLK, cols] for c in range(nch)], axis=1)


def _gmlp_fwd(z, ws, bst, ln_g, ln_b, tm=512):
    nch = tm // BLK

    def body(z_ref, ws_ref, b_ref, g_ref, be_ref, y_ref):
        zg = _gelu(z_ref[...].astype(F32))
        u = zg[:, :D]
        vn, _, _ = _ln_fwd(zg[:, D:], g_ref[...], be_ref[...])
        vnb = vn.astype(BF16)
        bt = b_ref[...]
        for g in range(8):
            w = _tril_ws(ws_ref, g).astype(BF16)
            cols = slice(g * BLK, (g + 1) * BLK)
            mixed = _dot_nn(w, _chunks_side_by_side(vnb, cols, nch)) + bt[:, g:g + 1]
            for c in range(nch):
                rows = slice(c * BLK, (c + 1) * BLK)
                y_ref[rows, cols] = (u[rows, cols] * mixed[:, c * BLK:(c + 1) * BLK]).astype(BF16)

    return pl.pallas_call(
        body,
        name="gmlp_fwd",
        grid=(S // tm,),
        in_specs=[
            pl.BlockSpec((tm, 2 * D), lambda i: (i, 0)),
            pl.BlockSpec((8, BLK, BLK), lambda i: (0, 0, 0)),
            pl.BlockSpec((BLK, 8), lambda i: (0, 0)),
            pl.BlockSpec((1, D), lambda i: (0, 0)),
            pl.BlockSpec((1, D), lambda i: (0, 0)),
        ],
        out_specs=pl.BlockSpec((tm, D), lambda i: (i, 0)),
        out_shape=jax.ShapeDtypeStruct((S, D), BF16),
        compiler_params=_cp(("parallel",), 48),
    )(z, ws, bst, ln_g, ln_b)


def _gmlp_bwd(z, dy, ws, bst, ln_g, ln_b, dproj, tm=512):
    nch = tm // BLK

    def body(z_ref, dy_ref, ws_ref, b_ref, g_ref, be_ref, _, dz_ref, dws_ref, dbs_ref, dg_ref, dbe_ref, dvn_scr, dm_acc):
        i = pl.program_id(0)
        zv = z_ref[...].astype(F32)
        zg = _gelu(zv)
        u = zg[:, :D]
        gam = g_ref[...]
        vn, xhat, rstd = _ln_fwd(zg[:, D:], gam, be_ref[...])
        vnb = vn.astype(BF16)
        dyv = dy_ref[...]
        dmix = dyv * u
        dmb = dmix.astype(BF16)
        bt = b_ref[...]
        tmask = lax.broadcasted_iota(jnp.int32, (BLK, BLK), 0) >= lax.broadcasted_iota(jnp.int32, (BLK, BLK), 1)
        dm_sum = dmix[0:BLK]
        for c in range(1, nch):
            dm_sum = dm_sum + dmix[c * BLK:(c + 1) * BLK]

        @pl.when(i == 0)
        def _():
            dm_acc[...] = jnp.zeros_like(dm_acc)
            dws_ref[...] = jnp.zeros_like(dws_ref)
            dg_ref[...] = jnp.zeros_like(dg_ref)
            dbe_ref[...] = jnp.zeros_like(dbe_ref)

        dm_acc[...] += dm_sum
        dus = []
        for g in range(8):
            w = _tril_ws(ws_ref, g).astype(BF16)
            cols = slice(g * BLK, (g + 1) * BLK)
            v_cat = _chunks_side_by_side(vnb, cols, nch)
            dm_cat = _chunks_side_by_side(dmb, cols, nch)
            mixed = _dot_nn(w, v_cat) + bt[:, g:g + 1]
            dus.append(jnp.concatenate(
                [dyv[c * BLK:(c + 1) * BLK, cols] * mixed[:, c * BLK:(c + 1) * BLK] for c in range(nch)], axis=0))
            dws_ref[g] += jnp.where(tmask, _dot_nt(dm_cat, v_cat), 0.0)
            dvn_cat = _dot_tn(w, dm_cat)
            for c in range(nch):
                dvn_scr[c * BLK:(c + 1) * BLK, cols] = dvn_cat[:, c * BLK:(c + 1) * BLK]

        dvn = dvn_scr[...]
        dg_ref[...] += _colsum(dvn * xhat)
        dbe_ref[...] += _colsum(dvn)
        dvg = _ln_bwd(dvn, xhat, rstd, gam)
        gp = _gelu_grad(zv)
        dz_ref[:, :D] = (jnp.concatenate(dus, axis=1) * gp[:, :D]).astype(BF16)
        dz_ref[:, D:] = (dvg * gp[:, D:]).astype(BF16)

        @pl.when(i == S // tm - 1)
        def _():
            acc = dm_acc[...]
            for g in range(8):
                dbs_ref[:, g:g + 1] = jnp.sum(acc[:, g * BLK:(g + 1) * BLK], axis=1, keepdims=True)

    vec = pl.BlockSpec((1, D), lambda i: (0, 0))
    return pl.pallas_call(
        body,
        name="gmlp_bwd",
        grid=(S // tm,),
        in_specs=[
            pl.BlockSpec((tm, 2 * D), lambda i: (i, 0)),
            pl.BlockSpec((tm, D), lambda i: (i, 0)),
            pl.BlockSpec((8, BLK, BLK), lambda i: (0, 0, 0)),
            pl.BlockSpec((BLK, 8), lambda i: (0, 0)),
            vec,
            vec,
            pl.BlockSpec(memory_space=pl.ANY),
        ],
        out_specs=[
            pl.BlockSpec((tm, 2 * D), lambda i: (i, DPROJ_Z_COL)),
            pl.BlockSpec((8, BLK, BLK), lambda i: (0, 0, 0)),
            pl.BlockSpec((BLK, 8), lambda i: (0, 0)),
            vec,
            vec,
        ],
        out_shape=[
            jax.ShapeDtypeStruct(dproj.shape, dproj.dtype),
            jax.ShapeDtypeStruct((8, BLK, BLK), F32),
            jax.ShapeDtypeStruct((BLK, 8), F32),
            jax.ShapeDtypeStruct((1, D), F32),
            jax.ShapeDtypeStruct((1, D), F32),
        ],
        input_output_aliases={6: 0},
        scratch_shapes=[pltpu.VMEM((tm, D), F32), pltpu.VMEM((BLK, D), F32)],
        compiler_params=_cp(("arbitrary",), 48),
    )(z, dy, ws, bst, ln_g, ln_b, dproj)


def _merge_fwd(ya, yg, glog, bgate, h1, wabt, wgb, wo, ln_g, ln_b, tm=256):
    def body(ya_ref, yg_ref, gl_ref, bg_ref, h1_ref, wab_ref, wgb_ref, wo_ref, g_ref, b_ref,
             h_ref, hb_ref, xh_ref, rs_ref, mg_ref, bra_ref, brg_ref):
        bra = _dot_nt(ya_ref[...], wab_ref[...])
        brg = _dot_nn(yg_ref[...], wgb_ref[...])
        gates = _sigmoid(gl_ref[...].astype(F32) + bg_ref[...])
        merged = (gates[:, :D] * bra + gates[:, D:] * brg).astype(BF16)
        mix = _dot_nn(merged, wo_ref[...])
        h, xhat, rstd = _ln_fwd(ALPHA * h1_ref[...] + mix, g_ref[...], b_ref[...])
        h_ref[...] = h
        hb_ref[...] = h.astype(BF16)
        xh_ref[...] = xhat
        rs_ref[...] = rstd
        mg_ref[...] = merged
        bra_ref[...] = bra
        brg_ref[...] = brg

    row = pl.BlockSpec((tm, D), lambda i: (i, 0))
    vec = pl.BlockSpec((1, D), lambda i: (0, 0))
    full = lambda shape: pl.BlockSpec(shape, lambda i: (0, 0))
    return pl.pallas_call(
        body,
        name="merge_fwd",
        grid=(S // tm,),
        in_specs=[
            pl.BlockSpec((tm, GW), lambda i: (i, 0)), row,
            pl.BlockSpec((tm, 2 * D), lambda i: (i, glog.shape[1] // (2 * D) - 1)),
            full((1, 2 * D)), row,
            full((D, GW)), full((D, D)), full((D, D)), vec, vec,
        ],
        out_specs=[row, row, row, pl.BlockSpec((tm, 1), lambda i: (i, 0)), row, row, row],
        out_shape=[
            jax.ShapeDtypeStruct((S, D), F32),
            jax.ShapeDtypeStruct((S, D), BF16),
            jax.ShapeDtypeStruct((S, D), F32),
            jax.ShapeDtypeStruct((S, 1), F32),
            jax.ShapeDtypeStruct((S, D), BF16),
            jax.ShapeDtypeStruct((S, D), F32),
            jax.ShapeDtypeStruct((S, D), F32),
        ],
        compiler_params=_cp(("parallel",), 48),
    )(ya, yg, glog, bgate, h1, wabt, wgb, wo, ln_g, ln_b)


def _merge_bwd(dh2, xhat, rstd, ln_g, bra, brg, glog, bgate, wabt, wgb, wo, tm=256):
    def body(dh_ref, xh_ref, rs_ref, g_ref, bra_ref, brg_ref, gl_ref, bg_ref, wab_ref, wgb_ref, wo_ref,
             dr_ref, drb_ref, dlog_ref, dba_ref, dbg_ref, dya_ref, dyg_ref, dbgate_ref, dg_ref, dbias_ref):
        i = pl.program_id(0)
        dh = dh_ref[...]
        xh = xh_ref[...]
        dr = _ln_bwd(dh, xh, rs_ref[...], g_ref[...])
        drb = dr.astype(BF16)
        dr_ref[...] = dr
        drb_ref[...] = drb
        dmerged = _dot_nt(drb, wo_ref[...])
        gates = _sigmoid(gl_ref[...].astype(F32) + bg_ref[...])
        g0, g1 = gates[:, :D], gates[:, D:]
        dl0 = dmerged * bra_ref[...] * g0 * (1.0 - g0)
        dl1 = dmerged * brg_ref[...] * g1 * (1.0 - g1)
        dlog_ref[:, :D] = dl0.astype(BF16)
        dlog_ref[:, D:] = dl1.astype(BF16)
        dba = (dmerged * g0).astype(BF16)
        dbg = (dmerged * g1).astype(BF16)
        dba_ref[...] = dba
        dbg_ref[...] = dbg
        dya_ref[...] = _dot_nn(dba, wab_ref[...])
        dyg_ref[...] = _dot_nt(dbg, wgb_ref[...])
        s0, s1 = _colsum(dl0), _colsum(dl1)
        sg, sb = _colsum(dh * xh), _colsum(dh)

        @pl.when(i == 0)
        def _():
            dbgate_ref[:, :D] = s0
            dbgate_ref[:, D:] = s1
            dg_ref[...] = sg
            dbias_ref[...] = sb

        @pl.when(i > 0)
        def _():
            dbgate_ref[:, :D] += s0
            dbgate_ref[:, D:] += s1
            dg_ref[...] += sg
            dbias_ref[...] += sb

    row = pl.BlockSpec((tm, D), lambda i: (i, 0))
    vec = pl.BlockSpec((1, D), lambda i: (0, 0))
    wide = pl.BlockSpec((tm, 2 * D), lambda i: (i, 0))
    full = lambda shape: pl.BlockSpec(shape, lambda i: (0, 0))
    return pl.pallas_call(
        body,
        name="merge_bwd",
        grid=(S // tm,),
        in_specs=[row, row, pl.BlockSpec((tm, 1), lambda i: (i, 0)), vec, row, row,
                  pl.BlockSpec((tm, 2 * D), lambda i: (i, glog.shape[1] // (2 * D) - 1)),
                  full((1, 2 * D)), full((D, GW)), full((D, D)), full((D, D))],
        out_specs=[row, row, pl.BlockSpec((tm, 2 * D), lambda i: (i, DPROJ_G_COL)), row, row,
                   pl.BlockSpec((tm, GW), lambda i: (i, 0)), row, full((1, 2 * D)), vec, vec],
        out_shape=[
            jax.ShapeDtypeStruct((S, D), F32),
            jax.ShapeDtypeStruct((S, D), BF16),
            jax.ShapeDtypeStruct((S, DPROJ_W), BF16),
            jax.ShapeDtypeStruct((S, D), BF16),
            jax.ShapeDtypeStruct((S, D), BF16),
            jax.ShapeDtypeStruct((S, GW), F32),
            jax.ShapeDtypeStruct((S, D), F32),
            jax.ShapeDtypeStruct((1, 2 * D), F32),
            jax.ShapeDtypeStruct((1, D), F32),
            jax.ShapeDtypeStruct((1, D), F32),
        ],
        compiler_params=_cp(("arbitrary",), 48),
    )(dh2, xhat, rstd, ln_g, bra, brg, glog, bgate, wabt, wgb, wo)


def _dproj_to_dh1(dproj, win, dep, tm=512):
    n_dep = 0 if dep is None else 1

    def body(a_ref, b_ref, *rest):
        rest[n_dep][...] = (_dot_nn(a_ref[:, :QKV_W], b_ref[:QKV_W, :])
                            + _dot_nn(a_ref[:, DPROJ_ZG_AT:], b_ref[QKV_W:, :]))

    return pl.pallas_call(
        body,
        name="dproj_to_dh1",
        grid=(S // tm,),
        in_specs=[pl.BlockSpec((tm, DPROJ_W), lambda i: (i, 0)),
                  pl.BlockSpec((IN_W, D), lambda i: (0, 0), pipeline_mode=pl.Buffered(1))]
        + [pl.BlockSpec(memory_space=pl.ANY)] * n_dep,
        out_specs=pl.BlockSpec((tm, D), lambda i: (i, 0)),
        out_shape=jax.ShapeDtypeStruct((S, D), F32),
        compiler_params=_cp(("parallel",), 56),
    )(dproj, win, *([] if dep is None else [dep]))


def _tie(x, dep):
    if dep is None:
        return x
    return x + dep[0, 0].astype(x.dtype)


def _local_step(x, pos_col, target, get_w, p, emit):
    w = get_w("ffn1", None)
    a1, b1, hm1 = _ffn_up(x, w["g1"], w["u1"], "ffn1_up")
    w.update(get_w("ffn1d", hm1))
    h1, h1b, xh1, rs1 = _ffn_down(hm1, w["d1"], x, p["ln1_g"], p["ln1_b"], "ffn1_down")

    w.update(get_w("win", h1b))
    qkv = _matmul(h1b, w["win"], "nt", BF16, S, 1536, D, "proj_qkv", b_off=0, n_out=QKV_W)
    z = glog = _matmul(h1b, w["win"], "nt", BF16, S, 512, D, "proj_zg", b_off=QKV_W // 512, n_out=4 * D)

    half = jnp.arange(0, HEAD_DIM, 2, dtype=F32) / HEAD_DIM
    inv_freq = ROPE_THETA ** (-half)
    invf = jnp.tile(inv_freq, 4).reshape(1, BLK)
    sign = jnp.tile(jnp.concatenate([-jnp.ones((32,), F32), jnp.ones((32,), F32)]), 2).reshape(1, BLK)
    cos_t, sin_t = _rope_tables(pos_col, invf, sign)

    preps = _qkv_prep(qkv, cos_t, sin_t)
    os_, lses = _attn_fwd(preps)
    ya = _combine_fwd(os_, lses)
    get_w("late", ya, early=True)
    bst = p["gmlp_b_s"].T
    yg = _gmlp_fwd(z, p["gmlp_w_s"], bst, p["gmlp_ln_g"], p["gmlp_ln_b"])
    w.update(get_w("late", yg))
    h2, h2b, xh2, rs2, merged, bra, brg = _merge_fwd(ya, yg, glog, p["b_gates"], h1, w["ab"], w["gb"], w["o"],
                                                      p["ln2_g"], p["ln2_b"])
    a2, b2, hm2 = _ffn_up(h2b, w["g2"], w["u2"], "ffn2_up")
    dh3, loss, xh3, rs3 = _ffn_down(hm2, w["d2"], h2, p["ln3_g"], p["ln3_b"], "ffn2_down", target=target)

    gp = {}
    dr3, df2, da2, db2, gp["ln3_g"], gp["ln3_b"] = _ffn_bwd_mid(dh3, None, xh3, rs3, p["ln3_g"], a2, b2, w["d2"],
                                                                "ffn2_bwd_mid")
    tok = emit("ffn2", {
        "g2": _matmul(da2, h2b, "tn", BF16, 1408, D, S, "wgrad_g2"),
        "u2": _matmul(db2, h2b, "tn", BF16, 1408, D, S, "wgrad_u2"),
        "d2": _matmul(hm2, df2, "tn", BF16, 1408, D, S, "wgrad_d2")})
    dh2 = _ffn_bwd_dx(dr3, da2, db2, w["g2"], w["u2"], "ffn2_bwd_dx")

    (dr2, dr2b, dproj, dba, dbg, dya, dyg, gp["b_gates"], gp["ln2_g"], gp["ln2_b"]) = _merge_bwd(
        dh2, xh2, rs2, _tie(p["ln2_g"], tok), bra, brg, glog, p["b_gates"], w["ab"], w["gb"], w["o"])
    tok = emit("mix", {
        "o": _matmul(merged, dr2b, "tn", BF16, 512, D, S, "wgrad_o"),
        "ab": _matmul(dba, ya, "tn", BF16, 512, GW, S, "wgrad_ab"),
        "gb": _matmul(yg, dbg, "tn", BF16, 512, D, S, "wgrad_gb")})

    seg = (jnp.arange(GW)[:, None] // HEAD_DIM == jnp.arange(GW)[None, :] // HEAD_DIM).astype(BF16)
    do0, do1, do2, c0, c1, c2 = _combine_bwd(dya, os_, lses, _tie(seg, tok))
    dproj = _qkv_unprep(_attn_bwd(preps, (do0, do1, do2), lses, (c0, c1, c2)), cos_t, sin_t, dproj)
    dproj, gp["gmlp_w_s"], dbst, gp["gmlp_ln_g"], gp["gmlp_ln_b"] = _gmlp_bwd(
        z, dyg, p["gmlp_w_s"], bst, p["gmlp_ln_g"], p["gmlp_ln_b"], dproj)
    gp["gmlp_b_s"] = dbst.T
    tok = emit("win", {"win": _matmul(dproj, h1b, "tn", BF16, 512, D, S, "wgrad_win", a_map=_dproj_tile, m_out=IN_W)})
    dh1m = _dproj_to_dh1(dproj, w["win"], tok)

    dr1, df1, da1, db1, gp["ln1_g"], gp["ln1_b"] = _ffn_bwd_mid(dr2, dh1m, xh1, rs1, p["ln1_g"], a1, b1, w["d1"],
                                                                "ffn1_bwd_mid")
    tok = emit("small", {**gp, "loss": loss})
    tok = emit("g1", {"g1": _matmul(da1, x, "tn", BF16, 1408, D, S, "wgrad_g1", dep=tok)})
    tok = emit("ud1", {"u1": _matmul(db1, x, "tn", BF16, 1408, D, S, "wgrad_u1", dep=tok),
                       "d1": _matmul(hm1, df1, "tn", BF16, 1408, D, S, "wgrad_d1", dep=tok)})
    dx = _ffn_bwd_dx(dr1, da1, db1, w["g1"], w["u1"], "ffn1_bwd_dx", dep=tok)
    return loss, dx, gp


_FLIPS = [(mx, my, mc) for mx in (0, 1) for my in (0, 1) for mc in (0, 1)][1:]
_GROUPS = {"ffn1": ("g1", "u1"), "ffn1d": ("d1",), "win": ("win",), "late": ("ab", "gb", "o", "g2", "u2", "d2")}
_SCATTERS = {"ffn2": ("g2", "u2", "d2"), "mix": ("ab", "gb", "o"), "win": ("win",), "g1": ("g1",), "ud1": ("u1", "d1")}
_TWO_STAGE = ("g1", "ud1")
_HBM = pl.BlockSpec(memory_space=pltpu.HBM)
_SEM = pl.BlockSpec(memory_space=pltpu.SEMAPHORE)
_EFFECT = pltpu.SideEffectType.DATAFLOW_SIDE_EFFECTING


def _me():
    return 4 * lax.axis_index("x") + 2 * lax.axis_index("y") + lax.axis_index("c")


def _copies_start(bufs, copies, n_sem, name, after=None):
    nb = len(bufs)
    n_after = 0 if after is None else 1

    def body(*refs):
        b = refs[:nb]
        send_sems, recv_sems = refs[nb + n_after], refs[nb + n_after + 1]
        token = refs[-1]
        x, y, c = lax.axis_index("x"), lax.axis_index("y"), lax.axis_index("c")
        me = 4 * x + 2 * y + c
        for si, s_slot, di, d_slot, flip, sem in copies:
            s_idx, d_idx = s_slot(me), d_slot(me)
            mx, my, mc = flip(c) if callable(flip) else flip
            pltpu.make_async_remote_copy(
                src_ref=b[si] if s_idx is None else b[si].at[s_idx],
                dst_ref=b[di] if d_idx is None else b[di].at[d_idx],
                send_sem=send_sems.at[sem], recv_sem=recv_sems.at[sem],
                device_id=(x ^ mx, y ^ my, c ^ mc), device_id_type=MESH).start()
        token[...] = jnp.zeros_like(token)

    ins = [pltpu.with_memory_space_constraint(a, pltpu.HBM) for a in bufs]
    outs = pl.pallas_call(
        body,
        name=name,
        in_specs=[_HBM] * nb + [pl.BlockSpec(memory_space=pl.ANY)] * n_after,
        out_specs=[_SEM, _SEM] + [_HBM] * nb + [pl.BlockSpec(memory_space=pltpu.VMEM)],
        out_shape=[pltpu.SemaphoreType.DMA((n_sem,)), pltpu.SemaphoreType.DMA((n_sem,))]
        + [pltpu.HBM(a.shape, a.dtype) for a in ins] + [jax.ShapeDtypeStruct((8, 128), F32)],
        input_output_aliases={k: 2 + k for k in range(nb)},
        compiler_params=pltpu.CompilerParams(has_side_effects=_EFFECT),
    )(*ins, *([] if after is None else [after]))
    return outs[0], outs[1], list(outs[2:2 + nb]), outs[-1]


def _copies_wait(started, waits, after, name):
    send_sems, recv_sems, bufs, _ = started
    nb = len(bufs)

    def body(*refs):
        b = refs[:nb]
        ss, rs = refs[nb], refs[nb + 1]
        me3 = (lax.axis_index("x"), lax.axis_index("y"), lax.axis_index("c"))
        for bi, n_blocks, sem, is_send in waits:
            blocks = b[bi].at[pl.ds(0, n_blocks)]
            cp = pltpu.make_async_remote_copy(src_ref=blocks, dst_ref=blocks, send_sem=ss.at[sem], recv_sem=rs.at[sem],
                                              device_id=me3, device_id_type=MESH)
            if is_send:
                cp.wait_send()
            else:
                cp.wait_recv()

    return pl.pallas_call(
        body,
        name=name,
        in_specs=[_HBM] * nb + [_SEM, _SEM, pl.BlockSpec(memory_space=pl.ANY)],
        out_specs=[_HBM] * nb,
        out_shape=[pltpu.HBM(a.shape, a.dtype) for a in bufs],
        input_output_aliases={k: k for k in range(nb)},
        compiler_params=pltpu.CompilerParams(has_side_effects=_EFFECT),
    )(*bufs, send_sems, recv_sems, after)


def _landing(own, me):
    return lax.dynamic_update_slice(lax.empty((N_DEV,) + own.shape[1:], own.dtype), own, (me, 0, 0))


_SIBLING = (0, 0, 1)
_OTHER_CHIPS = ((1, 0, 0), (0, 1, 0), (1, 1, 0))


def _bits(flip):
    return 4 * flip[0] + 2 * flip[1] + flip[2]


_X_NBR, _Y_NBR = (1, 0, 0), (0, 1, 0)


def _xor(bits):
    return functools.partial(lambda me, bits: me ^ bits, bits=bits)


def _gather_send(shards, grp, after):
    nw = len(shards)
    me = _me()
    bufs = list(shards) + [_landing(a[None], me) for a in shards]
    copies = []
    for k in range(nw):
        for s, flip in enumerate((_SIBLING, _X_NBR, _Y_NBR)):
            copies.append((k, lambda me: None, nw + k, lambda me: me, flip, 3 * k + s))
    return _copies_start(bufs, copies, 3 * nw, "gather_send_" + grp, after)


def _gather_pass(started, grp, after):
    nw = len(started[2]) // 2
    waits = [(nw + k, 1, 3 * k + s, is_send) for k in range(nw) for s in range(3) for is_send in (True, False)]
    lands = list(_copies_wait(started, waits, after, "gather_arrived_" + grp)[nw:])
    copies = []
    for k in range(nw):
        held = lambda me: me ^ (4 - 2 * (me % 2))
        copies.append((k, held, k, held, lambda c: (c, 1 - c, 0), 3 * k))
        copies.append((k, _xor(4), k, _xor(4), _SIBLING, 3 * k + 1))
        copies.append((k, _xor(2), k, _xor(2), _SIBLING, 3 * k + 2))
    return _copies_start(lands, copies, 3 * nw, "gather_pass_" + grp)


def _gather_finish(passed, grp, after):
    nw = len(passed[2])
    waits = [(k, 1, 3 * k + s, is_send) for k in range(nw) for s in range(3) for is_send in (True, False)]
    lands = list(_copies_wait(passed, waits, passed[3] if after is None else after, "gather_forwarded_" + grp))
    copies = [(k, _xor(6), k, _xor(6), _SIBLING, k) for k in range(nw)]
    last = _copies_start(lands, copies, nw, "gather_pass2_" + grp)
    waits = [(k, 1, k, is_send) for k in range(nw) for is_send in (True, False)]
    return _copies_wait(last, waits, last[3], "gather_done_" + grp)


def _scatter_send(parts, grp, after=None):
    nw = len(parts)
    me = _me()
    bufs = list(parts) + [_landing(lax.dynamic_index_in_dim(a, me, 0, keepdims=True), me) for a in parts]
    copies = []
    for k in range(nw):
        for f in _FLIPS:
            to = functools.partial(lambda me, bits: me ^ bits, bits=_bits(f))
            copies.append((k, to, nw + k, lambda me: me, f, k))
    return _copies_start(bufs, copies, nw, "scatter_send_" + grp, after)


def _scatter_finish(started, grp, after):
    nw = len(started[2]) // 2
    waits = [(nw + k, N_DEV - 1, k, is_send) for k in range(nw) for is_send in (True, False)]
    return _copies_wait(started, waits, after, "scatter_done_" + grp)[nw:]


N_CHIP = N_DEV // 2


def _pair_sum(part, other, key):
    _, r, c = part.shape
    tm = _ADAM_ROWS.get(r, r)
    core = lax.axis_index("c").astype(jnp.int32).reshape(1)

    def body(core_ref, a_ref, b_ref, o_ref):
        o_ref[...] = (a_ref[...].astype(F32) + b_ref[...].astype(F32)).astype(BF16)

    return pl.pallas_call(
        body,
        name="pair_sum_" + key,
        grid_spec=pltpu.PrefetchScalarGridSpec(
            num_scalar_prefetch=1,
            grid=(N_CHIP, r // tm),
            in_specs=[pl.BlockSpec((None, tm, c), lambda q, i, core_ref: (2 * q + core_ref[0], i, 0)),
                      pl.BlockSpec((None, tm, c), lambda q, i, core_ref: (q, i, 0))],
            out_specs=pl.BlockSpec((None, tm, c), lambda q, i, core_ref: (q, i, 0)),
        ),
        out_shape=jax.ShapeDtypeStruct((N_CHIP, r, c), BF16),
        compiler_params=_cp(("parallel", "parallel")),
    )(core, part, other)


def _scatter2_send(parts, grp, keys):
    nw = len(parts)
    chip = _me() // 2
    swap = []
    for k in range(nw):
        for q in range(N_CHIP):
            src = functools.partial(lambda me, q: 2 * q + 1 - me % 2, q=q)
            swap.append((k, src, nw + k, functools.partial(lambda me, q: q, q=q), _SIBLING, k))
    others = [lax.empty((N_CHIP,) + a.shape[1:], a.dtype) for a in parts]
    started = _copies_start(list(parts) + others, swap, nw, "scatter_swap_" + grp)
    waits = [(nw + k, N_CHIP, k, is_send) for k in range(nw) for is_send in (True, False)]
    swapped = _copies_wait(started, waits, started[3], "scatter_swapped_" + grp)
    pairs = [_pair_sum(swapped[k], swapped[nw + k], keys[k]) for k in range(nw)]
    lands = [lax.dynamic_update_slice(lax.empty(p.shape, p.dtype), lax.dynamic_index_in_dim(p, chip, 0, keepdims=True),
                                      (chip, 0, 0)) for p in pairs]
    copies = []
    for k in range(nw):
        for f in _OTHER_CHIPS:
            to = functools.partial(lambda me, bits: (me ^ bits) // 2, bits=_bits(f))
            copies.append((k, to, nw + k, lambda me: me // 2, f, k))
    return _copies_start(pairs + lands, copies, nw, "scatter_send_" + grp)


def _scatter2_finish(started, grp, after):
    nw = len(started[2]) // 2
    waits = [(nw + k, N_CHIP - 1, k, is_send) for k in range(nw) for is_send in (True, False)]
    return _copies_wait(started, waits, after, "scatter_done_" + grp)[nw:]


_ADAM_ROWS = {352: 176, 1088: 272}


def _sum_adamw(parts, wv, m, v, name, dep=None):
    n_parts, r, c = parts.shape
    tm = _ADAM_ROWS.get(r, r)
    assert r % tm == 0 and wv.shape == (r, c)
    n_dep = 0 if dep is None else 1

    def body(p_ref, w_ref, m_ref, v_ref, *rest):
        g_ref, d_ref, mo_ref, vo_ref = rest[n_dep:]
        gv = p_ref[0].astype(F32)
        for j in range(1, n_parts):
            gv = gv + p_ref[j].astype(F32)
        g_ref[...] = gv
        mn = ADAM_B1 * m_ref[...] + (1.0 - ADAM_B1) * gv
        vn = ADAM_B2 * v_ref[...] + (1.0 - ADAM_B2) * (gv * gv)
        m_hat = mn / (1.0 - ADAM_B1 ** ADAM_STEP)
        v_hat = vn / (1.0 - ADAM_B2 ** ADAM_STEP)
        d_ref[...] = -ADAM_LR * (m_hat / (jnp.sqrt(v_hat) + ADAM_EPS) + ADAM_WD * w_ref[...])
        mo_ref[...] = mn
        vo_ref[...] = vn

    sp = pl.BlockSpec((tm, c), lambda i: (i, 0))
    return pl.pallas_call(
        body,
        name=name,
        grid=(r // tm,),
        in_specs=[pl.BlockSpec((n_parts, tm, c), lambda i: (0, i, 0))] + [sp] * 3 + [pl.BlockSpec(memory_space=pl.ANY)] * n_dep,
        out_specs=[sp] * 4,
        out_shape=[jax.ShapeDtypeStruct((r, c), F32)] * 4,
        compiler_params=_cp(("parallel",), 48),
    )(parts, wv, m, v, *([] if dep is None else [dep]))


_WEIGHTS = ["ffn1_w_gate", "ffn1_w_up", "ffn1_w_down", "ln1_g", "ln1_b", "w_in", "b_gates", "gmlp_ln_g", "gmlp_ln_b",
            "gmlp_w_s", "gmlp_b_s", "w_attn_branch", "w_gmlp_branch", "w_out", "ln2_g", "ln2_b", "ffn2_w_gate",
            "ffn2_w_up", "ffn2_w_down", "ln3_g", "ln3_b"]
_BIG_OF = {"ffn1_w_gate": ("g1", True), "ffn1_w_up": ("u1", True), "ffn1_w_down": ("d1", False), "w_in": ("win", True),
           "w_attn_branch": ("ab", True), "w_gmlp_branch": ("gb", False), "w_out": ("o", False),
           "ffn2_w_gate": ("g2", True), "ffn2_w_up": ("u2", True), "ffn2_w_down": ("d2", False)}
_SMALL = [n for n in _WEIGHTS if n not in _BIG_OF]
_SMALL_ROWS = {"gmlp_w_s": 128, "b_gates": 2}
_SMALL_SLOT = 8


def _pack_small(d, last=None):
    rows = []
    for n in _SMALL:
        r = d[n].reshape(-1, D)
        slot = max(r.shape[0], _SMALL_SLOT)
        rows.append(jnp.pad(r, ((0, slot - r.shape[0]), (0, 0))))
    rows.append(jnp.zeros((_SMALL_SLOT, D), F32) if last is None else jnp.broadcast_to(last.reshape(1, 1), (_SMALL_SLOT, D)))
    return jnp.concatenate(rows, axis=0)


def _unpack_small(packed, shapes):
    out, at = {}, 0
    for n in _SMALL:
        k = _SMALL_ROWS.get(n, 1)
        out[n] = packed[at:at + k].reshape(shapes[n])
        at += max(k, _SMALL_SLOT)
    return out


def kernel(x, positions, ffn1_w_gate, ffn1_w_up, ffn1_w_down, ln1_g, ln1_b, w_in, b_gates, gmlp_ln_g, gmlp_ln_b, gmlp_w_s, gmlp_b_s, w_attn_branch, w_gmlp_branch, w_out, ln2_g, ln2_b, ffn2_w_gate, ffn2_w_up, ffn2_w_down, ln3_g, ln3_b, loss_target, m_ffn1_w_gate, m_ffn1_w_up, m_ffn1_w_down, m_ln1_g, m_ln1_b, m_w_in, m_b_gates, m_gmlp_ln_g, m_gmlp_ln_b, m_gmlp_w_s, m_gmlp_b_s, m_w_attn_branch, m_w_gmlp_branch, m_w_out, m_ln2_g, m_ln2_b, m_ffn2_w_gate, m_ffn2_w_up, m_ffn2_w_down, m_ln3_g, m_ln3_b, v_ffn1_w_gate, v_ffn1_w_up, v_ffn1_w_down, v_ln1_g, v_ln1_b, v_w_in, v_b_gates, v_gmlp_ln_g, v_gmlp_ln_b, v_gmlp_w_s, v_gmlp_b_s, v_w_attn_branch, v_w_gmlp_branch, v_w_out, v_ln2_g, v_ln2_b, v_ffn2_w_gate, v_ffn2_w_up, v_ffn2_w_down, v_ln3_g, v_ln3_b):
    args = dict(locals())
    wts = {n: args[n] for n in _WEIGHTS}
    ms = {n: args["m_" + n] for n in _WEIGHTS}
    vs = {n: args["v_" + n] for n in _WEIGHTS}

    name_of = {key: (n, tr) for n, (key, tr) in _BIG_OF.items()}

    shards = {}
    for grp, keys in _GROUPS.items():
        shards[grp] = []
        for key in keys:
            n, tr = name_of[key]
            s2 = wts[n][0]
            shards[grp].append((s2.T if tr else s2).astype(BF16))
    started, passed = {}, {}

    def send(grp, after):
        started[grp] = _gather_send(shards[grp], grp, after)
        return started[grp][3]

    def pass_on(grp, after):
        passed[grp] = _gather_pass(started[grp], grp, after)
        return passed[grp][3]

    def finish(grp, after):
        lands = _gather_finish(passed[grp], grp, after)
        return {key: g.reshape(-1, g.shape[-1]) for key, g in zip(_GROUPS[grp], lands)}

    def get_w(grp, after, early=False):
        if grp == "ffn1":
            return finish("ffn1", send("win", send("ffn1d", pass_on("ffn1", send("ffn1", None)))))
        if grp == "ffn1d":
            return finish("ffn1d", send("late", pass_on("win", pass_on("ffn1d", after))))
        if early:
            pass_on(grp, after)
            return None
        return finish(grp, after)

    sent = {}

    def emit(grp, grads):
        if grp == "small":
            sent[grp] = _gather_send([_pack_small(grads, last=grads["loss"])], "small", None)
        else:
            parts = [grads[key].reshape(N_DEV, -1, grads[key].shape[-1]) for key in _SCATTERS[grp]]
            sent[grp] = (_scatter2_send(parts, grp, _SCATTERS[grp]) if grp in _TWO_STAGE else _scatter_send(parts, grp))
        return sent[grp][3]

    p = {n: (wts[n][0] if n in ("gmlp_w_s", "gmlp_b_s") else wts[n]) for n in _SMALL}
    loss, dx, gp = _local_step(x[0], positions.reshape(S, 1), loss_target[0], get_w, p, emit)
    grads, deltas, new_m, new_v = {}, {}, {}, {}
    small_passed = _gather_pass(sent["small"], "small", dx)
    after = small_passed[3]
    for grp in ("ffn2", "mix", "win", "small", "g1", "ud1"):
        if grp == "small":
            parts = _gather_finish(small_passed, "small", after)[0]
            outs = _sum_adamw(parts, *[_pack_small({n: d[n] for n in _SMALL}) for d in (wts, ms, vs)], "update_small")
            shapes = {n: wts[n].shape for n in _SMALL}
            for dst, packed in zip((grads, deltas, new_m, new_v), outs):
                dst.update(_unpack_small(packed, shapes))
            loss = outs[0][-_SMALL_SLOT, 0]
            after = outs[1]
            continue
        arrived = (_scatter2_finish if grp in _TWO_STAGE else _scatter_finish)(sent[grp], grp, after)
        for key, part in zip(_SCATTERS[grp], arrived):
            n, tr = name_of[key]
            outs = _sum_adamw(part, *[(d[n][0].T if tr else d[n][0]) for d in (wts, ms, vs)], "update_" + key, dep=after)
            for dst, o in zip((grads, deltas, new_m, new_v), outs):
                dst[n] = (o.T if tr else o)[None]
            after = outs[1]

    return (loss, dx[None], *[grads[n] for n in _WEIGHTS], *[deltas[n] for n in _WEIGHTS],
            *[new_m[n] for n in _WEIGHTS], *[new_v[n] for n in _WEIGHTS])
```

```python
import functools
import math

import jax
import jax.numpy as jnp
from jax import lax
from jax.experimental import pallas as pl
from jax.experimental.pallas import tpu as pltpu

F32 = jnp.float32
BF16 = jnp.bfloat16

N_DEV = 8
D = 1024
S = 2048
F = 2816
HEAD_DIM = 64
HEADS = 8
GW = HEADS * HEAD_DIM
PATTERN_DILATIONS = (1, 4, 16)
BLK = 128
QKV_W = 3 * 3 * GW
IN_W = QKV_W + 2 * D + 2 * D
DPROJ_W = 5 * 2 * D
DPROJ_Z_COL, DPROJ_G_COL = 3, 4
DPROJ_ZG_AT = DPROJ_Z_COL * 2 * D


def _dproj_tile(t):
    return jnp.where(t < QKV_W // 512, t, t + (DPROJ_ZG_AT - QKV_W) // 512)
ROPE_THETA = 10000.0
ALPHA = 2.0 ** 0.25
LN_EPS = 1e-5
ADAM_LR, ADAM_B1, ADAM_B2, ADAM_EPS, ADAM_WD, ADAM_STEP = 0.001, 0.9, 0.999, 1e-08, 0.01, 10
NEG = -1e30
MESH = pl.DeviceIdType.MESH


def _cp(sem=None, vmem_mb=None):
    kw = {}
    if sem is not None:
        kw["dimension_semantics"] = sem
    if vmem_mb is not None:
        kw["vmem_limit_bytes"] = vmem_mb << 20
    return pltpu.CompilerParams(**kw)


def _dot_nn(a, b):
    return lax.dot_general(a, b, (((1,), (0,)), ((), ())), preferred_element_type=F32)


def _dot_nt(a, b):
    return lax.dot_general(a, b, (((1,), (1,)), ((), ())), preferred_element_type=F32)


def _dot_tn(a, b):
    return lax.dot_general(a, b, (((0,), (0,)), ((), ())), preferred_element_type=F32)


def _ln_fwd(r, g, b):
    mu = jnp.mean(r, axis=-1, keepdims=True)
    xc = r - mu
    var = jnp.mean(xc * xc, axis=-1, keepdims=True)
    rstd = lax.rsqrt(var + LN_EPS)
    xhat = xc * rstd
    return xhat * g + b, xhat, rstd


def _ln_bwd(dh, xhat, rstd, g):
    dxh = dh * g
    m1 = jnp.mean(dxh, axis=-1, keepdims=True)
    m2 = jnp.mean(dxh * xhat, axis=-1, keepdims=True)
    return rstd * (dxh - m1 - xhat * m2)


def _sigmoid(x):
    return 0.5 * jnp.tanh(0.5 * x) + 0.5


def _colsum(x):
    return jnp.sum(x, axis=0, keepdims=True)


def _matmul(a, b, mode, out_dtype, tm, tn, tk, name, b_off=0, n_out=None, dep=None, a_map=None, m_out=None):
    n_dep = 0 if dep is None else 1
    a_map = a_map or (lambda t: t)
    if mode == "nn":
        m, k = a.shape[0], b.shape[0]
        n = b.shape[1]
    elif mode == "nt":
        m, k = a.shape
        n = n_out if n_out is not None else b.shape[0]
    else:
        k, m = a.shape[0], m_out or a.shape[1]
        n = b.shape[1]
    nk = k // tk
    assert m % tm == 0 and n % tn == 0 and k % tk == 0
    dot = {"nn": _dot_nn, "nt": _dot_nt, "tn": _dot_tn}[mode]

    def body(a_ref, b_ref, *rest):
        o_ref, scr = rest[n_dep], rest[n_dep + 1:]
        r = dot(a_ref[...].astype(BF16), b_ref[...].astype(BF16))
        if nk == 1:
            o_ref[...] = r.astype(out_dtype)
        else:
            acc = scr[0]
            kk = pl.program_id(2)

            @pl.when(kk == 0)
            def _():
                acc[...] = r

            @pl.when(kk > 0)
            def _():
                acc[...] += r

            @pl.when(kk == nk - 1)
            def _():
                o_ref[...] = acc[...].astype(out_dtype)

    if mode == "nn":
        a_spec = pl.BlockSpec((tm, tk), lambda i, j, kk: (i, a_map(kk)))
        b_spec = pl.BlockSpec((tk, tn), lambda i, j, kk: (kk, j))
    elif mode == "nt":
        a_spec = pl.BlockSpec((tm, tk), lambda i, j, kk: (i, kk))
        b_spec = pl.BlockSpec((tn, tk), lambda i, j, kk: (j + b_off, kk))
    else:
        a_spec = pl.BlockSpec((tk, tm), lambda i, j, kk: (kk, a_map(i)))
        b_spec = pl.BlockSpec((tk, tn), lambda i, j, kk: (kk, j))
    return pl.pallas_call(
        body,
        name=name,
        grid=(m // tm, n // tn, nk),
        in_specs=[a_spec, b_spec] + [pl.BlockSpec(memory_space=pl.ANY)] * n_dep,
        out_specs=pl.BlockSpec((tm, tn), lambda i, j, kk: (i, j)),
        out_shape=jax.ShapeDtypeStruct((m, n), out_dtype),
        scratch_shapes=[] if nk == 1 else [pltpu.VMEM((tm, tn), F32)],
        compiler_params=_cp(("parallel", "parallel", "arbitrary"), 56),
    )(a, b, *([] if dep is None else [dep]))


def _ffn_up(x, wgt, wut, name, tm=256, tn=F, dep=None):
    n_dep = 0 if dep is None else 1

    def body(x_ref, wg_ref, wu_ref, *rest):
        ga_ref, gb_ref, hm_ref = rest[n_dep:]
        xb = x_ref[...].astype(BF16)
        a = _dot_nt(xb, wg_ref[...])
        b = _dot_nt(xb, wu_ref[...])
        sig = _sigmoid(a)
        silu = a * sig
        ga_ref[...] = (b * (sig + silu * (1.0 - sig))).astype(BF16)
        gb_ref[...] = silu.astype(BF16)
        hm_ref[...] = (silu * b).astype(BF16)

    wsp = pl.BlockSpec((tn, D), lambda i, j: (j, 0))
    mid = pl.BlockSpec((tm, tn), lambda i, j: (i, j))
    return pl.pallas_call(
        body,
        name=name,
        grid=(S // tm, F // tn),
        in_specs=[pl.BlockSpec((tm, D), lambda i, j: (i, 0)), wsp, wsp] + [pl.BlockSpec(memory_space=pl.ANY)] * n_dep,
        out_specs=[mid, mid, mid],
        out_shape=[jax.ShapeDtypeStruct((S, F), BF16)] * 3,
        compiler_params=_cp(("parallel", "arbitrary"), 56),
    )(x, wgt, wut, *([] if dep is None else [dep]))


def _ffn_down(hm, wd, x, ln_g, ln_b, name, target=None, tm=512):
    head = target is not None

    def body(hm_ref, wd_ref, x_ref, g_ref, b_ref, *rest):
        t_ref = rest[0] if head else None
        o1_ref, o2_ref, xh_ref, rs_ref = rest[1 if head else 0:]
        r = ALPHA * x_ref[...] + 0.5 * _dot_nn(hm_ref[...], wd_ref[...])
        h, xhat, rstd = _ln_fwd(r, g_ref[...], b_ref[...])
        xh_ref[...] = xhat
        rs_ref[...] = rstd
        if head:
            e = h - t_ref[...]
            o1_ref[...] = e * (1.0 / D)
            part = jnp.sum(_colsum(e * e), axis=1, keepdims=True) * (0.5 / D)

            @pl.when(pl.program_id(0) == 0)
            def _():
                o2_ref[...] = jnp.zeros_like(o2_ref)

            o2_ref[...] += part
        else:
            o1_ref[...] = h
            o2_ref[...] = h.astype(BF16)

    row = pl.BlockSpec((tm, D), lambda i: (i, 0))
    vec = pl.BlockSpec((1, D), lambda i: (0, 0))
    second = (pl.BlockSpec((1, 1), lambda i: (0, 0)), jax.ShapeDtypeStruct((1, 1), F32)) if head else (
        row, jax.ShapeDtypeStruct((S, D), BF16))
    return pl.pallas_call(
        body,
        name=name,
        grid=(S // tm,),
        in_specs=[pl.BlockSpec((tm, F), lambda i: (i, 0)), pl.BlockSpec((F, D), lambda i: (0, 0)), row, vec, vec]
        + ([row] if head else []),
        out_specs=[row, second[0], row, pl.BlockSpec((tm, 1), lambda i: (i, 0))],
        out_shape=[
            jax.ShapeDtypeStruct((S, D), F32),
            second[1],
            jax.ShapeDtypeStruct((S, D), F32),
            jax.ShapeDtypeStruct((S, 1), F32),
        ],
        compiler_params=_cp(("arbitrary",), 56),
    )(hm, wd, x, ln_g, ln_b, *([target] if head else []))


def _ffn_bwd_mid(dh_a, dh_b, xhat, rstd, ln_g, a, b, wd, name, tm=512, tn=1408):
    two = dh_b is not None

    def body(*refs):
        dha_ref = refs[0]
        dhb_ref = refs[1] if two else None
        (xh_ref, rs_ref, g_ref, a_ref, b_ref, wd_ref,
         dr_ref, df_ref, da_ref, db_ref, dg_ref, dbias_ref, df_scr) = refs[2 if two else 1:]
        i = pl.program_id(0)
        j = pl.program_id(1)

        @pl.when(j == 0)
        def _():
            dh = dha_ref[...]
            if two:
                dh = ALPHA * dh + dhb_ref[...]
            xhat = xh_ref[...]
            dr = _ln_bwd(dh, xhat, rs_ref[...], g_ref[...])
            dfb = (0.5 * dr).astype(BF16)
            dr_ref[...] = dr
            df_scr[...] = dfb
            df_ref[...] = dfb
            sg = _colsum(dh * xhat)
            sb = _colsum(dh)

            @pl.when(i == 0)
            def _():
                dg_ref[...] = sg
                dbias_ref[...] = sb

            @pl.when(i > 0)
            def _():
                dg_ref[...] += sg
                dbias_ref[...] += sb

        dhm = _dot_nt(df_scr[...], wd_ref[...])
        da_ref[...] = (dhm * a_ref[...].astype(F32)).astype(BF16)
        db_ref[...] = (dhm * b_ref[...].astype(F32)).astype(BF16)

    row = pl.BlockSpec((tm, D), lambda i, j: (i, 0))
    vec = pl.BlockSpec((1, D), lambda i, j: (0, 0))
    mid = pl.BlockSpec((tm, tn), lambda i, j: (i, j))
    ins = [dh_a] + ([dh_b] if two else []) + [xhat, rstd, ln_g, a, b, wd]
    in_specs = [row] * (2 if two else 1) + [row, pl.BlockSpec((tm, 1), lambda i, j: (i, 0)), vec, mid, mid,
                                            pl.BlockSpec((tn, D), lambda i, j: (j, 0))]
    return pl.pallas_call(
        body,
        name=name,
        grid=(S // tm, F // tn),
        in_specs=in_specs,
        out_specs=[row, row, mid, mid, vec, vec],
        out_shape=[
            jax.ShapeDtypeStruct((S, D), F32),
            jax.ShapeDtypeStruct((S, D), BF16),
            jax.ShapeDtypeStruct((S, F), BF16),
            jax.ShapeDtypeStruct((S, F), BF16),
            jax.ShapeDtypeStruct((1, D), F32),
            jax.ShapeDtypeStruct((1, D), F32),
        ],
        scratch_shapes=[pltpu.VMEM((tm, D), BF16)],
        compiler_params=_cp(("arbitrary", "arbitrary"), 56),
    )(*ins)


def _ffn_bwd_dx(dr, da, db, wgt, wut, name, tm=512, tk=1408, dep=None):
    nk = F // tk
    n_dep = 0 if dep is None else 1

    def body(dr_ref, da_ref, db_ref, wg_ref, wu_ref, *rest):
        dx_ref, acc = rest[n_dep], rest[n_dep + 1]
        kk = pl.program_id(1)
        part = _dot_nn(da_ref[...], wg_ref[...]) + _dot_nn(db_ref[...], wu_ref[...])

        @pl.when(kk == 0)
        def _():
            acc[...] = ALPHA * dr_ref[...] + part

        @pl.when(kk > 0)
        def _():
            acc[...] += part

        @pl.when(kk == nk - 1)
        def _():
            dx_ref[...] = acc[...]

    row = pl.BlockSpec((tm, D), lambda i, kk: (i, 0))
    mid = pl.BlockSpec((tm, tk), lambda i, kk: (i, kk))
    wsp = pl.BlockSpec((tk, D), lambda i, kk: (kk, 0))
    return pl.pallas_call(
        body,
        name=name,
        grid=(S // tm, nk),
        in_specs=[row, mid, mid, wsp, wsp] + [pl.BlockSpec(memory_space=pl.ANY)] * n_dep,
        out_specs=row,
        out_shape=jax.ShapeDtypeStruct((S, D), F32),
        scratch_shapes=[pltpu.VMEM((tm, D), F32)],
        compiler_params=_cp(("parallel", "arbitrary"), 56),
    )(dr, da, db, wgt, wut, *([] if dep is None else [dep]))


def _rope_tables(pos_col, invf, sign, tm=512):
    def body(p_ref, f_ref, s_ref, c_out, s_out):
        ang = p_ref[...].astype(F32) * f_ref[...]
        c_out[...] = jnp.cos(ang)
        s_out[...] = jnp.sin(ang) * s_ref[...]

    vec = pl.BlockSpec((1, BLK), lambda i: (0, 0))
    out = pl.BlockSpec((tm, BLK), lambda i: (i, 0))
    return pl.pallas_call(
        body,
        name="rope_tables",
        grid=(S // tm,),
        in_specs=[pl.BlockSpec((tm, 1), lambda i: (i, 0)), vec, vec],
        out_specs=[out, out],
        out_shape=[jax.ShapeDtypeStruct((S, BLK), F32)] * 2,
        compiler_params=_cp(("parallel",)),
    )(pos_col, invf, sign)


def _lane_lo(rows=BLK):
    return (lax.broadcasted_iota(jnp.int32, (rows, GW), 1) % HEAD_DIM) < (HEAD_DIM // 2)


def _swap_halves(t, lo):
    return jnp.where(lo, pltpu.roll(t, GW - HEAD_DIM // 2, 1), pltpu.roll(t, HEAD_DIM // 2, 1))


def _rope(t, cosf, sinf, lo):
    return t * cosf + _swap_halves(t, lo) * sinf


def _unrope(g, cosf, sinf, lo):
    return g * cosf + _swap_halves(g * sinf, lo)


def _tile4(v):
    return jnp.concatenate([v, v, v, v], axis=1)


def _band_mask(n):
    qi = lax.broadcasted_iota(jnp.int32, (BLK, 2 * BLK), 0)
    kj = lax.broadcasted_iota(jnp.int32, (BLK, 2 * BLK), 1)
    dist = qi + BLK - kj
    return (dist >= 0) & (dist <= BLK) & ((kj >= BLK) | (n >= 1))


ROWS = 256
LANES = 128


def _to_view(scr, y, dst_ref, d, dtype, col0=0):
    if d == 1:
        dst_ref[:, col0:col0 + GW] = y.astype(dtype)
        return
    for cb in range(GW // LANES):
        scr[cb][...] = y[:, cb * LANES:(cb + 1) * LANES]
    for r in range(d):
        for cb in range(GW // LANES):
            at = col0 + r * GW + cb * LANES
            dst_ref[:, at:at + LANES] = scr[cb][pl.ds(r, ROWS // d, stride=d), :].astype(dtype)


def _from_view(scr, src_ref, d):
    if d == 1:
        return src_ref[...].astype(F32)
    for r in range(d):
        for cb in range(GW // LANES):
            at = r * GW + cb * LANES
            scr[cb][pl.ds(r, ROWS // d, stride=d), :] = src_ref[:, at:at + LANES].astype(F32)
    return jnp.concatenate([scr[cb][...] for cb in range(GW // LANES)], axis=1)


def _view_spec(d):
    return pl.BlockSpec((ROWS // d, d * GW), lambda i: (i, 0))


_VIEW_SCRATCH = [pltpu.VMEM((ROWS, LANES), F32)] * (GW // LANES)


def _qkv_prep(qkv, cos_t, sin_t):
    def body(x_ref, c_ref, s_ref, *rest):
        outs, scr = rest[:9], rest[9:]
        lo = _lane_lo(ROWS)
        cf, sf = _tile4(c_ref[...]), _tile4(s_ref[...])
        for gi, d in enumerate(PATTERN_DILATIONS):
            q, k, v = (x_ref[:, (3 * part + gi) * GW:(3 * part + gi + 1) * GW].astype(F32) for part in range(3))
            _to_view(scr, _rope(q, cf, sf, lo) * (HEAD_DIM ** -0.5), outs[3 * gi], d, BF16)
            _to_view(scr, _rope(k, cf, sf, lo), outs[3 * gi + 1], d, BF16)
            _to_view(scr, v, outs[3 * gi + 2], d, BF16)

    tab = pl.BlockSpec((ROWS, BLK), lambda i: (i, 0))
    outs = pl.pallas_call(
        body,
        name="qkv_prep",
        grid=(S // ROWS,),
        in_specs=[pl.BlockSpec((ROWS, QKV_W), lambda i: (i, 0)), tab, tab],
        out_specs=[_view_spec(d) for d in PATTERN_DILATIONS for _ in range(3)],
        out_shape=[jax.ShapeDtypeStruct((S // d, d * GW), BF16) for d in PATTERN_DILATIONS for _ in range(3)],
        scratch_shapes=_VIEW_SCRATCH,
        compiler_params=_cp(("parallel",), 48),
    )(qkv, cos_t, sin_t)
    return [tuple(outs[3 * gi:3 * gi + 3]) for gi in range(3)]


N_ATTN_STEPS = S // BLK


def _class_and_block(gi, t):
    per_class = N_ATTN_STEPS // PATTERN_DILATIONS[gi]
    return t >> (per_class.bit_length() - 1), t & (per_class - 1)


def _attn_specs(gi):
    def cur(t):
        r, n = _class_and_block(gi, t)
        return n, r

    def prev(t):
        r, n = _class_and_block(gi, t)
        return jnp.maximum(n - 1, 0), r

    def whole(t):
        return 0, _class_and_block(gi, t)[0]

    sub = S // PATTERN_DILATIONS[gi]
    return (pl.BlockSpec((BLK, GW), cur), pl.BlockSpec((BLK, GW), prev), pl.BlockSpec((sub, GW), whole))


def _left_lanes():
    return lax.broadcasted_iota(jnp.int32, (BLK, LANES), 1) < HEAD_DIM


def _stack_heads(t2, left):
    zero = jnp.zeros_like(t2)
    return jnp.concatenate([jnp.where(left, t2, zero), jnp.where(left, zero, t2)], axis=0)


def _attn_fwd_block(n, q_ref, kc_ref, kp_ref, vc_ref, vp_ref, o_ref, l_ref):
    mask = _band_mask(n)
    left = _left_lanes()
    zero = jnp.zeros((BLK, LANES), BF16)
    for pr in range(GW // LANES):
        cs = slice(pr * LANES, (pr + 1) * LANES)
        q2 = q_ref[:, cs]
        k2 = jnp.concatenate([kp_ref[:, cs], kc_ref[:, cs]], axis=0)
        v2 = jnp.concatenate([vp_ref[:, cs], vc_ref[:, cs]], axis=0)
        o_h, lse_h = [], []
        for side in (left, ~left):
            s = jnp.where(mask, _dot_nt(jnp.where(side, q2, zero), k2), NEG)
            m = jnp.max(s, axis=1, keepdims=True)
            p = jnp.exp(s - m)
            l = jnp.sum(p, axis=1, keepdims=True)
            o_h.append(_dot_nn((p / l).astype(BF16), v2))
            lse_h.append(m + jnp.log(l))
        o_ref[:, cs] = jnp.where(left, o_h[0], o_h[1])
        l_ref[:, cs] = jnp.where(left, lse_h[0], lse_h[1])


def _attn_fwd(preps):
    def body(*refs):
        t = pl.program_id(0)
        for gi in range(3):
            _attn_fwd_block(_class_and_block(gi, t)[1], *refs[5 * gi:5 * gi + 5], *refs[15 + 2 * gi:17 + 2 * gi])

    in_specs, out_specs, out_shape, args = [], [], [], []
    for gi, d in enumerate(PATTERN_DILATIONS):
        cur, prev, _ = _attn_specs(gi)
        qr, kr, vv = preps[gi]
        in_specs += [cur, cur, prev, cur, prev]
        args += [qr, kr, kr, vv, vv]
        out_specs += [cur, cur]
        out_shape += [jax.ShapeDtypeStruct((S // d, d * GW), F32)] * 2
    outs = pl.pallas_call(
        body,
        name="attn_fwd",
        grid=(N_ATTN_STEPS,),
        in_specs=in_specs,
        out_specs=out_specs,
        out_shape=out_shape,
        compiler_params=_cp(("arbitrary",), 48),
    )(*args)
    return [outs[0], outs[2], outs[4]], [outs[1], outs[3], outs[5]]


def _attn_bwd_block(n, q_ref, kc_ref, kp_ref, vc_ref, vp_ref, do_ref, l_ref, c_ref, dq_ref, dk_ref, dv_ref):
    mask = _band_mask(n)
    left = _left_lanes()
    left2 = jnp.concatenate([left, left], axis=0)
    here = pl.ds(pl.multiple_of(n * BLK, BLK), BLK)
    before = pl.ds(pl.multiple_of(jnp.maximum(n - 1, 0) * BLK, BLK), BLK)
    for pr in range(GW // LANES):
        cs = slice(pr * LANES, (pr + 1) * LANES)
        q2 = q_ref[:, cs]
        do2 = do_ref[:, cs]
        k_bd = _stack_heads(jnp.concatenate([kp_ref[:, cs], kc_ref[:, cs]], axis=0), left2)
        v_bd = _stack_heads(jnp.concatenate([vp_ref[:, cs], vc_ref[:, cs]], axis=0), left2)
        s2 = _dot_nt(q2, k_bd)
        dp2 = _dot_nt(do2, v_bd)
        ps, dss = [], []
        for h in range(2):
            at = pr * LANES + h * HEAD_DIM
            half = slice(h * 2 * BLK, (h + 1) * 2 * BLK)
            p = jnp.exp(jnp.where(mask, s2[:, half], NEG) - l_ref[:, at:at + 1])
            ps.append(p.astype(BF16))
            dss.append((p * (dp2[:, half] - c_ref[:, at:at + 1])).astype(BF16))
        ds_both = jnp.concatenate(dss, axis=1)
        dq_ref[:, cs] = _dot_nn(ds_both, k_bd) * (HEAD_DIM ** -0.5)
        dk_bd = _dot_tn(ds_both, q2)
        dv_bd = _dot_tn(jnp.concatenate(ps, axis=1), do2)
        dk2 = jnp.where(left2, dk_bd[:2 * BLK], dk_bd[2 * BLK:])
        dv2 = jnp.where(left2, dv_bd[:2 * BLK], dv_bd[2 * BLK:])
        dk_ref[here, cs] = dk2[BLK:]
        dv_ref[here, cs] = dv2[BLK:]
        dk_ref[before, cs] += dk2[:BLK]
        dv_ref[before, cs] += dv2[:BLK]


def _attn_bwd(preps, dos, lses, cterms):
    def body(*refs):
        t = pl.program_id(0)
        for gi in range(3):
            _attn_bwd_block(_class_and_block(gi, t)[1], *refs[8 * gi:8 * gi + 8], *refs[24 + 3 * gi:27 + 3 * gi])

    in_specs, out_specs, out_shape, args = [], [], [], []
    for gi, d in enumerate(PATTERN_DILATIONS):
        cur, prev, whole = _attn_specs(gi)
        qr, kr, vv = preps[gi]
        in_specs += [cur, cur, prev, cur, prev, cur, cur, cur]
        args += [qr, kr, kr, vv, vv, dos[gi], lses[gi], cterms[gi]]
        out_specs += [cur, whole, whole]
        out_shape += [jax.ShapeDtypeStruct((S // d, d * GW), F32)] * 3
    outs = pl.pallas_call(
        body,
        name="attn_bwd",
        grid=(N_ATTN_STEPS,),
        in_specs=in_specs,
        out_specs=out_specs,
        out_shape=out_shape,
        compiler_params=_cp(("arbitrary",), 56),
    )(*args)
    return [tuple(outs[3 * gi:3 * gi + 3]) for gi in range(3)]


def _qkv_unprep(grads, cos_t, sin_t, dproj):
    def body(*refs):
        views, (c_ref, s_ref, _, out_ref), scr = refs[:9], refs[9:13], refs[13:]
        lo = _lane_lo(ROWS)
        cf, sf = _tile4(c_ref[...]), _tile4(s_ref[...])
        for gi, d in enumerate(PATTERN_DILATIONS):
            dq_ref, dk_ref, dv_ref = views[3 * gi:3 * gi + 3]
            qa, ka, va = gi * GW, (3 + gi) * GW, (6 + gi) * GW
            out_ref[:, qa:qa + GW] = _unrope(_from_view(scr, dq_ref, d), cf, sf, lo).astype(BF16)
            out_ref[:, ka:ka + GW] = _unrope(_from_view(scr, dk_ref, d), cf, sf, lo).astype(BF16)
            out_ref[:, va:va + GW] = _from_view(scr, dv_ref, d).astype(BF16)

    tab = pl.BlockSpec((ROWS, BLK), lambda i: (i, 0))
    return pl.pallas_call(
        body,
        name="qkv_unprep",
        grid=(S // ROWS,),
        in_specs=[_view_spec(d) for d in PATTERN_DILATIONS for _ in range(3)] + [tab, tab, pl.BlockSpec(memory_space=pl.ANY)],
        out_specs=pl.BlockSpec((ROWS, QKV_W), lambda i: (i, 0)),
        out_shape=jax.ShapeDtypeStruct(dproj.shape, dproj.dtype),
        input_output_aliases={11: 0},
        scratch_shapes=_VIEW_SCRATCH,
        compiler_params=_cp(("parallel",), 48),
    )(*[g for grp in grads for g in grp], cos_t, sin_t, dproj)


def _group_weights(l0, l1, l2):
    mx = jnp.maximum(jnp.maximum(l0, l1), l2)
    e0, e1, e2 = jnp.exp(l0 - mx), jnp.exp(l1 - mx), jnp.exp(l2 - mx)
    inv = 1.0 / (e0 + e1 + e2)
    return e0 * inv, e1 * inv, e2 * inv


def _combine_fwd(os_, lses):
    def body(o0, o1, o2, l0, l1, l2, y_ref, *scr):
        ov = [_from_view(scr, o, d) for o, d in zip((o0, o1, o2), PATTERN_DILATIONS)]
        lv = [_from_view(scr, l, d) for l, d in zip((l0, l1, l2), PATTERN_DILATIONS)]
        w0, w1, w2 = _group_weights(*lv)
        y_ref[...] = (w0 * ov[0] + w1 * ov[1] + w2 * ov[2]).astype(BF16)

    views = [_view_spec(d) for d in PATTERN_DILATIONS]
    return pl.pallas_call(
        body,
        name="attn_combine_fwd",
        grid=(S // ROWS,),
        in_specs=views + views,
        out_specs=pl.BlockSpec((ROWS, GW), lambda i: (i, 0)),
        out_shape=jax.ShapeDtypeStruct((S, GW), BF16),
        scratch_shapes=_VIEW_SCRATCH,
        compiler_params=_cp(("parallel",)),
    )(*os_, *lses)


def _combine_bwd(dy, os_, lses, seg):
    def body(dy_ref, o0, o1, o2, l0, l1, l2, seg_ref, d0, d1, d2, c0, c1, c2, *scr):
        ov = [_from_view(scr, o, d) for o, d in zip((o0, o1, o2), PATTERN_DILATIONS)]
        lv = [_from_view(scr, l, d) for l, d in zip((l0, l1, l2), PATTERN_DILATIONS)]
        ws = _group_weights(*lv)
        dyv = dy_ref[...]
        t = dyv * (ws[0] * ov[0] + ws[1] * ov[1] + ws[2] * ov[2])
        t_hi = t.astype(BF16)
        r1 = t - t_hi.astype(F32)
        t_mid = r1.astype(BF16)
        t_lo = (r1 - t_mid.astype(F32)).astype(BF16)
        sg = seg_ref[...]
        e = _dot_nn(t_hi, sg) + _dot_nn(t_mid, sg) + _dot_nn(t_lo, sg)
        for w, do_ref, c_ref, d in zip(ws, (d0, d1, d2), (c0, c1, c2), PATTERN_DILATIONS):
            _to_view(scr, w * dyv, do_ref, d, BF16)
            _to_view(scr, w * e, c_ref, d, F32)

    views = [_view_spec(d) for d in PATTERN_DILATIONS]
    return pl.pallas_call(
        body,
        name="attn_combine_bwd",
        grid=(S // ROWS,),
        in_specs=[pl.BlockSpec((ROWS, GW), lambda i: (i, 0))] + views + views + [pl.BlockSpec((GW, GW), lambda i: (0, 0))],
        out_specs=views + views,
        out_shape=[jax.ShapeDtypeStruct((S // d, d * GW), BF16) for d in PATTERN_DILATIONS]
        + [jax.ShapeDtypeStruct((S // d, d * GW), F32) for d in PATTERN_DILATIONS],
        scratch_shapes=_VIEW_SCRATCH,
        compiler_params=_cp(("parallel",)),
    )(dy, *os_, *lses, seg)


_SQRT_HALF = 0.7071067811865476
_INV_SQRT_2PI = 0.3989422804014327


def _gelu(z):
    return 0.5 * z * (1.0 + lax.erf(z * _SQRT_HALF))


def _gelu_grad(z):
    return 0.5 * (1.0 + lax.erf(z * _SQRT_HALF)) + z * (_INV_SQRT_2PI * jnp.exp(-0.5 * z * z))


def _tril_ws(ws_ref, g):
    t = lax.broadcasted_iota(jnp.int32, (BLK, BLK), 0)
    s = lax.broadcasted_iota(jnp.int32, (BLK, BLK), 1)
    return jnp.where(t >= s, ws_ref[g], 0.0)


def _chunks_side_by_side(x, cols, nch):
    return jnp.concatenate([x[c * BLK:(c + 1) * BLK, cols] for c in range(nch)], axis=1)


def _gmlp_fwd(z, ws, bst, ln_g, ln_b, tm=512):
    nch = tm // BLK

    def body(z_ref, ws_ref, b_ref, g_ref, be_ref, y_ref):
        zg = _gelu(z_ref[...].astype(F32))
        u = zg[:, :D]
        vn, _, _ = _ln_fwd(zg[:, D:], g_ref[...], be_ref[...])
        vnb = vn.astype(BF16)
        bt = b_ref[...]
        for g in range(8):
            w = _tril_ws(ws_ref, g).astype(BF16)
            cols = slice(g * BLK, (g + 1) * BLK)
            mixed = _dot_nn(w, _chunks_side_by_side(vnb, cols, nch)) + bt[:, g:g + 1]
            for c in range(nch):
                rows = slice(c * BLK, (c + 1) * BLK)
                y_ref[rows, cols] = (u[rows, cols] * mixed[:, c * BLK:(c + 1) * BLK]).astype(BF16)

    return pl.pallas_call(
        body,
        name="gmlp_fwd",
        grid=(S // tm,),
        in_specs=[
            pl.BlockSpec((tm, 2 * D), lambda i: (i, 0)),
            pl.BlockSpec((8, BLK, BLK), lambda i: (0, 0, 0)),
            pl.BlockSpec((BLK, 8), lambda i: (0, 0)),
            pl.BlockSpec((1, D), lambda i: (0, 0)),
            pl.BlockSpec((1, D), lambda i: (0, 0)),
        ],
        out_specs=pl.BlockSpec((tm, D), lambda i: (i, 0)),
        out_shape=jax.ShapeDtypeStruct((S, D), BF16),
        compiler_params=_cp(("parallel",), 48),
    )(z, ws, bst, ln_g, ln_b)


def _gmlp_bwd(z, dy, ws, bst, ln_g, ln_b, dproj, tm=512):
    nch = tm // BLK

    def body(z_ref, dy_ref, ws_ref, b_ref, g_ref, be_ref, _, dz_ref, dws_ref, dbs_ref, dg_ref, dbe_ref, dvn_scr, dm_acc):
        i = pl.program_id(0)
        zv = z_ref[...].astype(F32)
        zg = _gelu(zv)
        u = zg[:, :D]
        gam = g_ref[...]
        vn, xhat, rstd = _ln_fwd(zg[:, D:], gam, be_ref[...])
        vnb = vn.astype(BF16)
        dyv = dy_ref[...]
        dmix = dyv * u
        dmb = dmix.astype(BF16)
        bt = b_ref[...]
        tmask = lax.broadcasted_iota(jnp.int32, (BLK, BLK), 0) >= lax.broadcasted_iota(jnp.int32, (BLK, BLK), 1)
        dm_sum = dmix[0:BLK]
        for c in range(1, nch):
            dm_sum = dm_sum + dmix[c * BLK:(c + 1) * BLK]

        @pl.when(i == 0)
        def _():
            dm_acc[...] = jnp.zeros_like(dm_acc)
            dws_ref[...] = jnp.zeros_like(dws_ref)
            dg_ref[...] = jnp.zeros_like(dg_ref)
            dbe_ref[...] = jnp.zeros_like(dbe_ref)

        dm_acc[...] += dm_sum
        dus = []
        for g in range(8):
            w = _tril_ws(ws_ref, g).astype(BF16)
            cols = slice(g * BLK, (g + 1) * BLK)
            v_cat = _chunks_side_by_side(vnb, cols, nch)
            dm_cat = _chunks_side_by_side(dmb, cols, nch)
            mixed = _dot_nn(w, v_cat) + bt[:, g:g + 1]
            dus.append(jnp.concatenate(
                [dyv[c * BLK:(c + 1) * BLK, cols] * mixed[:, c * BLK:(c + 1) * BLK] for c in range(nch)], axis=0))
            dws_ref[g] += jnp.where(tmask, _dot_nt(dm_cat, v_cat), 0.0)
            dvn_cat = _dot_tn(w, dm_cat)
            for c in range(nch):
                dvn_scr[c * BLK:(c + 1) * BLK, cols] = dvn_cat[:, c * BLK:(c + 1) * BLK]

        dvn = dvn_scr[...]
        dg_ref[...] += _colsum(dvn * xhat)
        dbe_ref[...] += _colsum(dvn)
        dvg = _ln_bwd(dvn, xhat, rstd, gam)
        gp = _gelu_grad(zv)
        dz_ref[:, :D] = (jnp.concatenate(dus, axis=1) * gp[:, :D]).astype(BF16)
        dz_ref[:, D:] = (dvg * gp[:, D:]).astype(BF16)

        @pl.when(i == S // tm - 1)
        def _():
            acc = dm_acc[...]
            for g in range(8):
                dbs_ref[:, g:g + 1] = jnp.sum(acc[:, g * BLK:(g + 1) * BLK], axis=1, keepdims=True)

    vec = pl.BlockSpec((1, D), lambda i: (0, 0))
    return pl.pallas_call(
        body,
        name="gmlp_bwd",
        grid=(S // tm,),
        in_specs=[
            pl.BlockSpec((tm, 2 * D), lambda i: (i, 0)),
            pl.BlockSpec((tm, D), lambda i: (i, 0)),
            pl.BlockSpec((8, BLK, BLK), lambda i: (0, 0, 0)),
            pl.BlockSpec((BLK, 8), lambda i: (0, 0)),
            vec,
            vec,
            pl.BlockSpec(memory_space=pl.ANY),
        ],
        out_specs=[
            pl.BlockSpec((tm, 2 * D), lambda i: (i, DPROJ_Z_COL)),
            pl.BlockSpec((8, BLK, BLK), lambda i: (0, 0, 0)),
            pl.BlockSpec((BLK, 8), lambda i: (0, 0)),
            vec,
            vec,
        ],
        out_shape=[
            jax.ShapeDtypeStruct(dproj.shape, dproj.dtype),
            jax.ShapeDtypeStruct((8, BLK, BLK), F32),
            jax.ShapeDtypeStruct((BLK, 8), F32),
            jax.ShapeDtypeStruct((1, D), F32),
            jax.ShapeDtypeStruct((1, D), F32),
        ],
        input_output_aliases={6: 0},
        scratch_shapes=[pltpu.VMEM((tm, D), F32), pltpu.VMEM((BLK, D), F32)],
        compiler_params=_cp(("arbitrary",), 48),
    )(z, dy, ws, bst, ln_g, ln_b, dproj)


def _merge_fwd(ya, yg, glog, bgate, h1, wabt, wgb, wo, ln_g, ln_b, tm=256):
    def body(ya_ref, yg_ref, gl_ref, bg_ref, h1_ref, wab_ref, wgb_ref, wo_ref, g_ref, b_ref,
             h_ref, hb_ref, xh_ref, rs_ref, mg_ref, bra_ref, brg_ref):
        bra = _dot_nt(ya_ref[...], wab_ref[...])
        brg = _dot_nn(yg_ref[...], wgb_ref[...])
        gates = _sigmoid(gl_ref[...].astype(F32) + bg_ref[...])
        merged = (gates[:, :D] * bra + gates[:, D:] * brg).astype(BF16)
        mix = _dot_nn(merged, wo_ref[...])
        h, xhat, rstd = _ln_fwd(ALPHA * h1_ref[...] + mix, g_ref[...], b_ref[...])
        h_ref[...] = h
        hb_ref[...] = h.astype(BF16)
        xh_ref[...] = xhat
        rs_ref[...] = rstd
        mg_ref[...] = merged
        bra_ref[...] = bra
        brg_ref[...] = brg

    row = pl.BlockSpec((tm, D), lambda i: (i, 0))
    vec = pl.BlockSpec((1, D), lambda i: (0, 0))
    full = lambda shape: pl.BlockSpec(shape, lambda i: (0, 0))
    return pl.pallas_call(
        body,
        name="merge_fwd",
        grid=(S // tm,),
        in_specs=[
            pl.BlockSpec((tm, GW), lambda i: (i, 0)), row,
            pl.BlockSpec((tm, 2 * D), lambda i: (i, glog.shape[1] // (2 * D) - 1)),
            full((1, 2 * D)), row,
            full((D, GW)), full((D, D)), full((D, D)), vec, vec,
        ],
        out_specs=[row, row, row, pl.BlockSpec((tm, 1), lambda i: (i, 0)), row, row, row],
        out_shape=[
            jax.ShapeDtypeStruct((S, D), F32),
            jax.ShapeDtypeStruct((S, D), BF16),
            jax.ShapeDtypeStruct((S, D), F32),
            jax.ShapeDtypeStruct((S, 1), F32),
            jax.ShapeDtypeStruct((S, D), BF16),
            jax.ShapeDtypeStruct((S, D), F32),
            jax.ShapeDtypeStruct((S, D), F32),
        ],
        compiler_params=_cp(("parallel",), 48),
    )(ya, yg, glog, bgate, h1, wabt, wgb, wo, ln_g, ln_b)


def _merge_bwd(dh2, xhat, rstd, ln_g, bra, brg, glog, bgate, wabt, wgb, wo, tm=256):
    def body(dh_ref, xh_ref, rs_ref, g_ref, bra_ref, brg_ref, gl_ref, bg_ref, wab_ref, wgb_ref, wo_ref,
             dr_ref, drb_ref, dlog_ref, dba_ref, dbg_ref, dya_ref, dyg_ref, dbgate_ref, dg_ref, dbias_ref):
        i = pl.program_id(0)
        dh = dh_ref[...]
        xh = xh_ref[...]
        dr = _ln_bwd(dh, xh, rs_ref[...], g_ref[...])
        drb = dr.astype(BF16)
        dr_ref[...] = dr
        drb_ref[...] = drb
        dmerged = _dot_nt(drb, wo_ref[...])
        gates = _sigmoid(gl_ref[...].astype(F32) + bg_ref[...])
        g0, g1 = gates[:, :D], gates[:, D:]
        dl0 = dmerged * bra_ref[...] * g0 * (1.0 - g0)
        dl1 = dmerged * brg_ref[...] * g1 * (1.0 - g1)
        dlog_ref[:, :D] = dl0.astype(BF16)
        dlog_ref[:, D:] = dl1.astype(BF16)
        dba = (dmerged * g0).astype(BF16)
        dbg = (dmerged * g1).astype(BF16)
        dba_ref[...] = dba
        dbg_ref[...] = dbg
        dya_ref[...] = _dot_nn(dba, wab_ref[...])
        dyg_ref[...] = _dot_nt(dbg, wgb_ref[...])
        s0, s1 = _colsum(dl0), _colsum(dl1)
        sg, sb = _colsum(dh * xh), _colsum(dh)

        @pl.when(i == 0)
        def _():
            dbgate_ref[:, :D] = s0
            dbgate_ref[:, D:] = s1
            dg_ref[...] = sg
            dbias_ref[...] = sb

        @pl.when(i > 0)
        def _():
            dbgate_ref[:, :D] += s0
            dbgate_ref[:, D:] += s1
            dg_ref[...] += sg
            dbias_ref[...] += sb

    row = pl.BlockSpec((tm, D), lambda i: (i, 0))
    vec = pl.BlockSpec((1, D), lambda i: (0, 0))
    wide = pl.BlockSpec((tm, 2 * D), lambda i: (i, 0))
    full = lambda shape: pl.BlockSpec(shape, lambda i: (0, 0))
    return pl.pallas_call(
        body,
        name="merge_bwd",
        grid=(S // tm,),
        in_specs=[row, row, pl.BlockSpec((tm, 1), lambda i: (i, 0)), vec, row, row,
                  pl.BlockSpec((tm, 2 * D), lambda i: (i, glog.shape[1] // (2 * D) - 1)),
                  full((1, 2 * D)), full((D, GW)), full((D, D)), full((D, D))],
        out_specs=[row, row, pl.BlockSpec((tm, 2 * D), lambda i: (i, DPROJ_G_COL)), row, row,
                   pl.BlockSpec((tm, GW), lambda i: (i, 0)), row, full((1, 2 * D)), vec, vec],
        out_shape=[
            jax.ShapeDtypeStruct((S, D), F32),
            jax.ShapeDtypeStruct((S, D), BF16),
            jax.ShapeDtypeStruct((S, DPROJ_W), BF16),
            jax.ShapeDtypeStruct((S, D), BF16),
            jax.ShapeDtypeStruct((S, D), BF16),
            jax.ShapeDtypeStruct((S, GW), F32),
            jax.ShapeDtypeStruct((S, D), F32),
            jax.ShapeDtypeStruct((1, 2 * D), F32),
            jax.ShapeDtypeStruct((1, D), F32),
            jax.ShapeDtypeStruct((1, D), F32),
        ],
        compiler_params=_cp(("arbitrary",), 48),
    )(dh2, xhat, rstd, ln_g, bra, brg, glog, bgate, wabt, wgb, wo)


def _dproj_to_dh1(dproj, win, dep, tm=512):
    n_dep = 0 if dep is None else 1

    def body(a_ref, b_ref, *rest):
        rest[n_dep][...] = (_dot_nn(a_ref[:, :QKV_W], b_ref[:QKV_W, :])
                            + _dot_nn(a_ref[:, DPROJ_ZG_AT:], b_ref[QKV_W:, :]))

    return pl.pallas_call(
        body,
        name="dproj_to_dh1",
        grid=(S // tm,),
        in_specs=[pl.BlockSpec((tm, DPROJ_W), lambda i: (i, 0)),
                  pl.BlockSpec((IN_W, D), lambda i: (0, 0), pipeline_mode=pl.Buffered(1))]
        + [pl.BlockSpec(memory_space=pl.ANY)] * n_dep,
        out_specs=pl.BlockSpec((tm, D), lambda i: (i, 0)),
        out_shape=jax.ShapeDtypeStruct((S, D), F32),
        compiler_params=_cp(("parallel",), 56),
    )(dproj, win, *([] if dep is None else [dep]))


def _tie(x, dep):
    if dep is None:
        return x
    return x + dep[0, 0].astype(x.dtype)


def _local_step(x, pos_col, target, get_w, p, emit):
    w = get_w("ffn1", None)
    a1, b1, hm1 = _ffn_up(x, w["g1"], w["u1"], "ffn1_up")
    w.update(get_w("ffn1d", hm1))
    h1, h1b, xh1, rs1 = _ffn_down(hm1, w["d1"], x, p["ln1_g"], p["ln1_b"], "ffn1_down")

    w.update(get_w("win", h1b))
    qkv = _matmul(h1b, w["win"], "nt", BF16, S, 1536, D, "proj_qkv", b_off=0, n_out=QKV_W)
    z = glog = _matmul(h1b, w["win"], "nt", BF16, S, 512, D, "proj_zg", b_off=QKV_W // 512, n_out=4 * D)

    half = jnp.arange(0, HEAD_DIM, 2, dtype=F32) / HEAD_DIM
    inv_freq = ROPE_THETA ** (-half)
    invf = jnp.tile(inv_freq, 4).reshape(1, BLK)
    sign = jnp.tile(jnp.concatenate([-jnp.ones((32,), F32), jnp.ones((32,), F32)]), 2).reshape(1, BLK)
    cos_t, sin_t = _rope_tables(pos_col, invf, sign)

    preps = _qkv_prep(qkv, cos_t, sin_t)
    os_, lses = _attn_fwd(preps)
    ya = _combine_fwd(os_, lses)
    get_w("late", ya, early=True)
    bst = p["gmlp_b_s"].T
    yg = _gmlp_fwd(z, p["gmlp_w_s"], bst, p["gmlp_ln_g"], p["gmlp_ln_b"])
    w.update(get_w("late", yg))
    h2, h2b, xh2, rs2, merged, bra, brg = _merge_fwd(ya, yg, glog, p["b_gates"], h1, w["ab"], w["gb"], w["o"],
                                                      p["ln2_g"], p["ln2_b"])
    a2, b2, hm2 = _ffn_up(h2b, w["g2"], w["u2"], "ffn2_up")
    dh3, loss, xh3, rs3 = _ffn_down(hm2, w["d2"], h2, p["ln3_g"], p["ln3_b"], "ffn2_down", target=target)

    gp = {}
    dr3, df2, da2, db2, gp["ln3_g"], gp["ln3_b"] = _ffn_bwd_mid(dh3, None, xh3, rs3, p["ln3_g"], a2, b2, w["d2"],
                                                                "ffn2_bwd_mid")
    tok = emit("ffn2", {
        "g2": _matmul(da2, h2b, "tn", BF16, 1408, D, S, "wgrad_g2"),
        "u2": _matmul(db2, h2b, "tn", BF16, 1408, D, S, "wgrad_u2"),
        "d2": _matmul(hm2, df2, "tn", BF16, 1408, D, S, "wgrad_d2")})
    dh2 = _ffn_bwd_dx(dr3, da2, db2, w["g2"], w["u2"], "ffn2_bwd_dx")

    (dr2, dr2b, dproj, dba, dbg, dya, dyg, gp["b_gates"], gp["ln2_g"], gp["ln2_b"]) = _merge_bwd(
        dh2, xh2, rs2, _tie(p["ln2_g"], tok), bra, brg, glog, p["b_gates"], w["ab"], w["gb"], w["o"])
    tok = emit("mix", {
        "o": _matmul(merged, dr2b, "tn", BF16, 512, D, S, "wgrad_o"),
        "ab": _matmul(dba, ya, "tn", BF16, 512, GW, S, "wgrad_ab"),
        "gb": _matmul(yg, dbg, "tn", BF16, 512, D, S, "wgrad_gb")})

    seg = (jnp.arange(GW)[:, None] // HEAD_DIM == jnp.arange(GW)[None, :] // HEAD_DIM).astype(BF16)
    do0, do1, do2, c0, c1, c2 = _combine_bwd(dya, os_, lses, _tie(seg, tok))
    dproj = _qkv_unprep(_attn_bwd(preps, (do0, do1, do2), lses, (c0, c1, c2)), cos_t, sin_t, dproj)
    dproj, gp["gmlp_w_s"], dbst, gp["gmlp_ln_g"], gp["gmlp_ln_b"] = _gmlp_bwd(
        z, dyg, p["gmlp_w_s"], bst, p["gmlp_ln_g"], p["gmlp_ln_b"], dproj)
    gp["gmlp_b_s"] = dbst.T
    tok = emit("win", {"win": _matmul(dproj, h1b, "tn", BF16, 512, D, S, "wgrad_win", a_map=_dproj_tile, m_out=IN_W)})
    dh1m = _dproj_to_dh1(dproj, w["win"], tok)

    dr1, df1, da1, db1, gp["ln1_g"], gp["ln1_b"] = _ffn_bwd_mid(dr2, dh1m, xh1, rs1, p["ln1_g"], a1, b1, w["d1"],
                                                                "ffn1_bwd_mid")
    tok = emit("small", {**gp, "loss": loss})
    tok = emit("g1", {"g1": _matmul(da1, x, "tn", BF16, 1408, D, S, "wgrad_g1", dep=tok)})
    tok = emit("ud1", {"u1": _matmul(db1, x, "tn", BF16, 1408, D, S, "wgrad_u1", dep=tok),
                       "d1": _matmul(hm1, df1, "tn", BF16, 1408, D, S, "wgrad_d1", dep=tok)})
    dx = _ffn_bwd_dx(dr1, da1, db1, w["g1"], w["u1"], "ffn1_bwd_dx", dep=tok)
    return loss, dx, gp


_FLIPS = [(mx, my, mc) for mx in (0, 1) for my in (0, 1) for mc in (0, 1)][1:]
_GROUPS = {"ffn1": ("g1", "u1"), "ffn1d": ("d1",), "win": ("win",), "late": ("ab", "gb", "o", "g2", "u2", "d2")}
_SCATTERS = {"ffn2": ("g2", "u2", "d2"), "mix": ("ab", "gb", "o"), "win": ("win",), "g1": ("g1",), "ud1": ("u1", "d1")}
_TWO_STAGE = ("g1", "ud1")
_HBM = pl.BlockSpec(memory_space=pltpu.HBM)
_SEM = pl.BlockSpec(memory_space=pltpu.SEMAPHORE)
_EFFECT = pltpu.SideEffectType.DATAFLOW_SIDE_EFFECTING


def _me():
    return 4 * lax.axis_index("x") + 2 * lax.axis_index("y") + lax.axis_index("c")


def _copies_start(bufs, copies, n_sem, name, after=None):
    nb = len(bufs)
    n_after = 0 if after is None else 1

    def body(*refs):
        b = refs[:nb]
        send_sems, recv_sems = refs[nb + n_after], refs[nb + n_after + 1]
        token = refs[-1]
        x, y, c = lax.axis_index("x"), lax.axis_index("y"), lax.axis_index("c")
        me = 4 * x + 2 * y + c
        for si, s_slot, di, d_slot, flip, sem in copies:
            s_idx, d_idx = s_slot(me), d_slot(me)
            mx, my, mc = flip(c) if callable(flip) else flip
            pltpu.make_async_remote_copy(
                src_ref=b[si] if s_idx is None else b[si].at[s_idx],
                dst_ref=b[di] if d_idx is None else b[di].at[d_idx],
                send_sem=send_sems.at[sem], recv_sem=recv_sems.at[sem],
                device_id=(x ^ mx, y ^ my, c ^ mc), device_id_type=MESH).start()
        token[...] = jnp.zeros_like(token)

    ins = [pltpu.with_memory_space_constraint(a, pltpu.HBM) for a in bufs]
    outs = pl.pallas_call(
        body,
        name=name,
        in_specs=[_HBM] * nb + [pl.BlockSpec(memory_space=pl.ANY)] * n_after,
        out_specs=[_SEM, _SEM] + [_HBM] * nb + [pl.BlockSpec(memory_space=pltpu.VMEM)],
        out_shape=[pltpu.SemaphoreType.DMA((n_sem,)), pltpu.SemaphoreType.DMA((n_sem,))]
        + [pltpu.HBM(a.shape, a.dtype) for a in ins] + [jax.ShapeDtypeStruct((8, 128), F32)],
        input_output_aliases={k: 2 + k for k in range(nb)},
        compiler_params=pltpu.CompilerParams(has_side_effects=_EFFECT),
    )(*ins, *([] if after is None else [after]))
    return outs[0], outs[1], list(outs[2:2 + nb]), outs[-1]


def _copies_wait(started, waits, after, name):
    send_sems, recv_sems, bufs, _ = started
    nb = len(bufs)

    def body(*refs):
        b = refs[:nb]
        ss, rs = refs[nb], refs[nb + 1]
        me3 = (lax.axis_index("x"), lax.axis_index("y"), lax.axis_index("c"))
        for bi, n_blocks, sem, is_send in waits:
            blocks = b[bi].at[pl.ds(0, n_blocks)]
            cp = pltpu.make_async_remote_copy(src_ref=blocks, dst_ref=blocks, send_sem=ss.at[sem], recv_sem=rs.at[sem],
                                              device_id=me3, device_id_type=MESH)
            if is_send:
                cp.wait_send()
            else:
                cp.wait_recv()

    return pl.pallas_call(
        body,
        name=name,
        in_specs=[_HBM] * nb + [_SEM, _SEM, pl.BlockSpec(memory_space=pl.ANY)],
        out_specs=[_HBM] * nb,
        out_shape=[pltpu.HBM(a.shape, a.dtype) for a in bufs],
        input_output_aliases={k: k for k in range(nb)},
        compiler_params=pltpu.CompilerParams(has_side_effects=_EFFECT),
    )(*bufs, send_sems, recv_sems, after)


def _landing(own, me):
    return lax.dynamic_update_slice(lax.empty((N_DEV,) + own.shape[1:], own.dtype), own, (me, 0, 0))


_SIBLING = (0, 0, 1)
_OTHER_CHIPS = ((1, 0, 0), (0, 1, 0), (1, 1, 0))


def _bits(flip):
    return 4 * flip[0] + 2 * flip[1] + flip[2]


_X_NBR, _Y_NBR = (1, 0, 0), (0, 1, 0)


def _xor(bits):
    return functools.partial(lambda me, bits: me ^ bits, bits=bits)


def _gather_send(shards, grp, after):
    nw = len(shards)
    me = _me()
    bufs = list(shards) + [_landing(a[None], me) for a in shards]
    copies = []
    for k in range(nw):
        for s, flip in enumerate((_SIBLING, _X_NBR, _Y_NBR)):
            copies.append((k, lambda me: None, nw + k, lambda me: me, flip, 3 * k + s))
    return _copies_start(bufs, copies, 3 * nw, "gather_send_" + grp, after)


def _gather_pass(started, grp, after):
    nw = len(started[2]) // 2
    waits = [(nw + k, 1, 3 * k + s, is_send) for k in range(nw) for s in range(3) for is_send in (True, False)]
    lands = list(_copies_wait(started, waits, after, "gather_arrived_" + grp)[nw:])
    copies = []
    for k in range(nw):
        held = lambda me: me ^ (4 - 2 * (me % 2))
        copies.append((k, held, k, held, lambda c: (c, 1 - c, 0), 3 * k))
        copies.append((k, _xor(4), k, _xor(4), _SIBLING, 3 * k + 1))
        copies.append((k, _xor(2), k, _xor(2), _SIBLING, 3 * k + 2))
    return _copies_start(lands, copies, 3 * nw, "gather_pass_" + grp)


def _gather_finish(passed, grp, after):
    nw = len(passed[2])
    waits = [(k, 1, 3 * k + s, is_send) for k in range(nw) for s in range(3) for is_send in (True, False)]
    lands = list(_copies_wait(passed, waits, passed[3] if after is None else after, "gather_forwarded_" + grp))
    copies = [(k, _xor(6), k, _xor(6), _SIBLING, k) for k in range(nw)]
    last = _copies_start(lands, copies, nw, "gather_pass2_" + grp)
    waits = [(k, 1, k, is_send) for k in range(nw) for is_send in (True, False)]
    return _copies_wait(last, waits, last[3], "gather_done_" + grp)


def _small_send(block, after=None):
    me = _me()
    copies = [(0, lambda me: None, 1, lambda me: me, _SIBLING, 1)]
    copies += [(0, lambda me: None, 1, lambda me: me, f, 0) for f in _OTHER_CHIPS]
    return _copies_start([block, _landing(block[None], me)], copies, 2, "small_send", after)


def _small_pass(started, after):
    waits = [(1, 3, 0, True), (1, 1, 1, True), (1, 3, 0, False), (1, 1, 1, False)]
    land = _copies_wait(started, waits, after, "small_arrived")[1]
    copies = [(0, _xor(_bits(f)), 0, _xor(_bits(f)), _SIBLING, 0) for f in _OTHER_CHIPS]
    return _copies_start([land], copies, 1, "small_pass")


def _small_finish(passed, after):
    return _copies_wait(passed, [(0, 3, 0, True), (0, 3, 0, False)], after, "small_done")[0]


def _scatter_send(parts, grp, after=None):
    nw = len(parts)
    me = _me()
    bufs = list(parts) + [_landing(lax.dynamic_index_in_dim(a, me, 0, keepdims=True), me) for a in parts]
    copies = []
    for k in range(nw):
        for f in _FLIPS:
            to = functools.partial(lambda me, bits: me ^ bits, bits=_bits(f))
            copies.append((k, to, nw + k, lambda me: me, f, k))
    return _copies_start(bufs, copies, nw, "scatter_send_" + grp, after)


def _scatter_finish(started, grp, after):
    nw = len(started[2]) // 2
    waits = [(nw + k, N_DEV - 1, k, is_send) for k in range(nw) for is_send in (True, False)]
    return _copies_wait(started, waits, after, "scatter_done_" + grp)[nw:]


N_CHIP = N_DEV // 2


def _pair_sum(part, other, key):
    _, r, c = part.shape
    tm = r
    core = lax.axis_index("c").astype(jnp.int32).reshape(1)

    def body(core_ref, a_ref, b_ref, o_ref):
        o_ref[...] = (a_ref[...].astype(F32) + b_ref[...].astype(F32)).astype(BF16)

    return pl.pallas_call(
        body,
        name="pair_sum_" + key,
        grid_spec=pltpu.PrefetchScalarGridSpec(
            num_scalar_prefetch=1,
            grid=(N_CHIP, r // tm),
            in_specs=[pl.BlockSpec((None, tm, c), lambda q, i, core_ref: (2 * q + core_ref[0], i, 0)),
                      pl.BlockSpec((None, tm, c), lambda q, i, core_ref: (q, i, 0))],
            out_specs=pl.BlockSpec((None, tm, c), lambda q, i, core_ref: (q, i, 0)),
        ),
        out_shape=jax.ShapeDtypeStruct((N_CHIP, r, c), BF16),
        compiler_params=_cp(("parallel", "parallel")),
    )(core, part, other)


def _scatter2_send(parts, grp, keys):
    nw = len(parts)
    chip = _me() // 2
    swap = []
    for k in range(nw):
        for q in range(N_CHIP):
            src = functools.partial(lambda me, q: 2 * q + 1 - me % 2, q=q)
            swap.append((k, src, nw + k, functools.partial(lambda me, q: q, q=q), _SIBLING, k))
    others = [lax.empty((N_CHIP,) + a.shape[1:], a.dtype) for a in parts]
    started = _copies_start(list(parts) + others, swap, nw, "scatter_swap_" + grp)
    waits = [(nw + k, N_CHIP, k, is_send) for k in range(nw) for is_send in (True, False)]
    swapped = _copies_wait(started, waits, started[3], "scatter_swapped_" + grp)
    pairs = [_pair_sum(swapped[k], swapped[nw + k], keys[k]) for k in range(nw)]
    lands = [lax.dynamic_update_slice(lax.empty(p.shape, p.dtype), lax.dynamic_index_in_dim(p, chip, 0, keepdims=True),
                                      (chip, 0, 0)) for p in pairs]
    copies = []
    for k in range(nw):
        for f in _OTHER_CHIPS:
            to = functools.partial(lambda me, bits: (me ^ bits) // 2, bits=_bits(f))
            copies.append((k, to, nw + k, lambda me: me // 2, f, k))
    return _copies_start(pairs + lands, copies, nw, "scatter_send_" + grp)


def _scatter2_finish(started, grp, after):
    nw = len(started[2]) // 2
    waits = [(nw + k, N_CHIP - 1, k, is_send) for k in range(nw) for is_send in (True, False)]
    return _copies_wait(started, waits, after, "scatter_done_" + grp)[nw:]


_ADAM_ROWS = {352: 176, 1088: 272}


def _sum_adamw(parts, wv, m, v, name, dep=None):
    n_parts, r, c = parts.shape
    tm = _ADAM_ROWS.get(r, r)
    assert r % tm == 0 and wv.shape == (r, c)
    n_dep = 0 if dep is None else 1

    def body(p_ref, w_ref, m_ref, v_ref, *rest):
        g_ref, d_ref, mo_ref, vo_ref = rest[n_dep:]
        gv = p_ref[0].astype(F32)
        for j in range(1, n_parts):
            gv = gv + p_ref[j].astype(F32)
        g_ref[...] = gv
        mn = ADAM_B1 * m_ref[...] + (1.0 - ADAM_B1) * gv
        vn = ADAM_B2 * v_ref[...] + (1.0 - ADAM_B2) * (gv * gv)
        m_hat = mn / (1.0 - ADAM_B1 ** ADAM_STEP)
        v_hat = vn / (1.0 - ADAM_B2 ** ADAM_STEP)
        d_ref[...] = -ADAM_LR * (m_hat / (jnp.sqrt(v_hat) + ADAM_EPS) + ADAM_WD * w_ref[...])
        mo_ref[...] = mn
        vo_ref[...] = vn

    sp = pl.BlockSpec((tm, c), lambda i: (i, 0))
    return pl.pallas_call(
        body,
        name=name,
        grid=(r // tm,),
        in_specs=[pl.BlockSpec((n_parts, tm, c), lambda i: (0, i, 0))] + [sp] * 3 + [pl.BlockSpec(memory_space=pl.ANY)] * n_dep,
        out_specs=[sp] * 4,
        out_shape=[jax.ShapeDtypeStruct((r, c), F32)] * 4,
        compiler_params=_cp(("parallel",), 48),
    )(parts, wv, m, v, *([] if dep is None else [dep]))


_WEIGHTS = ["ffn1_w_gate", "ffn1_w_up", "ffn1_w_down", "ln1_g", "ln1_b", "w_in", "b_gates", "gmlp_ln_g", "gmlp_ln_b",
            "gmlp_w_s", "gmlp_b_s", "w_attn_branch", "w_gmlp_branch", "w_out", "ln2_g", "ln2_b", "ffn2_w_gate",
            "ffn2_w_up", "ffn2_w_down", "ln3_g", "ln3_b"]
_BIG_OF = {"ffn1_w_gate": ("g1", True), "ffn1_w_up": ("u1", True), "ffn1_w_down": ("d1", False), "w_in": ("win", True),
           "w_attn_branch": ("ab", True), "w_gmlp_branch": ("gb", False), "w_out": ("o", False),
           "ffn2_w_gate": ("g2", True), "ffn2_w_up": ("u2", True), "ffn2_w_down": ("d2", False)}
_SMALL = [n for n in _WEIGHTS if n not in _BIG_OF]
_SMALL_ROWS = {"gmlp_w_s": 128, "b_gates": 2}
_SMALL_SLOT = 8


def _pack_small(d, last=None):
    rows = []
    for n in _SMALL:
        r = d[n].reshape(-1, D)
        slot = max(r.shape[0], _SMALL_SLOT)
        rows.append(jnp.pad(r, ((0, slot - r.shape[0]), (0, 0))))
    rows.append(jnp.zeros((_SMALL_SLOT, D), F32) if last is None else jnp.broadcast_to(last.reshape(1, 1), (_SMALL_SLOT, D)))
    return jnp.concatenate(rows, axis=0)


def _unpack_small(packed, shapes):
    out, at = {}, 0
    for n in _SMALL:
        k = _SMALL_ROWS.get(n, 1)
        out[n] = packed[at:at + k].reshape(shapes[n])
        at += max(k, _SMALL_SLOT)
    return out


def kernel(x, positions, ffn1_w_gate, ffn1_w_up, ffn1_w_down, ln1_g, ln1_b, w_in, b_gates, gmlp_ln_g, gmlp_ln_b, gmlp_w_s, gmlp_b_s, w_attn_branch, w_gmlp_branch, w_out, ln2_g, ln2_b, ffn2_w_gate, ffn2_w_up, ffn2_w_down, ln3_g, ln3_b, loss_target, m_ffn1_w_gate, m_ffn1_w_up, m_ffn1_w_down, m_ln1_g, m_ln1_b, m_w_in, m_b_gates, m_gmlp_ln_g, m_gmlp_ln_b, m_gmlp_w_s, m_gmlp_b_s, m_w_attn_branch, m_w_gmlp_branch, m_w_out, m_ln2_g, m_ln2_b, m_ffn2_w_gate, m_ffn2_w_up, m_ffn2_w_down, m_ln3_g, m_ln3_b, v_ffn1_w_gate, v_ffn1_w_up, v_ffn1_w_down, v_ln1_g, v_ln1_b, v_w_in, v_b_gates, v_gmlp_ln_g, v_gmlp_ln_b, v_gmlp_w_s, v_gmlp_b_s, v_w_attn_branch, v_w_gmlp_branch, v_w_out, v_ln2_g, v_ln2_b, v_ffn2_w_gate, v_ffn2_w_up, v_ffn2_w_down, v_ln3_g, v_ln3_b):
    args = dict(locals())
    wts = {n: args[n] for n in _WEIGHTS}
    ms = {n: args["m_" + n] for n in _WEIGHTS}
    vs = {n: args["v_" + n] for n in _WEIGHTS}

    name_of = {key: (n, tr) for n, (key, tr) in _BIG_OF.items()}

    shards = {}
    for grp, keys in _GROUPS.items():
        shards[grp] = []
        for key in keys:
            n, tr = name_of[key]
            s2 = wts[n][0]
            shards[grp].append((s2.T if tr else s2).astype(BF16))
    started, passed = {}, {}

    def send(grp, after):
        started[grp] = _gather_send(shards[grp], grp, after)
        return started[grp][3]

    def pass_on(grp, after):
        passed[grp] = _gather_pass(started[grp], grp, after)
        return passed[grp][3]

    def finish(grp, after):
        lands = _gather_finish(passed[grp], grp, after)
        return {key: g.reshape(-1, g.shape[-1]) for key, g in zip(_GROUPS[grp], lands)}

    def get_w(grp, after, early=False):
        if grp == "ffn1":
            return finish("ffn1", send("win", send("ffn1d", pass_on("ffn1", send("ffn1", None)))))
        if grp == "ffn1d":
            return finish("ffn1d", send("late", pass_on("win", pass_on("ffn1d", after))))
        if early:
            pass_on(grp, after)
            return None
        return finish(grp, after)

    sent = {}

    def emit(grp, grads):
        if grp == "small":
            sent[grp] = _small_send(_pack_small(grads, last=grads["loss"]))
        else:
            parts = [grads[key].reshape(N_DEV, -1, grads[key].shape[-1]) for key in _SCATTERS[grp]]
            sent[grp] = (_scatter2_send(parts, grp, _SCATTERS[grp]) if grp in _TWO_STAGE else _scatter_send(parts, grp))
        return sent[grp][3]

    p = {n: (wts[n][0] if n in ("gmlp_w_s", "gmlp_b_s") else wts[n]) for n in _SMALL}
    loss, dx, gp = _local_step(x[0], positions.reshape(S, 1), loss_target[0], get_w, p, emit)
    grads, deltas, new_m, new_v = {}, {}, {}, {}
    small_passed = _small_pass(sent["small"], dx)
    after = small_passed[3]
    for grp in ("ffn2", "mix", "win", "small", "g1", "ud1"):
        if grp == "small":
            parts = _small_finish(small_passed, after)
            outs = _sum_adamw(parts, *[_pack_small({n: d[n] for n in _SMALL}) for d in (wts, ms, vs)], "update_small")
            shapes = {n: wts[n].shape for n in _SMALL}
            for dst, packed in zip((grads, deltas, new_m, new_v), outs):
                dst.update(_unpack_small(packed, shapes))
            loss = outs[0][-_SMALL_SLOT, 0]
            after = outs[1]
            continue
        arrived = (_scatter2_finish if grp in _TWO_STAGE else _scatter_finish)(sent[grp], grp, after)
        for key, part in zip(_SCATTERS[grp], arrived):
            n, tr = name_of[key]
            outs = _sum_adamw(part, *[(d[n][0].T if tr else d[n][0]) for d in (wts, ms, vs)], "update_" + key, dep=after)
            for dst, o in zip((grads, deltas, new_m, new_v), outs):
                dst[n] = (o.T if tr else o)[None]
            after = outs[1]

    return (loss, dx[None], *[grads[n] for n in _WEIGHTS], *[deltas[n] for n in _WEIGHTS],
            *[new_m[n] for n in _WEIGHTS], *[new_v[n] for n in _WEIGHTS])
```

```python
import functools
import math

import jax
import jax.numpy as jnp
from jax import lax
from jax.experimental import pallas as pl
from jax.experimental.pallas import tpu as pltpu

F32 = jnp.float32
BF16 = jnp.bfloat16

N_DEV = 8
D = 1024
S = 2048
F = 2816
HEAD_DIM = 64
HEADS = 8
GW = HEADS * HEAD_DIM
PATTERN_DILATIONS = (1, 4, 16)
BLK = 128
QKV_W = 3 * 3 * GW
IN_W = QKV_W + 2 * D + 2 * D
DPROJ_W = 5 * 2 * D
DPROJ_Z_COL, DPROJ_G_COL = 3, 4
DPROJ_ZG_AT = DPROJ_Z_COL * 2 * D


def _dproj_tile(t):
    return jnp.where(t < QKV_W // 512, t, t + (DPROJ_ZG_AT - QKV_W) // 512)
ROPE_THETA = 10000.0
ALPHA = 2.0 ** 0.25
LN_EPS = 1e-5
ADAM_LR, ADAM_B1, ADAM_B2, ADAM_EPS, ADAM_WD, ADAM_STEP = 0.001, 0.9, 0.999, 1e-08, 0.01, 10
NEG = -1e30
MESH = pl.DeviceIdType.MESH


def _cp(sem=None, vmem_mb=None):
    kw = {}
    if sem is not None:
        kw["dimension_semantics"] = sem
    if vmem_mb is not None:
        kw["vmem_limit_bytes"] = vmem_mb << 20
    return pltpu.CompilerParams(**kw)


def _dot_nn(a, b):
    return lax.dot_general(a, b, (((1,), (0,)), ((), ())), preferred_element_type=F32)


def _dot_nt(a, b):
    return lax.dot_general(a, b, (((1,), (1,)), ((), ())), preferred_element_type=F32)


def _dot_tn(a, b):
    return lax.dot_general(a, b, (((0,), (0,)), ((), ())), preferred_element_type=F32)


def _ln_fwd(r, g, b):
    mu = jnp.mean(r, axis=-1, keepdims=True)
    xc = r - mu
    var = jnp.mean(xc * xc, axis=-1, keepdims=True)
    rstd = lax.rsqrt(var + LN_EPS)
    xhat = xc * rstd
    return xhat * g + b, xhat, rstd


def _ln_bwd(dh, xhat, rstd, g):
    dxh = dh * g
    m1 = jnp.mean(dxh, axis=-1, keepdims=True)
    m2 = jnp.mean(dxh * xhat, axis=-1, keepdims=True)
    return rstd * (dxh - m1 - xhat * m2)


def _sigmoid(x):
    return 0.5 * jnp.tanh(0.5 * x) + 0.5


def _colsum(x):
    return jnp.sum(x, axis=0, keepdims=True)


def _matmul(a, b, mode, out_dtype, tm, tn, tk, name, b_off=0, n_out=None, dep=None, a_map=None, m_out=None):
    n_dep = 0 if dep is None else 1
    a_map = a_map or (lambda t: t)
    if mode == "nn":
        m, k = a.shape[0], b.shape[0]
        n = b.shape[1]
    elif mode == "nt":
        m, k = a.shape
        n = n_out if n_out is not None else b.shape[0]
    else:
        k, m = a.shape[0], m_out or a.shape[1]
        n = b.shape[1]
    nk = k // tk
    assert m % tm == 0 and n % tn == 0 and k % tk == 0
    dot = {"nn": _dot_nn, "nt": _dot_nt, "tn": _dot_tn}[mode]

    def body(a_ref, b_ref, *rest):
        o_ref, scr = rest[n_dep], rest[n_dep + 1:]
        r = dot(a_ref[...].astype(BF16), b_ref[...].astype(BF16))
        if nk == 1:
            o_ref[...] = r.astype(out_dtype)
        else:
            acc = scr[0]
            kk = pl.program_id(2)

            @pl.when(kk == 0)
            def _():
                acc[...] = r

            @pl.when(kk > 0)
            def _():
                acc[...] += r

            @pl.when(kk == nk - 1)
            def _():
                o_ref[...] = acc[...].astype(out_dtype)

    if mode == "nn":
        a_spec = pl.BlockSpec((tm, tk), lambda i, j, kk: (i, a_map(kk)))
        b_spec = pl.BlockSpec((tk, tn), lambda i, j, kk: (kk, j))
    elif mode == "nt":
        a_spec = pl.BlockSpec((tm, tk), lambda i, j, kk: (i, kk))
        b_spec = pl.BlockSpec((tn, tk), lambda i, j, kk: (j + b_off, kk))
    else:
        a_spec = pl.BlockSpec((tk, tm), lambda i, j, kk: (kk, a_map(i)))
        b_spec = pl.BlockSpec((tk, tn), lambda i, j, kk: (kk, j))
    return pl.pallas_call(
        body,
        name=name,
        grid=(m // tm, n // tn, nk),
        in_specs=[a_spec, b_spec] + [pl.BlockSpec(memory_space=pl.ANY)] * n_dep,
        out_specs=pl.BlockSpec((tm, tn), lambda i, j, kk: (i, j)),
        out_shape=jax.ShapeDtypeStruct((m, n), out_dtype),
        scratch_shapes=[] if nk == 1 else [pltpu.VMEM((tm, tn), F32)],
        compiler_params=_cp(("parallel", "parallel", "arbitrary"), 56),
    )(a, b, *([] if dep is None else [dep]))


def _ffn_up(x, wgt, wut, name, tm=256, tn=F, dep=None):
    n_dep = 0 if dep is None else 1

    def body(x_ref, wg_ref, wu_ref, *rest):
        ga_ref, gb_ref, hm_ref = rest[n_dep:]
        xb = x_ref[...].astype(BF16)
        a = _dot_nt(xb, wg_ref[...])
        b = _dot_nt(xb, wu_ref[...])
        sig = _sigmoid(a)
        silu = a * sig
        ga_ref[...] = (b * (sig + silu * (1.0 - sig))).astype(BF16)
        gb_ref[...] = silu.astype(BF16)
        hm_ref[...] = (silu * b).astype(BF16)

    wsp = pl.BlockSpec((tn, D), lambda i, j: (j, 0))
    mid = pl.BlockSpec((tm, tn), lambda i, j: (i, j))
    return pl.pallas_call(
        body,
        name=name,
        grid=(S // tm, F // tn),
        in_specs=[pl.BlockSpec((tm, D), lambda i, j: (i, 0)), wsp, wsp] + [pl.BlockSpec(memory_space=pl.ANY)] * n_dep,
        out_specs=[mid, mid, mid],
        out_shape=[jax.ShapeDtypeStruct((S, F), BF16)] * 3,
        compiler_params=_cp(("parallel", "arbitrary"), 56),
    )(x, wgt, wut, *([] if dep is None else [dep]))


def _ffn_down(hm, wd, x, ln_g, ln_b, name, target=None, tm=512):
    head = target is not None

    def body(hm_ref, wd_ref, x_ref, g_ref, b_ref, *rest):
        t_ref = rest[0] if head else None
        o1_ref, o2_ref, xh_ref, rs_ref = rest[1 if head else 0:]
        r = ALPHA * x_ref[...] + 0.5 * _dot_nn(hm_ref[...], wd_ref[...])
        h, xhat, rstd = _ln_fwd(r, g_ref[...], b_ref[...])
        xh_ref[...] = xhat
        rs_ref[...] = rstd
        if head:
            e = h - t_ref[...]
            o1_ref[...] = e * (1.0 / D)
            part = jnp.sum(_colsum(e * e), axis=1, keepdims=True) * (0.5 / D)

            @pl.when(pl.program_id(0) == 0)
            def _():
                o2_ref[...] = jnp.zeros_like(o2_ref)

            o2_ref[...] += part
        else:
            o1_ref[...] = h
            o2_ref[...] = h.astype(BF16)

    row = pl.BlockSpec((tm, D), lambda i: (i, 0))
    vec = pl.BlockSpec((1, D), lambda i: (0, 0))
    second = (pl.BlockSpec((1, 1), lambda i: (0, 0)), jax.ShapeDtypeStruct((1, 1), F32)) if head else (
        row, jax.ShapeDtypeStruct((S, D), BF16))
    return pl.pallas_call(
        body,
        name=name,
        grid=(S // tm,),
        in_specs=[pl.BlockSpec((tm, F), lambda i: (i, 0)), pl.BlockSpec((F, D), lambda i: (0, 0)), row, vec, vec]
        + ([row] if head else []),
        out_specs=[row, second[0], row, pl.BlockSpec((tm, 1), lambda i: (i, 0))],
        out_shape=[
            jax.ShapeDtypeStruct((S, D), F32),
            second[1],
            jax.ShapeDtypeStruct((S, D), F32),
            jax.ShapeDtypeStruct((S, 1), F32),
        ],
        compiler_params=_cp(("arbitrary",), 56),
    )(hm, wd, x, ln_g, ln_b, *([target] if head else []))


def _ffn_bwd_mid(dh_a, dh_b, xhat, rstd, ln_g, a, b, wd, name, tm=256, tn=F):
    two = dh_b is not None

    def body(*refs):
        dha_ref = refs[0]
        dhb_ref = refs[1] if two else None
        (xh_ref, rs_ref, g_ref, a_ref, b_ref, wd_ref,
         dr_ref, df_ref, da_ref, db_ref, dg_ref, dbias_ref, df_scr) = refs[2 if two else 1:]
        i = pl.program_id(0)
        j = pl.program_id(1)

        @pl.when(j == 0)
        def _():
            dh = dha_ref[...]
            if two:
                dh = ALPHA * dh + dhb_ref[...]
            xhat = xh_ref[...]
            dr = _ln_bwd(dh, xhat, rs_ref[...], g_ref[...])
            dfb = (0.5 * dr).astype(BF16)
            dr_ref[...] = dr
            df_scr[...] = dfb
            df_ref[...] = dfb
            sg = _colsum(dh * xhat)
            sb = _colsum(dh)

            @pl.when(i == 0)
            def _():
                dg_ref[...] = sg
                dbias_ref[...] = sb

            @pl.when(i > 0)
            def _():
                dg_ref[...] += sg
                dbias_ref[...] += sb

        dhm = _dot_nt(df_scr[...], wd_ref[...])
        da_ref[...] = (dhm * a_ref[...].astype(F32)).astype(BF16)
        db_ref[...] = (dhm * b_ref[...].astype(F32)).astype(BF16)

    row = pl.BlockSpec((tm, D), lambda i, j: (i, 0))
    vec = pl.BlockSpec((1, D), lambda i, j: (0, 0))
    mid = pl.BlockSpec((tm, tn), lambda i, j: (i, j))
    ins = [dh_a] + ([dh_b] if two else []) + [xhat, rstd, ln_g, a, b, wd]
    in_specs = [row] * (2 if two else 1) + [row, pl.BlockSpec((tm, 1), lambda i, j: (i, 0)), vec, mid, mid,
                                            pl.BlockSpec((tn, D), lambda i, j: (j, 0))]
    return pl.pallas_call(
        body,
        name=name,
        grid=(S // tm, F // tn),
        in_specs=in_specs,
        out_specs=[row, row, mid, mid, vec, vec],
        out_shape=[
            jax.ShapeDtypeStruct((S, D), F32),
            jax.ShapeDtypeStruct((S, D), BF16),
            jax.ShapeDtypeStruct((S, F), BF16),
            jax.ShapeDtypeStruct((S, F), BF16),
            jax.ShapeDtypeStruct((1, D), F32),
            jax.ShapeDtypeStruct((1, D), F32),
        ],
        scratch_shapes=[pltpu.VMEM((tm, D), BF16)],
        compiler_params=_cp(("arbitrary", "arbitrary"), 56),
    )(*ins)


def _ffn_bwd_dx(dr, da, db, wgt, wut, name, tm=256, tk=F, dep=None):
    nk = F // tk
    n_dep = 0 if dep is None else 1

    def body(dr_ref, da_ref, db_ref, wg_ref, wu_ref, *rest):
        dx_ref, acc = rest[n_dep], rest[n_dep + 1]
        kk = pl.program_id(1)
        part = _dot_nn(da_ref[...], wg_ref[...]) + _dot_nn(db_ref[...], wu_ref[...])

        @pl.when(kk == 0)
        def _():
            acc[...] = ALPHA * dr_ref[...] + part

        @pl.when(kk > 0)
        def _():
            acc[...] += part

        @pl.when(kk == nk - 1)
        def _():
            dx_ref[...] = acc[...]

    row = pl.BlockSpec((tm, D), lambda i, kk: (i, 0))
    mid = pl.BlockSpec((tm, tk), lambda i, kk: (i, kk))
    wsp = pl.BlockSpec((tk, D), lambda i, kk: (kk, 0))
    return pl.pallas_call(
        body,
        name=name,
        grid=(S // tm, nk),
        in_specs=[row, mid, mid, wsp, wsp] + [pl.BlockSpec(memory_space=pl.ANY)] * n_dep,
        out_specs=row,
        out_shape=jax.ShapeDtypeStruct((S, D), F32),
        scratch_shapes=[pltpu.VMEM((tm, D), F32)],
        compiler_params=_cp(("parallel", "arbitrary"), 56),
    )(dr, da, db, wgt, wut, *([] if dep is None else [dep]))


def _rope_tables(pos_col, invf, sign, tm=512):
    def body(p_ref, f_ref, s_ref, c_out, s_out):
        ang = p_ref[...].astype(F32) * f_ref[...]
        c_out[...] = jnp.cos(ang)
        s_out[...] = jnp.sin(ang) * s_ref[...]

    vec = pl.BlockSpec((1, BLK), lambda i: (0, 0))
    out = pl.BlockSpec((tm, BLK), lambda i: (i, 0))
    return pl.pallas_call(
        body,
        name="rope_tables",
        grid=(S // tm,),
        in_specs=[pl.BlockSpec((tm, 1), lambda i: (i, 0)), vec, vec],
        out_specs=[out, out],
        out_shape=[jax.ShapeDtypeStruct((S, BLK), F32)] * 2,
        compiler_params=_cp(("parallel",)),
    )(pos_col, invf, sign)


def _lane_lo(rows=BLK):
    return (lax.broadcasted_iota(jnp.int32, (rows, GW), 1) % HEAD_DIM) < (HEAD_DIM // 2)


def _swap_halves(t, lo):
    return jnp.where(lo, pltpu.roll(t, GW - HEAD_DIM // 2, 1), pltpu.roll(t, HEAD_DIM // 2, 1))


def _rope(t, cosf, sinf, lo):
    return t * cosf + _swap_halves(t, lo) * sinf


def _unrope(g, cosf, sinf, lo):
    return g * cosf + _swap_halves(g * sinf, lo)


def _tile4(v):
    return jnp.concatenate([v, v, v, v], axis=1)


def _band_mask(n):
    qi = lax.broadcasted_iota(jnp.int32, (BLK, 2 * BLK), 0)
    kj = lax.broadcasted_iota(jnp.int32, (BLK, 2 * BLK), 1)
    dist = qi + BLK - kj
    return (dist >= 0) & (dist <= BLK) & ((kj >= BLK) | (n >= 1))


ROWS = 256
LANES = 128


def _to_view(scr, y, dst_ref, d, dtype, col0=0):
    if d == 1:
        dst_ref[:, col0:col0 + GW] = y.astype(dtype)
        return
    for cb in range(GW // LANES):
        scr[cb][...] = y[:, cb * LANES:(cb + 1) * LANES]
    for r in range(d):
        for cb in range(GW // LANES):
            at = col0 + r * GW + cb * LANES
            dst_ref[:, at:at + LANES] = scr[cb][pl.ds(r, ROWS // d, stride=d), :].astype(dtype)


def _from_view(scr, src_ref, d):
    if d == 1:
        return src_ref[...].astype(F32)
    for r in range(d):
        for cb in range(GW // LANES):
            at = r * GW + cb * LANES
            scr[cb][pl.ds(r, ROWS // d, stride=d), :] = src_ref[:, at:at + LANES].astype(F32)
    return jnp.concatenate([scr[cb][...] for cb in range(GW // LANES)], axis=1)


def _view_spec(d):
    return pl.BlockSpec((ROWS // d, d * GW), lambda i: (i, 0))


_VIEW_SCRATCH = [pltpu.VMEM((ROWS, LANES), F32)] * (GW // LANES)


def _qkv_prep(qkv, cos_t, sin_t):
    def body(x_ref, c_ref, s_ref, *rest):
        outs, scr = rest[:9], rest[9:]
        lo = _lane_lo(ROWS)
        cf, sf = _tile4(c_ref[...]), _tile4(s_ref[...])
        for gi, d in enumerate(PATTERN_DILATIONS):
            q, k, v = (x_ref[:, (3 * part + gi) * GW:(3 * part + gi + 1) * GW].astype(F32) for part in range(3))
            _to_view(scr, _rope(q, cf, sf, lo) * (HEAD_DIM ** -0.5), outs[3 * gi], d, BF16)
            _to_view(scr, _rope(k, cf, sf, lo), outs[3 * gi + 1], d, BF16)
            _to_view(scr, v, outs[3 * gi + 2], d, BF16)

    tab = pl.BlockSpec((ROWS, BLK), lambda i: (i, 0))
    outs = pl.pallas_call(
        body,
        name="qkv_prep",
        grid=(S // ROWS,),
        in_specs=[pl.BlockSpec((ROWS, QKV_W), lambda i: (i, 0)), tab, tab],
        out_specs=[_view_spec(d) for d in PATTERN_DILATIONS for _ in range(3)],
        out_shape=[jax.ShapeDtypeStruct((S // d, d * GW), BF16) for d in PATTERN_DILATIONS for _ in range(3)],
        scratch_shapes=_VIEW_SCRATCH,
        compiler_params=_cp(("parallel",), 48),
    )(qkv, cos_t, sin_t)
    return [tuple(outs[3 * gi:3 * gi + 3]) for gi in range(3)]


N_ATTN_STEPS = S // BLK


def _class_and_block(gi, t):
    per_class = N_ATTN_STEPS // PATTERN_DILATIONS[gi]
    return t >> (per_class.bit_length() - 1), t & (per_class - 1)


def _attn_specs(gi):
    def cur(t):
        r, n = _class_and_block(gi, t)
        return n, r

    def prev(t):
        r, n = _class_and_block(gi, t)
        return jnp.maximum(n - 1, 0), r

    def whole(t):
        return 0, _class_and_block(gi, t)[0]

    sub = S // PATTERN_DILATIONS[gi]
    return (pl.BlockSpec((BLK, GW), cur), pl.BlockSpec((BLK, GW), prev), pl.BlockSpec((sub, GW), whole))


def _left_lanes():
    return lax.broadcasted_iota(jnp.int32, (BLK, LANES), 1) < HEAD_DIM


def _stack_heads(t2, left):
    zero = jnp.zeros_like(t2)
    return jnp.concatenate([jnp.where(left, t2, zero), jnp.where(left, zero, t2)], axis=0)


def _attn_fwd_block(n, q_ref, kc_ref, kp_ref, vc_ref, vp_ref, o_ref, l_ref):
    mask = _band_mask(n)
    left = _left_lanes()
    zero = jnp.zeros((BLK, LANES), BF16)
    for pr in range(GW // LANES):
        cs = slice(pr * LANES, (pr + 1) * LANES)
        q2 = q_ref[:, cs]
        k2 = jnp.concatenate([kp_ref[:, cs], kc_ref[:, cs]], axis=0)
        v2 = jnp.concatenate([vp_ref[:, cs], vc_ref[:, cs]], axis=0)
        o_h, lse_h = [], []
        for side in (left, ~left):
            s = jnp.where(mask, _dot_nt(jnp.where(side, q2, zero), k2), NEG)
            m = jnp.max(s, axis=1, keepdims=True)
            p = jnp.exp(s - m)
            l = jnp.sum(p, axis=1, keepdims=True)
            o_h.append(_dot_nn((p / l).astype(BF16), v2))
            lse_h.append(m + jnp.log(l))
        o_ref[:, cs] = jnp.where(left, o_h[0], o_h[1])
        l_ref[:, cs] = jnp.where(left, lse_h[0], lse_h[1])


def _attn_fwd(preps):
    def body(*refs):
        t = pl.program_id(0)
        for gi in range(3):
            _attn_fwd_block(_class_and_block(gi, t)[1], *refs[5 * gi:5 * gi + 5], *refs[15 + 2 * gi:17 + 2 * gi])

    in_specs, out_specs, out_shape, args = [], [], [], []
    for gi, d in enumerate(PATTERN_DILATIONS):
        cur, prev, _ = _attn_specs(gi)
        qr, kr, vv = preps[gi]
        in_specs += [cur, cur, prev, cur, prev]
        args += [qr, kr, kr, vv, vv]
        out_specs += [cur, cur]
        out_shape += [jax.ShapeDtypeStruct((S // d, d * GW), F32)] * 2
    outs = pl.pallas_call(
        body,
        name="attn_fwd",
        grid=(N_ATTN_STEPS,),
        in_specs=in_specs,
        out_specs=out_specs,
        out_shape=out_shape,
        compiler_params=_cp(("arbitrary",), 48),
    )(*args)
    return [outs[0], outs[2], outs[4]], [outs[1], outs[3], outs[5]]


def _attn_bwd_block(n, q_ref, kc_ref, kp_ref, vc_ref, vp_ref, do_ref, l_ref, c_ref, dq_ref, dk_ref, dv_ref):
    mask = _band_mask(n)
    left = _left_lanes()
    left2 = jnp.concatenate([left, left], axis=0)
    here = pl.ds(pl.multiple_of(n * BLK, BLK), BLK)
    before = pl.ds(pl.multiple_of(jnp.maximum(n - 1, 0) * BLK, BLK), BLK)
    for pr in range(GW // LANES):
        cs = slice(pr * LANES, (pr + 1) * LANES)
        q2 = q_ref[:, cs]
        do2 = do_ref[:, cs]
        k_bd = _stack_heads(jnp.concatenate([kp_ref[:, cs], kc_ref[:, cs]], axis=0), left2)
        v_bd = _stack_heads(jnp.concatenate([vp_ref[:, cs], vc_ref[:, cs]], axis=0), left2)
        s2 = _dot_nt(q2, k_bd)
        dp2 = _dot_nt(do2, v_bd)
        ps, dss = [], []
        for h in range(2):
            at = pr * LANES + h * HEAD_DIM
            half = slice(h * 2 * BLK, (h + 1) * 2 * BLK)
            p = jnp.exp(jnp.where(mask, s2[:, half], NEG) - l_ref[:, at:at + 1])
            ps.append(p.astype(BF16))
            dss.append((p * (dp2[:, half] - c_ref[:, at:at + 1])).astype(BF16))
        ds_both = jnp.concatenate(dss, axis=1)
        dq_ref[:, cs] = _dot_nn(ds_both, k_bd) * (HEAD_DIM ** -0.5)
        dk_bd = _dot_tn(ds_both, q2)
        dv_bd = _dot_tn(jnp.concatenate(ps, axis=1), do2)
        dk2 = jnp.where(left2, dk_bd[:2 * BLK], dk_bd[2 * BLK:])
        dv2 = jnp.where(left2, dv_bd[:2 * BLK], dv_bd[2 * BLK:])
        dk_ref[here, cs] = dk2[BLK:]
        dv_ref[here, cs] = dv2[BLK:]
        dk_ref[before, cs] += dk2[:BLK]
        dv_ref[before, cs] += dv2[:BLK]


def _attn_bwd(preps, dos, lses, cterms):
    def body(*refs):
        t = pl.program_id(0)
        for gi in range(3):
            _attn_bwd_block(_class_and_block(gi, t)[1], *refs[8 * gi:8 * gi + 8], *refs[24 + 3 * gi:27 + 3 * gi])

    in_specs, out_specs, out_shape, args = [], [], [], []
    for gi, d in enumerate(PATTERN_DILATIONS):
        cur, prev, whole = _attn_specs(gi)
        qr, kr, vv = preps[gi]
        in_specs += [cur, cur, prev, cur, prev, cur, cur, cur]
        args += [qr, kr, kr, vv, vv, dos[gi], lses[gi], cterms[gi]]
        out_specs += [cur, whole, whole]
        out_shape += [jax.ShapeDtypeStruct((S // d, d * GW), F32)] * 3
    outs = pl.pallas_call(
        body,
        name="attn_bwd",
        grid=(N_ATTN_STEPS,),
        in_specs=in_specs,
        out_specs=out_specs,
        out_shape=out_shape,
        compiler_params=_cp(("arbitrary",), 56),
    )(*args)
    return [tuple(outs[3 * gi:3 * gi + 3]) for gi in range(3)]


def _qkv_unprep(grads, cos_t, sin_t, dproj):
    def body(*refs):
        views, (c_ref, s_ref, _, out_ref), scr = refs[:9], refs[9:13], refs[13:]
        lo = _lane_lo(ROWS)
        cf, sf = _tile4(c_ref[...]), _tile4(s_ref[...])
        for gi, d in enumerate(PATTERN_DILATIONS):
            dq_ref, dk_ref, dv_ref = views[3 * gi:3 * gi + 3]
            qa, ka, va = gi * GW, (3 + gi) * GW, (6 + gi) * GW
            out_ref[:, qa:qa + GW] = _unrope(_from_view(scr, dq_ref, d), cf, sf, lo).astype(BF16)
            out_ref[:, ka:ka + GW] = _unrope(_from_view(scr, dk_ref, d), cf, sf, lo).astype(BF16)
            out_ref[:, va:va + GW] = _from_view(scr, dv_ref, d).astype(BF16)

    tab = pl.BlockSpec((ROWS, BLK), lambda i: (i, 0))
    return pl.pallas_call(
        body,
        name="qkv_unprep",
        grid=(S // ROWS,),
        in_specs=[_view_spec(d) for d in PATTERN_DILATIONS for _ in range(3)] + [tab, tab, pl.BlockSpec(memory_space=pl.ANY)],
        out_specs=pl.BlockSpec((ROWS, QKV_W), lambda i: (i, 0)),
        out_shape=jax.ShapeDtypeStruct(dproj.shape, dproj.dtype),
        input_output_aliases={11: 0},
        scratch_shapes=_VIEW_SCRATCH,
        compiler_params=_cp(("parallel",), 48),
    )(*[g for grp in grads for g in grp], cos_t, sin_t, dproj)


def _group_weights(l0, l1, l2):
    mx = jnp.maximum(jnp.maximum(l0, l1), l2)
    e0, e1, e2 = jnp.exp(l0 - mx), jnp.exp(l1 - mx), jnp.exp(l2 - mx)
    inv = 1.0 / (e0 + e1 + e2)
    return e0 * inv, e1 * inv, e2 * inv


def _combine_fwd(os_, lses):
    def body(o0, o1, o2, l0, l1, l2, y_ref, *scr):
        ov = [_from_view(scr, o, d) for o, d in zip((o0, o1, o2), PATTERN_DILATIONS)]
        lv = [_from_view(scr, l, d) for l, d in zip((l0, l1, l2), PATTERN_DILATIONS)]
        w0, w1, w2 = _group_weights(*lv)
        y_ref[...] = (w0 * ov[0] + w1 * ov[1] + w2 * ov[2]).astype(BF16)

    views = [_view_spec(d) for d in PATTERN_DILATIONS]
    return pl.pallas_call(
        body,
        name="attn_combine_fwd",
        grid=(S // ROWS,),
        in_specs=views + views,
        out_specs=pl.BlockSpec((ROWS, GW), lambda i: (i, 0)),
        out_shape=jax.ShapeDtypeStruct((S, GW), BF16),
        scratch_shapes=_VIEW_SCRATCH,
        compiler_params=_cp(("parallel",)),
    )(*os_, *lses)


def _combine_bwd(dy, os_, lses, seg):
    def body(dy_ref, o0, o1, o2, l0, l1, l2, seg_ref, d0, d1, d2, c0, c1, c2, *scr):
        ov = [_from_view(scr, o, d) for o, d in zip((o0, o1, o2), PATTERN_DILATIONS)]
        lv = [_from_view(scr, l, d) for l, d in zip((l0, l1, l2), PATTERN_DILATIONS)]
        ws = _group_weights(*lv)
        dyv = dy_ref[...]
        t = dyv * (ws[0] * ov[0] + ws[1] * ov[1] + ws[2] * ov[2])
        t_hi = t.astype(BF16)
        r1 = t - t_hi.astype(F32)
        t_mid = r1.astype(BF16)
        t_lo = (r1 - t_mid.astype(F32)).astype(BF16)
        sg = seg_ref[...]
        e = _dot_nn(t_hi, sg) + _dot_nn(t_mid, sg) + _dot_nn(t_lo, sg)
        for w, do_ref, c_ref, d in zip(ws, (d0, d1, d2), (c0, c1, c2), PATTERN_DILATIONS):
            _to_view(scr, w * dyv, do_ref, d, BF16)
            _to_view(scr, w * e, c_ref, d, F32)

    views = [_view_spec(d) for d in PATTERN_DILATIONS]
    return pl.pallas_call(
        body,
        name="attn_combine_bwd",
        grid=(S // ROWS,),
        in_specs=[pl.BlockSpec((ROWS, GW), lambda i: (i, 0))] + views + views + [pl.BlockSpec((GW, GW), lambda i: (0, 0))],
        out_specs=views + views,
        out_shape=[jax.ShapeDtypeStruct((S // d, d * GW), BF16) for d in PATTERN_DILATIONS]
        + [jax.ShapeDtypeStruct((S // d, d * GW), F32) for d in PATTERN_DILATIONS],
        scratch_shapes=_VIEW_SCRATCH,
        compiler_params=_cp(("parallel",)),
    )(dy, *os_, *lses, seg)


_SQRT_HALF = 0.7071067811865476
_INV_SQRT_2PI = 0.3989422804014327


def _gelu(z):
    return 0.5 * z * (1.0 + lax.erf(z * _SQRT_HALF))


def _gelu_grad(z):
    return 0.5 * (1.0 + lax.erf(z * _SQRT_HALF)) + z * (_INV_SQRT_2PI * jnp.exp(-0.5 * z * z))


def _tril_ws(ws_ref, g):
    t = lax.broadcasted_iota(jnp.int32, (BLK, BLK), 0)
    s = lax.broadcasted_iota(jnp.int32, (BLK, BLK), 1)
    return jnp.where(t >= s, ws_ref[g], 0.0)


def _chunks_side_by_side(x, cols, nch):
    return jnp.concatenate([x[c * BLK:(c + 1) * BLK, cols] for c in range(nch)], axis=1)


def _gmlp_fwd(z, ws, bst, ln_g, ln_b, tm=512):
    nch = tm // BLK

    def body(z_ref, ws_ref, b_ref, g_ref, be_ref, y_ref):
        zg = _gelu(z_ref[...].astype(F32))
        u = zg[:, :D]
        vn, _, _ = _ln_fwd(zg[:, D:], g_ref[...], be_ref[...])
        vnb = vn.astype(BF16)
        bt = b_ref[...]
        for g in range(8):
            w = _tril_ws(ws_ref, g).astype(BF16)
            cols = slice(g * BLK, (g + 1) * BLK)
            mixed = _dot_nn(w, _chunks_side_by_side(vnb, cols, nch)) + bt[:, g:g + 1]
            for c in range(nch):
                rows = slice(c * BLK, (c + 1) * BLK)
                y_ref[rows, cols] = (u[rows, cols] * mixed[:, c * BLK:(c + 1) * BLK]).astype(BF16)

    return pl.pallas_call(
        body,
        name="gmlp_fwd",
        grid=(S // tm,),
        in_specs=[
            pl.BlockSpec((tm, 2 * D), lambda i: (i, 0)),
            pl.BlockSpec((8, BLK, BLK), lambda i: (0, 0, 0)),
            pl.BlockSpec((BLK, 8), lambda i: (0, 0)),
            pl.BlockSpec((1, D), lambda i: (0, 0)),
            pl.BlockSpec((1, D), lambda i: (0, 0)),
        ],
        out_specs=pl.BlockSpec((tm, D), lambda i: (i, 0)),
        out_shape=jax.ShapeDtypeStruct((S, D), BF16),
        compiler_params=_cp(("parallel",), 48),
    )(z, ws, bst, ln_g, ln_b)


def _gmlp_bwd(z, dy, ws, bst, ln_g, ln_b, dproj, tm=512):
    nch = tm // BLK

    def body(z_ref, dy_ref, ws_ref, b_ref, g_ref, be_ref, _, dz_ref, dws_ref, dbs_ref, dg_ref, dbe_ref, dvn_scr, dm_acc):
        i = pl.program_id(0)
        zv = z_ref[...].astype(F32)
        zg = _gelu(zv)
        u = zg[:, :D]
        gam = g_ref[...]
        vn, xhat, rstd = _ln_fwd(zg[:, D:], gam, be_ref[...])
        vnb = vn.astype(BF16)
        dyv = dy_ref[...]
        dmix = dyv * u
        dmb = dmix.astype(BF16)
        bt = b_ref[...]
        tmask = lax.broadcasted_iota(jnp.int32, (BLK, BLK), 0) >= lax.broadcasted_iota(jnp.int32, (BLK, BLK), 1)
        dm_sum = dmix[0:BLK]
        for c in range(1, nch):
            dm_sum = dm_sum + dmix[c * BLK:(c + 1) * BLK]

        @pl.when(i == 0)
        def _():
            dm_acc[...] = jnp.zeros_like(dm_acc)
            dws_ref[...] = jnp.zeros_like(dws_ref)
            dg_ref[...] = jnp.zeros_like(dg_ref)
            dbe_ref[...] = jnp.zeros_like(dbe_ref)

        dm_acc[...] += dm_sum
        dus = []
        for g in range(8):
            w = _tril_ws(ws_ref, g).astype(BF16)
            cols = slice(g * BLK, (g + 1) * BLK)
            v_cat = _chunks_side_by_side(vnb, cols, nch)
            dm_cat = _chunks_side_by_side(dmb, cols, nch)
            mixed = _dot_nn(w, v_cat) + bt[:, g:g + 1]
            dus.append(jnp.concatenate(
                [dyv[c * BLK:(c + 1) * BLK, cols] * mixed[:, c * BLK:(c + 1) * BLK] for c in range(nch)], axis=0))
            dws_ref[g] += jnp.where(tmask, _dot_nt(dm_cat, v_cat), 0.0)
            dvn_cat = _dot_tn(w, dm_cat)
            for c in range(nch):
                dvn_scr[c * BLK:(c + 1) * BLK, cols] = dvn_cat[:, c * BLK:(c + 1) * BLK]

        dvn = dvn_scr[...]
        dg_ref[...] += _colsum(dvn * xhat)
        dbe_ref[...] += _colsum(dvn)
        dvg = _ln_bwd(dvn, xhat, rstd, gam)
        gp = _gelu_grad(zv)
        dz_ref[:, :D] = (jnp.concatenate(dus, axis=1) * gp[:, :D]).astype(BF16)
        dz_ref[:, D:] = (dvg * gp[:, D:]).astype(BF16)

        @pl.when(i == S // tm - 1)
        def _():
            acc = dm_acc[...]
            for g in range(8):
                dbs_ref[:, g:g + 1] = jnp.sum(acc[:, g * BLK:(g + 1) * BLK], axis=1, keepdims=True)

    vec = pl.BlockSpec((1, D), lambda i: (0, 0))
    return pl.pallas_call(
        body,
        name="gmlp_bwd",
        grid=(S // tm,),
        in_specs=[
            pl.BlockSpec((tm, 2 * D), lambda i: (i, 0)),
            pl.BlockSpec((tm, D), lambda i: (i, 0)),
            pl.BlockSpec((8, BLK, BLK), lambda i: (0, 0, 0)),
            pl.BlockSpec((BLK, 8), lambda i: (0, 0)),
            vec,
            vec,
            pl.BlockSpec(memory_space=pl.ANY),
        ],
        out_specs=[
            pl.BlockSpec((tm, 2 * D), lambda i: (i, DPROJ_Z_COL)),
            pl.BlockSpec((8, BLK, BLK), lambda i: (0, 0, 0)),
            pl.BlockSpec((BLK, 8), lambda i: (0, 0)),
            vec,
            vec,
        ],
        out_shape=[
            jax.ShapeDtypeStruct(dproj.shape, dproj.dtype),
            jax.ShapeDtypeStruct((8, BLK, BLK), F32),
            jax.ShapeDtypeStruct((BLK, 8), F32),
            jax.ShapeDtypeStruct((1, D), F32),
            jax.ShapeDtypeStruct((1, D), F32),
        ],
        input_output_aliases={6: 0},
        scratch_shapes=[pltpu.VMEM((tm, D), F32), pltpu.VMEM((BLK, D), F32)],
        compiler_params=_cp(("arbitrary",), 48),
    )(z, dy, ws, bst, ln_g, ln_b, dproj)


def _merge_fwd(ya, yg, glog, bgate, h1, wabt, wgb, wo, ln_g, ln_b, tm=256):
    def body(ya_ref, yg_ref, gl_ref, bg_ref, h1_ref, wab_ref, wgb_ref, wo_ref, g_ref, b_ref,
             h_ref, hb_ref, xh_ref, rs_ref, mg_ref, bra_ref, brg_ref):
        bra = _dot_nt(ya_ref[...], wab_ref[...])
        brg = _dot_nn(yg_ref[...], wgb_ref[...])
        gates = _sigmoid(gl_ref[...].astype(F32) + bg_ref[...])
        merged = (gates[:, :D] * bra + gates[:, D:] * brg).astype(BF16)
        mix = _dot_nn(merged, wo_ref[...])
        h, xhat, rstd = _ln_fwd(ALPHA * h1_ref[...] + mix, g_ref[...], b_ref[...])
        h_ref[...] = h
        hb_ref[...] = h.astype(BF16)
        xh_ref[...] = xhat
        rs_ref[...] = rstd
        mg_ref[...] = merged
        bra_ref[...] = bra
        brg_ref[...] = brg

    row = pl.BlockSpec((tm, D), lambda i: (i, 0))
    vec = pl.BlockSpec((1, D), lambda i: (0, 0))
    full = lambda shape: pl.BlockSpec(shape, lambda i: (0, 0))
    return pl.pallas_call(
        body,
        name="merge_fwd",
        grid=(S // tm,),
        in_specs=[
            pl.BlockSpec((tm, GW), lambda i: (i, 0)), row,
            pl.BlockSpec((tm, 2 * D), lambda i: (i, glog.shape[1] // (2 * D) - 1)),
            full((1, 2 * D)), row,
            full((D, GW)), full((D, D)), full((D, D)), vec, vec,
        ],
        out_specs=[row, row, row, pl.BlockSpec((tm, 1), lambda i: (i, 0)), row, row, row],
        out_shape=[
            jax.ShapeDtypeStruct((S, D), F32),
            jax.ShapeDtypeStruct((S, D), BF16),
            jax.ShapeDtypeStruct((S, D), F32),
            jax.ShapeDtypeStruct((S, 1), F32),
            jax.ShapeDtypeStruct((S, D), BF16),
            jax.ShapeDtypeStruct((S, D), F32),
            jax.ShapeDtypeStruct((S, D), F32),
        ],
        compiler_params=_cp(("parallel",), 48),
    )(ya, yg, glog, bgate, h1, wabt, wgb, wo, ln_g, ln_b)


def _merge_bwd(dh2, xhat, rstd, ln_g, bra, brg, glog, bgate, wabt, wgb, wo, tm=256):
    def body(dh_ref, xh_ref, rs_ref, g_ref, bra_ref, brg_ref, gl_ref, bg_ref, wab_ref, wgb_ref, wo_ref,
             dr_ref, drb_ref, dlog_ref, dba_ref, dbg_ref, dya_ref, dyg_ref, dbgate_ref, dg_ref, dbias_ref):
        i = pl.program_id(0)
        dh = dh_ref[...]
        xh = xh_ref[...]
        dr = _ln_bwd(dh, xh, rs_ref[...], g_ref[...])
        drb = dr.astype(BF16)
        dr_ref[...] = dr
        drb_ref[...] = drb
        dmerged = _dot_nt(drb, wo_ref[...])
        gates = _sigmoid(gl_ref[...].astype(F32) + bg_ref[...])
        g0, g1 = gates[:, :D], gates[:, D:]
        dl0 = dmerged * bra_ref[...] * g0 * (1.0 - g0)
        dl1 = dmerged * brg_ref[...] * g1 * (1.0 - g1)
        dlog_ref[:, :D] = dl0.astype(BF16)
        dlog_ref[:, D:] = dl1.astype(BF16)
        dba = (dmerged * g0).astype(BF16)
        dbg = (dmerged * g1).astype(BF16)
        dba_ref[...] = dba
        dbg_ref[...] = dbg
        dya_ref[...] = _dot_nn(dba, wab_ref[...])
        dyg_ref[...] = _dot_nt(dbg, wgb_ref[...])
        s0, s1 = _colsum(dl0), _colsum(dl1)
        sg, sb = _colsum(dh * xh), _colsum(dh)

        @pl.when(i == 0)
        def _():
            dbgate_ref[:, :D] = s0
            dbgate_ref[:, D:] = s1
            dg_ref[...] = sg
            dbias_ref[...] = sb

        @pl.when(i > 0)
        def _():
            dbgate_ref[:, :D] += s0
            dbgate_ref[:, D:] += s1
            dg_ref[...] += sg
            dbias_ref[...] += sb

    row = pl.BlockSpec((tm, D), lambda i: (i, 0))
    vec = pl.BlockSpec((1, D), lambda i: (0, 0))
    wide = pl.BlockSpec((tm, 2 * D), lambda i: (i, 0))
    full = lambda shape: pl.BlockSpec(shape, lambda i: (0, 0))
    return pl.pallas_call(
        body,
        name="merge_bwd",
        grid=(S // tm,),
        in_specs=[row, row, pl.BlockSpec((tm, 1), lambda i: (i, 0)), vec, row, row,
                  pl.BlockSpec((tm, 2 * D), lambda i: (i, glog.shape[1] // (2 * D) - 1)),
                  full((1, 2 * D)), full((D, GW)), full((D, D)), full((D, D))],
        out_specs=[row, row, pl.BlockSpec((tm, 2 * D), lambda i: (i, DPROJ_G_COL)), row, row,
                   pl.BlockSpec((tm, GW), lambda i: (i, 0)), row, full((1, 2 * D)), vec, vec],
        out_shape=[
            jax.ShapeDtypeStruct((S, D), F32),
            jax.ShapeDtypeStruct((S, D), BF16),
            jax.ShapeDtypeStruct((S, DPROJ_W), BF16),
            jax.ShapeDtypeStruct((S, D), BF16),
            jax.ShapeDtypeStruct((S, D), BF16),
            jax.ShapeDtypeStruct((S, GW), F32),
            jax.ShapeDtypeStruct((S, D), F32),
            jax.ShapeDtypeStruct((1, 2 * D), F32),
            jax.ShapeDtypeStruct((1, D), F32),
            jax.ShapeDtypeStruct((1, D), F32),
        ],
        compiler_params=_cp(("arbitrary",), 48),
    )(dh2, xhat, rstd, ln_g, bra, brg, glog, bgate, wabt, wgb, wo)


def _dproj_to_dh1(dproj, win, dep, tm=512):
    n_dep = 0 if dep is None else 1

    def body(a_ref, b_ref, *rest):
        rest[n_dep][...] = (_dot_nn(a_ref[:, :QKV_W], b_ref[:QKV_W, :])
                            + _dot_nn(a_ref[:, DPROJ_ZG_AT:], b_ref[QKV_W:, :]))

    return pl.pallas_call(
        body,
        name="dproj_to_dh1",
        grid=(S // tm,),
        in_specs=[pl.BlockSpec((tm, DPROJ_W), lambda i: (i, 0)),
                  pl.BlockSpec((IN_W, D), lambda i: (0, 0), pipeline_mode=pl.Buffered(1))]
        + [pl.BlockSpec(memory_space=pl.ANY)] * n_dep,
        out_specs=pl.BlockSpec((tm, D), lambda i: (i, 0)),
        out_shape=jax.ShapeDtypeStruct((S, D), F32),
        compiler_params=_cp(("parallel",), 56),
    )(dproj, win, *([] if dep is None else [dep]))


def _tie(x, dep):
    if dep is None:
        return x
    return x + dep[0, 0].astype(x.dtype)


def _local_step(x, pos_col, target, get_w, p, emit):
    w = get_w("ffn1", None)
    a1, b1, hm1 = _ffn_up(x, w["g1"], w["u1"], "ffn1_up")
    w.update(get_w("ffn1d", hm1))
    h1, h1b, xh1, rs1 = _ffn_down(hm1, w["d1"], x, p["ln1_g"], p["ln1_b"], "ffn1_down")

    w.update(get_w("win", h1b))
    qkv = _matmul(h1b, w["win"], "nt", BF16, S, 1536, D, "proj_qkv", b_off=0, n_out=QKV_W)
    z = glog = _matmul(h1b, w["win"], "nt", BF16, S, 512, D, "proj_zg", b_off=QKV_W // 512, n_out=4 * D)

    half = jnp.arange(0, HEAD_DIM, 2, dtype=F32) / HEAD_DIM
    inv_freq = ROPE_THETA ** (-half)
    invf = jnp.tile(inv_freq, 4).reshape(1, BLK)
    sign = jnp.tile(jnp.concatenate([-jnp.ones((32,), F32), jnp.ones((32,), F32)]), 2).reshape(1, BLK)
    cos_t, sin_t = _rope_tables(pos_col, invf, sign)

    preps = _qkv_prep(qkv, cos_t, sin_t)
    os_, lses = _attn_fwd(preps)
    ya = _combine_fwd(os_, lses)
    get_w("late", ya, early=True)
    bst = p["gmlp_b_s"].T
    yg = _gmlp_fwd(z, p["gmlp_w_s"], bst, p["gmlp_ln_g"], p["gmlp_ln_b"])
    w.update(get_w("late", yg))
    h2, h2b, xh2, rs2, merged, bra, brg = _merge_fwd(ya, yg, glog, p["b_gates"], h1, w["ab"], w["gb"], w["o"],
                                                      p["ln2_g"], p["ln2_b"])
    a2, b2, hm2 = _ffn_up(h2b, w["g2"], w["u2"], "ffn2_up")
    dh3, loss, xh3, rs3 = _ffn_down(hm2, w["d2"], h2, p["ln3_g"], p["ln3_b"], "ffn2_down", target=target)

    gp = {}
    dr3, df2, da2, db2, gp["ln3_g"], gp["ln3_b"] = _ffn_bwd_mid(dh3, None, xh3, rs3, p["ln3_g"], a2, b2, w["d2"],
                                                                "ffn2_bwd_mid")
    tok = emit("ffn2", {
        "g2": _matmul(da2, h2b, "tn", BF16, 1408, D, S, "wgrad_g2"),
        "u2": _matmul(db2, h2b, "tn", BF16, 1408, D, S, "wgrad_u2"),
        "d2": _matmul(hm2, df2, "tn", BF16, 1408, D, S, "wgrad_d2")})
    dh2 = _ffn_bwd_dx(dr3, da2, db2, w["g2"], w["u2"], "ffn2_bwd_dx")

    (dr2, dr2b, dproj, dba, dbg, dya, dyg, gp["b_gates"], gp["ln2_g"], gp["ln2_b"]) = _merge_bwd(
        dh2, xh2, rs2, _tie(p["ln2_g"], tok), bra, brg, glog, p["b_gates"], w["ab"], w["gb"], w["o"])
    tok = emit("mix", {
        "o": _matmul(merged, dr2b, "tn", BF16, 512, D, S, "wgrad_o"),
        "ab": _matmul(dba, ya, "tn", BF16, 512, GW, S, "wgrad_ab"),
        "gb": _matmul(yg, dbg, "tn", BF16, 512, D, S, "wgrad_gb")})

    seg = (jnp.arange(GW)[:, None] // HEAD_DIM == jnp.arange(GW)[None, :] // HEAD_DIM).astype(BF16)
    do0, do1, do2, c0, c1, c2 = _combine_bwd(dya, os_, lses, _tie(seg, tok))
    dproj = _qkv_unprep(_attn_bwd(preps, (do0, do1, do2), lses, (c0, c1, c2)), cos_t, sin_t, dproj)
    dproj, gp["gmlp_w_s"], dbst, gp["gmlp_ln_g"], gp["gmlp_ln_b"] = _gmlp_bwd(
        z, dyg, p["gmlp_w_s"], bst, p["gmlp_ln_g"], p["gmlp_ln_b"], dproj)
    gp["gmlp_b_s"] = dbst.T
    tok = emit("win", {"win": _matmul(dproj, h1b, "tn", BF16, 512, D, S, "wgrad_win", a_map=_dproj_tile, m_out=IN_W)})
    dh1m = _dproj_to_dh1(dproj, w["win"], tok)

    dr1, df1, da1, db1, gp["ln1_g"], gp["ln1_b"] = _ffn_bwd_mid(dr2, dh1m, xh1, rs1, p["ln1_g"], a1, b1, w["d1"],
                                                                "ffn1_bwd_mid")
    tok = emit("small", {**gp, "loss": loss})
    tok = emit("g1", {"g1": _matmul(da1, x, "tn", BF16, 1408, D, S, "wgrad_g1", dep=tok)})
    tok = emit("ud1", {"u1": _matmul(db1, x, "tn", BF16, 1408, D, S, "wgrad_u1", dep=tok),
                       "d1": _matmul(hm1, df1, "tn", BF16, 1408, D, S, "wgrad_d1", dep=tok)})
    dx = _ffn_bwd_dx(dr1, da1, db1, w["g1"], w["u1"], "ffn1_bwd_dx", dep=tok)
    return loss, dx, gp


_FLIPS = [(mx, my, mc) for mx in (0, 1) for my in (0, 1) for mc in (0, 1)][1:]
_GROUPS = {"ffn1": ("g1", "u1"), "ffn1d": ("d1",), "win": ("win",), "late": ("ab", "gb", "o", "g2", "u2", "d2")}
_SCATTERS = {"ffn2": ("g2", "u2", "d2"), "mix": ("ab", "gb", "o"), "win": ("win",), "g1": ("g1",), "ud1": ("u1", "d1")}
_TWO_STAGE = ("g1", "ud1")
_HBM = pl.BlockSpec(memory_space=pltpu.HBM)
_SEM = pl.BlockSpec(memory_space=pltpu.SEMAPHORE)
_EFFECT = pltpu.SideEffectType.DATAFLOW_SIDE_EFFECTING


def _me():
    return 4 * lax.axis_index("x") + 2 * lax.axis_index("y") + lax.axis_index("c")


def _copies_start(bufs, copies, n_sem, name, after=None):
    nb = len(bufs)
    n_after = 0 if after is None else 1

    def body(*refs):
        b = refs[:nb]
        send_sems, recv_sems = refs[nb + n_after], refs[nb + n_after + 1]
        token = refs[-1]
        x, y, c = lax.axis_index("x"), lax.axis_index("y"), lax.axis_index("c")
        me = 4 * x + 2 * y + c
        for si, s_slot, di, d_slot, flip, sem in copies:
            s_idx, d_idx = s_slot(me), d_slot(me)
            mx, my, mc = flip(c) if callable(flip) else flip
            pltpu.make_async_remote_copy(
                src_ref=b[si] if s_idx is None else b[si].at[s_idx],
                dst_ref=b[di] if d_idx is None else b[di].at[d_idx],
                send_sem=send_sems.at[sem], recv_sem=recv_sems.at[sem],
                device_id=(x ^ mx, y ^ my, c ^ mc), device_id_type=MESH).start()
        token[...] = jnp.zeros_like(token)

    ins = [pltpu.with_memory_space_constraint(a, pltpu.HBM) for a in bufs]
    outs = pl.pallas_call(
        body,
        name=name,
        in_specs=[_HBM] * nb + [pl.BlockSpec(memory_space=pl.ANY)] * n_after,
        out_specs=[_SEM, _SEM] + [_HBM] * nb + [pl.BlockSpec(memory_space=pltpu.VMEM)],
        out_shape=[pltpu.SemaphoreType.DMA((n_sem,)), pltpu.SemaphoreType.DMA((n_sem,))]
        + [pltpu.HBM(a.shape, a.dtype) for a in ins] + [jax.ShapeDtypeStruct((8, 128), F32)],
        input_output_aliases={k: 2 + k for k in range(nb)},
        compiler_params=pltpu.CompilerParams(has_side_effects=_EFFECT),
    )(*ins, *([] if after is None else [after]))
    return outs[0], outs[1], list(outs[2:2 + nb]), outs[-1]


def _copies_wait(started, waits, after, name):
    send_sems, recv_sems, bufs, _ = started
    nb = len(bufs)

    def body(*refs):
        b = refs[:nb]
        ss, rs = refs[nb], refs[nb + 1]
        me3 = (lax.axis_index("x"), lax.axis_index("y"), lax.axis_index("c"))
        for bi, n_blocks, sem, is_send in waits:
            blocks = b[bi].at[pl.ds(0, n_blocks)]
            cp = pltpu.make_async_remote_copy(src_ref=blocks, dst_ref=blocks, send_sem=ss.at[sem], recv_sem=rs.at[sem],
                                              device_id=me3, device_id_type=MESH)
            if is_send:
                cp.wait_send()
            else:
                cp.wait_recv()

    return pl.pallas_call(
        body,
        name=name,
        in_specs=[_HBM] * nb + [_SEM, _SEM, pl.BlockSpec(memory_space=pl.ANY)],
        out_specs=[_HBM] * nb,
        out_shape=[pltpu.HBM(a.shape, a.dtype) for a in bufs],
        input_output_aliases={k: k for k in range(nb)},
        compiler_params=pltpu.CompilerParams(has_side_effects=_EFFECT),
    )(*bufs, send_sems, recv_sems, after)


def _landing(own, me):
    return lax.dynamic_update_slice(lax.empty((N_DEV,) + own.shape[1:], own.dtype), own, (me, 0, 0))


_SIBLING = (0, 0, 1)
_OTHER_CHIPS = ((1, 0, 0), (0, 1, 0), (1, 1, 0))


def _bits(flip):
    return 4 * flip[0] + 2 * flip[1] + flip[2]


_X_NBR, _Y_NBR = (1, 0, 0), (0, 1, 0)


def _xor(bits):
    return functools.partial(lambda me, bits: me ^ bits, bits=bits)


def _gather_send(shards, grp, after):
    nw = len(shards)
    me = _me()
    bufs = list(shards) + [_landing(a[None], me) for a in shards]
    copies = []
    for k in range(nw):
        for s, flip in enumerate((_SIBLING, _X_NBR, _Y_NBR)):
            copies.append((k, lambda me: None, nw + k, lambda me: me, flip, 3 * k + s))
    return _copies_start(bufs, copies, 3 * nw, "gather_send_" + grp, after)


def _gather_pass(started, grp, after):
    nw = len(started[2]) // 2
    waits = [(nw + k, 1, 3 * k + s, is_send) for k in range(nw) for s in range(3) for is_send in (True, False)]
    lands = list(_copies_wait(started, waits, after, "gather_arrived_" + grp)[nw:])
    copies = []
    for k in range(nw):
        held = lambda me: me ^ (4 - 2 * (me % 2))
        copies.append((k, held, k, held, lambda c: (c, 1 - c, 0), 3 * k))
        copies.append((k, _xor(4), k, _xor(4), _SIBLING, 3 * k + 1))
        copies.append((k, _xor(2), k, _xor(2), _SIBLING, 3 * k + 2))
    return _copies_start(lands, copies, 3 * nw, "gather_pass_" + grp)


def _gather_finish(passed, grp, after):
    nw = len(passed[2])
    waits = [(k, 1, 3 * k + s, is_send) for k in range(nw) for s in range(3) for is_send in (True, False)]
    lands = list(_copies_wait(passed, waits, passed[3] if after is None else after, "gather_forwarded_" + grp))
    copies = [(k, _xor(6), k, _xor(6), _SIBLING, k) for k in range(nw)]
    last = _copies_start(lands, copies, nw, "gather_pass2_" + grp)
    waits = [(k, 1, k, is_send) for k in range(nw) for is_send in (True, False)]
    return _copies_wait(last, waits, last[3], "gather_done_" + grp)


def _small_send(block, after=None):
    me = _me()
    copies = [(0, lambda me: None, 1, lambda me: me, _SIBLING, 1)]
    copies += [(0, lambda me: None, 1, lambda me: me, f, 0) for f in _OTHER_CHIPS]
    return _copies_start([block, _landing(block[None], me)], copies, 2, "small_send", after)


def _small_pass(started, after):
    waits = [(1, 3, 0, True), (1, 1, 1, True), (1, 3, 0, False), (1, 1, 1, False)]
    land = _copies_wait(started, waits, after, "small_arrived")[1]
    copies = [(0, _xor(_bits(f)), 0, _xor(_bits(f)), _SIBLING, 0) for f in _OTHER_CHIPS]
    return _copies_start([land], copies, 1, "small_pass")


def _small_finish(passed, after):
    return _copies_wait(passed, [(0, 3, 0, True), (0, 3, 0, False)], after, "small_done")[0]


def _scatter_send(parts, grp, after=None):
    nw = len(parts)
    me = _me()
    bufs = list(parts) + [_landing(lax.dynamic_index_in_dim(a, me, 0, keepdims=True), me) for a in parts]
    copies = []
    for k in range(nw):
        for f in _FLIPS:
            to = functools.partial(lambda me, bits: me ^ bits, bits=_bits(f))
            copies.append((k, to, nw + k, lambda me: me, f, k))
    return _copies_start(bufs, copies, nw, "scatter_send_" + grp, after)


def _scatter_finish(started, grp, after):
    nw = len(started[2]) // 2
    waits = [(nw + k, N_DEV - 1, k, is_send) for k in range(nw) for is_send in (True, False)]
    return _copies_wait(started, waits, after, "scatter_done_" + grp)[nw:]


N_CHIP = N_DEV // 2


def _pair_sum(part, other, key):
    _, r, c = part.shape
    tm = r
    core = lax.axis_index("c").astype(jnp.int32).reshape(1)

    def body(core_ref, a_ref, b_ref, o_ref):
        o_ref[...] = (a_ref[...].astype(F32) + b_ref[...].astype(F32)).astype(BF16)

    return pl.pallas_call(
        body,
        name="pair_sum_" + key,
        grid_spec=pltpu.PrefetchScalarGridSpec(
            num_scalar_prefetch=1,
            grid=(N_CHIP, r // tm),
            in_specs=[pl.BlockSpec((None, tm, c), lambda q, i, core_ref: (2 * q + core_ref[0], i, 0)),
                      pl.BlockSpec((None, tm, c), lambda q, i, core_ref: (q, i, 0))],
            out_specs=pl.BlockSpec((None, tm, c), lambda q, i, core_ref: (q, i, 0)),
        ),
        out_shape=jax.ShapeDtypeStruct((N_CHIP, r, c), BF16),
        compiler_params=_cp(("parallel", "parallel")),
    )(core, part, other)


def _scatter2_send(parts, grp, keys):
    nw = len(parts)
    chip = _me() // 2
    swap = []
    for k in range(nw):
        for q in range(N_CHIP):
            src = functools.partial(lambda me, q: 2 * q + 1 - me % 2, q=q)
            swap.append((k, src, nw + k, functools.partial(lambda me, q: q, q=q), _SIBLING, k))
    others = [lax.empty((N_CHIP,) + a.shape[1:], a.dtype) for a in parts]
    started = _copies_start(list(parts) + others, swap, nw, "scatter_swap_" + grp)
    waits = [(nw + k, N_CHIP, k, is_send) for k in range(nw) for is_send in (True, False)]
    swapped = _copies_wait(started, waits, started[3], "scatter_swapped_" + grp)
    pairs = [_pair_sum(swapped[k], swapped[nw + k], keys[k]) for k in range(nw)]
    lands = [lax.dynamic_update_slice(lax.empty(p.shape, p.dtype), lax.dynamic_index_in_dim(p, chip, 0, keepdims=True),
                                      (chip, 0, 0)) for p in pairs]
    copies = []
    for k in range(nw):
        for f in _OTHER_CHIPS:
            to = functools.partial(lambda me, bits: (me ^ bits) // 2, bits=_bits(f))
            copies.append((k, to, nw + k, lambda me: me // 2, f, k))
    return _copies_start(pairs + lands, copies, nw, "scatter_send_" + grp)


def _scatter2_finish(started, grp, after):
    nw = len(started[2]) // 2
    waits = [(nw + k, N_CHIP - 1, k, is_send) for k in range(nw) for is_send in (True, False)]
    return _copies_wait(started, waits, after, "scatter_done_" + grp)[nw:]


_ADAM_ROWS = {352: 176, 1088: 272}


def _sum_adamw(parts, wv, m, v, name, dep=None):
    n_parts, r, c = parts.shape
    tm = _ADAM_ROWS.get(r, r)
    assert r % tm == 0 and wv.shape == (r, c)
    n_dep = 0 if dep is None else 1

    def body(p_ref, w_ref, m_ref, v_ref, *rest):
        g_ref, d_ref, mo_ref, vo_ref = rest[n_dep:]
        gv = p_ref[0].astype(F32)
        for j in range(1, n_parts):
            gv = gv + p_ref[j].astype(F32)
        g_ref[...] = gv
        mn = ADAM_B1 * m_ref[...] + (1.0 - ADAM_B1) * gv
        vn = ADAM_B2 * v_ref[...] + (1.0 - ADAM_B2) * (gv * gv)
        m_hat = mn / (1.0 - ADAM_B1 ** ADAM_STEP)
        v_hat = vn / (1.0 - ADAM_B2 ** ADAM_STEP)
        d_ref[...] = -ADAM_LR * (m_hat / (jnp.sqrt(v_hat) + ADAM_EPS) + ADAM_WD * w_ref[...])
        mo_ref[...] = mn
        vo_ref[...] = vn

    sp = pl.BlockSpec((tm, c), lambda i: (i, 0))
    return pl.pallas_call(
        body,
        name=name,
        grid=(r // tm,),
        in_specs=[pl.BlockSpec((n_parts, tm, c), lambda i: (0, i, 0))] + [sp] * 3 + [pl.BlockSpec(memory_space=pl.ANY)] * n_dep,
        out_specs=[sp] * 4,
        out_shape=[jax.ShapeDtypeStruct((r, c), F32)] * 4,
        compiler_params=_cp(("parallel",), 48),
    )(parts, wv, m, v, *([] if dep is None else [dep]))


_WEIGHTS = ["ffn1_w_gate", "ffn1_w_up", "ffn1_w_down", "ln1_g", "ln1_b", "w_in", "b_gates", "gmlp_ln_g", "gmlp_ln_b",
            "gmlp_w_s", "gmlp_b_s", "w_attn_branch", "w_gmlp_branch", "w_out", "ln2_g", "ln2_b", "ffn2_w_gate",
            "ffn2_w_up", "ffn2_w_down", "ln3_g", "ln3_b"]
_BIG_OF = {"ffn1_w_gate": ("g1", True), "ffn1_w_up": ("u1", True), "ffn1_w_down": ("d1", False), "w_in": ("win", True),
           "w_attn_branch": ("ab", True), "w_gmlp_branch": ("gb", False), "w_out": ("o", False),
           "ffn2_w_gate": ("g2", True), "ffn2_w_up": ("u2", True), "ffn2_w_down": ("d2", False)}
_SMALL = [n for n in _WEIGHTS if n not in _BIG_OF]
_SMALL_ROWS = {"gmlp_w_s": 128, "b_gates": 2}
_SMALL_SLOT = 8


def _pack_small(d, last=None):
    rows = []
    for n in _SMALL:
        r = d[n].reshape(-1, D)
        slot = max(r.shape[0], _SMALL_SLOT)
        rows.append(jnp.pad(r, ((0, slot - r.shape[0]), (0, 0))))
    rows.append(jnp.zeros((_SMALL_SLOT, D), F32) if last is None else jnp.broadcast_to(last.reshape(1, 1), (_SMALL_SLOT, D)))
    return jnp.concatenate(rows, axis=0)


def _unpack_small(packed, shapes):
    out, at = {}, 0
    for n in _SMALL:
        k = _SMALL_ROWS.get(n, 1)
        out[n] = packed[at:at + k].reshape(shapes[n])
        at += max(k, _SMALL_SLOT)
    return out


def kernel(x, positions, ffn1_w_gate, ffn1_w_up, ffn1_w_down, ln1_g, ln1_b, w_in, b_gates, gmlp_ln_g, gmlp_ln_b, gmlp_w_s, gmlp_b_s, w_attn_branch, w_gmlp_branch, w_out, ln2_g, ln2_b, ffn2_w_gate, ffn2_w_up, ffn2_w_down, ln3_g, ln3_b, loss_target, m_ffn1_w_gate, m_ffn1_w_up, m_ffn1_w_down, m_ln1_g, m_ln1_b, m_w_in, m_b_gates, m_gmlp_ln_g, m_gmlp_ln_b, m_gmlp_w_s, m_gmlp_b_s, m_w_attn_branch, m_w_gmlp_branch, m_w_out, m_ln2_g, m_ln2_b, m_ffn2_w_gate, m_ffn2_w_up, m_ffn2_w_down, m_ln3_g, m_ln3_b, v_ffn1_w_gate, v_ffn1_w_up, v_ffn1_w_down, v_ln1_g, v_ln1_b, v_w_in, v_b_gates, v_gmlp_ln_g, v_gmlp_ln_b, v_gmlp_w_s, v_gmlp_b_s, v_w_attn_branch, v_w_gmlp_branch, v_w_out, v_ln2_g, v_ln2_b, v_ffn2_w_gate, v_ffn2_w_up, v_ffn2_w_down, v_ln3_g, v_ln3_b):
    args = dict(locals())
    wts = {n: args[n] for n in _WEIGHTS}
    ms = {n: args["m_" + n] for n in _WEIGHTS}
    vs = {n: args["v_" + n] for n in _WEIGHTS}

    name_of = {key: (n, tr) for n, (key, tr) in _BIG_OF.items()}

    shards = {}
    for grp, keys in _GROUPS.items():
        shards[grp] = []
        for key in keys:
            n, tr = name_of[key]
            s2 = wts[n][0]
            shards[grp].append((s2.T if tr else s2).astype(BF16))
    started, passed = {}, {}

    def send(grp, after):
        started[grp] = _gather_send(shards[grp], grp, after)
        return started[grp][3]

    def pass_on(grp, after):
        passed[grp] = _gather_pass(started[grp], grp, after)
        return passed[grp][3]

    def finish(grp, after):
        lands = _gather_finish(passed[grp], grp, after)
        return {key: g.reshape(-1, g.shape[-1]) for key, g in zip(_GROUPS[grp], lands)}

    def get_w(grp, after, early=False):
        if grp == "ffn1":
            return finish("ffn1", send("win", send("ffn1d", pass_on("ffn1", send("ffn1", None)))))
        if grp == "ffn1d":
            return finish("ffn1d", send("late", pass_on("win", pass_on("ffn1d", after))))
        if early:
            pass_on(grp, after)
            return None
        return finish(grp, after)

    sent = {}

    def emit(grp, grads):
        if grp == "small":
            sent[grp] = _small_send(_pack_small(grads, last=grads["loss"]))
        else:
            parts = [grads[key].reshape(N_DEV, -1, grads[key].shape[-1]) for key in _SCATTERS[grp]]
            sent[grp] = (_scatter2_send(parts, grp, _SCATTERS[grp]) if grp in _TWO_STAGE else _scatter_send(parts, grp))
        return sent[grp][3]

    p = {n: (wts[n][0] if n in ("gmlp_w_s", "gmlp_b_s") else wts[n]) for n in _SMALL}
    loss, dx, gp = _local_step(x[0], positions.reshape(S, 1), loss_target[0], get_w, p, emit)
    grads, deltas, new_m, new_v = {}, {}, {}, {}
    small_passed = _small_pass(sent["small"], dx)
    after = small_passed[3]
    for grp in ("ffn2", "mix", "win", "small", "g1", "ud1"):
        if grp == "small":
            parts = _small_finish(small_passed, after)
            outs = _sum_adamw(parts, *[_pack_small({n: d[n] for n in _SMALL}) for d in (wts, ms, vs)], "update_small")
            shapes = {n: wts[n].shape for n in _SMALL}
            for dst, packed in zip((grads, deltas, new_m, new_v), outs):
                dst.update(_unpack_small(packed, shapes))
            loss = outs[0][-_SMALL_SLOT, 0]
            after = outs[1]
            continue
        arrived = (_scatter2_finish if grp in _TWO_STAGE else _scatter_finish)(sent[grp], grp, after)
        for key, part in zip(_SCATTERS[grp], arrived):
            n, tr = name_of[key]
            outs = _sum_adamw(part, *[(d[n][0].T if tr else d[n][0]) for d in (wts, ms, vs)], "update_" + key, dep=after)
            for dst, o in zip((grads, deltas, new_m, new_v), outs):
                dst[n] = (o.T if tr else o)[None]
            after = outs[1]

    return (loss, dx[None], *[grads[n] for n in _WEIGHTS], *[deltas[n] for n in _WEIGHTS],
            *[new_m[n] for n in _WEIGHTS], *[new_v[n] for n in _WEIGHTS])
```

```python
import functools
import math

import jax
import jax.numpy as jnp
from jax import lax
from jax.experimental import pallas as pl
from jax.experimental.pallas import tpu as pltpu

F32 = jnp.float32
BF16 = jnp.bfloat16

N_DEV = 8
D = 1024
S = 2048
F = 2816
HEAD_DIM = 64
HEADS = 8
GW = HEADS * HEAD_DIM
PATTERN_DILATIONS = (1, 4, 16)
BLK = 128
QKV_W = 3 * 3 * GW
IN_W = QKV_W + 2 * D + 2 * D
DPROJ_W = 5 * 2 * D
DPROJ_Z_COL, DPROJ_G_COL = 3, 4
DPROJ_ZG_AT = DPROJ_Z_COL * 2 * D


def _dproj_tile(t):
    return jnp.where(t < QKV_W // 512, t, t + (DPROJ_ZG_AT - QKV_W) // 512)
ROPE_THETA = 10000.0
ALPHA = 2.0 ** 0.25
LN_EPS = 1e-5
ADAM_LR, ADAM_B1, ADAM_B2, ADAM_EPS, ADAM_WD, ADAM_STEP = 0.001, 0.9, 0.999, 1e-08, 0.01, 10
NEG = -1e30
MESH = pl.DeviceIdType.MESH


def _cp(sem=None, vmem_mb=None):
    kw = {}
    if sem is not None:
        kw["dimension_semantics"] = sem
    if vmem_mb is not None:
        kw["vmem_limit_bytes"] = vmem_mb << 20
    return pltpu.CompilerParams(**kw)


def _dot_nn(a, b):
    return lax.dot_general(a, b, (((1,), (0,)), ((), ())), preferred_element_type=F32)


def _dot_nt(a, b):
    return lax.dot_general(a, b, (((1,), (1,)), ((), ())), preferred_element_type=F32)


def _dot_tn(a, b):
    return lax.dot_general(a, b, (((0,), (0,)), ((), ())), preferred_element_type=F32)


def _ln_fwd(r, g, b):
    mu = jnp.mean(r, axis=-1, keepdims=True)
    xc = r - mu
    var = jnp.mean(xc * xc, axis=-1, keepdims=True)
    rstd = lax.rsqrt(var + LN_EPS)
    xhat = xc * rstd
    return xhat * g + b, xhat, rstd


def _ln_bwd(dh, xhat, rstd, g):
    dxh = dh * g
    m1 = jnp.mean(dxh, axis=-1, keepdims=True)
    m2 = jnp.mean(dxh * xhat, axis=-1, keepdims=True)
    return rstd * (dxh - m1 - xhat * m2)


def _sigmoid(x):
    return 0.5 * jnp.tanh(0.5 * x) + 0.5


def _colsum(x):
    return jnp.sum(x, axis=0, keepdims=True)


def _matmul(a, b, mode, out_dtype, tm, tn, tk, name, b_off=0, n_out=None, dep=None, a_map=None, m_out=None):
    n_dep = 0 if dep is None else 1
    a_map = a_map or (lambda t: t)
    if mode == "nn":
        m, k = a.shape[0], b.shape[0]
        n = b.shape[1]
    elif mode == "nt":
        m, k = a.shape
        n = n_out if n_out is not None else b.shape[0]
    else:
        k, m = a.shape[0], m_out or a.shape[1]
        n = b.shape[1]
    nk = k // tk
    assert m % tm == 0 and n % tn == 0 and k % tk == 0
    dot = {"nn": _dot_nn, "nt": _dot_nt, "tn": _dot_tn}[mode]

    def body(a_ref, b_ref, *rest):
        o_ref, scr = rest[n_dep], rest[n_dep + 1:]
        r = dot(a_ref[...].astype(BF16), b_ref[...].astype(BF16))
        if nk == 1:
            o_ref[...] = r.astype(out_dtype)
        else:
            acc = scr[0]
            kk = pl.program_id(2)

            @pl.when(kk == 0)
            def _():
                acc[...] = r

            @pl.when(kk > 0)
            def _():
                acc[...] += r

            @pl.when(kk == nk - 1)
            def _():
                o_ref[...] = acc[...].astype(out_dtype)

    if mode == "nn":
        a_spec = pl.BlockSpec((tm, tk), lambda i, j, kk: (i, a_map(kk)))
        b_spec = pl.BlockSpec((tk, tn), lambda i, j, kk: (kk, j))
    elif mode == "nt":
        a_spec = pl.BlockSpec((tm, tk), lambda i, j, kk: (i, kk))
        b_spec = pl.BlockSpec((tn, tk), lambda i, j, kk: (j + b_off, kk))
    else:
        a_spec = pl.BlockSpec((tk, tm), lambda i, j, kk: (kk, a_map(i)))
        b_spec = pl.BlockSpec((tk, tn), lambda i, j, kk: (kk, j))
    return pl.pallas_call(
        body,
        name=name,
        grid=(m // tm, n // tn, nk),
        in_specs=[a_spec, b_spec] + [pl.BlockSpec(memory_space=pl.ANY)] * n_dep,
        out_specs=pl.BlockSpec((tm, tn), lambda i, j, kk: (i, j)),
        out_shape=jax.ShapeDtypeStruct((m, n), out_dtype),
        scratch_shapes=[] if nk == 1 else [pltpu.VMEM((tm, tn), F32)],
        compiler_params=_cp(("parallel", "parallel", "arbitrary"), 56),
    )(a, b, *([] if dep is None else [dep]))


def _ffn_up(x, wgt, wut, name, tm=256, tn=F, dep=None):
    n_dep = 0 if dep is None else 1

    def body(x_ref, wg_ref, wu_ref, *rest):
        ga_ref, gb_ref, hm_ref = rest[n_dep:]
        xb = x_ref[...].astype(BF16)
        a = _dot_nt(xb, wg_ref[...])
        b = _dot_nt(xb, wu_ref[...])
        sig = _sigmoid(a)
        silu = a * sig
        ga_ref[...] = (b * (sig + silu * (1.0 - sig))).astype(BF16)
        gb_ref[...] = silu.astype(BF16)
        hm_ref[...] = (silu * b).astype(BF16)

    wsp = pl.BlockSpec((tn, D), lambda i, j: (j, 0))
    mid = pl.BlockSpec((tm, tn), lambda i, j: (i, j))
    return pl.pallas_call(
        body,
        name=name,
        grid=(S // tm, F // tn),
        in_specs=[pl.BlockSpec((tm, D), lambda i, j: (i, 0)), wsp, wsp] + [pl.BlockSpec(memory_space=pl.ANY)] * n_dep,
        out_specs=[mid, mid, mid],
        out_shape=[jax.ShapeDtypeStruct((S, F), BF16)] * 3,
        compiler_params=_cp(("parallel", "arbitrary"), 56),
    )(x, wgt, wut, *([] if dep is None else [dep]))


def _ffn_down(hm, wd, x, ln_g, ln_b, name, target=None, tm=512):
    head = target is not None

    def body(hm_ref, wd_ref, x_ref, g_ref, b_ref, *rest):
        t_ref = rest[0] if head else None
        o1_ref, o2_ref, xh_ref, rs_ref = rest[1 if head else 0:]
        r = ALPHA * x_ref[...] + 0.5 * _dot_nn(hm_ref[...], wd_ref[...])
        h, xhat, rstd = _ln_fwd(r, g_ref[...], b_ref[...])
        xh_ref[...] = xhat
        rs_ref[...] = rstd
        if head:
            e = h - t_ref[...]
            o1_ref[...] = e * (1.0 / D)
            part = jnp.sum(_colsum(e * e), axis=1, keepdims=True) * (0.5 / D)

            @pl.when(pl.program_id(0) == 0)
            def _():
                o2_ref[...] = jnp.zeros_like(o2_ref)

            o2_ref[...] += part
        else:
            o1_ref[...] = h
            o2_ref[...] = h.astype(BF16)

    row = pl.BlockSpec((tm, D), lambda i: (i, 0))
    vec = pl.BlockSpec((1, D), lambda i: (0, 0))
    second = (pl.BlockSpec((1, 1), lambda i: (0, 0)), jax.ShapeDtypeStruct((1, 1), F32)) if head else (
        row, jax.ShapeDtypeStruct((S, D), BF16))
    return pl.pallas_call(
        body,
        name=name,
        grid=(S // tm,),
        in_specs=[pl.BlockSpec((tm, F), lambda i: (i, 0)), pl.BlockSpec((F, D), lambda i: (0, 0)), row, vec, vec]
        + ([row] if head else []),
        out_specs=[row, second[0], row, pl.BlockSpec((tm, 1), lambda i: (i, 0))],
        out_shape=[
            jax.ShapeDtypeStruct((S, D), F32),
            second[1],
            jax.ShapeDtypeStruct((S, D), F32),
            jax.ShapeDtypeStruct((S, 1), F32),
        ],
        compiler_params=_cp(("arbitrary",), 56),
    )(hm, wd, x, ln_g, ln_b, *([target] if head else []))


def _ffn_bwd_mid(dh_a, dh_b, xhat, rstd, ln_g, a, b, wd, name, tm=256, tn=F):
    two = dh_b is not None

    def body(*refs):
        dha_ref = refs[0]
        dhb_ref = refs[1] if two else None
        (xh_ref, rs_ref, g_ref, a_ref, b_ref, wd_ref,
         dr_ref, df_ref, da_ref, db_ref, dg_ref, dbias_ref, df_scr) = refs[2 if two else 1:]
        i = pl.program_id(0)
        j = pl.program_id(1)

        @pl.when(j == 0)
        def _():
            dh = dha_ref[...]
            if two:
                dh = ALPHA * dh + dhb_ref[...]
            xhat = xh_ref[...]
            dr = _ln_bwd(dh, xhat, rs_ref[...], g_ref[...])
            dfb = (0.5 * dr).astype(BF16)
            dr_ref[...] = dr
            df_scr[...] = dfb
            df_ref[...] = dfb
            sg = _colsum(dh * xhat)
            sb = _colsum(dh)

            @pl.when(i == 0)
            def _():
                dg_ref[...] = sg
                dbias_ref[...] = sb

            @pl.when(i > 0)
            def _():
                dg_ref[...] += sg
                dbias_ref[...] += sb

        dhm = _dot_nt(df_scr[...], wd_ref[...])
        da_ref[...] = (dhm * a_ref[...].astype(F32)).astype(BF16)
        db_ref[...] = (dhm * b_ref[...].astype(F32)).astype(BF16)

    row = pl.BlockSpec((tm, D), lambda i, j: (i, 0))
    vec = pl.BlockSpec((1, D), lambda i, j: (0, 0))
    mid = pl.BlockSpec((tm, tn), lambda i, j: (i, j))
    ins = [dh_a] + ([dh_b] if two else []) + [xhat, rstd, ln_g, a, b, wd]
    in_specs = [row] * (2 if two else 1) + [row, pl.BlockSpec((tm, 1), lambda i, j: (i, 0)), vec, mid, mid,
                                            pl.BlockSpec((tn, D), lambda i, j: (j, 0))]
    return pl.pallas_call(
        body,
        name=name,
        grid=(S // tm, F // tn),
        in_specs=in_specs,
        out_specs=[row, row, mid, mid, vec, vec],
        out_shape=[
            jax.ShapeDtypeStruct((S, D), F32),
            jax.ShapeDtypeStruct((S, D), BF16),
            jax.ShapeDtypeStruct((S, F), BF16),
            jax.ShapeDtypeStruct((S, F), BF16),
            jax.ShapeDtypeStruct((1, D), F32),
            jax.ShapeDtypeStruct((1, D), F32),
        ],
        scratch_shapes=[pltpu.VMEM((tm, D), BF16)],
        compiler_params=_cp(("arbitrary", "arbitrary"), 56),
    )(*ins)


def _ffn_bwd_dx(dr, da, db, wgt, wut, name, tm=256, tk=F, dep=None):
    nk = F // tk
    n_dep = 0 if dep is None else 1

    def body(dr_ref, da_ref, db_ref, wg_ref, wu_ref, *rest):
        dx_ref, acc = rest[n_dep], rest[n_dep + 1]
        kk = pl.program_id(1)
        part = _dot_nn(da_ref[...], wg_ref[...]) + _dot_nn(db_ref[...], wu_ref[...])

        @pl.when(kk == 0)
        def _():
            acc[...] = ALPHA * dr_ref[...] + part

        @pl.when(kk > 0)
        def _():
            acc[...] += part

        @pl.when(kk == nk - 1)
        def _():
            dx_ref[...] = acc[...]

    row = pl.BlockSpec((tm, D), lambda i, kk: (i, 0))
    mid = pl.BlockSpec((tm, tk), lambda i, kk: (i, kk))
    wsp = pl.BlockSpec((tk, D), lambda i, kk: (kk, 0))
    return pl.pallas_call(
        body,
        name=name,
        grid=(S // tm, nk),
        in_specs=[row, mid, mid, wsp, wsp] + [pl.BlockSpec(memory_space=pl.ANY)] * n_dep,
        out_specs=row,
        out_shape=jax.ShapeDtypeStruct((S, D), F32),
        scratch_shapes=[pltpu.VMEM((tm, D), F32)],
        compiler_params=_cp(("parallel", "arbitrary"), 56),
    )(dr, da, db, wgt, wut, *([] if dep is None else [dep]))


def _rope_tables(pos_col, invf, sign, tm=512):
    def body(p_ref, f_ref, s_ref, c_out, s_out):
        ang = p_ref[...].astype(F32) * f_ref[...]
        c_out[...] = jnp.cos(ang)
        s_out[...] = jnp.sin(ang) * s_ref[...]

    vec = pl.BlockSpec((1, BLK), lambda i: (0, 0))
    out = pl.BlockSpec((tm, BLK), lambda i: (i, 0))
    return pl.pallas_call(
        body,
        name="rope_tables",
        grid=(S // tm,),
        in_specs=[pl.BlockSpec((tm, 1), lambda i: (i, 0)), vec, vec],
        out_specs=[out, out],
        out_shape=[jax.ShapeDtypeStruct((S, BLK), F32)] * 2,
        compiler_params=_cp(("parallel",)),
    )(pos_col, invf, sign)


def _lane_lo(rows=BLK):
    return (lax.broadcasted_iota(jnp.int32, (rows, GW), 1) % HEAD_DIM) < (HEAD_DIM // 2)


def _swap_halves(t, lo):
    return jnp.where(lo, pltpu.roll(t, GW - HEAD_DIM // 2, 1), pltpu.roll(t, HEAD_DIM // 2, 1))


def _rope(t, cosf, sinf, lo):
    return t * cosf + _swap_halves(t, lo) * sinf


def _unrope(g, cosf, sinf, lo):
    return g * cosf + _swap_halves(g * sinf, lo)


def _tile4(v):
    return jnp.concatenate([v, v, v, v], axis=1)


def _band_mask(n):
    qi = lax.broadcasted_iota(jnp.int32, (BLK, 2 * BLK), 0)
    kj = lax.broadcasted_iota(jnp.int32, (BLK, 2 * BLK), 1)
    dist = qi + BLK - kj
    return (dist >= 0) & (dist <= BLK) & ((kj >= BLK) | (n >= 1))


ROWS = 256
LANES = 128


def _to_view(scr, y, dst_ref, d, dtype, col0=0):
    if d == 1:
        dst_ref[:, col0:col0 + GW] = y.astype(dtype)
        return
    for cb in range(GW // LANES):
        scr[cb][...] = y[:, cb * LANES:(cb + 1) * LANES]
    for r in range(d):
        for cb in range(GW // LANES):
            at = col0 + r * GW + cb * LANES
            dst_ref[:, at:at + LANES] = scr[cb][pl.ds(r, ROWS // d, stride=d), :].astype(dtype)


def _from_view(scr, src_ref, d):
    if d == 1:
        return src_ref[...].astype(F32)
    for r in range(d):
        for cb in range(GW // LANES):
            at = r * GW + cb * LANES
            scr[cb][pl.ds(r, ROWS // d, stride=d), :] = src_ref[:, at:at + LANES].astype(F32)
    return jnp.concatenate([scr[cb][...] for cb in range(GW // LANES)], axis=1)


def _view_spec(d):
    return pl.BlockSpec((ROWS // d, d * GW), lambda i: (i, 0))


_VIEW_SCRATCH = [pltpu.VMEM((ROWS, LANES), F32)] * (GW // LANES)


def _qkv_prep(qkv, cos_t, sin_t):
    def body(x_ref, c_ref, s_ref, *rest):
        outs, scr = rest[:9], rest[9:]
        lo = _lane_lo(ROWS)
        cf, sf = _tile4(c_ref[...]), _tile4(s_ref[...])
        for gi, d in enumerate(PATTERN_DILATIONS):
            q, k, v = (x_ref[:, (3 * part + gi) * GW:(3 * part + gi + 1) * GW].astype(F32) for part in range(3))
            _to_view(scr, _rope(q, cf, sf, lo) * (HEAD_DIM ** -0.5), outs[3 * gi], d, BF16)
            _to_view(scr, _rope(k, cf, sf, lo), outs[3 * gi + 1], d, BF16)
            _to_view(scr, v, outs[3 * gi + 2], d, BF16)

    tab = pl.BlockSpec((ROWS, BLK), lambda i: (i, 0))
    outs = pl.pallas_call(
        body,
        name="qkv_prep",
        grid=(S // ROWS,),
        in_specs=[pl.BlockSpec((ROWS, QKV_W), lambda i: (i, 0)), tab, tab],
        out_specs=[_view_spec(d) for d in PATTERN_DILATIONS for _ in range(3)],
        out_shape=[jax.ShapeDtypeStruct((S // d, d * GW), BF16) for d in PATTERN_DILATIONS for _ in range(3)],
        scratch_shapes=_VIEW_SCRATCH,
        compiler_params=_cp(("parallel",), 48),
    )(qkv, cos_t, sin_t)
    return [tuple(outs[3 * gi:3 * gi + 3]) for gi in range(3)]


N_ATTN_STEPS = S // BLK


def _class_and_block(gi, t):
    per_class = N_ATTN_STEPS // PATTERN_DILATIONS[gi]
    return t >> (per_class.bit_length() - 1), t & (per_class - 1)


def _attn_specs(gi):
    def cur(t):
        r, n = _class_and_block(gi, t)
        return n, r

    def prev(t):
        r, n = _class_and_block(gi, t)
        return jnp.maximum(n - 1, 0), r

    def whole(t):
        return 0, _class_and_block(gi, t)[0]

    sub = S // PATTERN_DILATIONS[gi]
    return (pl.BlockSpec((BLK, GW), cur), pl.BlockSpec((BLK, GW), prev), pl.BlockSpec((sub, GW), whole))


def _left_lanes():
    return lax.broadcasted_iota(jnp.int32, (BLK, LANES), 1) < HEAD_DIM


def _stack_heads(t2, left):
    zero = jnp.zeros_like(t2)
    return jnp.concatenate([jnp.where(left, t2, zero), jnp.where(left, zero, t2)], axis=0)


def _attn_fwd_block(n, q_ref, kc_ref, kp_ref, vc_ref, vp_ref, o_ref, l_ref):
    mask = _band_mask(n)
    left = _left_lanes()
    zero = jnp.zeros((BLK, LANES), BF16)
    for pr in range(GW // LANES):
        cs = slice(pr * LANES, (pr + 1) * LANES)
        q2 = q_ref[:, cs]
        k2 = jnp.concatenate([kp_ref[:, cs], kc_ref[:, cs]], axis=0)
        v2 = jnp.concatenate([vp_ref[:, cs], vc_ref[:, cs]], axis=0)
        o_h, lse_h = [], []
        for side in (left, ~left):
            s = jnp.where(mask, _dot_nt(jnp.where(side, q2, zero), k2), NEG)
            m = jnp.max(s, axis=1, keepdims=True)
            p = jnp.exp(s - m)
            l = jnp.sum(p, axis=1, keepdims=True)
            o_h.append(_dot_nn((p / l).astype(BF16), v2))
            lse_h.append(m + jnp.log(l))
        o_ref[:, cs] = jnp.where(left, o_h[0], o_h[1]).astype(BF16)
        l_ref[:, cs] = jnp.where(left, lse_h[0], lse_h[1])


def _attn_fwd(preps):
    def body(*refs):
        t = pl.program_id(0)
        for gi in range(3):
            _attn_fwd_block(_class_and_block(gi, t)[1], *refs[5 * gi:5 * gi + 5], *refs[15 + 2 * gi:17 + 2 * gi])

    in_specs, out_specs, out_shape, args = [], [], [], []
    for gi, d in enumerate(PATTERN_DILATIONS):
        cur, prev, _ = _attn_specs(gi)
        qr, kr, vv = preps[gi]
        in_specs += [cur, cur, prev, cur, prev]
        args += [qr, kr, kr, vv, vv]
        out_specs += [cur, cur]
        out_shape += [jax.ShapeDtypeStruct((S // d, d * GW), BF16), jax.ShapeDtypeStruct((S // d, d * GW), F32)]
    outs = pl.pallas_call(
        body,
        name="attn_fwd",
        grid=(N_ATTN_STEPS,),
        in_specs=in_specs,
        out_specs=out_specs,
        out_shape=out_shape,
        compiler_params=_cp(("arbitrary",), 48),
    )(*args)
    return [outs[0], outs[2], outs[4]], [outs[1], outs[3], outs[5]]


def _attn_bwd_block(n, q_ref, kc_ref, kp_ref, vc_ref, vp_ref, do_ref, l_ref, c_ref, dq_ref, dk_ref, dv_ref):
    mask = _band_mask(n)
    left = _left_lanes()
    left2 = jnp.concatenate([left, left], axis=0)
    here = pl.ds(pl.multiple_of(n * BLK, BLK), BLK)
    before = pl.ds(pl.multiple_of(jnp.maximum(n - 1, 0) * BLK, BLK), BLK)
    for pr in range(GW // LANES):
        cs = slice(pr * LANES, (pr + 1) * LANES)
        q2 = q_ref[:, cs]
        do2 = do_ref[:, cs]
        k_bd = _stack_heads(jnp.concatenate([kp_ref[:, cs], kc_ref[:, cs]], axis=0), left2)
        v_bd = _stack_heads(jnp.concatenate([vp_ref[:, cs], vc_ref[:, cs]], axis=0), left2)
        s2 = _dot_nt(q2, k_bd)
        dp2 = _dot_nt(do2, v_bd)
        ps, dss = [], []
        for h in range(2):
            at = pr * LANES + h * HEAD_DIM
            half = slice(h * 2 * BLK, (h + 1) * 2 * BLK)
            p = jnp.exp(jnp.where(mask, s2[:, half], NEG) - l_ref[:, at:at + 1])
            ps.append(p.astype(BF16))
            dss.append((p * (dp2[:, half] - c_ref[:, at:at + 1])).astype(BF16))
        ds_both = jnp.concatenate(dss, axis=1)
        dq_ref[:, cs] = _dot_nn(ds_both, k_bd) * (HEAD_DIM ** -0.5)
        dk_bd = _dot_tn(ds_both, q2)
        dv_bd = _dot_tn(jnp.concatenate(ps, axis=1), do2)
        dk2 = jnp.where(left2, dk_bd[:2 * BLK], dk_bd[2 * BLK:])
        dv2 = jnp.where(left2, dv_bd[:2 * BLK], dv_bd[2 * BLK:])
        dk_ref[here, cs] = dk2[BLK:]
        dv_ref[here, cs] = dv2[BLK:]
        dk_ref[before, cs] += dk2[:BLK]
        dv_ref[before, cs] += dv2[:BLK]


def _attn_bwd(preps, dos, lses, cterms):
    def body(*refs):
        t = pl.program_id(0)
        for gi in range(3):
            _attn_bwd_block(_class_and_block(gi, t)[1], *refs[8 * gi:8 * gi + 8], *refs[24 + 3 * gi:27 + 3 * gi])

    in_specs, out_specs, out_shape, args = [], [], [], []
    for gi, d in enumerate(PATTERN_DILATIONS):
        cur, prev, whole = _attn_specs(gi)
        qr, kr, vv = preps[gi]
        in_specs += [cur, cur, prev, cur, prev, cur, cur, cur]
        args += [qr, kr, kr, vv, vv, dos[gi], lses[gi], cterms[gi]]
        out_specs += [cur, whole, whole]
        out_shape += [jax.ShapeDtypeStruct((S // d, d * GW), F32)] * 3
    outs = pl.pallas_call(
        body,
        name="attn_bwd",
        grid=(N_ATTN_STEPS,),
        in_specs=in_specs,
        out_specs=out_specs,
        out_shape=out_shape,
        compiler_params=_cp(("arbitrary",), 56),
    )(*args)
    return [tuple(outs[3 * gi:3 * gi + 3]) for gi in range(3)]


def _qkv_unprep(grads, cos_t, sin_t, dproj):
    def body(*refs):
        views, (c_ref, s_ref, _, out_ref), scr = refs[:9], refs[9:13], refs[13:]
        lo = _lane_lo(ROWS)
        cf, sf = _tile4(c_ref[...]), _tile4(s_ref[...])
        for gi, d in enumerate(PATTERN_DILATIONS):
            dq_ref, dk_ref, dv_ref = views[3 * gi:3 * gi + 3]
            qa, ka, va = gi * GW, (3 + gi) * GW, (6 + gi) * GW
            out_ref[:, qa:qa + GW] = _unrope(_from_view(scr, dq_ref, d), cf, sf, lo).astype(BF16)
            out_ref[:, ka:ka + GW] = _unrope(_from_view(scr, dk_ref, d), cf, sf, lo).astype(BF16)
            out_ref[:, va:va + GW] = _from_view(scr, dv_ref, d).astype(BF16)

    tab = pl.BlockSpec((ROWS, BLK), lambda i: (i, 0))
    return pl.pallas_call(
        body,
        name="qkv_unprep",
        grid=(S // ROWS,),
        in_specs=[_view_spec(d) for d in PATTERN_DILATIONS for _ in range(3)] + [tab, tab, pl.BlockSpec(memory_space=pl.ANY)],
        out_specs=pl.BlockSpec((ROWS, QKV_W), lambda i: (i, 0)),
        out_shape=jax.ShapeDtypeStruct(dproj.shape, dproj.dtype),
        input_output_aliases={11: 0},
        scratch_shapes=_VIEW_SCRATCH,
        compiler_params=_cp(("parallel",), 48),
    )(*[g for grp in grads for g in grp], cos_t, sin_t, dproj)


def _group_weights(l0, l1, l2):
    mx = jnp.maximum(jnp.maximum(l0, l1), l2)
    e0, e1, e2 = jnp.exp(l0 - mx), jnp.exp(l1 - mx), jnp.exp(l2 - mx)
    inv = 1.0 / (e0 + e1 + e2)
    return e0 * inv, e1 * inv, e2 * inv


def _combine_fwd(os_, lses):
    def body(o0, o1, o2, l0, l1, l2, y_ref, *scr):
        ov = [_from_view(scr, o, d) for o, d in zip((o0, o1, o2), PATTERN_DILATIONS)]
        lv = [_from_view(scr, l, d) for l, d in zip((l0, l1, l2), PATTERN_DILATIONS)]
        w0, w1, w2 = _group_weights(*lv)
        y_ref[...] = (w0 * ov[0] + w1 * ov[1] + w2 * ov[2]).astype(BF16)

    views = [_view_spec(d) for d in PATTERN_DILATIONS]
    return pl.pallas_call(
        body,
        name="attn_combine_fwd",
        grid=(S // ROWS,),
        in_specs=views + views,
        out_specs=pl.BlockSpec((ROWS, GW), lambda i: (i, 0)),
        out_shape=jax.ShapeDtypeStruct((S, GW), BF16),
        scratch_shapes=_VIEW_SCRATCH,
        compiler_params=_cp(("parallel",)),
    )(*os_, *lses)


def _combine_bwd(dy, os_, lses, seg):
    def body(dy_ref, o0, o1, o2, l0, l1, l2, seg_ref, d0, d1, d2, c0, c1, c2, *scr):
        ov = [_from_view(scr, o, d) for o, d in zip((o0, o1, o2), PATTERN_DILATIONS)]
        lv = [_from_view(scr, l, d) for l, d in zip((l0, l1, l2), PATTERN_DILATIONS)]
        ws = _group_weights(*lv)
        dyv = dy_ref[...]
        t = dyv * (ws[0] * ov[0] + ws[1] * ov[1] + ws[2] * ov[2])
        t_hi = t.astype(BF16)
        r1 = t - t_hi.astype(F32)
        t_mid = r1.astype(BF16)
        t_lo = (r1 - t_mid.astype(F32)).astype(BF16)
        sg = seg_ref[...]
        e = _dot_nn(t_hi, sg) + _dot_nn(t_mid, sg) + _dot_nn(t_lo, sg)
        for w, do_ref, c_ref, d in zip(ws, (d0, d1, d2), (c0, c1, c2), PATTERN_DILATIONS):
            _to_view(scr, w * dyv, do_ref, d, BF16)
            _to_view(scr, w * e, c_ref, d, F32)

    views = [_view_spec(d) for d in PATTERN_DILATIONS]
    return pl.pallas_call(
        body,
        name="attn_combine_bwd",
        grid=(S // ROWS,),
        in_specs=[pl.BlockSpec((ROWS, GW), lambda i: (i, 0))] + views + views + [pl.BlockSpec((GW, GW), lambda i: (0, 0))],
        out_specs=views + views,
        out_shape=[jax.ShapeDtypeStruct((S // d, d * GW), BF16) for d in PATTERN_DILATIONS]
        + [jax.ShapeDtypeStruct((S // d, d * GW), F32) for d in PATTERN_DILATIONS],
        scratch_shapes=_VIEW_SCRATCH,
        compiler_params=_cp(("parallel",)),
    )(dy, *os_, *lses, seg)


_SQRT_HALF = 0.7071067811865476
_INV_SQRT_2PI = 0.3989422804014327


def _gelu(z):
    return 0.5 * z * (1.0 + lax.erf(z * _SQRT_HALF))


def _gelu_grad(z):
    return 0.5 * (1.0 + lax.erf(z * _SQRT_HALF)) + z * (_INV_SQRT_2PI * jnp.exp(-0.5 * z * z))


def _tril_ws(ws_ref, g):
    t = lax.broadcasted_iota(jnp.int32, (BLK, BLK), 0)
    s = lax.broadcasted_iota(jnp.int32, (BLK, BLK), 1)
    return jnp.where(t >= s, ws_ref[g], 0.0)


def _chunks_side_by_side(x, cols, nch):
    return jnp.concatenate([x[c * BLK:(c + 1) * BLK, cols] for c in range(nch)], axis=1)


def _gmlp_fwd(z, ws, bst, ln_g, ln_b, tm=512):
    nch = tm // BLK

    def body(z_ref, ws_ref, b_ref, g_ref, be_ref, y_ref):
        zg = _gelu(z_ref[...].astype(F32))
        u = zg[:, :D]
        vn, _, _ = _ln_fwd(zg[:, D:], g_ref[...], be_ref[...])
        vnb = vn.astype(BF16)
        bt = b_ref[...]
        for g in range(8):
            w = _tril_ws(ws_ref, g).astype(BF16)
            cols = slice(g * BLK, (g + 1) * BLK)
            mixed = _dot_nn(w, _chunks_side_by_side(vnb, cols, nch)) + bt[:, g:g + 1]
            for c in range(nch):
                rows = slice(c * BLK, (c + 1) * BLK)
                y_ref[rows, cols] = (u[rows, cols] * mixed[:, c * BLK:(c + 1) * BLK]).astype(BF16)

    return pl.pallas_call(
        body,
        name="gmlp_fwd",
        grid=(S // tm,),
        in_specs=[
            pl.BlockSpec((tm, 2 * D), lambda i: (i, 0)),
            pl.BlockSpec((8, BLK, BLK), lambda i: (0, 0, 0)),
            pl.BlockSpec((BLK, 8), lambda i: (0, 0)),
            pl.BlockSpec((1, D), lambda i: (0, 0)),
            pl.BlockSpec((1, D), lambda i: (0, 0)),
        ],
        out_specs=pl.BlockSpec((tm, D), lambda i: (i, 0)),
        out_shape=jax.ShapeDtypeStruct((S, D), BF16),
        compiler_params=_cp(("parallel",), 48),
    )(z, ws, bst, ln_g, ln_b)


def _gmlp_bwd(z, dy, ws, bst, ln_g, ln_b, dproj, tm=512):
    nch = tm // BLK

    def body(z_ref, dy_ref, ws_ref, b_ref, g_ref, be_ref, _, dz_ref, dws_ref, dbs_ref, dg_ref, dbe_ref, dvn_scr, dm_acc):
        i = pl.program_id(0)
        zv = z_ref[...].astype(F32)
        zg = _gelu(zv)
        u = zg[:, :D]
        gam = g_ref[...]
        vn, xhat, rstd = _ln_fwd(zg[:, D:], gam, be_ref[...])
        vnb = vn.astype(BF16)
        dyv = dy_ref[...]
        dmix = dyv * u
        dmb = dmix.astype(BF16)
        bt = b_ref[...]
        tmask = lax.broadcasted_iota(jnp.int32, (BLK, BLK), 0) >= lax.broadcasted_iota(jnp.int32, (BLK, BLK), 1)
        dm_sum = dmix[0:BLK]
        for c in range(1, nch):
            dm_sum = dm_sum + dmix[c * BLK:(c + 1) * BLK]

        @pl.when(i == 0)
        def _():
            dm_acc[...] = jnp.zeros_like(dm_acc)
            dws_ref[...] = jnp.zeros_like(dws_ref)
            dg_ref[...] = jnp.zeros_like(dg_ref)
            dbe_ref[...] = jnp.zeros_like(dbe_ref)

        dm_acc[...] += dm_sum
        dus = []
        for g in range(8):
            w = _tril_ws(ws_ref, g).astype(BF16)
            cols = slice(g * BLK, (g + 1) * BLK)
            v_cat = _chunks_side_by_side(vnb, cols, nch)
            dm_cat = _chunks_side_by_side(dmb, cols, nch)
            mixed = _dot_nn(w, v_cat) + bt[:, g:g + 1]
            dus.append(jnp.concatenate(
                [dyv[c * BLK:(c + 1) * BLK, cols] * mixed[:, c * BLK:(c + 1) * BLK] for c in range(nch)], axis=0))
            dws_ref[g] += jnp.where(tmask, _dot_nt(dm_cat, v_cat), 0.0)
            dvn_cat = _dot_tn(w, dm_cat)
            for c in range(nch):
                dvn_scr[c * BLK:(c + 1) * BLK, cols] = dvn_cat[:, c * BLK:(c + 1) * BLK]

        dvn = dvn_scr[...]
        dg_ref[...] += _colsum(dvn * xhat)
        dbe_ref[...] += _colsum(dvn)
        dvg = _ln_bwd(dvn, xhat, rstd, gam)
        gp = _gelu_grad(zv)
        dz_ref[:, :D] = (jnp.concatenate(dus, axis=1) * gp[:, :D]).astype(BF16)
        dz_ref[:, D:] = (dvg * gp[:, D:]).astype(BF16)

        @pl.when(i == S // tm - 1)
        def _():
            acc = dm_acc[...]
            for g in range(8):
                dbs_ref[:, g:g + 1] = jnp.sum(acc[:, g * BLK:(g + 1) * BLK], axis=1, keepdims=True)

    vec = pl.BlockSpec((1, D), lambda i: (0, 0))
    return pl.pallas_call(
        body,
        name="gmlp_bwd",
        grid=(S // tm,),
        in_specs=[
            pl.BlockSpec((tm, 2 * D), lambda i: (i, 0)),
            pl.BlockSpec((tm, D), lambda i: (i, 0)),
            pl.BlockSpec((8, BLK, BLK), lambda i: (0, 0, 0)),
            pl.BlockSpec((BLK, 8), lambda i: (0, 0)),
            vec,
            vec,
            pl.BlockSpec(memory_space=pl.ANY),
        ],
        out_specs=[
            pl.BlockSpec((tm, 2 * D), lambda i: (i, DPROJ_Z_COL)),
            pl.BlockSpec((8, BLK, BLK), lambda i: (0, 0, 0)),
            pl.BlockSpec((BLK, 8), lambda i: (0, 0)),
            vec,
            vec,
        ],
        out_shape=[
            jax.ShapeDtypeStruct(dproj.shape, dproj.dtype),
            jax.ShapeDtypeStruct((8, BLK, BLK), F32),
            jax.ShapeDtypeStruct((BLK, 8), F32),
            jax.ShapeDtypeStruct((1, D), F32),
            jax.ShapeDtypeStruct((1, D), F32),
        ],
        input_output_aliases={6: 0},
        scratch_shapes=[pltpu.VMEM((tm, D), F32), pltpu.VMEM((BLK, D), F32)],
        compiler_params=_cp(("arbitrary",), 48),
    )(z, dy, ws, bst, ln_g, ln_b, dproj)


def _merge_fwd(ya, yg, glog, bgate, h1, wabt, wgb, wo, ln_g, ln_b, tm=256):
    def body(ya_ref, yg_ref, gl_ref, bg_ref, h1_ref, wab_ref, wgb_ref, wo_ref, g_ref, b_ref,
             h_ref, hb_ref, xh_ref, rs_ref, mg_ref, bra_ref, brg_ref):
        bra = _dot_nt(ya_ref[...], wab_ref[...])
        brg = _dot_nn(yg_ref[...], wgb_ref[...])
        gates = _sigmoid(gl_ref[...].astype(F32) + bg_ref[...])
        merged = (gates[:, :D] * bra + gates[:, D:] * brg).astype(BF16)
        mix = _dot_nn(merged, wo_ref[...])
        h, xhat, rstd = _ln_fwd(ALPHA * h1_ref[...] + mix, g_ref[...], b_ref[...])
        h_ref[...] = h
        hb_ref[...] = h.astype(BF16)
        xh_ref[...] = xhat
        rs_ref[...] = rstd
        mg_ref[...] = merged
        bra_ref[...] = bra.astype(BF16)
        brg_ref[...] = brg.astype(BF16)

    row = pl.BlockSpec((tm, D), lambda i: (i, 0))
    vec = pl.BlockSpec((1, D), lambda i: (0, 0))
    full = lambda shape: pl.BlockSpec(shape, lambda i: (0, 0))
    return pl.pallas_call(
        body,
        name="merge_fwd",
        grid=(S // tm,),
        in_specs=[
            pl.BlockSpec((tm, GW), lambda i: (i, 0)), row,
            pl.BlockSpec((tm, 2 * D), lambda i: (i, glog.shape[1] // (2 * D) - 1)),
            full((1, 2 * D)), row,
            full((D, GW)), full((D, D)), full((D, D)), vec, vec,
        ],
        out_specs=[row, row, row, pl.BlockSpec((tm, 1), lambda i: (i, 0)), row, row, row],
        out_shape=[
            jax.ShapeDtypeStruct((S, D), F32),
            jax.ShapeDtypeStruct((S, D), BF16),
            jax.ShapeDtypeStruct((S, D), F32),
            jax.ShapeDtypeStruct((S, 1), F32),
            jax.ShapeDtypeStruct((S, D), BF16),
            jax.ShapeDtypeStruct((S, D), BF16),
            jax.ShapeDtypeStruct((S, D), BF16),
        ],
        compiler_params=_cp(("parallel",), 48),
    )(ya, yg, glog, bgate, h1, wabt, wgb, wo, ln_g, ln_b)


def _merge_bwd(dh2, xhat, rstd, ln_g, bra, brg, glog, bgate, wabt, wgb, wo, tm=256):
    def body(dh_ref, xh_ref, rs_ref, g_ref, bra_ref, brg_ref, gl_ref, bg_ref, wab_ref, wgb_ref, wo_ref,
             dr_ref, drb_ref, dlog_ref, dba_ref, dbg_ref, dya_ref, dyg_ref, dbgate_ref, dg_ref, dbias_ref):
        i = pl.program_id(0)
        dh = dh_ref[...]
        xh = xh_ref[...]
        dr = _ln_bwd(dh, xh, rs_ref[...], g_ref[...])
        drb = dr.astype(BF16)
        dr_ref[...] = dr
        drb_ref[...] = drb
        dmerged = _dot_nt(drb, wo_ref[...])
        gates = _sigmoid(gl_ref[...].astype(F32) + bg_ref[...])
        g0, g1 = gates[:, :D], gates[:, D:]
        dl0 = dmerged * bra_ref[...].astype(F32) * g0 * (1.0 - g0)
        dl1 = dmerged * brg_ref[...].astype(F32) * g1 * (1.0 - g1)
        dlog_ref[:, :D] = dl0.astype(BF16)
        dlog_ref[:, D:] = dl1.astype(BF16)
        dba = (dmerged * g0).astype(BF16)
        dbg = (dmerged * g1).astype(BF16)
        dba_ref[...] = dba
        dbg_ref[...] = dbg
        dya_ref[...] = _dot_nn(dba, wab_ref[...])
        dyg_ref[...] = _dot_nt(dbg, wgb_ref[...])
        s0, s1 = _colsum(dl0), _colsum(dl1)
        sg, sb = _colsum(dh * xh), _colsum(dh)

        @pl.when(i == 0)
        def _():
            dbgate_ref[:, :D] = s0
            dbgate_ref[:, D:] = s1
            dg_ref[...] = sg
            dbias_ref[...] = sb

        @pl.when(i > 0)
        def _():
            dbgate_ref[:, :D] += s0
            dbgate_ref[:, D:] += s1
            dg_ref[...] += sg
            dbias_ref[...] += sb

    row = pl.BlockSpec((tm, D), lambda i: (i, 0))
    vec = pl.BlockSpec((1, D), lambda i: (0, 0))
    wide = pl.BlockSpec((tm, 2 * D), lambda i: (i, 0))
    full = lambda shape: pl.BlockSpec(shape, lambda i: (0, 0))
    return pl.pallas_call(
        body,
        name="merge_bwd",
        grid=(S // tm,),
        in_specs=[row, row, pl.BlockSpec((tm, 1), lambda i: (i, 0)), vec, row, row,
                  pl.BlockSpec((tm, 2 * D), lambda i: (i, glog.shape[1] // (2 * D) - 1)),
                  full((1, 2 * D)), full((D, GW)), full((D, D)), full((D, D))],
        out_specs=[row, row, pl.BlockSpec((tm, 2 * D), lambda i: (i, DPROJ_G_COL)), row, row,
                   pl.BlockSpec((tm, GW), lambda i: (i, 0)), row, full((1, 2 * D)), vec, vec],
        out_shape=[
            jax.ShapeDtypeStruct((S, D), F32),
            jax.ShapeDtypeStruct((S, D), BF16),
            jax.ShapeDtypeStruct((S, DPROJ_W), BF16),
            jax.ShapeDtypeStruct((S, D), BF16),
            jax.ShapeDtypeStruct((S, D), BF16),
            jax.ShapeDtypeStruct((S, GW), F32),
            jax.ShapeDtypeStruct((S, D), F32),
            jax.ShapeDtypeStruct((1, 2 * D), F32),
            jax.ShapeDtypeStruct((1, D), F32),
            jax.ShapeDtypeStruct((1, D), F32),
        ],
        compiler_params=_cp(("arbitrary",), 48),
    )(dh2, xhat, rstd, ln_g, bra, brg, glog, bgate, wabt, wgb, wo)


def _dproj_to_dh1(dproj, win, dep, tm=512):
    n_dep = 0 if dep is None else 1

    def body(a_ref, b_ref, *rest):
        rest[n_dep][...] = (_dot_nn(a_ref[:, :QKV_W], b_ref[:QKV_W, :])
                            + _dot_nn(a_ref[:, DPROJ_ZG_AT:], b_ref[QKV_W:, :]))

    return pl.pallas_call(
        body,
        name="dproj_to_dh1",
        grid=(S // tm,),
        in_specs=[pl.BlockSpec((tm, DPROJ_W), lambda i: (i, 0)),
                  pl.BlockSpec((IN_W, D), lambda i: (0, 0), pipeline_mode=pl.Buffered(1))]
        + [pl.BlockSpec(memory_space=pl.ANY)] * n_dep,
        out_specs=pl.BlockSpec((tm, D), lambda i: (i, 0)),
        out_shape=jax.ShapeDtypeStruct((S, D), F32),
        compiler_params=_cp(("parallel",), 56),
    )(dproj, win, *([] if dep is None else [dep]))


def _tie(x, dep):
    if dep is None:
        return x
    return x + dep[0, 0].astype(x.dtype)


def _local_step(x, pos_col, target, get_w, p, emit):
    w = get_w("ffn1", None)
    a1, b1, hm1 = _ffn_up(x, w["g1"], w["u1"], "ffn1_up")
    w.update(get_w("ffn1d", hm1))
    h1, h1b, xh1, rs1 = _ffn_down(hm1, w["d1"], x, p["ln1_g"], p["ln1_b"], "ffn1_down")

    w.update(get_w("win", h1b))
    qkv = _matmul(h1b, w["win"], "nt", BF16, S, 1536, D, "proj_qkv", b_off=0, n_out=QKV_W)
    z = glog = _matmul(h1b, w["win"], "nt", BF16, S, 512, D, "proj_zg", b_off=QKV_W // 512, n_out=4 * D)

    half = jnp.arange(0, HEAD_DIM, 2, dtype=F32) / HEAD_DIM
    inv_freq = ROPE_THETA ** (-half)
    invf = jnp.tile(inv_freq, 4).reshape(1, BLK)
    sign = jnp.tile(jnp.concatenate([-jnp.ones((32,), F32), jnp.ones((32,), F32)]), 2).reshape(1, BLK)
    cos_t, sin_t = _rope_tables(pos_col, invf, sign)

    preps = _qkv_prep(qkv, cos_t, sin_t)
    os_, lses = _attn_fwd(preps)
    ya = _combine_fwd(os_, lses)
    get_w("late", ya, early=True)
    bst = p["gmlp_b_s"].T
    yg = _gmlp_fwd(z, p["gmlp_w_s"], bst, p["gmlp_ln_g"], p["gmlp_ln_b"])
    w.update(get_w("late", yg))
    h2, h2b, xh2, rs2, merged, bra, brg = _merge_fwd(ya, yg, glog, p["b_gates"], h1, w["ab"], w["gb"], w["o"],
                                                      p["ln2_g"], p["ln2_b"])
    a2, b2, hm2 = _ffn_up(h2b, w["g2"], w["u2"], "ffn2_up")
    dh3, loss, xh3, rs3 = _ffn_down(hm2, w["d2"], h2, p["ln3_g"], p["ln3_b"], "ffn2_down", target=target)

    gp = {}
    dr3, df2, da2, db2, gp["ln3_g"], gp["ln3_b"] = _ffn_bwd_mid(dh3, None, xh3, rs3, p["ln3_g"], a2, b2, w["d2"],
                                                                "ffn2_bwd_mid")
    tok = emit("ffn2", {
        "g2": _matmul(da2, h2b, "tn", BF16, 1408, D, S, "wgrad_g2"),
        "u2": _matmul(db2, h2b, "tn", BF16, 1408, D, S, "wgrad_u2"),
        "d2": _matmul(hm2, df2, "tn", BF16, 1408, D, S, "wgrad_d2")})
    dh2 = _ffn_bwd_dx(dr3, da2, db2, w["g2"], w["u2"], "ffn2_bwd_dx")

    (dr2, dr2b, dproj, dba, dbg, dya, dyg, gp["b_gates"], gp["ln2_g"], gp["ln2_b"]) = _merge_bwd(
        dh2, xh2, rs2, _tie(p["ln2_g"], tok), bra, brg, glog, p["b_gates"], w["ab"], w["gb"], w["o"])
    tok = emit("mix", {
        "o": _matmul(merged, dr2b, "tn", BF16, 512, D, S, "wgrad_o"),
        "ab": _matmul(dba, ya, "tn", BF16, 512, GW, S, "wgrad_ab"),
        "gb": _matmul(yg, dbg, "tn", BF16, 512, D, S, "wgrad_gb")})

    seg = (jnp.arange(GW)[:, None] // HEAD_DIM == jnp.arange(GW)[None, :] // HEAD_DIM).astype(BF16)
    do0, do1, do2, c0, c1, c2 = _combine_bwd(dya, os_, lses, _tie(seg, tok))
    dproj = _qkv_unprep(_attn_bwd(preps, (do0, do1, do2), lses, (c0, c1, c2)), cos_t, sin_t, dproj)
    dproj, gp["gmlp_w_s"], dbst, gp["gmlp_ln_g"], gp["gmlp_ln_b"] = _gmlp_bwd(
        z, dyg, p["gmlp_w_s"], bst, p["gmlp_ln_g"], p["gmlp_ln_b"], dproj)
    gp["gmlp_b_s"] = dbst.T
    tok = emit("win", {"win": _matmul(dproj, h1b, "tn", BF16, 512, D, S, "wgrad_win", a_map=_dproj_tile, m_out=IN_W)})
    dh1m = _dproj_to_dh1(dproj, w["win"], tok)

    dr1, df1, da1, db1, gp["ln1_g"], gp["ln1_b"] = _ffn_bwd_mid(dr2, dh1m, xh1, rs1, p["ln1_g"], a1, b1, w["d1"],
                                                                "ffn1_bwd_mid")
    tok = emit("small", {**gp, "loss": loss})
    tok = emit("g1", {"g1": _matmul(da1, x, "tn", BF16, 1408, D, S, "wgrad_g1", dep=tok)})
    tok = emit("ud1", {"u1": _matmul(db1, x, "tn", BF16, 1408, D, S, "wgrad_u1", dep=tok),
                       "d1": _matmul(hm1, df1, "tn", BF16, 1408, D, S, "wgrad_d1", dep=tok)})
    dx = _ffn_bwd_dx(dr1, da1, db1, w["g1"], w["u1"], "ffn1_bwd_dx", dep=tok)
    return loss, dx, gp


_FLIPS = [(mx, my, mc) for mx in (0, 1) for my in (0, 1) for mc in (0, 1)][1:]
_GROUPS = {"ffn1": ("g1", "u1"), "ffn1d": ("d1",), "win": ("win",), "late": ("ab", "gb", "o", "g2", "u2", "d2")}
_SCATTERS = {"ffn2": ("g2", "u2", "d2"), "mix": ("ab", "gb", "o"), "win": ("win",), "g1": ("g1",), "ud1": ("u1", "d1")}
_TWO_STAGE = ("g1", "ud1")
_HBM = pl.BlockSpec(memory_space=pltpu.HBM)
_SEM = pl.BlockSpec(memory_space=pltpu.SEMAPHORE)
_EFFECT = pltpu.SideEffectType.DATAFLOW_SIDE_EFFECTING


def _me():
    return 4 * lax.axis_index("x") + 2 * lax.axis_index("y") + lax.axis_index("c")


def _copies_start(bufs, copies, n_sem, name, after=None):
    nb = len(bufs)
    n_after = 0 if after is None else 1

    def body(*refs):
        b = refs[:nb]
        send_sems, recv_sems = refs[nb + n_after], refs[nb + n_after + 1]
        token = refs[-1]
        x, y, c = lax.axis_index("x"), lax.axis_index("y"), lax.axis_index("c")
        me = 4 * x + 2 * y + c
        for si, s_slot, di, d_slot, flip, sem in copies:
            s_idx, d_idx = s_slot(me), d_slot(me)
            mx, my, mc = flip(c) if callable(flip) else flip
            pltpu.make_async_remote_copy(
                src_ref=b[si] if s_idx is None else b[si].at[s_idx],
                dst_ref=b[di] if d_idx is None else b[di].at[d_idx],
                send_sem=send_sems.at[sem], recv_sem=recv_sems.at[sem],
                device_id=(x ^ mx, y ^ my, c ^ mc), device_id_type=MESH).start()
        token[...] = jnp.zeros_like(token)

    ins = [pltpu.with_memory_space_constraint(a, pltpu.HBM) for a in bufs]
    outs = pl.pallas_call(
        body,
        name=name,
        in_specs=[_HBM] * nb + [pl.BlockSpec(memory_space=pl.ANY)] * n_after,
        out_specs=[_SEM, _SEM] + [_HBM] * nb + [pl.BlockSpec(memory_space=pltpu.VMEM)],
        out_shape=[pltpu.SemaphoreType.DMA((n_sem,)), pltpu.SemaphoreType.DMA((n_sem,))]
        + [pltpu.HBM(a.shape, a.dtype) for a in ins] + [jax.ShapeDtypeStruct((8, 128), F32)],
        input_output_aliases={k: 2 + k for k in range(nb)},
        compiler_params=pltpu.CompilerParams(has_side_effects=_EFFECT),
    )(*ins, *([] if after is None else [after]))
    return outs[0], outs[1], list(outs[2:2 + nb]), outs[-1]


def _copies_wait(started, waits, after, name):
    send_sems, recv_sems, bufs, _ = started
    nb = len(bufs)

    def body(*refs):
        b = refs[:nb]
        ss, rs = refs[nb], refs[nb + 1]
        me3 = (lax.axis_index("x"), lax.axis_index("y"), lax.axis_index("c"))
        for bi, n_blocks, sem, is_send in waits:
            blocks = b[bi].at[pl.ds(0, n_blocks)]
            cp = pltpu.make_async_remote_copy(src_ref=blocks, dst_ref=blocks, send_sem=ss.at[sem], recv_sem=rs.at[sem],
                                              device_id=me3, device_id_type=MESH)
            if is_send:
                cp.wait_send()
            else:
                cp.wait_recv()

    return pl.pallas_call(
        body,
        name=name,
        in_specs=[_HBM] * nb + [_SEM, _SEM, pl.BlockSpec(memory_space=pl.ANY)],
        out_specs=[_HBM] * nb,
        out_shape=[pltpu.HBM(a.shape, a.dtype) for a in bufs],
        input_output_aliases={k: k for k in range(nb)},
        compiler_params=pltpu.CompilerParams(has_side_effects=_EFFECT),
    )(*bufs, send_sems, recv_sems, after)


def _landing(own, me):
    return lax.dynamic_update_slice(lax.empty((N_DEV,) + own.shape[1:], own.dtype), own, (me, 0, 0))


_SIBLING = (0, 0, 1)
_OTHER_CHIPS = ((1, 0, 0), (0, 1, 0), (1, 1, 0))


def _bits(flip):
    return 4 * flip[0] + 2 * flip[1] + flip[2]


_X_NBR, _Y_NBR = (1, 0, 0), (0, 1, 0)


def _xor(bits):
    return functools.partial(lambda me, bits: me ^ bits, bits=bits)


def _gather_send(shards, grp, after):
    nw = len(shards)
    me = _me()
    bufs = list(shards) + [_landing(a[None], me) for a in shards]
    copies = []
    for k in range(nw):
        for s, flip in enumerate((_SIBLING, _X_NBR, _Y_NBR)):
            copies.append((k, lambda me: None, nw + k, lambda me: me, flip, 3 * k + s))
    return _copies_start(bufs, copies, 3 * nw, "gather_send_" + grp, after)


def _gather_pass(started, grp, after):
    nw = len(started[2]) // 2
    waits = [(nw + k, 1, 3 * k + s, is_send) for k in range(nw) for s in range(3) for is_send in (True, False)]
    lands = list(_copies_wait(started, waits, after, "gather_arrived_" + grp)[nw:])
    copies = []
    for k in range(nw):
        held = lambda me: me ^ (4 - 2 * (me % 2))
        copies.append((k, held, k, held, lambda c: (c, 1 - c, 0), 3 * k))
        copies.append((k, _xor(4), k, _xor(4), _SIBLING, 3 * k + 1))
        copies.append((k, _xor(2), k, _xor(2), _SIBLING, 3 * k + 2))
    return _copies_start(lands, copies, 3 * nw, "gather_pass_" + grp)


def _gather_finish(passed, grp, after):
    nw = len(passed[2])
    waits = [(k, 1, 3 * k + s, is_send) for k in range(nw) for s in range(3) for is_send in (True, False)]
    lands = list(_copies_wait(passed, waits, passed[3] if after is None else after, "gather_forwarded_" + grp))
    copies = [(k, _xor(6), k, _xor(6), _SIBLING, k) for k in range(nw)]
    last = _copies_start(lands, copies, nw, "gather_pass2_" + grp)
    waits = [(k, 1, k, is_send) for k in range(nw) for is_send in (True, False)]
    return _copies_wait(last, waits, last[3], "gather_done_" + grp)


def _small_send(block, after=None):
    me = _me()
    copies = [(0, lambda me: None, 1, lambda me: me, _SIBLING, 1)]
    copies += [(0, lambda me: None, 1, lambda me: me, f, 0) for f in _OTHER_CHIPS]
    return _copies_start([block, _landing(block[None], me)], copies, 2, "small_send", after)


def _small_pass(started, after):
    waits = [(1, 3, 0, True), (1, 1, 1, True), (1, 3, 0, False), (1, 1, 1, False)]
    land = _copies_wait(started, waits, after, "small_arrived")[1]
    copies = [(0, _xor(_bits(f)), 0, _xor(_bits(f)), _SIBLING, 0) for f in _OTHER_CHIPS]
    return _copies_start([land], copies, 1, "small_pass")


def _small_finish(passed, after):
    return _copies_wait(passed, [(0, 3, 0, True), (0, 3, 0, False)], after, "small_done")[0]


def _scatter_send(parts, grp, after=None):
    nw = len(parts)
    me = _me()
    bufs = list(parts) + [_landing(lax.dynamic_index_in_dim(a, me, 0, keepdims=True), me) for a in parts]
    copies = []
    for k in range(nw):
        for f in _FLIPS:
            to = functools.partial(lambda me, bits: me ^ bits, bits=_bits(f))
            copies.append((k, to, nw + k, lambda me: me, f, k))
    return _copies_start(bufs, copies, nw, "scatter_send_" + grp, after)


def _scatter_finish(started, grp, after):
    nw = len(started[2]) // 2
    waits = [(nw + k, N_DEV - 1, k, is_send) for k in range(nw) for is_send in (True, False)]
    return _copies_wait(started, waits, after, "scatter_done_" + grp)[nw:]


N_CHIP = N_DEV // 2


def _pair_sum(part, other, key):
    _, r, c = part.shape
    tm = r
    core = lax.axis_index("c").astype(jnp.int32).reshape(1)

    def body(core_ref, a_ref, b_ref, o_ref):
        o_ref[...] = (a_ref[...].astype(F32) + b_ref[...].astype(F32)).astype(BF16)

    return pl.pallas_call(
        body,
        name="pair_sum_" + key,
        grid_spec=pltpu.PrefetchScalarGridSpec(
            num_scalar_prefetch=1,
            grid=(N_CHIP, r // tm),
            in_specs=[pl.BlockSpec((None, tm, c), lambda q, i, core_ref: (2 * q + core_ref[0], i, 0)),
                      pl.BlockSpec((None, tm, c), lambda q, i, core_ref: (q, i, 0))],
            out_specs=pl.BlockSpec((None, tm, c), lambda q, i, core_ref: (q, i, 0)),
        ),
        out_shape=jax.ShapeDtypeStruct((N_CHIP, r, c), BF16),
        compiler_params=_cp(("parallel", "parallel")),
    )(core, part, other)


def _scatter2_send(parts, grp, keys):
    nw = len(parts)
    chip = _me() // 2
    swap = []
    for k in range(nw):
        for q in range(N_CHIP):
            src = functools.partial(lambda me, q: 2 * q + 1 - me % 2, q=q)
            swap.append((k, src, nw + k, functools.partial(lambda me, q: q, q=q), _SIBLING, k))
    others = [lax.empty((N_CHIP,) + a.shape[1:], a.dtype) for a in parts]
    started = _copies_start(list(parts) + others, swap, nw, "scatter_swap_" + grp)
    waits = [(nw + k, N_CHIP, k, is_send) for k in range(nw) for is_send in (True, False)]
    swapped = _copies_wait(started, waits, started[3], "scatter_swapped_" + grp)
    pairs = [_pair_sum(swapped[k], swapped[nw + k], keys[k]) for k in range(nw)]
    lands = [lax.dynamic_update_slice(lax.empty(p.shape, p.dtype), lax.dynamic_index_in_dim(p, chip, 0, keepdims=True),
                                      (chip, 0, 0)) for p in pairs]
    copies = []
    for k in range(nw):
        for f in _OTHER_CHIPS:
            to = functools.partial(lambda me, bits: (me ^ bits) // 2, bits=_bits(f))
            copies.append((k, to, nw + k, lambda me: me // 2, f, k))
    return _copies_start(pairs + lands, copies, nw, "scatter_send_" + grp)


def _scatter2_finish(started, grp, after):
    nw = len(started[2]) // 2
    waits = [(nw + k, N_CHIP - 1, k, is_send) for k in range(nw) for is_send in (True, False)]
    return _copies_wait(started, waits, after, "scatter_done_" + grp)[nw:]


_ADAM_ROWS = {352: 176, 1088: 272}


def _sum_adamw(parts, wv, m, v, name, dep=None):
    n_parts, r, c = parts.shape
    tm = _ADAM_ROWS.get(r, r)
    assert r % tm == 0 and wv.shape == (r, c)
    n_dep = 0 if dep is None else 1

    def body(p_ref, w_ref, m_ref, v_ref, *rest):
        g_ref, d_ref, mo_ref, vo_ref = rest[n_dep:]
        gv = p_ref[0].astype(F32)
        for j in range(1, n_parts):
            gv = gv + p_ref[j].astype(F32)
        g_ref[...] = gv
        mn = ADAM_B1 * m_ref[...] + (1.0 - ADAM_B1) * gv
        vn = ADAM_B2 * v_ref[...] + (1.0 - ADAM_B2) * (gv * gv)
        m_hat = mn / (1.0 - ADAM_B1 ** ADAM_STEP)
        v_hat = vn / (1.0 - ADAM_B2 ** ADAM_STEP)
        d_ref[...] = -ADAM_LR * (m_hat / (jnp.sqrt(v_hat) + ADAM_EPS) + ADAM_WD * w_ref[...])
        mo_ref[...] = mn
        vo_ref[...] = vn

    sp = pl.BlockSpec((tm, c), lambda i: (i, 0))
    return pl.pallas_call(
        body,
        name=name,
        grid=(r // tm,),
        in_specs=[pl.BlockSpec((n_parts, tm, c), lambda i: (0, i, 0))] + [sp] * 3 + [pl.BlockSpec(memory_space=pl.ANY)] * n_dep,
        out_specs=[sp] * 4,
        out_shape=[jax.ShapeDtypeStruct((r, c), F32)] * 4,
        compiler_params=_cp(("parallel",), 48),
    )(parts, wv, m, v, *([] if dep is None else [dep]))


_WEIGHTS = ["ffn1_w_gate", "ffn1_w_up", "ffn1_w_down", "ln1_g", "ln1_b", "w_in", "b_gates", "gmlp_ln_g", "gmlp_ln_b",
            "gmlp_w_s", "gmlp_b_s", "w_attn_branch", "w_gmlp_branch", "w_out", "ln2_g", "ln2_b", "ffn2_w_gate",
            "ffn2_w_up", "ffn2_w_down", "ln3_g", "ln3_b"]
_BIG_OF = {"ffn1_w_gate": ("g1", True), "ffn1_w_up": ("u1", True), "ffn1_w_down": ("d1", False), "w_in": ("win", True),
           "w_attn_branch": ("ab", True), "w_gmlp_branch": ("gb", False), "w_out": ("o", False),
           "ffn2_w_gate": ("g2", True), "ffn2_w_up": ("u2", True), "ffn2_w_down": ("d2", False)}
_SMALL = [n for n in _WEIGHTS if n not in _BIG_OF]
_SMALL_ROWS = {"gmlp_w_s": 128, "b_gates": 2}
_SMALL_SLOT = 8


def _pack_small(d, last=None):
    rows = []
    for n in _SMALL:
        r = d[n].reshape(-1, D)
        slot = max(r.shape[0], _SMALL_SLOT)
        rows.append(jnp.pad(r, ((0, slot - r.shape[0]), (0, 0))))
    rows.append(jnp.zeros((_SMALL_SLOT, D), F32) if last is None else jnp.broadcast_to(last.reshape(1, 1), (_SMALL_SLOT, D)))
    return jnp.concatenate(rows, axis=0)


def _unpack_small(packed, shapes):
    out, at = {}, 0
    for n in _SMALL:
        k = _SMALL_ROWS.get(n, 1)
        out[n] = packed[at:at + k].reshape(shapes[n])
        at += max(k, _SMALL_SLOT)
    return out


def kernel(x, positions, ffn1_w_gate, ffn1_w_up, ffn1_w_down, ln1_g, ln1_b, w_in, b_gates, gmlp_ln_g, gmlp_ln_b, gmlp_w_s, gmlp_b_s, w_attn_branch, w_gmlp_branch, w_out, ln2_g, ln2_b, ffn2_w_gate, ffn2_w_up, ffn2_w_down, ln3_g, ln3_b, loss_target, m_ffn1_w_gate, m_ffn1_w_up, m_ffn1_w_down, m_ln1_g, m_ln1_b, m_w_in, m_b_gates, m_gmlp_ln_g, m_gmlp_ln_b, m_gmlp_w_s, m_gmlp_b_s, m_w_attn_branch, m_w_gmlp_branch, m_w_out, m_ln2_g, m_ln2_b, m_ffn2_w_gate, m_ffn2_w_up, m_ffn2_w_down, m_ln3_g, m_ln3_b, v_ffn1_w_gate, v_ffn1_w_up, v_ffn1_w_down, v_ln1_g, v_ln1_b, v_w_in, v_b_gates, v_gmlp_ln_g, v_gmlp_ln_b, v_gmlp_w_s, v_gmlp_b_s, v_w_attn_branch, v_w_gmlp_branch, v_w_out, v_ln2_g, v_ln2_b, v_ffn2_w_gate, v_ffn2_w_up, v_ffn2_w_down, v_ln3_g, v_ln3_b):
    args = dict(locals())
    wts = {n: args[n] for n in _WEIGHTS}
    ms = {n: args["m_" + n] for n in _WEIGHTS}
    vs = {n: args["v_" + n] for n in _WEIGHTS}

    name_of = {key: (n, tr) for n, (key, tr) in _BIG_OF.items()}

    shards = {}
    for grp, keys in _GROUPS.items():
        shards[grp] = []
        for key in keys:
            n, tr = name_of[key]
            s2 = wts[n][0]
            shards[grp].append((s2.T if tr else s2).astype(BF16))
    started, passed = {}, {}

    def send(grp, after):
        started[grp] = _gather_send(shards[grp], grp, after)
        return started[grp][3]

    def pass_on(grp, after):
        passed[grp] = _gather_pass(started[grp], grp, after)
        return passed[grp][3]

    def finish(grp, after):
        lands = _gather_finish(passed[grp], grp, after)
        return {key: g.reshape(-1, g.shape[-1]) for key, g in zip(_GROUPS[grp], lands)}

    def get_w(grp, after, early=False):
        if grp == "ffn1":
            return finish("ffn1", send("win", send("ffn1d", pass_on("ffn1", send("ffn1", None)))))
        if grp == "ffn1d":
            return finish("ffn1d", send("late", pass_on("win", pass_on("ffn1d", after))))
        if early:
            pass_on(grp, after)
            return None
        return finish(grp, after)

    sent = {}

    def emit(grp, grads):
        if grp == "small":
            sent[grp] = _small_send(_pack_small(grads, last=grads["loss"]))
        else:
            parts = [grads[key].reshape(N_DEV, -1, grads[key].shape[-1]) for key in _SCATTERS[grp]]
            sent[grp] = (_scatter2_send(parts, grp, _SCATTERS[grp]) if grp in _TWO_STAGE else _scatter_send(parts, grp))
        return sent[grp][3]

    p = {n: (wts[n][0] if n in ("gmlp_w_s", "gmlp_b_s") else wts[n]) for n in _SMALL}
    loss, dx, gp = _local_step(x[0], positions.reshape(S, 1), loss_target[0], get_w, p, emit)
    grads, deltas, new_m, new_v = {}, {}, {}, {}
    small_passed = _small_pass(sent["small"], dx)
    after = small_passed[3]
    for grp in ("ffn2", "mix", "win", "small", "g1", "ud1"):
        if grp == "small":
            parts = _small_finish(small_passed, after)
            outs = _sum_adamw(parts, *[_pack_small({n: d[n] for n in _SMALL}) for d in (wts, ms, vs)], "update_small")
            shapes = {n: wts[n].shape for n in _SMALL}
            for dst, packed in zip((grads, deltas, new_m, new_v), outs):
                dst.update(_unpack_small(packed, shapes))
            loss = outs[0][-_SMALL_SLOT, 0]
            after = outs[1]
            continue
        arrived = (_scatter2_finish if grp in _TWO_STAGE else _scatter_finish)(sent[grp], grp, after)
        for key, part in zip(_SCATTERS[grp], arrived):
            n, tr = name_of[key]
            outs = _sum_adamw(part, *[(d[n][0].T if tr else d[n][0]) for d in (wts, ms, vs)], "update_" + key, dep=after)
            for dst, o in zip((grads, deltas, new_m, new_v), outs):
                dst[n] = (o.T if tr else o)[None]
            after = outs[1]

    return (loss, dx[None], *[grads[n] for n in _WEIGHTS], *[deltas[n] for n in _WEIGHTS],
            *[new_m[n] for n in _WEIGHTS], *[new_v[n] for n in _WEIGHTS])
```

```python
import functools
import math

import jax
import jax.numpy as jnp
from jax import lax
from jax.experimental import pallas as pl
from jax.experimental.pallas import tpu as pltpu

F32 = jnp.float32
BF16 = jnp.bfloat16

N_DEV = 8
D = 1024
S = 2048
F = 2816
HEAD_DIM = 64
HEADS = 8
GW = HEADS * HEAD_DIM
PATTERN_DILATIONS = (1, 4, 16)
BLK = 128
QKV_W = 3 * 3 * GW
IN_W = QKV_W + 2 * D + 2 * D
DPROJ_W = 5 * 2 * D
DPROJ_Z_COL, DPROJ_G_COL = 3, 4
DPROJ_ZG_AT = DPROJ_Z_COL * 2 * D


def _dproj_tile(t):
    return jnp.where(t < QKV_W // 512, t, t + (DPROJ_ZG_AT - QKV_W) // 512)
ROPE_THETA = 10000.0
ALPHA = 2.0 ** 0.25
LN_EPS = 1e-5
ADAM_LR, ADAM_B1, ADAM_B2, ADAM_EPS, ADAM_WD, ADAM_STEP = 0.001, 0.9, 0.999, 1e-08, 0.01, 10
NEG = -1e30
MESH = pl.DeviceIdType.MESH


def _cp(sem=None, vmem_mb=None):
    kw = {}
    if sem is not None:
        kw["dimension_semantics"] = sem
    if vmem_mb is not None:
        kw["vmem_limit_bytes"] = vmem_mb << 20
    return pltpu.CompilerParams(**kw)


def _dot_nn(a, b):
    return lax.dot_general(a, b, (((1,), (0,)), ((), ())), preferred_element_type=F32)


def _dot_nt(a, b):
    return lax.dot_general(a, b, (((1,), (1,)), ((), ())), preferred_element_type=F32)


def _dot_tn(a, b):
    return lax.dot_general(a, b, (((0,), (0,)), ((), ())), preferred_element_type=F32)


def _ln_fwd(r, g, b):
    mu = jnp.mean(r, axis=-1, keepdims=True)
    xc = r - mu
    var = jnp.mean(xc * xc, axis=-1, keepdims=True)
    rstd = lax.rsqrt(var + LN_EPS)
    xhat = xc * rstd
    return xhat * g + b, xhat, rstd


def _ln_bwd(dh, xhat, rstd, g):
    dxh = dh * g
    m1 = jnp.mean(dxh, axis=-1, keepdims=True)
    m2 = jnp.mean(dxh * xhat, axis=-1, keepdims=True)
    return rstd * (dxh - m1 - xhat * m2)


def _sigmoid(x):
    return 0.5 * jnp.tanh(0.5 * x) + 0.5


def _colsum(x):
    return jnp.sum(x, axis=0, keepdims=True)


def _matmul(a, b, mode, out_dtype, tm, tn, tk, name, b_off=0, n_out=None, dep=None, a_map=None, m_out=None):
    n_dep = 0 if dep is None else 1
    a_map = a_map or (lambda t: t)
    if mode == "nn":
        m, k = a.shape[0], b.shape[0]
        n = b.shape[1]
    elif mode == "nt":
        m, k = a.shape
        n = n_out if n_out is not None else b.shape[0]
    else:
        k, m = a.shape[0], m_out or a.shape[1]
        n = b.shape[1]
    nk = k // tk
    assert m % tm == 0 and n % tn == 0 and k % tk == 0
    dot = {"nn": _dot_nn, "nt": _dot_nt, "tn": _dot_tn}[mode]

    def body(a_ref, b_ref, *rest):
        o_ref, scr = rest[n_dep], rest[n_dep + 1:]
        r = dot(a_ref[...].astype(BF16), b_ref[...].astype(BF16))
        if nk == 1:
            o_ref[...] = r.astype(out_dtype)
        else:
            acc = scr[0]
            kk = pl.program_id(2)

            @pl.when(kk == 0)
            def _():
                acc[...] = r

            @pl.when(kk > 0)
            def _():
                acc[...] += r

            @pl.when(kk == nk - 1)
            def _():
                o_ref[...] = acc[...].astype(out_dtype)

    if mode == "nn":
        a_spec = pl.BlockSpec((tm, tk), lambda i, j, kk: (i, a_map(kk)))
        b_spec = pl.BlockSpec((tk, tn), lambda i, j, kk: (kk, j))
    elif mode == "nt":
        a_spec = pl.BlockSpec((tm, tk), lambda i, j, kk: (i, kk))
        b_spec = pl.BlockSpec((tn, tk), lambda i, j, kk: (j + b_off, kk))
    else:
        a_spec = pl.BlockSpec((tk, tm), lambda i, j, kk: (kk, a_map(i)))
        b_spec = pl.BlockSpec((tk, tn), lambda i, j, kk: (kk, j))
    return pl.pallas_call(
        body,
        name=name,
        grid=(m // tm, n // tn, nk),
        in_specs=[a_spec, b_spec] + [pl.BlockSpec(memory_space=pl.ANY)] * n_dep,
        out_specs=pl.BlockSpec((tm, tn), lambda i, j, kk: (i, j)),
        out_shape=jax.ShapeDtypeStruct((m, n), out_dtype),
        scratch_shapes=[] if nk == 1 else [pltpu.VMEM((tm, tn), F32)],
        compiler_params=_cp(("parallel", "parallel", "arbitrary"), 56),
    )(a, b, *([] if dep is None else [dep]))


def _ffn_up(x, wgt, wut, name, tm=256, tn=F, dep=None):
    n_dep = 0 if dep is None else 1

    def body(x_ref, wg_ref, wu_ref, *rest):
        ga_ref, gb_ref, hm_ref = rest[n_dep:]
        xb = x_ref[...].astype(BF16)
        a = _dot_nt(xb, wg_ref[...])
        b = _dot_nt(xb, wu_ref[...])
        sig = _sigmoid(a)
        silu = a * sig
        ga_ref[...] = (b * (sig + silu * (1.0 - sig))).astype(BF16)
        gb_ref[...] = silu.astype(BF16)
        hm_ref[...] = (silu * b).astype(BF16)

    wsp = pl.BlockSpec((tn, D), lambda i, j: (j, 0))
    mid = pl.BlockSpec((tm, tn), lambda i, j: (i, j))
    return pl.pallas_call(
        body,
        name=name,
        grid=(S // tm, F // tn),
        in_specs=[pl.BlockSpec((tm, D), lambda i, j: (i, 0)), wsp, wsp] + [pl.BlockSpec(memory_space=pl.ANY)] * n_dep,
        out_specs=[mid, mid, mid],
        out_shape=[jax.ShapeDtypeStruct((S, F), BF16)] * 3,
        compiler_params=_cp(("parallel", "arbitrary"), 56),
    )(x, wgt, wut, *([] if dep is None else [dep]))


def _ffn_down(hm, wd, x, ln_g, ln_b, name, target=None, tm=512):
    head = target is not None

    def body(hm_ref, wd_ref, x_ref, g_ref, b_ref, *rest):
        t_ref = rest[0] if head else None
        o1_ref, o2_ref, xh_ref, rs_ref = rest[1 if head else 0:]
        r = ALPHA * x_ref[...] + 0.5 * _dot_nn(hm_ref[...], wd_ref[...])
        h, xhat, rstd = _ln_fwd(r, g_ref[...], b_ref[...])
        xh_ref[...] = xhat
        rs_ref[...] = rstd
        if head:
            e = h - t_ref[...]
            o1_ref[...] = e * (1.0 / D)
            part = jnp.sum(_colsum(e * e), axis=1, keepdims=True) * (0.5 / D)

            @pl.when(pl.program_id(0) == 0)
            def _():
                o2_ref[...] = jnp.zeros_like(o2_ref)

            o2_ref[...] += part
        else:
            o1_ref[...] = h
            o2_ref[...] = h.astype(BF16)

    row = pl.BlockSpec((tm, D), lambda i: (i, 0))
    vec = pl.BlockSpec((1, D), lambda i: (0, 0))
    second = (pl.BlockSpec((1, 1), lambda i: (0, 0)), jax.ShapeDtypeStruct((1, 1), F32)) if head else (
        row, jax.ShapeDtypeStruct((S, D), BF16))
    return pl.pallas_call(
        body,
        name=name,
        grid=(S // tm,),
        in_specs=[pl.BlockSpec((tm, F), lambda i: (i, 0)), pl.BlockSpec((F, D), lambda i: (0, 0)), row, vec, vec]
        + ([row] if head else []),
        out_specs=[row, second[0], row, pl.BlockSpec((tm, 1), lambda i: (i, 0))],
        out_shape=[
            jax.ShapeDtypeStruct((S, D), F32),
            second[1],
            jax.ShapeDtypeStruct((S, D), F32),
            jax.ShapeDtypeStruct((S, 1), F32),
        ],
        compiler_params=_cp(("arbitrary",), 56),
    )(hm, wd, x, ln_g, ln_b, *([target] if head else []))


def _ffn_bwd_mid(dh_a, dh_b, xhat, rstd, ln_g, a, b, wd, name, tm=256, tn=F):
    two = dh_b is not None

    def body(*refs):
        dha_ref = refs[0]
        dhb_ref = refs[1] if two else None
        (xh_ref, rs_ref, g_ref, a_ref, b_ref, wd_ref,
         dr_ref, df_ref, da_ref, db_ref, dg_ref, dbias_ref, df_scr) = refs[2 if two else 1:]
        i = pl.program_id(0)
        j = pl.program_id(1)

        @pl.when(j == 0)
        def _():
            dh = dha_ref[...]
            if two:
                dh = ALPHA * dh + dhb_ref[...]
            xhat = xh_ref[...]
            dr = _ln_bwd(dh, xhat, rs_ref[...], g_ref[...])
            dfb = (0.5 * dr).astype(BF16)
            dr_ref[...] = dr
            df_scr[...] = dfb
            df_ref[...] = dfb
            sg = _colsum(dh * xhat)
            sb = _colsum(dh)

            @pl.when(i == 0)
            def _():
                dg_ref[...] = sg
                dbias_ref[...] = sb

            @pl.when(i > 0)
            def _():
                dg_ref[...] += sg
                dbias_ref[...] += sb

        dhm = _dot_nt(df_scr[...], wd_ref[...])
        da_ref[...] = (dhm * a_ref[...].astype(F32)).astype(BF16)
        db_ref[...] = (dhm * b_ref[...].astype(F32)).astype(BF16)

    row = pl.BlockSpec((tm, D), lambda i, j: (i, 0))
    vec = pl.BlockSpec((1, D), lambda i, j: (0, 0))
    mid = pl.BlockSpec((tm, tn), lambda i, j: (i, j))
    ins = [dh_a] + ([dh_b] if two else []) + [xhat, rstd, ln_g, a, b, wd]
    in_specs = [row] * (2 if two else 1) + [row, pl.BlockSpec((tm, 1), lambda i, j: (i, 0)), vec, mid, mid,
                                            pl.BlockSpec((tn, D), lambda i, j: (j, 0))]
    return pl.pallas_call(
        body,
        name=name,
        grid=(S // tm, F // tn),
        in_specs=in_specs,
        out_specs=[row, row, mid, mid, vec, vec],
        out_shape=[
            jax.ShapeDtypeStruct((S, D), F32),
            jax.ShapeDtypeStruct((S, D), BF16),
            jax.ShapeDtypeStruct((S, F), BF16),
            jax.ShapeDtypeStruct((S, F), BF16),
            jax.ShapeDtypeStruct((1, D), F32),
            jax.ShapeDtypeStruct((1, D), F32),
        ],
        scratch_shapes=[pltpu.VMEM((tm, D), BF16)],
        compiler_params=_cp(("arbitrary", "arbitrary"), 56),
    )(*ins)


def _ffn_bwd_dx(dr, da, db, wgt, wut, name, tm=256, tk=F, dep=None):
    nk = F // tk
    n_dep = 0 if dep is None else 1

    def body(dr_ref, da_ref, db_ref, wg_ref, wu_ref, *rest):
        dx_ref, acc = rest[n_dep], rest[n_dep + 1]
        kk = pl.program_id(1)
        part = _dot_nn(da_ref[...], wg_ref[...]) + _dot_nn(db_ref[...], wu_ref[...])

        @pl.when(kk == 0)
        def _():
            acc[...] = ALPHA * dr_ref[...] + part

        @pl.when(kk > 0)
        def _():
            acc[...] += part

        @pl.when(kk == nk - 1)
        def _():
            dx_ref[...] = acc[...]

    row = pl.BlockSpec((tm, D), lambda i, kk: (i, 0))
    mid = pl.BlockSpec((tm, tk), lambda i, kk: (i, kk))
    wsp = pl.BlockSpec((tk, D), lambda i, kk: (kk, 0))
    return pl.pallas_call(
        body,
        name=name,
        grid=(S // tm, nk),
        in_specs=[row, mid, mid, wsp, wsp] + [pl.BlockSpec(memory_space=pl.ANY)] * n_dep,
        out_specs=row,
        out_shape=jax.ShapeDtypeStruct((S, D), F32),
        scratch_shapes=[pltpu.VMEM((tm, D), F32)],
        compiler_params=_cp(("parallel", "arbitrary"), 56),
    )(dr, da, db, wgt, wut, *([] if dep is None else [dep]))


def _rope_tables(pos_col, invf, sign, tm=512):
    def body(p_ref, f_ref, s_ref, c_out, s_out):
        ang = p_ref[...].astype(F32) * f_ref[...]
        c_out[...] = jnp.cos(ang)
        s_out[...] = jnp.sin(ang) * s_ref[...]

    vec = pl.BlockSpec((1, BLK), lambda i: (0, 0))
    out = pl.BlockSpec((tm, BLK), lambda i: (i, 0))
    return pl.pallas_call(
        body,
        name="rope_tables",
        grid=(S // tm,),
        in_specs=[pl.BlockSpec((tm, 1), lambda i: (i, 0)), vec, vec],
        out_specs=[out, out],
        out_shape=[jax.ShapeDtypeStruct((S, BLK), F32)] * 2,
        compiler_params=_cp(("parallel",)),
    )(pos_col, invf, sign)


def _lane_lo(rows=BLK):
    return (lax.broadcasted_iota(jnp.int32, (rows, GW), 1) % HEAD_DIM) < (HEAD_DIM // 2)


def _swap_halves(t, lo):
    return jnp.where(lo, pltpu.roll(t, GW - HEAD_DIM // 2, 1), pltpu.roll(t, HEAD_DIM // 2, 1))


def _rope(t, cosf, sinf, lo):
    return t * cosf + _swap_halves(t, lo) * sinf


def _unrope(g, cosf, sinf, lo):
    return g * cosf + _swap_halves(g * sinf, lo)


def _tile4(v):
    return jnp.concatenate([v, v, v, v], axis=1)


def _band_mask(n):
    qi = lax.broadcasted_iota(jnp.int32, (BLK, 2 * BLK), 0)
    kj = lax.broadcasted_iota(jnp.int32, (BLK, 2 * BLK), 1)
    dist = qi + BLK - kj
    return (dist >= 0) & (dist <= BLK) & ((kj >= BLK) | (n >= 1))


ROWS = 256
LANES = 128


def _to_view(scr, y, dst_ref, d, dtype, col0=0):
    if d == 1:
        dst_ref[:, col0:col0 + GW] = y.astype(dtype)
        return
    for cb in range(GW // LANES):
        scr[cb][...] = y[:, cb * LANES:(cb + 1) * LANES]
    for r in range(d):
        for cb in range(GW // LANES):
            at = col0 + r * GW + cb * LANES
            dst_ref[:, at:at + LANES] = scr[cb][pl.ds(r, ROWS // d, stride=d), :].astype(dtype)


def _from_view(scr, src_ref, d):
    if d == 1:
        return src_ref[...].astype(F32)
    for r in range(d):
        for cb in range(GW // LANES):
            at = r * GW + cb * LANES
            scr[cb][pl.ds(r, ROWS // d, stride=d), :] = src_ref[:, at:at + LANES].astype(F32)
    return jnp.concatenate([scr[cb][...] for cb in range(GW // LANES)], axis=1)


def _view_spec(d):
    return pl.BlockSpec((ROWS // d, d * GW), lambda i: (i, 0))


_VIEW_SCRATCH = [pltpu.VMEM((ROWS, LANES), F32)] * (GW // LANES)


def _qkv_prep(qkv, cos_t, sin_t):
    def body(x_ref, c_ref, s_ref, *rest):
        outs, scr = rest[:9], rest[9:]
        lo = _lane_lo(ROWS)
        cf, sf = _tile4(c_ref[...]), _tile4(s_ref[...])
        for gi, d in enumerate(PATTERN_DILATIONS):
            q, k, v = (x_ref[:, (3 * part + gi) * GW:(3 * part + gi + 1) * GW].astype(F32) for part in range(3))
            _to_view(scr, _rope(q, cf, sf, lo) * (HEAD_DIM ** -0.5), outs[3 * gi], d, BF16)
            _to_view(scr, _rope(k, cf, sf, lo), outs[3 * gi + 1], d, BF16)
            _to_view(scr, v, outs[3 * gi + 2], d, BF16)

    tab = pl.BlockSpec((ROWS, BLK), lambda i: (i, 0))
    outs = pl.pallas_call(
        body,
        name="qkv_prep",
        grid=(S // ROWS,),
        in_specs=[pl.BlockSpec((ROWS, QKV_W), lambda i: (i, 0)), tab, tab],
        out_specs=[_view_spec(d) for d in PATTERN_DILATIONS for _ in range(3)],
        out_shape=[jax.ShapeDtypeStruct((S // d, d * GW), BF16) for d in PATTERN_DILATIONS for _ in range(3)],
        scratch_shapes=_VIEW_SCRATCH,
        compiler_params=_cp(("parallel",), 48),
    )(qkv, cos_t, sin_t)
    return [tuple(outs[3 * gi:3 * gi + 3]) for gi in range(3)]


N_ATTN_STEPS = S // BLK


def _class_and_block(gi, t):
    per_class = N_ATTN_STEPS // PATTERN_DILATIONS[gi]
    return t >> (per_class.bit_length() - 1), t & (per_class - 1)


def _attn_specs(gi):
    def cur(t):
        r, n = _class_and_block(gi, t)
        return n, r

    def prev(t):
        r, n = _class_and_block(gi, t)
        return jnp.maximum(n - 1, 0), r

    def whole(t):
        return 0, _class_and_block(gi, t)[0]

    sub = S // PATTERN_DILATIONS[gi]
    return (pl.BlockSpec((BLK, GW), cur), pl.BlockSpec((BLK, GW), prev), pl.BlockSpec((sub, GW), whole))


def _left_lanes():
    return lax.broadcasted_iota(jnp.int32, (BLK, LANES), 1) < HEAD_DIM


def _stack_heads(t2, left):
    zero = jnp.zeros_like(t2)
    return jnp.concatenate([jnp.where(left, t2, zero), jnp.where(left, zero, t2)], axis=0)


def _attn_fwd_block(n, q_ref, kc_ref, kp_ref, vc_ref, vp_ref, o_ref, l_ref):
    mask = _band_mask(n)
    left = _left_lanes()
    zero = jnp.zeros((BLK, LANES), BF16)
    for pr in range(GW // LANES):
        cs = slice(pr * LANES, (pr + 1) * LANES)
        q2 = q_ref[:, cs]
        k2 = jnp.concatenate([kp_ref[:, cs], kc_ref[:, cs]], axis=0)
        v2 = jnp.concatenate([vp_ref[:, cs], vc_ref[:, cs]], axis=0)
        o_h, lse_h = [], []
        for side in (left, ~left):
            s = jnp.where(mask, _dot_nt(jnp.where(side, q2, zero), k2), NEG)
            m = jnp.max(s, axis=1, keepdims=True)
            p = jnp.exp(s - m)
            l = jnp.sum(p, axis=1, keepdims=True)
            o_h.append(_dot_nn((p / l).astype(BF16), v2))
            lse_h.append(m + jnp.log(l))
        o_ref[:, cs] = jnp.where(left, o_h[0], o_h[1]).astype(BF16)
        l_ref[:, cs] = jnp.where(left, lse_h[0], lse_h[1])


def _attn_fwd(preps):
    def body(*refs):
        t = pl.program_id(0)
        for gi in range(3):
            _attn_fwd_block(_class_and_block(gi, t)[1], *refs[5 * gi:5 * gi + 5], *refs[15 + 2 * gi:17 + 2 * gi])

    in_specs, out_specs, out_shape, args = [], [], [], []
    for gi, d in enumerate(PATTERN_DILATIONS):
        cur, prev, _ = _attn_specs(gi)
        qr, kr, vv = preps[gi]
        in_specs += [cur, cur, prev, cur, prev]
        args += [qr, kr, kr, vv, vv]
        out_specs += [cur, cur]
        out_shape += [jax.ShapeDtypeStruct((S // d, d * GW), BF16), jax.ShapeDtypeStruct((S // d, d * GW), F32)]
    outs = pl.pallas_call(
        body,
        name="attn_fwd",
        grid=(N_ATTN_STEPS,),
        in_specs=in_specs,
        out_specs=out_specs,
        out_shape=out_shape,
        compiler_params=_cp(("arbitrary",), 48),
    )(*args)
    return [outs[0], outs[2], outs[4]], [outs[1], outs[3], outs[5]]


def _attn_bwd_block(n, q_ref, kc_ref, kp_ref, vc_ref, vp_ref, do_ref, l_ref, c_ref, dq_ref, dk_ref, dv_ref):
    mask = _band_mask(n)
    left = _left_lanes()
    left2 = jnp.concatenate([left, left], axis=0)
    here = pl.ds(pl.multiple_of(n * BLK, BLK), BLK)
    before = pl.ds(pl.multiple_of(jnp.maximum(n - 1, 0) * BLK, BLK), BLK)
    for pr in range(GW // LANES):
        cs = slice(pr * LANES, (pr + 1) * LANES)
        q2 = q_ref[:, cs]
        do2 = do_ref[:, cs]
        k_bd = _stack_heads(jnp.concatenate([kp_ref[:, cs], kc_ref[:, cs]], axis=0), left2)
        v_bd = _stack_heads(jnp.concatenate([vp_ref[:, cs], vc_ref[:, cs]], axis=0), left2)
        s2 = _dot_nt(q2, k_bd)
        dp2 = _dot_nt(do2, v_bd)
        ps, dss = [], []
        for h in range(2):
            at = pr * LANES + h * HEAD_DIM
            half = slice(h * 2 * BLK, (h + 1) * 2 * BLK)
            p = jnp.exp(jnp.where(mask, s2[:, half], NEG) - l_ref[:, at:at + 1])
            ps.append(p.astype(BF16))
            dss.append((p * (dp2[:, half] - c_ref[:, at:at + 1])).astype(BF16))
        ds_both = jnp.concatenate(dss, axis=1)
        dq_ref[:, cs] = _dot_nn(ds_both, k_bd) * (HEAD_DIM ** -0.5)
        dk_bd = _dot_tn(ds_both, q2)
        dv_bd = _dot_tn(jnp.concatenate(ps, axis=1), do2)
        dk2 = jnp.where(left2, dk_bd[:2 * BLK], dk_bd[2 * BLK:])
        dv2 = jnp.where(left2, dv_bd[:2 * BLK], dv_bd[2 * BLK:])
        dk_ref[here, cs] = dk2[BLK:]
        dv_ref[here, cs] = dv2[BLK:]
        dk_ref[before, cs] += dk2[:BLK]
        dv_ref[before, cs] += dv2[:BLK]


def _attn_bwd(preps, dos, lses, cterms):
    def body(*refs):
        t = pl.program_id(0)
        for gi in range(3):
            _attn_bwd_block(_class_and_block(gi, t)[1], *refs[8 * gi:8 * gi + 8], *refs[24 + 3 * gi:27 + 3 * gi])

    in_specs, out_specs, out_shape, args = [], [], [], []
    for gi, d in enumerate(PATTERN_DILATIONS):
        cur, prev, whole = _attn_specs(gi)
        qr, kr, vv = preps[gi]
        in_specs += [cur, cur, prev, cur, prev, cur, cur, cur]
        args += [qr, kr, kr, vv, vv, dos[gi], lses[gi], cterms[gi]]
        out_specs += [cur, whole, whole]
        out_shape += [jax.ShapeDtypeStruct((S // d, d * GW), F32)] * 3
    outs = pl.pallas_call(
        body,
        name="attn_bwd",
        grid=(N_ATTN_STEPS,),
        in_specs=in_specs,
        out_specs=out_specs,
        out_shape=out_shape,
        compiler_params=_cp(("arbitrary",), 56),
    )(*args)
    return [tuple(outs[3 * gi:3 * gi + 3]) for gi in range(3)]


def _qkv_unprep(grads, cos_t, sin_t, dproj):
    def body(*refs):
        views, (c_ref, s_ref, _, out_ref), scr = refs[:9], refs[9:13], refs[13:]
        lo = _lane_lo(ROWS)
        cf, sf = _tile4(c_ref[...]), _tile4(s_ref[...])
        for gi, d in enumerate(PATTERN_DILATIONS):
            dq_ref, dk_ref, dv_ref = views[3 * gi:3 * gi + 3]
            qa, ka, va = gi * GW, (3 + gi) * GW, (6 + gi) * GW
            out_ref[:, qa:qa + GW] = _unrope(_from_view(scr, dq_ref, d), cf, sf, lo).astype(BF16)
            out_ref[:, ka:ka + GW] = _unrope(_from_view(scr, dk_ref, d), cf, sf, lo).astype(BF16)
            out_ref[:, va:va + GW] = _from_view(scr, dv_ref, d).astype(BF16)

    tab = pl.BlockSpec((ROWS, BLK), lambda i: (i, 0))
    return pl.pallas_call(
        body,
        name="qkv_unprep",
        grid=(S // ROWS,),
        in_specs=[_view_spec(d) for d in PATTERN_DILATIONS for _ in range(3)] + [tab, tab, pl.BlockSpec(memory_space=pl.ANY)],
        out_specs=pl.BlockSpec((ROWS, QKV_W), lambda i: (i, 0)),
        out_shape=jax.ShapeDtypeStruct(dproj.shape, dproj.dtype),
        input_output_aliases={11: 0},
        scratch_shapes=_VIEW_SCRATCH,
        compiler_params=_cp(("parallel",), 48),
    )(*[g for grp in grads for g in grp], cos_t, sin_t, dproj)


def _group_weights(l0, l1, l2):
    mx = jnp.maximum(jnp.maximum(l0, l1), l2)
    e0, e1, e2 = jnp.exp(l0 - mx), jnp.exp(l1 - mx), jnp.exp(l2 - mx)
    inv = 1.0 / (e0 + e1 + e2)
    return e0 * inv, e1 * inv, e2 * inv


def _combine_fwd(os_, lses):
    def body(o0, o1, o2, l0, l1, l2, y_ref, *scr):
        ov = [_from_view(scr, o, d) for o, d in zip((o0, o1, o2), PATTERN_DILATIONS)]
        lv = [_from_view(scr, l, d) for l, d in zip((l0, l1, l2), PATTERN_DILATIONS)]
        w0, w1, w2 = _group_weights(*lv)
        y_ref[...] = (w0 * ov[0] + w1 * ov[1] + w2 * ov[2]).astype(BF16)

    views = [_view_spec(d) for d in PATTERN_DILATIONS]
    return pl.pallas_call(
        body,
        name="attn_combine_fwd",
        grid=(S // ROWS,),
        in_specs=views + views,
        out_specs=pl.BlockSpec((ROWS, GW), lambda i: (i, 0)),
        out_shape=jax.ShapeDtypeStruct((S, GW), BF16),
        scratch_shapes=_VIEW_SCRATCH,
        compiler_params=_cp(("parallel",)),
    )(*os_, *lses)


def _combine_bwd(dy, os_, lses, seg):
    def body(dy_ref, o0, o1, o2, l0, l1, l2, seg_ref, d0, d1, d2, c0, c1, c2, *scr):
        ov = [_from_view(scr, o, d) for o, d in zip((o0, o1, o2), PATTERN_DILATIONS)]
        lv = [_from_view(scr, l, d) for l, d in zip((l0, l1, l2), PATTERN_DILATIONS)]
        ws = _group_weights(*lv)
        dyv = dy_ref[...]
        t = dyv * (ws[0] * ov[0] + ws[1] * ov[1] + ws[2] * ov[2])
        t_hi = t.astype(BF16)
        r1 = t - t_hi.astype(F32)
        t_mid = r1.astype(BF16)
        t_lo = (r1 - t_mid.astype(F32)).astype(BF16)
        sg = seg_ref[...]
        e = _dot_nn(t_hi, sg) + _dot_nn(t_mid, sg) + _dot_nn(t_lo, sg)
        for w, do_ref, c_ref, d in zip(ws, (d0, d1, d2), (c0, c1, c2), PATTERN_DILATIONS):
            _to_view(scr, w * dyv, do_ref, d, BF16)
            _to_view(scr, w * e, c_ref, d, F32)

    views = [_view_spec(d) for d in PATTERN_DILATIONS]
    return pl.pallas_call(
        body,
        name="attn_combine_bwd",
        grid=(S // ROWS,),
        in_specs=[pl.BlockSpec((ROWS, GW), lambda i: (i, 0))] + views + views + [pl.BlockSpec((GW, GW), lambda i: (0, 0))],
        out_specs=views + views,
        out_shape=[jax.ShapeDtypeStruct((S // d, d * GW), BF16) for d in PATTERN_DILATIONS]
        + [jax.ShapeDtypeStruct((S // d, d * GW), F32) for d in PATTERN_DILATIONS],
        scratch_shapes=_VIEW_SCRATCH,
        compiler_params=_cp(("parallel",)),
    )(dy, *os_, *lses, seg)


_SQRT_HALF = 0.7071067811865476
_INV_SQRT_2PI = 0.3989422804014327


def _gelu(z):
    return 0.5 * z * (1.0 + lax.erf(z * _SQRT_HALF))


def _gelu_grad(z):
    return 0.5 * (1.0 + lax.erf(z * _SQRT_HALF)) + z * (_INV_SQRT_2PI * jnp.exp(-0.5 * z * z))


def _tril_ws(ws_ref, g):
    t = lax.broadcasted_iota(jnp.int32, (BLK, BLK), 0)
    s = lax.broadcasted_iota(jnp.int32, (BLK, BLK), 1)
    return jnp.where(t >= s, ws_ref[g], 0.0)


def _chunks_side_by_side(x, cols, nch):
    return jnp.concatenate([x[c * BLK:(c + 1) * BLK, cols] for c in range(nch)], axis=1)


def _gmlp_fwd(z, ws, bst, ln_g, ln_b, tm=512):
    nch = tm // BLK

    def body(z_ref, ws_ref, b_ref, g_ref, be_ref, y_ref):
        zg = _gelu(z_ref[...].astype(F32))
        u = zg[:, :D]
        vn, _, _ = _ln_fwd(zg[:, D:], g_ref[...], be_ref[...])
        vnb = vn.astype(BF16)
        bt = b_ref[...]
        for g in range(8):
            w = _tril_ws(ws_ref, g).astype(BF16)
            cols = slice(g * BLK, (g + 1) * BLK)
            mixed = _dot_nn(w, _chunks_side_by_side(vnb, cols, nch)) + bt[:, g:g + 1]
            for c in range(nch):
                rows = slice(c * BLK, (c + 1) * BLK)
                y_ref[rows, cols] = (u[rows, cols] * mixed[:, c * BLK:(c + 1) * BLK]).astype(BF16)

    return pl.pallas_call(
        body,
        name="gmlp_fwd",
        grid=(S // tm,),
        in_specs=[
            pl.BlockSpec((tm, 2 * D), lambda i: (i, 0)),
            pl.BlockSpec((8, BLK, BLK), lambda i: (0, 0, 0)),
            pl.BlockSpec((BLK, 8), lambda i: (0, 0)),
            pl.BlockSpec((1, D), lambda i: (0, 0)),
            pl.BlockSpec((1, D), lambda i: (0, 0)),
        ],
        out_specs=pl.BlockSpec((tm, D), lambda i: (i, 0)),
        out_shape=jax.ShapeDtypeStruct((S, D), BF16),
        compiler_params=_cp(("parallel",), 48),
    )(z, ws, bst, ln_g, ln_b)


def _gmlp_bwd(z, dy, ws, bst, ln_g, ln_b, dproj, tm=512):
    nch = tm // BLK

    def body(z_ref, dy_ref, ws_ref, b_ref, g_ref, be_ref, _, dz_ref, dws_ref, dbs_ref, dg_ref, dbe_ref, dvn_scr, dm_acc):
        i = pl.program_id(0)
        zv = z_ref[...].astype(F32)
        zg = _gelu(zv)
        u = zg[:, :D]
        gam = g_ref[...]
        vn, xhat, rstd = _ln_fwd(zg[:, D:], gam, be_ref[...])
        vnb = vn.astype(BF16)
        dyv = dy_ref[...]
        dmix = dyv * u
        dmb = dmix.astype(BF16)
        bt = b_ref[...]
        tmask = lax.broadcasted_iota(jnp.int32, (BLK, BLK), 0) >= lax.broadcasted_iota(jnp.int32, (BLK, BLK), 1)
        dm_sum = dmix[0:BLK]
        for c in range(1, nch):
            dm_sum = dm_sum + dmix[c * BLK:(c + 1) * BLK]

        @pl.when(i == 0)
        def _():
            dm_acc[...] = jnp.zeros_like(dm_acc)
            dws_ref[...] = jnp.zeros_like(dws_ref)
            dg_ref[...] = jnp.zeros_like(dg_ref)
            dbe_ref[...] = jnp.zeros_like(dbe_ref)

        dm_acc[...] += dm_sum
        dus = []
        for g in range(8):
            w = _tril_ws(ws_ref, g).astype(BF16)
            cols = slice(g * BLK, (g + 1) * BLK)
            v_cat = _chunks_side_by_side(vnb, cols, nch)
            dm_cat = _chunks_side_by_side(dmb, cols, nch)
            mixed = _dot_nn(w, v_cat) + bt[:, g:g + 1]
            dus.append(jnp.concatenate(
                [dyv[c * BLK:(c + 1) * BLK, cols] * mixed[:, c * BLK:(c + 1) * BLK] for c in range(nch)], axis=0))
            dws_ref[g] += jnp.where(tmask, _dot_nt(dm_cat, v_cat), 0.0)
            dvn_cat = _dot_tn(w, dm_cat)
            for c in range(nch):
                dvn_scr[c * BLK:(c + 1) * BLK, cols] = dvn_cat[:, c * BLK:(c + 1) * BLK]

        dvn = dvn_scr[...]
        dg_ref[...] += _colsum(dvn * xhat)
        dbe_ref[...] += _colsum(dvn)
        dvg = _ln_bwd(dvn, xhat, rstd, gam)
        gp = _gelu_grad(zv)
        dz_ref[:, :D] = (jnp.concatenate(dus, axis=1) * gp[:, :D]).astype(BF16)
        dz_ref[:, D:] = (dvg * gp[:, D:]).astype(BF16)

        @pl.when(i == S // tm - 1)
        def _():
            acc = dm_acc[...]
            for g in range(8):
                dbs_ref[:, g:g + 1] = jnp.sum(acc[:, g * BLK:(g + 1) * BLK], axis=1, keepdims=True)

    vec = pl.BlockSpec((1, D), lambda i: (0, 0))
    return pl.pallas_call(
        body,
        name="gmlp_bwd",
        grid=(S // tm,),
        in_specs=[
            pl.BlockSpec((tm, 2 * D), lambda i: (i, 0)),
            pl.BlockSpec((tm, D), lambda i: (i, 0)),
            pl.BlockSpec((8, BLK, BLK), lambda i: (0, 0, 0)),
            pl.BlockSpec((BLK, 8), lambda i: (0, 0)),
            vec,
            vec,
            pl.BlockSpec(memory_space=pl.ANY),
        ],
        out_specs=[
            pl.BlockSpec((tm, 2 * D), lambda i: (i, DPROJ_Z_COL)),
            pl.BlockSpec((8, BLK, BLK), lambda i: (0, 0, 0)),
            pl.BlockSpec((BLK, 8), lambda i: (0, 0)),
            vec,
            vec,
        ],
        out_shape=[
            jax.ShapeDtypeStruct(dproj.shape, dproj.dtype),
            jax.ShapeDtypeStruct((8, BLK, BLK), F32),
            jax.ShapeDtypeStruct((BLK, 8), F32),
            jax.ShapeDtypeStruct((1, D), F32),
            jax.ShapeDtypeStruct((1, D), F32),
        ],
        input_output_aliases={6: 0},
        scratch_shapes=[pltpu.VMEM((tm, D), F32), pltpu.VMEM((BLK, D), F32)],
        compiler_params=_cp(("arbitrary",), 48),
    )(z, dy, ws, bst, ln_g, ln_b, dproj)


def _merge_fwd(ya, yg, glog, bgate, h1, wabt, wgb, wo, ln_g, ln_b, tm=256):
    def body(ya_ref, yg_ref, gl_ref, bg_ref, h1_ref, wab_ref, wgb_ref, wo_ref, g_ref, b_ref,
             h_ref, hb_ref, xh_ref, rs_ref, mg_ref, bra_ref, brg_ref):
        bra = _dot_nt(ya_ref[...], wab_ref[...])
        brg = _dot_nn(yg_ref[...], wgb_ref[...])
        gates = _sigmoid(gl_ref[...].astype(F32) + bg_ref[...])
        merged = (gates[:, :D] * bra + gates[:, D:] * brg).astype(BF16)
        mix = _dot_nn(merged, wo_ref[...])
        h, xhat, rstd = _ln_fwd(ALPHA * h1_ref[...] + mix, g_ref[...], b_ref[...])
        h_ref[...] = h
        hb_ref[...] = h.astype(BF16)
        xh_ref[...] = xhat
        rs_ref[...] = rstd
        mg_ref[...] = merged
        bra_ref[...] = bra.astype(BF16)
        brg_ref[...] = brg.astype(BF16)

    row = pl.BlockSpec((tm, D), lambda i: (i, 0))
    vec = pl.BlockSpec((1, D), lambda i: (0, 0))
    full = lambda shape: pl.BlockSpec(shape, lambda i: (0, 0))
    return pl.pallas_call(
        body,
        name="merge_fwd",
        grid=(S // tm,),
        in_specs=[
            pl.BlockSpec((tm, GW), lambda i: (i, 0)), row,
            pl.BlockSpec((tm, 2 * D), lambda i: (i, glog.shape[1] // (2 * D) - 1)),
            full((1, 2 * D)), row,
            full((D, GW)), full((D, D)), full((D, D)), vec, vec,
        ],
        out_specs=[row, row, row, pl.BlockSpec((tm, 1), lambda i: (i, 0)), row, row, row],
        out_shape=[
            jax.ShapeDtypeStruct((S, D), F32),
            jax.ShapeDtypeStruct((S, D), BF16),
            jax.ShapeDtypeStruct((S, D), F32),
            jax.ShapeDtypeStruct((S, 1), F32),
            jax.ShapeDtypeStruct((S, D), BF16),
            jax.ShapeDtypeStruct((S, D), BF16),
            jax.ShapeDtypeStruct((S, D), BF16),
        ],
        compiler_params=_cp(("parallel",), 48),
    )(ya, yg, glog, bgate, h1, wabt, wgb, wo, ln_g, ln_b)


def _merge_bwd(dh2, xhat, rstd, ln_g, bra, brg, glog, bgate, wabt, wgb, wo, tm=256):
    def body(dh_ref, xh_ref, rs_ref, g_ref, bra_ref, brg_ref, gl_ref, bg_ref, wab_ref, wgb_ref, wo_ref,
             dr_ref, drb_ref, dlog_ref, dba_ref, dbg_ref, dya_ref, dyg_ref, dbgate_ref, dg_ref, dbias_ref):
        i = pl.program_id(0)
        dh = dh_ref[...]
        xh = xh_ref[...]
        dr = _ln_bwd(dh, xh, rs_ref[...], g_ref[...])
        drb = dr.astype(BF16)
        dr_ref[...] = dr
        drb_ref[...] = drb
        dmerged = _dot_nt(drb, wo_ref[...])
        gates = _sigmoid(gl_ref[...].astype(F32) + bg_ref[...])
        g0, g1 = gates[:, :D], gates[:, D:]
        dl0 = dmerged * bra_ref[...].astype(F32) * g0 * (1.0 - g0)
        dl1 = dmerged * brg_ref[...].astype(F32) * g1 * (1.0 - g1)
        dlog_ref[:, :D] = dl0.astype(BF16)
        dlog_ref[:, D:] = dl1.astype(BF16)
        dba = (dmerged * g0).astype(BF16)
        dbg = (dmerged * g1).astype(BF16)
        dba_ref[...] = dba
        dbg_ref[...] = dbg
        dya_ref[...] = _dot_nn(dba, wab_ref[...])
        dyg_ref[...] = _dot_nt(dbg, wgb_ref[...])
        s0, s1 = _colsum(dl0), _colsum(dl1)
        sg, sb = _colsum(dh * xh), _colsum(dh)

        @pl.when(i == 0)
        def _():
            dbgate_ref[:, :D] = s0
            dbgate_ref[:, D:] = s1
            dg_ref[...] = sg
            dbias_ref[...] = sb

        @pl.when(i > 0)
        def _():
            dbgate_ref[:, :D] += s0
            dbgate_ref[:, D:] += s1
            dg_ref[...] += sg
            dbias_ref[...] += sb

    row = pl.BlockSpec((tm, D), lambda i: (i, 0))
    vec = pl.BlockSpec((1, D), lambda i: (0, 0))
    wide = pl.BlockSpec((tm, 2 * D), lambda i: (i, 0))
    full = lambda shape: pl.BlockSpec(shape, lambda i: (0, 0))
    return pl.pallas_call(
        body,
        name="merge_bwd",
        grid=(S // tm,),
        in_specs=[row, row, pl.BlockSpec((tm, 1), lambda i: (i, 0)), vec, row, row,
                  pl.BlockSpec((tm, 2 * D), lambda i: (i, glog.shape[1] // (2 * D) - 1)),
                  full((1, 2 * D)), full((D, GW)), full((D, D)), full((D, D))],
        out_specs=[row, row, pl.BlockSpec((tm, 2 * D), lambda i: (i, DPROJ_G_COL)), row, row,
                   pl.BlockSpec((tm, GW), lambda i: (i, 0)), row, full((1, 2 * D)), vec, vec],
        out_shape=[
            jax.ShapeDtypeStruct((S, D), F32),
            jax.ShapeDtypeStruct((S, D), BF16),
            jax.ShapeDtypeStruct((S, DPROJ_W), BF16),
            jax.ShapeDtypeStruct((S, D), BF16),
            jax.ShapeDtypeStruct((S, D), BF16),
            jax.ShapeDtypeStruct((S, GW), F32),
            jax.ShapeDtypeStruct((S, D), F32),
            jax.ShapeDtypeStruct((1, 2 * D), F32),
            jax.ShapeDtypeStruct((1, D), F32),
            jax.ShapeDtypeStruct((1, D), F32),
        ],
        compiler_params=_cp(("arbitrary",), 48),
    )(dh2, xhat, rstd, ln_g, bra, brg, glog, bgate, wabt, wgb, wo)


def _dproj_to_dh1(dproj, win, dep, tm=512):
    n_dep = 0 if dep is None else 1

    def body(a_ref, b_ref, *rest):
        rest[n_dep][...] = (_dot_nn(a_ref[:, :QKV_W], b_ref[:QKV_W, :])
                            + _dot_nn(a_ref[:, DPROJ_ZG_AT:], b_ref[QKV_W:, :]))

    return pl.pallas_call(
        body,
        name="dproj_to_dh1",
        grid=(S // tm,),
        in_specs=[pl.BlockSpec((tm, DPROJ_W), lambda i: (i, 0)),
                  pl.BlockSpec((IN_W, D), lambda i: (0, 0), pipeline_mode=pl.Buffered(1))]
        + [pl.BlockSpec(memory_space=pl.ANY)] * n_dep,
        out_specs=pl.BlockSpec((tm, D), lambda i: (i, 0)),
        out_shape=jax.ShapeDtypeStruct((S, D), F32),
        compiler_params=_cp(("parallel",), 56),
    )(dproj, win, *([] if dep is None else [dep]))


def _tie(x, dep):
    if dep is None:
        return x
    return x + dep[0, 0].astype(x.dtype)


def _local_step(x, pos_col, target, get_w, p, emit):
    w = get_w("ffn1", None)
    a1, b1, hm1 = _ffn_up(x, w["g1"], w["u1"], "ffn1_up")
    w.update(get_w("ffn1d", hm1))
    h1, h1b, xh1, rs1 = _ffn_down(hm1, w["d1"], x, p["ln1_g"], p["ln1_b"], "ffn1_down")

    w.update(get_w("win", h1b))
    qkv = _matmul(h1b, w["win"], "nt", BF16, S, 1536, D, "proj_qkv", b_off=0, n_out=QKV_W)
    z = glog = _matmul(h1b, w["win"], "nt", BF16, S, 512, D, "proj_zg", b_off=QKV_W // 512, n_out=4 * D)

    half = jnp.arange(0, HEAD_DIM, 2, dtype=F32) / HEAD_DIM
    inv_freq = ROPE_THETA ** (-half)
    invf = jnp.tile(inv_freq, 4).reshape(1, BLK)
    sign = jnp.tile(jnp.concatenate([-jnp.ones((32,), F32), jnp.ones((32,), F32)]), 2).reshape(1, BLK)
    cos_t, sin_t = _rope_tables(pos_col, invf, sign)

    preps = _qkv_prep(qkv, cos_t, sin_t)
    os_, lses = _attn_fwd(preps)
    ya = _combine_fwd(os_, lses)
    get_w("late", ya, early=True)
    bst = p["gmlp_b_s"].T
    yg = _gmlp_fwd(z, p["gmlp_w_s"], bst, p["gmlp_ln_g"], p["gmlp_ln_b"])
    w.update(get_w("late", yg))
    h2, h2b, xh2, rs2, merged, bra, brg = _merge_fwd(ya, yg, glog, p["b_gates"], h1, w["ab"], w["gb"], w["o"],
                                                      p["ln2_g"], p["ln2_b"])
    a2, b2, hm2 = _ffn_up(h2b, w["g2"], w["u2"], "ffn2_up")
    dh3, loss, xh3, rs3 = _ffn_down(hm2, w["d2"], h2, p["ln3_g"], p["ln3_b"], "ffn2_down", target=target)

    gp = {}
    dr3, df2, da2, db2, gp["ln3_g"], gp["ln3_b"] = _ffn_bwd_mid(dh3, None, xh3, rs3, p["ln3_g"], a2, b2, w["d2"],
                                                                "ffn2_bwd_mid")
    tok = emit("ffn2", {
        "g2": _matmul(da2, h2b, "tn", BF16, 1408, D, S, "wgrad_g2"),
        "u2": _matmul(db2, h2b, "tn", BF16, 1408, D, S, "wgrad_u2"),
        "d2": _matmul(hm2, df2, "tn", BF16, 1408, D, S, "wgrad_d2")})
    dh2 = _ffn_bwd_dx(dr3, da2, db2, w["g2"], w["u2"], "ffn2_bwd_dx")

    (dr2, dr2b, dproj, dba, dbg, dya, dyg, gp["b_gates"], gp["ln2_g"], gp["ln2_b"]) = _merge_bwd(
        dh2, xh2, rs2, _tie(p["ln2_g"], tok), bra, brg, glog, p["b_gates"], w["ab"], w["gb"], w["o"])
    tok = emit("mix", {
        "o": _matmul(merged, dr2b, "tn", BF16, 512, D, S, "wgrad_o"),
        "ab": _matmul(dba, ya, "tn", BF16, 512, GW, S, "wgrad_ab"),
        "gb": _matmul(yg, dbg, "tn", BF16, 512, D, S, "wgrad_gb")})

    seg = (jnp.arange(GW)[:, None] // HEAD_DIM == jnp.arange(GW)[None, :] // HEAD_DIM).astype(BF16)
    do0, do1, do2, c0, c1, c2 = _combine_bwd(dya, os_, lses, _tie(seg, tok))
    dproj = _qkv_unprep(_attn_bwd(preps, (do0, do1, do2), lses, (c0, c1, c2)), cos_t, sin_t, dproj)
    dproj, gp["gmlp_w_s"], dbst, gp["gmlp_ln_g"], gp["gmlp_ln_b"] = _gmlp_bwd(
        z, dyg, p["gmlp_w_s"], bst, p["gmlp_ln_g"], p["gmlp_ln_b"], dproj)
    gp["gmlp_b_s"] = dbst.T
    tok = emit("win", {"win": _matmul(dproj, h1b, "tn", BF16, 512, D, S, "wgrad_win", a_map=_dproj_tile, m_out=IN_W)})
    dh1m = _dproj_to_dh1(dproj, w["win"], tok)

    dr1, df1, da1, db1, gp["ln1_g"], gp["ln1_b"] = _ffn_bwd_mid(dr2, dh1m, xh1, rs1, p["ln1_g"], a1, b1, w["d1"],
                                                                "ffn1_bwd_mid")
    tok = emit("small", {**gp, "loss": loss})
    tok = emit("g1", {"g1": _matmul(da1, x, "tn", BF16, 1408, D, S, "wgrad_g1", dep=tok)})
    tok = emit("ud1", {"u1": _matmul(db1, x, "tn", BF16, 1408, D, S, "wgrad_u1", dep=tok),
                       "d1": _matmul(hm1, df1, "tn", BF16, 1408, D, S, "wgrad_d1", dep=tok)})
    dx = _ffn_bwd_dx(dr1, da1, db1, w["g1"], w["u1"], "ffn1_bwd_dx", dep=tok)
    return loss, dx, gp


_FLIPS = [(mx, my, mc) for mx in (0, 1) for my in (0, 1) for mc in (0, 1)][1:]
_GROUPS = {"ffn1": ("g1", "u1"), "ffn1d": ("d1",), "win": ("win",), "late": ("ab", "gb", "o", "g2", "u2", "d2")}
_SCATTERS = {"ffn2": ("g2", "u2", "d2"), "mix": ("ab", "gb", "o"), "win": ("win",), "g1": ("g1",), "ud1": ("u1", "d1")}
_TWO_STAGE = ("g1", "ud1")
_HBM = pl.BlockSpec(memory_space=pltpu.HBM)
_SEM = pl.BlockSpec(memory_space=pltpu.SEMAPHORE)
_EFFECT = pltpu.SideEffectType.DATAFLOW_SIDE_EFFECTING


def _me():
    return 4 * lax.axis_index("x") + 2 * lax.axis_index("y") + lax.axis_index("c")


def _copies_start(bufs, copies, n_sem, name, after=None):
    nb = len(bufs)
    n_after = 0 if after is None else 1

    def body(*refs):
        b = refs[:nb]
        send_sems, recv_sems = refs[nb + n_after], refs[nb + n_after + 1]
        token = refs[-1]
        x, y, c = lax.axis_index("x"), lax.axis_index("y"), lax.axis_index("c")
        me = 4 * x + 2 * y + c
        for si, s_slot, di, d_slot, flip, sem in copies:
            s_idx, d_idx = s_slot(me), d_slot(me)
            mx, my, mc = flip(c) if callable(flip) else flip
            pltpu.make_async_remote_copy(
                src_ref=b[si] if s_idx is None else b[si].at[s_idx],
                dst_ref=b[di] if d_idx is None else b[di].at[d_idx],
                send_sem=send_sems.at[sem], recv_sem=recv_sems.at[sem],
                device_id=(x ^ mx, y ^ my, c ^ mc), device_id_type=MESH).start()
        token[...] = jnp.zeros_like(token)

    ins = [pltpu.with_memory_space_constraint(a, pltpu.HBM) for a in bufs]
    outs = pl.pallas_call(
        body,
        name=name,
        in_specs=[_HBM] * nb + [pl.BlockSpec(memory_space=pl.ANY)] * n_after,
        out_specs=[_SEM, _SEM] + [_HBM] * nb + [pl.BlockSpec(memory_space=pltpu.VMEM)],
        out_shape=[pltpu.SemaphoreType.DMA((n_sem,)), pltpu.SemaphoreType.DMA((n_sem,))]
        + [pltpu.HBM(a.shape, a.dtype) for a in ins] + [jax.ShapeDtypeStruct((8, 128), F32)],
        input_output_aliases={k: 2 + k for k in range(nb)},
        compiler_params=pltpu.CompilerParams(has_side_effects=_EFFECT),
    )(*ins, *([] if after is None else [after]))
    return outs[0], outs[1], list(outs[2:2 + nb]), outs[-1]


def _copies_wait(started, waits, after, name):
    send_sems, recv_sems, bufs, _ = started
    nb = len(bufs)

    def body(*refs):
        b = refs[:nb]
        ss, rs = refs[nb], refs[nb + 1]
        me3 = (lax.axis_index("x"), lax.axis_index("y"), lax.axis_index("c"))
        for bi, n_blocks, sem, is_send in waits:
            blocks = b[bi].at[pl.ds(0, n_blocks)]
            cp = pltpu.make_async_remote_copy(src_ref=blocks, dst_ref=blocks, send_sem=ss.at[sem], recv_sem=rs.at[sem],
                                              device_id=me3, device_id_type=MESH)
            if is_send:
                cp.wait_send()
            else:
                cp.wait_recv()

    return pl.pallas_call(
        body,
        name=name,
        in_specs=[_HBM] * nb + [_SEM, _SEM, pl.BlockSpec(memory_space=pl.ANY)],
        out_specs=[_HBM] * nb,
        out_shape=[pltpu.HBM(a.shape, a.dtype) for a in bufs],
        input_output_aliases={k: k for k in range(nb)},
        compiler_params=pltpu.CompilerParams(has_side_effects=_EFFECT),
    )(*bufs, send_sems, recv_sems, after)


def _landing(own, me):
    return lax.dynamic_update_slice(lax.empty((N_DEV,) + own.shape[1:], own.dtype), own, (me, 0, 0))


_SIBLING = (0, 0, 1)
_OTHER_CHIPS = ((1, 0, 0), (0, 1, 0), (1, 1, 0))


def _bits(flip):
    return 4 * flip[0] + 2 * flip[1] + flip[2]


_X_NBR, _Y_NBR = (1, 0, 0), (0, 1, 0)


def _xor(bits):
    return functools.partial(lambda me, bits: me ^ bits, bits=bits)


def _gather_send(shards, grp, after):
    nw = len(shards)
    me = _me()
    bufs = list(shards) + [_landing(a[None], me) for a in shards]
    copies = []
    for k in range(nw):
        for s, flip in enumerate((_SIBLING, _X_NBR, _Y_NBR)):
            copies.append((k, lambda me: None, nw + k, lambda me: me, flip, 3 * k + s))
    return _copies_start(bufs, copies, 3 * nw, "gather_send_" + grp, after)


def _gather_pass(started, grp, after):
    nw = len(started[2]) // 2
    waits = [(nw + k, 1, 3 * k + s, is_send) for k in range(nw) for s in range(3) for is_send in (True, False)]
    lands = list(_copies_wait(started, waits, after, "gather_arrived_" + grp)[nw:])
    copies = []
    for k in range(nw):
        held = lambda me: me ^ (4 - 2 * (me % 2))
        copies.append((k, held, k, held, lambda c: (c, 1 - c, 0), 3 * k))
        copies.append((k, _xor(4), k, _xor(4), _SIBLING, 3 * k + 1))
        copies.append((k, _xor(2), k, _xor(2), _SIBLING, 3 * k + 2))
    return _copies_start(lands, copies, 3 * nw, "gather_pass_" + grp)


def _gather_finish(passed, grp, after):
    nw = len(passed[2])
    waits = [(k, 1, 3 * k + s, is_send) for k in range(nw) for s in range(3) for is_send in (True, False)]
    lands = list(_copies_wait(passed, waits, passed[3] if after is None else after, "gather_forwarded_" + grp))
    copies = [(k, _xor(6), k, _xor(6), _SIBLING, k) for k in range(nw)]
    last = _copies_start(lands, copies, nw, "gather_pass2_" + grp)
    waits = [(k, 1, k, is_send) for k in range(nw) for is_send in (True, False)]
    return _copies_wait(last, waits, last[3], "gather_done_" + grp)


def _small_send(block, after=None):
    me = _me()
    copies = [(0, lambda me: None, 1, lambda me: me, _SIBLING, 1)]
    copies += [(0, lambda me: None, 1, lambda me: me, f, 0) for f in _OTHER_CHIPS]
    return _copies_start([block, _landing(block[None], me)], copies, 2, "small_send", after)


def _small_pass(started, after):
    waits = [(1, 3, 0, True), (1, 1, 1, True), (1, 3, 0, False), (1, 1, 1, False)]
    land = _copies_wait(started, waits, after, "small_arrived")[1]
    copies = [(0, _xor(_bits(f)), 0, _xor(_bits(f)), _SIBLING, 0) for f in _OTHER_CHIPS]
    return _copies_start([land], copies, 1, "small_pass")


def _small_finish(passed, after):
    return _copies_wait(passed, [(0, 3, 0, True), (0, 3, 0, False)], after, "small_done")[0]


def _scatter_send(parts, grp, after=None):
    nw = len(parts)
    bufs = list(parts) + [lax.empty((N_DEV - 1,) + a.shape[1:], a.dtype) for a in parts]
    copies = []
    for k in range(nw):
        for f in _FLIPS:
            to = functools.partial(lambda me, bits: me ^ bits, bits=_bits(f))
            copies.append((k, to, nw + k, functools.partial(lambda me, s: s, s=_bits(f) - 1), f, k))
    return _copies_start(bufs, copies, nw, "scatter_send_" + grp, after)


def _scatter_finish(started, grp, after):
    nw = len(started[2]) // 2
    waits = [(nw + k, N_DEV - 1, k, is_send) for k in range(nw) for is_send in (True, False)]
    done = _copies_wait(started, waits, after, "scatter_done_" + grp)
    return [(done[nw + k], done[k]) for k in range(nw)]


N_CHIP = N_DEV // 2


def _pair_sum(part, other, key):
    _, r, c = part.shape
    tm = r
    core = lax.axis_index("c").astype(jnp.int32).reshape(1)

    def body(core_ref, a_ref, b_ref, o_ref):
        o_ref[...] = (a_ref[...].astype(F32) + b_ref[...].astype(F32)).astype(BF16)

    return pl.pallas_call(
        body,
        name="pair_sum_" + key,
        grid_spec=pltpu.PrefetchScalarGridSpec(
            num_scalar_prefetch=1,
            grid=(N_CHIP, r // tm),
            in_specs=[pl.BlockSpec((None, tm, c), lambda q, i, core_ref: (2 * q + core_ref[0], i, 0)),
                      pl.BlockSpec((None, tm, c), lambda q, i, core_ref: (q, i, 0))],
            out_specs=pl.BlockSpec((None, tm, c), lambda q, i, core_ref: (q, i, 0)),
        ),
        out_shape=jax.ShapeDtypeStruct((N_CHIP, r, c), BF16),
        compiler_params=_cp(("parallel", "parallel")),
    )(core, part, other)


def _scatter2_send(parts, grp, keys):
    nw = len(parts)
    swap = []
    for k in range(nw):
        for q in range(N_CHIP):
            src = functools.partial(lambda me, q: 2 * q + 1 - me % 2, q=q)
            swap.append((k, src, nw + k, functools.partial(lambda me, q: q, q=q), _SIBLING, k))
    others = [lax.empty((N_CHIP,) + a.shape[1:], a.dtype) for a in parts]
    started = _copies_start(list(parts) + others, swap, nw, "scatter_swap_" + grp)
    waits = [(nw + k, N_CHIP, k, is_send) for k in range(nw) for is_send in (True, False)]
    swapped = _copies_wait(started, waits, started[3], "scatter_swapped_" + grp)
    pairs = [_pair_sum(swapped[k], swapped[nw + k], keys[k]) for k in range(nw)]
    lands = [lax.empty((N_CHIP - 1,) + p.shape[1:], p.dtype) for p in pairs]
    copies = []
    for k in range(nw):
        for slot, f in enumerate(_OTHER_CHIPS):
            to = functools.partial(lambda me, bits: (me ^ bits) // 2, bits=_bits(f))
            copies.append((k, to, nw + k, functools.partial(lambda me, s: s, s=slot), f, k))
    return _copies_start(pairs + lands, copies, nw, "scatter_send_" + grp)


def _scatter2_finish(started, grp, after):
    nw = len(started[2]) // 2
    waits = [(nw + k, N_CHIP - 1, k, is_send) for k in range(nw) for is_send in (True, False)]
    done = _copies_wait(started, waits, after, "scatter_done_" + grp)
    return [(done[nw + k], done[k]) for k in range(nw)]


_ADAM_ROWS = {352: 176, 1088: 272}


def _sum_adamw(parts, wv, m, v, name, dep=None, own=None, own_at=None):
    n_parts, r, c = parts.shape
    tm = _ADAM_ROWS.get(r, r)
    assert r % tm == 0 and wv.shape == (r, c)
    n_dep = 0 if dep is None else 1
    n_own = 0 if own is None else 1

    def body(*refs):
        refs = refs[n_own:]
        o_ref = refs[0] if n_own else None
        p_ref, w_ref, m_ref, v_ref = refs[n_own:n_own + 4]
        g_ref, d_ref, mo_ref, vo_ref = refs[n_own + 4 + n_dep:]
        gv = p_ref[0].astype(F32)
        if n_own:
            gv = o_ref[...].astype(F32) + gv
        for j in range(1, n_parts):
            gv = gv + p_ref[j].astype(F32)
        g_ref[...] = gv
        mn = ADAM_B1 * m_ref[...] + (1.0 - ADAM_B1) * gv
        vn = ADAM_B2 * v_ref[...] + (1.0 - ADAM_B2) * (gv * gv)
        m_hat = mn / (1.0 - ADAM_B1 ** ADAM_STEP)
        v_hat = vn / (1.0 - ADAM_B2 ** ADAM_STEP)
        d_ref[...] = -ADAM_LR * (m_hat / (jnp.sqrt(v_hat) + ADAM_EPS) + ADAM_WD * w_ref[...])
        mo_ref[...] = mn
        vo_ref[...] = vn

    sp = pl.BlockSpec((tm, c), lambda i, *_: (i, 0))
    in_specs = ([pl.BlockSpec((None, tm, c), lambda i, at: (at[0], i, 0))] * n_own
                + [pl.BlockSpec((n_parts, tm, c), lambda i, *_: (0, i, 0))] + [sp] * 3
                + [pl.BlockSpec(memory_space=pl.ANY)] * n_dep)
    return pl.pallas_call(
        body,
        name=name,
        grid_spec=pltpu.PrefetchScalarGridSpec(num_scalar_prefetch=n_own, grid=(r // tm,), in_specs=in_specs,
                                               out_specs=[sp] * 4),
        out_shape=[jax.ShapeDtypeStruct((r, c), F32)] * 4,
        compiler_params=_cp(("parallel",), 48),
    )(*([own_at, own] if n_own else []), parts, wv, m, v, *([] if dep is None else [dep]))


_WEIGHTS = ["ffn1_w_gate", "ffn1_w_up", "ffn1_w_down", "ln1_g", "ln1_b", "w_in", "b_gates", "gmlp_ln_g", "gmlp_ln_b",
            "gmlp_w_s", "gmlp_b_s", "w_attn_branch", "w_gmlp_branch", "w_out", "ln2_g", "ln2_b", "ffn2_w_gate",
            "ffn2_w_up", "ffn2_w_down", "ln3_g", "ln3_b"]
_BIG_OF = {"ffn1_w_gate": ("g1", True), "ffn1_w_up": ("u1", True), "ffn1_w_down": ("d1", False), "w_in": ("win", True),
           "w_attn_branch": ("ab", True), "w_gmlp_branch": ("gb", False), "w_out": ("o", False),
           "ffn2_w_gate": ("g2", True), "ffn2_w_up": ("u2", True), "ffn2_w_down": ("d2", False)}
_SMALL = [n for n in _WEIGHTS if n not in _BIG_OF]
_SMALL_ROWS = {"gmlp_w_s": 128, "b_gates": 2}
_SMALL_SLOT = 8


def _pack_small(d, last=None):
    rows = []
    for n in _SMALL:
        r = d[n].reshape(-1, D)
        slot = max(r.shape[0], _SMALL_SLOT)
        rows.append(jnp.pad(r, ((0, slot - r.shape[0]), (0, 0))))
    rows.append(jnp.zeros((_SMALL_SLOT, D), F32) if last is None else jnp.broadcast_to(last.reshape(1, 1), (_SMALL_SLOT, D)))
    return jnp.concatenate(rows, axis=0)


def _unpack_small(packed, shapes):
    out, at = {}, 0
    for n in _SMALL:
        k = _SMALL_ROWS.get(n, 1)
        out[n] = packed[at:at + k].reshape(shapes[n])
        at += max(k, _SMALL_SLOT)
    return out


def kernel(x, positions, ffn1_w_gate, ffn1_w_up, ffn1_w_down, ln1_g, ln1_b, w_in, b_gates, gmlp_ln_g, gmlp_ln_b, gmlp_w_s, gmlp_b_s, w_attn_branch, w_gmlp_branch, w_out, ln2_g, ln2_b, ffn2_w_gate, ffn2_w_up, ffn2_w_down, ln3_g, ln3_b, loss_target, m_ffn1_w_gate, m_ffn1_w_up, m_ffn1_w_down, m_ln1_g, m_ln1_b, m_w_in, m_b_gates, m_gmlp_ln_g, m_gmlp_ln_b, m_gmlp_w_s, m_gmlp_b_s, m_w_attn_branch, m_w_gmlp_branch, m_w_out, m_ln2_g, m_ln2_b, m_ffn2_w_gate, m_ffn2_w_up, m_ffn2_w_down, m_ln3_g, m_ln3_b, v_ffn1_w_gate, v_ffn1_w_up, v_ffn1_w_down, v_ln1_g, v_ln1_b, v_w_in, v_b_gates, v_gmlp_ln_g, v_gmlp_ln_b, v_gmlp_w_s, v_gmlp_b_s, v_w_attn_branch, v_w_gmlp_branch, v_w_out, v_ln2_g, v_ln2_b, v_ffn2_w_gate, v_ffn2_w_up, v_ffn2_w_down, v_ln3_g, v_ln3_b):
    args = dict(locals())
    wts = {n: args[n] for n in _WEIGHTS}
    ms = {n: args["m_" + n] for n in _WEIGHTS}
    vs = {n: args["v_" + n] for n in _WEIGHTS}

    name_of = {key: (n, tr) for n, (key, tr) in _BIG_OF.items()}

    shards = {}
    for grp, keys in _GROUPS.items():
        shards[grp] = []
        for key in keys:
            n, tr = name_of[key]
            s2 = wts[n][0]
            shards[grp].append((s2.T if tr else s2).astype(BF16))
    started, passed = {}, {}

    def send(grp, after):
        started[grp] = _gather_send(shards[grp], grp, after)
        return started[grp][3]

    def pass_on(grp, after):
        passed[grp] = _gather_pass(started[grp], grp, after)
        return passed[grp][3]

    def finish(grp, after):
        lands = _gather_finish(passed[grp], grp, after)
        return {key: g.reshape(-1, g.shape[-1]) for key, g in zip(_GROUPS[grp], lands)}

    def get_w(grp, after, early=False):
        if grp == "ffn1":
            return finish("ffn1", send("win", send("ffn1d", pass_on("ffn1", send("ffn1", None)))))
        if grp == "ffn1d":
            return finish("ffn1d", send("late", pass_on("win", pass_on("ffn1d", after))))
        if early:
            pass_on(grp, after)
            return None
        return finish(grp, after)

    sent = {}

    def emit(grp, grads):
        if grp == "small":
            sent[grp] = _small_send(_pack_small(grads, last=grads["loss"]))
        else:
            parts = [grads[key].reshape(N_DEV, -1, grads[key].shape[-1]) for key in _SCATTERS[grp]]
            sent[grp] = (_scatter2_send(parts, grp, _SCATTERS[grp]) if grp in _TWO_STAGE else _scatter_send(parts, grp))
        return sent[grp][3]

    p = {n: (wts[n][0] if n in ("gmlp_w_s", "gmlp_b_s") else wts[n]) for n in _SMALL}
    loss, dx, gp = _local_step(x[0], positions.reshape(S, 1), loss_target[0], get_w, p, emit)
    grads, deltas, new_m, new_v = {}, {}, {}, {}
    small_passed = _small_pass(sent["small"], dx)
    after = small_passed[3]
    for grp in ("ffn2", "mix", "win", "small", "g1", "ud1"):
        if grp == "small":
            parts = _small_finish(small_passed, after)
            outs = _sum_adamw(parts, *[_pack_small({n: d[n] for n in _SMALL}) for d in (wts, ms, vs)], "update_small")
            shapes = {n: wts[n].shape for n in _SMALL}
            for dst, packed in zip((grads, deltas, new_m, new_v), outs):
                dst.update(_unpack_small(packed, shapes))
            loss = outs[0][-_SMALL_SLOT, 0]
            after = outs[1]
            continue
        arrived = (_scatter2_finish if grp in _TWO_STAGE else _scatter_finish)(sent[grp], grp, after)
        own_at = (_me() // 2 if grp in _TWO_STAGE else _me()).astype(jnp.int32).reshape(1)
        for key, (part, own) in zip(_SCATTERS[grp], arrived):
            n, tr = name_of[key]
            outs = _sum_adamw(part, *[(d[n][0].T if tr else d[n][0]) for d in (wts, ms, vs)], "update_" + key, dep=after,
                              own=own, own_at=own_at)
            for dst, o in zip((grads, deltas, new_m, new_v), outs):
                dst[n] = (o.T if tr else o)[None]
            after = outs[1]

    return (loss, dx[None], *[grads[n] for n in _WEIGHTS], *[deltas[n] for n in _WEIGHTS],
            *[new_m[n] for n in _WEIGHTS], *[new_v[n] for n in _WEIGHTS])
```

```python
import functools
import math

import jax
import jax.numpy as jnp
from jax import lax
from jax.experimental import pallas as pl
from jax.experimental.pallas import tpu as pltpu

F32 = jnp.float32
BF16 = jnp.bfloat16

N_DEV = 8
D = 1024
S = 2048
F = 2816
HEAD_DIM = 64
HEADS = 8
GW = HEADS * HEAD_DIM
PATTERN_DILATIONS = (1, 4, 16)
BLK = 128
QKV_W = 3 * 3 * GW
IN_W = QKV_W + 2 * D + 2 * D
DPROJ_W = 5 * 2 * D
DPROJ_Z_COL, DPROJ_G_COL = 3, 4
DPROJ_ZG_AT = DPROJ_Z_COL * 2 * D


def _dproj_tile(t):
    return jnp.where(t < QKV_W // 512, t, t + (DPROJ_ZG_AT - QKV_W) // 512)
ROPE_THETA = 10000.0
ALPHA = 2.0 ** 0.25
LN_EPS = 1e-5
ADAM_LR, ADAM_B1, ADAM_B2, ADAM_EPS, ADAM_WD, ADAM_STEP = 0.001, 0.9, 0.999, 1e-08, 0.01, 10
NEG = -1e30
MESH = pl.DeviceIdType.MESH


def _cp(sem=None, vmem_mb=None):
    kw = {}
    if sem is not None:
        kw["dimension_semantics"] = sem
    if vmem_mb is not None:
        kw["vmem_limit_bytes"] = vmem_mb << 20
    return pltpu.CompilerParams(**kw)


def _dot_nn(a, b):
    return lax.dot_general(a, b, (((1,), (0,)), ((), ())), preferred_element_type=F32)


def _dot_nt(a, b):
    return lax.dot_general(a, b, (((1,), (1,)), ((), ())), preferred_element_type=F32)


def _dot_tn(a, b):
    return lax.dot_general(a, b, (((0,), (0,)), ((), ())), preferred_element_type=F32)


def _ln_fwd(r, g, b):
    mu = jnp.mean(r, axis=-1, keepdims=True)
    xc = r - mu
    var = jnp.mean(xc * xc, axis=-1, keepdims=True)
    rstd = lax.rsqrt(var + LN_EPS)
    xhat = xc * rstd
    return xhat * g + b, xhat, rstd


def _ln_bwd(dh, xhat, rstd, g):
    dxh = dh * g
    m1 = jnp.mean(dxh, axis=-1, keepdims=True)
    m2 = jnp.mean(dxh * xhat, axis=-1, keepdims=True)
    return rstd * (dxh - m1 - xhat * m2)


def _sigmoid(x):
    return 0.5 * jnp.tanh(0.5 * x) + 0.5


def _colsum(x):
    return jnp.sum(x, axis=0, keepdims=True)


def _matmul(a, b, mode, out_dtype, tm, tn, tk, name, b_off=0, n_out=None, dep=None, a_map=None, m_out=None):
    n_dep = 0 if dep is None else 1
    a_map = a_map or (lambda t: t)
    if mode == "nn":
        m, k = a.shape[0], b.shape[0]
        n = b.shape[1]
    elif mode == "nt":
        m, k = a.shape
        n = n_out if n_out is not None else b.shape[0]
    else:
        k, m = a.shape[0], m_out or a.shape[1]
        n = b.shape[1]
    nk = k // tk
    assert m % tm == 0 and n % tn == 0 and k % tk == 0
    dot = {"nn": _dot_nn, "nt": _dot_nt, "tn": _dot_tn}[mode]

    def body(a_ref, b_ref, *rest):
        o_ref, scr = rest[n_dep], rest[n_dep + 1:]
        r = dot(a_ref[...].astype(BF16), b_ref[...].astype(BF16))
        if nk == 1:
            o_ref[...] = r.astype(out_dtype)
        else:
            acc = scr[0]
            kk = pl.program_id(2)

            @pl.when(kk == 0)
            def _():
                acc[...] = r

            @pl.when(kk > 0)
            def _():
                acc[...] += r

            @pl.when(kk == nk - 1)
            def _():
                o_ref[...] = acc[...].astype(out_dtype)

    if mode == "nn":
        a_spec = pl.BlockSpec((tm, tk), lambda i, j, kk: (i, a_map(kk)))
        b_spec = pl.BlockSpec((tk, tn), lambda i, j, kk: (kk, j))
    elif mode == "nt":
        a_spec = pl.BlockSpec((tm, tk), lambda i, j, kk: (i, kk))
        b_spec = pl.BlockSpec((tn, tk), lambda i, j, kk: (j + b_off, kk))
    else:
        a_spec = pl.BlockSpec((tk, tm), lambda i, j, kk: (kk, a_map(i)))
        b_spec = pl.BlockSpec((tk, tn), lambda i, j, kk: (kk, j))
    return pl.pallas_call(
        body,
        name=name,
        grid=(m // tm, n // tn, nk),
        in_specs=[a_spec, b_spec] + [pl.BlockSpec(memory_space=pl.ANY)] * n_dep,
        out_specs=pl.BlockSpec((tm, tn), lambda i, j, kk: (i, j)),
        out_shape=jax.ShapeDtypeStruct((m, n), out_dtype),
        scratch_shapes=[] if nk == 1 else [pltpu.VMEM((tm, tn), F32)],
        compiler_params=_cp(("parallel", "parallel", "arbitrary"), 56),
    )(a, b, *([] if dep is None else [dep]))


def _ffn_up(x, wgt, wut, name, tm=256, tn=F, dep=None):
    n_dep = 0 if dep is None else 1

    def body(x_ref, wg_ref, wu_ref, *rest):
        ga_ref, gb_ref, hm_ref = rest[n_dep:]
        xb = x_ref[...].astype(BF16)
        a = _dot_nt(xb, wg_ref[...])
        b = _dot_nt(xb, wu_ref[...])
        sig = _sigmoid(a)
        silu = a * sig
        ga_ref[...] = (b * (sig + silu * (1.0 - sig))).astype(BF16)
        gb_ref[...] = silu.astype(BF16)
        hm_ref[...] = (silu * b).astype(BF16)

    wsp = pl.BlockSpec((tn, D), lambda i, j: (j, 0))
    mid = pl.BlockSpec((tm, tn), lambda i, j: (i, j))
    return pl.pallas_call(
        body,
        name=name,
        grid=(S // tm, F // tn),
        in_specs=[pl.BlockSpec((tm, D), lambda i, j: (i, 0)), wsp, wsp] + [pl.BlockSpec(memory_space=pl.ANY)] * n_dep,
        out_specs=[mid, mid, mid],
        out_shape=[jax.ShapeDtypeStruct((S, F), BF16)] * 3,
        compiler_params=_cp(("parallel", "arbitrary"), 56),
    )(x, wgt, wut, *([] if dep is None else [dep]))


def _ffn_down(hm, wd, x, ln_g, ln_b, name, target=None, tm=512):
    head = target is not None

    def body(hm_ref, wd_ref, x_ref, g_ref, b_ref, *rest):
        t_ref = rest[0] if head else None
        o1_ref, o2_ref, xh_ref, rs_ref = rest[1 if head else 0:]
        r = ALPHA * x_ref[...] + 0.5 * _dot_nn(hm_ref[...], wd_ref[...])
        h, xhat, rstd = _ln_fwd(r, g_ref[...], b_ref[...])
        xh_ref[...] = xhat
        rs_ref[...] = rstd
        if head:
            e = h - t_ref[...]
            o1_ref[...] = e * (1.0 / D)
            part = jnp.sum(_colsum(e * e), axis=1, keepdims=True) * (0.5 / D)

            @pl.when(pl.program_id(0) == 0)
            def _():
                o2_ref[...] = jnp.zeros_like(o2_ref)

            o2_ref[...] += part
        else:
            o1_ref[...] = h
            o2_ref[...] = h.astype(BF16)

    row = pl.BlockSpec((tm, D), lambda i: (i, 0))
    vec = pl.BlockSpec((1, D), lambda i: (0, 0))
    second = (pl.BlockSpec((1, 1), lambda i: (0, 0)), jax.ShapeDtypeStruct((1, 1), F32)) if head else (
        row, jax.ShapeDtypeStruct((S, D), BF16))
    return pl.pallas_call(
        body,
        name=name,
        grid=(S // tm,),
        in_specs=[pl.BlockSpec((tm, F), lambda i: (i, 0)), pl.BlockSpec((F, D), lambda i: (0, 0)), row, vec, vec]
        + ([row] if head else []),
        out_specs=[row, second[0], row, pl.BlockSpec((tm, 1), lambda i: (i, 0))],
        out_shape=[
            jax.ShapeDtypeStruct((S, D), F32),
            second[1],
            jax.ShapeDtypeStruct((S, D), F32),
            jax.ShapeDtypeStruct((S, 1), F32),
        ],
        compiler_params=_cp(("arbitrary",), 56),
    )(hm, wd, x, ln_g, ln_b, *([target] if head else []))


def _ffn_bwd_mid(dh_a, dh_b, xhat, rstd, ln_g, a, b, wd, name, tm=256, tn=F):
    two = dh_b is not None

    def body(*refs):
        dha_ref = refs[0]
        dhb_ref = refs[1] if two else None
        (xh_ref, rs_ref, g_ref, a_ref, b_ref, wd_ref,
         dr_ref, df_ref, da_ref, db_ref, dg_ref, dbias_ref, df_scr) = refs[2 if two else 1:]
        i = pl.program_id(0)
        j = pl.program_id(1)

        @pl.when(j == 0)
        def _():
            dh = dha_ref[...]
            if two:
                dh = ALPHA * dh + dhb_ref[...]
            xhat = xh_ref[...]
            dr = _ln_bwd(dh, xhat, rs_ref[...], g_ref[...])
            dfb = (0.5 * dr).astype(BF16)
            dr_ref[...] = dr
            df_scr[...] = dfb
            df_ref[...] = dfb
            sg = _colsum(dh * xhat)
            sb = _colsum(dh)

            @pl.when(i == 0)
            def _():
                dg_ref[...] = sg
                dbias_ref[...] = sb

            @pl.when(i > 0)
            def _():
                dg_ref[...] += sg
                dbias_ref[...] += sb

        dhm = _dot_nt(df_scr[...], wd_ref[...])
        da_ref[...] = (dhm * a_ref[...].astype(F32)).astype(BF16)
        db_ref[...] = (dhm * b_ref[...].astype(F32)).astype(BF16)

    row = pl.BlockSpec((tm, D), lambda i, j: (i, 0))
    vec = pl.BlockSpec((1, D), lambda i, j: (0, 0))
    mid = pl.BlockSpec((tm, tn), lambda i, j: (i, j))
    ins = [dh_a] + ([dh_b] if two else []) + [xhat, rstd, ln_g, a, b, wd]
    in_specs = [row] * (2 if two else 1) + [row, pl.BlockSpec((tm, 1), lambda i, j: (i, 0)), vec, mid, mid,
                                            pl.BlockSpec((tn, D), lambda i, j: (j, 0))]
    return pl.pallas_call(
        body,
        name=name,
        grid=(S // tm, F // tn),
        in_specs=in_specs,
        out_specs=[row, row, mid, mid, vec, vec],
        out_shape=[
            jax.ShapeDtypeStruct((S, D), F32),
            jax.ShapeDtypeStruct((S, D), BF16),
            jax.ShapeDtypeStruct((S, F), BF16),
            jax.ShapeDtypeStruct((S, F), BF16),
            jax.ShapeDtypeStruct((1, D), F32),
            jax.ShapeDtypeStruct((1, D), F32),
        ],
        scratch_shapes=[pltpu.VMEM((tm, D), BF16)],
        compiler_params=_cp(("arbitrary", "arbitrary"), 56),
    )(*ins)


def _ffn_bwd_dx(dr, da, db, wgt, wut, name, tm=256, tk=F, dep=None):
    nk = F // tk
    n_dep = 0 if dep is None else 1

    def body(dr_ref, da_ref, db_ref, wg_ref, wu_ref, *rest):
        dx_ref, acc = rest[n_dep], rest[n_dep + 1]
        kk = pl.program_id(1)
        part = _dot_nn(da_ref[...], wg_ref[...]) + _dot_nn(db_ref[...], wu_ref[...])

        @pl.when(kk == 0)
        def _():
            acc[...] = ALPHA * dr_ref[...] + part

        @pl.when(kk > 0)
        def _():
            acc[...] += part

        @pl.when(kk == nk - 1)
        def _():
            dx_ref[...] = acc[...]

    row = pl.BlockSpec((tm, D), lambda i, kk: (i, 0))
    mid = pl.BlockSpec((tm, tk), lambda i, kk: (i, kk))
    wsp = pl.BlockSpec((tk, D), lambda i, kk: (kk, 0))
    return pl.pallas_call(
        body,
        name=name,
        grid=(S // tm, nk),
        in_specs=[row, mid, mid, wsp, wsp] + [pl.BlockSpec(memory_space=pl.ANY)] * n_dep,
        out_specs=row,
        out_shape=jax.ShapeDtypeStruct((S, D), F32),
        scratch_shapes=[pltpu.VMEM((tm, D), F32)],
        compiler_params=_cp(("parallel", "arbitrary"), 56),
    )(dr, da, db, wgt, wut, *([] if dep is None else [dep]))


def _rope_tables(pos_col, invf, sign, tm=512):
    def body(p_ref, f_ref, s_ref, c_out, s_out):
        ang = p_ref[...].astype(F32) * f_ref[...]
        c_out[...] = jnp.cos(ang)
        s_out[...] = jnp.sin(ang) * s_ref[...]

    vec = pl.BlockSpec((1, BLK), lambda i: (0, 0))
    out = pl.BlockSpec((tm, BLK), lambda i: (i, 0))
    return pl.pallas_call(
        body,
        name="rope_tables",
        grid=(S // tm,),
        in_specs=[pl.BlockSpec((tm, 1), lambda i: (i, 0)), vec, vec],
        out_specs=[out, out],
        out_shape=[jax.ShapeDtypeStruct((S, BLK), F32)] * 2,
        compiler_params=_cp(("parallel",)),
    )(pos_col, invf, sign)


def _lane_lo(rows=BLK):
    return (lax.broadcasted_iota(jnp.int32, (rows, GW), 1) % HEAD_DIM) < (HEAD_DIM // 2)


def _swap_halves(t, lo):
    return jnp.where(lo, pltpu.roll(t, GW - HEAD_DIM // 2, 1), pltpu.roll(t, HEAD_DIM // 2, 1))


def _rope(t, cosf, sinf, lo):
    return t * cosf + _swap_halves(t, lo) * sinf


def _unrope(g, cosf, sinf, lo):
    return g * cosf + _swap_halves(g * sinf, lo)


def _tile4(v):
    return jnp.concatenate([v, v, v, v], axis=1)


def _band_mask(n):
    qi = lax.broadcasted_iota(jnp.int32, (BLK, 2 * BLK), 0)
    kj = lax.broadcasted_iota(jnp.int32, (BLK, 2 * BLK), 1)
    dist = qi + BLK - kj
    return (dist >= 0) & (dist <= BLK) & ((kj >= BLK) | (n >= 1))


ROWS = 256
LANES = 128


def _to_view(scr, y, dst_ref, d, dtype, col0=0):
    if d == 1:
        dst_ref[:, col0:col0 + GW] = y.astype(dtype)
        return
    for cb in range(GW // LANES):
        scr[cb][...] = y[:, cb * LANES:(cb + 1) * LANES]
    for r in range(d):
        for cb in range(GW // LANES):
            at = col0 + r * GW + cb * LANES
            dst_ref[:, at:at + LANES] = scr[cb][pl.ds(r, ROWS // d, stride=d), :].astype(dtype)


def _from_view(scr, src_ref, d):
    if d == 1:
        return src_ref[...].astype(F32)
    for r in range(d):
        for cb in range(GW // LANES):
            at = r * GW + cb * LANES
            scr[cb][pl.ds(r, ROWS // d, stride=d), :] = src_ref[:, at:at + LANES].astype(F32)
    return jnp.concatenate([scr[cb][...] for cb in range(GW // LANES)], axis=1)


def _view_spec(d):
    return pl.BlockSpec((ROWS // d, d * GW), lambda i: (i, 0))


_VIEW_SCRATCH = [pltpu.VMEM((ROWS, LANES), F32)] * (GW // LANES)


def _qkv_prep(qkv, cos_t, sin_t):
    def body(x_ref, c_ref, s_ref, *rest):
        outs, scr = rest[:9], rest[9:]
        lo = _lane_lo(ROWS)
        cf, sf = _tile4(c_ref[...]), _tile4(s_ref[...])
        for gi, d in enumerate(PATTERN_DILATIONS):
            q, k, v = (x_ref[:, (3 * part + gi) * GW:(3 * part + gi + 1) * GW].astype(F32) for part in range(3))
            _to_view(scr, _rope(q, cf, sf, lo) * (HEAD_DIM ** -0.5), outs[3 * gi], d, BF16)
            _to_view(scr, _rope(k, cf, sf, lo), outs[3 * gi + 1], d, BF16)
            _to_view(scr, v, outs[3 * gi + 2], d, BF16)

    tab = pl.BlockSpec((ROWS, BLK), lambda i: (i, 0))
    outs = pl.pallas_call(
        body,
        name="qkv_prep",
        grid=(S // ROWS,),
        in_specs=[pl.BlockSpec((ROWS, QKV_W), lambda i: (i, 0)), tab, tab],
        out_specs=[_view_spec(d) for d in PATTERN_DILATIONS for _ in range(3)],
        out_shape=[jax.ShapeDtypeStruct((S // d, d * GW), BF16) for d in PATTERN_DILATIONS for _ in range(3)],
        scratch_shapes=_VIEW_SCRATCH,
        compiler_params=_cp(("parallel",), 48),
    )(qkv, cos_t, sin_t)
    return [tuple(outs[3 * gi:3 * gi + 3]) for gi in range(3)]


N_ATTN_STEPS = S // BLK


def _class_and_block(gi, t):
    per_class = N_ATTN_STEPS // PATTERN_DILATIONS[gi]
    return t >> (per_class.bit_length() - 1), t & (per_class - 1)


def _attn_specs(gi):
    def cur(t):
        r, n = _class_and_block(gi, t)
        return n, r

    def prev(t):
        r, n = _class_and_block(gi, t)
        return jnp.maximum(n - 1, 0), r

    def whole(t):
        return 0, _class_and_block(gi, t)[0]

    sub = S // PATTERN_DILATIONS[gi]
    return (pl.BlockSpec((BLK, GW), cur), pl.BlockSpec((BLK, GW), prev), pl.BlockSpec((sub, GW), whole))


def _left_lanes():
    return lax.broadcasted_iota(jnp.int32, (BLK, LANES), 1) < HEAD_DIM


def _stack_heads(t2, left):
    zero = jnp.zeros_like(t2)
    return jnp.concatenate([jnp.where(left, t2, zero), jnp.where(left, zero, t2)], axis=0)


def _attn_fwd_block(n, q_ref, kc_ref, kp_ref, vc_ref, vp_ref, o_ref, l_ref):
    mask = _band_mask(n)
    left = _left_lanes()
    zero = jnp.zeros((BLK, LANES), BF16)
    for pr in range(GW // LANES):
        cs = slice(pr * LANES, (pr + 1) * LANES)
        q2 = q_ref[:, cs]
        k2 = jnp.concatenate([kp_ref[:, cs], kc_ref[:, cs]], axis=0)
        v2 = jnp.concatenate([vp_ref[:, cs], vc_ref[:, cs]], axis=0)
        o_h, lse_h = [], []
        for side in (left, ~left):
            s = jnp.where(mask, _dot_nt(jnp.where(side, q2, zero), k2), NEG)
            m = jnp.max(s, axis=1, keepdims=True)
            p = jnp.exp(s - m)
            l = jnp.sum(p, axis=1, keepdims=True)
            o_h.append(_dot_nn((p / l).astype(BF16), v2))
            lse_h.append(m + jnp.log(l))
        o_ref[:, cs] = jnp.where(left, o_h[0], o_h[1]).astype(BF16)
        l_ref[:, cs] = jnp.where(left, lse_h[0], lse_h[1])


def _attn_fwd(preps):
    def body(*refs):
        t = pl.program_id(0)
        for gi in range(3):
            _attn_fwd_block(_class_and_block(gi, t)[1], *refs[5 * gi:5 * gi + 5], *refs[15 + 2 * gi:17 + 2 * gi])

    in_specs, out_specs, out_shape, args = [], [], [], []
    for gi, d in enumerate(PATTERN_DILATIONS):
        cur, prev, _ = _attn_specs(gi)
        qr, kr, vv = preps[gi]
        in_specs += [cur, cur, prev, cur, prev]
        args += [qr, kr, kr, vv, vv]
        out_specs += [cur, cur]
        out_shape += [jax.ShapeDtypeStruct((S // d, d * GW), BF16), jax.ShapeDtypeStruct((S // d, d * GW), F32)]
    outs = pl.pallas_call(
        body,
        name="attn_fwd",
        grid=(N_ATTN_STEPS,),
        in_specs=in_specs,
        out_specs=out_specs,
        out_shape=out_shape,
        compiler_params=_cp(("arbitrary",), 48),
    )(*args)
    return [outs[0], outs[2], outs[4]], [outs[1], outs[3], outs[5]]


def _attn_bwd_block(n, q_ref, kc_ref, kp_ref, vc_ref, vp_ref, do_ref, l_ref, c_ref, dq_ref, dk_ref, dv_ref):
    mask = _band_mask(n)
    left = _left_lanes()
    left2 = jnp.concatenate([left, left], axis=0)
    here = pl.ds(pl.multiple_of(n * BLK, BLK), BLK)
    before = pl.ds(pl.multiple_of(jnp.maximum(n - 1, 0) * BLK, BLK), BLK)
    for pr in range(GW // LANES):
        cs = slice(pr * LANES, (pr + 1) * LANES)
        q2 = q_ref[:, cs]
        do2 = do_ref[:, cs]
        k_bd = _stack_heads(jnp.concatenate([kp_ref[:, cs], kc_ref[:, cs]], axis=0), left2)
        v_bd = _stack_heads(jnp.concatenate([vp_ref[:, cs], vc_ref[:, cs]], axis=0), left2)
        s2 = _dot_nt(q2, k_bd)
        dp2 = _dot_nt(do2, v_bd)
        ps, dss = [], []
        for h in range(2):
            at = pr * LANES + h * HEAD_DIM
            half = slice(h * 2 * BLK, (h + 1) * 2 * BLK)
            p = jnp.exp(jnp.where(mask, s2[:, half], NEG) - l_ref[:, at:at + 1])
            ps.append(p.astype(BF16))
            dss.append((p * (dp2[:, half] - c_ref[:, at:at + 1])).astype(BF16))
        ds_both = jnp.concatenate(dss, axis=1)
        dq_ref[:, cs] = _dot_nn(ds_both, k_bd) * (HEAD_DIM ** -0.5)
        dk_bd = _dot_tn(ds_both, q2)
        dv_bd = _dot_tn(jnp.concatenate(ps, axis=1), do2)
        dk2 = jnp.where(left2, dk_bd[:2 * BLK], dk_bd[2 * BLK:])
        dv2 = jnp.where(left2, dv_bd[:2 * BLK], dv_bd[2 * BLK:])
        dk_ref[here, cs] = dk2[BLK:]
        dv_ref[here, cs] = dv2[BLK:]
        dk_ref[before, cs] += dk2[:BLK]
        dv_ref[before, cs] += dv2[:BLK]


def _attn_bwd(preps, dos, lses, cterms):
    def body(*refs):
        t = pl.program_id(0)
        for gi in range(3):
            _attn_bwd_block(_class_and_block(gi, t)[1], *refs[8 * gi:8 * gi + 8], *refs[24 + 3 * gi:27 + 3 * gi])

    in_specs, out_specs, out_shape, args = [], [], [], []
    for gi, d in enumerate(PATTERN_DILATIONS):
        cur, prev, whole = _attn_specs(gi)
        qr, kr, vv = preps[gi]
        in_specs += [cur, cur, prev, cur, prev, cur, cur, cur]
        args += [qr, kr, kr, vv, vv, dos[gi], lses[gi], cterms[gi]]
        out_specs += [cur, whole, whole]
        out_shape += [jax.ShapeDtypeStruct((S // d, d * GW), F32)] * 3
    outs = pl.pallas_call(
        body,
        name="attn_bwd",
        grid=(N_ATTN_STEPS,),
        in_specs=in_specs,
        out_specs=out_specs,
        out_shape=out_shape,
        compiler_params=_cp(("arbitrary",), 56),
    )(*args)
    return [tuple(outs[3 * gi:3 * gi + 3]) for gi in range(3)]


def _qkv_unprep(grads, cos_t, sin_t, dproj):
    def body(*refs):
        views, (c_ref, s_ref, _, out_ref), scr = refs[:9], refs[9:13], refs[13:]
        lo = _lane_lo(ROWS)
        cf, sf = _tile4(c_ref[...]), _tile4(s_ref[...])
        for gi, d in enumerate(PATTERN_DILATIONS):
            dq_ref, dk_ref, dv_ref = views[3 * gi:3 * gi + 3]
            qa, ka, va = gi * GW, (3 + gi) * GW, (6 + gi) * GW
            out_ref[:, qa:qa + GW] = _unrope(_from_view(scr, dq_ref, d), cf, sf, lo).astype(BF16)
            out_ref[:, ka:ka + GW] = _unrope(_from_view(scr, dk_ref, d), cf, sf, lo).astype(BF16)
            out_ref[:, va:va + GW] = _from_view(scr, dv_ref, d).astype(BF16)

    tab = pl.BlockSpec((ROWS, BLK), lambda i: (i, 0))
    return pl.pallas_call(
        body,
        name="qkv_unprep",
        grid=(S // ROWS,),
        in_specs=[_view_spec(d) for d in PATTERN_DILATIONS for _ in range(3)] + [tab, tab, pl.BlockSpec(memory_space=pl.ANY)],
        out_specs=pl.BlockSpec((ROWS, QKV_W), lambda i: (i, 0)),
        out_shape=jax.ShapeDtypeStruct(dproj.shape, dproj.dtype),
        input_output_aliases={11: 0},
        scratch_shapes=_VIEW_SCRATCH,
        compiler_params=_cp(("parallel",), 48),
    )(*[g for grp in grads for g in grp], cos_t, sin_t, dproj)


def _group_weights(l0, l1, l2):
    mx = jnp.maximum(jnp.maximum(l0, l1), l2)
    e0, e1, e2 = jnp.exp(l0 - mx), jnp.exp(l1 - mx), jnp.exp(l2 - mx)
    inv = 1.0 / (e0 + e1 + e2)
    return e0 * inv, e1 * inv, e2 * inv


def _combine_fwd(os_, lses):
    def body(o0, o1, o2, l0, l1, l2, y_ref, *scr):
        ov = [_from_view(scr, o, d) for o, d in zip((o0, o1, o2), PATTERN_DILATIONS)]
        lv = [_from_view(scr, l, d) for l, d in zip((l0, l1, l2), PATTERN_DILATIONS)]
        w0, w1, w2 = _group_weights(*lv)
        y_ref[...] = (w0 * ov[0] + w1 * ov[1] + w2 * ov[2]).astype(BF16)

    views = [_view_spec(d) for d in PATTERN_DILATIONS]
    return pl.pallas_call(
        body,
        name="attn_combine_fwd",
        grid=(S // ROWS,),
        in_specs=views + views,
        out_specs=pl.BlockSpec((ROWS, GW), lambda i: (i, 0)),
        out_shape=jax.ShapeDtypeStruct((S, GW), BF16),
        scratch_shapes=_VIEW_SCRATCH,
        compiler_params=_cp(("parallel",)),
    )(*os_, *lses)


def _combine_bwd(dy, os_, lses, seg):
    def body(dy_ref, o0, o1, o2, l0, l1, l2, seg_ref, d0, d1, d2, c0, c1, c2, *scr):
        ov = [_from_view(scr, o, d) for o, d in zip((o0, o1, o2), PATTERN_DILATIONS)]
        lv = [_from_view(scr, l, d) for l, d in zip((l0, l1, l2), PATTERN_DILATIONS)]
        ws = _group_weights(*lv)
        dyv = dy_ref[...]
        t = dyv * (ws[0] * ov[0] + ws[1] * ov[1] + ws[2] * ov[2])
        t_hi = t.astype(BF16)
        r1 = t - t_hi.astype(F32)
        t_mid = r1.astype(BF16)
        t_lo = (r1 - t_mid.astype(F32)).astype(BF16)
        sg = seg_ref[...]
        e = _dot_nn(t_hi, sg) + _dot_nn(t_mid, sg) + _dot_nn(t_lo, sg)
        for w, do_ref, c_ref, d in zip(ws, (d0, d1, d2), (c0, c1, c2), PATTERN_DILATIONS):
            _to_view(scr, w * dyv, do_ref, d, BF16)
            _to_view(scr, w * e, c_ref, d, F32)

    views = [_view_spec(d) for d in PATTERN_DILATIONS]
    return pl.pallas_call(
        body,
        name="attn_combine_bwd",
        grid=(S // ROWS,),
        in_specs=[pl.BlockSpec((ROWS, GW), lambda i: (i, 0))] + views + views + [pl.BlockSpec((GW, GW), lambda i: (0, 0))],
        out_specs=views + views,
        out_shape=[jax.ShapeDtypeStruct((S // d, d * GW), BF16) for d in PATTERN_DILATIONS]
        + [jax.ShapeDtypeStruct((S // d, d * GW), F32) for d in PATTERN_DILATIONS],
        scratch_shapes=_VIEW_SCRATCH,
        compiler_params=_cp(("parallel",)),
    )(dy, *os_, *lses, seg)


_SQRT_HALF = 0.7071067811865476
_INV_SQRT_2PI = 0.3989422804014327


def _gelu(z):
    return 0.5 * z * (1.0 + lax.erf(z * _SQRT_HALF))


def _gelu_grad(z):
    return 0.5 * (1.0 + lax.erf(z * _SQRT_HALF)) + z * (_INV_SQRT_2PI * jnp.exp(-0.5 * z * z))


def _tril_ws(ws_ref, g):
    t = lax.broadcasted_iota(jnp.int32, (BLK, BLK), 0)
    s = lax.broadcasted_iota(jnp.int32, (BLK, BLK), 1)
    return jnp.where(t >= s, ws_ref[g], 0.0)


def _chunks_side_by_side(x, cols, nch):
    return jnp.concatenate([x[c * BLK:(c + 1) * BLK, cols] for c in range(nch)], axis=1)


def _gmlp_fwd(z, ws, bst, ln_g, ln_b, tm=512):
    nch = tm // BLK

    def body(z_ref, ws_ref, b_ref, g_ref, be_ref, y_ref):
        zg = _gelu(z_ref[...].astype(F32))
        u = zg[:, :D]
        vn, _, _ = _ln_fwd(zg[:, D:], g_ref[...], be_ref[...])
        vnb = vn.astype(BF16)
        bt = b_ref[...]
        for g in range(8):
            w = _tril_ws(ws_ref, g).astype(BF16)
            cols = slice(g * BLK, (g + 1) * BLK)
            mixed = _dot_nn(w, _chunks_side_by_side(vnb, cols, nch)) + bt[:, g:g + 1]
            for c in range(nch):
                rows = slice(c * BLK, (c + 1) * BLK)
                y_ref[rows, cols] = (u[rows, cols] * mixed[:, c * BLK:(c + 1) * BLK]).astype(BF16)

    return pl.pallas_call(
        body,
        name="gmlp_fwd",
        grid=(S // tm,),
        in_specs=[
            pl.BlockSpec((tm, 2 * D), lambda i: (i, 0)),
            pl.BlockSpec((8, BLK, BLK), lambda i: (0, 0, 0)),
            pl.BlockSpec((BLK, 8), lambda i: (0, 0)),
            pl.BlockSpec((1, D), lambda i: (0, 0)),
            pl.BlockSpec((1, D), lambda i: (0, 0)),
        ],
        out_specs=pl.BlockSpec((tm, D), lambda i: (i, 0)),
        out_shape=jax.ShapeDtypeStruct((S, D), BF16),
        compiler_params=_cp(("parallel",), 48),
    )(z, ws, bst, ln_g, ln_b)


def _gmlp_bwd(z, dy, ws, bst, ln_g, ln_b, dproj, tm=512):
    nch = tm // BLK

    def body(z_ref, dy_ref, ws_ref, b_ref, g_ref, be_ref, _, dz_ref, dws_ref, dbs_ref, dg_ref, dbe_ref, dvn_scr, dm_acc):
        i = pl.program_id(0)
        zv = z_ref[...].astype(F32)
        zg = _gelu(zv)
        u = zg[:, :D]
        gam = g_ref[...]
        vn, xhat, rstd = _ln_fwd(zg[:, D:], gam, be_ref[...])
        vnb = vn.astype(BF16)
        dyv = dy_ref[...]
        dmix = dyv * u
        dmb = dmix.astype(BF16)
        bt = b_ref[...]
        tmask = lax.broadcasted_iota(jnp.int32, (BLK, BLK), 0) >= lax.broadcasted_iota(jnp.int32, (BLK, BLK), 1)
        dm_sum = dmix[0:BLK]
        for c in range(1, nch):
            dm_sum = dm_sum + dmix[c * BLK:(c + 1) * BLK]

        @pl.when(i == 0)
        def _():
            dm_acc[...] = jnp.zeros_like(dm_acc)
            dws_ref[...] = jnp.zeros_like(dws_ref)
            dg_ref[...] = jnp.zeros_like(dg_ref)
            dbe_ref[...] = jnp.zeros_like(dbe_ref)

        dm_acc[...] += dm_sum
        dus = []
        for g in range(8):
            w = _tril_ws(ws_ref, g).astype(BF16)
            cols = slice(g * BLK, (g + 1) * BLK)
            v_cat = _chunks_side_by_side(vnb, cols, nch)
            dm_cat = _chunks_side_by_side(dmb, cols, nch)
            mixed = _dot_nn(w, v_cat) + bt[:, g:g + 1]
            dus.append(jnp.concatenate(
                [dyv[c * BLK:(c + 1) * BLK, cols] * mixed[:, c * BLK:(c + 1) * BLK] for c in range(nch)], axis=0))
            dws_ref[g] += jnp.where(tmask, _dot_nt(dm_cat, v_cat), 0.0)
            dvn_cat = _dot_tn(w, dm_cat)
            for c in range(nch):
                dvn_scr[c * BLK:(c + 1) * BLK, cols] = dvn_cat[:, c * BLK:(c + 1) * BLK]

        dvn = dvn_scr[...]
        dg_ref[...] += _colsum(dvn * xhat)
        dbe_ref[...] += _colsum(dvn)
        dvg = _ln_bwd(dvn, xhat, rstd, gam)
        gp = _gelu_grad(zv)
        dz_ref[:, :D] = (jnp.concatenate(dus, axis=1) * gp[:, :D]).astype(BF16)
        dz_ref[:, D:] = (dvg * gp[:, D:]).astype(BF16)

        @pl.when(i == S // tm - 1)
        def _():
            acc = dm_acc[...]
            for g in range(8):
                dbs_ref[:, g:g + 1] = jnp.sum(acc[:, g * BLK:(g + 1) * BLK], axis=1, keepdims=True)

    vec = pl.BlockSpec((1, D), lambda i: (0, 0))
    return pl.pallas_call(
        body,
        name="gmlp_bwd",
        grid=(S // tm,),
        in_specs=[
            pl.BlockSpec((tm, 2 * D), lambda i: (i, 0)),
            pl.BlockSpec((tm, D), lambda i: (i, 0)),
            pl.BlockSpec((8, BLK, BLK), lambda i: (0, 0, 0)),
            pl.BlockSpec((BLK, 8), lambda i: (0, 0)),
            vec,
            vec,
            pl.BlockSpec(memory_space=pl.ANY),
        ],
        out_specs=[
            pl.BlockSpec((tm, 2 * D), lambda i: (i, DPROJ_Z_COL)),
            pl.BlockSpec((8, BLK, BLK), lambda i: (0, 0, 0)),
            pl.BlockSpec((BLK, 8), lambda i: (0, 0)),
            vec,
            vec,
        ],
        out_shape=[
            jax.ShapeDtypeStruct(dproj.shape, dproj.dtype),
            jax.ShapeDtypeStruct((8, BLK, BLK), F32),
            jax.ShapeDtypeStruct((BLK, 8), F32),
            jax.ShapeDtypeStruct((1, D), F32),
            jax.ShapeDtypeStruct((1, D), F32),
        ],
        input_output_aliases={6: 0},
        scratch_shapes=[pltpu.VMEM((tm, D), F32), pltpu.VMEM((BLK, D), F32)],
        compiler_params=_cp(("arbitrary",), 48),
    )(z, dy, ws, bst, ln_g, ln_b, dproj)


def _merge_fwd(ya, yg, glog, bgate, h1, wabt, wgb, wo, ln_g, ln_b, tm=256):
    def body(ya_ref, yg_ref, gl_ref, bg_ref, h1_ref, wab_ref, wgb_ref, wo_ref, g_ref, b_ref,
             h_ref, hb_ref, xh_ref, rs_ref, mg_ref, bra_ref, brg_ref):
        bra = _dot_nt(ya_ref[...], wab_ref[...])
        brg = _dot_nn(yg_ref[...], wgb_ref[...])
        gates = _sigmoid(gl_ref[...].astype(F32) + bg_ref[...])
        merged = (gates[:, :D] * bra + gates[:, D:] * brg).astype(BF16)
        mix = _dot_nn(merged, wo_ref[...])
        h, xhat, rstd = _ln_fwd(ALPHA * h1_ref[...] + mix, g_ref[...], b_ref[...])
        h_ref[...] = h
        hb_ref[...] = h.astype(BF16)
        xh_ref[...] = xhat
        rs_ref[...] = rstd
        mg_ref[...] = merged
        bra_ref[...] = bra.astype(BF16)
        brg_ref[...] = brg.astype(BF16)

    row = pl.BlockSpec((tm, D), lambda i: (i, 0))
    vec = pl.BlockSpec((1, D), lambda i: (0, 0))
    full = lambda shape: pl.BlockSpec(shape, lambda i: (0, 0))
    return pl.pallas_call(
        body,
        name="merge_fwd",
        grid=(S // tm,),
        in_specs=[
            pl.BlockSpec((tm, GW), lambda i: (i, 0)), row,
            pl.BlockSpec((tm, 2 * D), lambda i: (i, glog.shape[1] // (2 * D) - 1)),
            full((1, 2 * D)), row,
            full((D, GW)), full((D, D)), full((D, D)), vec, vec,
        ],
        out_specs=[row, row, row, pl.BlockSpec((tm, 1), lambda i: (i, 0)), row, row, row],
        out_shape=[
            jax.ShapeDtypeStruct((S, D), F32),
            jax.ShapeDtypeStruct((S, D), BF16),
            jax.ShapeDtypeStruct((S, D), F32),
            jax.ShapeDtypeStruct((S, 1), F32),
            jax.ShapeDtypeStruct((S, D), BF16),
            jax.ShapeDtypeStruct((S, D), BF16),
            jax.ShapeDtypeStruct((S, D), BF16),
        ],
        compiler_params=_cp(("parallel",), 48),
    )(ya, yg, glog, bgate, h1, wabt, wgb, wo, ln_g, ln_b)


def _merge_bwd(dh2, xhat, rstd, ln_g, bra, brg, glog, bgate, wabt, wgb, wo, tm=256):
    def body(dh_ref, xh_ref, rs_ref, g_ref, bra_ref, brg_ref, gl_ref, bg_ref, wab_ref, wgb_ref, wo_ref,
             dr_ref, drb_ref, dlog_ref, dba_ref, dbg_ref, dya_ref, dyg_ref, dbgate_ref, dg_ref, dbias_ref):
        i = pl.program_id(0)
        dh = dh_ref[...]
        xh = xh_ref[...]
        dr = _ln_bwd(dh, xh, rs_ref[...], g_ref[...])
        drb = dr.astype(BF16)
        dr_ref[...] = dr
        drb_ref[...] = drb
        dmerged = _dot_nt(drb, wo_ref[...])
        gates = _sigmoid(gl_ref[...].astype(F32) + bg_ref[...])
        g0, g1 = gates[:, :D], gates[:, D:]
        dl0 = dmerged * bra_ref[...].astype(F32) * g0 * (1.0 - g0)
        dl1 = dmerged * brg_ref[...].astype(F32) * g1 * (1.0 - g1)
        dlog_ref[:, :D] = dl0.astype(BF16)
        dlog_ref[:, D:] = dl1.astype(BF16)
        dba = (dmerged * g0).astype(BF16)
        dbg = (dmerged * g1).astype(BF16)
        dba_ref[...] = dba
        dbg_ref[...] = dbg
        dya_ref[...] = _dot_nn(dba, wab_ref[...])
        dyg_ref[...] = _dot_nt(dbg, wgb_ref[...])
        s0, s1 = _colsum(dl0), _colsum(dl1)
        sg, sb = _colsum(dh * xh), _colsum(dh)

        @pl.when(i == 0)
        def _():
            dbgate_ref[:, :D] = s0
            dbgate_ref[:, D:] = s1
            dg_ref[...] = sg
            dbias_ref[...] = sb

        @pl.when(i > 0)
        def _():
            dbgate_ref[:, :D] += s0
            dbgate_ref[:, D:] += s1
            dg_ref[...] += sg
            dbias_ref[...] += sb

    row = pl.BlockSpec((tm, D), lambda i: (i, 0))
    vec = pl.BlockSpec((1, D), lambda i: (0, 0))
    wide = pl.BlockSpec((tm, 2 * D), lambda i: (i, 0))
    full = lambda shape: pl.BlockSpec(shape, lambda i: (0, 0))
    return pl.pallas_call(
        body,
        name="merge_bwd",
        grid=(S // tm,),
        in_specs=[row, row, pl.BlockSpec((tm, 1), lambda i: (i, 0)), vec, row, row,
                  pl.BlockSpec((tm, 2 * D), lambda i: (i, glog.shape[1] // (2 * D) - 1)),
                  full((1, 2 * D)), full((D, GW)), full((D, D)), full((D, D))],
        out_specs=[row, row, pl.BlockSpec((tm, 2 * D), lambda i: (i, DPROJ_G_COL)), row, row,
                   pl.BlockSpec((tm, GW), lambda i: (i, 0)), row, full((1, 2 * D)), vec, vec],
        out_shape=[
            jax.ShapeDtypeStruct((S, D), F32),
            jax.ShapeDtypeStruct((S, D), BF16),
            jax.ShapeDtypeStruct((S, DPROJ_W), BF16),
            jax.ShapeDtypeStruct((S, D), BF16),
            jax.ShapeDtypeStruct((S, D), BF16),
            jax.ShapeDtypeStruct((S, GW), F32),
            jax.ShapeDtypeStruct((S, D), F32),
            jax.ShapeDtypeStruct((1, 2 * D), F32),
            jax.ShapeDtypeStruct((1, D), F32),
            jax.ShapeDtypeStruct((1, D), F32),
        ],
        compiler_params=_cp(("arbitrary",), 48),
    )(dh2, xhat, rstd, ln_g, bra, brg, glog, bgate, wabt, wgb, wo)


def _dproj_to_dh1(dproj, win, dep, tm=512):
    n_dep = 0 if dep is None else 1

    def body(a_ref, b_ref, *rest):
        rest[n_dep][...] = (_dot_nn(a_ref[:, :QKV_W], b_ref[:QKV_W, :])
                            + _dot_nn(a_ref[:, DPROJ_ZG_AT:], b_ref[QKV_W:, :]))

    return pl.pallas_call(
        body,
        name="dproj_to_dh1",
        grid=(S // tm,),
        in_specs=[pl.BlockSpec((tm, DPROJ_W), lambda i: (i, 0)),
                  pl.BlockSpec((IN_W, D), lambda i: (0, 0), pipeline_mode=pl.Buffered(1))]
        + [pl.BlockSpec(memory_space=pl.ANY)] * n_dep,
        out_specs=pl.BlockSpec((tm, D), lambda i: (i, 0)),
        out_shape=jax.ShapeDtypeStruct((S, D), F32),
        compiler_params=_cp(("parallel",), 56),
    )(dproj, win, *([] if dep is None else [dep]))


def _tie(x, dep):
    if dep is None:
        return x
    return x + dep[0, 0].astype(x.dtype)


def _local_step(x, pos_col, target, get_w, p, emit):
    w = get_w("ffn1", None)
    a1, b1, hm1 = _ffn_up(x, w["g1"], w["u1"], "ffn1_up")
    w.update(get_w("ffn1d", hm1))
    h1, h1b, xh1, rs1 = _ffn_down(hm1, w["d1"], x, p["ln1_g"], p["ln1_b"], "ffn1_down")

    w.update(get_w("win", h1b))
    qkv = _matmul(h1b, w["win"], "nt", BF16, S, 1536, D, "proj_qkv", b_off=0, n_out=QKV_W)
    z = glog = _matmul(h1b, w["win"], "nt", BF16, S, 512, D, "proj_zg", b_off=QKV_W // 512, n_out=4 * D)

    half = jnp.arange(0, HEAD_DIM, 2, dtype=F32) / HEAD_DIM
    inv_freq = ROPE_THETA ** (-half)
    invf = jnp.tile(inv_freq, 4).reshape(1, BLK)
    sign = jnp.tile(jnp.concatenate([-jnp.ones((32,), F32), jnp.ones((32,), F32)]), 2).reshape(1, BLK)
    cos_t, sin_t = _rope_tables(pos_col, invf, sign)

    preps = _qkv_prep(qkv, cos_t, sin_t)
    os_, lses = _attn_fwd(preps)
    ya = _combine_fwd(os_, lses)
    get_w("late", ya, early=True)
    bst = p["gmlp_b_s"].T
    yg = _gmlp_fwd(z, p["gmlp_w_s"], bst, p["gmlp_ln_g"], p["gmlp_ln_b"])
    w.update(get_w("late", yg))
    h2, h2b, xh2, rs2, merged, bra, brg = _merge_fwd(ya, yg, glog, p["b_gates"], h1, w["ab"], w["gb"], w["o"],
                                                      p["ln2_g"], p["ln2_b"])
    a2, b2, hm2 = _ffn_up(h2b, w["g2"], w["u2"], "ffn2_up")
    dh3, loss, xh3, rs3 = _ffn_down(hm2, w["d2"], h2, p["ln3_g"], p["ln3_b"], "ffn2_down", target=target)

    gp = {}
    dr3, df2, da2, db2, gp["ln3_g"], gp["ln3_b"] = _ffn_bwd_mid(dh3, None, xh3, rs3, p["ln3_g"], a2, b2, w["d2"],
                                                                "ffn2_bwd_mid")
    tok = emit("ffn2", {
        "g2": _matmul(da2, h2b, "tn", BF16, 1408, D, S, "wgrad_g2"),
        "u2": _matmul(db2, h2b, "tn", BF16, 1408, D, S, "wgrad_u2"),
        "d2": _matmul(hm2, df2, "tn", BF16, 1408, D, S, "wgrad_d2")})
    dh2 = _ffn_bwd_dx(dr3, da2, db2, w["g2"], w["u2"], "ffn2_bwd_dx")

    (dr2, dr2b, dproj, dba, dbg, dya, dyg, gp["b_gates"], gp["ln2_g"], gp["ln2_b"]) = _merge_bwd(
        dh2, xh2, rs2, _tie(p["ln2_g"], tok), bra, brg, glog, p["b_gates"], w["ab"], w["gb"], w["o"])
    tok = emit("mix", {
        "o": _matmul(merged, dr2b, "tn", BF16, 512, D, S, "wgrad_o"),
        "ab": _matmul(dba, ya, "tn", BF16, 512, GW, S, "wgrad_ab"),
        "gb": _matmul(yg, dbg, "tn", BF16, 512, D, S, "wgrad_gb")})

    seg = (jnp.arange(GW)[:, None] // HEAD_DIM == jnp.arange(GW)[None, :] // HEAD_DIM).astype(BF16)
    do0, do1, do2, c0, c1, c2 = _combine_bwd(dya, os_, lses, _tie(seg, tok))
    dproj = _qkv_unprep(_attn_bwd(preps, (do0, do1, do2), lses, (c0, c1, c2)), cos_t, sin_t, dproj)
    dproj, gp["gmlp_w_s"], dbst, gp["gmlp_ln_g"], gp["gmlp_ln_b"] = _gmlp_bwd(
        z, dyg, p["gmlp_w_s"], bst, p["gmlp_ln_g"], p["gmlp_ln_b"], dproj)
    gp["gmlp_b_s"] = dbst.T
    tok = emit("win", {"win": _matmul(dproj, h1b, "tn", BF16, 512, D, S, "wgrad_win", a_map=_dproj_tile, m_out=IN_W)})
    dh1m = _dproj_to_dh1(dproj, w["win"], tok)

    dr1, df1, da1, db1, gp["ln1_g"], gp["ln1_b"] = _ffn_bwd_mid(dr2, dh1m, xh1, rs1, p["ln1_g"], a1, b1, w["d1"],
                                                                "ffn1_bwd_mid")
    tok = emit("small", {**gp, "loss": loss})
    tok = emit("g1", {"g1": _matmul(da1, x, "tn", BF16, 1408, D, S, "wgrad_g1", dep=tok)})
    tok = emit("u1", {"u1": _matmul(db1, x, "tn", BF16, 1408, D, S, "wgrad_u1", dep=tok)})
    tok = emit("d1", {"d1": _matmul(hm1, df1, "tn", BF16, 1408, D, S, "wgrad_d1", dep=tok)})
    dx = _ffn_bwd_dx(dr1, da1, db1, w["g1"], w["u1"], "ffn1_bwd_dx", dep=tok)
    return loss, dx, gp


_FLIPS = [(mx, my, mc) for mx in (0, 1) for my in (0, 1) for mc in (0, 1)][1:]
_GROUPS = {"ffn1": ("g1", "u1"), "ffn1d": ("d1",), "win": ("win",), "late": ("ab", "gb", "o", "g2", "u2", "d2")}
_SCATTERS = {"ffn2": ("g2", "u2", "d2"), "mix": ("ab", "gb", "o"), "win": ("win",), "g1": ("g1",), "u1": ("u1",), "d1": ("d1",)}
_TWO_STAGE = ("g1", "u1", "d1")
_HBM = pl.BlockSpec(memory_space=pltpu.HBM)
_SEM = pl.BlockSpec(memory_space=pltpu.SEMAPHORE)
_EFFECT = pltpu.SideEffectType.DATAFLOW_SIDE_EFFECTING


def _me():
    return 4 * lax.axis_index("x") + 2 * lax.axis_index("y") + lax.axis_index("c")


def _copies_start(bufs, copies, n_sem, name, after=None):
    nb = len(bufs)
    n_after = 0 if after is None else 1

    def body(*refs):
        b = refs[:nb]
        send_sems, recv_sems = refs[nb + n_after], refs[nb + n_after + 1]
        token = refs[-1]
        x, y, c = lax.axis_index("x"), lax.axis_index("y"), lax.axis_index("c")
        me = 4 * x + 2 * y + c
        for si, s_slot, di, d_slot, flip, sem in copies:
            s_idx, d_idx = s_slot(me), d_slot(me)
            mx, my, mc = flip(c) if callable(flip) else flip
            pltpu.make_async_remote_copy(
                src_ref=b[si] if s_idx is None else b[si].at[s_idx],
                dst_ref=b[di] if d_idx is None else b[di].at[d_idx],
                send_sem=send_sems.at[sem], recv_sem=recv_sems.at[sem],
                device_id=(x ^ mx, y ^ my, c ^ mc), device_id_type=MESH).start()
        token[...] = jnp.zeros_like(token)

    ins = [pltpu.with_memory_space_constraint(a, pltpu.HBM) for a in bufs]
    outs = pl.pallas_call(
        body,
        name=name,
        in_specs=[_HBM] * nb + [pl.BlockSpec(memory_space=pl.ANY)] * n_after,
        out_specs=[_SEM, _SEM] + [_HBM] * nb + [pl.BlockSpec(memory_space=pltpu.VMEM)],
        out_shape=[pltpu.SemaphoreType.DMA((n_sem,)), pltpu.SemaphoreType.DMA((n_sem,))]
        + [pltpu.HBM(a.shape, a.dtype) for a in ins] + [jax.ShapeDtypeStruct((8, 128), F32)],
        input_output_aliases={k: 2 + k for k in range(nb)},
        compiler_params=pltpu.CompilerParams(has_side_effects=_EFFECT),
    )(*ins, *([] if after is None else [after]))
    return outs[0], outs[1], list(outs[2:2 + nb]), outs[-1]


def _copies_wait(started, waits, after, name):
    send_sems, recv_sems, bufs, _ = started
    nb = len(bufs)

    def body(*refs):
        b = refs[:nb]
        ss, rs = refs[nb], refs[nb + 1]
        me3 = (lax.axis_index("x"), lax.axis_index("y"), lax.axis_index("c"))
        for bi, n_blocks, sem, is_send in waits:
            blocks = b[bi].at[pl.ds(0, n_blocks)]
            cp = pltpu.make_async_remote_copy(src_ref=blocks, dst_ref=blocks, send_sem=ss.at[sem], recv_sem=rs.at[sem],
                                              device_id=me3, device_id_type=MESH)
            if is_send:
                cp.wait_send()
            else:
                cp.wait_recv()

    return pl.pallas_call(
        body,
        name=name,
        in_specs=[_HBM] * nb + [_SEM, _SEM, pl.BlockSpec(memory_space=pl.ANY)],
        out_specs=[_HBM] * nb,
        out_shape=[pltpu.HBM(a.shape, a.dtype) for a in bufs],
        input_output_aliases={k: k for k in range(nb)},
        compiler_params=pltpu.CompilerParams(has_side_effects=_EFFECT),
    )(*bufs, send_sems, recv_sems, after)


def _landing(own, me):
    return lax.dynamic_update_slice(lax.empty((N_DEV,) + own.shape[1:], own.dtype), own, (me, 0, 0))


_SIBLING = (0, 0, 1)
_OTHER_CHIPS = ((1, 0, 0), (0, 1, 0), (1, 1, 0))


def _bits(flip):
    return 4 * flip[0] + 2 * flip[1] + flip[2]


_X_NBR, _Y_NBR = (1, 0, 0), (0, 1, 0)


def _xor(bits):
    return functools.partial(lambda me, bits: me ^ bits, bits=bits)


def _gather_send(shards, grp, after):
    nw = len(shards)
    me = _me()
    bufs = list(shards) + [_landing(a[None], me) for a in shards]
    copies = []
    for k in range(nw):
        for s, flip in enumerate((_SIBLING, _X_NBR, _Y_NBR)):
            copies.append((k, lambda me: None, nw + k, lambda me: me, flip, 3 * k + s))
    return _copies_start(bufs, copies, 3 * nw, "gather_send_" + grp, after)


def _gather_pass(started, grp, after):
    nw = len(started[2]) // 2
    waits = [(nw + k, 1, 3 * k + s, is_send) for k in range(nw) for s in range(3) for is_send in (True, False)]
    lands = list(_copies_wait(started, waits, after, "gather_arrived_" + grp)[nw:])
    copies = []
    for k in range(nw):
        held = lambda me: me ^ (4 - 2 * (me % 2))
        copies.append((k, held, k, held, lambda c: (c, 1 - c, 0), 3 * k))
        copies.append((k, _xor(4), k, _xor(4), _SIBLING, 3 * k + 1))
        copies.append((k, _xor(2), k, _xor(2), _SIBLING, 3 * k + 2))
    return _copies_start(lands, copies, 3 * nw, "gather_pass_" + grp)


def _gather_finish(passed, grp, after):
    nw = len(passed[2])
    waits = [(k, 1, 3 * k + s, is_send) for k in range(nw) for s in range(3) for is_send in (True, False)]
    lands = list(_copies_wait(passed, waits, passed[3] if after is None else after, "gather_forwarded_" + grp))
    copies = [(k, _xor(6), k, _xor(6), _SIBLING, k) for k in range(nw)]
    last = _copies_start(lands, copies, nw, "gather_pass2_" + grp)
    waits = [(k, 1, k, is_send) for k in range(nw) for is_send in (True, False)]
    return _copies_wait(last, waits, last[3], "gather_done_" + grp)


def _small_send(block, after=None):
    me = _me()
    copies = [(0, lambda me: None, 1, lambda me: me, _SIBLING, 1)]
    copies += [(0, lambda me: None, 1, lambda me: me, f, 0) for f in _OTHER_CHIPS]
    return _copies_start([block, _landing(block[None], me)], copies, 2, "small_send", after)


def _small_pass(started, after):
    waits = [(1, 3, 0, True), (1, 1, 1, True), (1, 3, 0, False), (1, 1, 1, False)]
    land = _copies_wait(started, waits, after, "small_arrived")[1]
    copies = [(0, _xor(_bits(f)), 0, _xor(_bits(f)), _SIBLING, 0) for f in _OTHER_CHIPS]
    return _copies_start([land], copies, 1, "small_pass")


def _small_finish(passed, after):
    return _copies_wait(passed, [(0, 3, 0, True), (0, 3, 0, False)], after, "small_done")[0]


def _scatter_send(parts, grp, after=None):
    nw = len(parts)
    bufs = list(parts) + [lax.empty((N_DEV - 1,) + a.shape[1:], a.dtype) for a in parts]
    copies = []
    for k in range(nw):
        for f in _FLIPS:
            to = functools.partial(lambda me, bits: me ^ bits, bits=_bits(f))
            copies.append((k, to, nw + k, functools.partial(lambda me, s: s, s=_bits(f) - 1), f, k))
    return _copies_start(bufs, copies, nw, "scatter_send_" + grp, after)


def _scatter_finish(started, grp, after):
    nw = len(started[2]) // 2
    waits = [(nw + k, N_DEV - 1, k, is_send) for k in range(nw) for is_send in (True, False)]
    done = _copies_wait(started, waits, after, "scatter_done_" + grp)
    return [(done[nw + k], done[k]) for k in range(nw)]


N_CHIP = N_DEV // 2


def _pair_sum(part, other, key):
    _, r, c = part.shape
    tm = r
    core = lax.axis_index("c").astype(jnp.int32).reshape(1)

    def body(core_ref, a_ref, b_ref, o_ref):
        o_ref[...] = (a_ref[...].astype(F32) + b_ref[...].astype(F32)).astype(BF16)

    return pl.pallas_call(
        body,
        name="pair_sum_" + key,
        grid_spec=pltpu.PrefetchScalarGridSpec(
            num_scalar_prefetch=1,
            grid=(N_CHIP, r // tm),
            in_specs=[pl.BlockSpec((None, tm, c), lambda q, i, core_ref: (2 * q + core_ref[0], i, 0)),
                      pl.BlockSpec((None, tm, c), lambda q, i, core_ref: (q, i, 0))],
            out_specs=pl.BlockSpec((None, tm, c), lambda q, i, core_ref: (q, i, 0)),
        ),
        out_shape=jax.ShapeDtypeStruct((N_CHIP, r, c), BF16),
        compiler_params=_cp(("parallel", "parallel")),
    )(core, part, other)


def _scatter2_send(parts, grp, keys):
    nw = len(parts)
    swap = []
    for k in range(nw):
        for q in range(N_CHIP):
            src = functools.partial(lambda me, q: 2 * q + 1 - me % 2, q=q)
            swap.append((k, src, nw + k, functools.partial(lambda me, q: q, q=q), _SIBLING, k))
    others = [lax.empty((N_CHIP,) + a.shape[1:], a.dtype) for a in parts]
    started = _copies_start(list(parts) + others, swap, nw, "scatter_swap_" + grp)
    waits = [(nw + k, N_CHIP, k, is_send) for k in range(nw) for is_send in (True, False)]
    swapped = _copies_wait(started, waits, started[3], "scatter_swapped_" + grp)
    pairs = [_pair_sum(swapped[k], swapped[nw + k], keys[k]) for k in range(nw)]
    lands = [lax.empty((N_CHIP - 1,) + p.shape[1:], p.dtype) for p in pairs]
    copies = []
    for k in range(nw):
        for slot, f in enumerate(_OTHER_CHIPS):
            to = functools.partial(lambda me, bits: (me ^ bits) // 2, bits=_bits(f))
            copies.append((k, to, nw + k, functools.partial(lambda me, s: s, s=slot), f, k))
    return _copies_start(pairs + lands, copies, nw, "scatter_send_" + grp)


def _scatter2_finish(started, grp, after):
    nw = len(started[2]) // 2
    waits = [(nw + k, N_CHIP - 1, k, is_send) for k in range(nw) for is_send in (True, False)]
    done = _copies_wait(started, waits, after, "scatter_done_" + grp)
    return [(done[nw + k], done[k]) for k in range(nw)]


_ADAM_ROWS = {352: 176, 1088: 272}


def _sum_adamw(parts, wv, m, v, name, dep=None, own=None, own_at=None):
    n_parts, r, c = parts.shape
    tm = _ADAM_ROWS.get(r, r)
    assert r % tm == 0 and wv.shape == (r, c)
    n_dep = 0 if dep is None else 1
    n_own = 0 if own is None else 1

    def body(*refs):
        refs = refs[n_own:]
        o_ref = refs[0] if n_own else None
        p_ref, w_ref, m_ref, v_ref = refs[n_own:n_own + 4]
        g_ref, d_ref, mo_ref, vo_ref = refs[n_own + 4 + n_dep:]
        gv = p_ref[0].astype(F32)
        if n_own:
            gv = o_ref[...].astype(F32) + gv
        for j in range(1, n_parts):
            gv = gv + p_ref[j].astype(F32)
        g_ref[...] = gv
        mn = ADAM_B1 * m_ref[...] + (1.0 - ADAM_B1) * gv
        vn = ADAM_B2 * v_ref[...] + (1.0 - ADAM_B2) * (gv * gv)
        m_hat = mn / (1.0 - ADAM_B1 ** ADAM_STEP)
        v_hat = vn / (1.0 - ADAM_B2 ** ADAM_STEP)
        d_ref[...] = -ADAM_LR * (m_hat / (jnp.sqrt(v_hat) + ADAM_EPS) + ADAM_WD * w_ref[...])
        mo_ref[...] = mn
        vo_ref[...] = vn

    sp = pl.BlockSpec((tm, c), lambda i, *_: (i, 0))
    in_specs = ([pl.BlockSpec((None, tm, c), lambda i, at: (at[0], i, 0))] * n_own
                + [pl.BlockSpec((n_parts, tm, c), lambda i, *_: (0, i, 0))] + [sp] * 3
                + [pl.BlockSpec(memory_space=pl.ANY)] * n_dep)
    return pl.pallas_call(
        body,
        name=name,
        grid_spec=pltpu.PrefetchScalarGridSpec(num_scalar_prefetch=n_own, grid=(r // tm,), in_specs=in_specs,
                                               out_specs=[sp] * 4),
        out_shape=[jax.ShapeDtypeStruct((r, c), F32)] * 4,
        compiler_params=_cp(("parallel",), 48),
    )(*([own_at, own] if n_own else []), parts, wv, m, v, *([] if dep is None else [dep]))


_WEIGHTS = ["ffn1_w_gate", "ffn1_w_up", "ffn1_w_down", "ln1_g", "ln1_b", "w_in", "b_gates", "gmlp_ln_g", "gmlp_ln_b",
            "gmlp_w_s", "gmlp_b_s", "w_attn_branch", "w_gmlp_branch", "w_out", "ln2_g", "ln2_b", "ffn2_w_gate",
            "ffn2_w_up", "ffn2_w_down", "ln3_g", "ln3_b"]
_BIG_OF = {"ffn1_w_gate": ("g1", True), "ffn1_w_up": ("u1", True), "ffn1_w_down": ("d1", False), "w_in": ("win", True),
           "w_attn_branch": ("ab", True), "w_gmlp_branch": ("gb", False), "w_out": ("o", False),
           "ffn2_w_gate": ("g2", True), "ffn2_w_up": ("u2", True), "ffn2_w_down": ("d2", False)}
_SMALL = [n for n in _WEIGHTS if n not in _BIG_OF]
_SMALL_ROWS = {"gmlp_w_s": 128, "b_gates": 2}
_SMALL_SLOT = 8


def _pack_small(d, last=None):
    rows = []
    for n in _SMALL:
        r = d[n].reshape(-1, D)
        slot = max(r.shape[0], _SMALL_SLOT)
        rows.append(jnp.pad(r, ((0, slot - r.shape[0]), (0, 0))))
    rows.append(jnp.zeros((_SMALL_SLOT, D), F32) if last is None else jnp.broadcast_to(last.reshape(1, 1), (_SMALL_SLOT, D)))
    return jnp.concatenate(rows, axis=0)


def _unpack_small(packed, shapes):
    out, at = {}, 0
    for n in _SMALL:
        k = _SMALL_ROWS.get(n, 1)
        out[n] = packed[at:at + k].reshape(shapes[n])
        at += max(k, _SMALL_SLOT)
    return out


def kernel(x, positions, ffn1_w_gate, ffn1_w_up, ffn1_w_down, ln1_g, ln1_b, w_in, b_gates, gmlp_ln_g, gmlp_ln_b, gmlp_w_s, gmlp_b_s, w_attn_branch, w_gmlp_branch, w_out, ln2_g, ln2_b, ffn2_w_gate, ffn2_w_up, ffn2_w_down, ln3_g, ln3_b, loss_target, m_ffn1_w_gate, m_ffn1_w_up, m_ffn1_w_down, m_ln1_g, m_ln1_b, m_w_in, m_b_gates, m_gmlp_ln_g, m_gmlp_ln_b, m_gmlp_w_s, m_gmlp_b_s, m_w_attn_branch, m_w_gmlp_branch, m_w_out, m_ln2_g, m_ln2_b, m_ffn2_w_gate, m_ffn2_w_up, m_ffn2_w_down, m_ln3_g, m_ln3_b, v_ffn1_w_gate, v_ffn1_w_up, v_ffn1_w_down, v_ln1_g, v_ln1_b, v_w_in, v_b_gates, v_gmlp_ln_g, v_gmlp_ln_b, v_gmlp_w_s, v_gmlp_b_s, v_w_attn_branch, v_w_gmlp_branch, v_w_out, v_ln2_g, v_ln2_b, v_ffn2_w_gate, v_ffn2_w_up, v_ffn2_w_down, v_ln3_g, v_ln3_b):
    args = dict(locals())
    wts = {n: args[n] for n in _WEIGHTS}
    ms = {n: args["m_" + n] for n in _WEIGHTS}
    vs = {n: args["v_" + n] for n in _WEIGHTS}

    name_of = {key: (n, tr) for n, (key, tr) in _BIG_OF.items()}

    shards = {}
    for grp, keys in _GROUPS.items():
        shards[grp] = []
        for key in keys:
            n, tr = name_of[key]
            s2 = wts[n][0]
            shards[grp].append((s2.T if tr else s2).astype(BF16))
    started, passed = {}, {}

    def send(grp, after):
        started[grp] = _gather_send(shards[grp], grp, after)
        return started[grp][3]

    def pass_on(grp, after):
        passed[grp] = _gather_pass(started[grp], grp, after)
        return passed[grp][3]

    def finish(grp, after):
        lands = _gather_finish(passed[grp], grp, after)
        return {key: g.reshape(-1, g.shape[-1]) for key, g in zip(_GROUPS[grp], lands)}

    def get_w(grp, after, early=False):
        if grp == "ffn1":
            return finish("ffn1", send("win", send("ffn1d", pass_on("ffn1", send("ffn1", None)))))
        if grp == "ffn1d":
            return finish("ffn1d", send("late", pass_on("win", pass_on("ffn1d", after))))
        if early:
            pass_on(grp, after)
            return None
        return finish(grp, after)

    sent = {}

    def emit(grp, grads):
        if grp == "small":
            sent[grp] = _small_send(_pack_small(grads, last=grads["loss"]))
        else:
            parts = [grads[key].reshape(N_DEV, -1, grads[key].shape[-1]) for key in _SCATTERS[grp]]
            sent[grp] = (_scatter2_send(parts, grp, _SCATTERS[grp]) if grp in _TWO_STAGE else _scatter_send(parts, grp))
        return sent[grp][3]

    p = {n: (wts[n][0] if n in ("gmlp_w_s", "gmlp_b_s") else wts[n]) for n in _SMALL}
    loss, dx, gp = _local_step(x[0], positions.reshape(S, 1), loss_target[0], get_w, p, emit)
    grads, deltas, new_m, new_v = {}, {}, {}, {}
    small_passed = _small_pass(sent["small"], dx)
    after = small_passed[3]
    for grp in ("ffn2", "mix", "win", "small", "g1", "u1", "d1"):
        if grp == "small":
            parts = _small_finish(small_passed, after)
            outs = _sum_adamw(parts, *[_pack_small({n: d[n] for n in _SMALL}) for d in (wts, ms, vs)], "update_small")
            shapes = {n: wts[n].shape for n in _SMALL}
            for dst, packed in zip((grads, deltas, new_m, new_v), outs):
                dst.update(_unpack_small(packed, shapes))
            loss = outs[0][-_SMALL_SLOT, 0]
            after = outs[1]
            continue
        arrived = (_scatter2_finish if grp in _TWO_STAGE else _scatter_finish)(sent[grp], grp, after)
        own_at = (_me() // 2 if grp in _TWO_STAGE else _me()).astype(jnp.int32).reshape(1)
        for key, (part, own) in zip(_SCATTERS[grp], arrived):
            n, tr = name_of[key]
            outs = _sum_adamw(part, *[(d[n][0].T if tr else d[n][0]) for d in (wts, ms, vs)], "update_" + key, dep=after,
                              own=own, own_at=own_at)
            for dst, o in zip((grads, deltas, new_m, new_v), outs):
                dst[n] = (o.T if tr else o)[None]
            after = outs[1]

    return (loss, dx[None], *[grads[n] for n in _WEIGHTS], *[deltas[n] for n in _WEIGHTS],
            *[new_m[n] for n in _WEIGHTS], *[new_v[n] for n in _WEIGHTS])
```
